```python
import numpy as np
import jax
import jax.numpy as jnp
from jax import lax

D_MODEL = 2048
BATCH = 8
SEQ = 2048
DEPTH = 2

N_MIXERS = 4
HEAD_DIM = 128
GROUP_WIDTH = D_MODEL // N_MIXERS
MIX_WIDTH = N_MIXERS * GROUP_WIDTH
N_HEADS = GROUP_WIDTH // HEAD_DIM
Q_BLOCK = 128
ROPE_THETA = 10000.0
RMS_EPS = 1e-6
NEG_INF = -1e30

CMP_LEN = 32
CMP_STRIDE = 16
CMP_HIDDEN = HEAD_DIM
SEL_LEN = 64
SEL_TOPN = 16
WINDOW = 512
FORCED_BONUS = 1e6

MLA_Q_RANK = 384
MLA_KV_RANK = 128
MLA_NOPE = 128
MLA_ROPE = 64
MLA_V = 128

IN_SPLITS = (
    ("sb_q", GROUP_WIDTH), ("sb_k", GROUP_WIDTH), ("sb_v", GROUP_WIDTH), ("sb_gate", GROUP_WIDTH),
    ("nsa_q", GROUP_WIDTH), ("nsa_k_cmp", HEAD_DIM), ("nsa_v_cmp", HEAD_DIM),
    ("nsa_k_sel", HEAD_DIM), ("nsa_v_sel", HEAD_DIM), ("nsa_k_win", HEAD_DIM), ("nsa_v_win", HEAD_DIM),
    ("nsa_branch", 3 * N_HEADS), ("nsa_gate", GROUP_WIDTH),
    ("fox_q", GROUP_WIDTH), ("fox_k", GROUP_WIDTH), ("fox_v", GROUP_WIDTH), ("fox_f", N_HEADS),
    ("fox_gate", GROUP_WIDTH),
    ("mla_cq", MLA_Q_RANK), ("mla_ckv", MLA_KV_RANK), ("mla_k_rope", MLA_ROPE), ("mla_gate", GROUP_WIDTH),
)
IN_WIDTH = sum(width for _, width in IN_SPLITS)

kernel_name = "hybrid_sb_nsa_fox_mla_layer"


def rms_norm(x, g):
    xf = x.astype(jnp.float32)
    y = xf * lax.rsqrt(jnp.mean(xf * xf, axis=-1, keepdims=True) + RMS_EPS)
    return (y * g.astype(jnp.float32)).astype(x.dtype)


def apply_rope(x, pos):
    half = x.shape[-1] // 2
    inv_freq = ROPE_THETA ** (-jnp.arange(half, dtype=jnp.float32) / half)
    ang = pos.astype(jnp.float32)[:, None] * inv_freq[None, :]
    cos = jnp.cos(ang)[:, None, :]
    sin = jnp.sin(ang)[:, None, :]
    xf = x.astype(jnp.float32)
    x1, x2 = xf[..., :half], xf[..., half:]
    return jnp.concatenate([x1 * cos - x2 * sin, x2 * cos + x1 * sin], axis=-1).astype(x.dtype)


def masked_softmax(z, mask):
    p = jax.nn.softmax(jnp.where(mask, z, NEG_INF), axis=-1)
    return jnp.where(mask, p, 0.0)


def sweep_query_blocks(block_fn, seq_len):
    out = lax.map(block_fn, jnp.arange(seq_len // Q_BLOCK))
    n_blocks, b, q, h, d = out.shape
    return jnp.moveaxis(out, 0, 1).reshape(b, n_blocks * q, h, d)


def split_columns(z):
    parts = {}
    offset = 0
    for name, width in IN_SPLITS:
        parts[name] = z[..., offset:offset + width]
        offset += width
    return parts


def stick_breaking_attention(q, k, v):
    B, S, H, d = q.shape
    scale = d ** -0.5
    kpos = jnp.arange(S)

    def block(i):
        s0 = i * Q_BLOCK
        qb = lax.dynamic_slice_in_dim(q, s0, Q_BLOCK, axis=1)
        qpos = s0 + jnp.arange(Q_BLOCK)
        z = jnp.einsum("bqhd,bshd->bhqs", qb, k).astype(jnp.float32) * scale
        earlier = kpos[None, :] < qpos[:, None]
        log_keep = jnp.where(earlier, jax.nn.log_sigmoid(-z), 0.0)
        log_after = lax.cumsum(log_keep, axis=3, reverse=True) - log_keep
        w = jnp.where(earlier, jnp.exp(jax.nn.log_sigmoid(z) + log_after), 0.0)
        return jnp.einsum("bhqs,bshd->bqhd", w.astype(v.dtype), v)

    return sweep_query_blocks(block, S)


def compress_blocks(tok, pos_emb, w1, w2):
    B, S, d = tok.shape
    n_cmp = (S - CMP_LEN) // CMP_STRIDE + 1
    gather = np.arange(n_cmp)[:, None] * CMP_STRIDE + np.arange(CMP_LEN)[None, :]
    blocks = tok[:, gather] + pos_emb
    hidden = jax.nn.silu(blocks.reshape(B, n_cmp, CMP_LEN * d) @ w1)
    return hidden @ w2


def native_sparse_attention(q, kc_tok, vc_tok, ks, vs, kw, vw, branch_gates,
                            pos_k, w1_k, w2_k, pos_v, w1_v, w2_v, pos):
    B, S, H, d = q.shape
    scale = d ** -0.5
    n_cmp = (S - CMP_LEN) // CMP_STRIDE + 1
    n_sel = S // SEL_LEN
    top_n = min(SEL_TOPN, n_sel)
    cmp_start = np.arange(n_cmp) * CMP_STRIDE
    cmp_end = jnp.asarray(cmp_start + CMP_LEN - 1, dtype=jnp.int32)
    sel_start = np.arange(n_sel) * SEL_LEN
    overlap = np.clip(np.minimum(cmp_start[:, None] + CMP_LEN, sel_start[None, :] + SEL_LEN)
                      - np.maximum(cmp_start[:, None], sel_start[None, :]), 0, None)
    cmp_to_sel = jnp.asarray(overlap / CMP_LEN, dtype=jnp.float32)

    q = apply_rope(q, pos)
    kc = compress_blocks(kc_tok, pos_k, w1_k, w2_k)
    kc = apply_rope(kc[:, :, None, :], cmp_end)[:, :, 0, :]
    vc = compress_blocks(vc_tok, pos_v, w1_v, w2_v)
    ks = apply_rope(ks[:, :, None, :], pos)[:, :, 0, :]
    kw = apply_rope(kw[:, :, None, :], pos)[:, :, 0, :]
    ks_blocks = ks.reshape(B, n_sel, SEL_LEN, d)
    vs_blocks = vs.reshape(B, n_sel, SEL_LEN, d)
    kw_pad = jnp.pad(kw, ((0, 0), (WINDOW, 0), (0, 0)))
    vw_pad = jnp.pad(vw, ((0, 0), (WINDOW, 0), (0, 0)))
    sel_ids = jnp.arange(n_sel)
    gather_blocks = jax.vmap(lambda blocks, ids: blocks[ids])

    def block(i):
        s0 = i * Q_BLOCK
        qb = lax.dynamic_slice_in_dim(q, s0, Q_BLOCK, axis=1)
        gb = lax.dynamic_slice_in_dim(branch_gates, s0, Q_BLOCK, axis=1)
        qpos = s0 + jnp.arange(Q_BLOCK)
        zc = jnp.einsum("bqhd,bnd->bhqn", qb, kc).astype(jnp.float32) * scale
        pc = masked_softmax(zc, cmp_end[None, :] <= qpos[:, None])
        o_cmp = jnp.einsum("bhqn,bnd->bqhd", pc.astype(vc.dtype), vc)
        imp = jnp.einsum("bhqn,ns->bqs", pc, cmp_to_sel)
        cur = qpos // SEL_LEN
        valid = sel_ids[None, :] <= cur[:, None]
        forced = ((sel_ids[None, :] == 0) | (sel_ids[None, :] == cur[:, None])
                  | (sel_ids[None, :] == cur[:, None] - 1))
        score = jnp.where(valid, jnp.where(forced, FORCED_BONUS, imp), NEG_INF)
        _, idx = lax.top_k(score, top_n)
        kg = gather_blocks(ks_blocks, idx)
        vg = gather_blocks(vs_blocks, idx).reshape(B, Q_BLOCK, top_n * SEL_LEN, d)
        tok = idx[..., None] * SEL_LEN + jnp.arange(SEL_LEN)
        sel_mask = (tok <= qpos[None, :, None, None]).reshape(B, 1, Q_BLOCK, top_n * SEL_LEN)
        zs = jnp.einsum("bqhd,bqnld->bhqnl", qb, kg).astype(jnp.float32)
        zs = zs.reshape(B, H, Q_BLOCK, top_n * SEL_LEN) * scale
        ps = masked_softmax(zs, sel_mask)
        o_slc = jnp.einsum("bhqm,bqmd->bqhd", ps.astype(vg.dtype), vg)
        kwb = lax.dynamic_slice_in_dim(kw_pad, s0, WINDOW + Q_BLOCK, axis=1)
        vwb = lax.dynamic_slice_in_dim(vw_pad, s0, WINDOW + Q_BLOCK, axis=1)
        kpos = s0 - WINDOW + jnp.arange(WINDOW + Q_BLOCK)
        win_mask = ((kpos[None, :] >= 0) & (kpos[None, :] <= qpos[:, None])
                    & (kpos[None, :] > qpos[:, None] - WINDOW))
        zw = jnp.einsum("bqhd,bkd->bhqk", qb, kwb).astype(jnp.float32) * scale
        pw = masked_softmax(zw, win_mask)
        o_win = jnp.einsum("bhqk,bkd->bqhd", pw.astype(vwb.dtype), vwb)
        return gb[..., 0:1] * o_cmp + gb[..., 1:2] * o_slc + gb[..., 2:3] * o_win

    return sweep_query_blocks(block, S)


def forgetting_attention(q, k, v, log_f):
    B, S, H, d = q.shape
    scale = d ** -0.5
    cum = jnp.cumsum(log_f, axis=1).transpose(0, 2, 1)
    kpos = jnp.arange(S)

    def block(i):
        s0 = i * Q_BLOCK
        qb = lax.dynamic_slice_in_dim(q, s0, Q_BLOCK, axis=1)
        cq = lax.dynamic_slice_in_dim(cum, s0, Q_BLOCK, axis=2)
        qpos = s0 + jnp.arange(Q_BLOCK)
        z = (jnp.einsum("bqhd,bshd->bhqs", qb, k).astype(jnp.float32) * scale
             + cq[..., None] - cum[:, :, None, :])
        p = masked_softmax(z, kpos[None, :] <= qpos[:, None])
        return jnp.einsum("bhqs,bshd->bqhd", p.astype(v.dtype), v)

    return sweep_query_blocks(block, S)


def latent_attention(c_q, c_kv, k_rope, q_norm_g, w_uq, kv_norm_g, w_ukv, pos):
    B, S, _ = c_q.shape
    q = (rms_norm(c_q, q_norm_g) @ w_uq).reshape(B, S, N_HEADS, MLA_NOPE + MLA_ROPE)
    q_nope = q[..., :MLA_NOPE]
    q_rot = apply_rope(q[..., MLA_NOPE:], pos)
    kv = (rms_norm(c_kv, kv_norm_g) @ w_ukv).reshape(B, S, N_HEADS, MLA_NOPE + MLA_V)
    k_nope, v = kv[..., :MLA_NOPE], kv[..., MLA_NOPE:]
    k_rot = apply_rope(k_rope[:, :, None, :], pos)[:, :, 0, :]
    scale = (MLA_NOPE + MLA_ROPE) ** -0.5
    kpos = jnp.arange(S)

    def block(i):
        s0 = i * Q_BLOCK
        qn = lax.dynamic_slice_in_dim(q_nope, s0, Q_BLOCK, axis=1)
        qr = lax.dynamic_slice_in_dim(q_rot, s0, Q_BLOCK, axis=1)
        qpos = s0 + jnp.arange(Q_BLOCK)
        z = (jnp.einsum("bqhd,bshd->bhqs", qn, k_nope)
             + jnp.einsum("bqhr,bsr->bhqs", qr, k_rot)).astype(jnp.float32) * scale
        p = masked_softmax(z, kpos[None, :] <= qpos[:, None])
        return jnp.einsum("bhqs,bshd->bqhd", p.astype(v.dtype), v)

    return sweep_query_blocks(block, S)


def hybrid_layer(x, pre_g, post_g, w_in, b_in, w_out, forget_bias,
                 pos_k, w1_k, w2_k, pos_v, w1_v, w2_v,
                 q_norm_g, w_uq, kv_norm_g, w_ukv):
    B, S, _ = x.shape
    pos = jnp.arange(S)
    h = rms_norm(x, pre_g)
    p = split_columns(h @ w_in + b_in)

    def heads(t):
        return t.reshape(B, S, N_HEADS, HEAD_DIM)

    o_sb = stick_breaking_attention(heads(p["sb_q"]), heads(p["sb_k"]), heads(p["sb_v"]))

    branch_gates = jax.nn.sigmoid(p["nsa_branch"].reshape(B, S, N_HEADS, 3))
    o_nsa = native_sparse_attention(heads(p["nsa_q"]), p["nsa_k_cmp"], p["nsa_v_cmp"],
                                    p["nsa_k_sel"], p["nsa_v_sel"], p["nsa_k_win"], p["nsa_v_win"],
                                    branch_gates, pos_k, w1_k, w2_k, pos_v, w1_v, w2_v, pos)

    log_f = jax.nn.log_sigmoid((p["fox_f"] + forget_bias).astype(jnp.float32))
    o_fox = forgetting_attention(heads(p["fox_q"]), heads(p["fox_k"]), heads(p["fox_v"]), log_f)

    o_mla = latent_attention(p["mla_cq"], p["mla_ckv"], p["mla_k_rope"],
                             q_norm_g, w_uq, kv_norm_g, w_ukv, pos)

    mix = jnp.concatenate([
        o_sb.reshape(B, S, GROUP_WIDTH) * jax.nn.silu(p["sb_gate"]),
        o_nsa.reshape(B, S, GROUP_WIDTH) * jax.nn.silu(p["nsa_gate"]),
        o_fox.reshape(B, S, GROUP_WIDTH) * jax.nn.silu(p["fox_gate"]),
        o_mla.reshape(B, S, GROUP_WIDTH) * jax.nn.silu(p["mla_gate"]),
    ], axis=-1)
    return x + rms_norm(mix @ w_out, post_g)


def _fwd_setup_inputs(seed: int = 0) -> dict:
    key = jax.random.key(seed)
    ks = jax.random.split(key, 17)
    f32 = jnp.float32

    def normal(k, shape, scale):
        return jax.random.normal(k, shape, f32) * scale

    def gain(k, shape):
        return 1.0 + 0.02 * jax.random.normal(k, shape, f32)

    flat = CMP_LEN * HEAD_DIM
    return {
        "x": normal(ks[0], (BATCH, SEQ, D_MODEL), 1.0),
        "pre_norm_g": gain(ks[1], (DEPTH, D_MODEL)),
        "post_norm_g": gain(ks[2], (DEPTH, D_MODEL)),
        "w_in": normal(ks[3], (DEPTH, D_MODEL, IN_WIDTH), D_MODEL ** -0.5),
        "b_in": normal(ks[4], (DEPTH, IN_WIDTH), 0.02),
        "w_out": normal(ks[5], (DEPTH, MIX_WIDTH, D_MODEL), MIX_WIDTH ** -0.5),
        "fox_forget_bias": jax.random.uniform(ks[6], (DEPTH, N_HEADS), f32, 1.0, 4.0),
        "nsa_cmp_pos_k": normal(ks[7], (DEPTH, CMP_LEN, HEAD_DIM), 0.02),
        "nsa_cmp_w1_k": normal(ks[8], (DEPTH, flat, CMP_HIDDEN), flat ** -0.5),
        "nsa_cmp_w2_k": normal(ks[9], (DEPTH, CMP_HIDDEN, HEAD_DIM), CMP_HIDDEN ** -0.5),
        "nsa_cmp_pos_v": normal(ks[10], (DEPTH, CMP_LEN, HEAD_DIM), 0.02),
        "nsa_cmp_w1_v": normal(ks[11], (DEPTH, flat, CMP_HIDDEN), flat ** -0.5),
        "nsa_cmp_w2_v": normal(ks[12], (DEPTH, CMP_HIDDEN, HEAD_DIM), CMP_HIDDEN ** -0.5),
        "mla_q_norm_g": gain(ks[13], (DEPTH, MLA_Q_RANK)),
        "mla_w_uq": normal(ks[14], (DEPTH, MLA_Q_RANK, N_HEADS * (MLA_NOPE + MLA_ROPE)), MLA_Q_RANK ** -0.5),
        "mla_kv_norm_g": gain(ks[15], (DEPTH, MLA_KV_RANK)),
        "mla_w_ukv": normal(ks[16], (DEPTH, MLA_KV_RANK, N_HEADS * (MLA_NOPE + MLA_V)), MLA_KV_RANK ** -0.5),
    }


def _fwd_reference(x, pre_norm_g, post_norm_g, w_in, b_in, w_out, fox_forget_bias,
              nsa_cmp_pos_k, nsa_cmp_w1_k, nsa_cmp_w2_k,
              nsa_cmp_pos_v, nsa_cmp_w1_v, nsa_cmp_w2_v,
              mla_q_norm_g, mla_w_uq, mla_kv_norm_g, mla_w_ukv):
    for l in range(DEPTH):
        x = hybrid_layer(x, pre_norm_g[l], post_norm_g[l], w_in[l], b_in[l], w_out[l],
                         fox_forget_bias[l],
                         nsa_cmp_pos_k[l], nsa_cmp_w1_k[l], nsa_cmp_w2_k[l],
                         nsa_cmp_pos_v[l], nsa_cmp_w1_v[l], nsa_cmp_w2_v[l],
                         mla_q_norm_g[l], mla_w_uq[l], mla_kv_norm_g[l], mla_w_ukv[l])
    return x


import jax as _jax
import jax.numpy as _jnp

TWIN_FORMAT = 'train_step'
FWD_PARAMS = ['x', 'pre_norm_g', 'post_norm_g', 'w_in', 'b_in', 'w_out', 'fox_forget_bias', 'nsa_cmp_pos_k', 'nsa_cmp_w1_k', 'nsa_cmp_w2_k', 'nsa_cmp_pos_v', 'nsa_cmp_w1_v', 'nsa_cmp_w2_v', 'mla_q_norm_g', 'mla_w_uq', 'mla_kv_norm_g', 'mla_w_ukv']
TWIN_WEIGHTS = ['pre_norm_g', 'post_norm_g', 'w_in', 'b_in', 'w_out', 'fox_forget_bias', 'nsa_cmp_pos_k', 'nsa_cmp_w1_k', 'nsa_cmp_w2_k', 'nsa_cmp_pos_v', 'nsa_cmp_w1_v', 'nsa_cmp_w2_v', 'mla_q_norm_g', 'mla_w_uq', 'mla_kv_norm_g', 'mla_w_ukv']
TWIN_DIFF_INPUT = 'x'
TWIN_INPUTS = ['x', 'pre_norm_g', 'post_norm_g', 'w_in', 'b_in', 'w_out', 'fox_forget_bias', 'nsa_cmp_pos_k', 'nsa_cmp_w1_k', 'nsa_cmp_w2_k', 'nsa_cmp_pos_v', 'nsa_cmp_w1_v', 'nsa_cmp_w2_v', 'mla_q_norm_g', 'mla_w_uq', 'mla_kv_norm_g', 'mla_w_ukv', 'loss_target', 'm_pre_norm_g', 'm_post_norm_g', 'm_w_in', 'm_b_in', 'm_w_out', 'm_fox_forget_bias', 'm_nsa_cmp_pos_k', 'm_nsa_cmp_w1_k', 'm_nsa_cmp_w2_k', 'm_nsa_cmp_pos_v', 'm_nsa_cmp_w1_v', 'm_nsa_cmp_w2_v', 'm_mla_q_norm_g', 'm_mla_w_uq', 'm_mla_kv_norm_g', 'm_mla_w_ukv', 'v_pre_norm_g', 'v_post_norm_g', 'v_w_in', 'v_b_in', 'v_w_out', 'v_fox_forget_bias', 'v_nsa_cmp_pos_k', 'v_nsa_cmp_w1_k', 'v_nsa_cmp_w2_k', 'v_nsa_cmp_pos_v', 'v_nsa_cmp_w1_v', 'v_nsa_cmp_w2_v', 'v_mla_q_norm_g', 'v_mla_w_uq', 'v_mla_kv_norm_g', 'v_mla_w_ukv']
TWIN_OUTPUTS = ['loss', 'grad_x', 'grad_pre_norm_g', 'grad_post_norm_g', 'grad_w_in', 'grad_b_in', 'grad_w_out', 'grad_fox_forget_bias', 'grad_nsa_cmp_pos_k', 'grad_nsa_cmp_w1_k', 'grad_nsa_cmp_w2_k', 'grad_nsa_cmp_pos_v', 'grad_nsa_cmp_w1_v', 'grad_nsa_cmp_w2_v', 'grad_mla_q_norm_g', 'grad_mla_w_uq', 'grad_mla_kv_norm_g', 'grad_mla_w_ukv', 'delta_pre_norm_g', 'delta_post_norm_g', 'delta_w_in', 'delta_b_in', 'delta_w_out', 'delta_fox_forget_bias', 'delta_nsa_cmp_pos_k', 'delta_nsa_cmp_w1_k', 'delta_nsa_cmp_w2_k', 'delta_nsa_cmp_pos_v', 'delta_nsa_cmp_w1_v', 'delta_nsa_cmp_w2_v', 'delta_mla_q_norm_g', 'delta_mla_w_uq', 'delta_mla_kv_norm_g', 'delta_mla_w_ukv', 'new_m_pre_norm_g', 'new_m_post_norm_g', 'new_m_w_in', 'new_m_b_in', 'new_m_w_out', 'new_m_fox_forget_bias', 'new_m_nsa_cmp_pos_k', 'new_m_nsa_cmp_w1_k', 'new_m_nsa_cmp_w2_k', 'new_m_nsa_cmp_pos_v', 'new_m_nsa_cmp_w1_v', 'new_m_nsa_cmp_w2_v', 'new_m_mla_q_norm_g', 'new_m_mla_w_uq', 'new_m_mla_kv_norm_g', 'new_m_mla_w_ukv', 'new_v_pre_norm_g', 'new_v_post_norm_g', 'new_v_w_in', 'new_v_b_in', 'new_v_w_out', 'new_v_fox_forget_bias', 'new_v_nsa_cmp_pos_k', 'new_v_nsa_cmp_w1_k', 'new_v_nsa_cmp_w2_k', 'new_v_nsa_cmp_pos_v', 'new_v_nsa_cmp_w1_v', 'new_v_nsa_cmp_w2_v', 'new_v_mla_q_norm_g', 'new_v_mla_w_uq', 'new_v_mla_kv_norm_g', 'new_v_mla_w_ukv']
TWIN_LEAF_KINDS = {'loss': 'loss', 'grad_x': 'grad_x', 'grad_pre_norm_g': 'grad_w', 'grad_post_norm_g': 'grad_w', 'grad_w_in': 'grad_w', 'grad_b_in': 'grad_w', 'grad_w_out': 'grad_w', 'grad_fox_forget_bias': 'grad_w', 'grad_nsa_cmp_pos_k': 'grad_w', 'grad_nsa_cmp_w1_k': 'grad_w', 'grad_nsa_cmp_w2_k': 'grad_w', 'grad_nsa_cmp_pos_v': 'grad_w', 'grad_nsa_cmp_w1_v': 'grad_w', 'grad_nsa_cmp_w2_v': 'grad_w', 'grad_mla_q_norm_g': 'grad_w', 'grad_mla_w_uq': 'grad_w', 'grad_mla_kv_norm_g': 'grad_w', 'grad_mla_w_ukv': 'grad_w', 'delta_pre_norm_g': 'delta_w', 'delta_post_norm_g': 'delta_w', 'delta_w_in': 'delta_w', 'delta_b_in': 'delta_w', 'delta_w_out': 'delta_w', 'delta_fox_forget_bias': 'delta_w', 'delta_nsa_cmp_pos_k': 'delta_w', 'delta_nsa_cmp_w1_k': 'delta_w', 'delta_nsa_cmp_w2_k': 'delta_w', 'delta_nsa_cmp_pos_v': 'delta_w', 'delta_nsa_cmp_w1_v': 'delta_w', 'delta_nsa_cmp_w2_v': 'delta_w', 'delta_mla_q_norm_g': 'delta_w', 'delta_mla_w_uq': 'delta_w', 'delta_mla_kv_norm_g': 'delta_w', 'delta_mla_w_ukv': 'delta_w', 'new_m_pre_norm_g': 'new_m', 'new_m_post_norm_g': 'new_m', 'new_m_w_in': 'new_m', 'new_m_b_in': 'new_m', 'new_m_w_out': 'new_m', 'new_m_fox_forget_bias': 'new_m', 'new_m_nsa_cmp_pos_k': 'new_m', 'new_m_nsa_cmp_w1_k': 'new_m', 'new_m_nsa_cmp_w2_k': 'new_m', 'new_m_nsa_cmp_pos_v': 'new_m', 'new_m_nsa_cmp_w1_v': 'new_m', 'new_m_nsa_cmp_w2_v': 'new_m', 'new_m_mla_q_norm_g': 'new_m', 'new_m_mla_w_uq': 'new_m', 'new_m_mla_kv_norm_g': 'new_m', 'new_m_mla_w_ukv': 'new_m', 'new_v_pre_norm_g': 'new_v', 'new_v_post_norm_g': 'new_v', 'new_v_w_in': 'new_v', 'new_v_b_in': 'new_v', 'new_v_w_out': 'new_v', 'new_v_fox_forget_bias': 'new_v', 'new_v_nsa_cmp_pos_k': 'new_v', 'new_v_nsa_cmp_w1_k': 'new_v', 'new_v_nsa_cmp_w2_k': 'new_v', 'new_v_nsa_cmp_pos_v': 'new_v', 'new_v_nsa_cmp_w1_v': 'new_v', 'new_v_nsa_cmp_w2_v': 'new_v', 'new_v_mla_q_norm_g': 'new_v', 'new_v_mla_w_uq': 'new_v', 'new_v_mla_kv_norm_g': 'new_v', 'new_v_mla_w_ukv': 'new_v'}


def _forward(args):
    return _fwd_reference(*[args[k] for k in FWD_PARAMS])


def _output_shape():
    out = _jax.eval_shape(lambda: _forward(_fwd_setup_inputs(0)))
    return out.shape, out.dtype

N_MICROBATCH = 1
ADAM_LR = 0.001
ADAM_B1 = 0.9
ADAM_B2 = 0.999
ADAM_EPS = 1e-08
ADAM_WD = 0.01
ADAM_STEP = 10
PER_EXAMPLE_BATCH_AXIS = {'x': 0, 'loss_target': 0}
SHARED_INPUTS = []
_WEIGHT_DTYPES = {'pre_norm_g': _jnp.float32, 'post_norm_g': _jnp.float32, 'w_in': _jnp.float32, 'b_in': _jnp.float32, 'w_out': _jnp.float32, 'fox_forget_bias': _jnp.float32, 'nsa_cmp_pos_k': _jnp.float32, 'nsa_cmp_w1_k': _jnp.float32, 'nsa_cmp_w2_k': _jnp.float32, 'nsa_cmp_pos_v': _jnp.float32, 'nsa_cmp_w1_v': _jnp.float32, 'nsa_cmp_w2_v': _jnp.float32, 'mla_q_norm_g': _jnp.float32, 'mla_w_uq': _jnp.float32, 'mla_kv_norm_g': _jnp.float32, 'mla_w_ukv': _jnp.float32}
MOMENT_SCALE = {'pre_norm_g': 2.377985e-01, 'post_norm_g': 8.013176e+00, 'w_in': 1.267300e-01, 'b_in': 3.614955e-01, 'w_out': 1.462662e-01, 'fox_forget_bias': 1.241952e+00, 'nsa_cmp_pos_k': 5.862003e-03, 'nsa_cmp_w1_k': 4.285013e-02, 'nsa_cmp_w2_k': 4.065693e-02, 'nsa_cmp_pos_v': 6.700874e-02, 'nsa_cmp_w1_v': 1.171548e-01, 'nsa_cmp_w2_v': 1.388878e-01, 'mla_q_norm_g': 6.782135e-02, 'mla_w_uq': 5.209140e-02, 'mla_kv_norm_g': 2.092902e-01, 'mla_w_ukv': 6.609284e-02}


def _to_microbatches(a, axis):
    t = _jnp.moveaxis(a, axis, 0)
    t = t.reshape((N_MICROBATCH, t.shape[0] // N_MICROBATCH) + t.shape[1:])
    return _jnp.moveaxis(t, 1, axis + 1)


def setup_inputs(seed: int = 0) -> dict:
    inp = _fwd_setup_inputs(seed)
    key = _jax.random.fold_in(_jax.random.key(seed), 7919)
    shape, _ = _output_shape()
    out = dict(inp)
    out["loss_target"] = _jax.random.normal(_jax.random.fold_in(key, 0), shape, _jnp.float32)
    for i, name in enumerate(TWIN_WEIGHTS):
        w = inp[name].astype(_jnp.float32)
        if MOMENT_SCALE is None:
            s = _jnp.sqrt(_jnp.mean(_jnp.square(w)) + 1e-30)
        else:
            s = MOMENT_SCALE[name]
        km, kv = _jax.random.split(_jax.random.fold_in(key, i + 1))
        out[name] = w
        out["m_" + name] = s * _jax.random.normal(km, w.shape, _jnp.float32)
        out["v_" + name] = (s * s) * _jax.random.uniform(kv, w.shape, _jnp.float32, 0.5, 1.5)
    if N_MICROBATCH > 1:
        for name, axis in PER_EXAMPLE_BATCH_AXIS.items():
            out[name] = _to_microbatches(out[name], axis)
    return {'x': out['x'], 'pre_norm_g': out['pre_norm_g'], 'post_norm_g': out['post_norm_g'], 'w_in': out['w_in'], 'b_in': out['b_in'], 'w_out': out['w_out'], 'fox_forget_bias': out['fox_forget_bias'], 'nsa_cmp_pos_k': out['nsa_cmp_pos_k'], 'nsa_cmp_w1_k': out['nsa_cmp_w1_k'], 'nsa_cmp_w2_k': out['nsa_cmp_w2_k'], 'nsa_cmp_pos_v': out['nsa_cmp_pos_v'], 'nsa_cmp_w1_v': out['nsa_cmp_w1_v'], 'nsa_cmp_w2_v': out['nsa_cmp_w2_v'], 'mla_q_norm_g': out['mla_q_norm_g'], 'mla_w_uq': out['mla_w_uq'], 'mla_kv_norm_g': out['mla_kv_norm_g'], 'mla_w_ukv': out['mla_w_ukv'], 'loss_target': out['loss_target'], 'm_pre_norm_g': out['m_pre_norm_g'], 'm_post_norm_g': out['m_post_norm_g'], 'm_w_in': out['m_w_in'], 'm_b_in': out['m_b_in'], 'm_w_out': out['m_w_out'], 'm_fox_forget_bias': out['m_fox_forget_bias'], 'm_nsa_cmp_pos_k': out['m_nsa_cmp_pos_k'], 'm_nsa_cmp_w1_k': out['m_nsa_cmp_w1_k'], 'm_nsa_cmp_w2_k': out['m_nsa_cmp_w2_k'], 'm_nsa_cmp_pos_v': out['m_nsa_cmp_pos_v'], 'm_nsa_cmp_w1_v': out['m_nsa_cmp_w1_v'], 'm_nsa_cmp_w2_v': out['m_nsa_cmp_w2_v'], 'm_mla_q_norm_g': out['m_mla_q_norm_g'], 'm_mla_w_uq': out['m_mla_w_uq'], 'm_mla_kv_norm_g': out['m_mla_kv_norm_g'], 'm_mla_w_ukv': out['m_mla_w_ukv'], 'v_pre_norm_g': out['v_pre_norm_g'], 'v_post_norm_g': out['v_post_norm_g'], 'v_w_in': out['v_w_in'], 'v_b_in': out['v_b_in'], 'v_w_out': out['v_w_out'], 'v_fox_forget_bias': out['v_fox_forget_bias'], 'v_nsa_cmp_pos_k': out['v_nsa_cmp_pos_k'], 'v_nsa_cmp_w1_k': out['v_nsa_cmp_w1_k'], 'v_nsa_cmp_w2_k': out['v_nsa_cmp_w2_k'], 'v_nsa_cmp_pos_v': out['v_nsa_cmp_pos_v'], 'v_nsa_cmp_w1_v': out['v_nsa_cmp_w1_v'], 'v_nsa_cmp_w2_v': out['v_nsa_cmp_w2_v'], 'v_mla_q_norm_g': out['v_mla_q_norm_g'], 'v_mla_w_uq': out['v_mla_w_uq'], 'v_mla_kv_norm_g': out['v_mla_kv_norm_g'], 'v_mla_w_ukv': out['v_mla_w_ukv']}


def _loss(weights, diff, rest, loss_target):
    with _jax.named_scope("forward"):
        args = {**rest, TWIN_DIFF_INPUT: diff, **{k: w.astype(_WEIGHT_DTYPES[k]) for k, w in weights.items()}}
        y = _forward(args)
    with _jax.named_scope("loss_head"):
        err = _jnp.square(y.astype(_jnp.float32) - loss_target)
        return 0.5 * _jnp.sum(_jnp.mean(err, axis=-1)) if err.ndim else 0.5 * err


def _adamw(w, g, m, v):
    m = ADAM_B1 * m + (1.0 - ADAM_B1) * g
    v = ADAM_B2 * v + (1.0 - ADAM_B2) * _jnp.square(g)
    m_hat = m / (1.0 - ADAM_B1 ** ADAM_STEP)
    v_hat = v / (1.0 - ADAM_B2 ** ADAM_STEP)
    delta = -ADAM_LR * (m_hat / (_jnp.sqrt(v_hat) + ADAM_EPS) + ADAM_WD * w)
    return delta, m, v


def reference(x, pre_norm_g, post_norm_g, w_in, b_in, w_out, fox_forget_bias, nsa_cmp_pos_k, nsa_cmp_w1_k, nsa_cmp_w2_k, nsa_cmp_pos_v, nsa_cmp_w1_v, nsa_cmp_w2_v, mla_q_norm_g, mla_w_uq, mla_kv_norm_g, mla_w_ukv, loss_target, m_pre_norm_g, m_post_norm_g, m_w_in, m_b_in, m_w_out, m_fox_forget_bias, m_nsa_cmp_pos_k, m_nsa_cmp_w1_k, m_nsa_cmp_w2_k, m_nsa_cmp_pos_v, m_nsa_cmp_w1_v, m_nsa_cmp_w2_v, m_mla_q_norm_g, m_mla_w_uq, m_mla_kv_norm_g, m_mla_w_ukv, v_pre_norm_g, v_post_norm_g, v_w_in, v_b_in, v_w_out, v_fox_forget_bias, v_nsa_cmp_pos_k, v_nsa_cmp_w1_k, v_nsa_cmp_w2_k, v_nsa_cmp_pos_v, v_nsa_cmp_w1_v, v_nsa_cmp_w2_v, v_mla_q_norm_g, v_mla_w_uq, v_mla_kv_norm_g, v_mla_w_ukv):
    given = dict(x=x, pre_norm_g=pre_norm_g, post_norm_g=post_norm_g, w_in=w_in, b_in=b_in, w_out=w_out, fox_forget_bias=fox_forget_bias, nsa_cmp_pos_k=nsa_cmp_pos_k, nsa_cmp_w1_k=nsa_cmp_w1_k, nsa_cmp_w2_k=nsa_cmp_w2_k, nsa_cmp_pos_v=nsa_cmp_pos_v, nsa_cmp_w1_v=nsa_cmp_w1_v, nsa_cmp_w2_v=nsa_cmp_w2_v, mla_q_norm_g=mla_q_norm_g, mla_w_uq=mla_w_uq, mla_kv_norm_g=mla_kv_norm_g, mla_w_ukv=mla_w_ukv, loss_target=loss_target, m_pre_norm_g=m_pre_norm_g, m_post_norm_g=m_post_norm_g, m_w_in=m_w_in, m_b_in=m_b_in, m_w_out=m_w_out, m_fox_forget_bias=m_fox_forget_bias, m_nsa_cmp_pos_k=m_nsa_cmp_pos_k, m_nsa_cmp_w1_k=m_nsa_cmp_w1_k, m_nsa_cmp_w2_k=m_nsa_cmp_w2_k, m_nsa_cmp_pos_v=m_nsa_cmp_pos_v, m_nsa_cmp_w1_v=m_nsa_cmp_w1_v, m_nsa_cmp_w2_v=m_nsa_cmp_w2_v, m_mla_q_norm_g=m_mla_q_norm_g, m_mla_w_uq=m_mla_w_uq, m_mla_kv_norm_g=m_mla_kv_norm_g, m_mla_w_ukv=m_mla_w_ukv, v_pre_norm_g=v_pre_norm_g, v_post_norm_g=v_post_norm_g, v_w_in=v_w_in, v_b_in=v_b_in, v_w_out=v_w_out, v_fox_forget_bias=v_fox_forget_bias, v_nsa_cmp_pos_k=v_nsa_cmp_pos_k, v_nsa_cmp_w1_k=v_nsa_cmp_w1_k, v_nsa_cmp_w2_k=v_nsa_cmp_w2_k, v_nsa_cmp_pos_v=v_nsa_cmp_pos_v, v_nsa_cmp_w1_v=v_nsa_cmp_w1_v, v_nsa_cmp_w2_v=v_nsa_cmp_w2_v, v_mla_q_norm_g=v_mla_q_norm_g, v_mla_w_uq=v_mla_w_uq, v_mla_kv_norm_g=v_mla_kv_norm_g, v_mla_w_ukv=v_mla_w_ukv)
    weights = {n: given[n] for n in TWIN_WEIGHTS}
    shared = {n: given[n] for n in SHARED_INPUTS}
    per_example = {n: given[n] for n in ['x']}
    grad_fn = _jax.value_and_grad(_loss, argnums=(0, 1))

    def one_microbatch(ex, loss_target):
        ex = dict(ex)
        diff = ex.pop(TWIN_DIFF_INPUT)
        return grad_fn(weights, diff, {**shared, **ex}, loss_target)

    if N_MICROBATCH == 1:
        loss, (grad_w, grad_x) = one_microbatch(per_example, given["loss_target"])
    else:
        def body(carry, xs):
            loss_sum, grad_sum = carry
            l_k, (gw_k, gx_k) = one_microbatch(xs[0], xs[1])
            with _jax.named_scope("update"):
                return (loss_sum + l_k, _jax.tree.map(_jnp.add, grad_sum, gw_k)), gx_k

        init = (_jnp.zeros((), _jnp.float32), _jax.tree.map(_jnp.zeros_like, weights))
        (loss, grad_w), grad_x = _jax.lax.scan(body, init, (per_example, given["loss_target"]))
    with _jax.named_scope("update"):
        delta_w, new_m, new_v = {}, {}, {}
        for n in TWIN_WEIGHTS:
            delta_w[n], new_m[n], new_v[n] = _adamw(weights[n], grad_w[n], given["m_" + n], given["v_" + n])
    return (loss, grad_x, *[grad_w[n] for n in TWIN_WEIGHTS], *[delta_w[n] for n in TWIN_WEIGHTS],
            *[new_m[n] for n in TWIN_WEIGHTS], *[new_v[n] for n in TWIN_WEIGHTS])
```

```python
import functools
import math

import numpy as np
import jax
import jax.numpy as jnp
from jax import lax
from jax.experimental import pallas as pl
from jax.experimental.pallas import tpu as pltpu

f32 = jnp.float32
bf16 = jnp.bfloat16
_MXU = jnp.bfloat16
_SDS = jax.ShapeDtypeStruct
_ANY = pl.BlockSpec(memory_space=pl.ANY)
_MESH = pl.DeviceIdType.MESH

D_MODEL = 2048
N_HEADS = 4
HEAD_DIM = 128
GROUP = 512
RMS_EPS = 1e-6
NEG_INF = -1e30
ROPE_THETA = 10000.0
CMP_LEN, CMP_STRIDE, SEL_LEN, SEL_TOPN, WINDOW = 32, 16, 64, 16, 512
FORCED_BONUS = 1e6
MLA_Q_RANK, MLA_KV_RANK, MLA_NOPE, MLA_ROPE = 384, 128, 128, 64
ADAM_LR, ADAM_B1, ADAM_B2, ADAM_EPS, ADAM_WD, ADAM_STEP = 0.001, 0.9, 0.999, 1e-08, 0.01, 10
LANES = 128
VMEM_LIMIT = 48 * 1024 * 1024

_SEGS = (
    ("sb_q", 512, 512), ("sb_k", 512, 512), ("sb_v", 512, 512), ("sb_gate", 512, 512),
    ("nsa_q", 512, 512), ("nsa_k_cmp", 128, 128), ("nsa_v_cmp", 128, 128), ("nsa_k_sel", 128, 128),
    ("nsa_v_sel", 128, 128), ("nsa_k_win", 128, 128), ("nsa_v_win", 128, 128), ("nsa_branch", 12, 128),
    ("nsa_gate", 512, 512), ("fox_q", 512, 512), ("fox_k", 512, 512), ("fox_v", 512, 512), ("fox_f", 4, 128),
    ("fox_gate", 512, 512), ("mla_cq", 384, 384), ("mla_ckv", 128, 128), ("mla_k_rope", 64, 128),
    ("mla_gate", 512, 512),
)
_ORIG, _AL, _WID = {}, {}, {}
_o = _a = 0
for _n, _w, _wa in _SEGS:
    _ORIG[_n], _AL[_n], _WID[_n] = _o, _a, _w
    _o += _w
    _a += _wa
IN_WIDTH, ZW = _o, _a


def _cp(sem=None):
    return pltpu.CompilerParams(dimension_semantics=sem, vmem_limit_bytes=VMEM_LIMIT)


def _mm(a, b):
    return jnp.dot(a.astype(_MXU), b.astype(_MXU), preferred_element_type=f32)


def _mm_nt(a, b):
    return lax.dot_general(a.astype(_MXU), b.astype(_MXU), (((1,), (1,)), ((), ())), preferred_element_type=f32)


def _mm_tn(a, b):
    return lax.dot_general(a.astype(_MXU), b.astype(_MXU), (((0,), (0,)), ((), ())), preferred_element_type=f32)


def _mm_split(x, t):
    hi = x.astype(_MXU)
    lo = (x - hi.astype(f32)).astype(_MXU)
    return jnp.dot(hi, t, preferred_element_type=f32) + jnp.dot(lo, t, preferred_element_type=f32)


def _sigmoid(x):
    return 1.0 / (1.0 + jnp.exp(-x))


def _iota(shape, dim):
    return lax.broadcasted_iota(jnp.int32, shape, dim)


def _pick(n, prefs):
    for p in prefs:
        if n % p == 0:
            return p
    return n


def _matmul(a, b, mode, *, bias=None, out_dtype=f32, name):
    if mode == "nn":
        (M, K), (K2, N) = a.shape, b.shape
    elif mode == "nt":
        (M, K), (N, K2) = a.shape, b.shape
    else:
        (K, M), (K2, N) = a.shape, b.shape
    assert K == K2
    tm = _pick(M, (512, 384, 256, 128))
    tn = _pick(N, (512, 384, 256, 128))
    tk = K if K <= 2048 else _pick(K, (2048, 2432, 1024, 512))
    nk = K // tk
    a_spec = {"nn": pl.BlockSpec((tm, tk), lambda i, j, k: (i, k)),
              "nt": pl.BlockSpec((tm, tk), lambda i, j, k: (i, k)),
              "tn": pl.BlockSpec((tk, tm), lambda i, j, k: (k, i))}[mode]
    b_spec = {"nn": pl.BlockSpec((tk, tn), lambda i, j, k: (k, j)),
              "nt": pl.BlockSpec((tn, tk), lambda i, j, k: (j, k)),
              "tn": pl.BlockSpec((tk, tn), lambda i, j, k: (k, j))}[mode]
    dot = {"nn": _mm, "nt": _mm_nt, "tn": _mm_tn}[mode]
    has_bias = bias is not None

    def body(*refs):
        if has_bias:
            a_ref, b_ref, bias_ref, o_ref, acc_ref = refs
        else:
            a_ref, b_ref, o_ref, acc_ref = refs
            bias_ref = None
        k = pl.program_id(2)
        part = dot(a_ref[...], b_ref[...])

        def finish(total):
            if has_bias:
                total = total + bias_ref[...]
            o_ref[...] = total.astype(o_ref.dtype)

        if nk == 1:
            finish(part)
        else:
            @pl.when(k == 0)
            def _():
                acc_ref[...] = part

            @pl.when(k > 0)
            def _():
                acc_ref[...] += part

            @pl.when(k == nk - 1)
            def _():
                finish(acc_ref[...])

    in_specs = [a_spec, b_spec]
    args = [a, b]
    if has_bias:
        in_specs.append(pl.BlockSpec((1, tn), lambda i, j, k: (0, j)))
        args.append(bias.reshape(1, N))
    return pl.pallas_call(
        body, out_shape=_SDS((M, N), out_dtype), grid=(M // tm, N // tn, nk),
        in_specs=in_specs, out_specs=pl.BlockSpec((tm, tn), lambda i, j, k: (i, j)),
        scratch_shapes=[pltpu.VMEM((tm, tn), f32)],
        compiler_params=_cp(("parallel", "parallel", "arbitrary")), name=name,
    )(*args)


def _row_block(s):
    return _pick(s, (256, 128))


def _rms_fwd(x, g, *, out_dtype, name):
    s, d = x.shape
    rb = _row_block(s)

    def body(x_ref, g_ref, o_ref):
        xv = x_ref[...]
        r = lax.rsqrt(jnp.mean(xv * xv, axis=-1, keepdims=True) + RMS_EPS)
        o_ref[...] = (xv * r * g_ref[...]).astype(o_ref.dtype)

    return pl.pallas_call(
        body, out_shape=_SDS((s, d), out_dtype), grid=(s // rb,),
        in_specs=[pl.BlockSpec((rb, d), lambda i: (i, 0)), pl.BlockSpec((1, d), lambda i: (0, 0))],
        out_specs=pl.BlockSpec((rb, d), lambda i: (i, 0)), compiler_params=_cp(("parallel",)), name=name,
    )(x, g.reshape(1, d))


def _postnorm_fwd(u, g, x, *, name):
    s, d = u.shape
    rb = _row_block(s)

    def body(u_ref, g_ref, x_ref, o_ref):
        uv = u_ref[...]
        r = lax.rsqrt(jnp.mean(uv * uv, axis=-1, keepdims=True) + RMS_EPS)
        o_ref[...] = x_ref[...] + uv * r * g_ref[...]

    return pl.pallas_call(
        body, out_shape=_SDS((s, d), f32), grid=(s // rb,),
        in_specs=[pl.BlockSpec((rb, d), lambda i: (i, 0)), pl.BlockSpec((1, d), lambda i: (0, 0)),
                  pl.BlockSpec((rb, d), lambda i: (i, 0))],
        out_specs=pl.BlockSpec((rb, d), lambda i: (i, 0)), compiler_params=_cp(("parallel",)), name=name,
    )(u, g.reshape(1, d), x)


def _fold_rows(v):
    r = v.shape[0]
    acc = v[0:8]
    for k in range(1, r // 8):
        acc = acc + v[8 * k:8 * k + 8]
    return acc


def _rms_bwd(dy, x, g, res=None, *, name):
    s, d = x.shape
    rb = _row_block(s)
    nb = s // rb
    has_res = res is not None

    def body(*refs):
        if has_res:
            dy_ref, x_ref, g_ref, res_ref, dx_ref, dg_ref, acc_ref = refs
        else:
            dy_ref, x_ref, g_ref, dx_ref, dg_ref, acc_ref = refs
        i = pl.program_id(0)
        xv = x_ref[...]
        r = lax.rsqrt(jnp.mean(xv * xv, axis=-1, keepdims=True) + RMS_EPS)
        xh = xv * r
        dyv = dy_ref[...]
        dxh = dyv * g_ref[...]
        dx = r * (dxh - xh * jnp.mean(dxh * xh, axis=-1, keepdims=True))
        if has_res:
            dx = dx + res_ref[...]
        dx_ref[...] = dx
        part = _fold_rows(dyv * xh)

        @pl.when(i == 0)
        def _():
            acc_ref[...] = part

        @pl.when(i > 0)
        def _():
            acc_ref[...] += part

        @pl.when(i == nb - 1)
        def _():
            dg_ref[...] = jnp.sum(acc_ref[...], axis=0, keepdims=True)

    blk = pl.BlockSpec((rb, d), lambda i: (i, 0))
    in_specs = [blk, blk, pl.BlockSpec((1, d), lambda i: (0, 0))] + ([blk] if has_res else [])
    args = [dy, x, g.reshape(1, d)] + ([res] if has_res else [])
    return pl.pallas_call(
        body, out_shape=(_SDS((s, d), f32), _SDS((1, d), f32)), grid=(nb,), in_specs=in_specs,
        out_specs=(blk, pl.BlockSpec((1, d), lambda i: (0, 0))),
        scratch_shapes=[pltpu.VMEM((8, d), f32)], compiler_params=_cp(("arbitrary",)), name=name,
    )(*args)


def _loss_head(y, target, *, name):
    s, d = y.shape
    rb = _row_block(s)
    nb = s // rb

    def body(y_ref, t_ref, dy_ref, l_ref):
        i = pl.program_id(0)
        e = y_ref[...] - t_ref[...]
        dy_ref[...] = e * (1.0 / d)
        rows = _fold_rows(e * e)
        part = rows[:, 0:LANES]
        for k in range(1, d // LANES):
            part = part + rows[:, k * LANES:(k + 1) * LANES]
        part = part * (0.5 / d)

        @pl.when(i == 0)
        def _():
            l_ref[...] = part

        @pl.when(i > 0)
        def _():
            l_ref[...] += part

    blk = pl.BlockSpec((rb, d), lambda i: (i, 0))
    return pl.pallas_call(
        body, out_shape=(_SDS((s, d), f32), _SDS((8, LANES), f32)), grid=(nb,), in_specs=[blk, blk],
        out_specs=(blk, pl.BlockSpec((8, LANES), lambda i: (0, 0))),
        compiler_params=_cp(("arbitrary",)), name=name,
    )(y, target)


def _colsum(a, *, name):
    s, n = a.shape
    rb = _row_block(s)
    nb = s // rb
    tn = _pick(n, (2432, 2048, 1024, 512, 384, 128))

    def body(a_ref, o_ref, acc_ref):
        i = pl.program_id(1)
        part = _fold_rows(a_ref[...])

        @pl.when(i == 0)
        def _():
            acc_ref[...] = part

        @pl.when(i > 0)
        def _():
            acc_ref[...] += part

        @pl.when(i == nb - 1)
        def _():
            o_ref[...] = jnp.sum(acc_ref[...], axis=0, keepdims=True)

    return pl.pallas_call(
        body, out_shape=_SDS((1, n), f32), grid=(n // tn, nb),
        in_specs=[pl.BlockSpec((rb, tn), lambda j, i: (i, j))], out_specs=pl.BlockSpec((1, tn), lambda j, i: (0, j)),
        scratch_shapes=[pltpu.VMEM((8, tn), f32)], compiler_params=_cp(("parallel", "arbitrary")), name=name,
    )(a)


def _gate_fwd(o, gate, *, name):
    s, d = o.shape
    rb = _row_block(s)

    def body(o_ref, g_ref, m_ref):
        gv = g_ref[...]
        m_ref[...] = (o_ref[...] * (gv * _sigmoid(gv))).astype(m_ref.dtype)

    blk = pl.BlockSpec((rb, d), lambda i: (i, 0))
    return pl.pallas_call(body, out_shape=_SDS((s, d), _MXU), grid=(s // rb,), in_specs=[blk, blk], out_specs=blk,
                          compiler_params=_cp(("parallel",)), name=name)(o, gate)


def _gate_bwd(dmix, o, gate, *, name):
    s, d = o.shape
    rb = _row_block(s)

    def body(dm_ref, o_ref, g_ref, do_ref, dg_ref):
        gv = g_ref[...]
        sg = _sigmoid(gv)
        dm = dm_ref[...]
        do_ref[...] = dm * (gv * sg)
        dg_ref[...] = dm * o_ref[...] * (sg * (1.0 + gv * (1.0 - sg)))

    blk = pl.BlockSpec((rb, d), lambda i: (i, 0))
    return pl.pallas_call(body, out_shape=(_SDS((s, d), f32), _SDS((s, d), f32)), grid=(s // rb,),
                          in_specs=[blk, blk, blk], out_specs=(blk, blk), compiler_params=_cp(("parallel",)),
                          name=name)(dmix, o, gate)


def _adamw(w, g, m, v, *, name):
    shape = w.shape
    cols = shape[-1]
    rows = int(np.prod(shape[:-1])) if len(shape) > 1 else 1
    to2 = lambda t: t.reshape(rows, cols)
    rb = _pick(rows, (128, 64, 32, 16, 8)) if rows * cols * 4 > (1 << 20) else rows

    def body(w_ref, g_ref, m_ref, v_ref, d_ref, nm_ref, nv_ref):
        gv = g_ref[...]
        mn = ADAM_B1 * m_ref[...] + (1.0 - ADAM_B1) * gv
        vn = ADAM_B2 * v_ref[...] + (1.0 - ADAM_B2) * (gv * gv)
        m_hat = mn / (1.0 - ADAM_B1 ** ADAM_STEP)
        v_hat = vn / (1.0 - ADAM_B2 ** ADAM_STEP)
        d_ref[...] = -ADAM_LR * (m_hat / (jnp.sqrt(v_hat) + ADAM_EPS) + ADAM_WD * w_ref[...])
        nm_ref[...] = mn
        nv_ref[...] = vn

    blk = pl.BlockSpec((rb, cols), lambda i: (i, 0))
    out = pl.pallas_call(body, out_shape=tuple(_SDS((rows, cols), f32) for _ in range(3)), grid=(rows // rb,),
                         in_specs=[blk] * 4, out_specs=(blk,) * 3, compiler_params=_cp(("parallel",)),
                         name=name)(to2(w), to2(g), to2(m), to2(v))
    return tuple(t.reshape(shape) for t in out)


def _sum_slots(a, *, name):
    p, n, c = a.shape
    rb = n if p * n * c * 4 <= (4 << 20) else _pick(n, (1024, 976, 512, 256, 128, 64, 32, 16, 8))

    def body(a_ref, o_ref):
        acc = a_ref[0]
        for k in range(1, p):
            acc = acc + a_ref[k]
        o_ref[...] = acc

    return pl.pallas_call(body, out_shape=_SDS((n, c), f32), grid=(n // rb,),
                          in_specs=[pl.BlockSpec((p, rb, c), lambda i: (0, i, 0))],
                          out_specs=pl.BlockSpec((rb, c), lambda i: (i, 0)), compiler_params=_cp(("parallel",)),
                          name=name)(a)


def _add2(a, b, *, name):
    p, n, c = a.shape
    rb = _pick(n, (1024, 976, 512, 256, 128, 64, 32, 16, 8))

    def body(a_ref, b_ref, o_ref):
        o_ref[...] = a_ref[...] + b_ref[...]

    blk = pl.BlockSpec((1, rb, c), lambda s, i: (s, i, 0))
    return pl.pallas_call(body, out_shape=_SDS((p, n, c), f32), grid=(p, n // rb), in_specs=[blk, blk], out_specs=blk,
                          compiler_params=_cp(("parallel", "parallel")), name=name)(a, b)


def _rope_tables(pos, dim):
    half = dim // 2
    inv = ROPE_THETA ** (-jnp.arange(half, dtype=f32) / half)
    ang = pos.astype(f32)[:, None] * inv[None, :]
    c, s = jnp.cos(ang), jnp.sin(ang)
    z = jnp.zeros_like(c)
    pad = [jnp.zeros((pos.shape[0], LANES - dim), f32)] if dim < LANES else []
    return (jnp.concatenate([c, c] + pad, axis=1), jnp.concatenate([-s, z] + pad, axis=1),
            jnp.concatenate([z, s] + pad, axis=1))


def _rope(x, cos, sa, sb, half, transpose=False):
    if transpose:
        return x * cos + pltpu.roll(x * sa, half, 1) + pltpu.roll(x * sb, LANES - half, 1)
    return x * cos + pltpu.roll(x, LANES - half, 1) * sa + pltpu.roll(x, half, 1) * sb


def _rope_call(items, tables, half, transpose, *, name):
    s = items[0][0].shape[0]
    rb = _row_block(s)
    n = len(items)

    def body(*refs):
        cos, sa, sb = refs[n][...], refs[n + 1][...], refs[n + 2][...]
        for k in range(n):
            x_ref, o_ref = refs[k], refs[n + 3 + k]
            for j in range(items[k][1] // LANES):
                sl = slice(j * LANES, (j + 1) * LANES)
                o_ref[:, sl] = _rope(x_ref[:, sl], cos, sa, sb, half, transpose)

    in_specs = [pl.BlockSpec((rb, w), functools.partial(lambda i, cb: (i, cb), cb=cb)) for _, w, cb in items]
    in_specs += [pl.BlockSpec((rb, LANES), lambda i: (i, 0))] * 3
    out_specs = tuple(pl.BlockSpec((rb, w), lambda i: (i, 0)) for _, w, _ in items)
    return pl.pallas_call(
        body, out_shape=tuple(_SDS((s, w), f32) for _, w, _ in items), grid=(s // rb,), in_specs=in_specs,
        out_specs=out_specs, compiler_params=_cp(("parallel",)), name=name,
    )(*[a for a, _, _ in items], *tables)


def _attn_block(s):
    return _pick(s, (256, 128))


def _lower_mask(b, strict):
    r, c = _iota((b, b), 0), _iota((b, b), 1)
    return (c < r) if strict else (c <= r)


def _pick_lane(block, h):
    return jnp.sum(jnp.where(_iota(block.shape, 1) == h, block, 0.0), axis=1, keepdims=True)


def _attn_fwd(q, k, v, qcol, kcol, vcol, dq, cum, cum_t, *, scale, name):
    s = q.shape[0]
    b = _attn_block(s)
    nq = s // b
    has_bias = cum is not None

    def body(*refs):
        if has_bias:
            q_ref, k_ref, v_ref, cum_ref, cumt_ref, o_ref, lse_ref = refs
        else:
            q_ref, k_ref, v_ref, o_ref, lse_ref = refs
        h, i = pl.program_id(0), pl.program_id(1)
        qv = q_ref[...].astype(_MXU)
        cq = _pick_lane(cum_ref[...], h) if has_bias else None

        def chunk(c, carry, diag):
            m, l, acc = carry
            st = pl.multiple_of(c * b, b)
            z = _mm_nt(qv, k_ref[pl.ds(st, b), :]) * scale
            if has_bias:
                z = z + cq - cumt_ref[c]
            if diag:
                mask = _lower_mask(b, False)
                z = jnp.where(mask, z, NEG_INF)
            m_new = jnp.maximum(m, jnp.max(z, axis=1, keepdims=True))
            p = jnp.exp(z - m_new)
            if diag:
                p = jnp.where(mask, p, 0.0)
            alpha = jnp.exp(m - m_new)
            l = alpha * l + jnp.sum(p, axis=1, keepdims=True)
            acc = alpha * acc + _mm(p, v_ref[pl.ds(st, b), :])
            return m_new, l, acc

        init = (jnp.full((b, 1), NEG_INF, f32), jnp.zeros((b, 1), f32), jnp.zeros((b, HEAD_DIM), f32))
        carry = lax.fori_loop(0, i, lambda c, cr: chunk(c, cr, False), init)
        m, l, acc = chunk(i, carry, True)
        o_ref[...] = acc / l
        lse_ref[...] = m + jnp.log(l)

    in_specs = [pl.BlockSpec((b, dq), lambda h, i: (i, qcol + h)), pl.BlockSpec((s, dq), lambda h, i: (0, kcol + h)),
                pl.BlockSpec((s, HEAD_DIM), lambda h, i: (0, vcol + h))]
    args = [q, k, v]
    if has_bias:
        in_specs += [pl.BlockSpec((b, LANES), lambda h, i: (i, 0)),
                     pl.BlockSpec((None, nq, 1, b), lambda h, i: (h, 0, 0, 0))]
        args += [cum, cum_t]
    return pl.pallas_call(
        body, out_shape=(_SDS((s, N_HEADS * HEAD_DIM), f32), _SDS((N_HEADS, s, 1), f32)), grid=(N_HEADS, nq),
        in_specs=in_specs,
        out_specs=(pl.BlockSpec((b, HEAD_DIM), lambda h, i: (i, h)), pl.BlockSpec((None, b, 1), lambda h, i: (h, i, 0))),
        compiler_params=_cp(("parallel", "parallel")), name=name,
    )(*args)


def _attn_bwd(q, k, v, qcol, kcol, vcol, dq, do, o, lse, cum, cum_t, *, scale, name):
    s = q.shape[0]
    b = _attn_block(s)
    nq = s // b
    has_bias = cum is not None

    def body(*refs):
        if has_bias:
            (q_ref, k_ref, v_ref, do_ref, o_ref, lse_ref, cum_ref, cumt_ref, dq_ref, dk_ref, dv_ref, dck_ref,
             p_sc, dp_sc) = refs
        else:
            q_ref, k_ref, v_ref, do_ref, o_ref, lse_ref, dq_ref, dk_ref, dv_ref = refs
        h, i = pl.program_id(0), pl.program_id(1)

        @pl.when(i == 0)
        def _():
            dk_ref[...] = jnp.zeros_like(dk_ref)
            dv_ref[...] = jnp.zeros_like(dv_ref)
            if has_bias:
                dck_ref[...] = jnp.zeros_like(dck_ref)

        qv = q_ref[...].astype(_MXU)
        dov = do_ref[...]
        dob = dov.astype(_MXU)
        lse_v = lse_ref[...]
        cq = _pick_lane(cum_ref[...], h) if has_bias else None

        def probs(c, diag):
            st = pl.multiple_of(c * b, b)
            z = _mm_nt(qv, k_ref[pl.ds(st, b), :]) * scale
            if has_bias:
                z = z + cq - cumt_ref[c]
            p = jnp.exp(z - lse_v)
            if diag:
                p = jnp.where(_lower_mask(b, False), p, 0.0)
            return p, _mm_nt(dob, v_ref[pl.ds(st, b), :])

        if has_bias:
            def first(c, acc, diag):
                p, dp = probs(c, diag)
                p_sc[c] = p
                dp_sc[c] = dp
                return acc + jnp.sum(p * dp, axis=1, keepdims=True)

            delta = lax.fori_loop(0, i, lambda c, a: first(c, a, False), jnp.zeros((b, 1), f32))
            delta = first(i, delta, True)
        else:
            delta = jnp.sum(dov * o_ref[...], axis=1, keepdims=True)

        def chunk(c, dq_acc, diag):
            st = pl.multiple_of(c * b, b)
            kc = k_ref[pl.ds(st, b), :]
            p, dp = (p_sc[c], dp_sc[c]) if has_bias else probs(c, diag)
            ds = p * (dp - delta)
            dk_ref[pl.ds(st, b), :] += _mm_tn(ds, qv) * scale
            dv_ref[pl.ds(st, b), :] += _mm_tn(p, dob)
            if has_bias:
                dck_ref[c] += -jnp.sum(ds, axis=0, keepdims=True)
            return dq_acc + _mm(ds, kc)

        acc = lax.fori_loop(0, i, lambda c, a: chunk(c, a, False), jnp.zeros((b, dq), f32))
        acc = chunk(i, acc, True)
        dq_ref[...] = acc * scale

    in_specs = [pl.BlockSpec((b, dq), lambda h, i: (i, qcol + h)), pl.BlockSpec((s, dq), lambda h, i: (0, kcol + h)),
                pl.BlockSpec((s, HEAD_DIM), lambda h, i: (0, vcol + h)),
                pl.BlockSpec((b, HEAD_DIM), lambda h, i: (i, h)), pl.BlockSpec((b, HEAD_DIM), lambda h, i: (i, h)),
                pl.BlockSpec((None, b, 1), lambda h, i: (h, i, 0))]
    args = [q, k, v, do, o, lse]
    out_shape = [_SDS((s, N_HEADS * dq), f32), _SDS((s, N_HEADS * dq), f32), _SDS((s, N_HEADS * HEAD_DIM), f32)]
    out_specs = [pl.BlockSpec((b, dq), lambda h, i: (i, h)), pl.BlockSpec((s, dq), lambda h, i: (0, h)),
                 pl.BlockSpec((s, HEAD_DIM), lambda h, i: (0, h))]
    if has_bias:
        in_specs += [pl.BlockSpec((b, LANES), lambda h, i: (i, 0)),
                     pl.BlockSpec((None, nq, 1, b), lambda h, i: (h, 0, 0, 0))]
        args += [cum, cum_t]
        out_shape.append(_SDS((N_HEADS, nq, 1, b), f32))
        out_specs.append(pl.BlockSpec((None, nq, 1, b), lambda h, i: (h, 0, 0, 0)))
    return pl.pallas_call(
        body, out_shape=tuple(out_shape), grid=(N_HEADS, nq), in_specs=in_specs, out_specs=tuple(out_specs),
        scratch_shapes=[pltpu.VMEM((nq, b, b), f32)] * 2 if has_bias else [],
        compiler_params=_cp(("parallel", "arbitrary")), name=name,
    )(*args)


def _tri(b, kind):
    r, c = _iota((b, b), 0), _iota((b, b), 1)
    cond = {"row_gt": r > c, "row_lt": r < c, "row_ge": r >= c, "row_le": r <= c}[kind]
    return jnp.where(cond, 1.0, 0.0).astype(_MXU)


def _log_keep(z):
    return -(jnp.maximum(z, 0.0) + jnp.log1p(jnp.exp(-jnp.abs(z))))


def _sb_fwd(z_all, *, name):
    s = z_all.shape[0]
    b = _attn_block(s)
    nq = s // b
    scale = HEAD_DIM ** -0.5
    qcol, kcol, vcol = (_AL[n] // HEAD_DIM for n in ("sb_q", "sb_k", "sb_v"))

    def body(q_ref, k_ref, v_ref, o_ref):
        i = pl.program_id(1)
        qv = q_ref[...].astype(_MXU)
        upper = _tri(b, "row_gt")

        def chunk(c, carry, diag):
            rsum, acc = carry
            st = pl.multiple_of(c * b, b)
            z = _mm_nt(qv, k_ref[pl.ds(st, b), :]) * scale
            lk = _log_keep(z)
            if diag:
                mask = _lower_mask(b, True)
                lk = jnp.where(mask, lk, 0.0)
            a = z + lk + _mm_split(lk, upper) + rsum
            if diag:
                a = jnp.where(mask, a, NEG_INF)
            acc = acc + _mm(jnp.exp(a), v_ref[pl.ds(st, b), :])
            return rsum + jnp.sum(lk, axis=1, keepdims=True), acc

        carry = chunk(i, (jnp.zeros((b, 1), f32), jnp.zeros((b, HEAD_DIM), f32)), True)
        _, acc = lax.fori_loop(0, i, lambda j, cr: chunk(i - 1 - j, cr, False), carry)
        o_ref[...] = acc

    return pl.pallas_call(
        body, out_shape=_SDS((s, GROUP), f32), grid=(N_HEADS, nq),
        in_specs=[pl.BlockSpec((b, HEAD_DIM), lambda h, i: (i, qcol + h)),
                  pl.BlockSpec((s, HEAD_DIM), lambda h, i: (0, kcol + h)),
                  pl.BlockSpec((s, HEAD_DIM), lambda h, i: (0, vcol + h))],
        out_specs=pl.BlockSpec((b, HEAD_DIM), lambda h, i: (i, h)),
        compiler_params=_cp(("parallel", "parallel")), name=name,
    )(z_all, z_all, z_all)


def _sb_bwd(z_all, do, *, name):
    s = z_all.shape[0]
    b = _attn_block(s)
    nq = s // b
    scale = HEAD_DIM ** -0.5
    qcol, kcol, vcol = (_AL[n] // HEAD_DIM for n in ("sb_q", "sb_k", "sb_v"))

    def body(q_ref, k_ref, v_ref, do_ref, dq_ref, dk_ref, dv_ref, z_sc, lk_sc, r_sc):
        i = pl.program_id(1)

        @pl.when(i == 0)
        def _():
            dk_ref[...] = jnp.zeros_like(dk_ref)
            dv_ref[...] = jnp.zeros_like(dv_ref)

        qv = q_ref[...].astype(_MXU)
        dob = do_ref[...].astype(_MXU)
        upper = _tri(b, "row_gt")
        lower = _tri(b, "row_lt")

        def scores(c, rsum, diag):
            st = pl.multiple_of(c * b, b)
            z = _mm_nt(qv, k_ref[pl.ds(st, b), :]) * scale
            lk = _log_keep(z)
            if diag:
                lk = jnp.where(_lower_mask(b, True), lk, 0.0)
            z_sc[c] = z
            lk_sc[c] = lk
            r_sc[c] = _mm_split(lk, upper) + rsum
            return rsum + jnp.sum(lk, axis=1, keepdims=True)

        rsum = scores(i, jnp.zeros((b, 1), f32), True)
        lax.fori_loop(0, i, lambda j, r: scores(i - 1 - j, r, False), rsum)

        def grads(c, carry, diag):
            psum, dq_acc = carry
            st = pl.multiple_of(c * b, b)
            z, lk = z_sc[c], lk_sc[c]
            lb = z + lk
            a = lb + r_sc[c]
            if diag:
                mask = _lower_mask(b, True)
                a = jnp.where(mask, a, NEG_INF)
            w = jnp.exp(a)
            e = _mm_nt(dob, v_ref[pl.ds(st, b), :]) * w
            before = _mm_split(e, lower) + psum
            dz = e * jnp.exp(lk) - before * jnp.exp(lb)
            if diag:
                dz = jnp.where(mask, dz, 0.0)
            kc = k_ref[pl.ds(st, b), :]
            dk_ref[pl.ds(st, b), :] += _mm_tn(dz, qv) * scale
            dv_ref[pl.ds(st, b), :] += _mm_tn(w, dob)
            return psum + jnp.sum(e, axis=1, keepdims=True), dq_acc + _mm(dz, kc)

        carry = lax.fori_loop(0, i, lambda c, cr: grads(c, cr, False),
                              (jnp.zeros((b, 1), f32), jnp.zeros((b, HEAD_DIM), f32)))
        _, dq_acc = grads(i, carry, True)
        dq_ref[...] = dq_acc * scale

    blk = pl.BlockSpec((b, HEAD_DIM), lambda h, i: (i, h))
    full = pl.BlockSpec((s, HEAD_DIM), lambda h, i: (0, h))
    return pl.pallas_call(
        body, out_shape=tuple(_SDS((s, GROUP), f32) for _ in range(3)), grid=(N_HEADS, nq),
        in_specs=[pl.BlockSpec((b, HEAD_DIM), lambda h, i: (i, qcol + h)),
                  pl.BlockSpec((s, HEAD_DIM), lambda h, i: (0, kcol + h)),
                  pl.BlockSpec((s, HEAD_DIM), lambda h, i: (0, vcol + h)), blk],
        out_specs=(blk, full, full),
        scratch_shapes=[pltpu.VMEM((nq, b, b), f32)] * 3,
        compiler_params=_cp(("parallel", "arbitrary")), name=name,
    )(z_all, z_all, z_all, do)


def _split3_left(t, x):
    hi = x.astype(_MXU)
    r1 = x - hi.astype(f32)
    mid = r1.astype(_MXU)
    lo = (r1 - mid.astype(f32)).astype(_MXU)
    dot = functools.partial(jnp.dot, preferred_element_type=f32)
    return dot(t, hi) + dot(t, mid) + dot(t, lo)


def _split3_right(x, t):
    hi = x.astype(_MXU)
    r1 = x - hi.astype(f32)
    mid = r1.astype(_MXU)
    lo = (r1 - mid.astype(f32)).astype(_MXU)
    dot = functools.partial(jnp.dot, preferred_element_type=f32)
    return dot(hi, t) + dot(mid, t) + dot(lo, t)


def _fox_cum_fwd(z_all, bias, *, name):
    s = z_all.shape[0]
    b = _attn_block(s)
    fcol = _AL["fox_f"] // LANES

    def body(f_ref, b_ref, cum_ref, cumt_ref, carry_ref):
        i = pl.program_id(0)

        @pl.when(i == 0)
        def _():
            carry_ref[...] = jnp.zeros_like(carry_ref)

        u = f_ref[...] + b_ref[...]
        lf = jnp.minimum(u, 0.0) - jnp.log1p(jnp.exp(-jnp.abs(u)))
        cum = _split3_left(_tri(b, "row_ge"), lf) + carry_ref[...]
        cum_ref[...] = cum
        cumt_ref[...] = cum.T[0:8, :]
        carry_ref[...] = cum_ref[b - 1:b, :]

    return pl.pallas_call(
        body, out_shape=(_SDS((s, LANES), f32), _SDS((8, s), f32)), grid=(s // b,),
        in_specs=[pl.BlockSpec((b, LANES), lambda i: (i, fcol)), pl.BlockSpec((1, LANES), lambda i: (0, 0))],
        out_specs=(pl.BlockSpec((b, LANES), lambda i: (i, 0)), pl.BlockSpec((8, b), lambda i: (0, i))),
        scratch_shapes=[pltpu.VMEM((1, LANES), f32)], compiler_params=_cp(("arbitrary",)), name=name,
    )(z_all, bias)


def _fox_cum_bwd(z_all, bias, dcum_t, *, name):
    s = z_all.shape[0]
    b = _attn_block(s)
    nb = s // b
    fcol = _AL["fox_f"] // LANES

    def body(f_ref, b_ref, dc_ref, df_ref, db_ref, carry_ref):
        i = pl.program_id(0)

        @pl.when(i == 0)
        def _():
            carry_ref[...] = jnp.zeros_like(carry_ref)
            db_ref[...] = jnp.zeros_like(db_ref)

        dc = dc_ref[...]
        rev = _split3_right(dc, _tri(b, "row_ge")) + carry_ref[...]
        carry_ref[...] = carry_ref[...] + jnp.sum(dc, axis=1, keepdims=True)
        dlf = jnp.concatenate([rev, jnp.zeros((LANES - 8, b), f32)], axis=0).T
        u = f_ref[...] + b_ref[...]
        df = jnp.where(_iota((b, LANES), 1) < N_HEADS, dlf * (1.0 - _sigmoid(u)), 0.0)
        df_ref[...] = df
        db_ref[...] += jnp.sum(df, axis=0, keepdims=True)

    return pl.pallas_call(
        body, out_shape=(_SDS((s, LANES), f32), _SDS((1, LANES), f32)), grid=(nb,),
        in_specs=[pl.BlockSpec((b, LANES), lambda i: (nb - 1 - i, fcol)), pl.BlockSpec((1, LANES), lambda i: (0, 0)),
                  pl.BlockSpec((8, b), lambda i: (0, nb - 1 - i))],
        out_specs=(pl.BlockSpec((b, LANES), lambda i: (nb - 1 - i, 0)), pl.BlockSpec((1, LANES), lambda i: (0, 0))),
        scratch_shapes=[pltpu.VMEM((8, 1), f32)], compiler_params=_cp(("arbitrary",)), name=name,
    )(z_all, bias, dcum_t)


MLA_QW = 2 * LANES


def _rms_rows(x):
    r = lax.rsqrt(jnp.mean(x * x, axis=-1, keepdims=True) + RMS_EPS)
    return x * r, r


def _mla_prep_fwd(z_all, gq, gkv, wuq, wk, wv, tables, *, name):
    s = z_all.shape[0]
    rb = _row_block(s)
    half = MLA_ROPE // 2

    def body(cq_ref, ckv_ref, kr_ref, gq_ref, gkv_ref, wuq_ref, wk_ref, wv_ref, cos_ref, sa_ref, sb_ref,
             q_ref, k_ref, v_ref):
        cos, sa, sb = cos_ref[...], sa_ref[...], sb_ref[...]
        xh, _ = _rms_rows(cq_ref[...])
        qp = _mm(xh * gq_ref[...], wuq_ref[...])
        kh, _ = _rms_rows(ckv_ref[...])
        nkv = kh * gkv_ref[...]
        kn = _mm(nkv, wk_ref[...])
        v_ref[...] = _mm(nkv, wv_ref[...])
        kr = _rope(kr_ref[...], cos, sa, sb, half)
        for h in range(N_HEADS):
            lo, mid, hi = h * MLA_QW, h * MLA_QW + LANES, (h + 1) * MLA_QW
            q_ref[:, lo:mid] = qp[:, lo:mid]
            q_ref[:, mid:hi] = _rope(qp[:, mid:hi], cos, sa, sb, half)
            k_ref[:, lo:mid] = kn[:, h * LANES:(h + 1) * LANES]
            k_ref[:, mid:hi] = kr

    row = lambda w, cb: pl.BlockSpec((rb, w), lambda i: (i, cb))
    whole = lambda a: pl.BlockSpec(a.shape, lambda i: (0,) * a.ndim)
    return pl.pallas_call(
        body, out_shape=(_SDS((s, N_HEADS * MLA_QW), f32), _SDS((s, N_HEADS * MLA_QW), f32), _SDS((s, GROUP), f32)),
        grid=(s // rb,),
        in_specs=[row(MLA_Q_RANK, _AL["mla_cq"] // MLA_Q_RANK), row(LANES, _AL["mla_ckv"] // LANES),
                  row(LANES, _AL["mla_k_rope"] // LANES), whole(gq), whole(gkv), whole(wuq), whole(wk), whole(wv),
                  row(LANES, 0), row(LANES, 0), row(LANES, 0)],
        out_specs=(row(N_HEADS * MLA_QW, 0), row(N_HEADS * MLA_QW, 0), row(GROUP, 0)),
        compiler_params=_cp(("parallel",)), name=name,
    )(z_all, z_all, z_all, gq, gkv, wuq, wk, wv, *tables)


def _mla_prep_bwd(z_all, gq, gkv, wuq, wk, wv, tables, dq_cat, dk_cat, dv, *, name):
    s = z_all.shape[0]
    rb = _row_block(s)
    half = MLA_ROPE // 2

    def body(cq_ref, ckv_ref, gq_ref, gkv_ref, wuq_ref, wk_ref, wv_ref, cos_ref, sa_ref, sb_ref, dq_ref, dk_ref,
             dv_ref, dcq_ref, dckv_ref, dkr_ref, dwuq_ref, dwk_ref, dwv_ref, dgq_ref, dgkv_ref):
        i = pl.program_id(0)

        @pl.when(i == 0)
        def _():
            for r in (dwuq_ref, dwk_ref, dwv_ref, dgq_ref, dgkv_ref):
                r[...] = jnp.zeros_like(r)

        cos, sa, sb = cos_ref[...], sa_ref[...], sb_ref[...]
        parts, knp = [], []
        dkr = jnp.zeros((rb, LANES), f32)
        for h in range(N_HEADS):
            lo, mid, hi = h * MLA_QW, h * MLA_QW + LANES, (h + 1) * MLA_QW
            parts += [dq_ref[:, lo:mid], _rope(dq_ref[:, mid:hi], cos, sa, sb, half, transpose=True)]
            knp.append(dk_ref[:, lo:mid])
            dkr = dkr + _rope(dk_ref[:, mid:hi], cos, sa, sb, half, transpose=True)
        dkr_ref[...] = dkr
        dqp = jnp.concatenate(parts, axis=1)
        dkn = jnp.concatenate(knp, axis=1)
        dvv = dv_ref[...]

        def norm_bwd(x_ref, g_ref, w_pairs, dx_ref, dg_ref):
            xh, r = _rms_rows(x_ref[...])
            nx = xh * g_ref[...]
            dn = jnp.zeros_like(xh)
            for w_ref, dw_ref, dy in w_pairs:
                dw_ref[...] += _mm_tn(nx, dy)
                dn = dn + _mm_nt(dy, w_ref[...])
            dxh = dn * g_ref[...]
            dx_ref[...] = r * (dxh - xh * jnp.mean(dxh * xh, axis=-1, keepdims=True))
            dg_ref[...] += jnp.sum(dn * xh, axis=0, keepdims=True)

        norm_bwd(cq_ref, gq_ref, [(wuq_ref, dwuq_ref, dqp)], dcq_ref, dgq_ref)
        norm_bwd(ckv_ref, gkv_ref, [(wk_ref, dwk_ref, dkn), (wv_ref, dwv_ref, dvv)], dckv_ref, dgkv_ref)

    row = lambda w, cb: pl.BlockSpec((rb, w), lambda i: (i, cb))
    whole = lambda a: pl.BlockSpec(a.shape, lambda i: (0,) * a.ndim)
    return pl.pallas_call(
        body,
        out_shape=(_SDS((s, MLA_Q_RANK), f32), _SDS((s, LANES), f32), _SDS((s, LANES), f32), _SDS(wuq.shape, f32),
                   _SDS(wk.shape, f32), _SDS(wv.shape, f32), _SDS(gq.shape, f32), _SDS(gkv.shape, f32)),
        grid=(s // rb,),
        in_specs=[row(MLA_Q_RANK, _AL["mla_cq"] // MLA_Q_RANK), row(LANES, _AL["mla_ckv"] // LANES), whole(gq),
                  whole(gkv), whole(wuq), whole(wk), whole(wv), row(LANES, 0), row(LANES, 0), row(LANES, 0),
                  row(N_HEADS * MLA_QW, 0), row(N_HEADS * MLA_QW, 0), row(GROUP, 0)],
        out_specs=(row(MLA_Q_RANK, 0), row(LANES, 0), row(LANES, 0), whole(wuq), whole(wk), whole(wv), whole(gq),
                   whole(gkv)),
        compiler_params=_cp(("arbitrary",)), name=name,
    )(z_all, z_all, gq, gkv, wuq, wk, wv, *tables, dq_cat, dk_cat, dv)


def _silu_grad(x):
    sg = _sigmoid(x)
    return sg * (1.0 + x * (1.0 - sg))


def _nsa_cmp_fwd(ra, rb_, pos, w1, w2, tables, *, name):
    nr = ra.shape[1]
    hw = ra.shape[2]

    def body(ra_ref, rb_ref, pos_ref, w1_ref, w2_ref, cos_ref, sa_ref, sb_ref, out_ref, hp_ref):
        for k in range(2):
            xa = ra_ref[k] + pos_ref[k, :, 0:hw]
            xb = rb_ref[k] + pos_ref[k, :, hw:2 * hw]
            hp = _mm(xa, w1_ref[k, 0:hw, :]) + _mm(xb, w1_ref[k, hw:2 * hw, :])
            hp_ref[k] = hp
            out = _mm(hp * _sigmoid(hp), w2_ref[k])
            if k == 0:
                out = _rope(out, cos_ref[...], sa_ref[...], sb_ref[...], HEAD_DIM // 2)
            out_ref[k] = out

    return pl.pallas_call(body, out_shape=(_SDS((2, nr, HEAD_DIM), f32), _SDS((2, nr, HEAD_DIM), f32)),
                          compiler_params=_cp(), name=name)(ra, rb_, pos, w1, w2, *tables)


def _nsa_cmp_bwd(ra, rb_, pos, w1, w2, tables, hp, dout, *, name):
    nr = ra.shape[1]
    hw = ra.shape[2]

    def body(ra_ref, rb_ref, pos_ref, w1_ref, w2_ref, cos_ref, sa_ref, sb_ref, hp_ref, do_ref,
             dxa_ref, dxb_ref, dw1_ref, dw2_ref):
        for k in range(2):
            d_out = do_ref[k]
            if k == 0:
                d_out = _rope(d_out, cos_ref[...], sa_ref[...], sb_ref[...], HEAD_DIM // 2, transpose=True)
            hpv = hp_ref[k]
            dw2_ref[k] = _mm_tn(hpv * _sigmoid(hpv), d_out)
            dhp = _mm_nt(d_out, w2_ref[k]) * _silu_grad(hpv)
            xa = ra_ref[k] + pos_ref[k, :, 0:hw]
            xb = rb_ref[k] + pos_ref[k, :, hw:2 * hw]
            dw1_ref[k, 0:hw, :] = _mm_tn(xa, dhp)
            dw1_ref[k, hw:2 * hw, :] = _mm_tn(xb, dhp)
            dxa_ref[k] = _mm_nt(dhp, w1_ref[k, 0:hw, :])
            dxb_ref[k] = _mm_nt(dhp, w1_ref[k, hw:2 * hw, :])

    return pl.pallas_call(
        body, out_shape=(_SDS((2, nr, hw), f32), _SDS((2, nr, hw), f32), _SDS(w1.shape, f32), _SDS(w2.shape, f32)),
        compiler_params=_cp(), name=name)(ra, rb_, pos, w1, w2, *tables, hp, dout)


def _nsa_consts(s):
    b = _attn_block(s)
    nr = s // CMP_STRIDE
    n_cmp = (s - CMP_LEN) // CMP_STRIDE + 1
    n_sel = s // SEL_LEN
    cmp_start = np.arange(n_cmp) * CMP_STRIDE
    sel_start = np.arange(n_sel) * SEL_LEN
    overlap = np.clip(np.minimum(cmp_start[:, None] + CMP_LEN, sel_start[None, :] + SEL_LEN)
                      - np.maximum(cmp_start[:, None], sel_start[None, :]), 0, None)
    m2s = np.zeros((nr, LANES), np.float32)
    m2s[:n_cmp, :n_sel] = overlap / CMP_LEN
    e3 = np.zeros((s // b, LANES, b), np.float32)
    tok = np.arange(s)
    e3[tok // b, tok // SEL_LEN, tok % b] = 1.0
    return jnp.asarray(m2s, _MXU), jnp.asarray(e3, _MXU)


def _nsa_masks(i, b, d):
    qpos = i * b + _iota((b, b), 0)
    kpos = (i - d) * b + _iota((b, b), 1)
    return (kpos <= qpos) & (kpos > qpos - WINDOW)


def _nsa_fwd(qr, kvc, ksr, vs, kwr, vw, z_all, m2s, e3, *, name):
    s = qr.shape[0]
    b = _attn_block(s)
    nq = s // b
    nr = kvc.shape[1]
    n_sel = s // SEL_LEN
    top_n = min(SEL_TOPN, n_sel)
    nd = -(-WINDOW // b)
    scale = HEAD_DIM ** -0.5
    bcol = _AL["nsa_branch"] // LANES
    H = N_HEADS

    def body(q_ref, kvc_ref, ks_ref, vs_ref, kw_ref, vw_ref, br_ref, m2s_ref, e3_ref,
             o_ref, oc_ref, os_ref, ow_ref, st_ref, sel_ref, m_sc, l_sc, acc_sc):
        i = pl.program_id(0)
        lane = _iota((b, LANES), 1)
        hs = lambda h: slice(h * HEAD_DIM, (h + 1) * HEAD_DIM)

        cmp_mask = (CMP_STRIDE * _iota((b, nr), 1) + (CMP_LEN - 1)) <= (i * b + _iota((b, nr), 0))
        imp = jnp.zeros((b, LANES), f32)
        stats = jnp.zeros((b, LANES), f32)
        for h in range(H):
            zc = jnp.where(cmp_mask, _mm_nt(q_ref[:, hs(h)], kvc_ref[0]) * scale, NEG_INF)
            m = jnp.max(zc, axis=1, keepdims=True)
            p = jnp.where(cmp_mask, jnp.exp(zc - m), 0.0)
            l = jnp.sum(p, axis=1, keepdims=True)
            some = l > 0.0
            lsafe = jnp.where(some, l, 1.0)
            pc = p * jnp.where(some, 1.0 / lsafe, 0.0)
            oc_ref[:, hs(h)] = _mm(pc, kvc_ref[1])
            imp = imp + _mm(pc, m2s_ref[...])
            stats = jnp.where(lane == h, jnp.where(some, m + jnp.log(lsafe), 0.0), stats)

        cur = jnp.right_shift(i * b + _iota((b, LANES), 0), int(math.log2(SEL_LEN)))
        forced = (lane == 0) | (lane == cur) | (lane == cur - 1)
        score = jnp.where(lane <= cur, jnp.where(forced, FORCED_BONUS, imp), NEG_INF)
        score = jnp.where(lane < n_sel, score, -3e38)
        rank = jnp.zeros((b, LANES), f32)
        for j in range(n_sel):
            col = score[:, j:j + 1]
            rank = rank + jnp.where(col > score, 1.0, jnp.where(col == score, jnp.where(lane > j, 1.0, 0.0), 0.0))
        sel = jnp.where(lane < n_sel, jnp.where(rank < top_n, 1.0, 0.0), 0.0)
        sel_ref[...] = sel
        sel_b = sel.astype(_MXU)

        def reset():
            m_sc[...] = jnp.full(m_sc.shape, NEG_INF, f32)
            l_sc[...] = jnp.zeros_like(l_sc)
            acc_sc[...] = jnp.zeros_like(acc_sc)

        def update(h, z, mask, vch):
            zm = jnp.where(mask, z, NEG_INF)
            m_old = m_sc[h]
            m_new = jnp.maximum(m_old, jnp.max(zm, axis=1, keepdims=True))
            p = jnp.where(mask, jnp.exp(zm - m_new), 0.0)
            alpha = jnp.exp(m_old - m_new)
            l_sc[h] = alpha * l_sc[h] + jnp.sum(p, axis=1, keepdims=True)
            acc_sc[h] = alpha * acc_sc[h] + _mm(p, vch)
            m_sc[h] = m_new

        def finish(out_ref, branch, stats):
            for h in range(H):
                out_ref[:, hs(h)] = acc_sc[h] / l_sc[h]
                stats = jnp.where(lane == 4 * branch + h, m_sc[h] + jnp.log(l_sc[h]), stats)
            return stats

        def sel_chunk(c, diag):
            st = pl.multiple_of(c * b, b)
            mask = _mm(sel_b, e3_ref[c]) > 0.5
            if diag:
                mask = mask & _lower_mask(b, False)
            kch, vch = ks_ref[pl.ds(st, b), :], vs_ref[pl.ds(st, b), :]
            for h in range(H):
                update(h, _mm_nt(q_ref[:, hs(h)], kch) * scale, mask, vch)

        reset()

        def sel_loop(c, carry):
            sel_chunk(c, False)
            return carry

        lax.fori_loop(0, i, sel_loop, 0)
        sel_chunk(i, True)
        stats = finish(os_ref, 1, stats)

        reset()
        for d in range(nd, -1, -1):
            @pl.when(i >= d)
            def _():
                st = pl.multiple_of((i - d) * b, b)
                mask = _nsa_masks(i, b, d)
                kch, vch = kw_ref[pl.ds(st, b), :], vw_ref[pl.ds(st, b), :]
                for h in range(H):
                    update(h, _mm_nt(q_ref[:, hs(h)], kch) * scale, mask, vch)
        stats = finish(ow_ref, 2, stats)
        st_ref[...] = stats

        g = _sigmoid(br_ref[...])
        for h in range(H):
            o_ref[:, hs(h)] = (g[:, 3 * h:3 * h + 1] * oc_ref[:, hs(h)] + g[:, 3 * h + 1:3 * h + 2] * os_ref[:, hs(h)]
                               + g[:, 3 * h + 2:3 * h + 3] * ow_ref[:, hs(h)])

    blk = lambda w: pl.BlockSpec((b, w), lambda i: (i, 0))
    whole = lambda a: pl.BlockSpec(a.shape, lambda i: (0,) * a.ndim)
    return pl.pallas_call(
        body, out_shape=tuple(_SDS((s, GROUP), f32) for _ in range(4)) + (_SDS((s, LANES), f32), _SDS((s, LANES), f32)),
        grid=(nq,),
        in_specs=[blk(GROUP), whole(kvc), whole(ksr), whole(vs), whole(kwr), whole(vw),
                  pl.BlockSpec((b, LANES), lambda i: (i, bcol)), whole(m2s), whole(e3)],
        out_specs=(blk(GROUP),) * 4 + (blk(LANES), blk(LANES)),
        scratch_shapes=[pltpu.VMEM((H, b, 1), f32), pltpu.VMEM((H, b, 1), f32), pltpu.VMEM((H, b, HEAD_DIM), f32)],
        compiler_params=_cp(("parallel",)), name=name,
    )(qr, kvc, ksr, vs, kwr, vw, z_all, m2s, e3)


def _nsa_bwd(do, qr, kvc, ksr, vs, kwr, vw, z_all, oc, os_, ow, stats, sel, e3, *, name):
    s = qr.shape[0]
    b = _attn_block(s)
    nq = s // b
    nr = kvc.shape[1]
    nd = -(-WINDOW // b)
    scale = HEAD_DIM ** -0.5
    bcol = _AL["nsa_branch"] // LANES
    H = N_HEADS

    def body(do_ref, q_ref, kvc_ref, ks_ref, vs_ref, kw_ref, vw_ref, br_ref, oc_ref, os_ref, ow_ref, st_ref, sel_ref,
             e3_ref, dq_ref, dbr_ref, dkvc_ref, dks_ref, dvs_ref, dkw_ref, dvw_ref, dob_sc, delta_sc, dq_sc):
        i = pl.program_id(0)

        @pl.when(i == 0)
        def _():
            for r in (dkvc_ref, dks_ref, dvs_ref, dkw_ref, dvw_ref):
                r[...] = jnp.zeros_like(r)

        lane = _iota((b, LANES), 1)
        hs = lambda h: slice(h * HEAD_DIM, (h + 1) * HEAD_DIM)
        g = _sigmoid(br_ref[...])
        stats = st_ref[...]
        dbr = jnp.zeros((b, LANES), f32)
        outs = (oc_ref, os_ref, ow_ref)
        for h in range(H):
            doh = do_ref[:, hs(h)]
            for j in range(3):
                gj = g[:, 3 * h + j:3 * h + j + 1]
                dgj = jnp.sum(doh * outs[j][:, hs(h)], axis=1, keepdims=True)
                dbr = jnp.where(lane == 3 * h + j, dgj * gj * (1.0 - gj), dbr)
                dob_sc[j, :, hs(h)] = gj * doh
                delta_sc[j, h] = gj * dgj
        dbr_ref[...] = dbr
        dq_sc[...] = jnp.zeros_like(dq_sc)

        def branch(j, h, z, mask, kch, vch):
            qh = q_ref[:, hs(h)]
            p = jnp.where(mask, jnp.exp(jnp.where(mask, z, NEG_INF) - stats[:, 4 * j + h:4 * j + h + 1]), 0.0)
            dob = dob_sc[j, :, hs(h)]
            ds = p * (_mm_nt(dob, vch) - delta_sc[j, h])
            dq_sc[:, hs(h)] += _mm(ds, kch) * scale
            return _mm_tn(ds, qh) * scale, _mm_tn(p, dob)

        cmp_mask = (CMP_STRIDE * _iota((b, nr), 1) + (CMP_LEN - 1)) <= (i * b + _iota((b, nr), 0))
        kc, vc = kvc_ref[0], kvc_ref[1]
        for h in range(H):
            dk, dv = branch(0, h, _mm_nt(q_ref[:, hs(h)], kc) * scale, cmp_mask, kc, vc)
            dkvc_ref[0] += dk
            dkvc_ref[1] += dv

        sel_b = sel_ref[...].astype(_MXU)

        def chunk(j, c, mask, k_ref, v_ref, dk_ref, dv_ref):
            st = pl.multiple_of(c * b, b)
            kch, vch = k_ref[pl.ds(st, b), :], v_ref[pl.ds(st, b), :]
            dk = jnp.zeros((b, HEAD_DIM), f32)
            dv = jnp.zeros((b, HEAD_DIM), f32)
            for h in range(H):
                dkh, dvh = branch(j, h, _mm_nt(q_ref[:, hs(h)], kch) * scale, mask, kch, vch)
                dk, dv = dk + dkh, dv + dvh
            dk_ref[pl.ds(st, b), :] += dk
            dv_ref[pl.ds(st, b), :] += dv

        def sel_chunk(c, diag):
            mask = _mm(sel_b, e3_ref[c]) > 0.5
            if diag:
                mask = mask & _lower_mask(b, False)
            chunk(1, c, mask, ks_ref, vs_ref, dks_ref, dvs_ref)

        def sel_loop(c, carry):
            sel_chunk(c, False)
            return carry

        lax.fori_loop(0, i, sel_loop, 0)
        sel_chunk(i, True)

        for d in range(nd, -1, -1):
            @pl.when(i >= d)
            def _():
                chunk(2, i - d, _nsa_masks(i, b, d), kw_ref, vw_ref, dkw_ref, dvw_ref)

        dq_ref[...] = dq_sc[...]

    blk = lambda w: pl.BlockSpec((b, w), lambda i: (i, 0))
    whole = lambda a: pl.BlockSpec(a.shape, lambda i: (0,) * a.ndim)
    stream = _SDS((s, HEAD_DIM), f32)
    return pl.pallas_call(
        body, out_shape=(_SDS((s, GROUP), f32), _SDS((s, LANES), f32), _SDS(kvc.shape, f32), stream, stream, stream,
                         stream),
        grid=(nq,),
        in_specs=[blk(GROUP), blk(GROUP), whole(kvc), whole(ksr), whole(vs), whole(kwr), whole(vw),
                  pl.BlockSpec((b, LANES), lambda i: (i, bcol)), blk(GROUP), blk(GROUP), blk(GROUP), blk(LANES),
                  blk(LANES), whole(e3)],
        out_specs=(blk(GROUP), blk(LANES), whole(kvc), whole(ksr), whole(vs), whole(kwr), whole(vw)),
        scratch_shapes=[pltpu.VMEM((3, b, GROUP), f32), pltpu.VMEM((3, H, b, 1), f32), pltpu.VMEM((b, GROUP), f32)],
        compiler_params=_cp(("arbitrary",)), name=name,
    )(do, qr, kvc, ksr, vs, kwr, vw, z_all, oc, os_, ow, stats, sel, e3)


def _seg(a, name, width=None):
    return a[:, _AL[name]:_AL[name] + (width or _WID[name])]


def _cmp_rows(tok):
    s = tok.shape[0]
    r = tok.reshape(s // CMP_STRIDE, CMP_STRIDE * HEAD_DIM)
    return r, jnp.concatenate([r[1:], jnp.zeros((1, r.shape[1]), r.dtype)], axis=0)


def _cmp_unrows(dxa, dxb):
    s = dxa.shape[0] * CMP_STRIDE
    return (dxa + jnp.concatenate([jnp.zeros((1, dxa.shape[1]), dxa.dtype), dxb[:-1]], axis=0)).reshape(s, HEAD_DIM)


_GATES = ("sb_gate", "nsa_gate", "fox_gate", "mla_gate")


def _layer_fwd(x, p, c, tag):
    s = x.shape[0]
    b = _attn_block(s)
    h = _rms_fwd(x, p["pre_g"], out_dtype=_MXU, name=f"prenorm_{tag}")
    z = _matmul(h, p["w_in"], "nn", bias=p["b_in"], name=f"inproj_{tag}")
    o_sb = _sb_fwd(z, name=f"sb_fwd_{tag}")

    qr, ksr, kwr = _rope_call([(z, GROUP, _AL["nsa_q"] // GROUP), (z, LANES, _AL["nsa_k_sel"] // LANES),
                               (z, LANES, _AL["nsa_k_win"] // LANES)], c["tabs128"], HEAD_DIM // 2, False,
                              name=f"nsa_rope_{tag}")
    (rak, rbk), (rav, rbv) = _cmp_rows(_seg(z, "nsa_k_cmp")), _cmp_rows(_seg(z, "nsa_v_cmp"))
    ra, rb_ = jnp.stack([rak, rav]), jnp.stack([rbk, rbv])
    kvc, hp = _nsa_cmp_fwd(ra, rb_, p["cmp_pos"], p["cmp_w1"], p["cmp_w2"], c["tabs_cmp"], name=f"nsa_cmp_{tag}")
    vs, vw = _seg(z, "nsa_v_sel"), _seg(z, "nsa_v_win")
    o_nsa, oc, os_, ow, stats, sel = _nsa_fwd(qr, kvc, ksr, vs, kwr, vw, z, c["m2s"], c["e3"], name=f"nsa_fwd_{tag}")

    cum, cum_t8 = _fox_cum_fwd(z, p["fox_bias"], name=f"fox_cum_{tag}")
    cum_t = cum_t8.reshape(8, s // b, 1, b)
    fcols = tuple(_AL[n] // HEAD_DIM for n in ("fox_q", "fox_k", "fox_v"))
    o_fox, lse_fox = _attn_fwd(z, z, z, *fcols, HEAD_DIM, cum, cum_t, scale=HEAD_DIM ** -0.5, name=f"fox_fwd_{tag}")

    qcat, kcat, vm = _mla_prep_fwd(z, p["gq"], p["gkv"], p["wuq"], p["wk"], p["wv"], c["tabs64"],
                                   name=f"mla_prep_{tag}")
    o_mla, lse_mla = _attn_fwd(qcat, kcat, vm, 0, 0, 0, MLA_QW, None, None, scale=(MLA_NOPE + MLA_ROPE) ** -0.5,
                               name=f"mla_fwd_{tag}")

    o_all = jnp.concatenate([o_sb, o_nsa, o_fox, o_mla], axis=1)
    gates = jnp.concatenate([_seg(z, n) for n in _GATES], axis=1)
    mix = _gate_fwd(o_all, gates, name=f"gate_{tag}")
    u = _matmul(mix, p["w_out"], "nn", name=f"outproj_{tag}")
    y = _postnorm_fwd(u, p["post_g"], x, name=f"postnorm_{tag}")
    saved = dict(x=x, h=h, z=z, qr=qr, ksr=ksr, kwr=kwr, ra=ra, rb=rb_, kvc=kvc, hp=hp, vs=vs, vw=vw, oc=oc, os=os_,
                 ow=ow, stats=stats, sel=sel, cum=cum, cum_t=cum_t, o_fox=o_fox, lse_fox=lse_fox, qcat=qcat, kcat=kcat,
                 vm=vm, o_mla=o_mla, lse_mla=lse_mla, o_all=o_all, gates=gates, mix=mix, u=u)
    return y, saved


def _layer_bwd(dy, sv, p, c, tag):
    z = sv["z"]
    s = z.shape[0]
    du, dg_post = _rms_bwd(dy, sv["u"], p["post_g"], name=f"postnorm_bwd_{tag}")
    dmix = _matmul(du, p["w_out"], "nt", name=f"outproj_dx_{tag}")
    dw_out = _matmul(sv["mix"], du, "tn", name=f"outproj_dw_{tag}")
    do_all, dgates = _gate_bwd(dmix, sv["o_all"], sv["gates"], name=f"gate_bwd_{tag}")
    do_sb, do_nsa, do_fox, do_mla = (do_all[:, k * GROUP:(k + 1) * GROUP] for k in range(4))
    dgate = [dgates[:, k * GROUP:(k + 1) * GROUP] for k in range(4)]

    sb_dq, sb_dk, sb_dv = _sb_bwd(z, do_sb, name=f"sb_bwd_{tag}")

    n_dq, n_dbr, n_dkvc, n_dks, n_dvs, n_dkw, n_dvw = _nsa_bwd(
        do_nsa, sv["qr"], sv["kvc"], sv["ksr"], sv["vs"], sv["kwr"], sv["vw"], z, sv["oc"], sv["os"], sv["ow"],
        sv["stats"], sv["sel"], c["e3"], name=f"nsa_bwd_{tag}")
    dxa, dxb, dw1, dw2 = _nsa_cmp_bwd(sv["ra"], sv["rb"], p["cmp_pos"], p["cmp_w1"], p["cmp_w2"], c["tabs_cmp"],
                                      sv["hp"], n_dkvc, name=f"nsa_cmp_bwd_{tag}")
    n_dq, n_dks, n_dkw = _rope_call([(n_dq, GROUP, 0), (n_dks, LANES, 0), (n_dkw, LANES, 0)], c["tabs128"],
                                    HEAD_DIM // 2, True, name=f"nsa_rope_bwd_{tag}")
    dpos = _colsum(jnp.concatenate([dxa[0], dxb[0], dxa[1], dxb[1]], axis=1), name=f"nsa_dpos_{tag}")
    flat = CMP_LEN * HEAD_DIM

    fcols = tuple(_AL[n] // HEAD_DIM for n in ("fox_q", "fox_k", "fox_v"))
    f_dq, f_dk, f_dv, f_dck = _attn_bwd(z, z, z, *fcols, HEAD_DIM, do_fox, sv["o_fox"], sv["lse_fox"], sv["cum"],
                                        sv["cum_t"], scale=HEAD_DIM ** -0.5, name=f"fox_bwd_{tag}")
    dcum_t = jnp.pad(f_dck.reshape(N_HEADS, s), ((0, 8 - N_HEADS), (0, 0)))
    f_df, f_dbias = _fox_cum_bwd(z, p["fox_bias"], dcum_t, name=f"fox_cum_bwd_{tag}")

    m_dq, m_dk, m_dv = _attn_bwd(sv["qcat"], sv["kcat"], sv["vm"], 0, 0, 0, MLA_QW, do_mla, sv["o_mla"], sv["lse_mla"],
                                 None, None, scale=(MLA_NOPE + MLA_ROPE) ** -0.5, name=f"mla_bwd_{tag}")
    m_dcq, m_dckv, m_dkr, m_dwuq, m_dwk, m_dwv, m_dgq, m_dgkv = _mla_prep_bwd(
        z, p["gq"], p["gkv"], p["wuq"], p["wk"], p["wv"], c["tabs64"], m_dq, m_dk, m_dv, name=f"mla_prep_bwd_{tag}")

    dz = jnp.concatenate([
        sb_dq, sb_dk, sb_dv, dgate[0],
        n_dq, _cmp_unrows(dxa[0], dxb[0]), _cmp_unrows(dxa[1], dxb[1]), n_dks, n_dvs, n_dkw, n_dvw, n_dbr, dgate[1],
        f_dq, f_dk, f_dv, f_df, dgate[2],
        m_dcq, m_dckv, m_dkr, dgate[3]], axis=1)
    dh = _matmul(dz, p["w_in"], "nt", name=f"inproj_dx_{tag}")
    dw_in_al = _matmul(sv["h"], dz, "tn", name=f"inproj_dw_{tag}")
    db_al = _colsum(dz, name=f"inproj_db_{tag}")
    dx, dg_pre = _rms_bwd(dh, sv["x"], p["pre_g"], res=dy, name=f"prenorm_bwd_{tag}")

    unalign = lambda a: jnp.concatenate([a[:, _AL[n]:_AL[n] + w] for n, w, _ in _SEGS], axis=1)
    qw = MLA_NOPE + MLA_ROPE
    grads = {
        "pre_norm_g": dg_pre[0], "post_norm_g": dg_post[0], "w_in": unalign(dw_in_al), "b_in": unalign(db_al)[0],
        "w_out": dw_out, "fox_forget_bias": f_dbias[0, :N_HEADS],
        "nsa_cmp_pos_k": dpos[0, :flat].reshape(CMP_LEN, HEAD_DIM), "nsa_cmp_w1_k": dw1[0], "nsa_cmp_w2_k": dw2[0],
        "nsa_cmp_pos_v": dpos[0, flat:].reshape(CMP_LEN, HEAD_DIM), "nsa_cmp_w1_v": dw1[1], "nsa_cmp_w2_v": dw2[1],
        "mla_q_norm_g": m_dgq[0],
        "mla_w_uq": jnp.concatenate([m_dwuq[:, MLA_QW * h:MLA_QW * h + qw] for h in range(N_HEADS)], axis=1),
        "mla_kv_norm_g": m_dgkv[0],
        "mla_w_ukv": jnp.concatenate(sum([[m_dwk[:, LANES * h:LANES * (h + 1)], m_dwv[:, LANES * h:LANES * (h + 1)]]
                                          for h in range(N_HEADS)], []), axis=1),
    }
    return dx, grads


def _layer_params(w, l):
    w_in = w["w_in"][l]
    zero = lambda n: jnp.zeros((w_in.shape[0], n), w_in.dtype)
    cols = []
    for n, wd, wa in _SEGS:
        cols.append(w_in[:, _ORIG[n]:_ORIG[n] + wd])
        if wa > wd:
            cols.append(zero(wa - wd))
    b_in = w["b_in"][l]
    bcols = []
    for n, wd, wa in _SEGS:
        bcols.append(b_in[_ORIG[n]:_ORIG[n] + wd])
        if wa > wd:
            bcols.append(jnp.zeros((wa - wd,), f32))
    qw = MLA_NOPE + MLA_ROPE
    w_uq, w_ukv = w["mla_w_uq"][l], w["mla_w_ukv"][l]
    uq = []
    for h in range(N_HEADS):
        uq += [w_uq[:, qw * h:qw * (h + 1)], jnp.zeros((w_uq.shape[0], MLA_QW - qw), w_uq.dtype)]
    kw_ = 2 * LANES
    flat = CMP_LEN * HEAD_DIM
    return dict(
        pre_g=w["pre_norm_g"][l].reshape(1, -1), post_g=w["post_norm_g"][l].reshape(1, -1),
        w_in=jnp.concatenate(cols, axis=1), b_in=jnp.concatenate(bcols).reshape(1, -1), w_out=w["w_out"][l],
        fox_bias=jnp.pad(w["fox_forget_bias"][l], (0, LANES - N_HEADS)).reshape(1, LANES),
        cmp_pos=jnp.stack([w["nsa_cmp_pos_k"][l].reshape(1, flat), w["nsa_cmp_pos_v"][l].reshape(1, flat)]),
        cmp_w1=jnp.stack([w["nsa_cmp_w1_k"][l], w["nsa_cmp_w1_v"][l]]),
        cmp_w2=jnp.stack([w["nsa_cmp_w2_k"][l], w["nsa_cmp_w2_v"][l]]),
        gq=w["mla_q_norm_g"][l].reshape(1, -1), gkv=w["mla_kv_norm_g"][l].reshape(1, -1),
        wuq=jnp.concatenate(uq, axis=1),
        wk=jnp.concatenate([w_ukv[:, kw_ * h:kw_ * h + LANES] for h in range(N_HEADS)], axis=1),
        wv=jnp.concatenate([w_ukv[:, kw_ * h + LANES:kw_ * (h + 1)] for h in range(N_HEADS)], axis=1),
    )


def _consts(s):
    pos = jnp.arange(s)
    m2s, e3 = _nsa_consts(s)
    return dict(tabs128=_rope_tables(pos, HEAD_DIM), tabs64=_rope_tables(pos, MLA_ROPE),
                tabs_cmp=_rope_tables(jnp.arange(s // CMP_STRIDE) * CMP_STRIDE + (CMP_LEN - 1), HEAD_DIM),
                m2s=m2s, e3=e3)


def _place():
    return lax.axis_index("x"), lax.axis_index("y"), lax.axis_index("c")


def _other_chips(x, y):
    return [(1 - x, y), (x, 1 - y), (1 - x, 1 - y)]


_SEMS3 = [pltpu.SemaphoreType.DMA((3,)), pltpu.SemaphoreType.DMA((3,)), pltpu.SemaphoreType.DMA]


def _gather_chips(a, *, name):
    def body(a_ref, out_ref, send_sems, recv_sems, local_sem):
        x, y, c = _place()
        me = 2 * x + y
        mine = pltpu.make_async_copy(a_ref, out_ref.at[me], local_sem)
        mine.start()
        sends = [pltpu.make_async_remote_copy(a_ref, out_ref.at[me], send_sems.at[k], recv_sems.at[k],
                                              device_id=(px, py, c), device_id_type=_MESH)
                 for k, (px, py) in enumerate(_other_chips(x, y))]
        for cp in sends:
            cp.start()
        for k, (px, py) in enumerate(_other_chips(x, y)):
            pltpu.make_async_remote_copy(a_ref, out_ref.at[2 * px + py], send_sems.at[k], recv_sems.at[k],
                                         device_id=(px, py, c), device_id_type=_MESH).wait_recv()
        for cp in sends:
            cp.wait_send()
        mine.wait()

    return pl.pallas_call(body, out_shape=_SDS((4,) + a.shape, a.dtype), in_specs=[_ANY], out_specs=_ANY,
                          scratch_shapes=_SEMS3, name=name)(a)


def _alltoall_chips(g, *, name):
    def body(g_ref, out_ref, send_sems, recv_sems, local_sem):
        x, y, c = _place()
        me = 2 * x + y
        mine = pltpu.make_async_copy(g_ref.at[me], out_ref.at[me], local_sem)
        mine.start()
        sends = [pltpu.make_async_remote_copy(g_ref.at[2 * px + py], out_ref.at[me], send_sems.at[k], recv_sems.at[k],
                                              device_id=(px, py, c), device_id_type=_MESH)
                 for k, (px, py) in enumerate(_other_chips(x, y))]
        for cp in sends:
            cp.start()
        for k, (px, py) in enumerate(_other_chips(x, y)):
            pltpu.make_async_remote_copy(g_ref.at[me], out_ref.at[2 * px + py], send_sems.at[k], recv_sems.at[k],
                                         device_id=(px, py, c), device_id_type=_MESH).wait_recv()
        for cp in sends:
            cp.wait_send()
        mine.wait()

    return pl.pallas_call(body, out_shape=_SDS(g.shape, g.dtype), in_specs=[_ANY], out_specs=_ANY,
                          scratch_shapes=_SEMS3, name=name)(g)


def _swap_other_half(g, *, name):
    p, n2, w = g.shape
    h = n2 // 2

    def body(g_ref, out_ref, send_sem, recv_sem):
        x, y, c = _place()
        theirs = g_ref.at[:, pl.ds(pl.multiple_of((1 - c) * h, 8), h), :]
        cp = pltpu.make_async_remote_copy(theirs, out_ref, send_sem, recv_sem, device_id=(x, y, 1 - c),
                                          device_id_type=_MESH)
        cp.start()
        cp.wait()

    return pl.pallas_call(body, out_shape=_SDS((p, h, w), g.dtype), in_specs=[_ANY], out_specs=_ANY,
                          scratch_shapes=[pltpu.SemaphoreType.DMA, pltpu.SemaphoreType.DMA], name=name)(g)


def _pair_gather(f, *, name):
    h, w = f.shape

    def body(f_ref, out_ref, send_sem, recv_sem, local_sem):
        x, y, c = _place()
        my_rows = out_ref.at[pl.ds(pl.multiple_of(c * h, 8), h), :]
        their_rows = out_ref.at[pl.ds(pl.multiple_of((1 - c) * h, 8), h), :]
        mine = pltpu.make_async_copy(f_ref, my_rows, local_sem)
        mine.start()
        cp = pltpu.make_async_remote_copy(f_ref, my_rows, send_sem, recv_sem, device_id=(x, y, 1 - c),
                                          device_id_type=_MESH)
        cp.start()
        pltpu.make_async_remote_copy(f_ref, their_rows, send_sem, recv_sem, device_id=(x, y, 1 - c),
                                     device_id_type=_MESH).wait_recv()
        cp.wait_send()
        mine.wait()

    return pl.pallas_call(body, out_shape=_SDS((2 * h, w), f.dtype), in_specs=[_ANY], out_specs=_ANY,
                          scratch_shapes=[pltpu.SemaphoreType.DMA] * 3, name=name)(f)


def _gather_all(a, *, name):
    def body(a_ref, out_ref, send_sems, recv_sems, local_sem):
        x, y, c = _place()
        flip = lambda v, f: (1 - v) if f else v
        peers = [(flip(x, f & 4), flip(y, f & 2), flip(c, f & 1)) for f in range(1, 8)]
        me = 4 * x + 2 * y + c
        mine = pltpu.make_async_copy(a_ref, out_ref.at[me], local_sem)
        mine.start()
        sends = [pltpu.make_async_remote_copy(a_ref, out_ref.at[me], send_sems.at[k], recv_sems.at[k], device_id=peer,
                                              device_id_type=_MESH) for k, peer in enumerate(peers)]
        for cp in sends:
            cp.start()
        for k, (px, py, pc) in enumerate(peers):
            pltpu.make_async_remote_copy(a_ref, out_ref.at[4 * px + 2 * py + pc], send_sems.at[k], recv_sems.at[k],
                                         device_id=(px, py, pc), device_id_type=_MESH).wait_recv()
        for cp in sends:
            cp.wait_send()
        mine.wait()

    return pl.pallas_call(body, out_shape=_SDS((8,) + a.shape, a.dtype), in_specs=[_ANY], out_specs=_ANY,
                          scratch_shapes=[pltpu.SemaphoreType.DMA((7,)), pltpu.SemaphoreType.DMA((7,)),
                                          pltpu.SemaphoreType.DMA], name=name)(a)


def _add_my_half(g, r, *, name):
    p, n2, w = g.shape
    h = n2 // 2
    rb = _pick(h, (1024, 976, 512, 256, 128, 64, 32, 16, 8))
    nb = h // rb

    def body(c_ref, g_ref, r_ref, o_ref):
        o_ref[...] = g_ref[...] + r_ref[...]

    blk = pl.BlockSpec((1, rb, w), lambda s, i, c_ref: (s, i, 0))
    grid_spec = pltpu.PrefetchScalarGridSpec(
        num_scalar_prefetch=1, grid=(p, nb),
        in_specs=[pl.BlockSpec((1, rb, w), lambda s, i, c_ref: (s, i + c_ref[0] * nb, 0)), blk], out_specs=blk)
    c = lax.axis_index("c").astype(jnp.int32).reshape(1)
    return pl.pallas_call(body, out_shape=_SDS((p, h, w), f32), grid_spec=grid_spec,
                          compiler_params=_cp(("parallel", "parallel")), name=name)(c, g, r)


_WEIGHTS = ("pre_norm_g", "post_norm_g", "w_in", "b_in", "w_out", "fox_forget_bias", "nsa_cmp_pos_k", "nsa_cmp_w1_k",
            "nsa_cmp_w2_k", "nsa_cmp_pos_v", "nsa_cmp_w1_v", "nsa_cmp_w2_v", "mla_q_norm_g", "mla_w_uq",
            "mla_kv_norm_g", "mla_w_ukv")
_SHARD_AXIS = {"w_in": 2, "w_out": 1, "nsa_cmp_w1_k": 1, "nsa_cmp_w1_v": 1, "mla_w_uq": 2, "mla_w_ukv": 2}
_N_CHIPS = 4
_PACK_UNIT = 16 * LANES


def _pack(arrays, dtype):
    rows = []
    for a in arrays:
        v = a.astype(dtype).reshape(-1)
        pad = (-v.shape[0]) % _PACK_UNIT
        if pad:
            v = jnp.concatenate([v, jnp.zeros((pad,), dtype)])
        rows.append(v.reshape(-1, LANES))
    return jnp.concatenate(rows, axis=0)


def _unpack(flat, shapes):
    out, r = [], 0
    for shp in shapes:
        n = int(np.prod(shp))
        nr = -(-n // _PACK_UNIT) * (_PACK_UNIT // LANES)
        out.append(flat[r:r + nr].reshape(-1)[:n].reshape(shp))
        r += nr
    return out


def kernel(x, pre_norm_g, post_norm_g, w_in, b_in, w_out, fox_forget_bias, nsa_cmp_pos_k, nsa_cmp_w1_k, nsa_cmp_w2_k, nsa_cmp_pos_v, nsa_cmp_w1_v, nsa_cmp_w2_v, mla_q_norm_g, mla_w_uq, mla_kv_norm_g, mla_w_ukv, loss_target, m_pre_norm_g, m_post_norm_g, m_w_in, m_b_in, m_w_out, m_fox_forget_bias, m_nsa_cmp_pos_k, m_nsa_cmp_w1_k, m_nsa_cmp_w2_k, m_nsa_cmp_pos_v, m_nsa_cmp_w1_v, m_nsa_cmp_w2_v, m_mla_q_norm_g, m_mla_w_uq, m_mla_kv_norm_g, m_mla_w_ukv, v_pre_norm_g, v_post_norm_g, v_w_in, v_b_in, v_w_out, v_fox_forget_bias, v_nsa_cmp_pos_k, v_nsa_cmp_w1_k, v_nsa_cmp_w2_k, v_nsa_cmp_pos_v, v_nsa_cmp_w1_v, v_nsa_cmp_w2_v, v_mla_q_norm_g, v_mla_w_uq, v_mla_kv_norm_g, v_mla_w_ukv):
    given = dict(locals())
    local = {n: given[n] for n in _WEIGHTS}
    depth = pre_norm_g.shape[0]
    xs, target = x[0], loss_target[0]
    s = xs.shape[0]
    sharded = [n for n in _WEIGHTS if n in _SHARD_AXIS]
    small = [n for n in _WEIGHTS if n not in _SHARD_AXIS]

    shard_shapes = [local[n].shape for n in sharded]
    everyone = _gather_chips(_pack([local[n] for n in sharded], _MXU), name="gather_weights")
    per_chip = [_unpack(everyone[k], shard_shapes) for k in range(_N_CHIPS)]
    full = dict(local)
    for j, n in enumerate(sharded):
        full[n] = jnp.concatenate([per_chip[k][j] for k in range(_N_CHIPS)], axis=_SHARD_AXIS[n])

    consts = _consts(s)
    params = [_layer_params(full, l) for l in range(depth)]
    act, saved = xs, []
    for l in range(depth):
        act, sv = _layer_fwd(act, params[l], consts, f"l{l}")
        saved.append(sv)
    dy, loss_parts = _loss_head(act, target, name="loss_head")
    layer_grads = [None] * depth
    for l in reversed(range(depth)):
        dy, layer_grads[l] = _layer_bwd(dy, saved[l], params[l], consts, f"l{l}")
    grad_x = dy[None]
    grads = {n: jnp.stack([layer_grads[l][n] for l in range(depth)]) for n in _WEIGHTS}

    def chip_slice(n, k):
        a, ax = grads[n], _SHARD_AXIS[n]
        w = a.shape[ax] // _N_CHIPS
        return lax.slice_in_dim(a, k * w, (k + 1) * w, axis=ax)

    g_all = jnp.stack([_pack([chip_slice(n, k) for n in sharded], f32) for k in range(_N_CHIPS)])
    from_sibling = _swap_other_half(g_all, name="reduce_pair")
    pair_sum = _add_my_half(g_all, from_sibling, name="reduce_pair_add")
    from_chips = _alltoall_chips(pair_sum, name="reduce_chips")
    my_half = _sum_slots(from_chips, name="reduce_chips_add")
    g_shard = _unpack(_pair_gather(my_half, name="reduce_share"), shard_shapes)
    summed = dict(zip(sharded, g_shard))

    loss_row = jnp.concatenate([jnp.sum(loss_parts).reshape(1), jnp.zeros((LANES - 1,), f32)])
    small_shapes = [(LANES,)] + [grads[n].shape for n in small]
    contrib = _pack([loss_row] + [grads[n] for n in small], f32)
    pad_rows = (-contrib.shape[0]) % 8
    if pad_rows:
        contrib = jnp.concatenate([contrib, jnp.zeros((pad_rows, LANES), f32)], axis=0)
    total = _unpack(_sum_slots(_gather_all(contrib, name="gather_small"), name="sum_small"), small_shapes)
    loss = total[0][0]
    summed.update(zip(small, total[1:]))

    deltas, new_m, new_v = {}, {}, {}
    for n in _WEIGHTS:
        deltas[n], new_m[n], new_v[n] = _adamw(local[n], summed[n], given["m_" + n], given["v_" + n], name=f"adamw_{n}")
    return (loss, grad_x, *[summed[n] for n in _WEIGHTS], *[deltas[n] for n in _WEIGHTS],
            *[new_m[n] for n in _WEIGHTS], *[new_v[n] for n in _WEIGHTS])
```

```python
import functools
import math

import numpy as np
import jax
import jax.numpy as jnp
from jax import lax
from jax.experimental import pallas as pl
from jax.experimental.pallas import tpu as pltpu

f32 = jnp.float32
bf16 = jnp.bfloat16
_MXU = jnp.bfloat16
_WIRE = jnp.bfloat16
_SDS = jax.ShapeDtypeStruct
_ANY = pl.BlockSpec(memory_space=pl.ANY)
_MESH = pl.DeviceIdType.MESH

D_MODEL = 2048
N_HEADS = 4
HEAD_DIM = 128
GROUP = 512
RMS_EPS = 1e-6
NEG_INF = -1e30
ROPE_THETA = 10000.0
CMP_LEN, CMP_STRIDE, SEL_LEN, SEL_TOPN, WINDOW = 32, 16, 64, 16, 512
FORCED_BONUS = 1e6
MLA_Q_RANK, MLA_KV_RANK, MLA_NOPE, MLA_ROPE = 384, 128, 128, 64
ADAM_LR, ADAM_B1, ADAM_B2, ADAM_EPS, ADAM_WD, ADAM_STEP = 0.001, 0.9, 0.999, 1e-08, 0.01, 10
LANES = 128
VMEM_LIMIT = 48 * 1024 * 1024

_SEGS = (
    ("sb_q", 512, 512), ("sb_k", 512, 512), ("sb_v", 512, 512), ("sb_gate", 512, 512),
    ("nsa_q", 512, 512), ("nsa_k_cmp", 128, 128), ("nsa_v_cmp", 128, 128), ("nsa_k_sel", 128, 128),
    ("nsa_v_sel", 128, 128), ("nsa_k_win", 128, 128), ("nsa_v_win", 128, 128), ("nsa_branch", 12, 128),
    ("nsa_gate", 512, 512), ("fox_q", 512, 512), ("fox_k", 512, 512), ("fox_v", 512, 512), ("fox_f", 4, 128),
    ("fox_gate", 512, 512), ("mla_cq", 384, 384), ("mla_ckv", 128, 128), ("mla_k_rope", 64, 128),
    ("mla_gate", 512, 512),
)
_ORIG, _AL, _WID = {}, {}, {}
_o = _a = 0
for _n, _w, _wa in _SEGS:
    _ORIG[_n], _AL[_n], _WID[_n] = _o, _a, _w
    _o += _w
    _a += _wa
IN_WIDTH, ZW = _o, _a


def _cp(sem=None):
    return pltpu.CompilerParams(dimension_semantics=sem, vmem_limit_bytes=VMEM_LIMIT)


def _mm(a, b):
    return jnp.dot(a.astype(_MXU), b.astype(_MXU), preferred_element_type=f32)


def _mm_nt(a, b):
    return lax.dot_general(a.astype(_MXU), b.astype(_MXU), (((1,), (1,)), ((), ())), preferred_element_type=f32)


def _mm_tn(a, b):
    return lax.dot_general(a.astype(_MXU), b.astype(_MXU), (((0,), (0,)), ((), ())), preferred_element_type=f32)


def _mm_split(x, t):
    hi = x.astype(_MXU)
    lo = (x - hi.astype(f32)).astype(_MXU)
    return jnp.dot(hi, t, preferred_element_type=f32) + jnp.dot(lo, t, preferred_element_type=f32)


def _sigmoid(x):
    return 1.0 / (1.0 + jnp.exp(-x))


def _iota(shape, dim):
    return lax.broadcasted_iota(jnp.int32, shape, dim)


def _pick(n, prefs):
    for p in prefs:
        if n % p == 0:
            return p
    return n


def _matmul(a, b, mode, *, bias=None, out_dtype=f32, name):
    if mode == "nn":
        (M, K), (K2, N) = a.shape, b.shape
    elif mode == "nt":
        (M, K), (N, K2) = a.shape, b.shape
    else:
        (K, M), (K2, N) = a.shape, b.shape
    assert K == K2
    tm = _pick(M, (512, 384, 256, 128))
    tn = _pick(N, (512, 384, 256, 128))
    tk = K if K <= 2048 else _pick(K, (2048, 2432, 1024, 512))
    nk = K // tk
    a_spec = {"nn": pl.BlockSpec((tm, tk), lambda i, j, k: (i, k)),
              "nt": pl.BlockSpec((tm, tk), lambda i, j, k: (i, k)),
              "tn": pl.BlockSpec((tk, tm), lambda i, j, k: (k, i))}[mode]
    b_spec = {"nn": pl.BlockSpec((tk, tn), lambda i, j, k: (k, j)),
              "nt": pl.BlockSpec((tn, tk), lambda i, j, k: (j, k)),
              "tn": pl.BlockSpec((tk, tn), lambda i, j, k: (k, j))}[mode]
    dot = {"nn": _mm, "nt": _mm_nt, "tn": _mm_tn}[mode]
    has_bias = bias is not None

    def body(*refs):
        if has_bias:
            a_ref, b_ref, bias_ref, o_ref, acc_ref = refs
        else:
            a_ref, b_ref, o_ref, acc_ref = refs
            bias_ref = None
        k = pl.program_id(2)
        part = dot(a_ref[...], b_ref[...])

        def finish(total):
            if has_bias:
                total = total + bias_ref[...]
            o_ref[...] = total.astype(o_ref.dtype)

        if nk == 1:
            finish(part)
        else:
            @pl.when(k == 0)
            def _():
                acc_ref[...] = part

            @pl.when(k > 0)
            def _():
                acc_ref[...] += part

            @pl.when(k == nk - 1)
            def _():
                finish(acc_ref[...])

    in_specs = [a_spec, b_spec]
    args = [a, b]
    if has_bias:
        in_specs.append(pl.BlockSpec((1, tn), lambda i, j, k: (0, j)))
        args.append(bias.reshape(1, N))
    return pl.pallas_call(
        body, out_shape=_SDS((M, N), out_dtype), grid=(M // tm, N // tn, nk),
        in_specs=in_specs, out_specs=pl.BlockSpec((tm, tn), lambda i, j, k: (i, j)),
        scratch_shapes=[pltpu.VMEM((tm, tn), f32)],
        compiler_params=_cp(("parallel", "parallel", "arbitrary")), name=name,
    )(*args)


def _row_block(s):
    return _pick(s, (256, 128))


def _rms_fwd(x, g, *, out_dtype, name):
    s, d = x.shape
    rb = _row_block(s)

    def body(x_ref, g_ref, o_ref):
        xv = x_ref[...]
        r = lax.rsqrt(jnp.mean(xv * xv, axis=-1, keepdims=True) + RMS_EPS)
        o_ref[...] = (xv * r * g_ref[...]).astype(o_ref.dtype)

    return pl.pallas_call(
        body, out_shape=_SDS((s, d), out_dtype), grid=(s // rb,),
        in_specs=[pl.BlockSpec((rb, d), lambda i: (i, 0)), pl.BlockSpec((1, d), lambda i: (0, 0))],
        out_specs=pl.BlockSpec((rb, d), lambda i: (i, 0)), compiler_params=_cp(("parallel",)), name=name,
    )(x, g.reshape(1, d))


def _postnorm_fwd(u, g, x, *, name):
    s, d = u.shape
    rb = _row_block(s)

    def body(u_ref, g_ref, x_ref, o_ref):
        uv = u_ref[...]
        r = lax.rsqrt(jnp.mean(uv * uv, axis=-1, keepdims=True) + RMS_EPS)
        o_ref[...] = x_ref[...] + uv * r * g_ref[...]

    return pl.pallas_call(
        body, out_shape=_SDS((s, d), f32), grid=(s // rb,),
        in_specs=[pl.BlockSpec((rb, d), lambda i: (i, 0)), pl.BlockSpec((1, d), lambda i: (0, 0)),
                  pl.BlockSpec((rb, d), lambda i: (i, 0))],
        out_specs=pl.BlockSpec((rb, d), lambda i: (i, 0)), compiler_params=_cp(("parallel",)), name=name,
    )(u, g.reshape(1, d), x)


def _fold_rows(v):
    r = v.shape[0]
    acc = v[0:8]
    for k in range(1, r // 8):
        acc = acc + v[8 * k:8 * k + 8]
    return acc


def _rms_bwd(dy, x, g, res=None, *, name):
    s, d = x.shape
    rb = _row_block(s)
    nb = s // rb
    has_res = res is not None

    def body(*refs):
        if has_res:
            dy_ref, x_ref, g_ref, res_ref, dx_ref, dg_ref, acc_ref = refs
        else:
            dy_ref, x_ref, g_ref, dx_ref, dg_ref, acc_ref = refs
        i = pl.program_id(0)
        xv = x_ref[...]
        r = lax.rsqrt(jnp.mean(xv * xv, axis=-1, keepdims=True) + RMS_EPS)
        xh = xv * r
        dyv = dy_ref[...]
        dxh = dyv * g_ref[...]
        dx = r * (dxh - xh * jnp.mean(dxh * xh, axis=-1, keepdims=True))
        if has_res:
            dx = dx + res_ref[...]
        dx_ref[...] = dx
        part = _fold_rows(dyv * xh)

        @pl.when(i == 0)
        def _():
            acc_ref[...] = part

        @pl.when(i > 0)
        def _():
            acc_ref[...] += part

        @pl.when(i == nb - 1)
        def _():
            dg_ref[...] = jnp.sum(acc_ref[...], axis=0, keepdims=True)

    blk = pl.BlockSpec((rb, d), lambda i: (i, 0))
    in_specs = [blk, blk, pl.BlockSpec((1, d), lambda i: (0, 0))] + ([blk] if has_res else [])
    args = [dy, x, g.reshape(1, d)] + ([res] if has_res else [])
    return pl.pallas_call(
        body, out_shape=(_SDS((s, d), f32), _SDS((1, d), f32)), grid=(nb,), in_specs=in_specs,
        out_specs=(blk, pl.BlockSpec((1, d), lambda i: (0, 0))),
        scratch_shapes=[pltpu.VMEM((8, d), f32)], compiler_params=_cp(("arbitrary",)), name=name,
    )(*args)


def _loss_head(y, target, *, name):
    s, d = y.shape
    rb = _row_block(s)
    nb = s // rb

    def body(y_ref, t_ref, dy_ref, l_ref):
        i = pl.program_id(0)
        e = y_ref[...] - t_ref[...]
        dy_ref[...] = e * (1.0 / d)
        rows = _fold_rows(e * e)
        part = rows[:, 0:LANES]
        for k in range(1, d // LANES):
            part = part + rows[:, k * LANES:(k + 1) * LANES]
        part = part * (0.5 / d)

        @pl.when(i == 0)
        def _():
            l_ref[...] = part

        @pl.when(i > 0)
        def _():
            l_ref[...] += part

    blk = pl.BlockSpec((rb, d), lambda i: (i, 0))
    return pl.pallas_call(
        body, out_shape=(_SDS((s, d), f32), _SDS((8, LANES), f32)), grid=(nb,), in_specs=[blk, blk],
        out_specs=(blk, pl.BlockSpec((8, LANES), lambda i: (0, 0))),
        compiler_params=_cp(("arbitrary",)), name=name,
    )(y, target)


def _colsum(a, *, name):
    s, n = a.shape
    rb = _row_block(s)
    nb = s // rb
    tn = _pick(n, (2432, 2048, 1024, 512, 384, 128))

    def body(a_ref, o_ref, acc_ref):
        i = pl.program_id(1)
        part = _fold_rows(a_ref[...])

        @pl.when(i == 0)
        def _():
            acc_ref[...] = part

        @pl.when(i > 0)
        def _():
            acc_ref[...] += part

        @pl.when(i == nb - 1)
        def _():
            o_ref[...] = jnp.sum(acc_ref[...], axis=0, keepdims=True)

    return pl.pallas_call(
        body, out_shape=_SDS((1, n), f32), grid=(n // tn, nb),
        in_specs=[pl.BlockSpec((rb, tn), lambda j, i: (i, j))], out_specs=pl.BlockSpec((1, tn), lambda j, i: (0, j)),
        scratch_shapes=[pltpu.VMEM((8, tn), f32)], compiler_params=_cp(("parallel", "arbitrary")), name=name,
    )(a)


def _gate_fwd(o, gate, *, name):
    s, d = o.shape
    rb = _row_block(s)

    def body(o_ref, g_ref, m_ref):
        gv = g_ref[...]
        m_ref[...] = (o_ref[...] * (gv * _sigmoid(gv))).astype(m_ref.dtype)

    blk = pl.BlockSpec((rb, d), lambda i: (i, 0))
    return pl.pallas_call(body, out_shape=_SDS((s, d), _MXU), grid=(s // rb,), in_specs=[blk, blk], out_specs=blk,
                          compiler_params=_cp(("parallel",)), name=name)(o, gate)


def _gate_bwd(dmix, o, gate, *, name):
    s, d = o.shape
    rb = _row_block(s)

    def body(dm_ref, o_ref, g_ref, do_ref, dg_ref):
        gv = g_ref[...]
        sg = _sigmoid(gv)
        dm = dm_ref[...]
        do_ref[...] = dm * (gv * sg)
        dg_ref[...] = dm * o_ref[...] * (sg * (1.0 + gv * (1.0 - sg)))

    blk = pl.BlockSpec((rb, d), lambda i: (i, 0))
    return pl.pallas_call(body, out_shape=(_SDS((s, d), f32), _SDS((s, d), f32)), grid=(s // rb,),
                          in_specs=[blk, blk, blk], out_specs=(blk, blk), compiler_params=_cp(("parallel",)),
                          name=name)(dmix, o, gate)


def _adamw(w, g, m, v, *, name):
    shape = w.shape
    cols = shape[-1]
    rows = int(np.prod(shape[:-1])) if len(shape) > 1 else 1
    to2 = lambda t: t.reshape(rows, cols)
    rb = _pick(rows, (128, 64, 32, 16, 8)) if rows * cols * 4 > (1 << 20) else rows

    def body(w_ref, g_ref, m_ref, v_ref, d_ref, nm_ref, nv_ref):
        gv = g_ref[...]
        mn = ADAM_B1 * m_ref[...] + (1.0 - ADAM_B1) * gv
        vn = ADAM_B2 * v_ref[...] + (1.0 - ADAM_B2) * (gv * gv)
        m_hat = mn / (1.0 - ADAM_B1 ** ADAM_STEP)
        v_hat = vn / (1.0 - ADAM_B2 ** ADAM_STEP)
        d_ref[...] = -ADAM_LR * (m_hat / (jnp.sqrt(v_hat) + ADAM_EPS) + ADAM_WD * w_ref[...])
        nm_ref[...] = mn
        nv_ref[...] = vn

    blk = pl.BlockSpec((rb, cols), lambda i: (i, 0))
    out = pl.pallas_call(body, out_shape=tuple(_SDS((rows, cols), f32) for _ in range(3)), grid=(rows // rb,),
                         in_specs=[blk] * 4, out_specs=(blk,) * 3, compiler_params=_cp(("parallel",)),
                         name=name)(to2(w), to2(g), to2(m), to2(v))
    return tuple(t.reshape(shape) for t in out)


def _sum_slots(a, *, name):
    p, n, c = a.shape
    rb = n if p * n * c * 4 <= (4 << 20) else _pick(n, (1024, 976, 512, 256, 128, 64, 32, 16, 8))

    def body(a_ref, o_ref):
        acc = a_ref[0].astype(f32)
        for k in range(1, p):
            acc = acc + a_ref[k].astype(f32)
        o_ref[...] = acc

    return pl.pallas_call(body, out_shape=_SDS((n, c), f32), grid=(n // rb,),
                          in_specs=[pl.BlockSpec((p, rb, c), lambda i: (0, i, 0))],
                          out_specs=pl.BlockSpec((rb, c), lambda i: (i, 0)), compiler_params=_cp(("parallel",)),
                          name=name)(a)


def _add2(a, b, *, name):
    p, n, c = a.shape
    rb = _pick(n, (1024, 976, 512, 256, 128, 64, 32, 16, 8))

    def body(a_ref, b_ref, o_ref):
        o_ref[...] = a_ref[...] + b_ref[...]

    blk = pl.BlockSpec((1, rb, c), lambda s, i: (s, i, 0))
    return pl.pallas_call(body, out_shape=_SDS((p, n, c), f32), grid=(p, n // rb), in_specs=[blk, blk], out_specs=blk,
                          compiler_params=_cp(("parallel", "parallel")), name=name)(a, b)


def _rope_tables(pos, dim):
    half = dim // 2
    inv = ROPE_THETA ** (-jnp.arange(half, dtype=f32) / half)
    ang = pos.astype(f32)[:, None] * inv[None, :]
    c, s = jnp.cos(ang), jnp.sin(ang)
    z = jnp.zeros_like(c)
    pad = [jnp.zeros((pos.shape[0], LANES - dim), f32)] if dim < LANES else []
    return (jnp.concatenate([c, c] + pad, axis=1), jnp.concatenate([-s, z] + pad, axis=1),
            jnp.concatenate([z, s] + pad, axis=1))


def _rope(x, cos, sa, sb, half, transpose=False):
    if transpose:
        return x * cos + pltpu.roll(x * sa, half, 1) + pltpu.roll(x * sb, LANES - half, 1)
    return x * cos + pltpu.roll(x, LANES - half, 1) * sa + pltpu.roll(x, half, 1) * sb


def _rope_call(items, tables, half, transpose, *, name):
    s = items[0][0].shape[0]
    rb = _row_block(s)
    n = len(items)

    def body(*refs):
        cos, sa, sb = refs[n][...], refs[n + 1][...], refs[n + 2][...]
        for k in range(n):
            x_ref, o_ref = refs[k], refs[n + 3 + k]
            for j in range(items[k][1] // LANES):
                sl = slice(j * LANES, (j + 1) * LANES)
                o_ref[:, sl] = _rope(x_ref[:, sl], cos, sa, sb, half, transpose)

    in_specs = [pl.BlockSpec((rb, w), functools.partial(lambda i, cb: (i, cb), cb=cb)) for _, w, cb in items]
    in_specs += [pl.BlockSpec((rb, LANES), lambda i: (i, 0))] * 3
    out_specs = tuple(pl.BlockSpec((rb, w), lambda i: (i, 0)) for _, w, _ in items)
    return pl.pallas_call(
        body, out_shape=tuple(_SDS((s, w), f32) for _, w, _ in items), grid=(s // rb,), in_specs=in_specs,
        out_specs=out_specs, compiler_params=_cp(("parallel",)), name=name,
    )(*[a for a, _, _ in items], *tables)


def _attn_block(s):
    return _pick(s, (256, 128))


def _lower_mask(b, strict):
    r, c = _iota((b, b), 0), _iota((b, b), 1)
    return (c < r) if strict else (c <= r)


def _pick_lane(block, h):
    return jnp.sum(jnp.where(_iota(block.shape, 1) == h, block, 0.0), axis=1, keepdims=True)


def _attn_fwd(q, k, v, qcol, kcol, vcol, dq, cum, cum_t, *, scale, name):
    s = q.shape[0]
    b = _attn_block(s)
    nq = s // b
    has_bias = cum is not None

    def body(*refs):
        if has_bias:
            q_ref, k_ref, v_ref, cum_ref, cumt_ref, o_ref, lse_ref = refs
        else:
            q_ref, k_ref, v_ref, o_ref, lse_ref = refs
        h, i = pl.program_id(0), pl.program_id(1)
        qv = q_ref[...].astype(_MXU)
        cq = _pick_lane(cum_ref[...], h) if has_bias else None

        def chunk(c, carry, diag):
            m, l, acc = carry
            st = pl.multiple_of(c * b, b)
            z = _mm_nt(qv, k_ref[pl.ds(st, b), :]) * scale
            if has_bias:
                z = z + cq - cumt_ref[c]
            if diag:
                mask = _lower_mask(b, False)
                z = jnp.where(mask, z, NEG_INF)
            m_new = jnp.maximum(m, jnp.max(z, axis=1, keepdims=True))
            p = jnp.exp(z - m_new)
            if diag:
                p = jnp.where(mask, p, 0.0)
            alpha = jnp.exp(m - m_new)
            l = alpha * l + jnp.sum(p, axis=1, keepdims=True)
            acc = alpha * acc + _mm(p, v_ref[pl.ds(st, b), :])
            return m_new, l, acc

        init = (jnp.full((b, 1), NEG_INF, f32), jnp.zeros((b, 1), f32), jnp.zeros((b, HEAD_DIM), f32))
        carry = lax.fori_loop(0, i, lambda c, cr: chunk(c, cr, False), init)
        m, l, acc = chunk(i, carry, True)
        o_ref[...] = acc / l
        lse_ref[...] = m + jnp.log(l)

    in_specs = [pl.BlockSpec((b, dq), lambda h, i: (i, qcol + h)), pl.BlockSpec((s, dq), lambda h, i: (0, kcol + h)),
                pl.BlockSpec((s, HEAD_DIM), lambda h, i: (0, vcol + h))]
    args = [q, k, v]
    if has_bias:
        in_specs += [pl.BlockSpec((b, LANES), lambda h, i: (i, 0)),
                     pl.BlockSpec((None, nq, 1, b), lambda h, i: (h, 0, 0, 0))]
        args += [cum, cum_t]
    return pl.pallas_call(
        body, out_shape=(_SDS((s, N_HEADS * HEAD_DIM), f32), _SDS((N_HEADS, s, 1), f32)), grid=(N_HEADS, nq),
        in_specs=in_specs,
        out_specs=(pl.BlockSpec((b, HEAD_DIM), lambda h, i: (i, h)), pl.BlockSpec((None, b, 1), lambda h, i: (h, i, 0))),
        compiler_params=_cp(("parallel", "parallel")), name=name,
    )(*args)


def _attn_bwd(q, k, v, qcol, kcol, vcol, dq, do, o, lse, cum, cum_t, *, scale, name):
    s = q.shape[0]
    b = _attn_block(s)
    nq = s // b
    has_bias = cum is not None

    def body(*refs):
        if has_bias:
            (q_ref, k_ref, v_ref, do_ref, o_ref, lse_ref, cum_ref, cumt_ref, dq_ref, dk_ref, dv_ref, dck_ref,
             p_sc, dp_sc) = refs
        else:
            q_ref, k_ref, v_ref, do_ref, o_ref, lse_ref, dq_ref, dk_ref, dv_ref = refs
        h, i = pl.program_id(0), pl.program_id(1)

        @pl.when(i == 0)
        def _():
            dk_ref[...] = jnp.zeros_like(dk_ref)
            dv_ref[...] = jnp.zeros_like(dv_ref)
            if has_bias:
                dck_ref[...] = jnp.zeros_like(dck_ref)

        qv = q_ref[...].astype(_MXU)
        dov = do_ref[...]
        dob = dov.astype(_MXU)
        lse_v = lse_ref[...]
        cq = _pick_lane(cum_ref[...], h) if has_bias else None

        def probs(c, diag):
            st = pl.multiple_of(c * b, b)
            z = _mm_nt(qv, k_ref[pl.ds(st, b), :]) * scale
            if has_bias:
                z = z + cq - cumt_ref[c]
            p = jnp.exp(z - lse_v)
            if diag:
                p = jnp.where(_lower_mask(b, False), p, 0.0)
            return p, _mm_nt(dob, v_ref[pl.ds(st, b), :])

        if has_bias:
            def first(c, acc, diag):
                p, dp = probs(c, diag)
                p_sc[c] = p
                dp_sc[c] = dp
                return acc + jnp.sum(p * dp, axis=1, keepdims=True)

            delta = lax.fori_loop(0, i, lambda c, a: first(c, a, False), jnp.zeros((b, 1), f32))
            delta = first(i, delta, True)
        else:
            delta = jnp.sum(dov * o_ref[...], axis=1, keepdims=True)

        def chunk(c, dq_acc, diag):
            st = pl.multiple_of(c * b, b)
            kc = k_ref[pl.ds(st, b), :]
            p, dp = (p_sc[c], dp_sc[c]) if has_bias else probs(c, diag)
            ds = p * (dp - delta)
            dk_ref[pl.ds(st, b), :] += _mm_tn(ds, qv) * scale
            dv_ref[pl.ds(st, b), :] += _mm_tn(p, dob)
            if has_bias:
                dck_ref[c] += -jnp.sum(ds, axis=0, keepdims=True)
            return dq_acc + _mm(ds, kc)

        acc = lax.fori_loop(0, i, lambda c, a: chunk(c, a, False), jnp.zeros((b, dq), f32))
        acc = chunk(i, acc, True)
        dq_ref[...] = acc * scale

    in_specs = [pl.BlockSpec((b, dq), lambda h, i: (i, qcol + h)), pl.BlockSpec((s, dq), lambda h, i: (0, kcol + h)),
                pl.BlockSpec((s, HEAD_DIM), lambda h, i: (0, vcol + h)),
                pl.BlockSpec((b, HEAD_DIM), lambda h, i: (i, h)), pl.BlockSpec((b, HEAD_DIM), lambda h, i: (i, h)),
                pl.BlockSpec((None, b, 1), lambda h, i: (h, i, 0))]
    args = [q, k, v, do, o, lse]
    out_shape = [_SDS((s, N_HEADS * dq), f32), _SDS((s, N_HEADS * dq), f32), _SDS((s, N_HEADS * HEAD_DIM), f32)]
    out_specs = [pl.BlockSpec((b, dq), lambda h, i: (i, h)), pl.BlockSpec((s, dq), lambda h, i: (0, h)),
                 pl.BlockSpec((s, HEAD_DIM), lambda h, i: (0, h))]
    if has_bias:
        in_specs += [pl.BlockSpec((b, LANES), lambda h, i: (i, 0)),
                     pl.BlockSpec((None, nq, 1, b), lambda h, i: (h, 0, 0, 0))]
        args += [cum, cum_t]
        out_shape.append(_SDS((N_HEADS, nq, 1, b), f32))
        out_specs.append(pl.BlockSpec((None, nq, 1, b), lambda h, i: (h, 0, 0, 0)))
    return pl.pallas_call(
        body, out_shape=tuple(out_shape), grid=(N_HEADS, nq), in_specs=in_specs, out_specs=tuple(out_specs),
        scratch_shapes=[pltpu.VMEM((nq, b, b), f32)] * 2 if has_bias else [],
        compiler_params=_cp(("parallel", "arbitrary")), name=name,
    )(*args)


def _tri(b, kind):
    r, c = _iota((b, b), 0), _iota((b, b), 1)
    cond = {"row_gt": r > c, "row_lt": r < c, "row_ge": r >= c, "row_le": r <= c}[kind]
    return jnp.where(cond, 1.0, 0.0).astype(_MXU)


def _log_keep(z):
    return -(jnp.maximum(z, 0.0) + jnp.log1p(jnp.exp(-jnp.abs(z))))


def _sb_fwd(z_all, *, name):
    s = z_all.shape[0]
    b = _attn_block(s)
    nq = s // b
    scale = HEAD_DIM ** -0.5
    qcol, kcol, vcol = (_AL[n] // HEAD_DIM for n in ("sb_q", "sb_k", "sb_v"))

    def body(q_ref, k_ref, v_ref, o_ref):
        i = pl.program_id(1)
        qv = q_ref[...].astype(_MXU)
        upper = _tri(b, "row_gt")

        def chunk(c, carry, diag):
            rsum, acc = carry
            st = pl.multiple_of(c * b, b)
            z = _mm_nt(qv, k_ref[pl.ds(st, b), :]) * scale
            lk = _log_keep(z)
            if diag:
                mask = _lower_mask(b, True)
                lk = jnp.where(mask, lk, 0.0)
            a = z + lk + _mm_split(lk, upper) + rsum
            if diag:
                a = jnp.where(mask, a, NEG_INF)
            acc = acc + _mm(jnp.exp(a), v_ref[pl.ds(st, b), :])
            return rsum + jnp.sum(lk, axis=1, keepdims=True), acc

        carry = chunk(i, (jnp.zeros((b, 1), f32), jnp.zeros((b, HEAD_DIM), f32)), True)
        _, acc = lax.fori_loop(0, i, lambda j, cr: chunk(i - 1 - j, cr, False), carry)
        o_ref[...] = acc

    return pl.pallas_call(
        body, out_shape=_SDS((s, GROUP), f32), grid=(N_HEADS, nq),
        in_specs=[pl.BlockSpec((b, HEAD_DIM), lambda h, i: (i, qcol + h)),
                  pl.BlockSpec((s, HEAD_DIM), lambda h, i: (0, kcol + h)),
                  pl.BlockSpec((s, HEAD_DIM), lambda h, i: (0, vcol + h))],
        out_specs=pl.BlockSpec((b, HEAD_DIM), lambda h, i: (i, h)),
        compiler_params=_cp(("parallel", "parallel")), name=name,
    )(z_all, z_all, z_all)


def _sb_bwd(z_all, do, *, name):
    s = z_all.shape[0]
    b = _attn_block(s)
    nq = s // b
    scale = HEAD_DIM ** -0.5
    qcol, kcol, vcol = (_AL[n] // HEAD_DIM for n in ("sb_q", "sb_k", "sb_v"))

    def body(q_ref, k_ref, v_ref, do_ref, dq_ref, dk_ref, dv_ref, z_sc, lk_sc, r_sc):
        i = pl.program_id(1)

        @pl.when(i == 0)
        def _():
            dk_ref[...] = jnp.zeros_like(dk_ref)
            dv_ref[...] = jnp.zeros_like(dv_ref)

        qv = q_ref[...].astype(_MXU)
        dob = do_ref[...].astype(_MXU)
        upper = _tri(b, "row_gt")
        lower = _tri(b, "row_lt")

        def scores(c, rsum, diag):
            st = pl.multiple_of(c * b, b)
            z = _mm_nt(qv, k_ref[pl.ds(st, b), :]) * scale
            lk = _log_keep(z)
            if diag:
                lk = jnp.where(_lower_mask(b, True), lk, 0.0)
            z_sc[c] = z
            lk_sc[c] = lk
            r_sc[c] = _mm_split(lk, upper) + rsum
            return rsum + jnp.sum(lk, axis=1, keepdims=True)

        rsum = scores(i, jnp.zeros((b, 1), f32), True)
        lax.fori_loop(0, i, lambda j, r: scores(i - 1 - j, r, False), rsum)

        def grads(c, carry, diag):
            psum, dq_acc = carry
            st = pl.multiple_of(c * b, b)
            z, lk = z_sc[c], lk_sc[c]
            lb = z + lk
            a = lb + r_sc[c]
            if diag:
                mask = _lower_mask(b, True)
                a = jnp.where(mask, a, NEG_INF)
            w = jnp.exp(a)
            e = _mm_nt(dob, v_ref[pl.ds(st, b), :]) * w
            before = _mm_split(e, lower) + psum
            dz = e * jnp.exp(lk) - before * jnp.exp(lb)
            if diag:
                dz = jnp.where(mask, dz, 0.0)
            kc = k_ref[pl.ds(st, b), :]
            dk_ref[pl.ds(st, b), :] += _mm_tn(dz, qv) * scale
            dv_ref[pl.ds(st, b), :] += _mm_tn(w, dob)
            return psum + jnp.sum(e, axis=1, keepdims=True), dq_acc + _mm(dz, kc)

        carry = lax.fori_loop(0, i, lambda c, cr: grads(c, cr, False),
                              (jnp.zeros((b, 1), f32), jnp.zeros((b, HEAD_DIM), f32)))
        _, dq_acc = grads(i, carry, True)
        dq_ref[...] = dq_acc * scale

    blk = pl.BlockSpec((b, HEAD_DIM), lambda h, i: (i, h))
    full = pl.BlockSpec((s, HEAD_DIM), lambda h, i: (0, h))
    return pl.pallas_call(
        body, out_shape=tuple(_SDS((s, GROUP), f32) for _ in range(3)), grid=(N_HEADS, nq),
        in_specs=[pl.BlockSpec((b, HEAD_DIM), lambda h, i: (i, qcol + h)),
                  pl.BlockSpec((s, HEAD_DIM), lambda h, i: (0, kcol + h)),
                  pl.BlockSpec((s, HEAD_DIM), lambda h, i: (0, vcol + h)), blk],
        out_specs=(blk, full, full),
        scratch_shapes=[pltpu.VMEM((nq, b, b), f32)] * 3,
        compiler_params=_cp(("parallel", "arbitrary")), name=name,
    )(z_all, z_all, z_all, do)


def _split3_left(t, x):
    hi = x.astype(_MXU)
    r1 = x - hi.astype(f32)
    mid = r1.astype(_MXU)
    lo = (r1 - mid.astype(f32)).astype(_MXU)
    dot = functools.partial(jnp.dot, preferred_element_type=f32)
    return dot(t, hi) + dot(t, mid) + dot(t, lo)


def _split3_right(x, t):
    hi = x.astype(_MXU)
    r1 = x - hi.astype(f32)
    mid = r1.astype(_MXU)
    lo = (r1 - mid.astype(f32)).astype(_MXU)
    dot = functools.partial(jnp.dot, preferred_element_type=f32)
    return dot(hi, t) + dot(mid, t) + dot(lo, t)


def _fox_cum_fwd(z_all, bias, *, name):
    s = z_all.shape[0]
    b = _attn_block(s)
    fcol = _AL["fox_f"] // LANES

    def body(f_ref, b_ref, cum_ref, cumt_ref, carry_ref):
        i = pl.program_id(0)

        @pl.when(i == 0)
        def _():
            carry_ref[...] = jnp.zeros_like(carry_ref)

        u = f_ref[...] + b_ref[...]
        lf = jnp.minimum(u, 0.0) - jnp.log1p(jnp.exp(-jnp.abs(u)))
        cum = _split3_left(_tri(b, "row_ge"), lf) + carry_ref[...]
        cum_ref[...] = cum
        cumt_ref[...] = cum.T[0:8, :]
        carry_ref[...] = cum_ref[b - 1:b, :]

    return pl.pallas_call(
        body, out_shape=(_SDS((s, LANES), f32), _SDS((8, s), f32)), grid=(s // b,),
        in_specs=[pl.BlockSpec((b, LANES), lambda i: (i, fcol)), pl.BlockSpec((1, LANES), lambda i: (0, 0))],
        out_specs=(pl.BlockSpec((b, LANES), lambda i: (i, 0)), pl.BlockSpec((8, b), lambda i: (0, i))),
        scratch_shapes=[pltpu.VMEM((1, LANES), f32)], compiler_params=_cp(("arbitrary",)), name=name,
    )(z_all, bias)


def _fox_cum_bwd(z_all, bias, dcum_t, *, name):
    s = z_all.shape[0]
    b = _attn_block(s)
    nb = s // b
    fcol = _AL["fox_f"] // LANES

    def body(f_ref, b_ref, dc_ref, df_ref, db_ref, carry_ref):
        i = pl.program_id(0)

        @pl.when(i == 0)
        def _():
            carry_ref[...] = jnp.zeros_like(carry_ref)
            db_ref[...] = jnp.zeros_like(db_ref)

        dc = dc_ref[...]
        rev = _split3_right(dc, _tri(b, "row_ge")) + carry_ref[...]
        carry_ref[...] = carry_ref[...] + jnp.sum(dc, axis=1, keepdims=True)
        dlf = jnp.concatenate([rev, jnp.zeros((LANES - 8, b), f32)], axis=0).T
        u = f_ref[...] + b_ref[...]
        df = jnp.where(_iota((b, LANES), 1) < N_HEADS, dlf * (1.0 - _sigmoid(u)), 0.0)
        df_ref[...] = df
        db_ref[...] += jnp.sum(df, axis=0, keepdims=True)

    return pl.pallas_call(
        body, out_shape=(_SDS((s, LANES), f32), _SDS((1, LANES), f32)), grid=(nb,),
        in_specs=[pl.BlockSpec((b, LANES), lambda i: (nb - 1 - i, fcol)), pl.BlockSpec((1, LANES), lambda i: (0, 0)),
                  pl.BlockSpec((8, b), lambda i: (0, nb - 1 - i))],
        out_specs=(pl.BlockSpec((b, LANES), lambda i: (nb - 1 - i, 0)), pl.BlockSpec((1, LANES), lambda i: (0, 0))),
        scratch_shapes=[pltpu.VMEM((8, 1), f32)], compiler_params=_cp(("arbitrary",)), name=name,
    )(z_all, bias, dcum_t)


MLA_QW = 2 * LANES


def _rms_rows(x):
    r = lax.rsqrt(jnp.mean(x * x, axis=-1, keepdims=True) + RMS_EPS)
    return x * r, r


def _mla_prep_fwd(z_all, gq, gkv, wuq, wk, wv, tables, *, name):
    s = z_all.shape[0]
    rb = _row_block(s)
    half = MLA_ROPE // 2

    def body(cq_ref, ckv_ref, kr_ref, gq_ref, gkv_ref, wuq_ref, wk_ref, wv_ref, cos_ref, sa_ref, sb_ref,
             q_ref, k_ref, v_ref):
        cos, sa, sb = cos_ref[...], sa_ref[...], sb_ref[...]
        xh, _ = _rms_rows(cq_ref[...])
        qp = _mm(xh * gq_ref[...], wuq_ref[...])
        kh, _ = _rms_rows(ckv_ref[...])
        nkv = kh * gkv_ref[...]
        kn = _mm(nkv, wk_ref[...])
        v_ref[...] = _mm(nkv, wv_ref[...])
        kr = _rope(kr_ref[...], cos, sa, sb, half)
        for h in range(N_HEADS):
            lo, mid, hi = h * MLA_QW, h * MLA_QW + LANES, (h + 1) * MLA_QW
            q_ref[:, lo:mid] = qp[:, lo:mid]
            q_ref[:, mid:hi] = _rope(qp[:, mid:hi], cos, sa, sb, half)
            k_ref[:, lo:mid] = kn[:, h * LANES:(h + 1) * LANES]
            k_ref[:, mid:hi] = kr

    row = lambda w, cb: pl.BlockSpec((rb, w), lambda i: (i, cb))
    whole = lambda a: pl.BlockSpec(a.shape, lambda i: (0,) * a.ndim)
    return pl.pallas_call(
        body, out_shape=(_SDS((s, N_HEADS * MLA_QW), f32), _SDS((s, N_HEADS * MLA_QW), f32), _SDS((s, GROUP), f32)),
        grid=(s // rb,),
        in_specs=[row(MLA_Q_RANK, _AL["mla_cq"] // MLA_Q_RANK), row(LANES, _AL["mla_ckv"] // LANES),
                  row(LANES, _AL["mla_k_rope"] // LANES), whole(gq), whole(gkv), whole(wuq), whole(wk), whole(wv),
                  row(LANES, 0), row(LANES, 0), row(LANES, 0)],
        out_specs=(row(N_HEADS * MLA_QW, 0), row(N_HEADS * MLA_QW, 0), row(GROUP, 0)),
        compiler_params=_cp(("parallel",)), name=name,
    )(z_all, z_all, z_all, gq, gkv, wuq, wk, wv, *tables)


def _mla_prep_bwd(z_all, gq, gkv, wuq, wk, wv, tables, dq_cat, dk_cat, dv, *, name):
    s = z_all.shape[0]
    rb = _row_block(s)
    half = MLA_ROPE // 2

    def body(cq_ref, ckv_ref, gq_ref, gkv_ref, wuq_ref, wk_ref, wv_ref, cos_ref, sa_ref, sb_ref, dq_ref, dk_ref,
             dv_ref, dcq_ref, dckv_ref, dkr_ref, dwuq_ref, dwk_ref, dwv_ref, dgq_ref, dgkv_ref):
        i = pl.program_id(0)

        @pl.when(i == 0)
        def _():
            for r in (dwuq_ref, dwk_ref, dwv_ref, dgq_ref, dgkv_ref):
                r[...] = jnp.zeros_like(r)

        cos, sa, sb = cos_ref[...], sa_ref[...], sb_ref[...]
        parts, knp = [], []
        dkr = jnp.zeros((rb, LANES), f32)
        for h in range(N_HEADS):
            lo, mid, hi = h * MLA_QW, h * MLA_QW + LANES, (h + 1) * MLA_QW
            parts += [dq_ref[:, lo:mid], _rope(dq_ref[:, mid:hi], cos, sa, sb, half, transpose=True)]
            knp.append(dk_ref[:, lo:mid])
            dkr = dkr + _rope(dk_ref[:, mid:hi], cos, sa, sb, half, transpose=True)
        dkr_ref[...] = dkr
        dqp = jnp.concatenate(parts, axis=1)
        dkn = jnp.concatenate(knp, axis=1)
        dvv = dv_ref[...]

        def norm_bwd(x_ref, g_ref, w_pairs, dx_ref, dg_ref):
            xh, r = _rms_rows(x_ref[...])
            nx = xh * g_ref[...]
            dn = jnp.zeros_like(xh)
            for w_ref, dw_ref, dy in w_pairs:
                dw_ref[...] += _mm_tn(nx, dy)
                dn = dn + _mm_nt(dy, w_ref[...])
            dxh = dn * g_ref[...]
            dx_ref[...] = r * (dxh - xh * jnp.mean(dxh * xh, axis=-1, keepdims=True))
            dg_ref[...] += jnp.sum(dn * xh, axis=0, keepdims=True)

        norm_bwd(cq_ref, gq_ref, [(wuq_ref, dwuq_ref, dqp)], dcq_ref, dgq_ref)
        norm_bwd(ckv_ref, gkv_ref, [(wk_ref, dwk_ref, dkn), (wv_ref, dwv_ref, dvv)], dckv_ref, dgkv_ref)

    row = lambda w, cb: pl.BlockSpec((rb, w), lambda i: (i, cb))
    whole = lambda a: pl.BlockSpec(a.shape, lambda i: (0,) * a.ndim)
    return pl.pallas_call(
        body,
        out_shape=(_SDS((s, MLA_Q_RANK), f32), _SDS((s, LANES), f32), _SDS((s, LANES), f32), _SDS(wuq.shape, f32),
                   _SDS(wk.shape, f32), _SDS(wv.shape, f32), _SDS(gq.shape, f32), _SDS(gkv.shape, f32)),
        grid=(s // rb,),
        in_specs=[row(MLA_Q_RANK, _AL["mla_cq"] // MLA_Q_RANK), row(LANES, _AL["mla_ckv"] // LANES), whole(gq),
                  whole(gkv), whole(wuq), whole(wk), whole(wv), row(LANES, 0), row(LANES, 0), row(LANES, 0),
                  row(N_HEADS * MLA_QW, 0), row(N_HEADS * MLA_QW, 0), row(GROUP, 0)],
        out_specs=(row(MLA_Q_RANK, 0), row(LANES, 0), row(LANES, 0), whole(wuq), whole(wk), whole(wv), whole(gq),
                   whole(gkv)),
        compiler_params=_cp(("arbitrary",)), name=name,
    )(z_all, z_all, gq, gkv, wuq, wk, wv, *tables, dq_cat, dk_cat, dv)


def _silu_grad(x):
    sg = _sigmoid(x)
    return sg * (1.0 + x * (1.0 - sg))


def _nsa_cmp_fwd(ra, rb_, pos, w1, w2, tables, *, name):
    nr = ra.shape[1]
    hw = ra.shape[2]

    def body(ra_ref, rb_ref, pos_ref, w1_ref, w2_ref, cos_ref, sa_ref, sb_ref, out_ref, hp_ref):
        for k in range(2):
            xa = ra_ref[k] + pos_ref[k, :, 0:hw]
            xb = rb_ref[k] + pos_ref[k, :, hw:2 * hw]
            hp = _mm(xa, w1_ref[k, 0:hw, :]) + _mm(xb, w1_ref[k, hw:2 * hw, :])
            hp_ref[k] = hp
            out = _mm(hp * _sigmoid(hp), w2_ref[k])
            if k == 0:
                out = _rope(out, cos_ref[...], sa_ref[...], sb_ref[...], HEAD_DIM // 2)
            out_ref[k] = out

    return pl.pallas_call(body, out_shape=(_SDS((2, nr, HEAD_DIM), f32), _SDS((2, nr, HEAD_DIM), f32)),
                          compiler_params=_cp(), name=name)(ra, rb_, pos, w1, w2, *tables)


def _nsa_cmp_bwd(ra, rb_, pos, w1, w2, tables, hp, dout, *, name):
    nr = ra.shape[1]
    hw = ra.shape[2]

    def body(ra_ref, rb_ref, pos_ref, w1_ref, w2_ref, cos_ref, sa_ref, sb_ref, hp_ref, do_ref,
             dxa_ref, dxb_ref, dw1_ref, dw2_ref):
        for k in range(2):
            d_out = do_ref[k]
            if k == 0:
                d_out = _rope(d_out, cos_ref[...], sa_ref[...], sb_ref[...], HEAD_DIM // 2, transpose=True)
            hpv = hp_ref[k]
            dw2_ref[k] = _mm_tn(hpv * _sigmoid(hpv), d_out)
            dhp = _mm_nt(d_out, w2_ref[k]) * _silu_grad(hpv)
            xa = ra_ref[k] + pos_ref[k, :, 0:hw]
            xb = rb_ref[k] + pos_ref[k, :, hw:2 * hw]
            dw1_ref[k, 0:hw, :] = _mm_tn(xa, dhp)
            dw1_ref[k, hw:2 * hw, :] = _mm_tn(xb, dhp)
            dxa_ref[k] = _mm_nt(dhp, w1_ref[k, 0:hw, :])
            dxb_ref[k] = _mm_nt(dhp, w1_ref[k, hw:2 * hw, :])

    return pl.pallas_call(
        body, out_shape=(_SDS((2, nr, hw), f32), _SDS((2, nr, hw), f32), _SDS(w1.shape, f32), _SDS(w2.shape, f32)),
        compiler_params=_cp(), name=name)(ra, rb_, pos, w1, w2, *tables, hp, dout)


def _nsa_consts(s):
    b = _attn_block(s)
    nr = s // CMP_STRIDE
    n_cmp = (s - CMP_LEN) // CMP_STRIDE + 1
    n_sel = s // SEL_LEN
    cmp_start = np.arange(n_cmp) * CMP_STRIDE
    sel_start = np.arange(n_sel) * SEL_LEN
    overlap = np.clip(np.minimum(cmp_start[:, None] + CMP_LEN, sel_start[None, :] + SEL_LEN)
                      - np.maximum(cmp_start[:, None], sel_start[None, :]), 0, None)
    m2s = np.zeros((nr, LANES), np.float32)
    m2s[:n_cmp, :n_sel] = overlap / CMP_LEN
    e3 = np.zeros((s // b, LANES, b), np.float32)
    tok = np.arange(s)
    e3[tok // b, tok // SEL_LEN, tok % b] = 1.0
    return jnp.asarray(m2s, _MXU), jnp.asarray(e3, _MXU)


def _nsa_masks(i, b, d):
    qpos = i * b + _iota((b, b), 0)
    kpos = (i - d) * b + _iota((b, b), 1)
    return (kpos <= qpos) & (kpos > qpos - WINDOW)


def _nsa_fwd(qr, kvc, ksr, vs, kwr, vw, z_all, m2s, e3, *, name):
    s = qr.shape[0]
    b = _attn_block(s)
    nq = s // b
    nr = kvc.shape[1]
    n_sel = s // SEL_LEN
    top_n = min(SEL_TOPN, n_sel)
    nd = -(-WINDOW // b)
    scale = HEAD_DIM ** -0.5
    bcol = _AL["nsa_branch"] // LANES
    H = N_HEADS

    def body(q_ref, kvc_ref, ks_ref, vs_ref, kw_ref, vw_ref, br_ref, m2s_ref, e3_ref,
             o_ref, oc_ref, os_ref, ow_ref, st_ref, sel_ref, m_sc, l_sc, acc_sc):
        i = pl.program_id(0)
        lane = _iota((b, LANES), 1)
        hs = lambda h: slice(h * HEAD_DIM, (h + 1) * HEAD_DIM)

        cmp_mask = (CMP_STRIDE * _iota((b, nr), 1) + (CMP_LEN - 1)) <= (i * b + _iota((b, nr), 0))
        imp = jnp.zeros((b, LANES), f32)
        stats = jnp.zeros((b, LANES), f32)
        for h in range(H):
            zc = jnp.where(cmp_mask, _mm_nt(q_ref[:, hs(h)], kvc_ref[0]) * scale, NEG_INF)
            m = jnp.max(zc, axis=1, keepdims=True)
            p = jnp.where(cmp_mask, jnp.exp(zc - m), 0.0)
            l = jnp.sum(p, axis=1, keepdims=True)
            some = l > 0.0
            lsafe = jnp.where(some, l, 1.0)
            pc = p * jnp.where(some, 1.0 / lsafe, 0.0)
            oc_ref[:, hs(h)] = _mm(pc, kvc_ref[1])
            imp = imp + _mm(pc, m2s_ref[...])
            stats = jnp.where(lane == h, jnp.where(some, m + jnp.log(lsafe), 0.0), stats)

        cur = jnp.right_shift(i * b + _iota((b, LANES), 0), int(math.log2(SEL_LEN)))
        forced = (lane == 0) | (lane == cur) | (lane == cur - 1)
        score = jnp.where(lane <= cur, jnp.where(forced, FORCED_BONUS, imp), NEG_INF)
        score = jnp.where(lane < n_sel, score, -3e38)
        rank = jnp.zeros((b, LANES), f32)
        for j in range(n_sel):
            col = score[:, j:j + 1]
            rank = rank + jnp.where(col > score, 1.0, jnp.where(col == score, jnp.where(lane > j, 1.0, 0.0), 0.0))
        sel = jnp.where(lane < n_sel, jnp.where(rank < top_n, 1.0, 0.0), 0.0)
        sel_ref[...] = sel
        sel_b = sel.astype(_MXU)

        def reset():
            m_sc[...] = jnp.full(m_sc.shape, NEG_INF, f32)
            l_sc[...] = jnp.zeros_like(l_sc)
            acc_sc[...] = jnp.zeros_like(acc_sc)

        def update(h, z, mask, vch):
            zm = jnp.where(mask, z, NEG_INF)
            m_old = m_sc[h]
            m_new = jnp.maximum(m_old, jnp.max(zm, axis=1, keepdims=True))
            p = jnp.where(mask, jnp.exp(zm - m_new), 0.0)
            alpha = jnp.exp(m_old - m_new)
            l_sc[h] = alpha * l_sc[h] + jnp.sum(p, axis=1, keepdims=True)
            acc_sc[h] = alpha * acc_sc[h] + _mm(p, vch)
            m_sc[h] = m_new

        def finish(out_ref, branch, stats):
            for h in range(H):
                out_ref[:, hs(h)] = acc_sc[h] / l_sc[h]
                stats = jnp.where(lane == 4 * branch + h, m_sc[h] + jnp.log(l_sc[h]), stats)
            return stats

        def sel_chunk(c, diag):
            st = pl.multiple_of(c * b, b)
            mask = _mm(sel_b, e3_ref[c]) > 0.5
            if diag:
                mask = mask & _lower_mask(b, False)
            kch, vch = ks_ref[pl.ds(st, b), :], vs_ref[pl.ds(st, b), :]
            for h in range(H):
                update(h, _mm_nt(q_ref[:, hs(h)], kch) * scale, mask, vch)

        reset()

        def sel_loop(c, carry):
            sel_chunk(c, False)
            return carry

        lax.fori_loop(0, i, sel_loop, 0)
        sel_chunk(i, True)
        stats = finish(os_ref, 1, stats)

        reset()
        for d in range(nd, -1, -1):
            @pl.when(i >= d)
            def _():
                st = pl.multiple_of((i - d) * b, b)
                mask = _nsa_masks(i, b, d)
                kch, vch = kw_ref[pl.ds(st, b), :], vw_ref[pl.ds(st, b), :]
                for h in range(H):
                    update(h, _mm_nt(q_ref[:, hs(h)], kch) * scale, mask, vch)
        stats = finish(ow_ref, 2, stats)
        st_ref[...] = stats

        g = _sigmoid(br_ref[...])
        for h in range(H):
            o_ref[:, hs(h)] = (g[:, 3 * h:3 * h + 1] * oc_ref[:, hs(h)] + g[:, 3 * h + 1:3 * h + 2] * os_ref[:, hs(h)]
                               + g[:, 3 * h + 2:3 * h + 3] * ow_ref[:, hs(h)])

    blk = lambda w: pl.BlockSpec((b, w), lambda i: (i, 0))
    whole = lambda a: pl.BlockSpec(a.shape, lambda i: (0,) * a.ndim)
    return pl.pallas_call(
        body, out_shape=tuple(_SDS((s, GROUP), f32) for _ in range(4)) + (_SDS((s, LANES), f32), _SDS((s, LANES), f32)),
        grid=(nq,),
        in_specs=[blk(GROUP), whole(kvc), whole(ksr), whole(vs), whole(kwr), whole(vw),
                  pl.BlockSpec((b, LANES), lambda i: (i, bcol)), whole(m2s), whole(e3)],
        out_specs=(blk(GROUP),) * 4 + (blk(LANES), blk(LANES)),
        scratch_shapes=[pltpu.VMEM((H, b, 1), f32), pltpu.VMEM((H, b, 1), f32), pltpu.VMEM((H, b, HEAD_DIM), f32)],
        compiler_params=_cp(("parallel",)), name=name,
    )(qr, kvc, ksr, vs, kwr, vw, z_all, m2s, e3)


def _nsa_bwd(do, qr, kvc, ksr, vs, kwr, vw, z_all, oc, os_, ow, stats, sel, e3, *, name):
    s = qr.shape[0]
    b = _attn_block(s)
    nq = s // b
    nr = kvc.shape[1]
    nd = -(-WINDOW // b)
    scale = HEAD_DIM ** -0.5
    bcol = _AL["nsa_branch"] // LANES
    H = N_HEADS

    def body(do_ref, q_ref, kvc_ref, ks_ref, vs_ref, kw_ref, vw_ref, br_ref, oc_ref, os_ref, ow_ref, st_ref, sel_ref,
             e3_ref, dq_ref, dbr_ref, dkvc_ref, dks_ref, dvs_ref, dkw_ref, dvw_ref, dob_sc, delta_sc, dq_sc):
        i = pl.program_id(0)

        @pl.when(i == 0)
        def _():
            for r in (dkvc_ref, dks_ref, dvs_ref, dkw_ref, dvw_ref):
                r[...] = jnp.zeros_like(r)

        lane = _iota((b, LANES), 1)
        hs = lambda h: slice(h * HEAD_DIM, (h + 1) * HEAD_DIM)
        g = _sigmoid(br_ref[...])
        stats = st_ref[...]
        dbr = jnp.zeros((b, LANES), f32)
        outs = (oc_ref, os_ref, ow_ref)
        for h in range(H):
            doh = do_ref[:, hs(h)]
            for j in range(3):
                gj = g[:, 3 * h + j:3 * h + j + 1]
                dgj = jnp.sum(doh * outs[j][:, hs(h)], axis=1, keepdims=True)
                dbr = jnp.where(lane == 3 * h + j, dgj * gj * (1.0 - gj), dbr)
                dob_sc[j, :, hs(h)] = gj * doh
                delta_sc[j, h] = gj * dgj
        dbr_ref[...] = dbr
        dq_sc[...] = jnp.zeros_like(dq_sc)

        def branch(j, h, z, mask, kch, vch):
            qh = q_ref[:, hs(h)]
            p = jnp.where(mask, jnp.exp(jnp.where(mask, z, NEG_INF) - stats[:, 4 * j + h:4 * j + h + 1]), 0.0)
            dob = dob_sc[j, :, hs(h)]
            ds = p * (_mm_nt(dob, vch) - delta_sc[j, h])
            dq_sc[:, hs(h)] += _mm(ds, kch) * scale
            return _mm_tn(ds, qh) * scale, _mm_tn(p, dob)

        cmp_mask = (CMP_STRIDE * _iota((b, nr), 1) + (CMP_LEN - 1)) <= (i * b + _iota((b, nr), 0))
        kc, vc = kvc_ref[0], kvc_ref[1]
        for h in range(H):
            dk, dv = branch(0, h, _mm_nt(q_ref[:, hs(h)], kc) * scale, cmp_mask, kc, vc)
            dkvc_ref[0] += dk
            dkvc_ref[1] += dv

        sel_b = sel_ref[...].astype(_MXU)

        def chunk(j, c, mask, k_ref, v_ref, dk_ref, dv_ref):
            st = pl.multiple_of(c * b, b)
            kch, vch = k_ref[pl.ds(st, b), :], v_ref[pl.ds(st, b), :]
            dk = jnp.zeros((b, HEAD_DIM), f32)
            dv = jnp.zeros((b, HEAD_DIM), f32)
            for h in range(H):
                dkh, dvh = branch(j, h, _mm_nt(q_ref[:, hs(h)], kch) * scale, mask, kch, vch)
                dk, dv = dk + dkh, dv + dvh
            dk_ref[pl.ds(st, b), :] += dk
            dv_ref[pl.ds(st, b), :] += dv

        def sel_chunk(c, diag):
            mask = _mm(sel_b, e3_ref[c]) > 0.5
            if diag:
                mask = mask & _lower_mask(b, False)
            chunk(1, c, mask, ks_ref, vs_ref, dks_ref, dvs_ref)

        def sel_loop(c, carry):
            sel_chunk(c, False)
            return carry

        lax.fori_loop(0, i, sel_loop, 0)
        sel_chunk(i, True)

        for d in range(nd, -1, -1):
            @pl.when(i >= d)
            def _():
                chunk(2, i - d, _nsa_masks(i, b, d), kw_ref, vw_ref, dkw_ref, dvw_ref)

        dq_ref[...] = dq_sc[...]

    blk = lambda w: pl.BlockSpec((b, w), lambda i: (i, 0))
    whole = lambda a: pl.BlockSpec(a.shape, lambda i: (0,) * a.ndim)
    stream = _SDS((s, HEAD_DIM), f32)
    return pl.pallas_call(
        body, out_shape=(_SDS((s, GROUP), f32), _SDS((s, LANES), f32), _SDS(kvc.shape, f32), stream, stream, stream,
                         stream),
        grid=(nq,),
        in_specs=[blk(GROUP), blk(GROUP), whole(kvc), whole(ksr), whole(vs), whole(kwr), whole(vw),
                  pl.BlockSpec((b, LANES), lambda i: (i, bcol)), blk(GROUP), blk(GROUP), blk(GROUP), blk(LANES),
                  blk(LANES), whole(e3)],
        out_specs=(blk(GROUP), blk(LANES), whole(kvc), whole(ksr), whole(vs), whole(kwr), whole(vw)),
        scratch_shapes=[pltpu.VMEM((3, b, GROUP), f32), pltpu.VMEM((3, H, b, 1), f32), pltpu.VMEM((b, GROUP), f32)],
        compiler_params=_cp(("arbitrary",)), name=name,
    )(do, qr, kvc, ksr, vs, kwr, vw, z_all, oc, os_, ow, stats, sel, e3)


def _seg(a, name, width=None):
    return a[:, _AL[name]:_AL[name] + (width or _WID[name])]


def _cmp_rows(tok):
    s = tok.shape[0]
    r = tok.reshape(s // CMP_STRIDE, CMP_STRIDE * HEAD_DIM)
    return r, jnp.concatenate([r[1:], jnp.zeros((1, r.shape[1]), r.dtype)], axis=0)


def _cmp_unrows(dxa, dxb):
    s = dxa.shape[0] * CMP_STRIDE
    return (dxa + jnp.concatenate([jnp.zeros((1, dxa.shape[1]), dxa.dtype), dxb[:-1]], axis=0)).reshape(s, HEAD_DIM)


_GATES = ("sb_gate", "nsa_gate", "fox_gate", "mla_gate")


def _layer_fwd(x, p, c, tag):
    s = x.shape[0]
    b = _attn_block(s)
    h = _rms_fwd(x, p["pre_g"], out_dtype=_MXU, name=f"prenorm_{tag}")
    z = _matmul(h, p["w_in"], "nn", bias=p["b_in"], name=f"inproj_{tag}")
    o_sb = _sb_fwd(z, name=f"sb_fwd_{tag}")

    qr, ksr, kwr = _rope_call([(z, GROUP, _AL["nsa_q"] // GROUP), (z, LANES, _AL["nsa_k_sel"] // LANES),
                               (z, LANES, _AL["nsa_k_win"] // LANES)], c["tabs128"], HEAD_DIM // 2, False,
                              name=f"nsa_rope_{tag}")
    (rak, rbk), (rav, rbv) = _cmp_rows(_seg(z, "nsa_k_cmp")), _cmp_rows(_seg(z, "nsa_v_cmp"))
    ra, rb_ = jnp.stack([rak, rav]), jnp.stack([rbk, rbv])
    kvc, hp = _nsa_cmp_fwd(ra, rb_, p["cmp_pos"], p["cmp_w1"], p["cmp_w2"], c["tabs_cmp"], name=f"nsa_cmp_{tag}")
    vs, vw = _seg(z, "nsa_v_sel"), _seg(z, "nsa_v_win")
    o_nsa, oc, os_, ow, stats, sel = _nsa_fwd(qr, kvc, ksr, vs, kwr, vw, z, c["m2s"], c["e3"], name=f"nsa_fwd_{tag}")

    cum, cum_t8 = _fox_cum_fwd(z, p["fox_bias"], name=f"fox_cum_{tag}")
    cum_t = cum_t8.reshape(8, s // b, 1, b)
    fcols = tuple(_AL[n] // HEAD_DIM for n in ("fox_q", "fox_k", "fox_v"))
    o_fox, lse_fox = _attn_fwd(z, z, z, *fcols, HEAD_DIM, cum, cum_t, scale=HEAD_DIM ** -0.5, name=f"fox_fwd_{tag}")

    qcat, kcat, vm = _mla_prep_fwd(z, p["gq"], p["gkv"], p["wuq"], p["wk"], p["wv"], c["tabs64"],
                                   name=f"mla_prep_{tag}")
    o_mla, lse_mla = _attn_fwd(qcat, kcat, vm, 0, 0, 0, MLA_QW, None, None, scale=(MLA_NOPE + MLA_ROPE) ** -0.5,
                               name=f"mla_fwd_{tag}")

    o_all = jnp.concatenate([o_sb, o_nsa, o_fox, o_mla], axis=1)
    gates = jnp.concatenate([_seg(z, n) for n in _GATES], axis=1)
    mix = _gate_fwd(o_all, gates, name=f"gate_{tag}")
    u = _matmul(mix, p["w_out"], "nn", name=f"outproj_{tag}")
    y = _postnorm_fwd(u, p["post_g"], x, name=f"postnorm_{tag}")
    saved = dict(x=x, h=h, z=z, qr=qr, ksr=ksr, kwr=kwr, ra=ra, rb=rb_, kvc=kvc, hp=hp, vs=vs, vw=vw, oc=oc, os=os_,
                 ow=ow, stats=stats, sel=sel, cum=cum, cum_t=cum_t, o_fox=o_fox, lse_fox=lse_fox, qcat=qcat, kcat=kcat,
                 vm=vm, o_mla=o_mla, lse_mla=lse_mla, o_all=o_all, gates=gates, mix=mix, u=u)
    return y, saved


def _layer_bwd(dy, sv, p, c, tag):
    z = sv["z"]
    s = z.shape[0]
    du, dg_post = _rms_bwd(dy, sv["u"], p["post_g"], name=f"postnorm_bwd_{tag}")
    dmix = _matmul(du, p["w_out"], "nt", name=f"outproj_dx_{tag}")
    dw_out = _matmul(sv["mix"], du, "tn", name=f"outproj_dw_{tag}")
    do_all, dgates = _gate_bwd(dmix, sv["o_all"], sv["gates"], name=f"gate_bwd_{tag}")
    do_sb, do_nsa, do_fox, do_mla = (do_all[:, k * GROUP:(k + 1) * GROUP] for k in range(4))
    dgate = [dgates[:, k * GROUP:(k + 1) * GROUP] for k in range(4)]

    sb_dq, sb_dk, sb_dv = _sb_bwd(z, do_sb, name=f"sb_bwd_{tag}")

    n_dq, n_dbr, n_dkvc, n_dks, n_dvs, n_dkw, n_dvw = _nsa_bwd(
        do_nsa, sv["qr"], sv["kvc"], sv["ksr"], sv["vs"], sv["kwr"], sv["vw"], z, sv["oc"], sv["os"], sv["ow"],
        sv["stats"], sv["sel"], c["e3"], name=f"nsa_bwd_{tag}")
    dxa, dxb, dw1, dw2 = _nsa_cmp_bwd(sv["ra"], sv["rb"], p["cmp_pos"], p["cmp_w1"], p["cmp_w2"], c["tabs_cmp"],
                                      sv["hp"], n_dkvc, name=f"nsa_cmp_bwd_{tag}")
    n_dq, n_dks, n_dkw = _rope_call([(n_dq, GROUP, 0), (n_dks, LANES, 0), (n_dkw, LANES, 0)], c["tabs128"],
                                    HEAD_DIM // 2, True, name=f"nsa_rope_bwd_{tag}")
    dpos = _colsum(jnp.concatenate([dxa[0], dxb[0], dxa[1], dxb[1]], axis=1), name=f"nsa_dpos_{tag}")
    flat = CMP_LEN * HEAD_DIM

    fcols = tuple(_AL[n] // HEAD_DIM for n in ("fox_q", "fox_k", "fox_v"))
    f_dq, f_dk, f_dv, f_dck = _attn_bwd(z, z, z, *fcols, HEAD_DIM, do_fox, sv["o_fox"], sv["lse_fox"], sv["cum"],
                                        sv["cum_t"], scale=HEAD_DIM ** -0.5, name=f"fox_bwd_{tag}")
    dcum_t = jnp.pad(f_dck.reshape(N_HEADS, s), ((0, 8 - N_HEADS), (0, 0)))
    f_df, f_dbias = _fox_cum_bwd(z, p["fox_bias"], dcum_t, name=f"fox_cum_bwd_{tag}")

    m_dq, m_dk, m_dv = _attn_bwd(sv["qcat"], sv["kcat"], sv["vm"], 0, 0, 0, MLA_QW, do_mla, sv["o_mla"], sv["lse_mla"],
                                 None, None, scale=(MLA_NOPE + MLA_ROPE) ** -0.5, name=f"mla_bwd_{tag}")
    m_dcq, m_dckv, m_dkr, m_dwuq, m_dwk, m_dwv, m_dgq, m_dgkv = _mla_prep_bwd(
        z, p["gq"], p["gkv"], p["wuq"], p["wk"], p["wv"], c["tabs64"], m_dq, m_dk, m_dv, name=f"mla_prep_bwd_{tag}")

    dz = jnp.concatenate([
        sb_dq, sb_dk, sb_dv, dgate[0],
        n_dq, _cmp_unrows(dxa[0], dxb[0]), _cmp_unrows(dxa[1], dxb[1]), n_dks, n_dvs, n_dkw, n_dvw, n_dbr, dgate[1],
        f_dq, f_dk, f_dv, f_df, dgate[2],
        m_dcq, m_dckv, m_dkr, dgate[3]], axis=1)
    dh = _matmul(dz, p["w_in"], "nt", name=f"inproj_dx_{tag}")
    dw_in_al = _matmul(sv["h"], dz, "tn", name=f"inproj_dw_{tag}")
    db_al = _colsum(dz, name=f"inproj_db_{tag}")
    dx, dg_pre = _rms_bwd(dh, sv["x"], p["pre_g"], res=dy, name=f"prenorm_bwd_{tag}")

    unalign = lambda a: jnp.concatenate([a[:, _AL[n]:_AL[n] + w] for n, w, _ in _SEGS], axis=1)
    qw = MLA_NOPE + MLA_ROPE
    grads = {
        "pre_norm_g": dg_pre[0], "post_norm_g": dg_post[0], "w_in": unalign(dw_in_al), "b_in": unalign(db_al)[0],
        "w_out": dw_out, "fox_forget_bias": f_dbias[0, :N_HEADS],
        "nsa_cmp_pos_k": dpos[0, :flat].reshape(CMP_LEN, HEAD_DIM), "nsa_cmp_w1_k": dw1[0], "nsa_cmp_w2_k": dw2[0],
        "nsa_cmp_pos_v": dpos[0, flat:].reshape(CMP_LEN, HEAD_DIM), "nsa_cmp_w1_v": dw1[1], "nsa_cmp_w2_v": dw2[1],
        "mla_q_norm_g": m_dgq[0],
        "mla_w_uq": jnp.concatenate([m_dwuq[:, MLA_QW * h:MLA_QW * h + qw] for h in range(N_HEADS)], axis=1),
        "mla_kv_norm_g": m_dgkv[0],
        "mla_w_ukv": jnp.concatenate(sum([[m_dwk[:, LANES * h:LANES * (h + 1)], m_dwv[:, LANES * h:LANES * (h + 1)]]
                                          for h in range(N_HEADS)], []), axis=1),
    }
    return dx, grads


def _layer_params(w, l):
    w_in = w["w_in"][l]
    zero = lambda n: jnp.zeros((w_in.shape[0], n), w_in.dtype)
    cols = []
    for n, wd, wa in _SEGS:
        cols.append(w_in[:, _ORIG[n]:_ORIG[n] + wd])
        if wa > wd:
            cols.append(zero(wa - wd))
    b_in = w["b_in"][l]
    bcols = []
    for n, wd, wa in _SEGS:
        bcols.append(b_in[_ORIG[n]:_ORIG[n] + wd])
        if wa > wd:
            bcols.append(jnp.zeros((wa - wd,), f32))
    qw = MLA_NOPE + MLA_ROPE
    w_uq, w_ukv = w["mla_w_uq"][l], w["mla_w_ukv"][l]
    uq = []
    for h in range(N_HEADS):
        uq += [w_uq[:, qw * h:qw * (h + 1)], jnp.zeros((w_uq.shape[0], MLA_QW - qw), w_uq.dtype)]
    kw_ = 2 * LANES
    flat = CMP_LEN * HEAD_DIM
    return dict(
        pre_g=w["pre_norm_g"][l].reshape(1, -1), post_g=w["post_norm_g"][l].reshape(1, -1),
        w_in=jnp.concatenate(cols, axis=1), b_in=jnp.concatenate(bcols).reshape(1, -1), w_out=w["w_out"][l],
        fox_bias=jnp.pad(w["fox_forget_bias"][l], (0, LANES - N_HEADS)).reshape(1, LANES),
        cmp_pos=jnp.stack([w["nsa_cmp_pos_k"][l].reshape(1, flat), w["nsa_cmp_pos_v"][l].reshape(1, flat)]),
        cmp_w1=jnp.stack([w["nsa_cmp_w1_k"][l], w["nsa_cmp_w1_v"][l]]),
        cmp_w2=jnp.stack([w["nsa_cmp_w2_k"][l], w["nsa_cmp_w2_v"][l]]),
        gq=w["mla_q_norm_g"][l].reshape(1, -1), gkv=w["mla_kv_norm_g"][l].reshape(1, -1),
        wuq=jnp.concatenate(uq, axis=1),
        wk=jnp.concatenate([w_ukv[:, kw_ * h:kw_ * h + LANES] for h in range(N_HEADS)], axis=1),
        wv=jnp.concatenate([w_ukv[:, kw_ * h + LANES:kw_ * (h + 1)] for h in range(N_HEADS)], axis=1),
    )


def _consts(s):
    pos = jnp.arange(s)
    m2s, e3 = _nsa_consts(s)
    return dict(tabs128=_rope_tables(pos, HEAD_DIM), tabs64=_rope_tables(pos, MLA_ROPE),
                tabs_cmp=_rope_tables(jnp.arange(s // CMP_STRIDE) * CMP_STRIDE + (CMP_LEN - 1), HEAD_DIM),
                m2s=m2s, e3=e3)


def _place():
    return lax.axis_index("x"), lax.axis_index("y"), lax.axis_index("c")


def _other_chips(x, y):
    return [(1 - x, y), (x, 1 - y), (1 - x, 1 - y)]


_SEMS3 = [pltpu.SemaphoreType.DMA((3,)), pltpu.SemaphoreType.DMA((3,)), pltpu.SemaphoreType.DMA]


def _gather_chips(a, *, name):
    def body(a_ref, out_ref, send_sems, recv_sems, local_sem):
        x, y, c = _place()
        me = 2 * x + y
        sibling = (x, y, 1 - c)
        chips = _other_chips(x, y)

        def copy(k, src, dst, to):
            return pltpu.make_async_remote_copy(src, dst, send_sems.at[k], recv_sems.at[k], device_id=to,
                                                device_id_type=_MESH)

        mine = pltpu.make_async_copy(a_ref, out_ref.at[me], local_sem)
        mine.start()
        first = [copy(k, a_ref.at[c], out_ref.at[me, c], (px, py, c)) for k, (px, py) in enumerate(chips)]
        for cp in first:
            cp.start()
        passed = [copy(3 + k, out_ref.at[2 * px + py, c], out_ref.at[2 * px + py, c], sibling)
                  for k, (px, py) in enumerate(chips)]
        for k, (px, py) in enumerate(chips):
            copy(k, a_ref.at[c], out_ref.at[2 * px + py, c], (px, py, c)).wait_recv()
            passed[k].start()
        for k, (px, py) in enumerate(chips):
            copy(3 + k, a_ref.at[c], out_ref.at[2 * px + py, 1 - c], sibling).wait_recv()
        for cp in first + passed:
            cp.wait_send()
        mine.wait()

    return pl.pallas_call(body, out_shape=_SDS((4,) + a.shape, a.dtype), in_specs=[_ANY], out_specs=_ANY,
                          scratch_shapes=[pltpu.SemaphoreType.DMA((6,)), pltpu.SemaphoreType.DMA((6,)),
                                          pltpu.SemaphoreType.DMA], name=name)(a)


def _alltoall_chips(g, *, name):
    def body(g_ref, out_ref, send_sems, recv_sems, local_sem):
        x, y, c = _place()
        me = 2 * x + y
        mine = pltpu.make_async_copy(g_ref.at[me], out_ref.at[me], local_sem)
        mine.start()
        sends = [pltpu.make_async_remote_copy(g_ref.at[2 * px + py], out_ref.at[me], send_sems.at[k], recv_sems.at[k],
                                              device_id=(px, py, c), device_id_type=_MESH)
                 for k, (px, py) in enumerate(_other_chips(x, y))]
        for cp in sends:
            cp.start()
        for k, (px, py) in enumerate(_other_chips(x, y)):
            pltpu.make_async_remote_copy(g_ref.at[me], out_ref.at[2 * px + py], send_sems.at[k], recv_sems.at[k],
                                         device_id=(px, py, c), device_id_type=_MESH).wait_recv()
        for cp in sends:
            cp.wait_send()
        mine.wait()

    return pl.pallas_call(body, out_shape=_SDS(g.shape, g.dtype), in_specs=[_ANY], out_specs=_ANY,
                          scratch_shapes=_SEMS3, name=name)(g)


def _swap_other_half(g, *, name):
    p, n2, w = g.shape
    h = n2 // 2

    def body(g_ref, out_ref, send_sem, recv_sem):
        x, y, c = _place()
        theirs = g_ref.at[:, pl.ds(pl.multiple_of((1 - c) * h, 8), h), :]
        cp = pltpu.make_async_remote_copy(theirs, out_ref, send_sem, recv_sem, device_id=(x, y, 1 - c),
                                          device_id_type=_MESH)
        cp.start()
        cp.wait()

    return pl.pallas_call(body, out_shape=_SDS((p, h, w), g.dtype), in_specs=[_ANY], out_specs=_ANY,
                          scratch_shapes=[pltpu.SemaphoreType.DMA, pltpu.SemaphoreType.DMA], name=name)(g)


def _pair_gather(f, *, name):
    h, w = f.shape

    def body(f_ref, out_ref, send_sem, recv_sem, local_sem):
        x, y, c = _place()
        my_rows = out_ref.at[c]
        their_rows = out_ref.at[1 - c]
        mine = pltpu.make_async_copy(f_ref, my_rows, local_sem)
        mine.start()
        cp = pltpu.make_async_remote_copy(f_ref, my_rows, send_sem, recv_sem, device_id=(x, y, 1 - c),
                                          device_id_type=_MESH)
        cp.start()
        pltpu.make_async_remote_copy(f_ref, their_rows, send_sem, recv_sem, device_id=(x, y, 1 - c),
                                     device_id_type=_MESH).wait_recv()
        cp.wait_send()
        mine.wait()

    return pl.pallas_call(body, out_shape=_SDS((2, h, w), f.dtype), in_specs=[_ANY], out_specs=_ANY,
                          scratch_shapes=[pltpu.SemaphoreType.DMA] * 3, name=name)(f)


def _gather_all(a, *, name):
    def body(a_ref, out_ref, send_sems, recv_sems, local_sem):
        x, y, c = _place()
        flip = lambda v, f: (1 - v) if f else v
        peers = [(flip(x, f & 4), flip(y, f & 2), flip(c, f & 1)) for f in range(1, 8)]
        me = 4 * x + 2 * y + c
        mine = pltpu.make_async_copy(a_ref, out_ref.at[me], local_sem)
        mine.start()
        sends = [pltpu.make_async_remote_copy(a_ref, out_ref.at[me], send_sems.at[k], recv_sems.at[k], device_id=peer,
                                              device_id_type=_MESH) for k, peer in enumerate(peers)]
        for cp in sends:
            cp.start()
        for k, (px, py, pc) in enumerate(peers):
            pltpu.make_async_remote_copy(a_ref, out_ref.at[4 * px + 2 * py + pc], send_sems.at[k], recv_sems.at[k],
                                         device_id=(px, py, pc), device_id_type=_MESH).wait_recv()
        for cp in sends:
            cp.wait_send()
        mine.wait()

    return pl.pallas_call(body, out_shape=_SDS((8,) + a.shape, a.dtype), in_specs=[_ANY], out_specs=_ANY,
                          scratch_shapes=[pltpu.SemaphoreType.DMA((7,)), pltpu.SemaphoreType.DMA((7,)),
                                          pltpu.SemaphoreType.DMA], name=name)(a)


def _add_my_half(g, r, *, name):
    p, n2, w = g.shape
    h = n2 // 2
    rb = _pick(h, (1024, 976, 512, 256, 128, 64, 32, 16))
    nb = h // rb

    def body(c_ref, g_ref, r_ref, o_ref):
        o_ref[...] = (g_ref[...] + r_ref[...]).astype(o_ref.dtype)

    blk = pl.BlockSpec((1, rb, w), lambda s, i, c_ref: (s, i, 0))
    grid_spec = pltpu.PrefetchScalarGridSpec(
        num_scalar_prefetch=1, grid=(p, nb),
        in_specs=[pl.BlockSpec((1, rb, w), lambda s, i, c_ref: (s, i + c_ref[0] * nb, 0)), blk], out_specs=blk)
    c = lax.axis_index("c").astype(jnp.int32).reshape(1)
    return pl.pallas_call(body, out_shape=_SDS((p, h, w), _WIRE), grid_spec=grid_spec,
                          compiler_params=_cp(("parallel", "parallel")), name=name)(c, g, r)


_WEIGHTS = ("pre_norm_g", "post_norm_g", "w_in", "b_in", "w_out", "fox_forget_bias", "nsa_cmp_pos_k", "nsa_cmp_w1_k",
            "nsa_cmp_w2_k", "nsa_cmp_pos_v", "nsa_cmp_w1_v", "nsa_cmp_w2_v", "mla_q_norm_g", "mla_w_uq",
            "mla_kv_norm_g", "mla_w_ukv")
_SHARD_AXIS = {"w_in": 2, "w_out": 1, "nsa_cmp_w1_k": 1, "nsa_cmp_w1_v": 1, "mla_w_uq": 2, "mla_w_ukv": 2}
_N_CHIPS = 4
_PACK_UNIT = 16 * LANES


def _pack(arrays, dtype):
    rows = []
    for a in arrays:
        v = a.astype(dtype).reshape(-1)
        pad = (-v.shape[0]) % _PACK_UNIT
        if pad:
            v = jnp.concatenate([v, jnp.zeros((pad,), dtype)])
        rows.append(v.reshape(-1, LANES))
    return jnp.concatenate(rows, axis=0)


def _unpack(flat, shapes):
    out, r = [], 0
    for shp in shapes:
        n = int(np.prod(shp))
        nr = -(-n // _PACK_UNIT) * (_PACK_UNIT // LANES)
        out.append(flat[r:r + nr].reshape(-1)[:n].reshape(shp))
        r += nr
    return out


def kernel(x, pre_norm_g, post_norm_g, w_in, b_in, w_out, fox_forget_bias, nsa_cmp_pos_k, nsa_cmp_w1_k, nsa_cmp_w2_k, nsa_cmp_pos_v, nsa_cmp_w1_v, nsa_cmp_w2_v, mla_q_norm_g, mla_w_uq, mla_kv_norm_g, mla_w_ukv, loss_target, m_pre_norm_g, m_post_norm_g, m_w_in, m_b_in, m_w_out, m_fox_forget_bias, m_nsa_cmp_pos_k, m_nsa_cmp_w1_k, m_nsa_cmp_w2_k, m_nsa_cmp_pos_v, m_nsa_cmp_w1_v, m_nsa_cmp_w2_v, m_mla_q_norm_g, m_mla_w_uq, m_mla_kv_norm_g, m_mla_w_ukv, v_pre_norm_g, v_post_norm_g, v_w_in, v_b_in, v_w_out, v_fox_forget_bias, v_nsa_cmp_pos_k, v_nsa_cmp_w1_k, v_nsa_cmp_w2_k, v_nsa_cmp_pos_v, v_nsa_cmp_w1_v, v_nsa_cmp_w2_v, v_mla_q_norm_g, v_mla_w_uq, v_mla_kv_norm_g, v_mla_w_ukv):
    given = dict(locals())
    local = {n: given[n] for n in _WEIGHTS}
    depth = pre_norm_g.shape[0]
    xs, target = x[0], loss_target[0]
    s = xs.shape[0]
    sharded = [n for n in _WEIGHTS if n in _SHARD_AXIS]
    small = [n for n in _WEIGHTS if n not in _SHARD_AXIS]

    shard_shapes = [local[n].shape for n in sharded]
    mine = _pack([local[n] for n in sharded], _MXU)
    everyone = _gather_chips(mine.reshape(2, -1, LANES), name="gather_weights").reshape((_N_CHIPS,) + mine.shape)
    per_chip = [_unpack(everyone[k], shard_shapes) for k in range(_N_CHIPS)]
    full = dict(local)
    for j, n in enumerate(sharded):
        full[n] = jnp.concatenate([per_chip[k][j] for k in range(_N_CHIPS)], axis=_SHARD_AXIS[n])

    consts = _consts(s)
    params = [_layer_params(full, l) for l in range(depth)]
    act, saved = xs, []
    for l in range(depth):
        act, sv = _layer_fwd(act, params[l], consts, f"l{l}")
        saved.append(sv)
    dy, loss_parts = _loss_head(act, target, name="loss_head")
    layer_grads = [None] * depth
    for l in reversed(range(depth)):
        dy, layer_grads[l] = _layer_bwd(dy, saved[l], params[l], consts, f"l{l}")
    grad_x = dy[None]
    grads = {n: jnp.stack([layer_grads[l][n] for l in range(depth)]) for n in _WEIGHTS}

    def chip_slice(n, k):
        a, ax = grads[n], _SHARD_AXIS[n]
        w = a.shape[ax] // _N_CHIPS
        return lax.slice_in_dim(a, k * w, (k + 1) * w, axis=ax)

    g_all = jnp.stack([_pack([chip_slice(n, k) for n in sharded], f32) for k in range(_N_CHIPS)])
    from_sibling = _swap_other_half(g_all, name="reduce_pair")
    pair_sum = _add_my_half(g_all, from_sibling, name="reduce_pair_add")
    from_chips = _alltoall_chips(pair_sum, name="reduce_chips")
    my_half = _sum_slots(from_chips, name="reduce_chips_add")
    g_shard = _unpack(_pair_gather(my_half, name="reduce_share").reshape(-1, LANES), shard_shapes)
    summed = dict(zip(sharded, g_shard))

    loss_row = jnp.concatenate([jnp.sum(loss_parts).reshape(1), jnp.zeros((LANES - 1,), f32)])
    small_shapes = [(LANES,)] + [grads[n].shape for n in small]
    contrib = _pack([loss_row] + [grads[n] for n in small], f32)
    pad_rows = (-contrib.shape[0]) % 8
    if pad_rows:
        contrib = jnp.concatenate([contrib, jnp.zeros((pad_rows, LANES), f32)], axis=0)
    total = _unpack(_sum_slots(_gather_all(contrib, name="gather_small"), name="sum_small"), small_shapes)
    loss = total[0][0]
    summed.update(zip(small, total[1:]))

    deltas, new_m, new_v = {}, {}, {}
    for n in _WEIGHTS:
        deltas[n], new_m[n], new_v[n] = _adamw(local[n], summed[n], given["m_" + n], given["v_" + n], name=f"adamw_{n}")
    return (loss, grad_x, *[summed[n] for n in _WEIGHTS], *[deltas[n] for n in _WEIGHTS],
            *[new_m[n] for n in _WEIGHTS], *[new_v[n] for n in _WEIGHTS])
```

```python
import functools
import math

import numpy as np
import jax
import jax.numpy as jnp
from jax import lax
from jax.experimental import pallas as pl
from jax.experimental.pallas import tpu as pltpu

f32 = jnp.float32
bf16 = jnp.bfloat16
_MXU = jnp.bfloat16
_WIRE = jnp.bfloat16
_SDS = jax.ShapeDtypeStruct
_ANY = pl.BlockSpec(memory_space=pl.ANY)
_MESH = pl.DeviceIdType.MESH

D_MODEL = 2048
N_HEADS = 4
HEAD_DIM = 128
GROUP = 512
RMS_EPS = 1e-6
NEG_INF = -1e30
ROPE_THETA = 10000.0
CMP_LEN, CMP_STRIDE, SEL_LEN, SEL_TOPN, WINDOW = 32, 16, 64, 16, 512
FORCED_BONUS = 1e6
MLA_Q_RANK, MLA_KV_RANK, MLA_NOPE, MLA_ROPE = 384, 128, 128, 64
ADAM_LR, ADAM_B1, ADAM_B2, ADAM_EPS, ADAM_WD, ADAM_STEP = 0.001, 0.9, 0.999, 1e-08, 0.01, 10
LANES = 128
VMEM_LIMIT = 48 * 1024 * 1024

_SEGS = (
    ("sb_q", 512, 512), ("sb_k", 512, 512), ("sb_v", 512, 512), ("sb_gate", 512, 512),
    ("nsa_q", 512, 512), ("nsa_k_cmp", 128, 128), ("nsa_v_cmp", 128, 128), ("nsa_k_sel", 128, 128),
    ("nsa_v_sel", 128, 128), ("nsa_k_win", 128, 128), ("nsa_v_win", 128, 128), ("nsa_branch", 12, 128),
    ("nsa_gate", 512, 512), ("fox_q", 512, 512), ("fox_k", 512, 512), ("fox_v", 512, 512), ("fox_f", 4, 128),
    ("fox_gate", 512, 512), ("mla_cq", 384, 384), ("mla_ckv", 128, 128), ("mla_k_rope", 64, 128),
    ("mla_gate", 512, 512),
)
_ORIG, _AL, _WID = {}, {}, {}
_o = _a = 0
for _n, _w, _wa in _SEGS:
    _ORIG[_n], _AL[_n], _WID[_n] = _o, _a, _w
    _o += _w
    _a += _wa
IN_WIDTH, ZW = _o, _a


def _cp(sem=None):
    return pltpu.CompilerParams(dimension_semantics=sem, vmem_limit_bytes=VMEM_LIMIT)


def _mm(a, b):
    return jnp.dot(a.astype(_MXU), b.astype(_MXU), preferred_element_type=f32)


def _mm_nt(a, b):
    return lax.dot_general(a.astype(_MXU), b.astype(_MXU), (((1,), (1,)), ((), ())), preferred_element_type=f32)


def _mm_tn(a, b):
    return lax.dot_general(a.astype(_MXU), b.astype(_MXU), (((0,), (0,)), ((), ())), preferred_element_type=f32)


def _mm_split(x, t):
    hi = x.astype(_MXU)
    lo = (x - hi.astype(f32)).astype(_MXU)
    return jnp.dot(hi, t, preferred_element_type=f32) + jnp.dot(lo, t, preferred_element_type=f32)


def _sigmoid(x):
    return 1.0 / (1.0 + jnp.exp(-x))


def _iota(shape, dim):
    return lax.broadcasted_iota(jnp.int32, shape, dim)


def _pick(n, prefs):
    for p in prefs:
        if n % p == 0:
            return p
    return n


def _matmul(a, b, mode, *, bias=None, out_dtype=f32, name):
    if mode == "nn":
        (M, K), (K2, N) = a.shape, b.shape
    elif mode == "nt":
        (M, K), (N, K2) = a.shape, b.shape
    else:
        (K, M), (K2, N) = a.shape, b.shape
    assert K == K2
    tm = _pick(M, (512, 384, 256, 128))
    tn = _pick(N, (512, 384, 256, 128))
    tk = K if K <= 2048 else _pick(K, (2048, 2432, 1024, 512))
    nk = K // tk
    a_spec = {"nn": pl.BlockSpec((tm, tk), lambda i, j, k: (i, k)),
              "nt": pl.BlockSpec((tm, tk), lambda i, j, k: (i, k)),
              "tn": pl.BlockSpec((tk, tm), lambda i, j, k: (k, i))}[mode]
    b_spec = {"nn": pl.BlockSpec((tk, tn), lambda i, j, k: (k, j)),
              "nt": pl.BlockSpec((tn, tk), lambda i, j, k: (j, k)),
              "tn": pl.BlockSpec((tk, tn), lambda i, j, k: (k, j))}[mode]
    dot = {"nn": _mm, "nt": _mm_nt, "tn": _mm_tn}[mode]
    has_bias = bias is not None

    def body(*refs):
        if has_bias:
            a_ref, b_ref, bias_ref, o_ref, acc_ref = refs
        else:
            a_ref, b_ref, o_ref, acc_ref = refs
            bias_ref = None
        k = pl.program_id(2)
        part = dot(a_ref[...], b_ref[...])

        def finish(total):
            if has_bias:
                total = total + bias_ref[...]
            o_ref[...] = total.astype(o_ref.dtype)

        if nk == 1:
            finish(part)
        else:
            @pl.when(k == 0)
            def _():
                acc_ref[...] = part

            @pl.when(k > 0)
            def _():
                acc_ref[...] += part

            @pl.when(k == nk - 1)
            def _():
                finish(acc_ref[...])

    in_specs = [a_spec, b_spec]
    args = [a, b]
    if has_bias:
        in_specs.append(pl.BlockSpec((1, tn), lambda i, j, k: (0, j)))
        args.append(bias.reshape(1, N))
    return pl.pallas_call(
        body, out_shape=_SDS((M, N), out_dtype), grid=(M // tm, N // tn, nk),
        in_specs=in_specs, out_specs=pl.BlockSpec((tm, tn), lambda i, j, k: (i, j)),
        scratch_shapes=[pltpu.VMEM((tm, tn), f32)],
        compiler_params=_cp(("parallel", "parallel", "arbitrary")), name=name,
    )(*args)


def _row_block(s):
    return _pick(s, (256, 128))


def _rms_fwd(x, g, *, out_dtype, name):
    s, d = x.shape
    rb = _row_block(s)

    def body(x_ref, g_ref, o_ref):
        xv = x_ref[...]
        r = lax.rsqrt(jnp.mean(xv * xv, axis=-1, keepdims=True) + RMS_EPS)
        o_ref[...] = (xv * r * g_ref[...]).astype(o_ref.dtype)

    return pl.pallas_call(
        body, out_shape=_SDS((s, d), out_dtype), grid=(s // rb,),
        in_specs=[pl.BlockSpec((rb, d), lambda i: (i, 0)), pl.BlockSpec((1, d), lambda i: (0, 0))],
        out_specs=pl.BlockSpec((rb, d), lambda i: (i, 0)), compiler_params=_cp(("parallel",)), name=name,
    )(x, g.reshape(1, d))


def _postnorm_fwd(u, g, x, *, name):
    s, d = u.shape
    rb = _row_block(s)

    def body(u_ref, g_ref, x_ref, o_ref):
        uv = u_ref[...]
        r = lax.rsqrt(jnp.mean(uv * uv, axis=-1, keepdims=True) + RMS_EPS)
        o_ref[...] = x_ref[...] + uv * r * g_ref[...]

    return pl.pallas_call(
        body, out_shape=_SDS((s, d), f32), grid=(s // rb,),
        in_specs=[pl.BlockSpec((rb, d), lambda i: (i, 0)), pl.BlockSpec((1, d), lambda i: (0, 0)),
                  pl.BlockSpec((rb, d), lambda i: (i, 0))],
        out_specs=pl.BlockSpec((rb, d), lambda i: (i, 0)), compiler_params=_cp(("parallel",)), name=name,
    )(u, g.reshape(1, d), x)


def _fold_rows(v):
    r = v.shape[0]
    acc = v[0:8]
    for k in range(1, r // 8):
        acc = acc + v[8 * k:8 * k + 8]
    return acc


def _rms_bwd(dy, x, g, res=None, *, name):
    s, d = x.shape
    rb = _row_block(s)
    nb = s // rb
    has_res = res is not None

    def body(*refs):
        if has_res:
            dy_ref, x_ref, g_ref, res_ref, dx_ref, dg_ref, acc_ref = refs
        else:
            dy_ref, x_ref, g_ref, dx_ref, dg_ref, acc_ref = refs
        i = pl.program_id(0)
        xv = x_ref[...]
        r = lax.rsqrt(jnp.mean(xv * xv, axis=-1, keepdims=True) + RMS_EPS)
        xh = xv * r
        dyv = dy_ref[...]
        dxh = dyv * g_ref[...]
        dx = r * (dxh - xh * jnp.mean(dxh * xh, axis=-1, keepdims=True))
        if has_res:
            dx = dx + res_ref[...]
        dx_ref[...] = dx
        part = _fold_rows(dyv * xh)

        @pl.when(i == 0)
        def _():
            acc_ref[...] = part

        @pl.when(i > 0)
        def _():
            acc_ref[...] += part

        @pl.when(i == nb - 1)
        def _():
            dg_ref[...] = jnp.sum(acc_ref[...], axis=0, keepdims=True)

    blk = pl.BlockSpec((rb, d), lambda i: (i, 0))
    in_specs = [blk, blk, pl.BlockSpec((1, d), lambda i: (0, 0))] + ([blk] if has_res else [])
    args = [dy, x, g.reshape(1, d)] + ([res] if has_res else [])
    return pl.pallas_call(
        body, out_shape=(_SDS((s, d), f32), _SDS((1, d), f32)), grid=(nb,), in_specs=in_specs,
        out_specs=(blk, pl.BlockSpec((1, d), lambda i: (0, 0))),
        scratch_shapes=[pltpu.VMEM((8, d), f32)], compiler_params=_cp(("arbitrary",)), name=name,
    )(*args)


def _loss_head(y, target, *, name):
    s, d = y.shape
    rb = _row_block(s)
    nb = s // rb

    def body(y_ref, t_ref, dy_ref, l_ref):
        i = pl.program_id(0)
        e = y_ref[...] - t_ref[...]
        dy_ref[...] = e * (1.0 / d)
        rows = _fold_rows(e * e)
        part = rows[:, 0:LANES]
        for k in range(1, d // LANES):
            part = part + rows[:, k * LANES:(k + 1) * LANES]
        part = part * (0.5 / d)

        @pl.when(i == 0)
        def _():
            l_ref[...] = part

        @pl.when(i > 0)
        def _():
            l_ref[...] += part

    blk = pl.BlockSpec((rb, d), lambda i: (i, 0))
    return pl.pallas_call(
        body, out_shape=(_SDS((s, d), f32), _SDS((8, LANES), f32)), grid=(nb,), in_specs=[blk, blk],
        out_specs=(blk, pl.BlockSpec((8, LANES), lambda i: (0, 0))),
        compiler_params=_cp(("arbitrary",)), name=name,
    )(y, target)


def _colsum(a, *, name):
    s, n = a.shape
    rb = _row_block(s)
    nb = s // rb
    tn = _pick(n, (2432, 2048, 1024, 512, 384, 128))

    def body(a_ref, o_ref, acc_ref):
        i = pl.program_id(1)
        part = _fold_rows(a_ref[...])

        @pl.when(i == 0)
        def _():
            acc_ref[...] = part

        @pl.when(i > 0)
        def _():
            acc_ref[...] += part

        @pl.when(i == nb - 1)
        def _():
            o_ref[...] = jnp.sum(acc_ref[...], axis=0, keepdims=True)

    return pl.pallas_call(
        body, out_shape=_SDS((1, n), f32), grid=(n // tn, nb),
        in_specs=[pl.BlockSpec((rb, tn), lambda j, i: (i, j))], out_specs=pl.BlockSpec((1, tn), lambda j, i: (0, j)),
        scratch_shapes=[pltpu.VMEM((8, tn), f32)], compiler_params=_cp(("parallel", "arbitrary")), name=name,
    )(a)


def _gate_fwd(o, gate, *, name):
    s, d = o.shape
    rb = _row_block(s)

    def body(o_ref, g_ref, m_ref):
        gv = g_ref[...]
        m_ref[...] = (o_ref[...] * (gv * _sigmoid(gv))).astype(m_ref.dtype)

    blk = pl.BlockSpec((rb, d), lambda i: (i, 0))
    return pl.pallas_call(body, out_shape=_SDS((s, d), _MXU), grid=(s // rb,), in_specs=[blk, blk], out_specs=blk,
                          compiler_params=_cp(("parallel",)), name=name)(o, gate)


def _gate_bwd(dmix, o, gate, *, name):
    s, d = o.shape
    rb = _row_block(s)

    def body(dm_ref, o_ref, g_ref, do_ref, dg_ref):
        gv = g_ref[...]
        sg = _sigmoid(gv)
        dm = dm_ref[...]
        do_ref[...] = dm * (gv * sg)
        dg_ref[...] = dm * o_ref[...] * (sg * (1.0 + gv * (1.0 - sg)))

    blk = pl.BlockSpec((rb, d), lambda i: (i, 0))
    return pl.pallas_call(body, out_shape=(_SDS((s, d), f32), _SDS((s, d), f32)), grid=(s // rb,),
                          in_specs=[blk, blk, blk], out_specs=(blk, blk), compiler_params=_cp(("parallel",)),
                          name=name)(dmix, o, gate)


def _adamw(w, g, m, v, *, name):
    shape = w.shape
    cols = shape[-1]
    rows = int(np.prod(shape[:-1])) if len(shape) > 1 else 1
    to2 = lambda t: t.reshape(rows, cols)
    rb = _pick(rows, (128, 64, 32, 16, 8)) if rows * cols * 4 > (1 << 20) else rows

    def body(w_ref, g_ref, m_ref, v_ref, d_ref, nm_ref, nv_ref):
        gv = g_ref[...]
        mn = ADAM_B1 * m_ref[...] + (1.0 - ADAM_B1) * gv
        vn = ADAM_B2 * v_ref[...] + (1.0 - ADAM_B2) * (gv * gv)
        m_hat = mn / (1.0 - ADAM_B1 ** ADAM_STEP)
        v_hat = vn / (1.0 - ADAM_B2 ** ADAM_STEP)
        d_ref[...] = -ADAM_LR * (m_hat / (jnp.sqrt(v_hat) + ADAM_EPS) + ADAM_WD * w_ref[...])
        nm_ref[...] = mn
        nv_ref[...] = vn

    blk = pl.BlockSpec((rb, cols), lambda i: (i, 0))
    out = pl.pallas_call(body, out_shape=tuple(_SDS((rows, cols), f32) for _ in range(3)), grid=(rows // rb,),
                         in_specs=[blk] * 4, out_specs=(blk,) * 3, compiler_params=_cp(("parallel",)),
                         name=name)(to2(w), to2(g), to2(m), to2(v))
    return tuple(t.reshape(shape) for t in out)


def _sum_slots(a, *, name):
    p, n, c = a.shape
    rb = n if p * n * c * 4 <= (4 << 20) else _pick(n, (1024, 976, 512, 256, 128, 64, 32, 16, 8))

    def body(a_ref, o_ref):
        acc = a_ref[0].astype(f32)
        for k in range(1, p):
            acc = acc + a_ref[k].astype(f32)
        o_ref[...] = acc

    return pl.pallas_call(body, out_shape=_SDS((n, c), f32), grid=(n // rb,),
                          in_specs=[pl.BlockSpec((p, rb, c), lambda i: (0, i, 0))],
                          out_specs=pl.BlockSpec((rb, c), lambda i: (i, 0)), compiler_params=_cp(("parallel",)),
                          name=name)(a)


def _add2(a, b, *, name):
    p, n, c = a.shape
    rb = _pick(n, (1024, 976, 512, 256, 128, 64, 32, 16, 8))

    def body(a_ref, b_ref, o_ref):
        o_ref[...] = a_ref[...] + b_ref[...]

    blk = pl.BlockSpec((1, rb, c), lambda s, i: (s, i, 0))
    return pl.pallas_call(body, out_shape=_SDS((p, n, c), f32), grid=(p, n // rb), in_specs=[blk, blk], out_specs=blk,
                          compiler_params=_cp(("parallel", "parallel")), name=name)(a, b)


def _rope_tables(pos, dim):
    half = dim // 2
    inv = ROPE_THETA ** (-jnp.arange(half, dtype=f32) / half)
    ang = pos.astype(f32)[:, None] * inv[None, :]
    c, s = jnp.cos(ang), jnp.sin(ang)
    z = jnp.zeros_like(c)
    pad = [jnp.zeros((pos.shape[0], LANES - dim), f32)] if dim < LANES else []
    return (jnp.concatenate([c, c] + pad, axis=1), jnp.concatenate([-s, z] + pad, axis=1),
            jnp.concatenate([z, s] + pad, axis=1))


def _rope(x, cos, sa, sb, half, transpose=False):
    if transpose:
        return x * cos + pltpu.roll(x * sa, half, 1) + pltpu.roll(x * sb, LANES - half, 1)
    return x * cos + pltpu.roll(x, LANES - half, 1) * sa + pltpu.roll(x, half, 1) * sb


def _rope_call(items, tables, half, transpose, *, name):
    s = items[0][0].shape[0]
    rb = _row_block(s)
    n = len(items)

    def body(*refs):
        cos, sa, sb = refs[n][...], refs[n + 1][...], refs[n + 2][...]
        for k in range(n):
            x_ref, o_ref = refs[k], refs[n + 3 + k]
            for j in range(items[k][1] // LANES):
                sl = slice(j * LANES, (j + 1) * LANES)
                o_ref[:, sl] = _rope(x_ref[:, sl], cos, sa, sb, half, transpose)

    in_specs = [pl.BlockSpec((rb, w), functools.partial(lambda i, cb: (i, cb), cb=cb)) for _, w, cb in items]
    in_specs += [pl.BlockSpec((rb, LANES), lambda i: (i, 0))] * 3
    out_specs = tuple(pl.BlockSpec((rb, w), lambda i: (i, 0)) for _, w, _ in items)
    return pl.pallas_call(
        body, out_shape=tuple(_SDS((s, w), f32) for _, w, _ in items), grid=(s // rb,), in_specs=in_specs,
        out_specs=out_specs, compiler_params=_cp(("parallel",)), name=name,
    )(*[a for a, _, _ in items], *tables)


def _attn_block(s):
    return _pick(s, (256, 128))


def _lower_mask(b, strict):
    r, c = _iota((b, b), 0), _iota((b, b), 1)
    return (c < r) if strict else (c <= r)


def _pick_lane(block, h):
    return jnp.sum(jnp.where(_iota(block.shape, 1) == h, block, 0.0), axis=1, keepdims=True)


def _attn_fwd(q, k, v, qcol, kcol, vcol, dq, cum, cum_t, *, scale, name):
    s = q.shape[0]
    b = _attn_block(s)
    nq = s // b
    has_bias = cum is not None

    def body(*refs):
        if has_bias:
            q_ref, k_ref, v_ref, cum_ref, cumt_ref, o_ref, lse_ref = refs
        else:
            q_ref, k_ref, v_ref, o_ref, lse_ref = refs
        h, i = pl.program_id(0), pl.program_id(1)
        qv = q_ref[...].astype(_MXU)
        cq = _pick_lane(cum_ref[...], h) if has_bias else None

        def chunk(c, carry, diag):
            m, l, acc = carry
            st = pl.multiple_of(c * b, b)
            z = _mm_nt(qv, k_ref[pl.ds(st, b), :]) * scale
            if has_bias:
                z = z + cq - cumt_ref[c]
            if diag:
                mask = _lower_mask(b, False)
                z = jnp.where(mask, z, NEG_INF)
            m_new = jnp.maximum(m, jnp.max(z, axis=1, keepdims=True))
            p = jnp.exp(z - m_new)
            if diag:
                p = jnp.where(mask, p, 0.0)
            alpha = jnp.exp(m - m_new)
            l = alpha * l + jnp.sum(p, axis=1, keepdims=True)
            acc = alpha * acc + _mm(p, v_ref[pl.ds(st, b), :])
            return m_new, l, acc

        init = (jnp.full((b, 1), NEG_INF, f32), jnp.zeros((b, 1), f32), jnp.zeros((b, HEAD_DIM), f32))
        carry = lax.fori_loop(0, i, lambda c, cr: chunk(c, cr, False), init)
        m, l, acc = chunk(i, carry, True)
        o_ref[...] = acc / l
        lse_ref[...] = m + jnp.log(l)

    in_specs = [pl.BlockSpec((b, dq), lambda h, i: (i, qcol + h)), pl.BlockSpec((s, dq), lambda h, i: (0, kcol + h)),
                pl.BlockSpec((s, HEAD_DIM), lambda h, i: (0, vcol + h))]
    args = [q, k, v]
    if has_bias:
        in_specs += [pl.BlockSpec((b, LANES), lambda h, i: (i, 0)),
                     pl.BlockSpec((None, nq, 1, b), lambda h, i: (h, 0, 0, 0))]
        args += [cum, cum_t]
    return pl.pallas_call(
        body, out_shape=(_SDS((s, N_HEADS * HEAD_DIM), f32), _SDS((N_HEADS, s, 1), f32)), grid=(N_HEADS, nq),
        in_specs=in_specs,
        out_specs=(pl.BlockSpec((b, HEAD_DIM), lambda h, i: (i, h)), pl.BlockSpec((None, b, 1), lambda h, i: (h, i, 0))),
        compiler_params=_cp(("parallel", "parallel")), name=name,
    )(*args)


def _attn_bwd(q, k, v, qcol, kcol, vcol, dq, do, o, lse, cum, cum_t, *, scale, name):
    s = q.shape[0]
    b = _attn_block(s)
    nq = s // b
    has_bias = cum is not None

    def body(*refs):
        if has_bias:
            (q_ref, k_ref, v_ref, do_ref, o_ref, lse_ref, cum_ref, cumt_ref, dq_ref, dk_ref, dv_ref, dck_ref,
             p_sc, dp_sc) = refs
        else:
            q_ref, k_ref, v_ref, do_ref, o_ref, lse_ref, dq_ref, dk_ref, dv_ref = refs
        h, i = pl.program_id(0), pl.program_id(1)

        @pl.when(i == 0)
        def _():
            dk_ref[...] = jnp.zeros_like(dk_ref)
            dv_ref[...] = jnp.zeros_like(dv_ref)
            if has_bias:
                dck_ref[...] = jnp.zeros_like(dck_ref)

        qv = q_ref[...].astype(_MXU)
        dov = do_ref[...]
        dob = dov.astype(_MXU)
        lse_v = lse_ref[...]
        cq = _pick_lane(cum_ref[...], h) if has_bias else None

        def probs(c, diag):
            st = pl.multiple_of(c * b, b)
            z = _mm_nt(qv, k_ref[pl.ds(st, b), :]) * scale
            if has_bias:
                z = z + cq - cumt_ref[c]
            p = jnp.exp(z - lse_v)
            if diag:
                p = jnp.where(_lower_mask(b, False), p, 0.0)
            return p, _mm_nt(dob, v_ref[pl.ds(st, b), :])

        if has_bias:
            def first(c, acc, diag):
                p, dp = probs(c, diag)
                p_sc[c] = p
                dp_sc[c] = dp
                return acc + jnp.sum(p * dp, axis=1, keepdims=True)

            delta = lax.fori_loop(0, i, lambda c, a: first(c, a, False), jnp.zeros((b, 1), f32))
            delta = first(i, delta, True)
        else:
            delta = jnp.sum(dov * o_ref[...], axis=1, keepdims=True)

        def chunk(c, dq_acc, diag):
            st = pl.multiple_of(c * b, b)
            kc = k_ref[pl.ds(st, b), :]
            p, dp = (p_sc[c], dp_sc[c]) if has_bias else probs(c, diag)
            ds = p * (dp - delta)
            dk_ref[pl.ds(st, b), :] += _mm_tn(ds, qv) * scale
            dv_ref[pl.ds(st, b), :] += _mm_tn(p, dob)
            if has_bias:
                dck_ref[c] += -jnp.sum(ds, axis=0, keepdims=True)
            return dq_acc + _mm(ds, kc)

        acc = lax.fori_loop(0, i, lambda c, a: chunk(c, a, False), jnp.zeros((b, dq), f32))
        acc = chunk(i, acc, True)
        dq_ref[...] = acc * scale

    in_specs = [pl.BlockSpec((b, dq), lambda h, i: (i, qcol + h)), pl.BlockSpec((s, dq), lambda h, i: (0, kcol + h)),
                pl.BlockSpec((s, HEAD_DIM), lambda h, i: (0, vcol + h)),
                pl.BlockSpec((b, HEAD_DIM), lambda h, i: (i, h)), pl.BlockSpec((b, HEAD_DIM), lambda h, i: (i, h)),
                pl.BlockSpec((None, b, 1), lambda h, i: (h, i, 0))]
    args = [q, k, v, do, o, lse]
    out_shape = [_SDS((s, N_HEADS * dq), f32), _SDS((s, N_HEADS * dq), f32), _SDS((s, N_HEADS * HEAD_DIM), f32)]
    out_specs = [pl.BlockSpec((b, dq), lambda h, i: (i, h)), pl.BlockSpec((s, dq), lambda h, i: (0, h)),
                 pl.BlockSpec((s, HEAD_DIM), lambda h, i: (0, h))]
    if has_bias:
        in_specs += [pl.BlockSpec((b, LANES), lambda h, i: (i, 0)),
                     pl.BlockSpec((None, nq, 1, b), lambda h, i: (h, 0, 0, 0))]
        args += [cum, cum_t]
        out_shape.append(_SDS((N_HEADS, nq, 1, b), f32))
        out_specs.append(pl.BlockSpec((None, nq, 1, b), lambda h, i: (h, 0, 0, 0)))
    return pl.pallas_call(
        body, out_shape=tuple(out_shape), grid=(N_HEADS, nq), in_specs=in_specs, out_specs=tuple(out_specs),
        scratch_shapes=[pltpu.VMEM((nq, b, b), f32)] * 2 if has_bias else [],
        compiler_params=_cp(("parallel", "arbitrary")), name=name,
    )(*args)


def _tri(b, kind):
    r, c = _iota((b, b), 0), _iota((b, b), 1)
    cond = {"row_gt": r > c, "row_lt": r < c, "row_ge": r >= c, "row_le": r <= c}[kind]
    return jnp.where(cond, 1.0, 0.0).astype(_MXU)


def _log_keep(z):
    return -(jnp.maximum(z, 0.0) + jnp.log1p(jnp.exp(-jnp.abs(z))))


def _sb_fwd(z_all, *, name):
    s = z_all.shape[0]
    b = _attn_block(s)
    nq = s // b
    scale = HEAD_DIM ** -0.5
    qcol, kcol, vcol = (_AL[n] // HEAD_DIM for n in ("sb_q", "sb_k", "sb_v"))

    def body(q_ref, k_ref, v_ref, o_ref):
        i = pl.program_id(1)
        qv = q_ref[...].astype(_MXU)
        upper = _tri(b, "row_gt")

        def chunk(c, carry, diag):
            rsum, acc = carry
            st = pl.multiple_of(c * b, b)
            z = _mm_nt(qv, k_ref[pl.ds(st, b), :]) * scale
            lk = _log_keep(z)
            if diag:
                mask = _lower_mask(b, True)
                lk = jnp.where(mask, lk, 0.0)
            a = z + lk + _mm_split(lk, upper) + rsum
            if diag:
                a = jnp.where(mask, a, NEG_INF)
            acc = acc + _mm(jnp.exp(a), v_ref[pl.ds(st, b), :])
            return rsum + jnp.sum(lk, axis=1, keepdims=True), acc

        carry = chunk(i, (jnp.zeros((b, 1), f32), jnp.zeros((b, HEAD_DIM), f32)), True)
        _, acc = lax.fori_loop(0, i, lambda j, cr: chunk(i - 1 - j, cr, False), carry)
        o_ref[...] = acc

    return pl.pallas_call(
        body, out_shape=_SDS((s, GROUP), f32), grid=(N_HEADS, nq),
        in_specs=[pl.BlockSpec((b, HEAD_DIM), lambda h, i: (i, qcol + h)),
                  pl.BlockSpec((s, HEAD_DIM), lambda h, i: (0, kcol + h)),
                  pl.BlockSpec((s, HEAD_DIM), lambda h, i: (0, vcol + h))],
        out_specs=pl.BlockSpec((b, HEAD_DIM), lambda h, i: (i, h)),
        compiler_params=_cp(("parallel", "parallel")), name=name,
    )(z_all, z_all, z_all)


def _sb_bwd(z_all, do, *, name):
    s = z_all.shape[0]
    b = _attn_block(s)
    nq = s // b
    scale = HEAD_DIM ** -0.5
    qcol, kcol, vcol = (_AL[n] // HEAD_DIM for n in ("sb_q", "sb_k", "sb_v"))

    def body(q_ref, k_ref, v_ref, do_ref, dq_ref, dk_ref, dv_ref, z_sc, lk_sc, r_sc):
        i = pl.program_id(1)

        @pl.when(i == 0)
        def _():
            dk_ref[...] = jnp.zeros_like(dk_ref)
            dv_ref[...] = jnp.zeros_like(dv_ref)

        qv = q_ref[...].astype(_MXU)
        dob = do_ref[...].astype(_MXU)
        upper = _tri(b, "row_gt")
        lower = _tri(b, "row_lt")

        def scores(c, rsum, diag):
            st = pl.multiple_of(c * b, b)
            z = _mm_nt(qv, k_ref[pl.ds(st, b), :]) * scale
            lk = _log_keep(z)
            if diag:
                lk = jnp.where(_lower_mask(b, True), lk, 0.0)
            z_sc[c] = z
            lk_sc[c] = lk
            r_sc[c] = _mm_split(lk, upper) + rsum
            return rsum + jnp.sum(lk, axis=1, keepdims=True)

        rsum = scores(i, jnp.zeros((b, 1), f32), True)
        lax.fori_loop(0, i, lambda j, r: scores(i - 1 - j, r, False), rsum)

        def grads(c, carry, diag):
            psum, dq_acc = carry
            st = pl.multiple_of(c * b, b)
            z, lk = z_sc[c], lk_sc[c]
            lb = z + lk
            a = lb + r_sc[c]
            if diag:
                mask = _lower_mask(b, True)
                a = jnp.where(mask, a, NEG_INF)
            w = jnp.exp(a)
            e = _mm_nt(dob, v_ref[pl.ds(st, b), :]) * w
            before = _mm_split(e, lower) + psum
            dz = e * jnp.exp(lk) - before * jnp.exp(lb)
            if diag:
                dz = jnp.where(mask, dz, 0.0)
            kc = k_ref[pl.ds(st, b), :]
            dk_ref[pl.ds(st, b), :] += _mm_tn(dz, qv) * scale
            dv_ref[pl.ds(st, b), :] += _mm_tn(w, dob)
            return psum + jnp.sum(e, axis=1, keepdims=True), dq_acc + _mm(dz, kc)

        carry = lax.fori_loop(0, i, lambda c, cr: grads(c, cr, False),
                              (jnp.zeros((b, 1), f32), jnp.zeros((b, HEAD_DIM), f32)))
        _, dq_acc = grads(i, carry, True)
        dq_ref[...] = dq_acc * scale

    blk = pl.BlockSpec((b, HEAD_DIM), lambda h, i: (i, h))
    full = pl.BlockSpec((s, HEAD_DIM), lambda h, i: (0, h))
    return pl.pallas_call(
        body, out_shape=tuple(_SDS((s, GROUP), f32) for _ in range(3)), grid=(N_HEADS, nq),
        in_specs=[pl.BlockSpec((b, HEAD_DIM), lambda h, i: (i, qcol + h)),
                  pl.BlockSpec((s, HEAD_DIM), lambda h, i: (0, kcol + h)),
                  pl.BlockSpec((s, HEAD_DIM), lambda h, i: (0, vcol + h)), blk],
        out_specs=(blk, full, full),
        scratch_shapes=[pltpu.VMEM((nq, b, b), f32)] * 3,
        compiler_params=_cp(("parallel", "arbitrary")), name=name,
    )(z_all, z_all, z_all, do)


def _split3_left(t, x):
    hi = x.astype(_MXU)
    r1 = x - hi.astype(f32)
    mid = r1.astype(_MXU)
    lo = (r1 - mid.astype(f32)).astype(_MXU)
    dot = functools.partial(jnp.dot, preferred_element_type=f32)
    return dot(t, hi) + dot(t, mid) + dot(t, lo)


def _split3_right(x, t):
    hi = x.astype(_MXU)
    r1 = x - hi.astype(f32)
    mid = r1.astype(_MXU)
    lo = (r1 - mid.astype(f32)).astype(_MXU)
    dot = functools.partial(jnp.dot, preferred_element_type=f32)
    return dot(hi, t) + dot(mid, t) + dot(lo, t)


def _fox_cum_fwd(z_all, bias, *, name):
    s = z_all.shape[0]
    b = _attn_block(s)
    fcol = _AL["fox_f"] // LANES

    def body(f_ref, b_ref, cum_ref, cumt_ref, carry_ref):
        i = pl.program_id(0)

        @pl.when(i == 0)
        def _():
            carry_ref[...] = jnp.zeros_like(carry_ref)

        u = f_ref[...] + b_ref[...]
        lf = jnp.minimum(u, 0.0) - jnp.log1p(jnp.exp(-jnp.abs(u)))
        cum = _split3_left(_tri(b, "row_ge"), lf) + carry_ref[...]
        cum_ref[...] = cum
        cumt_ref[...] = cum.T[0:8, :]
        carry_ref[...] = cum_ref[b - 1:b, :]

    return pl.pallas_call(
        body, out_shape=(_SDS((s, LANES), f32), _SDS((8, s), f32)), grid=(s // b,),
        in_specs=[pl.BlockSpec((b, LANES), lambda i: (i, fcol)), pl.BlockSpec((1, LANES), lambda i: (0, 0))],
        out_specs=(pl.BlockSpec((b, LANES), lambda i: (i, 0)), pl.BlockSpec((8, b), lambda i: (0, i))),
        scratch_shapes=[pltpu.VMEM((1, LANES), f32)], compiler_params=_cp(("arbitrary",)), name=name,
    )(z_all, bias)


def _fox_cum_bwd(z_all, bias, dcum_t, *, name):
    s = z_all.shape[0]
    b = _attn_block(s)
    nb = s // b
    fcol = _AL["fox_f"] // LANES

    def body(f_ref, b_ref, dc_ref, df_ref, db_ref, carry_ref):
        i = pl.program_id(0)

        @pl.when(i == 0)
        def _():
            carry_ref[...] = jnp.zeros_like(carry_ref)
            db_ref[...] = jnp.zeros_like(db_ref)

        dc = dc_ref[...]
        rev = _split3_right(dc, _tri(b, "row_ge")) + carry_ref[...]
        carry_ref[...] = carry_ref[...] + jnp.sum(dc, axis=1, keepdims=True)
        dlf = jnp.concatenate([rev, jnp.zeros((LANES - 8, b), f32)], axis=0).T
        u = f_ref[...] + b_ref[...]
        df = jnp.where(_iota((b, LANES), 1) < N_HEADS, dlf * (1.0 - _sigmoid(u)), 0.0)
        df_ref[...] = df
        db_ref[...] += jnp.sum(df, axis=0, keepdims=True)

    return pl.pallas_call(
        body, out_shape=(_SDS((s, LANES), f32), _SDS((1, LANES), f32)), grid=(nb,),
        in_specs=[pl.BlockSpec((b, LANES), lambda i: (nb - 1 - i, fcol)), pl.BlockSpec((1, LANES), lambda i: (0, 0)),
                  pl.BlockSpec((8, b), lambda i: (0, nb - 1 - i))],
        out_specs=(pl.BlockSpec((b, LANES), lambda i: (nb - 1 - i, 0)), pl.BlockSpec((1, LANES), lambda i: (0, 0))),
        scratch_shapes=[pltpu.VMEM((8, 1), f32)], compiler_params=_cp(("arbitrary",)), name=name,
    )(z_all, bias, dcum_t)


MLA_QW = 2 * LANES


def _rms_rows(x):
    r = lax.rsqrt(jnp.mean(x * x, axis=-1, keepdims=True) + RMS_EPS)
    return x * r, r


def _mla_prep_fwd(z_all, gq, gkv, wuq, wk, wv, tables, *, name):
    s = z_all.shape[0]
    rb = _row_block(s)
    half = MLA_ROPE // 2

    def body(cq_ref, ckv_ref, kr_ref, gq_ref, gkv_ref, wuq_ref, wk_ref, wv_ref, cos_ref, sa_ref, sb_ref,
             q_ref, k_ref, v_ref):
        cos, sa, sb = cos_ref[...], sa_ref[...], sb_ref[...]
        xh, _ = _rms_rows(cq_ref[...])
        qp = _mm(xh * gq_ref[...], wuq_ref[...])
        kh, _ = _rms_rows(ckv_ref[...])
        nkv = kh * gkv_ref[...]
        kn = _mm(nkv, wk_ref[...])
        v_ref[...] = _mm(nkv, wv_ref[...])
        kr = _rope(kr_ref[...], cos, sa, sb, half)
        for h in range(N_HEADS):
            lo, mid, hi = h * MLA_QW, h * MLA_QW + LANES, (h + 1) * MLA_QW
            q_ref[:, lo:mid] = qp[:, lo:mid]
            q_ref[:, mid:hi] = _rope(qp[:, mid:hi], cos, sa, sb, half)
            k_ref[:, lo:mid] = kn[:, h * LANES:(h + 1) * LANES]
            k_ref[:, mid:hi] = kr

    row = lambda w, cb: pl.BlockSpec((rb, w), lambda i: (i, cb))
    whole = lambda a: pl.BlockSpec(a.shape, lambda i: (0,) * a.ndim)
    return pl.pallas_call(
        body, out_shape=(_SDS((s, N_HEADS * MLA_QW), f32), _SDS((s, N_HEADS * MLA_QW), f32), _SDS((s, GROUP), f32)),
        grid=(s // rb,),
        in_specs=[row(MLA_Q_RANK, _AL["mla_cq"] // MLA_Q_RANK), row(LANES, _AL["mla_ckv"] // LANES),
                  row(LANES, _AL["mla_k_rope"] // LANES), whole(gq), whole(gkv), whole(wuq), whole(wk), whole(wv),
                  row(LANES, 0), row(LANES, 0), row(LANES, 0)],
        out_specs=(row(N_HEADS * MLA_QW, 0), row(N_HEADS * MLA_QW, 0), row(GROUP, 0)),
        compiler_params=_cp(("parallel",)), name=name,
    )(z_all, z_all, z_all, gq, gkv, wuq, wk, wv, *tables)


def _mla_prep_bwd(z_all, gq, gkv, wuq, wk, wv, tables, dq_cat, dk_cat, dv, *, name):
    s = z_all.shape[0]
    rb = _row_block(s)
    half = MLA_ROPE // 2

    def body(cq_ref, ckv_ref, gq_ref, gkv_ref, wuq_ref, wk_ref, wv_ref, cos_ref, sa_ref, sb_ref, dq_ref, dk_ref,
             dv_ref, dcq_ref, dckv_ref, dkr_ref, dwuq_ref, dwk_ref, dwv_ref, dgq_ref, dgkv_ref):
        i = pl.program_id(0)

        @pl.when(i == 0)
        def _():
            for r in (dwuq_ref, dwk_ref, dwv_ref, dgq_ref, dgkv_ref):
                r[...] = jnp.zeros_like(r)

        cos, sa, sb = cos_ref[...], sa_ref[...], sb_ref[...]
        parts, knp = [], []
        dkr = jnp.zeros((rb, LANES), f32)
        for h in range(N_HEADS):
            lo, mid, hi = h * MLA_QW, h * MLA_QW + LANES, (h + 1) * MLA_QW
            parts += [dq_ref[:, lo:mid], _rope(dq_ref[:, mid:hi], cos, sa, sb, half, transpose=True)]
            knp.append(dk_ref[:, lo:mid])
            dkr = dkr + _rope(dk_ref[:, mid:hi], cos, sa, sb, half, transpose=True)
        dkr_ref[...] = dkr
        dqp = jnp.concatenate(parts, axis=1)
        dkn = jnp.concatenate(knp, axis=1)
        dvv = dv_ref[...]

        def norm_bwd(x_ref, g_ref, w_pairs, dx_ref, dg_ref):
            xh, r = _rms_rows(x_ref[...])
            nx = xh * g_ref[...]
            dn = jnp.zeros_like(xh)
            for w_ref, dw_ref, dy in w_pairs:
                dw_ref[...] += _mm_tn(nx, dy)
                dn = dn + _mm_nt(dy, w_ref[...])
            dxh = dn * g_ref[...]
            dx_ref[...] = r * (dxh - xh * jnp.mean(dxh * xh, axis=-1, keepdims=True))
            dg_ref[...] += jnp.sum(dn * xh, axis=0, keepdims=True)

        norm_bwd(cq_ref, gq_ref, [(wuq_ref, dwuq_ref, dqp)], dcq_ref, dgq_ref)
        norm_bwd(ckv_ref, gkv_ref, [(wk_ref, dwk_ref, dkn), (wv_ref, dwv_ref, dvv)], dckv_ref, dgkv_ref)

    row = lambda w, cb: pl.BlockSpec((rb, w), lambda i: (i, cb))
    whole = lambda a: pl.BlockSpec(a.shape, lambda i: (0,) * a.ndim)
    return pl.pallas_call(
        body,
        out_shape=(_SDS((s, MLA_Q_RANK), f32), _SDS((s, LANES), f32), _SDS((s, LANES), f32), _SDS(wuq.shape, f32),
                   _SDS(wk.shape, f32), _SDS(wv.shape, f32), _SDS(gq.shape, f32), _SDS(gkv.shape, f32)),
        grid=(s // rb,),
        in_specs=[row(MLA_Q_RANK, _AL["mla_cq"] // MLA_Q_RANK), row(LANES, _AL["mla_ckv"] // LANES), whole(gq),
                  whole(gkv), whole(wuq), whole(wk), whole(wv), row(LANES, 0), row(LANES, 0), row(LANES, 0),
                  row(N_HEADS * MLA_QW, 0), row(N_HEADS * MLA_QW, 0), row(GROUP, 0)],
        out_specs=(row(MLA_Q_RANK, 0), row(LANES, 0), row(LANES, 0), whole(wuq), whole(wk), whole(wv), whole(gq),
                   whole(gkv)),
        compiler_params=_cp(("arbitrary",)), name=name,
    )(z_all, z_all, gq, gkv, wuq, wk, wv, *tables, dq_cat, dk_cat, dv)


def _silu_grad(x):
    sg = _sigmoid(x)
    return sg * (1.0 + x * (1.0 - sg))


def _nsa_cmp_fwd(ra, rb_, pos, w1, w2, tables, *, name):
    nr = ra.shape[1]
    hw = ra.shape[2]

    def body(ra_ref, rb_ref, pos_ref, w1_ref, w2_ref, cos_ref, sa_ref, sb_ref, out_ref, hp_ref):
        for k in range(2):
            xa = ra_ref[k] + pos_ref[k, :, 0:hw]
            xb = rb_ref[k] + pos_ref[k, :, hw:2 * hw]
            hp = _mm(xa, w1_ref[k, 0:hw, :]) + _mm(xb, w1_ref[k, hw:2 * hw, :])
            hp_ref[k] = hp
            out = _mm(hp * _sigmoid(hp), w2_ref[k])
            if k == 0:
                out = _rope(out, cos_ref[...], sa_ref[...], sb_ref[...], HEAD_DIM // 2)
            out_ref[k] = out

    return pl.pallas_call(body, out_shape=(_SDS((2, nr, HEAD_DIM), f32), _SDS((2, nr, HEAD_DIM), f32)),
                          compiler_params=_cp(), name=name)(ra, rb_, pos, w1, w2, *tables)


def _nsa_cmp_bwd(ra, rb_, pos, w1, w2, tables, hp, dout, *, name):
    nr = ra.shape[1]
    hw = ra.shape[2]

    def body(ra_ref, rb_ref, pos_ref, w1_ref, w2_ref, cos_ref, sa_ref, sb_ref, hp_ref, do_ref,
             dxa_ref, dxb_ref, dw1_ref, dw2_ref):
        for k in range(2):
            d_out = do_ref[k]
            if k == 0:
                d_out = _rope(d_out, cos_ref[...], sa_ref[...], sb_ref[...], HEAD_DIM // 2, transpose=True)
            hpv = hp_ref[k]
            dw2_ref[k] = _mm_tn(hpv * _sigmoid(hpv), d_out)
            dhp = _mm_nt(d_out, w2_ref[k]) * _silu_grad(hpv)
            xa = ra_ref[k] + pos_ref[k, :, 0:hw]
            xb = rb_ref[k] + pos_ref[k, :, hw:2 * hw]
            dw1_ref[k, 0:hw, :] = _mm_tn(xa, dhp)
            dw1_ref[k, hw:2 * hw, :] = _mm_tn(xb, dhp)
            dxa_ref[k] = _mm_nt(dhp, w1_ref[k, 0:hw, :])
            dxb_ref[k] = _mm_nt(dhp, w1_ref[k, hw:2 * hw, :])

    return pl.pallas_call(
        body, out_shape=(_SDS((2, nr, hw), f32), _SDS((2, nr, hw), f32), _SDS(w1.shape, f32), _SDS(w2.shape, f32)),
        compiler_params=_cp(), name=name)(ra, rb_, pos, w1, w2, *tables, hp, dout)


def _nsa_consts(s):
    b = _attn_block(s)
    nr = s // CMP_STRIDE
    n_cmp = (s - CMP_LEN) // CMP_STRIDE + 1
    n_sel = s // SEL_LEN
    cmp_start = np.arange(n_cmp) * CMP_STRIDE
    sel_start = np.arange(n_sel) * SEL_LEN
    overlap = np.clip(np.minimum(cmp_start[:, None] + CMP_LEN, sel_start[None, :] + SEL_LEN)
                      - np.maximum(cmp_start[:, None], sel_start[None, :]), 0, None)
    m2s = np.zeros((nr, LANES), np.float32)
    m2s[:n_cmp, :n_sel] = overlap / CMP_LEN
    e3 = np.zeros((s // b, LANES, b), np.float32)
    tok = np.arange(s)
    e3[tok // b, tok // SEL_LEN, tok % b] = 1.0
    return jnp.asarray(m2s, _MXU), jnp.asarray(e3, _MXU)


def _nsa_masks(i, b, d):
    qpos = i * b + _iota((b, b), 0)
    kpos = (i - d) * b + _iota((b, b), 1)
    return (kpos <= qpos) & (kpos > qpos - WINDOW)


def _nsa_fwd(qr, kvc, ksr, vs, kwr, vw, z_all, m2s, e3, *, name):
    s = qr.shape[0]
    b = _attn_block(s)
    nq = s // b
    nr = kvc.shape[1]
    n_sel = s // SEL_LEN
    top_n = min(SEL_TOPN, n_sel)
    nd = -(-WINDOW // b)
    scale = HEAD_DIM ** -0.5
    bcol = _AL["nsa_branch"] // LANES
    H = N_HEADS

    def body(q_ref, kvc_ref, ks_ref, vs_ref, kw_ref, vw_ref, br_ref, m2s_ref, e3_ref,
             o_ref, oc_ref, os_ref, ow_ref, st_ref, sel_ref, m_sc, l_sc, acc_sc):
        i = pl.program_id(0)
        lane = _iota((b, LANES), 1)
        hs = lambda h: slice(h * HEAD_DIM, (h + 1) * HEAD_DIM)

        cmp_mask = (CMP_STRIDE * _iota((b, nr), 1) + (CMP_LEN - 1)) <= (i * b + _iota((b, nr), 0))
        imp = jnp.zeros((b, LANES), f32)
        stats = jnp.zeros((b, LANES), f32)
        for h in range(H):
            zc = jnp.where(cmp_mask, _mm_nt(q_ref[:, hs(h)], kvc_ref[0]) * scale, NEG_INF)
            m = jnp.max(zc, axis=1, keepdims=True)
            p = jnp.where(cmp_mask, jnp.exp(zc - m), 0.0)
            l = jnp.sum(p, axis=1, keepdims=True)
            some = l > 0.0
            lsafe = jnp.where(some, l, 1.0)
            pc = p * jnp.where(some, 1.0 / lsafe, 0.0)
            oc_ref[:, hs(h)] = _mm(pc, kvc_ref[1])
            imp = imp + _mm(pc, m2s_ref[...])
            stats = jnp.where(lane == h, jnp.where(some, m + jnp.log(lsafe), 0.0), stats)

        cur = jnp.right_shift(i * b + _iota((b, LANES), 0), int(math.log2(SEL_LEN)))
        forced = (lane == 0) | (lane == cur) | (lane == cur - 1)
        score = jnp.where(lane <= cur, jnp.where(forced, FORCED_BONUS, imp), NEG_INF)
        score = jnp.where(lane < n_sel, score, -3e38)
        rank = jnp.zeros((b, LANES), f32)
        for j in range(n_sel):
            col = score[:, j:j + 1]
            rank = rank + jnp.where(col > score, 1.0, jnp.where(col == score, jnp.where(lane > j, 1.0, 0.0), 0.0))
        sel = jnp.where(lane < n_sel, jnp.where(rank < top_n, 1.0, 0.0), 0.0)
        sel_ref[...] = sel
        sel_b = sel.astype(_MXU)

        def reset():
            m_sc[...] = jnp.full(m_sc.shape, NEG_INF, f32)
            l_sc[...] = jnp.zeros_like(l_sc)
            acc_sc[...] = jnp.zeros_like(acc_sc)

        def update(h, z, mask, vch):
            zm = jnp.where(mask, z, NEG_INF)
            m_old = m_sc[h]
            m_new = jnp.maximum(m_old, jnp.max(zm, axis=1, keepdims=True))
            p = jnp.where(mask, jnp.exp(zm - m_new), 0.0)
            alpha = jnp.exp(m_old - m_new)
            l_sc[h] = alpha * l_sc[h] + jnp.sum(p, axis=1, keepdims=True)
            acc_sc[h] = alpha * acc_sc[h] + _mm(p, vch)
            m_sc[h] = m_new

        def finish(out_ref, branch, stats):
            for h in range(H):
                out_ref[:, hs(h)] = acc_sc[h] / l_sc[h]
                stats = jnp.where(lane == 4 * branch + h, m_sc[h] + jnp.log(l_sc[h]), stats)
            return stats

        def sel_chunk(c, diag):
            st = pl.multiple_of(c * b, b)
            mask = _mm(sel_b, e3_ref[c]) > 0.5
            if diag:
                mask = mask & _lower_mask(b, False)
            kch, vch = ks_ref[pl.ds(st, b), :], vs_ref[pl.ds(st, b), :]
            for h in range(H):
                update(h, _mm_nt(q_ref[:, hs(h)], kch) * scale, mask, vch)

        reset()

        def sel_loop(c, carry):
            sel_chunk(c, False)
            return carry

        lax.fori_loop(0, i, sel_loop, 0)
        sel_chunk(i, True)
        stats = finish(os_ref, 1, stats)

        reset()
        for d in range(nd, -1, -1):
            @pl.when(i >= d)
            def _():
                st = pl.multiple_of((i - d) * b, b)
                mask = _nsa_masks(i, b, d)
                kch, vch = kw_ref[pl.ds(st, b), :], vw_ref[pl.ds(st, b), :]
                for h in range(H):
                    update(h, _mm_nt(q_ref[:, hs(h)], kch) * scale, mask, vch)
        stats = finish(ow_ref, 2, stats)
        st_ref[...] = stats

        g = _sigmoid(br_ref[...])
        for h in range(H):
            o_ref[:, hs(h)] = (g[:, 3 * h:3 * h + 1] * oc_ref[:, hs(h)] + g[:, 3 * h + 1:3 * h + 2] * os_ref[:, hs(h)]
                               + g[:, 3 * h + 2:3 * h + 3] * ow_ref[:, hs(h)])

    blk = lambda w: pl.BlockSpec((b, w), lambda i: (i, 0))
    whole = lambda a: pl.BlockSpec(a.shape, lambda i: (0,) * a.ndim)
    return pl.pallas_call(
        body, out_shape=tuple(_SDS((s, GROUP), f32) for _ in range(4)) + (_SDS((s, LANES), f32), _SDS((s, LANES), f32)),
        grid=(nq,),
        in_specs=[blk(GROUP), whole(kvc), whole(ksr), whole(vs), whole(kwr), whole(vw),
                  pl.BlockSpec((b, LANES), lambda i: (i, bcol)), whole(m2s), whole(e3)],
        out_specs=(blk(GROUP),) * 4 + (blk(LANES), blk(LANES)),
        scratch_shapes=[pltpu.VMEM((H, b, 1), f32), pltpu.VMEM((H, b, 1), f32), pltpu.VMEM((H, b, HEAD_DIM), f32)],
        compiler_params=_cp(("parallel",)), name=name,
    )(qr, kvc, ksr, vs, kwr, vw, z_all, m2s, e3)


def _nsa_bwd(do, qr, kvc, ksr, vs, kwr, vw, z_all, oc, os_, ow, stats, sel, e3, *, name):
    s = qr.shape[0]
    b = _attn_block(s)
    nq = s // b
    nr = kvc.shape[1]
    nd = -(-WINDOW // b)
    scale = HEAD_DIM ** -0.5
    bcol = _AL["nsa_branch"] // LANES
    H = N_HEADS

    def body(do_ref, q_ref, kvc_ref, ks_ref, vs_ref, kw_ref, vw_ref, br_ref, oc_ref, os_ref, ow_ref, st_ref, sel_ref,
             e3_ref, dq_ref, dbr_ref, dkvc_ref, dks_ref, dvs_ref, dkw_ref, dvw_ref, dob_sc, delta_sc, dq_sc):
        i = pl.program_id(0)

        @pl.when(i == 0)
        def _():
            for r in (dkvc_ref, dks_ref, dvs_ref, dkw_ref, dvw_ref):
                r[...] = jnp.zeros_like(r)

        lane = _iota((b, LANES), 1)
        hs = lambda h: slice(h * HEAD_DIM, (h + 1) * HEAD_DIM)
        g = _sigmoid(br_ref[...])
        stats = st_ref[...]
        dbr = jnp.zeros((b, LANES), f32)
        outs = (oc_ref, os_ref, ow_ref)
        for h in range(H):
            doh = do_ref[:, hs(h)]
            for j in range(3):
                gj = g[:, 3 * h + j:3 * h + j + 1]
                dgj = jnp.sum(doh * outs[j][:, hs(h)], axis=1, keepdims=True)
                dbr = jnp.where(lane == 3 * h + j, dgj * gj * (1.0 - gj), dbr)
                dob_sc[j, :, hs(h)] = gj * doh
                delta_sc[j, h] = gj * dgj
        dbr_ref[...] = dbr
        dq_sc[...] = jnp.zeros_like(dq_sc)

        def branch(j, h, z, mask, kch, vch):
            qh = q_ref[:, hs(h)]
            p = jnp.where(mask, jnp.exp(jnp.where(mask, z, NEG_INF) - stats[:, 4 * j + h:4 * j + h + 1]), 0.0)
            dob = dob_sc[j, :, hs(h)]
            ds = p * (_mm_nt(dob, vch) - delta_sc[j, h])
            dq_sc[:, hs(h)] += _mm(ds, kch) * scale
            return _mm_tn(ds, qh) * scale, _mm_tn(p, dob)

        cmp_mask = (CMP_STRIDE * _iota((b, nr), 1) + (CMP_LEN - 1)) <= (i * b + _iota((b, nr), 0))
        kc, vc = kvc_ref[0], kvc_ref[1]
        for h in range(H):
            dk, dv = branch(0, h, _mm_nt(q_ref[:, hs(h)], kc) * scale, cmp_mask, kc, vc)
            dkvc_ref[0] += dk
            dkvc_ref[1] += dv

        sel_b = sel_ref[...].astype(_MXU)

        def chunk(j, c, mask, k_ref, v_ref, dk_ref, dv_ref):
            st = pl.multiple_of(c * b, b)
            kch, vch = k_ref[pl.ds(st, b), :], v_ref[pl.ds(st, b), :]
            dk = jnp.zeros((b, HEAD_DIM), f32)
            dv = jnp.zeros((b, HEAD_DIM), f32)
            for h in range(H):
                dkh, dvh = branch(j, h, _mm_nt(q_ref[:, hs(h)], kch) * scale, mask, kch, vch)
                dk, dv = dk + dkh, dv + dvh
            dk_ref[pl.ds(st, b), :] += dk
            dv_ref[pl.ds(st, b), :] += dv

        def sel_chunk(c, diag):
            mask = _mm(sel_b, e3_ref[c]) > 0.5
            if diag:
                mask = mask & _lower_mask(b, False)
            chunk(1, c, mask, ks_ref, vs_ref, dks_ref, dvs_ref)

        def sel_loop(c, carry):
            sel_chunk(c, False)
            return carry

        lax.fori_loop(0, i, sel_loop, 0)
        sel_chunk(i, True)

        for d in range(nd, -1, -1):
            @pl.when(i >= d)
            def _():
                chunk(2, i - d, _nsa_masks(i, b, d), kw_ref, vw_ref, dkw_ref, dvw_ref)

        dq_ref[...] = dq_sc[...]

    blk = lambda w: pl.BlockSpec((b, w), lambda i: (i, 0))
    whole = lambda a: pl.BlockSpec(a.shape, lambda i: (0,) * a.ndim)
    stream = _SDS((s, HEAD_DIM), f32)
    return pl.pallas_call(
        body, out_shape=(_SDS((s, GROUP), f32), _SDS((s, LANES), f32), _SDS(kvc.shape, f32), stream, stream, stream,
                         stream),
        grid=(nq,),
        in_specs=[blk(GROUP), blk(GROUP), whole(kvc), whole(ksr), whole(vs), whole(kwr), whole(vw),
                  pl.BlockSpec((b, LANES), lambda i: (i, bcol)), blk(GROUP), blk(GROUP), blk(GROUP), blk(LANES),
                  blk(LANES), whole(e3)],
        out_specs=(blk(GROUP), blk(LANES), whole(kvc), whole(ksr), whole(vs), whole(kwr), whole(vw)),
        scratch_shapes=[pltpu.VMEM((3, b, GROUP), f32), pltpu.VMEM((3, H, b, 1), f32), pltpu.VMEM((b, GROUP), f32)],
        compiler_params=_cp(("arbitrary",)), name=name,
    )(do, qr, kvc, ksr, vs, kwr, vw, z_all, oc, os_, ow, stats, sel, e3)


def _seg(a, name, width=None):
    return a[:, _AL[name]:_AL[name] + (width or _WID[name])]


def _cmp_rows(tok):
    s = tok.shape[0]
    r = tok.reshape(s // CMP_STRIDE, CMP_STRIDE * HEAD_DIM)
    return r, jnp.concatenate([r[1:], jnp.zeros((1, r.shape[1]), r.dtype)], axis=0)


def _cmp_unrows(dxa, dxb):
    s = dxa.shape[0] * CMP_STRIDE
    return (dxa + jnp.concatenate([jnp.zeros((1, dxa.shape[1]), dxa.dtype), dxb[:-1]], axis=0)).reshape(s, HEAD_DIM)


_GATES = ("sb_gate", "nsa_gate", "fox_gate", "mla_gate")


def _layer_fwd(x, p, c, tag):
    s = x.shape[0]
    b = _attn_block(s)
    h = _rms_fwd(x, p["pre_g"], out_dtype=_MXU, name=f"prenorm_{tag}")
    z = _matmul(h, p["w_in"], "nn", bias=p["b_in"], name=f"inproj_{tag}")
    o_sb = _sb_fwd(z, name=f"sb_fwd_{tag}")

    qr, ksr, kwr = _rope_call([(z, GROUP, _AL["nsa_q"] // GROUP), (z, LANES, _AL["nsa_k_sel"] // LANES),
                               (z, LANES, _AL["nsa_k_win"] // LANES)], c["tabs128"], HEAD_DIM // 2, False,
                              name=f"nsa_rope_{tag}")
    (rak, rbk), (rav, rbv) = _cmp_rows(_seg(z, "nsa_k_cmp")), _cmp_rows(_seg(z, "nsa_v_cmp"))
    ra, rb_ = jnp.stack([rak, rav]), jnp.stack([rbk, rbv])
    kvc, hp = _nsa_cmp_fwd(ra, rb_, p["cmp_pos"], p["cmp_w1"], p["cmp_w2"], c["tabs_cmp"], name=f"nsa_cmp_{tag}")
    vs, vw = _seg(z, "nsa_v_sel"), _seg(z, "nsa_v_win")
    o_nsa, oc, os_, ow, stats, sel = _nsa_fwd(qr, kvc, ksr, vs, kwr, vw, z, c["m2s"], c["e3"], name=f"nsa_fwd_{tag}")

    cum, cum_t8 = _fox_cum_fwd(z, p["fox_bias"], name=f"fox_cum_{tag}")
    cum_t = cum_t8.reshape(8, s // b, 1, b)
    fcols = tuple(_AL[n] // HEAD_DIM for n in ("fox_q", "fox_k", "fox_v"))
    o_fox, lse_fox = _attn_fwd(z, z, z, *fcols, HEAD_DIM, cum, cum_t, scale=HEAD_DIM ** -0.5, name=f"fox_fwd_{tag}")

    qcat, kcat, vm = _mla_prep_fwd(z, p["gq"], p["gkv"], p["wuq"], p["wk"], p["wv"], c["tabs64"],
                                   name=f"mla_prep_{tag}")
    o_mla, lse_mla = _attn_fwd(qcat, kcat, vm, 0, 0, 0, MLA_QW, None, None, scale=(MLA_NOPE + MLA_ROPE) ** -0.5,
                               name=f"mla_fwd_{tag}")

    o_all = jnp.concatenate([o_sb, o_nsa, o_fox, o_mla], axis=1)
    gates = jnp.concatenate([_seg(z, n) for n in _GATES], axis=1)
    mix = _gate_fwd(o_all, gates, name=f"gate_{tag}")
    u = _matmul(mix, p["w_out"], "nn", name=f"outproj_{tag}")
    y = _postnorm_fwd(u, p["post_g"], x, name=f"postnorm_{tag}")
    saved = dict(x=x, h=h, z=z, qr=qr, ksr=ksr, kwr=kwr, ra=ra, rb=rb_, kvc=kvc, hp=hp, vs=vs, vw=vw, oc=oc, os=os_,
                 ow=ow, stats=stats, sel=sel, cum=cum, cum_t=cum_t, o_fox=o_fox, lse_fox=lse_fox, qcat=qcat, kcat=kcat,
                 vm=vm, o_mla=o_mla, lse_mla=lse_mla, o_all=o_all, gates=gates, mix=mix, u=u)
    return y, saved


def _layer_bwd(dy, sv, p, c, tag):
    z = sv["z"]
    s = z.shape[0]
    du, dg_post = _rms_bwd(dy, sv["u"], p["post_g"], name=f"postnorm_bwd_{tag}")
    dmix = _matmul(du, p["w_out"], "nt", name=f"outproj_dx_{tag}")
    dw_out = _matmul(sv["mix"], du, "tn", name=f"outproj_dw_{tag}")
    do_all, dgates = _gate_bwd(dmix, sv["o_all"], sv["gates"], name=f"gate_bwd_{tag}")
    do_sb, do_nsa, do_fox, do_mla = (do_all[:, k * GROUP:(k + 1) * GROUP] for k in range(4))
    dgate = [dgates[:, k * GROUP:(k + 1) * GROUP] for k in range(4)]

    sb_dq, sb_dk, sb_dv = _sb_bwd(z, do_sb, name=f"sb_bwd_{tag}")

    n_dq, n_dbr, n_dkvc, n_dks, n_dvs, n_dkw, n_dvw = _nsa_bwd(
        do_nsa, sv["qr"], sv["kvc"], sv["ksr"], sv["vs"], sv["kwr"], sv["vw"], z, sv["oc"], sv["os"], sv["ow"],
        sv["stats"], sv["sel"], c["e3"], name=f"nsa_bwd_{tag}")
    dxa, dxb, dw1, dw2 = _nsa_cmp_bwd(sv["ra"], sv["rb"], p["cmp_pos"], p["cmp_w1"], p["cmp_w2"], c["tabs_cmp"],
                                      sv["hp"], n_dkvc, name=f"nsa_cmp_bwd_{tag}")
    n_dq, n_dks, n_dkw = _rope_call([(n_dq, GROUP, 0), (n_dks, LANES, 0), (n_dkw, LANES, 0)], c["tabs128"],
                                    HEAD_DIM // 2, True, name=f"nsa_rope_bwd_{tag}")
    dpos = _colsum(jnp.concatenate([dxa[0], dxb[0], dxa[1], dxb[1]], axis=1), name=f"nsa_dpos_{tag}")
    flat = CMP_LEN * HEAD_DIM

    fcols = tuple(_AL[n] // HEAD_DIM for n in ("fox_q", "fox_k", "fox_v"))
    f_dq, f_dk, f_dv, f_dck = _attn_bwd(z, z, z, *fcols, HEAD_DIM, do_fox, sv["o_fox"], sv["lse_fox"], sv["cum"],
                                        sv["cum_t"], scale=HEAD_DIM ** -0.5, name=f"fox_bwd_{tag}")
    dcum_t = jnp.pad(f_dck.reshape(N_HEADS, s), ((0, 8 - N_HEADS), (0, 0)))
    f_df, f_dbias = _fox_cum_bwd(z, p["fox_bias"], dcum_t, name=f"fox_cum_bwd_{tag}")

    m_dq, m_dk, m_dv = _attn_bwd(sv["qcat"], sv["kcat"], sv["vm"], 0, 0, 0, MLA_QW, do_mla, sv["o_mla"], sv["lse_mla"],
                                 None, None, scale=(MLA_NOPE + MLA_ROPE) ** -0.5, name=f"mla_bwd_{tag}")
    m_dcq, m_dckv, m_dkr, m_dwuq, m_dwk, m_dwv, m_dgq, m_dgkv = _mla_prep_bwd(
        z, p["gq"], p["gkv"], p["wuq"], p["wk"], p["wv"], c["tabs64"], m_dq, m_dk, m_dv, name=f"mla_prep_bwd_{tag}")

    dz = jnp.concatenate([
        sb_dq, sb_dk, sb_dv, dgate[0],
        n_dq, _cmp_unrows(dxa[0], dxb[0]), _cmp_unrows(dxa[1], dxb[1]), n_dks, n_dvs, n_dkw, n_dvw, n_dbr, dgate[1],
        f_dq, f_dk, f_dv, f_df, dgate[2],
        m_dcq, m_dckv, m_dkr, dgate[3]], axis=1)
    dh = _matmul(dz, p["w_in"], "nt", name=f"inproj_dx_{tag}")
    dw_in_al = _matmul(sv["h"], dz, "tn", name=f"inproj_dw_{tag}")
    db_al = _colsum(dz, name=f"inproj_db_{tag}")
    dx, dg_pre = _rms_bwd(dh, sv["x"], p["pre_g"], res=dy, name=f"prenorm_bwd_{tag}")

    unalign = lambda a: jnp.concatenate([a[:, _AL[n]:_AL[n] + w] for n, w, _ in _SEGS], axis=1)
    qw = MLA_NOPE + MLA_ROPE
    grads = {
        "pre_norm_g": dg_pre[0], "post_norm_g": dg_post[0], "w_in": unalign(dw_in_al), "b_in": unalign(db_al)[0],
        "w_out": dw_out, "fox_forget_bias": f_dbias[0, :N_HEADS],
        "nsa_cmp_pos_k": dpos[0, :flat].reshape(CMP_LEN, HEAD_DIM), "nsa_cmp_w1_k": dw1[0], "nsa_cmp_w2_k": dw2[0],
        "nsa_cmp_pos_v": dpos[0, flat:].reshape(CMP_LEN, HEAD_DIM), "nsa_cmp_w1_v": dw1[1], "nsa_cmp_w2_v": dw2[1],
        "mla_q_norm_g": m_dgq[0],
        "mla_w_uq": jnp.concatenate([m_dwuq[:, MLA_QW * h:MLA_QW * h + qw] for h in range(N_HEADS)], axis=1),
        "mla_kv_norm_g": m_dgkv[0],
        "mla_w_ukv": jnp.concatenate(sum([[m_dwk[:, LANES * h:LANES * (h + 1)], m_dwv[:, LANES * h:LANES * (h + 1)]]
                                          for h in range(N_HEADS)], []), axis=1),
    }
    return dx, grads


def _layer_params(w, l):
    w_in = w["w_in"][l]
    zero = lambda n: jnp.zeros((w_in.shape[0], n), w_in.dtype)
    cols = []
    for n, wd, wa in _SEGS:
        cols.append(w_in[:, _ORIG[n]:_ORIG[n] + wd])
        if wa > wd:
            cols.append(zero(wa - wd))
    b_in = w["b_in"][l]
    bcols = []
    for n, wd, wa in _SEGS:
        bcols.append(b_in[_ORIG[n]:_ORIG[n] + wd])
        if wa > wd:
            bcols.append(jnp.zeros((wa - wd,), f32))
    qw = MLA_NOPE + MLA_ROPE
    w_uq, w_ukv = w["mla_w_uq"][l], w["mla_w_ukv"][l]
    uq = []
    for h in range(N_HEADS):
        uq += [w_uq[:, qw * h:qw * (h + 1)], jnp.zeros((w_uq.shape[0], MLA_QW - qw), w_uq.dtype)]
    kw_ = 2 * LANES
    flat = CMP_LEN * HEAD_DIM
    return dict(
        pre_g=w["pre_norm_g"][l].reshape(1, -1), post_g=w["post_norm_g"][l].reshape(1, -1),
        w_in=jnp.concatenate(cols, axis=1), b_in=jnp.concatenate(bcols).reshape(1, -1), w_out=w["w_out"][l],
        fox_bias=jnp.pad(w["fox_forget_bias"][l], (0, LANES - N_HEADS)).reshape(1, LANES),
        cmp_pos=jnp.stack([w["nsa_cmp_pos_k"][l].reshape(1, flat), w["nsa_cmp_pos_v"][l].reshape(1, flat)]),
        cmp_w1=jnp.stack([w["nsa_cmp_w1_k"][l], w["nsa_cmp_w1_v"][l]]),
        cmp_w2=jnp.stack([w["nsa_cmp_w2_k"][l], w["nsa_cmp_w2_v"][l]]),
        gq=w["mla_q_norm_g"][l].reshape(1, -1), gkv=w["mla_kv_norm_g"][l].reshape(1, -1),
        wuq=jnp.concatenate(uq, axis=1),
        wk=jnp.concatenate([w_ukv[:, kw_ * h:kw_ * h + LANES] for h in range(N_HEADS)], axis=1),
        wv=jnp.concatenate([w_ukv[:, kw_ * h + LANES:kw_ * (h + 1)] for h in range(N_HEADS)], axis=1),
    )


def _consts(s):
    pos = jnp.arange(s)
    m2s, e3 = _nsa_consts(s)
    return dict(tabs128=_rope_tables(pos, HEAD_DIM), tabs64=_rope_tables(pos, MLA_ROPE),
                tabs_cmp=_rope_tables(jnp.arange(s // CMP_STRIDE) * CMP_STRIDE + (CMP_LEN - 1), HEAD_DIM),
                m2s=m2s, e3=e3)


def _place():
    return lax.axis_index("x"), lax.axis_index("y"), lax.axis_index("c")


def _other_chips(x, y):
    return [(1 - x, y), (x, 1 - y), (1 - x, 1 - y)]


_SEMS3 = [pltpu.SemaphoreType.DMA((3,)), pltpu.SemaphoreType.DMA((3,)), pltpu.SemaphoreType.DMA]


def _gather_chips(a, *, name):
    def body(a_ref, out_ref, send_sems, recv_sems):
        x, y, c = _place()
        me = 2 * x + y
        sibling = (x, y, 1 - c)
        chips = _other_chips(x, y)

        def copy(k, src, dst, to):
            return pltpu.make_async_remote_copy(src, dst, send_sems.at[k], recv_sems.at[k], device_id=to,
                                                device_id_type=_MESH)

        first = [copy(k, a_ref.at[c], out_ref.at[me, c], (px, py, c)) for k, (px, py) in enumerate(chips)]
        for cp in first:
            cp.start()
        passed = [copy(3 + k, out_ref.at[2 * px + py, c], out_ref.at[2 * px + py, c], sibling)
                  for k, (px, py) in enumerate(chips)]
        for k, (px, py) in enumerate(chips):
            copy(k, a_ref.at[c], out_ref.at[2 * px + py, c], (px, py, c)).wait_recv()
            passed[k].start()
        for k, (px, py) in enumerate(chips):
            copy(3 + k, a_ref.at[c], out_ref.at[2 * px + py, 1 - c], sibling).wait_recv()
        for cp in first + passed:
            cp.wait_send()

    return pl.pallas_call(body, out_shape=_SDS((4,) + a.shape, a.dtype), in_specs=[_ANY], out_specs=_ANY,
                          scratch_shapes=[pltpu.SemaphoreType.DMA((6,)), pltpu.SemaphoreType.DMA((6,))],
                          name=name)(a)


def _alltoall_chips(g, *, name):
    def body(g_ref, out_ref, send_sems, recv_sems):
        x, y, c = _place()
        me = 2 * x + y
        sends = [pltpu.make_async_remote_copy(g_ref.at[2 * px + py], out_ref.at[me], send_sems.at[k], recv_sems.at[k],
                                              device_id=(px, py, c), device_id_type=_MESH)
                 for k, (px, py) in enumerate(_other_chips(x, y))]
        for cp in sends:
            cp.start()
        for k, (px, py) in enumerate(_other_chips(x, y)):
            pltpu.make_async_remote_copy(g_ref.at[me], out_ref.at[2 * px + py], send_sems.at[k], recv_sems.at[k],
                                         device_id=(px, py, c), device_id_type=_MESH).wait_recv()
        for cp in sends:
            cp.wait_send()

    return pl.pallas_call(body, out_shape=_SDS(g.shape, g.dtype), in_specs=[_ANY], out_specs=_ANY,
                          scratch_shapes=[pltpu.SemaphoreType.DMA((3,)), pltpu.SemaphoreType.DMA((3,))],
                          name=name)(g)


def _swap_other_half(g, *, name):
    p, n2, w = g.shape
    h = n2 // 2

    def body(g_ref, out_ref, send_sem, recv_sem):
        x, y, c = _place()
        theirs = g_ref.at[:, pl.ds(pl.multiple_of((1 - c) * h, 8), h), :]
        cp = pltpu.make_async_remote_copy(theirs, out_ref, send_sem, recv_sem, device_id=(x, y, 1 - c),
                                          device_id_type=_MESH)
        cp.start()
        cp.wait()

    return pl.pallas_call(body, out_shape=_SDS((p, h, w), g.dtype), in_specs=[_ANY], out_specs=_ANY,
                          scratch_shapes=[pltpu.SemaphoreType.DMA, pltpu.SemaphoreType.DMA], name=name)(g)


def _swap_sibling(f, *, name):
    def body(f_ref, out_ref, send_sem, recv_sem):
        x, y, c = _place()
        cp = pltpu.make_async_remote_copy(f_ref, out_ref, send_sem, recv_sem, device_id=(x, y, 1 - c),
                                          device_id_type=_MESH)
        cp.start()
        cp.wait()

    return pl.pallas_call(body, out_shape=_SDS(f.shape, f.dtype), in_specs=[_ANY], out_specs=_ANY,
                          scratch_shapes=[pltpu.SemaphoreType.DMA] * 2, name=name)(f)


def _gather_all(a, *, name):
    def body(a_ref, out_ref, send_sems, recv_sems, local_sem):
        x, y, c = _place()
        flip = lambda v, f: (1 - v) if f else v
        peers = [(flip(x, f & 4), flip(y, f & 2), flip(c, f & 1)) for f in range(1, 8)]
        me = 4 * x + 2 * y + c
        mine = pltpu.make_async_copy(a_ref, out_ref.at[me], local_sem)
        mine.start()
        sends = [pltpu.make_async_remote_copy(a_ref, out_ref.at[me], send_sems.at[k], recv_sems.at[k], device_id=peer,
                                              device_id_type=_MESH) for k, peer in enumerate(peers)]
        for cp in sends:
            cp.start()
        for k, (px, py, pc) in enumerate(peers):
            pltpu.make_async_remote_copy(a_ref, out_ref.at[4 * px + 2 * py + pc], send_sems.at[k], recv_sems.at[k],
                                         device_id=(px, py, pc), device_id_type=_MESH).wait_recv()
        for cp in sends:
            cp.wait_send()
        mine.wait()

    return pl.pallas_call(body, out_shape=_SDS((8,) + a.shape, a.dtype), in_specs=[_ANY], out_specs=_ANY,
                          scratch_shapes=[pltpu.SemaphoreType.DMA((7,)), pltpu.SemaphoreType.DMA((7,)),
                                          pltpu.SemaphoreType.DMA], name=name)(a)


def _add_my_half(g, r, *, name):
    p, n2, w = g.shape
    h = n2 // 2
    rb = _pick(h, (1024, 976, 512, 256, 128, 64, 32, 16))
    nb = h // rb

    def body(c_ref, g_ref, r_ref, o_ref):
        o_ref[...] = (g_ref[...] + r_ref[...]).astype(o_ref.dtype)

    blk = pl.BlockSpec((1, rb, w), lambda s, i, c_ref: (s, i, 0))
    grid_spec = pltpu.PrefetchScalarGridSpec(
        num_scalar_prefetch=1, grid=(p, nb),
        in_specs=[pl.BlockSpec((1, rb, w), lambda s, i, c_ref: (s, i + c_ref[0] * nb, 0)), blk], out_specs=blk)
    c = lax.axis_index("c").astype(jnp.int32).reshape(1)
    return pl.pallas_call(body, out_shape=_SDS((p, h, w), _WIRE), grid_spec=grid_spec,
                          compiler_params=_cp(("parallel", "parallel")), name=name)(c, g, r)


_WEIGHTS = ("pre_norm_g", "post_norm_g", "w_in", "b_in", "w_out", "fox_forget_bias", "nsa_cmp_pos_k", "nsa_cmp_w1_k",
            "nsa_cmp_w2_k", "nsa_cmp_pos_v", "nsa_cmp_w1_v", "nsa_cmp_w2_v", "mla_q_norm_g", "mla_w_uq",
            "mla_kv_norm_g", "mla_w_ukv")
_SHARD_AXIS = {"w_in": 2, "w_out": 1, "nsa_cmp_w1_k": 1, "nsa_cmp_w1_v": 1, "mla_w_uq": 2, "mla_w_ukv": 2}
_N_CHIPS = 4
_PACK_UNIT = 16 * LANES


def _pack(arrays, dtype):
    rows = []
    for a in arrays:
        v = a.astype(dtype).reshape(-1)
        pad = (-v.shape[0]) % _PACK_UNIT
        if pad:
            v = jnp.concatenate([v, jnp.zeros((pad,), dtype)])
        rows.append(v.reshape(-1, LANES))
    return jnp.concatenate(rows, axis=0)


def _unpack(flat, shapes):
    out, r = [], 0
    for shp in shapes:
        n = int(np.prod(shp))
        nr = -(-n // _PACK_UNIT) * (_PACK_UNIT // LANES)
        out.append(flat[r:r + nr].reshape(-1)[:n].reshape(shp))
        r += nr
    return out


def kernel(x, pre_norm_g, post_norm_g, w_in, b_in, w_out, fox_forget_bias, nsa_cmp_pos_k, nsa_cmp_w1_k, nsa_cmp_w2_k, nsa_cmp_pos_v, nsa_cmp_w1_v, nsa_cmp_w2_v, mla_q_norm_g, mla_w_uq, mla_kv_norm_g, mla_w_ukv, loss_target, m_pre_norm_g, m_post_norm_g, m_w_in, m_b_in, m_w_out, m_fox_forget_bias, m_nsa_cmp_pos_k, m_nsa_cmp_w1_k, m_nsa_cmp_w2_k, m_nsa_cmp_pos_v, m_nsa_cmp_w1_v, m_nsa_cmp_w2_v, m_mla_q_norm_g, m_mla_w_uq, m_mla_kv_norm_g, m_mla_w_ukv, v_pre_norm_g, v_post_norm_g, v_w_in, v_b_in, v_w_out, v_fox_forget_bias, v_nsa_cmp_pos_k, v_nsa_cmp_w1_k, v_nsa_cmp_w2_k, v_nsa_cmp_pos_v, v_nsa_cmp_w1_v, v_nsa_cmp_w2_v, v_mla_q_norm_g, v_mla_w_uq, v_mla_kv_norm_g, v_mla_w_ukv):
    given = dict(locals())
    local = {n: given[n] for n in _WEIGHTS}
    depth = pre_norm_g.shape[0]
    xs, target = x[0], loss_target[0]
    s = xs.shape[0]
    sharded = [n for n in _WEIGHTS if n in _SHARD_AXIS]
    small = [n for n in _WEIGHTS if n not in _SHARD_AXIS]

    shard_shapes = [local[n].shape for n in sharded]
    chip = 2 * lax.axis_index("x") + lax.axis_index("y")
    core = lax.axis_index("c")
    mine = _pack([local[n] for n in sharded], _MXU)
    everyone = _gather_chips(mine.reshape(2, -1, LANES), name="gather_weights").reshape((_N_CHIPS,) + mine.shape)
    everyone = lax.dynamic_update_slice_in_dim(everyone, mine[None], chip, axis=0)
    per_chip = [_unpack(everyone[k], shard_shapes) for k in range(_N_CHIPS)]
    full = dict(local)
    for j, n in enumerate(sharded):
        full[n] = jnp.concatenate([per_chip[k][j] for k in range(_N_CHIPS)], axis=_SHARD_AXIS[n])

    consts = _consts(s)
    params = [_layer_params(full, l) for l in range(depth)]
    act, saved = xs, []
    for l in range(depth):
        act, sv = _layer_fwd(act, params[l], consts, f"l{l}")
        saved.append(sv)
    dy, loss_parts = _loss_head(act, target, name="loss_head")
    layer_grads = [None] * depth
    for l in reversed(range(depth)):
        dy, layer_grads[l] = _layer_bwd(dy, saved[l], params[l], consts, f"l{l}")
    grad_x = dy[None]
    grads = {n: jnp.stack([layer_grads[l][n] for l in range(depth)]) for n in _WEIGHTS}

    def chip_slice(n, k):
        a, ax = grads[n], _SHARD_AXIS[n]
        w = a.shape[ax] // _N_CHIPS
        return lax.slice_in_dim(a, k * w, (k + 1) * w, axis=ax)

    g_all = jnp.stack([_pack([chip_slice(n, k) for n in sharded], f32) for k in range(_N_CHIPS)])
    from_sibling = _swap_other_half(g_all, name="reduce_pair")
    pair_sum = _add_my_half(g_all, from_sibling, name="reduce_pair_add")
    from_chips = _alltoall_chips(pair_sum, name="reduce_chips")
    from_chips = lax.dynamic_update_slice_in_dim(
        from_chips, lax.dynamic_index_in_dim(pair_sum, chip, axis=0, keepdims=True), chip, axis=0)
    my_half = _sum_slots(from_chips, name="reduce_chips_add")
    their_half = _swap_sibling(my_half, name="reduce_share")
    first = core == 0
    both = jnp.concatenate([jnp.where(first, my_half, their_half), jnp.where(first, their_half, my_half)], axis=0)
    g_shard = _unpack(both, shard_shapes)
    summed = dict(zip(sharded, g_shard))

    loss_row = jnp.concatenate([jnp.sum(loss_parts).reshape(1), jnp.zeros((LANES - 1,), f32)])
    small_shapes = [(LANES,)] + [grads[n].shape for n in small]
    contrib = _pack([loss_row] + [grads[n] for n in small], f32)
    pad_rows = (-contrib.shape[0]) % 8
    if pad_rows:
        contrib = jnp.concatenate([contrib, jnp.zeros((pad_rows, LANES), f32)], axis=0)
    total = _unpack(_sum_slots(_gather_all(contrib, name="gather_small"), name="sum_small"), small_shapes)
    loss = total[0][0]
    summed.update(zip(small, total[1:]))

    deltas, new_m, new_v = {}, {}, {}
    for n in _WEIGHTS:
        deltas[n], new_m[n], new_v[n] = _adamw(local[n], summed[n], given["m_" + n], given["v_" + n], name=f"adamw_{n}")
    return (loss, grad_x, *[summed[n] for n in _WEIGHTS], *[deltas[n] for n in _WEIGHTS],
            *[new_m[n] for n in _WEIGHTS], *[new_v[n] for n in _WEIGHTS])
```

```python
import functools
import math

import numpy as np
import jax
import jax.numpy as jnp
from jax import lax
from jax.experimental import pallas as pl
from jax.experimental.pallas import tpu as pltpu

f32 = jnp.float32
bf16 = jnp.bfloat16
_MXU = jnp.bfloat16
_WIRE = jnp.bfloat16
_SDS = jax.ShapeDtypeStruct
_ANY = pl.BlockSpec(memory_space=pl.ANY)
_MESH = pl.DeviceIdType.MESH

D_MODEL = 2048
N_HEADS = 4
HEAD_DIM = 128
GROUP = 512
RMS_EPS = 1e-6
NEG_INF = -1e30
ROPE_THETA = 10000.0
CMP_LEN, CMP_STRIDE, SEL_LEN, SEL_TOPN, WINDOW = 32, 16, 64, 16, 512
FORCED_BONUS = 1e6
MLA_Q_RANK, MLA_KV_RANK, MLA_NOPE, MLA_ROPE = 384, 128, 128, 64
ADAM_LR, ADAM_B1, ADAM_B2, ADAM_EPS, ADAM_WD, ADAM_STEP = 0.001, 0.9, 0.999, 1e-08, 0.01, 10
LANES = 128
VMEM_LIMIT = 48 * 1024 * 1024

_SEGS = (
    ("sb_q", 512), ("sb_k", 512), ("sb_v", 512), ("sb_gate", 512), ("nsa_q", 512), ("nsa_k_cmp", 128),
    ("nsa_v_cmp", 128), ("nsa_k_sel", 128), ("nsa_v_sel", 128), ("nsa_k_win", 128), ("nsa_v_win", 128),
    ("nsa_branch", 12), ("nsa_gate", 512), ("fox_q", 512), ("fox_k", 512), ("fox_v", 512), ("fox_f", 4),
    ("fox_gate", 512), ("mla_cq", 384), ("mla_ckv", 128), ("mla_k_rope", 64), ("mla_gate", 512),
)
_ORIG, _WID = {}, {}
_o = 0
for _n, _w in _SEGS:
    _ORIG[_n], _WID[_n] = _o, _w
    _o += _w
IN_WIDTH = _o
N_CHIPS = 4
CHIP_COLS = IN_WIDTH // N_CHIPS
GROUP_W = 2048
ZW = N_CHIPS * GROUP_W
_GROUPS = (
    (("sb_q", 0, 512, 0), ("sb_k", 0, 512, 512), ("sb_v", 0, 512, 1024), ("sb_gate", 0, 212, 1536)),
    (("nsa_q", 0, 512, 0), ("nsa_k_cmp", 0, 128, 512), ("nsa_v_cmp", 0, 128, 640), ("nsa_k_sel", 0, 128, 768),
     ("nsa_v_sel", 0, 128, 896), ("nsa_k_win", 0, 128, 1024), ("nsa_v_win", 0, 128, 1152), ("nsa_branch", 0, 12, 1280),
     ("sb_gate", 212, 512, 1408), ("nsa_gate", 0, 156, 1712)),
    (("fox_q", 0, 512, 0), ("fox_k", 0, 512, 512), ("fox_v", 0, 368, 1024), ("nsa_gate", 156, 512, 1408)),
    (("mla_cq", 0, 384, 0), ("mla_ckv", 0, 128, 384), ("mla_k_rope", 0, 64, 512), ("fox_f", 0, 4, 640),
     ("fox_v", 368, 512, 768), ("fox_gate", 0, 512, 1024), ("mla_gate", 0, 512, 1536)),
)
_PIECES = {n: [] for n, _ in _SEGS}
for _s, _grp in enumerate(_GROUPS):
    _cover = sorted((_ORIG[n] + lo, _ORIG[n] + hi) for n, lo, hi, _ in _grp)
    assert _cover[0][0] == _s * CHIP_COLS and _cover[-1][1] == (_s + 1) * CHIP_COLS
    assert all(a[1] == b[0] for a, b in zip(_cover, _cover[1:]))
    _ends = sorted((off, off + hi - lo) for _, lo, hi, off in _grp)
    assert all(a[1] <= b[0] for a, b in zip(_ends, _ends[1:])) and _ends[-1][1] <= GROUP_W
    assert _ends[0][0] == 0 and all(e[0] % 16 == 0 for e in _ends)
    for _n, _lo, _hi, _off in _grp:
        _PIECES[_n].append((_s * GROUP_W + _off, _lo, _hi))
_AL = {n: p[0][0] for n, p in _PIECES.items() if len(p) == 1}


def _cp(sem=None):
    return pltpu.CompilerParams(dimension_semantics=sem, vmem_limit_bytes=VMEM_LIMIT)


def _mm(a, b):
    return jnp.dot(a.astype(_MXU), b.astype(_MXU), preferred_element_type=f32)


def _mm_nt(a, b):
    return lax.dot_general(a.astype(_MXU), b.astype(_MXU), (((1,), (1,)), ((), ())), preferred_element_type=f32)


def _mm_tn(a, b):
    return lax.dot_general(a.astype(_MXU), b.astype(_MXU), (((0,), (0,)), ((), ())), preferred_element_type=f32)


def _mm_split(x, t):
    hi = x.astype(_MXU)
    lo = (x - hi.astype(f32)).astype(_MXU)
    return jnp.dot(hi, t, preferred_element_type=f32) + jnp.dot(lo, t, preferred_element_type=f32)


def _sigmoid(x):
    return 1.0 / (1.0 + jnp.exp(-x))


def _iota(shape, dim):
    return lax.broadcasted_iota(jnp.int32, shape, dim)


def _pick(n, prefs):
    for p in prefs:
        if n % p == 0:
            return p
    return n


def _matmul(a, b, mode, *, bias=None, out_dtype=f32, layer=None, name):
    b_shape = b.shape if layer is None else (b.shape[0] * b.shape[2], b.shape[3])
    if mode == "nn":
        (M, K), (K2, N) = a.shape, b_shape
    elif mode == "nt":
        (M, K), (N, K2) = a.shape, b_shape
    else:
        (K, M), (K2, N) = a.shape, b_shape
    assert K == K2
    tm = _pick(M, (512, 384, 256, 128))
    tn = _pick(N, (512, 384, 256, 128))
    tk = K if K <= 2048 else _pick(K, (2048, 2432, 1024, 512))
    nk = K // tk
    a_spec = {"nn": pl.BlockSpec((tm, tk), lambda i, j, k: (i, k)),
              "nt": pl.BlockSpec((tm, tk), lambda i, j, k: (i, k)),
              "tn": pl.BlockSpec((tk, tm), lambda i, j, k: (k, i))}[mode]
    if layer is None:
        b_spec = {"nn": pl.BlockSpec((tk, tn), lambda i, j, k: (k, j)),
                  "nt": pl.BlockSpec((tn, tk), lambda i, j, k: (j, k)),
                  "tn": pl.BlockSpec((tk, tn), lambda i, j, k: (k, j))}[mode]
    elif mode == "nt":
        per = b.shape[2] // tn
        b_spec = pl.BlockSpec((None, None, tn, tk), lambda i, j, k: (j // per, layer, j % per, k))
    else:
        assert mode == "nn"
        per = b.shape[2] // tk
        b_spec = pl.BlockSpec((None, None, tk, tn), lambda i, j, k: (k // per, layer, k % per, j))
    dot = {"nn": _mm, "nt": _mm_nt, "tn": _mm_tn}[mode]
    has_bias = bias is not None

    def body(*refs):
        if has_bias:
            a_ref, b_ref, bias_ref, o_ref, acc_ref = refs
        else:
            a_ref, b_ref, o_ref, acc_ref = refs
            bias_ref = None
        k = pl.program_id(2)
        part = dot(a_ref[...], b_ref[...])

        def finish(total):
            if has_bias:
                total = total + bias_ref[...]
            o_ref[...] = total.astype(o_ref.dtype)

        if nk == 1:
            finish(part)
        else:
            @pl.when(k == 0)
            def _():
                acc_ref[...] = part

            @pl.when(k > 0)
            def _():
                acc_ref[...] += part

            @pl.when(k == nk - 1)
            def _():
                finish(acc_ref[...])

    in_specs = [a_spec, b_spec]
    args = [a, b]
    if has_bias:
        in_specs.append(pl.BlockSpec((1, tn), lambda i, j, k: (0, j)))
        args.append(bias.reshape(1, N))
    return pl.pallas_call(
        body, out_shape=_SDS((M, N), out_dtype), grid=(M // tm, N // tn, nk),
        in_specs=in_specs, out_specs=pl.BlockSpec((tm, tn), lambda i, j, k: (i, j)),
        scratch_shapes=[pltpu.VMEM((tm, tn), f32)],
        compiler_params=_cp(("parallel", "parallel", "arbitrary")), name=name,
    )(*args)


def _row_block(s):
    return _pick(s, (256, 128))


def _rms_fwd(x, g, *, out_dtype, name):
    s, d = x.shape
    rb = _row_block(s)

    def body(x_ref, g_ref, o_ref):
        xv = x_ref[...]
        r = lax.rsqrt(jnp.mean(xv * xv, axis=-1, keepdims=True) + RMS_EPS)
        o_ref[...] = (xv * r * g_ref[...]).astype(o_ref.dtype)

    return pl.pallas_call(
        body, out_shape=_SDS((s, d), out_dtype), grid=(s // rb,),
        in_specs=[pl.BlockSpec((rb, d), lambda i: (i, 0)), pl.BlockSpec((1, d), lambda i: (0, 0))],
        out_specs=pl.BlockSpec((rb, d), lambda i: (i, 0)), compiler_params=_cp(("parallel",)), name=name,
    )(x, g.reshape(1, d))


def _postnorm_fwd(u, g, x, *, name):
    s, d = u.shape
    rb = _row_block(s)

    def body(u_ref, g_ref, x_ref, o_ref):
        uv = u_ref[...]
        r = lax.rsqrt(jnp.mean(uv * uv, axis=-1, keepdims=True) + RMS_EPS)
        o_ref[...] = x_ref[...] + uv * r * g_ref[...]

    return pl.pallas_call(
        body, out_shape=_SDS((s, d), f32), grid=(s // rb,),
        in_specs=[pl.BlockSpec((rb, d), lambda i: (i, 0)), pl.BlockSpec((1, d), lambda i: (0, 0)),
                  pl.BlockSpec((rb, d), lambda i: (i, 0))],
        out_specs=pl.BlockSpec((rb, d), lambda i: (i, 0)), compiler_params=_cp(("parallel",)), name=name,
    )(u, g.reshape(1, d), x)


def _fold_rows(v):
    r = v.shape[0]
    acc = v[0:8]
    for k in range(1, r // 8):
        acc = acc + v[8 * k:8 * k + 8]
    return acc


def _rms_bwd(dy, x, g, res=None, *, name):
    s, d = x.shape
    rb = _row_block(s)
    nb = s // rb
    has_res = res is not None

    def body(*refs):
        if has_res:
            dy_ref, x_ref, g_ref, res_ref, dx_ref, dg_ref, acc_ref = refs
        else:
            dy_ref, x_ref, g_ref, dx_ref, dg_ref, acc_ref = refs
        i = pl.program_id(0)
        xv = x_ref[...]
        r = lax.rsqrt(jnp.mean(xv * xv, axis=-1, keepdims=True) + RMS_EPS)
        xh = xv * r
        dyv = dy_ref[...]
        dxh = dyv * g_ref[...]
        dx = r * (dxh - xh * jnp.mean(dxh * xh, axis=-1, keepdims=True))
        if has_res:
            dx = dx + res_ref[...]
        dx_ref[...] = dx
        part = _fold_rows(dyv * xh)

        @pl.when(i == 0)
        def _():
            acc_ref[...] = part

        @pl.when(i > 0)
        def _():
            acc_ref[...] += part

        @pl.when(i == nb - 1)
        def _():
            dg_ref[...] = jnp.sum(acc_ref[...], axis=0, keepdims=True)

    blk = pl.BlockSpec((rb, d), lambda i: (i, 0))
    in_specs = [blk, blk, pl.BlockSpec((1, d), lambda i: (0, 0))] + ([blk] if has_res else [])
    args = [dy, x, g.reshape(1, d)] + ([res] if has_res else [])
    return pl.pallas_call(
        body, out_shape=(_SDS((s, d), f32), _SDS((1, d), f32)), grid=(nb,), in_specs=in_specs,
        out_specs=(blk, pl.BlockSpec((1, d), lambda i: (0, 0))),
        scratch_shapes=[pltpu.VMEM((8, d), f32)], compiler_params=_cp(("arbitrary",)), name=name,
    )(*args)


def _loss_head(y, target, *, name):
    s, d = y.shape
    rb = _row_block(s)
    nb = s // rb

    def body(y_ref, t_ref, dy_ref, l_ref):
        i = pl.program_id(0)
        e = y_ref[...] - t_ref[...]
        dy_ref[...] = e * (1.0 / d)
        rows = _fold_rows(e * e)
        part = rows[:, 0:LANES]
        for k in range(1, d // LANES):
            part = part + rows[:, k * LANES:(k + 1) * LANES]
        part = part * (0.5 / d)

        @pl.when(i == 0)
        def _():
            l_ref[...] = part

        @pl.when(i > 0)
        def _():
            l_ref[...] += part

    blk = pl.BlockSpec((rb, d), lambda i: (i, 0))
    return pl.pallas_call(
        body, out_shape=(_SDS((s, d), f32), _SDS((8, LANES), f32)), grid=(nb,), in_specs=[blk, blk],
        out_specs=(blk, pl.BlockSpec((8, LANES), lambda i: (0, 0))),
        compiler_params=_cp(("arbitrary",)), name=name,
    )(y, target)


def _colsum(a, *, name):
    s, n = a.shape
    rb = _row_block(s)
    nb = s // rb
    tn = _pick(n, (2432, 2048, 1024, 512, 384, 128))

    def body(a_ref, o_ref, acc_ref):
        i = pl.program_id(1)
        part = _fold_rows(a_ref[...])

        @pl.when(i == 0)
        def _():
            acc_ref[...] = part

        @pl.when(i > 0)
        def _():
            acc_ref[...] += part

        @pl.when(i == nb - 1)
        def _():
            o_ref[...] = jnp.sum(acc_ref[...], axis=0, keepdims=True)

    return pl.pallas_call(
        body, out_shape=_SDS((1, n), f32), grid=(n // tn, nb),
        in_specs=[pl.BlockSpec((rb, tn), lambda j, i: (i, j))], out_specs=pl.BlockSpec((1, tn), lambda j, i: (0, j)),
        scratch_shapes=[pltpu.VMEM((8, tn), f32)], compiler_params=_cp(("parallel", "arbitrary")), name=name,
    )(a)


def _gate_fwd(o, gate, *, name):
    s, d = o.shape
    rb = _row_block(s)

    def body(o_ref, g_ref, m_ref):
        gv = g_ref[...]
        m_ref[...] = (o_ref[...] * (gv * _sigmoid(gv))).astype(m_ref.dtype)

    blk = pl.BlockSpec((rb, d), lambda i: (i, 0))
    return pl.pallas_call(body, out_shape=_SDS((s, d), _MXU), grid=(s // rb,), in_specs=[blk, blk], out_specs=blk,
                          compiler_params=_cp(("parallel",)), name=name)(o, gate)


def _gate_bwd(dmix, o, gate, *, name):
    s, d = o.shape
    rb = _row_block(s)

    def body(dm_ref, o_ref, g_ref, do_ref, dg_ref):
        gv = g_ref[...]
        sg = _sigmoid(gv)
        dm = dm_ref[...]
        do_ref[...] = dm * (gv * sg)
        dg_ref[...] = dm * o_ref[...] * (sg * (1.0 + gv * (1.0 - sg)))

    blk = pl.BlockSpec((rb, d), lambda i: (i, 0))
    return pl.pallas_call(body, out_shape=(_SDS((s, d), f32), _SDS((s, d), f32)), grid=(s // rb,),
                          in_specs=[blk, blk, blk], out_specs=(blk, blk), compiler_params=_cp(("parallel",)),
                          name=name)(dmix, o, gate)


def _adamw(w, g, m, v, *, name):
    shape = w.shape
    cols = shape[-1]
    rows = int(np.prod(shape[:-1])) if len(shape) > 1 else 1
    to2 = lambda t: t.reshape(rows, cols)
    rb = _pick(rows, (128, 64, 32, 16, 8)) if rows * cols * 4 > (1 << 20) else rows

    def body(w_ref, g_ref, m_ref, v_ref, d_ref, nm_ref, nv_ref):
        gv = g_ref[...]
        mn = ADAM_B1 * m_ref[...] + (1.0 - ADAM_B1) * gv
        vn = ADAM_B2 * v_ref[...] + (1.0 - ADAM_B2) * (gv * gv)
        m_hat = mn / (1.0 - ADAM_B1 ** ADAM_STEP)
        v_hat = vn / (1.0 - ADAM_B2 ** ADAM_STEP)
        d_ref[...] = -ADAM_LR * (m_hat / (jnp.sqrt(v_hat) + ADAM_EPS) + ADAM_WD * w_ref[...])
        nm_ref[...] = mn
        nv_ref[...] = vn

    blk = pl.BlockSpec((rb, cols), lambda i: (i, 0))
    out = pl.pallas_call(body, out_shape=tuple(_SDS((rows, cols), f32) for _ in range(3)), grid=(rows // rb,),
                         in_specs=[blk] * 4, out_specs=(blk,) * 3, compiler_params=_cp(("parallel",)),
                         name=name)(to2(w), to2(g), to2(m), to2(v))
    return tuple(t.reshape(shape) for t in out)


def _sum_slots(a, *, name):
    p, n, c = a.shape
    rb = max(d for d in range(8, n + 1, 8) if n % d == 0 and (p * d * c * 4 <= (6 << 20) or d == 8))

    def body(a_ref, o_ref):
        acc = a_ref[0].astype(f32)
        for k in range(1, p):
            acc = acc + a_ref[k].astype(f32)
        o_ref[...] = acc

    return pl.pallas_call(body, out_shape=_SDS((n, c), f32), grid=(n // rb,),
                          in_specs=[pl.BlockSpec((p, rb, c), lambda i: (0, i, 0))],
                          out_specs=pl.BlockSpec((rb, c), lambda i: (i, 0)), compiler_params=_cp(("parallel",)),
                          name=name)(a)


def _add2(a, b, *, name):
    p, n, c = a.shape
    rb = _pick(n, (1024, 976, 512, 256, 128, 64, 32, 16, 8))

    def body(a_ref, b_ref, o_ref):
        o_ref[...] = a_ref[...] + b_ref[...]

    blk = pl.BlockSpec((1, rb, c), lambda s, i: (s, i, 0))
    return pl.pallas_call(body, out_shape=_SDS((p, n, c), f32), grid=(p, n // rb), in_specs=[blk, blk], out_specs=blk,
                          compiler_params=_cp(("parallel", "parallel")), name=name)(a, b)


def _rope_tables(pos, dim):
    half = dim // 2
    inv = ROPE_THETA ** (-jnp.arange(half, dtype=f32) / half)
    ang = pos.astype(f32)[:, None] * inv[None, :]
    c, s = jnp.cos(ang), jnp.sin(ang)
    z = jnp.zeros_like(c)
    pad = [jnp.zeros((pos.shape[0], LANES - dim), f32)] if dim < LANES else []
    return (jnp.concatenate([c, c] + pad, axis=1), jnp.concatenate([-s, z] + pad, axis=1),
            jnp.concatenate([z, s] + pad, axis=1))


def _rope(x, cos, sa, sb, half, transpose=False):
    if transpose:
        return x * cos + pltpu.roll(x * sa, half, 1) + pltpu.roll(x * sb, LANES - half, 1)
    return x * cos + pltpu.roll(x, LANES - half, 1) * sa + pltpu.roll(x, half, 1) * sb


def _rope_call(items, tables, half, transpose, *, name):
    s = items[0][0].shape[0]
    rb = _row_block(s)
    n = len(items)

    def body(*refs):
        cos, sa, sb = refs[n][...], refs[n + 1][...], refs[n + 2][...]
        for k in range(n):
            x_ref, o_ref = refs[k], refs[n + 3 + k]
            for j in range(items[k][1] // LANES):
                sl = slice(j * LANES, (j + 1) * LANES)
                o_ref[:, sl] = _rope(x_ref[:, sl], cos, sa, sb, half, transpose)

    in_specs = [pl.BlockSpec((rb, w), functools.partial(lambda i, cb: (i, cb), cb=cb)) for _, w, cb in items]
    in_specs += [pl.BlockSpec((rb, LANES), lambda i: (i, 0))] * 3
    out_specs = tuple(pl.BlockSpec((rb, w), lambda i: (i, 0)) for _, w, _ in items)
    return pl.pallas_call(
        body, out_shape=tuple(_SDS((s, w), f32) for _, w, _ in items), grid=(s // rb,), in_specs=in_specs,
        out_specs=out_specs, compiler_params=_cp(("parallel",)), name=name,
    )(*[a for a, _, _ in items], *tables)


def _attn_block(s):
    return _pick(s, (256, 128))


def _lower_mask(b, strict):
    r, c = _iota((b, b), 0), _iota((b, b), 1)
    return (c < r) if strict else (c <= r)


def _pick_lane(block, h):
    return jnp.sum(jnp.where(_iota(block.shape, 1) == h, block, 0.0), axis=1, keepdims=True)


def _attn_fwd(q, k, v, qcol, kcol, vcol, dq, cum, cum_t, *, scale, name):
    s = q.shape[0]
    b = _attn_block(s)
    nq = s // b
    has_bias = cum is not None

    def body(*refs):
        if has_bias:
            q_ref, k_ref, v_ref, cum_ref, cumt_ref, o_ref, lse_ref = refs
        else:
            q_ref, k_ref, v_ref, o_ref, lse_ref = refs
        h, i = pl.program_id(0), pl.program_id(1)
        qv = q_ref[...].astype(_MXU)
        cq = _pick_lane(cum_ref[...], h) if has_bias else None

        def chunk(c, carry, diag):
            m, l, acc = carry
            st = pl.multiple_of(c * b, b)
            z = _mm_nt(qv, k_ref[pl.ds(st, b), :]) * scale
            if has_bias:
                z = z + cq - cumt_ref[c]
            if diag:
                mask = _lower_mask(b, False)
                z = jnp.where(mask, z, NEG_INF)
            m_new = jnp.maximum(m, jnp.max(z, axis=1, keepdims=True))
            p = jnp.exp(z - m_new)
            if diag:
                p = jnp.where(mask, p, 0.0)
            alpha = jnp.exp(m - m_new)
            l = alpha * l + jnp.sum(p, axis=1, keepdims=True)
            acc = alpha * acc + _mm(p, v_ref[pl.ds(st, b), :])
            return m_new, l, acc

        init = (jnp.full((b, 1), NEG_INF, f32), jnp.zeros((b, 1), f32), jnp.zeros((b, HEAD_DIM), f32))
        carry = lax.fori_loop(0, i, lambda c, cr: chunk(c, cr, False), init)
        m, l, acc = chunk(i, carry, True)
        o_ref[...] = acc / l
        lse_ref[...] = m + jnp.log(l)

    in_specs = [pl.BlockSpec((b, dq), lambda h, i: (i, qcol + h)), pl.BlockSpec((s, dq), lambda h, i: (0, kcol + h)),
                pl.BlockSpec((s, HEAD_DIM), lambda h, i: (0, vcol + h))]
    args = [q, k, v]
    if has_bias:
        in_specs += [pl.BlockSpec((b, LANES), lambda h, i: (i, 0)),
                     pl.BlockSpec((None, nq, 1, b), lambda h, i: (h, 0, 0, 0))]
        args += [cum, cum_t]
    return pl.pallas_call(
        body, out_shape=(_SDS((s, N_HEADS * HEAD_DIM), f32), _SDS((N_HEADS, s, 1), f32)), grid=(N_HEADS, nq),
        in_specs=in_specs,
        out_specs=(pl.BlockSpec((b, HEAD_DIM), lambda h, i: (i, h)), pl.BlockSpec((None, b, 1), lambda h, i: (h, i, 0))),
        compiler_params=_cp(("parallel", "parallel")), name=name,
    )(*args)


def _attn_bwd(q, k, v, qcol, kcol, vcol, dq, do, o, lse, cum, cum_t, *, scale, name):
    s = q.shape[0]
    b = _attn_block(s)
    nq = s // b
    has_bias = cum is not None

    def body(*refs):
        if has_bias:
            (q_ref, k_ref, v_ref, do_ref, o_ref, lse_ref, cum_ref, cumt_ref, dq_ref, dk_ref, dv_ref, dck_ref,
             p_sc, dp_sc) = refs
        else:
            q_ref, k_ref, v_ref, do_ref, o_ref, lse_ref, dq_ref, dk_ref, dv_ref = refs
        h, i = pl.program_id(0), pl.program_id(1)

        @pl.when(i == 0)
        def _():
            dk_ref[...] = jnp.zeros_like(dk_ref)
            dv_ref[...] = jnp.zeros_like(dv_ref)
            if has_bias:
                dck_ref[...] = jnp.zeros_like(dck_ref)

        qv = q_ref[...].astype(_MXU)
        dov = do_ref[...]
        dob = dov.astype(_MXU)
        lse_v = lse_ref[...]
        cq = _pick_lane(cum_ref[...], h) if has_bias else None

        def probs(c, diag):
            st = pl.multiple_of(c * b, b)
            z = _mm_nt(qv, k_ref[pl.ds(st, b), :]) * scale
            if has_bias:
                z = z + cq - cumt_ref[c]
            p = jnp.exp(z - lse_v)
            if diag:
                p = jnp.where(_lower_mask(b, False), p, 0.0)
            return p, _mm_nt(dob, v_ref[pl.ds(st, b), :])

        if has_bias:
            def first(c, acc, diag):
                p, dp = probs(c, diag)
                p_sc[c] = p
                dp_sc[c] = dp
                return acc + jnp.sum(p * dp, axis=1, keepdims=True)

            delta = lax.fori_loop(0, i, lambda c, a: first(c, a, False), jnp.zeros((b, 1), f32))
            delta = first(i, delta, True)
        else:
            delta = jnp.sum(dov * o_ref[...], axis=1, keepdims=True)

        def chunk(c, dq_acc, diag):
            st = pl.multiple_of(c * b, b)
            kc = k_ref[pl.ds(st, b), :]
            p, dp = (p_sc[c], dp_sc[c]) if has_bias else probs(c, diag)
            ds = p * (dp - delta)
            dk_ref[pl.ds(st, b), :] += _mm_tn(ds, qv) * scale
            dv_ref[pl.ds(st, b), :] += _mm_tn(p, dob)
            if has_bias:
                dck_ref[c] += -jnp.sum(ds, axis=0, keepdims=True)
            return dq_acc + _mm(ds, kc)

        acc = lax.fori_loop(0, i, lambda c, a: chunk(c, a, False), jnp.zeros((b, dq), f32))
        acc = chunk(i, acc, True)
        dq_ref[...] = acc * scale

    in_specs = [pl.BlockSpec((b, dq), lambda h, i: (i, qcol + h)), pl.BlockSpec((s, dq), lambda h, i: (0, kcol + h)),
                pl.BlockSpec((s, HEAD_DIM), lambda h, i: (0, vcol + h)),
                pl.BlockSpec((b, HEAD_DIM), lambda h, i: (i, h)), pl.BlockSpec((b, HEAD_DIM), lambda h, i: (i, h)),
                pl.BlockSpec((None, b, 1), lambda h, i: (h, i, 0))]
    args = [q, k, v, do, o, lse]
    out_shape = [_SDS((s, N_HEADS * dq), f32), _SDS((s, N_HEADS * dq), f32), _SDS((s, N_HEADS * HEAD_DIM), f32)]
    out_specs = [pl.BlockSpec((b, dq), lambda h, i: (i, h)), pl.BlockSpec((s, dq), lambda h, i: (0, h)),
                 pl.BlockSpec((s, HEAD_DIM), lambda h, i: (0, h))]
    if has_bias:
        in_specs += [pl.BlockSpec((b, LANES), lambda h, i: (i, 0)),
                     pl.BlockSpec((None, nq, 1, b), lambda h, i: (h, 0, 0, 0))]
        args += [cum, cum_t]
        out_shape.append(_SDS((N_HEADS, nq, 1, b), f32))
        out_specs.append(pl.BlockSpec((None, nq, 1, b), lambda h, i: (h, 0, 0, 0)))
    return pl.pallas_call(
        body, out_shape=tuple(out_shape), grid=(N_HEADS, nq), in_specs=in_specs, out_specs=tuple(out_specs),
        scratch_shapes=[pltpu.VMEM((nq, b, b), f32)] * 2 if has_bias else [],
        compiler_params=_cp(("parallel", "arbitrary")), name=name,
    )(*args)


def _tri(b, kind):
    r, c = _iota((b, b), 0), _iota((b, b), 1)
    cond = {"row_gt": r > c, "row_lt": r < c, "row_ge": r >= c, "row_le": r <= c}[kind]
    return jnp.where(cond, 1.0, 0.0).astype(_MXU)


def _log_keep(z):
    return -(jnp.maximum(z, 0.0) + jnp.log1p(jnp.exp(-jnp.abs(z))))


def _sb_fwd(z_all, *, name):
    s = z_all.shape[0]
    b = _attn_block(s)
    nq = s // b
    scale = HEAD_DIM ** -0.5
    qcol, kcol, vcol = (_AL[n] // HEAD_DIM for n in ("sb_q", "sb_k", "sb_v"))

    def body(q_ref, k_ref, v_ref, o_ref):
        i = pl.program_id(1)
        qv = q_ref[...].astype(_MXU)
        upper = _tri(b, "row_gt")

        def chunk(c, carry, diag):
            rsum, acc = carry
            st = pl.multiple_of(c * b, b)
            z = _mm_nt(qv, k_ref[pl.ds(st, b), :]) * scale
            lk = _log_keep(z)
            if diag:
                mask = _lower_mask(b, True)
                lk = jnp.where(mask, lk, 0.0)
            a = z + lk + _mm_split(lk, upper) + rsum
            if diag:
                a = jnp.where(mask, a, NEG_INF)
            acc = acc + _mm(jnp.exp(a), v_ref[pl.ds(st, b), :])
            return rsum + jnp.sum(lk, axis=1, keepdims=True), acc

        carry = chunk(i, (jnp.zeros((b, 1), f32), jnp.zeros((b, HEAD_DIM), f32)), True)
        _, acc = lax.fori_loop(0, i, lambda j, cr: chunk(i - 1 - j, cr, False), carry)
        o_ref[...] = acc

    return pl.pallas_call(
        body, out_shape=_SDS((s, GROUP), f32), grid=(N_HEADS, nq),
        in_specs=[pl.BlockSpec((b, HEAD_DIM), lambda h, i: (i, qcol + h)),
                  pl.BlockSpec((s, HEAD_DIM), lambda h, i: (0, kcol + h)),
                  pl.BlockSpec((s, HEAD_DIM), lambda h, i: (0, vcol + h))],
        out_specs=pl.BlockSpec((b, HEAD_DIM), lambda h, i: (i, h)),
        compiler_params=_cp(("parallel", "parallel")), name=name,
    )(z_all, z_all, z_all)


def _sb_bwd(z_all, do, *, name):
    s = z_all.shape[0]
    b = _attn_block(s)
    nq = s // b
    scale = HEAD_DIM ** -0.5
    qcol, kcol, vcol = (_AL[n] // HEAD_DIM for n in ("sb_q", "sb_k", "sb_v"))

    def body(q_ref, k_ref, v_ref, do_ref, dq_ref, dk_ref, dv_ref, z_sc, lk_sc, r_sc):
        i = pl.program_id(1)

        @pl.when(i == 0)
        def _():
            dk_ref[...] = jnp.zeros_like(dk_ref)
            dv_ref[...] = jnp.zeros_like(dv_ref)

        qv = q_ref[...].astype(_MXU)
        dob = do_ref[...].astype(_MXU)
        upper = _tri(b, "row_gt")
        lower = _tri(b, "row_lt")

        def scores(c, rsum, diag):
            st = pl.multiple_of(c * b, b)
            z = _mm_nt(qv, k_ref[pl.ds(st, b), :]) * scale
            lk = _log_keep(z)
            if diag:
                lk = jnp.where(_lower_mask(b, True), lk, 0.0)
            z_sc[c] = z
            lk_sc[c] = lk
            r_sc[c] = _mm_split(lk, upper) + rsum
            return rsum + jnp.sum(lk, axis=1, keepdims=True)

        rsum = scores(i, jnp.zeros((b, 1), f32), True)
        lax.fori_loop(0, i, lambda j, r: scores(i - 1 - j, r, False), rsum)

        def grads(c, carry, diag):
            psum, dq_acc = carry
            st = pl.multiple_of(c * b, b)
            z, lk = z_sc[c], lk_sc[c]
            lb = z + lk
            a = lb + r_sc[c]
            if diag:
                mask = _lower_mask(b, True)
                a = jnp.where(mask, a, NEG_INF)
            w = jnp.exp(a)
            e = _mm_nt(dob, v_ref[pl.ds(st, b), :]) * w
            before = _mm_split(e, lower) + psum
            dz = e * jnp.exp(lk) - before * jnp.exp(lb)
            if diag:
                dz = jnp.where(mask, dz, 0.0)
            kc = k_ref[pl.ds(st, b), :]
            dk_ref[pl.ds(st, b), :] += _mm_tn(dz, qv) * scale
            dv_ref[pl.ds(st, b), :] += _mm_tn(w, dob)
            return psum + jnp.sum(e, axis=1, keepdims=True), dq_acc + _mm(dz, kc)

        carry = lax.fori_loop(0, i, lambda c, cr: grads(c, cr, False),
                              (jnp.zeros((b, 1), f32), jnp.zeros((b, HEAD_DIM), f32)))
        _, dq_acc = grads(i, carry, True)
        dq_ref[...] = dq_acc * scale

    blk = pl.BlockSpec((b, HEAD_DIM), lambda h, i: (i, h))
    full = pl.BlockSpec((s, HEAD_DIM), lambda h, i: (0, h))
    return pl.pallas_call(
        body, out_shape=tuple(_SDS((s, GROUP), f32) for _ in range(3)), grid=(N_HEADS, nq),
        in_specs=[pl.BlockSpec((b, HEAD_DIM), lambda h, i: (i, qcol + h)),
                  pl.BlockSpec((s, HEAD_DIM), lambda h, i: (0, kcol + h)),
                  pl.BlockSpec((s, HEAD_DIM), lambda h, i: (0, vcol + h)), blk],
        out_specs=(blk, full, full),
        scratch_shapes=[pltpu.VMEM((nq, b, b), f32)] * 3,
        compiler_params=_cp(("parallel", "arbitrary")), name=name,
    )(z_all, z_all, z_all, do)


def _split3_left(t, x):
    hi = x.astype(_MXU)
    r1 = x - hi.astype(f32)
    mid = r1.astype(_MXU)
    lo = (r1 - mid.astype(f32)).astype(_MXU)
    dot = functools.partial(jnp.dot, preferred_element_type=f32)
    return dot(t, hi) + dot(t, mid) + dot(t, lo)


def _split3_right(x, t):
    hi = x.astype(_MXU)
    r1 = x - hi.astype(f32)
    mid = r1.astype(_MXU)
    lo = (r1 - mid.astype(f32)).astype(_MXU)
    dot = functools.partial(jnp.dot, preferred_element_type=f32)
    return dot(hi, t) + dot(mid, t) + dot(lo, t)


def _fox_cum_fwd(z_all, bias, *, name):
    s = z_all.shape[0]
    b = _attn_block(s)
    fcol = _AL["fox_f"] // LANES

    def body(f_ref, b_ref, cum_ref, cumt_ref, carry_ref):
        i = pl.program_id(0)

        @pl.when(i == 0)
        def _():
            carry_ref[...] = jnp.zeros_like(carry_ref)

        u = f_ref[...] + b_ref[...]
        lf = jnp.minimum(u, 0.0) - jnp.log1p(jnp.exp(-jnp.abs(u)))
        cum = _split3_left(_tri(b, "row_ge"), lf) + carry_ref[...]
        cum_ref[...] = cum
        cumt_ref[...] = cum.T[0:8, :]
        carry_ref[...] = cum_ref[b - 1:b, :]

    return pl.pallas_call(
        body, out_shape=(_SDS((s, LANES), f32), _SDS((8, s), f32)), grid=(s // b,),
        in_specs=[pl.BlockSpec((b, LANES), lambda i: (i, fcol)), pl.BlockSpec((1, LANES), lambda i: (0, 0))],
        out_specs=(pl.BlockSpec((b, LANES), lambda i: (i, 0)), pl.BlockSpec((8, b), lambda i: (0, i))),
        scratch_shapes=[pltpu.VMEM((1, LANES), f32)], compiler_params=_cp(("arbitrary",)), name=name,
    )(z_all, bias)


def _fox_cum_bwd(z_all, bias, dcum_t, *, name):
    s = z_all.shape[0]
    b = _attn_block(s)
    nb = s // b
    fcol = _AL["fox_f"] // LANES

    def body(f_ref, b_ref, dc_ref, df_ref, db_ref, carry_ref):
        i = pl.program_id(0)

        @pl.when(i == 0)
        def _():
            carry_ref[...] = jnp.zeros_like(carry_ref)
            db_ref[...] = jnp.zeros_like(db_ref)

        dc = dc_ref[...]
        rev = _split3_right(dc, _tri(b, "row_ge")) + carry_ref[...]
        carry_ref[...] = carry_ref[...] + jnp.sum(dc, axis=1, keepdims=True)
        dlf = jnp.concatenate([rev, jnp.zeros((LANES - 8, b), f32)], axis=0).T
        u = f_ref[...] + b_ref[...]
        df = jnp.where(_iota((b, LANES), 1) < N_HEADS, dlf * (1.0 - _sigmoid(u)), 0.0)
        df_ref[...] = df
        db_ref[...] += jnp.sum(df, axis=0, keepdims=True)

    return pl.pallas_call(
        body, out_shape=(_SDS((s, LANES), f32), _SDS((1, LANES), f32)), grid=(nb,),
        in_specs=[pl.BlockSpec((b, LANES), lambda i: (nb - 1 - i, fcol)), pl.BlockSpec((1, LANES), lambda i: (0, 0)),
                  pl.BlockSpec((8, b), lambda i: (0, nb - 1 - i))],
        out_specs=(pl.BlockSpec((b, LANES), lambda i: (nb - 1 - i, 0)), pl.BlockSpec((1, LANES), lambda i: (0, 0))),
        scratch_shapes=[pltpu.VMEM((8, 1), f32)], compiler_params=_cp(("arbitrary",)), name=name,
    )(z_all, bias, dcum_t)


MLA_QW = 2 * LANES


def _rms_rows(x):
    r = lax.rsqrt(jnp.mean(x * x, axis=-1, keepdims=True) + RMS_EPS)
    return x * r, r


def _mla_prep_fwd(z_all, gq, gkv, wuq, wk, wv, tables, *, name):
    s = z_all.shape[0]
    rb = _row_block(s)
    half = MLA_ROPE // 2

    def body(cq_ref, ckv_ref, kr_ref, gq_ref, gkv_ref, wuq_ref, wk_ref, wv_ref, cos_ref, sa_ref, sb_ref,
             q_ref, k_ref, v_ref):
        cos, sa, sb = cos_ref[...], sa_ref[...], sb_ref[...]
        xh, _ = _rms_rows(cq_ref[...])
        qp = _mm(xh * gq_ref[...], wuq_ref[...])
        kh, _ = _rms_rows(ckv_ref[...])
        nkv = kh * gkv_ref[...]
        kn = _mm(nkv, wk_ref[...])
        v_ref[...] = _mm(nkv, wv_ref[...])
        kr = _rope(kr_ref[...], cos, sa, sb, half)
        for h in range(N_HEADS):
            lo, mid, hi = h * MLA_QW, h * MLA_QW + LANES, (h + 1) * MLA_QW
            q_ref[:, lo:mid] = qp[:, lo:mid]
            q_ref[:, mid:hi] = _rope(qp[:, mid:hi], cos, sa, sb, half)
            k_ref[:, lo:mid] = kn[:, h * LANES:(h + 1) * LANES]
            k_ref[:, mid:hi] = kr

    row = lambda w, cb: pl.BlockSpec((rb, w), lambda i: (i, cb))
    whole = lambda a: pl.BlockSpec(a.shape, lambda i: (0,) * a.ndim)
    return pl.pallas_call(
        body, out_shape=(_SDS((s, N_HEADS * MLA_QW), f32), _SDS((s, N_HEADS * MLA_QW), f32), _SDS((s, GROUP), f32)),
        grid=(s // rb,),
        in_specs=[row(MLA_Q_RANK, _AL["mla_cq"] // MLA_Q_RANK), row(LANES, _AL["mla_ckv"] // LANES),
                  row(LANES, _AL["mla_k_rope"] // LANES), whole(gq), whole(gkv), whole(wuq), whole(wk), whole(wv),
                  row(LANES, 0), row(LANES, 0), row(LANES, 0)],
        out_specs=(row(N_HEADS * MLA_QW, 0), row(N_HEADS * MLA_QW, 0), row(GROUP, 0)),
        compiler_params=_cp(("parallel",)), name=name,
    )(z_all, z_all, z_all, gq, gkv, wuq, wk, wv, *tables)


def _mla_prep_bwd(z_all, gq, gkv, wuq, wk, wv, tables, dq_cat, dk_cat, dv, *, name):
    s = z_all.shape[0]
    rb = _row_block(s)
    half = MLA_ROPE // 2

    def body(cq_ref, ckv_ref, gq_ref, gkv_ref, wuq_ref, wk_ref, wv_ref, cos_ref, sa_ref, sb_ref, dq_ref, dk_ref,
             dv_ref, dcq_ref, dckv_ref, dkr_ref, dwuq_ref, dwk_ref, dwv_ref, dgq_ref, dgkv_ref):
        i = pl.program_id(0)

        @pl.when(i == 0)
        def _():
            for r in (dwuq_ref, dwk_ref, dwv_ref, dgq_ref, dgkv_ref):
                r[...] = jnp.zeros_like(r)

        cos, sa, sb = cos_ref[...], sa_ref[...], sb_ref[...]
        parts, knp = [], []
        dkr = jnp.zeros((rb, LANES), f32)
        for h in range(N_HEADS):
            lo, mid, hi = h * MLA_QW, h * MLA_QW + LANES, (h + 1) * MLA_QW
            parts += [dq_ref[:, lo:mid], _rope(dq_ref[:, mid:hi], cos, sa, sb, half, transpose=True)]
            knp.append(dk_ref[:, lo:mid])
            dkr = dkr + _rope(dk_ref[:, mid:hi], cos, sa, sb, half, transpose=True)
        dkr_ref[...] = dkr
        dqp = jnp.concatenate(parts, axis=1)
        dkn = jnp.concatenate(knp, axis=1)
        dvv = dv_ref[...]

        def norm_bwd(x_ref, g_ref, w_pairs, dx_ref, dg_ref):
            xh, r = _rms_rows(x_ref[...])
            nx = xh * g_ref[...]
            dn = jnp.zeros_like(xh)
            for w_ref, dw_ref, dy in w_pairs:
                dw_ref[...] += _mm_tn(nx, dy)
                dn = dn + _mm_nt(dy, w_ref[...])
            dxh = dn * g_ref[...]
            dx_ref[...] = r * (dxh - xh * jnp.mean(dxh * xh, axis=-1, keepdims=True))
            dg_ref[...] += jnp.sum(dn * xh, axis=0, keepdims=True)

        norm_bwd(cq_ref, gq_ref, [(wuq_ref, dwuq_ref, dqp)], dcq_ref, dgq_ref)
        norm_bwd(ckv_ref, gkv_ref, [(wk_ref, dwk_ref, dkn), (wv_ref, dwv_ref, dvv)], dckv_ref, dgkv_ref)

    row = lambda w, cb: pl.BlockSpec((rb, w), lambda i: (i, cb))
    whole = lambda a: pl.BlockSpec(a.shape, lambda i: (0,) * a.ndim)
    return pl.pallas_call(
        body,
        out_shape=(_SDS((s, MLA_Q_RANK), f32), _SDS((s, LANES), f32), _SDS((s, LANES), f32), _SDS(wuq.shape, f32),
                   _SDS(wk.shape, f32), _SDS(wv.shape, f32), _SDS(gq.shape, f32), _SDS(gkv.shape, f32)),
        grid=(s // rb,),
        in_specs=[row(MLA_Q_RANK, _AL["mla_cq"] // MLA_Q_RANK), row(LANES, _AL["mla_ckv"] // LANES), whole(gq),
                  whole(gkv), whole(wuq), whole(wk), whole(wv), row(LANES, 0), row(LANES, 0), row(LANES, 0),
                  row(N_HEADS * MLA_QW, 0), row(N_HEADS * MLA_QW, 0), row(GROUP, 0)],
        out_specs=(row(MLA_Q_RANK, 0), row(LANES, 0), row(LANES, 0), whole(wuq), whole(wk), whole(wv), whole(gq),
                   whole(gkv)),
        compiler_params=_cp(("arbitrary",)), name=name,
    )(z_all, z_all, gq, gkv, wuq, wk, wv, *tables, dq_cat, dk_cat, dv)


def _silu_grad(x):
    sg = _sigmoid(x)
    return sg * (1.0 + x * (1.0 - sg))


def _nsa_cmp_fwd(ra, rb_, pos, w1, w2, tables, *, name):
    nr = ra.shape[1]
    hw = ra.shape[2]

    def body(ra_ref, rb_ref, pos_ref, w1_ref, w2_ref, cos_ref, sa_ref, sb_ref, out_ref, hp_ref):
        for k in range(2):
            xa = ra_ref[k] + pos_ref[k, :, 0:hw]
            xb = rb_ref[k] + pos_ref[k, :, hw:2 * hw]
            hp = _mm(xa, w1_ref[k, 0:hw, :]) + _mm(xb, w1_ref[k, hw:2 * hw, :])
            hp_ref[k] = hp
            out = _mm(hp * _sigmoid(hp), w2_ref[k])
            if k == 0:
                out = _rope(out, cos_ref[...], sa_ref[...], sb_ref[...], HEAD_DIM // 2)
            out_ref[k] = out

    return pl.pallas_call(body, out_shape=(_SDS((2, nr, HEAD_DIM), f32), _SDS((2, nr, HEAD_DIM), f32)),
                          compiler_params=_cp(), name=name)(ra, rb_, pos, w1, w2, *tables)


def _nsa_cmp_bwd(ra, rb_, pos, w1, w2, tables, hp, dout, *, name):
    nr = ra.shape[1]
    hw = ra.shape[2]

    def body(ra_ref, rb_ref, pos_ref, w1_ref, w2_ref, cos_ref, sa_ref, sb_ref, hp_ref, do_ref,
             dxa_ref, dxb_ref, dw1_ref, dw2_ref):
        for k in range(2):
            d_out = do_ref[k]
            if k == 0:
                d_out = _rope(d_out, cos_ref[...], sa_ref[...], sb_ref[...], HEAD_DIM // 2, transpose=True)
            hpv = hp_ref[k]
            dw2_ref[k] = _mm_tn(hpv * _sigmoid(hpv), d_out)
            dhp = _mm_nt(d_out, w2_ref[k]) * _silu_grad(hpv)
            xa = ra_ref[k] + pos_ref[k, :, 0:hw]
            xb = rb_ref[k] + pos_ref[k, :, hw:2 * hw]
            dw1_ref[k, 0:hw, :] = _mm_tn(xa, dhp)
            dw1_ref[k, hw:2 * hw, :] = _mm_tn(xb, dhp)
            dxa_ref[k] = _mm_nt(dhp, w1_ref[k, 0:hw, :])
            dxb_ref[k] = _mm_nt(dhp, w1_ref[k, hw:2 * hw, :])

    return pl.pallas_call(
        body, out_shape=(_SDS((2, nr, hw), f32), _SDS((2, nr, hw), f32), _SDS(w1.shape, f32), _SDS(w2.shape, f32)),
        compiler_params=_cp(), name=name)(ra, rb_, pos, w1, w2, *tables, hp, dout)


def _nsa_consts(s):
    b = _attn_block(s)
    nr = s // CMP_STRIDE
    n_cmp = (s - CMP_LEN) // CMP_STRIDE + 1
    n_sel = s // SEL_LEN
    cmp_start = np.arange(n_cmp) * CMP_STRIDE
    sel_start = np.arange(n_sel) * SEL_LEN
    overlap = np.clip(np.minimum(cmp_start[:, None] + CMP_LEN, sel_start[None, :] + SEL_LEN)
                      - np.maximum(cmp_start[:, None], sel_start[None, :]), 0, None)
    m2s = np.zeros((nr, LANES), np.float32)
    m2s[:n_cmp, :n_sel] = overlap / CMP_LEN
    e3 = np.zeros((s // b, LANES, b), np.float32)
    tok = np.arange(s)
    e3[tok // b, tok // SEL_LEN, tok % b] = 1.0
    return jnp.asarray(m2s, _MXU), jnp.asarray(e3, _MXU)


def _nsa_masks(i, b, d):
    qpos = i * b + _iota((b, b), 0)
    kpos = (i - d) * b + _iota((b, b), 1)
    return (kpos <= qpos) & (kpos > qpos - WINDOW)


def _nsa_fwd(qr, kvc, ksr, vs, kwr, vw, z_all, m2s, e3, *, name):
    s = qr.shape[0]
    b = _attn_block(s)
    nq = s // b
    nr = kvc.shape[1]
    n_sel = s // SEL_LEN
    top_n = min(SEL_TOPN, n_sel)
    nd = -(-WINDOW // b)
    scale = HEAD_DIM ** -0.5
    bcol = _AL["nsa_branch"] // LANES
    H = N_HEADS

    def body(q_ref, kvc_ref, ks_ref, vs_ref, kw_ref, vw_ref, br_ref, m2s_ref, e3_ref,
             o_ref, oc_ref, os_ref, ow_ref, st_ref, sel_ref, m_sc, l_sc, acc_sc):
        i = pl.program_id(0)
        lane = _iota((b, LANES), 1)
        hs = lambda h: slice(h * HEAD_DIM, (h + 1) * HEAD_DIM)

        cmp_mask = (CMP_STRIDE * _iota((b, nr), 1) + (CMP_LEN - 1)) <= (i * b + _iota((b, nr), 0))
        imp = jnp.zeros((b, LANES), f32)
        stats = jnp.zeros((b, LANES), f32)
        for h in range(H):
            zc = jnp.where(cmp_mask, _mm_nt(q_ref[:, hs(h)], kvc_ref[0]) * scale, NEG_INF)
            m = jnp.max(zc, axis=1, keepdims=True)
            p = jnp.where(cmp_mask, jnp.exp(zc - m), 0.0)
            l = jnp.sum(p, axis=1, keepdims=True)
            some = l > 0.0
            lsafe = jnp.where(some, l, 1.0)
            pc = p * jnp.where(some, 1.0 / lsafe, 0.0)
            oc_ref[:, hs(h)] = _mm(pc, kvc_ref[1])
            imp = imp + _mm(pc, m2s_ref[...])
            stats = jnp.where(lane == h, jnp.where(some, m + jnp.log(lsafe), 0.0), stats)

        cur = jnp.right_shift(i * b + _iota((b, LANES), 0), int(math.log2(SEL_LEN)))
        forced = (lane == 0) | (lane == cur) | (lane == cur - 1)
        score = jnp.where(lane <= cur, jnp.where(forced, FORCED_BONUS, imp), NEG_INF)
        score = jnp.where(lane < n_sel, score, -3e38)
        rank = jnp.zeros((b, LANES), f32)
        for j in range(n_sel):
            col = score[:, j:j + 1]
            rank = rank + jnp.where(col > score, 1.0, jnp.where(col == score, jnp.where(lane > j, 1.0, 0.0), 0.0))
        sel = jnp.where(lane < n_sel, jnp.where(rank < top_n, 1.0, 0.0), 0.0)
        sel_ref[...] = sel
        sel_b = sel.astype(_MXU)

        def reset():
            m_sc[...] = jnp.full(m_sc.shape, NEG_INF, f32)
            l_sc[...] = jnp.zeros_like(l_sc)
            acc_sc[...] = jnp.zeros_like(acc_sc)

        def update(h, z, mask, vch):
            zm = jnp.where(mask, z, NEG_INF)
            m_old = m_sc[h]
            m_new = jnp.maximum(m_old, jnp.max(zm, axis=1, keepdims=True))
            p = jnp.where(mask, jnp.exp(zm - m_new), 0.0)
            alpha = jnp.exp(m_old - m_new)
            l_sc[h] = alpha * l_sc[h] + jnp.sum(p, axis=1, keepdims=True)
            acc_sc[h] = alpha * acc_sc[h] + _mm(p, vch)
            m_sc[h] = m_new

        def finish(out_ref, branch, stats):
            for h in range(H):
                out_ref[:, hs(h)] = acc_sc[h] / l_sc[h]
                stats = jnp.where(lane == 4 * branch + h, m_sc[h] + jnp.log(l_sc[h]), stats)
            return stats

        def sel_chunk(c, diag):
            st = pl.multiple_of(c * b, b)
            mask = _mm(sel_b, e3_ref[c]) > 0.5
            if diag:
                mask = mask & _lower_mask(b, False)
            kch, vch = ks_ref[pl.ds(st, b), :], vs_ref[pl.ds(st, b), :]
            for h in range(H):
                update(h, _mm_nt(q_ref[:, hs(h)], kch) * scale, mask, vch)

        reset()

        def sel_loop(c, carry):
            sel_chunk(c, False)
            return carry

        lax.fori_loop(0, i, sel_loop, 0)
        sel_chunk(i, True)
        stats = finish(os_ref, 1, stats)

        reset()
        for d in range(nd, -1, -1):
            @pl.when(i >= d)
            def _():
                st = pl.multiple_of((i - d) * b, b)
                mask = _nsa_masks(i, b, d)
                kch, vch = kw_ref[pl.ds(st, b), :], vw_ref[pl.ds(st, b), :]
                for h in range(H):
                    update(h, _mm_nt(q_ref[:, hs(h)], kch) * scale, mask, vch)
        stats = finish(ow_ref, 2, stats)
        st_ref[...] = stats

        g = _sigmoid(br_ref[...])
        for h in range(H):
            o_ref[:, hs(h)] = (g[:, 3 * h:3 * h + 1] * oc_ref[:, hs(h)] + g[:, 3 * h + 1:3 * h + 2] * os_ref[:, hs(h)]
                               + g[:, 3 * h + 2:3 * h + 3] * ow_ref[:, hs(h)])

    blk = lambda w: pl.BlockSpec((b, w), lambda i: (i, 0))
    whole = lambda a: pl.BlockSpec(a.shape, lambda i: (0,) * a.ndim)
    return pl.pallas_call(
        body, out_shape=tuple(_SDS((s, GROUP), f32) for _ in range(4)) + (_SDS((s, LANES), f32), _SDS((s, LANES), f32)),
        grid=(nq,),
        in_specs=[blk(GROUP), whole(kvc), whole(ksr), whole(vs), whole(kwr), whole(vw),
                  pl.BlockSpec((b, LANES), lambda i: (i, bcol)), whole(m2s), whole(e3)],
        out_specs=(blk(GROUP),) * 4 + (blk(LANES), blk(LANES)),
        scratch_shapes=[pltpu.VMEM((H, b, 1), f32), pltpu.VMEM((H, b, 1), f32), pltpu.VMEM((H, b, HEAD_DIM), f32)],
        compiler_params=_cp(("parallel",)), name=name,
    )(qr, kvc, ksr, vs, kwr, vw, z_all, m2s, e3)


def _nsa_bwd(do, qr, kvc, ksr, vs, kwr, vw, z_all, oc, os_, ow, stats, sel, e3, *, name):
    s = qr.shape[0]
    b = _attn_block(s)
    nq = s // b
    nr = kvc.shape[1]
    nd = -(-WINDOW // b)
    scale = HEAD_DIM ** -0.5
    bcol = _AL["nsa_branch"] // LANES
    H = N_HEADS

    def body(do_ref, q_ref, kvc_ref, ks_ref, vs_ref, kw_ref, vw_ref, br_ref, oc_ref, os_ref, ow_ref, st_ref, sel_ref,
             e3_ref, dq_ref, dbr_ref, dkvc_ref, dks_ref, dvs_ref, dkw_ref, dvw_ref, dob_sc, delta_sc, dq_sc):
        i = pl.program_id(0)

        @pl.when(i == 0)
        def _():
            for r in (dkvc_ref, dks_ref, dvs_ref, dkw_ref, dvw_ref):
                r[...] = jnp.zeros_like(r)

        lane = _iota((b, LANES), 1)
        hs = lambda h: slice(h * HEAD_DIM, (h + 1) * HEAD_DIM)
        g = _sigmoid(br_ref[...])
        stats = st_ref[...]
        dbr = jnp.zeros((b, LANES), f32)
        outs = (oc_ref, os_ref, ow_ref)
        for h in range(H):
            doh = do_ref[:, hs(h)]
            for j in range(3):
                gj = g[:, 3 * h + j:3 * h + j + 1]
                dgj = jnp.sum(doh * outs[j][:, hs(h)], axis=1, keepdims=True)
                dbr = jnp.where(lane == 3 * h + j, dgj * gj * (1.0 - gj), dbr)
                dob_sc[j, :, hs(h)] = gj * doh
                delta_sc[j, h] = gj * dgj
        dbr_ref[...] = dbr
        dq_sc[...] = jnp.zeros_like(dq_sc)

        def branch(j, h, z, mask, kch, vch):
            qh = q_ref[:, hs(h)]
            p = jnp.where(mask, jnp.exp(jnp.where(mask, z, NEG_INF) - stats[:, 4 * j + h:4 * j + h + 1]), 0.0)
            dob = dob_sc[j, :, hs(h)]
            ds = p * (_mm_nt(dob, vch) - delta_sc[j, h])
            dq_sc[:, hs(h)] += _mm(ds, kch) * scale
            return _mm_tn(ds, qh) * scale, _mm_tn(p, dob)

        cmp_mask = (CMP_STRIDE * _iota((b, nr), 1) + (CMP_LEN - 1)) <= (i * b + _iota((b, nr), 0))
        kc, vc = kvc_ref[0], kvc_ref[1]
        for h in range(H):
            dk, dv = branch(0, h, _mm_nt(q_ref[:, hs(h)], kc) * scale, cmp_mask, kc, vc)
            dkvc_ref[0] += dk
            dkvc_ref[1] += dv

        sel_b = sel_ref[...].astype(_MXU)

        def chunk(j, c, mask, k_ref, v_ref, dk_ref, dv_ref):
            st = pl.multiple_of(c * b, b)
            kch, vch = k_ref[pl.ds(st, b), :], v_ref[pl.ds(st, b), :]
            dk = jnp.zeros((b, HEAD_DIM), f32)
            dv = jnp.zeros((b, HEAD_DIM), f32)
            for h in range(H):
                dkh, dvh = branch(j, h, _mm_nt(q_ref[:, hs(h)], kch) * scale, mask, kch, vch)
                dk, dv = dk + dkh, dv + dvh
            dk_ref[pl.ds(st, b), :] += dk
            dv_ref[pl.ds(st, b), :] += dv

        def sel_chunk(c, diag):
            mask = _mm(sel_b, e3_ref[c]) > 0.5
            if diag:
                mask = mask & _lower_mask(b, False)
            chunk(1, c, mask, ks_ref, vs_ref, dks_ref, dvs_ref)

        def sel_loop(c, carry):
            sel_chunk(c, False)
            return carry

        lax.fori_loop(0, i, sel_loop, 0)
        sel_chunk(i, True)

        for d in range(nd, -1, -1):
            @pl.when(i >= d)
            def _():
                chunk(2, i - d, _nsa_masks(i, b, d), kw_ref, vw_ref, dkw_ref, dvw_ref)

        dq_ref[...] = dq_sc[...]

    blk = lambda w: pl.BlockSpec((b, w), lambda i: (i, 0))
    whole = lambda a: pl.BlockSpec(a.shape, lambda i: (0,) * a.ndim)
    stream = _SDS((s, HEAD_DIM), f32)
    return pl.pallas_call(
        body, out_shape=(_SDS((s, GROUP), f32), _SDS((s, LANES), f32), _SDS(kvc.shape, f32), stream, stream, stream,
                         stream),
        grid=(nq,),
        in_specs=[blk(GROUP), blk(GROUP), whole(kvc), whole(ksr), whole(vs), whole(kwr), whole(vw),
                  pl.BlockSpec((b, LANES), lambda i: (i, bcol)), blk(GROUP), blk(GROUP), blk(GROUP), blk(LANES),
                  blk(LANES), whole(e3)],
        out_specs=(blk(GROUP), blk(LANES), whole(kvc), whole(ksr), whole(vs), whole(kwr), whole(vw)),
        scratch_shapes=[pltpu.VMEM((3, b, GROUP), f32), pltpu.VMEM((3, H, b, 1), f32), pltpu.VMEM((b, GROUP), f32)],
        compiler_params=_cp(("arbitrary",)), name=name,
    )(do, qr, kvc, ksr, vs, kwr, vw, z_all, oc, os_, ow, stats, sel, e3)


def _seg(a, name):
    parts = [lax.slice_in_dim(a, off, off + hi - lo, axis=a.ndim - 1) for off, lo, hi in _PIECES[name]]
    return parts[0] if len(parts) == 1 else jnp.concatenate(parts, axis=a.ndim - 1)


def _to_groups(segs, rows, dtype):
    cols = []
    for s, grp in enumerate(_GROUPS):
        at = 0
        for n, lo, hi, off in sorted(grp, key=lambda t: t[3]):
            if off > at:
                cols.append(jnp.zeros((rows, off - at), dtype))
            cols.append(segs[n][:, lo:hi].astype(dtype))
            at = off + hi - lo
        if at < GROUP_W:
            cols.append(jnp.zeros((rows, GROUP_W - at), dtype))
    return jnp.concatenate(cols, axis=1)


def _piece_from_shard(w_t, s):
    grp = sorted(_GROUPS[s], key=lambda t: t[3])
    ends = [t[3] for t in grp[1:]] + [GROUP_W]
    rows = []
    for (n, lo, hi, off), end in zip(grp, ends):
        first = _ORIG[n] + lo - s * CHIP_COLS
        rows.append(jnp.pad(w_t[:, first:first + hi - lo], ((0, 0), (0, end - off - (hi - lo)), (0, 0))))
    return jnp.concatenate(rows, axis=1)


def _shard_from_piece(g, s):
    return jnp.concatenate([g[:, off:off + hi - lo] for n, lo, hi, off in
                            sorted(_GROUPS[s], key=lambda t: _ORIG[t[0]] + t[1])], axis=1)


def _from_groups(a):
    return jnp.concatenate([_seg(a, n) for n, _ in _SEGS], axis=1)


def _cmp_rows(tok):
    s = tok.shape[0]
    r = tok.reshape(s // CMP_STRIDE, CMP_STRIDE * HEAD_DIM)
    return r, jnp.concatenate([r[1:], jnp.zeros((1, r.shape[1]), r.dtype)], axis=0)


def _cmp_unrows(dxa, dxb):
    s = dxa.shape[0] * CMP_STRIDE
    return (dxa + jnp.concatenate([jnp.zeros((1, dxa.shape[1]), dxa.dtype), dxb[:-1]], axis=0)).reshape(s, HEAD_DIM)


_GATES = ("sb_gate", "nsa_gate", "fox_gate", "mla_gate")


def _layer_fwd(x, p, c, tag):
    s = x.shape[0]
    b = _attn_block(s)
    h = _rms_fwd(x, p["pre_g"], out_dtype=_MXU, name=f"prenorm_{tag}")
    z = _matmul(h, p["w_in"], "nt", bias=p["b_in"], layer=p["layer"], name=f"inproj_{tag}")
    o_sb = _sb_fwd(z, name=f"sb_fwd_{tag}")

    qr, ksr, kwr = _rope_call([(z, GROUP, _AL["nsa_q"] // GROUP), (z, LANES, _AL["nsa_k_sel"] // LANES),
                               (z, LANES, _AL["nsa_k_win"] // LANES)], c["tabs128"], HEAD_DIM // 2, False,
                              name=f"nsa_rope_{tag}")
    (rak, rbk), (rav, rbv) = _cmp_rows(_seg(z, "nsa_k_cmp")), _cmp_rows(_seg(z, "nsa_v_cmp"))
    ra, rb_ = jnp.stack([rak, rav]), jnp.stack([rbk, rbv])
    kvc, hp = _nsa_cmp_fwd(ra, rb_, p["cmp_pos"], p["cmp_w1"], p["cmp_w2"], c["tabs_cmp"], name=f"nsa_cmp_{tag}")
    vs, vw = _seg(z, "nsa_v_sel"), _seg(z, "nsa_v_win")
    o_nsa, oc, os_, ow, stats, sel = _nsa_fwd(qr, kvc, ksr, vs, kwr, vw, z, c["m2s"], c["e3"], name=f"nsa_fwd_{tag}")

    cum, cum_t8 = _fox_cum_fwd(z, p["fox_bias"], name=f"fox_cum_{tag}")
    cum_t = cum_t8.reshape(8, s // b, 1, b)
    fox_v = _seg(z, "fox_v")
    fcols = (_AL["fox_q"] // HEAD_DIM, _AL["fox_k"] // HEAD_DIM, 0)
    o_fox, lse_fox = _attn_fwd(z, z, fox_v, *fcols, HEAD_DIM, cum, cum_t, scale=HEAD_DIM ** -0.5,
                               name=f"fox_fwd_{tag}")

    qcat, kcat, vm = _mla_prep_fwd(z, p["gq"], p["gkv"], p["wuq"], p["wk"], p["wv"], c["tabs64"],
                                   name=f"mla_prep_{tag}")
    o_mla, lse_mla = _attn_fwd(qcat, kcat, vm, 0, 0, 0, MLA_QW, None, None, scale=(MLA_NOPE + MLA_ROPE) ** -0.5,
                               name=f"mla_fwd_{tag}")

    o_all = jnp.concatenate([o_sb, o_nsa, o_fox, o_mla], axis=1)
    gates = jnp.concatenate([_seg(z, n) for n in _GATES], axis=1)
    mix = _gate_fwd(o_all, gates, name=f"gate_{tag}")
    u = _matmul(mix, p["w_out"], "nn", name=f"outproj_{tag}")
    y = _postnorm_fwd(u, p["post_g"], x, name=f"postnorm_{tag}")
    saved = dict(x=x, h=h, z=z, qr=qr, ksr=ksr, kwr=kwr, ra=ra, rb=rb_, kvc=kvc, hp=hp, vs=vs, vw=vw, oc=oc, os=os_,
                 ow=ow, stats=stats, sel=sel, cum=cum, cum_t=cum_t, fox_v=fox_v, o_fox=o_fox, lse_fox=lse_fox, qcat=qcat, kcat=kcat,
                 vm=vm, o_mla=o_mla, lse_mla=lse_mla, o_all=o_all, gates=gates, mix=mix, u=u)
    return y, saved


def _layer_bwd(dy, sv, p, c, tag):
    z = sv["z"]
    s = z.shape[0]
    du, dg_post = _rms_bwd(dy, sv["u"], p["post_g"], name=f"postnorm_bwd_{tag}")
    dmix = _matmul(du, p["w_out"], "nt", name=f"outproj_dx_{tag}")
    dw_out = _matmul(sv["mix"], du, "tn", name=f"outproj_dw_{tag}")
    do_all, dgates = _gate_bwd(dmix, sv["o_all"], sv["gates"], name=f"gate_bwd_{tag}")
    do_sb, do_nsa, do_fox, do_mla = (do_all[:, k * GROUP:(k + 1) * GROUP] for k in range(4))
    dgate = [dgates[:, k * GROUP:(k + 1) * GROUP] for k in range(4)]

    sb_dq, sb_dk, sb_dv = _sb_bwd(z, do_sb, name=f"sb_bwd_{tag}")

    n_dq, n_dbr, n_dkvc, n_dks, n_dvs, n_dkw, n_dvw = _nsa_bwd(
        do_nsa, sv["qr"], sv["kvc"], sv["ksr"], sv["vs"], sv["kwr"], sv["vw"], z, sv["oc"], sv["os"], sv["ow"],
        sv["stats"], sv["sel"], c["e3"], name=f"nsa_bwd_{tag}")
    dxa, dxb, dw1, dw2 = _nsa_cmp_bwd(sv["ra"], sv["rb"], p["cmp_pos"], p["cmp_w1"], p["cmp_w2"], c["tabs_cmp"],
                                      sv["hp"], n_dkvc, name=f"nsa_cmp_bwd_{tag}")
    n_dq, n_dks, n_dkw = _rope_call([(n_dq, GROUP, 0), (n_dks, LANES, 0), (n_dkw, LANES, 0)], c["tabs128"],
                                    HEAD_DIM // 2, True, name=f"nsa_rope_bwd_{tag}")
    dpos = _colsum(jnp.concatenate([dxa[0], dxb[0], dxa[1], dxb[1]], axis=1), name=f"nsa_dpos_{tag}")
    flat = CMP_LEN * HEAD_DIM

    fcols = (_AL["fox_q"] // HEAD_DIM, _AL["fox_k"] // HEAD_DIM, 0)
    f_dq, f_dk, f_dv, f_dck = _attn_bwd(z, z, sv["fox_v"], *fcols, HEAD_DIM, do_fox, sv["o_fox"], sv["lse_fox"],
                                        sv["cum"], sv["cum_t"], scale=HEAD_DIM ** -0.5, name=f"fox_bwd_{tag}")
    dcum_t = jnp.pad(f_dck.reshape(N_HEADS, s), ((0, 8 - N_HEADS), (0, 0)))
    f_df, f_dbias = _fox_cum_bwd(z, p["fox_bias"], dcum_t, name=f"fox_cum_bwd_{tag}")

    m_dq, m_dk, m_dv = _attn_bwd(sv["qcat"], sv["kcat"], sv["vm"], 0, 0, 0, MLA_QW, do_mla, sv["o_mla"], sv["lse_mla"],
                                 None, None, scale=(MLA_NOPE + MLA_ROPE) ** -0.5, name=f"mla_bwd_{tag}")
    m_dcq, m_dckv, m_dkr, m_dwuq, m_dwk, m_dwv, m_dgq, m_dgkv = _mla_prep_bwd(
        z, p["gq"], p["gkv"], p["wuq"], p["wk"], p["wv"], c["tabs64"], m_dq, m_dk, m_dv, name=f"mla_prep_bwd_{tag}")

    dz = _to_groups(dict(
        sb_q=sb_dq, sb_k=sb_dk, sb_v=sb_dv, sb_gate=dgate[0], nsa_q=n_dq, nsa_k_cmp=_cmp_unrows(dxa[0], dxb[0]),
        nsa_v_cmp=_cmp_unrows(dxa[1], dxb[1]), nsa_k_sel=n_dks, nsa_v_sel=n_dvs, nsa_k_win=n_dkw, nsa_v_win=n_dvw,
        nsa_branch=n_dbr, nsa_gate=dgate[1], fox_q=f_dq, fox_k=f_dk, fox_v=f_dv, fox_f=f_df, fox_gate=dgate[2],
        mla_cq=m_dcq, mla_ckv=m_dckv, mla_k_rope=m_dkr, mla_gate=dgate[3]), s, f32)
    dh = _matmul(dz, p["w_in"], "nn", layer=p["layer"], name=f"inproj_dx_{tag}")
    dw_in = _matmul(sv["h"], dz, "tn", name=f"inproj_dw_{tag}")
    db = _colsum(dz, name=f"inproj_db_{tag}")
    dx, dg_pre = _rms_bwd(dh, sv["x"], p["pre_g"], res=dy, name=f"prenorm_bwd_{tag}")

    qw = MLA_NOPE + MLA_ROPE
    grads = {
        "pre_norm_g": dg_pre[0], "post_norm_g": dg_post[0], "w_in": dw_in, "b_in": _from_groups(db)[0],
        "w_out": dw_out, "fox_forget_bias": f_dbias[0, :N_HEADS],
        "nsa_cmp_pos_k": dpos[0, :flat].reshape(CMP_LEN, HEAD_DIM), "nsa_cmp_w1_k": dw1[0], "nsa_cmp_w2_k": dw2[0],
        "nsa_cmp_pos_v": dpos[0, flat:].reshape(CMP_LEN, HEAD_DIM), "nsa_cmp_w1_v": dw1[1], "nsa_cmp_w2_v": dw2[1],
        "mla_q_norm_g": m_dgq[0],
        "mla_w_uq": jnp.concatenate([m_dwuq[:, MLA_QW * h:MLA_QW * h + qw] for h in range(N_HEADS)], axis=1),
        "mla_kv_norm_g": m_dgkv[0],
        "mla_w_ukv": jnp.concatenate(sum([[m_dwk[:, LANES * h:LANES * (h + 1)], m_dwv[:, LANES * h:LANES * (h + 1)]]
                                          for h in range(N_HEADS)], []), axis=1),
    }
    return dx, grads


def _layer_params(w, l):
    b_in = w["b_in"][l].reshape(1, -1)
    b_segs = {n: b_in[:, _ORIG[n]:_ORIG[n] + wd] for n, wd in _SEGS}
    qw = MLA_NOPE + MLA_ROPE
    w_uq, w_ukv = w["mla_w_uq"][l], w["mla_w_ukv"][l]
    uq = []
    for h in range(N_HEADS):
        uq += [w_uq[:, qw * h:qw * (h + 1)], jnp.zeros((w_uq.shape[0], MLA_QW - qw), w_uq.dtype)]
    kw_ = 2 * LANES
    flat = CMP_LEN * HEAD_DIM
    return dict(
        pre_g=w["pre_norm_g"][l].reshape(1, -1), post_g=w["post_norm_g"][l].reshape(1, -1),
        w_in=w["w_in"], layer=l, b_in=_to_groups(b_segs, 1, f32), w_out=w["w_out"][l],
        fox_bias=jnp.pad(w["fox_forget_bias"][l], (0, LANES - N_HEADS)).reshape(1, LANES),
        cmp_pos=jnp.stack([w["nsa_cmp_pos_k"][l].reshape(1, flat), w["nsa_cmp_pos_v"][l].reshape(1, flat)]),
        cmp_w1=jnp.stack([w["nsa_cmp_w1_k"][l], w["nsa_cmp_w1_v"][l]]),
        cmp_w2=jnp.stack([w["nsa_cmp_w2_k"][l], w["nsa_cmp_w2_v"][l]]),
        gq=w["mla_q_norm_g"][l].reshape(1, -1), gkv=w["mla_kv_norm_g"][l].reshape(1, -1),
        wuq=jnp.concatenate(uq, axis=1),
        wk=jnp.concatenate([w_ukv[:, kw_ * h:kw_ * h + LANES] for h in range(N_HEADS)], axis=1),
        wv=jnp.concatenate([w_ukv[:, kw_ * h + LANES:kw_ * (h + 1)] for h in range(N_HEADS)], axis=1),
    )


def _consts(s):
    pos = jnp.arange(s)
    m2s, e3 = _nsa_consts(s)
    return dict(tabs128=_rope_tables(pos, HEAD_DIM), tabs64=_rope_tables(pos, MLA_ROPE),
                tabs_cmp=_rope_tables(jnp.arange(s // CMP_STRIDE) * CMP_STRIDE + (CMP_LEN - 1), HEAD_DIM),
                m2s=m2s, e3=e3)


def _place():
    return lax.axis_index("x"), lax.axis_index("y"), lax.axis_index("c")


def _other_chips(x, y):
    return [(1 - x, y), (x, 1 - y), (1 - x, 1 - y)]


def _comm_call(body, out_shapes, n_sems, arrs, name):
    return pl.pallas_call(body, out_shape=tuple(out_shapes), in_specs=[_ANY] * len(arrs),
                          out_specs=tuple(_ANY for _ in out_shapes),
                          scratch_shapes=[pltpu.SemaphoreType.DMA((n_sems,)), pltpu.SemaphoreType.DMA((n_sems,))],
                          name=name)(*arrs)


def _gather_chips(arrs, *, name):
    n = len(arrs)

    def body(*refs):
        a_refs, out_refs, send_sems, recv_sems = refs[:n], refs[n:2 * n], refs[2 * n], refs[2 * n + 1]
        x, y, c = _place()
        me = 2 * x + y
        sibling = (x, y, 1 - c)
        chips = _other_chips(x, y)

        def copy(j, k, src, dst, to):
            return pltpu.make_async_remote_copy(src, dst, send_sems.at[6 * j + k], recv_sems.at[6 * j + k],
                                                device_id=to, device_id_type=_MESH)

        first = [copy(j, k, a_refs[j].at[c], out_refs[j].at[me, c], (px, py, c))
                 for k, (px, py) in enumerate(chips) for j in range(n)]
        for cp in first:
            cp.start()
        passed = []
        for k, (px, py) in enumerate(chips):
            for j in range(n):
                landed = out_refs[j].at[2 * px + py, c]
                copy(j, k, a_refs[j].at[c], landed, (px, py, c)).wait_recv()
                passed.append(copy(j, 3 + k, landed, landed, sibling))
                passed[-1].start()
        for k, (px, py) in enumerate(chips):
            for j in range(n):
                copy(j, 3 + k, a_refs[j].at[c], out_refs[j].at[2 * px + py, 1 - c], sibling).wait_recv()
        for cp in first + passed:
            cp.wait_send()

    return _comm_call(body, [_SDS((N_CHIPS,) + a.shape, a.dtype) for a in arrs], 6 * n, arrs, name)


def _alltoall_chips(arrs, lane_slots, *, name):
    n = len(arrs)

    def slot(ref, lanes, s):
        if lanes:
            w = ref.shape[2] // N_CHIPS
            return ref.at[0, :, pl.ds(s * w, w)]
        return ref.at[s]

    def body(*refs):
        g_refs, out_refs, send_sems, recv_sems = refs[:n], refs[n:2 * n], refs[2 * n], refs[2 * n + 1]
        x, y, c = _place()
        me = 2 * x + y

        def copy(j, s):
            return pltpu.make_async_remote_copy(slot(g_refs[j], lane_slots[j], s), out_refs[j].at[me],
                                                send_sems.at[N_CHIPS * j + s], recv_sems.at[N_CHIPS * j + me],
                                                device_id=(s // 2, s % 2, c), device_id_type=_MESH)

        for s in range(N_CHIPS):
            @pl.when(s != me)
            def _():
                for j in range(n):
                    copy(j, s).start()
        for t in range(N_CHIPS):
            @pl.when(t != me)
            def _():
                for j in range(n):
                    pltpu.make_async_remote_copy(slot(g_refs[j], lane_slots[j], t), out_refs[j].at[t],
                                                 send_sems.at[N_CHIPS * j + t], recv_sems.at[N_CHIPS * j + t],
                                                 device_id=(t // 2, t % 2, c), device_id_type=_MESH).wait_recv()
        for s in range(N_CHIPS):
            @pl.when(s != me)
            def _():
                for j in range(n):
                    copy(j, s).wait_send()

    outs = [_SDS((N_CHIPS, a.shape[1], a.shape[2] // N_CHIPS if lanes else a.shape[2]), a.dtype)
            for a, lanes in zip(arrs, lane_slots)]
    return _comm_call(body, outs, N_CHIPS * n, arrs, name)


def _swap_other_half(arrs, *, name):
    n = len(arrs)

    def body(*refs):
        g_refs, out_refs, send_sems, recv_sems = refs[:n], refs[n:2 * n], refs[2 * n], refs[2 * n + 1]
        x, y, c = _place()
        cps = [pltpu.make_async_remote_copy(g_refs[j].at[:, 1 - c], out_refs[j], send_sems.at[j], recv_sems.at[j],
                                            device_id=(x, y, 1 - c), device_id_type=_MESH) for j in range(n)]
        for cp in cps:
            cp.start()
        for cp in cps:
            cp.wait()

    return _comm_call(body, [_SDS((a.shape[0],) + a.shape[2:], a.dtype) for a in arrs], n, arrs, name)


def _swap_sibling(arrs, *, name):
    n = len(arrs)

    def body(*refs):
        f_refs, out_refs, send_sems, recv_sems = refs[:n], refs[n:2 * n], refs[2 * n], refs[2 * n + 1]
        x, y, c = _place()
        cps = [pltpu.make_async_remote_copy(f_refs[j], out_refs[j], send_sems.at[j], recv_sems.at[j],
                                            device_id=(x, y, 1 - c), device_id_type=_MESH) for j in range(n)]
        for cp in cps:
            cp.start()
        for cp in cps:
            cp.wait()

    return _comm_call(body, [_SDS(a.shape, a.dtype) for a in arrs], n, arrs, name)


def _gather_all(a, *, name):
    def body(a_ref, out_ref, send_sems, recv_sems, local_sem):
        x, y, c = _place()
        flip = lambda v, f: (1 - v) if f else v
        peers = [(flip(x, f & 4), flip(y, f & 2), flip(c, f & 1)) for f in range(1, 8)]
        me = 4 * x + 2 * y + c
        mine = pltpu.make_async_copy(a_ref, out_ref.at[me], local_sem)
        mine.start()
        sends = [pltpu.make_async_remote_copy(a_ref, out_ref.at[me], send_sems.at[k], recv_sems.at[k], device_id=peer,
                                              device_id_type=_MESH) for k, peer in enumerate(peers)]
        for cp in sends:
            cp.start()
        for k, (px, py, pc) in enumerate(peers):
            pltpu.make_async_remote_copy(a_ref, out_ref.at[4 * px + 2 * py + pc], send_sems.at[k], recv_sems.at[k],
                                         device_id=(px, py, pc), device_id_type=_MESH).wait_recv()
        for cp in sends:
            cp.wait_send()
        mine.wait()

    return pl.pallas_call(body, out_shape=_SDS((8,) + a.shape, a.dtype), in_specs=[_ANY], out_specs=_ANY,
                          scratch_shapes=[pltpu.SemaphoreType.DMA((7,)), pltpu.SemaphoreType.DMA((7,)),
                                          pltpu.SemaphoreType.DMA], name=name)(a)


def _add_my_half(g, r, *, name):
    p, _, h, w = g.shape
    tw = _pick(w, (2048, 1024, 512, 256, 128))
    rb = max(d for d in range(16, h + 1, 16) if h % d == 0 and d * tw * 4 <= (2 << 20))

    def body(c_ref, g_ref, r_ref, o_ref):
        o_ref[...] = (g_ref[...] + r_ref[...]).astype(o_ref.dtype)

    blk = pl.BlockSpec((None, rb, tw), lambda s, i, j, c_ref: (s, i, j))
    grid_spec = pltpu.PrefetchScalarGridSpec(
        num_scalar_prefetch=1, grid=(p, h // rb, w // tw),
        in_specs=[pl.BlockSpec((None, None, rb, tw), lambda s, i, j, c_ref: (s, c_ref[0], i, j)), blk], out_specs=blk)
    c = lax.axis_index("c").astype(jnp.int32).reshape(1)
    return pl.pallas_call(body, out_shape=_SDS((p, h, w), _WIRE), grid_spec=grid_spec,
                          compiler_params=_cp(("parallel", "parallel", "parallel")), name=name)(c, g, r)


_WEIGHTS = ("pre_norm_g", "post_norm_g", "w_in", "b_in", "w_out", "fox_forget_bias", "nsa_cmp_pos_k", "nsa_cmp_w1_k",
            "nsa_cmp_w2_k", "nsa_cmp_pos_v", "nsa_cmp_w1_v", "nsa_cmp_w2_v", "mla_q_norm_g", "mla_w_uq",
            "mla_kv_norm_g", "mla_w_ukv")
_SHARD_AXIS = {"w_in": 2, "w_out": 1, "nsa_cmp_w1_k": 1, "nsa_cmp_w1_v": 1, "mla_w_uq": 2, "mla_w_ukv": 2}
_PACK_UNIT = 16 * LANES


def _pack(arrays, dtype):
    rows = []
    for a in arrays:
        v = a.astype(dtype).reshape(-1)
        pad = (-v.shape[0]) % _PACK_UNIT
        if pad:
            v = jnp.concatenate([v, jnp.zeros((pad,), dtype)])
        rows.append(v.reshape(-1, LANES))
    return jnp.concatenate(rows, axis=0)


def _unpack(flat, shapes):
    out, r = [], 0
    for shp in shapes:
        n = int(np.prod(shp))
        nr = -(-n // _PACK_UNIT) * (_PACK_UNIT // LANES)
        out.append(flat[r:r + nr].reshape(-1)[:n].reshape(shp))
        r += nr
    return out


def kernel(x, pre_norm_g, post_norm_g, w_in, b_in, w_out, fox_forget_bias, nsa_cmp_pos_k, nsa_cmp_w1_k, nsa_cmp_w2_k, nsa_cmp_pos_v, nsa_cmp_w1_v, nsa_cmp_w2_v, mla_q_norm_g, mla_w_uq, mla_kv_norm_g, mla_w_ukv, loss_target, m_pre_norm_g, m_post_norm_g, m_w_in, m_b_in, m_w_out, m_fox_forget_bias, m_nsa_cmp_pos_k, m_nsa_cmp_w1_k, m_nsa_cmp_w2_k, m_nsa_cmp_pos_v, m_nsa_cmp_w1_v, m_nsa_cmp_w2_v, m_mla_q_norm_g, m_mla_w_uq, m_mla_kv_norm_g, m_mla_w_ukv, v_pre_norm_g, v_post_norm_g, v_w_in, v_b_in, v_w_out, v_fox_forget_bias, v_nsa_cmp_pos_k, v_nsa_cmp_w1_k, v_nsa_cmp_w2_k, v_nsa_cmp_pos_v, v_nsa_cmp_w1_v, v_nsa_cmp_w2_v, v_mla_q_norm_g, v_mla_w_uq, v_mla_kv_norm_g, v_mla_w_ukv):
    given = dict(locals())
    local = {n: given[n] for n in _WEIGHTS}
    depth = pre_norm_g.shape[0]
    xs, target = x[0], loss_target[0]
    s = xs.shape[0]
    sharded = [n for n in _WEIGHTS if n in _SHARD_AXIS and n != "w_in"]
    small = [n for n in _WEIGHTS if n not in _SHARD_AXIS]
    chip = 2 * lax.axis_index("x") + lax.axis_index("y")
    core = lax.axis_index("c")
    own = lambda slots, mine: lax.dynamic_update_slice_in_dim(slots, mine[None], chip, axis=0)

    w_in_t = jnp.swapaxes(w_in, 1, 2).astype(_MXU)
    piece = lax.switch(chip, [functools.partial(_piece_from_shard, s=k) for k in range(N_CHIPS)], w_in_t)
    shard_shapes = [local[n].shape for n in sharded]
    flat = _pack([local[n] for n in sharded], _MXU)
    flat2 = flat.reshape((2, -1, LANES))
    w_in_all, flat_all = _gather_chips([piece, flat2], name="gather_weights")
    w_in_all = own(w_in_all, piece)
    flat_all = own(flat_all, flat2).reshape((N_CHIPS,) + flat.shape)
    per_chip = [_unpack(flat_all[k], shard_shapes) for k in range(N_CHIPS)]
    full = dict(local)
    full["w_in"] = w_in_all
    for j, n in enumerate(sharded):
        full[n] = jnp.concatenate([per_chip[k][j] for k in range(N_CHIPS)], axis=_SHARD_AXIS[n])

    consts = _consts(s)
    params = [_layer_params(full, l) for l in range(depth)]
    act, saved = xs, []
    for l in range(depth):
        act, sv = _layer_fwd(act, params[l], consts, f"l{l}")
        saved.append(sv)
    dy, loss_parts = _loss_head(act, target, name="loss_head")
    layer_grads = [None] * depth
    for l in reversed(range(depth)):
        dy, layer_grads[l] = _layer_bwd(dy, saved[l], params[l], consts, f"l{l}")
    grad_x = dy[None]
    grads = {n: jnp.stack([layer_grads[l][n] for l in range(depth)]) for n in _WEIGHTS if n != "w_in"}

    def chip_slice(n, k):
        a, ax = grads[n], _SHARD_AXIS[n]
        w = a.shape[ax] // N_CHIPS
        return lax.slice_in_dim(a, k * w, (k + 1) * w, axis=ax)

    g_flat = jnp.stack([_pack([chip_slice(n, k) for n in sharded], f32) for k in range(N_CHIPS)])
    halves = [layer_grads[l]["w_in"].reshape(1, 2, D_MODEL // 2, ZW) for l in range(depth)]
    halves.append(g_flat.reshape(N_CHIPS, 2, -1, LANES))
    lane_slots = [True] * depth + [False]
    from_sibling = _swap_other_half(halves, name="reduce_pair")
    pair_sum = [_add_my_half(g, r, name=f"reduce_pair_add{j}") for j, (g, r) in enumerate(zip(halves, from_sibling))]
    from_chips = _alltoall_chips(pair_sum, lane_slots, name="reduce_chips")
    my_half = []
    for j, (slots, ps, lanes) in enumerate(zip(from_chips, pair_sum, lane_slots)):
        mine = lax.dynamic_slice_in_dim(ps[0], chip * GROUP_W, GROUP_W, axis=1) if lanes else \
            lax.dynamic_index_in_dim(ps, chip, axis=0, keepdims=False)
        my_half.append(_sum_slots(own(slots, mine), name=f"reduce_chips_add{j}"))
    their_half = _swap_sibling(my_half, name="reduce_share")
    first = core == 0
    both = [jnp.concatenate([jnp.where(first, a, b), jnp.where(first, b, a)], axis=0)
            for a, b in zip(my_half, their_half)]
    summed = dict(zip(sharded, _unpack(both[depth], shard_shapes)))
    unpiece = [functools.partial(_shard_from_piece, s=k) for k in range(N_CHIPS)]
    summed["w_in"] = jnp.stack([lax.switch(chip, unpiece, both[l]) for l in range(depth)])

    loss_row = jnp.concatenate([jnp.sum(loss_parts).reshape(1), jnp.zeros((LANES - 1,), f32)])
    small_shapes = [(LANES,)] + [grads[n].shape for n in small]
    contrib = _pack([loss_row] + [grads[n] for n in small], f32)
    pad_rows = (-contrib.shape[0]) % 8
    if pad_rows:
        contrib = jnp.concatenate([contrib, jnp.zeros((pad_rows, LANES), f32)], axis=0)
    total = _unpack(_sum_slots(_gather_all(contrib, name="gather_small"), name="sum_small"), small_shapes)
    loss = total[0][0]
    summed.update(zip(small, total[1:]))

    deltas, new_m, new_v = {}, {}, {}
    for n in _WEIGHTS:
        deltas[n], new_m[n], new_v[n] = _adamw(local[n], summed[n], given["m_" + n], given["v_" + n], name=f"adamw_{n}")
    return (loss, grad_x, *[summed[n] for n in _WEIGHTS], *[deltas[n] for n in _WEIGHTS],
            *[new_m[n] for n in _WEIGHTS], *[new_v[n] for n in _WEIGHTS])
```

```python
import functools
import math

import numpy as np
import jax
import jax.numpy as jnp
from jax import lax
from jax.experimental import pallas as pl
from jax.experimental.pallas import tpu as pltpu

f32 = jnp.float32
bf16 = jnp.bfloat16
_MXU = jnp.bfloat16
_WIRE = jnp.bfloat16
_SDS = jax.ShapeDtypeStruct
_ANY = pl.BlockSpec(memory_space=pl.ANY)
_MESH = pl.DeviceIdType.MESH

D_MODEL = 2048
N_HEADS = 4
HEAD_DIM = 128
GROUP = 512
RMS_EPS = 1e-6
NEG_INF = -1e30
ROPE_THETA = 10000.0
CMP_LEN, CMP_STRIDE, SEL_LEN, SEL_TOPN, WINDOW = 32, 16, 64, 16, 512
FORCED_BONUS = 1e6
MLA_Q_RANK, MLA_KV_RANK, MLA_NOPE, MLA_ROPE = 384, 128, 128, 64
ADAM_LR, ADAM_B1, ADAM_B2, ADAM_EPS, ADAM_WD, ADAM_STEP = 0.001, 0.9, 0.999, 1e-08, 0.01, 10
LANES = 128
VMEM_LIMIT = 48 * 1024 * 1024
HP_FWD, HP_BWD = 4, 2

_SEGS = (
    ("sb_q", 512), ("sb_k", 512), ("sb_v", 512), ("sb_gate", 512), ("nsa_q", 512), ("nsa_k_cmp", 128),
    ("nsa_v_cmp", 128), ("nsa_k_sel", 128), ("nsa_v_sel", 128), ("nsa_k_win", 128), ("nsa_v_win", 128),
    ("nsa_branch", 12), ("nsa_gate", 512), ("fox_q", 512), ("fox_k", 512), ("fox_v", 512), ("fox_f", 4),
    ("fox_gate", 512), ("mla_cq", 384), ("mla_ckv", 128), ("mla_k_rope", 64), ("mla_gate", 512),
)
_ORIG, _WID = {}, {}
_o = 0
for _n, _w in _SEGS:
    _ORIG[_n], _WID[_n] = _o, _w
    _o += _w
IN_WIDTH = _o
N_CHIPS = 4
CHIP_COLS = IN_WIDTH // N_CHIPS
GROUP_W = 2048
ZW = N_CHIPS * GROUP_W
_GROUPS = (
    (("sb_q", 0, 512, 0), ("sb_k", 0, 512, 512), ("sb_v", 0, 512, 1024), ("sb_gate", 0, 212, 1536)),
    (("nsa_q", 0, 512, 0), ("nsa_k_cmp", 0, 128, 512), ("nsa_v_cmp", 0, 128, 640), ("nsa_k_sel", 0, 128, 768),
     ("nsa_v_sel", 0, 128, 896), ("nsa_k_win", 0, 128, 1024), ("nsa_v_win", 0, 128, 1152), ("nsa_branch", 0, 12, 1280),
     ("sb_gate", 212, 512, 1408), ("nsa_gate", 0, 156, 1712)),
    (("fox_q", 0, 512, 0), ("fox_k", 0, 512, 512), ("fox_v", 0, 368, 1024), ("nsa_gate", 156, 512, 1408)),
    (("mla_cq", 0, 384, 0), ("mla_ckv", 0, 128, 384), ("mla_k_rope", 0, 64, 512), ("fox_f", 0, 4, 640),
     ("fox_v", 368, 512, 768), ("fox_gate", 0, 512, 1024), ("mla_gate", 0, 512, 1536)),
)
_PIECES = {n: [] for n, _ in _SEGS}
for _s, _grp in enumerate(_GROUPS):
    _cover = sorted((_ORIG[n] + lo, _ORIG[n] + hi) for n, lo, hi, _ in _grp)
    assert _cover[0][0] == _s * CHIP_COLS and _cover[-1][1] == (_s + 1) * CHIP_COLS
    assert all(a[1] == b[0] for a, b in zip(_cover, _cover[1:]))
    _ends = sorted((off, off + hi - lo) for _, lo, hi, off in _grp)
    assert all(a[1] <= b[0] for a, b in zip(_ends, _ends[1:])) and _ends[-1][1] <= GROUP_W
    assert _ends[0][0] == 0 and all(e[0] % 16 == 0 for e in _ends)
    for _n, _lo, _hi, _off in _grp:
        _PIECES[_n].append((_s * GROUP_W + _off, _lo, _hi))
_AL = {n: p[0][0] for n, p in _PIECES.items() if len(p) == 1}


def _cp(sem=None):
    return pltpu.CompilerParams(dimension_semantics=sem, vmem_limit_bytes=VMEM_LIMIT)


def _mm(a, b):
    return jnp.dot(a.astype(_MXU), b.astype(_MXU), preferred_element_type=f32)


def _mm_nt(a, b):
    return lax.dot_general(a.astype(_MXU), b.astype(_MXU), (((1,), (1,)), ((), ())), preferred_element_type=f32)


def _mm_tn(a, b):
    return lax.dot_general(a.astype(_MXU), b.astype(_MXU), (((0,), (0,)), ((), ())), preferred_element_type=f32)


def _mm_split(x, t):
    hi = x.astype(_MXU)
    lo = (x - hi.astype(f32)).astype(_MXU)
    return jnp.dot(hi, t, preferred_element_type=f32) + jnp.dot(lo, t, preferred_element_type=f32)


def _sigmoid(x):
    return 1.0 / (1.0 + jnp.exp(-x))


def _iota(shape, dim):
    return lax.broadcasted_iota(jnp.int32, shape, dim)


def _pick(n, prefs):
    for p in prefs:
        if n % p == 0:
            return p
    return n


def _matmul(a, b, mode, *, bias=None, out_dtype=f32, layer=None, name):
    b_shape = b.shape if layer is None else (b.shape[0] * b.shape[2], b.shape[3])
    if mode == "nn":
        (M, K), (K2, N) = a.shape, b_shape
    elif mode == "nt":
        (M, K), (N, K2) = a.shape, b_shape
    else:
        (K, M), (K2, N) = a.shape, b_shape
    assert K == K2
    tm = _pick(M, (512, 384, 256, 128))
    tn = _pick(N, (512, 384, 256, 128))
    tk = K if K <= 2048 else _pick(K, (2048, 2432, 1024, 512))
    nk = K // tk
    a_spec = {"nn": pl.BlockSpec((tm, tk), lambda i, j, k: (i, k)),
              "nt": pl.BlockSpec((tm, tk), lambda i, j, k: (i, k)),
              "tn": pl.BlockSpec((tk, tm), lambda i, j, k: (k, i))}[mode]
    if layer is None:
        b_spec = {"nn": pl.BlockSpec((tk, tn), lambda i, j, k: (k, j)),
                  "nt": pl.BlockSpec((tn, tk), lambda i, j, k: (j, k)),
                  "tn": pl.BlockSpec((tk, tn), lambda i, j, k: (k, j))}[mode]
    elif mode == "nt":
        per = b.shape[2] // tn
        b_spec = pl.BlockSpec((None, None, tn, tk), lambda i, j, k: (j // per, layer, j % per, k))
    else:
        assert mode == "nn"
        per = b.shape[2] // tk
        b_spec = pl.BlockSpec((None, None, tk, tn), lambda i, j, k: (k // per, layer, k % per, j))
    dot = {"nn": _mm, "nt": _mm_nt, "tn": _mm_tn}[mode]
    has_bias = bias is not None

    def body(*refs):
        if has_bias:
            a_ref, b_ref, bias_ref, o_ref, acc_ref = refs
        else:
            a_ref, b_ref, o_ref, acc_ref = refs
            bias_ref = None
        k = pl.program_id(2)
        part = dot(a_ref[...], b_ref[...])

        def finish(total):
            if has_bias:
                total = total + bias_ref[...]
            o_ref[...] = total.astype(o_ref.dtype)

        if nk == 1:
            finish(part)
        else:
            @pl.when(k == 0)
            def _():
                acc_ref[...] = part

            @pl.when(k > 0)
            def _():
                acc_ref[...] += part

            @pl.when(k == nk - 1)
            def _():
                finish(acc_ref[...])

    in_specs = [a_spec, b_spec]
    args = [a, b]
    if has_bias:
        in_specs.append(pl.BlockSpec((1, tn), lambda i, j, k: (0, j)))
        args.append(bias.reshape(1, N))
    return pl.pallas_call(
        body, out_shape=_SDS((M, N), out_dtype), grid=(M // tm, N // tn, nk),
        in_specs=in_specs, out_specs=pl.BlockSpec((tm, tn), lambda i, j, k: (i, j)),
        scratch_shapes=[pltpu.VMEM((tm, tn), f32)],
        compiler_params=_cp(("parallel", "parallel", "arbitrary")), name=name,
    )(*args)


def _row_block(s):
    return _pick(s, (256, 128))


def _rms_fwd(x, g, *, out_dtype, name):
    s, d = x.shape
    rb = _row_block(s)

    def body(x_ref, g_ref, o_ref):
        xv = x_ref[...]
        r = lax.rsqrt(jnp.mean(xv * xv, axis=-1, keepdims=True) + RMS_EPS)
        o_ref[...] = (xv * r * g_ref[...]).astype(o_ref.dtype)

    return pl.pallas_call(
        body, out_shape=_SDS((s, d), out_dtype), grid=(s // rb,),
        in_specs=[pl.BlockSpec((rb, d), lambda i: (i, 0)), pl.BlockSpec((1, d), lambda i: (0, 0))],
        out_specs=pl.BlockSpec((rb, d), lambda i: (i, 0)), compiler_params=_cp(("parallel",)), name=name,
    )(x, g.reshape(1, d))


def _postnorm_fwd(u, g, x, *, name):
    s, d = u.shape
    rb = _row_block(s)

    def body(u_ref, g_ref, x_ref, o_ref):
        uv = u_ref[...]
        r = lax.rsqrt(jnp.mean(uv * uv, axis=-1, keepdims=True) + RMS_EPS)
        o_ref[...] = x_ref[...] + uv * r * g_ref[...]

    return pl.pallas_call(
        body, out_shape=_SDS((s, d), f32), grid=(s // rb,),
        in_specs=[pl.BlockSpec((rb, d), lambda i: (i, 0)), pl.BlockSpec((1, d), lambda i: (0, 0)),
                  pl.BlockSpec((rb, d), lambda i: (i, 0))],
        out_specs=pl.BlockSpec((rb, d), lambda i: (i, 0)), compiler_params=_cp(("parallel",)), name=name,
    )(u, g.reshape(1, d), x)


def _fold_rows(v):
    r = v.shape[0]
    acc = v[0:8]
    for k in range(1, r // 8):
        acc = acc + v[8 * k:8 * k + 8]
    return acc


def _rms_bwd(dy, x, g, res=None, *, name):
    s, d = x.shape
    rb = _row_block(s)
    nb = s // rb
    has_res = res is not None

    def body(*refs):
        if has_res:
            dy_ref, x_ref, g_ref, res_ref, dx_ref, dg_ref, acc_ref = refs
        else:
            dy_ref, x_ref, g_ref, dx_ref, dg_ref, acc_ref = refs
        i = pl.program_id(0)
        xv = x_ref[...]
        r = lax.rsqrt(jnp.mean(xv * xv, axis=-1, keepdims=True) + RMS_EPS)
        xh = xv * r
        dyv = dy_ref[...]
        dxh = dyv * g_ref[...]
        dx = r * (dxh - xh * jnp.mean(dxh * xh, axis=-1, keepdims=True))
        if has_res:
            dx = dx + res_ref[...]
        dx_ref[...] = dx
        part = _fold_rows(dyv * xh)

        @pl.when(i == 0)
        def _():
            acc_ref[...] = part

        @pl.when(i > 0)
        def _():
            acc_ref[...] += part

        @pl.when(i == nb - 1)
        def _():
            dg_ref[...] = jnp.sum(acc_ref[...], axis=0, keepdims=True)

    blk = pl.BlockSpec((rb, d), lambda i: (i, 0))
    in_specs = [blk, blk, pl.BlockSpec((1, d), lambda i: (0, 0))] + ([blk] if has_res else [])
    args = [dy, x, g.reshape(1, d)] + ([res] if has_res else [])
    return pl.pallas_call(
        body, out_shape=(_SDS((s, d), f32), _SDS((1, d), f32)), grid=(nb,), in_specs=in_specs,
        out_specs=(blk, pl.BlockSpec((1, d), lambda i: (0, 0))),
        scratch_shapes=[pltpu.VMEM((8, d), f32)], compiler_params=_cp(("arbitrary",)), name=name,
    )(*args)


def _loss_head(y, target, *, name):
    s, d = y.shape
    rb = _row_block(s)
    nb = s // rb

    def body(y_ref, t_ref, dy_ref, l_ref):
        i = pl.program_id(0)
        e = y_ref[...] - t_ref[...]
        dy_ref[...] = e * (1.0 / d)
        rows = _fold_rows(e * e)
        part = rows[:, 0:LANES]
        for k in range(1, d // LANES):
            part = part + rows[:, k * LANES:(k + 1) * LANES]
        part = part * (0.5 / d)

        @pl.when(i == 0)
        def _():
            l_ref[...] = part

        @pl.when(i > 0)
        def _():
            l_ref[...] += part

    blk = pl.BlockSpec((rb, d), lambda i: (i, 0))
    return pl.pallas_call(
        body, out_shape=(_SDS((s, d), f32), _SDS((8, LANES), f32)), grid=(nb,), in_specs=[blk, blk],
        out_specs=(blk, pl.BlockSpec((8, LANES), lambda i: (0, 0))),
        compiler_params=_cp(("arbitrary",)), name=name,
    )(y, target)


def _colsum(a, *, name):
    s, n = a.shape
    rb = _row_block(s)
    nb = s // rb
    tn = _pick(n, (2432, 2048, 1024, 512, 384, 128))

    def body(a_ref, o_ref, acc_ref):
        i = pl.program_id(1)
        part = _fold_rows(a_ref[...].astype(f32))

        @pl.when(i == 0)
        def _():
            acc_ref[...] = part

        @pl.when(i > 0)
        def _():
            acc_ref[...] += part

        @pl.when(i == nb - 1)
        def _():
            o_ref[...] = jnp.sum(acc_ref[...], axis=0, keepdims=True)

    return pl.pallas_call(
        body, out_shape=_SDS((1, n), f32), grid=(n // tn, nb),
        in_specs=[pl.BlockSpec((rb, tn), lambda j, i: (i, j))], out_specs=pl.BlockSpec((1, tn), lambda j, i: (0, j)),
        scratch_shapes=[pltpu.VMEM((8, tn), f32)], compiler_params=_cp(("parallel", "arbitrary")), name=name,
    )(a)


def _gate_fwd(o, gate, *, name):
    s, d = o.shape
    rb = _row_block(s)

    def body(o_ref, g_ref, m_ref):
        gv = g_ref[...]
        m_ref[...] = (o_ref[...] * (gv * _sigmoid(gv))).astype(m_ref.dtype)

    blk = pl.BlockSpec((rb, d), lambda i: (i, 0))
    return pl.pallas_call(body, out_shape=_SDS((s, d), _MXU), grid=(s // rb,), in_specs=[blk, blk], out_specs=blk,
                          compiler_params=_cp(("parallel",)), name=name)(o, gate)


def _gate_bwd(dmix, o, gate, *, name):
    s, d = o.shape
    rb = _row_block(s)

    def body(dm_ref, o_ref, g_ref, do_ref, dg_ref):
        gv = g_ref[...]
        sg = _sigmoid(gv)
        dm = dm_ref[...]
        do_ref[...] = dm * (gv * sg)
        dg_ref[...] = dm * o_ref[...] * (sg * (1.0 + gv * (1.0 - sg)))

    blk = pl.BlockSpec((rb, d), lambda i: (i, 0))
    return pl.pallas_call(body, out_shape=(_SDS((s, d), f32), _SDS((s, d), f32)), grid=(s // rb,),
                          in_specs=[blk, blk, blk], out_specs=(blk, blk), compiler_params=_cp(("parallel",)),
                          name=name)(dmix, o, gate)


def _adamw(w, g, m, v, *, name):
    shape = w.shape
    cols = shape[-1]
    rows = int(np.prod(shape[:-1])) if len(shape) > 1 else 1
    to2 = lambda t: t.reshape(rows, cols)
    rb = _pick(rows, (128, 64, 32, 16, 8)) if rows * cols * 4 > (1 << 20) else rows

    def body(w_ref, g_ref, m_ref, v_ref, d_ref, nm_ref, nv_ref):
        gv = g_ref[...]
        mn = ADAM_B1 * m_ref[...] + (1.0 - ADAM_B1) * gv
        vn = ADAM_B2 * v_ref[...] + (1.0 - ADAM_B2) * (gv * gv)
        m_hat = mn / (1.0 - ADAM_B1 ** ADAM_STEP)
        v_hat = vn / (1.0 - ADAM_B2 ** ADAM_STEP)
        d_ref[...] = -ADAM_LR * (m_hat / (jnp.sqrt(v_hat) + ADAM_EPS) + ADAM_WD * w_ref[...])
        nm_ref[...] = mn
        nv_ref[...] = vn

    blk = pl.BlockSpec((rb, cols), lambda i: (i, 0))
    out = pl.pallas_call(body, out_shape=tuple(_SDS((rows, cols), f32) for _ in range(3)), grid=(rows // rb,),
                         in_specs=[blk] * 4, out_specs=(blk,) * 3, compiler_params=_cp(("parallel",)),
                         name=name)(to2(w), to2(g), to2(m), to2(v))
    return tuple(t.reshape(shape) for t in out)


def _sum_slots(a, *, name):
    p, n, c = a.shape
    rb = max(d for d in range(8, n + 1, 8) if n % d == 0 and (p * d * c * 4 <= (6 << 20) or d == 8))

    def body(a_ref, o_ref):
        acc = a_ref[0].astype(f32)
        for k in range(1, p):
            acc = acc + a_ref[k].astype(f32)
        o_ref[...] = acc

    return pl.pallas_call(body, out_shape=_SDS((n, c), f32), grid=(n // rb,),
                          in_specs=[pl.BlockSpec((p, rb, c), lambda i: (0, i, 0))],
                          out_specs=pl.BlockSpec((rb, c), lambda i: (i, 0)), compiler_params=_cp(("parallel",)),
                          name=name)(a)


def _add2(a, b, *, name):
    p, n, c = a.shape
    rb = _pick(n, (1024, 976, 512, 256, 128, 64, 32, 16, 8))

    def body(a_ref, b_ref, o_ref):
        o_ref[...] = a_ref[...] + b_ref[...]

    blk = pl.BlockSpec((1, rb, c), lambda s, i: (s, i, 0))
    return pl.pallas_call(body, out_shape=_SDS((p, n, c), f32), grid=(p, n // rb), in_specs=[blk, blk], out_specs=blk,
                          compiler_params=_cp(("parallel", "parallel")), name=name)(a, b)


def _rope_tables(pos, dim):
    half = dim // 2
    inv = ROPE_THETA ** (-jnp.arange(half, dtype=f32) / half)
    ang = pos.astype(f32)[:, None] * inv[None, :]
    c, s = jnp.cos(ang), jnp.sin(ang)
    z = jnp.zeros_like(c)
    pad = [jnp.zeros((pos.shape[0], LANES - dim), f32)] if dim < LANES else []
    return (jnp.concatenate([c, c] + pad, axis=1), jnp.concatenate([-s, z] + pad, axis=1),
            jnp.concatenate([z, s] + pad, axis=1))


def _rope(x, cos, sa, sb, half, transpose=False):
    if transpose:
        return x * cos + pltpu.roll(x * sa, half, 1) + pltpu.roll(x * sb, LANES - half, 1)
    return x * cos + pltpu.roll(x, LANES - half, 1) * sa + pltpu.roll(x, half, 1) * sb


def _rope_call(items, tables, half, transpose, *, name):
    s = items[0][0].shape[0]
    rb = _row_block(s)
    n = len(items)

    def body(*refs):
        cos, sa, sb = refs[n][...], refs[n + 1][...], refs[n + 2][...]
        for k in range(n):
            x_ref, o_ref = refs[k], refs[n + 3 + k]
            for j in range(items[k][1] // LANES):
                sl = slice(j * LANES, (j + 1) * LANES)
                o_ref[:, sl] = _rope(x_ref[:, sl], cos, sa, sb, half, transpose)

    in_specs = [pl.BlockSpec((rb, w), functools.partial(lambda i, cb: (i, cb), cb=cb)) for _, w, cb in items]
    in_specs += [pl.BlockSpec((rb, LANES), lambda i: (i, 0))] * 3
    out_specs = tuple(pl.BlockSpec((rb, w), lambda i: (i, 0)) for _, w, _ in items)
    return pl.pallas_call(
        body, out_shape=tuple(_SDS((s, w), f32) for _, w, _ in items), grid=(s // rb,), in_specs=in_specs,
        out_specs=out_specs, compiler_params=_cp(("parallel",)), name=name,
    )(*[a for a, _, _ in items], *tables)


def _attn_block(s):
    return _pick(s, (256, 128))


def _lower_mask(b, strict):
    r, c = _iota((b, b), 0), _iota((b, b), 1)
    return (c < r) if strict else (c <= r)


def _pick_lane(block, h):
    return jnp.sum(jnp.where(_iota(block.shape, 1) == h, block, 0.0), axis=1, keepdims=True)


def _head_bias(cum_blk, g, j, hp):
    if hp == N_HEADS:
        return cum_blk[:, j:j + 1]
    return _pick_lane(cum_blk, g * hp + j)


def _attn_fwd(q, k, v, qcol, kcol, vcol, dq, cum, cum_t, *, scale, hp, name):
    s = q.shape[0]
    b = _attn_block(s)
    nq = s // b
    has_bias = cum is not None
    assert qcol % hp == 0 and kcol % hp == 0 and vcol % hp == 0

    def body(*refs):
        if has_bias:
            q_ref, k_ref, v_ref, cum_ref, cumt_ref, o_ref, lse_ref = refs
        else:
            q_ref, k_ref, v_ref, o_ref, lse_ref = refs
        g, i = pl.program_id(0), pl.program_id(1)
        qs = [q_ref[:, j * dq:(j + 1) * dq].astype(_MXU) for j in range(hp)]
        cqs = [_head_bias(cum_ref[...], g, j, hp) for j in range(hp)] if has_bias else None

        def chunk(c, carry, diag):
            st = pl.multiple_of(c * b, b)
            mask = _lower_mask(b, False) if diag else None
            out = []
            for j in range(hp):
                m, l, acc = carry[j]
                z = _mm_nt(qs[j], k_ref[pl.ds(st, b), j * dq:(j + 1) * dq]) * scale
                if has_bias:
                    z = z + cqs[j] - cumt_ref[j, c]
                if diag:
                    z = jnp.where(mask, z, NEG_INF)
                m_new = jnp.maximum(m, jnp.max(z, axis=1, keepdims=True))
                p = jnp.exp(z - m_new)
                if diag:
                    p = jnp.where(mask, p, 0.0)
                alpha = jnp.exp(m - m_new)
                l = alpha * l + jnp.sum(p, axis=1, keepdims=True)
                acc = alpha * acc + _mm(p, v_ref[pl.ds(st, b), j * HEAD_DIM:(j + 1) * HEAD_DIM])
                out.append((m_new, l, acc))
            return tuple(out)

        init = tuple((jnp.full((b, 1), NEG_INF, f32), jnp.zeros((b, 1), f32), jnp.zeros((b, HEAD_DIM), f32))
                     for _ in range(hp))
        carry = lax.fori_loop(0, i, lambda c, cr: chunk(c, cr, False), init)
        for j, (m, l, acc) in enumerate(chunk(i, carry, True)):
            o_ref[:, j * HEAD_DIM:(j + 1) * HEAD_DIM] = acc / l
            lse_ref[j] = m + jnp.log(l)

    in_specs = [pl.BlockSpec((b, hp * dq), lambda g, i: (i, qcol // hp + g)),
                pl.BlockSpec((s, hp * dq), lambda g, i: (0, kcol // hp + g)),
                pl.BlockSpec((s, hp * HEAD_DIM), lambda g, i: (0, vcol // hp + g))]
    args = [q, k, v]
    if has_bias:
        in_specs += [pl.BlockSpec((b, LANES), lambda g, i: (i, 0)),
                     pl.BlockSpec((hp, nq, 1, b), lambda g, i: (g, 0, 0, 0))]
        args += [cum, cum_t]
    return pl.pallas_call(
        body, out_shape=(_SDS((s, N_HEADS * HEAD_DIM), f32), _SDS((N_HEADS, s, 1), f32)), grid=(N_HEADS // hp, nq),
        in_specs=in_specs,
        out_specs=(pl.BlockSpec((b, hp * HEAD_DIM), lambda g, i: (i, g)),
                   pl.BlockSpec((hp, b, 1), lambda g, i: (g, i, 0))),
        compiler_params=_cp(("parallel", "parallel")), name=name,
    )(*args)


def _attn_bwd(q, k, v, qcol, kcol, vcol, dq, do, o, lse, cum, cum_t, *, scale, hp, name):
    s = q.shape[0]
    b = _attn_block(s)
    nq = s // b
    has_bias = cum is not None
    assert qcol % hp == 0 and kcol % hp == 0 and vcol % hp == 0
    hd = lambda j: slice(j * HEAD_DIM, (j + 1) * HEAD_DIM)
    hq = lambda j: slice(j * dq, (j + 1) * dq)

    def body(*refs):
        if has_bias:
            (q_ref, k_ref, v_ref, do_ref, o_ref, lse_ref, cum_ref, cumt_ref, dq_ref, dk_ref, dv_ref, dck_ref,
             p_sc, dp_sc) = refs
        else:
            q_ref, k_ref, v_ref, do_ref, o_ref, lse_ref, dq_ref, dk_ref, dv_ref = refs
        g, i = pl.program_id(0), pl.program_id(1)

        @pl.when(i == 0)
        def _():
            dk_ref[...] = jnp.zeros_like(dk_ref)
            dv_ref[...] = jnp.zeros_like(dv_ref)
            if has_bias:
                dck_ref[...] = jnp.zeros_like(dck_ref)

        qs = [q_ref[:, hq(j)].astype(_MXU) for j in range(hp)]
        dos = [do_ref[:, hd(j)].astype(_MXU) for j in range(hp)]
        lses = [lse_ref[j] for j in range(hp)]
        cqs = [_head_bias(cum_ref[...], g, j, hp) for j in range(hp)] if has_bias else None

        def probs(j, c, diag):
            st = pl.multiple_of(c * b, b)
            z = _mm_nt(qs[j], k_ref[pl.ds(st, b), hq(j)]) * scale
            if has_bias:
                z = z + cqs[j] - cumt_ref[j, c]
            p = jnp.exp(z - lses[j])
            if diag:
                p = jnp.where(_lower_mask(b, False), p, 0.0)
            return p, _mm_nt(dos[j], v_ref[pl.ds(st, b), hd(j)])

        if has_bias:
            def first(c, accs, diag):
                out = []
                for j in range(hp):
                    p, dp = probs(j, c, diag)
                    p_sc[j, c] = p
                    dp_sc[j, c] = dp
                    out.append(accs[j] + jnp.sum(p * dp, axis=1, keepdims=True))
                return tuple(out)

            deltas = lax.fori_loop(0, i, lambda c, a: first(c, a, False),
                                   tuple(jnp.zeros((b, 1), f32) for _ in range(hp)))
            deltas = first(i, deltas, True)
        else:
            deltas = [jnp.sum(do_ref[:, hd(j)] * o_ref[:, hd(j)], axis=1, keepdims=True) for j in range(hp)]

        def chunk(c, dq_accs, diag):
            st = pl.multiple_of(c * b, b)
            out = []
            for j in range(hp):
                p, dp = (p_sc[j, c], dp_sc[j, c]) if has_bias else probs(j, c, diag)
                ds = p * (dp - deltas[j])
                dk_ref[pl.ds(st, b), hq(j)] += _mm_tn(ds, qs[j]) * scale
                dv_ref[pl.ds(st, b), hd(j)] += _mm_tn(p, dos[j])
                if has_bias:
                    dck_ref[j, c] += -jnp.sum(ds, axis=0, keepdims=True)
                out.append(dq_accs[j] + _mm(ds, k_ref[pl.ds(st, b), hq(j)]))
            return tuple(out)

        accs = lax.fori_loop(0, i, lambda c, a: chunk(c, a, False), tuple(jnp.zeros((b, dq), f32) for _ in range(hp)))
        for j, acc in enumerate(chunk(i, accs, True)):
            dq_ref[:, hq(j)] = acc * scale

    rowq = pl.BlockSpec((b, hp * HEAD_DIM), lambda g, i: (i, g))
    in_specs = [pl.BlockSpec((b, hp * dq), lambda g, i: (i, qcol // hp + g)),
                pl.BlockSpec((s, hp * dq), lambda g, i: (0, kcol // hp + g)),
                pl.BlockSpec((s, hp * HEAD_DIM), lambda g, i: (0, vcol // hp + g)), rowq, rowq,
                pl.BlockSpec((hp, b, 1), lambda g, i: (g, i, 0))]
    args = [q, k, v, do, o, lse]
    out_shape = [_SDS((s, N_HEADS * dq), f32), _SDS((s, N_HEADS * dq), f32), _SDS((s, N_HEADS * HEAD_DIM), f32)]
    out_specs = [pl.BlockSpec((b, hp * dq), lambda g, i: (i, g)), pl.BlockSpec((s, hp * dq), lambda g, i: (0, g)),
                 pl.BlockSpec((s, hp * HEAD_DIM), lambda g, i: (0, g))]
    if has_bias:
        in_specs += [pl.BlockSpec((b, LANES), lambda g, i: (i, 0)),
                     pl.BlockSpec((hp, nq, 1, b), lambda g, i: (g, 0, 0, 0))]
        args += [cum, cum_t]
        out_shape.append(_SDS((N_HEADS, nq, 1, b), f32))
        out_specs.append(pl.BlockSpec((hp, nq, 1, b), lambda g, i: (g, 0, 0, 0)))
    return pl.pallas_call(
        body, out_shape=tuple(out_shape), grid=(N_HEADS // hp, nq), in_specs=in_specs, out_specs=tuple(out_specs),
        scratch_shapes=[pltpu.VMEM((hp, nq, b, b), f32)] * 2 if has_bias else [],
        compiler_params=_cp(("parallel", "arbitrary")), name=name,
    )(*args)


def _tri(b, kind):
    r, c = _iota((b, b), 0), _iota((b, b), 1)
    cond = {"row_gt": r > c, "row_lt": r < c, "row_ge": r >= c, "row_le": r <= c}[kind]
    return jnp.where(cond, 1.0, 0.0).astype(_MXU)


def _log_keep(z):
    return -(jnp.maximum(z, 0.0) + jnp.log1p(jnp.exp(-jnp.abs(z))))


def _sb_fwd(z_all, *, hp, name):
    s = z_all.shape[0]
    b = _attn_block(s)
    nq = s // b
    scale = HEAD_DIM ** -0.5
    qcol, kcol, vcol = (_AL[n] // (hp * HEAD_DIM) for n in ("sb_q", "sb_k", "sb_v"))
    hd = lambda j: slice(j * HEAD_DIM, (j + 1) * HEAD_DIM)

    def body(q_ref, k_ref, v_ref, o_ref):
        i = pl.program_id(1)
        qs = [q_ref[:, hd(j)].astype(_MXU) for j in range(hp)]
        upper = _tri(b, "row_gt")

        def chunk(c, carry, diag):
            st = pl.multiple_of(c * b, b)
            mask = _lower_mask(b, True) if diag else None
            out = []
            for j in range(hp):
                rsum, acc = carry[j]
                z = _mm_nt(qs[j], k_ref[pl.ds(st, b), hd(j)]) * scale
                lk = _log_keep(z)
                if diag:
                    lk = jnp.where(mask, lk, 0.0)
                a = z + lk + _mm_split(lk, upper) + rsum
                if diag:
                    a = jnp.where(mask, a, NEG_INF)
                acc = acc + _mm(jnp.exp(a), v_ref[pl.ds(st, b), hd(j)])
                out.append((rsum + jnp.sum(lk, axis=1, keepdims=True), acc))
            return tuple(out)

        init = tuple((jnp.zeros((b, 1), f32), jnp.zeros((b, HEAD_DIM), f32)) for _ in range(hp))
        carry = lax.fori_loop(0, i, lambda t, cr: chunk(i - 1 - t, cr, False), chunk(i, init, True))
        for j in range(hp):
            o_ref[:, hd(j)] = carry[j][1]

    w = hp * HEAD_DIM
    return pl.pallas_call(
        body, out_shape=_SDS((s, GROUP), f32), grid=(N_HEADS // hp, nq),
        in_specs=[pl.BlockSpec((b, w), lambda g, i: (i, qcol + g)), pl.BlockSpec((s, w), lambda g, i: (0, kcol + g)),
                  pl.BlockSpec((s, w), lambda g, i: (0, vcol + g))],
        out_specs=pl.BlockSpec((b, w), lambda g, i: (i, g)),
        compiler_params=_cp(("parallel", "parallel")), name=name,
    )(z_all, z_all, z_all)


def _sb_bwd(z_all, do, *, hp, name):
    s = z_all.shape[0]
    b = _attn_block(s)
    nq = s // b
    scale = HEAD_DIM ** -0.5
    qcol, kcol, vcol = (_AL[n] // (hp * HEAD_DIM) for n in ("sb_q", "sb_k", "sb_v"))
    hd = lambda j: slice(j * HEAD_DIM, (j + 1) * HEAD_DIM)

    def body(q_ref, k_ref, v_ref, do_ref, dq_ref, dk_ref, dv_ref, z_sc, lk_sc, r_sc):
        i = pl.program_id(1)

        @pl.when(i == 0)
        def _():
            dk_ref[...] = jnp.zeros_like(dk_ref)
            dv_ref[...] = jnp.zeros_like(dv_ref)

        qs = [q_ref[:, hd(j)].astype(_MXU) for j in range(hp)]
        dos = [do_ref[:, hd(j)].astype(_MXU) for j in range(hp)]
        upper = _tri(b, "row_gt")
        lower = _tri(b, "row_lt")

        def scores(c, rsums, diag):
            st = pl.multiple_of(c * b, b)
            out = []
            for j in range(hp):
                z = _mm_nt(qs[j], k_ref[pl.ds(st, b), hd(j)]) * scale
                lk = _log_keep(z)
                if diag:
                    lk = jnp.where(_lower_mask(b, True), lk, 0.0)
                z_sc[j, c] = z
                lk_sc[j, c] = lk
                r_sc[j, c] = _mm_split(lk, upper) + rsums[j]
                out.append(rsums[j] + jnp.sum(lk, axis=1, keepdims=True))
            return tuple(out)

        rsums = scores(i, tuple(jnp.zeros((b, 1), f32) for _ in range(hp)), True)
        lax.fori_loop(0, i, lambda t, r: scores(i - 1 - t, r, False), rsums)

        def grads(c, carry, diag):
            st = pl.multiple_of(c * b, b)
            mask = _lower_mask(b, True) if diag else None
            out = []
            for j in range(hp):
                psum, dq_acc = carry[j]
                z, lk = z_sc[j, c], lk_sc[j, c]
                lb = z + lk
                a = lb + r_sc[j, c]
                if diag:
                    a = jnp.where(mask, a, NEG_INF)
                w = jnp.exp(a)
                e = _mm_nt(dos[j], v_ref[pl.ds(st, b), hd(j)]) * w
                before = _mm_split(e, lower) + psum
                dz = e * jnp.exp(lk) - before * jnp.exp(lb)
                if diag:
                    dz = jnp.where(mask, dz, 0.0)
                dk_ref[pl.ds(st, b), hd(j)] += _mm_tn(dz, qs[j]) * scale
                dv_ref[pl.ds(st, b), hd(j)] += _mm_tn(w, dos[j])
                out.append((psum + jnp.sum(e, axis=1, keepdims=True), dq_acc + _mm(dz, k_ref[pl.ds(st, b), hd(j)])))
            return tuple(out)

        init = tuple((jnp.zeros((b, 1), f32), jnp.zeros((b, HEAD_DIM), f32)) for _ in range(hp))
        carry = grads(i, lax.fori_loop(0, i, lambda c, cr: grads(c, cr, False), init), True)
        for j in range(hp):
            dq_ref[:, hd(j)] = carry[j][1] * scale

    w = hp * HEAD_DIM
    blk = pl.BlockSpec((b, w), lambda g, i: (i, g))
    full = pl.BlockSpec((s, w), lambda g, i: (0, g))
    return pl.pallas_call(
        body, out_shape=tuple(_SDS((s, GROUP), f32) for _ in range(3)), grid=(N_HEADS // hp, nq),
        in_specs=[pl.BlockSpec((b, w), lambda g, i: (i, qcol + g)), pl.BlockSpec((s, w), lambda g, i: (0, kcol + g)),
                  pl.BlockSpec((s, w), lambda g, i: (0, vcol + g)), blk],
        out_specs=(blk, full, full),
        scratch_shapes=[pltpu.VMEM((hp, nq, b, b), f32)] * 3,
        compiler_params=_cp(("parallel", "arbitrary")), name=name,
    )(z_all, z_all, z_all, do)


def _split3_left(t, x):
    hi = x.astype(_MXU)
    r1 = x - hi.astype(f32)
    mid = r1.astype(_MXU)
    lo = (r1 - mid.astype(f32)).astype(_MXU)
    dot = functools.partial(jnp.dot, preferred_element_type=f32)
    return dot(t, hi) + dot(t, mid) + dot(t, lo)


def _split3_right(x, t):
    hi = x.astype(_MXU)
    r1 = x - hi.astype(f32)
    mid = r1.astype(_MXU)
    lo = (r1 - mid.astype(f32)).astype(_MXU)
    dot = functools.partial(jnp.dot, preferred_element_type=f32)
    return dot(hi, t) + dot(mid, t) + dot(lo, t)


def _fox_cum_fwd(z_all, bias, *, name):
    s = z_all.shape[0]
    b = _attn_block(s)
    fcol = _AL["fox_f"] // LANES

    def body(f_ref, b_ref, cum_ref, cumt_ref, carry_ref):
        i = pl.program_id(0)

        @pl.when(i == 0)
        def _():
            carry_ref[...] = jnp.zeros_like(carry_ref)

        u = f_ref[...] + b_ref[...]
        lf = jnp.minimum(u, 0.0) - jnp.log1p(jnp.exp(-jnp.abs(u)))
        cum = _split3_left(_tri(b, "row_ge"), lf) + carry_ref[...]
        cum_ref[...] = cum
        cumt_ref[...] = cum.T[0:8, :]
        carry_ref[...] = cum_ref[b - 1:b, :]

    return pl.pallas_call(
        body, out_shape=(_SDS((s, LANES), f32), _SDS((8, s), f32)), grid=(s // b,),
        in_specs=[pl.BlockSpec((b, LANES), lambda i: (i, fcol)), pl.BlockSpec((1, LANES), lambda i: (0, 0))],
        out_specs=(pl.BlockSpec((b, LANES), lambda i: (i, 0)), pl.BlockSpec((8, b), lambda i: (0, i))),
        scratch_shapes=[pltpu.VMEM((1, LANES), f32)], compiler_params=_cp(("arbitrary",)), name=name,
    )(z_all, bias)


def _fox_cum_bwd(z_all, bias, dcum_t, *, name):
    s = z_all.shape[0]
    b = _attn_block(s)
    nb = s // b
    fcol = _AL["fox_f"] // LANES

    def body(f_ref, b_ref, dc_ref, df_ref, db_ref, carry_ref):
        i = pl.program_id(0)

        @pl.when(i == 0)
        def _():
            carry_ref[...] = jnp.zeros_like(carry_ref)
            db_ref[...] = jnp.zeros_like(db_ref)

        dc = dc_ref[...]
        rev = _split3_right(dc, _tri(b, "row_ge")) + carry_ref[...]
        carry_ref[...] = carry_ref[...] + jnp.sum(dc, axis=1, keepdims=True)
        dlf = jnp.concatenate([rev, jnp.zeros((LANES - 8, b), f32)], axis=0).T
        u = f_ref[...] + b_ref[...]
        df = jnp.where(_iota((b, LANES), 1) < N_HEADS, dlf * (1.0 - _sigmoid(u)), 0.0)
        df_ref[...] = df
        db_ref[...] += jnp.sum(df, axis=0, keepdims=True)

    return pl.pallas_call(
        body, out_shape=(_SDS((s, LANES), f32), _SDS((1, LANES), f32)), grid=(nb,),
        in_specs=[pl.BlockSpec((b, LANES), lambda i: (nb - 1 - i, fcol)), pl.BlockSpec((1, LANES), lambda i: (0, 0)),
                  pl.BlockSpec((8, b), lambda i: (0, nb - 1 - i))],
        out_specs=(pl.BlockSpec((b, LANES), lambda i: (nb - 1 - i, 0)), pl.BlockSpec((1, LANES), lambda i: (0, 0))),
        scratch_shapes=[pltpu.VMEM((8, 1), f32)], compiler_params=_cp(("arbitrary",)), name=name,
    )(z_all, bias, dcum_t)


MLA_QW = 2 * LANES


def _rms_rows(x):
    r = lax.rsqrt(jnp.mean(x * x, axis=-1, keepdims=True) + RMS_EPS)
    return x * r, r


def _mla_prep_fwd(z_all, gq, gkv, wuq, wk, wv, tables, *, name):
    s = z_all.shape[0]
    rb = _row_block(s)
    half = MLA_ROPE // 2

    def body(cq_ref, ckv_ref, kr_ref, gq_ref, gkv_ref, wuq_ref, wk_ref, wv_ref, cos_ref, sa_ref, sb_ref,
             q_ref, k_ref, v_ref):
        cos, sa, sb = cos_ref[...], sa_ref[...], sb_ref[...]
        xh, _ = _rms_rows(cq_ref[...])
        qp = _mm(xh * gq_ref[...], wuq_ref[...])
        kh, _ = _rms_rows(ckv_ref[...])
        nkv = kh * gkv_ref[...]
        kn = _mm(nkv, wk_ref[...])
        v_ref[...] = _mm(nkv, wv_ref[...])
        kr = _rope(kr_ref[...], cos, sa, sb, half)
        for h in range(N_HEADS):
            lo, mid, hi = h * MLA_QW, h * MLA_QW + LANES, (h + 1) * MLA_QW
            q_ref[:, lo:mid] = qp[:, lo:mid]
            q_ref[:, mid:hi] = _rope(qp[:, mid:hi], cos, sa, sb, half)
            k_ref[:, lo:mid] = kn[:, h * LANES:(h + 1) * LANES]
            k_ref[:, mid:hi] = kr

    row = lambda w, cb: pl.BlockSpec((rb, w), lambda i: (i, cb))
    whole = lambda a: pl.BlockSpec(a.shape, lambda i: (0,) * a.ndim)
    return pl.pallas_call(
        body, out_shape=(_SDS((s, N_HEADS * MLA_QW), f32), _SDS((s, N_HEADS * MLA_QW), f32), _SDS((s, GROUP), f32)),
        grid=(s // rb,),
        in_specs=[row(MLA_Q_RANK, _AL["mla_cq"] // MLA_Q_RANK), row(LANES, _AL["mla_ckv"] // LANES),
                  row(LANES, _AL["mla_k_rope"] // LANES), whole(gq), whole(gkv), whole(wuq), whole(wk), whole(wv),
                  row(LANES, 0), row(LANES, 0), row(LANES, 0)],
        out_specs=(row(N_HEADS * MLA_QW, 0), row(N_HEADS * MLA_QW, 0), row(GROUP, 0)),
        compiler_params=_cp(("parallel",)), name=name,
    )(z_all, z_all, z_all, gq, gkv, wuq, wk, wv, *tables)


def _mla_prep_bwd(z_all, gq, gkv, wuq, wk, wv, tables, dq_cat, dk_cat, dv, *, name):
    s = z_all.shape[0]
    rb = _row_block(s)
    half = MLA_ROPE // 2

    def body(cq_ref, ckv_ref, gq_ref, gkv_ref, wuq_ref, wk_ref, wv_ref, cos_ref, sa_ref, sb_ref, dq_ref, dk_ref,
             dv_ref, dcq_ref, dckv_ref, dkr_ref, dwuq_ref, dwk_ref, dwv_ref, dgq_ref, dgkv_ref):
        i = pl.program_id(0)

        @pl.when(i == 0)
        def _():
            for r in (dwuq_ref, dwk_ref, dwv_ref, dgq_ref, dgkv_ref):
                r[...] = jnp.zeros_like(r)

        cos, sa, sb = cos_ref[...], sa_ref[...], sb_ref[...]
        parts, knp = [], []
        dkr = jnp.zeros((rb, LANES), f32)
        for h in range(N_HEADS):
            lo, mid, hi = h * MLA_QW, h * MLA_QW + LANES, (h + 1) * MLA_QW
            parts += [dq_ref[:, lo:mid], _rope(dq_ref[:, mid:hi], cos, sa, sb, half, transpose=True)]
            knp.append(dk_ref[:, lo:mid])
            dkr = dkr + _rope(dk_ref[:, mid:hi], cos, sa, sb, half, transpose=True)
        dkr_ref[...] = dkr
        dqp = jnp.concatenate(parts, axis=1)
        dkn = jnp.concatenate(knp, axis=1)
        dvv = dv_ref[...]

        def norm_bwd(x_ref, g_ref, w_pairs, dx_ref, dg_ref):
            xh, r = _rms_rows(x_ref[...])
            nx = xh * g_ref[...]
            dn = jnp.zeros_like(xh)
            for w_ref, dw_ref, dy in w_pairs:
                dw_ref[...] += _mm_tn(nx, dy)
                dn = dn + _mm_nt(dy, w_ref[...])
            dxh = dn * g_ref[...]
            dx_ref[...] = r * (dxh - xh * jnp.mean(dxh * xh, axis=-1, keepdims=True))
            dg_ref[...] += jnp.sum(dn * xh, axis=0, keepdims=True)

        norm_bwd(cq_ref, gq_ref, [(wuq_ref, dwuq_ref, dqp)], dcq_ref, dgq_ref)
        norm_bwd(ckv_ref, gkv_ref, [(wk_ref, dwk_ref, dkn), (wv_ref, dwv_ref, dvv)], dckv_ref, dgkv_ref)

    row = lambda w, cb: pl.BlockSpec((rb, w), lambda i: (i, cb))
    whole = lambda a: pl.BlockSpec(a.shape, lambda i: (0,) * a.ndim)
    return pl.pallas_call(
        body,
        out_shape=(_SDS((s, MLA_Q_RANK), f32), _SDS((s, LANES), f32), _SDS((s, LANES), f32), _SDS(wuq.shape, f32),
                   _SDS(wk.shape, f32), _SDS(wv.shape, f32), _SDS(gq.shape, f32), _SDS(gkv.shape, f32)),
        grid=(s // rb,),
        in_specs=[row(MLA_Q_RANK, _AL["mla_cq"] // MLA_Q_RANK), row(LANES, _AL["mla_ckv"] // LANES), whole(gq),
                  whole(gkv), whole(wuq), whole(wk), whole(wv), row(LANES, 0), row(LANES, 0), row(LANES, 0),
                  row(N_HEADS * MLA_QW, 0), row(N_HEADS * MLA_QW, 0), row(GROUP, 0)],
        out_specs=(row(MLA_Q_RANK, 0), row(LANES, 0), row(LANES, 0), whole(wuq), whole(wk), whole(wv), whole(gq),
                   whole(gkv)),
        compiler_params=_cp(("arbitrary",)), name=name,
    )(z_all, z_all, gq, gkv, wuq, wk, wv, *tables, dq_cat, dk_cat, dv)


def _silu_grad(x):
    sg = _sigmoid(x)
    return sg * (1.0 + x * (1.0 - sg))


def _nsa_cmp_fwd(ra, rb_, pos, w1, w2, tables, *, name):
    nr = ra.shape[1]
    hw = ra.shape[2]

    def body(ra_ref, rb_ref, pos_ref, w1_ref, w2_ref, cos_ref, sa_ref, sb_ref, out_ref, hp_ref):
        for k in range(2):
            xa = ra_ref[k] + pos_ref[k, :, 0:hw]
            xb = rb_ref[k] + pos_ref[k, :, hw:2 * hw]
            hp = _mm(xa, w1_ref[k, 0:hw, :]) + _mm(xb, w1_ref[k, hw:2 * hw, :])
            hp_ref[k] = hp
            out = _mm(hp * _sigmoid(hp), w2_ref[k])
            if k == 0:
                out = _rope(out, cos_ref[...], sa_ref[...], sb_ref[...], HEAD_DIM // 2)
            out_ref[k] = out

    return pl.pallas_call(body, out_shape=(_SDS((2, nr, HEAD_DIM), f32), _SDS((2, nr, HEAD_DIM), f32)),
                          compiler_params=_cp(), name=name)(ra, rb_, pos, w1, w2, *tables)


def _nsa_cmp_bwd(ra, rb_, pos, w1, w2, tables, hp, dout, *, name):
    nr = ra.shape[1]
    hw = ra.shape[2]

    def body(ra_ref, rb_ref, pos_ref, w1_ref, w2_ref, cos_ref, sa_ref, sb_ref, hp_ref, do_ref,
             dxa_ref, dxb_ref, dw1_ref, dw2_ref):
        for k in range(2):
            d_out = do_ref[k]
            if k == 0:
                d_out = _rope(d_out, cos_ref[...], sa_ref[...], sb_ref[...], HEAD_DIM // 2, transpose=True)
            hpv = hp_ref[k]
            dw2_ref[k] = _mm_tn(hpv * _sigmoid(hpv), d_out)
            dhp = _mm_nt(d_out, w2_ref[k]) * _silu_grad(hpv)
            xa = ra_ref[k] + pos_ref[k, :, 0:hw]
            xb = rb_ref[k] + pos_ref[k, :, hw:2 * hw]
            dw1_ref[k, 0:hw, :] = _mm_tn(xa, dhp)
            dw1_ref[k, hw:2 * hw, :] = _mm_tn(xb, dhp)
            dxa_ref[k] = _mm_nt(dhp, w1_ref[k, 0:hw, :])
            dxb_ref[k] = _mm_nt(dhp, w1_ref[k, hw:2 * hw, :])

    return pl.pallas_call(
        body, out_shape=(_SDS((2, nr, hw), f32), _SDS((2, nr, hw), f32), _SDS(w1.shape, f32), _SDS(w2.shape, f32)),
        compiler_params=_cp(), name=name)(ra, rb_, pos, w1, w2, *tables, hp, dout)


def _nsa_consts(s):
    b = _attn_block(s)
    nr = s // CMP_STRIDE
    n_cmp = (s - CMP_LEN) // CMP_STRIDE + 1
    n_sel = s // SEL_LEN
    cmp_start = np.arange(n_cmp) * CMP_STRIDE
    sel_start = np.arange(n_sel) * SEL_LEN
    overlap = np.clip(np.minimum(cmp_start[:, None] + CMP_LEN, sel_start[None, :] + SEL_LEN)
                      - np.maximum(cmp_start[:, None], sel_start[None, :]), 0, None)
    m2s = np.zeros((nr, LANES), np.float32)
    m2s[:n_cmp, :n_sel] = overlap / CMP_LEN
    e3 = np.zeros((s // b, LANES, b), np.float32)
    tok = np.arange(s)
    e3[tok // b, tok // SEL_LEN, tok % b] = 1.0
    return jnp.asarray(m2s, _MXU), jnp.asarray(e3, _MXU)


def _nsa_masks(i, b, d):
    qpos = i * b + _iota((b, b), 0)
    kpos = (i - d) * b + _iota((b, b), 1)
    return (kpos <= qpos) & (kpos > qpos - WINDOW)


def _nsa_fwd(qr, kvc, ksr, vs, kwr, vw, z_all, m2s, e3, *, name):
    s = qr.shape[0]
    b = _attn_block(s)
    nq = s // b
    nr = kvc.shape[1]
    n_sel = s // SEL_LEN
    top_n = min(SEL_TOPN, n_sel)
    nd = -(-WINDOW // b)
    scale = HEAD_DIM ** -0.5
    bcol = _AL["nsa_branch"] // LANES
    H = N_HEADS

    def body(q_ref, kvc_ref, ks_ref, vs_ref, kw_ref, vw_ref, br_ref, m2s_ref, e3_ref,
             o_ref, oc_ref, os_ref, ow_ref, st_ref, sel_ref, m_sc, l_sc, acc_sc):
        i = pl.program_id(0)
        lane = _iota((b, LANES), 1)
        hs = lambda h: slice(h * HEAD_DIM, (h + 1) * HEAD_DIM)

        cmp_mask = (CMP_STRIDE * _iota((b, nr), 1) + (CMP_LEN - 1)) <= (i * b + _iota((b, nr), 0))
        imp = jnp.zeros((b, LANES), f32)
        stats = jnp.zeros((b, LANES), f32)
        for h in range(H):
            zc = jnp.where(cmp_mask, _mm_nt(q_ref[:, hs(h)], kvc_ref[0]) * scale, NEG_INF)
            m = jnp.max(zc, axis=1, keepdims=True)
            p = jnp.where(cmp_mask, jnp.exp(zc - m), 0.0)
            l = jnp.sum(p, axis=1, keepdims=True)
            some = l > 0.0
            lsafe = jnp.where(some, l, 1.0)
            pc = p * jnp.where(some, 1.0 / lsafe, 0.0)
            oc_ref[:, hs(h)] = _mm(pc, kvc_ref[1])
            imp = imp + _mm(pc, m2s_ref[...])
            stats = jnp.where(lane == h, jnp.where(some, m + jnp.log(lsafe), 0.0), stats)

        cur = jnp.right_shift(i * b + _iota((b, LANES), 0), int(math.log2(SEL_LEN)))
        forced = (lane == 0) | (lane == cur) | (lane == cur - 1)
        score = jnp.where(lane <= cur, jnp.where(forced, FORCED_BONUS, imp), NEG_INF)
        score = jnp.where(lane < n_sel, score, -3e38)
        rank = jnp.zeros((b, LANES), f32)
        for j in range(n_sel):
            col = score[:, j:j + 1]
            rank = rank + jnp.where(col > score, 1.0, jnp.where(col == score, jnp.where(lane > j, 1.0, 0.0), 0.0))
        sel = jnp.where(lane < n_sel, jnp.where(rank < top_n, 1.0, 0.0), 0.0)
        sel_ref[...] = sel
        sel_b = sel.astype(_MXU)

        def reset():
            m_sc[...] = jnp.full(m_sc.shape, NEG_INF, f32)
            l_sc[...] = jnp.zeros_like(l_sc)
            acc_sc[...] = jnp.zeros_like(acc_sc)

        def update(h, z, mask, vch):
            zm = jnp.where(mask, z, NEG_INF)
            m_old = m_sc[h]
            m_new = jnp.maximum(m_old, jnp.max(zm, axis=1, keepdims=True))
            p = jnp.where(mask, jnp.exp(zm - m_new), 0.0)
            alpha = jnp.exp(m_old - m_new)
            l_sc[h] = alpha * l_sc[h] + jnp.sum(p, axis=1, keepdims=True)
            acc_sc[h] = alpha * acc_sc[h] + _mm(p, vch)
            m_sc[h] = m_new

        def finish(out_ref, branch, stats):
            for h in range(H):
                out_ref[:, hs(h)] = acc_sc[h] / l_sc[h]
                stats = jnp.where(lane == 4 * branch + h, m_sc[h] + jnp.log(l_sc[h]), stats)
            return stats

        def sel_chunk(c, diag):
            st = pl.multiple_of(c * b, b)
            mask = _mm(sel_b, e3_ref[c]) > 0.5
            if diag:
                mask = mask & _lower_mask(b, False)
            kch, vch = ks_ref[pl.ds(st, b), :], vs_ref[pl.ds(st, b), :]
            for h in range(H):
                update(h, _mm_nt(q_ref[:, hs(h)], kch) * scale, mask, vch)

        reset()

        def sel_loop(c, carry):
            sel_chunk(c, False)
            return carry

        lax.fori_loop(0, i, sel_loop, 0)
        sel_chunk(i, True)
        stats = finish(os_ref, 1, stats)

        reset()
        for d in range(nd, -1, -1):
            @pl.when(i >= d)
            def _():
                st = pl.multiple_of((i - d) * b, b)
                mask = _nsa_masks(i, b, d)
                kch, vch = kw_ref[pl.ds(st, b), :], vw_ref[pl.ds(st, b), :]
                for h in range(H):
                    update(h, _mm_nt(q_ref[:, hs(h)], kch) * scale, mask, vch)
        stats = finish(ow_ref, 2, stats)
        st_ref[...] = stats

        g = _sigmoid(br_ref[...])
        for h in range(H):
            o_ref[:, hs(h)] = (g[:, 3 * h:3 * h + 1] * oc_ref[:, hs(h)] + g[:, 3 * h + 1:3 * h + 2] * os_ref[:, hs(h)]
                               + g[:, 3 * h + 2:3 * h + 3] * ow_ref[:, hs(h)])

    blk = lambda w: pl.BlockSpec((b, w), lambda i: (i, 0))
    whole = lambda a: pl.BlockSpec(a.shape, lambda i: (0,) * a.ndim)
    return pl.pallas_call(
        body, out_shape=tuple(_SDS((s, GROUP), f32) for _ in range(4)) + (_SDS((s, LANES), f32), _SDS((s, LANES), f32)),
        grid=(nq,),
        in_specs=[blk(GROUP), whole(kvc), whole(ksr), whole(vs), whole(kwr), whole(vw),
                  pl.BlockSpec((b, LANES), lambda i: (i, bcol)), whole(m2s), whole(e3)],
        out_specs=(blk(GROUP),) * 4 + (blk(LANES), blk(LANES)),
        scratch_shapes=[pltpu.VMEM((H, b, 1), f32), pltpu.VMEM((H, b, 1), f32), pltpu.VMEM((H, b, HEAD_DIM), f32)],
        compiler_params=_cp(("parallel",)), name=name,
    )(qr, kvc, ksr, vs, kwr, vw, z_all, m2s, e3)


def _nsa_bwd(do, qr, kvc, ksr, vs, kwr, vw, z_all, oc, os_, ow, stats, sel, e3, *, name):
    s = qr.shape[0]
    b = _attn_block(s)
    nq = s // b
    nr = kvc.shape[1]
    nd = -(-WINDOW // b)
    scale = HEAD_DIM ** -0.5
    bcol = _AL["nsa_branch"] // LANES
    H = N_HEADS

    def body(do_ref, q_ref, kvc_ref, ks_ref, vs_ref, kw_ref, vw_ref, br_ref, oc_ref, os_ref, ow_ref, st_ref, sel_ref,
             e3_ref, dq_ref, dbr_ref, dkvc_ref, dks_ref, dvs_ref, dkw_ref, dvw_ref, dob_sc, delta_sc, dq_sc):
        i = pl.program_id(0)

        @pl.when(i == 0)
        def _():
            for r in (dkvc_ref, dks_ref, dvs_ref, dkw_ref, dvw_ref):
                r[...] = jnp.zeros_like(r)

        lane = _iota((b, LANES), 1)
        hs = lambda h: slice(h * HEAD_DIM, (h + 1) * HEAD_DIM)
        g = _sigmoid(br_ref[...])
        stats = st_ref[...]
        dbr = jnp.zeros((b, LANES), f32)
        outs = (oc_ref, os_ref, ow_ref)
        for h in range(H):
            doh = do_ref[:, hs(h)]
            for j in range(3):
                gj = g[:, 3 * h + j:3 * h + j + 1]
                dgj = jnp.sum(doh * outs[j][:, hs(h)], axis=1, keepdims=True)
                dbr = jnp.where(lane == 3 * h + j, dgj * gj * (1.0 - gj), dbr)
                dob_sc[j, :, hs(h)] = gj * doh
                delta_sc[j, h] = gj * dgj
        dbr_ref[...] = dbr
        dq_sc[...] = jnp.zeros_like(dq_sc)

        def branch(j, h, z, mask, kch, vch):
            qh = q_ref[:, hs(h)]
            p = jnp.where(mask, jnp.exp(jnp.where(mask, z, NEG_INF) - stats[:, 4 * j + h:4 * j + h + 1]), 0.0)
            dob = dob_sc[j, :, hs(h)]
            ds = p * (_mm_nt(dob, vch) - delta_sc[j, h])
            dq_sc[:, hs(h)] += _mm(ds, kch) * scale
            return _mm_tn(ds, qh) * scale, _mm_tn(p, dob)

        cmp_mask = (CMP_STRIDE * _iota((b, nr), 1) + (CMP_LEN - 1)) <= (i * b + _iota((b, nr), 0))
        kc, vc = kvc_ref[0], kvc_ref[1]
        for h in range(H):
            dk, dv = branch(0, h, _mm_nt(q_ref[:, hs(h)], kc) * scale, cmp_mask, kc, vc)
            dkvc_ref[0] += dk
            dkvc_ref[1] += dv

        sel_b = sel_ref[...].astype(_MXU)

        def chunk(j, c, mask, k_ref, v_ref, dk_ref, dv_ref):
            st = pl.multiple_of(c * b, b)
            kch, vch = k_ref[pl.ds(st, b), :], v_ref[pl.ds(st, b), :]
            dk = jnp.zeros((b, HEAD_DIM), f32)
            dv = jnp.zeros((b, HEAD_DIM), f32)
            for h in range(H):
                dkh, dvh = branch(j, h, _mm_nt(q_ref[:, hs(h)], kch) * scale, mask, kch, vch)
                dk, dv = dk + dkh, dv + dvh
            dk_ref[pl.ds(st, b), :] += dk
            dv_ref[pl.ds(st, b), :] += dv

        def sel_chunk(c, diag):
            mask = _mm(sel_b, e3_ref[c]) > 0.5
            if diag:
                mask = mask & _lower_mask(b, False)
            chunk(1, c, mask, ks_ref, vs_ref, dks_ref, dvs_ref)

        def sel_loop(c, carry):
            sel_chunk(c, False)
            return carry

        lax.fori_loop(0, i, sel_loop, 0)
        sel_chunk(i, True)

        for d in range(nd, -1, -1):
            @pl.when(i >= d)
            def _():
                chunk(2, i - d, _nsa_masks(i, b, d), kw_ref, vw_ref, dkw_ref, dvw_ref)

        dq_ref[...] = dq_sc[...]

    blk = lambda w: pl.BlockSpec((b, w), lambda i: (i, 0))
    whole = lambda a: pl.BlockSpec(a.shape, lambda i: (0,) * a.ndim)
    stream = _SDS((s, HEAD_DIM), f32)
    return pl.pallas_call(
        body, out_shape=(_SDS((s, GROUP), f32), _SDS((s, LANES), f32), _SDS(kvc.shape, f32), stream, stream, stream,
                         stream),
        grid=(nq,),
        in_specs=[blk(GROUP), blk(GROUP), whole(kvc), whole(ksr), whole(vs), whole(kwr), whole(vw),
                  pl.BlockSpec((b, LANES), lambda i: (i, bcol)), blk(GROUP), blk(GROUP), blk(GROUP), blk(LANES),
                  blk(LANES), whole(e3)],
        out_specs=(blk(GROUP), blk(LANES), whole(kvc), whole(ksr), whole(vs), whole(kwr), whole(vw)),
        scratch_shapes=[pltpu.VMEM((3, b, GROUP), f32), pltpu.VMEM((3, H, b, 1), f32), pltpu.VMEM((b, GROUP), f32)],
        compiler_params=_cp(("arbitrary",)), name=name,
    )(do, qr, kvc, ksr, vs, kwr, vw, z_all, oc, os_, ow, stats, sel, e3)


def _seg(a, name):
    parts = [lax.slice_in_dim(a, off, off + hi - lo, axis=a.ndim - 1) for off, lo, hi in _PIECES[name]]
    return parts[0] if len(parts) == 1 else jnp.concatenate(parts, axis=a.ndim - 1)


def _to_groups(segs, rows, dtype):
    cols = []
    for s, grp in enumerate(_GROUPS):
        at = 0
        for n, lo, hi, off in sorted(grp, key=lambda t: t[3]):
            if off > at:
                cols.append(jnp.zeros((rows, off - at), dtype))
            cols.append(segs[n][:, lo:hi].astype(dtype))
            at = off + hi - lo
        if at < GROUP_W:
            cols.append(jnp.zeros((rows, GROUP_W - at), dtype))
    return jnp.concatenate(cols, axis=1)


def _piece_from_shard(w_t, s):
    grp = sorted(_GROUPS[s], key=lambda t: t[3])
    ends = [t[3] for t in grp[1:]] + [GROUP_W]
    rows = []
    for (n, lo, hi, off), end in zip(grp, ends):
        first = _ORIG[n] + lo - s * CHIP_COLS
        rows.append(jnp.pad(w_t[:, first:first + hi - lo], ((0, 0), (0, end - off - (hi - lo)), (0, 0))))
    return jnp.concatenate(rows, axis=1)


def _shard_from_piece(g, s):
    return jnp.concatenate([g[:, off:off + hi - lo] for n, lo, hi, off in
                            sorted(_GROUPS[s], key=lambda t: _ORIG[t[0]] + t[1])], axis=1)


def _from_groups(a):
    return jnp.concatenate([_seg(a, n) for n, _ in _SEGS], axis=1)


def _cmp_rows(tok):
    s = tok.shape[0]
    r = tok.reshape(s // CMP_STRIDE, CMP_STRIDE * HEAD_DIM)
    return r, jnp.concatenate([r[1:], jnp.zeros((1, r.shape[1]), r.dtype)], axis=0)


def _cmp_unrows(dxa, dxb):
    s = dxa.shape[0] * CMP_STRIDE
    return (dxa + jnp.concatenate([jnp.zeros((1, dxa.shape[1]), dxa.dtype), dxb[:-1]], axis=0)).reshape(s, HEAD_DIM)


_GATES = ("sb_gate", "nsa_gate", "fox_gate", "mla_gate")


def _layer_fwd(x, p, c, tag):
    s = x.shape[0]
    b = _attn_block(s)
    h = _rms_fwd(x, p["pre_g"], out_dtype=_MXU, name=f"prenorm_{tag}")
    z = _matmul(h, p["w_in"], "nt", bias=p["b_in"], layer=p["layer"], name=f"inproj_{tag}")
    o_sb = _sb_fwd(z, hp=HP_FWD, name=f"sb_fwd_{tag}")

    qr, ksr, kwr = _rope_call([(z, GROUP, _AL["nsa_q"] // GROUP), (z, LANES, _AL["nsa_k_sel"] // LANES),
                               (z, LANES, _AL["nsa_k_win"] // LANES)], c["tabs128"], HEAD_DIM // 2, False,
                              name=f"nsa_rope_{tag}")
    (rak, rbk), (rav, rbv) = _cmp_rows(_seg(z, "nsa_k_cmp")), _cmp_rows(_seg(z, "nsa_v_cmp"))
    ra, rb_ = jnp.stack([rak, rav]), jnp.stack([rbk, rbv])
    kvc, hp = _nsa_cmp_fwd(ra, rb_, p["cmp_pos"], p["cmp_w1"], p["cmp_w2"], c["tabs_cmp"], name=f"nsa_cmp_{tag}")
    vs, vw = _seg(z, "nsa_v_sel"), _seg(z, "nsa_v_win")
    o_nsa, oc, os_, ow, stats, sel = _nsa_fwd(qr, kvc, ksr, vs, kwr, vw, z, c["m2s"], c["e3"], name=f"nsa_fwd_{tag}")

    cum, cum_t8 = _fox_cum_fwd(z, p["fox_bias"], name=f"fox_cum_{tag}")
    cum_t = cum_t8.reshape(8, s // b, 1, b)
    fox_v = _seg(z, "fox_v")
    fcols = (_AL["fox_q"] // HEAD_DIM, _AL["fox_k"] // HEAD_DIM, 0)
    o_fox, lse_fox = _attn_fwd(z, z, fox_v, *fcols, HEAD_DIM, cum, cum_t, scale=HEAD_DIM ** -0.5, hp=HP_FWD,
                               name=f"fox_fwd_{tag}")

    qcat, kcat, vm = _mla_prep_fwd(z, p["gq"], p["gkv"], p["wuq"], p["wk"], p["wv"], c["tabs64"],
                                   name=f"mla_prep_{tag}")
    o_mla, lse_mla = _attn_fwd(qcat, kcat, vm, 0, 0, 0, MLA_QW, None, None, scale=(MLA_NOPE + MLA_ROPE) ** -0.5,
                               hp=HP_BWD, name=f"mla_fwd_{tag}")

    o_all = jnp.concatenate([o_sb, o_nsa, o_fox, o_mla], axis=1)
    gates = jnp.concatenate([_seg(z, n) for n in _GATES], axis=1)
    mix = _gate_fwd(o_all, gates, name=f"gate_{tag}")
    u = _matmul(mix, p["w_out"], "nn", name=f"outproj_{tag}")
    y = _postnorm_fwd(u, p["post_g"], x, name=f"postnorm_{tag}")
    saved = dict(x=x, h=h, z=z, qr=qr, ksr=ksr, kwr=kwr, ra=ra, rb=rb_, kvc=kvc, hp=hp, vs=vs, vw=vw, oc=oc, os=os_,
                 ow=ow, stats=stats, sel=sel, cum=cum, cum_t=cum_t, fox_v=fox_v, o_fox=o_fox, lse_fox=lse_fox, qcat=qcat, kcat=kcat,
                 vm=vm, o_mla=o_mla, lse_mla=lse_mla, o_all=o_all, gates=gates, mix=mix, u=u)
    return y, saved


def _layer_bwd(dy, sv, p, c, tag):
    z = sv["z"]
    s = z.shape[0]
    du, dg_post = _rms_bwd(dy, sv["u"], p["post_g"], name=f"postnorm_bwd_{tag}")
    dmix = _matmul(du, p["w_out"], "nt", name=f"outproj_dx_{tag}")
    dw_out = _matmul(sv["mix"], du, "tn", name=f"outproj_dw_{tag}")
    do_all, dgates = _gate_bwd(dmix, sv["o_all"], sv["gates"], name=f"gate_bwd_{tag}")
    do_sb, do_nsa, do_fox, do_mla = (do_all[:, k * GROUP:(k + 1) * GROUP] for k in range(4))
    dgate = [dgates[:, k * GROUP:(k + 1) * GROUP] for k in range(4)]

    sb_dq, sb_dk, sb_dv = _sb_bwd(z, do_sb, hp=HP_BWD, name=f"sb_bwd_{tag}")

    n_dq, n_dbr, n_dkvc, n_dks, n_dvs, n_dkw, n_dvw = _nsa_bwd(
        do_nsa, sv["qr"], sv["kvc"], sv["ksr"], sv["vs"], sv["kwr"], sv["vw"], z, sv["oc"], sv["os"], sv["ow"],
        sv["stats"], sv["sel"], c["e3"], name=f"nsa_bwd_{tag}")
    dxa, dxb, dw1, dw2 = _nsa_cmp_bwd(sv["ra"], sv["rb"], p["cmp_pos"], p["cmp_w1"], p["cmp_w2"], c["tabs_cmp"],
                                      sv["hp"], n_dkvc, name=f"nsa_cmp_bwd_{tag}")
    n_dq, n_dks, n_dkw = _rope_call([(n_dq, GROUP, 0), (n_dks, LANES, 0), (n_dkw, LANES, 0)], c["tabs128"],
                                    HEAD_DIM // 2, True, name=f"nsa_rope_bwd_{tag}")
    dpos = _colsum(jnp.concatenate([dxa[0], dxb[0], dxa[1], dxb[1]], axis=1), name=f"nsa_dpos_{tag}")
    flat = CMP_LEN * HEAD_DIM

    fcols = (_AL["fox_q"] // HEAD_DIM, _AL["fox_k"] // HEAD_DIM, 0)
    f_dq, f_dk, f_dv, f_dck = _attn_bwd(z, z, sv["fox_v"], *fcols, HEAD_DIM, do_fox, sv["o_fox"], sv["lse_fox"],
                                        sv["cum"], sv["cum_t"], scale=HEAD_DIM ** -0.5, hp=HP_BWD,
                                        name=f"fox_bwd_{tag}")
    dcum_t = jnp.pad(f_dck.reshape(N_HEADS, s), ((0, 8 - N_HEADS), (0, 0)))
    f_df, f_dbias = _fox_cum_bwd(z, p["fox_bias"], dcum_t, name=f"fox_cum_bwd_{tag}")

    m_dq, m_dk, m_dv = _attn_bwd(sv["qcat"], sv["kcat"], sv["vm"], 0, 0, 0, MLA_QW, do_mla, sv["o_mla"], sv["lse_mla"],
                                 None, None, scale=(MLA_NOPE + MLA_ROPE) ** -0.5, hp=HP_BWD, name=f"mla_bwd_{tag}")
    m_dcq, m_dckv, m_dkr, m_dwuq, m_dwk, m_dwv, m_dgq, m_dgkv = _mla_prep_bwd(
        z, p["gq"], p["gkv"], p["wuq"], p["wk"], p["wv"], c["tabs64"], m_dq, m_dk, m_dv, name=f"mla_prep_bwd_{tag}")

    dz = _to_groups(dict(
        sb_q=sb_dq, sb_k=sb_dk, sb_v=sb_dv, sb_gate=dgate[0], nsa_q=n_dq, nsa_k_cmp=_cmp_unrows(dxa[0], dxb[0]),
        nsa_v_cmp=_cmp_unrows(dxa[1], dxb[1]), nsa_k_sel=n_dks, nsa_v_sel=n_dvs, nsa_k_win=n_dkw, nsa_v_win=n_dvw,
        nsa_branch=n_dbr, nsa_gate=dgate[1], fox_q=f_dq, fox_k=f_dk, fox_v=f_dv, fox_f=f_df, fox_gate=dgate[2],
        mla_cq=m_dcq, mla_ckv=m_dckv, mla_k_rope=m_dkr, mla_gate=dgate[3]), s, _MXU)
    dh = _matmul(dz, p["w_in"], "nn", layer=p["layer"], name=f"inproj_dx_{tag}")
    dw_in = _matmul(sv["h"], dz, "tn", name=f"inproj_dw_{tag}")
    db = _colsum(dz, name=f"inproj_db_{tag}")
    dx, dg_pre = _rms_bwd(dh, sv["x"], p["pre_g"], res=dy, name=f"prenorm_bwd_{tag}")

    qw = MLA_NOPE + MLA_ROPE
    grads = {
        "pre_norm_g": dg_pre[0], "post_norm_g": dg_post[0], "w_in": dw_in, "b_in": _from_groups(db)[0],
        "w_out": dw_out, "fox_forget_bias": f_dbias[0, :N_HEADS],
        "nsa_cmp_pos_k": dpos[0, :flat].reshape(CMP_LEN, HEAD_DIM), "nsa_cmp_w1_k": dw1[0], "nsa_cmp_w2_k": dw2[0],
        "nsa_cmp_pos_v": dpos[0, flat:].reshape(CMP_LEN, HEAD_DIM), "nsa_cmp_w1_v": dw1[1], "nsa_cmp_w2_v": dw2[1],
        "mla_q_norm_g": m_dgq[0],
        "mla_w_uq": jnp.concatenate([m_dwuq[:, MLA_QW * h:MLA_QW * h + qw] for h in range(N_HEADS)], axis=1),
        "mla_kv_norm_g": m_dgkv[0],
        "mla_w_ukv": jnp.concatenate(sum([[m_dwk[:, LANES * h:LANES * (h + 1)], m_dwv[:, LANES * h:LANES * (h + 1)]]
                                          for h in range(N_HEADS)], []), axis=1),
    }
    return dx, grads


def _layer_params(w, l):
    b_in = w["b_in"][l].reshape(1, -1)
    b_segs = {n: b_in[:, _ORIG[n]:_ORIG[n] + wd] for n, wd in _SEGS}
    qw = MLA_NOPE + MLA_ROPE
    w_uq, w_ukv = w["mla_w_uq"][l], w["mla_w_ukv"][l]
    uq = []
    for h in range(N_HEADS):
        uq += [w_uq[:, qw * h:qw * (h + 1)], jnp.zeros((w_uq.shape[0], MLA_QW - qw), w_uq.dtype)]
    kw_ = 2 * LANES
    flat = CMP_LEN * HEAD_DIM
    return dict(
        pre_g=w["pre_norm_g"][l].reshape(1, -1), post_g=w["post_norm_g"][l].reshape(1, -1),
        w_in=w["w_in"], layer=l, b_in=_to_groups(b_segs, 1, f32), w_out=w["w_out"][l],
        fox_bias=jnp.pad(w["fox_forget_bias"][l], (0, LANES - N_HEADS)).reshape(1, LANES),
        cmp_pos=jnp.stack([w["nsa_cmp_pos_k"][l].reshape(1, flat), w["nsa_cmp_pos_v"][l].reshape(1, flat)]),
        cmp_w1=jnp.stack([w["nsa_cmp_w1_k"][l], w["nsa_cmp_w1_v"][l]]),
        cmp_w2=jnp.stack([w["nsa_cmp_w2_k"][l], w["nsa_cmp_w2_v"][l]]),
        gq=w["mla_q_norm_g"][l].reshape(1, -1), gkv=w["mla_kv_norm_g"][l].reshape(1, -1),
        wuq=jnp.concatenate(uq, axis=1),
        wk=jnp.concatenate([w_ukv[:, kw_ * h:kw_ * h + LANES] for h in range(N_HEADS)], axis=1),
        wv=jnp.concatenate([w_ukv[:, kw_ * h + LANES:kw_ * (h + 1)] for h in range(N_HEADS)], axis=1),
    )


def _consts(s):
    pos = jnp.arange(s)
    m2s, e3 = _nsa_consts(s)
    return dict(tabs128=_rope_tables(pos, HEAD_DIM), tabs64=_rope_tables(pos, MLA_ROPE),
                tabs_cmp=_rope_tables(jnp.arange(s // CMP_STRIDE) * CMP_STRIDE + (CMP_LEN - 1), HEAD_DIM),
                m2s=m2s, e3=e3)


def _place():
    return lax.axis_index("x"), lax.axis_index("y"), lax.axis_index("c")


def _other_chips(x, y):
    return [(1 - x, y), (x, 1 - y), (1 - x, 1 - y)]


def _comm_call(body, out_shapes, n_sems, arrs, name):
    return pl.pallas_call(body, out_shape=tuple(out_shapes), in_specs=[_ANY] * len(arrs),
                          out_specs=tuple(_ANY for _ in out_shapes),
                          scratch_shapes=[pltpu.SemaphoreType.DMA((n_sems,)), pltpu.SemaphoreType.DMA((n_sems,))],
                          name=name)(*arrs)


def _gather_chips(arrs, *, name):
    n = len(arrs)

    def body(*refs):
        a_refs, out_refs, send_sems, recv_sems = refs[:n], refs[n:2 * n], refs[2 * n], refs[2 * n + 1]
        x, y, c = _place()
        me = 2 * x + y
        sibling = (x, y, 1 - c)
        chips = _other_chips(x, y)

        def copy(j, k, src, dst, to):
            return pltpu.make_async_remote_copy(src, dst, send_sems.at[6 * j + k], recv_sems.at[6 * j + k],
                                                device_id=to, device_id_type=_MESH)

        first = [copy(j, k, a_refs[j].at[c], out_refs[j].at[me, c], (px, py, c))
                 for k, (px, py) in enumerate(chips) for j in range(n)]
        for cp in first:
            cp.start()
        passed = []
        for k, (px, py) in enumerate(chips):
            for j in range(n):
                landed = out_refs[j].at[2 * px + py, c]
                copy(j, k, a_refs[j].at[c], landed, (px, py, c)).wait_recv()
                passed.append(copy(j, 3 + k, landed, landed, sibling))
                passed[-1].start()
        for k, (px, py) in enumerate(chips):
            for j in range(n):
                copy(j, 3 + k, a_refs[j].at[c], out_refs[j].at[2 * px + py, 1 - c], sibling).wait_recv()
        for cp in first + passed:
            cp.wait_send()

    return _comm_call(body, [_SDS((N_CHIPS,) + a.shape, a.dtype) for a in arrs], 6 * n, arrs, name)


def _alltoall_chips(arrs, lane_slots, *, name):
    n = len(arrs)

    def slot(ref, lanes, s):
        if lanes:
            w = ref.shape[2] // N_CHIPS
            return ref.at[0, :, pl.ds(s * w, w)]
        return ref.at[s]

    def body(*refs):
        g_refs, out_refs, send_sems, recv_sems = refs[:n], refs[n:2 * n], refs[2 * n], refs[2 * n + 1]
        x, y, c = _place()
        me = 2 * x + y

        def copy(j, s):
            return pltpu.make_async_remote_copy(slot(g_refs[j], lane_slots[j], s), out_refs[j].at[me],
                                                send_sems.at[N_CHIPS * j + s], recv_sems.at[N_CHIPS * j + me],
                                                device_id=(s // 2, s % 2, c), device_id_type=_MESH)

        for s in range(N_CHIPS):
            @pl.when(s != me)
            def _():
                for j in range(n):
                    copy(j, s).start()
        for t in range(N_CHIPS):
            @pl.when(t != me)
            def _():
                for j in range(n):
                    pltpu.make_async_remote_copy(slot(g_refs[j], lane_slots[j], t), out_refs[j].at[t],
                                                 send_sems.at[N_CHIPS * j + t], recv_sems.at[N_CHIPS * j + t],
                                                 device_id=(t // 2, t % 2, c), device_id_type=_MESH).wait_recv()
        for s in range(N_CHIPS):
            @pl.when(s != me)
            def _():
                for j in range(n):
                    copy(j, s).wait_send()

    outs = [_SDS((N_CHIPS, a.shape[1], a.shape[2] // N_CHIPS if lanes else a.shape[2]), a.dtype)
            for a, lanes in zip(arrs, lane_slots)]
    return _comm_call(body, outs, N_CHIPS * n, arrs, name)


def _swap_other_half(arrs, *, name):
    n = len(arrs)

    def body(*refs):
        g_refs, out_refs, send_sems, recv_sems = refs[:n], refs[n:2 * n], refs[2 * n], refs[2 * n + 1]
        x, y, c = _place()
        cps = [pltpu.make_async_remote_copy(g_refs[j].at[:, 1 - c], out_refs[j], send_sems.at[j], recv_sems.at[j],
                                            device_id=(x, y, 1 - c), device_id_type=_MESH) for j in range(n)]
        for cp in cps:
            cp.start()
        for cp in cps:
            cp.wait()

    return _comm_call(body, [_SDS((a.shape[0],) + a.shape[2:], a.dtype) for a in arrs], n, arrs, name)


def _swap_sibling(arrs, *, name):
    n = len(arrs)

    def body(*refs):
        f_refs, out_refs, send_sems, recv_sems = refs[:n], refs[n:2 * n], refs[2 * n], refs[2 * n + 1]
        x, y, c = _place()
        cps = [pltpu.make_async_remote_copy(f_refs[j], out_refs[j], send_sems.at[j], recv_sems.at[j],
                                            device_id=(x, y, 1 - c), device_id_type=_MESH) for j in range(n)]
        for cp in cps:
            cp.start()
        for cp in cps:
            cp.wait()

    return _comm_call(body, [_SDS(a.shape, a.dtype) for a in arrs], n, arrs, name)


def _gather_all(a, *, name):
    def body(a_ref, out_ref, send_sems, recv_sems, local_sem):
        x, y, c = _place()
        flip = lambda v, f: (1 - v) if f else v
        peers = [(flip(x, f & 4), flip(y, f & 2), flip(c, f & 1)) for f in range(1, 8)]
        me = 4 * x + 2 * y + c
        mine = pltpu.make_async_copy(a_ref, out_ref.at[me], local_sem)
        mine.start()
        sends = [pltpu.make_async_remote_copy(a_ref, out_ref.at[me], send_sems.at[k], recv_sems.at[k], device_id=peer,
                                              device_id_type=_MESH) for k, peer in enumerate(peers)]
        for cp in sends:
            cp.start()
        for k, (px, py, pc) in enumerate(peers):
            pltpu.make_async_remote_copy(a_ref, out_ref.at[4 * px + 2 * py + pc], send_sems.at[k], recv_sems.at[k],
                                         device_id=(px, py, pc), device_id_type=_MESH).wait_recv()
        for cp in sends:
            cp.wait_send()
        mine.wait()

    return pl.pallas_call(body, out_shape=_SDS((8,) + a.shape, a.dtype), in_specs=[_ANY], out_specs=_ANY,
                          scratch_shapes=[pltpu.SemaphoreType.DMA((7,)), pltpu.SemaphoreType.DMA((7,)),
                                          pltpu.SemaphoreType.DMA], name=name)(a)


def _add_my_half(g, r, *, name):
    p, _, h, w = g.shape
    tw = _pick(w, (2048, 1024, 512, 256, 128))
    rb = max(d for d in range(16, h + 1, 16) if h % d == 0 and d * tw * 4 <= (2 << 20))

    def body(c_ref, g_ref, r_ref, o_ref):
        o_ref[...] = (g_ref[...] + r_ref[...]).astype(o_ref.dtype)

    blk = pl.BlockSpec((None, rb, tw), lambda s, i, j, c_ref: (s, i, j))
    grid_spec = pltpu.PrefetchScalarGridSpec(
        num_scalar_prefetch=1, grid=(p, h // rb, w // tw),
        in_specs=[pl.BlockSpec((None, None, rb, tw), lambda s, i, j, c_ref: (s, c_ref[0], i, j)), blk], out_specs=blk)
    c = lax.axis_index("c").astype(jnp.int32).reshape(1)
    return pl.pallas_call(body, out_shape=_SDS((p, h, w), _WIRE), grid_spec=grid_spec,
                          compiler_params=_cp(("parallel", "parallel", "parallel")), name=name)(c, g, r)


_WEIGHTS = ("pre_norm_g", "post_norm_g", "w_in", "b_in", "w_out", "fox_forget_bias", "nsa_cmp_pos_k", "nsa_cmp_w1_k",
            "nsa_cmp_w2_k", "nsa_cmp_pos_v", "nsa_cmp_w1_v", "nsa_cmp_w2_v", "mla_q_norm_g", "mla_w_uq",
            "mla_kv_norm_g", "mla_w_ukv")
_SHARD_AXIS = {"w_in": 2, "w_out": 1, "nsa_cmp_w1_k": 1, "nsa_cmp_w1_v": 1, "mla_w_uq": 2, "mla_w_ukv": 2}
_PACK_UNIT = 16 * LANES


def _pack(arrays, dtype):
    rows = []
    for a in arrays:
        v = a.astype(dtype).reshape(-1)
        pad = (-v.shape[0]) % _PACK_UNIT
        if pad:
            v = jnp.concatenate([v, jnp.zeros((pad,), dtype)])
        rows.append(v.reshape(-1, LANES))
    return jnp.concatenate(rows, axis=0)


def _unpack(flat, shapes):
    out, r = [], 0
    for shp in shapes:
        n = int(np.prod(shp))
        nr = -(-n // _PACK_UNIT) * (_PACK_UNIT // LANES)
        out.append(flat[r:r + nr].reshape(-1)[:n].reshape(shp))
        r += nr
    return out


def kernel(x, pre_norm_g, post_norm_g, w_in, b_in, w_out, fox_forget_bias, nsa_cmp_pos_k, nsa_cmp_w1_k, nsa_cmp_w2_k, nsa_cmp_pos_v, nsa_cmp_w1_v, nsa_cmp_w2_v, mla_q_norm_g, mla_w_uq, mla_kv_norm_g, mla_w_ukv, loss_target, m_pre_norm_g, m_post_norm_g, m_w_in, m_b_in, m_w_out, m_fox_forget_bias, m_nsa_cmp_pos_k, m_nsa_cmp_w1_k, m_nsa_cmp_w2_k, m_nsa_cmp_pos_v, m_nsa_cmp_w1_v, m_nsa_cmp_w2_v, m_mla_q_norm_g, m_mla_w_uq, m_mla_kv_norm_g, m_mla_w_ukv, v_pre_norm_g, v_post_norm_g, v_w_in, v_b_in, v_w_out, v_fox_forget_bias, v_nsa_cmp_pos_k, v_nsa_cmp_w1_k, v_nsa_cmp_w2_k, v_nsa_cmp_pos_v, v_nsa_cmp_w1_v, v_nsa_cmp_w2_v, v_mla_q_norm_g, v_mla_w_uq, v_mla_kv_norm_g, v_mla_w_ukv):
    given = dict(locals())
    local = {n: given[n] for n in _WEIGHTS}
    depth = pre_norm_g.shape[0]
    xs, target = x[0], loss_target[0]
    s = xs.shape[0]
    sharded = [n for n in _WEIGHTS if n in _SHARD_AXIS and n != "w_in"]
    small = [n for n in _WEIGHTS if n not in _SHARD_AXIS]
    chip = 2 * lax.axis_index("x") + lax.axis_index("y")
    core = lax.axis_index("c")
    own = lambda slots, mine: lax.dynamic_update_slice_in_dim(slots, mine[None], chip, axis=0)

    w_in_t = jnp.swapaxes(w_in, 1, 2).astype(_MXU)
    piece = lax.switch(chip, [functools.partial(_piece_from_shard, s=k) for k in range(N_CHIPS)], w_in_t)
    shard_shapes = [local[n].shape for n in sharded]
    flat = _pack([local[n] for n in sharded], _MXU)
    flat2 = flat.reshape((2, -1, LANES))
    w_in_all, flat_all = _gather_chips([piece, flat2], name="gather_weights")
    w_in_all = own(w_in_all, piece)
    flat_all = own(flat_all, flat2).reshape((N_CHIPS,) + flat.shape)
    per_chip = [_unpack(flat_all[k], shard_shapes) for k in range(N_CHIPS)]
    full = dict(local)
    full["w_in"] = w_in_all
    for j, n in enumerate(sharded):
        full[n] = jnp.concatenate([per_chip[k][j] for k in range(N_CHIPS)], axis=_SHARD_AXIS[n])

    consts = _consts(s)
    params = [_layer_params(full, l) for l in range(depth)]
    act, saved = xs, []
    for l in range(depth):
        act, sv = _layer_fwd(act, params[l], consts, f"l{l}")
        saved.append(sv)
    dy, loss_parts = _loss_head(act, target, name="loss_head")
    layer_grads = [None] * depth
    for l in reversed(range(depth)):
        dy, layer_grads[l] = _layer_bwd(dy, saved[l], params[l], consts, f"l{l}")
    grad_x = dy[None]
    grads = {n: jnp.stack([layer_grads[l][n] for l in range(depth)]) for n in _WEIGHTS if n != "w_in"}

    def chip_slice(n, k):
        a, ax = grads[n], _SHARD_AXIS[n]
        w = a.shape[ax] // N_CHIPS
        return lax.slice_in_dim(a, k * w, (k + 1) * w, axis=ax)

    g_flat = jnp.stack([_pack([chip_slice(n, k) for n in sharded], f32) for k in range(N_CHIPS)])
    halves = [layer_grads[l]["w_in"].reshape(1, 2, D_MODEL // 2, ZW) for l in range(depth)]
    halves.append(g_flat.reshape(N_CHIPS, 2, -1, LANES))
    lane_slots = [True] * depth + [False]
    from_sibling = _swap_other_half(halves, name="reduce_pair")
    pair_sum = [_add_my_half(g, r, name=f"reduce_pair_add{j}") for j, (g, r) in enumerate(zip(halves, from_sibling))]
    from_chips = _alltoall_chips(pair_sum, lane_slots, name="reduce_chips")
    my_half = []
    for j, (slots, ps, lanes) in enumerate(zip(from_chips, pair_sum, lane_slots)):
        mine = lax.dynamic_slice_in_dim(ps[0], chip * GROUP_W, GROUP_W, axis=1) if lanes else \
            lax.dynamic_index_in_dim(ps, chip, axis=0, keepdims=False)
        my_half.append(_sum_slots(own(slots, mine), name=f"reduce_chips_add{j}"))
    their_half = _swap_sibling(my_half, name="reduce_share")
    first = core == 0
    both = [jnp.concatenate([jnp.where(first, a, b), jnp.where(first, b, a)], axis=0)
            for a, b in zip(my_half, their_half)]
    summed = dict(zip(sharded, _unpack(both[depth], shard_shapes)))
    unpiece = [functools.partial(_shard_from_piece, s=k) for k in range(N_CHIPS)]
    summed["w_in"] = jnp.stack([lax.switch(chip, unpiece, both[l]) for l in range(depth)])

    loss_row = jnp.concatenate([jnp.sum(loss_parts).reshape(1), jnp.zeros((LANES - 1,), f32)])
    small_shapes = [(LANES,)] + [grads[n].shape for n in small]
    contrib = _pack([loss_row] + [grads[n] for n in small], f32)
    pad_rows = (-contrib.shape[0]) % 8
    if pad_rows:
        contrib = jnp.concatenate([contrib, jnp.zeros((pad_rows, LANES), f32)], axis=0)
    total = _unpack(_sum_slots(_gather_all(contrib, name="gather_small"), name="sum_small"), small_shapes)
    loss = total[0][0]
    summed.update(zip(small, total[1:]))

    deltas, new_m, new_v = {}, {}, {}
    for n in _WEIGHTS:
        deltas[n], new_m[n], new_v[n] = _adamw(local[n], summed[n], given["m_" + n], given["v_" + n], name=f"adamw_{n}")
    return (loss, grad_x, *[summed[n] for n in _WEIGHTS], *[deltas[n] for n in _WEIGHTS],
            *[new_m[n] for n in _WEIGHTS], *[new_v[n] for n in _WEIGHTS])
```

```python
import functools
import math

import numpy as np
import jax
import jax.numpy as jnp
from jax import lax
from jax.experimental import pallas as pl
from jax.experimental.pallas import tpu as pltpu

f32 = jnp.float32
bf16 = jnp.bfloat16
_MXU = jnp.bfloat16
_WIRE = jnp.bfloat16
_SDS = jax.ShapeDtypeStruct
_ANY = pl.BlockSpec(memory_space=pl.ANY)
_MESH = pl.DeviceIdType.MESH

D_MODEL = 2048
N_HEADS = 4
HEAD_DIM = 128
GROUP = 512
RMS_EPS = 1e-6
NEG_INF = -1e30
ROPE_THETA = 10000.0
CMP_LEN, CMP_STRIDE, SEL_LEN, SEL_TOPN, WINDOW = 32, 16, 64, 16, 512
FORCED_BONUS = 1e6
MLA_Q_RANK, MLA_KV_RANK, MLA_NOPE, MLA_ROPE = 384, 128, 128, 64
ADAM_LR, ADAM_B1, ADAM_B2, ADAM_EPS, ADAM_WD, ADAM_STEP = 0.001, 0.9, 0.999, 1e-08, 0.01, 10
LANES = 128
VMEM_LIMIT = 48 * 1024 * 1024
HP_FWD, HP_BWD = 4, 2

_SEGS = (
    ("sb_q", 512), ("sb_k", 512), ("sb_v", 512), ("sb_gate", 512), ("nsa_q", 512), ("nsa_k_cmp", 128),
    ("nsa_v_cmp", 128), ("nsa_k_sel", 128), ("nsa_v_sel", 128), ("nsa_k_win", 128), ("nsa_v_win", 128),
    ("nsa_branch", 12), ("nsa_gate", 512), ("fox_q", 512), ("fox_k", 512), ("fox_v", 512), ("fox_f", 4),
    ("fox_gate", 512), ("mla_cq", 384), ("mla_ckv", 128), ("mla_k_rope", 64), ("mla_gate", 512),
)
_ORIG, _WID = {}, {}
_o = 0
for _n, _w in _SEGS:
    _ORIG[_n], _WID[_n] = _o, _w
    _o += _w
IN_WIDTH = _o
N_CHIPS = 4
CHIP_COLS = IN_WIDTH // N_CHIPS
GROUP_W = 2048
ZW = N_CHIPS * GROUP_W
_GROUPS = (
    (("sb_q", 0, 512, 0), ("sb_k", 0, 512, 512), ("sb_v", 0, 512, 1024), ("sb_gate", 0, 212, 1536)),
    (("nsa_q", 0, 512, 0), ("nsa_k_cmp", 0, 128, 512), ("nsa_v_cmp", 0, 128, 640), ("nsa_k_sel", 0, 128, 768),
     ("nsa_v_sel", 0, 128, 896), ("nsa_k_win", 0, 128, 1024), ("nsa_v_win", 0, 128, 1152), ("nsa_branch", 0, 12, 1280),
     ("sb_gate", 212, 512, 1408), ("nsa_gate", 0, 156, 1712)),
    (("fox_q", 0, 512, 0), ("fox_k", 0, 512, 512), ("fox_v", 0, 368, 1024), ("nsa_gate", 156, 512, 1408)),
    (("mla_cq", 0, 384, 0), ("mla_ckv", 0, 128, 384), ("mla_k_rope", 0, 64, 512), ("fox_f", 0, 4, 640),
     ("fox_v", 368, 512, 768), ("fox_gate", 0, 512, 1024), ("mla_gate", 0, 512, 1536)),
)
_PIECES = {n: [] for n, _ in _SEGS}
for _s, _grp in enumerate(_GROUPS):
    _cover = sorted((_ORIG[n] + lo, _ORIG[n] + hi) for n, lo, hi, _ in _grp)
    assert _cover[0][0] == _s * CHIP_COLS and _cover[-1][1] == (_s + 1) * CHIP_COLS
    assert all(a[1] == b[0] for a, b in zip(_cover, _cover[1:]))
    _ends = sorted((off, off + hi - lo) for _, lo, hi, off in _grp)
    assert all(a[1] <= b[0] for a, b in zip(_ends, _ends[1:])) and _ends[-1][1] <= GROUP_W
    assert _ends[0][0] == 0 and all(e[0] % 16 == 0 for e in _ends)
    for _n, _lo, _hi, _off in _grp:
        _PIECES[_n].append((_s * GROUP_W + _off, _lo, _hi))
_AL = {n: p[0][0] for n, p in _PIECES.items() if len(p) == 1}


def _cp(sem=None):
    return pltpu.CompilerParams(dimension_semantics=sem, vmem_limit_bytes=VMEM_LIMIT)


def _mm(a, b):
    return jnp.dot(a.astype(_MXU), b.astype(_MXU), preferred_element_type=f32)


def _mm_nt(a, b):
    return lax.dot_general(a.astype(_MXU), b.astype(_MXU), (((1,), (1,)), ((), ())), preferred_element_type=f32)


def _mm_tn(a, b):
    return lax.dot_general(a.astype(_MXU), b.astype(_MXU), (((0,), (0,)), ((), ())), preferred_element_type=f32)


def _mm_split(x, t):
    hi = x.astype(_MXU)
    lo = (x - hi.astype(f32)).astype(_MXU)
    return jnp.dot(hi, t, preferred_element_type=f32) + jnp.dot(lo, t, preferred_element_type=f32)


def _sigmoid(x):
    return 1.0 / (1.0 + jnp.exp(-x))


def _iota(shape, dim):
    return lax.broadcasted_iota(jnp.int32, shape, dim)


def _pick(n, prefs):
    for p in prefs:
        if n % p == 0:
            return p
    return n


def _matmul(a, b, mode, *, bias=None, out_dtype=f32, name):
    grouped = b.ndim == 3
    b_shape = (b.shape[0] * b.shape[1], b.shape[2]) if grouped else b.shape
    if mode == "nn":
        (M, K), (K2, N) = a.shape, b_shape
    elif mode == "nt":
        (M, K), (N, K2) = a.shape, b_shape
    else:
        (K, M), (K2, N) = a.shape, b_shape
    assert K == K2
    tm = _pick(M, (512, 384, 256, 128))
    tn = _pick(N, (512, 384, 256, 128))
    tk = K if K <= 2048 else _pick(K, (2048, 2432, 1024, 512))
    nk = K // tk
    a_spec = {"nn": pl.BlockSpec((tm, tk), lambda i, j, k: (i, k)),
              "nt": pl.BlockSpec((tm, tk), lambda i, j, k: (i, k)),
              "tn": pl.BlockSpec((tk, tm), lambda i, j, k: (k, i))}[mode]
    if not grouped:
        b_spec = {"nn": pl.BlockSpec((tk, tn), lambda i, j, k: (k, j)),
                  "nt": pl.BlockSpec((tn, tk), lambda i, j, k: (j, k)),
                  "tn": pl.BlockSpec((tk, tn), lambda i, j, k: (k, j))}[mode]
    elif mode == "nt":
        per = b.shape[1] // tn
        b_spec = pl.BlockSpec((None, tn, tk), lambda i, j, k: (j // per, j % per, k))
    else:
        assert mode == "nn"
        per = b.shape[1] // tk
        b_spec = pl.BlockSpec((None, tk, tn), lambda i, j, k: (k // per, k % per, j))
    dot = {"nn": _mm, "nt": _mm_nt, "tn": _mm_tn}[mode]
    has_bias = bias is not None

    def body(*refs):
        if has_bias:
            a_ref, b_ref, bias_ref, o_ref, acc_ref = refs
        else:
            a_ref, b_ref, o_ref, acc_ref = refs
            bias_ref = None
        k = pl.program_id(2)
        part = dot(a_ref[...], b_ref[...])

        def finish(total):
            if has_bias:
                total = total + bias_ref[...]
            o_ref[...] = total.astype(o_ref.dtype)

        if nk == 1:
            finish(part)
        else:
            @pl.when(k == 0)
            def _():
                acc_ref[...] = part

            @pl.when(k > 0)
            def _():
                acc_ref[...] += part

            @pl.when(k == nk - 1)
            def _():
                finish(acc_ref[...])

    in_specs = [a_spec, b_spec]
    args = [a, b]
    if has_bias:
        in_specs.append(pl.BlockSpec((1, tn), lambda i, j, k: (0, j)))
        args.append(bias.reshape(1, N))
    return pl.pallas_call(
        body, out_shape=_SDS((M, N), out_dtype), grid=(M // tm, N // tn, nk),
        in_specs=in_specs, out_specs=pl.BlockSpec((tm, tn), lambda i, j, k: (i, j)),
        scratch_shapes=[pltpu.VMEM((tm, tn), f32)],
        compiler_params=_cp(("parallel", "parallel", "arbitrary")), name=name,
    )(*args)


def _row_block(s):
    return _pick(s, (256, 128))


def _rms_fwd(x, g, *, out_dtype, name):
    s, d = x.shape
    rb = _row_block(s)

    def body(x_ref, g_ref, o_ref):
        xv = x_ref[...]
        r = lax.rsqrt(jnp.mean(xv * xv, axis=-1, keepdims=True) + RMS_EPS)
        o_ref[...] = (xv * r * g_ref[...]).astype(o_ref.dtype)

    return pl.pallas_call(
        body, out_shape=_SDS((s, d), out_dtype), grid=(s // rb,),
        in_specs=[pl.BlockSpec((rb, d), lambda i: (i, 0)), pl.BlockSpec((1, d), lambda i: (0, 0))],
        out_specs=pl.BlockSpec((rb, d), lambda i: (i, 0)), compiler_params=_cp(("parallel",)), name=name,
    )(x, g.reshape(1, d))


def _postnorm_fwd(u, g, x, *, name):
    s, d = u.shape
    rb = _row_block(s)

    def body(u_ref, g_ref, x_ref, o_ref):
        uv = u_ref[...]
        r = lax.rsqrt(jnp.mean(uv * uv, axis=-1, keepdims=True) + RMS_EPS)
        o_ref[...] = x_ref[...] + uv * r * g_ref[...]

    return pl.pallas_call(
        body, out_shape=_SDS((s, d), f32), grid=(s // rb,),
        in_specs=[pl.BlockSpec((rb, d), lambda i: (i, 0)), pl.BlockSpec((1, d), lambda i: (0, 0)),
                  pl.BlockSpec((rb, d), lambda i: (i, 0))],
        out_specs=pl.BlockSpec((rb, d), lambda i: (i, 0)), compiler_params=_cp(("parallel",)), name=name,
    )(u, g.reshape(1, d), x)


def _fold_rows(v):
    r = v.shape[0]
    acc = v[0:8]
    for k in range(1, r // 8):
        acc = acc + v[8 * k:8 * k + 8]
    return acc


def _rms_bwd(dy, x, g, res=None, *, name):
    s, d = x.shape
    rb = _row_block(s)
    nb = s // rb
    has_res = res is not None

    def body(*refs):
        if has_res:
            dy_ref, x_ref, g_ref, res_ref, dx_ref, dg_ref, acc_ref = refs
        else:
            dy_ref, x_ref, g_ref, dx_ref, dg_ref, acc_ref = refs
        i = pl.program_id(0)
        xv = x_ref[...]
        r = lax.rsqrt(jnp.mean(xv * xv, axis=-1, keepdims=True) + RMS_EPS)
        xh = xv * r
        dyv = dy_ref[...]
        dxh = dyv * g_ref[...]
        dx = r * (dxh - xh * jnp.mean(dxh * xh, axis=-1, keepdims=True))
        if has_res:
            dx = dx + res_ref[...]
        dx_ref[...] = dx
        part = _fold_rows(dyv * xh)

        @pl.when(i == 0)
        def _():
            acc_ref[...] = part

        @pl.when(i > 0)
        def _():
            acc_ref[...] += part

        @pl.when(i == nb - 1)
        def _():
            dg_ref[...] = jnp.sum(acc_ref[...], axis=0, keepdims=True)

    blk = pl.BlockSpec((rb, d), lambda i: (i, 0))
    in_specs = [blk, blk, pl.BlockSpec((1, d), lambda i: (0, 0))] + ([blk] if has_res else [])
    args = [dy, x, g.reshape(1, d)] + ([res] if has_res else [])
    return pl.pallas_call(
        body, out_shape=(_SDS((s, d), f32), _SDS((1, d), f32)), grid=(nb,), in_specs=in_specs,
        out_specs=(blk, pl.BlockSpec((1, d), lambda i: (0, 0))),
        scratch_shapes=[pltpu.VMEM((8, d), f32)], compiler_params=_cp(("arbitrary",)), name=name,
    )(*args)


def _loss_head(y, target, *, name):
    s, d = y.shape
    rb = _row_block(s)
    nb = s // rb

    def body(y_ref, t_ref, dy_ref, l_ref):
        i = pl.program_id(0)
        e = y_ref[...] - t_ref[...]
        dy_ref[...] = e * (1.0 / d)
        rows = _fold_rows(e * e)
        part = rows[:, 0:LANES]
        for k in range(1, d // LANES):
            part = part + rows[:, k * LANES:(k + 1) * LANES]
        part = part * (0.5 / d)

        @pl.when(i == 0)
        def _():
            l_ref[...] = part

        @pl.when(i > 0)
        def _():
            l_ref[...] += part

    blk = pl.BlockSpec((rb, d), lambda i: (i, 0))
    return pl.pallas_call(
        body, out_shape=(_SDS((s, d), f32), _SDS((8, LANES), f32)), grid=(nb,), in_specs=[blk, blk],
        out_specs=(blk, pl.BlockSpec((8, LANES), lambda i: (0, 0))),
        compiler_params=_cp(("arbitrary",)), name=name,
    )(y, target)


def _colsum(a, *, name):
    s, n = a.shape
    rb = _row_block(s)
    nb = s // rb
    tn = _pick(n, (2432, 2048, 1024, 512, 384, 128))

    def body(a_ref, o_ref, acc_ref):
        i = pl.program_id(1)
        part = _fold_rows(a_ref[...].astype(f32))

        @pl.when(i == 0)
        def _():
            acc_ref[...] = part

        @pl.when(i > 0)
        def _():
            acc_ref[...] += part

        @pl.when(i == nb - 1)
        def _():
            o_ref[...] = jnp.sum(acc_ref[...], axis=0, keepdims=True)

    return pl.pallas_call(
        body, out_shape=_SDS((1, n), f32), grid=(n // tn, nb),
        in_specs=[pl.BlockSpec((rb, tn), lambda j, i: (i, j))], out_specs=pl.BlockSpec((1, tn), lambda j, i: (0, j)),
        scratch_shapes=[pltpu.VMEM((8, tn), f32)], compiler_params=_cp(("parallel", "arbitrary")), name=name,
    )(a)


def _gate_fwd(o, gate, *, name):
    s, d = o.shape
    rb = _row_block(s)

    def body(o_ref, g_ref, m_ref):
        gv = g_ref[...]
        m_ref[...] = (o_ref[...] * (gv * _sigmoid(gv))).astype(m_ref.dtype)

    blk = pl.BlockSpec((rb, d), lambda i: (i, 0))
    return pl.pallas_call(body, out_shape=_SDS((s, d), _MXU), grid=(s // rb,), in_specs=[blk, blk], out_specs=blk,
                          compiler_params=_cp(("parallel",)), name=name)(o, gate)


def _gate_bwd(dmix, o, gate, *, name):
    s, d = o.shape
    rb = _row_block(s)

    def body(dm_ref, o_ref, g_ref, do_ref, dg_ref):
        gv = g_ref[...]
        sg = _sigmoid(gv)
        dm = dm_ref[...]
        do_ref[...] = dm * (gv * sg)
        dg_ref[...] = dm * o_ref[...] * (sg * (1.0 + gv * (1.0 - sg)))

    blk = pl.BlockSpec((rb, d), lambda i: (i, 0))
    return pl.pallas_call(body, out_shape=(_SDS((s, d), f32), _SDS((s, d), f32)), grid=(s // rb,),
                          in_specs=[blk, blk, blk], out_specs=(blk, blk), compiler_params=_cp(("parallel",)),
                          name=name)(dmix, o, gate)


def _adamw(w, g, m, v, *, name):
    shape = w.shape
    cols = shape[-1]
    rows = int(np.prod(shape[:-1])) if len(shape) > 1 else 1
    to2 = lambda t: t.reshape(rows, cols)
    rb = _pick(rows, (128, 64, 32, 16, 8)) if rows * cols * 4 > (1 << 20) else rows

    def body(w_ref, g_ref, m_ref, v_ref, d_ref, nm_ref, nv_ref):
        gv = g_ref[...]
        mn = ADAM_B1 * m_ref[...] + (1.0 - ADAM_B1) * gv
        vn = ADAM_B2 * v_ref[...] + (1.0 - ADAM_B2) * (gv * gv)
        m_hat = mn / (1.0 - ADAM_B1 ** ADAM_STEP)
        v_hat = vn / (1.0 - ADAM_B2 ** ADAM_STEP)
        d_ref[...] = -ADAM_LR * (m_hat / (jnp.sqrt(v_hat) + ADAM_EPS) + ADAM_WD * w_ref[...])
        nm_ref[...] = mn
        nv_ref[...] = vn

    blk = pl.BlockSpec((rb, cols), lambda i: (i, 0))
    out = pl.pallas_call(body, out_shape=tuple(_SDS((rows, cols), f32) for _ in range(3)), grid=(rows // rb,),
                         in_specs=[blk] * 4, out_specs=(blk,) * 3, compiler_params=_cp(("parallel",)),
                         name=name)(to2(w), to2(g), to2(m), to2(v))
    return tuple(t.reshape(shape) for t in out)


def _sum_slots(a, *, name):
    p, n, c = a.shape
    rb = max(d for d in range(8, n + 1, 8) if n % d == 0 and (p * d * c * 4 <= (6 << 20) or d == 8))

    def body(a_ref, o_ref):
        acc = a_ref[0].astype(f32)
        for k in range(1, p):
            acc = acc + a_ref[k].astype(f32)
        o_ref[...] = acc

    return pl.pallas_call(body, out_shape=_SDS((n, c), f32), grid=(n // rb,),
                          in_specs=[pl.BlockSpec((p, rb, c), lambda i: (0, i, 0))],
                          out_specs=pl.BlockSpec((rb, c), lambda i: (i, 0)), compiler_params=_cp(("parallel",)),
                          name=name)(a)


def _add2(a, b, *, name):
    p, n, c = a.shape
    rb = _pick(n, (1024, 976, 512, 256, 128, 64, 32, 16, 8))

    def body(a_ref, b_ref, o_ref):
        o_ref[...] = a_ref[...] + b_ref[...]

    blk = pl.BlockSpec((1, rb, c), lambda s, i: (s, i, 0))
    return pl.pallas_call(body, out_shape=_SDS((p, n, c), f32), grid=(p, n // rb), in_specs=[blk, blk], out_specs=blk,
                          compiler_params=_cp(("parallel", "parallel")), name=name)(a, b)


def _rope_tables(pos, dim):
    half = dim // 2
    inv = ROPE_THETA ** (-jnp.arange(half, dtype=f32) / half)
    ang = pos.astype(f32)[:, None] * inv[None, :]
    c, s = jnp.cos(ang), jnp.sin(ang)
    z = jnp.zeros_like(c)
    pad = [jnp.zeros((pos.shape[0], LANES - dim), f32)] if dim < LANES else []
    return (jnp.concatenate([c, c] + pad, axis=1), jnp.concatenate([-s, z] + pad, axis=1),
            jnp.concatenate([z, s] + pad, axis=1))


def _rope(x, cos, sa, sb, half, transpose=False):
    if transpose:
        return x * cos + pltpu.roll(x * sa, half, 1) + pltpu.roll(x * sb, LANES - half, 1)
    return x * cos + pltpu.roll(x, LANES - half, 1) * sa + pltpu.roll(x, half, 1) * sb


def _rope_call(items, tables, half, transpose, *, name):
    s = items[0][0].shape[0]
    rb = _row_block(s)
    n = len(items)

    def body(*refs):
        cos, sa, sb = refs[n][...], refs[n + 1][...], refs[n + 2][...]
        for k in range(n):
            x_ref, o_ref = refs[k], refs[n + 3 + k]
            for j in range(items[k][1] // LANES):
                sl = slice(j * LANES, (j + 1) * LANES)
                o_ref[:, sl] = _rope(x_ref[:, sl], cos, sa, sb, half, transpose)

    in_specs = [pl.BlockSpec((rb, w), functools.partial(lambda i, cb: (i, cb), cb=cb)) for _, w, cb in items]
    in_specs += [pl.BlockSpec((rb, LANES), lambda i: (i, 0))] * 3
    out_specs = tuple(pl.BlockSpec((rb, w), lambda i: (i, 0)) for _, w, _ in items)
    return pl.pallas_call(
        body, out_shape=tuple(_SDS((s, w), f32) for _, w, _ in items), grid=(s // rb,), in_specs=in_specs,
        out_specs=out_specs, compiler_params=_cp(("parallel",)), name=name,
    )(*[a for a, _, _ in items], *tables)


def _attn_block(s):
    return _pick(s, (256, 128))


def _lower_mask(b, strict):
    r, c = _iota((b, b), 0), _iota((b, b), 1)
    return (c < r) if strict else (c <= r)


def _pick_lane(block, h):
    return jnp.sum(jnp.where(_iota(block.shape, 1) == h, block, 0.0), axis=1, keepdims=True)


def _head_bias(cum_blk, g, j, hp):
    if hp == N_HEADS:
        return cum_blk[:, j:j + 1]
    return _pick_lane(cum_blk, g * hp + j)


def _attn_fwd(q, k, v, qcol, kcol, vcol, dq, cum, cum_t, *, scale, hp, name):
    s = q.shape[0]
    b = _attn_block(s)
    nq = s // b
    has_bias = cum is not None
    assert qcol % hp == 0 and kcol % hp == 0 and vcol % hp == 0

    def body(*refs):
        if has_bias:
            q_ref, k_ref, v_ref, cum_ref, cumt_ref, o_ref, lse_ref = refs
        else:
            q_ref, k_ref, v_ref, o_ref, lse_ref = refs
        g, i = pl.program_id(0), pl.program_id(1)
        qs = [q_ref[:, j * dq:(j + 1) * dq].astype(_MXU) for j in range(hp)]
        cqs = [_head_bias(cum_ref[...], g, j, hp) for j in range(hp)] if has_bias else None

        def chunk(c, carry, diag):
            st = pl.multiple_of(c * b, b)
            mask = _lower_mask(b, False) if diag else None
            out = []
            for j in range(hp):
                m, l, acc = carry[j]
                z = _mm_nt(qs[j], k_ref[pl.ds(st, b), j * dq:(j + 1) * dq]) * scale
                if has_bias:
                    z = z + cqs[j] - cumt_ref[j, c]
                if diag:
                    z = jnp.where(mask, z, NEG_INF)
                m_new = jnp.maximum(m, jnp.max(z, axis=1, keepdims=True))
                p = jnp.exp(z - m_new)
                if diag:
                    p = jnp.where(mask, p, 0.0)
                alpha = jnp.exp(m - m_new)
                l = alpha * l + jnp.sum(p, axis=1, keepdims=True)
                acc = alpha * acc + _mm(p, v_ref[pl.ds(st, b), j * HEAD_DIM:(j + 1) * HEAD_DIM])
                out.append((m_new, l, acc))
            return tuple(out)

        init = tuple((jnp.full((b, 1), NEG_INF, f32), jnp.zeros((b, 1), f32), jnp.zeros((b, HEAD_DIM), f32))
                     for _ in range(hp))
        carry = lax.fori_loop(0, i, lambda c, cr: chunk(c, cr, False), init)
        for j, (m, l, acc) in enumerate(chunk(i, carry, True)):
            o_ref[:, j * HEAD_DIM:(j + 1) * HEAD_DIM] = acc / l
            lse_ref[j] = m + jnp.log(l)

    in_specs = [pl.BlockSpec((b, hp * dq), lambda g, i: (i, qcol // hp + g)),
                pl.BlockSpec((s, hp * dq), lambda g, i: (0, kcol // hp + g)),
                pl.BlockSpec((s, hp * HEAD_DIM), lambda g, i: (0, vcol // hp + g))]
    args = [q, k, v]
    if has_bias:
        in_specs += [pl.BlockSpec((b, LANES), lambda g, i: (i, 0)),
                     pl.BlockSpec((hp, nq, 1, b), lambda g, i: (g, 0, 0, 0))]
        args += [cum, cum_t]
    return pl.pallas_call(
        body, out_shape=(_SDS((s, N_HEADS * HEAD_DIM), f32), _SDS((N_HEADS, s, 1), f32)), grid=(N_HEADS // hp, nq),
        in_specs=in_specs,
        out_specs=(pl.BlockSpec((b, hp * HEAD_DIM), lambda g, i: (i, g)),
                   pl.BlockSpec((hp, b, 1), lambda g, i: (g, i, 0))),
        compiler_params=_cp(("parallel", "parallel")), name=name,
    )(*args)


def _attn_bwd(q, k, v, qcol, kcol, vcol, dq, do, o, lse, cum, cum_t, *, scale, hp, name):
    s = q.shape[0]
    b = _attn_block(s)
    nq = s // b
    has_bias = cum is not None
    assert qcol % hp == 0 and kcol % hp == 0 and vcol % hp == 0
    hd = lambda j: slice(j * HEAD_DIM, (j + 1) * HEAD_DIM)
    hq = lambda j: slice(j * dq, (j + 1) * dq)

    def body(*refs):
        if has_bias:
            (q_ref, k_ref, v_ref, do_ref, o_ref, lse_ref, cum_ref, cumt_ref, dq_ref, dk_ref, dv_ref, dck_ref,
             p_sc, dp_sc) = refs
        else:
            q_ref, k_ref, v_ref, do_ref, o_ref, lse_ref, dq_ref, dk_ref, dv_ref = refs
        g, i = pl.program_id(0), pl.program_id(1)

        @pl.when(i == 0)
        def _():
            dk_ref[...] = jnp.zeros_like(dk_ref)
            dv_ref[...] = jnp.zeros_like(dv_ref)
            if has_bias:
                dck_ref[...] = jnp.zeros_like(dck_ref)

        qs = [q_ref[:, hq(j)].astype(_MXU) for j in range(hp)]
        dos = [do_ref[:, hd(j)].astype(_MXU) for j in range(hp)]
        lses = [lse_ref[j] for j in range(hp)]
        cqs = [_head_bias(cum_ref[...], g, j, hp) for j in range(hp)] if has_bias else None

        def probs(j, c, diag):
            st = pl.multiple_of(c * b, b)
            z = _mm_nt(qs[j], k_ref[pl.ds(st, b), hq(j)]) * scale
            if has_bias:
                z = z + cqs[j] - cumt_ref[j, c]
            p = jnp.exp(z - lses[j])
            if diag:
                p = jnp.where(_lower_mask(b, False), p, 0.0)
            return p, _mm_nt(dos[j], v_ref[pl.ds(st, b), hd(j)])

        if has_bias:
            def first(c, accs, diag):
                out = []
                for j in range(hp):
                    p, dp = probs(j, c, diag)
                    p_sc[j, c] = p
                    dp_sc[j, c] = dp
                    out.append(accs[j] + jnp.sum(p * dp, axis=1, keepdims=True))
                return tuple(out)

            deltas = lax.fori_loop(0, i, lambda c, a: first(c, a, False),
                                   tuple(jnp.zeros((b, 1), f32) for _ in range(hp)))
            deltas = first(i, deltas, True)
        else:
            deltas = [jnp.sum(do_ref[:, hd(j)] * o_ref[:, hd(j)], axis=1, keepdims=True) for j in range(hp)]

        def chunk(c, dq_accs, diag):
            st = pl.multiple_of(c * b, b)
            out = []
            for j in range(hp):
                p, dp = (p_sc[j, c], dp_sc[j, c]) if has_bias else probs(j, c, diag)
                ds = p * (dp - deltas[j])
                dk_ref[pl.ds(st, b), hq(j)] += _mm_tn(ds, qs[j]) * scale
                dv_ref[pl.ds(st, b), hd(j)] += _mm_tn(p, dos[j])
                if has_bias:
                    dck_ref[j, c] += -jnp.sum(ds, axis=0, keepdims=True)
                out.append(dq_accs[j] + _mm(ds, k_ref[pl.ds(st, b), hq(j)]))
            return tuple(out)

        accs = lax.fori_loop(0, i, lambda c, a: chunk(c, a, False), tuple(jnp.zeros((b, dq), f32) for _ in range(hp)))
        for j, acc in enumerate(chunk(i, accs, True)):
            dq_ref[:, hq(j)] = acc * scale

    rowq = pl.BlockSpec((b, hp * HEAD_DIM), lambda g, i: (i, g))
    in_specs = [pl.BlockSpec((b, hp * dq), lambda g, i: (i, qcol // hp + g)),
                pl.BlockSpec((s, hp * dq), lambda g, i: (0, kcol // hp + g)),
                pl.BlockSpec((s, hp * HEAD_DIM), lambda g, i: (0, vcol // hp + g)), rowq, rowq,
                pl.BlockSpec((hp, b, 1), lambda g, i: (g, i, 0))]
    args = [q, k, v, do, o, lse]
    out_shape = [_SDS((s, N_HEADS * dq), f32), _SDS((s, N_HEADS * dq), f32), _SDS((s, N_HEADS * HEAD_DIM), f32)]
    out_specs = [pl.BlockSpec((b, hp * dq), lambda g, i: (i, g)), pl.BlockSpec((s, hp * dq), lambda g, i: (0, g)),
                 pl.BlockSpec((s, hp * HEAD_DIM), lambda g, i: (0, g))]
    if has_bias:
        in_specs += [pl.BlockSpec((b, LANES), lambda g, i: (i, 0)),
                     pl.BlockSpec((hp, nq, 1, b), lambda g, i: (g, 0, 0, 0))]
        args += [cum, cum_t]
        out_shape.append(_SDS((N_HEADS, nq, 1, b), f32))
        out_specs.append(pl.BlockSpec((hp, nq, 1, b), lambda g, i: (g, 0, 0, 0)))
    return pl.pallas_call(
        body, out_shape=tuple(out_shape), grid=(N_HEADS // hp, nq), in_specs=in_specs, out_specs=tuple(out_specs),
        scratch_shapes=[pltpu.VMEM((hp, nq, b, b), f32)] * 2 if has_bias else [],
        compiler_params=_cp(("parallel", "arbitrary")), name=name,
    )(*args)


def _tri(b, kind):
    r, c = _iota((b, b), 0), _iota((b, b), 1)
    cond = {"row_gt": r > c, "row_lt": r < c, "row_ge": r >= c, "row_le": r <= c}[kind]
    return jnp.where(cond, 1.0, 0.0).astype(_MXU)


def _log_keep(z):
    return -(jnp.maximum(z, 0.0) + jnp.log1p(jnp.exp(-jnp.abs(z))))


def _sb_fwd(z_all, *, hp, name):
    s = z_all.shape[0]
    b = _attn_block(s)
    nq = s // b
    scale = HEAD_DIM ** -0.5
    qcol, kcol, vcol = (_AL[n] // (hp * HEAD_DIM) for n in ("sb_q", "sb_k", "sb_v"))
    hd = lambda j: slice(j * HEAD_DIM, (j + 1) * HEAD_DIM)

    def body(q_ref, k_ref, v_ref, o_ref):
        i = pl.program_id(1)
        qs = [q_ref[:, hd(j)].astype(_MXU) for j in range(hp)]
        upper = _tri(b, "row_gt")

        def chunk(c, carry, diag):
            st = pl.multiple_of(c * b, b)
            mask = _lower_mask(b, True) if diag else None
            out = []
            for j in range(hp):
                rsum, acc = carry[j]
                z = _mm_nt(qs[j], k_ref[pl.ds(st, b), hd(j)]) * scale
                lk = _log_keep(z)
                if diag:
                    lk = jnp.where(mask, lk, 0.0)
                a = z + lk + _mm_split(lk, upper) + rsum
                if diag:
                    a = jnp.where(mask, a, NEG_INF)
                acc = acc + _mm(jnp.exp(a), v_ref[pl.ds(st, b), hd(j)])
                out.append((rsum + jnp.sum(lk, axis=1, keepdims=True), acc))
            return tuple(out)

        init = tuple((jnp.zeros((b, 1), f32), jnp.zeros((b, HEAD_DIM), f32)) for _ in range(hp))
        carry = lax.fori_loop(0, i, lambda t, cr: chunk(i - 1 - t, cr, False), chunk(i, init, True))
        for j in range(hp):
            o_ref[:, hd(j)] = carry[j][1]

    w = hp * HEAD_DIM
    return pl.pallas_call(
        body, out_shape=_SDS((s, GROUP), f32), grid=(N_HEADS // hp, nq),
        in_specs=[pl.BlockSpec((b, w), lambda g, i: (i, qcol + g)), pl.BlockSpec((s, w), lambda g, i: (0, kcol + g)),
                  pl.BlockSpec((s, w), lambda g, i: (0, vcol + g))],
        out_specs=pl.BlockSpec((b, w), lambda g, i: (i, g)),
        compiler_params=_cp(("parallel", "parallel")), name=name,
    )(z_all, z_all, z_all)


def _sb_bwd(z_all, do, *, hp, name):
    s = z_all.shape[0]
    b = _attn_block(s)
    nq = s // b
    scale = HEAD_DIM ** -0.5
    qcol, kcol, vcol = (_AL[n] // (hp * HEAD_DIM) for n in ("sb_q", "sb_k", "sb_v"))
    hd = lambda j: slice(j * HEAD_DIM, (j + 1) * HEAD_DIM)

    def body(q_ref, k_ref, v_ref, do_ref, dq_ref, dk_ref, dv_ref, z_sc, lk_sc, r_sc):
        i = pl.program_id(1)

        @pl.when(i == 0)
        def _():
            dk_ref[...] = jnp.zeros_like(dk_ref)
            dv_ref[...] = jnp.zeros_like(dv_ref)

        qs = [q_ref[:, hd(j)].astype(_MXU) for j in range(hp)]
        dos = [do_ref[:, hd(j)].astype(_MXU) for j in range(hp)]
        upper = _tri(b, "row_gt")
        lower = _tri(b, "row_lt")

        def scores(c, rsums, diag):
            st = pl.multiple_of(c * b, b)
            out = []
            for j in range(hp):
                z = _mm_nt(qs[j], k_ref[pl.ds(st, b), hd(j)]) * scale
                lk = _log_keep(z)
                if diag:
                    lk = jnp.where(_lower_mask(b, True), lk, 0.0)
                z_sc[j, c] = z
                lk_sc[j, c] = lk
                r_sc[j, c] = _mm_split(lk, upper) + rsums[j]
                out.append(rsums[j] + jnp.sum(lk, axis=1, keepdims=True))
            return tuple(out)

        rsums = scores(i, tuple(jnp.zeros((b, 1), f32) for _ in range(hp)), True)
        lax.fori_loop(0, i, lambda t, r: scores(i - 1 - t, r, False), rsums)

        def grads(c, carry, diag):
            st = pl.multiple_of(c * b, b)
            mask = _lower_mask(b, True) if diag else None
            out = []
            for j in range(hp):
                psum, dq_acc = carry[j]
                z, lk = z_sc[j, c], lk_sc[j, c]
                lb = z + lk
                a = lb + r_sc[j, c]
                if diag:
                    a = jnp.where(mask, a, NEG_INF)
                w = jnp.exp(a)
                e = _mm_nt(dos[j], v_ref[pl.ds(st, b), hd(j)]) * w
                before = _mm_split(e, lower) + psum
                dz = e * jnp.exp(lk) - before * jnp.exp(lb)
                if diag:
                    dz = jnp.where(mask, dz, 0.0)
                dk_ref[pl.ds(st, b), hd(j)] += _mm_tn(dz, qs[j]) * scale
                dv_ref[pl.ds(st, b), hd(j)] += _mm_tn(w, dos[j])
                out.append((psum + jnp.sum(e, axis=1, keepdims=True), dq_acc + _mm(dz, k_ref[pl.ds(st, b), hd(j)])))
            return tuple(out)

        init = tuple((jnp.zeros((b, 1), f32), jnp.zeros((b, HEAD_DIM), f32)) for _ in range(hp))
        carry = grads(i, lax.fori_loop(0, i, lambda c, cr: grads(c, cr, False), init), True)
        for j in range(hp):
            dq_ref[:, hd(j)] = carry[j][1] * scale

    w = hp * HEAD_DIM
    blk = pl.BlockSpec((b, w), lambda g, i: (i, g))
    full = pl.BlockSpec((s, w), lambda g, i: (0, g))
    return pl.pallas_call(
        body, out_shape=tuple(_SDS((s, GROUP), f32) for _ in range(3)), grid=(N_HEADS // hp, nq),
        in_specs=[pl.BlockSpec((b, w), lambda g, i: (i, qcol + g)), pl.BlockSpec((s, w), lambda g, i: (0, kcol + g)),
                  pl.BlockSpec((s, w), lambda g, i: (0, vcol + g)), blk],
        out_specs=(blk, full, full),
        scratch_shapes=[pltpu.VMEM((hp, nq, b, b), f32)] * 3,
        compiler_params=_cp(("parallel", "arbitrary")), name=name,
    )(z_all, z_all, z_all, do)


def _split3_left(t, x):
    hi = x.astype(_MXU)
    r1 = x - hi.astype(f32)
    mid = r1.astype(_MXU)
    lo = (r1 - mid.astype(f32)).astype(_MXU)
    dot = functools.partial(jnp.dot, preferred_element_type=f32)
    return dot(t, hi) + dot(t, mid) + dot(t, lo)


def _split3_right(x, t):
    hi = x.astype(_MXU)
    r1 = x - hi.astype(f32)
    mid = r1.astype(_MXU)
    lo = (r1 - mid.astype(f32)).astype(_MXU)
    dot = functools.partial(jnp.dot, preferred_element_type=f32)
    return dot(hi, t) + dot(mid, t) + dot(lo, t)


def _fox_cum_fwd(z_all, bias, *, name):
    s = z_all.shape[0]
    b = _attn_block(s)
    fcol = _AL["fox_f"] // LANES

    def body(f_ref, b_ref, cum_ref, cumt_ref, carry_ref):
        i = pl.program_id(0)

        @pl.when(i == 0)
        def _():
            carry_ref[...] = jnp.zeros_like(carry_ref)

        u = f_ref[...] + b_ref[...]
        lf = jnp.minimum(u, 0.0) - jnp.log1p(jnp.exp(-jnp.abs(u)))
        cum = _split3_left(_tri(b, "row_ge"), lf) + carry_ref[...]
        cum_ref[...] = cum
        cumt_ref[...] = cum.T[0:8, :]
        carry_ref[...] = cum_ref[b - 1:b, :]

    return pl.pallas_call(
        body, out_shape=(_SDS((s, LANES), f32), _SDS((8, s), f32)), grid=(s // b,),
        in_specs=[pl.BlockSpec((b, LANES), lambda i: (i, fcol)), pl.BlockSpec((1, LANES), lambda i: (0, 0))],
        out_specs=(pl.BlockSpec((b, LANES), lambda i: (i, 0)), pl.BlockSpec((8, b), lambda i: (0, i))),
        scratch_shapes=[pltpu.VMEM((1, LANES), f32)], compiler_params=_cp(("arbitrary",)), name=name,
    )(z_all, bias)


def _fox_cum_bwd(z_all, bias, dcum_t, *, name):
    s = z_all.shape[0]
    b = _attn_block(s)
    nb = s // b
    fcol = _AL["fox_f"] // LANES

    def body(f_ref, b_ref, dc_ref, df_ref, db_ref, carry_ref):
        i = pl.program_id(0)

        @pl.when(i == 0)
        def _():
            carry_ref[...] = jnp.zeros_like(carry_ref)
            db_ref[...] = jnp.zeros_like(db_ref)

        dc = dc_ref[...]
        rev = _split3_right(dc, _tri(b, "row_ge")) + carry_ref[...]
        carry_ref[...] = carry_ref[...] + jnp.sum(dc, axis=1, keepdims=True)
        dlf = jnp.concatenate([rev, jnp.zeros((LANES - 8, b), f32)], axis=0).T
        u = f_ref[...] + b_ref[...]
        df = jnp.where(_iota((b, LANES), 1) < N_HEADS, dlf * (1.0 - _sigmoid(u)), 0.0)
        df_ref[...] = df
        db_ref[...] += jnp.sum(df, axis=0, keepdims=True)

    return pl.pallas_call(
        body, out_shape=(_SDS((s, LANES), f32), _SDS((1, LANES), f32)), grid=(nb,),
        in_specs=[pl.BlockSpec((b, LANES), lambda i: (nb - 1 - i, fcol)), pl.BlockSpec((1, LANES), lambda i: (0, 0)),
                  pl.BlockSpec((8, b), lambda i: (0, nb - 1 - i))],
        out_specs=(pl.BlockSpec((b, LANES), lambda i: (nb - 1 - i, 0)), pl.BlockSpec((1, LANES), lambda i: (0, 0))),
        scratch_shapes=[pltpu.VMEM((8, 1), f32)], compiler_params=_cp(("arbitrary",)), name=name,
    )(z_all, bias, dcum_t)


MLA_QW = 2 * LANES


def _rms_rows(x):
    r = lax.rsqrt(jnp.mean(x * x, axis=-1, keepdims=True) + RMS_EPS)
    return x * r, r


def _mla_prep_fwd(z_all, gq, gkv, wuq, wk, wv, tables, *, name):
    s = z_all.shape[0]
    rb = _row_block(s)
    half = MLA_ROPE // 2

    def body(cq_ref, ckv_ref, kr_ref, gq_ref, gkv_ref, wuq_ref, wk_ref, wv_ref, cos_ref, sa_ref, sb_ref,
             q_ref, k_ref, v_ref):
        cos, sa, sb = cos_ref[...], sa_ref[...], sb_ref[...]
        xh, _ = _rms_rows(cq_ref[...])
        qp = _mm(xh * gq_ref[...], wuq_ref[...])
        kh, _ = _rms_rows(ckv_ref[...])
        nkv = kh * gkv_ref[...]
        kn = _mm(nkv, wk_ref[...])
        v_ref[...] = _mm(nkv, wv_ref[...])
        kr = _rope(kr_ref[...], cos, sa, sb, half)
        for h in range(N_HEADS):
            lo, mid, hi = h * MLA_QW, h * MLA_QW + LANES, (h + 1) * MLA_QW
            q_ref[:, lo:mid] = qp[:, lo:mid]
            q_ref[:, mid:hi] = _rope(qp[:, mid:hi], cos, sa, sb, half)
            k_ref[:, lo:mid] = kn[:, h * LANES:(h + 1) * LANES]
            k_ref[:, mid:hi] = kr

    row = lambda w, cb: pl.BlockSpec((rb, w), lambda i: (i, cb))
    whole = lambda a: pl.BlockSpec(a.shape, lambda i: (0,) * a.ndim)
    return pl.pallas_call(
        body, out_shape=(_SDS((s, N_HEADS * MLA_QW), f32), _SDS((s, N_HEADS * MLA_QW), f32), _SDS((s, GROUP), f32)),
        grid=(s // rb,),
        in_specs=[row(MLA_Q_RANK, _AL["mla_cq"] // MLA_Q_RANK), row(LANES, _AL["mla_ckv"] // LANES),
                  row(LANES, _AL["mla_k_rope"] // LANES), whole(gq), whole(gkv), whole(wuq), whole(wk), whole(wv),
                  row(LANES, 0), row(LANES, 0), row(LANES, 0)],
        out_specs=(row(N_HEADS * MLA_QW, 0), row(N_HEADS * MLA_QW, 0), row(GROUP, 0)),
        compiler_params=_cp(("parallel",)), name=name,
    )(z_all, z_all, z_all, gq, gkv, wuq, wk, wv, *tables)


def _mla_prep_bwd(z_all, gq, gkv, wuq, wk, wv, tables, dq_cat, dk_cat, dv, *, name):
    s = z_all.shape[0]
    rb = _row_block(s)
    half = MLA_ROPE // 2

    def body(cq_ref, ckv_ref, gq_ref, gkv_ref, wuq_ref, wk_ref, wv_ref, cos_ref, sa_ref, sb_ref, dq_ref, dk_ref,
             dv_ref, dcq_ref, dckv_ref, dkr_ref, dwuq_ref, dwk_ref, dwv_ref, dgq_ref, dgkv_ref):
        i = pl.program_id(0)

        @pl.when(i == 0)
        def _():
            for r in (dwuq_ref, dwk_ref, dwv_ref, dgq_ref, dgkv_ref):
                r[...] = jnp.zeros_like(r)

        cos, sa, sb = cos_ref[...], sa_ref[...], sb_ref[...]
        parts, knp = [], []
        dkr = jnp.zeros((rb, LANES), f32)
        for h in range(N_HEADS):
            lo, mid, hi = h * MLA_QW, h * MLA_QW + LANES, (h + 1) * MLA_QW
            parts += [dq_ref[:, lo:mid], _rope(dq_ref[:, mid:hi], cos, sa, sb, half, transpose=True)]
            knp.append(dk_ref[:, lo:mid])
            dkr = dkr + _rope(dk_ref[:, mid:hi], cos, sa, sb, half, transpose=True)
        dkr_ref[...] = dkr
        dqp = jnp.concatenate(parts, axis=1)
        dkn = jnp.concatenate(knp, axis=1)
        dvv = dv_ref[...]

        def norm_bwd(x_ref, g_ref, w_pairs, dx_ref, dg_ref):
            xh, r = _rms_rows(x_ref[...])
            nx = xh * g_ref[...]
            dn = jnp.zeros_like(xh)
            for w_ref, dw_ref, dy in w_pairs:
                dw_ref[...] += _mm_tn(nx, dy)
                dn = dn + _mm_nt(dy, w_ref[...])
            dxh = dn * g_ref[...]
            dx_ref[...] = r * (dxh - xh * jnp.mean(dxh * xh, axis=-1, keepdims=True))
            dg_ref[...] += jnp.sum(dn * xh, axis=0, keepdims=True)

        norm_bwd(cq_ref, gq_ref, [(wuq_ref, dwuq_ref, dqp)], dcq_ref, dgq_ref)
        norm_bwd(ckv_ref, gkv_ref, [(wk_ref, dwk_ref, dkn), (wv_ref, dwv_ref, dvv)], dckv_ref, dgkv_ref)

    row = lambda w, cb: pl.BlockSpec((rb, w), lambda i: (i, cb))
    whole = lambda a: pl.BlockSpec(a.shape, lambda i: (0,) * a.ndim)
    return pl.pallas_call(
        body,
        out_shape=(_SDS((s, MLA_Q_RANK), f32), _SDS((s, LANES), f32), _SDS((s, LANES), f32), _SDS(wuq.shape, f32),
                   _SDS(wk.shape, f32), _SDS(wv.shape, f32), _SDS(gq.shape, f32), _SDS(gkv.shape, f32)),
        grid=(s // rb,),
        in_specs=[row(MLA_Q_RANK, _AL["mla_cq"] // MLA_Q_RANK), row(LANES, _AL["mla_ckv"] // LANES), whole(gq),
                  whole(gkv), whole(wuq), whole(wk), whole(wv), row(LANES, 0), row(LANES, 0), row(LANES, 0),
                  row(N_HEADS * MLA_QW, 0), row(N_HEADS * MLA_QW, 0), row(GROUP, 0)],
        out_specs=(row(MLA_Q_RANK, 0), row(LANES, 0), row(LANES, 0), whole(wuq), whole(wk), whole(wv), whole(gq),
                   whole(gkv)),
        compiler_params=_cp(("arbitrary",)), name=name,
    )(z_all, z_all, gq, gkv, wuq, wk, wv, *tables, dq_cat, dk_cat, dv)


def _silu_grad(x):
    sg = _sigmoid(x)
    return sg * (1.0 + x * (1.0 - sg))


def _nsa_cmp_fwd(ra, rb_, pos, w1, w2, tables, *, name):
    nr = ra.shape[1]
    hw = ra.shape[2]

    def body(ra_ref, rb_ref, pos_ref, w1_ref, w2_ref, cos_ref, sa_ref, sb_ref, out_ref, hp_ref):
        for k in range(2):
            xa = ra_ref[k] + pos_ref[k, :, 0:hw]
            xb = rb_ref[k] + pos_ref[k, :, hw:2 * hw]
            hp = _mm(xa, w1_ref[k, 0:hw, :]) + _mm(xb, w1_ref[k, hw:2 * hw, :])
            hp_ref[k] = hp
            out = _mm(hp * _sigmoid(hp), w2_ref[k])
            if k == 0:
                out = _rope(out, cos_ref[...], sa_ref[...], sb_ref[...], HEAD_DIM // 2)
            out_ref[k] = out

    return pl.pallas_call(body, out_shape=(_SDS((2, nr, HEAD_DIM), f32), _SDS((2, nr, HEAD_DIM), f32)),
                          compiler_params=_cp(), name=name)(ra, rb_, pos, w1, w2, *tables)


def _nsa_cmp_bwd(ra, rb_, pos, w1, w2, tables, hp, dout, *, name):
    nr = ra.shape[1]
    hw = ra.shape[2]

    def body(ra_ref, rb_ref, pos_ref, w1_ref, w2_ref, cos_ref, sa_ref, sb_ref, hp_ref, do_ref,
             dxa_ref, dxb_ref, dw1_ref, dw2_ref):
        for k in range(2):
            d_out = do_ref[k]
            if k == 0:
                d_out = _rope(d_out, cos_ref[...], sa_ref[...], sb_ref[...], HEAD_DIM // 2, transpose=True)
            hpv = hp_ref[k]
            dw2_ref[k] = _mm_tn(hpv * _sigmoid(hpv), d_out)
            dhp = _mm_nt(d_out, w2_ref[k]) * _silu_grad(hpv)
            xa = ra_ref[k] + pos_ref[k, :, 0:hw]
            xb = rb_ref[k] + pos_ref[k, :, hw:2 * hw]
            dw1_ref[k, 0:hw, :] = _mm_tn(xa, dhp)
            dw1_ref[k, hw:2 * hw, :] = _mm_tn(xb, dhp)
            dxa_ref[k] = _mm_nt(dhp, w1_ref[k, 0:hw, :])
            dxb_ref[k] = _mm_nt(dhp, w1_ref[k, hw:2 * hw, :])

    return pl.pallas_call(
        body, out_shape=(_SDS((2, nr, hw), f32), _SDS((2, nr, hw), f32), _SDS(w1.shape, f32), _SDS(w2.shape, f32)),
        compiler_params=_cp(), name=name)(ra, rb_, pos, w1, w2, *tables, hp, dout)


def _nsa_consts(s):
    b = _attn_block(s)
    nr = s // CMP_STRIDE
    n_cmp = (s - CMP_LEN) // CMP_STRIDE + 1
    n_sel = s // SEL_LEN
    cmp_start = np.arange(n_cmp) * CMP_STRIDE
    sel_start = np.arange(n_sel) * SEL_LEN
    overlap = np.clip(np.minimum(cmp_start[:, None] + CMP_LEN, sel_start[None, :] + SEL_LEN)
                      - np.maximum(cmp_start[:, None], sel_start[None, :]), 0, None)
    m2s = np.zeros((nr, LANES), np.float32)
    m2s[:n_cmp, :n_sel] = overlap / CMP_LEN
    e3 = np.zeros((s // b, LANES, b), np.float32)
    tok = np.arange(s)
    e3[tok // b, tok // SEL_LEN, tok % b] = 1.0
    return jnp.asarray(m2s, _MXU), jnp.asarray(e3, _MXU)


def _nsa_masks(i, b, d):
    qpos = i * b + _iota((b, b), 0)
    kpos = (i - d) * b + _iota((b, b), 1)
    return (kpos <= qpos) & (kpos > qpos - WINDOW)


def _nsa_fwd(qr, kvc, ksr, vs, kwr, vw, z_all, m2s, e3, *, name):
    s = qr.shape[0]
    b = _attn_block(s)
    nq = s // b
    nr = kvc.shape[1]
    n_sel = s // SEL_LEN
    top_n = min(SEL_TOPN, n_sel)
    nd = -(-WINDOW // b)
    scale = HEAD_DIM ** -0.5
    bcol = _AL["nsa_branch"] // LANES
    H = N_HEADS

    def body(q_ref, kvc_ref, ks_ref, vs_ref, kw_ref, vw_ref, br_ref, m2s_ref, e3_ref,
             o_ref, oc_ref, os_ref, ow_ref, st_ref, sel_ref, m_sc, l_sc, acc_sc):
        i = pl.program_id(0)
        lane = _iota((b, LANES), 1)
        hs = lambda h: slice(h * HEAD_DIM, (h + 1) * HEAD_DIM)

        cmp_mask = (CMP_STRIDE * _iota((b, nr), 1) + (CMP_LEN - 1)) <= (i * b + _iota((b, nr), 0))
        imp = jnp.zeros((b, LANES), f32)
        stats = jnp.zeros((b, LANES), f32)
        for h in range(H):
            zc = jnp.where(cmp_mask, _mm_nt(q_ref[:, hs(h)], kvc_ref[0]) * scale, NEG_INF)
            m = jnp.max(zc, axis=1, keepdims=True)
            p = jnp.where(cmp_mask, jnp.exp(zc - m), 0.0)
            l = jnp.sum(p, axis=1, keepdims=True)
            some = l > 0.0
            lsafe = jnp.where(some, l, 1.0)
            pc = p * jnp.where(some, 1.0 / lsafe, 0.0)
            oc_ref[:, hs(h)] = _mm(pc, kvc_ref[1])
            imp = imp + _mm(pc, m2s_ref[...])
            stats = jnp.where(lane == h, jnp.where(some, m + jnp.log(lsafe), 0.0), stats)

        cur = jnp.right_shift(i * b + _iota((b, LANES), 0), int(math.log2(SEL_LEN)))
        forced = (lane == 0) | (lane == cur) | (lane == cur - 1)
        score = jnp.where(lane <= cur, jnp.where(forced, FORCED_BONUS, imp), NEG_INF)
        score = jnp.where(lane < n_sel, score, -3e38)
        rank = jnp.zeros((b, LANES), f32)
        for j in range(n_sel):
            col = score[:, j:j + 1]
            rank = rank + jnp.where(col > score, 1.0, jnp.where(col == score, jnp.where(lane > j, 1.0, 0.0), 0.0))
        sel = jnp.where(lane < n_sel, jnp.where(rank < top_n, 1.0, 0.0), 0.0)
        sel_ref[...] = sel
        sel_b = sel.astype(_MXU)

        def reset():
            m_sc[...] = jnp.full(m_sc.shape, NEG_INF, f32)
            l_sc[...] = jnp.zeros_like(l_sc)
            acc_sc[...] = jnp.zeros_like(acc_sc)

        def update(h, z, mask, vch):
            zm = jnp.where(mask, z, NEG_INF)
            m_old = m_sc[h]
            m_new = jnp.maximum(m_old, jnp.max(zm, axis=1, keepdims=True))
            p = jnp.where(mask, jnp.exp(zm - m_new), 0.0)
            alpha = jnp.exp(m_old - m_new)
            l_sc[h] = alpha * l_sc[h] + jnp.sum(p, axis=1, keepdims=True)
            acc_sc[h] = alpha * acc_sc[h] + _mm(p, vch)
            m_sc[h] = m_new

        def finish(out_ref, branch, stats):
            for h in range(H):
                out_ref[:, hs(h)] = acc_sc[h] / l_sc[h]
                stats = jnp.where(lane == 4 * branch + h, m_sc[h] + jnp.log(l_sc[h]), stats)
            return stats

        def sel_chunk(c, diag):
            st = pl.multiple_of(c * b, b)
            mask = _mm(sel_b, e3_ref[c]) > 0.5
            if diag:
                mask = mask & _lower_mask(b, False)
            kch, vch = ks_ref[pl.ds(st, b), :], vs_ref[pl.ds(st, b), :]
            for h in range(H):
                update(h, _mm_nt(q_ref[:, hs(h)], kch) * scale, mask, vch)

        reset()

        def sel_loop(c, carry):
            sel_chunk(c, False)
            return carry

        lax.fori_loop(0, i, sel_loop, 0)
        sel_chunk(i, True)
        stats = finish(os_ref, 1, stats)

        reset()
        for d in range(nd, -1, -1):
            @pl.when(i >= d)
            def _():
                st = pl.multiple_of((i - d) * b, b)
                mask = _nsa_masks(i, b, d)
                kch, vch = kw_ref[pl.ds(st, b), :], vw_ref[pl.ds(st, b), :]
                for h in range(H):
                    update(h, _mm_nt(q_ref[:, hs(h)], kch) * scale, mask, vch)
        stats = finish(ow_ref, 2, stats)
        st_ref[...] = stats

        g = _sigmoid(br_ref[...])
        for h in range(H):
            o_ref[:, hs(h)] = (g[:, 3 * h:3 * h + 1] * oc_ref[:, hs(h)] + g[:, 3 * h + 1:3 * h + 2] * os_ref[:, hs(h)]
                               + g[:, 3 * h + 2:3 * h + 3] * ow_ref[:, hs(h)])

    blk = lambda w: pl.BlockSpec((b, w), lambda i: (i, 0))
    whole = lambda a: pl.BlockSpec(a.shape, lambda i: (0,) * a.ndim)
    return pl.pallas_call(
        body, out_shape=tuple(_SDS((s, GROUP), f32) for _ in range(4)) + (_SDS((s, LANES), f32), _SDS((s, LANES), f32)),
        grid=(nq,),
        in_specs=[blk(GROUP), whole(kvc), whole(ksr), whole(vs), whole(kwr), whole(vw),
                  pl.BlockSpec((b, LANES), lambda i: (i, bcol)), whole(m2s), whole(e3)],
        out_specs=(blk(GROUP),) * 4 + (blk(LANES), blk(LANES)),
        scratch_shapes=[pltpu.VMEM((H, b, 1), f32), pltpu.VMEM((H, b, 1), f32), pltpu.VMEM((H, b, HEAD_DIM), f32)],
        compiler_params=_cp(("parallel",)), name=name,
    )(qr, kvc, ksr, vs, kwr, vw, z_all, m2s, e3)


def _nsa_bwd(do, qr, kvc, ksr, vs, kwr, vw, z_all, oc, os_, ow, stats, sel, e3, *, name):
    s = qr.shape[0]
    b = _attn_block(s)
    nq = s // b
    nr = kvc.shape[1]
    nd = -(-WINDOW // b)
    scale = HEAD_DIM ** -0.5
    bcol = _AL["nsa_branch"] // LANES
    H = N_HEADS

    def body(do_ref, q_ref, kvc_ref, ks_ref, vs_ref, kw_ref, vw_ref, br_ref, oc_ref, os_ref, ow_ref, st_ref, sel_ref,
             e3_ref, dq_ref, dbr_ref, dkvc_ref, dks_ref, dvs_ref, dkw_ref, dvw_ref, dob_sc, delta_sc, dq_sc):
        i = pl.program_id(0)

        @pl.when(i == 0)
        def _():
            for r in (dkvc_ref, dks_ref, dvs_ref, dkw_ref, dvw_ref):
                r[...] = jnp.zeros_like(r)

        lane = _iota((b, LANES), 1)
        hs = lambda h: slice(h * HEAD_DIM, (h + 1) * HEAD_DIM)
        g = _sigmoid(br_ref[...])
        stats = st_ref[...]
        dbr = jnp.zeros((b, LANES), f32)
        outs = (oc_ref, os_ref, ow_ref)
        for h in range(H):
            doh = do_ref[:, hs(h)]
            for j in range(3):
                gj = g[:, 3 * h + j:3 * h + j + 1]
                dgj = jnp.sum(doh * outs[j][:, hs(h)], axis=1, keepdims=True)
                dbr = jnp.where(lane == 3 * h + j, dgj * gj * (1.0 - gj), dbr)
                dob_sc[j, :, hs(h)] = gj * doh
                delta_sc[j, h] = gj * dgj
        dbr_ref[...] = dbr
        dq_sc[...] = jnp.zeros_like(dq_sc)

        def branch(j, h, z, mask, kch, vch):
            qh = q_ref[:, hs(h)]
            p = jnp.where(mask, jnp.exp(jnp.where(mask, z, NEG_INF) - stats[:, 4 * j + h:4 * j + h + 1]), 0.0)
            dob = dob_sc[j, :, hs(h)]
            ds = p * (_mm_nt(dob, vch) - delta_sc[j, h])
            dq_sc[:, hs(h)] += _mm(ds, kch) * scale
            return _mm_tn(ds, qh) * scale, _mm_tn(p, dob)

        cmp_mask = (CMP_STRIDE * _iota((b, nr), 1) + (CMP_LEN - 1)) <= (i * b + _iota((b, nr), 0))
        kc, vc = kvc_ref[0], kvc_ref[1]
        for h in range(H):
            dk, dv = branch(0, h, _mm_nt(q_ref[:, hs(h)], kc) * scale, cmp_mask, kc, vc)
            dkvc_ref[0] += dk
            dkvc_ref[1] += dv

        sel_b = sel_ref[...].astype(_MXU)

        def chunk(j, c, mask, k_ref, v_ref, dk_ref, dv_ref):
            st = pl.multiple_of(c * b, b)
            kch, vch = k_ref[pl.ds(st, b), :], v_ref[pl.ds(st, b), :]
            dk = jnp.zeros((b, HEAD_DIM), f32)
            dv = jnp.zeros((b, HEAD_DIM), f32)
            for h in range(H):
                dkh, dvh = branch(j, h, _mm_nt(q_ref[:, hs(h)], kch) * scale, mask, kch, vch)
                dk, dv = dk + dkh, dv + dvh
            dk_ref[pl.ds(st, b), :] += dk
            dv_ref[pl.ds(st, b), :] += dv

        def sel_chunk(c, diag):
            mask = _mm(sel_b, e3_ref[c]) > 0.5
            if diag:
                mask = mask & _lower_mask(b, False)
            chunk(1, c, mask, ks_ref, vs_ref, dks_ref, dvs_ref)

        def sel_loop(c, carry):
            sel_chunk(c, False)
            return carry

        lax.fori_loop(0, i, sel_loop, 0)
        sel_chunk(i, True)

        for d in range(nd, -1, -1):
            @pl.when(i >= d)
            def _():
                chunk(2, i - d, _nsa_masks(i, b, d), kw_ref, vw_ref, dkw_ref, dvw_ref)

        dq_ref[...] = dq_sc[...]

    blk = lambda w: pl.BlockSpec((b, w), lambda i: (i, 0))
    whole = lambda a: pl.BlockSpec(a.shape, lambda i: (0,) * a.ndim)
    stream = _SDS((s, HEAD_DIM), f32)
    return pl.pallas_call(
        body, out_shape=(_SDS((s, GROUP), f32), _SDS((s, LANES), f32), _SDS(kvc.shape, f32), stream, stream, stream,
                         stream),
        grid=(nq,),
        in_specs=[blk(GROUP), blk(GROUP), whole(kvc), whole(ksr), whole(vs), whole(kwr), whole(vw),
                  pl.BlockSpec((b, LANES), lambda i: (i, bcol)), blk(GROUP), blk(GROUP), blk(GROUP), blk(LANES),
                  blk(LANES), whole(e3)],
        out_specs=(blk(GROUP), blk(LANES), whole(kvc), whole(ksr), whole(vs), whole(kwr), whole(vw)),
        scratch_shapes=[pltpu.VMEM((3, b, GROUP), f32), pltpu.VMEM((3, H, b, 1), f32), pltpu.VMEM((b, GROUP), f32)],
        compiler_params=_cp(("arbitrary",)), name=name,
    )(do, qr, kvc, ksr, vs, kwr, vw, z_all, oc, os_, ow, stats, sel, e3)


def _seg(a, name):
    parts = [lax.slice_in_dim(a, off, off + hi - lo, axis=a.ndim - 1) for off, lo, hi in _PIECES[name]]
    return parts[0] if len(parts) == 1 else jnp.concatenate(parts, axis=a.ndim - 1)


def _to_groups(segs, rows, dtype):
    cols = []
    for s, grp in enumerate(_GROUPS):
        at = 0
        for n, lo, hi, off in sorted(grp, key=lambda t: t[3]):
            if off > at:
                cols.append(jnp.zeros((rows, off - at), dtype))
            cols.append(segs[n][:, lo:hi].astype(dtype))
            at = off + hi - lo
        if at < GROUP_W:
            cols.append(jnp.zeros((rows, GROUP_W - at), dtype))
    return jnp.concatenate(cols, axis=1)


def _piece_from_shard(w_t, s):
    grp = sorted(_GROUPS[s], key=lambda t: t[3])
    ends = [t[3] for t in grp[1:]] + [GROUP_W]
    rows = []
    for (n, lo, hi, off), end in zip(grp, ends):
        first = _ORIG[n] + lo - s * CHIP_COLS
        rows.append(jnp.pad(w_t[:, first:first + hi - lo], ((0, 0), (0, end - off - (hi - lo)), (0, 0))))
    return jnp.concatenate(rows, axis=1)


def _shard_from_piece(g, s):
    return jnp.concatenate([g[:, off:off + hi - lo] for n, lo, hi, off in
                            sorted(_GROUPS[s], key=lambda t: _ORIG[t[0]] + t[1])], axis=1)


def _from_groups(a):
    return jnp.concatenate([_seg(a, n) for n, _ in _SEGS], axis=1)


def _cmp_rows(tok):
    s = tok.shape[0]
    r = tok.reshape(s // CMP_STRIDE, CMP_STRIDE * HEAD_DIM)
    return r, jnp.concatenate([r[1:], jnp.zeros((1, r.shape[1]), r.dtype)], axis=0)


def _cmp_unrows(dxa, dxb):
    s = dxa.shape[0] * CMP_STRIDE
    return (dxa + jnp.concatenate([jnp.zeros((1, dxa.shape[1]), dxa.dtype), dxb[:-1]], axis=0)).reshape(s, HEAD_DIM)


_GATES = ("sb_gate", "nsa_gate", "fox_gate", "mla_gate")


def _layer_fwd(x, p, c, tag):
    s = x.shape[0]
    b = _attn_block(s)
    h = _rms_fwd(x, p["pre_g"], out_dtype=_MXU, name=f"prenorm_{tag}")
    z = _matmul(h, p["w_in"], "nt", bias=p["b_in"], name=f"inproj_{tag}")
    o_sb = _sb_fwd(z, hp=HP_FWD, name=f"sb_fwd_{tag}")

    qr, ksr, kwr = _rope_call([(z, GROUP, _AL["nsa_q"] // GROUP), (z, LANES, _AL["nsa_k_sel"] // LANES),
                               (z, LANES, _AL["nsa_k_win"] // LANES)], c["tabs128"], HEAD_DIM // 2, False,
                              name=f"nsa_rope_{tag}")
    (rak, rbk), (rav, rbv) = _cmp_rows(_seg(z, "nsa_k_cmp")), _cmp_rows(_seg(z, "nsa_v_cmp"))
    ra, rb_ = jnp.stack([rak, rav]), jnp.stack([rbk, rbv])
    kvc, hp = _nsa_cmp_fwd(ra, rb_, p["cmp_pos"], p["cmp_w1"], p["cmp_w2"], c["tabs_cmp"], name=f"nsa_cmp_{tag}")
    vs, vw = _seg(z, "nsa_v_sel"), _seg(z, "nsa_v_win")
    o_nsa, oc, os_, ow, stats, sel = _nsa_fwd(qr, kvc, ksr, vs, kwr, vw, z, c["m2s"], c["e3"], name=f"nsa_fwd_{tag}")

    cum, cum_t8 = _fox_cum_fwd(z, p["fox_bias"], name=f"fox_cum_{tag}")
    cum_t = cum_t8.reshape(8, s // b, 1, b)
    fox_v = _seg(z, "fox_v")
    fcols = (_AL["fox_q"] // HEAD_DIM, _AL["fox_k"] // HEAD_DIM, 0)
    o_fox, lse_fox = _attn_fwd(z, z, fox_v, *fcols, HEAD_DIM, cum, cum_t, scale=HEAD_DIM ** -0.5, hp=HP_FWD,
                               name=f"fox_fwd_{tag}")

    qcat, kcat, vm = _mla_prep_fwd(z, p["gq"], p["gkv"], p["wuq"], p["wk"], p["wv"], c["tabs64"],
                                   name=f"mla_prep_{tag}")
    o_mla, lse_mla = _attn_fwd(qcat, kcat, vm, 0, 0, 0, MLA_QW, None, None, scale=(MLA_NOPE + MLA_ROPE) ** -0.5,
                               hp=HP_BWD, name=f"mla_fwd_{tag}")

    o_all = jnp.concatenate([o_sb, o_nsa, o_fox, o_mla], axis=1)
    gates = jnp.concatenate([_seg(z, n) for n in _GATES], axis=1)
    mix = _gate_fwd(o_all, gates, name=f"gate_{tag}")
    u = _matmul(mix, p["w_out"], "nn", name=f"outproj_{tag}")
    y = _postnorm_fwd(u, p["post_g"], x, name=f"postnorm_{tag}")
    saved = dict(x=x, h=h, z=z, qr=qr, ksr=ksr, kwr=kwr, ra=ra, rb=rb_, kvc=kvc, hp=hp, vs=vs, vw=vw, oc=oc, os=os_,
                 ow=ow, stats=stats, sel=sel, cum=cum, cum_t=cum_t, fox_v=fox_v, o_fox=o_fox, lse_fox=lse_fox, qcat=qcat, kcat=kcat,
                 vm=vm, o_mla=o_mla, lse_mla=lse_mla, o_all=o_all, gates=gates, mix=mix, u=u)
    return y, saved


def _layer_bwd(dy, sv, p, c, tag):
    z = sv["z"]
    s = z.shape[0]
    du, dg_post = _rms_bwd(dy, sv["u"], p["post_g"], name=f"postnorm_bwd_{tag}")
    dmix = _matmul(du, p["w_out"], "nt", name=f"outproj_dx_{tag}")
    dw_out = _matmul(sv["mix"], du, "tn", name=f"outproj_dw_{tag}")
    do_all, dgates = _gate_bwd(dmix, sv["o_all"], sv["gates"], name=f"gate_bwd_{tag}")
    do_sb, do_nsa, do_fox, do_mla = (do_all[:, k * GROUP:(k + 1) * GROUP] for k in range(4))
    dgate = [dgates[:, k * GROUP:(k + 1) * GROUP] for k in range(4)]

    sb_dq, sb_dk, sb_dv = _sb_bwd(z, do_sb, hp=HP_BWD, name=f"sb_bwd_{tag}")

    n_dq, n_dbr, n_dkvc, n_dks, n_dvs, n_dkw, n_dvw = _nsa_bwd(
        do_nsa, sv["qr"], sv["kvc"], sv["ksr"], sv["vs"], sv["kwr"], sv["vw"], z, sv["oc"], sv["os"], sv["ow"],
        sv["stats"], sv["sel"], c["e3"], name=f"nsa_bwd_{tag}")
    dxa, dxb, dw1, dw2 = _nsa_cmp_bwd(sv["ra"], sv["rb"], p["cmp_pos"], p["cmp_w1"], p["cmp_w2"], c["tabs_cmp"],
                                      sv["hp"], n_dkvc, name=f"nsa_cmp_bwd_{tag}")
    n_dq, n_dks, n_dkw = _rope_call([(n_dq, GROUP, 0), (n_dks, LANES, 0), (n_dkw, LANES, 0)], c["tabs128"],
                                    HEAD_DIM // 2, True, name=f"nsa_rope_bwd_{tag}")
    dpos = _colsum(jnp.concatenate([dxa[0], dxb[0], dxa[1], dxb[1]], axis=1), name=f"nsa_dpos_{tag}")
    flat = CMP_LEN * HEAD_DIM

    fcols = (_AL["fox_q"] // HEAD_DIM, _AL["fox_k"] // HEAD_DIM, 0)
    f_dq, f_dk, f_dv, f_dck = _attn_bwd(z, z, sv["fox_v"], *fcols, HEAD_DIM, do_fox, sv["o_fox"], sv["lse_fox"],
                                        sv["cum"], sv["cum_t"], scale=HEAD_DIM ** -0.5, hp=HP_BWD,
                                        name=f"fox_bwd_{tag}")
    dcum_t = jnp.pad(f_dck.reshape(N_HEADS, s), ((0, 8 - N_HEADS), (0, 0)))
    f_df, f_dbias = _fox_cum_bwd(z, p["fox_bias"], dcum_t, name=f"fox_cum_bwd_{tag}")

    m_dq, m_dk, m_dv = _attn_bwd(sv["qcat"], sv["kcat"], sv["vm"], 0, 0, 0, MLA_QW, do_mla, sv["o_mla"], sv["lse_mla"],
                                 None, None, scale=(MLA_NOPE + MLA_ROPE) ** -0.5, hp=HP_BWD, name=f"mla_bwd_{tag}")
    m_dcq, m_dckv, m_dkr, m_dwuq, m_dwk, m_dwv, m_dgq, m_dgkv = _mla_prep_bwd(
        z, p["gq"], p["gkv"], p["wuq"], p["wk"], p["wv"], c["tabs64"], m_dq, m_dk, m_dv, name=f"mla_prep_bwd_{tag}")

    dz = _to_groups(dict(
        sb_q=sb_dq, sb_k=sb_dk, sb_v=sb_dv, sb_gate=dgate[0], nsa_q=n_dq, nsa_k_cmp=_cmp_unrows(dxa[0], dxb[0]),
        nsa_v_cmp=_cmp_unrows(dxa[1], dxb[1]), nsa_k_sel=n_dks, nsa_v_sel=n_dvs, nsa_k_win=n_dkw, nsa_v_win=n_dvw,
        nsa_branch=n_dbr, nsa_gate=dgate[1], fox_q=f_dq, fox_k=f_dk, fox_v=f_dv, fox_f=f_df, fox_gate=dgate[2],
        mla_cq=m_dcq, mla_ckv=m_dckv, mla_k_rope=m_dkr, mla_gate=dgate[3]), s, _MXU)
    dh = _matmul(dz, p["w_in"], "nn", name=f"inproj_dx_{tag}")
    dw_in = _matmul(sv["h"], dz, "tn", name=f"inproj_dw_{tag}")
    db = _colsum(dz, name=f"inproj_db_{tag}")
    dx, dg_pre = _rms_bwd(dh, sv["x"], p["pre_g"], res=dy, name=f"prenorm_bwd_{tag}")

    qw = MLA_NOPE + MLA_ROPE
    grads = {
        "pre_norm_g": dg_pre[0], "post_norm_g": dg_post[0], "w_in": dw_in, "b_in": _from_groups(db)[0],
        "w_out": dw_out, "fox_forget_bias": f_dbias[0, :N_HEADS],
        "nsa_cmp_pos_k": dpos[0, :flat].reshape(CMP_LEN, HEAD_DIM), "nsa_cmp_w1_k": dw1[0], "nsa_cmp_w2_k": dw2[0],
        "nsa_cmp_pos_v": dpos[0, flat:].reshape(CMP_LEN, HEAD_DIM), "nsa_cmp_w1_v": dw1[1], "nsa_cmp_w2_v": dw2[1],
        "mla_q_norm_g": m_dgq[0],
        "mla_w_uq": jnp.concatenate([m_dwuq[:, MLA_QW * h:MLA_QW * h + qw] for h in range(N_HEADS)], axis=1),
        "mla_kv_norm_g": m_dgkv[0],
        "mla_w_ukv": jnp.concatenate(sum([[m_dwk[:, LANES * h:LANES * (h + 1)], m_dwv[:, LANES * h:LANES * (h + 1)]]
                                          for h in range(N_HEADS)], []), axis=1),
    }
    return dx, grads


def _layer_params(w, l):
    b_in = w["b_in"][l].reshape(1, -1)
    b_segs = {n: b_in[:, _ORIG[n]:_ORIG[n] + wd] for n, wd in _SEGS}
    qw = MLA_NOPE + MLA_ROPE
    w_uq, w_ukv = w["mla_w_uq"][l], w["mla_w_ukv"][l]
    uq = []
    for h in range(N_HEADS):
        uq += [w_uq[:, qw * h:qw * (h + 1)], jnp.zeros((w_uq.shape[0], MLA_QW - qw), w_uq.dtype)]
    kw_ = 2 * LANES
    flat = CMP_LEN * HEAD_DIM
    return dict(
        pre_g=w["pre_norm_g"][l].reshape(1, -1), post_g=w["post_norm_g"][l].reshape(1, -1),
        w_in=w["w_in"][l], b_in=_to_groups(b_segs, 1, f32), w_out=w["w_out"][l],
        fox_bias=jnp.pad(w["fox_forget_bias"][l], (0, LANES - N_HEADS)).reshape(1, LANES),
        cmp_pos=jnp.stack([w["nsa_cmp_pos_k"][l].reshape(1, flat), w["nsa_cmp_pos_v"][l].reshape(1, flat)]),
        cmp_w1=jnp.stack([w["nsa_cmp_w1_k"][l], w["nsa_cmp_w1_v"][l]]),
        cmp_w2=jnp.stack([w["nsa_cmp_w2_k"][l], w["nsa_cmp_w2_v"][l]]),
        gq=w["mla_q_norm_g"][l].reshape(1, -1), gkv=w["mla_kv_norm_g"][l].reshape(1, -1),
        wuq=jnp.concatenate(uq, axis=1),
        wk=jnp.concatenate([w_ukv[:, kw_ * h:kw_ * h + LANES] for h in range(N_HEADS)], axis=1),
        wv=jnp.concatenate([w_ukv[:, kw_ * h + LANES:kw_ * (h + 1)] for h in range(N_HEADS)], axis=1),
    )


def _consts(s):
    pos = jnp.arange(s)
    m2s, e3 = _nsa_consts(s)
    return dict(tabs128=_rope_tables(pos, HEAD_DIM), tabs64=_rope_tables(pos, MLA_ROPE),
                tabs_cmp=_rope_tables(jnp.arange(s // CMP_STRIDE) * CMP_STRIDE + (CMP_LEN - 1), HEAD_DIM),
                m2s=m2s, e3=e3)


def _place():
    return lax.axis_index("x"), lax.axis_index("y"), lax.axis_index("c")


def _other_chips(x, y):
    return [(1 - x, y), (x, 1 - y), (1 - x, 1 - y)]


def _comm_call(body, out_shapes, n_sems, arrs, name):
    return pl.pallas_call(body, out_shape=tuple(out_shapes), in_specs=[_ANY] * len(arrs),
                          out_specs=tuple(_ANY for _ in out_shapes),
                          scratch_shapes=[pltpu.SemaphoreType.DMA((n_sems,)), pltpu.SemaphoreType.DMA((n_sems,))],
                          name=name)(*arrs)


def _gather_chips(arrs, *, name):
    n = len(arrs)

    def body(*refs):
        a_refs, out_refs, send_sems, recv_sems = refs[:n], refs[n:2 * n], refs[2 * n], refs[2 * n + 1]
        x, y, c = _place()
        me = 2 * x + y
        sibling = (x, y, 1 - c)
        chips = _other_chips(x, y)

        def copy(j, k, src, dst, to):
            return pltpu.make_async_remote_copy(src, dst, send_sems.at[6 * j + k], recv_sems.at[6 * j + k],
                                                device_id=to, device_id_type=_MESH)

        first = [copy(j, k, a_refs[j].at[c], out_refs[j].at[me, c], (px, py, c))
                 for k, (px, py) in enumerate(chips) for j in range(n)]
        for cp in first:
            cp.start()
        passed = []
        for k, (px, py) in enumerate(chips):
            for j in range(n):
                landed = out_refs[j].at[2 * px + py, c]
                copy(j, k, a_refs[j].at[c], landed, (px, py, c)).wait_recv()
                passed.append(copy(j, 3 + k, landed, landed, sibling))
                passed[-1].start()
        for k, (px, py) in enumerate(chips):
            for j in range(n):
                copy(j, 3 + k, a_refs[j].at[c], out_refs[j].at[2 * px + py, 1 - c], sibling).wait_recv()
        for cp in first + passed:
            cp.wait_send()

    return _comm_call(body, [_SDS((N_CHIPS,) + a.shape, a.dtype) for a in arrs], 6 * n, arrs, name)


def _alltoall_chips(arrs, lane_slots, *, name):
    n = len(arrs)

    def slot(ref, lanes, s):
        if lanes:
            w = ref.shape[2] // N_CHIPS
            return ref.at[0, :, pl.ds(s * w, w)]
        return ref.at[s]

    def body(*refs):
        g_refs, out_refs, send_sems, recv_sems = refs[:n], refs[n:2 * n], refs[2 * n], refs[2 * n + 1]
        x, y, c = _place()
        me = 2 * x + y

        def copy(j, s):
            return pltpu.make_async_remote_copy(slot(g_refs[j], lane_slots[j], s), out_refs[j].at[me],
                                                send_sems.at[N_CHIPS * j + s], recv_sems.at[N_CHIPS * j + me],
                                                device_id=(s // 2, s % 2, c), device_id_type=_MESH)

        for s in range(N_CHIPS):
            @pl.when(s != me)
            def _():
                for j in range(n):
                    copy(j, s).start()
        for t in range(N_CHIPS):
            @pl.when(t != me)
            def _():
                for j in range(n):
                    pltpu.make_async_remote_copy(slot(g_refs[j], lane_slots[j], t), out_refs[j].at[t],
                                                 send_sems.at[N_CHIPS * j + t], recv_sems.at[N_CHIPS * j + t],
                                                 device_id=(t // 2, t % 2, c), device_id_type=_MESH).wait_recv()
        for s in range(N_CHIPS):
            @pl.when(s != me)
            def _():
                for j in range(n):
                    copy(j, s).wait_send()

    outs = [_SDS((N_CHIPS, a.shape[1], a.shape[2] // N_CHIPS if lanes else a.shape[2]), a.dtype)
            for a, lanes in zip(arrs, lane_slots)]
    return _comm_call(body, outs, N_CHIPS * n, arrs, name)


def _swap_other_half(arrs, *, name):
    n = len(arrs)

    def body(*refs):
        g_refs, out_refs, send_sems, recv_sems = refs[:n], refs[n:2 * n], refs[2 * n], refs[2 * n + 1]
        x, y, c = _place()
        cps = [pltpu.make_async_remote_copy(g_refs[j].at[:, 1 - c], out_refs[j], send_sems.at[j], recv_sems.at[j],
                                            device_id=(x, y, 1 - c), device_id_type=_MESH) for j in range(n)]
        for cp in cps:
            cp.start()
        for cp in cps:
            cp.wait()

    return _comm_call(body, [_SDS((a.shape[0],) + a.shape[2:], a.dtype) for a in arrs], n, arrs, name)


def _swap_sibling(arrs, *, name):
    n = len(arrs)

    def body(*refs):
        f_refs, out_refs, send_sems, recv_sems = refs[:n], refs[n:2 * n], refs[2 * n], refs[2 * n + 1]
        x, y, c = _place()
        cps = [pltpu.make_async_remote_copy(f_refs[j], out_refs[j], send_sems.at[j], recv_sems.at[j],
                                            device_id=(x, y, 1 - c), device_id_type=_MESH) for j in range(n)]
        for cp in cps:
            cp.start()
        for cp in cps:
            cp.wait()

    return _comm_call(body, [_SDS(a.shape, a.dtype) for a in arrs], n, arrs, name)


_HBM = pl.BlockSpec(memory_space=pltpu.HBM)
_SEM = pl.BlockSpec(memory_space=pltpu.SEMAPHORE)
_EFFECT = pltpu.SideEffectType.DATAFLOW_SIDE_EFFECTING


def _slot_ref(ref, mode, s):
    if mode == "same":
        return ref
    if mode == "lanes":
        w = ref.shape[-1] // N_CHIPS
        return ref.at[:, pl.ds(s * w, w)]
    return ref.at[s]


def _slot_shape(a, mode):
    return {"same": a.shape, "lanes": a.shape[:-1] + (a.shape[-1] // N_CHIPS,), "slots": a.shape[1:]}[mode]


def _send_start(arrs, modes, after, *, name):
    n = len(arrs)
    lands = [lax.empty((N_CHIPS,) + _slot_shape(a, m), a.dtype) for a, m in zip(arrs, modes)]

    def body(*refs):
        srcs, land_refs, send_sems, recv_sems, token = refs[:n], refs[n:2 * n], refs[2 * n + 1], refs[2 * n + 2], refs[-1]
        x, y, c = _place()
        me = 2 * x + y
        for s in range(N_CHIPS):
            @pl.when(s != me)
            def _():
                for j in range(n):
                    pltpu.make_async_remote_copy(_slot_ref(srcs[j], modes[j], s), land_refs[j].at[me],
                                                 send_sems.at[N_CHIPS * j + s], recv_sems.at[N_CHIPS * j + me],
                                                 device_id=(s // 2, s % 2, c), device_id_type=_MESH).start()
        token[...] = jnp.zeros_like(token)

    hbm = lambda a: pltpu.HBM(a.shape, a.dtype)
    sems = pltpu.SemaphoreType.DMA((N_CHIPS * n,))
    out = pl.pallas_call(
        body, name=name, out_shape=(sems, sems, *[hbm(a) for a in arrs], *[hbm(a) for a in lands], _SDS((8, LANES), f32)),
        in_specs=[_HBM] * (2 * n) + [_ANY], out_specs=(_SEM, _SEM, *[_HBM] * (2 * n), pl.BlockSpec(memory_space=pltpu.VMEM)),
        input_output_aliases={j: 2 + j for j in range(2 * n)},
        compiler_params=pltpu.CompilerParams(has_side_effects=_EFFECT),
    )(*[pltpu.with_memory_space_constraint(a, pltpu.HBM) for a in arrs + lands], after)
    return out[:-1], out[-1]


def _send_wait(started, modes, after, *, name):
    send_sems, recv_sems = started[0], started[1]
    n = (len(started) - 2) // 2
    thru = list(started[2:])

    def body(*refs):
        srcs, land_refs, send_sems, recv_sems = refs[:n], refs[n:2 * n], refs[2 * n], refs[2 * n + 1]
        x, y, c = _place()
        me = 2 * x + y
        for s in range(N_CHIPS):
            @pl.when(s != me)
            def _():
                for j in range(n):
                    cp = pltpu.make_async_remote_copy(_slot_ref(srcs[j], modes[j], s), land_refs[j].at[s],
                                                      send_sems.at[N_CHIPS * j + s], recv_sems.at[N_CHIPS * j + s],
                                                      device_id=(s // 2, s % 2, c), device_id_type=_MESH)
                    cp.wait_send()
                    cp.wait_recv()

    hbm = lambda a: pltpu.HBM(a.shape, a.dtype)
    out = pl.pallas_call(
        body, name=name, out_shape=tuple(hbm(a) for a in thru), in_specs=[_HBM] * (2 * n) + [_SEM, _SEM, _ANY],
        out_specs=tuple([_HBM] * (2 * n)), input_output_aliases={j: j for j in range(2 * n)},
        compiler_params=pltpu.CompilerParams(has_side_effects=_EFFECT),
    )(*thru, send_sems, recv_sems, after)
    return list(out[n:])


def _gather_all(a, *, name):
    def body(a_ref, out_ref, send_sems, recv_sems, local_sem):
        x, y, c = _place()
        flip = lambda v, f: (1 - v) if f else v
        peers = [(flip(x, f & 4), flip(y, f & 2), flip(c, f & 1)) for f in range(1, 8)]
        me = 4 * x + 2 * y + c
        mine = pltpu.make_async_copy(a_ref, out_ref.at[me], local_sem)
        mine.start()
        sends = [pltpu.make_async_remote_copy(a_ref, out_ref.at[me], send_sems.at[k], recv_sems.at[k], device_id=peer,
                                              device_id_type=_MESH) for k, peer in enumerate(peers)]
        for cp in sends:
            cp.start()
        for k, (px, py, pc) in enumerate(peers):
            pltpu.make_async_remote_copy(a_ref, out_ref.at[4 * px + 2 * py + pc], send_sems.at[k], recv_sems.at[k],
                                         device_id=(px, py, pc), device_id_type=_MESH).wait_recv()
        for cp in sends:
            cp.wait_send()
        mine.wait()

    return pl.pallas_call(body, out_shape=_SDS((8,) + a.shape, a.dtype), in_specs=[_ANY], out_specs=_ANY,
                          scratch_shapes=[pltpu.SemaphoreType.DMA((7,)), pltpu.SemaphoreType.DMA((7,)),
                                          pltpu.SemaphoreType.DMA], name=name)(a)


def _add_my_half(g, r, *, name):
    p, _, h, w = g.shape
    tw = _pick(w, (2048, 1024, 512, 256, 128))
    rb = max(d for d in range(16, h + 1, 16) if h % d == 0 and d * tw * 4 <= (2 << 20))

    def body(c_ref, g_ref, r_ref, o_ref):
        o_ref[...] = (g_ref[...] + r_ref[...]).astype(o_ref.dtype)

    blk = pl.BlockSpec((None, rb, tw), lambda s, i, j, c_ref: (s, i, j))
    grid_spec = pltpu.PrefetchScalarGridSpec(
        num_scalar_prefetch=1, grid=(p, h // rb, w // tw),
        in_specs=[pl.BlockSpec((None, None, rb, tw), lambda s, i, j, c_ref: (s, c_ref[0], i, j)), blk], out_specs=blk)
    c = lax.axis_index("c").astype(jnp.int32).reshape(1)
    return pl.pallas_call(body, out_shape=_SDS((p, h, w), _WIRE), grid_spec=grid_spec,
                          compiler_params=_cp(("parallel", "parallel", "parallel")), name=name)(c, g, r)


_WEIGHTS = ("pre_norm_g", "post_norm_g", "w_in", "b_in", "w_out", "fox_forget_bias", "nsa_cmp_pos_k", "nsa_cmp_w1_k",
            "nsa_cmp_w2_k", "nsa_cmp_pos_v", "nsa_cmp_w1_v", "nsa_cmp_w2_v", "mla_q_norm_g", "mla_w_uq",
            "mla_kv_norm_g", "mla_w_ukv")
_SHARD_AXIS = {"w_in": 2, "w_out": 1, "nsa_cmp_w1_k": 1, "nsa_cmp_w1_v": 1, "mla_w_uq": 2, "mla_w_ukv": 2}
_PACK_UNIT = 16 * LANES


def _pack(arrays, dtype):
    rows = []
    for a in arrays:
        v = a.astype(dtype).reshape(-1)
        pad = (-v.shape[0]) % _PACK_UNIT
        if pad:
            v = jnp.concatenate([v, jnp.zeros((pad,), dtype)])
        rows.append(v.reshape(-1, LANES))
    return jnp.concatenate(rows, axis=0)


def _unpack(flat, shapes):
    out, r = [], 0
    for shp in shapes:
        n = int(np.prod(shp))
        nr = -(-n // _PACK_UNIT) * (_PACK_UNIT // LANES)
        out.append(flat[r:r + nr].reshape(-1)[:n].reshape(shp))
        r += nr
    return out


def kernel(x, pre_norm_g, post_norm_g, w_in, b_in, w_out, fox_forget_bias, nsa_cmp_pos_k, nsa_cmp_w1_k, nsa_cmp_w2_k, nsa_cmp_pos_v, nsa_cmp_w1_v, nsa_cmp_w2_v, mla_q_norm_g, mla_w_uq, mla_kv_norm_g, mla_w_ukv, loss_target, m_pre_norm_g, m_post_norm_g, m_w_in, m_b_in, m_w_out, m_fox_forget_bias, m_nsa_cmp_pos_k, m_nsa_cmp_w1_k, m_nsa_cmp_w2_k, m_nsa_cmp_pos_v, m_nsa_cmp_w1_v, m_nsa_cmp_w2_v, m_mla_q_norm_g, m_mla_w_uq, m_mla_kv_norm_g, m_mla_w_ukv, v_pre_norm_g, v_post_norm_g, v_w_in, v_b_in, v_w_out, v_fox_forget_bias, v_nsa_cmp_pos_k, v_nsa_cmp_w1_k, v_nsa_cmp_w2_k, v_nsa_cmp_pos_v, v_nsa_cmp_w1_v, v_nsa_cmp_w2_v, v_mla_q_norm_g, v_mla_w_uq, v_mla_kv_norm_g, v_mla_w_ukv):
    given = dict(locals())
    local = {n: given[n] for n in _WEIGHTS}
    depth = pre_norm_g.shape[0]
    xs, target = x[0], loss_target[0]
    s = xs.shape[0]
    sharded = [n for n in _WEIGHTS if n in _SHARD_AXIS and n != "w_in"]
    small = [n for n in _WEIGHTS if n not in _SHARD_AXIS]
    chip = 2 * lax.axis_index("x") + lax.axis_index("y")
    core = lax.axis_index("c")
    own = lambda slots, mine: lax.dynamic_update_slice_in_dim(slots, mine[None], chip, axis=0)

    w_in_t = jnp.swapaxes(w_in, 1, 2).astype(_MXU)
    piece = lax.switch(chip, [functools.partial(_piece_from_shard, s=k) for k in range(N_CHIPS)], w_in_t)
    shard_shapes = [local[n].shape for n in sharded]
    flat = _pack([local[n] for n in sharded], _MXU)
    flat2 = flat.reshape((2, -1, LANES))
    first = piece[0].reshape(2, GROUP_W // 2, D_MODEL)
    first_all, flat_all = _gather_chips([first, flat2], name="gather_weights")
    flat_all = own(flat_all, flat2).reshape((N_CHIPS,) + flat.shape)
    per_chip = [_unpack(flat_all[k], shard_shapes) for k in range(N_CHIPS)]
    full = dict(local)
    full["w_in"] = [own(first_all, first).reshape(N_CHIPS, GROUP_W, D_MODEL)]
    for j, n in enumerate(sharded):
        full[n] = jnp.concatenate([per_chip[k][j] for k in range(N_CHIPS)], axis=_SHARD_AXIS[n])
    later = [piece[l] for l in range(1, depth)]
    started, token = _send_start(later, ["same"] * len(later), flat_all, name="gather_later_start")
    full["pre_norm_g"] = pre_norm_g + token[0, 0]

    consts = _consts(s)
    params, act, saved = [], xs, []
    for l in range(depth):
        if l == 1:
            landed = _send_wait(started, ["same"] * len(later), act, name="gather_later_wait")
            full["w_in"] += [own(a, b) for a, b in zip(landed, later)]
        params.append(_layer_params(full, l))
        act, sv = _layer_fwd(act, params[l], consts, f"l{l}")
        saved.append(sv)
    dy, loss_parts = _loss_head(act, target, name="loss_head")
    layer_grads = [None] * depth
    for l in reversed(range(depth)):
        dy, layer_grads[l] = _layer_bwd(dy, saved[l], params[l], consts, f"l{l}")
    grad_x = dy[None]
    grads = {n: jnp.stack([layer_grads[l][n] for l in range(depth)]) for n in _WEIGHTS if n != "w_in"}

    def chip_slice(n, k):
        a, ax = grads[n], _SHARD_AXIS[n]
        w = a.shape[ax] // N_CHIPS
        return lax.slice_in_dim(a, k * w, (k + 1) * w, axis=ax)

    g_flat = jnp.stack([_pack([chip_slice(n, k) for n in sharded], f32) for k in range(N_CHIPS)])
    halves = [layer_grads[l]["w_in"].reshape(1, 2, D_MODEL // 2, ZW) for l in range(depth)]
    halves.append(g_flat.reshape(N_CHIPS, 2, -1, LANES))
    lane_slots = [True] * depth + [False]
    from_sibling = _swap_other_half(halves, name="reduce_pair")
    pair_sum = [_add_my_half(g, r, name=f"reduce_pair_add{j}") for j, (g, r) in enumerate(zip(halves, from_sibling))]
    from_chips = _alltoall_chips(pair_sum, lane_slots, name="reduce_chips")
    my_half = []
    for j, (slots, ps, lanes) in enumerate(zip(from_chips, pair_sum, lane_slots)):
        mine = lax.dynamic_slice_in_dim(ps[0], chip * GROUP_W, GROUP_W, axis=1) if lanes else \
            lax.dynamic_index_in_dim(ps, chip, axis=0, keepdims=False)
        my_half.append(_sum_slots(own(slots, mine), name=f"reduce_chips_add{j}"))
    their_half = _swap_sibling(my_half, name="reduce_share")
    first = core == 0
    both = [jnp.concatenate([jnp.where(first, a, b), jnp.where(first, b, a)], axis=0)
            for a, b in zip(my_half, their_half)]
    summed = dict(zip(sharded, _unpack(both[depth], shard_shapes)))
    unpiece = [functools.partial(_shard_from_piece, s=k) for k in range(N_CHIPS)]
    summed["w_in"] = jnp.stack([lax.switch(chip, unpiece, both[l]) for l in range(depth)])

    loss_row = jnp.concatenate([jnp.sum(loss_parts).reshape(1), jnp.zeros((LANES - 1,), f32)])
    small_shapes = [(LANES,)] + [grads[n].shape for n in small]
    contrib = _pack([loss_row] + [grads[n] for n in small], f32)
    pad_rows = (-contrib.shape[0]) % 8
    if pad_rows:
        contrib = jnp.concatenate([contrib, jnp.zeros((pad_rows, LANES), f32)], axis=0)
    total = _unpack(_sum_slots(_gather_all(contrib, name="gather_small"), name="sum_small"), small_shapes)
    loss = total[0][0]
    summed.update(zip(small, total[1:]))

    deltas, new_m, new_v = {}, {}, {}
    for n in _WEIGHTS:
        deltas[n], new_m[n], new_v[n] = _adamw(local[n], summed[n], given["m_" + n], given["v_" + n], name=f"adamw_{n}")
    return (loss, grad_x, *[summed[n] for n in _WEIGHTS], *[deltas[n] for n in _WEIGHTS],
            *[new_m[n] for n in _WEIGHTS], *[new_v[n] for n in _WEIGHTS])
```

```python
import functools
import math

import numpy as np
import jax
import jax.numpy as jnp
from jax import lax
from jax.experimental import pallas as pl
from jax.experimental.pallas import tpu as pltpu

f32 = jnp.float32
bf16 = jnp.bfloat16
_MXU = jnp.bfloat16
_WIRE = jnp.bfloat16
_SDS = jax.ShapeDtypeStruct
_ANY = pl.BlockSpec(memory_space=pl.ANY)
_MESH = pl.DeviceIdType.MESH

D_MODEL = 2048
N_HEADS = 4
HEAD_DIM = 128
GROUP = 512
RMS_EPS = 1e-6
NEG_INF = -1e30
ROPE_THETA = 10000.0
CMP_LEN, CMP_STRIDE, SEL_LEN, SEL_TOPN, WINDOW = 32, 16, 64, 16, 512
FORCED_BONUS = 1e6
MLA_Q_RANK, MLA_KV_RANK, MLA_NOPE, MLA_ROPE = 384, 128, 128, 64
ADAM_LR, ADAM_B1, ADAM_B2, ADAM_EPS, ADAM_WD, ADAM_STEP = 0.001, 0.9, 0.999, 1e-08, 0.01, 10
LANES = 128
VMEM_LIMIT = 48 * 1024 * 1024
HP_FWD, HP_BWD = 4, 2

_SEGS = (
    ("sb_q", 512), ("sb_k", 512), ("sb_v", 512), ("sb_gate", 512), ("nsa_q", 512), ("nsa_k_cmp", 128),
    ("nsa_v_cmp", 128), ("nsa_k_sel", 128), ("nsa_v_sel", 128), ("nsa_k_win", 128), ("nsa_v_win", 128),
    ("nsa_branch", 12), ("nsa_gate", 512), ("fox_q", 512), ("fox_k", 512), ("fox_v", 512), ("fox_f", 4),
    ("fox_gate", 512), ("mla_cq", 384), ("mla_ckv", 128), ("mla_k_rope", 64), ("mla_gate", 512),
)
_ORIG, _WID = {}, {}
_o = 0
for _n, _w in _SEGS:
    _ORIG[_n], _WID[_n] = _o, _w
    _o += _w
IN_WIDTH = _o
N_CHIPS = 4
CHIP_COLS = IN_WIDTH // N_CHIPS
GROUP_W = 2048
ZW = N_CHIPS * GROUP_W
_GROUPS = (
    (("sb_q", 0, 512, 0), ("sb_k", 0, 512, 512), ("sb_v", 0, 512, 1024), ("sb_gate", 0, 212, 1536)),
    (("nsa_q", 0, 512, 0), ("nsa_k_cmp", 0, 128, 512), ("nsa_v_cmp", 0, 128, 640), ("nsa_k_sel", 0, 128, 768),
     ("nsa_v_sel", 0, 128, 896), ("nsa_k_win", 0, 128, 1024), ("nsa_v_win", 0, 128, 1152), ("nsa_branch", 0, 12, 1280),
     ("sb_gate", 212, 512, 1408), ("nsa_gate", 0, 156, 1712)),
    (("fox_q", 0, 512, 0), ("fox_k", 0, 512, 512), ("fox_v", 0, 368, 1024), ("nsa_gate", 156, 512, 1408)),
    (("mla_cq", 0, 384, 0), ("mla_ckv", 0, 128, 384), ("mla_k_rope", 0, 64, 512), ("fox_f", 0, 4, 640),
     ("fox_v", 368, 512, 768), ("fox_gate", 0, 512, 1024), ("mla_gate", 0, 512, 1536)),
)
_PIECES = {n: [] for n, _ in _SEGS}
for _s, _grp in enumerate(_GROUPS):
    _cover = sorted((_ORIG[n] + lo, _ORIG[n] + hi) for n, lo, hi, _ in _grp)
    assert _cover[0][0] == _s * CHIP_COLS and _cover[-1][1] == (_s + 1) * CHIP_COLS
    assert all(a[1] == b[0] for a, b in zip(_cover, _cover[1:]))
    _ends = sorted((off, off + hi - lo) for _, lo, hi, off in _grp)
    assert all(a[1] <= b[0] for a, b in zip(_ends, _ends[1:])) and _ends[-1][1] <= GROUP_W
    assert _ends[0][0] == 0 and all(e[0] % 16 == 0 for e in _ends)
    for _n, _lo, _hi, _off in _grp:
        _PIECES[_n].append((_s * GROUP_W + _off, _lo, _hi))
_AL = {n: p[0][0] for n, p in _PIECES.items() if len(p) == 1}


def _cp(sem=None):
    return pltpu.CompilerParams(dimension_semantics=sem, vmem_limit_bytes=VMEM_LIMIT)


def _mm(a, b):
    return jnp.dot(a.astype(_MXU), b.astype(_MXU), preferred_element_type=f32)


def _mm_nt(a, b):
    return lax.dot_general(a.astype(_MXU), b.astype(_MXU), (((1,), (1,)), ((), ())), preferred_element_type=f32)


def _mm_tn(a, b):
    return lax.dot_general(a.astype(_MXU), b.astype(_MXU), (((0,), (0,)), ((), ())), preferred_element_type=f32)


def _mm_split(x, t):
    hi = x.astype(_MXU)
    lo = (x - hi.astype(f32)).astype(_MXU)
    return jnp.dot(hi, t, preferred_element_type=f32) + jnp.dot(lo, t, preferred_element_type=f32)


def _sigmoid(x):
    return 1.0 / (1.0 + jnp.exp(-x))


def _iota(shape, dim):
    return lax.broadcasted_iota(jnp.int32, shape, dim)


def _pick(n, prefs):
    for p in prefs:
        if n % p == 0:
            return p
    return n


def _matmul(a, b, mode, *, bias=None, out_dtype=f32, name):
    grouped = b.ndim == 3
    b_shape = (b.shape[0] * b.shape[1], b.shape[2]) if grouped else b.shape
    if mode == "nn":
        (M, K), (K2, N) = a.shape, b_shape
    elif mode == "nt":
        (M, K), (N, K2) = a.shape, b_shape
    else:
        (K, M), (K2, N) = a.shape, b_shape
    assert K == K2
    tm = _pick(M, (512, 384, 256, 128))
    tn = _pick(N, (512, 384, 256, 128))
    tk = K if K <= 2048 else _pick(K, (2048, 2432, 1024, 512))
    nk = K // tk
    a_spec = {"nn": pl.BlockSpec((tm, tk), lambda i, j, k: (i, k)),
              "nt": pl.BlockSpec((tm, tk), lambda i, j, k: (i, k)),
              "tn": pl.BlockSpec((tk, tm), lambda i, j, k: (k, i))}[mode]
    if not grouped:
        b_spec = {"nn": pl.BlockSpec((tk, tn), lambda i, j, k: (k, j)),
                  "nt": pl.BlockSpec((tn, tk), lambda i, j, k: (j, k)),
                  "tn": pl.BlockSpec((tk, tn), lambda i, j, k: (k, j))}[mode]
    elif mode == "nt":
        per = b.shape[1] // tn
        b_spec = pl.BlockSpec((None, tn, tk), lambda i, j, k: (j // per, j % per, k))
    else:
        assert mode == "nn"
        per = b.shape[1] // tk
        b_spec = pl.BlockSpec((None, tk, tn), lambda i, j, k: (k // per, k % per, j))
    dot = {"nn": _mm, "nt": _mm_nt, "tn": _mm_tn}[mode]
    has_bias = bias is not None

    def body(*refs):
        if has_bias:
            a_ref, b_ref, bias_ref, o_ref, acc_ref = refs
        else:
            a_ref, b_ref, o_ref, acc_ref = refs
            bias_ref = None
        k = pl.program_id(2)
        part = dot(a_ref[...], b_ref[...])

        def finish(total):
            if has_bias:
                total = total + bias_ref[...]
            o_ref[...] = total.astype(o_ref.dtype)

        if nk == 1:
            finish(part)
        else:
            @pl.when(k == 0)
            def _():
                acc_ref[...] = part

            @pl.when(k > 0)
            def _():
                acc_ref[...] += part

            @pl.when(k == nk - 1)
            def _():
                finish(acc_ref[...])

    in_specs = [a_spec, b_spec]
    args = [a, b]
    if has_bias:
        in_specs.append(pl.BlockSpec((1, tn), lambda i, j, k: (0, j)))
        args.append(bias.reshape(1, N))
    return pl.pallas_call(
        body, out_shape=_SDS((M, N), out_dtype), grid=(M // tm, N // tn, nk),
        in_specs=in_specs, out_specs=pl.BlockSpec((tm, tn), lambda i, j, k: (i, j)),
        scratch_shapes=[pltpu.VMEM((tm, tn), f32)],
        compiler_params=_cp(("parallel", "parallel", "arbitrary")), name=name,
    )(*args)


def _row_block(s):
    return _pick(s, (256, 128))


def _rms_fwd(x, g, *, out_dtype, name):
    s, d = x.shape
    rb = _row_block(s)

    def body(x_ref, g_ref, o_ref):
        xv = x_ref[...]
        r = lax.rsqrt(jnp.mean(xv * xv, axis=-1, keepdims=True) + RMS_EPS)
        o_ref[...] = (xv * r * g_ref[...]).astype(o_ref.dtype)

    return pl.pallas_call(
        body, out_shape=_SDS((s, d), out_dtype), grid=(s // rb,),
        in_specs=[pl.BlockSpec((rb, d), lambda i: (i, 0)), pl.BlockSpec((1, d), lambda i: (0, 0))],
        out_specs=pl.BlockSpec((rb, d), lambda i: (i, 0)), compiler_params=_cp(("parallel",)), name=name,
    )(x, g.reshape(1, d))


def _postnorm_fwd(u, g, x, *, name):
    s, d = u.shape
    rb = _row_block(s)

    def body(u_ref, g_ref, x_ref, o_ref):
        uv = u_ref[...]
        r = lax.rsqrt(jnp.mean(uv * uv, axis=-1, keepdims=True) + RMS_EPS)
        o_ref[...] = x_ref[...] + uv * r * g_ref[...]

    return pl.pallas_call(
        body, out_shape=_SDS((s, d), f32), grid=(s // rb,),
        in_specs=[pl.BlockSpec((rb, d), lambda i: (i, 0)), pl.BlockSpec((1, d), lambda i: (0, 0)),
                  pl.BlockSpec((rb, d), lambda i: (i, 0))],
        out_specs=pl.BlockSpec((rb, d), lambda i: (i, 0)), compiler_params=_cp(("parallel",)), name=name,
    )(u, g.reshape(1, d), x)


def _fold_rows(v):
    r = v.shape[0]
    acc = v[0:8]
    for k in range(1, r // 8):
        acc = acc + v[8 * k:8 * k + 8]
    return acc


def _rms_bwd(dy, x, g, res=None, *, name):
    s, d = x.shape
    rb = _row_block(s)
    nb = s // rb
    has_res = res is not None

    def body(*refs):
        if has_res:
            dy_ref, x_ref, g_ref, res_ref, dx_ref, dg_ref, acc_ref = refs
        else:
            dy_ref, x_ref, g_ref, dx_ref, dg_ref, acc_ref = refs
        i = pl.program_id(0)
        xv = x_ref[...]
        r = lax.rsqrt(jnp.mean(xv * xv, axis=-1, keepdims=True) + RMS_EPS)
        xh = xv * r
        dyv = dy_ref[...]
        dxh = dyv * g_ref[...]
        dx = r * (dxh - xh * jnp.mean(dxh * xh, axis=-1, keepdims=True))
        if has_res:
            dx = dx + res_ref[...]
        dx_ref[...] = dx
        part = _fold_rows(dyv * xh)

        @pl.when(i == 0)
        def _():
            acc_ref[...] = part

        @pl.when(i > 0)
        def _():
            acc_ref[...] += part

        @pl.when(i == nb - 1)
        def _():
            dg_ref[...] = jnp.sum(acc_ref[...], axis=0, keepdims=True)

    blk = pl.BlockSpec((rb, d), lambda i: (i, 0))
    in_specs = [blk, blk, pl.BlockSpec((1, d), lambda i: (0, 0))] + ([blk] if has_res else [])
    args = [dy, x, g.reshape(1, d)] + ([res] if has_res else [])
    return pl.pallas_call(
        body, out_shape=(_SDS((s, d), f32), _SDS((1, d), f32)), grid=(nb,), in_specs=in_specs,
        out_specs=(blk, pl.BlockSpec((1, d), lambda i: (0, 0))),
        scratch_shapes=[pltpu.VMEM((8, d), f32)], compiler_params=_cp(("arbitrary",)), name=name,
    )(*args)


def _loss_head(y, target, *, name):
    s, d = y.shape
    rb = _row_block(s)
    nb = s // rb

    def body(y_ref, t_ref, dy_ref, l_ref):
        i = pl.program_id(0)
        e = y_ref[...] - t_ref[...]
        dy_ref[...] = e * (1.0 / d)
        rows = _fold_rows(e * e)
        part = rows[:, 0:LANES]
        for k in range(1, d // LANES):
            part = part + rows[:, k * LANES:(k + 1) * LANES]
        part = part * (0.5 / d)

        @pl.when(i == 0)
        def _():
            l_ref[...] = part

        @pl.when(i > 0)
        def _():
            l_ref[...] += part

    blk = pl.BlockSpec((rb, d), lambda i: (i, 0))
    return pl.pallas_call(
        body, out_shape=(_SDS((s, d), f32), _SDS((8, LANES), f32)), grid=(nb,), in_specs=[blk, blk],
        out_specs=(blk, pl.BlockSpec((8, LANES), lambda i: (0, 0))),
        compiler_params=_cp(("arbitrary",)), name=name,
    )(y, target)


def _colsum(a, *, name):
    s, n = a.shape
    rb = _row_block(s)
    nb = s // rb
    tn = _pick(n, (2432, 2048, 1024, 512, 384, 128))

    def body(a_ref, o_ref, acc_ref):
        i = pl.program_id(1)
        part = _fold_rows(a_ref[...].astype(f32))

        @pl.when(i == 0)
        def _():
            acc_ref[...] = part

        @pl.when(i > 0)
        def _():
            acc_ref[...] += part

        @pl.when(i == nb - 1)
        def _():
            o_ref[...] = jnp.sum(acc_ref[...], axis=0, keepdims=True)

    return pl.pallas_call(
        body, out_shape=_SDS((1, n), f32), grid=(n // tn, nb),
        in_specs=[pl.BlockSpec((rb, tn), lambda j, i: (i, j))], out_specs=pl.BlockSpec((1, tn), lambda j, i: (0, j)),
        scratch_shapes=[pltpu.VMEM((8, tn), f32)], compiler_params=_cp(("parallel", "arbitrary")), name=name,
    )(a)


def _gate_fwd(o, gate, *, name):
    s, d = o.shape
    rb = _row_block(s)

    def body(o_ref, g_ref, m_ref):
        gv = g_ref[...]
        m_ref[...] = (o_ref[...] * (gv * _sigmoid(gv))).astype(m_ref.dtype)

    blk = pl.BlockSpec((rb, d), lambda i: (i, 0))
    return pl.pallas_call(body, out_shape=_SDS((s, d), _MXU), grid=(s // rb,), in_specs=[blk, blk], out_specs=blk,
                          compiler_params=_cp(("parallel",)), name=name)(o, gate)


def _gate_bwd(dmix, o, gate, *, name):
    s, d = o.shape
    rb = _row_block(s)

    def body(dm_ref, o_ref, g_ref, do_ref, dg_ref):
        gv = g_ref[...]
        sg = _sigmoid(gv)
        dm = dm_ref[...]
        do_ref[...] = dm * (gv * sg)
        dg_ref[...] = dm * o_ref[...] * (sg * (1.0 + gv * (1.0 - sg)))

    blk = pl.BlockSpec((rb, d), lambda i: (i, 0))
    return pl.pallas_call(body, out_shape=(_SDS((s, d), f32), _SDS((s, d), f32)), grid=(s // rb,),
                          in_specs=[blk, blk, blk], out_specs=(blk, blk), compiler_params=_cp(("parallel",)),
                          name=name)(dmix, o, gate)


def _adamw(w, g, m, v, *, name):
    shape = w.shape
    cols = shape[-1]
    rows = int(np.prod(shape[:-1])) if len(shape) > 1 else 1
    to2 = lambda t: t.reshape(rows, cols)
    rb = _pick(rows, (128, 64, 32, 16, 8)) if rows * cols * 4 > (1 << 20) else rows

    def body(w_ref, g_ref, m_ref, v_ref, d_ref, nm_ref, nv_ref):
        gv = g_ref[...]
        mn = ADAM_B1 * m_ref[...] + (1.0 - ADAM_B1) * gv
        vn = ADAM_B2 * v_ref[...] + (1.0 - ADAM_B2) * (gv * gv)
        m_hat = mn / (1.0 - ADAM_B1 ** ADAM_STEP)
        v_hat = vn / (1.0 - ADAM_B2 ** ADAM_STEP)
        d_ref[...] = -ADAM_LR * (m_hat / (jnp.sqrt(v_hat) + ADAM_EPS) + ADAM_WD * w_ref[...])
        nm_ref[...] = mn
        nv_ref[...] = vn

    blk = pl.BlockSpec((rb, cols), lambda i: (i, 0))
    out = pl.pallas_call(body, out_shape=tuple(_SDS((rows, cols), f32) for _ in range(3)), grid=(rows // rb,),
                         in_specs=[blk] * 4, out_specs=(blk,) * 3, compiler_params=_cp(("parallel",)),
                         name=name)(to2(w), to2(g), to2(m), to2(v))
    return tuple(t.reshape(shape) for t in out)


def _sum_slots(a, *, name):
    p, n, c = a.shape
    rb = max(d for d in range(8, n + 1, 8) if n % d == 0 and (p * d * c * 4 <= (6 << 20) or d == 8))

    def body(a_ref, o_ref):
        acc = a_ref[0].astype(f32)
        for k in range(1, p):
            acc = acc + a_ref[k].astype(f32)
        o_ref[...] = acc

    return pl.pallas_call(body, out_shape=_SDS((n, c), f32), grid=(n // rb,),
                          in_specs=[pl.BlockSpec((p, rb, c), lambda i: (0, i, 0))],
                          out_specs=pl.BlockSpec((rb, c), lambda i: (i, 0)), compiler_params=_cp(("parallel",)),
                          name=name)(a)


def _add2(a, b, *, name):
    p, n, c = a.shape
    rb = max(d for d in range(8, n + 1, 8) if n % d == 0 and (d * c * 4 <= (2 << 20) or d == 8))

    def body(a_ref, b_ref, o_ref):
        o_ref[...] = a_ref[...] + b_ref[...]

    blk = pl.BlockSpec((1, rb, c), lambda s, i: (s, i, 0))
    return pl.pallas_call(body, out_shape=_SDS((p, n, c), f32), grid=(p, n // rb), in_specs=[blk, blk], out_specs=blk,
                          compiler_params=_cp(("parallel", "parallel")), name=name)(a, b)


def _rope_tables(pos, dim):
    half = dim // 2
    inv = ROPE_THETA ** (-jnp.arange(half, dtype=f32) / half)
    ang = pos.astype(f32)[:, None] * inv[None, :]
    c, s = jnp.cos(ang), jnp.sin(ang)
    z = jnp.zeros_like(c)
    pad = [jnp.zeros((pos.shape[0], LANES - dim), f32)] if dim < LANES else []
    return (jnp.concatenate([c, c] + pad, axis=1), jnp.concatenate([-s, z] + pad, axis=1),
            jnp.concatenate([z, s] + pad, axis=1))


def _rope(x, cos, sa, sb, half, transpose=False):
    if transpose:
        return x * cos + pltpu.roll(x * sa, half, 1) + pltpu.roll(x * sb, LANES - half, 1)
    return x * cos + pltpu.roll(x, LANES - half, 1) * sa + pltpu.roll(x, half, 1) * sb


def _rope_call(items, tables, half, transpose, *, name):
    s = items[0][0].shape[0]
    rb = _row_block(s)
    n = len(items)

    def body(*refs):
        cos, sa, sb = refs[n][...], refs[n + 1][...], refs[n + 2][...]
        for k in range(n):
            x_ref, o_ref = refs[k], refs[n + 3 + k]
            for j in range(items[k][1] // LANES):
                sl = slice(j * LANES, (j + 1) * LANES)
                o_ref[:, sl] = _rope(x_ref[:, sl], cos, sa, sb, half, transpose)

    in_specs = [pl.BlockSpec((rb, w), functools.partial(lambda i, cb: (i, cb), cb=cb)) for _, w, cb in items]
    in_specs += [pl.BlockSpec((rb, LANES), lambda i: (i, 0))] * 3
    out_specs = tuple(pl.BlockSpec((rb, w), lambda i: (i, 0)) for _, w, _ in items)
    return pl.pallas_call(
        body, out_shape=tuple(_SDS((s, w), f32) for _, w, _ in items), grid=(s // rb,), in_specs=in_specs,
        out_specs=out_specs, compiler_params=_cp(("parallel",)), name=name,
    )(*[a for a, _, _ in items], *tables)


def _attn_block(s):
    return _pick(s, (256, 128))


def _lower_mask(b, strict):
    r, c = _iota((b, b), 0), _iota((b, b), 1)
    return (c < r) if strict else (c <= r)


def _pick_lane(block, h):
    return jnp.sum(jnp.where(_iota(block.shape, 1) == h, block, 0.0), axis=1, keepdims=True)


def _head_bias(cum_blk, g, j, hp):
    if hp == N_HEADS:
        return cum_blk[:, j:j + 1]
    return _pick_lane(cum_blk, g * hp + j)


def _attn_fwd(q, k, v, qcol, kcol, vcol, dq, cum, cum_t, *, scale, hp, name):
    s = q.shape[0]
    b = _attn_block(s)
    nq = s // b
    has_bias = cum is not None
    assert qcol % hp == 0 and kcol % hp == 0 and vcol % hp == 0

    def body(*refs):
        if has_bias:
            q_ref, k_ref, v_ref, cum_ref, cumt_ref, o_ref, lse_ref = refs
        else:
            q_ref, k_ref, v_ref, o_ref, lse_ref = refs
        g, i = pl.program_id(0), pl.program_id(1)
        qs = [q_ref[:, j * dq:(j + 1) * dq].astype(_MXU) for j in range(hp)]
        cqs = [_head_bias(cum_ref[...], g, j, hp) for j in range(hp)] if has_bias else None

        def chunk(c, carry, diag):
            st = pl.multiple_of(c * b, b)
            mask = _lower_mask(b, False) if diag else None
            out = []
            for j in range(hp):
                m, l, acc = carry[j]
                z = _mm_nt(qs[j], k_ref[pl.ds(st, b), j * dq:(j + 1) * dq]) * scale
                if has_bias:
                    z = z + cqs[j] - cumt_ref[j, c]
                if diag:
                    z = jnp.where(mask, z, NEG_INF)
                m_new = jnp.maximum(m, jnp.max(z, axis=1, keepdims=True))
                p = jnp.exp(z - m_new)
                if diag:
                    p = jnp.where(mask, p, 0.0)
                alpha = jnp.exp(m - m_new)
                l = alpha * l + jnp.sum(p, axis=1, keepdims=True)
                acc = alpha * acc + _mm(p, v_ref[pl.ds(st, b), j * HEAD_DIM:(j + 1) * HEAD_DIM])
                out.append((m_new, l, acc))
            return tuple(out)

        init = tuple((jnp.full((b, 1), NEG_INF, f32), jnp.zeros((b, 1), f32), jnp.zeros((b, HEAD_DIM), f32))
                     for _ in range(hp))
        carry = lax.fori_loop(0, i, lambda c, cr: chunk(c, cr, False), init)
        for j, (m, l, acc) in enumerate(chunk(i, carry, True)):
            o_ref[:, j * HEAD_DIM:(j + 1) * HEAD_DIM] = acc / l
            lse_ref[j] = m + jnp.log(l)

    in_specs = [pl.BlockSpec((b, hp * dq), lambda g, i: (i, qcol // hp + g)),
                pl.BlockSpec((s, hp * dq), lambda g, i: (0, kcol // hp + g)),
                pl.BlockSpec((s, hp * HEAD_DIM), lambda g, i: (0, vcol // hp + g))]
    args = [q, k, v]
    if has_bias:
        in_specs += [pl.BlockSpec((b, LANES), lambda g, i: (i, 0)),
                     pl.BlockSpec((hp, nq, 1, b), lambda g, i: (g, 0, 0, 0))]
        args += [cum, cum_t]
    return pl.pallas_call(
        body, out_shape=(_SDS((s, N_HEADS * HEAD_DIM), f32), _SDS((N_HEADS, s, 1), f32)), grid=(N_HEADS // hp, nq),
        in_specs=in_specs,
        out_specs=(pl.BlockSpec((b, hp * HEAD_DIM), lambda g, i: (i, g)),
                   pl.BlockSpec((hp, b, 1), lambda g, i: (g, i, 0))),
        compiler_params=_cp(("parallel", "parallel")), name=name,
    )(*args)


def _attn_bwd(q, k, v, qcol, kcol, vcol, dq, do, o, lse, cum, cum_t, *, scale, hp, name):
    s = q.shape[0]
    b = _attn_block(s)
    nq = s // b
    has_bias = cum is not None
    assert qcol % hp == 0 and kcol % hp == 0 and vcol % hp == 0
    hd = lambda j: slice(j * HEAD_DIM, (j + 1) * HEAD_DIM)
    hq = lambda j: slice(j * dq, (j + 1) * dq)

    def body(*refs):
        if has_bias:
            (q_ref, k_ref, v_ref, do_ref, o_ref, lse_ref, cum_ref, cumt_ref, dq_ref, dk_ref, dv_ref, dck_ref,
             p_sc, dp_sc) = refs
        else:
            q_ref, k_ref, v_ref, do_ref, o_ref, lse_ref, dq_ref, dk_ref, dv_ref = refs
        g, i = pl.program_id(0), pl.program_id(1)

        @pl.when(i == 0)
        def _():
            dk_ref[...] = jnp.zeros_like(dk_ref)
            dv_ref[...] = jnp.zeros_like(dv_ref)
            if has_bias:
                dck_ref[...] = jnp.zeros_like(dck_ref)

        qs = [q_ref[:, hq(j)].astype(_MXU) for j in range(hp)]
        dos = [do_ref[:, hd(j)].astype(_MXU) for j in range(hp)]
        lses = [lse_ref[j] for j in range(hp)]
        cqs = [_head_bias(cum_ref[...], g, j, hp) for j in range(hp)] if has_bias else None

        def probs(j, c, diag):
            st = pl.multiple_of(c * b, b)
            z = _mm_nt(qs[j], k_ref[pl.ds(st, b), hq(j)]) * scale
            if has_bias:
                z = z + cqs[j] - cumt_ref[j, c]
            p = jnp.exp(z - lses[j])
            if diag:
                p = jnp.where(_lower_mask(b, False), p, 0.0)
            return p, _mm_nt(dos[j], v_ref[pl.ds(st, b), hd(j)])

        if has_bias:
            def first(c, accs, diag):
                out = []
                for j in range(hp):
                    p, dp = probs(j, c, diag)
                    p_sc[j, c] = p
                    dp_sc[j, c] = dp
                    out.append(accs[j] + jnp.sum(p * dp, axis=1, keepdims=True))
                return tuple(out)

            deltas = lax.fori_loop(0, i, lambda c, a: first(c, a, False),
                                   tuple(jnp.zeros((b, 1), f32) for _ in range(hp)))
            deltas = first(i, deltas, True)
        else:
            deltas = [jnp.sum(do_ref[:, hd(j)] * o_ref[:, hd(j)], axis=1, keepdims=True) for j in range(hp)]

        def chunk(c, dq_accs, diag):
            st = pl.multiple_of(c * b, b)
            out = []
            for j in range(hp):
                p, dp = (p_sc[j, c], dp_sc[j, c]) if has_bias else probs(j, c, diag)
                ds = p * (dp - deltas[j])
                dk_ref[pl.ds(st, b), hq(j)] += _mm_tn(ds, qs[j]) * scale
                dv_ref[pl.ds(st, b), hd(j)] += _mm_tn(p, dos[j])
                if has_bias:
                    dck_ref[j, c] += -jnp.sum(ds, axis=0, keepdims=True)
                out.append(dq_accs[j] + _mm(ds, k_ref[pl.ds(st, b), hq(j)]))
            return tuple(out)

        accs = lax.fori_loop(0, i, lambda c, a: chunk(c, a, False), tuple(jnp.zeros((b, dq), f32) for _ in range(hp)))
        for j, acc in enumerate(chunk(i, accs, True)):
            dq_ref[:, hq(j)] = acc * scale

    rowq = pl.BlockSpec((b, hp * HEAD_DIM), lambda g, i: (i, g))
    in_specs = [pl.BlockSpec((b, hp * dq), lambda g, i: (i, qcol // hp + g)),
                pl.BlockSpec((s, hp * dq), lambda g, i: (0, kcol // hp + g)),
                pl.BlockSpec((s, hp * HEAD_DIM), lambda g, i: (0, vcol // hp + g)), rowq, rowq,
                pl.BlockSpec((hp, b, 1), lambda g, i: (g, i, 0))]
    args = [q, k, v, do, o, lse]
    out_shape = [_SDS((s, N_HEADS * dq), f32), _SDS((s, N_HEADS * dq), f32), _SDS((s, N_HEADS * HEAD_DIM), f32)]
    out_specs = [pl.BlockSpec((b, hp * dq), lambda g, i: (i, g)), pl.BlockSpec((s, hp * dq), lambda g, i: (0, g)),
                 pl.BlockSpec((s, hp * HEAD_DIM), lambda g, i: (0, g))]
    if has_bias:
        in_specs += [pl.BlockSpec((b, LANES), lambda g, i: (i, 0)),
                     pl.BlockSpec((hp, nq, 1, b), lambda g, i: (g, 0, 0, 0))]
        args += [cum, cum_t]
        out_shape.append(_SDS((N_HEADS, nq, 1, b), f32))
        out_specs.append(pl.BlockSpec((hp, nq, 1, b), lambda g, i: (g, 0, 0, 0)))
    return pl.pallas_call(
        body, out_shape=tuple(out_shape), grid=(N_HEADS // hp, nq), in_specs=in_specs, out_specs=tuple(out_specs),
        scratch_shapes=[pltpu.VMEM((hp, nq, b, b), f32)] * 2 if has_bias else [],
        compiler_params=_cp(("parallel", "arbitrary")), name=name,
    )(*args)


def _tri(b, kind):
    r, c = _iota((b, b), 0), _iota((b, b), 1)
    cond = {"row_gt": r > c, "row_lt": r < c, "row_ge": r >= c, "row_le": r <= c}[kind]
    return jnp.where(cond, 1.0, 0.0).astype(_MXU)


def _log_keep(z):
    return -(jnp.maximum(z, 0.0) + jnp.log1p(jnp.exp(-jnp.abs(z))))


def _sb_fwd(z_all, *, hp, name):
    s = z_all.shape[0]
    b = _attn_block(s)
    nq = s // b
    scale = HEAD_DIM ** -0.5
    qcol, kcol, vcol = (_AL[n] // (hp * HEAD_DIM) for n in ("sb_q", "sb_k", "sb_v"))
    hd = lambda j: slice(j * HEAD_DIM, (j + 1) * HEAD_DIM)

    def body(q_ref, k_ref, v_ref, o_ref):
        i = pl.program_id(1)
        qs = [q_ref[:, hd(j)].astype(_MXU) for j in range(hp)]
        upper = _tri(b, "row_gt")

        def chunk(c, carry, diag):
            st = pl.multiple_of(c * b, b)
            mask = _lower_mask(b, True) if diag else None
            out = []
            for j in range(hp):
                rsum, acc = carry[j]
                z = _mm_nt(qs[j], k_ref[pl.ds(st, b), hd(j)]) * scale
                lk = _log_keep(z)
                if diag:
                    lk = jnp.where(mask, lk, 0.0)
                a = z + lk + _mm_split(lk, upper) + rsum
                if diag:
                    a = jnp.where(mask, a, NEG_INF)
                acc = acc + _mm(jnp.exp(a), v_ref[pl.ds(st, b), hd(j)])
                out.append((rsum + jnp.sum(lk, axis=1, keepdims=True), acc))
            return tuple(out)

        init = tuple((jnp.zeros((b, 1), f32), jnp.zeros((b, HEAD_DIM), f32)) for _ in range(hp))
        carry = lax.fori_loop(0, i, lambda t, cr: chunk(i - 1 - t, cr, False), chunk(i, init, True))
        for j in range(hp):
            o_ref[:, hd(j)] = carry[j][1]

    w = hp * HEAD_DIM
    return pl.pallas_call(
        body, out_shape=_SDS((s, GROUP), f32), grid=(N_HEADS // hp, nq),
        in_specs=[pl.BlockSpec((b, w), lambda g, i: (i, qcol + g)), pl.BlockSpec((s, w), lambda g, i: (0, kcol + g)),
                  pl.BlockSpec((s, w), lambda g, i: (0, vcol + g))],
        out_specs=pl.BlockSpec((b, w), lambda g, i: (i, g)),
        compiler_params=_cp(("parallel", "parallel")), name=name,
    )(z_all, z_all, z_all)


def _sb_bwd(z_all, do, *, hp, name):
    s = z_all.shape[0]
    b = _attn_block(s)
    nq = s // b
    scale = HEAD_DIM ** -0.5
    qcol, kcol, vcol = (_AL[n] // (hp * HEAD_DIM) for n in ("sb_q", "sb_k", "sb_v"))
    hd = lambda j: slice(j * HEAD_DIM, (j + 1) * HEAD_DIM)

    def body(q_ref, k_ref, v_ref, do_ref, dq_ref, dk_ref, dv_ref, z_sc, lk_sc, r_sc):
        i = pl.program_id(1)

        @pl.when(i == 0)
        def _():
            dk_ref[...] = jnp.zeros_like(dk_ref)
            dv_ref[...] = jnp.zeros_like(dv_ref)

        qs = [q_ref[:, hd(j)].astype(_MXU) for j in range(hp)]
        dos = [do_ref[:, hd(j)].astype(_MXU) for j in range(hp)]
        upper = _tri(b, "row_gt")
        lower = _tri(b, "row_lt")

        def scores(c, rsums, diag):
            st = pl.multiple_of(c * b, b)
            out = []
            for j in range(hp):
                z = _mm_nt(qs[j], k_ref[pl.ds(st, b), hd(j)]) * scale
                lk = _log_keep(z)
                if diag:
                    lk = jnp.where(_lower_mask(b, True), lk, 0.0)
                z_sc[j, c] = z
                lk_sc[j, c] = lk
                r_sc[j, c] = _mm_split(lk, upper) + rsums[j]
                out.append(rsums[j] + jnp.sum(lk, axis=1, keepdims=True))
            return tuple(out)

        rsums = scores(i, tuple(jnp.zeros((b, 1), f32) for _ in range(hp)), True)
        lax.fori_loop(0, i, lambda t, r: scores(i - 1 - t, r, False), rsums)

        def grads(c, carry, diag):
            st = pl.multiple_of(c * b, b)
            mask = _lower_mask(b, True) if diag else None
            out = []
            for j in range(hp):
                psum, dq_acc = carry[j]
                z, lk = z_sc[j, c], lk_sc[j, c]
                lb = z + lk
                a = lb + r_sc[j, c]
                if diag:
                    a = jnp.where(mask, a, NEG_INF)
                w = jnp.exp(a)
                e = _mm_nt(dos[j], v_ref[pl.ds(st, b), hd(j)]) * w
                before = _mm_split(e, lower) + psum
                dz = e * jnp.exp(lk) - before * jnp.exp(lb)
                if diag:
                    dz = jnp.where(mask, dz, 0.0)
                dk_ref[pl.ds(st, b), hd(j)] += _mm_tn(dz, qs[j]) * scale
                dv_ref[pl.ds(st, b), hd(j)] += _mm_tn(w, dos[j])
                out.append((psum + jnp.sum(e, axis=1, keepdims=True), dq_acc + _mm(dz, k_ref[pl.ds(st, b), hd(j)])))
            return tuple(out)

        init = tuple((jnp.zeros((b, 1), f32), jnp.zeros((b, HEAD_DIM), f32)) for _ in range(hp))
        carry = grads(i, lax.fori_loop(0, i, lambda c, cr: grads(c, cr, False), init), True)
        for j in range(hp):
            dq_ref[:, hd(j)] = carry[j][1] * scale

    w = hp * HEAD_DIM
    blk = pl.BlockSpec((b, w), lambda g, i: (i, g))
    full = pl.BlockSpec((s, w), lambda g, i: (0, g))
    return pl.pallas_call(
        body, out_shape=tuple(_SDS((s, GROUP), f32) for _ in range(3)), grid=(N_HEADS // hp, nq),
        in_specs=[pl.BlockSpec((b, w), lambda g, i: (i, qcol + g)), pl.BlockSpec((s, w), lambda g, i: (0, kcol + g)),
                  pl.BlockSpec((s, w), lambda g, i: (0, vcol + g)), blk],
        out_specs=(blk, full, full),
        scratch_shapes=[pltpu.VMEM((hp, nq, b, b), f32)] * 3,
        compiler_params=_cp(("parallel", "arbitrary")), name=name,
    )(z_all, z_all, z_all, do)


def _split3_left(t, x):
    hi = x.astype(_MXU)
    r1 = x - hi.astype(f32)
    mid = r1.astype(_MXU)
    lo = (r1 - mid.astype(f32)).astype(_MXU)
    dot = functools.partial(jnp.dot, preferred_element_type=f32)
    return dot(t, hi) + dot(t, mid) + dot(t, lo)


def _split3_right(x, t):
    hi = x.astype(_MXU)
    r1 = x - hi.astype(f32)
    mid = r1.astype(_MXU)
    lo = (r1 - mid.astype(f32)).astype(_MXU)
    dot = functools.partial(jnp.dot, preferred_element_type=f32)
    return dot(hi, t) + dot(mid, t) + dot(lo, t)


def _fox_cum_fwd(z_all, bias, *, name):
    s = z_all.shape[0]
    b = _attn_block(s)
    fcol = _AL["fox_f"] // LANES

    def body(f_ref, b_ref, cum_ref, cumt_ref, carry_ref):
        i = pl.program_id(0)

        @pl.when(i == 0)
        def _():
            carry_ref[...] = jnp.zeros_like(carry_ref)

        u = f_ref[...] + b_ref[...]
        lf = jnp.minimum(u, 0.0) - jnp.log1p(jnp.exp(-jnp.abs(u)))
        cum = _split3_left(_tri(b, "row_ge"), lf) + carry_ref[...]
        cum_ref[...] = cum
        cumt_ref[...] = cum.T[0:8, :]
        carry_ref[...] = cum_ref[b - 1:b, :]

    return pl.pallas_call(
        body, out_shape=(_SDS((s, LANES), f32), _SDS((8, s), f32)), grid=(s // b,),
        in_specs=[pl.BlockSpec((b, LANES), lambda i: (i, fcol)), pl.BlockSpec((1, LANES), lambda i: (0, 0))],
        out_specs=(pl.BlockSpec((b, LANES), lambda i: (i, 0)), pl.BlockSpec((8, b), lambda i: (0, i))),
        scratch_shapes=[pltpu.VMEM((1, LANES), f32)], compiler_params=_cp(("arbitrary",)), name=name,
    )(z_all, bias)


def _fox_cum_bwd(z_all, bias, dcum_t, *, name):
    s = z_all.shape[0]
    b = _attn_block(s)
    nb = s // b
    fcol = _AL["fox_f"] // LANES

    def body(f_ref, b_ref, dc_ref, df_ref, db_ref, carry_ref):
        i = pl.program_id(0)

        @pl.when(i == 0)
        def _():
            carry_ref[...] = jnp.zeros_like(carry_ref)
            db_ref[...] = jnp.zeros_like(db_ref)

        dc = dc_ref[...]
        rev = _split3_right(dc, _tri(b, "row_ge")) + carry_ref[...]
        carry_ref[...] = carry_ref[...] + jnp.sum(dc, axis=1, keepdims=True)
        dlf = jnp.concatenate([rev, jnp.zeros((LANES - 8, b), f32)], axis=0).T
        u = f_ref[...] + b_ref[...]
        df = jnp.where(_iota((b, LANES), 1) < N_HEADS, dlf * (1.0 - _sigmoid(u)), 0.0)
        df_ref[...] = df
        db_ref[...] += jnp.sum(df, axis=0, keepdims=True)

    return pl.pallas_call(
        body, out_shape=(_SDS((s, LANES), f32), _SDS((1, LANES), f32)), grid=(nb,),
        in_specs=[pl.BlockSpec((b, LANES), lambda i: (nb - 1 - i, fcol)), pl.BlockSpec((1, LANES), lambda i: (0, 0)),
                  pl.BlockSpec((8, b), lambda i: (0, nb - 1 - i))],
        out_specs=(pl.BlockSpec((b, LANES), lambda i: (nb - 1 - i, 0)), pl.BlockSpec((1, LANES), lambda i: (0, 0))),
        scratch_shapes=[pltpu.VMEM((8, 1), f32)], compiler_params=_cp(("arbitrary",)), name=name,
    )(z_all, bias, dcum_t)


MLA_QW = 2 * LANES


def _rms_rows(x):
    r = lax.rsqrt(jnp.mean(x * x, axis=-1, keepdims=True) + RMS_EPS)
    return x * r, r


def _mla_prep_fwd(z_all, gq, gkv, wuq, wk, wv, tables, *, name):
    s = z_all.shape[0]
    rb = _row_block(s)
    half = MLA_ROPE // 2

    def body(cq_ref, ckv_ref, kr_ref, gq_ref, gkv_ref, wuq_ref, wk_ref, wv_ref, cos_ref, sa_ref, sb_ref,
             q_ref, k_ref, v_ref):
        cos, sa, sb = cos_ref[...], sa_ref[...], sb_ref[...]
        xh, _ = _rms_rows(cq_ref[...])
        qp = _mm(xh * gq_ref[...], wuq_ref[...])
        kh, _ = _rms_rows(ckv_ref[...])
        nkv = kh * gkv_ref[...]
        kn = _mm(nkv, wk_ref[...])
        v_ref[...] = _mm(nkv, wv_ref[...])
        kr = _rope(kr_ref[...], cos, sa, sb, half)
        for h in range(N_HEADS):
            lo, mid, hi = h * MLA_QW, h * MLA_QW + LANES, (h + 1) * MLA_QW
            q_ref[:, lo:mid] = qp[:, lo:mid]
            q_ref[:, mid:hi] = _rope(qp[:, mid:hi], cos, sa, sb, half)
            k_ref[:, lo:mid] = kn[:, h * LANES:(h + 1) * LANES]
            k_ref[:, mid:hi] = kr

    row = lambda w, cb: pl.BlockSpec((rb, w), lambda i: (i, cb))
    whole = lambda a: pl.BlockSpec(a.shape, lambda i: (0,) * a.ndim)
    return pl.pallas_call(
        body, out_shape=(_SDS((s, N_HEADS * MLA_QW), f32), _SDS((s, N_HEADS * MLA_QW), f32), _SDS((s, GROUP), f32)),
        grid=(s // rb,),
        in_specs=[row(MLA_Q_RANK, _AL["mla_cq"] // MLA_Q_RANK), row(LANES, _AL["mla_ckv"] // LANES),
                  row(LANES, _AL["mla_k_rope"] // LANES), whole(gq), whole(gkv), whole(wuq), whole(wk), whole(wv),
                  row(LANES, 0), row(LANES, 0), row(LANES, 0)],
        out_specs=(row(N_HEADS * MLA_QW, 0), row(N_HEADS * MLA_QW, 0), row(GROUP, 0)),
        compiler_params=_cp(("parallel",)), name=name,
    )(z_all, z_all, z_all, gq, gkv, wuq, wk, wv, *tables)


def _mla_prep_bwd(z_all, gq, gkv, wuq, wk, wv, tables, dq_cat, dk_cat, dv, *, name):
    s = z_all.shape[0]
    rb = _row_block(s)
    half = MLA_ROPE // 2

    def body(cq_ref, ckv_ref, gq_ref, gkv_ref, wuq_ref, wk_ref, wv_ref, cos_ref, sa_ref, sb_ref, dq_ref, dk_ref,
             dv_ref, dcq_ref, dckv_ref, dkr_ref, dwuq_ref, dwk_ref, dwv_ref, dgq_ref, dgkv_ref):
        i = pl.program_id(0)

        @pl.when(i == 0)
        def _():
            for r in (dwuq_ref, dwk_ref, dwv_ref, dgq_ref, dgkv_ref):
                r[...] = jnp.zeros_like(r)

        cos, sa, sb = cos_ref[...], sa_ref[...], sb_ref[...]
        parts, knp = [], []
        dkr = jnp.zeros((rb, LANES), f32)
        for h in range(N_HEADS):
            lo, mid, hi = h * MLA_QW, h * MLA_QW + LANES, (h + 1) * MLA_QW
            parts += [dq_ref[:, lo:mid], _rope(dq_ref[:, mid:hi], cos, sa, sb, half, transpose=True)]
            knp.append(dk_ref[:, lo:mid])
            dkr = dkr + _rope(dk_ref[:, mid:hi], cos, sa, sb, half, transpose=True)
        dkr_ref[...] = dkr
        dqp = jnp.concatenate(parts, axis=1)
        dkn = jnp.concatenate(knp, axis=1)
        dvv = dv_ref[...]

        def norm_bwd(x_ref, g_ref, w_pairs, dx_ref, dg_ref):
            xh, r = _rms_rows(x_ref[...])
            nx = xh * g_ref[...]
            dn = jnp.zeros_like(xh)
            for w_ref, dw_ref, dy in w_pairs:
                dw_ref[...] += _mm_tn(nx, dy)
                dn = dn + _mm_nt(dy, w_ref[...])
            dxh = dn * g_ref[...]
            dx_ref[...] = r * (dxh - xh * jnp.mean(dxh * xh, axis=-1, keepdims=True))
            dg_ref[...] += jnp.sum(dn * xh, axis=0, keepdims=True)

        norm_bwd(cq_ref, gq_ref, [(wuq_ref, dwuq_ref, dqp)], dcq_ref, dgq_ref)
        norm_bwd(ckv_ref, gkv_ref, [(wk_ref, dwk_ref, dkn), (wv_ref, dwv_ref, dvv)], dckv_ref, dgkv_ref)

    row = lambda w, cb: pl.BlockSpec((rb, w), lambda i: (i, cb))
    whole = lambda a: pl.BlockSpec(a.shape, lambda i: (0,) * a.ndim)
    return pl.pallas_call(
        body,
        out_shape=(_SDS((s, MLA_Q_RANK), f32), _SDS((s, LANES), f32), _SDS((s, LANES), f32), _SDS(wuq.shape, f32),
                   _SDS(wk.shape, f32), _SDS(wv.shape, f32), _SDS(gq.shape, f32), _SDS(gkv.shape, f32)),
        grid=(s // rb,),
        in_specs=[row(MLA_Q_RANK, _AL["mla_cq"] // MLA_Q_RANK), row(LANES, _AL["mla_ckv"] // LANES), whole(gq),
                  whole(gkv), whole(wuq), whole(wk), whole(wv), row(LANES, 0), row(LANES, 0), row(LANES, 0),
                  row(N_HEADS * MLA_QW, 0), row(N_HEADS * MLA_QW, 0), row(GROUP, 0)],
        out_specs=(row(MLA_Q_RANK, 0), row(LANES, 0), row(LANES, 0), whole(wuq), whole(wk), whole(wv), whole(gq),
                   whole(gkv)),
        compiler_params=_cp(("arbitrary",)), name=name,
    )(z_all, z_all, gq, gkv, wuq, wk, wv, *tables, dq_cat, dk_cat, dv)


def _silu_grad(x):
    sg = _sigmoid(x)
    return sg * (1.0 + x * (1.0 - sg))


def _nsa_cmp_fwd(ra, rb_, pos, w1, w2, tables, *, name):
    nr = ra.shape[1]
    hw = ra.shape[2]

    def body(ra_ref, rb_ref, pos_ref, w1_ref, w2_ref, cos_ref, sa_ref, sb_ref, out_ref, hp_ref):
        for k in range(2):
            xa = ra_ref[k] + pos_ref[k, :, 0:hw]
            xb = rb_ref[k] + pos_ref[k, :, hw:2 * hw]
            hp = _mm(xa, w1_ref[k, 0:hw, :]) + _mm(xb, w1_ref[k, hw:2 * hw, :])
            hp_ref[k] = hp
            out = _mm(hp * _sigmoid(hp), w2_ref[k])
            if k == 0:
                out = _rope(out, cos_ref[...], sa_ref[...], sb_ref[...], HEAD_DIM // 2)
            out_ref[k] = out

    return pl.pallas_call(body, out_shape=(_SDS((2, nr, HEAD_DIM), f32), _SDS((2, nr, HEAD_DIM), f32)),
                          compiler_params=_cp(), name=name)(ra, rb_, pos, w1, w2, *tables)


def _nsa_cmp_bwd(ra, rb_, pos, w1, w2, tables, hp, dout, *, name):
    nr = ra.shape[1]
    hw = ra.shape[2]

    def body(ra_ref, rb_ref, pos_ref, w1_ref, w2_ref, cos_ref, sa_ref, sb_ref, hp_ref, do_ref,
             dxa_ref, dxb_ref, dw1_ref, dw2_ref):
        for k in range(2):
            d_out = do_ref[k]
            if k == 0:
                d_out = _rope(d_out, cos_ref[...], sa_ref[...], sb_ref[...], HEAD_DIM // 2, transpose=True)
            hpv = hp_ref[k]
            dw2_ref[k] = _mm_tn(hpv * _sigmoid(hpv), d_out)
            dhp = _mm_nt(d_out, w2_ref[k]) * _silu_grad(hpv)
            xa = ra_ref[k] + pos_ref[k, :, 0:hw]
            xb = rb_ref[k] + pos_ref[k, :, hw:2 * hw]
            dw1_ref[k, 0:hw, :] = _mm_tn(xa, dhp)
            dw1_ref[k, hw:2 * hw, :] = _mm_tn(xb, dhp)
            dxa_ref[k] = _mm_nt(dhp, w1_ref[k, 0:hw, :])
            dxb_ref[k] = _mm_nt(dhp, w1_ref[k, hw:2 * hw, :])

    return pl.pallas_call(
        body, out_shape=(_SDS((2, nr, hw), f32), _SDS((2, nr, hw), f32), _SDS(w1.shape, f32), _SDS(w2.shape, f32)),
        compiler_params=_cp(), name=name)(ra, rb_, pos, w1, w2, *tables, hp, dout)


def _nsa_consts(s):
    b = _attn_block(s)
    nr = s // CMP_STRIDE
    n_cmp = (s - CMP_LEN) // CMP_STRIDE + 1
    n_sel = s // SEL_LEN
    cmp_start = np.arange(n_cmp) * CMP_STRIDE
    sel_start = np.arange(n_sel) * SEL_LEN
    overlap = np.clip(np.minimum(cmp_start[:, None] + CMP_LEN, sel_start[None, :] + SEL_LEN)
                      - np.maximum(cmp_start[:, None], sel_start[None, :]), 0, None)
    m2s = np.zeros((nr, LANES), np.float32)
    m2s[:n_cmp, :n_sel] = overlap / CMP_LEN
    e3 = np.zeros((s // b, LANES, b), np.float32)
    tok = np.arange(s)
    e3[tok // b, tok // SEL_LEN, tok % b] = 1.0
    return jnp.asarray(m2s, _MXU), jnp.asarray(e3, _MXU)


def _nsa_masks(i, b, d):
    qpos = i * b + _iota((b, b), 0)
    kpos = (i - d) * b + _iota((b, b), 1)
    return (kpos <= qpos) & (kpos > qpos - WINDOW)


def _nsa_fwd(qr, kvc, ksr, vs, kwr, vw, z_all, m2s, e3, *, name):
    s = qr.shape[0]
    b = _attn_block(s)
    nq = s // b
    nr = kvc.shape[1]
    n_sel = s // SEL_LEN
    top_n = min(SEL_TOPN, n_sel)
    nd = -(-WINDOW // b)
    scale = HEAD_DIM ** -0.5
    bcol = _AL["nsa_branch"] // LANES
    H = N_HEADS

    def body(q_ref, kvc_ref, ks_ref, vs_ref, kw_ref, vw_ref, br_ref, m2s_ref, e3_ref,
             o_ref, oc_ref, os_ref, ow_ref, st_ref, sel_ref, m_sc, l_sc, acc_sc):
        i = pl.program_id(0)
        lane = _iota((b, LANES), 1)
        hs = lambda h: slice(h * HEAD_DIM, (h + 1) * HEAD_DIM)

        cmp_mask = (CMP_STRIDE * _iota((b, nr), 1) + (CMP_LEN - 1)) <= (i * b + _iota((b, nr), 0))
        imp = jnp.zeros((b, LANES), f32)
        stats = jnp.zeros((b, LANES), f32)
        for h in range(H):
            zc = jnp.where(cmp_mask, _mm_nt(q_ref[:, hs(h)], kvc_ref[0]) * scale, NEG_INF)
            m = jnp.max(zc, axis=1, keepdims=True)
            p = jnp.where(cmp_mask, jnp.exp(zc - m), 0.0)
            l = jnp.sum(p, axis=1, keepdims=True)
            some = l > 0.0
            lsafe = jnp.where(some, l, 1.0)
            pc = p * jnp.where(some, 1.0 / lsafe, 0.0)
            oc_ref[:, hs(h)] = _mm(pc, kvc_ref[1])
            imp = imp + _mm(pc, m2s_ref[...])
            stats = jnp.where(lane == h, jnp.where(some, m + jnp.log(lsafe), 0.0), stats)

        cur = jnp.right_shift(i * b + _iota((b, LANES), 0), int(math.log2(SEL_LEN)))
        forced = (lane == 0) | (lane == cur) | (lane == cur - 1)
        score = jnp.where(lane <= cur, jnp.where(forced, FORCED_BONUS, imp), NEG_INF)
        score = jnp.where(lane < n_sel, score, -3e38)
        rank = jnp.zeros((b, LANES), f32)
        for j in range(n_sel):
            col = score[:, j:j + 1]
            rank = rank + jnp.where(col > score, 1.0, jnp.where(col == score, jnp.where(lane > j, 1.0, 0.0), 0.0))
        sel = jnp.where(lane < n_sel, jnp.where(rank < top_n, 1.0, 0.0), 0.0)
        sel_ref[...] = sel
        sel_b = sel.astype(_MXU)

        def reset():
            m_sc[...] = jnp.full(m_sc.shape, NEG_INF, f32)
            l_sc[...] = jnp.zeros_like(l_sc)
            acc_sc[...] = jnp.zeros_like(acc_sc)

        def update(h, z, mask, vch):
            zm = jnp.where(mask, z, NEG_INF)
            m_old = m_sc[h]
            m_new = jnp.maximum(m_old, jnp.max(zm, axis=1, keepdims=True))
            p = jnp.where(mask, jnp.exp(zm - m_new), 0.0)
            alpha = jnp.exp(m_old - m_new)
            l_sc[h] = alpha * l_sc[h] + jnp.sum(p, axis=1, keepdims=True)
            acc_sc[h] = alpha * acc_sc[h] + _mm(p, vch)
            m_sc[h] = m_new

        def finish(out_ref, branch, stats):
            for h in range(H):
                out_ref[:, hs(h)] = acc_sc[h] / l_sc[h]
                stats = jnp.where(lane == 4 * branch + h, m_sc[h] + jnp.log(l_sc[h]), stats)
            return stats

        def sel_chunk(c, diag):
            st = pl.multiple_of(c * b, b)
            mask = _mm(sel_b, e3_ref[c]) > 0.5
            if diag:
                mask = mask & _lower_mask(b, False)
            kch, vch = ks_ref[pl.ds(st, b), :], vs_ref[pl.ds(st, b), :]
            for h in range(H):
                update(h, _mm_nt(q_ref[:, hs(h)], kch) * scale, mask, vch)

        reset()

        def sel_loop(c, carry):
            sel_chunk(c, False)
            return carry

        lax.fori_loop(0, i, sel_loop, 0)
        sel_chunk(i, True)
        stats = finish(os_ref, 1, stats)

        reset()
        for d in range(nd, -1, -1):
            @pl.when(i >= d)
            def _():
                st = pl.multiple_of((i - d) * b, b)
                mask = _nsa_masks(i, b, d)
                kch, vch = kw_ref[pl.ds(st, b), :], vw_ref[pl.ds(st, b), :]
                for h in range(H):
                    update(h, _mm_nt(q_ref[:, hs(h)], kch) * scale, mask, vch)
        stats = finish(ow_ref, 2, stats)
        st_ref[...] = stats

        g = _sigmoid(br_ref[...])
        for h in range(H):
            o_ref[:, hs(h)] = (g[:, 3 * h:3 * h + 1] * oc_ref[:, hs(h)] + g[:, 3 * h + 1:3 * h + 2] * os_ref[:, hs(h)]
                               + g[:, 3 * h + 2:3 * h + 3] * ow_ref[:, hs(h)])

    blk = lambda w: pl.BlockSpec((b, w), lambda i: (i, 0))
    whole = lambda a: pl.BlockSpec(a.shape, lambda i: (0,) * a.ndim)
    return pl.pallas_call(
        body, out_shape=tuple(_SDS((s, GROUP), f32) for _ in range(4)) + (_SDS((s, LANES), f32), _SDS((s, LANES), f32)),
        grid=(nq,),
        in_specs=[blk(GROUP), whole(kvc), whole(ksr), whole(vs), whole(kwr), whole(vw),
                  pl.BlockSpec((b, LANES), lambda i: (i, bcol)), whole(m2s), whole(e3)],
        out_specs=(blk(GROUP),) * 4 + (blk(LANES), blk(LANES)),
        scratch_shapes=[pltpu.VMEM((H, b, 1), f32), pltpu.VMEM((H, b, 1), f32), pltpu.VMEM((H, b, HEAD_DIM), f32)],
        compiler_params=_cp(("parallel",)), name=name,
    )(qr, kvc, ksr, vs, kwr, vw, z_all, m2s, e3)


def _nsa_bwd(do, qr, kvc, ksr, vs, kwr, vw, z_all, oc, os_, ow, stats, sel, e3, *, name):
    s = qr.shape[0]
    b = _attn_block(s)
    nq = s // b
    nr = kvc.shape[1]
    nd = -(-WINDOW // b)
    scale = HEAD_DIM ** -0.5
    bcol = _AL["nsa_branch"] // LANES
    H = N_HEADS

    def body(do_ref, q_ref, kvc_ref, ks_ref, vs_ref, kw_ref, vw_ref, br_ref, oc_ref, os_ref, ow_ref, st_ref, sel_ref,
             e3_ref, dq_ref, dbr_ref, dkvc_ref, dks_ref, dvs_ref, dkw_ref, dvw_ref, dob_sc, delta_sc, dq_sc):
        i = pl.program_id(0)

        @pl.when(i == 0)
        def _():
            for r in (dkvc_ref, dks_ref, dvs_ref, dkw_ref, dvw_ref):
                r[...] = jnp.zeros_like(r)

        lane = _iota((b, LANES), 1)
        hs = lambda h: slice(h * HEAD_DIM, (h + 1) * HEAD_DIM)
        g = _sigmoid(br_ref[...])
        stats = st_ref[...]
        dbr = jnp.zeros((b, LANES), f32)
        outs = (oc_ref, os_ref, ow_ref)
        for h in range(H):
            doh = do_ref[:, hs(h)]
            for j in range(3):
                gj = g[:, 3 * h + j:3 * h + j + 1]
                dgj = jnp.sum(doh * outs[j][:, hs(h)], axis=1, keepdims=True)
                dbr = jnp.where(lane == 3 * h + j, dgj * gj * (1.0 - gj), dbr)
                dob_sc[j, :, hs(h)] = gj * doh
                delta_sc[j, h] = gj * dgj
        dbr_ref[...] = dbr
        dq_sc[...] = jnp.zeros_like(dq_sc)

        def branch(j, h, z, mask, kch, vch):
            qh = q_ref[:, hs(h)]
            p = jnp.where(mask, jnp.exp(jnp.where(mask, z, NEG_INF) - stats[:, 4 * j + h:4 * j + h + 1]), 0.0)
            dob = dob_sc[j, :, hs(h)]
            ds = p * (_mm_nt(dob, vch) - delta_sc[j, h])
            dq_sc[:, hs(h)] += _mm(ds, kch) * scale
            return _mm_tn(ds, qh) * scale, _mm_tn(p, dob)

        cmp_mask = (CMP_STRIDE * _iota((b, nr), 1) + (CMP_LEN - 1)) <= (i * b + _iota((b, nr), 0))
        kc, vc = kvc_ref[0], kvc_ref[1]
        for h in range(H):
            dk, dv = branch(0, h, _mm_nt(q_ref[:, hs(h)], kc) * scale, cmp_mask, kc, vc)
            dkvc_ref[0] += dk
            dkvc_ref[1] += dv

        sel_b = sel_ref[...].astype(_MXU)

        def chunk(j, c, mask, k_ref, v_ref, dk_ref, dv_ref):
            st = pl.multiple_of(c * b, b)
            kch, vch = k_ref[pl.ds(st, b), :], v_ref[pl.ds(st, b), :]
            dk = jnp.zeros((b, HEAD_DIM), f32)
            dv = jnp.zeros((b, HEAD_DIM), f32)
            for h in range(H):
                dkh, dvh = branch(j, h, _mm_nt(q_ref[:, hs(h)], kch) * scale, mask, kch, vch)
                dk, dv = dk + dkh, dv + dvh
            dk_ref[pl.ds(st, b), :] += dk
            dv_ref[pl.ds(st, b), :] += dv

        def sel_chunk(c, diag):
            mask = _mm(sel_b, e3_ref[c]) > 0.5
            if diag:
                mask = mask & _lower_mask(b, False)
            chunk(1, c, mask, ks_ref, vs_ref, dks_ref, dvs_ref)

        def sel_loop(c, carry):
            sel_chunk(c, False)
            return carry

        lax.fori_loop(0, i, sel_loop, 0)
        sel_chunk(i, True)

        for d in range(nd, -1, -1):
            @pl.when(i >= d)
            def _():
                chunk(2, i - d, _nsa_masks(i, b, d), kw_ref, vw_ref, dkw_ref, dvw_ref)

        dq_ref[...] = dq_sc[...]

    blk = lambda w: pl.BlockSpec((b, w), lambda i: (i, 0))
    whole = lambda a: pl.BlockSpec(a.shape, lambda i: (0,) * a.ndim)
    stream = _SDS((s, HEAD_DIM), f32)
    return pl.pallas_call(
        body, out_shape=(_SDS((s, GROUP), f32), _SDS((s, LANES), f32), _SDS(kvc.shape, f32), stream, stream, stream,
                         stream),
        grid=(nq,),
        in_specs=[blk(GROUP), blk(GROUP), whole(kvc), whole(ksr), whole(vs), whole(kwr), whole(vw),
                  pl.BlockSpec((b, LANES), lambda i: (i, bcol)), blk(GROUP), blk(GROUP), blk(GROUP), blk(LANES),
                  blk(LANES), whole(e3)],
        out_specs=(blk(GROUP), blk(LANES), whole(kvc), whole(ksr), whole(vs), whole(kwr), whole(vw)),
        scratch_shapes=[pltpu.VMEM((3, b, GROUP), f32), pltpu.VMEM((3, H, b, 1), f32), pltpu.VMEM((b, GROUP), f32)],
        compiler_params=_cp(("arbitrary",)), name=name,
    )(do, qr, kvc, ksr, vs, kwr, vw, z_all, oc, os_, ow, stats, sel, e3)


def _seg(a, name):
    parts = [lax.slice_in_dim(a, off, off + hi - lo, axis=a.ndim - 1) for off, lo, hi in _PIECES[name]]
    return parts[0] if len(parts) == 1 else jnp.concatenate(parts, axis=a.ndim - 1)


def _to_groups(segs, rows, dtype):
    cols = []
    for s, grp in enumerate(_GROUPS):
        at = 0
        for n, lo, hi, off in sorted(grp, key=lambda t: t[3]):
            if off > at:
                cols.append(jnp.zeros((rows, off - at), dtype))
            cols.append(segs[n][:, lo:hi].astype(dtype))
            at = off + hi - lo
        if at < GROUP_W:
            cols.append(jnp.zeros((rows, GROUP_W - at), dtype))
    return jnp.concatenate(cols, axis=1)


def _piece_from_shard(w_t, s):
    grp = sorted(_GROUPS[s], key=lambda t: t[3])
    ends = [t[3] for t in grp[1:]] + [GROUP_W]
    rows = []
    for (n, lo, hi, off), end in zip(grp, ends):
        first = _ORIG[n] + lo - s * CHIP_COLS
        rows.append(jnp.pad(w_t[:, first:first + hi - lo], ((0, 0), (0, end - off - (hi - lo)), (0, 0))))
    return jnp.concatenate(rows, axis=1)


def _shard_from_piece(g, s):
    return jnp.concatenate([g[:, off:off + hi - lo] for n, lo, hi, off in
                            sorted(_GROUPS[s], key=lambda t: _ORIG[t[0]] + t[1])], axis=1)


def _from_groups(a):
    return jnp.concatenate([_seg(a, n) for n, _ in _SEGS], axis=1)


def _cmp_rows(tok):
    s = tok.shape[0]
    r = tok.reshape(s // CMP_STRIDE, CMP_STRIDE * HEAD_DIM)
    return r, jnp.concatenate([r[1:], jnp.zeros((1, r.shape[1]), r.dtype)], axis=0)


def _cmp_unrows(dxa, dxb):
    s = dxa.shape[0] * CMP_STRIDE
    return (dxa + jnp.concatenate([jnp.zeros((1, dxa.shape[1]), dxa.dtype), dxb[:-1]], axis=0)).reshape(s, HEAD_DIM)


_GATES = ("sb_gate", "nsa_gate", "fox_gate", "mla_gate")


def _layer_fwd(x, p, c, tag):
    s = x.shape[0]
    b = _attn_block(s)
    h = _rms_fwd(x, p["pre_g"], out_dtype=_MXU, name=f"prenorm_{tag}")
    z = _matmul(h, p["w_in"], "nt", bias=p["b_in"], name=f"inproj_{tag}")
    o_sb = _sb_fwd(z, hp=HP_FWD, name=f"sb_fwd_{tag}")

    qr, ksr, kwr = _rope_call([(z, GROUP, _AL["nsa_q"] // GROUP), (z, LANES, _AL["nsa_k_sel"] // LANES),
                               (z, LANES, _AL["nsa_k_win"] // LANES)], c["tabs128"], HEAD_DIM // 2, False,
                              name=f"nsa_rope_{tag}")
    (rak, rbk), (rav, rbv) = _cmp_rows(_seg(z, "nsa_k_cmp")), _cmp_rows(_seg(z, "nsa_v_cmp"))
    ra, rb_ = jnp.stack([rak, rav]), jnp.stack([rbk, rbv])
    kvc, hp = _nsa_cmp_fwd(ra, rb_, p["cmp_pos"], p["cmp_w1"], p["cmp_w2"], c["tabs_cmp"], name=f"nsa_cmp_{tag}")
    vs, vw = _seg(z, "nsa_v_sel"), _seg(z, "nsa_v_win")
    o_nsa, oc, os_, ow, stats, sel = _nsa_fwd(qr, kvc, ksr, vs, kwr, vw, z, c["m2s"], c["e3"], name=f"nsa_fwd_{tag}")

    cum, cum_t8 = _fox_cum_fwd(z, p["fox_bias"], name=f"fox_cum_{tag}")
    cum_t = cum_t8.reshape(8, s // b, 1, b)
    fox_v = _seg(z, "fox_v")
    fcols = (_AL["fox_q"] // HEAD_DIM, _AL["fox_k"] // HEAD_DIM, 0)
    o_fox, lse_fox = _attn_fwd(z, z, fox_v, *fcols, HEAD_DIM, cum, cum_t, scale=HEAD_DIM ** -0.5, hp=HP_FWD,
                               name=f"fox_fwd_{tag}")

    qcat, kcat, vm = _mla_prep_fwd(z, p["gq"], p["gkv"], p["wuq"], p["wk"], p["wv"], c["tabs64"],
                                   name=f"mla_prep_{tag}")
    o_mla, lse_mla = _attn_fwd(qcat, kcat, vm, 0, 0, 0, MLA_QW, None, None, scale=(MLA_NOPE + MLA_ROPE) ** -0.5,
                               hp=HP_BWD, name=f"mla_fwd_{tag}")

    o_all = jnp.concatenate([o_sb, o_nsa, o_fox, o_mla], axis=1)
    gates = jnp.concatenate([_seg(z, n) for n in _GATES], axis=1)
    mix = _gate_fwd(o_all, gates, name=f"gate_{tag}")
    u = _matmul(mix, p["w_out"], "nn", name=f"outproj_{tag}")
    y = _postnorm_fwd(u, p["post_g"], x, name=f"postnorm_{tag}")
    saved = dict(x=x, h=h, z=z, qr=qr, ksr=ksr, kwr=kwr, ra=ra, rb=rb_, kvc=kvc, hp=hp, vs=vs, vw=vw, oc=oc, os=os_,
                 ow=ow, stats=stats, sel=sel, cum=cum, cum_t=cum_t, fox_v=fox_v, o_fox=o_fox, lse_fox=lse_fox, qcat=qcat, kcat=kcat,
                 vm=vm, o_mla=o_mla, lse_mla=lse_mla, o_all=o_all, gates=gates, mix=mix, u=u)
    return y, saved


def _layer_bwd(dy, sv, p, c, tag):
    z = sv["z"]
    s = z.shape[0]
    du, dg_post = _rms_bwd(dy, sv["u"], p["post_g"], name=f"postnorm_bwd_{tag}")
    dmix = _matmul(du, p["w_out"], "nt", name=f"outproj_dx_{tag}")
    dw_out = _matmul(sv["mix"], du, "tn", name=f"outproj_dw_{tag}")
    do_all, dgates = _gate_bwd(dmix, sv["o_all"], sv["gates"], name=f"gate_bwd_{tag}")
    do_sb, do_nsa, do_fox, do_mla = (do_all[:, k * GROUP:(k + 1) * GROUP] for k in range(4))
    dgate = [dgates[:, k * GROUP:(k + 1) * GROUP] for k in range(4)]

    sb_dq, sb_dk, sb_dv = _sb_bwd(z, do_sb, hp=HP_BWD, name=f"sb_bwd_{tag}")

    n_dq, n_dbr, n_dkvc, n_dks, n_dvs, n_dkw, n_dvw = _nsa_bwd(
        do_nsa, sv["qr"], sv["kvc"], sv["ksr"], sv["vs"], sv["kwr"], sv["vw"], z, sv["oc"], sv["os"], sv["ow"],
        sv["stats"], sv["sel"], c["e3"], name=f"nsa_bwd_{tag}")
    dxa, dxb, dw1, dw2 = _nsa_cmp_bwd(sv["ra"], sv["rb"], p["cmp_pos"], p["cmp_w1"], p["cmp_w2"], c["tabs_cmp"],
                                      sv["hp"], n_dkvc, name=f"nsa_cmp_bwd_{tag}")
    n_dq, n_dks, n_dkw = _rope_call([(n_dq, GROUP, 0), (n_dks, LANES, 0), (n_dkw, LANES, 0)], c["tabs128"],
                                    HEAD_DIM // 2, True, name=f"nsa_rope_bwd_{tag}")
    dpos = _colsum(jnp.concatenate([dxa[0], dxb[0], dxa[1], dxb[1]], axis=1), name=f"nsa_dpos_{tag}")
    flat = CMP_LEN * HEAD_DIM

    fcols = (_AL["fox_q"] // HEAD_DIM, _AL["fox_k"] // HEAD_DIM, 0)
    f_dq, f_dk, f_dv, f_dck = _attn_bwd(z, z, sv["fox_v"], *fcols, HEAD_DIM, do_fox, sv["o_fox"], sv["lse_fox"],
                                        sv["cum"], sv["cum_t"], scale=HEAD_DIM ** -0.5, hp=HP_BWD,
                                        name=f"fox_bwd_{tag}")
    dcum_t = jnp.pad(f_dck.reshape(N_HEADS, s), ((0, 8 - N_HEADS), (0, 0)))
    f_df, f_dbias = _fox_cum_bwd(z, p["fox_bias"], dcum_t, name=f"fox_cum_bwd_{tag}")

    m_dq, m_dk, m_dv = _attn_bwd(sv["qcat"], sv["kcat"], sv["vm"], 0, 0, 0, MLA_QW, do_mla, sv["o_mla"], sv["lse_mla"],
                                 None, None, scale=(MLA_NOPE + MLA_ROPE) ** -0.5, hp=HP_BWD, name=f"mla_bwd_{tag}")
    m_dcq, m_dckv, m_dkr, m_dwuq, m_dwk, m_dwv, m_dgq, m_dgkv = _mla_prep_bwd(
        z, p["gq"], p["gkv"], p["wuq"], p["wk"], p["wv"], c["tabs64"], m_dq, m_dk, m_dv, name=f"mla_prep_bwd_{tag}")

    dz = _to_groups(dict(
        sb_q=sb_dq, sb_k=sb_dk, sb_v=sb_dv, sb_gate=dgate[0], nsa_q=n_dq, nsa_k_cmp=_cmp_unrows(dxa[0], dxb[0]),
        nsa_v_cmp=_cmp_unrows(dxa[1], dxb[1]), nsa_k_sel=n_dks, nsa_v_sel=n_dvs, nsa_k_win=n_dkw, nsa_v_win=n_dvw,
        nsa_branch=n_dbr, nsa_gate=dgate[1], fox_q=f_dq, fox_k=f_dk, fox_v=f_dv, fox_f=f_df, fox_gate=dgate[2],
        mla_cq=m_dcq, mla_ckv=m_dckv, mla_k_rope=m_dkr, mla_gate=dgate[3]), s, _MXU)
    dh = _matmul(dz, p["w_in"], "nn", name=f"inproj_dx_{tag}")
    dw_in = _matmul(sv["h"], dz, "tn", name=f"inproj_dw_{tag}")
    db = _colsum(dz, name=f"inproj_db_{tag}")
    dx, dg_pre = _rms_bwd(dh, sv["x"], p["pre_g"], res=dy, name=f"prenorm_bwd_{tag}")

    qw = MLA_NOPE + MLA_ROPE
    grads = {
        "pre_norm_g": dg_pre[0], "post_norm_g": dg_post[0], "w_in": dw_in, "b_in": _from_groups(db)[0],
        "w_out": dw_out, "fox_forget_bias": f_dbias[0, :N_HEADS],
        "nsa_cmp_pos_k": dpos[0, :flat].reshape(CMP_LEN, HEAD_DIM), "nsa_cmp_w1_k": dw1[0], "nsa_cmp_w2_k": dw2[0],
        "nsa_cmp_pos_v": dpos[0, flat:].reshape(CMP_LEN, HEAD_DIM), "nsa_cmp_w1_v": dw1[1], "nsa_cmp_w2_v": dw2[1],
        "mla_q_norm_g": m_dgq[0],
        "mla_w_uq": jnp.concatenate([m_dwuq[:, MLA_QW * h:MLA_QW * h + qw] for h in range(N_HEADS)], axis=1),
        "mla_kv_norm_g": m_dgkv[0],
        "mla_w_ukv": jnp.concatenate(sum([[m_dwk[:, LANES * h:LANES * (h + 1)], m_dwv[:, LANES * h:LANES * (h + 1)]]
                                          for h in range(N_HEADS)], []), axis=1),
    }
    return dx, grads


def _layer_params(w, l):
    b_in = w["b_in"][l].reshape(1, -1)
    b_segs = {n: b_in[:, _ORIG[n]:_ORIG[n] + wd] for n, wd in _SEGS}
    qw = MLA_NOPE + MLA_ROPE
    w_uq, w_ukv = w["mla_w_uq"][l], w["mla_w_ukv"][l]
    uq = []
    for h in range(N_HEADS):
        uq += [w_uq[:, qw * h:qw * (h + 1)], jnp.zeros((w_uq.shape[0], MLA_QW - qw), w_uq.dtype)]
    kw_ = 2 * LANES
    flat = CMP_LEN * HEAD_DIM
    return dict(
        pre_g=w["pre_norm_g"][l].reshape(1, -1), post_g=w["post_norm_g"][l].reshape(1, -1),
        w_in=w["w_in"][l], b_in=_to_groups(b_segs, 1, f32), w_out=w["w_out"][l],
        fox_bias=jnp.pad(w["fox_forget_bias"][l], (0, LANES - N_HEADS)).reshape(1, LANES),
        cmp_pos=jnp.stack([w["nsa_cmp_pos_k"][l].reshape(1, flat), w["nsa_cmp_pos_v"][l].reshape(1, flat)]),
        cmp_w1=jnp.stack([w["nsa_cmp_w1_k"][l], w["nsa_cmp_w1_v"][l]]),
        cmp_w2=jnp.stack([w["nsa_cmp_w2_k"][l], w["nsa_cmp_w2_v"][l]]),
        gq=w["mla_q_norm_g"][l].reshape(1, -1), gkv=w["mla_kv_norm_g"][l].reshape(1, -1),
        wuq=jnp.concatenate(uq, axis=1),
        wk=jnp.concatenate([w_ukv[:, kw_ * h:kw_ * h + LANES] for h in range(N_HEADS)], axis=1),
        wv=jnp.concatenate([w_ukv[:, kw_ * h + LANES:kw_ * (h + 1)] for h in range(N_HEADS)], axis=1),
    )


def _consts(s):
    pos = jnp.arange(s)
    m2s, e3 = _nsa_consts(s)
    return dict(tabs128=_rope_tables(pos, HEAD_DIM), tabs64=_rope_tables(pos, MLA_ROPE),
                tabs_cmp=_rope_tables(jnp.arange(s // CMP_STRIDE) * CMP_STRIDE + (CMP_LEN - 1), HEAD_DIM),
                m2s=m2s, e3=e3)


def _place():
    return lax.axis_index("x"), lax.axis_index("y"), lax.axis_index("c")


def _other_chips(x, y):
    return [(1 - x, y), (x, 1 - y), (1 - x, 1 - y)]


def _comm_call(body, out_shapes, n_sems, arrs, name):
    return pl.pallas_call(body, out_shape=tuple(out_shapes), in_specs=[_ANY] * len(arrs),
                          out_specs=tuple(_ANY for _ in out_shapes),
                          scratch_shapes=[pltpu.SemaphoreType.DMA((n_sems,)), pltpu.SemaphoreType.DMA((n_sems,))],
                          name=name)(*arrs)


def _gather_chips(arrs, *, name):
    n = len(arrs)

    def body(*refs):
        a_refs, out_refs, send_sems, recv_sems = refs[:n], refs[n:2 * n], refs[2 * n], refs[2 * n + 1]
        x, y, c = _place()
        me = 2 * x + y
        sibling = (x, y, 1 - c)
        chips = _other_chips(x, y)

        def copy(j, k, src, dst, to):
            return pltpu.make_async_remote_copy(src, dst, send_sems.at[6 * j + k], recv_sems.at[6 * j + k],
                                                device_id=to, device_id_type=_MESH)

        first = [copy(j, k, a_refs[j].at[c], out_refs[j].at[me, c], (px, py, c))
                 for k, (px, py) in enumerate(chips) for j in range(n)]
        for cp in first:
            cp.start()
        passed = []
        for k, (px, py) in enumerate(chips):
            for j in range(n):
                landed = out_refs[j].at[2 * px + py, c]
                copy(j, k, a_refs[j].at[c], landed, (px, py, c)).wait_recv()
                passed.append(copy(j, 3 + k, landed, landed, sibling))
                passed[-1].start()
        for k, (px, py) in enumerate(chips):
            for j in range(n):
                copy(j, 3 + k, a_refs[j].at[c], out_refs[j].at[2 * px + py, 1 - c], sibling).wait_recv()
        for cp in first + passed:
            cp.wait_send()

    return _comm_call(body, [_SDS((N_CHIPS,) + a.shape, a.dtype) for a in arrs], 6 * n, arrs, name)


def _alltoall_chips(arrs, lane_slots, *, name):
    n = len(arrs)

    def slot(ref, lanes, s):
        if lanes:
            w = ref.shape[2] // N_CHIPS
            return ref.at[0, :, pl.ds(s * w, w)]
        return ref.at[s]

    def body(*refs):
        g_refs, out_refs, send_sems, recv_sems = refs[:n], refs[n:2 * n], refs[2 * n], refs[2 * n + 1]
        x, y, c = _place()
        me = 2 * x + y

        def copy(j, s):
            return pltpu.make_async_remote_copy(slot(g_refs[j], lane_slots[j], s), out_refs[j].at[me],
                                                send_sems.at[N_CHIPS * j + s], recv_sems.at[N_CHIPS * j + me],
                                                device_id=(s // 2, s % 2, c), device_id_type=_MESH)

        for s in range(N_CHIPS):
            @pl.when(s != me)
            def _():
                for j in range(n):
                    copy(j, s).start()
        for t in range(N_CHIPS):
            @pl.when(t != me)
            def _():
                for j in range(n):
                    pltpu.make_async_remote_copy(slot(g_refs[j], lane_slots[j], t), out_refs[j].at[t],
                                                 send_sems.at[N_CHIPS * j + t], recv_sems.at[N_CHIPS * j + t],
                                                 device_id=(t // 2, t % 2, c), device_id_type=_MESH).wait_recv()
        for s in range(N_CHIPS):
            @pl.when(s != me)
            def _():
                for j in range(n):
                    copy(j, s).wait_send()

    outs = [_SDS((N_CHIPS, a.shape[1], a.shape[2] // N_CHIPS if lanes else a.shape[2]), a.dtype)
            for a, lanes in zip(arrs, lane_slots)]
    return _comm_call(body, outs, N_CHIPS * n, arrs, name)


def _swap_other_half(arrs, *, name):
    n = len(arrs)

    def body(*refs):
        g_refs, out_refs, send_sems, recv_sems = refs[:n], refs[n:2 * n], refs[2 * n], refs[2 * n + 1]
        x, y, c = _place()
        cps = [pltpu.make_async_remote_copy(g_refs[j].at[:, 1 - c], out_refs[j], send_sems.at[j], recv_sems.at[j],
                                            device_id=(x, y, 1 - c), device_id_type=_MESH) for j in range(n)]
        for cp in cps:
            cp.start()
        for cp in cps:
            cp.wait()

    return _comm_call(body, [_SDS((a.shape[0],) + a.shape[2:], a.dtype) for a in arrs], n, arrs, name)


def _swap_sibling(arrs, *, name):
    n = len(arrs)

    def body(*refs):
        f_refs, out_refs, send_sems, recv_sems = refs[:n], refs[n:2 * n], refs[2 * n], refs[2 * n + 1]
        x, y, c = _place()
        cps = [pltpu.make_async_remote_copy(f_refs[j], out_refs[j], send_sems.at[j], recv_sems.at[j],
                                            device_id=(x, y, 1 - c), device_id_type=_MESH) for j in range(n)]
        for cp in cps:
            cp.start()
        for cp in cps:
            cp.wait()

    return _comm_call(body, [_SDS(a.shape, a.dtype) for a in arrs], n, arrs, name)


_HBM = pl.BlockSpec(memory_space=pltpu.HBM)
_SEM = pl.BlockSpec(memory_space=pltpu.SEMAPHORE)
_EFFECT = pltpu.SideEffectType.DATAFLOW_SIDE_EFFECTING


def _slot_ref(ref, mode, s):
    if mode == "same":
        return ref
    if mode == "lanes":
        w = ref.shape[-1] // N_CHIPS
        return ref.at[:, pl.ds(s * w, w)]
    return ref.at[s]


def _slot_shape(a, mode):
    return {"same": a.shape, "lanes": a.shape[:-1] + (a.shape[-1] // N_CHIPS,), "slots": a.shape[1:]}[mode]


def _send_start(arrs, modes, after, *, name):
    n = len(arrs)
    lands = [lax.empty((N_CHIPS,) + _slot_shape(a, m), a.dtype) for a, m in zip(arrs, modes)]

    def body(*refs):
        srcs, land_refs, send_sems, recv_sems, token = refs[:n], refs[n:2 * n], refs[2 * n + 1], refs[2 * n + 2], refs[-1]
        x, y, c = _place()
        me = 2 * x + y
        for s in range(N_CHIPS):
            @pl.when(s != me)
            def _():
                for j in range(n):
                    pltpu.make_async_remote_copy(_slot_ref(srcs[j], modes[j], s), land_refs[j].at[me],
                                                 send_sems.at[N_CHIPS * j + s], recv_sems.at[N_CHIPS * j + me],
                                                 device_id=(s // 2, s % 2, c), device_id_type=_MESH).start()
        token[...] = jnp.zeros_like(token)

    hbm = lambda a: pltpu.HBM(a.shape, a.dtype)
    sems = pltpu.SemaphoreType.DMA((N_CHIPS * n,))
    out = pl.pallas_call(
        body, name=name, out_shape=(sems, sems, *[hbm(a) for a in arrs], *[hbm(a) for a in lands], _SDS((8, LANES), f32)),
        in_specs=[_HBM] * (2 * n) + [_ANY], out_specs=(_SEM, _SEM, *[_HBM] * (2 * n), pl.BlockSpec(memory_space=pltpu.VMEM)),
        input_output_aliases={j: 2 + j for j in range(2 * n)},
        compiler_params=pltpu.CompilerParams(has_side_effects=_EFFECT),
    )(*[pltpu.with_memory_space_constraint(a, pltpu.HBM) for a in arrs + lands], after)
    return out[:-1], out[-1]


def _send_wait(started, modes, after, *, name):
    send_sems, recv_sems = started[0], started[1]
    n = (len(started) - 2) // 2
    thru = list(started[2:])

    def body(*refs):
        srcs, land_refs, send_sems, recv_sems = refs[:n], refs[n:2 * n], refs[2 * n], refs[2 * n + 1]
        x, y, c = _place()
        me = 2 * x + y
        for s in range(N_CHIPS):
            @pl.when(s != me)
            def _():
                for j in range(n):
                    cp = pltpu.make_async_remote_copy(_slot_ref(srcs[j], modes[j], s), land_refs[j].at[s],
                                                      send_sems.at[N_CHIPS * j + s], recv_sems.at[N_CHIPS * j + s],
                                                      device_id=(s // 2, s % 2, c), device_id_type=_MESH)
                    cp.wait_send()
                    cp.wait_recv()

    hbm = lambda a: pltpu.HBM(a.shape, a.dtype)
    out = pl.pallas_call(
        body, name=name, out_shape=tuple(hbm(a) for a in thru), in_specs=[_HBM] * (2 * n) + [_SEM, _SEM, _ANY],
        out_specs=tuple([_HBM] * (2 * n)), input_output_aliases={j: j for j in range(2 * n)},
        compiler_params=pltpu.CompilerParams(has_side_effects=_EFFECT),
    )(*thru, send_sems, recv_sems, after)
    return list(out[n:])


def _gather_all(a, *, name):
    def body(a_ref, out_ref, send_sems, recv_sems, local_sem):
        x, y, c = _place()
        flip = lambda v, f: (1 - v) if f else v
        peers = [(flip(x, f & 4), flip(y, f & 2), flip(c, f & 1)) for f in range(1, 8)]
        me = 4 * x + 2 * y + c
        mine = pltpu.make_async_copy(a_ref, out_ref.at[me], local_sem)
        mine.start()
        sends = [pltpu.make_async_remote_copy(a_ref, out_ref.at[me], send_sems.at[k], recv_sems.at[k], device_id=peer,
                                              device_id_type=_MESH) for k, peer in enumerate(peers)]
        for cp in sends:
            cp.start()
        for k, (px, py, pc) in enumerate(peers):
            pltpu.make_async_remote_copy(a_ref, out_ref.at[4 * px + 2 * py + pc], send_sems.at[k], recv_sems.at[k],
                                         device_id=(px, py, pc), device_id_type=_MESH).wait_recv()
        for cp in sends:
            cp.wait_send()
        mine.wait()

    return pl.pallas_call(body, out_shape=_SDS((8,) + a.shape, a.dtype), in_specs=[_ANY], out_specs=_ANY,
                          scratch_shapes=[pltpu.SemaphoreType.DMA((7,)), pltpu.SemaphoreType.DMA((7,)),
                                          pltpu.SemaphoreType.DMA], name=name)(a)


def _add_my_half(g, r, *, name):
    p, _, h, w = g.shape
    tw = _pick(w, (2048, 1024, 512, 256, 128))
    rb = max(d for d in range(16, h + 1, 16) if h % d == 0 and d * tw * 4 <= (2 << 20))

    def body(c_ref, g_ref, r_ref, o_ref):
        o_ref[...] = (g_ref[...] + r_ref[...]).astype(o_ref.dtype)

    blk = pl.BlockSpec((None, rb, tw), lambda s, i, j, c_ref: (s, i, j))
    grid_spec = pltpu.PrefetchScalarGridSpec(
        num_scalar_prefetch=1, grid=(p, h // rb, w // tw),
        in_specs=[pl.BlockSpec((None, None, rb, tw), lambda s, i, j, c_ref: (s, c_ref[0], i, j)), blk], out_specs=blk)
    c = lax.axis_index("c").astype(jnp.int32).reshape(1)
    return pl.pallas_call(body, out_shape=_SDS((p, h, w), _WIRE), grid_spec=grid_spec,
                          compiler_params=_cp(("parallel", "parallel", "parallel")), name=name)(c, g, r)


_WEIGHTS = ("pre_norm_g", "post_norm_g", "w_in", "b_in", "w_out", "fox_forget_bias", "nsa_cmp_pos_k", "nsa_cmp_w1_k",
            "nsa_cmp_w2_k", "nsa_cmp_pos_v", "nsa_cmp_w1_v", "nsa_cmp_w2_v", "mla_q_norm_g", "mla_w_uq",
            "mla_kv_norm_g", "mla_w_ukv")
_SHARD_AXIS = {"w_in": 2, "w_out": 1, "nsa_cmp_w1_k": 1, "nsa_cmp_w1_v": 1, "mla_w_uq": 2, "mla_w_ukv": 2}
_PACK_UNIT = 16 * LANES


def _pack(arrays, dtype):
    rows = []
    for a in arrays:
        v = a.astype(dtype).reshape(-1)
        pad = (-v.shape[0]) % _PACK_UNIT
        if pad:
            v = jnp.concatenate([v, jnp.zeros((pad,), dtype)])
        rows.append(v.reshape(-1, LANES))
    return jnp.concatenate(rows, axis=0)


def _unpack(flat, shapes):
    out, r = [], 0
    for shp in shapes:
        n = int(np.prod(shp))
        nr = -(-n // _PACK_UNIT) * (_PACK_UNIT // LANES)
        out.append(flat[r:r + nr].reshape(-1)[:n].reshape(shp))
        r += nr
    return out


def kernel(x, pre_norm_g, post_norm_g, w_in, b_in, w_out, fox_forget_bias, nsa_cmp_pos_k, nsa_cmp_w1_k, nsa_cmp_w2_k, nsa_cmp_pos_v, nsa_cmp_w1_v, nsa_cmp_w2_v, mla_q_norm_g, mla_w_uq, mla_kv_norm_g, mla_w_ukv, loss_target, m_pre_norm_g, m_post_norm_g, m_w_in, m_b_in, m_w_out, m_fox_forget_bias, m_nsa_cmp_pos_k, m_nsa_cmp_w1_k, m_nsa_cmp_w2_k, m_nsa_cmp_pos_v, m_nsa_cmp_w1_v, m_nsa_cmp_w2_v, m_mla_q_norm_g, m_mla_w_uq, m_mla_kv_norm_g, m_mla_w_ukv, v_pre_norm_g, v_post_norm_g, v_w_in, v_b_in, v_w_out, v_fox_forget_bias, v_nsa_cmp_pos_k, v_nsa_cmp_w1_k, v_nsa_cmp_w2_k, v_nsa_cmp_pos_v, v_nsa_cmp_w1_v, v_nsa_cmp_w2_v, v_mla_q_norm_g, v_mla_w_uq, v_mla_kv_norm_g, v_mla_w_ukv):
    given = dict(locals())
    local = {n: given[n] for n in _WEIGHTS}
    depth = pre_norm_g.shape[0]
    xs, target = x[0], loss_target[0]
    s = xs.shape[0]
    sharded = [n for n in _WEIGHTS if n in _SHARD_AXIS and n != "w_in"]
    small = [n for n in _WEIGHTS if n not in _SHARD_AXIS]
    chip = 2 * lax.axis_index("x") + lax.axis_index("y")
    core = lax.axis_index("c")
    own = lambda slots, mine: lax.dynamic_update_slice_in_dim(slots, mine[None], chip, axis=0)

    w_in_t = jnp.swapaxes(w_in, 1, 2).astype(_MXU)
    piece = lax.switch(chip, [functools.partial(_piece_from_shard, s=k) for k in range(N_CHIPS)], w_in_t)
    shard_shapes = [local[n].shape for n in sharded]
    flat = _pack([local[n] for n in sharded], _MXU)
    flat2 = flat.reshape((2, -1, LANES))
    first = piece[0].reshape(2, GROUP_W // 2, D_MODEL)
    first_all, flat_all = _gather_chips([first, flat2], name="gather_weights")
    flat_all = own(flat_all, flat2).reshape((N_CHIPS,) + flat.shape)
    per_chip = [_unpack(flat_all[k], shard_shapes) for k in range(N_CHIPS)]
    full = dict(local)
    full["w_in"] = [own(first_all, first).reshape(N_CHIPS, GROUP_W, D_MODEL)]
    for j, n in enumerate(sharded):
        full[n] = jnp.concatenate([per_chip[k][j] for k in range(N_CHIPS)], axis=_SHARD_AXIS[n])
    later = [piece[l] for l in range(1, depth)]
    started, token = _send_start(later, ["same"] * len(later), flat_all, name="gather_later_start")
    full["pre_norm_g"] = pre_norm_g + token[0, 0]

    consts = _consts(s)
    params, act, saved = [], xs, []
    for l in range(depth):
        if l == 1:
            landed = _send_wait(started, ["same"] * len(later), act, name="gather_later_wait")
            full["w_in"] += [own(a, b) for a, b in zip(landed, later)]
        params.append(_layer_params(full, l))
        act, sv = _layer_fwd(act, params[l], consts, f"l{l}")
        saved.append(sv)
    dy, loss_parts = _loss_head(act, target, name="loss_head")
    layer_shapes = [local[n].shape[1:] for n in sharded]

    def flat_slots(g, dtype):
        def part(n, k):
            a, ax = g[n], _SHARD_AXIS[n] - 1
            w = a.shape[ax] // N_CHIPS
            return lax.slice_in_dim(a, k * w, (k + 1) * w, axis=ax)
        return jnp.stack([_pack([part(n, k) for n in sharded], dtype) for k in range(N_CHIPS)])

    def own_slot(a, mode):
        return lax.dynamic_slice_in_dim(a, chip * GROUP_W, GROUP_W, axis=1) if mode == "lanes" else \
            lax.dynamic_index_in_dim(a, chip, axis=0, keepdims=False)

    modes = ["lanes", "slots"]
    layer_grads, in_flight = [None] * depth, {}
    for l in reversed(range(depth)):
        dy, layer_grads[l] = _layer_bwd(dy, saved[l], params[l], consts, f"l{l}")
        if l > 0:
            wire = [layer_grads[l]["w_in"].astype(_WIRE), flat_slots(layer_grads[l], _WIRE)]
            started, token = _send_start(wire, modes, dy, name=f"reduce_l{l}_start")
            in_flight[l] = (started, wire)
            params[l - 1] = dict(params[l - 1], post_g=params[l - 1]["post_g"] + token[0, 0])
    grad_x = dy[None]
    grads = {n: jnp.stack([layer_grads[l][n] for l in range(depth)]) for n in small}

    halves = [layer_grads[0]["w_in"].reshape(1, 2, D_MODEL // 2, ZW),
              flat_slots(layer_grads[0], f32).reshape(N_CHIPS, 2, -1, LANES)]
    from_sibling = _swap_other_half(halves, name="reduce_pair")
    pair_sum = [_add_my_half(g, r, name=f"reduce_pair_add{j}") for j, (g, r) in enumerate(zip(halves, from_sibling))]
    from_chips = _alltoall_chips(pair_sum, [True, False], name="reduce_chips")
    pair_sum[0] = pair_sum[0][0]
    my_half = [_sum_slots(own(slots, own_slot(ps, mode)), name=f"reduce_chips_add{j}")
               for j, (slots, ps, mode) in enumerate(zip(from_chips, pair_sum, modes))]
    partial = []
    for l in range(1, depth):
        started, wire = in_flight[l]
        landed = _send_wait(started, modes, dy, name=f"reduce_l{l}_wait")
        partial += [_sum_slots(own(slots, own_slot(a, mode)), name=f"reduce_l{l}_add{j}")
                    for j, (slots, a, mode) in enumerate(zip(landed, wire, modes))]
    theirs = _swap_sibling(my_half + partial, name="reduce_share")
    first = core == 0
    whole = [jnp.concatenate([jnp.where(first, a, b), jnp.where(first, b, a)], axis=0)
             for a, b in zip(my_half, theirs[:2])]
    whole += [_add2(a[None], b[None], name=f"reduce_cores_add{j}")[0] for j, (a, b) in enumerate(zip(partial, theirs[2:]))]
    unpiece = [functools.partial(_shard_from_piece, s=k) for k in range(N_CHIPS)]
    summed = {"w_in": jnp.stack([lax.switch(chip, unpiece, whole[2 * l]) for l in range(depth)])}
    rest = [_unpack(whole[2 * l + 1], layer_shapes) for l in range(depth)]
    for j, n in enumerate(sharded):
        summed[n] = jnp.stack([rest[l][j] for l in range(depth)])

    loss_row = jnp.concatenate([jnp.sum(loss_parts).reshape(1), jnp.zeros((LANES - 1,), f32)])
    small_shapes = [(LANES,)] + [grads[n].shape for n in small]
    contrib = _pack([loss_row] + [grads[n] for n in small], f32)
    pad_rows = (-contrib.shape[0]) % 8
    if pad_rows:
        contrib = jnp.concatenate([contrib, jnp.zeros((pad_rows, LANES), f32)], axis=0)
    total = _unpack(_sum_slots(_gather_all(contrib, name="gather_small"), name="sum_small"), small_shapes)
    loss = total[0][0]
    summed.update(zip(small, total[1:]))

    deltas, new_m, new_v = {}, {}, {}
    for n in _WEIGHTS:
        deltas[n], new_m[n], new_v[n] = _adamw(local[n], summed[n], given["m_" + n], given["v_" + n], name=f"adamw_{n}")
    return (loss, grad_x, *[summed[n] for n in _WEIGHTS], *[deltas[n] for n in _WEIGHTS],
            *[new_m[n] for n in _WEIGHTS], *[new_v[n] for n in _WEIGHTS])
```

```python
import functools
import math

import numpy as np
import jax
import jax.numpy as jnp
from jax import lax
from jax.experimental import pallas as pl
from jax.experimental.pallas import tpu as pltpu

f32 = jnp.float32
bf16 = jnp.bfloat16
_MXU = jnp.bfloat16
_WIRE = jnp.bfloat16
_SDS = jax.ShapeDtypeStruct
_ANY = pl.BlockSpec(memory_space=pl.ANY)
_MESH = pl.DeviceIdType.MESH

D_MODEL = 2048
N_HEADS = 4
HEAD_DIM = 128
GROUP = 512
RMS_EPS = 1e-6
NEG_INF = -1e30
ROPE_THETA = 10000.0
CMP_LEN, CMP_STRIDE, SEL_LEN, SEL_TOPN, WINDOW = 32, 16, 64, 16, 512
FORCED_BONUS = 1e6
MLA_Q_RANK, MLA_KV_RANK, MLA_NOPE, MLA_ROPE = 384, 128, 128, 64
ADAM_LR, ADAM_B1, ADAM_B2, ADAM_EPS, ADAM_WD, ADAM_STEP = 0.001, 0.9, 0.999, 1e-08, 0.01, 10
LANES = 128
VMEM_LIMIT = 48 * 1024 * 1024
HP_FWD, HP_BWD = 4, 2

_SEGS = (
    ("sb_q", 512), ("sb_k", 512), ("sb_v", 512), ("sb_gate", 512), ("nsa_q", 512), ("nsa_k_cmp", 128),
    ("nsa_v_cmp", 128), ("nsa_k_sel", 128), ("nsa_v_sel", 128), ("nsa_k_win", 128), ("nsa_v_win", 128),
    ("nsa_branch", 12), ("nsa_gate", 512), ("fox_q", 512), ("fox_k", 512), ("fox_v", 512), ("fox_f", 4),
    ("fox_gate", 512), ("mla_cq", 384), ("mla_ckv", 128), ("mla_k_rope", 64), ("mla_gate", 512),
)
_ORIG, _WID = {}, {}
_o = 0
for _n, _w in _SEGS:
    _ORIG[_n], _WID[_n] = _o, _w
    _o += _w
IN_WIDTH = _o
N_CHIPS = 4
CHIP_COLS = IN_WIDTH // N_CHIPS
GROUP_W = 2048
ZW = N_CHIPS * GROUP_W
_GROUPS = (
    (("sb_q", 0, 512, 0), ("sb_k", 0, 512, 512), ("sb_v", 0, 512, 1024), ("sb_gate", 0, 212, 1536)),
    (("nsa_q", 0, 512, 0), ("nsa_k_cmp", 0, 128, 512), ("nsa_v_cmp", 0, 128, 640), ("nsa_k_sel", 0, 128, 768),
     ("nsa_v_sel", 0, 128, 896), ("nsa_k_win", 0, 128, 1024), ("nsa_v_win", 0, 128, 1152), ("nsa_branch", 0, 12, 1280),
     ("sb_gate", 212, 512, 1408), ("nsa_gate", 0, 156, 1712)),
    (("fox_q", 0, 512, 0), ("fox_k", 0, 512, 512), ("fox_v", 0, 368, 1024), ("nsa_gate", 156, 512, 1408)),
    (("mla_cq", 0, 384, 0), ("mla_ckv", 0, 128, 384), ("mla_k_rope", 0, 64, 512), ("fox_f", 0, 4, 640),
     ("fox_v", 368, 512, 768), ("fox_gate", 0, 512, 1024), ("mla_gate", 0, 512, 1536)),
)
_PIECES = {n: [] for n, _ in _SEGS}
for _s, _grp in enumerate(_GROUPS):
    _cover = sorted((_ORIG[n] + lo, _ORIG[n] + hi) for n, lo, hi, _ in _grp)
    assert _cover[0][0] == _s * CHIP_COLS and _cover[-1][1] == (_s + 1) * CHIP_COLS
    assert all(a[1] == b[0] for a, b in zip(_cover, _cover[1:]))
    _ends = sorted((off, off + hi - lo) for _, lo, hi, off in _grp)
    assert all(a[1] <= b[0] for a, b in zip(_ends, _ends[1:])) and _ends[-1][1] <= GROUP_W
    assert _ends[0][0] == 0 and all(e[0] % 16 == 0 for e in _ends)
    for _n, _lo, _hi, _off in _grp:
        _PIECES[_n].append((_s * GROUP_W + _off, _lo, _hi))
_AL = {n: p[0][0] for n, p in _PIECES.items() if len(p) == 1}


def _cp(sem=None):
    return pltpu.CompilerParams(dimension_semantics=sem, vmem_limit_bytes=VMEM_LIMIT)


def _mm(a, b):
    return jnp.dot(a.astype(_MXU), b.astype(_MXU), preferred_element_type=f32)


def _mm_nt(a, b):
    return lax.dot_general(a.astype(_MXU), b.astype(_MXU), (((1,), (1,)), ((), ())), preferred_element_type=f32)


def _mm_tn(a, b):
    return lax.dot_general(a.astype(_MXU), b.astype(_MXU), (((0,), (0,)), ((), ())), preferred_element_type=f32)


def _mm_split(x, t):
    hi = x.astype(_MXU)
    lo = (x - hi.astype(f32)).astype(_MXU)
    return jnp.dot(hi, t, preferred_element_type=f32) + jnp.dot(lo, t, preferred_element_type=f32)


def _sigmoid(x):
    return 1.0 / (1.0 + jnp.exp(-x))


def _iota(shape, dim):
    return lax.broadcasted_iota(jnp.int32, shape, dim)


def _pick(n, prefs):
    for p in prefs:
        if n % p == 0:
            return p
    return n


def _matmul(a, b, mode, *, bias=None, out_dtype=f32, name):
    grouped = b.ndim == 3
    b_shape = (b.shape[0] * b.shape[1], b.shape[2]) if grouped else b.shape
    if mode == "nn":
        (M, K), (K2, N) = a.shape, b_shape
    elif mode == "nt":
        (M, K), (N, K2) = a.shape, b_shape
    else:
        (K, M), (K2, N) = a.shape, b_shape
    assert K == K2
    tm = _pick(M, (1024, 512, 384, 256, 128))
    tn = _pick(N, (512, 384, 256, 128))
    tk = K if K <= 2048 else _pick(K, (2048, 2432, 1024, 512))
    nk = K // tk
    a_spec = {"nn": pl.BlockSpec((tm, tk), lambda i, j, k: (i, k)),
              "nt": pl.BlockSpec((tm, tk), lambda i, j, k: (i, k)),
              "tn": pl.BlockSpec((tk, tm), lambda i, j, k: (k, i))}[mode]
    if not grouped:
        b_spec = {"nn": pl.BlockSpec((tk, tn), lambda i, j, k: (k, j)),
                  "nt": pl.BlockSpec((tn, tk), lambda i, j, k: (j, k)),
                  "tn": pl.BlockSpec((tk, tn), lambda i, j, k: (k, j))}[mode]
    elif mode == "nt":
        per = b.shape[1] // tn
        b_spec = pl.BlockSpec((None, tn, tk), lambda i, j, k: (j // per, j % per, k))
    else:
        assert mode == "nn"
        per = b.shape[1] // tk
        b_spec = pl.BlockSpec((None, tk, tn), lambda i, j, k: (k // per, k % per, j))
    dot = {"nn": _mm, "nt": _mm_nt, "tn": _mm_tn}[mode]
    has_bias = bias is not None

    def body(*refs):
        if has_bias:
            a_ref, b_ref, bias_ref, o_ref, acc_ref = refs
        else:
            a_ref, b_ref, o_ref, acc_ref = refs
            bias_ref = None
        k = pl.program_id(2)
        part = dot(a_ref[...], b_ref[...])

        def finish(total):
            if has_bias:
                total = total + bias_ref[...]
            o_ref[...] = total.astype(o_ref.dtype)

        if nk == 1:
            finish(part)
        else:
            @pl.when(k == 0)
            def _():
                acc_ref[...] = part

            @pl.when(k > 0)
            def _():
                acc_ref[...] += part

            @pl.when(k == nk - 1)
            def _():
                finish(acc_ref[...])

    in_specs = [a_spec, b_spec]
    args = [a, b]
    if has_bias:
        in_specs.append(pl.BlockSpec((1, tn), lambda i, j, k: (0, j)))
        args.append(bias.reshape(1, N))
    return pl.pallas_call(
        body, out_shape=_SDS((M, N), out_dtype), grid=(M // tm, N // tn, nk),
        in_specs=in_specs, out_specs=pl.BlockSpec((tm, tn), lambda i, j, k: (i, j)),
        scratch_shapes=[pltpu.VMEM((tm, tn), f32)],
        compiler_params=_cp(("parallel", "parallel", "arbitrary")), name=name,
    )(*args)


def _row_block(s):
    return _pick(s, (256, 128))


def _rms_fwd(x, g, *, out_dtype, name):
    s, d = x.shape
    rb = _row_block(s)

    def body(x_ref, g_ref, o_ref):
        xv = x_ref[...]
        r = lax.rsqrt(jnp.mean(xv * xv, axis=-1, keepdims=True) + RMS_EPS)
        o_ref[...] = (xv * r * g_ref[...]).astype(o_ref.dtype)

    return pl.pallas_call(
        body, out_shape=_SDS((s, d), out_dtype), grid=(s // rb,),
        in_specs=[pl.BlockSpec((rb, d), lambda i: (i, 0)), pl.BlockSpec((1, d), lambda i: (0, 0))],
        out_specs=pl.BlockSpec((rb, d), lambda i: (i, 0)), compiler_params=_cp(("parallel",)), name=name,
    )(x, g.reshape(1, d))


def _postnorm_fwd(u, g, x, *, name):
    s, d = u.shape
    rb = _row_block(s)

    def body(u_ref, g_ref, x_ref, o_ref):
        uv = u_ref[...]
        r = lax.rsqrt(jnp.mean(uv * uv, axis=-1, keepdims=True) + RMS_EPS)
        o_ref[...] = x_ref[...] + uv * r * g_ref[...]

    return pl.pallas_call(
        body, out_shape=_SDS((s, d), f32), grid=(s // rb,),
        in_specs=[pl.BlockSpec((rb, d), lambda i: (i, 0)), pl.BlockSpec((1, d), lambda i: (0, 0)),
                  pl.BlockSpec((rb, d), lambda i: (i, 0))],
        out_specs=pl.BlockSpec((rb, d), lambda i: (i, 0)), compiler_params=_cp(("parallel",)), name=name,
    )(u, g.reshape(1, d), x)


def _fold_rows(v):
    r = v.shape[0]
    acc = v[0:8]
    for k in range(1, r // 8):
        acc = acc + v[8 * k:8 * k + 8]
    return acc


def _rms_bwd(dy, x, g, res=None, *, name):
    s, d = x.shape
    rb = _row_block(s)
    nb = s // rb
    has_res = res is not None

    def body(*refs):
        if has_res:
            dy_ref, x_ref, g_ref, res_ref, dx_ref, dg_ref, acc_ref = refs
        else:
            dy_ref, x_ref, g_ref, dx_ref, dg_ref, acc_ref = refs
        i = pl.program_id(0)
        xv = x_ref[...]
        r = lax.rsqrt(jnp.mean(xv * xv, axis=-1, keepdims=True) + RMS_EPS)
        xh = xv * r
        dyv = dy_ref[...]
        dxh = dyv * g_ref[...]
        dx = r * (dxh - xh * jnp.mean(dxh * xh, axis=-1, keepdims=True))
        if has_res:
            dx = dx + res_ref[...]
        dx_ref[...] = dx
        part = _fold_rows(dyv * xh)

        @pl.when(i == 0)
        def _():
            acc_ref[...] = part

        @pl.when(i > 0)
        def _():
            acc_ref[...] += part

        @pl.when(i == nb - 1)
        def _():
            dg_ref[...] = jnp.sum(acc_ref[...], axis=0, keepdims=True)

    blk = pl.BlockSpec((rb, d), lambda i: (i, 0))
    in_specs = [blk, blk, pl.BlockSpec((1, d), lambda i: (0, 0))] + ([blk] if has_res else [])
    args = [dy, x, g.reshape(1, d)] + ([res] if has_res else [])
    return pl.pallas_call(
        body, out_shape=(_SDS((s, d), f32), _SDS((1, d), f32)), grid=(nb,), in_specs=in_specs,
        out_specs=(blk, pl.BlockSpec((1, d), lambda i: (0, 0))),
        scratch_shapes=[pltpu.VMEM((8, d), f32)], compiler_params=_cp(("arbitrary",)), name=name,
    )(*args)


def _loss_head(y, target, *, name):
    s, d = y.shape
    rb = _row_block(s)
    nb = s // rb

    def body(y_ref, t_ref, dy_ref, l_ref):
        i = pl.program_id(0)
        e = y_ref[...] - t_ref[...]
        dy_ref[...] = e * (1.0 / d)
        rows = _fold_rows(e * e)
        part = rows[:, 0:LANES]
        for k in range(1, d // LANES):
            part = part + rows[:, k * LANES:(k + 1) * LANES]
        part = part * (0.5 / d)

        @pl.when(i == 0)
        def _():
            l_ref[...] = part

        @pl.when(i > 0)
        def _():
            l_ref[...] += part

    blk = pl.BlockSpec((rb, d), lambda i: (i, 0))
    return pl.pallas_call(
        body, out_shape=(_SDS((s, d), f32), _SDS((8, LANES), f32)), grid=(nb,), in_specs=[blk, blk],
        out_specs=(blk, pl.BlockSpec((8, LANES), lambda i: (0, 0))),
        compiler_params=_cp(("arbitrary",)), name=name,
    )(y, target)


def _colsum(a, *, name):
    s, n = a.shape
    rb = _row_block(s)
    nb = s // rb
    tn = _pick(n, (2432, 2048, 1024, 512, 384, 128))

    def body(a_ref, o_ref, acc_ref):
        i = pl.program_id(1)
        part = _fold_rows(a_ref[...].astype(f32))

        @pl.when(i == 0)
        def _():
            acc_ref[...] = part

        @pl.when(i > 0)
        def _():
            acc_ref[...] += part

        @pl.when(i == nb - 1)
        def _():
            o_ref[...] = jnp.sum(acc_ref[...], axis=0, keepdims=True)

    return pl.pallas_call(
        body, out_shape=_SDS((1, n), f32), grid=(n // tn, nb),
        in_specs=[pl.BlockSpec((rb, tn), lambda j, i: (i, j))], out_specs=pl.BlockSpec((1, tn), lambda j, i: (0, j)),
        scratch_shapes=[pltpu.VMEM((8, tn), f32)], compiler_params=_cp(("parallel", "arbitrary")), name=name,
    )(a)


def _gate_fwd(o, gate, *, name):
    s, d = o.shape
    rb = _row_block(s)

    def body(o_ref, g_ref, m_ref):
        gv = g_ref[...]
        m_ref[...] = (o_ref[...] * (gv * _sigmoid(gv))).astype(m_ref.dtype)

    blk = pl.BlockSpec((rb, d), lambda i: (i, 0))
    return pl.pallas_call(body, out_shape=_SDS((s, d), _MXU), grid=(s // rb,), in_specs=[blk, blk], out_specs=blk,
                          compiler_params=_cp(("parallel",)), name=name)(o, gate)


def _gate_bwd(dmix, o, gate, *, name):
    s, d = o.shape
    rb = _row_block(s)

    def body(dm_ref, o_ref, g_ref, do_ref, dg_ref):
        gv = g_ref[...]
        sg = _sigmoid(gv)
        dm = dm_ref[...]
        do_ref[...] = dm * (gv * sg)
        dg_ref[...] = dm * o_ref[...] * (sg * (1.0 + gv * (1.0 - sg)))

    blk = pl.BlockSpec((rb, d), lambda i: (i, 0))
    return pl.pallas_call(body, out_shape=(_SDS((s, d), f32), _SDS((s, d), f32)), grid=(s // rb,),
                          in_specs=[blk, blk, blk], out_specs=(blk, blk), compiler_params=_cp(("parallel",)),
                          name=name)(dmix, o, gate)


def _adamw(w, g, m, v, *, name):
    shape = w.shape
    cols = shape[-1]
    rows = int(np.prod(shape[:-1])) if len(shape) > 1 else 1
    to2 = lambda t: t.reshape(rows, cols)
    rb = rows
    if rows * cols * 4 > (1 << 20):
        rb = max(d for d in range(8, rows + 1, 8) if rows % d == 0 and (d * cols * 4 <= (1600 << 10) or d == 8))

    def body(w_ref, g_ref, m_ref, v_ref, d_ref, nm_ref, nv_ref):
        gv = g_ref[...]
        mn = ADAM_B1 * m_ref[...] + (1.0 - ADAM_B1) * gv
        vn = ADAM_B2 * v_ref[...] + (1.0 - ADAM_B2) * (gv * gv)
        m_hat = mn / (1.0 - ADAM_B1 ** ADAM_STEP)
        v_hat = vn / (1.0 - ADAM_B2 ** ADAM_STEP)
        d_ref[...] = -ADAM_LR * (m_hat / (jnp.sqrt(v_hat) + ADAM_EPS) + ADAM_WD * w_ref[...])
        nm_ref[...] = mn
        nv_ref[...] = vn

    blk = pl.BlockSpec((rb, cols), lambda i: (i, 0))
    out = pl.pallas_call(body, out_shape=tuple(_SDS((rows, cols), f32) for _ in range(3)), grid=(rows // rb,),
                         in_specs=[blk] * 4, out_specs=(blk,) * 3, compiler_params=_cp(("parallel",)),
                         name=name)(to2(w), to2(g), to2(m), to2(v))
    return tuple(t.reshape(shape) for t in out)


def _sum_slots(a, *, name):
    p, n, c = a.shape
    rb = max(d for d in range(8, n + 1, 8) if n % d == 0 and (p * d * c * 4 <= (6 << 20) or d == 8))

    def body(a_ref, o_ref):
        acc = a_ref[0].astype(f32)
        for k in range(1, p):
            acc = acc + a_ref[k].astype(f32)
        o_ref[...] = acc

    return pl.pallas_call(body, out_shape=_SDS((n, c), f32), grid=(n // rb,),
                          in_specs=[pl.BlockSpec((p, rb, c), lambda i: (0, i, 0))],
                          out_specs=pl.BlockSpec((rb, c), lambda i: (i, 0)), compiler_params=_cp(("parallel",)),
                          name=name)(a)


def _add2(a, b, *, name):
    p, n, c = a.shape
    rb = max(d for d in range(8, n + 1, 8) if n % d == 0 and (d * c * 4 <= (2 << 20) or d == 8))

    def body(a_ref, b_ref, o_ref):
        o_ref[...] = a_ref[...] + b_ref[...]

    blk = pl.BlockSpec((1, rb, c), lambda s, i: (s, i, 0))
    return pl.pallas_call(body, out_shape=_SDS((p, n, c), f32), grid=(p, n // rb), in_specs=[blk, blk], out_specs=blk,
                          compiler_params=_cp(("parallel", "parallel")), name=name)(a, b)


def _rope_tables(pos, dim):
    half = dim // 2
    inv = ROPE_THETA ** (-jnp.arange(half, dtype=f32) / half)
    ang = pos.astype(f32)[:, None] * inv[None, :]
    c, s = jnp.cos(ang), jnp.sin(ang)
    z = jnp.zeros_like(c)
    pad = [jnp.zeros((pos.shape[0], LANES - dim), f32)] if dim < LANES else []
    return (jnp.concatenate([c, c] + pad, axis=1), jnp.concatenate([-s, z] + pad, axis=1),
            jnp.concatenate([z, s] + pad, axis=1))


def _rope(x, cos, sa, sb, half, transpose=False):
    if transpose:
        return x * cos + pltpu.roll(x * sa, half, 1) + pltpu.roll(x * sb, LANES - half, 1)
    return x * cos + pltpu.roll(x, LANES - half, 1) * sa + pltpu.roll(x, half, 1) * sb


def _rope_call(items, tables, half, transpose, *, name):
    s = items[0][0].shape[0]
    rb = _row_block(s)
    n = len(items)

    def body(*refs):
        cos, sa, sb = refs[n][...], refs[n + 1][...], refs[n + 2][...]
        for k in range(n):
            x_ref, o_ref = refs[k], refs[n + 3 + k]
            for j in range(items[k][1] // LANES):
                sl = slice(j * LANES, (j + 1) * LANES)
                o_ref[:, sl] = _rope(x_ref[:, sl], cos, sa, sb, half, transpose)

    in_specs = [pl.BlockSpec((rb, w), functools.partial(lambda i, cb: (i, cb), cb=cb)) for _, w, cb in items]
    in_specs += [pl.BlockSpec((rb, LANES), lambda i: (i, 0))] * 3
    out_specs = tuple(pl.BlockSpec((rb, w), lambda i: (i, 0)) for _, w, _ in items)
    return pl.pallas_call(
        body, out_shape=tuple(_SDS((s, w), f32) for _, w, _ in items), grid=(s // rb,), in_specs=in_specs,
        out_specs=out_specs, compiler_params=_cp(("parallel",)), name=name,
    )(*[a for a, _, _ in items], *tables)


def _attn_block(s):
    return _pick(s, (256, 128))


def _lower_mask(b, strict):
    r, c = _iota((b, b), 0), _iota((b, b), 1)
    return (c < r) if strict else (c <= r)


def _pick_lane(block, h):
    return jnp.sum(jnp.where(_iota(block.shape, 1) == h, block, 0.0), axis=1, keepdims=True)


def _head_bias(cum_blk, g, j, hp):
    if hp == N_HEADS:
        return cum_blk[:, j:j + 1]
    return _pick_lane(cum_blk, g * hp + j)


def _attn_fwd(q, k, v, qcol, kcol, vcol, dq, cum, cum_t, *, scale, hp, name):
    s = q.shape[0]
    b = _attn_block(s)
    nq = s // b
    has_bias = cum is not None
    assert qcol % hp == 0 and kcol % hp == 0 and vcol % hp == 0

    def body(*refs):
        if has_bias:
            q_ref, k_ref, v_ref, cum_ref, cumt_ref, o_ref, lse_ref = refs
        else:
            q_ref, k_ref, v_ref, o_ref, lse_ref = refs
        g, i = pl.program_id(0), pl.program_id(1)
        qs = [q_ref[:, j * dq:(j + 1) * dq].astype(_MXU) for j in range(hp)]
        cqs = [_head_bias(cum_ref[...], g, j, hp) for j in range(hp)] if has_bias else None

        def chunk(c, carry, diag):
            st = pl.multiple_of(c * b, b)
            mask = _lower_mask(b, False) if diag else None
            out = []
            for j in range(hp):
                m, l, acc = carry[j]
                z = _mm_nt(qs[j], k_ref[pl.ds(st, b), j * dq:(j + 1) * dq]) * scale
                if has_bias:
                    z = z + cqs[j] - cumt_ref[j, c]
                if diag:
                    z = jnp.where(mask, z, NEG_INF)
                m_new = jnp.maximum(m, jnp.max(z, axis=1, keepdims=True))
                p = jnp.exp(z - m_new)
                if diag:
                    p = jnp.where(mask, p, 0.0)
                alpha = jnp.exp(m - m_new)
                l = alpha * l + jnp.sum(p, axis=1, keepdims=True)
                acc = alpha * acc + _mm(p, v_ref[pl.ds(st, b), j * HEAD_DIM:(j + 1) * HEAD_DIM])
                out.append((m_new, l, acc))
            return tuple(out)

        init = tuple((jnp.full((b, 1), NEG_INF, f32), jnp.zeros((b, 1), f32), jnp.zeros((b, HEAD_DIM), f32))
                     for _ in range(hp))
        carry = lax.fori_loop(0, i, lambda c, cr: chunk(c, cr, False), init)
        for j, (m, l, acc) in enumerate(chunk(i, carry, True)):
            o_ref[:, j * HEAD_DIM:(j + 1) * HEAD_DIM] = acc / l
            lse_ref[j] = m + jnp.log(l)

    in_specs = [pl.BlockSpec((b, hp * dq), lambda g, i: (i, qcol // hp + g)),
                pl.BlockSpec((s, hp * dq), lambda g, i: (0, kcol // hp + g)),
                pl.BlockSpec((s, hp * HEAD_DIM), lambda g, i: (0, vcol // hp + g))]
    args = [q, k, v]
    if has_bias:
        in_specs += [pl.BlockSpec((b, LANES), lambda g, i: (i, 0)),
                     pl.BlockSpec((hp, nq, 1, b), lambda g, i: (g, 0, 0, 0))]
        args += [cum, cum_t]
    return pl.pallas_call(
        body, out_shape=(_SDS((s, N_HEADS * HEAD_DIM), f32), _SDS((N_HEADS, s, 1), f32)), grid=(N_HEADS // hp, nq),
        in_specs=in_specs,
        out_specs=(pl.BlockSpec((b, hp * HEAD_DIM), lambda g, i: (i, g)),
                   pl.BlockSpec((hp, b, 1), lambda g, i: (g, i, 0))),
        compiler_params=_cp(("parallel", "parallel")), name=name,
    )(*args)


def _attn_bwd(q, k, v, qcol, kcol, vcol, dq, do, o, lse, cum, cum_t, *, scale, hp, name):
    s = q.shape[0]
    b = _attn_block(s)
    nq = s // b
    has_bias = cum is not None
    assert qcol % hp == 0 and kcol % hp == 0 and vcol % hp == 0
    hd = lambda j: slice(j * HEAD_DIM, (j + 1) * HEAD_DIM)
    hq = lambda j: slice(j * dq, (j + 1) * dq)

    def body(*refs):
        if has_bias:
            (q_ref, k_ref, v_ref, do_ref, o_ref, lse_ref, cum_ref, cumt_ref, dq_ref, dk_ref, dv_ref, dck_ref,
             p_sc, dp_sc) = refs
        else:
            q_ref, k_ref, v_ref, do_ref, o_ref, lse_ref, dq_ref, dk_ref, dv_ref = refs
        g, i = pl.program_id(0), pl.program_id(1)

        @pl.when(i == 0)
        def _():
            dk_ref[...] = jnp.zeros_like(dk_ref)
            dv_ref[...] = jnp.zeros_like(dv_ref)
            if has_bias:
                dck_ref[...] = jnp.zeros_like(dck_ref)

        qs = [q_ref[:, hq(j)].astype(_MXU) for j in range(hp)]
        dos = [do_ref[:, hd(j)].astype(_MXU) for j in range(hp)]
        lses = [lse_ref[j] for j in range(hp)]
        cqs = [_head_bias(cum_ref[...], g, j, hp) for j in range(hp)] if has_bias else None

        def probs(j, c, diag):
            st = pl.multiple_of(c * b, b)
            z = _mm_nt(qs[j], k_ref[pl.ds(st, b), hq(j)]) * scale
            if has_bias:
                z = z + cqs[j] - cumt_ref[j, c]
            p = jnp.exp(z - lses[j])
            if diag:
                p = jnp.where(_lower_mask(b, False), p, 0.0)
            return p, _mm_nt(dos[j], v_ref[pl.ds(st, b), hd(j)])

        if has_bias:
            def first(c, accs, diag):
                out = []
                for j in range(hp):
                    p, dp = probs(j, c, diag)
                    p_sc[j, c] = p
                    dp_sc[j, c] = dp
                    out.append(accs[j] + jnp.sum(p * dp, axis=1, keepdims=True))
                return tuple(out)

            deltas = lax.fori_loop(0, i, lambda c, a: first(c, a, False),
                                   tuple(jnp.zeros((b, 1), f32) for _ in range(hp)))
            deltas = first(i, deltas, True)
        else:
            deltas = [jnp.sum(do_ref[:, hd(j)] * o_ref[:, hd(j)], axis=1, keepdims=True) for j in range(hp)]

        def chunk(c, dq_accs, diag):
            st = pl.multiple_of(c * b, b)
            out = []
            for j in range(hp):
                p, dp = (p_sc[j, c], dp_sc[j, c]) if has_bias else probs(j, c, diag)
                ds = p * (dp - deltas[j])
                dk_ref[pl.ds(st, b), hq(j)] += _mm_tn(ds, qs[j]) * scale
                dv_ref[pl.ds(st, b), hd(j)] += _mm_tn(p, dos[j])
                if has_bias:
                    dck_ref[j, c] += -jnp.sum(ds, axis=0, keepdims=True)
                out.append(dq_accs[j] + _mm(ds, k_ref[pl.ds(st, b), hq(j)]))
            return tuple(out)

        accs = lax.fori_loop(0, i, lambda c, a: chunk(c, a, False), tuple(jnp.zeros((b, dq), f32) for _ in range(hp)))
        for j, acc in enumerate(chunk(i, accs, True)):
            dq_ref[:, hq(j)] = acc * scale

    rowq = pl.BlockSpec((b, hp * HEAD_DIM), lambda g, i: (i, g))
    in_specs = [pl.BlockSpec((b, hp * dq), lambda g, i: (i, qcol // hp + g)),
                pl.BlockSpec((s, hp * dq), lambda g, i: (0, kcol // hp + g)),
                pl.BlockSpec((s, hp * HEAD_DIM), lambda g, i: (0, vcol // hp + g)), rowq, rowq,
                pl.BlockSpec((hp, b, 1), lambda g, i: (g, i, 0))]
    args = [q, k, v, do, o, lse]
    out_shape = [_SDS((s, N_HEADS * dq), f32), _SDS((s, N_HEADS * dq), f32), _SDS((s, N_HEADS * HEAD_DIM), f32)]
    out_specs = [pl.BlockSpec((b, hp * dq), lambda g, i: (i, g)), pl.BlockSpec((s, hp * dq), lambda g, i: (0, g)),
                 pl.BlockSpec((s, hp * HEAD_DIM), lambda g, i: (0, g))]
    if has_bias:
        in_specs += [pl.BlockSpec((b, LANES), lambda g, i: (i, 0)),
                     pl.BlockSpec((hp, nq, 1, b), lambda g, i: (g, 0, 0, 0))]
        args += [cum, cum_t]
        out_shape.append(_SDS((N_HEADS, nq, 1, b), f32))
        out_specs.append(pl.BlockSpec((hp, nq, 1, b), lambda g, i: (g, 0, 0, 0)))
    return pl.pallas_call(
        body, out_shape=tuple(out_shape), grid=(N_HEADS // hp, nq), in_specs=in_specs, out_specs=tuple(out_specs),
        scratch_shapes=[pltpu.VMEM((hp, nq, b, b), f32)] * 2 if has_bias else [],
        compiler_params=_cp(("parallel", "arbitrary")), name=name,
    )(*args)


def _tri(b, kind):
    r, c = _iota((b, b), 0), _iota((b, b), 1)
    cond = {"row_gt": r > c, "row_lt": r < c, "row_ge": r >= c, "row_le": r <= c}[kind]
    return jnp.where(cond, 1.0, 0.0).astype(_MXU)


def _log_keep(z):
    return -(jnp.maximum(z, 0.0) + jnp.log1p(jnp.exp(-jnp.abs(z))))


def _sb_fwd(z_all, *, hp, name):
    s = z_all.shape[0]
    b = _attn_block(s)
    nq = s // b
    scale = HEAD_DIM ** -0.5
    qcol, kcol, vcol = (_AL[n] // (hp * HEAD_DIM) for n in ("sb_q", "sb_k", "sb_v"))
    hd = lambda j: slice(j * HEAD_DIM, (j + 1) * HEAD_DIM)

    def body(q_ref, k_ref, v_ref, o_ref):
        i = pl.program_id(1)
        qs = [q_ref[:, hd(j)].astype(_MXU) for j in range(hp)]
        upper = _tri(b, "row_gt")

        def chunk(c, carry, diag):
            st = pl.multiple_of(c * b, b)
            mask = _lower_mask(b, True) if diag else None
            out = []
            for j in range(hp):
                rsum, acc = carry[j]
                z = _mm_nt(qs[j], k_ref[pl.ds(st, b), hd(j)]) * scale
                lk = _log_keep(z)
                if diag:
                    lk = jnp.where(mask, lk, 0.0)
                a = z + lk + _mm_split(lk, upper) + rsum
                if diag:
                    a = jnp.where(mask, a, NEG_INF)
                acc = acc + _mm(jnp.exp(a), v_ref[pl.ds(st, b), hd(j)])
                out.append((rsum + jnp.sum(lk, axis=1, keepdims=True), acc))
            return tuple(out)

        init = tuple((jnp.zeros((b, 1), f32), jnp.zeros((b, HEAD_DIM), f32)) for _ in range(hp))
        carry = lax.fori_loop(0, i, lambda t, cr: chunk(i - 1 - t, cr, False), chunk(i, init, True))
        for j in range(hp):
            o_ref[:, hd(j)] = carry[j][1]

    w = hp * HEAD_DIM
    return pl.pallas_call(
        body, out_shape=_SDS((s, GROUP), f32), grid=(N_HEADS // hp, nq),
        in_specs=[pl.BlockSpec((b, w), lambda g, i: (i, qcol + g)), pl.BlockSpec((s, w), lambda g, i: (0, kcol + g)),
                  pl.BlockSpec((s, w), lambda g, i: (0, vcol + g))],
        out_specs=pl.BlockSpec((b, w), lambda g, i: (i, g)),
        compiler_params=_cp(("parallel", "parallel")), name=name,
    )(z_all, z_all, z_all)


def _sb_bwd(z_all, do, *, hp, name):
    s = z_all.shape[0]
    b = _attn_block(s)
    nq = s // b
    scale = HEAD_DIM ** -0.5
    qcol, kcol, vcol = (_AL[n] // (hp * HEAD_DIM) for n in ("sb_q", "sb_k", "sb_v"))
    hd = lambda j: slice(j * HEAD_DIM, (j + 1) * HEAD_DIM)

    def body(q_ref, k_ref, v_ref, do_ref, dq_ref, dk_ref, dv_ref, z_sc, lk_sc, r_sc):
        i = pl.program_id(1)

        @pl.when(i == 0)
        def _():
            dk_ref[...] = jnp.zeros_like(dk_ref)
            dv_ref[...] = jnp.zeros_like(dv_ref)

        qs = [q_ref[:, hd(j)].astype(_MXU) for j in range(hp)]
        dos = [do_ref[:, hd(j)].astype(_MXU) for j in range(hp)]
        upper = _tri(b, "row_gt")
        lower = _tri(b, "row_lt")

        def scores(c, rsums, diag):
            st = pl.multiple_of(c * b, b)
            out = []
            for j in range(hp):
                z = _mm_nt(qs[j], k_ref[pl.ds(st, b), hd(j)]) * scale
                lk = _log_keep(z)
                if diag:
                    lk = jnp.where(_lower_mask(b, True), lk, 0.0)
                z_sc[j, c] = z
                lk_sc[j, c] = lk
                r_sc[j, c] = _mm_split(lk, upper) + rsums[j]
                out.append(rsums[j] + jnp.sum(lk, axis=1, keepdims=True))
            return tuple(out)

        rsums = scores(i, tuple(jnp.zeros((b, 1), f32) for _ in range(hp)), True)
        lax.fori_loop(0, i, lambda t, r: scores(i - 1 - t, r, False), rsums)

        def grads(c, carry, diag):
            st = pl.multiple_of(c * b, b)
            mask = _lower_mask(b, True) if diag else None
            out = []
            for j in range(hp):
                psum, dq_acc = carry[j]
                z, lk = z_sc[j, c], lk_sc[j, c]
                lb = z + lk
                a = lb + r_sc[j, c]
                if diag:
                    a = jnp.where(mask, a, NEG_INF)
                w = jnp.exp(a)
                e = _mm_nt(dos[j], v_ref[pl.ds(st, b), hd(j)]) * w
                before = _mm_split(e, lower) + psum
                dz = e * jnp.exp(lk) - before * jnp.exp(lb)
                if diag:
                    dz = jnp.where(mask, dz, 0.0)
                dk_ref[pl.ds(st, b), hd(j)] += _mm_tn(dz, qs[j]) * scale
                dv_ref[pl.ds(st, b), hd(j)] += _mm_tn(w, dos[j])
                out.append((psum + jnp.sum(e, axis=1, keepdims=True), dq_acc + _mm(dz, k_ref[pl.ds(st, b), hd(j)])))
            return tuple(out)

        init = tuple((jnp.zeros((b, 1), f32), jnp.zeros((b, HEAD_DIM), f32)) for _ in range(hp))
        carry = grads(i, lax.fori_loop(0, i, lambda c, cr: grads(c, cr, False), init), True)
        for j in range(hp):
            dq_ref[:, hd(j)] = carry[j][1] * scale

    w = hp * HEAD_DIM
    blk = pl.BlockSpec((b, w), lambda g, i: (i, g))
    full = pl.BlockSpec((s, w), lambda g, i: (0, g))
    return pl.pallas_call(
        body, out_shape=tuple(_SDS((s, GROUP), f32) for _ in range(3)), grid=(N_HEADS // hp, nq),
        in_specs=[pl.BlockSpec((b, w), lambda g, i: (i, qcol + g)), pl.BlockSpec((s, w), lambda g, i: (0, kcol + g)),
                  pl.BlockSpec((s, w), lambda g, i: (0, vcol + g)), blk],
        out_specs=(blk, full, full),
        scratch_shapes=[pltpu.VMEM((hp, nq, b, b), f32)] * 3,
        compiler_params=_cp(("parallel", "arbitrary")), name=name,
    )(z_all, z_all, z_all, do)


def _split3_left(t, x):
    hi = x.astype(_MXU)
    r1 = x - hi.astype(f32)
    mid = r1.astype(_MXU)
    lo = (r1 - mid.astype(f32)).astype(_MXU)
    dot = functools.partial(jnp.dot, preferred_element_type=f32)
    return dot(t, hi) + dot(t, mid) + dot(t, lo)


def _split3_right(x, t):
    hi = x.astype(_MXU)
    r1 = x - hi.astype(f32)
    mid = r1.astype(_MXU)
    lo = (r1 - mid.astype(f32)).astype(_MXU)
    dot = functools.partial(jnp.dot, preferred_element_type=f32)
    return dot(hi, t) + dot(mid, t) + dot(lo, t)


def _fox_cum_fwd(z_all, bias, *, name):
    s = z_all.shape[0]
    b = _attn_block(s)
    fcol = _AL["fox_f"] // LANES

    def body(f_ref, b_ref, cum_ref, cumt_ref, carry_ref):
        i = pl.program_id(0)

        @pl.when(i == 0)
        def _():
            carry_ref[...] = jnp.zeros_like(carry_ref)

        u = f_ref[...] + b_ref[...]
        lf = jnp.minimum(u, 0.0) - jnp.log1p(jnp.exp(-jnp.abs(u)))
        cum = _split3_left(_tri(b, "row_ge"), lf) + carry_ref[...]
        cum_ref[...] = cum
        cumt_ref[...] = cum.T[0:8, :]
        carry_ref[...] = cum_ref[b - 1:b, :]

    return pl.pallas_call(
        body, out_shape=(_SDS((s, LANES), f32), _SDS((8, s), f32)), grid=(s // b,),
        in_specs=[pl.BlockSpec((b, LANES), lambda i: (i, fcol)), pl.BlockSpec((1, LANES), lambda i: (0, 0))],
        out_specs=(pl.BlockSpec((b, LANES), lambda i: (i, 0)), pl.BlockSpec((8, b), lambda i: (0, i))),
        scratch_shapes=[pltpu.VMEM((1, LANES), f32)], compiler_params=_cp(("arbitrary",)), name=name,
    )(z_all, bias)


def _fox_cum_bwd(z_all, bias, dcum_t, *, name):
    s = z_all.shape[0]
    b = _attn_block(s)
    nb = s // b
    fcol = _AL["fox_f"] // LANES

    def body(f_ref, b_ref, dc_ref, df_ref, db_ref, carry_ref):
        i = pl.program_id(0)

        @pl.when(i == 0)
        def _():
            carry_ref[...] = jnp.zeros_like(carry_ref)
            db_ref[...] = jnp.zeros_like(db_ref)

        dc = dc_ref[...]
        rev = _split3_right(dc, _tri(b, "row_ge")) + carry_ref[...]
        carry_ref[...] = carry_ref[...] + jnp.sum(dc, axis=1, keepdims=True)
        dlf = jnp.concatenate([rev, jnp.zeros((LANES - 8, b), f32)], axis=0).T
        u = f_ref[...] + b_ref[...]
        df = jnp.where(_iota((b, LANES), 1) < N_HEADS, dlf * (1.0 - _sigmoid(u)), 0.0)
        df_ref[...] = df
        db_ref[...] += jnp.sum(df, axis=0, keepdims=True)

    return pl.pallas_call(
        body, out_shape=(_SDS((s, LANES), f32), _SDS((1, LANES), f32)), grid=(nb,),
        in_specs=[pl.BlockSpec((b, LANES), lambda i: (nb - 1 - i, fcol)), pl.BlockSpec((1, LANES), lambda i: (0, 0)),
                  pl.BlockSpec((8, b), lambda i: (0, nb - 1 - i))],
        out_specs=(pl.BlockSpec((b, LANES), lambda i: (nb - 1 - i, 0)), pl.BlockSpec((1, LANES), lambda i: (0, 0))),
        scratch_shapes=[pltpu.VMEM((8, 1), f32)], compiler_params=_cp(("arbitrary",)), name=name,
    )(z_all, bias, dcum_t)


MLA_QW = 2 * LANES


def _rms_rows(x):
    r = lax.rsqrt(jnp.mean(x * x, axis=-1, keepdims=True) + RMS_EPS)
    return x * r, r


def _mla_prep_fwd(z_all, gq, gkv, wuq, wk, wv, tables, *, name):
    s = z_all.shape[0]
    rb = _row_block(s)
    half = MLA_ROPE // 2

    def body(cq_ref, ckv_ref, kr_ref, gq_ref, gkv_ref, wuq_ref, wk_ref, wv_ref, cos_ref, sa_ref, sb_ref,
             q_ref, k_ref, v_ref):
        cos, sa, sb = cos_ref[...], sa_ref[...], sb_ref[...]
        xh, _ = _rms_rows(cq_ref[...])
        qp = _mm(xh * gq_ref[...], wuq_ref[...])
        kh, _ = _rms_rows(ckv_ref[...])
        nkv = kh * gkv_ref[...]
        kn = _mm(nkv, wk_ref[...])
        v_ref[...] = _mm(nkv, wv_ref[...])
        kr = _rope(kr_ref[...], cos, sa, sb, half)
        for h in range(N_HEADS):
            lo, mid, hi = h * MLA_QW, h * MLA_QW + LANES, (h + 1) * MLA_QW
            q_ref[:, lo:mid] = qp[:, lo:mid]
            q_ref[:, mid:hi] = _rope(qp[:, mid:hi], cos, sa, sb, half)
            k_ref[:, lo:mid] = kn[:, h * LANES:(h + 1) * LANES]
            k_ref[:, mid:hi] = kr

    row = lambda w, cb: pl.BlockSpec((rb, w), lambda i: (i, cb))
    whole = lambda a: pl.BlockSpec(a.shape, lambda i: (0,) * a.ndim)
    return pl.pallas_call(
        body, out_shape=(_SDS((s, N_HEADS * MLA_QW), f32), _SDS((s, N_HEADS * MLA_QW), f32), _SDS((s, GROUP), f32)),
        grid=(s // rb,),
        in_specs=[row(MLA_Q_RANK, _AL["mla_cq"] // MLA_Q_RANK), row(LANES, _AL["mla_ckv"] // LANES),
                  row(LANES, _AL["mla_k_rope"] // LANES), whole(gq), whole(gkv), whole(wuq), whole(wk), whole(wv),
                  row(LANES, 0), row(LANES, 0), row(LANES, 0)],
        out_specs=(row(N_HEADS * MLA_QW, 0), row(N_HEADS * MLA_QW, 0), row(GROUP, 0)),
        compiler_params=_cp(("parallel",)), name=name,
    )(z_all, z_all, z_all, gq, gkv, wuq, wk, wv, *tables)


def _mla_prep_bwd(z_all, gq, gkv, wuq, wk, wv, tables, dq_cat, dk_cat, dv, *, name):
    s = z_all.shape[0]
    rb = _row_block(s)
    half = MLA_ROPE // 2

    def body(cq_ref, ckv_ref, gq_ref, gkv_ref, wuq_ref, wk_ref, wv_ref, cos_ref, sa_ref, sb_ref, dq_ref, dk_ref,
             dv_ref, dcq_ref, dckv_ref, dkr_ref, dwuq_ref, dwk_ref, dwv_ref, dgq_ref, dgkv_ref):
        i = pl.program_id(0)

        @pl.when(i == 0)
        def _():
            for r in (dwuq_ref, dwk_ref, dwv_ref, dgq_ref, dgkv_ref):
                r[...] = jnp.zeros_like(r)

        cos, sa, sb = cos_ref[...], sa_ref[...], sb_ref[...]
        parts, knp = [], []
        dkr = jnp.zeros((rb, LANES), f32)
        for h in range(N_HEADS):
            lo, mid, hi = h * MLA_QW, h * MLA_QW + LANES, (h + 1) * MLA_QW
            parts += [dq_ref[:, lo:mid], _rope(dq_ref[:, mid:hi], cos, sa, sb, half, transpose=True)]
            knp.append(dk_ref[:, lo:mid])
            dkr = dkr + _rope(dk_ref[:, mid:hi], cos, sa, sb, half, transpose=True)
        dkr_ref[...] = dkr
        dqp = jnp.concatenate(parts, axis=1)
        dkn = jnp.concatenate(knp, axis=1)
        dvv = dv_ref[...]

        def norm_bwd(x_ref, g_ref, w_pairs, dx_ref, dg_ref):
            xh, r = _rms_rows(x_ref[...])
            nx = xh * g_ref[...]
            dn = jnp.zeros_like(xh)
            for w_ref, dw_ref, dy in w_pairs:
                dw_ref[...] += _mm_tn(nx, dy)
                dn = dn + _mm_nt(dy, w_ref[...])
            dxh = dn * g_ref[...]
            dx_ref[...] = r * (dxh - xh * jnp.mean(dxh * xh, axis=-1, keepdims=True))
            dg_ref[...] += jnp.sum(dn * xh, axis=0, keepdims=True)

        norm_bwd(cq_ref, gq_ref, [(wuq_ref, dwuq_ref, dqp)], dcq_ref, dgq_ref)
        norm_bwd(ckv_ref, gkv_ref, [(wk_ref, dwk_ref, dkn), (wv_ref, dwv_ref, dvv)], dckv_ref, dgkv_ref)

    row = lambda w, cb: pl.BlockSpec((rb, w), lambda i: (i, cb))
    whole = lambda a: pl.BlockSpec(a.shape, lambda i: (0,) * a.ndim)
    return pl.pallas_call(
        body,
        out_shape=(_SDS((s, MLA_Q_RANK), f32), _SDS((s, LANES), f32), _SDS((s, LANES), f32), _SDS(wuq.shape, f32),
                   _SDS(wk.shape, f32), _SDS(wv.shape, f32), _SDS(gq.shape, f32), _SDS(gkv.shape, f32)),
        grid=(s // rb,),
        in_specs=[row(MLA_Q_RANK, _AL["mla_cq"] // MLA_Q_RANK), row(LANES, _AL["mla_ckv"] // LANES), whole(gq),
                  whole(gkv), whole(wuq), whole(wk), whole(wv), row(LANES, 0), row(LANES, 0), row(LANES, 0),
                  row(N_HEADS * MLA_QW, 0), row(N_HEADS * MLA_QW, 0), row(GROUP, 0)],
        out_specs=(row(MLA_Q_RANK, 0), row(LANES, 0), row(LANES, 0), whole(wuq), whole(wk), whole(wv), whole(gq),
                   whole(gkv)),
        compiler_params=_cp(("arbitrary",)), name=name,
    )(z_all, z_all, gq, gkv, wuq, wk, wv, *tables, dq_cat, dk_cat, dv)


def _silu_grad(x):
    sg = _sigmoid(x)
    return sg * (1.0 + x * (1.0 - sg))


def _nsa_cmp_fwd(ra, rb_, pos, w1, w2, tables, *, name):
    nr = ra.shape[1]
    hw = ra.shape[2]

    def body(ra_ref, rb_ref, pos_ref, w1_ref, w2_ref, cos_ref, sa_ref, sb_ref, out_ref, hp_ref):
        for k in range(2):
            xa = ra_ref[k] + pos_ref[k, :, 0:hw]
            xb = rb_ref[k] + pos_ref[k, :, hw:2 * hw]
            hp = _mm(xa, w1_ref[k, 0:hw, :]) + _mm(xb, w1_ref[k, hw:2 * hw, :])
            hp_ref[k] = hp
            out = _mm(hp * _sigmoid(hp), w2_ref[k])
            if k == 0:
                out = _rope(out, cos_ref[...], sa_ref[...], sb_ref[...], HEAD_DIM // 2)
            out_ref[k] = out

    return pl.pallas_call(body, out_shape=(_SDS((2, nr, HEAD_DIM), f32), _SDS((2, nr, HEAD_DIM), f32)),
                          compiler_params=_cp(), name=name)(ra, rb_, pos, w1, w2, *tables)


def _nsa_cmp_bwd(ra, rb_, pos, w1, w2, tables, hp, dout, *, name):
    nr = ra.shape[1]
    hw = ra.shape[2]

    def body(ra_ref, rb_ref, pos_ref, w1_ref, w2_ref, cos_ref, sa_ref, sb_ref, hp_ref, do_ref,
             dxa_ref, dxb_ref, dw1_ref, dw2_ref):
        for k in range(2):
            d_out = do_ref[k]
            if k == 0:
                d_out = _rope(d_out, cos_ref[...], sa_ref[...], sb_ref[...], HEAD_DIM // 2, transpose=True)
            hpv = hp_ref[k]
            dw2_ref[k] = _mm_tn(hpv * _sigmoid(hpv), d_out)
            dhp = _mm_nt(d_out, w2_ref[k]) * _silu_grad(hpv)
            xa = ra_ref[k] + pos_ref[k, :, 0:hw]
            xb = rb_ref[k] + pos_ref[k, :, hw:2 * hw]
            dw1_ref[k, 0:hw, :] = _mm_tn(xa, dhp)
            dw1_ref[k, hw:2 * hw, :] = _mm_tn(xb, dhp)
            dxa_ref[k] = _mm_nt(dhp, w1_ref[k, 0:hw, :])
            dxb_ref[k] = _mm_nt(dhp, w1_ref[k, hw:2 * hw, :])

    return pl.pallas_call(
        body, out_shape=(_SDS((2, nr, hw), f32), _SDS((2, nr, hw), f32), _SDS(w1.shape, f32), _SDS(w2.shape, f32)),
        compiler_params=_cp(), name=name)(ra, rb_, pos, w1, w2, *tables, hp, dout)


def _nsa_consts(s):
    b = _attn_block(s)
    nr = s // CMP_STRIDE
    n_cmp = (s - CMP_LEN) // CMP_STRIDE + 1
    n_sel = s // SEL_LEN
    cmp_start = np.arange(n_cmp) * CMP_STRIDE
    sel_start = np.arange(n_sel) * SEL_LEN
    overlap = np.clip(np.minimum(cmp_start[:, None] + CMP_LEN, sel_start[None, :] + SEL_LEN)
                      - np.maximum(cmp_start[:, None], sel_start[None, :]), 0, None)
    m2s = np.zeros((nr, LANES), np.float32)
    m2s[:n_cmp, :n_sel] = overlap / CMP_LEN
    e3 = np.zeros((s // b, LANES, b), np.float32)
    tok = np.arange(s)
    e3[tok // b, tok // SEL_LEN, tok % b] = 1.0
    return jnp.asarray(m2s, _MXU), jnp.asarray(e3, _MXU)


def _nsa_masks(i, b, d):
    qpos = i * b + _iota((b, b), 0)
    kpos = (i - d) * b + _iota((b, b), 1)
    return (kpos <= qpos) & (kpos > qpos - WINDOW)


def _nsa_fwd(qr, kvc, ksr, vs, kwr, vw, z_all, m2s, e3, *, name):
    s = qr.shape[0]
    b = _attn_block(s)
    nq = s // b
    nr = kvc.shape[1]
    n_sel = s // SEL_LEN
    top_n = min(SEL_TOPN, n_sel)
    nd = -(-WINDOW // b)
    scale = HEAD_DIM ** -0.5
    bcol = _AL["nsa_branch"] // LANES
    H = N_HEADS

    def body(q_ref, kvc_ref, ks_ref, vs_ref, kw_ref, vw_ref, br_ref, m2s_ref, e3_ref,
             o_ref, oc_ref, os_ref, ow_ref, st_ref, sel_ref, m_sc, l_sc, acc_sc):
        i = pl.program_id(0)
        lane = _iota((b, LANES), 1)
        hs = lambda h: slice(h * HEAD_DIM, (h + 1) * HEAD_DIM)

        cmp_mask = (CMP_STRIDE * _iota((b, nr), 1) + (CMP_LEN - 1)) <= (i * b + _iota((b, nr), 0))
        imp = jnp.zeros((b, LANES), f32)
        stats = jnp.zeros((b, LANES), f32)
        for h in range(H):
            zc = jnp.where(cmp_mask, _mm_nt(q_ref[:, hs(h)], kvc_ref[0]) * scale, NEG_INF)
            m = jnp.max(zc, axis=1, keepdims=True)
            p = jnp.where(cmp_mask, jnp.exp(zc - m), 0.0)
            l = jnp.sum(p, axis=1, keepdims=True)
            some = l > 0.0
            lsafe = jnp.where(some, l, 1.0)
            pc = p * jnp.where(some, 1.0 / lsafe, 0.0)
            oc_ref[:, hs(h)] = _mm(pc, kvc_ref[1])
            imp = imp + _mm(pc, m2s_ref[...])
            stats = jnp.where(lane == h, jnp.where(some, m + jnp.log(lsafe), 0.0), stats)

        cur = jnp.right_shift(i * b + _iota((b, LANES), 0), int(math.log2(SEL_LEN)))
        forced = (lane == 0) | (lane == cur) | (lane == cur - 1)
        score = jnp.where(lane <= cur, jnp.where(forced, FORCED_BONUS, imp), NEG_INF)
        score = jnp.where(lane < n_sel, score, -3e38)
        rank = jnp.zeros((b, LANES), f32)
        for j in range(n_sel):
            col = score[:, j:j + 1]
            rank = rank + jnp.where(col > score, 1.0, jnp.where(col == score, jnp.where(lane > j, 1.0, 0.0), 0.0))
        sel = jnp.where(lane < n_sel, jnp.where(rank < top_n, 1.0, 0.0), 0.0)
        sel_ref[...] = sel
        sel_b = sel.astype(_MXU)

        def reset():
            m_sc[...] = jnp.full(m_sc.shape, NEG_INF, f32)
            l_sc[...] = jnp.zeros_like(l_sc)
            acc_sc[...] = jnp.zeros_like(acc_sc)

        def update(h, z, mask, vch):
            zm = jnp.where(mask, z, NEG_INF)
            m_old = m_sc[h]
            m_new = jnp.maximum(m_old, jnp.max(zm, axis=1, keepdims=True))
            p = jnp.where(mask, jnp.exp(zm - m_new), 0.0)
            alpha = jnp.exp(m_old - m_new)
            l_sc[h] = alpha * l_sc[h] + jnp.sum(p, axis=1, keepdims=True)
            acc_sc[h] = alpha * acc_sc[h] + _mm(p, vch)
            m_sc[h] = m_new

        def finish(out_ref, branch, stats):
            for h in range(H):
                out_ref[:, hs(h)] = acc_sc[h] / l_sc[h]
                stats = jnp.where(lane == 4 * branch + h, m_sc[h] + jnp.log(l_sc[h]), stats)
            return stats

        def sel_chunk(c, diag):
            st = pl.multiple_of(c * b, b)
            mask = _mm(sel_b, e3_ref[c]) > 0.5
            if diag:
                mask = mask & _lower_mask(b, False)
            kch, vch = ks_ref[pl.ds(st, b), :], vs_ref[pl.ds(st, b), :]
            for h in range(H):
                update(h, _mm_nt(q_ref[:, hs(h)], kch) * scale, mask, vch)

        reset()

        def sel_loop(c, carry):
            sel_chunk(c, False)
            return carry

        lax.fori_loop(0, i, sel_loop, 0)
        sel_chunk(i, True)
        stats = finish(os_ref, 1, stats)

        reset()
        for d in range(nd, -1, -1):
            @pl.when(i >= d)
            def _():
                st = pl.multiple_of((i - d) * b, b)
                mask = _nsa_masks(i, b, d)
                kch, vch = kw_ref[pl.ds(st, b), :], vw_ref[pl.ds(st, b), :]
                for h in range(H):
                    update(h, _mm_nt(q_ref[:, hs(h)], kch) * scale, mask, vch)
        stats = finish(ow_ref, 2, stats)
        st_ref[...] = stats

        g = _sigmoid(br_ref[...])
        for h in range(H):
            o_ref[:, hs(h)] = (g[:, 3 * h:3 * h + 1] * oc_ref[:, hs(h)] + g[:, 3 * h + 1:3 * h + 2] * os_ref[:, hs(h)]
                               + g[:, 3 * h + 2:3 * h + 3] * ow_ref[:, hs(h)])

    blk = lambda w: pl.BlockSpec((b, w), lambda i: (i, 0))
    whole = lambda a: pl.BlockSpec(a.shape, lambda i: (0,) * a.ndim)
    return pl.pallas_call(
        body, out_shape=tuple(_SDS((s, GROUP), f32) for _ in range(4)) + (_SDS((s, LANES), f32), _SDS((s, LANES), f32)),
        grid=(nq,),
        in_specs=[blk(GROUP), whole(kvc), whole(ksr), whole(vs), whole(kwr), whole(vw),
                  pl.BlockSpec((b, LANES), lambda i: (i, bcol)), whole(m2s), whole(e3)],
        out_specs=(blk(GROUP),) * 4 + (blk(LANES), blk(LANES)),
        scratch_shapes=[pltpu.VMEM((H, b, 1), f32), pltpu.VMEM((H, b, 1), f32), pltpu.VMEM((H, b, HEAD_DIM), f32)],
        compiler_params=_cp(("parallel",)), name=name,
    )(qr, kvc, ksr, vs, kwr, vw, z_all, m2s, e3)


def _nsa_bwd(do, qr, kvc, ksr, vs, kwr, vw, z_all, oc, os_, ow, stats, sel, e3, *, name):
    s = qr.shape[0]
    b = _attn_block(s)
    nq = s // b
    nr = kvc.shape[1]
    nd = -(-WINDOW // b)
    scale = HEAD_DIM ** -0.5
    bcol = _AL["nsa_branch"] // LANES
    H = N_HEADS

    def body(do_ref, q_ref, kvc_ref, ks_ref, vs_ref, kw_ref, vw_ref, br_ref, oc_ref, os_ref, ow_ref, st_ref, sel_ref,
             e3_ref, dq_ref, dbr_ref, dkvc_ref, dks_ref, dvs_ref, dkw_ref, dvw_ref, dob_sc, delta_sc, dq_sc):
        i = pl.program_id(0)

        @pl.when(i == 0)
        def _():
            for r in (dkvc_ref, dks_ref, dvs_ref, dkw_ref, dvw_ref):
                r[...] = jnp.zeros_like(r)

        lane = _iota((b, LANES), 1)
        hs = lambda h: slice(h * HEAD_DIM, (h + 1) * HEAD_DIM)
        g = _sigmoid(br_ref[...])
        stats = st_ref[...]
        dbr = jnp.zeros((b, LANES), f32)
        outs = (oc_ref, os_ref, ow_ref)
        for h in range(H):
            doh = do_ref[:, hs(h)]
            for j in range(3):
                gj = g[:, 3 * h + j:3 * h + j + 1]
                dgj = jnp.sum(doh * outs[j][:, hs(h)], axis=1, keepdims=True)
                dbr = jnp.where(lane == 3 * h + j, dgj * gj * (1.0 - gj), dbr)
                dob_sc[j, :, hs(h)] = gj * doh
                delta_sc[j, h] = gj * dgj
        dbr_ref[...] = dbr
        dq_sc[...] = jnp.zeros_like(dq_sc)

        def branch(j, h, z, mask, kch, vch):
            qh = q_ref[:, hs(h)]
            p = jnp.where(mask, jnp.exp(jnp.where(mask, z, NEG_INF) - stats[:, 4 * j + h:4 * j + h + 1]), 0.0)
            dob = dob_sc[j, :, hs(h)]
            ds = p * (_mm_nt(dob, vch) - delta_sc[j, h])
            dq_sc[:, hs(h)] += _mm(ds, kch) * scale
            return _mm_tn(ds, qh) * scale, _mm_tn(p, dob)

        cmp_mask = (CMP_STRIDE * _iota((b, nr), 1) + (CMP_LEN - 1)) <= (i * b + _iota((b, nr), 0))
        kc, vc = kvc_ref[0], kvc_ref[1]
        for h in range(H):
            dk, dv = branch(0, h, _mm_nt(q_ref[:, hs(h)], kc) * scale, cmp_mask, kc, vc)
            dkvc_ref[0] += dk
            dkvc_ref[1] += dv

        sel_b = sel_ref[...].astype(_MXU)

        def chunk(j, c, mask, k_ref, v_ref, dk_ref, dv_ref):
            st = pl.multiple_of(c * b, b)
            kch, vch = k_ref[pl.ds(st, b), :], v_ref[pl.ds(st, b), :]
            dk = jnp.zeros((b, HEAD_DIM), f32)
            dv = jnp.zeros((b, HEAD_DIM), f32)
            for h in range(H):
                dkh, dvh = branch(j, h, _mm_nt(q_ref[:, hs(h)], kch) * scale, mask, kch, vch)
                dk, dv = dk + dkh, dv + dvh
            dk_ref[pl.ds(st, b), :] += dk
            dv_ref[pl.ds(st, b), :] += dv

        def sel_chunk(c, diag):
            mask = _mm(sel_b, e3_ref[c]) > 0.5
            if diag:
                mask = mask & _lower_mask(b, False)
            chunk(1, c, mask, ks_ref, vs_ref, dks_ref, dvs_ref)

        def sel_loop(c, carry):
            sel_chunk(c, False)
            return carry

        lax.fori_loop(0, i, sel_loop, 0)
        sel_chunk(i, True)

        for d in range(nd, -1, -1):
            @pl.when(i >= d)
            def _():
                chunk(2, i - d, _nsa_masks(i, b, d), kw_ref, vw_ref, dkw_ref, dvw_ref)

        dq_ref[...] = dq_sc[...]

    blk = lambda w: pl.BlockSpec((b, w), lambda i: (i, 0))
    whole = lambda a: pl.BlockSpec(a.shape, lambda i: (0,) * a.ndim)
    stream = _SDS((s, HEAD_DIM), f32)
    return pl.pallas_call(
        body, out_shape=(_SDS((s, GROUP), f32), _SDS((s, LANES), f32), _SDS(kvc.shape, f32), stream, stream, stream,
                         stream),
        grid=(nq,),
        in_specs=[blk(GROUP), blk(GROUP), whole(kvc), whole(ksr), whole(vs), whole(kwr), whole(vw),
                  pl.BlockSpec((b, LANES), lambda i: (i, bcol)), blk(GROUP), blk(GROUP), blk(GROUP), blk(LANES),
                  blk(LANES), whole(e3)],
        out_specs=(blk(GROUP), blk(LANES), whole(kvc), whole(ksr), whole(vs), whole(kwr), whole(vw)),
        scratch_shapes=[pltpu.VMEM((3, b, GROUP), f32), pltpu.VMEM((3, H, b, 1), f32), pltpu.VMEM((b, GROUP), f32)],
        compiler_params=_cp(("arbitrary",)), name=name,
    )(do, qr, kvc, ksr, vs, kwr, vw, z_all, oc, os_, ow, stats, sel, e3)


def _seg(a, name):
    parts = [lax.slice_in_dim(a, off, off + hi - lo, axis=a.ndim - 1) for off, lo, hi in _PIECES[name]]
    return parts[0] if len(parts) == 1 else jnp.concatenate(parts, axis=a.ndim - 1)


def _to_groups(segs, rows, dtype):
    cols = []
    for s, grp in enumerate(_GROUPS):
        at = 0
        for n, lo, hi, off in sorted(grp, key=lambda t: t[3]):
            if off > at:
                cols.append(jnp.zeros((rows, off - at), dtype))
            cols.append(segs[n][:, lo:hi].astype(dtype))
            at = off + hi - lo
        if at < GROUP_W:
            cols.append(jnp.zeros((rows, GROUP_W - at), dtype))
    return jnp.concatenate(cols, axis=1)


def _piece_from_shard(w_t, s):
    grp = sorted(_GROUPS[s], key=lambda t: t[3])
    ends = [t[3] for t in grp[1:]] + [GROUP_W]
    rows = []
    for (n, lo, hi, off), end in zip(grp, ends):
        first = _ORIG[n] + lo - s * CHIP_COLS
        rows.append(jnp.pad(w_t[:, first:first + hi - lo], ((0, 0), (0, end - off - (hi - lo)), (0, 0))))
    return jnp.concatenate(rows, axis=1)


def _shard_from_piece(g, s):
    return jnp.concatenate([g[:, off:off + hi - lo] for n, lo, hi, off in
                            sorted(_GROUPS[s], key=lambda t: _ORIG[t[0]] + t[1])], axis=1)


def _from_groups(a):
    return jnp.concatenate([_seg(a, n) for n, _ in _SEGS], axis=1)


def _cmp_rows(tok):
    s = tok.shape[0]
    r = tok.reshape(s // CMP_STRIDE, CMP_STRIDE * HEAD_DIM)
    return r, jnp.concatenate([r[1:], jnp.zeros((1, r.shape[1]), r.dtype)], axis=0)


def _cmp_unrows(dxa, dxb):
    s = dxa.shape[0] * CMP_STRIDE
    return (dxa + jnp.concatenate([jnp.zeros((1, dxa.shape[1]), dxa.dtype), dxb[:-1]], axis=0)).reshape(s, HEAD_DIM)


_GATES = ("sb_gate", "nsa_gate", "fox_gate", "mla_gate")


def _layer_fwd(x, p, c, tag):
    s = x.shape[0]
    b = _attn_block(s)
    h = _rms_fwd(x, p["pre_g"], out_dtype=_MXU, name=f"prenorm_{tag}")
    z = _matmul(h, p["w_in"], "nt", bias=p["b_in"], name=f"inproj_{tag}")
    o_sb = _sb_fwd(z, hp=HP_FWD, name=f"sb_fwd_{tag}")

    qr, ksr, kwr = _rope_call([(z, GROUP, _AL["nsa_q"] // GROUP), (z, LANES, _AL["nsa_k_sel"] // LANES),
                               (z, LANES, _AL["nsa_k_win"] // LANES)], c["tabs128"], HEAD_DIM // 2, False,
                              name=f"nsa_rope_{tag}")
    (rak, rbk), (rav, rbv) = _cmp_rows(_seg(z, "nsa_k_cmp")), _cmp_rows(_seg(z, "nsa_v_cmp"))
    ra, rb_ = jnp.stack([rak, rav]), jnp.stack([rbk, rbv])
    kvc, hp = _nsa_cmp_fwd(ra, rb_, p["cmp_pos"], p["cmp_w1"], p["cmp_w2"], c["tabs_cmp"], name=f"nsa_cmp_{tag}")
    vs, vw = _seg(z, "nsa_v_sel"), _seg(z, "nsa_v_win")
    o_nsa, oc, os_, ow, stats, sel = _nsa_fwd(qr, kvc, ksr, vs, kwr, vw, z, c["m2s"], c["e3"], name=f"nsa_fwd_{tag}")

    cum, cum_t8 = _fox_cum_fwd(z, p["fox_bias"], name=f"fox_cum_{tag}")
    cum_t = cum_t8.reshape(8, s // b, 1, b)
    fox_v = _seg(z, "fox_v")
    fcols = (_AL["fox_q"] // HEAD_DIM, _AL["fox_k"] // HEAD_DIM, 0)
    o_fox, lse_fox = _attn_fwd(z, z, fox_v, *fcols, HEAD_DIM, cum, cum_t, scale=HEAD_DIM ** -0.5, hp=HP_FWD,
                               name=f"fox_fwd_{tag}")

    qcat, kcat, vm = _mla_prep_fwd(z, p["gq"], p["gkv"], p["wuq"], p["wk"], p["wv"], c["tabs64"],
                                   name=f"mla_prep_{tag}")
    o_mla, lse_mla = _attn_fwd(qcat, kcat, vm, 0, 0, 0, MLA_QW, None, None, scale=(MLA_NOPE + MLA_ROPE) ** -0.5,
                               hp=HP_BWD, name=f"mla_fwd_{tag}")

    o_all = jnp.concatenate([o_sb, o_nsa, o_fox, o_mla], axis=1)
    gates = jnp.concatenate([_seg(z, n) for n in _GATES], axis=1)
    mix = _gate_fwd(o_all, gates, name=f"gate_{tag}")
    u = _matmul(mix, p["w_out"], "nn", name=f"outproj_{tag}")
    y = _postnorm_fwd(u, p["post_g"], x, name=f"postnorm_{tag}")
    saved = dict(x=x, h=h, z=z, qr=qr, ksr=ksr, kwr=kwr, ra=ra, rb=rb_, kvc=kvc, hp=hp, vs=vs, vw=vw, oc=oc, os=os_,
                 ow=ow, stats=stats, sel=sel, cum=cum, cum_t=cum_t, fox_v=fox_v, o_fox=o_fox, lse_fox=lse_fox, qcat=qcat, kcat=kcat,
                 vm=vm, o_mla=o_mla, lse_mla=lse_mla, o_all=o_all, gates=gates, mix=mix, u=u)
    return y, saved


def _layer_bwd(dy, sv, p, c, tag):
    z = sv["z"]
    s = z.shape[0]
    du, dg_post = _rms_bwd(dy, sv["u"], p["post_g"], name=f"postnorm_bwd_{tag}")
    dmix = _matmul(du, p["w_out"], "nt", name=f"outproj_dx_{tag}")
    dw_out = _matmul(sv["mix"], du, "tn", name=f"outproj_dw_{tag}")
    do_all, dgates = _gate_bwd(dmix, sv["o_all"], sv["gates"], name=f"gate_bwd_{tag}")
    do_sb, do_nsa, do_fox, do_mla = (do_all[:, k * GROUP:(k + 1) * GROUP] for k in range(4))
    dgate = [dgates[:, k * GROUP:(k + 1) * GROUP] for k in range(4)]

    sb_dq, sb_dk, sb_dv = _sb_bwd(z, do_sb, hp=HP_BWD, name=f"sb_bwd_{tag}")

    n_dq, n_dbr, n_dkvc, n_dks, n_dvs, n_dkw, n_dvw = _nsa_bwd(
        do_nsa, sv["qr"], sv["kvc"], sv["ksr"], sv["vs"], sv["kwr"], sv["vw"], z, sv["oc"], sv["os"], sv["ow"],
        sv["stats"], sv["sel"], c["e3"], name=f"nsa_bwd_{tag}")
    dxa, dxb, dw1, dw2 = _nsa_cmp_bwd(sv["ra"], sv["rb"], p["cmp_pos"], p["cmp_w1"], p["cmp_w2"], c["tabs_cmp"],
                                      sv["hp"], n_dkvc, name=f"nsa_cmp_bwd_{tag}")
    n_dq, n_dks, n_dkw = _rope_call([(n_dq, GROUP, 0), (n_dks, LANES, 0), (n_dkw, LANES, 0)], c["tabs128"],
                                    HEAD_DIM // 2, True, name=f"nsa_rope_bwd_{tag}")
    dpos = _colsum(jnp.concatenate([dxa[0], dxb[0], dxa[1], dxb[1]], axis=1), name=f"nsa_dpos_{tag}")
    flat = CMP_LEN * HEAD_DIM

    fcols = (_AL["fox_q"] // HEAD_DIM, _AL["fox_k"] // HEAD_DIM, 0)
    f_dq, f_dk, f_dv, f_dck = _attn_bwd(z, z, sv["fox_v"], *fcols, HEAD_DIM, do_fox, sv["o_fox"], sv["lse_fox"],
                                        sv["cum"], sv["cum_t"], scale=HEAD_DIM ** -0.5, hp=HP_BWD,
                                        name=f"fox_bwd_{tag}")
    dcum_t = jnp.pad(f_dck.reshape(N_HEADS, s), ((0, 8 - N_HEADS), (0, 0)))
    f_df, f_dbias = _fox_cum_bwd(z, p["fox_bias"], dcum_t, name=f"fox_cum_bwd_{tag}")

    m_dq, m_dk, m_dv = _attn_bwd(sv["qcat"], sv["kcat"], sv["vm"], 0, 0, 0, MLA_QW, do_mla, sv["o_mla"], sv["lse_mla"],
                                 None, None, scale=(MLA_NOPE + MLA_ROPE) ** -0.5, hp=HP_BWD, name=f"mla_bwd_{tag}")
    m_dcq, m_dckv, m_dkr, m_dwuq, m_dwk, m_dwv, m_dgq, m_dgkv = _mla_prep_bwd(
        z, p["gq"], p["gkv"], p["wuq"], p["wk"], p["wv"], c["tabs64"], m_dq, m_dk, m_dv, name=f"mla_prep_bwd_{tag}")

    dz = _to_groups(dict(
        sb_q=sb_dq, sb_k=sb_dk, sb_v=sb_dv, sb_gate=dgate[0], nsa_q=n_dq, nsa_k_cmp=_cmp_unrows(dxa[0], dxb[0]),
        nsa_v_cmp=_cmp_unrows(dxa[1], dxb[1]), nsa_k_sel=n_dks, nsa_v_sel=n_dvs, nsa_k_win=n_dkw, nsa_v_win=n_dvw,
        nsa_branch=n_dbr, nsa_gate=dgate[1], fox_q=f_dq, fox_k=f_dk, fox_v=f_dv, fox_f=f_df, fox_gate=dgate[2],
        mla_cq=m_dcq, mla_ckv=m_dckv, mla_k_rope=m_dkr, mla_gate=dgate[3]), s, _MXU)
    dh = _matmul(dz, p["w_in"], "nn", name=f"inproj_dx_{tag}")
    dw_in = _matmul(dz, sv["h"], "tn", name=f"inproj_dw_{tag}")
    db = _colsum(dz, name=f"inproj_db_{tag}")
    dx, dg_pre = _rms_bwd(dh, sv["x"], p["pre_g"], res=dy, name=f"prenorm_bwd_{tag}")

    qw = MLA_NOPE + MLA_ROPE
    grads = {
        "pre_norm_g": dg_pre[0], "post_norm_g": dg_post[0], "w_in": dw_in, "b_in": _from_groups(db)[0],
        "w_out": dw_out, "fox_forget_bias": f_dbias[0, :N_HEADS],
        "nsa_cmp_pos_k": dpos[0, :flat].reshape(CMP_LEN, HEAD_DIM), "nsa_cmp_w1_k": dw1[0], "nsa_cmp_w2_k": dw2[0],
        "nsa_cmp_pos_v": dpos[0, flat:].reshape(CMP_LEN, HEAD_DIM), "nsa_cmp_w1_v": dw1[1], "nsa_cmp_w2_v": dw2[1],
        "mla_q_norm_g": m_dgq[0],
        "mla_w_uq": jnp.concatenate([m_dwuq[:, MLA_QW * h:MLA_QW * h + qw] for h in range(N_HEADS)], axis=1),
        "mla_kv_norm_g": m_dgkv[0],
        "mla_w_ukv": jnp.concatenate(sum([[m_dwk[:, LANES * h:LANES * (h + 1)], m_dwv[:, LANES * h:LANES * (h + 1)]]
                                          for h in range(N_HEADS)], []), axis=1),
    }
    return dx, grads


def _layer_params(w, l):
    b_in = w["b_in"][l].reshape(1, -1)
    b_segs = {n: b_in[:, _ORIG[n]:_ORIG[n] + wd] for n, wd in _SEGS}
    qw = MLA_NOPE + MLA_ROPE
    w_uq, w_ukv = w["mla_w_uq"][l], w["mla_w_ukv"][l]
    uq = []
    for h in range(N_HEADS):
        uq += [w_uq[:, qw * h:qw * (h + 1)], jnp.zeros((w_uq.shape[0], MLA_QW - qw), w_uq.dtype)]
    kw_ = 2 * LANES
    flat = CMP_LEN * HEAD_DIM
    return dict(
        pre_g=w["pre_norm_g"][l].reshape(1, -1), post_g=w["post_norm_g"][l].reshape(1, -1),
        w_in=w["w_in"][l], b_in=_to_groups(b_segs, 1, f32), w_out=w["w_out"][l],
        fox_bias=jnp.pad(w["fox_forget_bias"][l], (0, LANES - N_HEADS)).reshape(1, LANES),
        cmp_pos=jnp.stack([w["nsa_cmp_pos_k"][l].reshape(1, flat), w["nsa_cmp_pos_v"][l].reshape(1, flat)]),
        cmp_w1=jnp.stack([w["nsa_cmp_w1_k"][l], w["nsa_cmp_w1_v"][l]]),
        cmp_w2=jnp.stack([w["nsa_cmp_w2_k"][l], w["nsa_cmp_w2_v"][l]]),
        gq=w["mla_q_norm_g"][l].reshape(1, -1), gkv=w["mla_kv_norm_g"][l].reshape(1, -1),
        wuq=jnp.concatenate(uq, axis=1),
        wk=jnp.concatenate([w_ukv[:, kw_ * h:kw_ * h + LANES] for h in range(N_HEADS)], axis=1),
        wv=jnp.concatenate([w_ukv[:, kw_ * h + LANES:kw_ * (h + 1)] for h in range(N_HEADS)], axis=1),
    )


def _consts(s):
    pos = jnp.arange(s)
    m2s, e3 = _nsa_consts(s)
    return dict(tabs128=_rope_tables(pos, HEAD_DIM), tabs64=_rope_tables(pos, MLA_ROPE),
                tabs_cmp=_rope_tables(jnp.arange(s // CMP_STRIDE) * CMP_STRIDE + (CMP_LEN - 1), HEAD_DIM),
                m2s=m2s, e3=e3)


def _place():
    return lax.axis_index("x"), lax.axis_index("y"), lax.axis_index("c")


def _other_chips(x, y):
    return [(1 - x, y), (x, 1 - y), (1 - x, 1 - y)]


def _comm_call(body, out_shapes, n_sems, arrs, name):
    return pl.pallas_call(body, out_shape=tuple(out_shapes), in_specs=[_ANY] * len(arrs),
                          out_specs=tuple(_ANY for _ in out_shapes),
                          scratch_shapes=[pltpu.SemaphoreType.DMA((n_sems,)), pltpu.SemaphoreType.DMA((n_sems,))],
                          name=name)(*arrs)


def _gather_chips(arrs, *, name):
    n = len(arrs)

    def body(*refs):
        a_refs, out_refs, send_sems, recv_sems = refs[:n], refs[n:2 * n], refs[2 * n], refs[2 * n + 1]
        x, y, c = _place()
        me = 2 * x + y
        sibling = (x, y, 1 - c)
        chips = _other_chips(x, y)

        def copy(j, k, src, dst, to):
            return pltpu.make_async_remote_copy(src, dst, send_sems.at[6 * j + k], recv_sems.at[6 * j + k],
                                                device_id=to, device_id_type=_MESH)

        first = [copy(j, k, a_refs[j].at[c], out_refs[j].at[me, c], (px, py, c))
                 for k, (px, py) in enumerate(chips) for j in range(n)]
        for cp in first:
            cp.start()
        passed = []
        for k, (px, py) in enumerate(chips):
            for j in range(n):
                landed = out_refs[j].at[2 * px + py, c]
                copy(j, k, a_refs[j].at[c], landed, (px, py, c)).wait_recv()
                passed.append(copy(j, 3 + k, landed, landed, sibling))
                passed[-1].start()
        for k, (px, py) in enumerate(chips):
            for j in range(n):
                copy(j, 3 + k, a_refs[j].at[c], out_refs[j].at[2 * px + py, 1 - c], sibling).wait_recv()
        for cp in first + passed:
            cp.wait_send()

    return _comm_call(body, [_SDS((N_CHIPS,) + a.shape, a.dtype) for a in arrs], 6 * n, arrs, name)


def _alltoall_chips(arrs, modes, *, name):
    n = len(arrs)
    slot = lambda ref, mode, s: _slot_ref(ref, mode, s)
    lane_slots = modes

    def body(*refs):
        g_refs, out_refs, send_sems, recv_sems = refs[:n], refs[n:2 * n], refs[2 * n], refs[2 * n + 1]
        x, y, c = _place()
        me = 2 * x + y

        def copy(j, s):
            return pltpu.make_async_remote_copy(slot(g_refs[j], lane_slots[j], s), out_refs[j].at[me],
                                                send_sems.at[N_CHIPS * j + s], recv_sems.at[N_CHIPS * j + me],
                                                device_id=(s // 2, s % 2, c), device_id_type=_MESH)

        for s in range(N_CHIPS):
            @pl.when(s != me)
            def _():
                for j in range(n):
                    copy(j, s).start()
        for t in range(N_CHIPS):
            @pl.when(t != me)
            def _():
                for j in range(n):
                    pltpu.make_async_remote_copy(slot(g_refs[j], lane_slots[j], t), out_refs[j].at[t],
                                                 send_sems.at[N_CHIPS * j + t], recv_sems.at[N_CHIPS * j + t],
                                                 device_id=(t // 2, t % 2, c), device_id_type=_MESH).wait_recv()
        for s in range(N_CHIPS):
            @pl.when(s != me)
            def _():
                for j in range(n):
                    copy(j, s).wait_send()

    outs = [_SDS((N_CHIPS,) + _slot_shape(a, m), a.dtype) for a, m in zip(arrs, modes)]
    return _comm_call(body, outs, N_CHIPS * n, arrs, name)


def _swap_other_half(arrs, *, name):
    n = len(arrs)

    def body(*refs):
        g_refs, out_refs, send_sems, recv_sems = refs[:n], refs[n:2 * n], refs[2 * n], refs[2 * n + 1]
        x, y, c = _place()
        cps = [pltpu.make_async_remote_copy(g_refs[j].at[:, 1 - c], out_refs[j], send_sems.at[j], recv_sems.at[j],
                                            device_id=(x, y, 1 - c), device_id_type=_MESH) for j in range(n)]
        for cp in cps:
            cp.start()
        for cp in cps:
            cp.wait()

    return _comm_call(body, [_SDS((a.shape[0],) + a.shape[2:], a.dtype) for a in arrs], n, arrs, name)


def _swap_sibling(arrs, *, name):
    n = len(arrs)

    def body(*refs):
        f_refs, out_refs, send_sems, recv_sems = refs[:n], refs[n:2 * n], refs[2 * n], refs[2 * n + 1]
        x, y, c = _place()
        cps = [pltpu.make_async_remote_copy(f_refs[j], out_refs[j], send_sems.at[j], recv_sems.at[j],
                                            device_id=(x, y, 1 - c), device_id_type=_MESH) for j in range(n)]
        for cp in cps:
            cp.start()
        for cp in cps:
            cp.wait()

    return _comm_call(body, [_SDS(a.shape, a.dtype) for a in arrs], n, arrs, name)


_HBM = pl.BlockSpec(memory_space=pltpu.HBM)
_SEM = pl.BlockSpec(memory_space=pltpu.SEMAPHORE)
_EFFECT = pltpu.SideEffectType.DATAFLOW_SIDE_EFFECTING


def _slot_ref(ref, mode, s):
    return ref if mode == "same" else ref.at[s]


def _slot_shape(a, mode):
    return a.shape if mode == "same" else a.shape[1:]


def _send_start(arrs, modes, after, *, name):
    n = len(arrs)
    lands = [lax.empty((N_CHIPS,) + _slot_shape(a, m), a.dtype) for a, m in zip(arrs, modes)]

    def body(*refs):
        srcs, land_refs, send_sems, recv_sems, token = refs[:n], refs[n:2 * n], refs[2 * n + 1], refs[2 * n + 2], refs[-1]
        x, y, c = _place()
        me = 2 * x + y
        for s in range(N_CHIPS):
            @pl.when(s != me)
            def _():
                for j in range(n):
                    pltpu.make_async_remote_copy(_slot_ref(srcs[j], modes[j], s), land_refs[j].at[me],
                                                 send_sems.at[N_CHIPS * j + s], recv_sems.at[N_CHIPS * j + me],
                                                 device_id=(s // 2, s % 2, c), device_id_type=_MESH).start()
        token[...] = jnp.zeros_like(token)

    hbm = lambda a: pltpu.HBM(a.shape, a.dtype)
    sems = pltpu.SemaphoreType.DMA((N_CHIPS * n,))
    out = pl.pallas_call(
        body, name=name, out_shape=(sems, sems, *[hbm(a) for a in arrs], *[hbm(a) for a in lands], _SDS((8, LANES), f32)),
        in_specs=[_HBM] * (2 * n) + [_ANY], out_specs=(_SEM, _SEM, *[_HBM] * (2 * n), pl.BlockSpec(memory_space=pltpu.VMEM)),
        input_output_aliases={j: 2 + j for j in range(2 * n)},
        compiler_params=pltpu.CompilerParams(has_side_effects=_EFFECT),
    )(*[pltpu.with_memory_space_constraint(a, pltpu.HBM) for a in arrs + lands], after)
    return out[:-1], out[-1]


def _send_wait(started, modes, after, *, name):
    send_sems, recv_sems = started[0], started[1]
    n = (len(started) - 2) // 2
    thru = list(started[2:])

    def body(*refs):
        srcs, land_refs, send_sems, recv_sems = refs[:n], refs[n:2 * n], refs[2 * n], refs[2 * n + 1]
        x, y, c = _place()
        me = 2 * x + y
        for s in range(N_CHIPS):
            @pl.when(s != me)
            def _():
                for j in range(n):
                    cp = pltpu.make_async_remote_copy(_slot_ref(srcs[j], modes[j], s), land_refs[j].at[s],
                                                      send_sems.at[N_CHIPS * j + s], recv_sems.at[N_CHIPS * j + s],
                                                      device_id=(s // 2, s % 2, c), device_id_type=_MESH)
                    cp.wait_send()
                    cp.wait_recv()

    hbm = lambda a: pltpu.HBM(a.shape, a.dtype)
    out = pl.pallas_call(
        body, name=name, out_shape=tuple(hbm(a) for a in thru), in_specs=[_HBM] * (2 * n) + [_SEM, _SEM, _ANY],
        out_specs=tuple([_HBM] * (2 * n)), input_output_aliases={j: j for j in range(2 * n)},
        compiler_params=pltpu.CompilerParams(has_side_effects=_EFFECT),
    )(*thru, send_sems, recv_sems, after)
    return list(out[n:])


def _gather_all(a, *, name):
    def body(a_ref, out_ref, send_sems, recv_sems, local_sem):
        x, y, c = _place()
        flip = lambda v, f: (1 - v) if f else v
        peers = [(flip(x, f & 4), flip(y, f & 2), flip(c, f & 1)) for f in range(1, 8)]
        me = 4 * x + 2 * y + c
        mine = pltpu.make_async_copy(a_ref, out_ref.at[me], local_sem)
        mine.start()
        sends = [pltpu.make_async_remote_copy(a_ref, out_ref.at[me], send_sems.at[k], recv_sems.at[k], device_id=peer,
                                              device_id_type=_MESH) for k, peer in enumerate(peers)]
        for cp in sends:
            cp.start()
        for k, (px, py, pc) in enumerate(peers):
            pltpu.make_async_remote_copy(a_ref, out_ref.at[4 * px + 2 * py + pc], send_sems.at[k], recv_sems.at[k],
                                         device_id=(px, py, pc), device_id_type=_MESH).wait_recv()
        for cp in sends:
            cp.wait_send()
        mine.wait()

    return pl.pallas_call(body, out_shape=_SDS((8,) + a.shape, a.dtype), in_specs=[_ANY], out_specs=_ANY,
                          scratch_shapes=[pltpu.SemaphoreType.DMA((7,)), pltpu.SemaphoreType.DMA((7,)),
                                          pltpu.SemaphoreType.DMA], name=name)(a)


def _add_my_half(g, r, *, name):
    p, _, h, w = g.shape
    tw = _pick(w, (2048, 1024, 512, 256, 128))
    rb = max(d for d in range(16, h + 1, 16) if h % d == 0 and d * tw * 4 <= (2 << 20))

    def body(c_ref, g_ref, r_ref, o_ref):
        o_ref[...] = (g_ref[...] + r_ref[...]).astype(o_ref.dtype)

    blk = pl.BlockSpec((None, rb, tw), lambda s, i, j, c_ref: (s, i, j))
    grid_spec = pltpu.PrefetchScalarGridSpec(
        num_scalar_prefetch=1, grid=(p, h // rb, w // tw),
        in_specs=[pl.BlockSpec((None, None, rb, tw), lambda s, i, j, c_ref: (s, c_ref[0], i, j)), blk], out_specs=blk)
    c = lax.axis_index("c").astype(jnp.int32).reshape(1)
    return pl.pallas_call(body, out_shape=_SDS((p, h, w), _WIRE), grid_spec=grid_spec,
                          compiler_params=_cp(("parallel", "parallel", "parallel")), name=name)(c, g, r)


_WEIGHTS = ("pre_norm_g", "post_norm_g", "w_in", "b_in", "w_out", "fox_forget_bias", "nsa_cmp_pos_k", "nsa_cmp_w1_k",
            "nsa_cmp_w2_k", "nsa_cmp_pos_v", "nsa_cmp_w1_v", "nsa_cmp_w2_v", "mla_q_norm_g", "mla_w_uq",
            "mla_kv_norm_g", "mla_w_ukv")
_SHARD_AXIS = {"w_in": 2, "w_out": 1, "nsa_cmp_w1_k": 1, "nsa_cmp_w1_v": 1, "mla_w_uq": 2, "mla_w_ukv": 2}
_PACK_UNIT = 16 * LANES


def _pack(arrays, dtype):
    rows = []
    for a in arrays:
        v = a.astype(dtype).reshape(-1)
        pad = (-v.shape[0]) % _PACK_UNIT
        if pad:
            v = jnp.concatenate([v, jnp.zeros((pad,), dtype)])
        rows.append(v.reshape(-1, LANES))
    return jnp.concatenate(rows, axis=0)


def _unpack(flat, shapes):
    out, r = [], 0
    for shp in shapes:
        n = int(np.prod(shp))
        nr = -(-n // _PACK_UNIT) * (_PACK_UNIT // LANES)
        out.append(flat[r:r + nr].reshape(-1)[:n].reshape(shp))
        r += nr
    return out


def kernel(x, pre_norm_g, post_norm_g, w_in, b_in, w_out, fox_forget_bias, nsa_cmp_pos_k, nsa_cmp_w1_k, nsa_cmp_w2_k, nsa_cmp_pos_v, nsa_cmp_w1_v, nsa_cmp_w2_v, mla_q_norm_g, mla_w_uq, mla_kv_norm_g, mla_w_ukv, loss_target, m_pre_norm_g, m_post_norm_g, m_w_in, m_b_in, m_w_out, m_fox_forget_bias, m_nsa_cmp_pos_k, m_nsa_cmp_w1_k, m_nsa_cmp_w2_k, m_nsa_cmp_pos_v, m_nsa_cmp_w1_v, m_nsa_cmp_w2_v, m_mla_q_norm_g, m_mla_w_uq, m_mla_kv_norm_g, m_mla_w_ukv, v_pre_norm_g, v_post_norm_g, v_w_in, v_b_in, v_w_out, v_fox_forget_bias, v_nsa_cmp_pos_k, v_nsa_cmp_w1_k, v_nsa_cmp_w2_k, v_nsa_cmp_pos_v, v_nsa_cmp_w1_v, v_nsa_cmp_w2_v, v_mla_q_norm_g, v_mla_w_uq, v_mla_kv_norm_g, v_mla_w_ukv):
    given = dict(locals())
    local = {n: given[n] for n in _WEIGHTS}
    depth = pre_norm_g.shape[0]
    xs, target = x[0], loss_target[0]
    s = xs.shape[0]
    sharded = [n for n in _WEIGHTS if n in _SHARD_AXIS and n != "w_in"]
    small = [n for n in _WEIGHTS if n not in _SHARD_AXIS]
    chip = 2 * lax.axis_index("x") + lax.axis_index("y")
    core = lax.axis_index("c")
    own = lambda slots, mine: lax.dynamic_update_slice_in_dim(slots, mine[None], chip, axis=0)

    w_in_t = jnp.swapaxes(w_in, 1, 2).astype(_MXU)
    piece = lax.switch(chip, [functools.partial(_piece_from_shard, s=k) for k in range(N_CHIPS)], w_in_t)
    shard_shapes = [local[n].shape for n in sharded]
    flat = _pack([local[n] for n in sharded], _MXU)
    flat2 = flat.reshape((2, -1, LANES))
    first = piece[0].reshape(2, GROUP_W // 2, D_MODEL)
    first_all, flat_all = _gather_chips([first, flat2], name="gather_weights")
    flat_all = own(flat_all, flat2).reshape((N_CHIPS,) + flat.shape)
    per_chip = [_unpack(flat_all[k], shard_shapes) for k in range(N_CHIPS)]
    full = dict(local)
    full["w_in"] = [own(first_all, first).reshape(N_CHIPS, GROUP_W, D_MODEL)]
    for j, n in enumerate(sharded):
        full[n] = jnp.concatenate([per_chip[k][j] for k in range(N_CHIPS)], axis=_SHARD_AXIS[n])
    later = [piece[l] for l in range(1, depth)]
    started, token = _send_start(later, ["same"] * len(later), flat_all, name="gather_later_start")
    full["pre_norm_g"] = pre_norm_g + token[0, 0]

    consts = _consts(s)
    params, act, saved = [], xs, []
    for l in range(depth):
        if l == 1:
            landed = _send_wait(started, ["same"] * len(later), act, name="gather_later_wait")
            full["w_in"] += [own(a, b) for a, b in zip(landed, later)]
        params.append(_layer_params(full, l))
        act, sv = _layer_fwd(act, params[l], consts, f"l{l}")
        saved.append(sv)
    dy, loss_parts = _loss_head(act, target, name="loss_head")
    layer_shapes = [local[n].shape[1:] for n in sharded]

    def flat_slots(g, dtype):
        def part(n, k):
            a, ax = g[n], _SHARD_AXIS[n] - 1
            w = a.shape[ax] // N_CHIPS
            return lax.slice_in_dim(a, k * w, (k + 1) * w, axis=ax)
        return jnp.stack([_pack([part(n, k) for n in sharded], dtype) for k in range(N_CHIPS)])

    own_slot = lambda a: lax.dynamic_index_in_dim(a, chip, axis=0, keepdims=False)
    slots_of = lambda g: g["w_in"].reshape(N_CHIPS, GROUP_W, D_MODEL)

    modes = ["slots", "slots"]
    layer_grads, in_flight = [None] * depth, {}
    for l in reversed(range(depth)):
        dy, layer_grads[l] = _layer_bwd(dy, saved[l], params[l], consts, f"l{l}")
        if l > 0:
            wire = [slots_of(layer_grads[l]).astype(_WIRE), flat_slots(layer_grads[l], _WIRE)]
            started, token = _send_start(wire, modes, dy, name=f"reduce_l{l}_start")
            in_flight[l] = (started, wire)
            params[l - 1] = dict(params[l - 1], post_g=params[l - 1]["post_g"] + token[0, 0])
    grad_x = dy[None]
    grads = {n: jnp.stack([layer_grads[l][n] for l in range(depth)]) for n in small}
    loss_row = jnp.concatenate([jnp.sum(loss_parts).reshape(1), jnp.zeros((LANES - 1,), f32)])
    small_shapes = [(LANES,)] + [grads[n].shape for n in small]
    contrib = _pack([loss_row] + [grads[n] for n in small], f32)

    halves = [slots_of(layer_grads[0]).reshape(N_CHIPS, 2, GROUP_W // 2, D_MODEL),
              flat_slots(layer_grads[0], f32).reshape(N_CHIPS, 2, -1, LANES)]
    from_sibling = _swap_other_half(halves, name="reduce_pair")
    pair_sum = [_add_my_half(g, r, name=f"reduce_pair_add{j}") for j, (g, r) in enumerate(zip(halves, from_sibling))]
    from_chips = _alltoall_chips(pair_sum + [contrib], modes + ["same"], name="reduce_chips")
    my_half = [_sum_slots(own(slots, own_slot(ps)), name=f"reduce_chips_add{j}")
               for j, (slots, ps) in enumerate(zip(from_chips, pair_sum))]
    partial = []
    for l in range(1, depth):
        started, wire = in_flight[l]
        landed = _send_wait(started, modes, dy, name=f"reduce_l{l}_wait")
        partial += [_sum_slots(own(slots, own_slot(a)), name=f"reduce_l{l}_add{j}")
                    for j, (slots, a) in enumerate(zip(landed, wire))]
    partial.append(_sum_slots(own(from_chips[2], contrib), name="sum_small"))
    theirs = _swap_sibling(my_half + partial, name="reduce_share")
    first = core == 0
    whole = [jnp.concatenate([jnp.where(first, a, b), jnp.where(first, b, a)], axis=0)
             for a, b in zip(my_half, theirs[:2])]
    whole += [_add2(a[None], b[None], name=f"reduce_cores_add{j}")[0] for j, (a, b) in enumerate(zip(partial, theirs[2:]))]
    unpiece = [functools.partial(_shard_from_piece, s=k) for k in range(N_CHIPS)]
    summed = {"w_in": jnp.stack([lax.switch(chip, unpiece, whole[2 * l].T) for l in range(depth)])}
    rest = [_unpack(whole[2 * l + 1], layer_shapes) for l in range(depth)]
    for j, n in enumerate(sharded):
        summed[n] = jnp.stack([rest[l][j] for l in range(depth)])
    total = _unpack(whole[2 * depth], small_shapes)
    loss = total[0][0]
    summed.update(zip(small, total[1:]))

    deltas, new_m, new_v = {}, {}, {}
    for n in _WEIGHTS:
        deltas[n], new_m[n], new_v[n] = _adamw(local[n], summed[n], given["m_" + n], given["v_" + n], name=f"adamw_{n}")
    return (loss, grad_x, *[summed[n] for n in _WEIGHTS], *[deltas[n] for n in _WEIGHTS],
            *[new_m[n] for n in _WEIGHTS], *[new_v[n] for n in _WEIGHTS])
```

```python
import functools
import math

import numpy as np
import jax
import jax.numpy as jnp
from jax import lax
from jax.experimental import pallas as pl
from jax.experimental.pallas import tpu as pltpu

f32 = jnp.float32
bf16 = jnp.bfloat16
_MXU = jnp.bfloat16
_WIRE = jnp.bfloat16
_SDS = jax.ShapeDtypeStruct
_ANY = pl.BlockSpec(memory_space=pl.ANY)
_MESH = pl.DeviceIdType.MESH

D_MODEL = 2048
N_HEADS = 4
HEAD_DIM = 128
GROUP = 512
RMS_EPS = 1e-6
NEG_INF = -1e30
ROPE_THETA = 10000.0
CMP_LEN, CMP_STRIDE, SEL_LEN, SEL_TOPN, WINDOW = 32, 16, 64, 16, 512
FORCED_BONUS = 1e6
MLA_Q_RANK, MLA_KV_RANK, MLA_NOPE, MLA_ROPE = 384, 128, 128, 64
ADAM_LR, ADAM_B1, ADAM_B2, ADAM_EPS, ADAM_WD, ADAM_STEP = 0.001, 0.9, 0.999, 1e-08, 0.01, 10
LANES = 128
VMEM_LIMIT = 48 * 1024 * 1024
HP_FWD, HP_BWD = 2, 1

_SEGS = (
    ("sb_q", 512), ("sb_k", 512), ("sb_v", 512), ("sb_gate", 512), ("nsa_q", 512), ("nsa_k_cmp", 128),
    ("nsa_v_cmp", 128), ("nsa_k_sel", 128), ("nsa_v_sel", 128), ("nsa_k_win", 128), ("nsa_v_win", 128),
    ("nsa_branch", 12), ("nsa_gate", 512), ("fox_q", 512), ("fox_k", 512), ("fox_v", 512), ("fox_f", 4),
    ("fox_gate", 512), ("mla_cq", 384), ("mla_ckv", 128), ("mla_k_rope", 64), ("mla_gate", 512),
)
_ORIG, _WID = {}, {}
_o = 0
for _n, _w in _SEGS:
    _ORIG[_n], _WID[_n] = _o, _w
    _o += _w
IN_WIDTH = _o
N_CHIPS = 4
CHIP_COLS = IN_WIDTH // N_CHIPS
GROUP_W = 2048
ZW = N_CHIPS * GROUP_W
_GROUPS = (
    (("sb_q", 0, 512, 0), ("sb_k", 0, 512, 512), ("sb_v", 0, 512, 1024), ("sb_gate", 0, 212, 1536)),
    (("nsa_q", 0, 512, 0), ("nsa_k_cmp", 0, 128, 512), ("nsa_v_cmp", 0, 128, 640), ("nsa_k_sel", 0, 128, 768),
     ("nsa_v_sel", 0, 128, 896), ("nsa_k_win", 0, 128, 1024), ("nsa_v_win", 0, 128, 1152), ("nsa_branch", 0, 12, 1280),
     ("sb_gate", 212, 512, 1408), ("nsa_gate", 0, 156, 1712)),
    (("fox_q", 0, 512, 0), ("fox_k", 0, 512, 512), ("fox_v", 0, 368, 1024), ("nsa_gate", 156, 512, 1408)),
    (("mla_cq", 0, 384, 0), ("mla_ckv", 0, 128, 384), ("mla_k_rope", 0, 64, 512), ("fox_f", 0, 4, 640),
     ("fox_v", 368, 512, 768), ("fox_gate", 0, 512, 1024), ("mla_gate", 0, 512, 1536)),
)
_PIECES = {n: [] for n, _ in _SEGS}
for _s, _grp in enumerate(_GROUPS):
    _cover = sorted((_ORIG[n] + lo, _ORIG[n] + hi) for n, lo, hi, _ in _grp)
    assert _cover[0][0] == _s * CHIP_COLS and _cover[-1][1] == (_s + 1) * CHIP_COLS
    assert all(a[1] == b[0] for a, b in zip(_cover, _cover[1:]))
    _ends = sorted((off, off + hi - lo) for _, lo, hi, off in _grp)
    assert all(a[1] <= b[0] for a, b in zip(_ends, _ends[1:])) and _ends[-1][1] <= GROUP_W
    assert _ends[0][0] == 0 and all(e[0] % 16 == 0 for e in _ends)
    for _n, _lo, _hi, _off in _grp:
        _PIECES[_n].append((_s * GROUP_W + _off, _lo, _hi))
_AL = {n: p[0][0] for n, p in _PIECES.items() if len(p) == 1}


def _cp(sem=None):
    return pltpu.CompilerParams(dimension_semantics=sem, vmem_limit_bytes=VMEM_LIMIT)


def _mm(a, b):
    return jnp.dot(a.astype(_MXU), b.astype(_MXU), preferred_element_type=f32)


def _mm_nt(a, b):
    return lax.dot_general(a.astype(_MXU), b.astype(_MXU), (((1,), (1,)), ((), ())), preferred_element_type=f32)


def _mm_tn(a, b):
    return lax.dot_general(a.astype(_MXU), b.astype(_MXU), (((0,), (0,)), ((), ())), preferred_element_type=f32)


def _mm_split(x, t):
    hi = x.astype(_MXU)
    lo = (x - hi.astype(f32)).astype(_MXU)
    return jnp.dot(hi, t, preferred_element_type=f32) + jnp.dot(lo, t, preferred_element_type=f32)


def _sigmoid(x):
    return 1.0 / (1.0 + jnp.exp(-x))


def _iota(shape, dim):
    return lax.broadcasted_iota(jnp.int32, shape, dim)


def _pick(n, prefs):
    for p in prefs:
        if n % p == 0:
            return p
    return n


def _matmul(a, b, mode, *, bias=None, out_dtype=f32, name):
    grouped = b.ndim == 3
    b_shape = (b.shape[0] * b.shape[1], b.shape[2]) if grouped else b.shape
    if mode == "nn":
        (M, K), (K2, N) = a.shape, b_shape
    elif mode == "nt":
        (M, K), (N, K2) = a.shape, b_shape
    else:
        (K, M), (K2, N) = a.shape, b_shape
    assert K == K2
    tm = _pick(M, (1024, 512, 384, 256, 128))
    tn = _pick(N, (512, 384, 256, 128))
    tk = K if K <= 2048 else _pick(K, (2048, 2432, 1024, 512))
    nk = K // tk
    a_spec = {"nn": pl.BlockSpec((tm, tk), lambda i, j, k: (i, k)),
              "nt": pl.BlockSpec((tm, tk), lambda i, j, k: (i, k)),
              "tn": pl.BlockSpec((tk, tm), lambda i, j, k: (k, i))}[mode]
    if not grouped:
        b_spec = {"nn": pl.BlockSpec((tk, tn), lambda i, j, k: (k, j)),
                  "nt": pl.BlockSpec((tn, tk), lambda i, j, k: (j, k)),
                  "tn": pl.BlockSpec((tk, tn), lambda i, j, k: (k, j))}[mode]
    elif mode == "nt":
        per = b.shape[1] // tn
        b_spec = pl.BlockSpec((None, tn, tk), lambda i, j, k: (j // per, j % per, k))
    else:
        assert mode == "nn"
        per = b.shape[1] // tk
        b_spec = pl.BlockSpec((None, tk, tn), lambda i, j, k: (k // per, k % per, j))
    dot = {"nn": _mm, "nt": _mm_nt, "tn": _mm_tn}[mode]
    has_bias = bias is not None

    def body(*refs):
        if has_bias:
            a_ref, b_ref, bias_ref, o_ref, acc_ref = refs
        else:
            a_ref, b_ref, o_ref, acc_ref = refs
            bias_ref = None
        k = pl.program_id(2)
        part = dot(a_ref[...], b_ref[...])

        def finish(total):
            if has_bias:
                total = total + bias_ref[...]
            o_ref[...] = total.astype(o_ref.dtype)

        if nk == 1:
            finish(part)
        else:
            @pl.when(k == 0)
            def _():
                acc_ref[...] = part

            @pl.when(k > 0)
            def _():
                acc_ref[...] += part

            @pl.when(k == nk - 1)
            def _():
                finish(acc_ref[...])

    in_specs = [a_spec, b_spec]
    args = [a, b]
    if has_bias:
        in_specs.append(pl.BlockSpec((1, tn), lambda i, j, k: (0, j)))
        args.append(bias.reshape(1, N))
    return pl.pallas_call(
        body, out_shape=_SDS((M, N), out_dtype), grid=(M // tm, N // tn, nk),
        in_specs=in_specs, out_specs=pl.BlockSpec((tm, tn), lambda i, j, k: (i, j)),
        scratch_shapes=[pltpu.VMEM((tm, tn), f32)],
        compiler_params=_cp(("parallel", "parallel", "arbitrary")), name=name,
    )(*args)


def _row_block(s):
    return _pick(s, (256, 128))


def _rms_fwd(x, g, *, out_dtype, name):
    s, d = x.shape
    rb = _row_block(s)

    def body(x_ref, g_ref, o_ref):
        xv = x_ref[...]
        r = lax.rsqrt(jnp.mean(xv * xv, axis=-1, keepdims=True) + RMS_EPS)
        o_ref[...] = (xv * r * g_ref[...]).astype(o_ref.dtype)

    return pl.pallas_call(
        body, out_shape=_SDS((s, d), out_dtype), grid=(s // rb,),
        in_specs=[pl.BlockSpec((rb, d), lambda i: (i, 0)), pl.BlockSpec((1, d), lambda i: (0, 0))],
        out_specs=pl.BlockSpec((rb, d), lambda i: (i, 0)), compiler_params=_cp(("parallel",)), name=name,
    )(x, g.reshape(1, d))


def _postnorm_fwd(u, g, x, *, name):
    s, d = u.shape
    rb = _row_block(s)

    def body(u_ref, g_ref, x_ref, o_ref):
        uv = u_ref[...]
        r = lax.rsqrt(jnp.mean(uv * uv, axis=-1, keepdims=True) + RMS_EPS)
        o_ref[...] = x_ref[...] + uv * r * g_ref[...]

    return pl.pallas_call(
        body, out_shape=_SDS((s, d), f32), grid=(s // rb,),
        in_specs=[pl.BlockSpec((rb, d), lambda i: (i, 0)), pl.BlockSpec((1, d), lambda i: (0, 0)),
                  pl.BlockSpec((rb, d), lambda i: (i, 0))],
        out_specs=pl.BlockSpec((rb, d), lambda i: (i, 0)), compiler_params=_cp(("parallel",)), name=name,
    )(u, g.reshape(1, d), x)


def _fold_rows(v):
    r = v.shape[0]
    acc = v[0:8]
    for k in range(1, r // 8):
        acc = acc + v[8 * k:8 * k + 8]
    return acc


def _rms_bwd(dy, x, g, res=None, *, name):
    s, d = x.shape
    rb = _row_block(s)
    nb = s // rb
    has_res = res is not None

    def body(*refs):
        if has_res:
            dy_ref, x_ref, g_ref, res_ref, dx_ref, dg_ref, acc_ref = refs
        else:
            dy_ref, x_ref, g_ref, dx_ref, dg_ref, acc_ref = refs
        i = pl.program_id(0)
        xv = x_ref[...]
        r = lax.rsqrt(jnp.mean(xv * xv, axis=-1, keepdims=True) + RMS_EPS)
        xh = xv * r
        dyv = dy_ref[...]
        dxh = dyv * g_ref[...]
        dx = r * (dxh - xh * jnp.mean(dxh * xh, axis=-1, keepdims=True))
        if has_res:
            dx = dx + res_ref[...]
        dx_ref[...] = dx
        part = _fold_rows(dyv * xh)

        @pl.when(i == 0)
        def _():
            acc_ref[...] = part

        @pl.when(i > 0)
        def _():
            acc_ref[...] += part

        @pl.when(i == nb - 1)
        def _():
            dg_ref[...] = jnp.sum(acc_ref[...], axis=0, keepdims=True)

    blk = pl.BlockSpec((rb, d), lambda i: (i, 0))
    in_specs = [blk, blk, pl.BlockSpec((1, d), lambda i: (0, 0))] + ([blk] if has_res else [])
    args = [dy, x, g.reshape(1, d)] + ([res] if has_res else [])
    return pl.pallas_call(
        body, out_shape=(_SDS((s, d), f32), _SDS((1, d), f32)), grid=(nb,), in_specs=in_specs,
        out_specs=(blk, pl.BlockSpec((1, d), lambda i: (0, 0))),
        scratch_shapes=[pltpu.VMEM((8, d), f32)], compiler_params=_cp(("arbitrary",)), name=name,
    )(*args)


def _loss_head(y, target, *, name):
    s, d = y.shape
    rb = _row_block(s)
    nb = s // rb

    def body(y_ref, t_ref, dy_ref, l_ref):
        i = pl.program_id(0)
        e = y_ref[...] - t_ref[...]
        dy_ref[...] = e * (1.0 / d)
        rows = _fold_rows(e * e)
        part = rows[:, 0:LANES]
        for k in range(1, d // LANES):
            part = part + rows[:, k * LANES:(k + 1) * LANES]
        part = part * (0.5 / d)

        @pl.when(i == 0)
        def _():
            l_ref[...] = part

        @pl.when(i > 0)
        def _():
            l_ref[...] += part

    blk = pl.BlockSpec((rb, d), lambda i: (i, 0))
    return pl.pallas_call(
        body, out_shape=(_SDS((s, d), f32), _SDS((8, LANES), f32)), grid=(nb,), in_specs=[blk, blk],
        out_specs=(blk, pl.BlockSpec((8, LANES), lambda i: (0, 0))),
        compiler_params=_cp(("arbitrary",)), name=name,
    )(y, target)


def _colsum(a, *, name):
    s, n = a.shape
    rb = _row_block(s)
    nb = s // rb
    tn = _pick(n, (2432, 2048, 1024, 512, 384, 128))

    def body(a_ref, o_ref, acc_ref):
        i = pl.program_id(1)
        part = _fold_rows(a_ref[...].astype(f32))

        @pl.when(i == 0)
        def _():
            acc_ref[...] = part

        @pl.when(i > 0)
        def _():
            acc_ref[...] += part

        @pl.when(i == nb - 1)
        def _():
            o_ref[...] = jnp.sum(acc_ref[...], axis=0, keepdims=True)

    return pl.pallas_call(
        body, out_shape=_SDS((1, n), f32), grid=(n // tn, nb),
        in_specs=[pl.BlockSpec((rb, tn), lambda j, i: (i, j))], out_specs=pl.BlockSpec((1, tn), lambda j, i: (0, j)),
        scratch_shapes=[pltpu.VMEM((8, tn), f32)], compiler_params=_cp(("parallel", "arbitrary")), name=name,
    )(a)


def _gate_fwd(o, gate, *, name):
    s, d = o.shape
    rb = _row_block(s)

    def body(o_ref, g_ref, m_ref):
        gv = g_ref[...]
        m_ref[...] = (o_ref[...] * (gv * _sigmoid(gv))).astype(m_ref.dtype)

    blk = pl.BlockSpec((rb, d), lambda i: (i, 0))
    return pl.pallas_call(body, out_shape=_SDS((s, d), _MXU), grid=(s // rb,), in_specs=[blk, blk], out_specs=blk,
                          compiler_params=_cp(("parallel",)), name=name)(o, gate)


def _gate_bwd(dmix, o, gate, *, name):
    s, d = o.shape
    rb = _row_block(s)

    def body(dm_ref, o_ref, g_ref, do_ref, dg_ref):
        gv = g_ref[...]
        sg = _sigmoid(gv)
        dm = dm_ref[...]
        do_ref[...] = dm * (gv * sg)
        dg_ref[...] = dm * o_ref[...] * (sg * (1.0 + gv * (1.0 - sg)))

    blk = pl.BlockSpec((rb, d), lambda i: (i, 0))
    return pl.pallas_call(body, out_shape=(_SDS((s, d), f32), _SDS((s, d), f32)), grid=(s // rb,),
                          in_specs=[blk, blk, blk], out_specs=(blk, blk), compiler_params=_cp(("parallel",)),
                          name=name)(dmix, o, gate)


def _adamw(w, g, m, v, *, name):
    shape = w.shape
    cols = shape[-1]
    rows = int(np.prod(shape[:-1])) if len(shape) > 1 else 1
    to2 = lambda t: t.reshape(rows, cols)
    rb = rows
    if rows * cols * 4 > (1 << 20):
        rb = max(d for d in range(8, rows + 1, 8) if rows % d == 0 and (d * cols * 4 <= (1600 << 10) or d == 8))

    def body(w_ref, g_ref, m_ref, v_ref, d_ref, nm_ref, nv_ref):
        gv = g_ref[...]
        mn = ADAM_B1 * m_ref[...] + (1.0 - ADAM_B1) * gv
        vn = ADAM_B2 * v_ref[...] + (1.0 - ADAM_B2) * (gv * gv)
        m_hat = mn / (1.0 - ADAM_B1 ** ADAM_STEP)
        v_hat = vn / (1.0 - ADAM_B2 ** ADAM_STEP)
        d_ref[...] = -ADAM_LR * (m_hat / (jnp.sqrt(v_hat) + ADAM_EPS) + ADAM_WD * w_ref[...])
        nm_ref[...] = mn
        nv_ref[...] = vn

    blk = pl.BlockSpec((rb, cols), lambda i: (i, 0))
    out = pl.pallas_call(body, out_shape=tuple(_SDS((rows, cols), f32) for _ in range(3)), grid=(rows // rb,),
                         in_specs=[blk] * 4, out_specs=(blk,) * 3, compiler_params=_cp(("parallel",)),
                         name=name)(to2(w), to2(g), to2(m), to2(v))
    return tuple(t.reshape(shape) for t in out)


def _sum_slots(a, *, name):
    p, n, c = a.shape
    rb = max(d for d in range(8, n + 1, 8) if n % d == 0 and (p * d * c * 4 <= (6 << 20) or d == 8))

    def body(a_ref, o_ref):
        acc = a_ref[0].astype(f32)
        for k in range(1, p):
            acc = acc + a_ref[k].astype(f32)
        o_ref[...] = acc

    return pl.pallas_call(body, out_shape=_SDS((n, c), f32), grid=(n // rb,),
                          in_specs=[pl.BlockSpec((p, rb, c), lambda i: (0, i, 0))],
                          out_specs=pl.BlockSpec((rb, c), lambda i: (i, 0)), compiler_params=_cp(("parallel",)),
                          name=name)(a)


def _add2(a, b, *, name):
    p, n, c = a.shape
    rb = max(d for d in range(8, n + 1, 8) if n % d == 0 and (d * c * 4 <= (2 << 20) or d == 8))

    def body(a_ref, b_ref, o_ref):
        o_ref[...] = a_ref[...] + b_ref[...]

    blk = pl.BlockSpec((1, rb, c), lambda s, i: (s, i, 0))
    return pl.pallas_call(body, out_shape=_SDS((p, n, c), f32), grid=(p, n // rb), in_specs=[blk, blk], out_specs=blk,
                          compiler_params=_cp(("parallel", "parallel")), name=name)(a, b)


def _rope_tables(pos, dim):
    half = dim // 2
    inv = ROPE_THETA ** (-jnp.arange(half, dtype=f32) / half)
    ang = pos.astype(f32)[:, None] * inv[None, :]
    c, s = jnp.cos(ang), jnp.sin(ang)
    z = jnp.zeros_like(c)
    pad = [jnp.zeros((pos.shape[0], LANES - dim), f32)] if dim < LANES else []
    return (jnp.concatenate([c, c] + pad, axis=1), jnp.concatenate([-s, z] + pad, axis=1),
            jnp.concatenate([z, s] + pad, axis=1))


def _rope(x, cos, sa, sb, half, transpose=False):
    if transpose:
        return x * cos + pltpu.roll(x * sa, half, 1) + pltpu.roll(x * sb, LANES - half, 1)
    return x * cos + pltpu.roll(x, LANES - half, 1) * sa + pltpu.roll(x, half, 1) * sb


def _rope_call(items, tables, half, transpose, *, name):
    s = items[0][0].shape[0]
    rb = _row_block(s)
    n = len(items)

    def body(*refs):
        cos, sa, sb = refs[n][...], refs[n + 1][...], refs[n + 2][...]
        for k in range(n):
            x_ref, o_ref = refs[k], refs[n + 3 + k]
            for j in range(items[k][1] // LANES):
                sl = slice(j * LANES, (j + 1) * LANES)
                o_ref[:, sl] = _rope(x_ref[:, sl], cos, sa, sb, half, transpose)

    in_specs = [pl.BlockSpec((rb, w), functools.partial(lambda i, cb: (i, cb), cb=cb)) for _, w, cb in items]
    in_specs += [pl.BlockSpec((rb, LANES), lambda i: (i, 0))] * 3
    out_specs = tuple(pl.BlockSpec((rb, w), lambda i: (i, 0)) for _, w, _ in items)
    return pl.pallas_call(
        body, out_shape=tuple(_SDS((s, w), f32) for _, w, _ in items), grid=(s // rb,), in_specs=in_specs,
        out_specs=out_specs, compiler_params=_cp(("parallel",)), name=name,
    )(*[a for a, _, _ in items], *tables)


def _attn_block(s):
    return _pick(s, (512, 256, 128))


def _lower_mask(b, strict):
    r, c = _iota((b, b), 0), _iota((b, b), 1)
    return (c < r) if strict else (c <= r)


def _pick_lane(block, h):
    return jnp.sum(jnp.where(_iota(block.shape, 1) == h, block, 0.0), axis=1, keepdims=True)


def _head_bias(cum_blk, g, j, hp):
    if hp == N_HEADS:
        return cum_blk[:, j:j + 1]
    return _pick_lane(cum_blk, g * hp + j)


def _attn_fwd(q, k, v, qcol, kcol, vcol, dq, cum, cum_t, *, scale, hp, name):
    s = q.shape[0]
    b = _attn_block(s)
    nq = s // b
    has_bias = cum is not None
    assert qcol % hp == 0 and kcol % hp == 0 and vcol % hp == 0

    def body(*refs):
        if has_bias:
            q_ref, k_ref, v_ref, cum_ref, cumt_ref, o_ref, lse_ref = refs
        else:
            q_ref, k_ref, v_ref, o_ref, lse_ref = refs
        g, i = pl.program_id(0), pl.program_id(1)
        qs = [q_ref[:, j * dq:(j + 1) * dq].astype(_MXU) for j in range(hp)]
        cqs = [_head_bias(cum_ref[...], g, j, hp) for j in range(hp)] if has_bias else None

        def chunk(c, carry, diag):
            st = pl.multiple_of(c * b, b)
            mask = _lower_mask(b, False) if diag else None
            out = []
            for j in range(hp):
                m, l, acc = carry[j]
                z = _mm_nt(qs[j], k_ref[pl.ds(st, b), j * dq:(j + 1) * dq]) * scale
                if has_bias:
                    z = z + cqs[j] - cumt_ref[j, c]
                if diag:
                    z = jnp.where(mask, z, NEG_INF)
                m_new = jnp.maximum(m, jnp.max(z, axis=1, keepdims=True))
                p = jnp.exp(z - m_new)
                if diag:
                    p = jnp.where(mask, p, 0.0)
                alpha = jnp.exp(m - m_new)
                l = alpha * l + jnp.sum(p, axis=1, keepdims=True)
                acc = alpha * acc + _mm(p, v_ref[pl.ds(st, b), j * HEAD_DIM:(j + 1) * HEAD_DIM])
                out.append((m_new, l, acc))
            return tuple(out)

        init = tuple((jnp.full((b, 1), NEG_INF, f32), jnp.zeros((b, 1), f32), jnp.zeros((b, HEAD_DIM), f32))
                     for _ in range(hp))
        carry = lax.fori_loop(0, i, lambda c, cr: chunk(c, cr, False), init)
        for j, (m, l, acc) in enumerate(chunk(i, carry, True)):
            o_ref[:, j * HEAD_DIM:(j + 1) * HEAD_DIM] = acc / l
            lse_ref[j] = m + jnp.log(l)

    in_specs = [pl.BlockSpec((b, hp * dq), lambda g, i: (i, qcol // hp + g)),
                pl.BlockSpec((s, hp * dq), lambda g, i: (0, kcol // hp + g)),
                pl.BlockSpec((s, hp * HEAD_DIM), lambda g, i: (0, vcol // hp + g))]
    args = [q, k, v]
    if has_bias:
        in_specs += [pl.BlockSpec((b, LANES), lambda g, i: (i, 0)),
                     pl.BlockSpec((hp, nq, 1, b), lambda g, i: (g, 0, 0, 0))]
        args += [cum, cum_t]
    return pl.pallas_call(
        body, out_shape=(_SDS((s, N_HEADS * HEAD_DIM), f32), _SDS((N_HEADS, s, 1), f32)), grid=(N_HEADS // hp, nq),
        in_specs=in_specs,
        out_specs=(pl.BlockSpec((b, hp * HEAD_DIM), lambda g, i: (i, g)),
                   pl.BlockSpec((hp, b, 1), lambda g, i: (g, i, 0))),
        compiler_params=_cp(("parallel", "parallel")), name=name,
    )(*args)


def _attn_bwd(q, k, v, qcol, kcol, vcol, dq, do, o, lse, cum, cum_t, *, scale, hp, name):
    s = q.shape[0]
    b = _attn_block(s)
    nq = s // b
    has_bias = cum is not None
    assert qcol % hp == 0 and kcol % hp == 0 and vcol % hp == 0
    hd = lambda j: slice(j * HEAD_DIM, (j + 1) * HEAD_DIM)
    hq = lambda j: slice(j * dq, (j + 1) * dq)

    def body(*refs):
        if has_bias:
            (q_ref, k_ref, v_ref, do_ref, o_ref, lse_ref, cum_ref, cumt_ref, dq_ref, dk_ref, dv_ref, dck_ref,
             p_sc, dp_sc) = refs
        else:
            q_ref, k_ref, v_ref, do_ref, o_ref, lse_ref, dq_ref, dk_ref, dv_ref = refs
        g, i = pl.program_id(0), pl.program_id(1)

        @pl.when(i == 0)
        def _():
            dk_ref[...] = jnp.zeros_like(dk_ref)
            dv_ref[...] = jnp.zeros_like(dv_ref)
            if has_bias:
                dck_ref[...] = jnp.zeros_like(dck_ref)

        qs = [q_ref[:, hq(j)].astype(_MXU) for j in range(hp)]
        dos = [do_ref[:, hd(j)].astype(_MXU) for j in range(hp)]
        lses = [lse_ref[j] for j in range(hp)]
        cqs = [_head_bias(cum_ref[...], g, j, hp) for j in range(hp)] if has_bias else None

        def probs(j, c, diag):
            st = pl.multiple_of(c * b, b)
            z = _mm_nt(qs[j], k_ref[pl.ds(st, b), hq(j)]) * scale
            if has_bias:
                z = z + cqs[j] - cumt_ref[j, c]
            p = jnp.exp(z - lses[j])
            if diag:
                p = jnp.where(_lower_mask(b, False), p, 0.0)
            return p, _mm_nt(dos[j], v_ref[pl.ds(st, b), hd(j)])

        if has_bias:
            def first(c, accs, diag):
                out = []
                for j in range(hp):
                    p, dp = probs(j, c, diag)
                    p_sc[j, c] = p
                    dp_sc[j, c] = dp
                    out.append(accs[j] + jnp.sum(p * dp, axis=1, keepdims=True))
                return tuple(out)

            deltas = lax.fori_loop(0, i, lambda c, a: first(c, a, False),
                                   tuple(jnp.zeros((b, 1), f32) for _ in range(hp)))
            deltas = first(i, deltas, True)
        else:
            deltas = [jnp.sum(do_ref[:, hd(j)] * o_ref[:, hd(j)], axis=1, keepdims=True) for j in range(hp)]

        def chunk(c, dq_accs, diag):
            st = pl.multiple_of(c * b, b)
            out = []
            for j in range(hp):
                p, dp = (p_sc[j, c], dp_sc[j, c]) if has_bias else probs(j, c, diag)
                ds = p * (dp - deltas[j])
                dk_ref[pl.ds(st, b), hq(j)] += _mm_tn(ds, qs[j]) * scale
                dv_ref[pl.ds(st, b), hd(j)] += _mm_tn(p, dos[j])
                if has_bias:
                    dck_ref[j, c] += -jnp.sum(ds, axis=0, keepdims=True)
                out.append(dq_accs[j] + _mm(ds, k_ref[pl.ds(st, b), hq(j)]))
            return tuple(out)

        accs = lax.fori_loop(0, i, lambda c, a: chunk(c, a, False), tuple(jnp.zeros((b, dq), f32) for _ in range(hp)))
        for j, acc in enumerate(chunk(i, accs, True)):
            dq_ref[:, hq(j)] = acc * scale

    rowq = pl.BlockSpec((b, hp * HEAD_DIM), lambda g, i: (i, g))
    in_specs = [pl.BlockSpec((b, hp * dq), lambda g, i: (i, qcol // hp + g)),
                pl.BlockSpec((s, hp * dq), lambda g, i: (0, kcol // hp + g)),
                pl.BlockSpec((s, hp * HEAD_DIM), lambda g, i: (0, vcol // hp + g)), rowq, rowq,
                pl.BlockSpec((hp, b, 1), lambda g, i: (g, i, 0))]
    args = [q, k, v, do, o, lse]
    out_shape = [_SDS((s, N_HEADS * dq), f32), _SDS((s, N_HEADS * dq), f32), _SDS((s, N_HEADS * HEAD_DIM), f32)]
    out_specs = [pl.BlockSpec((b, hp * dq), lambda g, i: (i, g)), pl.BlockSpec((s, hp * dq), lambda g, i: (0, g)),
                 pl.BlockSpec((s, hp * HEAD_DIM), lambda g, i: (0, g))]
    if has_bias:
        in_specs += [pl.BlockSpec((b, LANES), lambda g, i: (i, 0)),
                     pl.BlockSpec((hp, nq, 1, b), lambda g, i: (g, 0, 0, 0))]
        args += [cum, cum_t]
        out_shape.append(_SDS((N_HEADS, nq, 1, b), f32))
        out_specs.append(pl.BlockSpec((hp, nq, 1, b), lambda g, i: (g, 0, 0, 0)))
    return pl.pallas_call(
        body, out_shape=tuple(out_shape), grid=(N_HEADS // hp, nq), in_specs=in_specs, out_specs=tuple(out_specs),
        scratch_shapes=[pltpu.VMEM((hp, nq, b, b), f32)] * 2 if has_bias else [],
        compiler_params=_cp(("parallel", "arbitrary")), name=name,
    )(*args)


def _tri(b, kind):
    r, c = _iota((b, b), 0), _iota((b, b), 1)
    cond = {"row_gt": r > c, "row_lt": r < c, "row_ge": r >= c, "row_le": r <= c}[kind]
    return jnp.where(cond, 1.0, 0.0).astype(_MXU)


def _log_keep(z):
    return -(jnp.maximum(z, 0.0) + jnp.log1p(jnp.exp(-jnp.abs(z))))


def _sb_fwd(z_all, *, hp, name):
    s = z_all.shape[0]
    b = _attn_block(s)
    nq = s // b
    scale = HEAD_DIM ** -0.5
    qcol, kcol, vcol = (_AL[n] // (hp * HEAD_DIM) for n in ("sb_q", "sb_k", "sb_v"))
    hd = lambda j: slice(j * HEAD_DIM, (j + 1) * HEAD_DIM)

    def body(q_ref, k_ref, v_ref, o_ref):
        i = pl.program_id(1)
        qs = [q_ref[:, hd(j)].astype(_MXU) for j in range(hp)]
        upper = _tri(b, "row_gt")

        def chunk(c, carry, diag):
            st = pl.multiple_of(c * b, b)
            mask = _lower_mask(b, True) if diag else None
            out = []
            for j in range(hp):
                rsum, acc = carry[j]
                z = _mm_nt(qs[j], k_ref[pl.ds(st, b), hd(j)]) * scale
                lk = _log_keep(z)
                if diag:
                    lk = jnp.where(mask, lk, 0.0)
                a = z + lk + _mm_split(lk, upper) + rsum
                if diag:
                    a = jnp.where(mask, a, NEG_INF)
                acc = acc + _mm(jnp.exp(a), v_ref[pl.ds(st, b), hd(j)])
                out.append((rsum + jnp.sum(lk, axis=1, keepdims=True), acc))
            return tuple(out)

        init = tuple((jnp.zeros((b, 1), f32), jnp.zeros((b, HEAD_DIM), f32)) for _ in range(hp))
        carry = lax.fori_loop(0, i, lambda t, cr: chunk(i - 1 - t, cr, False), chunk(i, init, True))
        for j in range(hp):
            o_ref[:, hd(j)] = carry[j][1]

    w = hp * HEAD_DIM
    return pl.pallas_call(
        body, out_shape=_SDS((s, GROUP), f32), grid=(N_HEADS // hp, nq),
        in_specs=[pl.BlockSpec((b, w), lambda g, i: (i, qcol + g)), pl.BlockSpec((s, w), lambda g, i: (0, kcol + g)),
                  pl.BlockSpec((s, w), lambda g, i: (0, vcol + g))],
        out_specs=pl.BlockSpec((b, w), lambda g, i: (i, g)),
        compiler_params=_cp(("parallel", "parallel")), name=name,
    )(z_all, z_all, z_all)


def _sb_bwd(z_all, do, *, hp, name):
    s = z_all.shape[0]
    b = _attn_block(s)
    nq = s // b
    scale = HEAD_DIM ** -0.5
    qcol, kcol, vcol = (_AL[n] // (hp * HEAD_DIM) for n in ("sb_q", "sb_k", "sb_v"))
    hd = lambda j: slice(j * HEAD_DIM, (j + 1) * HEAD_DIM)

    def body(q_ref, k_ref, v_ref, do_ref, dq_ref, dk_ref, dv_ref, z_sc, lk_sc, r_sc):
        i = pl.program_id(1)

        @pl.when(i == 0)
        def _():
            dk_ref[...] = jnp.zeros_like(dk_ref)
            dv_ref[...] = jnp.zeros_like(dv_ref)

        qs = [q_ref[:, hd(j)].astype(_MXU) for j in range(hp)]
        dos = [do_ref[:, hd(j)].astype(_MXU) for j in range(hp)]
        upper = _tri(b, "row_gt")
        lower = _tri(b, "row_lt")

        def scores(c, rsums, diag):
            st = pl.multiple_of(c * b, b)
            out = []
            for j in range(hp):
                z = _mm_nt(qs[j], k_ref[pl.ds(st, b), hd(j)]) * scale
                lk = _log_keep(z)
                if diag:
                    lk = jnp.where(_lower_mask(b, True), lk, 0.0)
                z_sc[j, c] = z
                lk_sc[j, c] = lk
                r_sc[j, c] = _mm_split(lk, upper) + rsums[j]
                out.append(rsums[j] + jnp.sum(lk, axis=1, keepdims=True))
            return tuple(out)

        rsums = scores(i, tuple(jnp.zeros((b, 1), f32) for _ in range(hp)), True)
        lax.fori_loop(0, i, lambda t, r: scores(i - 1 - t, r, False), rsums)

        def grads(c, carry, diag):
            st = pl.multiple_of(c * b, b)
            mask = _lower_mask(b, True) if diag else None
            out = []
            for j in range(hp):
                psum, dq_acc = carry[j]
                z, lk = z_sc[j, c], lk_sc[j, c]
                lb = z + lk
                a = lb + r_sc[j, c]
                if diag:
                    a = jnp.where(mask, a, NEG_INF)
                w = jnp.exp(a)
                e = _mm_nt(dos[j], v_ref[pl.ds(st, b), hd(j)]) * w
                before = _mm_split(e, lower) + psum
                dz = e * jnp.exp(lk) - before * jnp.exp(lb)
                if diag:
                    dz = jnp.where(mask, dz, 0.0)
                dk_ref[pl.ds(st, b), hd(j)] += _mm_tn(dz, qs[j]) * scale
                dv_ref[pl.ds(st, b), hd(j)] += _mm_tn(w, dos[j])
                out.append((psum + jnp.sum(e, axis=1, keepdims=True), dq_acc + _mm(dz, k_ref[pl.ds(st, b), hd(j)])))
            return tuple(out)

        init = tuple((jnp.zeros((b, 1), f32), jnp.zeros((b, HEAD_DIM), f32)) for _ in range(hp))
        carry = grads(i, lax.fori_loop(0, i, lambda c, cr: grads(c, cr, False), init), True)
        for j in range(hp):
            dq_ref[:, hd(j)] = carry[j][1] * scale

    w = hp * HEAD_DIM
    blk = pl.BlockSpec((b, w), lambda g, i: (i, g))
    full = pl.BlockSpec((s, w), lambda g, i: (0, g))
    return pl.pallas_call(
        body, out_shape=tuple(_SDS((s, GROUP), f32) for _ in range(3)), grid=(N_HEADS // hp, nq),
        in_specs=[pl.BlockSpec((b, w), lambda g, i: (i, qcol + g)), pl.BlockSpec((s, w), lambda g, i: (0, kcol + g)),
                  pl.BlockSpec((s, w), lambda g, i: (0, vcol + g)), blk],
        out_specs=(blk, full, full),
        scratch_shapes=[pltpu.VMEM((hp, nq, b, b), f32)] * 3,
        compiler_params=_cp(("parallel", "arbitrary")), name=name,
    )(z_all, z_all, z_all, do)


def _split3_left(t, x):
    hi = x.astype(_MXU)
    r1 = x - hi.astype(f32)
    mid = r1.astype(_MXU)
    lo = (r1 - mid.astype(f32)).astype(_MXU)
    dot = functools.partial(jnp.dot, preferred_element_type=f32)
    return dot(t, hi) + dot(t, mid) + dot(t, lo)


def _split3_right(x, t):
    hi = x.astype(_MXU)
    r1 = x - hi.astype(f32)
    mid = r1.astype(_MXU)
    lo = (r1 - mid.astype(f32)).astype(_MXU)
    dot = functools.partial(jnp.dot, preferred_element_type=f32)
    return dot(hi, t) + dot(mid, t) + dot(lo, t)


def _fox_cum_fwd(z_all, bias, *, name):
    s = z_all.shape[0]
    b = _attn_block(s)
    fcol = _AL["fox_f"] // LANES

    def body(f_ref, b_ref, cum_ref, cumt_ref, carry_ref):
        i = pl.program_id(0)

        @pl.when(i == 0)
        def _():
            carry_ref[...] = jnp.zeros_like(carry_ref)

        u = f_ref[...] + b_ref[...]
        lf = jnp.minimum(u, 0.0) - jnp.log1p(jnp.exp(-jnp.abs(u)))
        cum = _split3_left(_tri(b, "row_ge"), lf) + carry_ref[...]
        cum_ref[...] = cum
        cumt_ref[...] = cum.T[0:8, :]
        carry_ref[...] = cum_ref[b - 1:b, :]

    return pl.pallas_call(
        body, out_shape=(_SDS((s, LANES), f32), _SDS((8, s), f32)), grid=(s // b,),
        in_specs=[pl.BlockSpec((b, LANES), lambda i: (i, fcol)), pl.BlockSpec((1, LANES), lambda i: (0, 0))],
        out_specs=(pl.BlockSpec((b, LANES), lambda i: (i, 0)), pl.BlockSpec((8, b), lambda i: (0, i))),
        scratch_shapes=[pltpu.VMEM((1, LANES), f32)], compiler_params=_cp(("arbitrary",)), name=name,
    )(z_all, bias)


def _fox_cum_bwd(z_all, bias, dcum_t, *, name):
    s = z_all.shape[0]
    b = _attn_block(s)
    nb = s // b
    fcol = _AL["fox_f"] // LANES

    def body(f_ref, b_ref, dc_ref, df_ref, db_ref, carry_ref):
        i = pl.program_id(0)

        @pl.when(i == 0)
        def _():
            carry_ref[...] = jnp.zeros_like(carry_ref)
            db_ref[...] = jnp.zeros_like(db_ref)

        dc = dc_ref[...]
        rev = _split3_right(dc, _tri(b, "row_ge")) + carry_ref[...]
        carry_ref[...] = carry_ref[...] + jnp.sum(dc, axis=1, keepdims=True)
        dlf = jnp.concatenate([rev, jnp.zeros((LANES - 8, b), f32)], axis=0).T
        u = f_ref[...] + b_ref[...]
        df = jnp.where(_iota((b, LANES), 1) < N_HEADS, dlf * (1.0 - _sigmoid(u)), 0.0)
        df_ref[...] = df
        db_ref[...] += jnp.sum(df, axis=0, keepdims=True)

    return pl.pallas_call(
        body, out_shape=(_SDS((s, LANES), f32), _SDS((1, LANES), f32)), grid=(nb,),
        in_specs=[pl.BlockSpec((b, LANES), lambda i: (nb - 1 - i, fcol)), pl.BlockSpec((1, LANES), lambda i: (0, 0)),
                  pl.BlockSpec((8, b), lambda i: (0, nb - 1 - i))],
        out_specs=(pl.BlockSpec((b, LANES), lambda i: (nb - 1 - i, 0)), pl.BlockSpec((1, LANES), lambda i: (0, 0))),
        scratch_shapes=[pltpu.VMEM((8, 1), f32)], compiler_params=_cp(("arbitrary",)), name=name,
    )(z_all, bias, dcum_t)


MLA_QW = 2 * LANES


def _rms_rows(x):
    r = lax.rsqrt(jnp.mean(x * x, axis=-1, keepdims=True) + RMS_EPS)
    return x * r, r


def _mla_prep_fwd(z_all, gq, gkv, wuq, wk, wv, tables, *, name):
    s = z_all.shape[0]
    rb = _row_block(s)
    half = MLA_ROPE // 2

    def body(cq_ref, ckv_ref, kr_ref, gq_ref, gkv_ref, wuq_ref, wk_ref, wv_ref, cos_ref, sa_ref, sb_ref,
             q_ref, k_ref, v_ref):
        cos, sa, sb = cos_ref[...], sa_ref[...], sb_ref[...]
        xh, _ = _rms_rows(cq_ref[...])
        qp = _mm(xh * gq_ref[...], wuq_ref[...])
        kh, _ = _rms_rows(ckv_ref[...])
        nkv = kh * gkv_ref[...]
        kn = _mm(nkv, wk_ref[...])
        v_ref[...] = _mm(nkv, wv_ref[...])
        kr = _rope(kr_ref[...], cos, sa, sb, half)
        for h in range(N_HEADS):
            lo, mid, hi = h * MLA_QW, h * MLA_QW + LANES, (h + 1) * MLA_QW
            q_ref[:, lo:mid] = qp[:, lo:mid]
            q_ref[:, mid:hi] = _rope(qp[:, mid:hi], cos, sa, sb, half)
            k_ref[:, lo:mid] = kn[:, h * LANES:(h + 1) * LANES]
            k_ref[:, mid:hi] = kr

    row = lambda w, cb: pl.BlockSpec((rb, w), lambda i: (i, cb))
    whole = lambda a: pl.BlockSpec(a.shape, lambda i: (0,) * a.ndim)
    return pl.pallas_call(
        body, out_shape=(_SDS((s, N_HEADS * MLA_QW), f32), _SDS((s, N_HEADS * MLA_QW), f32), _SDS((s, GROUP), f32)),
        grid=(s // rb,),
        in_specs=[row(MLA_Q_RANK, _AL["mla_cq"] // MLA_Q_RANK), row(LANES, _AL["mla_ckv"] // LANES),
                  row(LANES, _AL["mla_k_rope"] // LANES), whole(gq), whole(gkv), whole(wuq), whole(wk), whole(wv),
                  row(LANES, 0), row(LANES, 0), row(LANES, 0)],
        out_specs=(row(N_HEADS * MLA_QW, 0), row(N_HEADS * MLA_QW, 0), row(GROUP, 0)),
        compiler_params=_cp(("parallel",)), name=name,
    )(z_all, z_all, z_all, gq, gkv, wuq, wk, wv, *tables)


def _mla_prep_bwd(z_all, gq, gkv, wuq, wk, wv, tables, dq_cat, dk_cat, dv, *, name):
    s = z_all.shape[0]
    rb = _row_block(s)
    half = MLA_ROPE // 2

    def body(cq_ref, ckv_ref, gq_ref, gkv_ref, wuq_ref, wk_ref, wv_ref, cos_ref, sa_ref, sb_ref, dq_ref, dk_ref,
             dv_ref, dcq_ref, dckv_ref, dkr_ref, dwuq_ref, dwk_ref, dwv_ref, dgq_ref, dgkv_ref):
        i = pl.program_id(0)

        @pl.when(i == 0)
        def _():
            for r in (dwuq_ref, dwk_ref, dwv_ref, dgq_ref, dgkv_ref):
                r[...] = jnp.zeros_like(r)

        cos, sa, sb = cos_ref[...], sa_ref[...], sb_ref[...]
        parts, knp = [], []
        dkr = jnp.zeros((rb, LANES), f32)
        for h in range(N_HEADS):
            lo, mid, hi = h * MLA_QW, h * MLA_QW + LANES, (h + 1) * MLA_QW
            parts += [dq_ref[:, lo:mid], _rope(dq_ref[:, mid:hi], cos, sa, sb, half, transpose=True)]
            knp.append(dk_ref[:, lo:mid])
            dkr = dkr + _rope(dk_ref[:, mid:hi], cos, sa, sb, half, transpose=True)
        dkr_ref[...] = dkr
        dqp = jnp.concatenate(parts, axis=1)
        dkn = jnp.concatenate(knp, axis=1)
        dvv = dv_ref[...]

        def norm_bwd(x_ref, g_ref, w_pairs, dx_ref, dg_ref):
            xh, r = _rms_rows(x_ref[...])
            nx = xh * g_ref[...]
            dn = jnp.zeros_like(xh)
            for w_ref, dw_ref, dy in w_pairs:
                dw_ref[...] += _mm_tn(nx, dy)
                dn = dn + _mm_nt(dy, w_ref[...])
            dxh = dn * g_ref[...]
            dx_ref[...] = r * (dxh - xh * jnp.mean(dxh * xh, axis=-1, keepdims=True))
            dg_ref[...] += jnp.sum(dn * xh, axis=0, keepdims=True)

        norm_bwd(cq_ref, gq_ref, [(wuq_ref, dwuq_ref, dqp)], dcq_ref, dgq_ref)
        norm_bwd(ckv_ref, gkv_ref, [(wk_ref, dwk_ref, dkn), (wv_ref, dwv_ref, dvv)], dckv_ref, dgkv_ref)

    row = lambda w, cb: pl.BlockSpec((rb, w), lambda i: (i, cb))
    whole = lambda a: pl.BlockSpec(a.shape, lambda i: (0,) * a.ndim)
    return pl.pallas_call(
        body,
        out_shape=(_SDS((s, MLA_Q_RANK), f32), _SDS((s, LANES), f32), _SDS((s, LANES), f32), _SDS(wuq.shape, f32),
                   _SDS(wk.shape, f32), _SDS(wv.shape, f32), _SDS(gq.shape, f32), _SDS(gkv.shape, f32)),
        grid=(s // rb,),
        in_specs=[row(MLA_Q_RANK, _AL["mla_cq"] // MLA_Q_RANK), row(LANES, _AL["mla_ckv"] // LANES), whole(gq),
                  whole(gkv), whole(wuq), whole(wk), whole(wv), row(LANES, 0), row(LANES, 0), row(LANES, 0),
                  row(N_HEADS * MLA_QW, 0), row(N_HEADS * MLA_QW, 0), row(GROUP, 0)],
        out_specs=(row(MLA_Q_RANK, 0), row(LANES, 0), row(LANES, 0), whole(wuq), whole(wk), whole(wv), whole(gq),
                   whole(gkv)),
        compiler_params=_cp(("arbitrary",)), name=name,
    )(z_all, z_all, gq, gkv, wuq, wk, wv, *tables, dq_cat, dk_cat, dv)


def _silu_grad(x):
    sg = _sigmoid(x)
    return sg * (1.0 + x * (1.0 - sg))


def _nsa_cmp_fwd(ra, rb_, pos, w1, w2, tables, *, name):
    nr = ra.shape[1]
    hw = ra.shape[2]

    def body(ra_ref, rb_ref, pos_ref, w1_ref, w2_ref, cos_ref, sa_ref, sb_ref, out_ref, hp_ref):
        for k in range(2):
            xa = ra_ref[k] + pos_ref[k, :, 0:hw]
            xb = rb_ref[k] + pos_ref[k, :, hw:2 * hw]
            hp = _mm(xa, w1_ref[k, 0:hw, :]) + _mm(xb, w1_ref[k, hw:2 * hw, :])
            hp_ref[k] = hp
            out = _mm(hp * _sigmoid(hp), w2_ref[k])
            if k == 0:
                out = _rope(out, cos_ref[...], sa_ref[...], sb_ref[...], HEAD_DIM // 2)
            out_ref[k] = out

    return pl.pallas_call(body, out_shape=(_SDS((2, nr, HEAD_DIM), f32), _SDS((2, nr, HEAD_DIM), f32)),
                          compiler_params=_cp(), name=name)(ra, rb_, pos, w1, w2, *tables)


def _nsa_cmp_bwd(ra, rb_, pos, w1, w2, tables, hp, dout, *, name):
    nr = ra.shape[1]
    hw = ra.shape[2]

    def body(ra_ref, rb_ref, pos_ref, w1_ref, w2_ref, cos_ref, sa_ref, sb_ref, hp_ref, do_ref,
             dxa_ref, dxb_ref, dw1_ref, dw2_ref):
        for k in range(2):
            d_out = do_ref[k]
            if k == 0:
                d_out = _rope(d_out, cos_ref[...], sa_ref[...], sb_ref[...], HEAD_DIM // 2, transpose=True)
            hpv = hp_ref[k]
            dw2_ref[k] = _mm_tn(hpv * _sigmoid(hpv), d_out)
            dhp = _mm_nt(d_out, w2_ref[k]) * _silu_grad(hpv)
            xa = ra_ref[k] + pos_ref[k, :, 0:hw]
            xb = rb_ref[k] + pos_ref[k, :, hw:2 * hw]
            dw1_ref[k, 0:hw, :] = _mm_tn(xa, dhp)
            dw1_ref[k, hw:2 * hw, :] = _mm_tn(xb, dhp)
            dxa_ref[k] = _mm_nt(dhp, w1_ref[k, 0:hw, :])
            dxb_ref[k] = _mm_nt(dhp, w1_ref[k, hw:2 * hw, :])

    return pl.pallas_call(
        body, out_shape=(_SDS((2, nr, hw), f32), _SDS((2, nr, hw), f32), _SDS(w1.shape, f32), _SDS(w2.shape, f32)),
        compiler_params=_cp(), name=name)(ra, rb_, pos, w1, w2, *tables, hp, dout)


def _nsa_consts(s):
    b = _attn_block(s)
    nr = s // CMP_STRIDE
    n_cmp = (s - CMP_LEN) // CMP_STRIDE + 1
    n_sel = s // SEL_LEN
    cmp_start = np.arange(n_cmp) * CMP_STRIDE
    sel_start = np.arange(n_sel) * SEL_LEN
    overlap = np.clip(np.minimum(cmp_start[:, None] + CMP_LEN, sel_start[None, :] + SEL_LEN)
                      - np.maximum(cmp_start[:, None], sel_start[None, :]), 0, None)
    m2s = np.zeros((nr, LANES), np.float32)
    m2s[:n_cmp, :n_sel] = overlap / CMP_LEN
    e3 = np.zeros((s // b, LANES, b), np.float32)
    tok = np.arange(s)
    e3[tok // b, tok // SEL_LEN, tok % b] = 1.0
    return jnp.asarray(m2s, _MXU), jnp.asarray(e3, _MXU)


def _nsa_masks(i, b, d):
    qpos = i * b + _iota((b, b), 0)
    kpos = (i - d) * b + _iota((b, b), 1)
    return (kpos <= qpos) & (kpos > qpos - WINDOW)


def _nsa_fwd(qr, kvc, ksr, vs, kwr, vw, z_all, m2s, e3, *, name):
    s = qr.shape[0]
    b = _attn_block(s)
    nq = s // b
    nr = kvc.shape[1]
    n_sel = s // SEL_LEN
    top_n = min(SEL_TOPN, n_sel)
    nd = -(-WINDOW // b)
    scale = HEAD_DIM ** -0.5
    bcol = _AL["nsa_branch"] // LANES
    H = N_HEADS

    def body(q_ref, kvc_ref, ks_ref, vs_ref, kw_ref, vw_ref, br_ref, m2s_ref, e3_ref,
             o_ref, oc_ref, os_ref, ow_ref, st_ref, sel_ref, m_sc, l_sc, acc_sc):
        i = pl.program_id(0)
        lane = _iota((b, LANES), 1)
        hs = lambda h: slice(h * HEAD_DIM, (h + 1) * HEAD_DIM)

        cmp_mask = (CMP_STRIDE * _iota((b, nr), 1) + (CMP_LEN - 1)) <= (i * b + _iota((b, nr), 0))
        imp = jnp.zeros((b, LANES), f32)
        stats = jnp.zeros((b, LANES), f32)
        for h in range(H):
            zc = jnp.where(cmp_mask, _mm_nt(q_ref[:, hs(h)], kvc_ref[0]) * scale, NEG_INF)
            m = jnp.max(zc, axis=1, keepdims=True)
            p = jnp.where(cmp_mask, jnp.exp(zc - m), 0.0)
            l = jnp.sum(p, axis=1, keepdims=True)
            some = l > 0.0
            lsafe = jnp.where(some, l, 1.0)
            pc = p * jnp.where(some, 1.0 / lsafe, 0.0)
            oc_ref[:, hs(h)] = _mm(pc, kvc_ref[1])
            imp = imp + _mm(pc, m2s_ref[...])
            stats = jnp.where(lane == h, jnp.where(some, m + jnp.log(lsafe), 0.0), stats)

        cur = jnp.right_shift(i * b + _iota((b, LANES), 0), int(math.log2(SEL_LEN)))
        forced = (lane == 0) | (lane == cur) | (lane == cur - 1)
        score = jnp.where(lane <= cur, jnp.where(forced, FORCED_BONUS, imp), NEG_INF)
        score = jnp.where(lane < n_sel, score, -3e38)
        rank = jnp.zeros((b, LANES), f32)
        for j in range(n_sel):
            col = score[:, j:j + 1]
            rank = rank + jnp.where(col > score, 1.0, jnp.where(col == score, jnp.where(lane > j, 1.0, 0.0), 0.0))
        sel = jnp.where(lane < n_sel, jnp.where(rank < top_n, 1.0, 0.0), 0.0)
        sel_ref[...] = sel
        sel_b = sel.astype(_MXU)

        def reset():
            m_sc[...] = jnp.full(m_sc.shape, NEG_INF, f32)
            l_sc[...] = jnp.zeros_like(l_sc)
            acc_sc[...] = jnp.zeros_like(acc_sc)

        def update(h, z, mask, vch):
            zm = jnp.where(mask, z, NEG_INF)
            m_old = m_sc[h]
            m_new = jnp.maximum(m_old, jnp.max(zm, axis=1, keepdims=True))
            p = jnp.where(mask, jnp.exp(zm - m_new), 0.0)
            alpha = jnp.exp(m_old - m_new)
            l_sc[h] = alpha * l_sc[h] + jnp.sum(p, axis=1, keepdims=True)
            acc_sc[h] = alpha * acc_sc[h] + _mm(p, vch)
            m_sc[h] = m_new

        def finish(out_ref, branch, stats):
            for h in range(H):
                out_ref[:, hs(h)] = acc_sc[h] / l_sc[h]
                stats = jnp.where(lane == 4 * branch + h, m_sc[h] + jnp.log(l_sc[h]), stats)
            return stats

        def sel_chunk(c, diag):
            st = pl.multiple_of(c * b, b)
            mask = _mm(sel_b, e3_ref[c]) > 0.5
            if diag:
                mask = mask & _lower_mask(b, False)
            kch, vch = ks_ref[pl.ds(st, b), :], vs_ref[pl.ds(st, b), :]
            for h in range(H):
                update(h, _mm_nt(q_ref[:, hs(h)], kch) * scale, mask, vch)

        reset()

        def sel_loop(c, carry):
            sel_chunk(c, False)
            return carry

        lax.fori_loop(0, i, sel_loop, 0)
        sel_chunk(i, True)
        stats = finish(os_ref, 1, stats)

        reset()
        for d in range(nd, -1, -1):
            @pl.when(i >= d)
            def _():
                st = pl.multiple_of((i - d) * b, b)
                mask = _nsa_masks(i, b, d)
                kch, vch = kw_ref[pl.ds(st, b), :], vw_ref[pl.ds(st, b), :]
                for h in range(H):
                    update(h, _mm_nt(q_ref[:, hs(h)], kch) * scale, mask, vch)
        stats = finish(ow_ref, 2, stats)
        st_ref[...] = stats

        g = _sigmoid(br_ref[...])
        for h in range(H):
            o_ref[:, hs(h)] = (g[:, 3 * h:3 * h + 1] * oc_ref[:, hs(h)] + g[:, 3 * h + 1:3 * h + 2] * os_ref[:, hs(h)]
                               + g[:, 3 * h + 2:3 * h + 3] * ow_ref[:, hs(h)])

    blk = lambda w: pl.BlockSpec((b, w), lambda i: (i, 0))
    whole = lambda a: pl.BlockSpec(a.shape, lambda i: (0,) * a.ndim)
    return pl.pallas_call(
        body, out_shape=tuple(_SDS((s, GROUP), f32) for _ in range(4)) + (_SDS((s, LANES), f32), _SDS((s, LANES), f32)),
        grid=(nq,),
        in_specs=[blk(GROUP), whole(kvc), whole(ksr), whole(vs), whole(kwr), whole(vw),
                  pl.BlockSpec((b, LANES), lambda i: (i, bcol)), whole(m2s), whole(e3)],
        out_specs=(blk(GROUP),) * 4 + (blk(LANES), blk(LANES)),
        scratch_shapes=[pltpu.VMEM((H, b, 1), f32), pltpu.VMEM((H, b, 1), f32), pltpu.VMEM((H, b, HEAD_DIM), f32)],
        compiler_params=_cp(("parallel",)), name=name,
    )(qr, kvc, ksr, vs, kwr, vw, z_all, m2s, e3)


def _nsa_bwd(do, qr, kvc, ksr, vs, kwr, vw, z_all, oc, os_, ow, stats, sel, e3, *, name):
    s = qr.shape[0]
    b = _attn_block(s)
    nq = s // b
    nr = kvc.shape[1]
    nd = -(-WINDOW // b)
    scale = HEAD_DIM ** -0.5
    bcol = _AL["nsa_branch"] // LANES
    H = N_HEADS

    def body(do_ref, q_ref, kvc_ref, ks_ref, vs_ref, kw_ref, vw_ref, br_ref, oc_ref, os_ref, ow_ref, st_ref, sel_ref,
             e3_ref, dq_ref, dbr_ref, dkvc_ref, dks_ref, dvs_ref, dkw_ref, dvw_ref, dob_sc, delta_sc, dq_sc):
        i = pl.program_id(0)

        @pl.when(i == 0)
        def _():
            for r in (dkvc_ref, dks_ref, dvs_ref, dkw_ref, dvw_ref):
                r[...] = jnp.zeros_like(r)

        lane = _iota((b, LANES), 1)
        hs = lambda h: slice(h * HEAD_DIM, (h + 1) * HEAD_DIM)
        g = _sigmoid(br_ref[...])
        stats = st_ref[...]
        dbr = jnp.zeros((b, LANES), f32)
        outs = (oc_ref, os_ref, ow_ref)
        for h in range(H):
            doh = do_ref[:, hs(h)]
            for j in range(3):
                gj = g[:, 3 * h + j:3 * h + j + 1]
                dgj = jnp.sum(doh * outs[j][:, hs(h)], axis=1, keepdims=True)
                dbr = jnp.where(lane == 3 * h + j, dgj * gj * (1.0 - gj), dbr)
                dob_sc[j, :, hs(h)] = gj * doh
                delta_sc[j, h] = gj * dgj
        dbr_ref[...] = dbr
        dq_sc[...] = jnp.zeros_like(dq_sc)

        def branch(j, h, z, mask, kch, vch):
            qh = q_ref[:, hs(h)]
            p = jnp.where(mask, jnp.exp(jnp.where(mask, z, NEG_INF) - stats[:, 4 * j + h:4 * j + h + 1]), 0.0)
            dob = dob_sc[j, :, hs(h)]
            ds = p * (_mm_nt(dob, vch) - delta_sc[j, h])
            dq_sc[:, hs(h)] += _mm(ds, kch) * scale
            return _mm_tn(ds, qh) * scale, _mm_tn(p, dob)

        cmp_mask = (CMP_STRIDE * _iota((b, nr), 1) + (CMP_LEN - 1)) <= (i * b + _iota((b, nr), 0))
        kc, vc = kvc_ref[0], kvc_ref[1]
        for h in range(H):
            dk, dv = branch(0, h, _mm_nt(q_ref[:, hs(h)], kc) * scale, cmp_mask, kc, vc)
            dkvc_ref[0] += dk
            dkvc_ref[1] += dv

        sel_b = sel_ref[...].astype(_MXU)

        def chunk(j, c, mask, k_ref, v_ref, dk_ref, dv_ref):
            st = pl.multiple_of(c * b, b)
            kch, vch = k_ref[pl.ds(st, b), :], v_ref[pl.ds(st, b), :]
            dk = jnp.zeros((b, HEAD_DIM), f32)
            dv = jnp.zeros((b, HEAD_DIM), f32)
            for h in range(H):
                dkh, dvh = branch(j, h, _mm_nt(q_ref[:, hs(h)], kch) * scale, mask, kch, vch)
                dk, dv = dk + dkh, dv + dvh
            dk_ref[pl.ds(st, b), :] += dk
            dv_ref[pl.ds(st, b), :] += dv

        def sel_chunk(c, diag):
            mask = _mm(sel_b, e3_ref[c]) > 0.5
            if diag:
                mask = mask & _lower_mask(b, False)
            chunk(1, c, mask, ks_ref, vs_ref, dks_ref, dvs_ref)

        def sel_loop(c, carry):
            sel_chunk(c, False)
            return carry

        lax.fori_loop(0, i, sel_loop, 0)
        sel_chunk(i, True)

        for d in range(nd, -1, -1):
            @pl.when(i >= d)
            def _():
                chunk(2, i - d, _nsa_masks(i, b, d), kw_ref, vw_ref, dkw_ref, dvw_ref)

        dq_ref[...] = dq_sc[...]

    blk = lambda w: pl.BlockSpec((b, w), lambda i: (i, 0))
    whole = lambda a: pl.BlockSpec(a.shape, lambda i: (0,) * a.ndim)
    stream = _SDS((s, HEAD_DIM), f32)
    return pl.pallas_call(
        body, out_shape=(_SDS((s, GROUP), f32), _SDS((s, LANES), f32), _SDS(kvc.shape, f32), stream, stream, stream,
                         stream),
        grid=(nq,),
        in_specs=[blk(GROUP), blk(GROUP), whole(kvc), whole(ksr), whole(vs), whole(kwr), whole(vw),
                  pl.BlockSpec((b, LANES), lambda i: (i, bcol)), blk(GROUP), blk(GROUP), blk(GROUP), blk(LANES),
                  blk(LANES), whole(e3)],
        out_specs=(blk(GROUP), blk(LANES), whole(kvc), whole(ksr), whole(vs), whole(kwr), whole(vw)),
        scratch_shapes=[pltpu.VMEM((3, b, GROUP), f32), pltpu.VMEM((3, H, b, 1), f32), pltpu.VMEM((b, GROUP), f32)],
        compiler_params=_cp(("arbitrary",)), name=name,
    )(do, qr, kvc, ksr, vs, kwr, vw, z_all, oc, os_, ow, stats, sel, e3)


def _seg(a, name):
    parts = [lax.slice_in_dim(a, off, off + hi - lo, axis=a.ndim - 1) for off, lo, hi in _PIECES[name]]
    return parts[0] if len(parts) == 1 else jnp.concatenate(parts, axis=a.ndim - 1)


def _to_groups(segs, rows, dtype):
    cols = []
    for s, grp in enumerate(_GROUPS):
        at = 0
        for n, lo, hi, off in sorted(grp, key=lambda t: t[3]):
            if off > at:
                cols.append(jnp.zeros((rows, off - at), dtype))
            cols.append(segs[n][:, lo:hi].astype(dtype))
            at = off + hi - lo
        if at < GROUP_W:
            cols.append(jnp.zeros((rows, GROUP_W - at), dtype))
    return jnp.concatenate(cols, axis=1)


def _piece_from_shard(w_t, s):
    grp = sorted(_GROUPS[s], key=lambda t: t[3])
    ends = [t[3] for t in grp[1:]] + [GROUP_W]
    rows = []
    for (n, lo, hi, off), end in zip(grp, ends):
        first = _ORIG[n] + lo - s * CHIP_COLS
        rows.append(jnp.pad(w_t[:, first:first + hi - lo], ((0, 0), (0, end - off - (hi - lo)), (0, 0))))
    return jnp.concatenate(rows, axis=1)


def _shard_from_piece(g, s):
    return jnp.concatenate([g[:, off:off + hi - lo] for n, lo, hi, off in
                            sorted(_GROUPS[s], key=lambda t: _ORIG[t[0]] + t[1])], axis=1)


def _from_groups(a):
    return jnp.concatenate([_seg(a, n) for n, _ in _SEGS], axis=1)


def _cmp_rows(tok):
    s = tok.shape[0]
    r = tok.reshape(s // CMP_STRIDE, CMP_STRIDE * HEAD_DIM)
    return r, jnp.concatenate([r[1:], jnp.zeros((1, r.shape[1]), r.dtype)], axis=0)


def _cmp_unrows(dxa, dxb):
    s = dxa.shape[0] * CMP_STRIDE
    return (dxa + jnp.concatenate([jnp.zeros((1, dxa.shape[1]), dxa.dtype), dxb[:-1]], axis=0)).reshape(s, HEAD_DIM)


_GATES = ("sb_gate", "nsa_gate", "fox_gate", "mla_gate")


def _layer_fwd(x, p, c, tag):
    s = x.shape[0]
    b = _attn_block(s)
    h = _rms_fwd(x, p["pre_g"], out_dtype=_MXU, name=f"prenorm_{tag}")
    z = _matmul(h, p["w_in"], "nt", bias=p["b_in"], name=f"inproj_{tag}")
    o_sb = _sb_fwd(z, hp=HP_FWD, name=f"sb_fwd_{tag}")

    qr, ksr, kwr = _rope_call([(z, GROUP, _AL["nsa_q"] // GROUP), (z, LANES, _AL["nsa_k_sel"] // LANES),
                               (z, LANES, _AL["nsa_k_win"] // LANES)], c["tabs128"], HEAD_DIM // 2, False,
                              name=f"nsa_rope_{tag}")
    (rak, rbk), (rav, rbv) = _cmp_rows(_seg(z, "nsa_k_cmp")), _cmp_rows(_seg(z, "nsa_v_cmp"))
    ra, rb_ = jnp.stack([rak, rav]), jnp.stack([rbk, rbv])
    kvc, hp = _nsa_cmp_fwd(ra, rb_, p["cmp_pos"], p["cmp_w1"], p["cmp_w2"], c["tabs_cmp"], name=f"nsa_cmp_{tag}")
    vs, vw = _seg(z, "nsa_v_sel"), _seg(z, "nsa_v_win")
    o_nsa, oc, os_, ow, stats, sel = _nsa_fwd(qr, kvc, ksr, vs, kwr, vw, z, c["m2s"], c["e3"], name=f"nsa_fwd_{tag}")

    cum, cum_t8 = _fox_cum_fwd(z, p["fox_bias"], name=f"fox_cum_{tag}")
    cum_t = cum_t8.reshape(8, s // b, 1, b)
    fox_v = _seg(z, "fox_v")
    fcols = (_AL["fox_q"] // HEAD_DIM, _AL["fox_k"] // HEAD_DIM, 0)
    o_fox, lse_fox = _attn_fwd(z, z, fox_v, *fcols, HEAD_DIM, cum, cum_t, scale=HEAD_DIM ** -0.5, hp=HP_FWD,
                               name=f"fox_fwd_{tag}")

    qcat, kcat, vm = _mla_prep_fwd(z, p["gq"], p["gkv"], p["wuq"], p["wk"], p["wv"], c["tabs64"],
                                   name=f"mla_prep_{tag}")
    o_mla, lse_mla = _attn_fwd(qcat, kcat, vm, 0, 0, 0, MLA_QW, None, None, scale=(MLA_NOPE + MLA_ROPE) ** -0.5,
                               hp=HP_BWD, name=f"mla_fwd_{tag}")

    o_all = jnp.concatenate([o_sb, o_nsa, o_fox, o_mla], axis=1)
    gates = jnp.concatenate([_seg(z, n) for n in _GATES], axis=1)
    mix = _gate_fwd(o_all, gates, name=f"gate_{tag}")
    u = _matmul(mix, p["w_out"], "nn", name=f"outproj_{tag}")
    y = _postnorm_fwd(u, p["post_g"], x, name=f"postnorm_{tag}")
    saved = dict(x=x, h=h, z=z, qr=qr, ksr=ksr, kwr=kwr, ra=ra, rb=rb_, kvc=kvc, hp=hp, vs=vs, vw=vw, oc=oc, os=os_,
                 ow=ow, stats=stats, sel=sel, cum=cum, cum_t=cum_t, fox_v=fox_v, o_fox=o_fox, lse_fox=lse_fox, qcat=qcat, kcat=kcat,
                 vm=vm, o_mla=o_mla, lse_mla=lse_mla, o_all=o_all, gates=gates, mix=mix, u=u)
    return y, saved


def _layer_bwd(dy, sv, p, c, tag):
    z = sv["z"]
    s = z.shape[0]
    du, dg_post = _rms_bwd(dy, sv["u"], p["post_g"], name=f"postnorm_bwd_{tag}")
    dmix = _matmul(du, p["w_out"], "nt", name=f"outproj_dx_{tag}")
    dw_out = _matmul(sv["mix"], du, "tn", name=f"outproj_dw_{tag}")
    do_all, dgates = _gate_bwd(dmix, sv["o_all"], sv["gates"], name=f"gate_bwd_{tag}")
    do_sb, do_nsa, do_fox, do_mla = (do_all[:, k * GROUP:(k + 1) * GROUP] for k in range(4))
    dgate = [dgates[:, k * GROUP:(k + 1) * GROUP] for k in range(4)]

    sb_dq, sb_dk, sb_dv = _sb_bwd(z, do_sb, hp=HP_BWD, name=f"sb_bwd_{tag}")

    n_dq, n_dbr, n_dkvc, n_dks, n_dvs, n_dkw, n_dvw = _nsa_bwd(
        do_nsa, sv["qr"], sv["kvc"], sv["ksr"], sv["vs"], sv["kwr"], sv["vw"], z, sv["oc"], sv["os"], sv["ow"],
        sv["stats"], sv["sel"], c["e3"], name=f"nsa_bwd_{tag}")
    dxa, dxb, dw1, dw2 = _nsa_cmp_bwd(sv["ra"], sv["rb"], p["cmp_pos"], p["cmp_w1"], p["cmp_w2"], c["tabs_cmp"],
                                      sv["hp"], n_dkvc, name=f"nsa_cmp_bwd_{tag}")
    n_dq, n_dks, n_dkw = _rope_call([(n_dq, GROUP, 0), (n_dks, LANES, 0), (n_dkw, LANES, 0)], c["tabs128"],
                                    HEAD_DIM // 2, True, name=f"nsa_rope_bwd_{tag}")
    dpos = _colsum(jnp.concatenate([dxa[0], dxb[0], dxa[1], dxb[1]], axis=1), name=f"nsa_dpos_{tag}")
    flat = CMP_LEN * HEAD_DIM

    fcols = (_AL["fox_q"] // HEAD_DIM, _AL["fox_k"] // HEAD_DIM, 0)
    f_dq, f_dk, f_dv, f_dck = _attn_bwd(z, z, sv["fox_v"], *fcols, HEAD_DIM, do_fox, sv["o_fox"], sv["lse_fox"],
                                        sv["cum"], sv["cum_t"], scale=HEAD_DIM ** -0.5, hp=HP_BWD,
                                        name=f"fox_bwd_{tag}")
    dcum_t = jnp.pad(f_dck.reshape(N_HEADS, s), ((0, 8 - N_HEADS), (0, 0)))
    f_df, f_dbias = _fox_cum_bwd(z, p["fox_bias"], dcum_t, name=f"fox_cum_bwd_{tag}")

    m_dq, m_dk, m_dv = _attn_bwd(sv["qcat"], sv["kcat"], sv["vm"], 0, 0, 0, MLA_QW, do_mla, sv["o_mla"], sv["lse_mla"],
                                 None, None, scale=(MLA_NOPE + MLA_ROPE) ** -0.5, hp=HP_BWD, name=f"mla_bwd_{tag}")
    m_dcq, m_dckv, m_dkr, m_dwuq, m_dwk, m_dwv, m_dgq, m_dgkv = _mla_prep_bwd(
        z, p["gq"], p["gkv"], p["wuq"], p["wk"], p["wv"], c["tabs64"], m_dq, m_dk, m_dv, name=f"mla_prep_bwd_{tag}")

    dz = _to_groups(dict(
        sb_q=sb_dq, sb_k=sb_dk, sb_v=sb_dv, sb_gate=dgate[0], nsa_q=n_dq, nsa_k_cmp=_cmp_unrows(dxa[0], dxb[0]),
        nsa_v_cmp=_cmp_unrows(dxa[1], dxb[1]), nsa_k_sel=n_dks, nsa_v_sel=n_dvs, nsa_k_win=n_dkw, nsa_v_win=n_dvw,
        nsa_branch=n_dbr, nsa_gate=dgate[1], fox_q=f_dq, fox_k=f_dk, fox_v=f_dv, fox_f=f_df, fox_gate=dgate[2],
        mla_cq=m_dcq, mla_ckv=m_dckv, mla_k_rope=m_dkr, mla_gate=dgate[3]), s, _MXU)
    dh = _matmul(dz, p["w_in"], "nn", name=f"inproj_dx_{tag}")
    dw_in = _matmul(dz, sv["h"], "tn", name=f"inproj_dw_{tag}")
    db = _colsum(dz, name=f"inproj_db_{tag}")
    dx, dg_pre = _rms_bwd(dh, sv["x"], p["pre_g"], res=dy, name=f"prenorm_bwd_{tag}")

    qw = MLA_NOPE + MLA_ROPE
    grads = {
        "pre_norm_g": dg_pre[0], "post_norm_g": dg_post[0], "w_in": dw_in, "b_in": _from_groups(db)[0],
        "w_out": dw_out, "fox_forget_bias": f_dbias[0, :N_HEADS],
        "nsa_cmp_pos_k": dpos[0, :flat].reshape(CMP_LEN, HEAD_DIM), "nsa_cmp_w1_k": dw1[0], "nsa_cmp_w2_k": dw2[0],
        "nsa_cmp_pos_v": dpos[0, flat:].reshape(CMP_LEN, HEAD_DIM), "nsa_cmp_w1_v": dw1[1], "nsa_cmp_w2_v": dw2[1],
        "mla_q_norm_g": m_dgq[0],
        "mla_w_uq": jnp.concatenate([m_dwuq[:, MLA_QW * h:MLA_QW * h + qw] for h in range(N_HEADS)], axis=1),
        "mla_kv_norm_g": m_dgkv[0],
        "mla_w_ukv": jnp.concatenate(sum([[m_dwk[:, LANES * h:LANES * (h + 1)], m_dwv[:, LANES * h:LANES * (h + 1)]]
                                          for h in range(N_HEADS)], []), axis=1),
    }
    return dx, grads


def _layer_params(w, l):
    b_in = w["b_in"][l].reshape(1, -1)
    b_segs = {n: b_in[:, _ORIG[n]:_ORIG[n] + wd] for n, wd in _SEGS}
    qw = MLA_NOPE + MLA_ROPE
    w_uq, w_ukv = w["mla_w_uq"][l], w["mla_w_ukv"][l]
    uq = []
    for h in range(N_HEADS):
        uq += [w_uq[:, qw * h:qw * (h + 1)], jnp.zeros((w_uq.shape[0], MLA_QW - qw), w_uq.dtype)]
    kw_ = 2 * LANES
    flat = CMP_LEN * HEAD_DIM
    return dict(
        pre_g=w["pre_norm_g"][l].reshape(1, -1), post_g=w["post_norm_g"][l].reshape(1, -1),
        w_in=w["w_in"][l], b_in=_to_groups(b_segs, 1, f32), w_out=w["w_out"][l],
        fox_bias=jnp.pad(w["fox_forget_bias"][l], (0, LANES - N_HEADS)).reshape(1, LANES),
        cmp_pos=jnp.stack([w["nsa_cmp_pos_k"][l].reshape(1, flat), w["nsa_cmp_pos_v"][l].reshape(1, flat)]),
        cmp_w1=jnp.stack([w["nsa_cmp_w1_k"][l], w["nsa_cmp_w1_v"][l]]),
        cmp_w2=jnp.stack([w["nsa_cmp_w2_k"][l], w["nsa_cmp_w2_v"][l]]),
        gq=w["mla_q_norm_g"][l].reshape(1, -1), gkv=w["mla_kv_norm_g"][l].reshape(1, -1),
        wuq=jnp.concatenate(uq, axis=1),
        wk=jnp.concatenate([w_ukv[:, kw_ * h:kw_ * h + LANES] for h in range(N_HEADS)], axis=1),
        wv=jnp.concatenate([w_ukv[:, kw_ * h + LANES:kw_ * (h + 1)] for h in range(N_HEADS)], axis=1),
    )


def _consts(s):
    pos = jnp.arange(s)
    m2s, e3 = _nsa_consts(s)
    return dict(tabs128=_rope_tables(pos, HEAD_DIM), tabs64=_rope_tables(pos, MLA_ROPE),
                tabs_cmp=_rope_tables(jnp.arange(s // CMP_STRIDE) * CMP_STRIDE + (CMP_LEN - 1), HEAD_DIM),
                m2s=m2s, e3=e3)


def _place():
    return lax.axis_index("x"), lax.axis_index("y"), lax.axis_index("c")


def _other_chips(x, y):
    return [(1 - x, y), (x, 1 - y), (1 - x, 1 - y)]


def _comm_call(body, out_shapes, n_sems, arrs, name):
    return pl.pallas_call(body, out_shape=tuple(out_shapes), in_specs=[_ANY] * len(arrs),
                          out_specs=tuple(_ANY for _ in out_shapes),
                          scratch_shapes=[pltpu.SemaphoreType.DMA((n_sems,)), pltpu.SemaphoreType.DMA((n_sems,))],
                          name=name)(*arrs)


def _gather_chips(arrs, *, name):
    n = len(arrs)

    def body(*refs):
        a_refs, out_refs, send_sems, recv_sems = refs[:n], refs[n:2 * n], refs[2 * n], refs[2 * n + 1]
        x, y, c = _place()
        me = 2 * x + y
        sibling = (x, y, 1 - c)
        chips = _other_chips(x, y)

        def copy(j, k, src, dst, to):
            return pltpu.make_async_remote_copy(src, dst, send_sems.at[6 * j + k], recv_sems.at[6 * j + k],
                                                device_id=to, device_id_type=_MESH)

        first = [copy(j, k, a_refs[j].at[c], out_refs[j].at[me, c], (px, py, c))
                 for k, (px, py) in enumerate(chips) for j in range(n)]
        for cp in first:
            cp.start()
        passed = []
        for k, (px, py) in enumerate(chips):
            for j in range(n):
                landed = out_refs[j].at[2 * px + py, c]
                copy(j, k, a_refs[j].at[c], landed, (px, py, c)).wait_recv()
                passed.append(copy(j, 3 + k, landed, landed, sibling))
                passed[-1].start()
        for k, (px, py) in enumerate(chips):
            for j in range(n):
                copy(j, 3 + k, a_refs[j].at[c], out_refs[j].at[2 * px + py, 1 - c], sibling).wait_recv()
        for cp in first + passed:
            cp.wait_send()

    return _comm_call(body, [_SDS((N_CHIPS,) + a.shape, a.dtype) for a in arrs], 6 * n, arrs, name)


def _alltoall_chips(arrs, modes, *, name):
    n = len(arrs)
    slot = lambda ref, mode, s: _slot_ref(ref, mode, s)
    lane_slots = modes

    def body(*refs):
        g_refs, out_refs, send_sems, recv_sems = refs[:n], refs[n:2 * n], refs[2 * n], refs[2 * n + 1]
        x, y, c = _place()
        me = 2 * x + y

        def copy(j, s):
            return pltpu.make_async_remote_copy(slot(g_refs[j], lane_slots[j], s), out_refs[j].at[me],
                                                send_sems.at[N_CHIPS * j + s], recv_sems.at[N_CHIPS * j + me],
                                                device_id=(s // 2, s % 2, c), device_id_type=_MESH)

        for s in range(N_CHIPS):
            @pl.when(s != me)
            def _():
                for j in range(n):
                    copy(j, s).start()
        for t in range(N_CHIPS):
            @pl.when(t != me)
            def _():
                for j in range(n):
                    pltpu.make_async_remote_copy(slot(g_refs[j], lane_slots[j], t), out_refs[j].at[t],
                                                 send_sems.at[N_CHIPS * j + t], recv_sems.at[N_CHIPS * j + t],
                                                 device_id=(t // 2, t % 2, c), device_id_type=_MESH).wait_recv()
        for s in range(N_CHIPS):
            @pl.when(s != me)
            def _():
                for j in range(n):
                    copy(j, s).wait_send()

    outs = [_SDS((N_CHIPS,) + _slot_shape(a, m), a.dtype) for a, m in zip(arrs, modes)]
    return _comm_call(body, outs, N_CHIPS * n, arrs, name)


def _swap_other_half(arrs, *, name):
    n = len(arrs)

    def body(*refs):
        g_refs, out_refs, send_sems, recv_sems = refs[:n], refs[n:2 * n], refs[2 * n], refs[2 * n + 1]
        x, y, c = _place()
        cps = [pltpu.make_async_remote_copy(g_refs[j].at[:, 1 - c], out_refs[j], send_sems.at[j], recv_sems.at[j],
                                            device_id=(x, y, 1 - c), device_id_type=_MESH) for j in range(n)]
        for cp in cps:
            cp.start()
        for cp in cps:
            cp.wait()

    return _comm_call(body, [_SDS((a.shape[0],) + a.shape[2:], a.dtype) for a in arrs], n, arrs, name)


def _swap_sibling(arrs, *, name):
    n = len(arrs)

    def body(*refs):
        f_refs, out_refs, send_sems, recv_sems = refs[:n], refs[n:2 * n], refs[2 * n], refs[2 * n + 1]
        x, y, c = _place()
        cps = [pltpu.make_async_remote_copy(f_refs[j], out_refs[j], send_sems.at[j], recv_sems.at[j],
                                            device_id=(x, y, 1 - c), device_id_type=_MESH) for j in range(n)]
        for cp in cps:
            cp.start()
        for cp in cps:
            cp.wait()

    return _comm_call(body, [_SDS(a.shape, a.dtype) for a in arrs], n, arrs, name)


_HBM = pl.BlockSpec(memory_space=pltpu.HBM)
_SEM = pl.BlockSpec(memory_space=pltpu.SEMAPHORE)
_EFFECT = pltpu.SideEffectType.DATAFLOW_SIDE_EFFECTING


def _slot_ref(ref, mode, s):
    return ref if mode == "same" else ref.at[s]


def _slot_shape(a, mode):
    return a.shape if mode == "same" else a.shape[1:]


def _send_start(arrs, modes, after, *, name):
    n = len(arrs)
    lands = [lax.empty((N_CHIPS,) + _slot_shape(a, m), a.dtype) for a, m in zip(arrs, modes)]

    def body(*refs):
        srcs, land_refs, send_sems, recv_sems, token = refs[:n], refs[n:2 * n], refs[2 * n + 1], refs[2 * n + 2], refs[-1]
        x, y, c = _place()
        me = 2 * x + y
        for s in range(N_CHIPS):
            @pl.when(s != me)
            def _():
                for j in range(n):
                    pltpu.make_async_remote_copy(_slot_ref(srcs[j], modes[j], s), land_refs[j].at[me],
                                                 send_sems.at[N_CHIPS * j + s], recv_sems.at[N_CHIPS * j + me],
                                                 device_id=(s // 2, s % 2, c), device_id_type=_MESH).start()
        token[...] = jnp.zeros_like(token)

    hbm = lambda a: pltpu.HBM(a.shape, a.dtype)
    sems = pltpu.SemaphoreType.DMA((N_CHIPS * n,))
    out = pl.pallas_call(
        body, name=name, out_shape=(sems, sems, *[hbm(a) for a in arrs], *[hbm(a) for a in lands], _SDS((8, LANES), f32)),
        in_specs=[_HBM] * (2 * n) + [_ANY], out_specs=(_SEM, _SEM, *[_HBM] * (2 * n), pl.BlockSpec(memory_space=pltpu.VMEM)),
        input_output_aliases={j: 2 + j for j in range(2 * n)},
        compiler_params=pltpu.CompilerParams(has_side_effects=_EFFECT),
    )(*[pltpu.with_memory_space_constraint(a, pltpu.HBM) for a in arrs + lands], after)
    return out[:-1], out[-1]


def _send_wait(started, modes, after, *, name):
    send_sems, recv_sems = started[0], started[1]
    n = (len(started) - 2) // 2
    thru = list(started[2:])

    def body(*refs):
        srcs, land_refs, send_sems, recv_sems = refs[:n], refs[n:2 * n], refs[2 * n], refs[2 * n + 1]
        x, y, c = _place()
        me = 2 * x + y
        for s in range(N_CHIPS):
            @pl.when(s != me)
            def _():
                for j in range(n):
                    cp = pltpu.make_async_remote_copy(_slot_ref(srcs[j], modes[j], s), land_refs[j].at[s],
                                                      send_sems.at[N_CHIPS * j + s], recv_sems.at[N_CHIPS * j + s],
                                                      device_id=(s // 2, s % 2, c), device_id_type=_MESH)
                    cp.wait_send()
                    cp.wait_recv()

    hbm = lambda a: pltpu.HBM(a.shape, a.dtype)
    out = pl.pallas_call(
        body, name=name, out_shape=tuple(hbm(a) for a in thru), in_specs=[_HBM] * (2 * n) + [_SEM, _SEM, _ANY],
        out_specs=tuple([_HBM] * (2 * n)), input_output_aliases={j: j for j in range(2 * n)},
        compiler_params=pltpu.CompilerParams(has_side_effects=_EFFECT),
    )(*thru, send_sems, recv_sems, after)
    return list(out[n:])


def _gather_all(a, *, name):
    def body(a_ref, out_ref, send_sems, recv_sems, local_sem):
        x, y, c = _place()
        flip = lambda v, f: (1 - v) if f else v
        peers = [(flip(x, f & 4), flip(y, f & 2), flip(c, f & 1)) for f in range(1, 8)]
        me = 4 * x + 2 * y + c
        mine = pltpu.make_async_copy(a_ref, out_ref.at[me], local_sem)
        mine.start()
        sends = [pltpu.make_async_remote_copy(a_ref, out_ref.at[me], send_sems.at[k], recv_sems.at[k], device_id=peer,
                                              device_id_type=_MESH) for k, peer in enumerate(peers)]
        for cp in sends:
            cp.start()
        for k, (px, py, pc) in enumerate(peers):
            pltpu.make_async_remote_copy(a_ref, out_ref.at[4 * px + 2 * py + pc], send_sems.at[k], recv_sems.at[k],
                                         device_id=(px, py, pc), device_id_type=_MESH).wait_recv()
        for cp in sends:
            cp.wait_send()
        mine.wait()

    return pl.pallas_call(body, out_shape=_SDS((8,) + a.shape, a.dtype), in_specs=[_ANY], out_specs=_ANY,
                          scratch_shapes=[pltpu.SemaphoreType.DMA((7,)), pltpu.SemaphoreType.DMA((7,)),
                                          pltpu.SemaphoreType.DMA], name=name)(a)


def _add_my_half(g, r, *, name):
    p, _, h, w = g.shape
    tw = _pick(w, (2048, 1024, 512, 256, 128))
    rb = max(d for d in range(16, h + 1, 16) if h % d == 0 and d * tw * 4 <= (2 << 20))

    def body(c_ref, g_ref, r_ref, o_ref):
        o_ref[...] = (g_ref[...] + r_ref[...]).astype(o_ref.dtype)

    blk = pl.BlockSpec((None, rb, tw), lambda s, i, j, c_ref: (s, i, j))
    grid_spec = pltpu.PrefetchScalarGridSpec(
        num_scalar_prefetch=1, grid=(p, h // rb, w // tw),
        in_specs=[pl.BlockSpec((None, None, rb, tw), lambda s, i, j, c_ref: (s, c_ref[0], i, j)), blk], out_specs=blk)
    c = lax.axis_index("c").astype(jnp.int32).reshape(1)
    return pl.pallas_call(body, out_shape=_SDS((p, h, w), _WIRE), grid_spec=grid_spec,
                          compiler_params=_cp(("parallel", "parallel", "parallel")), name=name)(c, g, r)


_WEIGHTS = ("pre_norm_g", "post_norm_g", "w_in", "b_in", "w_out", "fox_forget_bias", "nsa_cmp_pos_k", "nsa_cmp_w1_k",
            "nsa_cmp_w2_k", "nsa_cmp_pos_v", "nsa_cmp_w1_v", "nsa_cmp_w2_v", "mla_q_norm_g", "mla_w_uq",
            "mla_kv_norm_g", "mla_w_ukv")
_SHARD_AXIS = {"w_in": 2, "w_out": 1, "nsa_cmp_w1_k": 1, "nsa_cmp_w1_v": 1, "mla_w_uq": 2, "mla_w_ukv": 2}
_PACK_UNIT = 16 * LANES


def _pack(arrays, dtype):
    rows = []
    for a in arrays:
        v = a.astype(dtype).reshape(-1)
        pad = (-v.shape[0]) % _PACK_UNIT
        if pad:
            v = jnp.concatenate([v, jnp.zeros((pad,), dtype)])
        rows.append(v.reshape(-1, LANES))
    return jnp.concatenate(rows, axis=0)


def _unpack(flat, shapes):
    out, r = [], 0
    for shp in shapes:
        n = int(np.prod(shp))
        nr = -(-n // _PACK_UNIT) * (_PACK_UNIT // LANES)
        out.append(flat[r:r + nr].reshape(-1)[:n].reshape(shp))
        r += nr
    return out


def kernel(x, pre_norm_g, post_norm_g, w_in, b_in, w_out, fox_forget_bias, nsa_cmp_pos_k, nsa_cmp_w1_k, nsa_cmp_w2_k, nsa_cmp_pos_v, nsa_cmp_w1_v, nsa_cmp_w2_v, mla_q_norm_g, mla_w_uq, mla_kv_norm_g, mla_w_ukv, loss_target, m_pre_norm_g, m_post_norm_g, m_w_in, m_b_in, m_w_out, m_fox_forget_bias, m_nsa_cmp_pos_k, m_nsa_cmp_w1_k, m_nsa_cmp_w2_k, m_nsa_cmp_pos_v, m_nsa_cmp_w1_v, m_nsa_cmp_w2_v, m_mla_q_norm_g, m_mla_w_uq, m_mla_kv_norm_g, m_mla_w_ukv, v_pre_norm_g, v_post_norm_g, v_w_in, v_b_in, v_w_out, v_fox_forget_bias, v_nsa_cmp_pos_k, v_nsa_cmp_w1_k, v_nsa_cmp_w2_k, v_nsa_cmp_pos_v, v_nsa_cmp_w1_v, v_nsa_cmp_w2_v, v_mla_q_norm_g, v_mla_w_uq, v_mla_kv_norm_g, v_mla_w_ukv):
    given = dict(locals())
    local = {n: given[n] for n in _WEIGHTS}
    depth = pre_norm_g.shape[0]
    xs, target = x[0], loss_target[0]
    s = xs.shape[0]
    sharded = [n for n in _WEIGHTS if n in _SHARD_AXIS and n != "w_in"]
    small = [n for n in _WEIGHTS if n not in _SHARD_AXIS]
    chip = 2 * lax.axis_index("x") + lax.axis_index("y")
    core = lax.axis_index("c")
    own = lambda slots, mine: lax.dynamic_update_slice_in_dim(slots, mine[None], chip, axis=0)

    w_in_t = jnp.swapaxes(w_in, 1, 2).astype(_MXU)
    piece = lax.switch(chip, [functools.partial(_piece_from_shard, s=k) for k in range(N_CHIPS)], w_in_t)
    shard_shapes = [local[n].shape for n in sharded]
    flat = _pack([local[n] for n in sharded], _MXU)
    flat2 = flat.reshape((2, -1, LANES))
    first = piece[0].reshape(2, GROUP_W // 2, D_MODEL)
    first_all, flat_all = _gather_chips([first, flat2], name="gather_weights")
    flat_all = own(flat_all, flat2).reshape((N_CHIPS,) + flat.shape)
    per_chip = [_unpack(flat_all[k], shard_shapes) for k in range(N_CHIPS)]
    full = dict(local)
    full["w_in"] = [own(first_all, first).reshape(N_CHIPS, GROUP_W, D_MODEL)]
    for j, n in enumerate(sharded):
        full[n] = jnp.concatenate([per_chip[k][j] for k in range(N_CHIPS)], axis=_SHARD_AXIS[n])
    later = [piece[l] for l in range(1, depth)]
    started, token = _send_start(later, ["same"] * len(later), flat_all, name="gather_later_start")
    full["pre_norm_g"] = pre_norm_g + token[0, 0]

    consts = _consts(s)
    params, act, saved = [], xs, []
    for l in range(depth):
        if l == 1:
            landed = _send_wait(started, ["same"] * len(later), act, name="gather_later_wait")
            full["w_in"] += [own(a, b) for a, b in zip(landed, later)]
        params.append(_layer_params(full, l))
        act, sv = _layer_fwd(act, params[l], consts, f"l{l}")
        saved.append(sv)
    dy, loss_parts = _loss_head(act, target, name="loss_head")
    layer_shapes = [local[n].shape[1:] for n in sharded]

    def flat_slots(g, dtype):
        def part(n, k):
            a, ax = g[n], _SHARD_AXIS[n] - 1
            w = a.shape[ax] // N_CHIPS
            return lax.slice_in_dim(a, k * w, (k + 1) * w, axis=ax)
        return jnp.stack([_pack([part(n, k) for n in sharded], dtype) for k in range(N_CHIPS)])

    own_slot = lambda a: lax.dynamic_index_in_dim(a, chip, axis=0, keepdims=False)
    slots_of = lambda g: g["w_in"].reshape(N_CHIPS, GROUP_W, D_MODEL)

    modes = ["slots", "slots"]
    layer_grads, in_flight = [None] * depth, {}
    for l in reversed(range(depth)):
        dy, layer_grads[l] = _layer_bwd(dy, saved[l], params[l], consts, f"l{l}")
        if l > 0:
            wire = [slots_of(layer_grads[l]).astype(_WIRE), flat_slots(layer_grads[l], _WIRE)]
            started, token = _send_start(wire, modes, dy, name=f"reduce_l{l}_start")
            in_flight[l] = (started, wire)
            params[l - 1] = dict(params[l - 1], post_g=params[l - 1]["post_g"] + token[0, 0])
    grad_x = dy[None]
    grads = {n: jnp.stack([layer_grads[l][n] for l in range(depth)]) for n in small}
    loss_row = jnp.concatenate([jnp.sum(loss_parts).reshape(1), jnp.zeros((LANES - 1,), f32)])
    small_shapes = [(LANES,)] + [grads[n].shape for n in small]
    contrib = _pack([loss_row] + [grads[n] for n in small], f32)

    halves = [slots_of(layer_grads[0]).reshape(N_CHIPS, 2, GROUP_W // 2, D_MODEL),
              flat_slots(layer_grads[0], f32).reshape(N_CHIPS, 2, -1, LANES)]
    from_sibling = _swap_other_half(halves, name="reduce_pair")
    pair_sum = [_add_my_half(g, r, name=f"reduce_pair_add{j}") for j, (g, r) in enumerate(zip(halves, from_sibling))]
    from_chips = _alltoall_chips(pair_sum + [contrib], modes + ["same"], name="reduce_chips")
    my_half = [_sum_slots(own(slots, own_slot(ps)), name=f"reduce_chips_add{j}")
               for j, (slots, ps) in enumerate(zip(from_chips, pair_sum))]
    partial = []
    for l in range(1, depth):
        started, wire = in_flight[l]
        landed = _send_wait(started, modes, dy, name=f"reduce_l{l}_wait")
        partial += [_sum_slots(own(slots, own_slot(a)), name=f"reduce_l{l}_add{j}")
                    for j, (slots, a) in enumerate(zip(landed, wire))]
    partial.append(_sum_slots(own(from_chips[2], contrib), name="sum_small"))
    theirs = _swap_sibling(my_half + partial, name="reduce_share")
    first = core == 0
    whole = [jnp.concatenate([jnp.where(first, a, b), jnp.where(first, b, a)], axis=0)
             for a, b in zip(my_half, theirs[:2])]
    whole += [_add2(a[None], b[None], name=f"reduce_cores_add{j}")[0] for j, (a, b) in enumerate(zip(partial, theirs[2:]))]
    unpiece = [functools.partial(_shard_from_piece, s=k) for k in range(N_CHIPS)]
    summed = {"w_in": jnp.stack([lax.switch(chip, unpiece, whole[2 * l].T) for l in range(depth)])}
    rest = [_unpack(whole[2 * l + 1], layer_shapes) for l in range(depth)]
    for j, n in enumerate(sharded):
        summed[n] = jnp.stack([rest[l][j] for l in range(depth)])
    total = _unpack(whole[2 * depth], small_shapes)
    loss = total[0][0]
    summed.update(zip(small, total[1:]))

    deltas, new_m, new_v = {}, {}, {}
    for n in _WEIGHTS:
        deltas[n], new_m[n], new_v[n] = _adamw(local[n], summed[n], given["m_" + n], given["v_" + n], name=f"adamw_{n}")
    return (loss, grad_x, *[summed[n] for n in _WEIGHTS], *[deltas[n] for n in _WEIGHTS],
            *[new_m[n] for n in _WEIGHTS], *[new_v[n] for n in _WEIGHTS])
```

```python
import functools
import math

import numpy as np
import jax
import jax.numpy as jnp
from jax import lax
from jax.experimental import pallas as pl
from jax.experimental.pallas import tpu as pltpu

f32 = jnp.float32
bf16 = jnp.bfloat16
_MXU = jnp.bfloat16
_WIRE = jnp.bfloat16
_SDS = jax.ShapeDtypeStruct
_ANY = pl.BlockSpec(memory_space=pl.ANY)
_MESH = pl.DeviceIdType.MESH

D_MODEL = 2048
N_HEADS = 4
HEAD_DIM = 128
GROUP = 512
RMS_EPS = 1e-6
NEG_INF = -1e30
ROPE_THETA = 10000.0
CMP_LEN, CMP_STRIDE, SEL_LEN, SEL_TOPN, WINDOW = 32, 16, 64, 16, 512
FORCED_BONUS = 1e6
MLA_Q_RANK, MLA_KV_RANK, MLA_NOPE, MLA_ROPE = 384, 128, 128, 64
ADAM_LR, ADAM_B1, ADAM_B2, ADAM_EPS, ADAM_WD, ADAM_STEP = 0.001, 0.9, 0.999, 1e-08, 0.01, 10
LANES = 128
VMEM_LIMIT = 56 * 1024 * 1024
HP_FWD, HP_BWD = 2, 2

_SEGS = (
    ("sb_q", 512), ("sb_k", 512), ("sb_v", 512), ("sb_gate", 512), ("nsa_q", 512), ("nsa_k_cmp", 128),
    ("nsa_v_cmp", 128), ("nsa_k_sel", 128), ("nsa_v_sel", 128), ("nsa_k_win", 128), ("nsa_v_win", 128),
    ("nsa_branch", 12), ("nsa_gate", 512), ("fox_q", 512), ("fox_k", 512), ("fox_v", 512), ("fox_f", 4),
    ("fox_gate", 512), ("mla_cq", 384), ("mla_ckv", 128), ("mla_k_rope", 64), ("mla_gate", 512),
)
_ORIG, _WID = {}, {}
_o = 0
for _n, _w in _SEGS:
    _ORIG[_n], _WID[_n] = _o, _w
    _o += _w
IN_WIDTH = _o
N_CHIPS = 4
CHIP_COLS = IN_WIDTH // N_CHIPS
GROUP_W = 2048
ZW = N_CHIPS * GROUP_W
_GROUPS = (
    (("sb_q", 0, 512, 0), ("sb_k", 0, 512, 512), ("sb_v", 0, 512, 1024), ("sb_gate", 0, 212, 1536)),
    (("nsa_q", 0, 512, 0), ("nsa_k_cmp", 0, 128, 512), ("nsa_v_cmp", 0, 128, 640), ("nsa_k_sel", 0, 128, 768),
     ("nsa_v_sel", 0, 128, 896), ("nsa_k_win", 0, 128, 1024), ("nsa_v_win", 0, 128, 1152), ("nsa_branch", 0, 12, 1280),
     ("sb_gate", 212, 512, 1408), ("nsa_gate", 0, 156, 1712)),
    (("fox_q", 0, 512, 0), ("fox_k", 0, 512, 512), ("fox_v", 0, 368, 1024), ("nsa_gate", 156, 512, 1408)),
    (("mla_cq", 0, 384, 0), ("mla_ckv", 0, 128, 384), ("mla_k_rope", 0, 64, 512), ("fox_f", 0, 4, 640),
     ("fox_v", 368, 512, 768), ("fox_gate", 0, 512, 1024), ("mla_gate", 0, 512, 1536)),
)
_PIECES = {n: [] for n, _ in _SEGS}
for _s, _grp in enumerate(_GROUPS):
    _cover = sorted((_ORIG[n] + lo, _ORIG[n] + hi) for n, lo, hi, _ in _grp)
    assert _cover[0][0] == _s * CHIP_COLS and _cover[-1][1] == (_s + 1) * CHIP_COLS
    assert all(a[1] == b[0] for a, b in zip(_cover, _cover[1:]))
    _ends = sorted((off, off + hi - lo) for _, lo, hi, off in _grp)
    assert all(a[1] <= b[0] for a, b in zip(_ends, _ends[1:])) and _ends[-1][1] <= GROUP_W
    assert _ends[0][0] == 0 and all(e[0] % 16 == 0 for e in _ends)
    for _n, _lo, _hi, _off in _grp:
        _PIECES[_n].append((_s * GROUP_W + _off, _lo, _hi))
_AL = {n: p[0][0] for n, p in _PIECES.items() if len(p) == 1}


def _cp(sem=None):
    return pltpu.CompilerParams(dimension_semantics=sem, vmem_limit_bytes=VMEM_LIMIT)


def _mm(a, b):
    return jnp.dot(a.astype(_MXU), b.astype(_MXU), preferred_element_type=f32)


def _mm_nt(a, b):
    return lax.dot_general(a.astype(_MXU), b.astype(_MXU), (((1,), (1,)), ((), ())), preferred_element_type=f32)


def _mm_tn(a, b):
    return lax.dot_general(a.astype(_MXU), b.astype(_MXU), (((0,), (0,)), ((), ())), preferred_element_type=f32)


def _mm_split(x, t):
    hi = x.astype(_MXU)
    lo = (x - hi.astype(f32)).astype(_MXU)
    return jnp.dot(hi, t, preferred_element_type=f32) + jnp.dot(lo, t, preferred_element_type=f32)


def _sigmoid(x):
    return 1.0 / (1.0 + jnp.exp(-x))


def _iota(shape, dim):
    return lax.broadcasted_iota(jnp.int32, shape, dim)


def _pick(n, prefs):
    for p in prefs:
        if n % p == 0:
            return p
    return n


def _matmul(a, b, mode, *, bias=None, out_dtype=f32, name):
    grouped = b.ndim == 3
    b_shape = (b.shape[0] * b.shape[1], b.shape[2]) if grouped else b.shape
    if mode == "nn":
        (M, K), (K2, N) = a.shape, b_shape
    elif mode == "nt":
        (M, K), (N, K2) = a.shape, b_shape
    else:
        (K, M), (K2, N) = a.shape, b_shape
    assert K == K2
    tm = _pick(M, (1024, 512, 384, 256, 128))
    tn = _pick(N, (512, 384, 256, 128))
    tk = K if K <= 2048 else _pick(K, (2048, 2432, 1024, 512))
    nk = K // tk
    a_spec = {"nn": pl.BlockSpec((tm, tk), lambda i, j, k: (i, k)),
              "nt": pl.BlockSpec((tm, tk), lambda i, j, k: (i, k)),
              "tn": pl.BlockSpec((tk, tm), lambda i, j, k: (k, i))}[mode]
    if not grouped:
        b_spec = {"nn": pl.BlockSpec((tk, tn), lambda i, j, k: (k, j)),
                  "nt": pl.BlockSpec((tn, tk), lambda i, j, k: (j, k)),
                  "tn": pl.BlockSpec((tk, tn), lambda i, j, k: (k, j))}[mode]
    elif mode == "nt":
        per = b.shape[1] // tn
        b_spec = pl.BlockSpec((None, tn, tk), lambda i, j, k: (j // per, j % per, k))
    else:
        assert mode == "nn"
        per = b.shape[1] // tk
        b_spec = pl.BlockSpec((None, tk, tn), lambda i, j, k: (k // per, k % per, j))
    dot = {"nn": _mm, "nt": _mm_nt, "tn": _mm_tn}[mode]
    has_bias = bias is not None

    def body(*refs):
        if has_bias:
            a_ref, b_ref, bias_ref, o_ref, acc_ref = refs
        else:
            a_ref, b_ref, o_ref, acc_ref = refs
            bias_ref = None
        k = pl.program_id(2)
        part = dot(a_ref[...], b_ref[...])

        def finish(total):
            if has_bias:
                total = total + bias_ref[...]
            o_ref[...] = total.astype(o_ref.dtype)

        if nk == 1:
            finish(part)
        else:
            @pl.when(k == 0)
            def _():
                acc_ref[...] = part

            @pl.when(k > 0)
            def _():
                acc_ref[...] += part

            @pl.when(k == nk - 1)
            def _():
                finish(acc_ref[...])

    in_specs = [a_spec, b_spec]
    args = [a, b]
    if has_bias:
        in_specs.append(pl.BlockSpec((1, tn), lambda i, j, k: (0, j)))
        args.append(bias.reshape(1, N))
    return pl.pallas_call(
        body, out_shape=_SDS((M, N), out_dtype), grid=(M // tm, N // tn, nk),
        in_specs=in_specs, out_specs=pl.BlockSpec((tm, tn), lambda i, j, k: (i, j)),
        scratch_shapes=[pltpu.VMEM((tm, tn), f32)],
        compiler_params=_cp(("parallel", "parallel", "arbitrary")), name=name,
    )(*args)


def _row_block(s):
    return _pick(s, (256, 128))


def _rms_fwd(x, g, *, out_dtype, name):
    s, d = x.shape
    rb = _row_block(s)

    def body(x_ref, g_ref, o_ref):
        xv = x_ref[...]
        r = lax.rsqrt(jnp.mean(xv * xv, axis=-1, keepdims=True) + RMS_EPS)
        o_ref[...] = (xv * r * g_ref[...]).astype(o_ref.dtype)

    return pl.pallas_call(
        body, out_shape=_SDS((s, d), out_dtype), grid=(s // rb,),
        in_specs=[pl.BlockSpec((rb, d), lambda i: (i, 0)), pl.BlockSpec((1, d), lambda i: (0, 0))],
        out_specs=pl.BlockSpec((rb, d), lambda i: (i, 0)), compiler_params=_cp(("parallel",)), name=name,
    )(x, g.reshape(1, d))


def _postnorm_fwd(u, g, x, *, name):
    s, d = u.shape
    rb = _row_block(s)

    def body(u_ref, g_ref, x_ref, o_ref):
        uv = u_ref[...]
        r = lax.rsqrt(jnp.mean(uv * uv, axis=-1, keepdims=True) + RMS_EPS)
        o_ref[...] = x_ref[...] + uv * r * g_ref[...]

    return pl.pallas_call(
        body, out_shape=_SDS((s, d), f32), grid=(s // rb,),
        in_specs=[pl.BlockSpec((rb, d), lambda i: (i, 0)), pl.BlockSpec((1, d), lambda i: (0, 0)),
                  pl.BlockSpec((rb, d), lambda i: (i, 0))],
        out_specs=pl.BlockSpec((rb, d), lambda i: (i, 0)), compiler_params=_cp(("parallel",)), name=name,
    )(u, g.reshape(1, d), x)


def _fold_rows(v):
    r = v.shape[0]
    acc = v[0:8]
    for k in range(1, r // 8):
        acc = acc + v[8 * k:8 * k + 8]
    return acc


def _rms_bwd(dy, x, g, res=None, *, name):
    s, d = x.shape
    rb = _row_block(s)
    nb = s // rb
    has_res = res is not None

    def body(*refs):
        if has_res:
            dy_ref, x_ref, g_ref, res_ref, dx_ref, dg_ref, acc_ref = refs
        else:
            dy_ref, x_ref, g_ref, dx_ref, dg_ref, acc_ref = refs
        i = pl.program_id(0)
        xv = x_ref[...]
        r = lax.rsqrt(jnp.mean(xv * xv, axis=-1, keepdims=True) + RMS_EPS)
        xh = xv * r
        dyv = dy_ref[...]
        dxh = dyv * g_ref[...]
        dx = r * (dxh - xh * jnp.mean(dxh * xh, axis=-1, keepdims=True))
        if has_res:
            dx = dx + res_ref[...]
        dx_ref[...] = dx
        part = _fold_rows(dyv * xh)

        @pl.when(i == 0)
        def _():
            acc_ref[...] = part

        @pl.when(i > 0)
        def _():
            acc_ref[...] += part

        @pl.when(i == nb - 1)
        def _():
            dg_ref[...] = jnp.sum(acc_ref[...], axis=0, keepdims=True)

    blk = pl.BlockSpec((rb, d), lambda i: (i, 0))
    in_specs = [blk, blk, pl.BlockSpec((1, d), lambda i: (0, 0))] + ([blk] if has_res else [])
    args = [dy, x, g.reshape(1, d)] + ([res] if has_res else [])
    return pl.pallas_call(
        body, out_shape=(_SDS((s, d), f32), _SDS((1, d), f32)), grid=(nb,), in_specs=in_specs,
        out_specs=(blk, pl.BlockSpec((1, d), lambda i: (0, 0))),
        scratch_shapes=[pltpu.VMEM((8, d), f32)], compiler_params=_cp(("arbitrary",)), name=name,
    )(*args)


def _loss_head(y, target, *, name):
    s, d = y.shape
    rb = _row_block(s)
    nb = s // rb

    def body(y_ref, t_ref, dy_ref, l_ref):
        i = pl.program_id(0)
        e = y_ref[...] - t_ref[...]
        dy_ref[...] = e * (1.0 / d)
        rows = _fold_rows(e * e)
        part = rows[:, 0:LANES]
        for k in range(1, d // LANES):
            part = part + rows[:, k * LANES:(k + 1) * LANES]
        part = part * (0.5 / d)

        @pl.when(i == 0)
        def _():
            l_ref[...] = part

        @pl.when(i > 0)
        def _():
            l_ref[...] += part

    blk = pl.BlockSpec((rb, d), lambda i: (i, 0))
    return pl.pallas_call(
        body, out_shape=(_SDS((s, d), f32), _SDS((8, LANES), f32)), grid=(nb,), in_specs=[blk, blk],
        out_specs=(blk, pl.BlockSpec((8, LANES), lambda i: (0, 0))),
        compiler_params=_cp(("arbitrary",)), name=name,
    )(y, target)


def _colsum(a, *, name):
    s, n = a.shape
    rb = _row_block(s)
    nb = s // rb
    tn = _pick(n, (2432, 2048, 1024, 512, 384, 128))

    def body(a_ref, o_ref, acc_ref):
        i = pl.program_id(1)
        part = _fold_rows(a_ref[...].astype(f32))

        @pl.when(i == 0)
        def _():
            acc_ref[...] = part

        @pl.when(i > 0)
        def _():
            acc_ref[...] += part

        @pl.when(i == nb - 1)
        def _():
            o_ref[...] = jnp.sum(acc_ref[...], axis=0, keepdims=True)

    return pl.pallas_call(
        body, out_shape=_SDS((1, n), f32), grid=(n // tn, nb),
        in_specs=[pl.BlockSpec((rb, tn), lambda j, i: (i, j))], out_specs=pl.BlockSpec((1, tn), lambda j, i: (0, j)),
        scratch_shapes=[pltpu.VMEM((8, tn), f32)], compiler_params=_cp(("parallel", "arbitrary")), name=name,
    )(a)


def _gate_fwd(outs, gate, *, name):
    s, d = gate.shape
    rb = _row_block(s)
    n = len(outs)
    w = d // n

    def body(*refs):
        g_ref, m_ref = refs[n], refs[n + 1]
        for k in range(n):
            gv = g_ref[:, k * w:(k + 1) * w]
            m_ref[:, k * w:(k + 1) * w] = (refs[k][...] * (gv * _sigmoid(gv))).astype(m_ref.dtype)

    blk = pl.BlockSpec((rb, d), lambda i: (i, 0))
    part = pl.BlockSpec((rb, w), lambda i: (i, 0))
    return pl.pallas_call(body, out_shape=_SDS((s, d), _MXU), grid=(s // rb,), in_specs=[part] * n + [blk],
                          out_specs=blk, compiler_params=_cp(("parallel",)), name=name)(*outs, gate)


def _gate_bwd(dmix, outs, gate, *, name):
    s, d = gate.shape
    rb = _row_block(s)
    n = len(outs)
    w = d // n

    def body(*refs):
        dm_ref, o_refs, g_ref, do_refs, dg_ref = refs[0], refs[1:1 + n], refs[1 + n], refs[2 + n:2 + 2 * n], refs[-1]
        for k in range(n):
            sl = slice(k * w, (k + 1) * w)
            gv = g_ref[:, sl]
            sg = _sigmoid(gv)
            dm = dm_ref[:, sl]
            do_refs[k][...] = dm * (gv * sg)
            dg_ref[:, sl] = dm * o_refs[k][...] * (sg * (1.0 + gv * (1.0 - sg)))

    blk = pl.BlockSpec((rb, d), lambda i: (i, 0))
    part = pl.BlockSpec((rb, w), lambda i: (i, 0))
    return pl.pallas_call(body, out_shape=tuple(_SDS((s, w), f32) for _ in range(n)) + (_SDS((s, d), f32),),
                          grid=(s // rb,), in_specs=[blk] + [part] * n + [blk], out_specs=(part,) * n + (blk,),
                          compiler_params=_cp(("parallel",)), name=name)(dmix, *outs, gate)


def _adamw(w, g, m, v, *, name):
    shape = w.shape
    cols = shape[-1]
    rows = int(np.prod(shape[:-1])) if len(shape) > 1 else 1
    to2 = lambda t: t.reshape(rows, cols)
    rb = rows
    if rows * cols * 4 > (1 << 20):
        rb = max(d for d in range(8, rows + 1, 8) if rows % d == 0 and (d * cols * 4 <= (1600 << 10) or d == 8))

    def body(w_ref, g_ref, m_ref, v_ref, d_ref, nm_ref, nv_ref):
        gv = g_ref[...]
        mn = ADAM_B1 * m_ref[...] + (1.0 - ADAM_B1) * gv
        vn = ADAM_B2 * v_ref[...] + (1.0 - ADAM_B2) * (gv * gv)
        m_hat = mn / (1.0 - ADAM_B1 ** ADAM_STEP)
        v_hat = vn / (1.0 - ADAM_B2 ** ADAM_STEP)
        d_ref[...] = -ADAM_LR * (m_hat / (jnp.sqrt(v_hat) + ADAM_EPS) + ADAM_WD * w_ref[...])
        nm_ref[...] = mn
        nv_ref[...] = vn

    blk = pl.BlockSpec((rb, cols), lambda i: (i, 0))
    out = pl.pallas_call(body, out_shape=tuple(_SDS((rows, cols), f32) for _ in range(3)), grid=(rows // rb,),
                         in_specs=[blk] * 4, out_specs=(blk,) * 3, compiler_params=_cp(("parallel",)),
                         name=name)(to2(w), to2(g), to2(m), to2(v))
    return tuple(t.reshape(shape) for t in out)


def _sum_slots(a, *, name):
    p, n, c = a.shape
    rb = max(d for d in range(8, n + 1, 8) if n % d == 0 and (p * d * c * 4 <= (6 << 20) or d == 8))

    def body(a_ref, o_ref):
        acc = a_ref[0].astype(f32)
        for k in range(1, p):
            acc = acc + a_ref[k].astype(f32)
        o_ref[...] = acc

    return pl.pallas_call(body, out_shape=_SDS((n, c), f32), grid=(n // rb,),
                          in_specs=[pl.BlockSpec((p, rb, c), lambda i: (0, i, 0))],
                          out_specs=pl.BlockSpec((rb, c), lambda i: (i, 0)), compiler_params=_cp(("parallel",)),
                          name=name)(a)


def _add2(a, b, *, name):
    p, n, c = a.shape
    rb = max(d for d in range(8, n + 1, 8) if n % d == 0 and (d * c * 4 <= (2 << 20) or d == 8))

    def body(a_ref, b_ref, o_ref):
        o_ref[...] = a_ref[...] + b_ref[...]

    blk = pl.BlockSpec((1, rb, c), lambda s, i: (s, i, 0))
    return pl.pallas_call(body, out_shape=_SDS((p, n, c), f32), grid=(p, n // rb), in_specs=[blk, blk], out_specs=blk,
                          compiler_params=_cp(("parallel", "parallel")), name=name)(a, b)


def _rope_tables(pos, dim):
    half = dim // 2
    inv = ROPE_THETA ** (-jnp.arange(half, dtype=f32) / half)
    ang = pos.astype(f32)[:, None] * inv[None, :]
    c, s = jnp.cos(ang), jnp.sin(ang)
    z = jnp.zeros_like(c)
    pad = [jnp.zeros((pos.shape[0], LANES - dim), f32)] if dim < LANES else []
    return (jnp.concatenate([c, c] + pad, axis=1), jnp.concatenate([-s, z] + pad, axis=1),
            jnp.concatenate([z, s] + pad, axis=1))


def _rope(x, cos, sa, sb, half, transpose=False):
    if transpose:
        return x * cos + pltpu.roll(x * sa, half, 1) + pltpu.roll(x * sb, LANES - half, 1)
    return x * cos + pltpu.roll(x, LANES - half, 1) * sa + pltpu.roll(x, half, 1) * sb


def _rope_call(items, tables, half, transpose, *, name):
    s = items[0][0].shape[0]
    rb = _row_block(s)
    n = len(items)

    def body(*refs):
        cos, sa, sb = refs[n][...], refs[n + 1][...], refs[n + 2][...]
        for k in range(n):
            x_ref, o_ref = refs[k], refs[n + 3 + k]
            for j in range(items[k][1] // LANES):
                sl = slice(j * LANES, (j + 1) * LANES)
                o_ref[:, sl] = _rope(x_ref[:, sl], cos, sa, sb, half, transpose)

    in_specs = [pl.BlockSpec((rb, w), functools.partial(lambda i, cb: (i, cb), cb=cb)) for _, w, cb in items]
    in_specs += [pl.BlockSpec((rb, LANES), lambda i: (i, 0))] * 3
    out_specs = tuple(pl.BlockSpec((rb, w), lambda i: (i, 0)) for _, w, _ in items)
    return pl.pallas_call(
        body, out_shape=tuple(_SDS((s, w), f32) for _, w, _ in items), grid=(s // rb,), in_specs=in_specs,
        out_specs=out_specs, compiler_params=_cp(("parallel",)), name=name,
    )(*[a for a, _, _ in items], *tables)


def _attn_block(s):
    return _pick(s, (512, 256, 128))


def _lower_mask(b, strict):
    r, c = _iota((b, b), 0), _iota((b, b), 1)
    return (c < r) if strict else (c <= r)


def _pick_lane(block, h):
    return jnp.sum(jnp.where(_iota(block.shape, 1) == h, block, 0.0), axis=1, keepdims=True)


def _head_bias(cum_blk, g, j, hp):
    if hp == N_HEADS:
        return cum_blk[:, j:j + 1]
    return _pick_lane(cum_blk, g * hp + j)


def _attn_fwd(q, k, v, qcol, kcol, vcol, dq, cum, cum_t, *, scale, hp, name):
    s = q.shape[0]
    b = _attn_block(s)
    nq = s // b
    has_bias = cum is not None
    assert qcol % hp == 0 and kcol % hp == 0 and vcol % hp == 0

    def body(*refs):
        if has_bias:
            q_ref, k_ref, v_ref, cum_ref, cumt_ref, o_ref, lse_ref = refs
        else:
            q_ref, k_ref, v_ref, o_ref, lse_ref = refs
        g, i = pl.program_id(0), pl.program_id(1)
        qs = [q_ref[:, j * dq:(j + 1) * dq].astype(_MXU) for j in range(hp)]
        cqs = [_head_bias(cum_ref[...], g, j, hp) for j in range(hp)] if has_bias else None

        def chunk(c, carry, diag):
            st = pl.multiple_of(c * b, b)
            mask = _lower_mask(b, False) if diag else None
            out = []
            for j in range(hp):
                m, l, acc = carry[j]
                z = _mm_nt(qs[j], k_ref[pl.ds(st, b), j * dq:(j + 1) * dq]) * scale
                if has_bias:
                    z = z + cqs[j] - cumt_ref[j, c]
                if diag:
                    z = jnp.where(mask, z, NEG_INF)
                m_new = jnp.maximum(m, jnp.max(z, axis=1, keepdims=True))
                p = jnp.exp(z - m_new)
                if diag:
                    p = jnp.where(mask, p, 0.0)
                alpha = jnp.exp(m - m_new)
                l = alpha * l + jnp.sum(p, axis=1, keepdims=True)
                acc = alpha * acc + _mm(p, v_ref[pl.ds(st, b), j * HEAD_DIM:(j + 1) * HEAD_DIM])
                out.append((m_new, l, acc))
            return tuple(out)

        init = tuple((jnp.full((b, 1), NEG_INF, f32), jnp.zeros((b, 1), f32), jnp.zeros((b, HEAD_DIM), f32))
                     for _ in range(hp))
        carry = lax.fori_loop(0, i, lambda c, cr: chunk(c, cr, False), init)
        for j, (m, l, acc) in enumerate(chunk(i, carry, True)):
            o_ref[:, j * HEAD_DIM:(j + 1) * HEAD_DIM] = acc / l
            lse_ref[j] = m + jnp.log(l)

    in_specs = [pl.BlockSpec((b, hp * dq), lambda g, i: (i, qcol // hp + g)),
                pl.BlockSpec((s, hp * dq), lambda g, i: (0, kcol // hp + g)),
                pl.BlockSpec((s, hp * HEAD_DIM), lambda g, i: (0, vcol // hp + g))]
    args = [q, k, v]
    if has_bias:
        in_specs += [pl.BlockSpec((b, LANES), lambda g, i: (i, 0)),
                     pl.BlockSpec((hp, nq, 1, b), lambda g, i: (g, 0, 0, 0))]
        args += [cum, cum_t]
    return pl.pallas_call(
        body, out_shape=(_SDS((s, N_HEADS * HEAD_DIM), f32), _SDS((N_HEADS, s, 1), f32)), grid=(N_HEADS // hp, nq),
        in_specs=in_specs,
        out_specs=(pl.BlockSpec((b, hp * HEAD_DIM), lambda g, i: (i, g)),
                   pl.BlockSpec((hp, b, 1), lambda g, i: (g, i, 0))),
        compiler_params=_cp(("parallel", "parallel")), name=name,
    )(*args)


def _attn_bwd(q, k, v, qcol, kcol, vcol, dq, do, o, lse, cum, cum_t, *, scale, hp, name):
    s = q.shape[0]
    b = _attn_block(s)
    nq = s // b
    has_bias = cum is not None
    assert qcol % hp == 0 and kcol % hp == 0 and vcol % hp == 0
    hd = lambda j: slice(j * HEAD_DIM, (j + 1) * HEAD_DIM)
    hq = lambda j: slice(j * dq, (j + 1) * dq)

    def body(*refs):
        if has_bias:
            (q_ref, k_ref, v_ref, do_ref, o_ref, lse_ref, cum_ref, cumt_ref, dq_ref, dk_ref, dv_ref, dck_ref,
             p_sc, dp_sc) = refs
        else:
            q_ref, k_ref, v_ref, do_ref, o_ref, lse_ref, dq_ref, dk_ref, dv_ref = refs
        g, i = pl.program_id(0), pl.program_id(1)

        @pl.when(i == 0)
        def _():
            dk_ref[...] = jnp.zeros_like(dk_ref)
            dv_ref[...] = jnp.zeros_like(dv_ref)
            if has_bias:
                dck_ref[...] = jnp.zeros_like(dck_ref)

        qs = [q_ref[:, hq(j)].astype(_MXU) for j in range(hp)]
        dos = [do_ref[:, hd(j)].astype(_MXU) for j in range(hp)]
        lses = [lse_ref[j] for j in range(hp)]
        cqs = [_head_bias(cum_ref[...], g, j, hp) for j in range(hp)] if has_bias else None

        def probs(j, c, diag):
            st = pl.multiple_of(c * b, b)
            z = _mm_nt(qs[j], k_ref[pl.ds(st, b), hq(j)]) * scale
            if has_bias:
                z = z + cqs[j] - cumt_ref[j, c]
            p = jnp.exp(z - lses[j])
            if diag:
                p = jnp.where(_lower_mask(b, False), p, 0.0)
            return p, _mm_nt(dos[j], v_ref[pl.ds(st, b), hd(j)])

        if has_bias:
            def first(c, accs, diag):
                out = []
                for j in range(hp):
                    p, dp = probs(j, c, diag)
                    p_sc[j, c] = p
                    dp_sc[j, c] = dp
                    out.append(accs[j] + jnp.sum(p * dp, axis=1, keepdims=True))
                return tuple(out)

            deltas = lax.fori_loop(0, i, lambda c, a: first(c, a, False),
                                   tuple(jnp.zeros((b, 1), f32) for _ in range(hp)))
            deltas = first(i, deltas, True)
        else:
            deltas = [jnp.sum(do_ref[:, hd(j)] * o_ref[:, hd(j)], axis=1, keepdims=True) for j in range(hp)]

        def chunk(c, dq_accs, diag):
            st = pl.multiple_of(c * b, b)
            out = []
            for j in range(hp):
                p, dp = (p_sc[j, c], dp_sc[j, c]) if has_bias else probs(j, c, diag)
                ds = p * (dp - deltas[j])
                dk_ref[pl.ds(st, b), hq(j)] += _mm_tn(ds, qs[j]) * scale
                dv_ref[pl.ds(st, b), hd(j)] += _mm_tn(p, dos[j])
                if has_bias:
                    dck_ref[j, c] += -jnp.sum(ds, axis=0, keepdims=True)
                out.append(dq_accs[j] + _mm(ds, k_ref[pl.ds(st, b), hq(j)]))
            return tuple(out)

        accs = lax.fori_loop(0, i, lambda c, a: chunk(c, a, False), tuple(jnp.zeros((b, dq), f32) for _ in range(hp)))
        for j, acc in enumerate(chunk(i, accs, True)):
            dq_ref[:, hq(j)] = acc * scale

    rowq = pl.BlockSpec((b, hp * HEAD_DIM), lambda g, i: (i, g))
    in_specs = [pl.BlockSpec((b, hp * dq), lambda g, i: (i, qcol // hp + g)),
                pl.BlockSpec((s, hp * dq), lambda g, i: (0, kcol // hp + g)),
                pl.BlockSpec((s, hp * HEAD_DIM), lambda g, i: (0, vcol // hp + g)), rowq, rowq,
                pl.BlockSpec((hp, b, 1), lambda g, i: (g, i, 0))]
    args = [q, k, v, do, o, lse]
    out_shape = [_SDS((s, N_HEADS * dq), f32), _SDS((s, N_HEADS * dq), f32), _SDS((s, N_HEADS * HEAD_DIM), f32)]
    out_specs = [pl.BlockSpec((b, hp * dq), lambda g, i: (i, g)), pl.BlockSpec((s, hp * dq), lambda g, i: (0, g)),
                 pl.BlockSpec((s, hp * HEAD_DIM), lambda g, i: (0, g))]
    if has_bias:
        in_specs += [pl.BlockSpec((b, LANES), lambda g, i: (i, 0)),
                     pl.BlockSpec((hp, nq, 1, b), lambda g, i: (g, 0, 0, 0))]
        args += [cum, cum_t]
        out_shape.append(_SDS((N_HEADS, nq, 1, b), f32))
        out_specs.append(pl.BlockSpec((hp, nq, 1, b), lambda g, i: (g, 0, 0, 0)))
    return pl.pallas_call(
        body, out_shape=tuple(out_shape), grid=(N_HEADS // hp, nq), in_specs=in_specs, out_specs=tuple(out_specs),
        scratch_shapes=[pltpu.VMEM((hp, nq, b, b), f32)] * 2 if has_bias else [],
        compiler_params=_cp(("parallel", "arbitrary")), name=name,
    )(*args)


def _tri(b, kind):
    r, c = _iota((b, b), 0), _iota((b, b), 1)
    cond = {"row_gt": r > c, "row_lt": r < c, "row_ge": r >= c, "row_le": r <= c}[kind]
    return jnp.where(cond, 1.0, 0.0).astype(_MXU)


def _log_keep(z):
    return -(jnp.maximum(z, 0.0) + jnp.log1p(jnp.exp(-jnp.abs(z))))


def _sb_fwd(z_all, *, hp, name):
    s = z_all.shape[0]
    b = _attn_block(s)
    nq = s // b
    scale = HEAD_DIM ** -0.5
    qcol, kcol, vcol = (_AL[n] // (hp * HEAD_DIM) for n in ("sb_q", "sb_k", "sb_v"))
    hd = lambda j: slice(j * HEAD_DIM, (j + 1) * HEAD_DIM)

    def body(q_ref, k_ref, v_ref, o_ref):
        i = pl.program_id(1)
        qs = [q_ref[:, hd(j)].astype(_MXU) for j in range(hp)]
        upper = _tri(b, "row_gt")

        def chunk(c, carry, diag):
            st = pl.multiple_of(c * b, b)
            mask = _lower_mask(b, True) if diag else None
            out = []
            for j in range(hp):
                rsum, acc = carry[j]
                z = _mm_nt(qs[j], k_ref[pl.ds(st, b), hd(j)]) * scale
                lk = _log_keep(z)
                if diag:
                    lk = jnp.where(mask, lk, 0.0)
                a = z + lk + _mm_split(lk, upper) + rsum
                if diag:
                    a = jnp.where(mask, a, NEG_INF)
                acc = acc + _mm(jnp.exp(a), v_ref[pl.ds(st, b), hd(j)])
                out.append((rsum + jnp.sum(lk, axis=1, keepdims=True), acc))
            return tuple(out)

        init = tuple((jnp.zeros((b, 1), f32), jnp.zeros((b, HEAD_DIM), f32)) for _ in range(hp))
        carry = lax.fori_loop(0, i, lambda t, cr: chunk(i - 1 - t, cr, False), chunk(i, init, True))
        for j in range(hp):
            o_ref[:, hd(j)] = carry[j][1]

    w = hp * HEAD_DIM
    return pl.pallas_call(
        body, out_shape=_SDS((s, GROUP), f32), grid=(N_HEADS // hp, nq),
        in_specs=[pl.BlockSpec((b, w), lambda g, i: (i, qcol + g)), pl.BlockSpec((s, w), lambda g, i: (0, kcol + g)),
                  pl.BlockSpec((s, w), lambda g, i: (0, vcol + g))],
        out_specs=pl.BlockSpec((b, w), lambda g, i: (i, g)),
        compiler_params=_cp(("parallel", "parallel")), name=name,
    )(z_all, z_all, z_all)


def _sb_bwd(z_all, do, *, hp, name):
    s = z_all.shape[0]
    b = _attn_block(s)
    nq = s // b
    scale = HEAD_DIM ** -0.5
    qcol, kcol, vcol = (_AL[n] // (hp * HEAD_DIM) for n in ("sb_q", "sb_k", "sb_v"))
    hd = lambda j: slice(j * HEAD_DIM, (j + 1) * HEAD_DIM)

    def body(q_ref, k_ref, v_ref, do_ref, dq_ref, dk_ref, dv_ref, z_sc, lk_sc, r_sc):
        i = pl.program_id(1)

        @pl.when(i == 0)
        def _():
            dk_ref[...] = jnp.zeros_like(dk_ref)
            dv_ref[...] = jnp.zeros_like(dv_ref)

        qs = [q_ref[:, hd(j)].astype(_MXU) for j in range(hp)]
        dos = [do_ref[:, hd(j)].astype(_MXU) for j in range(hp)]
        upper = _tri(b, "row_gt")
        lower = _tri(b, "row_lt")

        def scores(c, rsums, diag):
            st = pl.multiple_of(c * b, b)
            out = []
            for j in range(hp):
                z = _mm_nt(qs[j], k_ref[pl.ds(st, b), hd(j)]) * scale
                lk = _log_keep(z)
                if diag:
                    lk = jnp.where(_lower_mask(b, True), lk, 0.0)
                z_sc[j, c] = z
                lk_sc[j, c] = lk
                r_sc[j, c] = _mm_split(lk, upper) + rsums[j]
                out.append(rsums[j] + jnp.sum(lk, axis=1, keepdims=True))
            return tuple(out)

        rsums = scores(i, tuple(jnp.zeros((b, 1), f32) for _ in range(hp)), True)
        lax.fori_loop(0, i, lambda t, r: scores(i - 1 - t, r, False), rsums)

        def grads(c, carry, diag):
            st = pl.multiple_of(c * b, b)
            mask = _lower_mask(b, True) if diag else None
            out = []
            for j in range(hp):
                psum, dq_acc = carry[j]
                z, lk = z_sc[j, c], lk_sc[j, c]
                lb = z + lk
                a = lb + r_sc[j, c]
                if diag:
                    a = jnp.where(mask, a, NEG_INF)
                w = jnp.exp(a)
                e = _mm_nt(dos[j], v_ref[pl.ds(st, b), hd(j)]) * w
                before = _mm_split(e, lower) + psum
                dz = e * jnp.exp(lk) - before * jnp.exp(lb)
                if diag:
                    dz = jnp.where(mask, dz, 0.0)
                dk_ref[pl.ds(st, b), hd(j)] += _mm_tn(dz, qs[j]) * scale
                dv_ref[pl.ds(st, b), hd(j)] += _mm_tn(w, dos[j])
                out.append((psum + jnp.sum(e, axis=1, keepdims=True), dq_acc + _mm(dz, k_ref[pl.ds(st, b), hd(j)])))
            return tuple(out)

        init = tuple((jnp.zeros((b, 1), f32), jnp.zeros((b, HEAD_DIM), f32)) for _ in range(hp))
        carry = grads(i, lax.fori_loop(0, i, lambda c, cr: grads(c, cr, False), init), True)
        for j in range(hp):
            dq_ref[:, hd(j)] = carry[j][1] * scale

    w = hp * HEAD_DIM
    blk = pl.BlockSpec((b, w), lambda g, i: (i, g))
    full = pl.BlockSpec((s, w), lambda g, i: (0, g))
    return pl.pallas_call(
        body, out_shape=tuple(_SDS((s, GROUP), f32) for _ in range(3)), grid=(N_HEADS // hp, nq),
        in_specs=[pl.BlockSpec((b, w), lambda g, i: (i, qcol + g)), pl.BlockSpec((s, w), lambda g, i: (0, kcol + g)),
                  pl.BlockSpec((s, w), lambda g, i: (0, vcol + g)), blk],
        out_specs=(blk, full, full),
        scratch_shapes=[pltpu.VMEM((hp, nq, b, b), f32)] * 3,
        compiler_params=_cp(("parallel", "arbitrary")), name=name,
    )(z_all, z_all, z_all, do)


def _split3_left(t, x):
    hi = x.astype(_MXU)
    r1 = x - hi.astype(f32)
    mid = r1.astype(_MXU)
    lo = (r1 - mid.astype(f32)).astype(_MXU)
    dot = functools.partial(jnp.dot, preferred_element_type=f32)
    return dot(t, hi) + dot(t, mid) + dot(t, lo)


def _split3_right(x, t):
    hi = x.astype(_MXU)
    r1 = x - hi.astype(f32)
    mid = r1.astype(_MXU)
    lo = (r1 - mid.astype(f32)).astype(_MXU)
    dot = functools.partial(jnp.dot, preferred_element_type=f32)
    return dot(hi, t) + dot(mid, t) + dot(lo, t)


def _fox_cum_fwd(z_all, bias, *, name):
    s = z_all.shape[0]
    b = _attn_block(s)
    fcol = _AL["fox_f"] // LANES

    def body(f_ref, b_ref, cum_ref, cumt_ref, carry_ref):
        i = pl.program_id(0)

        @pl.when(i == 0)
        def _():
            carry_ref[...] = jnp.zeros_like(carry_ref)

        u = f_ref[...] + b_ref[...]
        lf = jnp.minimum(u, 0.0) - jnp.log1p(jnp.exp(-jnp.abs(u)))
        cum = _split3_left(_tri(b, "row_ge"), lf) + carry_ref[...]
        cum_ref[...] = cum
        cumt_ref[...] = cum.T[0:8, :]
        carry_ref[...] = cum_ref[b - 1:b, :]

    return pl.pallas_call(
        body, out_shape=(_SDS((s, LANES), f32), _SDS((8, s), f32)), grid=(s // b,),
        in_specs=[pl.BlockSpec((b, LANES), lambda i: (i, fcol)), pl.BlockSpec((1, LANES), lambda i: (0, 0))],
        out_specs=(pl.BlockSpec((b, LANES), lambda i: (i, 0)), pl.BlockSpec((8, b), lambda i: (0, i))),
        scratch_shapes=[pltpu.VMEM((1, LANES), f32)], compiler_params=_cp(("arbitrary",)), name=name,
    )(z_all, bias)


def _fox_cum_bwd(z_all, bias, dcum_t, *, name):
    s = z_all.shape[0]
    b = _attn_block(s)
    nb = s // b
    fcol = _AL["fox_f"] // LANES

    def body(f_ref, b_ref, dc_ref, df_ref, db_ref, carry_ref):
        i = pl.program_id(0)

        @pl.when(i == 0)
        def _():
            carry_ref[...] = jnp.zeros_like(carry_ref)
            db_ref[...] = jnp.zeros_like(db_ref)

        dc = dc_ref[...]
        rev = _split3_right(dc, _tri(b, "row_ge")) + carry_ref[...]
        carry_ref[...] = carry_ref[...] + jnp.sum(dc, axis=1, keepdims=True)
        dlf = jnp.concatenate([rev, jnp.zeros((LANES - 8, b), f32)], axis=0).T
        u = f_ref[...] + b_ref[...]
        df = jnp.where(_iota((b, LANES), 1) < N_HEADS, dlf * (1.0 - _sigmoid(u)), 0.0)
        df_ref[...] = df
        db_ref[...] += jnp.sum(df, axis=0, keepdims=True)

    return pl.pallas_call(
        body, out_shape=(_SDS((s, LANES), f32), _SDS((1, LANES), f32)), grid=(nb,),
        in_specs=[pl.BlockSpec((b, LANES), lambda i: (nb - 1 - i, fcol)), pl.BlockSpec((1, LANES), lambda i: (0, 0)),
                  pl.BlockSpec((8, b), lambda i: (0, nb - 1 - i))],
        out_specs=(pl.BlockSpec((b, LANES), lambda i: (nb - 1 - i, 0)), pl.BlockSpec((1, LANES), lambda i: (0, 0))),
        scratch_shapes=[pltpu.VMEM((8, 1), f32)], compiler_params=_cp(("arbitrary",)), name=name,
    )(z_all, bias, dcum_t)


MLA_QW = 2 * LANES


def _rms_rows(x):
    r = lax.rsqrt(jnp.mean(x * x, axis=-1, keepdims=True) + RMS_EPS)
    return x * r, r


def _mla_prep_fwd(z_all, gq, gkv, wuq, wk, wv, tables, *, name):
    s = z_all.shape[0]
    rb = _row_block(s)
    half = MLA_ROPE // 2

    def body(cq_ref, ckv_ref, kr_ref, gq_ref, gkv_ref, wuq_ref, wk_ref, wv_ref, cos_ref, sa_ref, sb_ref,
             q_ref, k_ref, v_ref):
        cos, sa, sb = cos_ref[...], sa_ref[...], sb_ref[...]
        xh, _ = _rms_rows(cq_ref[...])
        qp = _mm(xh * gq_ref[...], wuq_ref[...])
        kh, _ = _rms_rows(ckv_ref[...])
        nkv = kh * gkv_ref[...]
        kn = _mm(nkv, wk_ref[...])
        v_ref[...] = _mm(nkv, wv_ref[...])
        kr = _rope(kr_ref[...], cos, sa, sb, half)
        for h in range(N_HEADS):
            lo, mid, hi = h * MLA_QW, h * MLA_QW + LANES, (h + 1) * MLA_QW
            q_ref[:, lo:mid] = qp[:, lo:mid]
            q_ref[:, mid:hi] = _rope(qp[:, mid:hi], cos, sa, sb, half)
            k_ref[:, lo:mid] = kn[:, h * LANES:(h + 1) * LANES]
            k_ref[:, mid:hi] = kr

    row = lambda w, cb: pl.BlockSpec((rb, w), lambda i: (i, cb))
    whole = lambda a: pl.BlockSpec(a.shape, lambda i: (0,) * a.ndim)
    return pl.pallas_call(
        body, out_shape=(_SDS((s, N_HEADS * MLA_QW), f32), _SDS((s, N_HEADS * MLA_QW), f32), _SDS((s, GROUP), f32)),
        grid=(s // rb,),
        in_specs=[row(MLA_Q_RANK, _AL["mla_cq"] // MLA_Q_RANK), row(LANES, _AL["mla_ckv"] // LANES),
                  row(LANES, _AL["mla_k_rope"] // LANES), whole(gq), whole(gkv), whole(wuq), whole(wk), whole(wv),
                  row(LANES, 0), row(LANES, 0), row(LANES, 0)],
        out_specs=(row(N_HEADS * MLA_QW, 0), row(N_HEADS * MLA_QW, 0), row(GROUP, 0)),
        compiler_params=_cp(("parallel",)), name=name,
    )(z_all, z_all, z_all, gq, gkv, wuq, wk, wv, *tables)


def _mla_prep_bwd(z_all, gq, gkv, wuq, wk, wv, tables, dq_cat, dk_cat, dv, *, name):
    s = z_all.shape[0]
    rb = _row_block(s)
    half = MLA_ROPE // 2

    def body(cq_ref, ckv_ref, gq_ref, gkv_ref, wuq_ref, wk_ref, wv_ref, cos_ref, sa_ref, sb_ref, dq_ref, dk_ref,
             dv_ref, dcq_ref, dckv_ref, dkr_ref, dwuq_ref, dwk_ref, dwv_ref, dgq_ref, dgkv_ref):
        i = pl.program_id(0)

        @pl.when(i == 0)
        def _():
            for r in (dwuq_ref, dwk_ref, dwv_ref, dgq_ref, dgkv_ref):
                r[...] = jnp.zeros_like(r)

        cos, sa, sb = cos_ref[...], sa_ref[...], sb_ref[...]
        parts, knp = [], []
        dkr = jnp.zeros((rb, LANES), f32)
        for h in range(N_HEADS):
            lo, mid, hi = h * MLA_QW, h * MLA_QW + LANES, (h + 1) * MLA_QW
            parts += [dq_ref[:, lo:mid], _rope(dq_ref[:, mid:hi], cos, sa, sb, half, transpose=True)]
            knp.append(dk_ref[:, lo:mid])
            dkr = dkr + _rope(dk_ref[:, mid:hi], cos, sa, sb, half, transpose=True)
        dkr_ref[...] = dkr
        dqp = jnp.concatenate(parts, axis=1)
        dkn = jnp.concatenate(knp, axis=1)
        dvv = dv_ref[...]

        def norm_bwd(x_ref, g_ref, w_pairs, dx_ref, dg_ref):
            xh, r = _rms_rows(x_ref[...])
            nx = xh * g_ref[...]
            dn = jnp.zeros_like(xh)
            for w_ref, dw_ref, dy in w_pairs:
                dw_ref[...] += _mm_tn(nx, dy)
                dn = dn + _mm_nt(dy, w_ref[...])
            dxh = dn * g_ref[...]
            dx_ref[...] = r * (dxh - xh * jnp.mean(dxh * xh, axis=-1, keepdims=True))
            dg_ref[...] += jnp.sum(dn * xh, axis=0, keepdims=True)

        norm_bwd(cq_ref, gq_ref, [(wuq_ref, dwuq_ref, dqp)], dcq_ref, dgq_ref)
        norm_bwd(ckv_ref, gkv_ref, [(wk_ref, dwk_ref, dkn), (wv_ref, dwv_ref, dvv)], dckv_ref, dgkv_ref)

    row = lambda w, cb: pl.BlockSpec((rb, w), lambda i: (i, cb))
    whole = lambda a: pl.BlockSpec(a.shape, lambda i: (0,) * a.ndim)
    return pl.pallas_call(
        body,
        out_shape=(_SDS((s, MLA_Q_RANK), f32), _SDS((s, LANES), f32), _SDS((s, LANES), f32), _SDS(wuq.shape, f32),
                   _SDS(wk.shape, f32), _SDS(wv.shape, f32), _SDS(gq.shape, f32), _SDS(gkv.shape, f32)),
        grid=(s // rb,),
        in_specs=[row(MLA_Q_RANK, _AL["mla_cq"] // MLA_Q_RANK), row(LANES, _AL["mla_ckv"] // LANES), whole(gq),
                  whole(gkv), whole(wuq), whole(wk), whole(wv), row(LANES, 0), row(LANES, 0), row(LANES, 0),
                  row(N_HEADS * MLA_QW, 0), row(N_HEADS * MLA_QW, 0), row(GROUP, 0)],
        out_specs=(row(MLA_Q_RANK, 0), row(LANES, 0), row(LANES, 0), whole(wuq), whole(wk), whole(wv), whole(gq),
                   whole(gkv)),
        compiler_params=_cp(("arbitrary",)), name=name,
    )(z_all, z_all, gq, gkv, wuq, wk, wv, *tables, dq_cat, dk_cat, dv)


def _silu_grad(x):
    sg = _sigmoid(x)
    return sg * (1.0 + x * (1.0 - sg))


def _nsa_cmp_fwd(ra, rb_, pos, w1, w2, tables, *, name):
    nr = ra.shape[1]
    hw = ra.shape[2]

    def body(ra_ref, rb_ref, pos_ref, w1_ref, w2_ref, cos_ref, sa_ref, sb_ref, out_ref, hp_ref):
        for k in range(2):
            xa = ra_ref[k] + pos_ref[k, :, 0:hw]
            xb = rb_ref[k] + pos_ref[k, :, hw:2 * hw]
            hp = _mm(xa, w1_ref[k, 0:hw, :]) + _mm(xb, w1_ref[k, hw:2 * hw, :])
            hp_ref[k] = hp
            out = _mm(hp * _sigmoid(hp), w2_ref[k])
            if k == 0:
                out = _rope(out, cos_ref[...], sa_ref[...], sb_ref[...], HEAD_DIM // 2)
            out_ref[k] = out

    return pl.pallas_call(body, out_shape=(_SDS((2, nr, HEAD_DIM), f32), _SDS((2, nr, HEAD_DIM), f32)),
                          compiler_params=_cp(), name=name)(ra, rb_, pos, w1, w2, *tables)


def _nsa_cmp_bwd(ra, rb_, pos, w1, w2, tables, hp, dout, *, name):
    nr = ra.shape[1]
    hw = ra.shape[2]

    def body(ra_ref, rb_ref, pos_ref, w1_ref, w2_ref, cos_ref, sa_ref, sb_ref, hp_ref, do_ref,
             dxa_ref, dxb_ref, dw1_ref, dw2_ref):
        for k in range(2):
            d_out = do_ref[k]
            if k == 0:
                d_out = _rope(d_out, cos_ref[...], sa_ref[...], sb_ref[...], HEAD_DIM // 2, transpose=True)
            hpv = hp_ref[k]
            dw2_ref[k] = _mm_tn(hpv * _sigmoid(hpv), d_out)
            dhp = _mm_nt(d_out, w2_ref[k]) * _silu_grad(hpv)
            xa = ra_ref[k] + pos_ref[k, :, 0:hw]
            xb = rb_ref[k] + pos_ref[k, :, hw:2 * hw]
            dw1_ref[k, 0:hw, :] = _mm_tn(xa, dhp)
            dw1_ref[k, hw:2 * hw, :] = _mm_tn(xb, dhp)
            dxa_ref[k] = _mm_nt(dhp, w1_ref[k, 0:hw, :])
            dxb_ref[k] = _mm_nt(dhp, w1_ref[k, hw:2 * hw, :])

    return pl.pallas_call(
        body, out_shape=(_SDS((2, nr, hw), f32), _SDS((2, nr, hw), f32), _SDS(w1.shape, f32), _SDS(w2.shape, f32)),
        compiler_params=_cp(), name=name)(ra, rb_, pos, w1, w2, *tables, hp, dout)


def _nsa_consts(s):
    b = _attn_block(s)
    nr = s // CMP_STRIDE
    n_cmp = (s - CMP_LEN) // CMP_STRIDE + 1
    n_sel = s // SEL_LEN
    cmp_start = np.arange(n_cmp) * CMP_STRIDE
    sel_start = np.arange(n_sel) * SEL_LEN
    overlap = np.clip(np.minimum(cmp_start[:, None] + CMP_LEN, sel_start[None, :] + SEL_LEN)
                      - np.maximum(cmp_start[:, None], sel_start[None, :]), 0, None)
    m2s = np.zeros((nr, LANES), np.float32)
    m2s[:n_cmp, :n_sel] = overlap / CMP_LEN
    e3 = np.zeros((s // b, LANES, b), np.float32)
    tok = np.arange(s)
    e3[tok // b, tok // SEL_LEN, tok % b] = 1.0
    return jnp.asarray(m2s, _MXU), jnp.asarray(e3, _MXU)


def _nsa_masks(i, b, d):
    qpos = i * b + _iota((b, b), 0)
    kpos = (i - d) * b + _iota((b, b), 1)
    return (kpos <= qpos) & (kpos > qpos - WINDOW)


def _nsa_fwd(qr, kvc, ksr, vs, kwr, vw, z_all, m2s, e3, *, name):
    s = qr.shape[0]
    b = _attn_block(s)
    nq = s // b
    nr = kvc.shape[1]
    n_sel = s // SEL_LEN
    top_n = min(SEL_TOPN, n_sel)
    nd = -(-WINDOW // b)
    scale = HEAD_DIM ** -0.5
    bcol = _AL["nsa_branch"] // LANES
    H = N_HEADS

    def body(q_ref, kvc_ref, ks_ref, vs_ref, kw_ref, vw_ref, br_ref, m2s_ref, e3_ref,
             o_ref, oc_ref, os_ref, ow_ref, st_ref, sel_ref, m_sc, l_sc, acc_sc):
        i = pl.program_id(0)
        lane = _iota((b, LANES), 1)
        hs = lambda h: slice(h * HEAD_DIM, (h + 1) * HEAD_DIM)

        cmp_mask = (CMP_STRIDE * _iota((b, nr), 1) + (CMP_LEN - 1)) <= (i * b + _iota((b, nr), 0))
        imp = jnp.zeros((b, LANES), f32)
        stats = jnp.zeros((b, LANES), f32)
        for h in range(H):
            zc = jnp.where(cmp_mask, _mm_nt(q_ref[:, hs(h)], kvc_ref[0]) * scale, NEG_INF)
            m = jnp.max(zc, axis=1, keepdims=True)
            p = jnp.where(cmp_mask, jnp.exp(zc - m), 0.0)
            l = jnp.sum(p, axis=1, keepdims=True)
            some = l > 0.0
            lsafe = jnp.where(some, l, 1.0)
            pc = p * jnp.where(some, 1.0 / lsafe, 0.0)
            oc_ref[:, hs(h)] = _mm(pc, kvc_ref[1])
            imp = imp + _mm(pc, m2s_ref[...])
            stats = jnp.where(lane == h, jnp.where(some, m + jnp.log(lsafe), 0.0), stats)

        cur = jnp.right_shift(i * b + _iota((b, LANES), 0), int(math.log2(SEL_LEN)))
        forced = (lane == 0) | (lane == cur) | (lane == cur - 1)
        score = jnp.where(lane <= cur, jnp.where(forced, FORCED_BONUS, imp), NEG_INF)
        score = jnp.where(lane < n_sel, score, -3e38)
        rank = jnp.zeros((b, LANES), f32)
        for j in range(n_sel):
            col = score[:, j:j + 1]
            rank = rank + jnp.where(col > score, 1.0, jnp.where(col == score, jnp.where(lane > j, 1.0, 0.0), 0.0))
        sel = jnp.where(lane < n_sel, jnp.where(rank < top_n, 1.0, 0.0), 0.0)
        sel_ref[...] = sel
        sel_b = sel.astype(_MXU)

        def reset():
            m_sc[...] = jnp.full(m_sc.shape, NEG_INF, f32)
            l_sc[...] = jnp.zeros_like(l_sc)
            acc_sc[...] = jnp.zeros_like(acc_sc)

        def update(h, z, mask, vch):
            zm = jnp.where(mask, z, NEG_INF)
            m_old = m_sc[h]
            m_new = jnp.maximum(m_old, jnp.max(zm, axis=1, keepdims=True))
            p = jnp.where(mask, jnp.exp(zm - m_new), 0.0)
            alpha = jnp.exp(m_old - m_new)
            l_sc[h] = alpha * l_sc[h] + jnp.sum(p, axis=1, keepdims=True)
            acc_sc[h] = alpha * acc_sc[h] + _mm(p, vch)
            m_sc[h] = m_new

        def finish(out_ref, branch, stats):
            for h in range(H):
                out_ref[:, hs(h)] = acc_sc[h] / l_sc[h]
                stats = jnp.where(lane == 4 * branch + h, m_sc[h] + jnp.log(l_sc[h]), stats)
            return stats

        def sel_chunk(c, diag):
            st = pl.multiple_of(c * b, b)
            mask = _mm(sel_b, e3_ref[c]) > 0.5
            if diag:
                mask = mask & _lower_mask(b, False)
            kch, vch = ks_ref[pl.ds(st, b), :], vs_ref[pl.ds(st, b), :]
            for h in range(H):
                update(h, _mm_nt(q_ref[:, hs(h)], kch) * scale, mask, vch)

        reset()

        def sel_loop(c, carry):
            sel_chunk(c, False)
            return carry

        lax.fori_loop(0, i, sel_loop, 0)
        sel_chunk(i, True)
        stats = finish(os_ref, 1, stats)

        reset()
        for d in range(nd, -1, -1):
            @pl.when(i >= d)
            def _():
                st = pl.multiple_of((i - d) * b, b)
                mask = _nsa_masks(i, b, d)
                kch, vch = kw_ref[pl.ds(st, b), :], vw_ref[pl.ds(st, b), :]
                for h in range(H):
                    update(h, _mm_nt(q_ref[:, hs(h)], kch) * scale, mask, vch)
        stats = finish(ow_ref, 2, stats)
        st_ref[...] = stats

        g = _sigmoid(br_ref[...])
        for h in range(H):
            o_ref[:, hs(h)] = (g[:, 3 * h:3 * h + 1] * oc_ref[:, hs(h)] + g[:, 3 * h + 1:3 * h + 2] * os_ref[:, hs(h)]
                               + g[:, 3 * h + 2:3 * h + 3] * ow_ref[:, hs(h)])

    blk = lambda w: pl.BlockSpec((b, w), lambda i: (i, 0))
    whole = lambda a: pl.BlockSpec(a.shape, lambda i: (0,) * a.ndim)
    return pl.pallas_call(
        body, out_shape=tuple(_SDS((s, GROUP), f32) for _ in range(4)) + (_SDS((s, LANES), f32), _SDS((s, LANES), f32)),
        grid=(nq,),
        in_specs=[blk(GROUP), whole(kvc), whole(ksr), whole(vs), whole(kwr), whole(vw),
                  pl.BlockSpec((b, LANES), lambda i: (i, bcol)), whole(m2s), whole(e3)],
        out_specs=(blk(GROUP),) * 4 + (blk(LANES), blk(LANES)),
        scratch_shapes=[pltpu.VMEM((H, b, 1), f32), pltpu.VMEM((H, b, 1), f32), pltpu.VMEM((H, b, HEAD_DIM), f32)],
        compiler_params=_cp(("parallel",)), name=name,
    )(qr, kvc, ksr, vs, kwr, vw, z_all, m2s, e3)


def _nsa_bwd(do, qr, kvc, ksr, vs, kwr, vw, z_all, oc, os_, ow, stats, sel, e3, *, name):
    s = qr.shape[0]
    b = _attn_block(s)
    nq = s // b
    nr = kvc.shape[1]
    nd = -(-WINDOW // b)
    scale = HEAD_DIM ** -0.5
    bcol = _AL["nsa_branch"] // LANES
    H = N_HEADS

    def body(do_ref, q_ref, kvc_ref, ks_ref, vs_ref, kw_ref, vw_ref, br_ref, oc_ref, os_ref, ow_ref, st_ref, sel_ref,
             e3_ref, dq_ref, dbr_ref, dkvc_ref, dks_ref, dvs_ref, dkw_ref, dvw_ref, dob_sc, delta_sc, dq_sc):
        i = pl.program_id(0)

        @pl.when(i == 0)
        def _():
            for r in (dkvc_ref, dks_ref, dvs_ref, dkw_ref, dvw_ref):
                r[...] = jnp.zeros_like(r)

        lane = _iota((b, LANES), 1)
        hs = lambda h: slice(h * HEAD_DIM, (h + 1) * HEAD_DIM)
        g = _sigmoid(br_ref[...])
        stats = st_ref[...]
        dbr = jnp.zeros((b, LANES), f32)
        outs = (oc_ref, os_ref, ow_ref)
        for h in range(H):
            doh = do_ref[:, hs(h)]
            for j in range(3):
                gj = g[:, 3 * h + j:3 * h + j + 1]
                dgj = jnp.sum(doh * outs[j][:, hs(h)], axis=1, keepdims=True)
                dbr = jnp.where(lane == 3 * h + j, dgj * gj * (1.0 - gj), dbr)
                dob_sc[j, :, hs(h)] = gj * doh
                delta_sc[j, h] = gj * dgj
        dbr_ref[...] = dbr
        dq_sc[...] = jnp.zeros_like(dq_sc)

        def branch(j, h, z, mask, kch, vch):
            qh = q_ref[:, hs(h)]
            p = jnp.where(mask, jnp.exp(jnp.where(mask, z, NEG_INF) - stats[:, 4 * j + h:4 * j + h + 1]), 0.0)
            dob = dob_sc[j, :, hs(h)]
            ds = p * (_mm_nt(dob, vch) - delta_sc[j, h])
            dq_sc[:, hs(h)] += _mm(ds, kch) * scale
            return _mm_tn(ds, qh) * scale, _mm_tn(p, dob)

        cmp_mask = (CMP_STRIDE * _iota((b, nr), 1) + (CMP_LEN - 1)) <= (i * b + _iota((b, nr), 0))
        kc, vc = kvc_ref[0], kvc_ref[1]
        for h in range(H):
            dk, dv = branch(0, h, _mm_nt(q_ref[:, hs(h)], kc) * scale, cmp_mask, kc, vc)
            dkvc_ref[0] += dk
            dkvc_ref[1] += dv

        sel_b = sel_ref[...].astype(_MXU)

        def chunk(j, c, mask, k_ref, v_ref, dk_ref, dv_ref):
            st = pl.multiple_of(c * b, b)
            kch, vch = k_ref[pl.ds(st, b), :], v_ref[pl.ds(st, b), :]
            dk = jnp.zeros((b, HEAD_DIM), f32)
            dv = jnp.zeros((b, HEAD_DIM), f32)
            for h in range(H):
                dkh, dvh = branch(j, h, _mm_nt(q_ref[:, hs(h)], kch) * scale, mask, kch, vch)
                dk, dv = dk + dkh, dv + dvh
            dk_ref[pl.ds(st, b), :] += dk
            dv_ref[pl.ds(st, b), :] += dv

        def sel_chunk(c, diag):
            mask = _mm(sel_b, e3_ref[c]) > 0.5
            if diag:
                mask = mask & _lower_mask(b, False)
            chunk(1, c, mask, ks_ref, vs_ref, dks_ref, dvs_ref)

        def sel_loop(c, carry):
            sel_chunk(c, False)
            return carry

        lax.fori_loop(0, i, sel_loop, 0)
        sel_chunk(i, True)

        for d in range(nd, -1, -1):
            @pl.when(i >= d)
            def _():
                chunk(2, i - d, _nsa_masks(i, b, d), kw_ref, vw_ref, dkw_ref, dvw_ref)

        dq_ref[...] = dq_sc[...]

    blk = lambda w: pl.BlockSpec((b, w), lambda i: (i, 0))
    whole = lambda a: pl.BlockSpec(a.shape, lambda i: (0,) * a.ndim)
    stream = _SDS((s, HEAD_DIM), f32)
    return pl.pallas_call(
        body, out_shape=(_SDS((s, GROUP), f32), _SDS((s, LANES), f32), _SDS(kvc.shape, f32), stream, stream, stream,
                         stream),
        grid=(nq,),
        in_specs=[blk(GROUP), blk(GROUP), whole(kvc), whole(ksr), whole(vs), whole(kwr), whole(vw),
                  pl.BlockSpec((b, LANES), lambda i: (i, bcol)), blk(GROUP), blk(GROUP), blk(GROUP), blk(LANES),
                  blk(LANES), whole(e3)],
        out_specs=(blk(GROUP), blk(LANES), whole(kvc), whole(ksr), whole(vs), whole(kwr), whole(vw)),
        scratch_shapes=[pltpu.VMEM((3, b, GROUP), f32), pltpu.VMEM((3, H, b, 1), f32), pltpu.VMEM((b, GROUP), f32)],
        compiler_params=_cp(("arbitrary",)), name=name,
    )(do, qr, kvc, ksr, vs, kwr, vw, z_all, oc, os_, ow, stats, sel, e3)


def _seg(a, name):
    parts = [lax.slice_in_dim(a, off, off + hi - lo, axis=a.ndim - 1) for off, lo, hi in _PIECES[name]]
    return parts[0] if len(parts) == 1 else jnp.concatenate(parts, axis=a.ndim - 1)


def _to_groups(segs, rows, dtype):
    cols = []
    for s, grp in enumerate(_GROUPS):
        at = 0
        for n, lo, hi, off in sorted(grp, key=lambda t: t[3]):
            if off > at:
                cols.append(jnp.zeros((rows, off - at), dtype))
            cols.append(segs[n][:, lo:hi].astype(dtype))
            at = off + hi - lo
        if at < GROUP_W:
            cols.append(jnp.zeros((rows, GROUP_W - at), dtype))
    return jnp.concatenate(cols, axis=1)


def _piece_from_shard(w_t, s):
    grp = sorted(_GROUPS[s], key=lambda t: t[3])
    ends = [t[3] for t in grp[1:]] + [GROUP_W]
    rows = []
    for (n, lo, hi, off), end in zip(grp, ends):
        first = _ORIG[n] + lo - s * CHIP_COLS
        rows.append(jnp.pad(w_t[:, first:first + hi - lo], ((0, 0), (0, end - off - (hi - lo)), (0, 0))))
    return jnp.concatenate(rows, axis=1)


def _shard_from_piece(g, s):
    return jnp.concatenate([g[:, off:off + hi - lo] for n, lo, hi, off in
                            sorted(_GROUPS[s], key=lambda t: _ORIG[t[0]] + t[1])], axis=1)


def _from_groups(a):
    return jnp.concatenate([_seg(a, n) for n, _ in _SEGS], axis=1)


def _cmp_rows(tok):
    s = tok.shape[0]
    r = tok.reshape(s // CMP_STRIDE, CMP_STRIDE * HEAD_DIM)
    return r, jnp.concatenate([r[1:], jnp.zeros((1, r.shape[1]), r.dtype)], axis=0)


def _cmp_unrows(dxa, dxb):
    s = dxa.shape[0] * CMP_STRIDE
    return (dxa + jnp.concatenate([jnp.zeros((1, dxa.shape[1]), dxa.dtype), dxb[:-1]], axis=0)).reshape(s, HEAD_DIM)


_GATES = ("sb_gate", "nsa_gate", "fox_gate", "mla_gate")


def _layer_fwd(x, p, c, tag):
    s = x.shape[0]
    b = _attn_block(s)
    h = _rms_fwd(x, p["pre_g"], out_dtype=_MXU, name=f"prenorm_{tag}")
    z = _matmul(h, p["w_in"], "nt", bias=p["b_in"], name=f"inproj_{tag}")
    o_sb = _sb_fwd(z, hp=HP_FWD, name=f"sb_fwd_{tag}")

    qr, ksr, kwr = _rope_call([(z, GROUP, _AL["nsa_q"] // GROUP), (z, LANES, _AL["nsa_k_sel"] // LANES),
                               (z, LANES, _AL["nsa_k_win"] // LANES)], c["tabs128"], HEAD_DIM // 2, False,
                              name=f"nsa_rope_{tag}")
    (rak, rbk), (rav, rbv) = _cmp_rows(_seg(z, "nsa_k_cmp")), _cmp_rows(_seg(z, "nsa_v_cmp"))
    ra, rb_ = jnp.stack([rak, rav]), jnp.stack([rbk, rbv])
    kvc, hp = _nsa_cmp_fwd(ra, rb_, p["cmp_pos"], p["cmp_w1"], p["cmp_w2"], c["tabs_cmp"], name=f"nsa_cmp_{tag}")
    vs, vw = _seg(z, "nsa_v_sel"), _seg(z, "nsa_v_win")
    o_nsa, oc, os_, ow, stats, sel = _nsa_fwd(qr, kvc, ksr, vs, kwr, vw, z, c["m2s"], c["e3"], name=f"nsa_fwd_{tag}")

    cum, cum_t8 = _fox_cum_fwd(z, p["fox_bias"], name=f"fox_cum_{tag}")
    cum_t = cum_t8.reshape(8, s // b, 1, b)
    fox_v = _seg(z, "fox_v")
    fcols = (_AL["fox_q"] // HEAD_DIM, _AL["fox_k"] // HEAD_DIM, 0)
    o_fox, lse_fox = _attn_fwd(z, z, fox_v, *fcols, HEAD_DIM, cum, cum_t, scale=HEAD_DIM ** -0.5, hp=HP_FWD,
                               name=f"fox_fwd_{tag}")

    qcat, kcat, vm = _mla_prep_fwd(z, p["gq"], p["gkv"], p["wuq"], p["wk"], p["wv"], c["tabs64"],
                                   name=f"mla_prep_{tag}")
    o_mla, lse_mla = _attn_fwd(qcat, kcat, vm, 0, 0, 0, MLA_QW, None, None, scale=(MLA_NOPE + MLA_ROPE) ** -0.5,
                               hp=HP_BWD, name=f"mla_fwd_{tag}")

    o_all = (o_sb, o_nsa, o_fox, o_mla)
    gates = jnp.concatenate([_seg(z, n) for n in _GATES], axis=1)
    mix = _gate_fwd(o_all, gates, name=f"gate_{tag}")
    u = _matmul(mix, p["w_out"], "nn", name=f"outproj_{tag}")
    y = _postnorm_fwd(u, p["post_g"], x, name=f"postnorm_{tag}")
    saved = dict(x=x, h=h, z=z, qr=qr, ksr=ksr, kwr=kwr, ra=ra, rb=rb_, kvc=kvc, hp=hp, vs=vs, vw=vw, oc=oc, os=os_,
                 ow=ow, stats=stats, sel=sel, cum=cum, cum_t=cum_t, fox_v=fox_v, o_fox=o_fox, lse_fox=lse_fox, qcat=qcat, kcat=kcat,
                 vm=vm, o_mla=o_mla, lse_mla=lse_mla, o_all=o_all, gates=gates, mix=mix, u=u)
    return y, saved


def _layer_bwd(dy, sv, p, c, tag):
    z = sv["z"]
    s = z.shape[0]
    du, dg_post = _rms_bwd(dy, sv["u"], p["post_g"], name=f"postnorm_bwd_{tag}")
    dmix = _matmul(du, p["w_out"], "nt", name=f"outproj_dx_{tag}")
    dw_out = _matmul(sv["mix"], du, "tn", name=f"outproj_dw_{tag}")
    do_sb, do_nsa, do_fox, do_mla, dgates = _gate_bwd(dmix, sv["o_all"], sv["gates"], name=f"gate_bwd_{tag}")
    dgate = [dgates[:, k * GROUP:(k + 1) * GROUP] for k in range(4)]

    sb_dq, sb_dk, sb_dv = _sb_bwd(z, do_sb, hp=HP_BWD, name=f"sb_bwd_{tag}")

    n_dq, n_dbr, n_dkvc, n_dks, n_dvs, n_dkw, n_dvw = _nsa_bwd(
        do_nsa, sv["qr"], sv["kvc"], sv["ksr"], sv["vs"], sv["kwr"], sv["vw"], z, sv["oc"], sv["os"], sv["ow"],
        sv["stats"], sv["sel"], c["e3"], name=f"nsa_bwd_{tag}")
    dxa, dxb, dw1, dw2 = _nsa_cmp_bwd(sv["ra"], sv["rb"], p["cmp_pos"], p["cmp_w1"], p["cmp_w2"], c["tabs_cmp"],
                                      sv["hp"], n_dkvc, name=f"nsa_cmp_bwd_{tag}")
    n_dq, n_dks, n_dkw = _rope_call([(n_dq, GROUP, 0), (n_dks, LANES, 0), (n_dkw, LANES, 0)], c["tabs128"],
                                    HEAD_DIM // 2, True, name=f"nsa_rope_bwd_{tag}")
    dpos = _colsum(jnp.concatenate([dxa[0], dxb[0], dxa[1], dxb[1]], axis=1), name=f"nsa_dpos_{tag}")
    flat = CMP_LEN * HEAD_DIM

    fcols = (_AL["fox_q"] // HEAD_DIM, _AL["fox_k"] // HEAD_DIM, 0)
    f_dq, f_dk, f_dv, f_dck = _attn_bwd(z, z, sv["fox_v"], *fcols, HEAD_DIM, do_fox, sv["o_fox"], sv["lse_fox"],
                                        sv["cum"], sv["cum_t"], scale=HEAD_DIM ** -0.5, hp=HP_BWD,
                                        name=f"fox_bwd_{tag}")
    dcum_t = jnp.pad(f_dck.reshape(N_HEADS, s), ((0, 8 - N_HEADS), (0, 0)))
    f_df, f_dbias = _fox_cum_bwd(z, p["fox_bias"], dcum_t, name=f"fox_cum_bwd_{tag}")

    m_dq, m_dk, m_dv = _attn_bwd(sv["qcat"], sv["kcat"], sv["vm"], 0, 0, 0, MLA_QW, do_mla, sv["o_mla"], sv["lse_mla"],
                                 None, None, scale=(MLA_NOPE + MLA_ROPE) ** -0.5, hp=HP_BWD, name=f"mla_bwd_{tag}")
    m_dcq, m_dckv, m_dkr, m_dwuq, m_dwk, m_dwv, m_dgq, m_dgkv = _mla_prep_bwd(
        z, p["gq"], p["gkv"], p["wuq"], p["wk"], p["wv"], c["tabs64"], m_dq, m_dk, m_dv, name=f"mla_prep_bwd_{tag}")

    dz = _to_groups(dict(
        sb_q=sb_dq, sb_k=sb_dk, sb_v=sb_dv, sb_gate=dgate[0], nsa_q=n_dq, nsa_k_cmp=_cmp_unrows(dxa[0], dxb[0]),
        nsa_v_cmp=_cmp_unrows(dxa[1], dxb[1]), nsa_k_sel=n_dks, nsa_v_sel=n_dvs, nsa_k_win=n_dkw, nsa_v_win=n_dvw,
        nsa_branch=n_dbr, nsa_gate=dgate[1], fox_q=f_dq, fox_k=f_dk, fox_v=f_dv, fox_f=f_df, fox_gate=dgate[2],
        mla_cq=m_dcq, mla_ckv=m_dckv, mla_k_rope=m_dkr, mla_gate=dgate[3]), s, _MXU)
    dh = _matmul(dz, p["w_in"], "nn", name=f"inproj_dx_{tag}")
    dw_in = _matmul(dz, sv["h"], "tn", name=f"inproj_dw_{tag}")
    db = _colsum(dz, name=f"inproj_db_{tag}")
    dx, dg_pre = _rms_bwd(dh, sv["x"], p["pre_g"], res=dy, name=f"prenorm_bwd_{tag}")

    qw = MLA_NOPE + MLA_ROPE
    grads = {
        "pre_norm_g": dg_pre[0], "post_norm_g": dg_post[0], "w_in": dw_in, "b_in": _from_groups(db)[0],
        "w_out": dw_out, "fox_forget_bias": f_dbias[0, :N_HEADS],
        "nsa_cmp_pos_k": dpos[0, :flat].reshape(CMP_LEN, HEAD_DIM), "nsa_cmp_w1_k": dw1[0], "nsa_cmp_w2_k": dw2[0],
        "nsa_cmp_pos_v": dpos[0, flat:].reshape(CMP_LEN, HEAD_DIM), "nsa_cmp_w1_v": dw1[1], "nsa_cmp_w2_v": dw2[1],
        "mla_q_norm_g": m_dgq[0],
        "mla_w_uq": jnp.concatenate([m_dwuq[:, MLA_QW * h:MLA_QW * h + qw] for h in range(N_HEADS)], axis=1),
        "mla_kv_norm_g": m_dgkv[0],
        "mla_w_ukv": jnp.concatenate(sum([[m_dwk[:, LANES * h:LANES * (h + 1)], m_dwv[:, LANES * h:LANES * (h + 1)]]
                                          for h in range(N_HEADS)], []), axis=1),
    }
    return dx, grads


def _layer_params(w, l):
    b_in = w["b_in"][l].reshape(1, -1)
    b_segs = {n: b_in[:, _ORIG[n]:_ORIG[n] + wd] for n, wd in _SEGS}
    qw = MLA_NOPE + MLA_ROPE
    w_uq, w_ukv = w["mla_w_uq"][l], w["mla_w_ukv"][l]
    uq = []
    for h in range(N_HEADS):
        uq += [w_uq[:, qw * h:qw * (h + 1)], jnp.zeros((w_uq.shape[0], MLA_QW - qw), w_uq.dtype)]
    kw_ = 2 * LANES
    flat = CMP_LEN * HEAD_DIM
    return dict(
        pre_g=w["pre_norm_g"][l].reshape(1, -1), post_g=w["post_norm_g"][l].reshape(1, -1),
        w_in=w["w_in"][l], b_in=_to_groups(b_segs, 1, f32), w_out=w["w_out"][l],
        fox_bias=jnp.pad(w["fox_forget_bias"][l], (0, LANES - N_HEADS)).reshape(1, LANES),
        cmp_pos=jnp.stack([w["nsa_cmp_pos_k"][l].reshape(1, flat), w["nsa_cmp_pos_v"][l].reshape(1, flat)]),
        cmp_w1=jnp.stack([w["nsa_cmp_w1_k"][l], w["nsa_cmp_w1_v"][l]]),
        cmp_w2=jnp.stack([w["nsa_cmp_w2_k"][l], w["nsa_cmp_w2_v"][l]]),
        gq=w["mla_q_norm_g"][l].reshape(1, -1), gkv=w["mla_kv_norm_g"][l].reshape(1, -1),
        wuq=jnp.concatenate(uq, axis=1),
        wk=jnp.concatenate([w_ukv[:, kw_ * h:kw_ * h + LANES] for h in range(N_HEADS)], axis=1),
        wv=jnp.concatenate([w_ukv[:, kw_ * h + LANES:kw_ * (h + 1)] for h in range(N_HEADS)], axis=1),
    )


def _consts(s):
    pos = jnp.arange(s)
    m2s, e3 = _nsa_consts(s)
    return dict(tabs128=_rope_tables(pos, HEAD_DIM), tabs64=_rope_tables(pos, MLA_ROPE),
                tabs_cmp=_rope_tables(jnp.arange(s // CMP_STRIDE) * CMP_STRIDE + (CMP_LEN - 1), HEAD_DIM),
                m2s=m2s, e3=e3)


def _place():
    return lax.axis_index("x"), lax.axis_index("y"), lax.axis_index("c")


def _other_chips(x, y):
    return [(1 - x, y), (x, 1 - y), (1 - x, 1 - y)]


def _comm_call(body, out_shapes, n_sems, arrs, name):
    return pl.pallas_call(body, out_shape=tuple(out_shapes), in_specs=[_ANY] * len(arrs),
                          out_specs=tuple(_ANY for _ in out_shapes),
                          scratch_shapes=[pltpu.SemaphoreType.DMA((n_sems,)), pltpu.SemaphoreType.DMA((n_sems,))],
                          name=name)(*arrs)


def _gather_chips(arrs, *, name):
    n = len(arrs)

    def body(*refs):
        a_refs, out_refs, send_sems, recv_sems = refs[:n], refs[n:2 * n], refs[2 * n], refs[2 * n + 1]
        x, y, c = _place()
        me = 2 * x + y
        sibling = (x, y, 1 - c)
        chips = _other_chips(x, y)

        def copy(j, k, src, dst, to):
            return pltpu.make_async_remote_copy(src, dst, send_sems.at[6 * j + k], recv_sems.at[6 * j + k],
                                                device_id=to, device_id_type=_MESH)

        first = [copy(j, k, a_refs[j].at[c], out_refs[j].at[me, c], (px, py, c))
                 for k, (px, py) in enumerate(chips) for j in range(n)]
        for cp in first:
            cp.start()
        passed = []
        for k, (px, py) in enumerate(chips):
            for j in range(n):
                landed = out_refs[j].at[2 * px + py, c]
                copy(j, k, a_refs[j].at[c], landed, (px, py, c)).wait_recv()
                passed.append(copy(j, 3 + k, landed, landed, sibling))
                passed[-1].start()
        for k, (px, py) in enumerate(chips):
            for j in range(n):
                copy(j, 3 + k, a_refs[j].at[c], out_refs[j].at[2 * px + py, 1 - c], sibling).wait_recv()
        for cp in first + passed:
            cp.wait_send()

    return _comm_call(body, [_SDS((N_CHIPS,) + a.shape, a.dtype) for a in arrs], 6 * n, arrs, name)


def _alltoall_chips(arrs, modes, *, name):
    n = len(arrs)
    slot = lambda ref, mode, s: _slot_ref(ref, mode, s)
    lane_slots = modes

    def body(*refs):
        g_refs, out_refs, send_sems, recv_sems = refs[:n], refs[n:2 * n], refs[2 * n], refs[2 * n + 1]
        x, y, c = _place()
        me = 2 * x + y

        def copy(j, s):
            return pltpu.make_async_remote_copy(slot(g_refs[j], lane_slots[j], s), out_refs[j].at[me],
                                                send_sems.at[N_CHIPS * j + s], recv_sems.at[N_CHIPS * j + me],
                                                device_id=(s // 2, s % 2, c), device_id_type=_MESH)

        for s in range(N_CHIPS):
            @pl.when(s != me)
            def _():
                for j in range(n):
                    copy(j, s).start()
        for t in range(N_CHIPS):
            @pl.when(t != me)
            def _():
                for j in range(n):
                    pltpu.make_async_remote_copy(slot(g_refs[j], lane_slots[j], t), out_refs[j].at[t],
                                                 send_sems.at[N_CHIPS * j + t], recv_sems.at[N_CHIPS * j + t],
                                                 device_id=(t // 2, t % 2, c), device_id_type=_MESH).wait_recv()
        for s in range(N_CHIPS):
            @pl.when(s != me)
            def _():
                for j in range(n):
                    copy(j, s).wait_send()

    outs = [_SDS((N_CHIPS,) + _slot_shape(a, m), a.dtype) for a, m in zip(arrs, modes)]
    return _comm_call(body, outs, N_CHIPS * n, arrs, name)


def _swap_other_half(arrs, *, name):
    n = len(arrs)

    def body(*refs):
        g_refs, out_refs, send_sems, recv_sems = refs[:n], refs[n:2 * n], refs[2 * n], refs[2 * n + 1]
        x, y, c = _place()
        cps = [pltpu.make_async_remote_copy(g_refs[j].at[:, 1 - c], out_refs[j], send_sems.at[j], recv_sems.at[j],
                                            device_id=(x, y, 1 - c), device_id_type=_MESH) for j in range(n)]
        for cp in cps:
            cp.start()
        for cp in cps:
            cp.wait()

    return _comm_call(body, [_SDS((a.shape[0],) + a.shape[2:], a.dtype) for a in arrs], n, arrs, name)


def _swap_sibling(arrs, *, name):
    n = len(arrs)

    def body(*refs):
        f_refs, out_refs, send_sems, recv_sems = refs[:n], refs[n:2 * n], refs[2 * n], refs[2 * n + 1]
        x, y, c = _place()
        cps = [pltpu.make_async_remote_copy(f_refs[j], out_refs[j], send_sems.at[j], recv_sems.at[j],
                                            device_id=(x, y, 1 - c), device_id_type=_MESH) for j in range(n)]
        for cp in cps:
            cp.start()
        for cp in cps:
            cp.wait()

    return _comm_call(body, [_SDS(a.shape, a.dtype) for a in arrs], n, arrs, name)


_HBM = pl.BlockSpec(memory_space=pltpu.HBM)
_SEM = pl.BlockSpec(memory_space=pltpu.SEMAPHORE)
_EFFECT = pltpu.SideEffectType.DATAFLOW_SIDE_EFFECTING


def _slot_ref(ref, mode, s):
    return ref if mode == "same" else ref.at[s]


def _slot_shape(a, mode):
    return a.shape if mode == "same" else a.shape[1:]


def _send_start(arrs, modes, after, *, name):
    n = len(arrs)
    lands = [lax.empty((N_CHIPS,) + _slot_shape(a, m), a.dtype) for a, m in zip(arrs, modes)]

    def body(*refs):
        srcs, land_refs, send_sems, recv_sems, token = refs[:n], refs[n:2 * n], refs[2 * n + 1], refs[2 * n + 2], refs[-1]
        x, y, c = _place()
        me = 2 * x + y
        for s in range(N_CHIPS):
            @pl.when(s != me)
            def _():
                for j in range(n):
                    pltpu.make_async_remote_copy(_slot_ref(srcs[j], modes[j], s), land_refs[j].at[me],
                                                 send_sems.at[N_CHIPS * j + s], recv_sems.at[N_CHIPS * j + me],
                                                 device_id=(s // 2, s % 2, c), device_id_type=_MESH).start()
        token[...] = jnp.zeros_like(token)

    hbm = lambda a: pltpu.HBM(a.shape, a.dtype)
    sems = pltpu.SemaphoreType.DMA((N_CHIPS * n,))
    out = pl.pallas_call(
        body, name=name, out_shape=(sems, sems, *[hbm(a) for a in arrs], *[hbm(a) for a in lands], _SDS((8, LANES), f32)),
        in_specs=[_HBM] * (2 * n) + [_ANY], out_specs=(_SEM, _SEM, *[_HBM] * (2 * n), pl.BlockSpec(memory_space=pltpu.VMEM)),
        input_output_aliases={j: 2 + j for j in range(2 * n)},
        compiler_params=pltpu.CompilerParams(has_side_effects=_EFFECT),
    )(*[pltpu.with_memory_space_constraint(a, pltpu.HBM) for a in arrs + lands], after)
    return out[:-1], out[-1]


def _send_wait(started, modes, after, *, name):
    send_sems, recv_sems = started[0], started[1]
    n = (len(started) - 2) // 2
    thru = list(started[2:])

    def body(*refs):
        srcs, land_refs, send_sems, recv_sems = refs[:n], refs[n:2 * n], refs[2 * n], refs[2 * n + 1]
        x, y, c = _place()
        me = 2 * x + y
        for s in range(N_CHIPS):
            @pl.when(s != me)
            def _():
                for j in range(n):
                    cp = pltpu.make_async_remote_copy(_slot_ref(srcs[j], modes[j], s), land_refs[j].at[s],
                                                      send_sems.at[N_CHIPS * j + s], recv_sems.at[N_CHIPS * j + s],
                                                      device_id=(s // 2, s % 2, c), device_id_type=_MESH)
                    cp.wait_send()
                    cp.wait_recv()

    hbm = lambda a: pltpu.HBM(a.shape, a.dtype)
    out = pl.pallas_call(
        body, name=name, out_shape=tuple(hbm(a) for a in thru), in_specs=[_HBM] * (2 * n) + [_SEM, _SEM, _ANY],
        out_specs=tuple([_HBM] * (2 * n)), input_output_aliases={j: j for j in range(2 * n)},
        compiler_params=pltpu.CompilerParams(has_side_effects=_EFFECT),
    )(*thru, send_sems, recv_sems, after)
    return list(out[n:])


def _gather_all(a, *, name):
    def body(a_ref, out_ref, send_sems, recv_sems, local_sem):
        x, y, c = _place()
        flip = lambda v, f: (1 - v) if f else v
        peers = [(flip(x, f & 4), flip(y, f & 2), flip(c, f & 1)) for f in range(1, 8)]
        me = 4 * x + 2 * y + c
        mine = pltpu.make_async_copy(a_ref, out_ref.at[me], local_sem)
        mine.start()
        sends = [pltpu.make_async_remote_copy(a_ref, out_ref.at[me], send_sems.at[k], recv_sems.at[k], device_id=peer,
                                              device_id_type=_MESH) for k, peer in enumerate(peers)]
        for cp in sends:
            cp.start()
        for k, (px, py, pc) in enumerate(peers):
            pltpu.make_async_remote_copy(a_ref, out_ref.at[4 * px + 2 * py + pc], send_sems.at[k], recv_sems.at[k],
                                         device_id=(px, py, pc), device_id_type=_MESH).wait_recv()
        for cp in sends:
            cp.wait_send()
        mine.wait()

    return pl.pallas_call(body, out_shape=_SDS((8,) + a.shape, a.dtype), in_specs=[_ANY], out_specs=_ANY,
                          scratch_shapes=[pltpu.SemaphoreType.DMA((7,)), pltpu.SemaphoreType.DMA((7,)),
                                          pltpu.SemaphoreType.DMA], name=name)(a)


def _add_my_half(g, r, *, name):
    p, _, h, w = g.shape
    tw = _pick(w, (2048, 1024, 512, 256, 128))
    rb = max(d for d in range(16, h + 1, 16) if h % d == 0 and d * tw * 4 <= (2 << 20))

    def body(c_ref, g_ref, r_ref, o_ref):
        o_ref[...] = (g_ref[...] + r_ref[...]).astype(o_ref.dtype)

    blk = pl.BlockSpec((None, rb, tw), lambda s, i, j, c_ref: (s, i, j))
    grid_spec = pltpu.PrefetchScalarGridSpec(
        num_scalar_prefetch=1, grid=(p, h // rb, w // tw),
        in_specs=[pl.BlockSpec((None, None, rb, tw), lambda s, i, j, c_ref: (s, c_ref[0], i, j)), blk], out_specs=blk)
    c = lax.axis_index("c").astype(jnp.int32).reshape(1)
    return pl.pallas_call(body, out_shape=_SDS((p, h, w), _WIRE), grid_spec=grid_spec,
                          compiler_params=_cp(("parallel", "parallel", "parallel")), name=name)(c, g, r)


_WEIGHTS = ("pre_norm_g", "post_norm_g", "w_in", "b_in", "w_out", "fox_forget_bias", "nsa_cmp_pos_k", "nsa_cmp_w1_k",
            "nsa_cmp_w2_k", "nsa_cmp_pos_v", "nsa_cmp_w1_v", "nsa_cmp_w2_v", "mla_q_norm_g", "mla_w_uq",
            "mla_kv_norm_g", "mla_w_ukv")
_SHARD_AXIS = {"w_in": 2, "w_out": 1, "nsa_cmp_w1_k": 1, "nsa_cmp_w1_v": 1, "mla_w_uq": 2, "mla_w_ukv": 2}
_PACK_UNIT = 16 * LANES


def _pack(arrays, dtype):
    rows = []
    for a in arrays:
        v = a.astype(dtype).reshape(-1)
        pad = (-v.shape[0]) % _PACK_UNIT
        if pad:
            v = jnp.concatenate([v, jnp.zeros((pad,), dtype)])
        rows.append(v.reshape(-1, LANES))
    return jnp.concatenate(rows, axis=0)


def _unpack(flat, shapes):
    out, r = [], 0
    for shp in shapes:
        n = int(np.prod(shp))
        nr = -(-n // _PACK_UNIT) * (_PACK_UNIT // LANES)
        out.append(flat[r:r + nr].reshape(-1)[:n].reshape(shp))
        r += nr
    return out


def kernel(x, pre_norm_g, post_norm_g, w_in, b_in, w_out, fox_forget_bias, nsa_cmp_pos_k, nsa_cmp_w1_k, nsa_cmp_w2_k, nsa_cmp_pos_v, nsa_cmp_w1_v, nsa_cmp_w2_v, mla_q_norm_g, mla_w_uq, mla_kv_norm_g, mla_w_ukv, loss_target, m_pre_norm_g, m_post_norm_g, m_w_in, m_b_in, m_w_out, m_fox_forget_bias, m_nsa_cmp_pos_k, m_nsa_cmp_w1_k, m_nsa_cmp_w2_k, m_nsa_cmp_pos_v, m_nsa_cmp_w1_v, m_nsa_cmp_w2_v, m_mla_q_norm_g, m_mla_w_uq, m_mla_kv_norm_g, m_mla_w_ukv, v_pre_norm_g, v_post_norm_g, v_w_in, v_b_in, v_w_out, v_fox_forget_bias, v_nsa_cmp_pos_k, v_nsa_cmp_w1_k, v_nsa_cmp_w2_k, v_nsa_cmp_pos_v, v_nsa_cmp_w1_v, v_nsa_cmp_w2_v, v_mla_q_norm_g, v_mla_w_uq, v_mla_kv_norm_g, v_mla_w_ukv):
    given = dict(locals())
    local = {n: given[n] for n in _WEIGHTS}
    depth = pre_norm_g.shape[0]
    xs, target = x[0], loss_target[0]
    s = xs.shape[0]
    sharded = [n for n in _WEIGHTS if n in _SHARD_AXIS and n != "w_in"]
    small = [n for n in _WEIGHTS if n not in _SHARD_AXIS]
    chip = 2 * lax.axis_index("x") + lax.axis_index("y")
    core = lax.axis_index("c")
    own = lambda slots, mine: lax.dynamic_update_slice_in_dim(slots, mine[None], chip, axis=0)

    w_in_t = jnp.swapaxes(w_in, 1, 2).astype(_MXU)
    piece = lax.switch(chip, [functools.partial(_piece_from_shard, s=k) for k in range(N_CHIPS)], w_in_t)
    layer_shapes = [local[n].shape[1:] for n in sharded]
    flat = [_pack([local[n][l] for n in sharded], _MXU) for l in range(depth)]
    full = dict(local)
    for n in ["w_in"] + sharded:
        full[n] = []

    def add_layer(w_in_slots, flat_slots_):
        full["w_in"].append(w_in_slots)
        per_chip = [_unpack(flat_slots_[k], layer_shapes) for k in range(N_CHIPS)]
        for j, n in enumerate(sharded):
            full[n].append(jnp.concatenate([per_chip[k][j] for k in range(N_CHIPS)], axis=_SHARD_AXIS[n] - 1))

    halved = [piece[0].reshape(2, GROUP_W // 2, D_MODEL), flat[0].reshape(2, -1, LANES)]
    first_all = [own(a, b) for a, b in zip(_gather_chips(halved, name="gather_weights"), halved)]
    add_layer(first_all[0].reshape(N_CHIPS, GROUP_W, D_MODEL), first_all[1].reshape((N_CHIPS,) + flat[0].shape))
    later = [piece[l] for l in range(1, depth)] + flat[1:]
    started, token = _send_start(later, ["same"] * len(later), first_all[1], name="gather_later_start")
    full["pre_norm_g"] = pre_norm_g + token[0, 0]

    consts = _consts(s)
    params, act, saved = [], xs, []
    for l in range(depth):
        if l == 1:
            landed = [own(a, b) for a, b in zip(_send_wait(started, ["same"] * len(later), act,
                                                           name="gather_later_wait"), later)]
            for k in range(depth - 1):
                add_layer(landed[k], landed[depth - 1 + k])
        params.append(_layer_params(full, l))
        act, sv = _layer_fwd(act, params[l], consts, f"l{l}")
        saved.append(sv)
    dy, loss_parts = _loss_head(act, target, name="loss_head")

    def flat_slots(g, dtype):
        def part(n, k):
            a, ax = g[n], _SHARD_AXIS[n] - 1
            w = a.shape[ax] // N_CHIPS
            return lax.slice_in_dim(a, k * w, (k + 1) * w, axis=ax)
        return jnp.stack([_pack([part(n, k) for n in sharded], dtype) for k in range(N_CHIPS)])

    own_slot = lambda a: lax.dynamic_index_in_dim(a, chip, axis=0, keepdims=False)
    slots_of = lambda g: g["w_in"].reshape(N_CHIPS, GROUP_W, D_MODEL)

    modes = ["slots", "slots"]
    layer_grads, in_flight = [None] * depth, {}
    for l in reversed(range(depth)):
        dy, layer_grads[l] = _layer_bwd(dy, saved[l], params[l], consts, f"l{l}")
        if l > 0:
            wire = [slots_of(layer_grads[l]).astype(_WIRE), flat_slots(layer_grads[l], _WIRE)]
            started, token = _send_start(wire, modes, dy, name=f"reduce_l{l}_start")
            in_flight[l] = (started, wire)
            params[l - 1] = dict(params[l - 1], post_g=params[l - 1]["post_g"] + token[0, 0])
    grad_x = dy[None]
    grads = {n: jnp.stack([layer_grads[l][n] for l in range(depth)]) for n in small}
    loss_row = jnp.concatenate([jnp.sum(loss_parts).reshape(1), jnp.zeros((LANES - 1,), f32)])
    small_shapes = [(LANES,)] + [grads[n].shape for n in small]
    contrib = _pack([loss_row] + [grads[n] for n in small], f32)

    halves = [slots_of(layer_grads[0]).reshape(N_CHIPS, 2, GROUP_W // 2, D_MODEL),
              flat_slots(layer_grads[0], f32).reshape(N_CHIPS, 2, -1, LANES)]
    from_sibling = _swap_other_half(halves, name="reduce_pair")
    pair_sum = [_add_my_half(g, r, name=f"reduce_pair_add{j}") for j, (g, r) in enumerate(zip(halves, from_sibling))]
    from_chips = _alltoall_chips(pair_sum + [contrib], modes + ["same"], name="reduce_chips")
    my_half = [_sum_slots(own(slots, own_slot(ps)), name=f"reduce_chips_add{j}")
               for j, (slots, ps) in enumerate(zip(from_chips, pair_sum))]
    partial = []
    for l in range(1, depth):
        started, wire = in_flight[l]
        landed = _send_wait(started, modes, dy, name=f"reduce_l{l}_wait")
        partial += [_sum_slots(own(slots, own_slot(a)), name=f"reduce_l{l}_add{j}")
                    for j, (slots, a) in enumerate(zip(landed, wire))]
    partial.append(_sum_slots(own(from_chips[2], contrib), name="sum_small"))
    theirs = _swap_sibling(my_half + partial, name="reduce_share")
    first = core == 0
    whole = [jnp.concatenate([jnp.where(first, a, b), jnp.where(first, b, a)], axis=0)
             for a, b in zip(my_half, theirs[:2])]
    whole += [_add2(a[None], b[None], name=f"reduce_cores_add{j}")[0] for j, (a, b) in enumerate(zip(partial, theirs[2:]))]
    unpiece = [functools.partial(_shard_from_piece, s=k) for k in range(N_CHIPS)]
    summed = {"w_in": jnp.stack([lax.switch(chip, unpiece, whole[2 * l].T) for l in range(depth)])}
    rest = [_unpack(whole[2 * l + 1], layer_shapes) for l in range(depth)]
    for j, n in enumerate(sharded):
        summed[n] = jnp.stack([rest[l][j] for l in range(depth)])
    total = _unpack(whole[2 * depth], small_shapes)
    loss = total[0][0]
    summed.update(zip(small, total[1:]))

    deltas, new_m, new_v = {}, {}, {}
    for n in _WEIGHTS:
        deltas[n], new_m[n], new_v[n] = _adamw(local[n], summed[n], given["m_" + n], given["v_" + n], name=f"adamw_{n}")
    return (loss, grad_x, *[summed[n] for n in _WEIGHTS], *[deltas[n] for n in _WEIGHTS],
            *[new_m[n] for n in _WEIGHTS], *[new_v[n] for n in _WEIGHTS])
```

```python
import functools
import math

import numpy as np
import jax
import jax.numpy as jnp
from jax import lax
from jax.experimental import pallas as pl
from jax.experimental.pallas import tpu as pltpu

f32 = jnp.float32
bf16 = jnp.bfloat16
_MXU = jnp.bfloat16
_WIRE = jnp.bfloat16
_SDS = jax.ShapeDtypeStruct
_ANY = pl.BlockSpec(memory_space=pl.ANY)
_MESH = pl.DeviceIdType.MESH

D_MODEL = 2048
N_HEADS = 4
HEAD_DIM = 128
GROUP = 512
RMS_EPS = 1e-6
NEG_INF = -1e30
ROPE_THETA = 10000.0
CMP_LEN, CMP_STRIDE, SEL_LEN, SEL_TOPN, WINDOW = 32, 16, 64, 16, 512
FORCED_BONUS = 1e6
MLA_Q_RANK, MLA_KV_RANK, MLA_NOPE, MLA_ROPE = 384, 128, 128, 64
ADAM_LR, ADAM_B1, ADAM_B2, ADAM_EPS, ADAM_WD, ADAM_STEP = 0.001, 0.9, 0.999, 1e-08, 0.01, 10
LANES = 128
VMEM_LIMIT = 56 * 1024 * 1024
HP_FWD, HP_BWD = 2, 2

_SEGS = (
    ("sb_q", 512), ("sb_k", 512), ("sb_v", 512), ("sb_gate", 512), ("nsa_q", 512), ("nsa_k_cmp", 128),
    ("nsa_v_cmp", 128), ("nsa_k_sel", 128), ("nsa_v_sel", 128), ("nsa_k_win", 128), ("nsa_v_win", 128),
    ("nsa_branch", 12), ("nsa_gate", 512), ("fox_q", 512), ("fox_k", 512), ("fox_v", 512), ("fox_f", 4),
    ("fox_gate", 512), ("mla_cq", 384), ("mla_ckv", 128), ("mla_k_rope", 64), ("mla_gate", 512),
)
_ORIG, _WID = {}, {}
_o = 0
for _n, _w in _SEGS:
    _ORIG[_n], _WID[_n] = _o, _w
    _o += _w
IN_WIDTH = _o
N_CHIPS = 4
CHIP_COLS = IN_WIDTH // N_CHIPS
GROUP_W = 2048
ZW = N_CHIPS * GROUP_W
_GROUPS = (
    (("sb_q", 0, 512, 0), ("sb_k", 0, 512, 512), ("sb_v", 0, 512, 1024), ("sb_gate", 0, 212, 1536)),
    (("nsa_q", 0, 512, 0), ("nsa_k_cmp", 0, 128, 512), ("nsa_v_cmp", 0, 128, 640), ("nsa_k_sel", 0, 128, 768),
     ("nsa_v_sel", 0, 128, 896), ("nsa_k_win", 0, 128, 1024), ("nsa_v_win", 0, 128, 1152), ("nsa_branch", 0, 12, 1280),
     ("sb_gate", 212, 512, 1408), ("nsa_gate", 0, 156, 1712)),
    (("fox_q", 0, 512, 0), ("fox_k", 0, 512, 512), ("fox_v", 0, 368, 1024), ("nsa_gate", 156, 512, 1408)),
    (("mla_cq", 0, 384, 0), ("mla_ckv", 0, 128, 384), ("mla_k_rope", 0, 64, 512), ("fox_f", 0, 4, 640),
     ("fox_v", 368, 512, 768), ("fox_gate", 0, 512, 1024), ("mla_gate", 0, 512, 1536)),
)
_PIECES = {n: [] for n, _ in _SEGS}
for _s, _grp in enumerate(_GROUPS):
    _cover = sorted((_ORIG[n] + lo, _ORIG[n] + hi) for n, lo, hi, _ in _grp)
    assert _cover[0][0] == _s * CHIP_COLS and _cover[-1][1] == (_s + 1) * CHIP_COLS
    assert all(a[1] == b[0] for a, b in zip(_cover, _cover[1:]))
    _ends = sorted((off, off + hi - lo) for _, lo, hi, off in _grp)
    assert all(a[1] <= b[0] for a, b in zip(_ends, _ends[1:])) and _ends[-1][1] <= GROUP_W
    assert _ends[0][0] == 0 and all(e[0] % 16 == 0 for e in _ends)
    for _n, _lo, _hi, _off in _grp:
        _PIECES[_n].append((_s * GROUP_W + _off, _lo, _hi))
_AL = {n: p[0][0] for n, p in _PIECES.items() if len(p) == 1}


def _cp(sem=None):
    return pltpu.CompilerParams(dimension_semantics=sem, vmem_limit_bytes=VMEM_LIMIT)


def _mm(a, b):
    return jnp.dot(a.astype(_MXU), b.astype(_MXU), preferred_element_type=f32)


def _mm_nt(a, b):
    return lax.dot_general(a.astype(_MXU), b.astype(_MXU), (((1,), (1,)), ((), ())), preferred_element_type=f32)


def _mm_tn(a, b):
    return lax.dot_general(a.astype(_MXU), b.astype(_MXU), (((0,), (0,)), ((), ())), preferred_element_type=f32)


def _mm_split(x, t):
    hi = x.astype(_MXU)
    lo = (x - hi.astype(f32)).astype(_MXU)
    return jnp.dot(hi, t, preferred_element_type=f32) + jnp.dot(lo, t, preferred_element_type=f32)


def _sigmoid(x):
    return 1.0 / (1.0 + jnp.exp(-x))


def _iota(shape, dim):
    return lax.broadcasted_iota(jnp.int32, shape, dim)


def _pick(n, prefs):
    for p in prefs:
        if n % p == 0:
            return p
    return n


def _matmul(a, b, mode, *, bias=None, out_dtype=f32, name):
    grouped = b.ndim == 3
    b_shape = (b.shape[0] * b.shape[1], b.shape[2]) if grouped else b.shape
    if mode == "nn":
        (M, K), (K2, N) = a.shape, b_shape
    elif mode == "nt":
        (M, K), (N, K2) = a.shape, b_shape
    else:
        (K, M), (K2, N) = a.shape, b_shape
    assert K == K2
    tm = _pick(M, (1024, 512, 384, 256, 128))
    tn = _pick(N, (1024, 512, 384, 256, 128))
    tk = K if K <= 2048 else _pick(K, (2048, 2432, 1024, 512))
    nk = K // tk
    a_spec = {"nn": pl.BlockSpec((tm, tk), lambda i, j, k: (i, k)),
              "nt": pl.BlockSpec((tm, tk), lambda i, j, k: (i, k)),
              "tn": pl.BlockSpec((tk, tm), lambda i, j, k: (k, i))}[mode]
    if not grouped:
        b_spec = {"nn": pl.BlockSpec((tk, tn), lambda i, j, k: (k, j)),
                  "nt": pl.BlockSpec((tn, tk), lambda i, j, k: (j, k)),
                  "tn": pl.BlockSpec((tk, tn), lambda i, j, k: (k, j))}[mode]
    elif mode == "nt":
        per = b.shape[1] // tn
        b_spec = pl.BlockSpec((None, tn, tk), lambda i, j, k: (j // per, j % per, k))
    else:
        assert mode == "nn"
        per = b.shape[1] // tk
        b_spec = pl.BlockSpec((None, tk, tn), lambda i, j, k: (k // per, k % per, j))
    dot = {"nn": _mm, "nt": _mm_nt, "tn": _mm_tn}[mode]
    has_bias = bias is not None

    def body(*refs):
        if has_bias:
            a_ref, b_ref, bias_ref, o_ref, acc_ref = refs
        else:
            a_ref, b_ref, o_ref, acc_ref = refs
            bias_ref = None
        k = pl.program_id(2)
        part = dot(a_ref[...], b_ref[...])

        def finish(total):
            if has_bias:
                total = total + bias_ref[...]
            o_ref[...] = total.astype(o_ref.dtype)

        if nk == 1:
            finish(part)
        else:
            @pl.when(k == 0)
            def _():
                acc_ref[...] = part

            @pl.when(k > 0)
            def _():
                acc_ref[...] += part

            @pl.when(k == nk - 1)
            def _():
                finish(acc_ref[...])

    in_specs = [a_spec, b_spec]
    args = [a, b]
    if has_bias:
        in_specs.append(pl.BlockSpec((1, tn), lambda i, j, k: (0, j)))
        args.append(bias.reshape(1, N))
    return pl.pallas_call(
        body, out_shape=_SDS((M, N), out_dtype), grid=(M // tm, N // tn, nk),
        in_specs=in_specs, out_specs=pl.BlockSpec((tm, tn), lambda i, j, k: (i, j)),
        scratch_shapes=[pltpu.VMEM((tm, tn), f32)],
        compiler_params=_cp(("parallel", "parallel", "arbitrary")), name=name,
    )(*args)


def _row_block(s):
    return _pick(s, (256, 128))


def _rms_fwd(x, g, *, out_dtype, name):
    s, d = x.shape
    rb = _row_block(s)

    def body(x_ref, g_ref, o_ref):
        xv = x_ref[...]
        r = lax.rsqrt(jnp.mean(xv * xv, axis=-1, keepdims=True) + RMS_EPS)
        o_ref[...] = (xv * r * g_ref[...]).astype(o_ref.dtype)

    return pl.pallas_call(
        body, out_shape=_SDS((s, d), out_dtype), grid=(s // rb,),
        in_specs=[pl.BlockSpec((rb, d), lambda i: (i, 0)), pl.BlockSpec((1, d), lambda i: (0, 0))],
        out_specs=pl.BlockSpec((rb, d), lambda i: (i, 0)), compiler_params=_cp(("parallel",)), name=name,
    )(x, g.reshape(1, d))


def _postnorm_fwd(u, g, x, *, name):
    s, d = u.shape
    rb = _row_block(s)

    def body(u_ref, g_ref, x_ref, o_ref):
        uv = u_ref[...]
        r = lax.rsqrt(jnp.mean(uv * uv, axis=-1, keepdims=True) + RMS_EPS)
        o_ref[...] = x_ref[...] + uv * r * g_ref[...]

    return pl.pallas_call(
        body, out_shape=_SDS((s, d), f32), grid=(s // rb,),
        in_specs=[pl.BlockSpec((rb, d), lambda i: (i, 0)), pl.BlockSpec((1, d), lambda i: (0, 0)),
                  pl.BlockSpec((rb, d), lambda i: (i, 0))],
        out_specs=pl.BlockSpec((rb, d), lambda i: (i, 0)), compiler_params=_cp(("parallel",)), name=name,
    )(u, g.reshape(1, d), x)


def _fold_rows(v):
    r = v.shape[0]
    acc = v[0:8]
    for k in range(1, r // 8):
        acc = acc + v[8 * k:8 * k + 8]
    return acc


def _rms_bwd(dy, x, g, res=None, *, name):
    s, d = x.shape
    rb = _row_block(s)
    nb = s // rb
    has_res = res is not None

    def body(*refs):
        if has_res:
            dy_ref, x_ref, g_ref, res_ref, dx_ref, dg_ref, acc_ref = refs
        else:
            dy_ref, x_ref, g_ref, dx_ref, dg_ref, acc_ref = refs
        i = pl.program_id(0)
        xv = x_ref[...]
        r = lax.rsqrt(jnp.mean(xv * xv, axis=-1, keepdims=True) + RMS_EPS)
        xh = xv * r
        dyv = dy_ref[...]
        dxh = dyv * g_ref[...]
        dx = r * (dxh - xh * jnp.mean(dxh * xh, axis=-1, keepdims=True))
        if has_res:
            dx = dx + res_ref[...]
        dx_ref[...] = dx
        part = _fold_rows(dyv * xh)

        @pl.when(i == 0)
        def _():
            acc_ref[...] = part

        @pl.when(i > 0)
        def _():
            acc_ref[...] += part

        @pl.when(i == nb - 1)
        def _():
            dg_ref[...] = jnp.sum(acc_ref[...], axis=0, keepdims=True)

    blk = pl.BlockSpec((rb, d), lambda i: (i, 0))
    in_specs = [blk, blk, pl.BlockSpec((1, d), lambda i: (0, 0))] + ([blk] if has_res else [])
    args = [dy, x, g.reshape(1, d)] + ([res] if has_res else [])
    return pl.pallas_call(
        body, out_shape=(_SDS((s, d), f32), _SDS((1, d), f32)), grid=(nb,), in_specs=in_specs,
        out_specs=(blk, pl.BlockSpec((1, d), lambda i: (0, 0))),
        scratch_shapes=[pltpu.VMEM((8, d), f32)], compiler_params=_cp(("arbitrary",)), name=name,
    )(*args)


def _loss_head(y, target, *, name):
    s, d = y.shape
    rb = _row_block(s)
    nb = s // rb

    def body(y_ref, t_ref, dy_ref, l_ref):
        i = pl.program_id(0)
        e = y_ref[...] - t_ref[...]
        dy_ref[...] = e * (1.0 / d)
        rows = _fold_rows(e * e)
        part = rows[:, 0:LANES]
        for k in range(1, d // LANES):
            part = part + rows[:, k * LANES:(k + 1) * LANES]
        part = part * (0.5 / d)

        @pl.when(i == 0)
        def _():
            l_ref[...] = part

        @pl.when(i > 0)
        def _():
            l_ref[...] += part

    blk = pl.BlockSpec((rb, d), lambda i: (i, 0))
    return pl.pallas_call(
        body, out_shape=(_SDS((s, d), f32), _SDS((8, LANES), f32)), grid=(nb,), in_specs=[blk, blk],
        out_specs=(blk, pl.BlockSpec((8, LANES), lambda i: (0, 0))),
        compiler_params=_cp(("arbitrary",)), name=name,
    )(y, target)


def _colsum(a, *, name):
    s, n = a.shape
    rb = _row_block(s)
    nb = s // rb
    tn = _pick(n, (2432, 2048, 1024, 512, 384, 128))

    def body(a_ref, o_ref, acc_ref):
        i = pl.program_id(1)
        part = _fold_rows(a_ref[...].astype(f32))

        @pl.when(i == 0)
        def _():
            acc_ref[...] = part

        @pl.when(i > 0)
        def _():
            acc_ref[...] += part

        @pl.when(i == nb - 1)
        def _():
            o_ref[...] = jnp.sum(acc_ref[...], axis=0, keepdims=True)

    return pl.pallas_call(
        body, out_shape=_SDS((1, n), f32), grid=(n // tn, nb),
        in_specs=[pl.BlockSpec((rb, tn), lambda j, i: (i, j))], out_specs=pl.BlockSpec((1, tn), lambda j, i: (0, j)),
        scratch_shapes=[pltpu.VMEM((8, tn), f32)], compiler_params=_cp(("parallel", "arbitrary")), name=name,
    )(a)


def _gate_fwd(outs, gate, *, name):
    s, d = gate.shape
    rb = _row_block(s)
    n = len(outs)
    w = d // n

    def body(*refs):
        g_ref, m_ref = refs[n], refs[n + 1]
        for k in range(n):
            gv = g_ref[:, k * w:(k + 1) * w]
            m_ref[:, k * w:(k + 1) * w] = (refs[k][...] * (gv * _sigmoid(gv))).astype(m_ref.dtype)

    blk = pl.BlockSpec((rb, d), lambda i: (i, 0))
    part = pl.BlockSpec((rb, w), lambda i: (i, 0))
    return pl.pallas_call(body, out_shape=_SDS((s, d), _MXU), grid=(s // rb,), in_specs=[part] * n + [blk],
                          out_specs=blk, compiler_params=_cp(("parallel",)), name=name)(*outs, gate)


def _gate_bwd(dmix, outs, gate, *, name):
    s, d = gate.shape
    rb = _row_block(s)
    n = len(outs)
    w = d // n

    def body(*refs):
        dm_ref, o_refs, g_ref, do_refs, dg_ref = refs[0], refs[1:1 + n], refs[1 + n], refs[2 + n:2 + 2 * n], refs[-1]
        for k in range(n):
            sl = slice(k * w, (k + 1) * w)
            gv = g_ref[:, sl]
            sg = _sigmoid(gv)
            dm = dm_ref[:, sl]
            do_refs[k][...] = dm * (gv * sg)
            dg_ref[:, sl] = dm * o_refs[k][...] * (sg * (1.0 + gv * (1.0 - sg)))

    blk = pl.BlockSpec((rb, d), lambda i: (i, 0))
    part = pl.BlockSpec((rb, w), lambda i: (i, 0))
    return pl.pallas_call(body, out_shape=tuple(_SDS((s, w), f32) for _ in range(n)) + (_SDS((s, d), f32),),
                          grid=(s // rb,), in_specs=[blk] + [part] * n + [blk], out_specs=(part,) * n + (blk,),
                          compiler_params=_cp(("parallel",)), name=name)(dmix, *outs, gate)


def _adamw(w, g, m, v, *, name):
    shape = w.shape
    cols = shape[-1]
    rows = int(np.prod(shape[:-1])) if len(shape) > 1 else 1
    to2 = lambda t: t.reshape(rows, cols)
    rb = rows
    if rows * cols * 4 > (1 << 20):
        rb = max(d for d in range(8, rows + 1, 8) if rows % d == 0 and (d * cols * 4 <= (1600 << 10) or d == 8))

    def body(w_ref, g_ref, m_ref, v_ref, d_ref, nm_ref, nv_ref):
        gv = g_ref[...]
        mn = ADAM_B1 * m_ref[...] + (1.0 - ADAM_B1) * gv
        vn = ADAM_B2 * v_ref[...] + (1.0 - ADAM_B2) * (gv * gv)
        m_hat = mn / (1.0 - ADAM_B1 ** ADAM_STEP)
        v_hat = vn / (1.0 - ADAM_B2 ** ADAM_STEP)
        d_ref[...] = -ADAM_LR * (m_hat / (jnp.sqrt(v_hat) + ADAM_EPS) + ADAM_WD * w_ref[...])
        nm_ref[...] = mn
        nv_ref[...] = vn

    blk = pl.BlockSpec((rb, cols), lambda i: (i, 0))
    out = pl.pallas_call(body, out_shape=tuple(_SDS((rows, cols), f32) for _ in range(3)), grid=(rows // rb,),
                         in_specs=[blk] * 4, out_specs=(blk,) * 3, compiler_params=_cp(("parallel",)),
                         name=name)(to2(w), to2(g), to2(m), to2(v))
    return tuple(t.reshape(shape) for t in out)


def _sum_slots(a, *, name):
    p, n, c = a.shape
    rb = max(d for d in range(8, n + 1, 8) if n % d == 0 and (p * d * c * 4 <= (6 << 20) or d == 8))

    def body(a_ref, o_ref):
        acc = a_ref[0].astype(f32)
        for k in range(1, p):
            acc = acc + a_ref[k].astype(f32)
        o_ref[...] = acc

    return pl.pallas_call(body, out_shape=_SDS((n, c), f32), grid=(n // rb,),
                          in_specs=[pl.BlockSpec((p, rb, c), lambda i: (0, i, 0))],
                          out_specs=pl.BlockSpec((rb, c), lambda i: (i, 0)), compiler_params=_cp(("parallel",)),
                          name=name)(a)


def _add2(a, b, *, name):
    p, n, c = a.shape
    rb = max(d for d in range(8, n + 1, 8) if n % d == 0 and (d * c * 4 <= (2 << 20) or d == 8))

    def body(a_ref, b_ref, o_ref):
        o_ref[...] = a_ref[...] + b_ref[...]

    blk = pl.BlockSpec((1, rb, c), lambda s, i: (s, i, 0))
    return pl.pallas_call(body, out_shape=_SDS((p, n, c), f32), grid=(p, n // rb), in_specs=[blk, blk], out_specs=blk,
                          compiler_params=_cp(("parallel", "parallel")), name=name)(a, b)


def _rope_tables(pos, dim):
    half = dim // 2
    inv = ROPE_THETA ** (-jnp.arange(half, dtype=f32) / half)
    ang = pos.astype(f32)[:, None] * inv[None, :]
    c, s = jnp.cos(ang), jnp.sin(ang)
    z = jnp.zeros_like(c)
    pad = [jnp.zeros((pos.shape[0], LANES - dim), f32)] if dim < LANES else []
    return (jnp.concatenate([c, c] + pad, axis=1), jnp.concatenate([-s, z] + pad, axis=1),
            jnp.concatenate([z, s] + pad, axis=1))


def _rope(x, cos, sa, sb, half, transpose=False):
    if transpose:
        return x * cos + pltpu.roll(x * sa, half, 1) + pltpu.roll(x * sb, LANES - half, 1)
    return x * cos + pltpu.roll(x, LANES - half, 1) * sa + pltpu.roll(x, half, 1) * sb


def _rope_call(items, tables, half, transpose, *, name):
    s = items[0][0].shape[0]
    rb = _row_block(s)
    n = len(items)

    def body(*refs):
        cos, sa, sb = refs[n][...], refs[n + 1][...], refs[n + 2][...]
        for k in range(n):
            x_ref, o_ref = refs[k], refs[n + 3 + k]
            for j in range(items[k][1] // LANES):
                sl = slice(j * LANES, (j + 1) * LANES)
                o_ref[:, sl] = _rope(x_ref[:, sl], cos, sa, sb, half, transpose)

    in_specs = [pl.BlockSpec((rb, w), functools.partial(lambda i, cb: (i, cb), cb=cb)) for _, w, cb in items]
    in_specs += [pl.BlockSpec((rb, LANES), lambda i: (i, 0))] * 3
    out_specs = tuple(pl.BlockSpec((rb, w), lambda i: (i, 0)) for _, w, _ in items)
    return pl.pallas_call(
        body, out_shape=tuple(_SDS((s, w), f32) for _, w, _ in items), grid=(s // rb,), in_specs=in_specs,
        out_specs=out_specs, compiler_params=_cp(("parallel",)), name=name,
    )(*[a for a, _, _ in items], *tables)


def _attn_block(s):
    return _pick(s, (512, 256, 128))


def _lower_mask(b, strict):
    r, c = _iota((b, b), 0), _iota((b, b), 1)
    return (c < r) if strict else (c <= r)


def _pick_lane(block, h):
    return jnp.sum(jnp.where(_iota(block.shape, 1) == h, block, 0.0), axis=1, keepdims=True)


def _head_bias(cum_blk, g, j, hp):
    if hp == N_HEADS:
        return cum_blk[:, j:j + 1]
    return _pick_lane(cum_blk, g * hp + j)


def _attn_fwd(q, k, v, qcol, kcol, vcol, dq, cum, cum_t, *, scale, hp, name):
    s = q.shape[0]
    b = _attn_block(s)
    nq = s // b
    has_bias = cum is not None
    assert qcol % hp == 0 and kcol % hp == 0 and vcol % hp == 0

    def body(*refs):
        if has_bias:
            q_ref, k_ref, v_ref, cum_ref, cumt_ref, o_ref, lse_ref = refs
        else:
            q_ref, k_ref, v_ref, o_ref, lse_ref = refs
        g, i = pl.program_id(0), pl.program_id(1)
        qs = [q_ref[:, j * dq:(j + 1) * dq].astype(_MXU) for j in range(hp)]
        cqs = [_head_bias(cum_ref[...], g, j, hp) for j in range(hp)] if has_bias else None

        def chunk(c, carry, diag):
            st = pl.multiple_of(c * b, b)
            mask = _lower_mask(b, False) if diag else None
            out = []
            for j in range(hp):
                m, l, acc = carry[j]
                z = _mm_nt(qs[j], k_ref[pl.ds(st, b), j * dq:(j + 1) * dq]) * scale
                if has_bias:
                    z = z + (cqs[j] - cumt_ref[j, c])
                if diag:
                    z = jnp.where(mask, z, NEG_INF)
                m_new = jnp.maximum(m, jnp.max(z, axis=1, keepdims=True))
                p = jnp.exp(z - m_new)
                if diag:
                    p = jnp.where(mask, p, 0.0)
                alpha = jnp.exp(m - m_new)
                l = alpha * l + jnp.sum(p, axis=1, keepdims=True)
                acc = alpha * acc + _mm(p, v_ref[pl.ds(st, b), j * HEAD_DIM:(j + 1) * HEAD_DIM])
                out.append((m_new, l, acc))
            return tuple(out)

        init = tuple((jnp.full((b, 1), NEG_INF, f32), jnp.zeros((b, 1), f32), jnp.zeros((b, HEAD_DIM), f32))
                     for _ in range(hp))
        carry = lax.fori_loop(0, i, lambda c, cr: chunk(c, cr, False), init)
        for j, (m, l, acc) in enumerate(chunk(i, carry, True)):
            o_ref[:, j * HEAD_DIM:(j + 1) * HEAD_DIM] = acc / l
            lse_ref[j] = m + jnp.log(l)

    in_specs = [pl.BlockSpec((b, hp * dq), lambda g, i: (i, qcol // hp + g)),
                pl.BlockSpec((s, hp * dq), lambda g, i: (0, kcol // hp + g)),
                pl.BlockSpec((s, hp * HEAD_DIM), lambda g, i: (0, vcol // hp + g))]
    args = [q, k, v]
    if has_bias:
        in_specs += [pl.BlockSpec((b, LANES), lambda g, i: (i, 0)),
                     pl.BlockSpec((hp, nq, 1, b), lambda g, i: (g, 0, 0, 0))]
        args += [cum, cum_t]
    return pl.pallas_call(
        body, out_shape=(_SDS((s, N_HEADS * HEAD_DIM), f32), _SDS((N_HEADS, s, 1), f32)), grid=(N_HEADS // hp, nq),
        in_specs=in_specs,
        out_specs=(pl.BlockSpec((b, hp * HEAD_DIM), lambda g, i: (i, g)),
                   pl.BlockSpec((hp, b, 1), lambda g, i: (g, i, 0))),
        compiler_params=_cp(("parallel", "parallel")), name=name,
    )(*args)


def _attn_bwd(q, k, v, qcol, kcol, vcol, dq, do, o, lse, cum, cum_t, *, scale, hp, name):
    s = q.shape[0]
    b = _attn_block(s)
    nq = s // b
    has_bias = cum is not None
    assert qcol % hp == 0 and kcol % hp == 0 and vcol % hp == 0
    hd = lambda j: slice(j * HEAD_DIM, (j + 1) * HEAD_DIM)
    hq = lambda j: slice(j * dq, (j + 1) * dq)

    def body(*refs):
        if has_bias:
            (q_ref, k_ref, v_ref, do_ref, o_ref, lse_ref, cum_ref, cumt_ref, dq_ref, dk_ref, dv_ref, dck_ref,
             p_sc, dp_sc) = refs
        else:
            q_ref, k_ref, v_ref, do_ref, o_ref, lse_ref, dq_ref, dk_ref, dv_ref = refs
        g, i = pl.program_id(0), pl.program_id(1)

        @pl.when(i == 0)
        def _():
            dk_ref[...] = jnp.zeros_like(dk_ref)
            dv_ref[...] = jnp.zeros_like(dv_ref)
            if has_bias:
                dck_ref[...] = jnp.zeros_like(dck_ref)

        qs = [q_ref[:, hq(j)].astype(_MXU) for j in range(hp)]
        dos = [do_ref[:, hd(j)].astype(_MXU) for j in range(hp)]
        lses = [lse_ref[j] for j in range(hp)]
        cqs = [_head_bias(cum_ref[...], g, j, hp) for j in range(hp)] if has_bias else None

        def probs(j, c, diag):
            st = pl.multiple_of(c * b, b)
            z = _mm_nt(qs[j], k_ref[pl.ds(st, b), hq(j)]) * scale
            if has_bias:
                z = z + (cqs[j] - cumt_ref[j, c])
            p = jnp.exp(z - lses[j])
            if diag:
                p = jnp.where(_lower_mask(b, False), p, 0.0)
            return p, _mm_nt(dos[j], v_ref[pl.ds(st, b), hd(j)])

        if has_bias:
            def first(c, accs, diag):
                out = []
                for j in range(hp):
                    p, dp = probs(j, c, diag)
                    p_sc[j, c] = p
                    dp_sc[j, c] = dp
                    out.append(accs[j] + jnp.sum(p * dp, axis=1, keepdims=True))
                return tuple(out)

            deltas = lax.fori_loop(0, i, lambda c, a: first(c, a, False),
                                   tuple(jnp.zeros((b, 1), f32) for _ in range(hp)))
            deltas = first(i, deltas, True)
        else:
            deltas = [jnp.sum(do_ref[:, hd(j)] * o_ref[:, hd(j)], axis=1, keepdims=True) for j in range(hp)]

        def chunk(c, dq_accs, diag):
            st = pl.multiple_of(c * b, b)
            out = []
            for j in range(hp):
                p, dp = (p_sc[j, c], dp_sc[j, c]) if has_bias else probs(j, c, diag)
                ds = p * (dp - deltas[j])
                dk_ref[pl.ds(st, b), hq(j)] += _mm_tn(ds, qs[j]) * scale
                dv_ref[pl.ds(st, b), hd(j)] += _mm_tn(p, dos[j])
                if has_bias:
                    dck_ref[j, c] += -jnp.sum(ds, axis=0, keepdims=True)
                out.append(dq_accs[j] + _mm(ds, k_ref[pl.ds(st, b), hq(j)]))
            return tuple(out)

        accs = lax.fori_loop(0, i, lambda c, a: chunk(c, a, False), tuple(jnp.zeros((b, dq), f32) for _ in range(hp)))
        for j, acc in enumerate(chunk(i, accs, True)):
            dq_ref[:, hq(j)] = acc * scale

    rowq = pl.BlockSpec((b, hp * HEAD_DIM), lambda g, i: (i, g))
    in_specs = [pl.BlockSpec((b, hp * dq), lambda g, i: (i, qcol // hp + g)),
                pl.BlockSpec((s, hp * dq), lambda g, i: (0, kcol // hp + g)),
                pl.BlockSpec((s, hp * HEAD_DIM), lambda g, i: (0, vcol // hp + g)), rowq, rowq,
                pl.BlockSpec((hp, b, 1), lambda g, i: (g, i, 0))]
    args = [q, k, v, do, o, lse]
    out_shape = [_SDS((s, N_HEADS * dq), f32), _SDS((s, N_HEADS * dq), f32), _SDS((s, N_HEADS * HEAD_DIM), f32)]
    out_specs = [pl.BlockSpec((b, hp * dq), lambda g, i: (i, g)), pl.BlockSpec((s, hp * dq), lambda g, i: (0, g)),
                 pl.BlockSpec((s, hp * HEAD_DIM), lambda g, i: (0, g))]
    if has_bias:
        in_specs += [pl.BlockSpec((b, LANES), lambda g, i: (i, 0)),
                     pl.BlockSpec((hp, nq, 1, b), lambda g, i: (g, 0, 0, 0))]
        args += [cum, cum_t]
        out_shape.append(_SDS((N_HEADS, nq, 1, b), f32))
        out_specs.append(pl.BlockSpec((hp, nq, 1, b), lambda g, i: (g, 0, 0, 0)))
    return pl.pallas_call(
        body, out_shape=tuple(out_shape), grid=(N_HEADS // hp, nq), in_specs=in_specs, out_specs=tuple(out_specs),
        scratch_shapes=[pltpu.VMEM((hp, nq, b, b), f32)] * 2 if has_bias else [],
        compiler_params=_cp(("parallel", "arbitrary")), name=name,
    )(*args)


def _tri(b, kind):
    r, c = _iota((b, b), 0), _iota((b, b), 1)
    cond = {"row_gt": r > c, "row_lt": r < c, "row_ge": r >= c, "row_le": r <= c}[kind]
    return jnp.where(cond, 1.0, 0.0).astype(_MXU)


def _log_keep(z):
    return -(jnp.maximum(z, 0.0) + jnp.log1p(jnp.exp(-jnp.abs(z))))


def _sb_fwd(z_all, *, hp, name):
    s = z_all.shape[0]
    b = _attn_block(s)
    nq = s // b
    scale = HEAD_DIM ** -0.5
    qcol, kcol, vcol = (_AL[n] // (hp * HEAD_DIM) for n in ("sb_q", "sb_k", "sb_v"))
    hd = lambda j: slice(j * HEAD_DIM, (j + 1) * HEAD_DIM)

    def body(q_ref, k_ref, v_ref, o_ref):
        i = pl.program_id(1)
        qs = [q_ref[:, hd(j)].astype(_MXU) for j in range(hp)]
        upper = _tri(b, "row_gt")

        def chunk(c, carry, diag):
            st = pl.multiple_of(c * b, b)
            mask = _lower_mask(b, True) if diag else None
            out = []
            for j in range(hp):
                rsum, acc = carry[j]
                z = _mm_nt(qs[j], k_ref[pl.ds(st, b), hd(j)]) * scale
                lk = _log_keep(z)
                if diag:
                    lk = jnp.where(mask, lk, 0.0)
                a = z + lk + _mm_split(lk, upper) + rsum
                if diag:
                    a = jnp.where(mask, a, NEG_INF)
                acc = acc + _mm(jnp.exp(a), v_ref[pl.ds(st, b), hd(j)])
                out.append((rsum + jnp.sum(lk, axis=1, keepdims=True), acc))
            return tuple(out)

        init = tuple((jnp.zeros((b, 1), f32), jnp.zeros((b, HEAD_DIM), f32)) for _ in range(hp))
        carry = lax.fori_loop(0, i, lambda t, cr: chunk(i - 1 - t, cr, False), chunk(i, init, True))
        for j in range(hp):
            o_ref[:, hd(j)] = carry[j][1]

    w = hp * HEAD_DIM
    return pl.pallas_call(
        body, out_shape=_SDS((s, GROUP), f32), grid=(N_HEADS // hp, nq),
        in_specs=[pl.BlockSpec((b, w), lambda g, i: (i, qcol + g)), pl.BlockSpec((s, w), lambda g, i: (0, kcol + g)),
                  pl.BlockSpec((s, w), lambda g, i: (0, vcol + g))],
        out_specs=pl.BlockSpec((b, w), lambda g, i: (i, g)),
        compiler_params=_cp(("parallel", "parallel")), name=name,
    )(z_all, z_all, z_all)


def _sb_bwd(z_all, do, *, hp, name):
    s = z_all.shape[0]
    b = _attn_block(s)
    nq = s // b
    scale = HEAD_DIM ** -0.5
    qcol, kcol, vcol = (_AL[n] // (hp * HEAD_DIM) for n in ("sb_q", "sb_k", "sb_v"))
    hd = lambda j: slice(j * HEAD_DIM, (j + 1) * HEAD_DIM)

    def body(q_ref, k_ref, v_ref, do_ref, dq_ref, dk_ref, dv_ref, z_sc, lk_sc, r_sc):
        i = pl.program_id(1)

        @pl.when(i == 0)
        def _():
            dk_ref[...] = jnp.zeros_like(dk_ref)
            dv_ref[...] = jnp.zeros_like(dv_ref)

        qs = [q_ref[:, hd(j)].astype(_MXU) for j in range(hp)]
        dos = [do_ref[:, hd(j)].astype(_MXU) for j in range(hp)]
        upper = _tri(b, "row_gt")
        lower = _tri(b, "row_lt")

        def scores(c, rsums, diag):
            st = pl.multiple_of(c * b, b)
            out = []
            for j in range(hp):
                z = _mm_nt(qs[j], k_ref[pl.ds(st, b), hd(j)]) * scale
                lk = _log_keep(z)
                if diag:
                    lk = jnp.where(_lower_mask(b, True), lk, 0.0)
                z_sc[j, c] = z
                lk_sc[j, c] = lk
                r_sc[j, c] = _mm_split(lk, upper) + rsums[j]
                out.append(rsums[j] + jnp.sum(lk, axis=1, keepdims=True))
            return tuple(out)

        rsums = scores(i, tuple(jnp.zeros((b, 1), f32) for _ in range(hp)), True)
        lax.fori_loop(0, i, lambda t, r: scores(i - 1 - t, r, False), rsums)

        def grads(c, carry, diag):
            st = pl.multiple_of(c * b, b)
            mask = _lower_mask(b, True) if diag else None
            out = []
            for j in range(hp):
                psum, dq_acc = carry[j]
                z, lk = z_sc[j, c], lk_sc[j, c]
                lb = z + lk
                a = lb + r_sc[j, c]
                if diag:
                    a = jnp.where(mask, a, NEG_INF)
                w = jnp.exp(a)
                e = _mm_nt(dos[j], v_ref[pl.ds(st, b), hd(j)]) * w
                before = _mm_split(e, lower) + psum
                dz = e * jnp.exp(lk) - before * jnp.exp(lb)
                if diag:
                    dz = jnp.where(mask, dz, 0.0)
                dk_ref[pl.ds(st, b), hd(j)] += _mm_tn(dz, qs[j]) * scale
                dv_ref[pl.ds(st, b), hd(j)] += _mm_tn(w, dos[j])
                out.append((psum + jnp.sum(e, axis=1, keepdims=True), dq_acc + _mm(dz, k_ref[pl.ds(st, b), hd(j)])))
            return tuple(out)

        init = tuple((jnp.zeros((b, 1), f32), jnp.zeros((b, HEAD_DIM), f32)) for _ in range(hp))
        carry = grads(i, lax.fori_loop(0, i, lambda c, cr: grads(c, cr, False), init), True)
        for j in range(hp):
            dq_ref[:, hd(j)] = carry[j][1] * scale

    w = hp * HEAD_DIM
    blk = pl.BlockSpec((b, w), lambda g, i: (i, g))
    full = pl.BlockSpec((s, w), lambda g, i: (0, g))
    return pl.pallas_call(
        body, out_shape=tuple(_SDS((s, GROUP), f32) for _ in range(3)), grid=(N_HEADS // hp, nq),
        in_specs=[pl.BlockSpec((b, w), lambda g, i: (i, qcol + g)), pl.BlockSpec((s, w), lambda g, i: (0, kcol + g)),
                  pl.BlockSpec((s, w), lambda g, i: (0, vcol + g)), blk],
        out_specs=(blk, full, full),
        scratch_shapes=[pltpu.VMEM((hp, nq, b, b), f32)] * 3,
        compiler_params=_cp(("parallel", "arbitrary")), name=name,
    )(z_all, z_all, z_all, do)


def _split3_left(t, x):
    hi = x.astype(_MXU)
    r1 = x - hi.astype(f32)
    mid = r1.astype(_MXU)
    lo = (r1 - mid.astype(f32)).astype(_MXU)
    dot = functools.partial(jnp.dot, preferred_element_type=f32)
    return dot(t, hi) + dot(t, mid) + dot(t, lo)


def _split3_right(x, t):
    hi = x.astype(_MXU)
    r1 = x - hi.astype(f32)
    mid = r1.astype(_MXU)
    lo = (r1 - mid.astype(f32)).astype(_MXU)
    dot = functools.partial(jnp.dot, preferred_element_type=f32)
    return dot(hi, t) + dot(mid, t) + dot(lo, t)


def _fox_cum_fwd(z_all, bias, *, name):
    s = z_all.shape[0]
    b = _attn_block(s)
    fcol = _AL["fox_f"] // LANES

    def body(f_ref, b_ref, cum_ref, cumt_ref, carry_ref):
        i = pl.program_id(0)

        @pl.when(i == 0)
        def _():
            carry_ref[...] = jnp.zeros_like(carry_ref)

        u = f_ref[...] + b_ref[...]
        lf = jnp.minimum(u, 0.0) - jnp.log1p(jnp.exp(-jnp.abs(u)))
        cum = _split3_left(_tri(b, "row_ge"), lf) + carry_ref[...]
        cum_ref[...] = cum
        cumt_ref[...] = cum.T[0:8, :]
        carry_ref[...] = cum_ref[b - 1:b, :]

    return pl.pallas_call(
        body, out_shape=(_SDS((s, LANES), f32), _SDS((8, s), f32)), grid=(s // b,),
        in_specs=[pl.BlockSpec((b, LANES), lambda i: (i, fcol)), pl.BlockSpec((1, LANES), lambda i: (0, 0))],
        out_specs=(pl.BlockSpec((b, LANES), lambda i: (i, 0)), pl.BlockSpec((8, b), lambda i: (0, i))),
        scratch_shapes=[pltpu.VMEM((1, LANES), f32)], compiler_params=_cp(("arbitrary",)), name=name,
    )(z_all, bias)


def _fox_cum_bwd(z_all, bias, dcum_t, *, name):
    s = z_all.shape[0]
    b = _attn_block(s)
    nb = s // b
    fcol = _AL["fox_f"] // LANES

    def body(f_ref, b_ref, dc_ref, df_ref, db_ref, carry_ref):
        i = pl.program_id(0)

        @pl.when(i == 0)
        def _():
            carry_ref[...] = jnp.zeros_like(carry_ref)
            db_ref[...] = jnp.zeros_like(db_ref)

        dc = dc_ref[...]
        rev = _split3_right(dc, _tri(b, "row_ge")) + carry_ref[...]
        carry_ref[...] = carry_ref[...] + jnp.sum(dc, axis=1, keepdims=True)
        dlf = jnp.concatenate([rev, jnp.zeros((LANES - 8, b), f32)], axis=0).T
        u = f_ref[...] + b_ref[...]
        df = jnp.where(_iota((b, LANES), 1) < N_HEADS, dlf * (1.0 - _sigmoid(u)), 0.0)
        df_ref[...] = df
        db_ref[...] += jnp.sum(df, axis=0, keepdims=True)

    return pl.pallas_call(
        body, out_shape=(_SDS((s, LANES), f32), _SDS((1, LANES), f32)), grid=(nb,),
        in_specs=[pl.BlockSpec((b, LANES), lambda i: (nb - 1 - i, fcol)), pl.BlockSpec((1, LANES), lambda i: (0, 0)),
                  pl.BlockSpec((8, b), lambda i: (0, nb - 1 - i))],
        out_specs=(pl.BlockSpec((b, LANES), lambda i: (nb - 1 - i, 0)), pl.BlockSpec((1, LANES), lambda i: (0, 0))),
        scratch_shapes=[pltpu.VMEM((8, 1), f32)], compiler_params=_cp(("arbitrary",)), name=name,
    )(z_all, bias, dcum_t)


MLA_QW = 2 * LANES


def _rms_rows(x):
    r = lax.rsqrt(jnp.mean(x * x, axis=-1, keepdims=True) + RMS_EPS)
    return x * r, r


def _mla_prep_fwd(z_all, gq, gkv, wuq, wk, wv, tables, *, name):
    s = z_all.shape[0]
    rb = _row_block(s)
    half = MLA_ROPE // 2

    def body(cq_ref, ckv_ref, kr_ref, gq_ref, gkv_ref, wuq_ref, wk_ref, wv_ref, cos_ref, sa_ref, sb_ref,
             q_ref, k_ref, v_ref):
        cos, sa, sb = cos_ref[...], sa_ref[...], sb_ref[...]
        xh, _ = _rms_rows(cq_ref[...])
        qp = _mm(xh * gq_ref[...], wuq_ref[...])
        kh, _ = _rms_rows(ckv_ref[...])
        nkv = kh * gkv_ref[...]
        kn = _mm(nkv, wk_ref[...])
        v_ref[...] = _mm(nkv, wv_ref[...])
        kr = _rope(kr_ref[...], cos, sa, sb, half)
        for h in range(N_HEADS):
            lo, mid, hi = h * MLA_QW, h * MLA_QW + LANES, (h + 1) * MLA_QW
            q_ref[:, lo:mid] = qp[:, lo:mid]
            q_ref[:, mid:hi] = _rope(qp[:, mid:hi], cos, sa, sb, half)
            k_ref[:, lo:mid] = kn[:, h * LANES:(h + 1) * LANES]
            k_ref[:, mid:hi] = kr

    row = lambda w, cb: pl.BlockSpec((rb, w), lambda i: (i, cb))
    whole = lambda a: pl.BlockSpec(a.shape, lambda i: (0,) * a.ndim)
    return pl.pallas_call(
        body, out_shape=(_SDS((s, N_HEADS * MLA_QW), f32), _SDS((s, N_HEADS * MLA_QW), f32), _SDS((s, GROUP), f32)),
        grid=(s // rb,),
        in_specs=[row(MLA_Q_RANK, _AL["mla_cq"] // MLA_Q_RANK), row(LANES, _AL["mla_ckv"] // LANES),
                  row(LANES, _AL["mla_k_rope"] // LANES), whole(gq), whole(gkv), whole(wuq), whole(wk), whole(wv),
                  row(LANES, 0), row(LANES, 0), row(LANES, 0)],
        out_specs=(row(N_HEADS * MLA_QW, 0), row(N_HEADS * MLA_QW, 0), row(GROUP, 0)),
        compiler_params=_cp(("parallel",)), name=name,
    )(z_all, z_all, z_all, gq, gkv, wuq, wk, wv, *tables)


def _mla_prep_bwd(z_all, gq, gkv, wuq, wk, wv, tables, dq_cat, dk_cat, dv, *, name):
    s = z_all.shape[0]
    rb = _row_block(s)
    half = MLA_ROPE // 2

    def body(cq_ref, ckv_ref, gq_ref, gkv_ref, wuq_ref, wk_ref, wv_ref, cos_ref, sa_ref, sb_ref, dq_ref, dk_ref,
             dv_ref, dcq_ref, dckv_ref, dkr_ref, dwuq_ref, dwk_ref, dwv_ref, dgq_ref, dgkv_ref):
        i = pl.program_id(0)

        @pl.when(i == 0)
        def _():
            for r in (dwuq_ref, dwk_ref, dwv_ref, dgq_ref, dgkv_ref):
                r[...] = jnp.zeros_like(r)

        cos, sa, sb = cos_ref[...], sa_ref[...], sb_ref[...]
        parts, knp = [], []
        dkr = jnp.zeros((rb, LANES), f32)
        for h in range(N_HEADS):
            lo, mid, hi = h * MLA_QW, h * MLA_QW + LANES, (h + 1) * MLA_QW
            parts += [dq_ref[:, lo:mid], _rope(dq_ref[:, mid:hi], cos, sa, sb, half, transpose=True)]
            knp.append(dk_ref[:, lo:mid])
            dkr = dkr + _rope(dk_ref[:, mid:hi], cos, sa, sb, half, transpose=True)
        dkr_ref[...] = dkr
        dqp = jnp.concatenate(parts, axis=1)
        dkn = jnp.concatenate(knp, axis=1)
        dvv = dv_ref[...]

        def norm_bwd(x_ref, g_ref, w_pairs, dx_ref, dg_ref):
            xh, r = _rms_rows(x_ref[...])
            nx = xh * g_ref[...]
            dn = jnp.zeros_like(xh)
            for w_ref, dw_ref, dy in w_pairs:
                dw_ref[...] += _mm_tn(nx, dy)
                dn = dn + _mm_nt(dy, w_ref[...])
            dxh = dn * g_ref[...]
            dx_ref[...] = r * (dxh - xh * jnp.mean(dxh * xh, axis=-1, keepdims=True))
            dg_ref[...] += jnp.sum(dn * xh, axis=0, keepdims=True)

        norm_bwd(cq_ref, gq_ref, [(wuq_ref, dwuq_ref, dqp)], dcq_ref, dgq_ref)
        norm_bwd(ckv_ref, gkv_ref, [(wk_ref, dwk_ref, dkn), (wv_ref, dwv_ref, dvv)], dckv_ref, dgkv_ref)

    row = lambda w, cb: pl.BlockSpec((rb, w), lambda i: (i, cb))
    whole = lambda a: pl.BlockSpec(a.shape, lambda i: (0,) * a.ndim)
    return pl.pallas_call(
        body,
        out_shape=(_SDS((s, MLA_Q_RANK), f32), _SDS((s, LANES), f32), _SDS((s, LANES), f32), _SDS(wuq.shape, f32),
                   _SDS(wk.shape, f32), _SDS(wv.shape, f32), _SDS(gq.shape, f32), _SDS(gkv.shape, f32)),
        grid=(s // rb,),
        in_specs=[row(MLA_Q_RANK, _AL["mla_cq"] // MLA_Q_RANK), row(LANES, _AL["mla_ckv"] // LANES), whole(gq),
                  whole(gkv), whole(wuq), whole(wk), whole(wv), row(LANES, 0), row(LANES, 0), row(LANES, 0),
                  row(N_HEADS * MLA_QW, 0), row(N_HEADS * MLA_QW, 0), row(GROUP, 0)],
        out_specs=(row(MLA_Q_RANK, 0), row(LANES, 0), row(LANES, 0), whole(wuq), whole(wk), whole(wv), whole(gq),
                   whole(gkv)),
        compiler_params=_cp(("arbitrary",)), name=name,
    )(z_all, z_all, gq, gkv, wuq, wk, wv, *tables, dq_cat, dk_cat, dv)


def _silu_grad(x):
    sg = _sigmoid(x)
    return sg * (1.0 + x * (1.0 - sg))


def _nsa_cmp_fwd(ra, rb_, pos, w1, w2, tables, *, name):
    nr = ra.shape[1]
    hw = ra.shape[2]

    def body(ra_ref, rb_ref, pos_ref, w1_ref, w2_ref, cos_ref, sa_ref, sb_ref, out_ref, hp_ref):
        for k in range(2):
            xa = ra_ref[k] + pos_ref[k, :, 0:hw]
            xb = rb_ref[k] + pos_ref[k, :, hw:2 * hw]
            hp = _mm(xa, w1_ref[k, 0:hw, :]) + _mm(xb, w1_ref[k, hw:2 * hw, :])
            hp_ref[k] = hp
            out = _mm(hp * _sigmoid(hp), w2_ref[k])
            if k == 0:
                out = _rope(out, cos_ref[...], sa_ref[...], sb_ref[...], HEAD_DIM // 2)
            out_ref[k] = out

    return pl.pallas_call(body, out_shape=(_SDS((2, nr, HEAD_DIM), f32), _SDS((2, nr, HEAD_DIM), f32)),
                          compiler_params=_cp(), name=name)(ra, rb_, pos, w1, w2, *tables)


def _nsa_cmp_bwd(ra, rb_, pos, w1, w2, tables, hp, dout, *, name):
    nr = ra.shape[1]
    hw = ra.shape[2]

    def body(ra_ref, rb_ref, pos_ref, w1_ref, w2_ref, cos_ref, sa_ref, sb_ref, hp_ref, do_ref,
             dxa_ref, dxb_ref, dw1_ref, dw2_ref):
        for k in range(2):
            d_out = do_ref[k]
            if k == 0:
                d_out = _rope(d_out, cos_ref[...], sa_ref[...], sb_ref[...], HEAD_DIM // 2, transpose=True)
            hpv = hp_ref[k]
            dw2_ref[k] = _mm_tn(hpv * _sigmoid(hpv), d_out)
            dhp = _mm_nt(d_out, w2_ref[k]) * _silu_grad(hpv)
            xa = ra_ref[k] + pos_ref[k, :, 0:hw]
            xb = rb_ref[k] + pos_ref[k, :, hw:2 * hw]
            dw1_ref[k, 0:hw, :] = _mm_tn(xa, dhp)
            dw1_ref[k, hw:2 * hw, :] = _mm_tn(xb, dhp)
            dxa_ref[k] = _mm_nt(dhp, w1_ref[k, 0:hw, :])
            dxb_ref[k] = _mm_nt(dhp, w1_ref[k, hw:2 * hw, :])

    return pl.pallas_call(
        body, out_shape=(_SDS((2, nr, hw), f32), _SDS((2, nr, hw), f32), _SDS(w1.shape, f32), _SDS(w2.shape, f32)),
        compiler_params=_cp(), name=name)(ra, rb_, pos, w1, w2, *tables, hp, dout)


def _nsa_consts(s):
    b = _attn_block(s)
    nr = s // CMP_STRIDE
    n_cmp = (s - CMP_LEN) // CMP_STRIDE + 1
    n_sel = s // SEL_LEN
    cmp_start = np.arange(n_cmp) * CMP_STRIDE
    sel_start = np.arange(n_sel) * SEL_LEN
    overlap = np.clip(np.minimum(cmp_start[:, None] + CMP_LEN, sel_start[None, :] + SEL_LEN)
                      - np.maximum(cmp_start[:, None], sel_start[None, :]), 0, None)
    m2s = np.zeros((nr, LANES), np.float32)
    m2s[:n_cmp, :n_sel] = overlap / CMP_LEN
    e3 = np.zeros((s // b, LANES, b), np.float32)
    tok = np.arange(s)
    e3[tok // b, tok // SEL_LEN, tok % b] = 1.0
    return jnp.asarray(m2s, _MXU), jnp.asarray(e3, _MXU)


def _nsa_masks(i, b, d):
    qpos = i * b + _iota((b, b), 0)
    kpos = (i - d) * b + _iota((b, b), 1)
    return (kpos <= qpos) & (kpos > qpos - WINDOW)


def _nsa_fwd(qr, kvc, ksr, vs, kwr, vw, z_all, m2s, e3, *, name):
    s = qr.shape[0]
    b = _attn_block(s)
    nq = s // b
    nr = kvc.shape[1]
    n_sel = s // SEL_LEN
    top_n = min(SEL_TOPN, n_sel)
    nd = -(-WINDOW // b)
    scale = HEAD_DIM ** -0.5
    bcol = _AL["nsa_branch"] // LANES
    H = N_HEADS

    def body(q_ref, kvc_ref, ks_ref, vs_ref, kw_ref, vw_ref, br_ref, m2s_ref, e3_ref,
             o_ref, oc_ref, os_ref, ow_ref, st_ref, sel_ref, m_sc, l_sc, acc_sc):
        i = pl.program_id(0)
        lane = _iota((b, LANES), 1)
        hs = lambda h: slice(h * HEAD_DIM, (h + 1) * HEAD_DIM)

        cmp_mask = (CMP_STRIDE * _iota((b, nr), 1) + (CMP_LEN - 1)) <= (i * b + _iota((b, nr), 0))
        imp = jnp.zeros((b, LANES), f32)
        stats = jnp.zeros((b, LANES), f32)
        for h in range(H):
            zc = jnp.where(cmp_mask, _mm_nt(q_ref[:, hs(h)], kvc_ref[0]) * scale, NEG_INF)
            m = jnp.max(zc, axis=1, keepdims=True)
            p = jnp.where(cmp_mask, jnp.exp(zc - m), 0.0)
            l = jnp.sum(p, axis=1, keepdims=True)
            some = l > 0.0
            lsafe = jnp.where(some, l, 1.0)
            pc = p * jnp.where(some, 1.0 / lsafe, 0.0)
            oc_ref[:, hs(h)] = _mm(pc, kvc_ref[1])
            imp = imp + _mm(pc, m2s_ref[...])
            stats = jnp.where(lane == h, jnp.where(some, m + jnp.log(lsafe), 0.0), stats)

        cur = jnp.right_shift(i * b + _iota((b, LANES), 0), int(math.log2(SEL_LEN)))
        forced = (lane == 0) | (lane == cur) | (lane == cur - 1)
        score = jnp.where(lane <= cur, jnp.where(forced, FORCED_BONUS, imp), NEG_INF)
        score = jnp.where(lane < n_sel, score, -3e38)
        rank = jnp.zeros((b, LANES), f32)
        for j in range(n_sel):
            col = score[:, j:j + 1]
            rank = rank + jnp.where(col > score, 1.0, jnp.where(col == score, jnp.where(lane > j, 1.0, 0.0), 0.0))
        sel = jnp.where(lane < n_sel, jnp.where(rank < top_n, 1.0, 0.0), 0.0)
        sel_ref[...] = sel
        sel_b = sel.astype(_MXU)

        def reset():
            m_sc[...] = jnp.full(m_sc.shape, NEG_INF, f32)
            l_sc[...] = jnp.zeros_like(l_sc)
            acc_sc[...] = jnp.zeros_like(acc_sc)

        def update(h, z, mask, vch):
            zm = jnp.where(mask, z, NEG_INF)
            m_old = m_sc[h]
            m_new = jnp.maximum(m_old, jnp.max(zm, axis=1, keepdims=True))
            p = jnp.where(mask, jnp.exp(zm - m_new), 0.0)
            alpha = jnp.exp(m_old - m_new)
            l_sc[h] = alpha * l_sc[h] + jnp.sum(p, axis=1, keepdims=True)
            acc_sc[h] = alpha * acc_sc[h] + _mm(p, vch)
            m_sc[h] = m_new

        def finish(out_ref, branch, stats):
            for h in range(H):
                out_ref[:, hs(h)] = acc_sc[h] / l_sc[h]
                stats = jnp.where(lane == 4 * branch + h, m_sc[h] + jnp.log(l_sc[h]), stats)
            return stats

        def sel_chunk(c, diag):
            st = pl.multiple_of(c * b, b)
            mask = _mm(sel_b, e3_ref[c]) > 0.5
            if diag:
                mask = mask & _lower_mask(b, False)
            kch, vch = ks_ref[pl.ds(st, b), :], vs_ref[pl.ds(st, b), :]
            for h in range(H):
                update(h, _mm_nt(q_ref[:, hs(h)], kch) * scale, mask, vch)

        reset()

        def sel_loop(c, carry):
            sel_chunk(c, False)
            return carry

        lax.fori_loop(0, i, sel_loop, 0)
        sel_chunk(i, True)
        stats = finish(os_ref, 1, stats)

        reset()
        for d in range(nd, -1, -1):
            @pl.when(i >= d)
            def _():
                st = pl.multiple_of((i - d) * b, b)
                mask = _nsa_masks(i, b, d)
                kch, vch = kw_ref[pl.ds(st, b), :], vw_ref[pl.ds(st, b), :]
                for h in range(H):
                    update(h, _mm_nt(q_ref[:, hs(h)], kch) * scale, mask, vch)
        stats = finish(ow_ref, 2, stats)
        st_ref[...] = stats

        g = _sigmoid(br_ref[...])
        for h in range(H):
            o_ref[:, hs(h)] = (g[:, 3 * h:3 * h + 1] * oc_ref[:, hs(h)] + g[:, 3 * h + 1:3 * h + 2] * os_ref[:, hs(h)]
                               + g[:, 3 * h + 2:3 * h + 3] * ow_ref[:, hs(h)])

    blk = lambda w: pl.BlockSpec((b, w), lambda i: (i, 0))
    whole = lambda a: pl.BlockSpec(a.shape, lambda i: (0,) * a.ndim)
    return pl.pallas_call(
        body, out_shape=tuple(_SDS((s, GROUP), f32) for _ in range(4)) + (_SDS((s, LANES), f32), _SDS((s, LANES), f32)),
        grid=(nq,),
        in_specs=[blk(GROUP), whole(kvc), whole(ksr), whole(vs), whole(kwr), whole(vw),
                  pl.BlockSpec((b, LANES), lambda i: (i, bcol)), whole(m2s), whole(e3)],
        out_specs=(blk(GROUP),) * 4 + (blk(LANES), blk(LANES)),
        scratch_shapes=[pltpu.VMEM((H, b, 1), f32), pltpu.VMEM((H, b, 1), f32), pltpu.VMEM((H, b, HEAD_DIM), f32)],
        compiler_params=_cp(("parallel",)), name=name,
    )(qr, kvc, ksr, vs, kwr, vw, z_all, m2s, e3)


def _nsa_bwd(do, qr, kvc, ksr, vs, kwr, vw, z_all, oc, os_, ow, stats, sel, e3, *, name):
    s = qr.shape[0]
    b = _attn_block(s)
    nq = s // b
    nr = kvc.shape[1]
    nd = -(-WINDOW // b)
    scale = HEAD_DIM ** -0.5
    bcol = _AL["nsa_branch"] // LANES
    H = N_HEADS

    def body(do_ref, q_ref, kvc_ref, ks_ref, vs_ref, kw_ref, vw_ref, br_ref, oc_ref, os_ref, ow_ref, st_ref, sel_ref,
             e3_ref, dq_ref, dbr_ref, dkvc_ref, dks_ref, dvs_ref, dkw_ref, dvw_ref, dob_sc, delta_sc, dq_sc):
        i = pl.program_id(0)

        @pl.when(i == 0)
        def _():
            for r in (dkvc_ref, dks_ref, dvs_ref, dkw_ref, dvw_ref):
                r[...] = jnp.zeros_like(r)

        lane = _iota((b, LANES), 1)
        hs = lambda h: slice(h * HEAD_DIM, (h + 1) * HEAD_DIM)
        g = _sigmoid(br_ref[...])
        stats = st_ref[...]
        dbr = jnp.zeros((b, LANES), f32)
        outs = (oc_ref, os_ref, ow_ref)
        for h in range(H):
            doh = do_ref[:, hs(h)]
            for j in range(3):
                gj = g[:, 3 * h + j:3 * h + j + 1]
                dgj = jnp.sum(doh * outs[j][:, hs(h)], axis=1, keepdims=True)
                dbr = jnp.where(lane == 3 * h + j, dgj * gj * (1.0 - gj), dbr)
                dob_sc[j, :, hs(h)] = gj * doh
                delta_sc[j, h] = gj * dgj
        dbr_ref[...] = dbr
        dq_sc[...] = jnp.zeros_like(dq_sc)

        def branch(j, h, z, mask, kch, vch):
            qh = q_ref[:, hs(h)]
            p = jnp.where(mask, jnp.exp(jnp.where(mask, z, NEG_INF) - stats[:, 4 * j + h:4 * j + h + 1]), 0.0)
            dob = dob_sc[j, :, hs(h)]
            ds = p * (_mm_nt(dob, vch) - delta_sc[j, h])
            dq_sc[:, hs(h)] += _mm(ds, kch) * scale
            return _mm_tn(ds, qh) * scale, _mm_tn(p, dob)

        cmp_mask = (CMP_STRIDE * _iota((b, nr), 1) + (CMP_LEN - 1)) <= (i * b + _iota((b, nr), 0))
        kc, vc = kvc_ref[0], kvc_ref[1]
        for h in range(H):
            dk, dv = branch(0, h, _mm_nt(q_ref[:, hs(h)], kc) * scale, cmp_mask, kc, vc)
            dkvc_ref[0] += dk
            dkvc_ref[1] += dv

        sel_b = sel_ref[...].astype(_MXU)

        def chunk(j, c, mask, k_ref, v_ref, dk_ref, dv_ref):
            st = pl.multiple_of(c * b, b)
            kch, vch = k_ref[pl.ds(st, b), :], v_ref[pl.ds(st, b), :]
            dk = jnp.zeros((b, HEAD_DIM), f32)
            dv = jnp.zeros((b, HEAD_DIM), f32)
            for h in range(H):
                dkh, dvh = branch(j, h, _mm_nt(q_ref[:, hs(h)], kch) * scale, mask, kch, vch)
                dk, dv = dk + dkh, dv + dvh
            dk_ref[pl.ds(st, b), :] += dk
            dv_ref[pl.ds(st, b), :] += dv

        def sel_chunk(c, diag):
            mask = _mm(sel_b, e3_ref[c]) > 0.5
            if diag:
                mask = mask & _lower_mask(b, False)
            chunk(1, c, mask, ks_ref, vs_ref, dks_ref, dvs_ref)

        def sel_loop(c, carry):
            sel_chunk(c, False)
            return carry

        lax.fori_loop(0, i, sel_loop, 0)
        sel_chunk(i, True)

        for d in range(nd, -1, -1):
            @pl.when(i >= d)
            def _():
                chunk(2, i - d, _nsa_masks(i, b, d), kw_ref, vw_ref, dkw_ref, dvw_ref)

        dq_ref[...] = dq_sc[...]

    blk = lambda w: pl.BlockSpec((b, w), lambda i: (i, 0))
    whole = lambda a: pl.BlockSpec(a.shape, lambda i: (0,) * a.ndim)
    stream = _SDS((s, HEAD_DIM), f32)
    return pl.pallas_call(
        body, out_shape=(_SDS((s, GROUP), f32), _SDS((s, LANES), f32), _SDS(kvc.shape, f32), stream, stream, stream,
                         stream),
        grid=(nq,),
        in_specs=[blk(GROUP), blk(GROUP), whole(kvc), whole(ksr), whole(vs), whole(kwr), whole(vw),
                  pl.BlockSpec((b, LANES), lambda i: (i, bcol)), blk(GROUP), blk(GROUP), blk(GROUP), blk(LANES),
                  blk(LANES), whole(e3)],
        out_specs=(blk(GROUP), blk(LANES), whole(kvc), whole(ksr), whole(vs), whole(kwr), whole(vw)),
        scratch_shapes=[pltpu.VMEM((3, b, GROUP), f32), pltpu.VMEM((3, H, b, 1), f32), pltpu.VMEM((b, GROUP), f32)],
        compiler_params=_cp(("arbitrary",)), name=name,
    )(do, qr, kvc, ksr, vs, kwr, vw, z_all, oc, os_, ow, stats, sel, e3)


def _seg(a, name):
    parts = [lax.slice_in_dim(a, off, off + hi - lo, axis=a.ndim - 1) for off, lo, hi in _PIECES[name]]
    return parts[0] if len(parts) == 1 else jnp.concatenate(parts, axis=a.ndim - 1)


def _to_groups(segs, rows, dtype):
    cols = []
    for s, grp in enumerate(_GROUPS):
        at = 0
        for n, lo, hi, off in sorted(grp, key=lambda t: t[3]):
            if off > at:
                cols.append(jnp.zeros((rows, off - at), dtype))
            cols.append(segs[n][:, lo:hi].astype(dtype))
            at = off + hi - lo
        if at < GROUP_W:
            cols.append(jnp.zeros((rows, GROUP_W - at), dtype))
    return jnp.concatenate(cols, axis=1)


def _piece_from_shard(w_t, s):
    grp = sorted(_GROUPS[s], key=lambda t: t[3])
    ends = [t[3] for t in grp[1:]] + [GROUP_W]
    rows = []
    for (n, lo, hi, off), end in zip(grp, ends):
        first = _ORIG[n] + lo - s * CHIP_COLS
        rows.append(jnp.pad(w_t[:, first:first + hi - lo], ((0, 0), (0, end - off - (hi - lo)), (0, 0))))
    return jnp.concatenate(rows, axis=1)


def _shard_from_piece(g, s):
    return jnp.concatenate([g[:, off:off + hi - lo] for n, lo, hi, off in
                            sorted(_GROUPS[s], key=lambda t: _ORIG[t[0]] + t[1])], axis=1)


def _from_groups(a):
    return jnp.concatenate([_seg(a, n) for n, _ in _SEGS], axis=1)


def _cmp_rows(tok):
    s = tok.shape[0]
    r = tok.reshape(s // CMP_STRIDE, CMP_STRIDE * HEAD_DIM)
    return r, jnp.concatenate([r[1:], jnp.zeros((1, r.shape[1]), r.dtype)], axis=0)


def _cmp_unrows(dxa, dxb):
    s = dxa.shape[0] * CMP_STRIDE
    return (dxa + jnp.concatenate([jnp.zeros((1, dxa.shape[1]), dxa.dtype), dxb[:-1]], axis=0)).reshape(s, HEAD_DIM)


_GATES = ("sb_gate", "nsa_gate", "fox_gate", "mla_gate")


def _layer_fwd(x, p, c, tag):
    s = x.shape[0]
    b = _attn_block(s)
    h = _rms_fwd(x, p["pre_g"], out_dtype=_MXU, name=f"prenorm_{tag}")
    z = _matmul(h, p["w_in"], "nt", bias=p["b_in"], name=f"inproj_{tag}")
    o_sb = _sb_fwd(z, hp=HP_FWD, name=f"sb_fwd_{tag}")

    qr, ksr, kwr = _rope_call([(z, GROUP, _AL["nsa_q"] // GROUP), (z, LANES, _AL["nsa_k_sel"] // LANES),
                               (z, LANES, _AL["nsa_k_win"] // LANES)], c["tabs128"], HEAD_DIM // 2, False,
                              name=f"nsa_rope_{tag}")
    (rak, rbk), (rav, rbv) = _cmp_rows(_seg(z, "nsa_k_cmp")), _cmp_rows(_seg(z, "nsa_v_cmp"))
    ra, rb_ = jnp.stack([rak, rav]), jnp.stack([rbk, rbv])
    kvc, hp = _nsa_cmp_fwd(ra, rb_, p["cmp_pos"], p["cmp_w1"], p["cmp_w2"], c["tabs_cmp"], name=f"nsa_cmp_{tag}")
    vs, vw = _seg(z, "nsa_v_sel"), _seg(z, "nsa_v_win")
    o_nsa, oc, os_, ow, stats, sel = _nsa_fwd(qr, kvc, ksr, vs, kwr, vw, z, c["m2s"], c["e3"], name=f"nsa_fwd_{tag}")

    cum, cum_t8 = _fox_cum_fwd(z, p["fox_bias"], name=f"fox_cum_{tag}")
    cum_t = cum_t8.reshape(8, s // b, 1, b)
    fox_v = _seg(z, "fox_v")
    fcols = (_AL["fox_q"] // HEAD_DIM, _AL["fox_k"] // HEAD_DIM, 0)
    o_fox, lse_fox = _attn_fwd(z, z, fox_v, *fcols, HEAD_DIM, cum, cum_t, scale=HEAD_DIM ** -0.5, hp=HP_FWD,
                               name=f"fox_fwd_{tag}")

    qcat, kcat, vm = _mla_prep_fwd(z, p["gq"], p["gkv"], p["wuq"], p["wk"], p["wv"], c["tabs64"],
                                   name=f"mla_prep_{tag}")
    o_mla, lse_mla = _attn_fwd(qcat, kcat, vm, 0, 0, 0, MLA_QW, None, None, scale=(MLA_NOPE + MLA_ROPE) ** -0.5,
                               hp=HP_BWD, name=f"mla_fwd_{tag}")

    o_all = (o_sb, o_nsa, o_fox, o_mla)
    gates = jnp.concatenate([_seg(z, n) for n in _GATES], axis=1)
    mix = _gate_fwd(o_all, gates, name=f"gate_{tag}")
    u = _matmul(mix, p["w_out"], "nn", name=f"outproj_{tag}")
    y = _postnorm_fwd(u, p["post_g"], x, name=f"postnorm_{tag}")
    saved = dict(x=x, h=h, z=z, qr=qr, ksr=ksr, kwr=kwr, ra=ra, rb=rb_, kvc=kvc, hp=hp, vs=vs, vw=vw, oc=oc, os=os_,
                 ow=ow, stats=stats, sel=sel, cum=cum, cum_t=cum_t, fox_v=fox_v, o_fox=o_fox, lse_fox=lse_fox, qcat=qcat, kcat=kcat,
                 vm=vm, o_mla=o_mla, lse_mla=lse_mla, o_all=o_all, gates=gates, mix=mix, u=u)
    return y, saved


def _layer_bwd(dy, sv, p, c, tag, dw_dtype=f32):
    z = sv["z"]
    s = z.shape[0]
    du, dg_post = _rms_bwd(dy, sv["u"], p["post_g"], name=f"postnorm_bwd_{tag}")
    dmix = _matmul(du, p["w_out"], "nt", name=f"outproj_dx_{tag}")
    dw_out = _matmul(sv["mix"], du, "tn", name=f"outproj_dw_{tag}")
    do_sb, do_nsa, do_fox, do_mla, dgates = _gate_bwd(dmix, sv["o_all"], sv["gates"], name=f"gate_bwd_{tag}")
    dgate = [dgates[:, k * GROUP:(k + 1) * GROUP] for k in range(4)]

    sb_dq, sb_dk, sb_dv = _sb_bwd(z, do_sb, hp=HP_BWD, name=f"sb_bwd_{tag}")

    n_dq, n_dbr, n_dkvc, n_dks, n_dvs, n_dkw, n_dvw = _nsa_bwd(
        do_nsa, sv["qr"], sv["kvc"], sv["ksr"], sv["vs"], sv["kwr"], sv["vw"], z, sv["oc"], sv["os"], sv["ow"],
        sv["stats"], sv["sel"], c["e3"], name=f"nsa_bwd_{tag}")
    dxa, dxb, dw1, dw2 = _nsa_cmp_bwd(sv["ra"], sv["rb"], p["cmp_pos"], p["cmp_w1"], p["cmp_w2"], c["tabs_cmp"],
                                      sv["hp"], n_dkvc, name=f"nsa_cmp_bwd_{tag}")
    n_dq, n_dks, n_dkw = _rope_call([(n_dq, GROUP, 0), (n_dks, LANES, 0), (n_dkw, LANES, 0)], c["tabs128"],
                                    HEAD_DIM // 2, True, name=f"nsa_rope_bwd_{tag}")
    dpos = _colsum(jnp.concatenate([dxa[0], dxb[0], dxa[1], dxb[1]], axis=1), name=f"nsa_dpos_{tag}")
    flat = CMP_LEN * HEAD_DIM

    fcols = (_AL["fox_q"] // HEAD_DIM, _AL["fox_k"] // HEAD_DIM, 0)
    f_dq, f_dk, f_dv, f_dck = _attn_bwd(z, z, sv["fox_v"], *fcols, HEAD_DIM, do_fox, sv["o_fox"], sv["lse_fox"],
                                        sv["cum"], sv["cum_t"], scale=HEAD_DIM ** -0.5, hp=HP_BWD,
                                        name=f"fox_bwd_{tag}")
    dcum_t = jnp.pad(f_dck.reshape(N_HEADS, s), ((0, 8 - N_HEADS), (0, 0)))
    f_df, f_dbias = _fox_cum_bwd(z, p["fox_bias"], dcum_t, name=f"fox_cum_bwd_{tag}")

    m_dq, m_dk, m_dv = _attn_bwd(sv["qcat"], sv["kcat"], sv["vm"], 0, 0, 0, MLA_QW, do_mla, sv["o_mla"], sv["lse_mla"],
                                 None, None, scale=(MLA_NOPE + MLA_ROPE) ** -0.5, hp=HP_BWD, name=f"mla_bwd_{tag}")
    m_dcq, m_dckv, m_dkr, m_dwuq, m_dwk, m_dwv, m_dgq, m_dgkv = _mla_prep_bwd(
        z, p["gq"], p["gkv"], p["wuq"], p["wk"], p["wv"], c["tabs64"], m_dq, m_dk, m_dv, name=f"mla_prep_bwd_{tag}")

    dz = _to_groups(dict(
        sb_q=sb_dq, sb_k=sb_dk, sb_v=sb_dv, sb_gate=dgate[0], nsa_q=n_dq, nsa_k_cmp=_cmp_unrows(dxa[0], dxb[0]),
        nsa_v_cmp=_cmp_unrows(dxa[1], dxb[1]), nsa_k_sel=n_dks, nsa_v_sel=n_dvs, nsa_k_win=n_dkw, nsa_v_win=n_dvw,
        nsa_branch=n_dbr, nsa_gate=dgate[1], fox_q=f_dq, fox_k=f_dk, fox_v=f_dv, fox_f=f_df, fox_gate=dgate[2],
        mla_cq=m_dcq, mla_ckv=m_dckv, mla_k_rope=m_dkr, mla_gate=dgate[3]), s, _MXU)
    dh = _matmul(dz, p["w_in"], "nn", name=f"inproj_dx_{tag}")
    dw_in = _matmul(dz, sv["h"], "tn", out_dtype=dw_dtype, name=f"inproj_dw_{tag}")
    db = _colsum(dz, name=f"inproj_db_{tag}")
    dx, dg_pre = _rms_bwd(dh, sv["x"], p["pre_g"], res=dy, name=f"prenorm_bwd_{tag}")

    qw = MLA_NOPE + MLA_ROPE
    grads = {
        "pre_norm_g": dg_pre[0], "post_norm_g": dg_post[0], "w_in": dw_in, "b_in": _from_groups(db)[0],
        "w_out": dw_out, "fox_forget_bias": f_dbias[0, :N_HEADS],
        "nsa_cmp_pos_k": dpos[0, :flat].reshape(CMP_LEN, HEAD_DIM), "nsa_cmp_w1_k": dw1[0], "nsa_cmp_w2_k": dw2[0],
        "nsa_cmp_pos_v": dpos[0, flat:].reshape(CMP_LEN, HEAD_DIM), "nsa_cmp_w1_v": dw1[1], "nsa_cmp_w2_v": dw2[1],
        "mla_q_norm_g": m_dgq[0],
        "mla_w_uq": jnp.concatenate([m_dwuq[:, MLA_QW * h:MLA_QW * h + qw] for h in range(N_HEADS)], axis=1),
        "mla_kv_norm_g": m_dgkv[0],
        "mla_w_ukv": jnp.concatenate(sum([[m_dwk[:, LANES * h:LANES * (h + 1)], m_dwv[:, LANES * h:LANES * (h + 1)]]
                                          for h in range(N_HEADS)], []), axis=1),
    }
    return dx, grads


def _layer_params(w, l):
    b_in = w["b_in"][l].reshape(1, -1)
    b_segs = {n: b_in[:, _ORIG[n]:_ORIG[n] + wd] for n, wd in _SEGS}
    qw = MLA_NOPE + MLA_ROPE
    w_uq, w_ukv = w["mla_w_uq"][l], w["mla_w_ukv"][l]
    uq = []
    for h in range(N_HEADS):
        uq += [w_uq[:, qw * h:qw * (h + 1)], jnp.zeros((w_uq.shape[0], MLA_QW - qw), w_uq.dtype)]
    kw_ = 2 * LANES
    flat = CMP_LEN * HEAD_DIM
    return dict(
        pre_g=w["pre_norm_g"][l].reshape(1, -1), post_g=w["post_norm_g"][l].reshape(1, -1),
        w_in=w["w_in"][l], b_in=_to_groups(b_segs, 1, f32), w_out=w["w_out"][l],
        fox_bias=jnp.pad(w["fox_forget_bias"][l], (0, LANES - N_HEADS)).reshape(1, LANES),
        cmp_pos=jnp.stack([w["nsa_cmp_pos_k"][l].reshape(1, flat), w["nsa_cmp_pos_v"][l].reshape(1, flat)]),
        cmp_w1=jnp.stack([w["nsa_cmp_w1_k"][l], w["nsa_cmp_w1_v"][l]]),
        cmp_w2=jnp.stack([w["nsa_cmp_w2_k"][l], w["nsa_cmp_w2_v"][l]]),
        gq=w["mla_q_norm_g"][l].reshape(1, -1), gkv=w["mla_kv_norm_g"][l].reshape(1, -1),
        wuq=jnp.concatenate(uq, axis=1),
        wk=jnp.concatenate([w_ukv[:, kw_ * h:kw_ * h + LANES] for h in range(N_HEADS)], axis=1),
        wv=jnp.concatenate([w_ukv[:, kw_ * h + LANES:kw_ * (h + 1)] for h in range(N_HEADS)], axis=1),
    )


def _consts(s):
    pos = jnp.arange(s)
    m2s, e3 = _nsa_consts(s)
    return dict(tabs128=_rope_tables(pos, HEAD_DIM), tabs64=_rope_tables(pos, MLA_ROPE),
                tabs_cmp=_rope_tables(jnp.arange(s // CMP_STRIDE) * CMP_STRIDE + (CMP_LEN - 1), HEAD_DIM),
                m2s=m2s, e3=e3)


def _place():
    return lax.axis_index("x"), lax.axis_index("y"), lax.axis_index("c")


def _other_chips(x, y):
    return [(1 - x, y), (x, 1 - y), (1 - x, 1 - y)]


def _comm_call(body, out_shapes, n_sems, arrs, name):
    return pl.pallas_call(body, out_shape=tuple(out_shapes), in_specs=[_ANY] * len(arrs),
                          out_specs=tuple(_ANY for _ in out_shapes),
                          scratch_shapes=[pltpu.SemaphoreType.DMA((n_sems,)), pltpu.SemaphoreType.DMA((n_sems,))],
                          name=name)(*arrs)


def _gather_chips(arrs, *, name):
    n = len(arrs)

    def body(*refs):
        a_refs, out_refs, send_sems, recv_sems = refs[:n], refs[n:2 * n], refs[2 * n], refs[2 * n + 1]
        x, y, c = _place()
        me = 2 * x + y
        sibling = (x, y, 1 - c)
        chips = _other_chips(x, y)

        def copy(j, k, src, dst, to):
            return pltpu.make_async_remote_copy(src, dst, send_sems.at[6 * j + k], recv_sems.at[6 * j + k],
                                                device_id=to, device_id_type=_MESH)

        first = [copy(j, k, a_refs[j].at[c], out_refs[j].at[me, c], (px, py, c))
                 for k, (px, py) in enumerate(chips) for j in range(n)]
        for cp in first:
            cp.start()
        passed = []
        for k, (px, py) in enumerate(chips):
            for j in range(n):
                landed = out_refs[j].at[2 * px + py, c]
                copy(j, k, a_refs[j].at[c], landed, (px, py, c)).wait_recv()
                passed.append(copy(j, 3 + k, landed, landed, sibling))
                passed[-1].start()
        for k, (px, py) in enumerate(chips):
            for j in range(n):
                copy(j, 3 + k, a_refs[j].at[c], out_refs[j].at[2 * px + py, 1 - c], sibling).wait_recv()
        for cp in first + passed:
            cp.wait_send()

    return _comm_call(body, [_SDS((N_CHIPS,) + a.shape, a.dtype) for a in arrs], 6 * n, arrs, name)


def _alltoall_chips(arrs, modes, *, name):
    n = len(arrs)
    slot = lambda ref, mode, s: _slot_ref(ref, mode, s)
    lane_slots = modes

    def body(*refs):
        g_refs, out_refs, send_sems, recv_sems = refs[:n], refs[n:2 * n], refs[2 * n], refs[2 * n + 1]
        x, y, c = _place()
        me = 2 * x + y

        def copy(j, s):
            return pltpu.make_async_remote_copy(slot(g_refs[j], lane_slots[j], s), out_refs[j].at[me],
                                                send_sems.at[N_CHIPS * j + s], recv_sems.at[N_CHIPS * j + me],
                                                device_id=(s // 2, s % 2, c), device_id_type=_MESH)

        for s in range(N_CHIPS):
            @pl.when(s != me)
            def _():
                for j in range(n):
                    copy(j, s).start()
        for t in range(N_CHIPS):
            @pl.when(t != me)
            def _():
                for j in range(n):
                    pltpu.make_async_remote_copy(slot(g_refs[j], lane_slots[j], t), out_refs[j].at[t],
                                                 send_sems.at[N_CHIPS * j + t], recv_sems.at[N_CHIPS * j + t],
                                                 device_id=(t // 2, t % 2, c), device_id_type=_MESH).wait_recv()
        for s in range(N_CHIPS):
            @pl.when(s != me)
            def _():
                for j in range(n):
                    copy(j, s).wait_send()

    outs = [_SDS((N_CHIPS,) + _slot_shape(a, m), a.dtype) for a, m in zip(arrs, modes)]
    return _comm_call(body, outs, N_CHIPS * n, arrs, name)


def _swap_other_half(arrs, *, name):
    n = len(arrs)

    def body(*refs):
        g_refs, out_refs, send_sems, recv_sems = refs[:n], refs[n:2 * n], refs[2 * n], refs[2 * n + 1]
        x, y, c = _place()
        cps = [pltpu.make_async_remote_copy(g_refs[j].at[:, 1 - c], out_refs[j], send_sems.at[j], recv_sems.at[j],
                                            device_id=(x, y, 1 - c), device_id_type=_MESH) for j in range(n)]
        for cp in cps:
            cp.start()
        for cp in cps:
            cp.wait()

    return _comm_call(body, [_SDS((a.shape[0],) + a.shape[2:], a.dtype) for a in arrs], n, arrs, name)


def _swap_sibling(arrs, *, name):
    n = len(arrs)

    def body(*refs):
        f_refs, out_refs, send_sems, recv_sems = refs[:n], refs[n:2 * n], refs[2 * n], refs[2 * n + 1]
        x, y, c = _place()
        cps = [pltpu.make_async_remote_copy(f_refs[j], out_refs[j], send_sems.at[j], recv_sems.at[j],
                                            device_id=(x, y, 1 - c), device_id_type=_MESH) for j in range(n)]
        for cp in cps:
            cp.start()
        for cp in cps:
            cp.wait()

    return _comm_call(body, [_SDS(a.shape, a.dtype) for a in arrs], n, arrs, name)


_HBM = pl.BlockSpec(memory_space=pltpu.HBM)
_SEM = pl.BlockSpec(memory_space=pltpu.SEMAPHORE)
_EFFECT = pltpu.SideEffectType.DATAFLOW_SIDE_EFFECTING


def _slot_ref(ref, mode, s):
    return ref if mode == "same" else ref.at[s]


def _slot_shape(a, mode):
    return a.shape if mode == "same" else a.shape[1:]


def _send_start(arrs, modes, after, *, name):
    n = len(arrs)
    lands = [lax.empty((N_CHIPS,) + _slot_shape(a, m), a.dtype) for a, m in zip(arrs, modes)]

    def body(*refs):
        srcs, land_refs, send_sems, recv_sems, token = refs[:n], refs[n:2 * n], refs[2 * n + 1], refs[2 * n + 2], refs[-1]
        x, y, c = _place()
        me = 2 * x + y
        for s in range(N_CHIPS):
            @pl.when(s != me)
            def _():
                for j in range(n):
                    pltpu.make_async_remote_copy(_slot_ref(srcs[j], modes[j], s), land_refs[j].at[me],
                                                 send_sems.at[N_CHIPS * j + s], recv_sems.at[N_CHIPS * j + me],
                                                 device_id=(s // 2, s % 2, c), device_id_type=_MESH).start()
        token[...] = jnp.zeros_like(token)

    hbm = lambda a: pltpu.HBM(a.shape, a.dtype)
    sems = pltpu.SemaphoreType.DMA((N_CHIPS * n,))
    out = pl.pallas_call(
        body, name=name, out_shape=(sems, sems, *[hbm(a) for a in arrs], *[hbm(a) for a in lands], _SDS((8, LANES), f32)),
        in_specs=[_HBM] * (2 * n) + [_ANY], out_specs=(_SEM, _SEM, *[_HBM] * (2 * n), pl.BlockSpec(memory_space=pltpu.VMEM)),
        input_output_aliases={j: 2 + j for j in range(2 * n)},
        compiler_params=pltpu.CompilerParams(has_side_effects=_EFFECT),
    )(*[pltpu.with_memory_space_constraint(a, pltpu.HBM) for a in arrs + lands], after)
    return out[:-1], out[-1]


def _send_wait(started, modes, after, *, name):
    send_sems, recv_sems = started[0], started[1]
    n = (len(started) - 2) // 2
    thru = list(started[2:])

    def body(*refs):
        srcs, land_refs, send_sems, recv_sems = refs[:n], refs[n:2 * n], refs[2 * n], refs[2 * n + 1]
        x, y, c = _place()
        me = 2 * x + y
        for s in range(N_CHIPS):
            @pl.when(s != me)
            def _():
                for j in range(n):
                    cp = pltpu.make_async_remote_copy(_slot_ref(srcs[j], modes[j], s), land_refs[j].at[s],
                                                      send_sems.at[N_CHIPS * j + s], recv_sems.at[N_CHIPS * j + s],
                                                      device_id=(s // 2, s % 2, c), device_id_type=_MESH)
                    cp.wait_send()
                    cp.wait_recv()

    hbm = lambda a: pltpu.HBM(a.shape, a.dtype)
    out = pl.pallas_call(
        body, name=name, out_shape=tuple(hbm(a) for a in thru), in_specs=[_HBM] * (2 * n) + [_SEM, _SEM, _ANY],
        out_specs=tuple([_HBM] * (2 * n)), input_output_aliases={j: j for j in range(2 * n)},
        compiler_params=pltpu.CompilerParams(has_side_effects=_EFFECT),
    )(*thru, send_sems, recv_sems, after)
    return list(out[n:])


def _gather_all(a, *, name):
    def body(a_ref, out_ref, send_sems, recv_sems, local_sem):
        x, y, c = _place()
        flip = lambda v, f: (1 - v) if f else v
        peers = [(flip(x, f & 4), flip(y, f & 2), flip(c, f & 1)) for f in range(1, 8)]
        me = 4 * x + 2 * y + c
        mine = pltpu.make_async_copy(a_ref, out_ref.at[me], local_sem)
        mine.start()
        sends = [pltpu.make_async_remote_copy(a_ref, out_ref.at[me], send_sems.at[k], recv_sems.at[k], device_id=peer,
                                              device_id_type=_MESH) for k, peer in enumerate(peers)]
        for cp in sends:
            cp.start()
        for k, (px, py, pc) in enumerate(peers):
            pltpu.make_async_remote_copy(a_ref, out_ref.at[4 * px + 2 * py + pc], send_sems.at[k], recv_sems.at[k],
                                         device_id=(px, py, pc), device_id_type=_MESH).wait_recv()
        for cp in sends:
            cp.wait_send()
        mine.wait()

    return pl.pallas_call(body, out_shape=_SDS((8,) + a.shape, a.dtype), in_specs=[_ANY], out_specs=_ANY,
                          scratch_shapes=[pltpu.SemaphoreType.DMA((7,)), pltpu.SemaphoreType.DMA((7,)),
                                          pltpu.SemaphoreType.DMA], name=name)(a)


def _add_my_half(g, r, *, name):
    p, _, h, w = g.shape
    tw = _pick(w, (2048, 1024, 512, 256, 128))
    rb = max(d for d in range(16, h + 1, 16) if h % d == 0 and d * tw * 4 <= (2 << 20))

    def body(c_ref, g_ref, r_ref, o_ref):
        o_ref[...] = (g_ref[...] + r_ref[...]).astype(o_ref.dtype)

    blk = pl.BlockSpec((None, rb, tw), lambda s, i, j, c_ref: (s, i, j))
    grid_spec = pltpu.PrefetchScalarGridSpec(
        num_scalar_prefetch=1, grid=(p, h // rb, w // tw),
        in_specs=[pl.BlockSpec((None, None, rb, tw), lambda s, i, j, c_ref: (s, c_ref[0], i, j)), blk], out_specs=blk)
    c = lax.axis_index("c").astype(jnp.int32).reshape(1)
    return pl.pallas_call(body, out_shape=_SDS((p, h, w), _WIRE), grid_spec=grid_spec,
                          compiler_params=_cp(("parallel", "parallel", "parallel")), name=name)(c, g, r)


_WEIGHTS = ("pre_norm_g", "post_norm_g", "w_in", "b_in", "w_out", "fox_forget_bias", "nsa_cmp_pos_k", "nsa_cmp_w1_k",
            "nsa_cmp_w2_k", "nsa_cmp_pos_v", "nsa_cmp_w1_v", "nsa_cmp_w2_v", "mla_q_norm_g", "mla_w_uq",
            "mla_kv_norm_g", "mla_w_ukv")
_SHARD_AXIS = {"w_in": 2, "w_out": 1, "nsa_cmp_w1_k": 1, "nsa_cmp_w1_v": 1, "mla_w_uq": 2, "mla_w_ukv": 2}
_PACK_UNIT = 16 * LANES


def _pack(arrays, dtype):
    rows = []
    for a in arrays:
        v = a.astype(dtype).reshape(-1)
        pad = (-v.shape[0]) % _PACK_UNIT
        if pad:
            v = jnp.concatenate([v, jnp.zeros((pad,), dtype)])
        rows.append(v.reshape(-1, LANES))
    return jnp.concatenate(rows, axis=0)


def _unpack(flat, shapes):
    out, r = [], 0
    for shp in shapes:
        n = int(np.prod(shp))
        nr = -(-n // _PACK_UNIT) * (_PACK_UNIT // LANES)
        out.append(flat[r:r + nr].reshape(-1)[:n].reshape(shp))
        r += nr
    return out


def kernel(x, pre_norm_g, post_norm_g, w_in, b_in, w_out, fox_forget_bias, nsa_cmp_pos_k, nsa_cmp_w1_k, nsa_cmp_w2_k, nsa_cmp_pos_v, nsa_cmp_w1_v, nsa_cmp_w2_v, mla_q_norm_g, mla_w_uq, mla_kv_norm_g, mla_w_ukv, loss_target, m_pre_norm_g, m_post_norm_g, m_w_in, m_b_in, m_w_out, m_fox_forget_bias, m_nsa_cmp_pos_k, m_nsa_cmp_w1_k, m_nsa_cmp_w2_k, m_nsa_cmp_pos_v, m_nsa_cmp_w1_v, m_nsa_cmp_w2_v, m_mla_q_norm_g, m_mla_w_uq, m_mla_kv_norm_g, m_mla_w_ukv, v_pre_norm_g, v_post_norm_g, v_w_in, v_b_in, v_w_out, v_fox_forget_bias, v_nsa_cmp_pos_k, v_nsa_cmp_w1_k, v_nsa_cmp_w2_k, v_nsa_cmp_pos_v, v_nsa_cmp_w1_v, v_nsa_cmp_w2_v, v_mla_q_norm_g, v_mla_w_uq, v_mla_kv_norm_g, v_mla_w_ukv):
    given = dict(locals())
    local = {n: given[n] for n in _WEIGHTS}
    depth = pre_norm_g.shape[0]
    xs, target = x[0], loss_target[0]
    s = xs.shape[0]
    sharded = [n for n in _WEIGHTS if n in _SHARD_AXIS and n != "w_in"]
    small = [n for n in _WEIGHTS if n not in _SHARD_AXIS]
    chip = 2 * lax.axis_index("x") + lax.axis_index("y")
    core = lax.axis_index("c")
    own = lambda slots, mine: lax.dynamic_update_slice_in_dim(slots, mine[None], chip, axis=0)

    w_in_t = jnp.swapaxes(w_in, 1, 2).astype(_MXU)
    piece = lax.switch(chip, [functools.partial(_piece_from_shard, s=k) for k in range(N_CHIPS)], w_in_t)
    layer_shapes = [local[n].shape[1:] for n in sharded]
    flat = [_pack([local[n][l] for n in sharded], _MXU) for l in range(depth)]
    full = dict(local)
    for n in ["w_in"] + sharded:
        full[n] = []

    def add_layer(w_in_slots, flat_slots_):
        full["w_in"].append(w_in_slots)
        per_chip = [_unpack(flat_slots_[k], layer_shapes) for k in range(N_CHIPS)]
        for j, n in enumerate(sharded):
            full[n].append(jnp.concatenate([per_chip[k][j] for k in range(N_CHIPS)], axis=_SHARD_AXIS[n] - 1))

    halved = [piece[0].reshape(2, GROUP_W // 2, D_MODEL), flat[0].reshape(2, -1, LANES)]
    first_all = [own(a, b) for a, b in zip(_gather_chips(halved, name="gather_weights"), halved)]
    add_layer(first_all[0].reshape(N_CHIPS, GROUP_W, D_MODEL), first_all[1].reshape((N_CHIPS,) + flat[0].shape))
    later = [piece[l] for l in range(1, depth)] + flat[1:]
    started, token = _send_start(later, ["same"] * len(later), first_all[1], name="gather_later_start")
    full["pre_norm_g"] = pre_norm_g + token[0, 0]

    consts = _consts(s)
    params, act, saved = [], xs, []
    for l in range(depth):
        if l == 1:
            landed = [own(a, b) for a, b in zip(_send_wait(started, ["same"] * len(later), act,
                                                           name="gather_later_wait"), later)]
            for k in range(depth - 1):
                add_layer(landed[k], landed[depth - 1 + k])
        params.append(_layer_params(full, l))
        act, sv = _layer_fwd(act, params[l], consts, f"l{l}")
        saved.append(sv)
    dy, loss_parts = _loss_head(act, target, name="loss_head")

    def flat_slots(g, dtype):
        def part(n, k):
            a, ax = g[n], _SHARD_AXIS[n] - 1
            w = a.shape[ax] // N_CHIPS
            return lax.slice_in_dim(a, k * w, (k + 1) * w, axis=ax)
        return jnp.stack([_pack([part(n, k) for n in sharded], dtype) for k in range(N_CHIPS)])

    own_slot = lambda a: lax.dynamic_index_in_dim(a, chip, axis=0, keepdims=False)
    slots_of = lambda g: g["w_in"].reshape(N_CHIPS, GROUP_W, D_MODEL)

    modes = ["slots", "slots"]
    layer_grads, in_flight = [None] * depth, {}
    for l in reversed(range(depth)):
        dy, layer_grads[l] = _layer_bwd(dy, saved[l], params[l], consts, f"l{l}", _WIRE if l > 0 else f32)
        if l > 0:
            wire = [slots_of(layer_grads[l]), flat_slots(layer_grads[l], _WIRE)]
            started, token = _send_start(wire, modes, dy, name=f"reduce_l{l}_start")
            in_flight[l] = (started, wire)
            params[l - 1] = dict(params[l - 1], post_g=params[l - 1]["post_g"] + token[0, 0])
    grad_x = dy[None]
    grads = {n: jnp.stack([layer_grads[l][n] for l in range(depth)]) for n in small}
    loss_row = jnp.concatenate([jnp.sum(loss_parts).reshape(1), jnp.zeros((LANES - 1,), f32)])
    small_shapes = [(LANES,)] + [grads[n].shape for n in small]
    contrib = _pack([loss_row] + [grads[n] for n in small], f32)

    halves = [slots_of(layer_grads[0]).reshape(N_CHIPS, 2, GROUP_W // 2, D_MODEL),
              flat_slots(layer_grads[0], f32).reshape(N_CHIPS, 2, -1, LANES)]
    from_sibling = _swap_other_half(halves, name="reduce_pair")
    pair_sum = [_add_my_half(g, r, name=f"reduce_pair_add{j}") for j, (g, r) in enumerate(zip(halves, from_sibling))]
    from_chips = _alltoall_chips(pair_sum + [contrib], modes + ["same"], name="reduce_chips")
    my_half = [_sum_slots(own(slots, own_slot(ps)), name=f"reduce_chips_add{j}")
               for j, (slots, ps) in enumerate(zip(from_chips, pair_sum))]
    partial = []
    for l in range(1, depth):
        started, wire = in_flight[l]
        landed = _send_wait(started, modes, dy, name=f"reduce_l{l}_wait")
        partial += [_sum_slots(own(slots, own_slot(a)), name=f"reduce_l{l}_add{j}")
                    for j, (slots, a) in enumerate(zip(landed, wire))]
    partial.append(_sum_slots(own(from_chips[2], contrib), name="sum_small"))
    theirs = _swap_sibling(my_half + partial, name="reduce_share")
    first = core == 0
    whole = [jnp.concatenate([jnp.where(first, a, b), jnp.where(first, b, a)], axis=0)
             for a, b in zip(my_half, theirs[:2])]
    whole += [_add2(a[None], b[None], name=f"reduce_cores_add{j}")[0] for j, (a, b) in enumerate(zip(partial, theirs[2:]))]
    unpiece = [functools.partial(_shard_from_piece, s=k) for k in range(N_CHIPS)]
    summed = {"w_in": jnp.stack([lax.switch(chip, unpiece, whole[2 * l].T) for l in range(depth)])}
    rest = [_unpack(whole[2 * l + 1], layer_shapes) for l in range(depth)]
    for j, n in enumerate(sharded):
        summed[n] = jnp.stack([rest[l][j] for l in range(depth)])
    total = _unpack(whole[2 * depth], small_shapes)
    loss = total[0][0]
    summed.update(zip(small, total[1:]))

    deltas, new_m, new_v = {}, {}, {}
    for n in _WEIGHTS:
        deltas[n], new_m[n], new_v[n] = _adamw(local[n], summed[n], given["m_" + n], given["v_" + n], name=f"adamw_{n}")
    return (loss, grad_x, *[summed[n] for n in _WEIGHTS], *[deltas[n] for n in _WEIGHTS],
            *[new_m[n] for n in _WEIGHTS], *[new_v[n] for n in _WEIGHTS])
```

```python
import functools
import math

import numpy as np
import jax
import jax.numpy as jnp
from jax import lax
from jax.experimental import pallas as pl
from jax.experimental.pallas import tpu as pltpu

f32 = jnp.float32
bf16 = jnp.bfloat16
_MXU = jnp.bfloat16
_WIRE = jnp.bfloat16
_SDS = jax.ShapeDtypeStruct
_ANY = pl.BlockSpec(memory_space=pl.ANY)
_MESH = pl.DeviceIdType.MESH

D_MODEL = 2048
N_HEADS = 4
HEAD_DIM = 128
GROUP = 512
RMS_EPS = 1e-6
NEG_INF = -1e30
ROPE_THETA = 10000.0
CMP_LEN, CMP_STRIDE, SEL_LEN, SEL_TOPN, WINDOW = 32, 16, 64, 16, 512
FORCED_BONUS = 1e6
MLA_Q_RANK, MLA_KV_RANK, MLA_NOPE, MLA_ROPE = 384, 128, 128, 64
ADAM_LR, ADAM_B1, ADAM_B2, ADAM_EPS, ADAM_WD, ADAM_STEP = 0.001, 0.9, 0.999, 1e-08, 0.01, 10
LANES = 128
VMEM_LIMIT = 56 * 1024 * 1024
HP_FWD, HP_BWD = 2, 2

_SEGS = (
    ("sb_q", 512), ("sb_k", 512), ("sb_v", 512), ("sb_gate", 512), ("nsa_q", 512), ("nsa_k_cmp", 128),
    ("nsa_v_cmp", 128), ("nsa_k_sel", 128), ("nsa_v_sel", 128), ("nsa_k_win", 128), ("nsa_v_win", 128),
    ("nsa_branch", 12), ("nsa_gate", 512), ("fox_q", 512), ("fox_k", 512), ("fox_v", 512), ("fox_f", 4),
    ("fox_gate", 512), ("mla_cq", 384), ("mla_ckv", 128), ("mla_k_rope", 64), ("mla_gate", 512),
)
_ORIG, _WID = {}, {}
_o = 0
for _n, _w in _SEGS:
    _ORIG[_n], _WID[_n] = _o, _w
    _o += _w
IN_WIDTH = _o
N_CHIPS = 4
CHIP_COLS = IN_WIDTH // N_CHIPS
GROUP_W = 2048
ZW = N_CHIPS * GROUP_W
_GROUPS = (
    (("sb_q", 0, 512, 0), ("sb_k", 0, 512, 512), ("sb_v", 0, 512, 1024), ("sb_gate", 0, 212, 1536)),
    (("nsa_q", 0, 512, 0), ("nsa_k_cmp", 0, 128, 512), ("nsa_v_cmp", 0, 128, 640), ("nsa_k_sel", 0, 128, 768),
     ("nsa_v_sel", 0, 128, 896), ("nsa_k_win", 0, 128, 1024), ("nsa_v_win", 0, 128, 1152), ("nsa_branch", 0, 12, 1280),
     ("sb_gate", 212, 512, 1408), ("nsa_gate", 0, 156, 1712)),
    (("fox_q", 0, 512, 0), ("fox_k", 0, 512, 512), ("fox_v", 0, 368, 1024), ("nsa_gate", 156, 512, 1408)),
    (("mla_cq", 0, 384, 0), ("mla_ckv", 0, 128, 384), ("mla_k_rope", 0, 64, 512), ("fox_f", 0, 4, 640),
     ("fox_v", 368, 512, 768), ("fox_gate", 0, 512, 1024), ("mla_gate", 0, 512, 1536)),
)
_PIECES = {n: [] for n, _ in _SEGS}
for _s, _grp in enumerate(_GROUPS):
    _cover = sorted((_ORIG[n] + lo, _ORIG[n] + hi) for n, lo, hi, _ in _grp)
    assert _cover[0][0] == _s * CHIP_COLS and _cover[-1][1] == (_s + 1) * CHIP_COLS
    assert all(a[1] == b[0] for a, b in zip(_cover, _cover[1:]))
    _ends = sorted((off, off + hi - lo) for _, lo, hi, off in _grp)
    assert all(a[1] <= b[0] for a, b in zip(_ends, _ends[1:])) and _ends[-1][1] <= GROUP_W
    assert _ends[0][0] == 0 and all(e[0] % 16 == 0 for e in _ends)
    for _n, _lo, _hi, _off in _grp:
        _PIECES[_n].append((_s * GROUP_W + _off, _lo, _hi))
_AL = {n: p[0][0] for n, p in _PIECES.items() if len(p) == 1}


def _cp(sem=None):
    return pltpu.CompilerParams(dimension_semantics=sem, vmem_limit_bytes=VMEM_LIMIT)


def _mm(a, b):
    return jnp.dot(a.astype(_MXU), b.astype(_MXU), preferred_element_type=f32)


def _mm_nt(a, b):
    return lax.dot_general(a.astype(_MXU), b.astype(_MXU), (((1,), (1,)), ((), ())), preferred_element_type=f32)


def _mm_tn(a, b):
    return lax.dot_general(a.astype(_MXU), b.astype(_MXU), (((0,), (0,)), ((), ())), preferred_element_type=f32)


def _mm_split(x, t):
    hi = x.astype(_MXU)
    lo = (x - hi.astype(f32)).astype(_MXU)
    return jnp.dot(hi, t, preferred_element_type=f32) + jnp.dot(lo, t, preferred_element_type=f32)


def _sigmoid(x):
    return 1.0 / (1.0 + jnp.exp(-x))


def _iota(shape, dim):
    return lax.broadcasted_iota(jnp.int32, shape, dim)


def _pick(n, prefs):
    for p in prefs:
        if n % p == 0:
            return p
    return n


def _matmul(a, b, mode, *, bias=None, out_dtype=f32, name):
    grouped = b.ndim == 3
    b_shape = (b.shape[0] * b.shape[1], b.shape[2]) if grouped else b.shape
    if mode == "nn":
        (M, K), (K2, N) = a.shape, b_shape
    elif mode == "nt":
        (M, K), (N, K2) = a.shape, b_shape
    else:
        (K, M), (K2, N) = a.shape, b_shape
    assert K == K2
    tm = _pick(M, (1024, 512, 384, 256, 128))
    tn = _pick(N, (1024, 512, 384, 256, 128))
    tk = K if K <= 2048 else _pick(K, (2048, 2432, 1024, 512))
    nk = K // tk
    a_spec = {"nn": pl.BlockSpec((tm, tk), lambda i, j, k: (i, k)),
              "nt": pl.BlockSpec((tm, tk), lambda i, j, k: (i, k)),
              "tn": pl.BlockSpec((tk, tm), lambda i, j, k: (k, i))}[mode]
    if not grouped:
        b_spec = {"nn": pl.BlockSpec((tk, tn), lambda i, j, k: (k, j)),
                  "nt": pl.BlockSpec((tn, tk), lambda i, j, k: (j, k)),
                  "tn": pl.BlockSpec((tk, tn), lambda i, j, k: (k, j))}[mode]
    elif mode == "nt":
        per = b.shape[1] // tn
        b_spec = pl.BlockSpec((None, tn, tk), lambda i, j, k: (j // per, j % per, k))
    else:
        assert mode == "nn"
        per = b.shape[1] // tk
        b_spec = pl.BlockSpec((None, tk, tn), lambda i, j, k: (k // per, k % per, j))
    dot = {"nn": _mm, "nt": _mm_nt, "tn": _mm_tn}[mode]
    has_bias = bias is not None

    def body(*refs):
        if has_bias:
            a_ref, b_ref, bias_ref, o_ref, acc_ref = refs
        else:
            a_ref, b_ref, o_ref, acc_ref = refs
            bias_ref = None
        k = pl.program_id(2)
        part = dot(a_ref[...], b_ref[...])

        def finish(total):
            if has_bias:
                total = total + bias_ref[...]
            o_ref[...] = total.astype(o_ref.dtype)

        if nk == 1:
            finish(part)
        else:
            @pl.when(k == 0)
            def _():
                acc_ref[...] = part

            @pl.when(k > 0)
            def _():
                acc_ref[...] += part

            @pl.when(k == nk - 1)
            def _():
                finish(acc_ref[...])

    in_specs = [a_spec, b_spec]
    args = [a, b]
    if has_bias:
        in_specs.append(pl.BlockSpec((1, tn), lambda i, j, k: (0, j)))
        args.append(bias.reshape(1, N))
    return pl.pallas_call(
        body, out_shape=_SDS((M, N), out_dtype), grid=(M // tm, N // tn, nk),
        in_specs=in_specs, out_specs=pl.BlockSpec((tm, tn), lambda i, j, k: (i, j)),
        scratch_shapes=[pltpu.VMEM((tm, tn), f32)],
        compiler_params=_cp(("parallel", "parallel", "arbitrary")), name=name,
    )(*args)


def _row_block(s):
    return _pick(s, (256, 128))


def _rms_fwd(x, g, *, out_dtype, name):
    s, d = x.shape
    rb = _row_block(s)

    def body(x_ref, g_ref, o_ref):
        xv = x_ref[...]
        r = lax.rsqrt(jnp.mean(xv * xv, axis=-1, keepdims=True) + RMS_EPS)
        o_ref[...] = (xv * r * g_ref[...]).astype(o_ref.dtype)

    return pl.pallas_call(
        body, out_shape=_SDS((s, d), out_dtype), grid=(s // rb,),
        in_specs=[pl.BlockSpec((rb, d), lambda i: (i, 0)), pl.BlockSpec((1, d), lambda i: (0, 0))],
        out_specs=pl.BlockSpec((rb, d), lambda i: (i, 0)), compiler_params=_cp(("parallel",)), name=name,
    )(x, g.reshape(1, d))


def _postnorm_fwd(u, g, x, *, name):
    s, d = u.shape
    rb = _row_block(s)

    def body(u_ref, g_ref, x_ref, o_ref):
        uv = u_ref[...]
        r = lax.rsqrt(jnp.mean(uv * uv, axis=-1, keepdims=True) + RMS_EPS)
        o_ref[...] = x_ref[...] + uv * r * g_ref[...]

    return pl.pallas_call(
        body, out_shape=_SDS((s, d), f32), grid=(s // rb,),
        in_specs=[pl.BlockSpec((rb, d), lambda i: (i, 0)), pl.BlockSpec((1, d), lambda i: (0, 0)),
                  pl.BlockSpec((rb, d), lambda i: (i, 0))],
        out_specs=pl.BlockSpec((rb, d), lambda i: (i, 0)), compiler_params=_cp(("parallel",)), name=name,
    )(u, g.reshape(1, d), x)


def _fold_rows(v):
    r = v.shape[0]
    acc = v[0:8]
    for k in range(1, r // 8):
        acc = acc + v[8 * k:8 * k + 8]
    return acc


def _rms_bwd(dy, x, g, res=None, *, name):
    s, d = x.shape
    rb = _row_block(s)
    nb = s // rb
    has_res = res is not None

    def body(*refs):
        if has_res:
            dy_ref, x_ref, g_ref, res_ref, dx_ref, dg_ref, acc_ref = refs
        else:
            dy_ref, x_ref, g_ref, dx_ref, dg_ref, acc_ref = refs
        i = pl.program_id(0)
        xv = x_ref[...]
        r = lax.rsqrt(jnp.mean(xv * xv, axis=-1, keepdims=True) + RMS_EPS)
        xh = xv * r
        dyv = dy_ref[...]
        dxh = dyv * g_ref[...]
        dx = r * (dxh - xh * jnp.mean(dxh * xh, axis=-1, keepdims=True))
        if has_res:
            dx = dx + res_ref[...]
        dx_ref[...] = dx
        part = _fold_rows(dyv * xh)

        @pl.when(i == 0)
        def _():
            acc_ref[...] = part

        @pl.when(i > 0)
        def _():
            acc_ref[...] += part

        @pl.when(i == nb - 1)
        def _():
            dg_ref[...] = jnp.sum(acc_ref[...], axis=0, keepdims=True)

    blk = pl.BlockSpec((rb, d), lambda i: (i, 0))
    in_specs = [blk, blk, pl.BlockSpec((1, d), lambda i: (0, 0))] + ([blk] if has_res else [])
    args = [dy, x, g.reshape(1, d)] + ([res] if has_res else [])
    return pl.pallas_call(
        body, out_shape=(_SDS((s, d), f32), _SDS((1, d), f32)), grid=(nb,), in_specs=in_specs,
        out_specs=(blk, pl.BlockSpec((1, d), lambda i: (0, 0))),
        scratch_shapes=[pltpu.VMEM((8, d), f32)], compiler_params=_cp(("arbitrary",)), name=name,
    )(*args)


def _loss_head(y, target, *, name):
    s, d = y.shape
    rb = _row_block(s)
    nb = s // rb

    def body(y_ref, t_ref, dy_ref, l_ref):
        i = pl.program_id(0)
        e = y_ref[...] - t_ref[...]
        dy_ref[...] = e * (1.0 / d)
        rows = _fold_rows(e * e)
        part = rows[:, 0:LANES]
        for k in range(1, d // LANES):
            part = part + rows[:, k * LANES:(k + 1) * LANES]
        part = part * (0.5 / d)

        @pl.when(i == 0)
        def _():
            l_ref[...] = part

        @pl.when(i > 0)
        def _():
            l_ref[...] += part

    blk = pl.BlockSpec((rb, d), lambda i: (i, 0))
    return pl.pallas_call(
        body, out_shape=(_SDS((s, d), f32), _SDS((8, LANES), f32)), grid=(nb,), in_specs=[blk, blk],
        out_specs=(blk, pl.BlockSpec((8, LANES), lambda i: (0, 0))),
        compiler_params=_cp(("arbitrary",)), name=name,
    )(y, target)


def _colsum(a, *, name):
    s, n = a.shape
    rb = _row_block(s)
    nb = s // rb
    tn = _pick(n, (2432, 2048, 1024, 512, 384, 128))

    def body(a_ref, o_ref, acc_ref):
        i = pl.program_id(1)
        part = _fold_rows(a_ref[...].astype(f32))

        @pl.when(i == 0)
        def _():
            acc_ref[...] = part

        @pl.when(i > 0)
        def _():
            acc_ref[...] += part

        @pl.when(i == nb - 1)
        def _():
            o_ref[...] = jnp.sum(acc_ref[...], axis=0, keepdims=True)

    return pl.pallas_call(
        body, out_shape=_SDS((1, n), f32), grid=(n // tn, nb),
        in_specs=[pl.BlockSpec((rb, tn), lambda j, i: (i, j))], out_specs=pl.BlockSpec((1, tn), lambda j, i: (0, j)),
        scratch_shapes=[pltpu.VMEM((8, tn), f32)], compiler_params=_cp(("parallel", "arbitrary")), name=name,
    )(a)


def _gate_fwd(outs, gate, *, name):
    s, d = gate.shape
    rb = _row_block(s)
    n = len(outs)
    w = d // n

    def body(*refs):
        g_ref, m_ref = refs[n], refs[n + 1]
        for k in range(n):
            gv = g_ref[:, k * w:(k + 1) * w]
            m_ref[:, k * w:(k + 1) * w] = (refs[k][...] * (gv * _sigmoid(gv))).astype(m_ref.dtype)

    blk = pl.BlockSpec((rb, d), lambda i: (i, 0))
    part = pl.BlockSpec((rb, w), lambda i: (i, 0))
    return pl.pallas_call(body, out_shape=_SDS((s, d), _MXU), grid=(s // rb,), in_specs=[part] * n + [blk],
                          out_specs=blk, compiler_params=_cp(("parallel",)), name=name)(*outs, gate)


def _gate_bwd(dmix, outs, gate, *, name):
    s, d = gate.shape
    rb = _row_block(s)
    n = len(outs)
    w = d // n

    def body(*refs):
        dm_ref, o_refs, g_ref, do_refs, dg_ref = refs[0], refs[1:1 + n], refs[1 + n], refs[2 + n:2 + 2 * n], refs[-1]
        for k in range(n):
            sl = slice(k * w, (k + 1) * w)
            gv = g_ref[:, sl]
            sg = _sigmoid(gv)
            dm = dm_ref[:, sl]
            do_refs[k][...] = dm * (gv * sg)
            dg_ref[:, sl] = dm * o_refs[k][...] * (sg * (1.0 + gv * (1.0 - sg)))

    blk = pl.BlockSpec((rb, d), lambda i: (i, 0))
    part = pl.BlockSpec((rb, w), lambda i: (i, 0))
    return pl.pallas_call(body, out_shape=tuple(_SDS((s, w), f32) for _ in range(n)) + (_SDS((s, d), f32),),
                          grid=(s // rb,), in_specs=[blk] + [part] * n + [blk], out_specs=(part,) * n + (blk,),
                          compiler_params=_cp(("parallel",)), name=name)(dmix, *outs, gate)


def _adamw(w, g, m, v, *, name):
    shape = w.shape
    cols = shape[-1]
    rows = int(np.prod(shape[:-1])) if len(shape) > 1 else 1
    to2 = lambda t: t.reshape(rows, cols)
    rb = rows
    if rows * cols * 4 > (1 << 20):
        rb = max(d for d in range(8, rows + 1, 8) if rows % d == 0 and (d * cols * 4 <= (1600 << 10) or d == 8))

    def body(w_ref, g_ref, m_ref, v_ref, d_ref, nm_ref, nv_ref):
        gv = g_ref[...]
        mn = ADAM_B1 * m_ref[...] + (1.0 - ADAM_B1) * gv
        vn = ADAM_B2 * v_ref[...] + (1.0 - ADAM_B2) * (gv * gv)
        m_hat = mn / (1.0 - ADAM_B1 ** ADAM_STEP)
        v_hat = vn / (1.0 - ADAM_B2 ** ADAM_STEP)
        d_ref[...] = -ADAM_LR * (m_hat / (jnp.sqrt(v_hat) + ADAM_EPS) + ADAM_WD * w_ref[...])
        nm_ref[...] = mn
        nv_ref[...] = vn

    blk = pl.BlockSpec((rb, cols), lambda i: (i, 0))
    out = pl.pallas_call(body, out_shape=tuple(_SDS((rows, cols), f32) for _ in range(3)), grid=(rows // rb,),
                         in_specs=[blk] * 4, out_specs=(blk,) * 3, compiler_params=_cp(("parallel",)),
                         name=name)(to2(w), to2(g), to2(m), to2(v))
    return tuple(t.reshape(shape) for t in out)


def _sum_slots(a, *, name):
    p, n, c = a.shape
    rb = max(d for d in range(8, n + 1, 8) if n % d == 0 and (p * d * c * 4 <= (6 << 20) or d == 8))

    def body(a_ref, o_ref):
        acc = a_ref[0].astype(f32)
        for k in range(1, p):
            acc = acc + a_ref[k].astype(f32)
        o_ref[...] = acc

    return pl.pallas_call(body, out_shape=_SDS((n, c), f32), grid=(n // rb,),
                          in_specs=[pl.BlockSpec((p, rb, c), lambda i: (0, i, 0))],
                          out_specs=pl.BlockSpec((rb, c), lambda i: (i, 0)), compiler_params=_cp(("parallel",)),
                          name=name)(a)


def _add2(a, b, *, name):
    p, n, c = a.shape
    rb = max(d for d in range(8, n + 1, 8) if n % d == 0 and (d * c * 4 <= (2 << 20) or d == 8))

    def body(a_ref, b_ref, o_ref):
        o_ref[...] = a_ref[...] + b_ref[...]

    blk = pl.BlockSpec((1, rb, c), lambda s, i: (s, i, 0))
    return pl.pallas_call(body, out_shape=_SDS((p, n, c), f32), grid=(p, n // rb), in_specs=[blk, blk], out_specs=blk,
                          compiler_params=_cp(("parallel", "parallel")), name=name)(a, b)


def _rope_tables(pos, dim):
    half = dim // 2
    inv = ROPE_THETA ** (-jnp.arange(half, dtype=f32) / half)
    ang = pos.astype(f32)[:, None] * inv[None, :]
    c, s = jnp.cos(ang), jnp.sin(ang)
    z = jnp.zeros_like(c)
    pad = [jnp.zeros((pos.shape[0], LANES - dim), f32)] if dim < LANES else []
    return (jnp.concatenate([c, c] + pad, axis=1), jnp.concatenate([-s, z] + pad, axis=1),
            jnp.concatenate([z, s] + pad, axis=1))


def _rope(x, cos, sa, sb, half, transpose=False):
    if transpose:
        return x * cos + pltpu.roll(x * sa, half, 1) + pltpu.roll(x * sb, LANES - half, 1)
    return x * cos + pltpu.roll(x, LANES - half, 1) * sa + pltpu.roll(x, half, 1) * sb


def _rope_call(items, tables, half, transpose, *, name):
    s = items[0][0].shape[0]
    rb = _row_block(s)
    n = len(items)

    def body(*refs):
        cos, sa, sb = refs[n][...], refs[n + 1][...], refs[n + 2][...]
        for k in range(n):
            x_ref, o_ref = refs[k], refs[n + 3 + k]
            for j in range(items[k][1] // LANES):
                sl = slice(j * LANES, (j + 1) * LANES)
                o_ref[:, sl] = _rope(x_ref[:, sl], cos, sa, sb, half, transpose)

    in_specs = [pl.BlockSpec((rb, w), functools.partial(lambda i, cb: (i, cb), cb=cb)) for _, w, cb in items]
    in_specs += [pl.BlockSpec((rb, LANES), lambda i: (i, 0))] * 3
    out_specs = tuple(pl.BlockSpec((rb, w), lambda i: (i, 0)) for _, w, _ in items)
    return pl.pallas_call(
        body, out_shape=tuple(_SDS((s, w), f32) for _, w, _ in items), grid=(s // rb,), in_specs=in_specs,
        out_specs=out_specs, compiler_params=_cp(("parallel",)), name=name,
    )(*[a for a, _, _ in items], *tables)


def _attn_block(s):
    return _pick(s, (512, 256, 128))


def _lower_mask(b, strict):
    r, c = _iota((b, b), 0), _iota((b, b), 1)
    return (c < r) if strict else (c <= r)


def _pick_lane(block, h):
    return jnp.sum(jnp.where(_iota(block.shape, 1) == h, block, 0.0), axis=1, keepdims=True)


def _head_bias(cum_blk, g, j, hp):
    if hp == N_HEADS:
        return cum_blk[:, j:j + 1]
    return _pick_lane(cum_blk, g * hp + j)


def _attn_fwd(q, k, v, qcol, kcol, vcol, dq, cum, cum_t, *, scale, hp, name):
    s = q.shape[0]
    b = _attn_block(s)
    nq = s // b
    has_bias = cum is not None
    assert qcol % hp == 0 and kcol % hp == 0 and vcol % hp == 0

    def body(*refs):
        if has_bias:
            q_ref, k_ref, v_ref, cum_ref, cumt_ref, o_ref, lse_ref = refs
        else:
            q_ref, k_ref, v_ref, o_ref, lse_ref = refs
        g, i = pl.program_id(0), pl.program_id(1)
        qs = [q_ref[:, j * dq:(j + 1) * dq].astype(_MXU) for j in range(hp)]
        cqs = [_head_bias(cum_ref[...], g, j, hp) for j in range(hp)] if has_bias else None

        def chunk(c, carry, diag):
            st = pl.multiple_of(c * b, b)
            mask = _lower_mask(b, False) if diag else None
            out = []
            for j in range(hp):
                m, l, acc = carry[j]
                z = _mm_nt(qs[j], k_ref[pl.ds(st, b), j * dq:(j + 1) * dq]) * scale
                if has_bias:
                    z = z + (cqs[j] - cumt_ref[j, c])
                if diag:
                    z = jnp.where(mask, z, NEG_INF)
                m_new = jnp.maximum(m, jnp.max(z, axis=1, keepdims=True))
                p = jnp.exp(z - m_new)
                if diag:
                    p = jnp.where(mask, p, 0.0)
                alpha = jnp.exp(m - m_new)
                l = alpha * l + jnp.sum(p, axis=1, keepdims=True)
                acc = alpha * acc + _mm(p, v_ref[pl.ds(st, b), j * HEAD_DIM:(j + 1) * HEAD_DIM])
                out.append((m_new, l, acc))
            return tuple(out)

        init = tuple((jnp.full((b, 1), NEG_INF, f32), jnp.zeros((b, 1), f32), jnp.zeros((b, HEAD_DIM), f32))
                     for _ in range(hp))
        carry = lax.fori_loop(0, i, lambda c, cr: chunk(c, cr, False), init)
        for j, (m, l, acc) in enumerate(chunk(i, carry, True)):
            o_ref[:, j * HEAD_DIM:(j + 1) * HEAD_DIM] = acc / l
            lse_ref[j] = m + jnp.log(l)

    in_specs = [pl.BlockSpec((b, hp * dq), lambda g, i: (i, qcol // hp + g)),
                pl.BlockSpec((s, hp * dq), lambda g, i: (0, kcol // hp + g)),
                pl.BlockSpec((s, hp * HEAD_DIM), lambda g, i: (0, vcol // hp + g))]
    args = [q, k, v]
    if has_bias:
        in_specs += [pl.BlockSpec((b, LANES), lambda g, i: (i, 0)),
                     pl.BlockSpec((hp, nq, 1, b), lambda g, i: (g, 0, 0, 0))]
        args += [cum, cum_t]
    return pl.pallas_call(
        body, out_shape=(_SDS((s, N_HEADS * HEAD_DIM), f32), _SDS((N_HEADS, s, 1), f32)), grid=(N_HEADS // hp, nq),
        in_specs=in_specs,
        out_specs=(pl.BlockSpec((b, hp * HEAD_DIM), lambda g, i: (i, g)),
                   pl.BlockSpec((hp, b, 1), lambda g, i: (g, i, 0))),
        compiler_params=_cp(("parallel", "parallel")), name=name,
    )(*args)


def _attn_bwd(q, k, v, qcol, kcol, vcol, dq, do, o, lse, cum, cum_t, *, scale, hp, name):
    s = q.shape[0]
    b = _attn_block(s)
    nq = s // b
    has_bias = cum is not None
    assert qcol % hp == 0 and kcol % hp == 0 and vcol % hp == 0
    hd = lambda j: slice(j * HEAD_DIM, (j + 1) * HEAD_DIM)
    hq = lambda j: slice(j * dq, (j + 1) * dq)

    def body(*refs):
        if has_bias:
            (q_ref, k_ref, v_ref, do_ref, o_ref, lse_ref, cum_ref, cumt_ref, dq_ref, dk_ref, dv_ref, dck_ref,
             p_sc, dp_sc) = refs
        else:
            q_ref, k_ref, v_ref, do_ref, o_ref, lse_ref, dq_ref, dk_ref, dv_ref = refs
        g, i = pl.program_id(0), pl.program_id(1)

        @pl.when(i == 0)
        def _():
            dk_ref[...] = jnp.zeros_like(dk_ref)
            dv_ref[...] = jnp.zeros_like(dv_ref)
            if has_bias:
                dck_ref[...] = jnp.zeros_like(dck_ref)

        qs = [q_ref[:, hq(j)].astype(_MXU) for j in range(hp)]
        dos = [do_ref[:, hd(j)].astype(_MXU) for j in range(hp)]
        lses = [lse_ref[j] for j in range(hp)]
        cqs = [_head_bias(cum_ref[...], g, j, hp) for j in range(hp)] if has_bias else None

        def probs(j, c, diag):
            st = pl.multiple_of(c * b, b)
            z = _mm_nt(qs[j], k_ref[pl.ds(st, b), hq(j)]) * scale
            if has_bias:
                z = z + (cqs[j] - cumt_ref[j, c])
            p = jnp.exp(z - lses[j])
            if diag:
                p = jnp.where(_lower_mask(b, False), p, 0.0)
            return p, _mm_nt(dos[j], v_ref[pl.ds(st, b), hd(j)])

        if has_bias:
            def first(c, accs, diag):
                out = []
                for j in range(hp):
                    p, dp = probs(j, c, diag)
                    p_sc[j, c] = p
                    dp_sc[j, c] = dp
                    out.append(accs[j] + jnp.sum(p * dp, axis=1, keepdims=True))
                return tuple(out)

            deltas = lax.fori_loop(0, i, lambda c, a: first(c, a, False),
                                   tuple(jnp.zeros((b, 1), f32) for _ in range(hp)))
            deltas = first(i, deltas, True)
        else:
            deltas = [jnp.sum(do_ref[:, hd(j)] * o_ref[:, hd(j)], axis=1, keepdims=True) for j in range(hp)]

        def chunk(c, dq_accs, diag):
            st = pl.multiple_of(c * b, b)
            out = []
            for j in range(hp):
                p, dp = (p_sc[j, c], dp_sc[j, c]) if has_bias else probs(j, c, diag)
                ds = p * (dp - deltas[j])
                dk_ref[pl.ds(st, b), hq(j)] += _mm_tn(ds, qs[j]) * scale
                dv_ref[pl.ds(st, b), hd(j)] += _mm_tn(p, dos[j])
                if has_bias:
                    dck_ref[j, c] += -jnp.sum(ds, axis=0, keepdims=True)
                out.append(dq_accs[j] + _mm(ds, k_ref[pl.ds(st, b), hq(j)]))
            return tuple(out)

        accs = lax.fori_loop(0, i, lambda c, a: chunk(c, a, False), tuple(jnp.zeros((b, dq), f32) for _ in range(hp)))
        for j, acc in enumerate(chunk(i, accs, True)):
            dq_ref[:, hq(j)] = acc * scale

    rowq = pl.BlockSpec((b, hp * HEAD_DIM), lambda g, i: (i, g))
    in_specs = [pl.BlockSpec((b, hp * dq), lambda g, i: (i, qcol // hp + g)),
                pl.BlockSpec((s, hp * dq), lambda g, i: (0, kcol // hp + g)),
                pl.BlockSpec((s, hp * HEAD_DIM), lambda g, i: (0, vcol // hp + g)), rowq, rowq,
                pl.BlockSpec((hp, b, 1), lambda g, i: (g, i, 0))]
    args = [q, k, v, do, o, lse]
    out_shape = [_SDS((s, N_HEADS * dq), f32), _SDS((s, N_HEADS * dq), f32), _SDS((s, N_HEADS * HEAD_DIM), f32)]
    out_specs = [pl.BlockSpec((b, hp * dq), lambda g, i: (i, g)), pl.BlockSpec((s, hp * dq), lambda g, i: (0, g)),
                 pl.BlockSpec((s, hp * HEAD_DIM), lambda g, i: (0, g))]
    if has_bias:
        in_specs += [pl.BlockSpec((b, LANES), lambda g, i: (i, 0)),
                     pl.BlockSpec((hp, nq, 1, b), lambda g, i: (g, 0, 0, 0))]
        args += [cum, cum_t]
        out_shape.append(_SDS((N_HEADS, nq, 1, b), f32))
        out_specs.append(pl.BlockSpec((hp, nq, 1, b), lambda g, i: (g, 0, 0, 0)))
    return pl.pallas_call(
        body, out_shape=tuple(out_shape), grid=(N_HEADS // hp, nq), in_specs=in_specs, out_specs=tuple(out_specs),
        scratch_shapes=[pltpu.VMEM((hp, nq, b, b), f32)] * 2 if has_bias else [],
        compiler_params=_cp(("parallel", "arbitrary")), name=name,
    )(*args)


def _tri(b, kind):
    r, c = _iota((b, b), 0), _iota((b, b), 1)
    cond = {"row_gt": r > c, "row_lt": r < c, "row_ge": r >= c, "row_le": r <= c}[kind]
    return jnp.where(cond, 1.0, 0.0).astype(_MXU)


def _log_keep(z):
    return -(jnp.maximum(z, 0.0) + jnp.log1p(jnp.exp(-jnp.abs(z))))


def _sb_fwd(z_all, *, hp, name):
    s = z_all.shape[0]
    b = _attn_block(s)
    nq = s // b
    scale = HEAD_DIM ** -0.5
    qcol, kcol, vcol = (_AL[n] // (hp * HEAD_DIM) for n in ("sb_q", "sb_k", "sb_v"))
    hd = lambda j: slice(j * HEAD_DIM, (j + 1) * HEAD_DIM)

    def body(q_ref, k_ref, v_ref, o_ref):
        i = pl.program_id(1)
        qs = [q_ref[:, hd(j)].astype(_MXU) for j in range(hp)]
        upper = _tri(b, "row_gt")

        def chunk(c, carry, diag):
            st = pl.multiple_of(c * b, b)
            mask = _lower_mask(b, True) if diag else None
            out = []
            for j in range(hp):
                rsum, acc = carry[j]
                z = _mm_nt(qs[j], k_ref[pl.ds(st, b), hd(j)]) * scale
                lk = _log_keep(z)
                if diag:
                    lk = jnp.where(mask, lk, 0.0)
                a = z + lk + _mm_split(lk, upper) + rsum
                if diag:
                    a = jnp.where(mask, a, NEG_INF)
                acc = acc + _mm(jnp.exp(a), v_ref[pl.ds(st, b), hd(j)])
                out.append((rsum + jnp.sum(lk, axis=1, keepdims=True), acc))
            return tuple(out)

        init = tuple((jnp.zeros((b, 1), f32), jnp.zeros((b, HEAD_DIM), f32)) for _ in range(hp))
        carry = lax.fori_loop(0, i, lambda t, cr: chunk(i - 1 - t, cr, False), chunk(i, init, True))
        for j in range(hp):
            o_ref[:, hd(j)] = carry[j][1]

    w = hp * HEAD_DIM
    return pl.pallas_call(
        body, out_shape=_SDS((s, GROUP), f32), grid=(N_HEADS // hp, nq),
        in_specs=[pl.BlockSpec((b, w), lambda g, i: (i, qcol + g)), pl.BlockSpec((s, w), lambda g, i: (0, kcol + g)),
                  pl.BlockSpec((s, w), lambda g, i: (0, vcol + g))],
        out_specs=pl.BlockSpec((b, w), lambda g, i: (i, g)),
        compiler_params=_cp(("parallel", "parallel")), name=name,
    )(z_all, z_all, z_all)


def _sb_bwd(z_all, do, *, hp, name):
    s = z_all.shape[0]
    b = _attn_block(s)
    nq = s // b
    scale = HEAD_DIM ** -0.5
    qcol, kcol, vcol = (_AL[n] // (hp * HEAD_DIM) for n in ("sb_q", "sb_k", "sb_v"))
    hd = lambda j: slice(j * HEAD_DIM, (j + 1) * HEAD_DIM)

    def body(q_ref, k_ref, v_ref, do_ref, dq_ref, dk_ref, dv_ref, z_sc, lk_sc, r_sc):
        i = pl.program_id(1)

        @pl.when(i == 0)
        def _():
            dk_ref[...] = jnp.zeros_like(dk_ref)
            dv_ref[...] = jnp.zeros_like(dv_ref)

        qs = [q_ref[:, hd(j)].astype(_MXU) for j in range(hp)]
        dos = [do_ref[:, hd(j)].astype(_MXU) for j in range(hp)]
        upper = _tri(b, "row_gt")
        lower = _tri(b, "row_lt")

        def scores(c, rsums, diag):
            st = pl.multiple_of(c * b, b)
            out = []
            for j in range(hp):
                z = _mm_nt(qs[j], k_ref[pl.ds(st, b), hd(j)]) * scale
                lk = _log_keep(z)
                if diag:
                    lk = jnp.where(_lower_mask(b, True), lk, 0.0)
                z_sc[j, c] = z
                lk_sc[j, c] = lk
                r_sc[j, c] = _mm_split(lk, upper) + rsums[j]
                out.append(rsums[j] + jnp.sum(lk, axis=1, keepdims=True))
            return tuple(out)

        rsums = scores(i, tuple(jnp.zeros((b, 1), f32) for _ in range(hp)), True)
        lax.fori_loop(0, i, lambda t, r: scores(i - 1 - t, r, False), rsums)

        def grads(c, carry, diag):
            st = pl.multiple_of(c * b, b)
            mask = _lower_mask(b, True) if diag else None
            out = []
            for j in range(hp):
                psum, dq_acc = carry[j]
                z, lk = z_sc[j, c], lk_sc[j, c]
                lb = z + lk
                a = lb + r_sc[j, c]
                if diag:
                    a = jnp.where(mask, a, NEG_INF)
                w = jnp.exp(a)
                e = _mm_nt(dos[j], v_ref[pl.ds(st, b), hd(j)]) * w
                before = _mm_split(e, lower) + psum
                dz = e * jnp.exp(lk) - before * jnp.exp(lb)
                if diag:
                    dz = jnp.where(mask, dz, 0.0)
                dk_ref[pl.ds(st, b), hd(j)] += _mm_tn(dz, qs[j]) * scale
                dv_ref[pl.ds(st, b), hd(j)] += _mm_tn(w, dos[j])
                out.append((psum + jnp.sum(e, axis=1, keepdims=True), dq_acc + _mm(dz, k_ref[pl.ds(st, b), hd(j)])))
            return tuple(out)

        init = tuple((jnp.zeros((b, 1), f32), jnp.zeros((b, HEAD_DIM), f32)) for _ in range(hp))
        carry = grads(i, lax.fori_loop(0, i, lambda c, cr: grads(c, cr, False), init), True)
        for j in range(hp):
            dq_ref[:, hd(j)] = carry[j][1] * scale

    w = hp * HEAD_DIM
    blk = pl.BlockSpec((b, w), lambda g, i: (i, g))
    full = pl.BlockSpec((s, w), lambda g, i: (0, g))
    return pl.pallas_call(
        body, out_shape=tuple(_SDS((s, GROUP), f32) for _ in range(3)), grid=(N_HEADS // hp, nq),
        in_specs=[pl.BlockSpec((b, w), lambda g, i: (i, qcol + g)), pl.BlockSpec((s, w), lambda g, i: (0, kcol + g)),
                  pl.BlockSpec((s, w), lambda g, i: (0, vcol + g)), blk],
        out_specs=(blk, full, full),
        scratch_shapes=[pltpu.VMEM((hp, nq, b, b), f32)] * 3,
        compiler_params=_cp(("parallel", "arbitrary")), name=name,
    )(z_all, z_all, z_all, do)


def _split3_left(t, x):
    hi = x.astype(_MXU)
    r1 = x - hi.astype(f32)
    mid = r1.astype(_MXU)
    lo = (r1 - mid.astype(f32)).astype(_MXU)
    dot = functools.partial(jnp.dot, preferred_element_type=f32)
    return dot(t, hi) + dot(t, mid) + dot(t, lo)


def _split3_right(x, t):
    hi = x.astype(_MXU)
    r1 = x - hi.astype(f32)
    mid = r1.astype(_MXU)
    lo = (r1 - mid.astype(f32)).astype(_MXU)
    dot = functools.partial(jnp.dot, preferred_element_type=f32)
    return dot(hi, t) + dot(mid, t) + dot(lo, t)


def _fox_cum_fwd(z_all, bias, *, name):
    s = z_all.shape[0]
    b = _attn_block(s)
    fcol = _AL["fox_f"] // LANES

    def body(f_ref, b_ref, cum_ref, cumt_ref, carry_ref):
        i = pl.program_id(0)

        @pl.when(i == 0)
        def _():
            carry_ref[...] = jnp.zeros_like(carry_ref)

        u = f_ref[...] + b_ref[...]
        lf = jnp.minimum(u, 0.0) - jnp.log1p(jnp.exp(-jnp.abs(u)))
        cum = _split3_left(_tri(b, "row_ge"), lf) + carry_ref[...]
        cum_ref[...] = cum
        cumt_ref[...] = cum.T[0:8, :]
        carry_ref[...] = cum_ref[b - 1:b, :]

    return pl.pallas_call(
        body, out_shape=(_SDS((s, LANES), f32), _SDS((8, s), f32)), grid=(s // b,),
        in_specs=[pl.BlockSpec((b, LANES), lambda i: (i, fcol)), pl.BlockSpec((1, LANES), lambda i: (0, 0))],
        out_specs=(pl.BlockSpec((b, LANES), lambda i: (i, 0)), pl.BlockSpec((8, b), lambda i: (0, i))),
        scratch_shapes=[pltpu.VMEM((1, LANES), f32)], compiler_params=_cp(("arbitrary",)), name=name,
    )(z_all, bias)


def _fox_cum_bwd(z_all, bias, dcum_t, *, name):
    s = z_all.shape[0]
    b = _attn_block(s)
    nb = s // b
    fcol = _AL["fox_f"] // LANES

    def body(f_ref, b_ref, dc_ref, df_ref, db_ref, carry_ref):
        i = pl.program_id(0)

        @pl.when(i == 0)
        def _():
            carry_ref[...] = jnp.zeros_like(carry_ref)
            db_ref[...] = jnp.zeros_like(db_ref)

        dc = dc_ref[...]
        rev = _split3_right(dc, _tri(b, "row_ge")) + carry_ref[...]
        carry_ref[...] = carry_ref[...] + jnp.sum(dc, axis=1, keepdims=True)
        dlf = jnp.concatenate([rev, jnp.zeros((LANES - 8, b), f32)], axis=0).T
        u = f_ref[...] + b_ref[...]
        df = jnp.where(_iota((b, LANES), 1) < N_HEADS, dlf * (1.0 - _sigmoid(u)), 0.0)
        df_ref[...] = df
        db_ref[...] += jnp.sum(df, axis=0, keepdims=True)

    return pl.pallas_call(
        body, out_shape=(_SDS((s, LANES), f32), _SDS((1, LANES), f32)), grid=(nb,),
        in_specs=[pl.BlockSpec((b, LANES), lambda i: (nb - 1 - i, fcol)), pl.BlockSpec((1, LANES), lambda i: (0, 0)),
                  pl.BlockSpec((8, b), lambda i: (0, nb - 1 - i))],
        out_specs=(pl.BlockSpec((b, LANES), lambda i: (nb - 1 - i, 0)), pl.BlockSpec((1, LANES), lambda i: (0, 0))),
        scratch_shapes=[pltpu.VMEM((8, 1), f32)], compiler_params=_cp(("arbitrary",)), name=name,
    )(z_all, bias, dcum_t)


MLA_QW = 2 * LANES


def _rms_rows(x):
    r = lax.rsqrt(jnp.mean(x * x, axis=-1, keepdims=True) + RMS_EPS)
    return x * r, r


def _mla_prep_fwd(z_all, gq, gkv, wuq, wk, wv, tables, *, name):
    s = z_all.shape[0]
    rb = _row_block(s)
    half = MLA_ROPE // 2

    def body(cq_ref, ckv_ref, kr_ref, gq_ref, gkv_ref, wuq_ref, wk_ref, wv_ref, cos_ref, sa_ref, sb_ref,
             q_ref, k_ref, v_ref):
        cos, sa, sb = cos_ref[...], sa_ref[...], sb_ref[...]
        xh, _ = _rms_rows(cq_ref[...])
        qp = _mm(xh * gq_ref[...], wuq_ref[...])
        kh, _ = _rms_rows(ckv_ref[...])
        nkv = kh * gkv_ref[...]
        kn = _mm(nkv, wk_ref[...])
        v_ref[...] = _mm(nkv, wv_ref[...])
        kr = _rope(kr_ref[...], cos, sa, sb, half)
        for h in range(N_HEADS):
            lo, mid, hi = h * MLA_QW, h * MLA_QW + LANES, (h + 1) * MLA_QW
            q_ref[:, lo:mid] = qp[:, lo:mid]
            q_ref[:, mid:hi] = _rope(qp[:, mid:hi], cos, sa, sb, half)
            k_ref[:, lo:mid] = kn[:, h * LANES:(h + 1) * LANES]
            k_ref[:, mid:hi] = kr

    row = lambda w, cb: pl.BlockSpec((rb, w), lambda i: (i, cb))
    whole = lambda a: pl.BlockSpec(a.shape, lambda i: (0,) * a.ndim)
    return pl.pallas_call(
        body, out_shape=(_SDS((s, N_HEADS * MLA_QW), f32), _SDS((s, N_HEADS * MLA_QW), f32), _SDS((s, GROUP), f32)),
        grid=(s // rb,),
        in_specs=[row(MLA_Q_RANK, _AL["mla_cq"] // MLA_Q_RANK), row(LANES, _AL["mla_ckv"] // LANES),
                  row(LANES, _AL["mla_k_rope"] // LANES), whole(gq), whole(gkv), whole(wuq), whole(wk), whole(wv),
                  row(LANES, 0), row(LANES, 0), row(LANES, 0)],
        out_specs=(row(N_HEADS * MLA_QW, 0), row(N_HEADS * MLA_QW, 0), row(GROUP, 0)),
        compiler_params=_cp(("parallel",)), name=name,
    )(z_all, z_all, z_all, gq, gkv, wuq, wk, wv, *tables)


def _mla_prep_bwd(z_all, gq, gkv, wuq, wk, wv, tables, dq_cat, dk_cat, dv, *, name):
    s = z_all.shape[0]
    rb = _row_block(s)
    half = MLA_ROPE // 2

    def body(cq_ref, ckv_ref, gq_ref, gkv_ref, wuq_ref, wk_ref, wv_ref, cos_ref, sa_ref, sb_ref, dq_ref, dk_ref,
             dv_ref, dcq_ref, dckv_ref, dkr_ref, dwuq_ref, dwk_ref, dwv_ref, dgq_ref, dgkv_ref):
        i = pl.program_id(0)

        @pl.when(i == 0)
        def _():
            for r in (dwuq_ref, dwk_ref, dwv_ref, dgq_ref, dgkv_ref):
                r[...] = jnp.zeros_like(r)

        cos, sa, sb = cos_ref[...], sa_ref[...], sb_ref[...]
        parts, knp = [], []
        dkr = jnp.zeros((rb, LANES), f32)
        for h in range(N_HEADS):
            lo, mid, hi = h * MLA_QW, h * MLA_QW + LANES, (h + 1) * MLA_QW
            parts += [dq_ref[:, lo:mid], _rope(dq_ref[:, mid:hi], cos, sa, sb, half, transpose=True)]
            knp.append(dk_ref[:, lo:mid])
            dkr = dkr + _rope(dk_ref[:, mid:hi], cos, sa, sb, half, transpose=True)
        dkr_ref[...] = dkr
        dqp = jnp.concatenate(parts, axis=1)
        dkn = jnp.concatenate(knp, axis=1)
        dvv = dv_ref[...]

        def norm_bwd(x_ref, g_ref, w_pairs, dx_ref, dg_ref):
            xh, r = _rms_rows(x_ref[...])
            nx = xh * g_ref[...]
            dn = jnp.zeros_like(xh)
            for w_ref, dw_ref, dy in w_pairs:
                dw_ref[...] += _mm_tn(nx, dy)
                dn = dn + _mm_nt(dy, w_ref[...])
            dxh = dn * g_ref[...]
            dx_ref[...] = r * (dxh - xh * jnp.mean(dxh * xh, axis=-1, keepdims=True))
            dg_ref[...] += jnp.sum(dn * xh, axis=0, keepdims=True)

        norm_bwd(cq_ref, gq_ref, [(wuq_ref, dwuq_ref, dqp)], dcq_ref, dgq_ref)
        norm_bwd(ckv_ref, gkv_ref, [(wk_ref, dwk_ref, dkn), (wv_ref, dwv_ref, dvv)], dckv_ref, dgkv_ref)

    row = lambda w, cb: pl.BlockSpec((rb, w), lambda i: (i, cb))
    whole = lambda a: pl.BlockSpec(a.shape, lambda i: (0,) * a.ndim)
    return pl.pallas_call(
        body,
        out_shape=(_SDS((s, MLA_Q_RANK), f32), _SDS((s, LANES), f32), _SDS((s, LANES), f32), _SDS(wuq.shape, f32),
                   _SDS(wk.shape, f32), _SDS(wv.shape, f32), _SDS(gq.shape, f32), _SDS(gkv.shape, f32)),
        grid=(s // rb,),
        in_specs=[row(MLA_Q_RANK, _AL["mla_cq"] // MLA_Q_RANK), row(LANES, _AL["mla_ckv"] // LANES), whole(gq),
                  whole(gkv), whole(wuq), whole(wk), whole(wv), row(LANES, 0), row(LANES, 0), row(LANES, 0),
                  row(N_HEADS * MLA_QW, 0), row(N_HEADS * MLA_QW, 0), row(GROUP, 0)],
        out_specs=(row(MLA_Q_RANK, 0), row(LANES, 0), row(LANES, 0), whole(wuq), whole(wk), whole(wv), whole(gq),
                   whole(gkv)),
        compiler_params=_cp(("arbitrary",)), name=name,
    )(z_all, z_all, gq, gkv, wuq, wk, wv, *tables, dq_cat, dk_cat, dv)


def _silu_grad(x):
    sg = _sigmoid(x)
    return sg * (1.0 + x * (1.0 - sg))


def _nsa_cmp_fwd(ra, rb_, pos, w1, w2, tables, *, name):
    nr = ra.shape[1]
    hw = ra.shape[2]

    def body(ra_ref, rb_ref, pos_ref, w1_ref, w2_ref, cos_ref, sa_ref, sb_ref, out_ref, hp_ref):
        for k in range(2):
            xa = ra_ref[k] + pos_ref[k, :, 0:hw]
            xb = rb_ref[k] + pos_ref[k, :, hw:2 * hw]
            hp = _mm(xa, w1_ref[k, 0:hw, :]) + _mm(xb, w1_ref[k, hw:2 * hw, :])
            hp_ref[k] = hp
            out = _mm(hp * _sigmoid(hp), w2_ref[k])
            if k == 0:
                out = _rope(out, cos_ref[...], sa_ref[...], sb_ref[...], HEAD_DIM // 2)
            out_ref[k] = out

    return pl.pallas_call(body, out_shape=(_SDS((2, nr, HEAD_DIM), f32), _SDS((2, nr, HEAD_DIM), f32)),
                          compiler_params=_cp(), name=name)(ra, rb_, pos, w1, w2, *tables)


def _nsa_cmp_bwd(ra, rb_, pos, w1, w2, tables, hp, dout, *, name):
    nr = ra.shape[1]
    hw = ra.shape[2]

    def body(ra_ref, rb_ref, pos_ref, w1_ref, w2_ref, cos_ref, sa_ref, sb_ref, hp_ref, do_ref,
             dxa_ref, dxb_ref, dw1_ref, dw2_ref):
        for k in range(2):
            d_out = do_ref[k]
            if k == 0:
                d_out = _rope(d_out, cos_ref[...], sa_ref[...], sb_ref[...], HEAD_DIM // 2, transpose=True)
            hpv = hp_ref[k]
            dw2_ref[k] = _mm_tn(hpv * _sigmoid(hpv), d_out)
            dhp = _mm_nt(d_out, w2_ref[k]) * _silu_grad(hpv)
            xa = ra_ref[k] + pos_ref[k, :, 0:hw]
            xb = rb_ref[k] + pos_ref[k, :, hw:2 * hw]
            dw1_ref[k, 0:hw, :] = _mm_tn(xa, dhp)
            dw1_ref[k, hw:2 * hw, :] = _mm_tn(xb, dhp)
            dxa_ref[k] = _mm_nt(dhp, w1_ref[k, 0:hw, :])
            dxb_ref[k] = _mm_nt(dhp, w1_ref[k, hw:2 * hw, :])

    return pl.pallas_call(
        body, out_shape=(_SDS((2, nr, hw), f32), _SDS((2, nr, hw), f32), _SDS(w1.shape, f32), _SDS(w2.shape, f32)),
        compiler_params=_cp(), name=name)(ra, rb_, pos, w1, w2, *tables, hp, dout)


def _nsa_consts(s):
    b = _attn_block(s)
    nr = s // CMP_STRIDE
    n_cmp = (s - CMP_LEN) // CMP_STRIDE + 1
    n_sel = s // SEL_LEN
    cmp_start = np.arange(n_cmp) * CMP_STRIDE
    sel_start = np.arange(n_sel) * SEL_LEN
    overlap = np.clip(np.minimum(cmp_start[:, None] + CMP_LEN, sel_start[None, :] + SEL_LEN)
                      - np.maximum(cmp_start[:, None], sel_start[None, :]), 0, None)
    m2s = np.zeros((nr, LANES), np.float32)
    m2s[:n_cmp, :n_sel] = overlap / CMP_LEN
    e3 = np.zeros((s // b, LANES, b), np.float32)
    tok = np.arange(s)
    e3[tok // b, tok // SEL_LEN, tok % b] = 1.0
    return jnp.asarray(m2s, _MXU), jnp.asarray(e3, _MXU)


def _nsa_masks(i, b, d):
    qpos = i * b + _iota((b, b), 0)
    kpos = (i - d) * b + _iota((b, b), 1)
    return (kpos <= qpos) & (kpos > qpos - WINDOW)


def _nsa_fwd(qr, kvc, ksr, vs, kwr, vw, z_all, m2s, e3, *, name):
    s = qr.shape[0]
    b = _attn_block(s)
    nq = s // b
    nr = kvc.shape[1]
    n_sel = s // SEL_LEN
    top_n = min(SEL_TOPN, n_sel)
    nd = -(-WINDOW // b)
    scale = HEAD_DIM ** -0.5
    bcol = _AL["nsa_branch"] // LANES
    H = N_HEADS

    def body(q_ref, kvc_ref, ks_ref, vs_ref, kw_ref, vw_ref, br_ref, m2s_ref, e3_ref,
             o_ref, oc_ref, os_ref, ow_ref, st_ref, sel_ref, m_sc, l_sc, acc_sc):
        i = pl.program_id(0)
        lane = _iota((b, LANES), 1)
        hs = lambda h: slice(h * HEAD_DIM, (h + 1) * HEAD_DIM)

        cmp_mask = (CMP_STRIDE * _iota((b, nr), 1) + (CMP_LEN - 1)) <= (i * b + _iota((b, nr), 0))
        imp = jnp.zeros((b, LANES), f32)
        stats = jnp.zeros((b, LANES), f32)
        for h in range(H):
            zc = jnp.where(cmp_mask, _mm_nt(q_ref[:, hs(h)], kvc_ref[0]) * scale, NEG_INF)
            m = jnp.max(zc, axis=1, keepdims=True)
            p = jnp.where(cmp_mask, jnp.exp(zc - m), 0.0)
            l = jnp.sum(p, axis=1, keepdims=True)
            some = l > 0.0
            lsafe = jnp.where(some, l, 1.0)
            pc = p * jnp.where(some, 1.0 / lsafe, 0.0)
            oc_ref[:, hs(h)] = _mm(pc, kvc_ref[1])
            imp = imp + _mm(pc, m2s_ref[...])
            stats = jnp.where(lane == h, jnp.where(some, m + jnp.log(lsafe), 0.0), stats)

        cur = jnp.right_shift(i * b + _iota((b, LANES), 0), int(math.log2(SEL_LEN)))
        forced = (lane == 0) | (lane == cur) | (lane == cur - 1)
        score = jnp.where(lane <= cur, jnp.where(forced, FORCED_BONUS, imp), NEG_INF)
        score = jnp.where(lane < n_sel, score, -3e38)
        rank = jnp.zeros((b, LANES), f32)
        for j in range(n_sel):
            col = score[:, j:j + 1]
            rank = rank + jnp.where(col > score, 1.0, jnp.where(col == score, jnp.where(lane > j, 1.0, 0.0), 0.0))
        sel = jnp.where(lane < n_sel, jnp.where(rank < top_n, 1.0, 0.0), 0.0)
        sel_ref[...] = sel
        sel_b = sel.astype(_MXU)

        def reset():
            m_sc[...] = jnp.full(m_sc.shape, NEG_INF, f32)
            l_sc[...] = jnp.zeros_like(l_sc)
            acc_sc[...] = jnp.zeros_like(acc_sc)

        def update(h, z, mask, vch):
            zm = jnp.where(mask, z, NEG_INF)
            m_old = m_sc[h]
            m_new = jnp.maximum(m_old, jnp.max(zm, axis=1, keepdims=True))
            p = jnp.where(mask, jnp.exp(zm - m_new), 0.0)
            alpha = jnp.exp(m_old - m_new)
            l_sc[h] = alpha * l_sc[h] + jnp.sum(p, axis=1, keepdims=True)
            acc_sc[h] = alpha * acc_sc[h] + _mm(p, vch)
            m_sc[h] = m_new

        def finish(out_ref, branch, stats):
            for h in range(H):
                out_ref[:, hs(h)] = acc_sc[h] / l_sc[h]
                stats = jnp.where(lane == 4 * branch + h, m_sc[h] + jnp.log(l_sc[h]), stats)
            return stats

        def sel_chunk(c, diag):
            st = pl.multiple_of(c * b, b)
            mask = _mm(sel_b, e3_ref[c]) > 0.5
            if diag:
                mask = mask & _lower_mask(b, False)
            kch, vch = ks_ref[pl.ds(st, b), :], vs_ref[pl.ds(st, b), :]
            for h in range(H):
                update(h, _mm_nt(q_ref[:, hs(h)], kch) * scale, mask, vch)

        reset()

        def sel_loop(c, carry):
            sel_chunk(c, False)
            return carry

        lax.fori_loop(0, i, sel_loop, 0)
        sel_chunk(i, True)
        stats = finish(os_ref, 1, stats)

        reset()
        for d in range(nd, -1, -1):
            @pl.when(i >= d)
            def _():
                st = pl.multiple_of((i - d) * b, b)
                mask = _nsa_masks(i, b, d)
                kch, vch = kw_ref[pl.ds(st, b), :], vw_ref[pl.ds(st, b), :]
                for h in range(H):
                    update(h, _mm_nt(q_ref[:, hs(h)], kch) * scale, mask, vch)
        stats = finish(ow_ref, 2, stats)
        st_ref[...] = stats

        g = _sigmoid(br_ref[...])
        for h in range(H):
            o_ref[:, hs(h)] = (g[:, 3 * h:3 * h + 1] * oc_ref[:, hs(h)] + g[:, 3 * h + 1:3 * h + 2] * os_ref[:, hs(h)]
                               + g[:, 3 * h + 2:3 * h + 3] * ow_ref[:, hs(h)])

    blk = lambda w: pl.BlockSpec((b, w), lambda i: (i, 0))
    whole = lambda a: pl.BlockSpec(a.shape, lambda i: (0,) * a.ndim)
    return pl.pallas_call(
        body, out_shape=tuple(_SDS((s, GROUP), f32) for _ in range(4)) + (_SDS((s, LANES), f32), _SDS((s, LANES), f32)),
        grid=(nq,),
        in_specs=[blk(GROUP), whole(kvc), whole(ksr), whole(vs), whole(kwr), whole(vw),
                  pl.BlockSpec((b, LANES), lambda i: (i, bcol)), whole(m2s), whole(e3)],
        out_specs=(blk(GROUP),) * 4 + (blk(LANES), blk(LANES)),
        scratch_shapes=[pltpu.VMEM((H, b, 1), f32), pltpu.VMEM((H, b, 1), f32), pltpu.VMEM((H, b, HEAD_DIM), f32)],
        compiler_params=_cp(("parallel",)), name=name,
    )(qr, kvc, ksr, vs, kwr, vw, z_all, m2s, e3)


def _nsa_bwd(do, qr, kvc, ksr, vs, kwr, vw, z_all, oc, os_, ow, stats, sel, e3, *, name):
    s = qr.shape[0]
    b = _attn_block(s)
    nq = s // b
    nr = kvc.shape[1]
    nd = -(-WINDOW // b)
    scale = HEAD_DIM ** -0.5
    bcol = _AL["nsa_branch"] // LANES
    H = N_HEADS

    def body(do_ref, q_ref, kvc_ref, ks_ref, vs_ref, kw_ref, vw_ref, br_ref, oc_ref, os_ref, ow_ref, st_ref, sel_ref,
             e3_ref, dq_ref, dbr_ref, dkvc_ref, dks_ref, dvs_ref, dkw_ref, dvw_ref, dob_sc, delta_sc, dq_sc):
        i = pl.program_id(0)

        @pl.when(i == 0)
        def _():
            for r in (dkvc_ref, dks_ref, dvs_ref, dkw_ref, dvw_ref):
                r[...] = jnp.zeros_like(r)

        lane = _iota((b, LANES), 1)
        hs = lambda h: slice(h * HEAD_DIM, (h + 1) * HEAD_DIM)
        g = _sigmoid(br_ref[...])
        stats = st_ref[...]
        dbr = jnp.zeros((b, LANES), f32)
        outs = (oc_ref, os_ref, ow_ref)
        for h in range(H):
            doh = do_ref[:, hs(h)]
            for j in range(3):
                gj = g[:, 3 * h + j:3 * h + j + 1]
                dgj = jnp.sum(doh * outs[j][:, hs(h)], axis=1, keepdims=True)
                dbr = jnp.where(lane == 3 * h + j, dgj * gj * (1.0 - gj), dbr)
                dob_sc[j, :, hs(h)] = gj * doh
                delta_sc[j, h] = gj * dgj
        dbr_ref[...] = dbr
        dq_sc[...] = jnp.zeros_like(dq_sc)

        def branch(j, h, z, mask, kch, vch):
            qh = q_ref[:, hs(h)]
            p = jnp.where(mask, jnp.exp(jnp.where(mask, z, NEG_INF) - stats[:, 4 * j + h:4 * j + h + 1]), 0.0)
            dob = dob_sc[j, :, hs(h)]
            ds = p * (_mm_nt(dob, vch) - delta_sc[j, h])
            dq_sc[:, hs(h)] += _mm(ds, kch) * scale
            return _mm_tn(ds, qh) * scale, _mm_tn(p, dob)

        cmp_mask = (CMP_STRIDE * _iota((b, nr), 1) + (CMP_LEN - 1)) <= (i * b + _iota((b, nr), 0))
        kc, vc = kvc_ref[0], kvc_ref[1]
        for h in range(H):
            dk, dv = branch(0, h, _mm_nt(q_ref[:, hs(h)], kc) * scale, cmp_mask, kc, vc)
            dkvc_ref[0] += dk
            dkvc_ref[1] += dv

        sel_b = sel_ref[...].astype(_MXU)

        def chunk(j, c, mask, k_ref, v_ref, dk_ref, dv_ref):
            st = pl.multiple_of(c * b, b)
            kch, vch = k_ref[pl.ds(st, b), :], v_ref[pl.ds(st, b), :]
            dk = jnp.zeros((b, HEAD_DIM), f32)
            dv = jnp.zeros((b, HEAD_DIM), f32)
            for h in range(H):
                dkh, dvh = branch(j, h, _mm_nt(q_ref[:, hs(h)], kch) * scale, mask, kch, vch)
                dk, dv = dk + dkh, dv + dvh
            dk_ref[pl.ds(st, b), :] += dk
            dv_ref[pl.ds(st, b), :] += dv

        def sel_chunk(c, diag):
            mask = _mm(sel_b, e3_ref[c]) > 0.5
            if diag:
                mask = mask & _lower_mask(b, False)
            chunk(1, c, mask, ks_ref, vs_ref, dks_ref, dvs_ref)

        def sel_loop(c, carry):
            sel_chunk(c, False)
            return carry

        lax.fori_loop(0, i, sel_loop, 0)
        sel_chunk(i, True)

        for d in range(nd, -1, -1):
            @pl.when(i >= d)
            def _():
                chunk(2, i - d, _nsa_masks(i, b, d), kw_ref, vw_ref, dkw_ref, dvw_ref)

        dq_ref[...] = dq_sc[...]

    blk = lambda w: pl.BlockSpec((b, w), lambda i: (i, 0))
    whole = lambda a: pl.BlockSpec(a.shape, lambda i: (0,) * a.ndim)
    stream = _SDS((s, HEAD_DIM), f32)
    return pl.pallas_call(
        body, out_shape=(_SDS((s, GROUP), f32), _SDS((s, LANES), f32), _SDS(kvc.shape, f32), stream, stream, stream,
                         stream),
        grid=(nq,),
        in_specs=[blk(GROUP), blk(GROUP), whole(kvc), whole(ksr), whole(vs), whole(kwr), whole(vw),
                  pl.BlockSpec((b, LANES), lambda i: (i, bcol)), blk(GROUP), blk(GROUP), blk(GROUP), blk(LANES),
                  blk(LANES), whole(e3)],
        out_specs=(blk(GROUP), blk(LANES), whole(kvc), whole(ksr), whole(vs), whole(kwr), whole(vw)),
        scratch_shapes=[pltpu.VMEM((3, b, GROUP), f32), pltpu.VMEM((3, H, b, 1), f32), pltpu.VMEM((b, GROUP), f32)],
        compiler_params=_cp(("arbitrary",)), name=name,
    )(do, qr, kvc, ksr, vs, kwr, vw, z_all, oc, os_, ow, stats, sel, e3)


def _seg(a, name):
    parts = [lax.slice_in_dim(a, off, off + hi - lo, axis=a.ndim - 1) for off, lo, hi in _PIECES[name]]
    return parts[0] if len(parts) == 1 else jnp.concatenate(parts, axis=a.ndim - 1)


def _to_groups(segs, rows, dtype):
    cols = []
    for s, grp in enumerate(_GROUPS):
        at = 0
        for n, lo, hi, off in sorted(grp, key=lambda t: t[3]):
            if off > at:
                cols.append(jnp.zeros((rows, off - at), dtype))
            cols.append(segs[n][:, lo:hi].astype(dtype))
            at = off + hi - lo
        if at < GROUP_W:
            cols.append(jnp.zeros((rows, GROUP_W - at), dtype))
    return jnp.concatenate(cols, axis=1)


def _piece_from_shard(w_t, s):
    grp = sorted(_GROUPS[s], key=lambda t: t[3])
    ends = [t[3] for t in grp[1:]] + [GROUP_W]
    rows = []
    for (n, lo, hi, off), end in zip(grp, ends):
        first = _ORIG[n] + lo - s * CHIP_COLS
        rows.append(jnp.pad(w_t[:, first:first + hi - lo], ((0, 0), (0, end - off - (hi - lo)), (0, 0))))
    return jnp.concatenate(rows, axis=1)


def _shard_from_piece(g, s):
    return jnp.concatenate([g[:, off:off + hi - lo] for n, lo, hi, off in
                            sorted(_GROUPS[s], key=lambda t: _ORIG[t[0]] + t[1])], axis=1)


def _from_groups(a):
    return jnp.concatenate([_seg(a, n) for n, _ in _SEGS], axis=1)


def _cmp_rows(tok):
    s = tok.shape[0]
    r = tok.reshape(s // CMP_STRIDE, CMP_STRIDE * HEAD_DIM)
    return r, jnp.concatenate([r[1:], jnp.zeros((1, r.shape[1]), r.dtype)], axis=0)


def _cmp_unrows(dxa, dxb):
    s = dxa.shape[0] * CMP_STRIDE
    return (dxa + jnp.concatenate([jnp.zeros((1, dxa.shape[1]), dxa.dtype), dxb[:-1]], axis=0)).reshape(s, HEAD_DIM)


_GATES = ("sb_gate", "nsa_gate", "fox_gate", "mla_gate")


def _layer_fwd(x, p, c, tag):
    s = x.shape[0]
    b = _attn_block(s)
    h = _rms_fwd(x, p["pre_g"], out_dtype=_MXU, name=f"prenorm_{tag}")
    z = _matmul(h, p["w_in"], "nt", bias=p["b_in"], name=f"inproj_{tag}")
    o_sb = _sb_fwd(z, hp=HP_FWD, name=f"sb_fwd_{tag}")

    qr, ksr, kwr = _rope_call([(z, GROUP, _AL["nsa_q"] // GROUP), (z, LANES, _AL["nsa_k_sel"] // LANES),
                               (z, LANES, _AL["nsa_k_win"] // LANES)], c["tabs128"], HEAD_DIM // 2, False,
                              name=f"nsa_rope_{tag}")
    (rak, rbk), (rav, rbv) = _cmp_rows(_seg(z, "nsa_k_cmp")), _cmp_rows(_seg(z, "nsa_v_cmp"))
    ra, rb_ = jnp.stack([rak, rav]), jnp.stack([rbk, rbv])
    kvc, hp = _nsa_cmp_fwd(ra, rb_, p["cmp_pos"], p["cmp_w1"], p["cmp_w2"], c["tabs_cmp"], name=f"nsa_cmp_{tag}")
    vs, vw = _seg(z, "nsa_v_sel"), _seg(z, "nsa_v_win")
    o_nsa, oc, os_, ow, stats, sel = _nsa_fwd(qr, kvc, ksr, vs, kwr, vw, z, c["m2s"], c["e3"], name=f"nsa_fwd_{tag}")

    cum, cum_t8 = _fox_cum_fwd(z, p["fox_bias"], name=f"fox_cum_{tag}")
    cum_t = cum_t8.reshape(8, s // b, 1, b)
    fox_v = _seg(z, "fox_v")
    fcols = (_AL["fox_q"] // HEAD_DIM, _AL["fox_k"] // HEAD_DIM, 0)
    o_fox, lse_fox = _attn_fwd(z, z, fox_v, *fcols, HEAD_DIM, cum, cum_t, scale=HEAD_DIM ** -0.5, hp=HP_FWD,
                               name=f"fox_fwd_{tag}")

    qcat, kcat, vm = _mla_prep_fwd(z, p["gq"], p["gkv"], p["wuq"], p["wk"], p["wv"], c["tabs64"],
                                   name=f"mla_prep_{tag}")
    o_mla, lse_mla = _attn_fwd(qcat, kcat, vm, 0, 0, 0, MLA_QW, None, None, scale=(MLA_NOPE + MLA_ROPE) ** -0.5,
                               hp=HP_BWD, name=f"mla_fwd_{tag}")

    o_all = (o_sb, o_nsa, o_fox, o_mla)
    gates = jnp.concatenate([_seg(z, n) for n in _GATES], axis=1)
    mix = _gate_fwd(o_all, gates, name=f"gate_{tag}")
    u = _matmul(mix, p["w_out"], "nn", name=f"outproj_{tag}")
    y = _postnorm_fwd(u, p["post_g"], x, name=f"postnorm_{tag}")
    saved = dict(x=x, h=h, z=z, qr=qr, ksr=ksr, kwr=kwr, ra=ra, rb=rb_, kvc=kvc, hp=hp, vs=vs, vw=vw, oc=oc, os=os_,
                 ow=ow, stats=stats, sel=sel, cum=cum, cum_t=cum_t, fox_v=fox_v, o_fox=o_fox, lse_fox=lse_fox, qcat=qcat, kcat=kcat,
                 vm=vm, o_mla=o_mla, lse_mla=lse_mla, o_all=o_all, gates=gates, mix=mix, u=u)
    return y, saved


def _layer_bwd(dy, sv, p, c, tag, dw_dtype=f32):
    z = sv["z"]
    s = z.shape[0]
    du, dg_post = _rms_bwd(dy, sv["u"], p["post_g"], name=f"postnorm_bwd_{tag}")
    dmix = _matmul(du, p["w_out"], "nt", name=f"outproj_dx_{tag}")
    dw_out = _matmul(sv["mix"], du, "tn", name=f"outproj_dw_{tag}")
    do_sb, do_nsa, do_fox, do_mla, dgates = _gate_bwd(dmix, sv["o_all"], sv["gates"], name=f"gate_bwd_{tag}")
    dgate = [dgates[:, k * GROUP:(k + 1) * GROUP] for k in range(4)]

    sb_dq, sb_dk, sb_dv = _sb_bwd(z, do_sb, hp=HP_BWD, name=f"sb_bwd_{tag}")

    n_dq, n_dbr, n_dkvc, n_dks, n_dvs, n_dkw, n_dvw = _nsa_bwd(
        do_nsa, sv["qr"], sv["kvc"], sv["ksr"], sv["vs"], sv["kwr"], sv["vw"], z, sv["oc"], sv["os"], sv["ow"],
        sv["stats"], sv["sel"], c["e3"], name=f"nsa_bwd_{tag}")
    dxa, dxb, dw1, dw2 = _nsa_cmp_bwd(sv["ra"], sv["rb"], p["cmp_pos"], p["cmp_w1"], p["cmp_w2"], c["tabs_cmp"],
                                      sv["hp"], n_dkvc, name=f"nsa_cmp_bwd_{tag}")
    n_dq, n_dks, n_dkw = _rope_call([(n_dq, GROUP, 0), (n_dks, LANES, 0), (n_dkw, LANES, 0)], c["tabs128"],
                                    HEAD_DIM // 2, True, name=f"nsa_rope_bwd_{tag}")
    dpos = _colsum(jnp.concatenate([dxa[0], dxb[0], dxa[1], dxb[1]], axis=1), name=f"nsa_dpos_{tag}")
    flat = CMP_LEN * HEAD_DIM

    fcols = (_AL["fox_q"] // HEAD_DIM, _AL["fox_k"] // HEAD_DIM, 0)
    f_dq, f_dk, f_dv, f_dck = _attn_bwd(z, z, sv["fox_v"], *fcols, HEAD_DIM, do_fox, sv["o_fox"], sv["lse_fox"],
                                        sv["cum"], sv["cum_t"], scale=HEAD_DIM ** -0.5, hp=HP_BWD,
                                        name=f"fox_bwd_{tag}")
    dcum_t = jnp.pad(f_dck.reshape(N_HEADS, s), ((0, 8 - N_HEADS), (0, 0)))
    f_df, f_dbias = _fox_cum_bwd(z, p["fox_bias"], dcum_t, name=f"fox_cum_bwd_{tag}")

    m_dq, m_dk, m_dv = _attn_bwd(sv["qcat"], sv["kcat"], sv["vm"], 0, 0, 0, MLA_QW, do_mla, sv["o_mla"], sv["lse_mla"],
                                 None, None, scale=(MLA_NOPE + MLA_ROPE) ** -0.5, hp=HP_BWD, name=f"mla_bwd_{tag}")
    m_dcq, m_dckv, m_dkr, m_dwuq, m_dwk, m_dwv, m_dgq, m_dgkv = _mla_prep_bwd(
        z, p["gq"], p["gkv"], p["wuq"], p["wk"], p["wv"], c["tabs64"], m_dq, m_dk, m_dv, name=f"mla_prep_bwd_{tag}")

    dz = _to_groups(dict(
        sb_q=sb_dq, sb_k=sb_dk, sb_v=sb_dv, sb_gate=dgate[0], nsa_q=n_dq, nsa_k_cmp=_cmp_unrows(dxa[0], dxb[0]),
        nsa_v_cmp=_cmp_unrows(dxa[1], dxb[1]), nsa_k_sel=n_dks, nsa_v_sel=n_dvs, nsa_k_win=n_dkw, nsa_v_win=n_dvw,
        nsa_branch=n_dbr, nsa_gate=dgate[1], fox_q=f_dq, fox_k=f_dk, fox_v=f_dv, fox_f=f_df, fox_gate=dgate[2],
        mla_cq=m_dcq, mla_ckv=m_dckv, mla_k_rope=m_dkr, mla_gate=dgate[3]), s, _MXU)
    dh = _matmul(dz, p["w_in"], "nn", name=f"inproj_dx_{tag}")
    dw_in = _matmul(dz, sv["h"], "tn", out_dtype=dw_dtype, name=f"inproj_dw_{tag}")
    db = _colsum(dz, name=f"inproj_db_{tag}")
    dx, dg_pre = _rms_bwd(dh, sv["x"], p["pre_g"], res=dy, name=f"prenorm_bwd_{tag}")

    qw = MLA_NOPE + MLA_ROPE
    grads = {
        "pre_norm_g": dg_pre[0], "post_norm_g": dg_post[0], "w_in": dw_in, "b_in": _from_groups(db)[0],
        "w_out": dw_out, "fox_forget_bias": f_dbias[0, :N_HEADS],
        "nsa_cmp_pos_k": dpos[0, :flat].reshape(CMP_LEN, HEAD_DIM), "nsa_cmp_w1_k": dw1[0], "nsa_cmp_w2_k": dw2[0],
        "nsa_cmp_pos_v": dpos[0, flat:].reshape(CMP_LEN, HEAD_DIM), "nsa_cmp_w1_v": dw1[1], "nsa_cmp_w2_v": dw2[1],
        "mla_q_norm_g": m_dgq[0],
        "mla_w_uq": jnp.concatenate([m_dwuq[:, MLA_QW * h:MLA_QW * h + qw] for h in range(N_HEADS)], axis=1),
        "mla_kv_norm_g": m_dgkv[0],
        "mla_w_ukv": jnp.concatenate(sum([[m_dwk[:, LANES * h:LANES * (h + 1)], m_dwv[:, LANES * h:LANES * (h + 1)]]
                                          for h in range(N_HEADS)], []), axis=1),
    }
    return dx, grads


def _layer_params(w, l):
    b_in = w["b_in"][l].reshape(1, -1)
    b_segs = {n: b_in[:, _ORIG[n]:_ORIG[n] + wd] for n, wd in _SEGS}
    qw = MLA_NOPE + MLA_ROPE
    w_uq, w_ukv = w["mla_w_uq"][l], w["mla_w_ukv"][l]
    uq = []
    for h in range(N_HEADS):
        uq += [w_uq[:, qw * h:qw * (h + 1)], jnp.zeros((w_uq.shape[0], MLA_QW - qw), w_uq.dtype)]
    kw_ = 2 * LANES
    flat = CMP_LEN * HEAD_DIM
    return dict(
        pre_g=w["pre_norm_g"][l].reshape(1, -1), post_g=w["post_norm_g"][l].reshape(1, -1),
        w_in=w["w_in"][l], b_in=_to_groups(b_segs, 1, f32), w_out=w["w_out"][l],
        fox_bias=jnp.pad(w["fox_forget_bias"][l], (0, LANES - N_HEADS)).reshape(1, LANES),
        cmp_pos=jnp.stack([w["nsa_cmp_pos_k"][l].reshape(1, flat), w["nsa_cmp_pos_v"][l].reshape(1, flat)]),
        cmp_w1=jnp.stack([w["nsa_cmp_w1_k"][l], w["nsa_cmp_w1_v"][l]]),
        cmp_w2=jnp.stack([w["nsa_cmp_w2_k"][l], w["nsa_cmp_w2_v"][l]]),
        gq=w["mla_q_norm_g"][l].reshape(1, -1), gkv=w["mla_kv_norm_g"][l].reshape(1, -1),
        wuq=jnp.concatenate(uq, axis=1),
        wk=jnp.concatenate([w_ukv[:, kw_ * h:kw_ * h + LANES] for h in range(N_HEADS)], axis=1),
        wv=jnp.concatenate([w_ukv[:, kw_ * h + LANES:kw_ * (h + 1)] for h in range(N_HEADS)], axis=1),
    )


def _consts(s):
    pos = jnp.arange(s)
    m2s, e3 = _nsa_consts(s)
    return dict(tabs128=_rope_tables(pos, HEAD_DIM), tabs64=_rope_tables(pos, MLA_ROPE),
                tabs_cmp=_rope_tables(jnp.arange(s // CMP_STRIDE) * CMP_STRIDE + (CMP_LEN - 1), HEAD_DIM),
                m2s=m2s, e3=e3)


def _place():
    return lax.axis_index("x"), lax.axis_index("y"), lax.axis_index("c")


def _other_chips(x, y):
    return [(1 - x, y), (x, 1 - y), (1 - x, 1 - y)]


def _comm_call(body, out_shapes, n_sems, arrs, name):
    return pl.pallas_call(body, out_shape=tuple(out_shapes), in_specs=[_ANY] * len(arrs),
                          out_specs=tuple(_ANY for _ in out_shapes),
                          scratch_shapes=[pltpu.SemaphoreType.DMA((n_sems,)), pltpu.SemaphoreType.DMA((n_sems,))],
                          name=name)(*arrs)


def _gather_chips(arrs, *, name):
    n = len(arrs)

    def body(*refs):
        a_refs, out_refs, send_sems, recv_sems = refs[:n], refs[n:2 * n], refs[2 * n], refs[2 * n + 1]
        x, y, c = _place()
        me = 2 * x + y
        sibling = (x, y, 1 - c)
        chips = _other_chips(x, y)

        def copy(j, k, src, dst, to):
            return pltpu.make_async_remote_copy(src, dst, send_sems.at[6 * j + k], recv_sems.at[6 * j + k],
                                                device_id=to, device_id_type=_MESH)

        first = [copy(j, k, a_refs[j].at[c], out_refs[j].at[me, c], (px, py, c))
                 for k, (px, py) in enumerate(chips) for j in range(n)]
        for cp in first:
            cp.start()
        passed = []
        for k, (px, py) in enumerate(chips):
            for j in range(n):
                landed = out_refs[j].at[2 * px + py, c]
                copy(j, k, a_refs[j].at[c], landed, (px, py, c)).wait_recv()
                passed.append(copy(j, 3 + k, landed, landed, sibling))
                passed[-1].start()
        for k, (px, py) in enumerate(chips):
            for j in range(n):
                copy(j, 3 + k, a_refs[j].at[c], out_refs[j].at[2 * px + py, 1 - c], sibling).wait_recv()
        for cp in first + passed:
            cp.wait_send()

    return _comm_call(body, [_SDS((N_CHIPS,) + a.shape, a.dtype) for a in arrs], 6 * n, arrs, name)


def _alltoall_chips(arrs, modes, *, name):
    n = len(arrs)
    slot = lambda ref, mode, s: _slot_ref(ref, mode, s)
    lane_slots = modes

    def body(*refs):
        g_refs, out_refs, send_sems, recv_sems = refs[:n], refs[n:2 * n], refs[2 * n], refs[2 * n + 1]
        x, y, c = _place()
        me = 2 * x + y

        def copy(j, s):
            return pltpu.make_async_remote_copy(slot(g_refs[j], lane_slots[j], s), out_refs[j].at[me],
                                                send_sems.at[N_CHIPS * j + s], recv_sems.at[N_CHIPS * j + me],
                                                device_id=(s // 2, s % 2, c), device_id_type=_MESH)

        for s in range(N_CHIPS):
            @pl.when(s != me)
            def _():
                for j in range(n):
                    copy(j, s).start()
        for t in range(N_CHIPS):
            @pl.when(t != me)
            def _():
                for j in range(n):
                    pltpu.make_async_remote_copy(slot(g_refs[j], lane_slots[j], t), out_refs[j].at[t],
                                                 send_sems.at[N_CHIPS * j + t], recv_sems.at[N_CHIPS * j + t],
                                                 device_id=(t // 2, t % 2, c), device_id_type=_MESH).wait_recv()
        for s in range(N_CHIPS):
            @pl.when(s != me)
            def _():
                for j in range(n):
                    copy(j, s).wait_send()

    outs = [_SDS((N_CHIPS,) + _slot_shape(a, m), a.dtype) for a, m in zip(arrs, modes)]
    return _comm_call(body, outs, N_CHIPS * n, arrs, name)


def _swap_other_half(arrs, *, name):
    n = len(arrs)

    def body(*refs):
        g_refs, out_refs, send_sems, recv_sems = refs[:n], refs[n:2 * n], refs[2 * n], refs[2 * n + 1]
        x, y, c = _place()
        cps = [pltpu.make_async_remote_copy(g_refs[j].at[:, 1 - c], out_refs[j], send_sems.at[j], recv_sems.at[j],
                                            device_id=(x, y, 1 - c), device_id_type=_MESH) for j in range(n)]
        for cp in cps:
            cp.start()
        for cp in cps:
            cp.wait()

    return _comm_call(body, [_SDS((a.shape[0],) + a.shape[2:], a.dtype) for a in arrs], n, arrs, name)


def _swap_sibling(arrs, *, name):
    n = len(arrs)

    def body(*refs):
        f_refs, out_refs, send_sems, recv_sems = refs[:n], refs[n:2 * n], refs[2 * n], refs[2 * n + 1]
        x, y, c = _place()
        cps = [pltpu.make_async_remote_copy(f_refs[j], out_refs[j], send_sems.at[j], recv_sems.at[j],
                                            device_id=(x, y, 1 - c), device_id_type=_MESH) for j in range(n)]
        for cp in cps:
            cp.start()
        for cp in cps:
            cp.wait()

    return _comm_call(body, [_SDS(a.shape, a.dtype) for a in arrs], n, arrs, name)


_HBM = pl.BlockSpec(memory_space=pltpu.HBM)
_SEM = pl.BlockSpec(memory_space=pltpu.SEMAPHORE)
_EFFECT = pltpu.SideEffectType.DATAFLOW_SIDE_EFFECTING


def _slot_ref(ref, mode, s):
    return ref if mode == "same" else ref.at[s]


def _slot_shape(a, mode):
    return a.shape if mode == "same" else a.shape[1:]


def _send_start(arrs, modes, after, *, name):
    n = len(arrs)
    lands = [lax.empty((N_CHIPS,) + _slot_shape(a, m), a.dtype) for a, m in zip(arrs, modes)]

    def body(*refs):
        srcs, land_refs, send_sems, recv_sems, token = refs[:n], refs[n:2 * n], refs[2 * n + 1], refs[2 * n + 2], refs[-1]
        x, y, c = _place()
        me = 2 * x + y
        for s in range(N_CHIPS):
            @pl.when(s != me)
            def _():
                for j in range(n):
                    pltpu.make_async_remote_copy(_slot_ref(srcs[j], modes[j], s), land_refs[j].at[me],
                                                 send_sems.at[N_CHIPS * j + s], recv_sems.at[N_CHIPS * j + me],
                                                 device_id=(s // 2, s % 2, c), device_id_type=_MESH).start()
        token[...] = jnp.zeros_like(token)

    hbm = lambda a: pltpu.HBM(a.shape, a.dtype)
    sems = pltpu.SemaphoreType.DMA((N_CHIPS * n,))
    out = pl.pallas_call(
        body, name=name, out_shape=(sems, sems, *[hbm(a) for a in arrs], *[hbm(a) for a in lands], _SDS((8, LANES), f32)),
        in_specs=[_HBM] * (2 * n) + [_ANY], out_specs=(_SEM, _SEM, *[_HBM] * (2 * n), pl.BlockSpec(memory_space=pltpu.VMEM)),
        input_output_aliases={j: 2 + j for j in range(2 * n)},
        compiler_params=pltpu.CompilerParams(has_side_effects=_EFFECT),
    )(*[pltpu.with_memory_space_constraint(a, pltpu.HBM) for a in arrs + lands], after)
    return out[:-1], out[-1]


def _send_wait(started, modes, after, *, name):
    send_sems, recv_sems = started[0], started[1]
    n = (len(started) - 2) // 2
    thru = list(started[2:])

    def body(*refs):
        srcs, land_refs, send_sems, recv_sems = refs[:n], refs[n:2 * n], refs[2 * n], refs[2 * n + 1]
        x, y, c = _place()
        me = 2 * x + y
        for s in range(N_CHIPS):
            @pl.when(s != me)
            def _():
                for j in range(n):
                    cp = pltpu.make_async_remote_copy(_slot_ref(srcs[j], modes[j], s), land_refs[j].at[s],
                                                      send_sems.at[N_CHIPS * j + s], recv_sems.at[N_CHIPS * j + s],
                                                      device_id=(s // 2, s % 2, c), device_id_type=_MESH)
                    cp.wait_send()
                    cp.wait_recv()

    hbm = lambda a: pltpu.HBM(a.shape, a.dtype)
    out = pl.pallas_call(
        body, name=name, out_shape=tuple(hbm(a) for a in thru), in_specs=[_HBM] * (2 * n) + [_SEM, _SEM, _ANY],
        out_specs=tuple([_HBM] * (2 * n)), input_output_aliases={j: j for j in range(2 * n)},
        compiler_params=pltpu.CompilerParams(has_side_effects=_EFFECT),
    )(*thru, send_sems, recv_sems, after)
    return list(out[n:])


def _gather_all(a, *, name):
    def body(a_ref, out_ref, send_sems, recv_sems, local_sem):
        x, y, c = _place()
        flip = lambda v, f: (1 - v) if f else v
        peers = [(flip(x, f & 4), flip(y, f & 2), flip(c, f & 1)) for f in range(1, 8)]
        me = 4 * x + 2 * y + c
        mine = pltpu.make_async_copy(a_ref, out_ref.at[me], local_sem)
        mine.start()
        sends = [pltpu.make_async_remote_copy(a_ref, out_ref.at[me], send_sems.at[k], recv_sems.at[k], device_id=peer,
                                              device_id_type=_MESH) for k, peer in enumerate(peers)]
        for cp in sends:
            cp.start()
        for k, (px, py, pc) in enumerate(peers):
            pltpu.make_async_remote_copy(a_ref, out_ref.at[4 * px + 2 * py + pc], send_sems.at[k], recv_sems.at[k],
                                         device_id=(px, py, pc), device_id_type=_MESH).wait_recv()
        for cp in sends:
            cp.wait_send()
        mine.wait()

    return pl.pallas_call(body, out_shape=_SDS((8,) + a.shape, a.dtype), in_specs=[_ANY], out_specs=_ANY,
                          scratch_shapes=[pltpu.SemaphoreType.DMA((7,)), pltpu.SemaphoreType.DMA((7,)),
                                          pltpu.SemaphoreType.DMA], name=name)(a)


def _add_my_half(g, r, *, name):
    p, _, h, w = g.shape
    tw = _pick(w, (2048, 1024, 512, 256, 128))
    rb = max(d for d in range(16, h + 1, 16) if h % d == 0 and d * tw * 4 <= (2 << 20))

    def body(c_ref, g_ref, r_ref, o_ref):
        o_ref[...] = (g_ref[...].astype(f32) + r_ref[...].astype(f32)).astype(o_ref.dtype)

    blk = pl.BlockSpec((None, rb, tw), lambda s, i, j, c_ref: (s, i, j))
    grid_spec = pltpu.PrefetchScalarGridSpec(
        num_scalar_prefetch=1, grid=(p, h // rb, w // tw),
        in_specs=[pl.BlockSpec((None, None, rb, tw), lambda s, i, j, c_ref: (s, c_ref[0], i, j)), blk], out_specs=blk)
    c = lax.axis_index("c").astype(jnp.int32).reshape(1)
    return pl.pallas_call(body, out_shape=_SDS((p, h, w), _WIRE), grid_spec=grid_spec,
                          compiler_params=_cp(("parallel", "parallel", "parallel")), name=name)(c, g, r)


_WEIGHTS = ("pre_norm_g", "post_norm_g", "w_in", "b_in", "w_out", "fox_forget_bias", "nsa_cmp_pos_k", "nsa_cmp_w1_k",
            "nsa_cmp_w2_k", "nsa_cmp_pos_v", "nsa_cmp_w1_v", "nsa_cmp_w2_v", "mla_q_norm_g", "mla_w_uq",
            "mla_kv_norm_g", "mla_w_ukv")
_SHARD_AXIS = {"w_in": 2, "w_out": 1, "nsa_cmp_w1_k": 1, "nsa_cmp_w1_v": 1, "mla_w_uq": 2, "mla_w_ukv": 2}
_PACK_UNIT = 16 * LANES


def _pack(arrays, dtype):
    rows = []
    for a in arrays:
        v = a.astype(dtype).reshape(-1)
        pad = (-v.shape[0]) % _PACK_UNIT
        if pad:
            v = jnp.concatenate([v, jnp.zeros((pad,), dtype)])
        rows.append(v.reshape(-1, LANES))
    return jnp.concatenate(rows, axis=0)


def _unpack(flat, shapes):
    out, r = [], 0
    for shp in shapes:
        n = int(np.prod(shp))
        nr = -(-n // _PACK_UNIT) * (_PACK_UNIT // LANES)
        out.append(flat[r:r + nr].reshape(-1)[:n].reshape(shp))
        r += nr
    return out


def kernel(x, pre_norm_g, post_norm_g, w_in, b_in, w_out, fox_forget_bias, nsa_cmp_pos_k, nsa_cmp_w1_k, nsa_cmp_w2_k, nsa_cmp_pos_v, nsa_cmp_w1_v, nsa_cmp_w2_v, mla_q_norm_g, mla_w_uq, mla_kv_norm_g, mla_w_ukv, loss_target, m_pre_norm_g, m_post_norm_g, m_w_in, m_b_in, m_w_out, m_fox_forget_bias, m_nsa_cmp_pos_k, m_nsa_cmp_w1_k, m_nsa_cmp_w2_k, m_nsa_cmp_pos_v, m_nsa_cmp_w1_v, m_nsa_cmp_w2_v, m_mla_q_norm_g, m_mla_w_uq, m_mla_kv_norm_g, m_mla_w_ukv, v_pre_norm_g, v_post_norm_g, v_w_in, v_b_in, v_w_out, v_fox_forget_bias, v_nsa_cmp_pos_k, v_nsa_cmp_w1_k, v_nsa_cmp_w2_k, v_nsa_cmp_pos_v, v_nsa_cmp_w1_v, v_nsa_cmp_w2_v, v_mla_q_norm_g, v_mla_w_uq, v_mla_kv_norm_g, v_mla_w_ukv):
    given = dict(locals())
    local = {n: given[n] for n in _WEIGHTS}
    depth = pre_norm_g.shape[0]
    xs, target = x[0], loss_target[0]
    s = xs.shape[0]
    sharded = [n for n in _WEIGHTS if n in _SHARD_AXIS and n != "w_in"]
    small = [n for n in _WEIGHTS if n not in _SHARD_AXIS]
    chip = 2 * lax.axis_index("x") + lax.axis_index("y")
    core = lax.axis_index("c")
    own = lambda slots, mine: lax.dynamic_update_slice_in_dim(slots, mine[None], chip, axis=0)

    w_in_t = jnp.swapaxes(w_in, 1, 2).astype(_MXU)
    piece = lax.switch(chip, [functools.partial(_piece_from_shard, s=k) for k in range(N_CHIPS)], w_in_t)
    layer_shapes = [local[n].shape[1:] for n in sharded]
    flat = [_pack([local[n][l] for n in sharded], _MXU) for l in range(depth)]
    full = dict(local)
    for n in ["w_in"] + sharded:
        full[n] = []

    def add_layer(w_in_slots, flat_slots_):
        full["w_in"].append(w_in_slots)
        per_chip = [_unpack(flat_slots_[k], layer_shapes) for k in range(N_CHIPS)]
        for j, n in enumerate(sharded):
            full[n].append(jnp.concatenate([per_chip[k][j] for k in range(N_CHIPS)], axis=_SHARD_AXIS[n] - 1))

    halved = [piece[0].reshape(2, GROUP_W // 2, D_MODEL), flat[0].reshape(2, -1, LANES)]
    first_all = [own(a, b) for a, b in zip(_gather_chips(halved, name="gather_weights"), halved)]
    add_layer(first_all[0].reshape(N_CHIPS, GROUP_W, D_MODEL), first_all[1].reshape((N_CHIPS,) + flat[0].shape))
    later = [piece[l] for l in range(1, depth)] + flat[1:]
    started, token = _send_start(later, ["same"] * len(later), first_all[1], name="gather_later_start")
    full["pre_norm_g"] = pre_norm_g + token[0, 0]

    consts = _consts(s)
    params, act, saved = [], xs, []
    for l in range(depth):
        if l == 1:
            landed = [own(a, b) for a, b in zip(_send_wait(started, ["same"] * len(later), act,
                                                           name="gather_later_wait"), later)]
            for k in range(depth - 1):
                add_layer(landed[k], landed[depth - 1 + k])
        params.append(_layer_params(full, l))
        act, sv = _layer_fwd(act, params[l], consts, f"l{l}")
        saved.append(sv)
    dy, loss_parts = _loss_head(act, target, name="loss_head")

    def flat_slots(g, dtype):
        def part(n, k):
            a, ax = g[n], _SHARD_AXIS[n] - 1
            w = a.shape[ax] // N_CHIPS
            return lax.slice_in_dim(a, k * w, (k + 1) * w, axis=ax)
        return jnp.stack([_pack([part(n, k) for n in sharded], dtype) for k in range(N_CHIPS)])

    own_slot = lambda a: lax.dynamic_index_in_dim(a, chip, axis=0, keepdims=False)
    slots_of = lambda g: g["w_in"].reshape(N_CHIPS, GROUP_W, D_MODEL)

    modes = ["slots", "slots"]
    layer_grads, in_flight = [None] * depth, {}
    for l in reversed(range(depth)):
        dy, layer_grads[l] = _layer_bwd(dy, saved[l], params[l], consts, f"l{l}", _WIRE)
        if l > 0:
            wire = [slots_of(layer_grads[l]), flat_slots(layer_grads[l], _WIRE)]
            started, token = _send_start(wire, modes, dy, name=f"reduce_l{l}_start")
            in_flight[l] = (started, wire)
            params[l - 1] = dict(params[l - 1], post_g=params[l - 1]["post_g"] + token[0, 0])
    grad_x = dy[None]
    grads = {n: jnp.stack([layer_grads[l][n] for l in range(depth)]) for n in small}
    loss_row = jnp.concatenate([jnp.sum(loss_parts).reshape(1), jnp.zeros((LANES - 1,), f32)])
    small_shapes = [(LANES,)] + [grads[n].shape for n in small]
    contrib = _pack([loss_row] + [grads[n] for n in small], f32)

    halves = [slots_of(layer_grads[0]).reshape(N_CHIPS, 2, GROUP_W // 2, D_MODEL),
              flat_slots(layer_grads[0], _WIRE).reshape(N_CHIPS, 2, -1, LANES)]
    from_sibling = _swap_other_half(halves, name="reduce_pair")
    pair_sum = [_add_my_half(g, r, name=f"reduce_pair_add{j}") for j, (g, r) in enumerate(zip(halves, from_sibling))]
    from_chips = _alltoall_chips(pair_sum + [contrib], modes + ["same"], name="reduce_chips")
    my_half = [_sum_slots(own(slots, own_slot(ps)), name=f"reduce_chips_add{j}")
               for j, (slots, ps) in enumerate(zip(from_chips, pair_sum))]
    partial = []
    for l in range(1, depth):
        started, wire = in_flight[l]
        landed = _send_wait(started, modes, dy, name=f"reduce_l{l}_wait")
        partial += [_sum_slots(own(slots, own_slot(a)), name=f"reduce_l{l}_add{j}")
                    for j, (slots, a) in enumerate(zip(landed, wire))]
    partial.append(_sum_slots(own(from_chips[2], contrib), name="sum_small"))
    theirs = _swap_sibling(my_half + partial, name="reduce_share")
    first = core == 0
    whole = [jnp.concatenate([jnp.where(first, a, b), jnp.where(first, b, a)], axis=0)
             for a, b in zip(my_half, theirs[:2])]
    whole += [_add2(a[None], b[None], name=f"reduce_cores_add{j}")[0] for j, (a, b) in enumerate(zip(partial, theirs[2:]))]
    unpiece = [functools.partial(_shard_from_piece, s=k) for k in range(N_CHIPS)]
    summed = {"w_in": jnp.stack([lax.switch(chip, unpiece, whole[2 * l].T) for l in range(depth)])}
    rest = [_unpack(whole[2 * l + 1], layer_shapes) for l in range(depth)]
    for j, n in enumerate(sharded):
        summed[n] = jnp.stack([rest[l][j] for l in range(depth)])
    total = _unpack(whole[2 * depth], small_shapes)
    loss = total[0][0]
    summed.update(zip(small, total[1:]))

    deltas, new_m, new_v = {}, {}, {}
    for n in _WEIGHTS:
        deltas[n], new_m[n], new_v[n] = _adamw(local[n], summed[n], given["m_" + n], given["v_" + n], name=f"adamw_{n}")
    return (loss, grad_x, *[summed[n] for n in _WEIGHTS], *[deltas[n] for n in _WEIGHTS],
            *[new_m[n] for n in _WEIGHTS], *[new_v[n] for n in _WEIGHTS])
```

```python
import functools
import math

import numpy as np
import jax
import jax.numpy as jnp
from jax import lax
from jax.experimental import pallas as pl
from jax.experimental.pallas import tpu as pltpu

f32 = jnp.float32
bf16 = jnp.bfloat16
_MXU = jnp.bfloat16
_WIRE = jnp.bfloat16
_SDS = jax.ShapeDtypeStruct
_ANY = pl.BlockSpec(memory_space=pl.ANY)
_MESH = pl.DeviceIdType.MESH

D_MODEL = 2048
N_HEADS = 4
HEAD_DIM = 128
GROUP = 512
RMS_EPS = 1e-6
NEG_INF = -1e30
ROPE_THETA = 10000.0
CMP_LEN, CMP_STRIDE, SEL_LEN, SEL_TOPN, WINDOW = 32, 16, 64, 16, 512
FORCED_BONUS = 1e6
MLA_Q_RANK, MLA_KV_RANK, MLA_NOPE, MLA_ROPE = 384, 128, 128, 64
ADAM_LR, ADAM_B1, ADAM_B2, ADAM_EPS, ADAM_WD, ADAM_STEP = 0.001, 0.9, 0.999, 1e-08, 0.01, 10
LANES = 128
VMEM_LIMIT = 56 * 1024 * 1024
HP_FWD, HP_BWD = 2, 2

_SEGS = (
    ("sb_q", 512), ("sb_k", 512), ("sb_v", 512), ("sb_gate", 512), ("nsa_q", 512), ("nsa_k_cmp", 128),
    ("nsa_v_cmp", 128), ("nsa_k_sel", 128), ("nsa_v_sel", 128), ("nsa_k_win", 128), ("nsa_v_win", 128),
    ("nsa_branch", 12), ("nsa_gate", 512), ("fox_q", 512), ("fox_k", 512), ("fox_v", 512), ("fox_f", 4),
    ("fox_gate", 512), ("mla_cq", 384), ("mla_ckv", 128), ("mla_k_rope", 64), ("mla_gate", 512),
)
_ORIG, _WID = {}, {}
_o = 0
for _n, _w in _SEGS:
    _ORIG[_n], _WID[_n] = _o, _w
    _o += _w
IN_WIDTH = _o
N_CHIPS = 4
CHIP_COLS = IN_WIDTH // N_CHIPS
GROUP_W = 2048
ZW = N_CHIPS * GROUP_W
_GROUPS = (
    (("sb_q", 0, 512, 0), ("sb_k", 0, 512, 512), ("sb_v", 0, 512, 1024), ("sb_gate", 0, 212, 1536)),
    (("nsa_q", 0, 512, 0), ("nsa_k_cmp", 0, 128, 512), ("nsa_v_cmp", 0, 128, 640), ("nsa_k_sel", 0, 128, 768),
     ("nsa_v_sel", 0, 128, 896), ("nsa_k_win", 0, 128, 1024), ("nsa_v_win", 0, 128, 1152), ("nsa_branch", 0, 12, 1280),
     ("sb_gate", 212, 512, 1408), ("nsa_gate", 0, 156, 1712)),
    (("fox_q", 0, 512, 0), ("fox_k", 0, 512, 512), ("fox_v", 0, 368, 1024), ("nsa_gate", 156, 512, 1408)),
    (("mla_cq", 0, 384, 0), ("mla_ckv", 0, 128, 384), ("mla_k_rope", 0, 64, 512), ("fox_f", 0, 4, 640),
     ("fox_v", 368, 512, 768), ("fox_gate", 0, 512, 1024), ("mla_gate", 0, 512, 1536)),
)
_PIECES = {n: [] for n, _ in _SEGS}
for _s, _grp in enumerate(_GROUPS):
    _cover = sorted((_ORIG[n] + lo, _ORIG[n] + hi) for n, lo, hi, _ in _grp)
    assert _cover[0][0] == _s * CHIP_COLS and _cover[-1][1] == (_s + 1) * CHIP_COLS
    assert all(a[1] == b[0] for a, b in zip(_cover, _cover[1:]))
    _ends = sorted((off, off + hi - lo) for _, lo, hi, off in _grp)
    assert all(a[1] <= b[0] for a, b in zip(_ends, _ends[1:])) and _ends[-1][1] <= GROUP_W
    assert _ends[0][0] == 0 and all(e[0] % 16 == 0 for e in _ends)
    for _n, _lo, _hi, _off in _grp:
        _PIECES[_n].append((_s * GROUP_W + _off, _lo, _hi))
_AL = {n: p[0][0] for n, p in _PIECES.items() if len(p) == 1}


def _cp(sem=None):
    return pltpu.CompilerParams(dimension_semantics=sem, vmem_limit_bytes=VMEM_LIMIT)


def _mm(a, b):
    return jnp.dot(a.astype(_MXU), b.astype(_MXU), preferred_element_type=f32)


def _mm_nt(a, b):
    return lax.dot_general(a.astype(_MXU), b.astype(_MXU), (((1,), (1,)), ((), ())), preferred_element_type=f32)


def _mm_tn(a, b):
    return lax.dot_general(a.astype(_MXU), b.astype(_MXU), (((0,), (0,)), ((), ())), preferred_element_type=f32)


def _mm_split(x, t):
    hi = x.astype(_MXU)
    lo = (x - hi.astype(f32)).astype(_MXU)
    return jnp.dot(hi, t, preferred_element_type=f32) + jnp.dot(lo, t, preferred_element_type=f32)


def _sigmoid(x):
    return 1.0 / (1.0 + jnp.exp(-x))


def _iota(shape, dim):
    return lax.broadcasted_iota(jnp.int32, shape, dim)


def _pick(n, prefs):
    for p in prefs:
        if n % p == 0:
            return p
    return n


def _matmul(a, b, mode, *, bias=None, out_dtype=f32, name):
    grouped = b.ndim == 3
    b_shape = (b.shape[0] * b.shape[1], b.shape[2]) if grouped else b.shape
    if mode == "nn":
        (M, K), (K2, N) = a.shape, b_shape
    elif mode == "nt":
        (M, K), (N, K2) = a.shape, b_shape
    else:
        (K, M), (K2, N) = a.shape, b_shape
    assert K == K2
    tm = _pick(M, (1024, 512, 384, 256, 128))
    tn = _pick(N, (1024, 512, 384, 256, 128))
    tk = K if K <= 2048 else _pick(K, (2048, 2432, 1024, 512))
    nk = K // tk
    a_spec = {"nn": pl.BlockSpec((tm, tk), lambda i, j, k: (i, k)),
              "nt": pl.BlockSpec((tm, tk), lambda i, j, k: (i, k)),
              "tn": pl.BlockSpec((tk, tm), lambda i, j, k: (k, i))}[mode]
    if not grouped:
        b_spec = {"nn": pl.BlockSpec((tk, tn), lambda i, j, k: (k, j)),
                  "nt": pl.BlockSpec((tn, tk), lambda i, j, k: (j, k)),
                  "tn": pl.BlockSpec((tk, tn), lambda i, j, k: (k, j))}[mode]
    elif mode == "nt":
        per = b.shape[1] // tn
        b_spec = pl.BlockSpec((None, tn, tk), lambda i, j, k: (j // per, j % per, k))
    else:
        assert mode == "nn"
        per = b.shape[1] // tk
        b_spec = pl.BlockSpec((None, tk, tn), lambda i, j, k: (k // per, k % per, j))
    dot = {"nn": _mm, "nt": _mm_nt, "tn": _mm_tn}[mode]
    has_bias = bias is not None

    def body(*refs):
        if has_bias:
            a_ref, b_ref, bias_ref, o_ref, acc_ref = refs
        else:
            a_ref, b_ref, o_ref, acc_ref = refs
            bias_ref = None
        k = pl.program_id(2)
        part = dot(a_ref[...], b_ref[...])

        def finish(total):
            if has_bias:
                total = total + bias_ref[...]
            o_ref[...] = total.astype(o_ref.dtype)

        if nk == 1:
            finish(part)
        else:
            @pl.when(k == 0)
            def _():
                acc_ref[...] = part

            @pl.when(k > 0)
            def _():
                acc_ref[...] += part

            @pl.when(k == nk - 1)
            def _():
                finish(acc_ref[...])

    in_specs = [a_spec, b_spec]
    args = [a, b]
    if has_bias:
        in_specs.append(pl.BlockSpec((1, tn), lambda i, j, k: (0, j)))
        args.append(bias.reshape(1, N))
    return pl.pallas_call(
        body, out_shape=_SDS((M, N), out_dtype), grid=(M // tm, N // tn, nk),
        in_specs=in_specs, out_specs=pl.BlockSpec((tm, tn), lambda i, j, k: (i, j)),
        scratch_shapes=[pltpu.VMEM((tm, tn), f32)],
        compiler_params=_cp(("parallel", "parallel", "arbitrary")), name=name,
    )(*args)


def _row_block(s):
    return _pick(s, (256, 128))


def _rms_fwd(x, g, *, out_dtype, name):
    s, d = x.shape
    rb = _row_block(s)

    def body(x_ref, g_ref, o_ref):
        xv = x_ref[...]
        r = lax.rsqrt(jnp.mean(xv * xv, axis=-1, keepdims=True) + RMS_EPS)
        o_ref[...] = (xv * r * g_ref[...]).astype(o_ref.dtype)

    return pl.pallas_call(
        body, out_shape=_SDS((s, d), out_dtype), grid=(s // rb,),
        in_specs=[pl.BlockSpec((rb, d), lambda i: (i, 0)), pl.BlockSpec((1, d), lambda i: (0, 0))],
        out_specs=pl.BlockSpec((rb, d), lambda i: (i, 0)), compiler_params=_cp(("parallel",)), name=name,
    )(x, g.reshape(1, d))


def _postnorm_fwd(u, g, x, *, name):
    s, d = u.shape
    rb = _row_block(s)

    def body(u_ref, g_ref, x_ref, o_ref):
        uv = u_ref[...]
        r = lax.rsqrt(jnp.mean(uv * uv, axis=-1, keepdims=True) + RMS_EPS)
        o_ref[...] = x_ref[...] + uv * r * g_ref[...]

    return pl.pallas_call(
        body, out_shape=_SDS((s, d), f32), grid=(s // rb,),
        in_specs=[pl.BlockSpec((rb, d), lambda i: (i, 0)), pl.BlockSpec((1, d), lambda i: (0, 0)),
                  pl.BlockSpec((rb, d), lambda i: (i, 0))],
        out_specs=pl.BlockSpec((rb, d), lambda i: (i, 0)), compiler_params=_cp(("parallel",)), name=name,
    )(u, g.reshape(1, d), x)


def _fold_rows(v):
    r = v.shape[0]
    acc = v[0:8]
    for k in range(1, r // 8):
        acc = acc + v[8 * k:8 * k + 8]
    return acc


def _rms_bwd(dy, x, g, res=None, *, name):
    s, d = x.shape
    rb = _row_block(s)
    nb = s // rb
    has_res = res is not None

    def body(*refs):
        if has_res:
            dy_ref, x_ref, g_ref, res_ref, dx_ref, dg_ref, acc_ref = refs
        else:
            dy_ref, x_ref, g_ref, dx_ref, dg_ref, acc_ref = refs
        i = pl.program_id(0)
        xv = x_ref[...]
        r = lax.rsqrt(jnp.mean(xv * xv, axis=-1, keepdims=True) + RMS_EPS)
        xh = xv * r
        dyv = dy_ref[...]
        dxh = dyv * g_ref[...]
        dx = r * (dxh - xh * jnp.mean(dxh * xh, axis=-1, keepdims=True))
        if has_res:
            dx = dx + res_ref[...]
        dx_ref[...] = dx
        part = _fold_rows(dyv * xh)

        @pl.when(i == 0)
        def _():
            acc_ref[...] = part

        @pl.when(i > 0)
        def _():
            acc_ref[...] += part

        @pl.when(i == nb - 1)
        def _():
            dg_ref[...] = jnp.sum(acc_ref[...], axis=0, keepdims=True)

    blk = pl.BlockSpec((rb, d), lambda i: (i, 0))
    in_specs = [blk, blk, pl.BlockSpec((1, d), lambda i: (0, 0))] + ([blk] if has_res else [])
    args = [dy, x, g.reshape(1, d)] + ([res] if has_res else [])
    return pl.pallas_call(
        body, out_shape=(_SDS((s, d), f32), _SDS((1, d), f32)), grid=(nb,), in_specs=in_specs,
        out_specs=(blk, pl.BlockSpec((1, d), lambda i: (0, 0))),
        scratch_shapes=[pltpu.VMEM((8, d), f32)], compiler_params=_cp(("arbitrary",)), name=name,
    )(*args)


def _loss_head(y, target, *, name):
    s, d = y.shape
    rb = _row_block(s)
    nb = s // rb

    def body(y_ref, t_ref, dy_ref, l_ref):
        i = pl.program_id(0)
        e = y_ref[...] - t_ref[...]
        dy_ref[...] = e * (1.0 / d)
        rows = _fold_rows(e * e)
        part = rows[:, 0:LANES]
        for k in range(1, d // LANES):
            part = part + rows[:, k * LANES:(k + 1) * LANES]
        part = part * (0.5 / d)

        @pl.when(i == 0)
        def _():
            l_ref[...] = part

        @pl.when(i > 0)
        def _():
            l_ref[...] += part

    blk = pl.BlockSpec((rb, d), lambda i: (i, 0))
    return pl.pallas_call(
        body, out_shape=(_SDS((s, d), f32), _SDS((8, LANES), f32)), grid=(nb,), in_specs=[blk, blk],
        out_specs=(blk, pl.BlockSpec((8, LANES), lambda i: (0, 0))),
        compiler_params=_cp(("arbitrary",)), name=name,
    )(y, target)


def _colsum(a, *, name):
    s, n = a.shape
    rb = _row_block(s)
    nb = s // rb
    tn = _pick(n, (2432, 2048, 1024, 512, 384, 128))

    def body(a_ref, o_ref, acc_ref):
        i = pl.program_id(1)
        part = _fold_rows(a_ref[...].astype(f32))

        @pl.when(i == 0)
        def _():
            acc_ref[...] = part

        @pl.when(i > 0)
        def _():
            acc_ref[...] += part

        @pl.when(i == nb - 1)
        def _():
            o_ref[...] = jnp.sum(acc_ref[...], axis=0, keepdims=True)

    return pl.pallas_call(
        body, out_shape=_SDS((1, n), f32), grid=(n // tn, nb),
        in_specs=[pl.BlockSpec((rb, tn), lambda j, i: (i, j))], out_specs=pl.BlockSpec((1, tn), lambda j, i: (0, j)),
        scratch_shapes=[pltpu.VMEM((8, tn), f32)], compiler_params=_cp(("parallel", "arbitrary")), name=name,
    )(a)


def _gate_fwd(outs, gate, *, name):
    s, d = gate.shape
    rb = _row_block(s)
    n = len(outs)
    w = d // n

    def body(*refs):
        g_ref, m_ref = refs[n], refs[n + 1]
        for k in range(n):
            gv = g_ref[:, k * w:(k + 1) * w]
            m_ref[:, k * w:(k + 1) * w] = (refs[k][...] * (gv * _sigmoid(gv))).astype(m_ref.dtype)

    blk = pl.BlockSpec((rb, d), lambda i: (i, 0))
    part = pl.BlockSpec((rb, w), lambda i: (i, 0))
    return pl.pallas_call(body, out_shape=_SDS((s, d), _MXU), grid=(s // rb,), in_specs=[part] * n + [blk],
                          out_specs=blk, compiler_params=_cp(("parallel",)), name=name)(*outs, gate)


def _gate_bwd(dmix, outs, gate, *, name):
    s, d = gate.shape
    rb = _row_block(s)
    n = len(outs)
    w = d // n

    def body(*refs):
        dm_ref, o_refs, g_ref, do_refs, dg_ref = refs[0], refs[1:1 + n], refs[1 + n], refs[2 + n:2 + 2 * n], refs[-1]
        for k in range(n):
            sl = slice(k * w, (k + 1) * w)
            gv = g_ref[:, sl]
            sg = _sigmoid(gv)
            dm = dm_ref[:, sl]
            do_refs[k][...] = dm * (gv * sg)
            dg_ref[:, sl] = dm * o_refs[k][...] * (sg * (1.0 + gv * (1.0 - sg)))

    blk = pl.BlockSpec((rb, d), lambda i: (i, 0))
    part = pl.BlockSpec((rb, w), lambda i: (i, 0))
    return pl.pallas_call(body, out_shape=tuple(_SDS((s, w), f32) for _ in range(n)) + (_SDS((s, d), f32),),
                          grid=(s // rb,), in_specs=[blk] + [part] * n + [blk], out_specs=(part,) * n + (blk,),
                          compiler_params=_cp(("parallel",)), name=name)(dmix, *outs, gate)


def _adamw(w, g, m, v, *, name):
    shape = w.shape
    cols = shape[-1]
    rows = int(np.prod(shape[:-1])) if len(shape) > 1 else 1
    to2 = lambda t: t.reshape(rows, cols)
    rb = rows
    if rows * cols * 4 > (1 << 20):
        rb = max(d for d in range(8, rows + 1, 8) if rows % d == 0 and (d * cols * 4 <= (1600 << 10) or d == 8))

    def body(w_ref, g_ref, m_ref, v_ref, d_ref, nm_ref, nv_ref):
        gv = g_ref[...]
        mn = ADAM_B1 * m_ref[...] + (1.0 - ADAM_B1) * gv
        vn = ADAM_B2 * v_ref[...] + (1.0 - ADAM_B2) * (gv * gv)
        m_hat = mn / (1.0 - ADAM_B1 ** ADAM_STEP)
        v_hat = vn / (1.0 - ADAM_B2 ** ADAM_STEP)
        d_ref[...] = -ADAM_LR * (m_hat / (jnp.sqrt(v_hat) + ADAM_EPS) + ADAM_WD * w_ref[...])
        nm_ref[...] = mn
        nv_ref[...] = vn

    blk = pl.BlockSpec((rb, cols), lambda i: (i, 0))
    out = pl.pallas_call(body, out_shape=tuple(_SDS((rows, cols), f32) for _ in range(3)), grid=(rows // rb,),
                         in_specs=[blk] * 4, out_specs=(blk,) * 3, compiler_params=_cp(("parallel",)),
                         name=name)(to2(w), to2(g), to2(m), to2(v))
    return tuple(t.reshape(shape) for t in out)


def _sum_slots(a, *, name):
    p, n, c = a.shape
    rb = max(d for d in range(8, n + 1, 8) if n % d == 0 and (p * d * c * 4 <= (6 << 20) or d == 8))

    def body(a_ref, o_ref):
        acc = a_ref[0].astype(f32)
        for k in range(1, p):
            acc = acc + a_ref[k].astype(f32)
        o_ref[...] = acc

    return pl.pallas_call(body, out_shape=_SDS((n, c), f32), grid=(n // rb,),
                          in_specs=[pl.BlockSpec((p, rb, c), lambda i: (0, i, 0))],
                          out_specs=pl.BlockSpec((rb, c), lambda i: (i, 0)), compiler_params=_cp(("parallel",)),
                          name=name)(a)


def _add2(a, b, *, name):
    p, n, c = a.shape
    rb = max(d for d in range(8, n + 1, 8) if n % d == 0 and (d * c * 4 <= (2 << 20) or d == 8))

    def body(a_ref, b_ref, o_ref):
        o_ref[...] = a_ref[...] + b_ref[...]

    blk = pl.BlockSpec((1, rb, c), lambda s, i: (s, i, 0))
    return pl.pallas_call(body, out_shape=_SDS((p, n, c), f32), grid=(p, n // rb), in_specs=[blk, blk], out_specs=blk,
                          compiler_params=_cp(("parallel", "parallel")), name=name)(a, b)


def _rope_tables(pos, dim):
    half = dim // 2
    inv = ROPE_THETA ** (-jnp.arange(half, dtype=f32) / half)
    ang = pos.astype(f32)[:, None] * inv[None, :]
    c, s = jnp.cos(ang), jnp.sin(ang)
    z = jnp.zeros_like(c)
    pad = [jnp.zeros((pos.shape[0], LANES - dim), f32)] if dim < LANES else []
    return (jnp.concatenate([c, c] + pad, axis=1), jnp.concatenate([-s, z] + pad, axis=1),
            jnp.concatenate([z, s] + pad, axis=1))


def _rope(x, cos, sa, sb, half, transpose=False):
    if transpose:
        return x * cos + pltpu.roll(x * sa, half, 1) + pltpu.roll(x * sb, LANES - half, 1)
    return x * cos + pltpu.roll(x, LANES - half, 1) * sa + pltpu.roll(x, half, 1) * sb


def _rope_call(items, tables, half, transpose, *, name):
    s = items[0][0].shape[0]
    rb = _row_block(s)
    n = len(items)

    def body(*refs):
        cos, sa, sb = refs[n][...], refs[n + 1][...], refs[n + 2][...]
        for k in range(n):
            x_ref, o_ref = refs[k], refs[n + 3 + k]
            for j in range(items[k][1] // LANES):
                sl = slice(j * LANES, (j + 1) * LANES)
                o_ref[:, sl] = _rope(x_ref[:, sl], cos, sa, sb, half, transpose)

    in_specs = [pl.BlockSpec((rb, w), functools.partial(lambda i, cb: (i, cb), cb=cb)) for _, w, cb in items]
    in_specs += [pl.BlockSpec((rb, LANES), lambda i: (i, 0))] * 3
    out_specs = tuple(pl.BlockSpec((rb, w), lambda i: (i, 0)) for _, w, _ in items)
    return pl.pallas_call(
        body, out_shape=tuple(_SDS((s, w), f32) for _, w, _ in items), grid=(s // rb,), in_specs=in_specs,
        out_specs=out_specs, compiler_params=_cp(("parallel",)), name=name,
    )(*[a for a, _, _ in items], *tables)


def _attn_block(s):
    return _pick(s, (512, 256, 128))


def _lower_mask(b, strict):
    r, c = _iota((b, b), 0), _iota((b, b), 1)
    return (c < r) if strict else (c <= r)


def _pick_lane(block, h):
    return jnp.sum(jnp.where(_iota(block.shape, 1) == h, block, 0.0), axis=1, keepdims=True)


def _head_bias(cum_blk, g, j, hp):
    if hp == N_HEADS:
        return cum_blk[:, j:j + 1]
    return _pick_lane(cum_blk, g * hp + j)


def _attn_fwd(q, k, v, qcol, kcol, vcol, dq, cum, cum_t, *, scale, hp, name):
    s = q.shape[0]
    b = _attn_block(s)
    nq = s // b
    has_bias = cum is not None
    assert qcol % hp == 0 and kcol % hp == 0 and vcol % hp == 0

    def body(*refs):
        if has_bias:
            q_ref, k_ref, v_ref, cum_ref, cumt_ref, o_ref, lse_ref = refs
        else:
            q_ref, k_ref, v_ref, o_ref, lse_ref = refs
        g, i = pl.program_id(0), pl.program_id(1)
        qs = [q_ref[:, j * dq:(j + 1) * dq].astype(_MXU) for j in range(hp)]
        cqs = [_head_bias(cum_ref[...], g, j, hp) for j in range(hp)] if has_bias else None

        def chunk(c, carry, diag):
            st = pl.multiple_of(c * b, b)
            mask = _lower_mask(b, False) if diag else None
            out = []
            for j in range(hp):
                m, l, acc = carry[j]
                z = _mm_nt(qs[j], k_ref[pl.ds(st, b), j * dq:(j + 1) * dq]) * scale
                if has_bias:
                    z = z + (cqs[j] - cumt_ref[j, c])
                if diag:
                    z = jnp.where(mask, z, NEG_INF)
                m_new = jnp.maximum(m, jnp.max(z, axis=1, keepdims=True))
                p = jnp.exp(z - m_new)
                if diag:
                    p = jnp.where(mask, p, 0.0)
                alpha = jnp.exp(m - m_new)
                l = alpha * l + jnp.sum(p, axis=1, keepdims=True)
                acc = alpha * acc + _mm(p, v_ref[pl.ds(st, b), j * HEAD_DIM:(j + 1) * HEAD_DIM])
                out.append((m_new, l, acc))
            return tuple(out)

        init = tuple((jnp.full((b, 1), NEG_INF, f32), jnp.zeros((b, 1), f32), jnp.zeros((b, HEAD_DIM), f32))
                     for _ in range(hp))
        carry = lax.fori_loop(0, i, lambda c, cr: chunk(c, cr, False), init)
        for j, (m, l, acc) in enumerate(chunk(i, carry, True)):
            o_ref[:, j * HEAD_DIM:(j + 1) * HEAD_DIM] = acc / l
            lse_ref[j] = m + jnp.log(l)

    in_specs = [pl.BlockSpec((b, hp * dq), lambda g, i: (i, qcol // hp + g)),
                pl.BlockSpec((s, hp * dq), lambda g, i: (0, kcol // hp + g)),
                pl.BlockSpec((s, hp * HEAD_DIM), lambda g, i: (0, vcol // hp + g))]
    args = [q, k, v]
    if has_bias:
        in_specs += [pl.BlockSpec((b, LANES), lambda g, i: (i, 0)),
                     pl.BlockSpec((hp, nq, 1, b), lambda g, i: (g, 0, 0, 0))]
        args += [cum, cum_t]
    return pl.pallas_call(
        body, out_shape=(_SDS((s, N_HEADS * HEAD_DIM), f32), _SDS((N_HEADS, s, 1), f32)), grid=(N_HEADS // hp, nq),
        in_specs=in_specs,
        out_specs=(pl.BlockSpec((b, hp * HEAD_DIM), lambda g, i: (i, g)),
                   pl.BlockSpec((hp, b, 1), lambda g, i: (g, i, 0))),
        compiler_params=_cp(("parallel", "parallel")), name=name,
    )(*args)


def _attn_bwd(q, k, v, qcol, kcol, vcol, dq, do, o, lse, cum, cum_t, *, scale, hp, name):
    s = q.shape[0]
    b = _attn_block(s)
    nq = s // b
    has_bias = cum is not None
    assert qcol % hp == 0 and kcol % hp == 0 and vcol % hp == 0
    hd = lambda j: slice(j * HEAD_DIM, (j + 1) * HEAD_DIM)
    hq = lambda j: slice(j * dq, (j + 1) * dq)

    def body(*refs):
        if has_bias:
            (q_ref, k_ref, v_ref, do_ref, o_ref, lse_ref, cum_ref, cumt_ref, dq_ref, dk_ref, dv_ref, dck_ref,
             p_sc, dp_sc) = refs
        else:
            q_ref, k_ref, v_ref, do_ref, o_ref, lse_ref, dq_ref, dk_ref, dv_ref = refs
        g, i = pl.program_id(0), pl.program_id(1)

        @pl.when(i == 0)
        def _():
            dk_ref[...] = jnp.zeros_like(dk_ref)
            dv_ref[...] = jnp.zeros_like(dv_ref)
            if has_bias:
                dck_ref[...] = jnp.zeros_like(dck_ref)

        qs = [q_ref[:, hq(j)].astype(_MXU) for j in range(hp)]
        dos = [do_ref[:, hd(j)].astype(_MXU) for j in range(hp)]
        lses = [lse_ref[j] for j in range(hp)]
        cqs = [_head_bias(cum_ref[...], g, j, hp) for j in range(hp)] if has_bias else None

        def probs(j, c, diag):
            st = pl.multiple_of(c * b, b)
            z = _mm_nt(qs[j], k_ref[pl.ds(st, b), hq(j)]) * scale
            if has_bias:
                z = z + (cqs[j] - cumt_ref[j, c])
            p = jnp.exp(z - lses[j])
            if diag:
                p = jnp.where(_lower_mask(b, False), p, 0.0)
            return p, _mm_nt(dos[j], v_ref[pl.ds(st, b), hd(j)])

        if has_bias:
            def first(c, accs, diag):
                out = []
                for j in range(hp):
                    p, dp = probs(j, c, diag)
                    p_sc[j, c] = p
                    dp_sc[j, c] = dp
                    out.append(accs[j] + jnp.sum(p * dp, axis=1, keepdims=True))
                return tuple(out)

            deltas = lax.fori_loop(0, i, lambda c, a: first(c, a, False),
                                   tuple(jnp.zeros((b, 1), f32) for _ in range(hp)))
            deltas = first(i, deltas, True)
        else:
            deltas = [jnp.sum(do_ref[:, hd(j)] * o_ref[:, hd(j)], axis=1, keepdims=True) for j in range(hp)]

        def chunk(c, dq_accs, diag):
            st = pl.multiple_of(c * b, b)
            out = []
            for j in range(hp):
                p, dp = (p_sc[j, c], dp_sc[j, c]) if has_bias else probs(j, c, diag)
                ds = p * (dp - deltas[j])
                dk_ref[pl.ds(st, b), hq(j)] += _mm_tn(ds, qs[j]) * scale
                dv_ref[pl.ds(st, b), hd(j)] += _mm_tn(p, dos[j])
                if has_bias:
                    dck_ref[j, c] += -jnp.sum(ds, axis=0, keepdims=True)
                out.append(dq_accs[j] + _mm(ds, k_ref[pl.ds(st, b), hq(j)]))
            return tuple(out)

        accs = lax.fori_loop(0, i, lambda c, a: chunk(c, a, False), tuple(jnp.zeros((b, dq), f32) for _ in range(hp)))
        for j, acc in enumerate(chunk(i, accs, True)):
            dq_ref[:, hq(j)] = acc * scale

    rowq = pl.BlockSpec((b, hp * HEAD_DIM), lambda g, i: (i, g))
    in_specs = [pl.BlockSpec((b, hp * dq), lambda g, i: (i, qcol // hp + g)),
                pl.BlockSpec((s, hp * dq), lambda g, i: (0, kcol // hp + g)),
                pl.BlockSpec((s, hp * HEAD_DIM), lambda g, i: (0, vcol // hp + g)), rowq, rowq,
                pl.BlockSpec((hp, b, 1), lambda g, i: (g, i, 0))]
    args = [q, k, v, do, o, lse]
    out_shape = [_SDS((s, N_HEADS * dq), f32), _SDS((s, N_HEADS * dq), f32), _SDS((s, N_HEADS * HEAD_DIM), f32)]
    out_specs = [pl.BlockSpec((b, hp * dq), lambda g, i: (i, g)), pl.BlockSpec((s, hp * dq), lambda g, i: (0, g)),
                 pl.BlockSpec((s, hp * HEAD_DIM), lambda g, i: (0, g))]
    if has_bias:
        in_specs += [pl.BlockSpec((b, LANES), lambda g, i: (i, 0)),
                     pl.BlockSpec((hp, nq, 1, b), lambda g, i: (g, 0, 0, 0))]
        args += [cum, cum_t]
        out_shape.append(_SDS((N_HEADS, nq, 1, b), f32))
        out_specs.append(pl.BlockSpec((hp, nq, 1, b), lambda g, i: (g, 0, 0, 0)))
    return pl.pallas_call(
        body, out_shape=tuple(out_shape), grid=(N_HEADS // hp, nq), in_specs=in_specs, out_specs=tuple(out_specs),
        scratch_shapes=[pltpu.VMEM((hp, nq, b, b), f32)] * 2 if has_bias else [],
        compiler_params=_cp(("parallel", "arbitrary")), name=name,
    )(*args)


def _tri(b, kind):
    r, c = _iota((b, b), 0), _iota((b, b), 1)
    cond = {"row_gt": r > c, "row_lt": r < c, "row_ge": r >= c, "row_le": r <= c}[kind]
    return jnp.where(cond, 1.0, 0.0).astype(_MXU)


def _log_keep(z):
    return -(jnp.maximum(z, 0.0) + jnp.log(1.0 + jnp.exp(-jnp.abs(z))))


def _sb_fwd(z_all, *, hp, name):
    s = z_all.shape[0]
    b = _attn_block(s)
    nq = s // b
    scale = HEAD_DIM ** -0.5
    qcol, kcol, vcol = (_AL[n] // (hp * HEAD_DIM) for n in ("sb_q", "sb_k", "sb_v"))
    hd = lambda j: slice(j * HEAD_DIM, (j + 1) * HEAD_DIM)

    def body(q_ref, k_ref, v_ref, o_ref):
        i = pl.program_id(1)
        qs = [q_ref[:, hd(j)].astype(_MXU) for j in range(hp)]
        upper = _tri(b, "row_gt")

        def chunk(c, carry, diag):
            st = pl.multiple_of(c * b, b)
            mask = _lower_mask(b, True) if diag else None
            out = []
            for j in range(hp):
                rsum, acc = carry[j]
                z = _mm_nt(qs[j], k_ref[pl.ds(st, b), hd(j)]) * scale
                lk = _log_keep(z)
                if diag:
                    lk = jnp.where(mask, lk, 0.0)
                a = z + lk + _mm_split(lk, upper) + rsum
                if diag:
                    a = jnp.where(mask, a, NEG_INF)
                acc = acc + _mm(jnp.exp(a), v_ref[pl.ds(st, b), hd(j)])
                out.append((rsum + jnp.sum(lk, axis=1, keepdims=True), acc))
            return tuple(out)

        init = tuple((jnp.zeros((b, 1), f32), jnp.zeros((b, HEAD_DIM), f32)) for _ in range(hp))
        carry = lax.fori_loop(0, i, lambda t, cr: chunk(i - 1 - t, cr, False), chunk(i, init, True))
        for j in range(hp):
            o_ref[:, hd(j)] = carry[j][1]

    w = hp * HEAD_DIM
    return pl.pallas_call(
        body, out_shape=_SDS((s, GROUP), f32), grid=(N_HEADS // hp, nq),
        in_specs=[pl.BlockSpec((b, w), lambda g, i: (i, qcol + g)), pl.BlockSpec((s, w), lambda g, i: (0, kcol + g)),
                  pl.BlockSpec((s, w), lambda g, i: (0, vcol + g))],
        out_specs=pl.BlockSpec((b, w), lambda g, i: (i, g)),
        compiler_params=_cp(("parallel", "parallel")), name=name,
    )(z_all, z_all, z_all)


def _sb_bwd(z_all, do, *, hp, name):
    s = z_all.shape[0]
    b = _attn_block(s)
    nq = s // b
    scale = HEAD_DIM ** -0.5
    qcol, kcol, vcol = (_AL[n] // (hp * HEAD_DIM) for n in ("sb_q", "sb_k", "sb_v"))
    hd = lambda j: slice(j * HEAD_DIM, (j + 1) * HEAD_DIM)

    def body(q_ref, k_ref, v_ref, do_ref, dq_ref, dk_ref, dv_ref, z_sc, lk_sc, r_sc):
        i = pl.program_id(1)

        @pl.when(i == 0)
        def _():
            dk_ref[...] = jnp.zeros_like(dk_ref)
            dv_ref[...] = jnp.zeros_like(dv_ref)

        qs = [q_ref[:, hd(j)].astype(_MXU) for j in range(hp)]
        dos = [do_ref[:, hd(j)].astype(_MXU) for j in range(hp)]
        upper = _tri(b, "row_gt")
        lower = _tri(b, "row_lt")

        def scores(c, rsums, diag):
            st = pl.multiple_of(c * b, b)
            out = []
            for j in range(hp):
                z = _mm_nt(qs[j], k_ref[pl.ds(st, b), hd(j)]) * scale
                lk = _log_keep(z)
                if diag:
                    lk = jnp.where(_lower_mask(b, True), lk, 0.0)
                z_sc[j, c] = z
                lk_sc[j, c] = lk
                r_sc[j, c] = _mm_split(lk, upper) + rsums[j]
                out.append(rsums[j] + jnp.sum(lk, axis=1, keepdims=True))
            return tuple(out)

        rsums = scores(i, tuple(jnp.zeros((b, 1), f32) for _ in range(hp)), True)
        lax.fori_loop(0, i, lambda t, r: scores(i - 1 - t, r, False), rsums)

        def grads(c, carry, diag):
            st = pl.multiple_of(c * b, b)
            mask = _lower_mask(b, True) if diag else None
            out = []
            for j in range(hp):
                psum, dq_acc = carry[j]
                z, lk = z_sc[j, c], lk_sc[j, c]
                lb = z + lk
                a = lb + r_sc[j, c]
                if diag:
                    a = jnp.where(mask, a, NEG_INF)
                w = jnp.exp(a)
                e = _mm_nt(dos[j], v_ref[pl.ds(st, b), hd(j)]) * w
                before = _mm_split(e, lower) + psum
                dz = e * jnp.exp(lk) - before * jnp.exp(lb)
                if diag:
                    dz = jnp.where(mask, dz, 0.0)
                dk_ref[pl.ds(st, b), hd(j)] += _mm_tn(dz, qs[j]) * scale
                dv_ref[pl.ds(st, b), hd(j)] += _mm_tn(w, dos[j])
                out.append((psum + jnp.sum(e, axis=1, keepdims=True), dq_acc + _mm(dz, k_ref[pl.ds(st, b), hd(j)])))
            return tuple(out)

        init = tuple((jnp.zeros((b, 1), f32), jnp.zeros((b, HEAD_DIM), f32)) for _ in range(hp))
        carry = grads(i, lax.fori_loop(0, i, lambda c, cr: grads(c, cr, False), init), True)
        for j in range(hp):
            dq_ref[:, hd(j)] = carry[j][1] * scale

    w = hp * HEAD_DIM
    blk = pl.BlockSpec((b, w), lambda g, i: (i, g))
    full = pl.BlockSpec((s, w), lambda g, i: (0, g))
    return pl.pallas_call(
        body, out_shape=tuple(_SDS((s, GROUP), f32) for _ in range(3)), grid=(N_HEADS // hp, nq),
        in_specs=[pl.BlockSpec((b, w), lambda g, i: (i, qcol + g)), pl.BlockSpec((s, w), lambda g, i: (0, kcol + g)),
                  pl.BlockSpec((s, w), lambda g, i: (0, vcol + g)), blk],
        out_specs=(blk, full, full),
        scratch_shapes=[pltpu.VMEM((hp, nq, b, b), f32)] * 3,
        compiler_params=_cp(("parallel", "arbitrary")), name=name,
    )(z_all, z_all, z_all, do)


def _split3_left(t, x):
    hi = x.astype(_MXU)
    r1 = x - hi.astype(f32)
    mid = r1.astype(_MXU)
    lo = (r1 - mid.astype(f32)).astype(_MXU)
    dot = functools.partial(jnp.dot, preferred_element_type=f32)
    return dot(t, hi) + dot(t, mid) + dot(t, lo)


def _split3_right(x, t):
    hi = x.astype(_MXU)
    r1 = x - hi.astype(f32)
    mid = r1.astype(_MXU)
    lo = (r1 - mid.astype(f32)).astype(_MXU)
    dot = functools.partial(jnp.dot, preferred_element_type=f32)
    return dot(hi, t) + dot(mid, t) + dot(lo, t)


def _fox_cum_fwd(z_all, bias, *, name):
    s = z_all.shape[0]
    b = _attn_block(s)
    fcol = _AL["fox_f"] // LANES

    def body(f_ref, b_ref, cum_ref, cumt_ref, carry_ref):
        i = pl.program_id(0)

        @pl.when(i == 0)
        def _():
            carry_ref[...] = jnp.zeros_like(carry_ref)

        u = f_ref[...] + b_ref[...]
        lf = jnp.minimum(u, 0.0) - jnp.log1p(jnp.exp(-jnp.abs(u)))
        cum = _split3_left(_tri(b, "row_ge"), lf) + carry_ref[...]
        cum_ref[...] = cum
        cumt_ref[...] = cum.T[0:8, :]
        carry_ref[...] = cum_ref[b - 1:b, :]

    return pl.pallas_call(
        body, out_shape=(_SDS((s, LANES), f32), _SDS((8, s), f32)), grid=(s // b,),
        in_specs=[pl.BlockSpec((b, LANES), lambda i: (i, fcol)), pl.BlockSpec((1, LANES), lambda i: (0, 0))],
        out_specs=(pl.BlockSpec((b, LANES), lambda i: (i, 0)), pl.BlockSpec((8, b), lambda i: (0, i))),
        scratch_shapes=[pltpu.VMEM((1, LANES), f32)], compiler_params=_cp(("arbitrary",)), name=name,
    )(z_all, bias)


def _fox_cum_bwd(z_all, bias, dcum_t, *, name):
    s = z_all.shape[0]
    b = _attn_block(s)
    nb = s // b
    fcol = _AL["fox_f"] // LANES

    def body(f_ref, b_ref, dc_ref, df_ref, db_ref, carry_ref):
        i = pl.program_id(0)

        @pl.when(i == 0)
        def _():
            carry_ref[...] = jnp.zeros_like(carry_ref)
            db_ref[...] = jnp.zeros_like(db_ref)

        dc = dc_ref[...]
        rev = _split3_right(dc, _tri(b, "row_ge")) + carry_ref[...]
        carry_ref[...] = carry_ref[...] + jnp.sum(dc, axis=1, keepdims=True)
        dlf = jnp.concatenate([rev, jnp.zeros((LANES - 8, b), f32)], axis=0).T
        u = f_ref[...] + b_ref[...]
        df = jnp.where(_iota((b, LANES), 1) < N_HEADS, dlf * (1.0 - _sigmoid(u)), 0.0)
        df_ref[...] = df
        db_ref[...] += jnp.sum(df, axis=0, keepdims=True)

    return pl.pallas_call(
        body, out_shape=(_SDS((s, LANES), f32), _SDS((1, LANES), f32)), grid=(nb,),
        in_specs=[pl.BlockSpec((b, LANES), lambda i: (nb - 1 - i, fcol)), pl.BlockSpec((1, LANES), lambda i: (0, 0)),
                  pl.BlockSpec((8, b), lambda i: (0, nb - 1 - i))],
        out_specs=(pl.BlockSpec((b, LANES), lambda i: (nb - 1 - i, 0)), pl.BlockSpec((1, LANES), lambda i: (0, 0))),
        scratch_shapes=[pltpu.VMEM((8, 1), f32)], compiler_params=_cp(("arbitrary",)), name=name,
    )(z_all, bias, dcum_t)


MLA_QW = 2 * LANES


def _rms_rows(x):
    r = lax.rsqrt(jnp.mean(x * x, axis=-1, keepdims=True) + RMS_EPS)
    return x * r, r


def _mla_prep_fwd(z_all, gq, gkv, wuq, wk, wv, tables, *, name):
    s = z_all.shape[0]
    rb = _row_block(s)
    half = MLA_ROPE // 2

    def body(cq_ref, ckv_ref, kr_ref, gq_ref, gkv_ref, wuq_ref, wk_ref, wv_ref, cos_ref, sa_ref, sb_ref,
             q_ref, k_ref, v_ref):
        cos, sa, sb = cos_ref[...], sa_ref[...], sb_ref[...]
        xh, _ = _rms_rows(cq_ref[...])
        qp = _mm(xh * gq_ref[...], wuq_ref[...])
        kh, _ = _rms_rows(ckv_ref[...])
        nkv = kh * gkv_ref[...]
        kn = _mm(nkv, wk_ref[...])
        v_ref[...] = _mm(nkv, wv_ref[...])
        kr = _rope(kr_ref[...], cos, sa, sb, half)
        for h in range(N_HEADS):
            lo, mid, hi = h * MLA_QW, h * MLA_QW + LANES, (h + 1) * MLA_QW
            q_ref[:, lo:mid] = qp[:, lo:mid]
            q_ref[:, mid:hi] = _rope(qp[:, mid:hi], cos, sa, sb, half)
            k_ref[:, lo:mid] = kn[:, h * LANES:(h + 1) * LANES]
            k_ref[:, mid:hi] = kr

    row = lambda w, cb: pl.BlockSpec((rb, w), lambda i: (i, cb))
    whole = lambda a: pl.BlockSpec(a.shape, lambda i: (0,) * a.ndim)
    return pl.pallas_call(
        body, out_shape=(_SDS((s, N_HEADS * MLA_QW), f32), _SDS((s, N_HEADS * MLA_QW), f32), _SDS((s, GROUP), f32)),
        grid=(s // rb,),
        in_specs=[row(MLA_Q_RANK, _AL["mla_cq"] // MLA_Q_RANK), row(LANES, _AL["mla_ckv"] // LANES),
                  row(LANES, _AL["mla_k_rope"] // LANES), whole(gq), whole(gkv), whole(wuq), whole(wk), whole(wv),
                  row(LANES, 0), row(LANES, 0), row(LANES, 0)],
        out_specs=(row(N_HEADS * MLA_QW, 0), row(N_HEADS * MLA_QW, 0), row(GROUP, 0)),
        compiler_params=_cp(("parallel",)), name=name,
    )(z_all, z_all, z_all, gq, gkv, wuq, wk, wv, *tables)


def _mla_prep_bwd(z_all, gq, gkv, wuq, wk, wv, tables, dq_cat, dk_cat, dv, *, name):
    s = z_all.shape[0]
    rb = _row_block(s)
    half = MLA_ROPE // 2

    def body(cq_ref, ckv_ref, gq_ref, gkv_ref, wuq_ref, wk_ref, wv_ref, cos_ref, sa_ref, sb_ref, dq_ref, dk_ref,
             dv_ref, dcq_ref, dckv_ref, dkr_ref, dwuq_ref, dwk_ref, dwv_ref, dgq_ref, dgkv_ref):
        i = pl.program_id(0)

        @pl.when(i == 0)
        def _():
            for r in (dwuq_ref, dwk_ref, dwv_ref, dgq_ref, dgkv_ref):
                r[...] = jnp.zeros_like(r)

        cos, sa, sb = cos_ref[...], sa_ref[...], sb_ref[...]
        parts, knp = [], []
        dkr = jnp.zeros((rb, LANES), f32)
        for h in range(N_HEADS):
            lo, mid, hi = h * MLA_QW, h * MLA_QW + LANES, (h + 1) * MLA_QW
            parts += [dq_ref[:, lo:mid], _rope(dq_ref[:, mid:hi], cos, sa, sb, half, transpose=True)]
            knp.append(dk_ref[:, lo:mid])
            dkr = dkr + _rope(dk_ref[:, mid:hi], cos, sa, sb, half, transpose=True)
        dkr_ref[...] = dkr
        dqp = jnp.concatenate(parts, axis=1)
        dkn = jnp.concatenate(knp, axis=1)
        dvv = dv_ref[...]

        def norm_bwd(x_ref, g_ref, w_pairs, dx_ref, dg_ref):
            xh, r = _rms_rows(x_ref[...])
            nx = xh * g_ref[...]
            dn = jnp.zeros_like(xh)
            for w_ref, dw_ref, dy in w_pairs:
                dw_ref[...] += _mm_tn(nx, dy)
                dn = dn + _mm_nt(dy, w_ref[...])
            dxh = dn * g_ref[...]
            dx_ref[...] = r * (dxh - xh * jnp.mean(dxh * xh, axis=-1, keepdims=True))
            dg_ref[...] += jnp.sum(dn * xh, axis=0, keepdims=True)

        norm_bwd(cq_ref, gq_ref, [(wuq_ref, dwuq_ref, dqp)], dcq_ref, dgq_ref)
        norm_bwd(ckv_ref, gkv_ref, [(wk_ref, dwk_ref, dkn), (wv_ref, dwv_ref, dvv)], dckv_ref, dgkv_ref)

    row = lambda w, cb: pl.BlockSpec((rb, w), lambda i: (i, cb))
    whole = lambda a: pl.BlockSpec(a.shape, lambda i: (0,) * a.ndim)
    return pl.pallas_call(
        body,
        out_shape=(_SDS((s, MLA_Q_RANK), f32), _SDS((s, LANES), f32), _SDS((s, LANES), f32), _SDS(wuq.shape, f32),
                   _SDS(wk.shape, f32), _SDS(wv.shape, f32), _SDS(gq.shape, f32), _SDS(gkv.shape, f32)),
        grid=(s // rb,),
        in_specs=[row(MLA_Q_RANK, _AL["mla_cq"] // MLA_Q_RANK), row(LANES, _AL["mla_ckv"] // LANES), whole(gq),
                  whole(gkv), whole(wuq), whole(wk), whole(wv), row(LANES, 0), row(LANES, 0), row(LANES, 0),
                  row(N_HEADS * MLA_QW, 0), row(N_HEADS * MLA_QW, 0), row(GROUP, 0)],
        out_specs=(row(MLA_Q_RANK, 0), row(LANES, 0), row(LANES, 0), whole(wuq), whole(wk), whole(wv), whole(gq),
                   whole(gkv)),
        compiler_params=_cp(("arbitrary",)), name=name,
    )(z_all, z_all, gq, gkv, wuq, wk, wv, *tables, dq_cat, dk_cat, dv)


def _silu_grad(x):
    sg = _sigmoid(x)
    return sg * (1.0 + x * (1.0 - sg))


def _nsa_cmp_fwd(ra, rb_, pos, w1, w2, tables, *, name):
    nr = ra.shape[1]
    hw = ra.shape[2]

    def body(ra_ref, rb_ref, pos_ref, w1_ref, w2_ref, cos_ref, sa_ref, sb_ref, out_ref, hp_ref):
        for k in range(2):
            xa = ra_ref[k] + pos_ref[k, :, 0:hw]
            xb = rb_ref[k] + pos_ref[k, :, hw:2 * hw]
            hp = _mm(xa, w1_ref[k, 0:hw, :]) + _mm(xb, w1_ref[k, hw:2 * hw, :])
            hp_ref[k] = hp
            out = _mm(hp * _sigmoid(hp), w2_ref[k])
            if k == 0:
                out = _rope(out, cos_ref[...], sa_ref[...], sb_ref[...], HEAD_DIM // 2)
            out_ref[k] = out

    return pl.pallas_call(body, out_shape=(_SDS((2, nr, HEAD_DIM), f32), _SDS((2, nr, HEAD_DIM), f32)),
                          compiler_params=_cp(), name=name)(ra, rb_, pos, w1, w2, *tables)


def _nsa_cmp_bwd(ra, rb_, pos, w1, w2, tables, hp, dout, *, name):
    nr = ra.shape[1]
    hw = ra.shape[2]

    def body(ra_ref, rb_ref, pos_ref, w1_ref, w2_ref, cos_ref, sa_ref, sb_ref, hp_ref, do_ref,
             dxa_ref, dxb_ref, dw1_ref, dw2_ref):
        for k in range(2):
            d_out = do_ref[k]
            if k == 0:
                d_out = _rope(d_out, cos_ref[...], sa_ref[...], sb_ref[...], HEAD_DIM // 2, transpose=True)
            hpv = hp_ref[k]
            dw2_ref[k] = _mm_tn(hpv * _sigmoid(hpv), d_out)
            dhp = _mm_nt(d_out, w2_ref[k]) * _silu_grad(hpv)
            xa = ra_ref[k] + pos_ref[k, :, 0:hw]
            xb = rb_ref[k] + pos_ref[k, :, hw:2 * hw]
            dw1_ref[k, 0:hw, :] = _mm_tn(xa, dhp)
            dw1_ref[k, hw:2 * hw, :] = _mm_tn(xb, dhp)
            dxa_ref[k] = _mm_nt(dhp, w1_ref[k, 0:hw, :])
            dxb_ref[k] = _mm_nt(dhp, w1_ref[k, hw:2 * hw, :])

    return pl.pallas_call(
        body, out_shape=(_SDS((2, nr, hw), f32), _SDS((2, nr, hw), f32), _SDS(w1.shape, f32), _SDS(w2.shape, f32)),
        compiler_params=_cp(), name=name)(ra, rb_, pos, w1, w2, *tables, hp, dout)


def _nsa_consts(s):
    b = _attn_block(s)
    nr = s // CMP_STRIDE
    n_cmp = (s - CMP_LEN) // CMP_STRIDE + 1
    n_sel = s // SEL_LEN
    cmp_start = np.arange(n_cmp) * CMP_STRIDE
    sel_start = np.arange(n_sel) * SEL_LEN
    overlap = np.clip(np.minimum(cmp_start[:, None] + CMP_LEN, sel_start[None, :] + SEL_LEN)
                      - np.maximum(cmp_start[:, None], sel_start[None, :]), 0, None)
    m2s = np.zeros((nr, LANES), np.float32)
    m2s[:n_cmp, :n_sel] = overlap / CMP_LEN
    e3 = np.zeros((s // b, LANES, b), np.float32)
    tok = np.arange(s)
    e3[tok // b, tok // SEL_LEN, tok % b] = 1.0
    return jnp.asarray(m2s, _MXU), jnp.asarray(e3, _MXU)


def _nsa_masks(i, b, d):
    qpos = i * b + _iota((b, b), 0)
    kpos = (i - d) * b + _iota((b, b), 1)
    return (kpos <= qpos) & (kpos > qpos - WINDOW)


def _nsa_fwd(qr, kvc, ksr, vs, kwr, vw, z_all, m2s, e3, *, name):
    s = qr.shape[0]
    b = _attn_block(s)
    nq = s // b
    nr = kvc.shape[1]
    n_sel = s // SEL_LEN
    top_n = min(SEL_TOPN, n_sel)
    nd = -(-WINDOW // b)
    scale = HEAD_DIM ** -0.5
    bcol = _AL["nsa_branch"] // LANES
    H = N_HEADS

    def body(q_ref, kvc_ref, ks_ref, vs_ref, kw_ref, vw_ref, br_ref, m2s_ref, e3_ref,
             o_ref, oc_ref, os_ref, ow_ref, st_ref, sel_ref, m_sc, l_sc, acc_sc):
        i = pl.program_id(0)
        lane = _iota((b, LANES), 1)
        hs = lambda h: slice(h * HEAD_DIM, (h + 1) * HEAD_DIM)

        cmp_mask = (CMP_STRIDE * _iota((b, nr), 1) + (CMP_LEN - 1)) <= (i * b + _iota((b, nr), 0))
        imp = jnp.zeros((b, LANES), f32)
        stats = jnp.zeros((b, LANES), f32)
        for h in range(H):
            zc = jnp.where(cmp_mask, _mm_nt(q_ref[:, hs(h)], kvc_ref[0]) * scale, NEG_INF)
            m = jnp.max(zc, axis=1, keepdims=True)
            p = jnp.where(cmp_mask, jnp.exp(zc - m), 0.0)
            l = jnp.sum(p, axis=1, keepdims=True)
            some = l > 0.0
            lsafe = jnp.where(some, l, 1.0)
            pc = p * jnp.where(some, 1.0 / lsafe, 0.0)
            oc_ref[:, hs(h)] = _mm(pc, kvc_ref[1])
            imp = imp + _mm(pc, m2s_ref[...])
            stats = jnp.where(lane == h, jnp.where(some, m + jnp.log(lsafe), 0.0), stats)

        cur = jnp.right_shift(i * b + _iota((b, LANES), 0), int(math.log2(SEL_LEN)))
        forced = (lane == 0) | (lane == cur) | (lane == cur - 1)
        score = jnp.where(lane <= cur, jnp.where(forced, FORCED_BONUS, imp), NEG_INF)
        score = jnp.where(lane < n_sel, score, -3e38)
        rank = jnp.zeros((b, LANES), f32)
        for j in range(n_sel):
            col = score[:, j:j + 1]
            rank = rank + jnp.where(col > score, 1.0, jnp.where(col == score, jnp.where(lane > j, 1.0, 0.0), 0.0))
        sel = jnp.where(lane < n_sel, jnp.where(rank < top_n, 1.0, 0.0), 0.0)
        sel_ref[...] = sel
        sel_b = sel.astype(_MXU)

        def reset():
            m_sc[...] = jnp.full(m_sc.shape, NEG_INF, f32)
            l_sc[...] = jnp.zeros_like(l_sc)
            acc_sc[...] = jnp.zeros_like(acc_sc)

        def update(h, z, mask, vch):
            zm = jnp.where(mask, z, NEG_INF)
            m_old = m_sc[h]
            m_new = jnp.maximum(m_old, jnp.max(zm, axis=1, keepdims=True))
            p = jnp.where(mask, jnp.exp(zm - m_new), 0.0)
            alpha = jnp.exp(m_old - m_new)
            l_sc[h] = alpha * l_sc[h] + jnp.sum(p, axis=1, keepdims=True)
            acc_sc[h] = alpha * acc_sc[h] + _mm(p, vch)
            m_sc[h] = m_new

        def finish(out_ref, branch, stats):
            for h in range(H):
                out_ref[:, hs(h)] = acc_sc[h] / l_sc[h]
                stats = jnp.where(lane == 4 * branch + h, m_sc[h] + jnp.log(l_sc[h]), stats)
            return stats

        def sel_chunk(c, diag):
            st = pl.multiple_of(c * b, b)
            mask = _mm(sel_b, e3_ref[c]) > 0.5
            if diag:
                mask = mask & _lower_mask(b, False)
            kch, vch = ks_ref[pl.ds(st, b), :], vs_ref[pl.ds(st, b), :]
            for h in range(H):
                update(h, _mm_nt(q_ref[:, hs(h)], kch) * scale, mask, vch)

        reset()

        def sel_loop(c, carry):
            sel_chunk(c, False)
            return carry

        lax.fori_loop(0, i, sel_loop, 0)
        sel_chunk(i, True)
        stats = finish(os_ref, 1, stats)

        reset()
        for d in range(nd, -1, -1):
            @pl.when(i >= d)
            def _():
                st = pl.multiple_of((i - d) * b, b)
                mask = _nsa_masks(i, b, d)
                kch, vch = kw_ref[pl.ds(st, b), :], vw_ref[pl.ds(st, b), :]
                for h in range(H):
                    update(h, _mm_nt(q_ref[:, hs(h)], kch) * scale, mask, vch)
        stats = finish(ow_ref, 2, stats)
        st_ref[...] = stats

        g = _sigmoid(br_ref[...])
        for h in range(H):
            o_ref[:, hs(h)] = (g[:, 3 * h:3 * h + 1] * oc_ref[:, hs(h)] + g[:, 3 * h + 1:3 * h + 2] * os_ref[:, hs(h)]
                               + g[:, 3 * h + 2:3 * h + 3] * ow_ref[:, hs(h)])

    blk = lambda w: pl.BlockSpec((b, w), lambda i: (i, 0))
    whole = lambda a: pl.BlockSpec(a.shape, lambda i: (0,) * a.ndim)
    return pl.pallas_call(
        body, out_shape=tuple(_SDS((s, GROUP), f32) for _ in range(4)) + (_SDS((s, LANES), f32), _SDS((s, LANES), f32)),
        grid=(nq,),
        in_specs=[blk(GROUP), whole(kvc), whole(ksr), whole(vs), whole(kwr), whole(vw),
                  pl.BlockSpec((b, LANES), lambda i: (i, bcol)), whole(m2s), whole(e3)],
        out_specs=(blk(GROUP),) * 4 + (blk(LANES), blk(LANES)),
        scratch_shapes=[pltpu.VMEM((H, b, 1), f32), pltpu.VMEM((H, b, 1), f32), pltpu.VMEM((H, b, HEAD_DIM), f32)],
        compiler_params=_cp(("parallel",)), name=name,
    )(qr, kvc, ksr, vs, kwr, vw, z_all, m2s, e3)


def _nsa_bwd(do, qr, kvc, ksr, vs, kwr, vw, z_all, oc, os_, ow, stats, sel, e3, *, name):
    s = qr.shape[0]
    b = _attn_block(s)
    nq = s // b
    nr = kvc.shape[1]
    nd = -(-WINDOW // b)
    scale = HEAD_DIM ** -0.5
    bcol = _AL["nsa_branch"] // LANES
    H = N_HEADS

    def body(do_ref, q_ref, kvc_ref, ks_ref, vs_ref, kw_ref, vw_ref, br_ref, oc_ref, os_ref, ow_ref, st_ref, sel_ref,
             e3_ref, dq_ref, dbr_ref, dkvc_ref, dks_ref, dvs_ref, dkw_ref, dvw_ref, dob_sc, delta_sc, dq_sc):
        i = pl.program_id(0)

        @pl.when(i == 0)
        def _():
            for r in (dkvc_ref, dks_ref, dvs_ref, dkw_ref, dvw_ref):
                r[...] = jnp.zeros_like(r)

        lane = _iota((b, LANES), 1)
        hs = lambda h: slice(h * HEAD_DIM, (h + 1) * HEAD_DIM)
        g = _sigmoid(br_ref[...])
        stats = st_ref[...]
        dbr = jnp.zeros((b, LANES), f32)
        outs = (oc_ref, os_ref, ow_ref)
        for h in range(H):
            doh = do_ref[:, hs(h)]
            for j in range(3):
                gj = g[:, 3 * h + j:3 * h + j + 1]
                dgj = jnp.sum(doh * outs[j][:, hs(h)], axis=1, keepdims=True)
                dbr = jnp.where(lane == 3 * h + j, dgj * gj * (1.0 - gj), dbr)
                dob_sc[j, :, hs(h)] = gj * doh
                delta_sc[j, h] = gj * dgj
        dbr_ref[...] = dbr
        dq_sc[...] = jnp.zeros_like(dq_sc)

        def branch(j, h, z, mask, kch, vch):
            qh = q_ref[:, hs(h)]
            p = jnp.where(mask, jnp.exp(jnp.where(mask, z, NEG_INF) - stats[:, 4 * j + h:4 * j + h + 1]), 0.0)
            dob = dob_sc[j, :, hs(h)]
            ds = p * (_mm_nt(dob, vch) - delta_sc[j, h])
            dq_sc[:, hs(h)] += _mm(ds, kch) * scale
            return _mm_tn(ds, qh) * scale, _mm_tn(p, dob)

        cmp_mask = (CMP_STRIDE * _iota((b, nr), 1) + (CMP_LEN - 1)) <= (i * b + _iota((b, nr), 0))
        kc, vc = kvc_ref[0], kvc_ref[1]
        for h in range(H):
            dk, dv = branch(0, h, _mm_nt(q_ref[:, hs(h)], kc) * scale, cmp_mask, kc, vc)
            dkvc_ref[0] += dk
            dkvc_ref[1] += dv

        sel_b = sel_ref[...].astype(_MXU)

        def chunk(j, c, mask, k_ref, v_ref, dk_ref, dv_ref):
            st = pl.multiple_of(c * b, b)
            kch, vch = k_ref[pl.ds(st, b), :], v_ref[pl.ds(st, b), :]
            dk = jnp.zeros((b, HEAD_DIM), f32)
            dv = jnp.zeros((b, HEAD_DIM), f32)
            for h in range(H):
                dkh, dvh = branch(j, h, _mm_nt(q_ref[:, hs(h)], kch) * scale, mask, kch, vch)
                dk, dv = dk + dkh, dv + dvh
            dk_ref[pl.ds(st, b), :] += dk
            dv_ref[pl.ds(st, b), :] += dv

        def sel_chunk(c, diag):
            mask = _mm(sel_b, e3_ref[c]) > 0.5
            if diag:
                mask = mask & _lower_mask(b, False)
            chunk(1, c, mask, ks_ref, vs_ref, dks_ref, dvs_ref)

        def sel_loop(c, carry):
            sel_chunk(c, False)
            return carry

        lax.fori_loop(0, i, sel_loop, 0)
        sel_chunk(i, True)

        for d in range(nd, -1, -1):
            @pl.when(i >= d)
            def _():
                chunk(2, i - d, _nsa_masks(i, b, d), kw_ref, vw_ref, dkw_ref, dvw_ref)

        dq_ref[...] = dq_sc[...]

    blk = lambda w: pl.BlockSpec((b, w), lambda i: (i, 0))
    whole = lambda a: pl.BlockSpec(a.shape, lambda i: (0,) * a.ndim)
    stream = _SDS((s, HEAD_DIM), f32)
    return pl.pallas_call(
        body, out_shape=(_SDS((s, GROUP), f32), _SDS((s, LANES), f32), _SDS(kvc.shape, f32), stream, stream, stream,
                         stream),
        grid=(nq,),
        in_specs=[blk(GROUP), blk(GROUP), whole(kvc), whole(ksr), whole(vs), whole(kwr), whole(vw),
                  pl.BlockSpec((b, LANES), lambda i: (i, bcol)), blk(GROUP), blk(GROUP), blk(GROUP), blk(LANES),
                  blk(LANES), whole(e3)],
        out_specs=(blk(GROUP), blk(LANES), whole(kvc), whole(ksr), whole(vs), whole(kwr), whole(vw)),
        scratch_shapes=[pltpu.VMEM((3, b, GROUP), f32), pltpu.VMEM((3, H, b, 1), f32), pltpu.VMEM((b, GROUP), f32)],
        compiler_params=_cp(("arbitrary",)), name=name,
    )(do, qr, kvc, ksr, vs, kwr, vw, z_all, oc, os_, ow, stats, sel, e3)


def _seg(a, name):
    parts = [lax.slice_in_dim(a, off, off + hi - lo, axis=a.ndim - 1) for off, lo, hi in _PIECES[name]]
    return parts[0] if len(parts) == 1 else jnp.concatenate(parts, axis=a.ndim - 1)


def _to_groups(segs, rows, dtype):
    cols = []
    for s, grp in enumerate(_GROUPS):
        at = 0
        for n, lo, hi, off in sorted(grp, key=lambda t: t[3]):
            if off > at:
                cols.append(jnp.zeros((rows, off - at), dtype))
            cols.append(segs[n][:, lo:hi].astype(dtype))
            at = off + hi - lo
        if at < GROUP_W:
            cols.append(jnp.zeros((rows, GROUP_W - at), dtype))
    return jnp.concatenate(cols, axis=1)


def _piece_from_shard(w_t, s):
    grp = sorted(_GROUPS[s], key=lambda t: t[3])
    ends = [t[3] for t in grp[1:]] + [GROUP_W]
    rows = []
    for (n, lo, hi, off), end in zip(grp, ends):
        first = _ORIG[n] + lo - s * CHIP_COLS
        rows.append(jnp.pad(w_t[:, first:first + hi - lo], ((0, 0), (0, end - off - (hi - lo)), (0, 0))))
    return jnp.concatenate(rows, axis=1)


def _shard_from_piece(g, s):
    return jnp.concatenate([g[:, off:off + hi - lo] for n, lo, hi, off in
                            sorted(_GROUPS[s], key=lambda t: _ORIG[t[0]] + t[1])], axis=1)


def _from_groups(a):
    return jnp.concatenate([_seg(a, n) for n, _ in _SEGS], axis=1)


def _cmp_rows(tok):
    s = tok.shape[0]
    r = tok.reshape(s // CMP_STRIDE, CMP_STRIDE * HEAD_DIM)
    return r, jnp.concatenate([r[1:], jnp.zeros((1, r.shape[1]), r.dtype)], axis=0)


def _cmp_unrows(dxa, dxb):
    s = dxa.shape[0] * CMP_STRIDE
    return (dxa + jnp.concatenate([jnp.zeros((1, dxa.shape[1]), dxa.dtype), dxb[:-1]], axis=0)).reshape(s, HEAD_DIM)


_GATES = ("sb_gate", "nsa_gate", "fox_gate", "mla_gate")


def _layer_fwd(x, p, c, tag):
    s = x.shape[0]
    b = _attn_block(s)
    h = _rms_fwd(x, p["pre_g"], out_dtype=_MXU, name=f"prenorm_{tag}")
    z = _matmul(h, p["w_in"], "nt", bias=p["b_in"], name=f"inproj_{tag}")
    o_sb = _sb_fwd(z, hp=HP_FWD, name=f"sb_fwd_{tag}")

    qr, ksr, kwr = _rope_call([(z, GROUP, _AL["nsa_q"] // GROUP), (z, LANES, _AL["nsa_k_sel"] // LANES),
                               (z, LANES, _AL["nsa_k_win"] // LANES)], c["tabs128"], HEAD_DIM // 2, False,
                              name=f"nsa_rope_{tag}")
    (rak, rbk), (rav, rbv) = _cmp_rows(_seg(z, "nsa_k_cmp")), _cmp_rows(_seg(z, "nsa_v_cmp"))
    ra, rb_ = jnp.stack([rak, rav]), jnp.stack([rbk, rbv])
    kvc, hp = _nsa_cmp_fwd(ra, rb_, p["cmp_pos"], p["cmp_w1"], p["cmp_w2"], c["tabs_cmp"], name=f"nsa_cmp_{tag}")
    vs, vw = _seg(z, "nsa_v_sel"), _seg(z, "nsa_v_win")
    o_nsa, oc, os_, ow, stats, sel = _nsa_fwd(qr, kvc, ksr, vs, kwr, vw, z, c["m2s"], c["e3"], name=f"nsa_fwd_{tag}")

    cum, cum_t8 = _fox_cum_fwd(z, p["fox_bias"], name=f"fox_cum_{tag}")
    cum_t = cum_t8.reshape(8, s // b, 1, b)
    fox_v = _seg(z, "fox_v")
    fcols = (_AL["fox_q"] // HEAD_DIM, _AL["fox_k"] // HEAD_DIM, 0)
    o_fox, lse_fox = _attn_fwd(z, z, fox_v, *fcols, HEAD_DIM, cum, cum_t, scale=HEAD_DIM ** -0.5, hp=HP_FWD,
                               name=f"fox_fwd_{tag}")

    qcat, kcat, vm = _mla_prep_fwd(z, p["gq"], p["gkv"], p["wuq"], p["wk"], p["wv"], c["tabs64"],
                                   name=f"mla_prep_{tag}")
    o_mla, lse_mla = _attn_fwd(qcat, kcat, vm, 0, 0, 0, MLA_QW, None, None, scale=(MLA_NOPE + MLA_ROPE) ** -0.5,
                               hp=HP_BWD, name=f"mla_fwd_{tag}")

    o_all = (o_sb, o_nsa, o_fox, o_mla)
    gates = jnp.concatenate([_seg(z, n) for n in _GATES], axis=1)
    mix = _gate_fwd(o_all, gates, name=f"gate_{tag}")
    u = _matmul(mix, p["w_out"], "nn", name=f"outproj_{tag}")
    y = _postnorm_fwd(u, p["post_g"], x, name=f"postnorm_{tag}")
    saved = dict(x=x, h=h, z=z, qr=qr, ksr=ksr, kwr=kwr, ra=ra, rb=rb_, kvc=kvc, hp=hp, vs=vs, vw=vw, oc=oc, os=os_,
                 ow=ow, stats=stats, sel=sel, cum=cum, cum_t=cum_t, fox_v=fox_v, o_fox=o_fox, lse_fox=lse_fox, qcat=qcat, kcat=kcat,
                 vm=vm, o_mla=o_mla, lse_mla=lse_mla, o_all=o_all, gates=gates, mix=mix, u=u)
    return y, saved


def _layer_bwd(dy, sv, p, c, tag, dw_dtype=f32):
    z = sv["z"]
    s = z.shape[0]
    du, dg_post = _rms_bwd(dy, sv["u"], p["post_g"], name=f"postnorm_bwd_{tag}")
    dmix = _matmul(du, p["w_out"], "nt", name=f"outproj_dx_{tag}")
    dw_out = _matmul(sv["mix"], du, "tn", name=f"outproj_dw_{tag}")
    do_sb, do_nsa, do_fox, do_mla, dgates = _gate_bwd(dmix, sv["o_all"], sv["gates"], name=f"gate_bwd_{tag}")
    dgate = [dgates[:, k * GROUP:(k + 1) * GROUP] for k in range(4)]

    sb_dq, sb_dk, sb_dv = _sb_bwd(z, do_sb, hp=HP_BWD, name=f"sb_bwd_{tag}")

    n_dq, n_dbr, n_dkvc, n_dks, n_dvs, n_dkw, n_dvw = _nsa_bwd(
        do_nsa, sv["qr"], sv["kvc"], sv["ksr"], sv["vs"], sv["kwr"], sv["vw"], z, sv["oc"], sv["os"], sv["ow"],
        sv["stats"], sv["sel"], c["e3"], name=f"nsa_bwd_{tag}")
    dxa, dxb, dw1, dw2 = _nsa_cmp_bwd(sv["ra"], sv["rb"], p["cmp_pos"], p["cmp_w1"], p["cmp_w2"], c["tabs_cmp"],
                                      sv["hp"], n_dkvc, name=f"nsa_cmp_bwd_{tag}")
    n_dq, n_dks, n_dkw = _rope_call([(n_dq, GROUP, 0), (n_dks, LANES, 0), (n_dkw, LANES, 0)], c["tabs128"],
                                    HEAD_DIM // 2, True, name=f"nsa_rope_bwd_{tag}")
    dpos = _colsum(jnp.concatenate([dxa[0], dxb[0], dxa[1], dxb[1]], axis=1), name=f"nsa_dpos_{tag}")
    flat = CMP_LEN * HEAD_DIM

    fcols = (_AL["fox_q"] // HEAD_DIM, _AL["fox_k"] // HEAD_DIM, 0)
    f_dq, f_dk, f_dv, f_dck = _attn_bwd(z, z, sv["fox_v"], *fcols, HEAD_DIM, do_fox, sv["o_fox"], sv["lse_fox"],
                                        sv["cum"], sv["cum_t"], scale=HEAD_DIM ** -0.5, hp=HP_BWD,
                                        name=f"fox_bwd_{tag}")
    dcum_t = jnp.pad(f_dck.reshape(N_HEADS, s), ((0, 8 - N_HEADS), (0, 0)))
    f_df, f_dbias = _fox_cum_bwd(z, p["fox_bias"], dcum_t, name=f"fox_cum_bwd_{tag}")

    m_dq, m_dk, m_dv = _attn_bwd(sv["qcat"], sv["kcat"], sv["vm"], 0, 0, 0, MLA_QW, do_mla, sv["o_mla"], sv["lse_mla"],
                                 None, None, scale=(MLA_NOPE + MLA_ROPE) ** -0.5, hp=HP_BWD, name=f"mla_bwd_{tag}")
    m_dcq, m_dckv, m_dkr, m_dwuq, m_dwk, m_dwv, m_dgq, m_dgkv = _mla_prep_bwd(
        z, p["gq"], p["gkv"], p["wuq"], p["wk"], p["wv"], c["tabs64"], m_dq, m_dk, m_dv, name=f"mla_prep_bwd_{tag}")

    dz = _to_groups(dict(
        sb_q=sb_dq, sb_k=sb_dk, sb_v=sb_dv, sb_gate=dgate[0], nsa_q=n_dq, nsa_k_cmp=_cmp_unrows(dxa[0], dxb[0]),
        nsa_v_cmp=_cmp_unrows(dxa[1], dxb[1]), nsa_k_sel=n_dks, nsa_v_sel=n_dvs, nsa_k_win=n_dkw, nsa_v_win=n_dvw,
        nsa_branch=n_dbr, nsa_gate=dgate[1], fox_q=f_dq, fox_k=f_dk, fox_v=f_dv, fox_f=f_df, fox_gate=dgate[2],
        mla_cq=m_dcq, mla_ckv=m_dckv, mla_k_rope=m_dkr, mla_gate=dgate[3]), s, _MXU)
    dh = _matmul(dz, p["w_in"], "nn", name=f"inproj_dx_{tag}")
    dw_in = _matmul(dz, sv["h"], "tn", out_dtype=dw_dtype, name=f"inproj_dw_{tag}")
    db = _colsum(dz, name=f"inproj_db_{tag}")
    dx, dg_pre = _rms_bwd(dh, sv["x"], p["pre_g"], res=dy, name=f"prenorm_bwd_{tag}")

    qw = MLA_NOPE + MLA_ROPE
    grads = {
        "pre_norm_g": dg_pre[0], "post_norm_g": dg_post[0], "w_in": dw_in, "b_in": _from_groups(db)[0],
        "w_out": dw_out, "fox_forget_bias": f_dbias[0, :N_HEADS],
        "nsa_cmp_pos_k": dpos[0, :flat].reshape(CMP_LEN, HEAD_DIM), "nsa_cmp_w1_k": dw1[0], "nsa_cmp_w2_k": dw2[0],
        "nsa_cmp_pos_v": dpos[0, flat:].reshape(CMP_LEN, HEAD_DIM), "nsa_cmp_w1_v": dw1[1], "nsa_cmp_w2_v": dw2[1],
        "mla_q_norm_g": m_dgq[0],
        "mla_w_uq": jnp.concatenate([m_dwuq[:, MLA_QW * h:MLA_QW * h + qw] for h in range(N_HEADS)], axis=1),
        "mla_kv_norm_g": m_dgkv[0],
        "mla_w_ukv": jnp.concatenate(sum([[m_dwk[:, LANES * h:LANES * (h + 1)], m_dwv[:, LANES * h:LANES * (h + 1)]]
                                          for h in range(N_HEADS)], []), axis=1),
    }
    return dx, grads


def _layer_params(w, l):
    b_in = w["b_in"][l].reshape(1, -1)
    b_segs = {n: b_in[:, _ORIG[n]:_ORIG[n] + wd] for n, wd in _SEGS}
    qw = MLA_NOPE + MLA_ROPE
    w_uq, w_ukv = w["mla_w_uq"][l], w["mla_w_ukv"][l]
    uq = []
    for h in range(N_HEADS):
        uq += [w_uq[:, qw * h:qw * (h + 1)], jnp.zeros((w_uq.shape[0], MLA_QW - qw), w_uq.dtype)]
    kw_ = 2 * LANES
    flat = CMP_LEN * HEAD_DIM
    return dict(
        pre_g=w["pre_norm_g"][l].reshape(1, -1), post_g=w["post_norm_g"][l].reshape(1, -1),
        w_in=w["w_in"][l], b_in=_to_groups(b_segs, 1, f32), w_out=w["w_out"][l],
        fox_bias=jnp.pad(w["fox_forget_bias"][l], (0, LANES - N_HEADS)).reshape(1, LANES),
        cmp_pos=jnp.stack([w["nsa_cmp_pos_k"][l].reshape(1, flat), w["nsa_cmp_pos_v"][l].reshape(1, flat)]),
        cmp_w1=jnp.stack([w["nsa_cmp_w1_k"][l], w["nsa_cmp_w1_v"][l]]),
        cmp_w2=jnp.stack([w["nsa_cmp_w2_k"][l], w["nsa_cmp_w2_v"][l]]),
        gq=w["mla_q_norm_g"][l].reshape(1, -1), gkv=w["mla_kv_norm_g"][l].reshape(1, -1),
        wuq=jnp.concatenate(uq, axis=1),
        wk=jnp.concatenate([w_ukv[:, kw_ * h:kw_ * h + LANES] for h in range(N_HEADS)], axis=1),
        wv=jnp.concatenate([w_ukv[:, kw_ * h + LANES:kw_ * (h + 1)] for h in range(N_HEADS)], axis=1),
    )


def _consts(s):
    pos = jnp.arange(s)
    m2s, e3 = _nsa_consts(s)
    return dict(tabs128=_rope_tables(pos, HEAD_DIM), tabs64=_rope_tables(pos, MLA_ROPE),
                tabs_cmp=_rope_tables(jnp.arange(s // CMP_STRIDE) * CMP_STRIDE + (CMP_LEN - 1), HEAD_DIM),
                m2s=m2s, e3=e3)


def _place():
    return lax.axis_index("x"), lax.axis_index("y"), lax.axis_index("c")


def _other_chips(x, y):
    return [(1 - x, y), (x, 1 - y), (1 - x, 1 - y)]


def _comm_call(body, out_shapes, n_sems, arrs, name):
    return pl.pallas_call(body, out_shape=tuple(out_shapes), in_specs=[_ANY] * len(arrs),
                          out_specs=tuple(_ANY for _ in out_shapes),
                          scratch_shapes=[pltpu.SemaphoreType.DMA((n_sems,)), pltpu.SemaphoreType.DMA((n_sems,))],
                          name=name)(*arrs)


def _gather_chips(arrs, *, name):
    n = len(arrs)

    def body(*refs):
        a_refs, out_refs, send_sems, recv_sems = refs[:n], refs[n:2 * n], refs[2 * n], refs[2 * n + 1]
        x, y, c = _place()
        me = 2 * x + y
        sibling = (x, y, 1 - c)
        chips = _other_chips(x, y)

        def copy(j, k, src, dst, to):
            return pltpu.make_async_remote_copy(src, dst, send_sems.at[6 * j + k], recv_sems.at[6 * j + k],
                                                device_id=to, device_id_type=_MESH)

        first = [copy(j, k, a_refs[j].at[c], out_refs[j].at[me, c], (px, py, c))
                 for k, (px, py) in enumerate(chips) for j in range(n)]
        for cp in first:
            cp.start()
        passed = []
        for k, (px, py) in enumerate(chips):
            for j in range(n):
                landed = out_refs[j].at[2 * px + py, c]
                copy(j, k, a_refs[j].at[c], landed, (px, py, c)).wait_recv()
                passed.append(copy(j, 3 + k, landed, landed, sibling))
                passed[-1].start()
        for k, (px, py) in enumerate(chips):
            for j in range(n):
                copy(j, 3 + k, a_refs[j].at[c], out_refs[j].at[2 * px + py, 1 - c], sibling).wait_recv()
        for cp in first + passed:
            cp.wait_send()

    return _comm_call(body, [_SDS((N_CHIPS,) + a.shape, a.dtype) for a in arrs], 6 * n, arrs, name)


def _alltoall_chips(arrs, modes, *, name):
    n = len(arrs)
    slot = lambda ref, mode, s: _slot_ref(ref, mode, s)
    lane_slots = modes

    def body(*refs):
        g_refs, out_refs, send_sems, recv_sems = refs[:n], refs[n:2 * n], refs[2 * n], refs[2 * n + 1]
        x, y, c = _place()
        me = 2 * x + y

        def copy(j, s):
            return pltpu.make_async_remote_copy(slot(g_refs[j], lane_slots[j], s), out_refs[j].at[me],
                                                send_sems.at[N_CHIPS * j + s], recv_sems.at[N_CHIPS * j + me],
                                                device_id=(s // 2, s % 2, c), device_id_type=_MESH)

        for s in range(N_CHIPS):
            @pl.when(s != me)
            def _():
                for j in range(n):
                    copy(j, s).start()
        for t in range(N_CHIPS):
            @pl.when(t != me)
            def _():
                for j in range(n):
                    pltpu.make_async_remote_copy(slot(g_refs[j], lane_slots[j], t), out_refs[j].at[t],
                                                 send_sems.at[N_CHIPS * j + t], recv_sems.at[N_CHIPS * j + t],
                                                 device_id=(t // 2, t % 2, c), device_id_type=_MESH).wait_recv()
        for s in range(N_CHIPS):
            @pl.when(s != me)
            def _():
                for j in range(n):
                    copy(j, s).wait_send()

    outs = [_SDS((N_CHIPS,) + _slot_shape(a, m), a.dtype) for a, m in zip(arrs, modes)]
    return _comm_call(body, outs, N_CHIPS * n, arrs, name)


def _swap_other_half(arrs, *, name):
    n = len(arrs)

    def body(*refs):
        g_refs, out_refs, send_sems, recv_sems = refs[:n], refs[n:2 * n], refs[2 * n], refs[2 * n + 1]
        x, y, c = _place()
        cps = [pltpu.make_async_remote_copy(g_refs[j].at[:, 1 - c], out_refs[j], send_sems.at[j], recv_sems.at[j],
                                            device_id=(x, y, 1 - c), device_id_type=_MESH) for j in range(n)]
        for cp in cps:
            cp.start()
        for cp in cps:
            cp.wait()

    return _comm_call(body, [_SDS((a.shape[0],) + a.shape[2:], a.dtype) for a in arrs], n, arrs, name)


def _swap_sibling(arrs, *, name):
    n = len(arrs)

    def body(*refs):
        f_refs, out_refs, send_sems, recv_sems = refs[:n], refs[n:2 * n], refs[2 * n], refs[2 * n + 1]
        x, y, c = _place()
        cps = [pltpu.make_async_remote_copy(f_refs[j], out_refs[j], send_sems.at[j], recv_sems.at[j],
                                            device_id=(x, y, 1 - c), device_id_type=_MESH) for j in range(n)]
        for cp in cps:
            cp.start()
        for cp in cps:
            cp.wait()

    return _comm_call(body, [_SDS(a.shape, a.dtype) for a in arrs], n, arrs, name)


_HBM = pl.BlockSpec(memory_space=pltpu.HBM)
_SEM = pl.BlockSpec(memory_space=pltpu.SEMAPHORE)
_EFFECT = pltpu.SideEffectType.DATAFLOW_SIDE_EFFECTING


def _slot_ref(ref, mode, s):
    return ref if mode == "same" else ref.at[s]


def _slot_shape(a, mode):
    return a.shape if mode == "same" else a.shape[1:]


def _send_start(arrs, modes, after, *, name):
    n = len(arrs)
    lands = [lax.empty((N_CHIPS,) + _slot_shape(a, m), a.dtype) for a, m in zip(arrs, modes)]

    def body(*refs):
        srcs, land_refs, send_sems, recv_sems, token = refs[:n], refs[n:2 * n], refs[2 * n + 1], refs[2 * n + 2], refs[-1]
        x, y, c = _place()
        me = 2 * x + y
        for s in range(N_CHIPS):
            @pl.when(s != me)
            def _():
                for j in range(n):
                    pltpu.make_async_remote_copy(_slot_ref(srcs[j], modes[j], s), land_refs[j].at[me],
                                                 send_sems.at[N_CHIPS * j + s], recv_sems.at[N_CHIPS * j + me],
                                                 device_id=(s // 2, s % 2, c), device_id_type=_MESH).start()
        token[...] = jnp.zeros_like(token)

    hbm = lambda a: pltpu.HBM(a.shape, a.dtype)
    sems = pltpu.SemaphoreType.DMA((N_CHIPS * n,))
    out = pl.pallas_call(
        body, name=name, out_shape=(sems, sems, *[hbm(a) for a in arrs], *[hbm(a) for a in lands], _SDS((8, LANES), f32)),
        in_specs=[_HBM] * (2 * n) + [_ANY], out_specs=(_SEM, _SEM, *[_HBM] * (2 * n), pl.BlockSpec(memory_space=pltpu.VMEM)),
        input_output_aliases={j: 2 + j for j in range(2 * n)},
        compiler_params=pltpu.CompilerParams(has_side_effects=_EFFECT),
    )(*[pltpu.with_memory_space_constraint(a, pltpu.HBM) for a in arrs + lands], after)
    return out[:-1], out[-1]


def _send_wait(started, modes, after, *, name):
    send_sems, recv_sems = started[0], started[1]
    n = (len(started) - 2) // 2
    thru = list(started[2:])

    def body(*refs):
        srcs, land_refs, send_sems, recv_sems = refs[:n], refs[n:2 * n], refs[2 * n], refs[2 * n + 1]
        x, y, c = _place()
        me = 2 * x + y
        for s in range(N_CHIPS):
            @pl.when(s != me)
            def _():
                for j in range(n):
                    cp = pltpu.make_async_remote_copy(_slot_ref(srcs[j], modes[j], s), land_refs[j].at[s],
                                                      send_sems.at[N_CHIPS * j + s], recv_sems.at[N_CHIPS * j + s],
                                                      device_id=(s // 2, s % 2, c), device_id_type=_MESH)
                    cp.wait_send()
                    cp.wait_recv()

    hbm = lambda a: pltpu.HBM(a.shape, a.dtype)
    out = pl.pallas_call(
        body, name=name, out_shape=tuple(hbm(a) for a in thru), in_specs=[_HBM] * (2 * n) + [_SEM, _SEM, _ANY],
        out_specs=tuple([_HBM] * (2 * n)), input_output_aliases={j: j for j in range(2 * n)},
        compiler_params=pltpu.CompilerParams(has_side_effects=_EFFECT),
    )(*thru, send_sems, recv_sems, after)
    return list(out[n:])


def _add_my_half(g, r, *, name):
    p, _, h, w = g.shape
    tw = _pick(w, (2048, 1024, 512, 256, 128))
    rb = max(d for d in range(16, h + 1, 16) if h % d == 0 and d * tw * 4 <= (2 << 20))

    def body(c_ref, g_ref, r_ref, o_ref):
        o_ref[...] = (g_ref[...].astype(f32) + r_ref[...].astype(f32)).astype(o_ref.dtype)

    blk = pl.BlockSpec((None, rb, tw), lambda s, i, j, c_ref: (s, i, j))
    grid_spec = pltpu.PrefetchScalarGridSpec(
        num_scalar_prefetch=1, grid=(p, h // rb, w // tw),
        in_specs=[pl.BlockSpec((None, None, rb, tw), lambda s, i, j, c_ref: (s, c_ref[0], i, j)), blk], out_specs=blk)
    c = lax.axis_index("c").astype(jnp.int32).reshape(1)
    return pl.pallas_call(body, out_shape=_SDS((p, h, w), _WIRE), grid_spec=grid_spec,
                          compiler_params=_cp(("parallel", "parallel", "parallel")), name=name)(c, g, r)


_WEIGHTS = ("pre_norm_g", "post_norm_g", "w_in", "b_in", "w_out", "fox_forget_bias", "nsa_cmp_pos_k", "nsa_cmp_w1_k",
            "nsa_cmp_w2_k", "nsa_cmp_pos_v", "nsa_cmp_w1_v", "nsa_cmp_w2_v", "mla_q_norm_g", "mla_w_uq",
            "mla_kv_norm_g", "mla_w_ukv")
_SHARD_AXIS = {"w_in": 2, "w_out": 1, "nsa_cmp_w1_k": 1, "nsa_cmp_w1_v": 1, "mla_w_uq": 2, "mla_w_ukv": 2}
_PACK_UNIT = 16 * LANES


def _pack(arrays, dtype):
    rows = []
    for a in arrays:
        v = a.astype(dtype).reshape(-1)
        pad = (-v.shape[0]) % _PACK_UNIT
        if pad:
            v = jnp.concatenate([v, jnp.zeros((pad,), dtype)])
        rows.append(v.reshape(-1, LANES))
    return jnp.concatenate(rows, axis=0)


def _unpack(flat, shapes):
    out, r = [], 0
    for shp in shapes:
        n = int(np.prod(shp))
        nr = -(-n // _PACK_UNIT) * (_PACK_UNIT // LANES)
        out.append(flat[r:r + nr].reshape(-1)[:n].reshape(shp))
        r += nr
    return out


def kernel(x, pre_norm_g, post_norm_g, w_in, b_in, w_out, fox_forget_bias, nsa_cmp_pos_k, nsa_cmp_w1_k, nsa_cmp_w2_k, nsa_cmp_pos_v, nsa_cmp_w1_v, nsa_cmp_w2_v, mla_q_norm_g, mla_w_uq, mla_kv_norm_g, mla_w_ukv, loss_target, m_pre_norm_g, m_post_norm_g, m_w_in, m_b_in, m_w_out, m_fox_forget_bias, m_nsa_cmp_pos_k, m_nsa_cmp_w1_k, m_nsa_cmp_w2_k, m_nsa_cmp_pos_v, m_nsa_cmp_w1_v, m_nsa_cmp_w2_v, m_mla_q_norm_g, m_mla_w_uq, m_mla_kv_norm_g, m_mla_w_ukv, v_pre_norm_g, v_post_norm_g, v_w_in, v_b_in, v_w_out, v_fox_forget_bias, v_nsa_cmp_pos_k, v_nsa_cmp_w1_k, v_nsa_cmp_w2_k, v_nsa_cmp_pos_v, v_nsa_cmp_w1_v, v_nsa_cmp_w2_v, v_mla_q_norm_g, v_mla_w_uq, v_mla_kv_norm_g, v_mla_w_ukv):
    given = dict(locals())
    local = {n: given[n] for n in _WEIGHTS}
    depth = pre_norm_g.shape[0]
    xs, target = x[0], loss_target[0]
    s = xs.shape[0]
    sharded = [n for n in _WEIGHTS if n in _SHARD_AXIS and n != "w_in"]
    small = [n for n in _WEIGHTS if n not in _SHARD_AXIS]
    chip = 2 * lax.axis_index("x") + lax.axis_index("y")
    core = lax.axis_index("c")
    own = lambda slots, mine: lax.dynamic_update_slice_in_dim(slots, mine[None], chip, axis=0)

    w_in_t = jnp.swapaxes(w_in, 1, 2).astype(_MXU)
    piece = lax.switch(chip, [functools.partial(_piece_from_shard, s=k) for k in range(N_CHIPS)], w_in_t)
    layer_shapes = [local[n].shape[1:] for n in sharded]
    flat = [_pack([local[n][l] for n in sharded], _MXU) for l in range(depth)]
    full = dict(local)
    for n in ["w_in"] + sharded:
        full[n] = []

    def add_layer(w_in_slots, flat_slots_):
        full["w_in"].append(w_in_slots)
        per_chip = [_unpack(flat_slots_[k], layer_shapes) for k in range(N_CHIPS)]
        for j, n in enumerate(sharded):
            full[n].append(jnp.concatenate([per_chip[k][j] for k in range(N_CHIPS)], axis=_SHARD_AXIS[n] - 1))

    halved = [piece[0].reshape(2, GROUP_W // 2, D_MODEL), flat[0].reshape(2, -1, LANES)]
    first_all = [own(a, b) for a, b in zip(_gather_chips(halved, name="gather_weights"), halved)]
    add_layer(first_all[0].reshape(N_CHIPS, GROUP_W, D_MODEL), first_all[1].reshape((N_CHIPS,) + flat[0].shape))
    later = [piece[l] for l in range(1, depth)] + flat[1:]
    started, token = _send_start(later, ["same"] * len(later), first_all[1], name="gather_later_start")
    full["pre_norm_g"] = pre_norm_g + token[0, 0]

    consts = _consts(s)
    params, act, saved = [], xs, []
    for l in range(depth):
        if l == 1:
            landed = [own(a, b) for a, b in zip(_send_wait(started, ["same"] * len(later), act,
                                                           name="gather_later_wait"), later)]
            for k in range(depth - 1):
                add_layer(landed[k], landed[depth - 1 + k])
        params.append(_layer_params(full, l))
        act, sv = _layer_fwd(act, params[l], consts, f"l{l}")
        saved.append(sv)
    dy, loss_parts = _loss_head(act, target, name="loss_head")

    def flat_slots(g, dtype):
        def part(n, k):
            a, ax = g[n], _SHARD_AXIS[n] - 1
            w = a.shape[ax] // N_CHIPS
            return lax.slice_in_dim(a, k * w, (k + 1) * w, axis=ax)
        return jnp.stack([_pack([part(n, k) for n in sharded], dtype) for k in range(N_CHIPS)])

    own_slot = lambda a: lax.dynamic_index_in_dim(a, chip, axis=0, keepdims=False)
    slots_of = lambda g: g["w_in"].reshape(N_CHIPS, GROUP_W, D_MODEL)

    modes = ["slots", "slots"]
    layer_grads, in_flight = [None] * depth, {}
    for l in reversed(range(depth)):
        dy, layer_grads[l] = _layer_bwd(dy, saved[l], params[l], consts, f"l{l}", _WIRE)
        if l > 0:
            wire = [slots_of(layer_grads[l]), flat_slots(layer_grads[l], _WIRE)]
            started, token = _send_start(wire, modes, dy, name=f"reduce_l{l}_start")
            in_flight[l] = (started, wire)
            params[l - 1] = dict(params[l - 1], post_g=params[l - 1]["post_g"] + token[0, 0])
    grad_x = dy[None]
    grads = {n: jnp.stack([layer_grads[l][n] for l in range(depth)]) for n in small}
    loss_row = jnp.concatenate([jnp.sum(loss_parts).reshape(1), jnp.zeros((LANES - 1,), f32)])
    small_shapes = [(LANES,)] + [grads[n].shape for n in small]
    contrib = _pack([loss_row] + [grads[n] for n in small], f32)

    halves = [slots_of(layer_grads[0]).reshape(N_CHIPS, 2, GROUP_W // 2, D_MODEL),
              flat_slots(layer_grads[0], _WIRE).reshape(N_CHIPS, 2, -1, LANES)]
    from_sibling = _swap_other_half(halves, name="reduce_pair")
    pair_sum = [_add_my_half(g, r, name=f"reduce_pair_add{j}") for j, (g, r) in enumerate(zip(halves, from_sibling))]
    from_chips = _alltoall_chips(pair_sum + [contrib], modes + ["same"], name="reduce_chips")
    my_half = [_sum_slots(own(slots, own_slot(ps)), name=f"reduce_chips_add{j}")
               for j, (slots, ps) in enumerate(zip(from_chips, pair_sum))]
    partial = []
    for l in range(1, depth):
        started, wire = in_flight[l]
        landed = _send_wait(started, modes, dy, name=f"reduce_l{l}_wait")
        partial += [_sum_slots(own(slots, own_slot(a)), name=f"reduce_l{l}_add{j}")
                    for j, (slots, a) in enumerate(zip(landed, wire))]
    partial.append(_sum_slots(own(from_chips[2], contrib), name="sum_small"))
    theirs = _swap_sibling(my_half + partial, name="reduce_share")
    first = core == 0
    whole = [jnp.concatenate([jnp.where(first, a, b), jnp.where(first, b, a)], axis=0)
             for a, b in zip(my_half, theirs[:2])]
    whole += [_add2(a[None], b[None], name=f"reduce_cores_add{j}")[0] for j, (a, b) in enumerate(zip(partial, theirs[2:]))]
    unpiece = [functools.partial(_shard_from_piece, s=k) for k in range(N_CHIPS)]
    summed = {"w_in": jnp.stack([lax.switch(chip, unpiece, whole[2 * l].T) for l in range(depth)])}
    rest = [_unpack(whole[2 * l + 1], layer_shapes) for l in range(depth)]
    for j, n in enumerate(sharded):
        summed[n] = jnp.stack([rest[l][j] for l in range(depth)])
    total = _unpack(whole[2 * depth], small_shapes)
    loss = total[0][0]
    summed.update(zip(small, total[1:]))

    deltas, new_m, new_v = {}, {}, {}
    for n in _WEIGHTS:
        deltas[n], new_m[n], new_v[n] = _adamw(local[n], summed[n], given["m_" + n], given["v_" + n], name=f"adamw_{n}")
    return (loss, grad_x, *[summed[n] for n in _WEIGHTS], *[deltas[n] for n in _WEIGHTS],
            *[new_m[n] for n in _WEIGHTS], *[new_v[n] for n in _WEIGHTS])
```

```python
import functools
import math

import numpy as np
import jax
import jax.numpy as jnp
from jax import lax
from jax.experimental import pallas as pl
from jax.experimental.pallas import tpu as pltpu

f32 = jnp.float32
bf16 = jnp.bfloat16
_MXU = jnp.bfloat16
_WIRE = jnp.bfloat16
_SDS = jax.ShapeDtypeStruct
_ANY = pl.BlockSpec(memory_space=pl.ANY)
_MESH = pl.DeviceIdType.MESH

D_MODEL = 2048
N_HEADS = 4
HEAD_DIM = 128
GROUP = 512
RMS_EPS = 1e-6
NEG_INF = -1e30
ROPE_THETA = 10000.0
CMP_LEN, CMP_STRIDE, SEL_LEN, SEL_TOPN, WINDOW = 32, 16, 64, 16, 512
FORCED_BONUS = 1e6
MLA_Q_RANK, MLA_KV_RANK, MLA_NOPE, MLA_ROPE = 384, 128, 128, 64
ADAM_LR, ADAM_B1, ADAM_B2, ADAM_EPS, ADAM_WD, ADAM_STEP = 0.001, 0.9, 0.999, 1e-08, 0.01, 10
LANES = 128
VMEM_LIMIT = 56 * 1024 * 1024
HP_FWD, HP_BWD = 2, 2

_SEGS = (
    ("sb_q", 512), ("sb_k", 512), ("sb_v", 512), ("sb_gate", 512), ("nsa_q", 512), ("nsa_k_cmp", 128),
    ("nsa_v_cmp", 128), ("nsa_k_sel", 128), ("nsa_v_sel", 128), ("nsa_k_win", 128), ("nsa_v_win", 128),
    ("nsa_branch", 12), ("nsa_gate", 512), ("fox_q", 512), ("fox_k", 512), ("fox_v", 512), ("fox_f", 4),
    ("fox_gate", 512), ("mla_cq", 384), ("mla_ckv", 128), ("mla_k_rope", 64), ("mla_gate", 512),
)
_ORIG, _WID = {}, {}
_o = 0
for _n, _w in _SEGS:
    _ORIG[_n], _WID[_n] = _o, _w
    _o += _w
IN_WIDTH = _o
N_CHIPS = 4
CHIP_COLS = IN_WIDTH // N_CHIPS
GROUP_W = 2048
ZW = N_CHIPS * GROUP_W
_GROUPS = (
    (("sb_q", 0, 512, 0), ("sb_k", 0, 512, 512), ("sb_v", 0, 512, 1024), ("sb_gate", 0, 212, 1536)),
    (("nsa_q", 0, 512, 0), ("nsa_k_cmp", 0, 128, 512), ("nsa_v_cmp", 0, 128, 640), ("nsa_k_sel", 0, 128, 768),
     ("nsa_v_sel", 0, 128, 896), ("nsa_k_win", 0, 128, 1024), ("nsa_v_win", 0, 128, 1152), ("nsa_branch", 0, 12, 1280),
     ("sb_gate", 212, 512, 1408), ("nsa_gate", 0, 156, 1712)),
    (("fox_q", 0, 512, 0), ("fox_k", 0, 512, 512), ("fox_v", 0, 368, 1024), ("nsa_gate", 156, 512, 1408)),
    (("mla_cq", 0, 384, 0), ("mla_ckv", 0, 128, 384), ("mla_k_rope", 0, 64, 512), ("fox_f", 0, 4, 640),
     ("fox_v", 368, 512, 768), ("fox_gate", 0, 512, 1024), ("mla_gate", 0, 512, 1536)),
)
_PIECES = {n: [] for n, _ in _SEGS}
for _s, _grp in enumerate(_GROUPS):
    _cover = sorted((_ORIG[n] + lo, _ORIG[n] + hi) for n, lo, hi, _ in _grp)
    assert _cover[0][0] == _s * CHIP_COLS and _cover[-1][1] == (_s + 1) * CHIP_COLS
    assert all(a[1] == b[0] for a, b in zip(_cover, _cover[1:]))
    _ends = sorted((off, off + hi - lo) for _, lo, hi, off in _grp)
    assert all(a[1] <= b[0] for a, b in zip(_ends, _ends[1:])) and _ends[-1][1] <= GROUP_W
    assert _ends[0][0] == 0 and all(e[0] % 16 == 0 for e in _ends)
    for _n, _lo, _hi, _off in _grp:
        _PIECES[_n].append((_s * GROUP_W + _off, _lo, _hi))
_AL = {n: p[0][0] for n, p in _PIECES.items() if len(p) == 1}


def _cp(sem=None):
    return pltpu.CompilerParams(dimension_semantics=sem, vmem_limit_bytes=VMEM_LIMIT)


def _mm(a, b):
    return jnp.dot(a.astype(_MXU), b.astype(_MXU), preferred_element_type=f32)


def _mm_nt(a, b):
    return lax.dot_general(a.astype(_MXU), b.astype(_MXU), (((1,), (1,)), ((), ())), preferred_element_type=f32)


def _mm_tn(a, b):
    return lax.dot_general(a.astype(_MXU), b.astype(_MXU), (((0,), (0,)), ((), ())), preferred_element_type=f32)


def _mm_split(x, t):
    hi = x.astype(_MXU)
    lo = (x - hi.astype(f32)).astype(_MXU)
    return jnp.dot(hi, t, preferred_element_type=f32) + jnp.dot(lo, t, preferred_element_type=f32)


def _sigmoid(x):
    return 1.0 / (1.0 + jnp.exp(-x))


def _iota(shape, dim):
    return lax.broadcasted_iota(jnp.int32, shape, dim)


def _pick(n, prefs):
    for p in prefs:
        if n % p == 0:
            return p
    return n


def _matmul(a, b, mode, *, bias=None, out_dtype=f32, name):
    grouped = b.ndim == 3
    b_shape = (b.shape[0] * b.shape[1], b.shape[2]) if grouped else b.shape
    if mode == "nn":
        (M, K), (K2, N) = a.shape, b_shape
    elif mode == "nt":
        (M, K), (N, K2) = a.shape, b_shape
    else:
        (K, M), (K2, N) = a.shape, b_shape
    assert K == K2
    tm = _pick(M, (1024, 512, 384, 256, 128))
    tn = _pick(N, (1024, 512, 384, 256, 128))
    tk = K if K <= 2048 else _pick(K, (2048, 2432, 1024, 512))
    nk = K // tk
    a_spec = {"nn": pl.BlockSpec((tm, tk), lambda i, j, k: (i, k)),
              "nt": pl.BlockSpec((tm, tk), lambda i, j, k: (i, k)),
              "tn": pl.BlockSpec((tk, tm), lambda i, j, k: (k, i))}[mode]
    if not grouped:
        b_spec = {"nn": pl.BlockSpec((tk, tn), lambda i, j, k: (k, j)),
                  "nt": pl.BlockSpec((tn, tk), lambda i, j, k: (j, k)),
                  "tn": pl.BlockSpec((tk, tn), lambda i, j, k: (k, j))}[mode]
    elif mode == "nt":
        per = b.shape[1] // tn
        b_spec = pl.BlockSpec((None, tn, tk), lambda i, j, k: (j // per, j % per, k))
    else:
        assert mode == "nn"
        per = b.shape[1] // tk
        b_spec = pl.BlockSpec((None, tk, tn), lambda i, j, k: (k // per, k % per, j))
    dot = {"nn": _mm, "nt": _mm_nt, "tn": _mm_tn}[mode]
    has_bias = bias is not None

    def body(*refs):
        if has_bias:
            a_ref, b_ref, bias_ref, o_ref, acc_ref = refs
        else:
            a_ref, b_ref, o_ref, acc_ref = refs
            bias_ref = None
        k = pl.program_id(2)
        part = dot(a_ref[...], b_ref[...])

        def finish(total):
            if has_bias:
                total = total + bias_ref[...]
            o_ref[...] = total.astype(o_ref.dtype)

        if nk == 1:
            finish(part)
        else:
            @pl.when(k == 0)
            def _():
                acc_ref[...] = part

            @pl.when(k > 0)
            def _():
                acc_ref[...] += part

            @pl.when(k == nk - 1)
            def _():
                finish(acc_ref[...])

    in_specs = [a_spec, b_spec]
    args = [a, b]
    if has_bias:
        in_specs.append(pl.BlockSpec((1, tn), lambda i, j, k: (0, j)))
        args.append(bias.reshape(1, N))
    return pl.pallas_call(
        body, out_shape=_SDS((M, N), out_dtype), grid=(M // tm, N // tn, nk),
        in_specs=in_specs, out_specs=pl.BlockSpec((tm, tn), lambda i, j, k: (i, j)),
        scratch_shapes=[pltpu.VMEM((tm, tn), f32)],
        compiler_params=_cp(("parallel", "parallel", "arbitrary")), name=name,
    )(*args)


def _row_block(s):
    return _pick(s, (256, 128))


def _rms_fwd(x, g, *, out_dtype, name):
    s, d = x.shape
    rb = _row_block(s)

    def body(x_ref, g_ref, o_ref):
        xv = x_ref[...]
        r = lax.rsqrt(jnp.mean(xv * xv, axis=-1, keepdims=True) + RMS_EPS)
        o_ref[...] = (xv * r * g_ref[...]).astype(o_ref.dtype)

    return pl.pallas_call(
        body, out_shape=_SDS((s, d), out_dtype), grid=(s // rb,),
        in_specs=[pl.BlockSpec((rb, d), lambda i: (i, 0)), pl.BlockSpec((1, d), lambda i: (0, 0))],
        out_specs=pl.BlockSpec((rb, d), lambda i: (i, 0)), compiler_params=_cp(("parallel",)), name=name,
    )(x, g.reshape(1, d))


def _postnorm_fwd(u, g, x, *, name):
    s, d = u.shape
    rb = _row_block(s)

    def body(u_ref, g_ref, x_ref, o_ref):
        uv = u_ref[...]
        r = lax.rsqrt(jnp.mean(uv * uv, axis=-1, keepdims=True) + RMS_EPS)
        o_ref[...] = x_ref[...] + uv * r * g_ref[...]

    return pl.pallas_call(
        body, out_shape=_SDS((s, d), f32), grid=(s // rb,),
        in_specs=[pl.BlockSpec((rb, d), lambda i: (i, 0)), pl.BlockSpec((1, d), lambda i: (0, 0)),
                  pl.BlockSpec((rb, d), lambda i: (i, 0))],
        out_specs=pl.BlockSpec((rb, d), lambda i: (i, 0)), compiler_params=_cp(("parallel",)), name=name,
    )(u, g.reshape(1, d), x)


def _fold_rows(v):
    r = v.shape[0]
    acc = v[0:8]
    for k in range(1, r // 8):
        acc = acc + v[8 * k:8 * k + 8]
    return acc


def _rms_bwd(dy, x, g, res=None, *, name):
    s, d = x.shape
    rb = _row_block(s)
    nb = s // rb
    has_res = res is not None

    def body(*refs):
        if has_res:
            dy_ref, x_ref, g_ref, res_ref, dx_ref, dg_ref, acc_ref = refs
        else:
            dy_ref, x_ref, g_ref, dx_ref, dg_ref, acc_ref = refs
        i = pl.program_id(0)
        xv = x_ref[...]
        r = lax.rsqrt(jnp.mean(xv * xv, axis=-1, keepdims=True) + RMS_EPS)
        xh = xv * r
        dyv = dy_ref[...]
        dxh = dyv * g_ref[...]
        dx = r * (dxh - xh * jnp.mean(dxh * xh, axis=-1, keepdims=True))
        if has_res:
            dx = dx + res_ref[...]
        dx_ref[...] = dx
        part = _fold_rows(dyv * xh)

        @pl.when(i == 0)
        def _():
            acc_ref[...] = part

        @pl.when(i > 0)
        def _():
            acc_ref[...] += part

        @pl.when(i == nb - 1)
        def _():
            dg_ref[...] = jnp.sum(acc_ref[...], axis=0, keepdims=True)

    blk = pl.BlockSpec((rb, d), lambda i: (i, 0))
    in_specs = [blk, blk, pl.BlockSpec((1, d), lambda i: (0, 0))] + ([blk] if has_res else [])
    args = [dy, x, g.reshape(1, d)] + ([res] if has_res else [])
    return pl.pallas_call(
        body, out_shape=(_SDS((s, d), f32), _SDS((1, d), f32)), grid=(nb,), in_specs=in_specs,
        out_specs=(blk, pl.BlockSpec((1, d), lambda i: (0, 0))),
        scratch_shapes=[pltpu.VMEM((8, d), f32)], compiler_params=_cp(("arbitrary",)), name=name,
    )(*args)


def _loss_head(y, target, *, name):
    s, d = y.shape
    rb = _row_block(s)
    nb = s // rb

    def body(y_ref, t_ref, dy_ref, l_ref):
        i = pl.program_id(0)
        e = y_ref[...] - t_ref[...]
        dy_ref[...] = e * (1.0 / d)
        rows = _fold_rows(e * e)
        part = rows[:, 0:LANES]
        for k in range(1, d // LANES):
            part = part + rows[:, k * LANES:(k + 1) * LANES]
        part = part * (0.5 / d)

        @pl.when(i == 0)
        def _():
            l_ref[...] = part

        @pl.when(i > 0)
        def _():
            l_ref[...] += part

    blk = pl.BlockSpec((rb, d), lambda i: (i, 0))
    return pl.pallas_call(
        body, out_shape=(_SDS((s, d), f32), _SDS((8, LANES), f32)), grid=(nb,), in_specs=[blk, blk],
        out_specs=(blk, pl.BlockSpec((8, LANES), lambda i: (0, 0))),
        compiler_params=_cp(("arbitrary",)), name=name,
    )(y, target)


def _colsum(a, *, name):
    s, n = a.shape
    rb = _row_block(s)
    nb = s // rb
    tn = _pick(n, (2432, 2048, 1024, 512, 384, 128))

    def body(a_ref, o_ref, acc_ref):
        i = pl.program_id(1)
        part = _fold_rows(a_ref[...].astype(f32))

        @pl.when(i == 0)
        def _():
            acc_ref[...] = part

        @pl.when(i > 0)
        def _():
            acc_ref[...] += part

        @pl.when(i == nb - 1)
        def _():
            o_ref[...] = jnp.sum(acc_ref[...], axis=0, keepdims=True)

    return pl.pallas_call(
        body, out_shape=_SDS((1, n), f32), grid=(n // tn, nb),
        in_specs=[pl.BlockSpec((rb, tn), lambda j, i: (i, j))], out_specs=pl.BlockSpec((1, tn), lambda j, i: (0, j)),
        scratch_shapes=[pltpu.VMEM((8, tn), f32)], compiler_params=_cp(("parallel", "arbitrary")), name=name,
    )(a)


def _gate_fwd(outs, gate, *, name):
    s, d = gate.shape
    rb = _row_block(s)
    n = len(outs)
    w = d // n

    def body(*refs):
        g_ref, m_ref = refs[n], refs[n + 1]
        for k in range(n):
            gv = g_ref[:, k * w:(k + 1) * w]
            m_ref[:, k * w:(k + 1) * w] = (refs[k][...] * (gv * _sigmoid(gv))).astype(m_ref.dtype)

    blk = pl.BlockSpec((rb, d), lambda i: (i, 0))
    part = pl.BlockSpec((rb, w), lambda i: (i, 0))
    return pl.pallas_call(body, out_shape=_SDS((s, d), _MXU), grid=(s // rb,), in_specs=[part] * n + [blk],
                          out_specs=blk, compiler_params=_cp(("parallel",)), name=name)(*outs, gate)


def _gate_bwd(dmix, outs, gate, *, name):
    s, d = gate.shape
    rb = _row_block(s)
    n = len(outs)
    w = d // n

    def body(*refs):
        dm_ref, o_refs, g_ref, do_refs, dg_ref = refs[0], refs[1:1 + n], refs[1 + n], refs[2 + n:2 + 2 * n], refs[-1]
        for k in range(n):
            sl = slice(k * w, (k + 1) * w)
            gv = g_ref[:, sl]
            sg = _sigmoid(gv)
            dm = dm_ref[:, sl]
            do_refs[k][...] = dm * (gv * sg)
            dg_ref[:, sl] = dm * o_refs[k][...] * (sg * (1.0 + gv * (1.0 - sg)))

    blk = pl.BlockSpec((rb, d), lambda i: (i, 0))
    part = pl.BlockSpec((rb, w), lambda i: (i, 0))
    return pl.pallas_call(body, out_shape=tuple(_SDS((s, w), f32) for _ in range(n)) + (_SDS((s, d), f32),),
                          grid=(s // rb,), in_specs=[blk] + [part] * n + [blk], out_specs=(part,) * n + (blk,),
                          compiler_params=_cp(("parallel",)), name=name)(dmix, *outs, gate)


def _adamw(w, g, m, v, *, name):
    shape = w.shape
    cols = shape[-1]
    rows = int(np.prod(shape[:-1])) if len(shape) > 1 else 1
    to2 = lambda t: t.reshape(rows, cols)
    rb = rows
    if rows * cols * 4 > (1 << 20):
        rb = max(d for d in range(8, rows + 1, 8) if rows % d == 0 and (d * cols * 4 <= (1600 << 10) or d == 8))

    def body(w_ref, g_ref, m_ref, v_ref, d_ref, nm_ref, nv_ref):
        gv = g_ref[...]
        mn = ADAM_B1 * m_ref[...] + (1.0 - ADAM_B1) * gv
        vn = ADAM_B2 * v_ref[...] + (1.0 - ADAM_B2) * (gv * gv)
        m_hat = mn / (1.0 - ADAM_B1 ** ADAM_STEP)
        v_hat = vn / (1.0 - ADAM_B2 ** ADAM_STEP)
        d_ref[...] = -ADAM_LR * (m_hat / (jnp.sqrt(v_hat) + ADAM_EPS) + ADAM_WD * w_ref[...])
        nm_ref[...] = mn
        nv_ref[...] = vn

    blk = pl.BlockSpec((rb, cols), lambda i: (i, 0))
    out = pl.pallas_call(body, out_shape=tuple(_SDS((rows, cols), f32) for _ in range(3)), grid=(rows // rb,),
                         in_specs=[blk] * 4, out_specs=(blk,) * 3, compiler_params=_cp(("parallel",)),
                         name=name)(to2(w), to2(g), to2(m), to2(v))
    return tuple(t.reshape(shape) for t in out)


def _sum_slots(a, *, name):
    p, n, c = a.shape
    rb = max(d for d in range(8, n + 1, 8) if n % d == 0 and (p * d * c * 4 <= (6 << 20) or d == 8))

    def body(a_ref, o_ref):
        acc = a_ref[0].astype(f32)
        for k in range(1, p):
            acc = acc + a_ref[k].astype(f32)
        o_ref[...] = acc

    return pl.pallas_call(body, out_shape=_SDS((n, c), f32), grid=(n // rb,),
                          in_specs=[pl.BlockSpec((p, rb, c), lambda i: (0, i, 0))],
                          out_specs=pl.BlockSpec((rb, c), lambda i: (i, 0)), compiler_params=_cp(("parallel",)),
                          name=name)(a)


def _add2(a, b, *, name):
    p, n, c = a.shape
    rb = max(d for d in range(8, n + 1, 8) if n % d == 0 and (d * c * 4 <= (2 << 20) or d == 8))

    def body(a_ref, b_ref, o_ref):
        o_ref[...] = a_ref[...] + b_ref[...]

    blk = pl.BlockSpec((1, rb, c), lambda s, i: (s, i, 0))
    return pl.pallas_call(body, out_shape=_SDS((p, n, c), f32), grid=(p, n // rb), in_specs=[blk, blk], out_specs=blk,
                          compiler_params=_cp(("parallel", "parallel")), name=name)(a, b)


def _rope_tables(pos, dim):
    half = dim // 2
    inv = ROPE_THETA ** (-jnp.arange(half, dtype=f32) / half)
    ang = pos.astype(f32)[:, None] * inv[None, :]
    c, s = jnp.cos(ang), jnp.sin(ang)
    z = jnp.zeros_like(c)
    pad = [jnp.zeros((pos.shape[0], LANES - dim), f32)] if dim < LANES else []
    return (jnp.concatenate([c, c] + pad, axis=1), jnp.concatenate([-s, z] + pad, axis=1),
            jnp.concatenate([z, s] + pad, axis=1))


def _rope(x, cos, sa, sb, half, transpose=False):
    if transpose:
        return x * cos + pltpu.roll(x * sa, half, 1) + pltpu.roll(x * sb, LANES - half, 1)
    return x * cos + pltpu.roll(x, LANES - half, 1) * sa + pltpu.roll(x, half, 1) * sb


def _rope_call(items, tables, half, transpose, *, name):
    s = items[0][0].shape[0]
    rb = _row_block(s)
    n = len(items)

    def body(*refs):
        cos, sa, sb = refs[n][...], refs[n + 1][...], refs[n + 2][...]
        for k in range(n):
            x_ref, o_ref = refs[k], refs[n + 3 + k]
            for j in range(items[k][1] // LANES):
                sl = slice(j * LANES, (j + 1) * LANES)
                o_ref[:, sl] = _rope(x_ref[:, sl], cos, sa, sb, half, transpose)

    in_specs = [pl.BlockSpec((rb, w), functools.partial(lambda i, cb: (i, cb), cb=cb)) for _, w, cb in items]
    in_specs += [pl.BlockSpec((rb, LANES), lambda i: (i, 0))] * 3
    out_specs = tuple(pl.BlockSpec((rb, w), lambda i: (i, 0)) for _, w, _ in items)
    return pl.pallas_call(
        body, out_shape=tuple(_SDS((s, w), f32) for _, w, _ in items), grid=(s // rb,), in_specs=in_specs,
        out_specs=out_specs, compiler_params=_cp(("parallel",)), name=name,
    )(*[a for a, _, _ in items], *tables)


def _attn_block(s):
    return _pick(s, (512, 256, 128))


def _lower_mask(b, strict):
    r, c = _iota((b, b), 0), _iota((b, b), 1)
    return (c < r) if strict else (c <= r)


def _pick_lane(block, h):
    return jnp.sum(jnp.where(_iota(block.shape, 1) == h, block, 0.0), axis=1, keepdims=True)


def _head_bias(cum_blk, g, j, hp):
    if hp == N_HEADS:
        return cum_blk[:, j:j + 1]
    return _pick_lane(cum_blk, g * hp + j)


def _attn_fwd(q, k, v, qcol, kcol, vcol, dq, cum, cum_t, *, scale, hp, name):
    s = q.shape[0]
    b = _attn_block(s)
    nq = s // b
    has_bias = cum is not None
    assert qcol % hp == 0 and kcol % hp == 0 and vcol % hp == 0

    def body(*refs):
        if has_bias:
            q_ref, k_ref, v_ref, cum_ref, cumt_ref, o_ref, lse_ref = refs
        else:
            q_ref, k_ref, v_ref, o_ref, lse_ref = refs
        g, i = pl.program_id(0), pl.program_id(1)
        qs = [q_ref[:, j * dq:(j + 1) * dq].astype(_MXU) for j in range(hp)]
        cqs = [_head_bias(cum_ref[...], g, j, hp) for j in range(hp)] if has_bias else None

        def chunk(c, carry, diag):
            st = pl.multiple_of(c * b, b)
            mask = _lower_mask(b, False) if diag else None
            out = []
            for j in range(hp):
                m, l, acc = carry[j]
                z = _mm_nt(qs[j], k_ref[pl.ds(st, b), j * dq:(j + 1) * dq]) * scale
                if has_bias:
                    z = z + (cqs[j] - cumt_ref[j, c])
                if diag:
                    z = jnp.where(mask, z, NEG_INF)
                m_new = jnp.maximum(m, jnp.max(z, axis=1, keepdims=True))
                p = jnp.exp(z - m_new)
                if diag:
                    p = jnp.where(mask, p, 0.0)
                alpha = jnp.exp(m - m_new)
                l = alpha * l + jnp.sum(p, axis=1, keepdims=True)
                acc = alpha * acc + _mm(p, v_ref[pl.ds(st, b), j * HEAD_DIM:(j + 1) * HEAD_DIM])
                out.append((m_new, l, acc))
            return tuple(out)

        init = tuple((jnp.full((b, 1), NEG_INF, f32), jnp.zeros((b, 1), f32), jnp.zeros((b, HEAD_DIM), f32))
                     for _ in range(hp))
        carry = lax.fori_loop(0, i, lambda c, cr: chunk(c, cr, False), init)
        for j, (m, l, acc) in enumerate(chunk(i, carry, True)):
            o_ref[:, j * HEAD_DIM:(j + 1) * HEAD_DIM] = acc / l
            lse_ref[j] = m + jnp.log(l)

    in_specs = [pl.BlockSpec((b, hp * dq), lambda g, i: (i, qcol // hp + g)),
                pl.BlockSpec((s, hp * dq), lambda g, i: (0, kcol // hp + g)),
                pl.BlockSpec((s, hp * HEAD_DIM), lambda g, i: (0, vcol // hp + g))]
    args = [q, k, v]
    if has_bias:
        in_specs += [pl.BlockSpec((b, LANES), lambda g, i: (i, 0)),
                     pl.BlockSpec((hp, nq, 1, b), lambda g, i: (g, 0, 0, 0))]
        args += [cum, cum_t]
    return pl.pallas_call(
        body, out_shape=(_SDS((s, N_HEADS * HEAD_DIM), f32), _SDS((N_HEADS, s, 1), f32)), grid=(N_HEADS // hp, nq),
        in_specs=in_specs,
        out_specs=(pl.BlockSpec((b, hp * HEAD_DIM), lambda g, i: (i, g)),
                   pl.BlockSpec((hp, b, 1), lambda g, i: (g, i, 0))),
        compiler_params=_cp(("parallel", "parallel")), name=name,
    )(*args)


def _attn_bwd(q, k, v, qcol, kcol, vcol, dq, do, o, lse, cum, cum_t, *, scale, hp, name):
    s = q.shape[0]
    b = _attn_block(s)
    nq = s // b
    has_bias = cum is not None
    assert qcol % hp == 0 and kcol % hp == 0 and vcol % hp == 0
    hd = lambda j: slice(j * HEAD_DIM, (j + 1) * HEAD_DIM)
    hq = lambda j: slice(j * dq, (j + 1) * dq)

    def body(*refs):
        if has_bias:
            (q_ref, k_ref, v_ref, do_ref, o_ref, lse_ref, cum_ref, cumt_ref, dq_ref, dk_ref, dv_ref, dck_ref,
             dkt_sc, dvt_sc, p_sc, dp_sc) = refs
        else:
            q_ref, k_ref, v_ref, do_ref, o_ref, lse_ref, dq_ref, dk_ref, dv_ref, dkt_sc, dvt_sc = refs
        g, i = pl.program_id(0), pl.program_id(1)

        @pl.when(i == 0)
        def _():
            dkt_sc[...] = jnp.zeros_like(dkt_sc)
            dvt_sc[...] = jnp.zeros_like(dvt_sc)
            if has_bias:
                dck_ref[...] = jnp.zeros_like(dck_ref)

        qs = [q_ref[:, hq(j)].astype(_MXU) for j in range(hp)]
        dos = [do_ref[:, hd(j)].astype(_MXU) for j in range(hp)]
        qts = [q_ref[:, hq(j)].T.astype(_MXU) for j in range(hp)]
        dots = [do_ref[:, hd(j)].T.astype(_MXU) for j in range(hp)]
        lses = [lse_ref[j] for j in range(hp)]
        cqs = [_head_bias(cum_ref[...], g, j, hp) for j in range(hp)] if has_bias else None

        def probs(j, c, diag):
            st = pl.multiple_of(c * b, b)
            z = _mm_nt(qs[j], k_ref[pl.ds(st, b), hq(j)]) * scale
            if has_bias:
                z = z + (cqs[j] - cumt_ref[j, c])
            p = jnp.exp(z - lses[j])
            if diag:
                p = jnp.where(_lower_mask(b, False), p, 0.0)
            return p, _mm_nt(dos[j], v_ref[pl.ds(st, b), hd(j)])

        if has_bias:
            def first(c, accs, diag):
                out = []
                for j in range(hp):
                    p, dp = probs(j, c, diag)
                    p_sc[j, c] = p
                    dp_sc[j, c] = dp
                    out.append(accs[j] + jnp.sum(p * dp, axis=1, keepdims=True))
                return tuple(out)

            deltas = lax.fori_loop(0, i, lambda c, a: first(c, a, False),
                                   tuple(jnp.zeros((b, 1), f32) for _ in range(hp)))
            deltas = first(i, deltas, True)
        else:
            deltas = [jnp.sum(do_ref[:, hd(j)] * o_ref[:, hd(j)], axis=1, keepdims=True) for j in range(hp)]

        def chunk(c, dq_accs, diag):
            st = pl.multiple_of(c * b, b)
            out = []
            for j in range(hp):
                p, dp = (p_sc[j, c], dp_sc[j, c]) if has_bias else probs(j, c, diag)
                ds = p * (dp - deltas[j])
                dkt_sc[j, c] += _mm(qts[j], ds)
                dvt_sc[j, c] += _mm(dots[j], p)
                if has_bias:
                    dck_ref[j, c] += -jnp.sum(ds, axis=0, keepdims=True)
                out.append(dq_accs[j] + _mm(ds, k_ref[pl.ds(st, b), hq(j)]))
            return tuple(out)

        accs = lax.fori_loop(0, i, lambda c, a: chunk(c, a, False), tuple(jnp.zeros((b, dq), f32) for _ in range(hp)))
        for j, acc in enumerate(chunk(i, accs, True)):
            dq_ref[:, hq(j)] = acc * scale

        @pl.when(i == nq - 1)
        def _():
            for j in range(hp):
                for c in range(nq):
                    dk_ref[c * b:(c + 1) * b, hq(j)] = dkt_sc[j, c].T * scale
                    dv_ref[c * b:(c + 1) * b, hd(j)] = dvt_sc[j, c].T

    rowq = pl.BlockSpec((b, hp * HEAD_DIM), lambda g, i: (i, g))
    in_specs = [pl.BlockSpec((b, hp * dq), lambda g, i: (i, qcol // hp + g)),
                pl.BlockSpec((s, hp * dq), lambda g, i: (0, kcol // hp + g)),
                pl.BlockSpec((s, hp * HEAD_DIM), lambda g, i: (0, vcol // hp + g)), rowq, rowq,
                pl.BlockSpec((hp, b, 1), lambda g, i: (g, i, 0))]
    args = [q, k, v, do, o, lse]
    out_shape = [_SDS((s, N_HEADS * dq), f32), _SDS((s, N_HEADS * dq), f32), _SDS((s, N_HEADS * HEAD_DIM), f32)]
    out_specs = [pl.BlockSpec((b, hp * dq), lambda g, i: (i, g)), pl.BlockSpec((s, hp * dq), lambda g, i: (0, g)),
                 pl.BlockSpec((s, hp * HEAD_DIM), lambda g, i: (0, g))]
    if has_bias:
        in_specs += [pl.BlockSpec((b, LANES), lambda g, i: (i, 0)),
                     pl.BlockSpec((hp, nq, 1, b), lambda g, i: (g, 0, 0, 0))]
        args += [cum, cum_t]
        out_shape.append(_SDS((N_HEADS, nq, 1, b), f32))
        out_specs.append(pl.BlockSpec((hp, nq, 1, b), lambda g, i: (g, 0, 0, 0)))
    return pl.pallas_call(
        body, out_shape=tuple(out_shape), grid=(N_HEADS // hp, nq), in_specs=in_specs, out_specs=tuple(out_specs),
        scratch_shapes=[pltpu.VMEM((hp, nq, dq, b), f32), pltpu.VMEM((hp, nq, HEAD_DIM, b), f32)]
        + ([pltpu.VMEM((hp, nq, b, b), f32)] * 2 if has_bias else []),
        compiler_params=_cp(("parallel", "arbitrary")), name=name,
    )(*args)


def _tri(b, kind):
    r, c = _iota((b, b), 0), _iota((b, b), 1)
    cond = {"row_gt": r > c, "row_lt": r < c, "row_ge": r >= c, "row_le": r <= c}[kind]
    return jnp.where(cond, 1.0, 0.0).astype(_MXU)


def _log_keep(z):
    return -(jnp.maximum(z, 0.0) + jnp.log(1.0 + jnp.exp(-jnp.abs(z))))


def _sb_fwd(z_all, *, hp, name):
    s = z_all.shape[0]
    b = _attn_block(s)
    nq = s // b
    scale = HEAD_DIM ** -0.5
    qcol, kcol, vcol = (_AL[n] // (hp * HEAD_DIM) for n in ("sb_q", "sb_k", "sb_v"))
    hd = lambda j: slice(j * HEAD_DIM, (j + 1) * HEAD_DIM)

    def body(q_ref, k_ref, v_ref, o_ref):
        i = pl.program_id(1)
        qs = [q_ref[:, hd(j)].astype(_MXU) for j in range(hp)]
        upper = _tri(b, "row_gt")

        def chunk(c, carry, diag):
            st = pl.multiple_of(c * b, b)
            mask = _lower_mask(b, True) if diag else None
            out = []
            for j in range(hp):
                rsum, acc = carry[j]
                z = _mm_nt(qs[j], k_ref[pl.ds(st, b), hd(j)]) * scale
                lk = _log_keep(z)
                if diag:
                    lk = jnp.where(mask, lk, 0.0)
                a = z + lk + _mm_split(lk, upper) + rsum
                if diag:
                    a = jnp.where(mask, a, NEG_INF)
                acc = acc + _mm(jnp.exp(a), v_ref[pl.ds(st, b), hd(j)])
                out.append((rsum + jnp.sum(lk, axis=1, keepdims=True), acc))
            return tuple(out)

        init = tuple((jnp.zeros((b, 1), f32), jnp.zeros((b, HEAD_DIM), f32)) for _ in range(hp))
        carry = lax.fori_loop(0, i, lambda t, cr: chunk(i - 1 - t, cr, False), chunk(i, init, True))
        for j in range(hp):
            o_ref[:, hd(j)] = carry[j][1]

    w = hp * HEAD_DIM
    return pl.pallas_call(
        body, out_shape=_SDS((s, GROUP), f32), grid=(N_HEADS // hp, nq),
        in_specs=[pl.BlockSpec((b, w), lambda g, i: (i, qcol + g)), pl.BlockSpec((s, w), lambda g, i: (0, kcol + g)),
                  pl.BlockSpec((s, w), lambda g, i: (0, vcol + g))],
        out_specs=pl.BlockSpec((b, w), lambda g, i: (i, g)),
        compiler_params=_cp(("parallel", "parallel")), name=name,
    )(z_all, z_all, z_all)


def _sb_bwd(z_all, do, *, hp, name):
    s = z_all.shape[0]
    b = _attn_block(s)
    nq = s // b
    scale = HEAD_DIM ** -0.5
    qcol, kcol, vcol = (_AL[n] // (hp * HEAD_DIM) for n in ("sb_q", "sb_k", "sb_v"))
    hd = lambda j: slice(j * HEAD_DIM, (j + 1) * HEAD_DIM)

    def body(q_ref, k_ref, v_ref, do_ref, dq_ref, dk_ref, dv_ref, z_sc, lk_sc, r_sc):
        i = pl.program_id(1)

        @pl.when(i == 0)
        def _():
            dk_ref[...] = jnp.zeros_like(dk_ref)
            dv_ref[...] = jnp.zeros_like(dv_ref)

        qs = [q_ref[:, hd(j)].astype(_MXU) for j in range(hp)]
        dos = [do_ref[:, hd(j)].astype(_MXU) for j in range(hp)]
        upper = _tri(b, "row_gt")
        lower = _tri(b, "row_lt")

        def scores(c, rsums, diag):
            st = pl.multiple_of(c * b, b)
            out = []
            for j in range(hp):
                z = _mm_nt(qs[j], k_ref[pl.ds(st, b), hd(j)]) * scale
                lk = _log_keep(z)
                if diag:
                    lk = jnp.where(_lower_mask(b, True), lk, 0.0)
                z_sc[j, c] = z
                lk_sc[j, c] = lk
                r_sc[j, c] = _mm_split(lk, upper) + rsums[j]
                out.append(rsums[j] + jnp.sum(lk, axis=1, keepdims=True))
            return tuple(out)

        rsums = scores(i, tuple(jnp.zeros((b, 1), f32) for _ in range(hp)), True)
        lax.fori_loop(0, i, lambda t, r: scores(i - 1 - t, r, False), rsums)

        def grads(c, carry, diag):
            st = pl.multiple_of(c * b, b)
            mask = _lower_mask(b, True) if diag else None
            out = []
            for j in range(hp):
                psum, dq_acc = carry[j]
                z, lk = z_sc[j, c], lk_sc[j, c]
                lb = z + lk
                a = lb + r_sc[j, c]
                if diag:
                    a = jnp.where(mask, a, NEG_INF)
                w = jnp.exp(a)
                e = _mm_nt(dos[j], v_ref[pl.ds(st, b), hd(j)]) * w
                before = _mm_split(e, lower) + psum
                dz = e * jnp.exp(lk) - before * jnp.exp(lb)
                if diag:
                    dz = jnp.where(mask, dz, 0.0)
                dk_ref[pl.ds(st, b), hd(j)] += _mm_tn(dz, qs[j]) * scale
                dv_ref[pl.ds(st, b), hd(j)] += _mm_tn(w, dos[j])
                out.append((psum + jnp.sum(e, axis=1, keepdims=True), dq_acc + _mm(dz, k_ref[pl.ds(st, b), hd(j)])))
            return tuple(out)

        init = tuple((jnp.zeros((b, 1), f32), jnp.zeros((b, HEAD_DIM), f32)) for _ in range(hp))
        carry = grads(i, lax.fori_loop(0, i, lambda c, cr: grads(c, cr, False), init), True)
        for j in range(hp):
            dq_ref[:, hd(j)] = carry[j][1] * scale

    w = hp * HEAD_DIM
    blk = pl.BlockSpec((b, w), lambda g, i: (i, g))
    full = pl.BlockSpec((s, w), lambda g, i: (0, g))
    return pl.pallas_call(
        body, out_shape=tuple(_SDS((s, GROUP), f32) for _ in range(3)), grid=(N_HEADS // hp, nq),
        in_specs=[pl.BlockSpec((b, w), lambda g, i: (i, qcol + g)), pl.BlockSpec((s, w), lambda g, i: (0, kcol + g)),
                  pl.BlockSpec((s, w), lambda g, i: (0, vcol + g)), blk],
        out_specs=(blk, full, full),
        scratch_shapes=[pltpu.VMEM((hp, nq, b, b), f32)] * 3,
        compiler_params=_cp(("parallel", "arbitrary")), name=name,
    )(z_all, z_all, z_all, do)


def _split3_left(t, x):
    hi = x.astype(_MXU)
    r1 = x - hi.astype(f32)
    mid = r1.astype(_MXU)
    lo = (r1 - mid.astype(f32)).astype(_MXU)
    dot = functools.partial(jnp.dot, preferred_element_type=f32)
    return dot(t, hi) + dot(t, mid) + dot(t, lo)


def _split3_right(x, t):
    hi = x.astype(_MXU)
    r1 = x - hi.astype(f32)
    mid = r1.astype(_MXU)
    lo = (r1 - mid.astype(f32)).astype(_MXU)
    dot = functools.partial(jnp.dot, preferred_element_type=f32)
    return dot(hi, t) + dot(mid, t) + dot(lo, t)


def _fox_cum_fwd(z_all, bias, *, name):
    s = z_all.shape[0]
    b = _attn_block(s)
    fcol = _AL["fox_f"] // LANES

    def body(f_ref, b_ref, cum_ref, cumt_ref, carry_ref):
        i = pl.program_id(0)

        @pl.when(i == 0)
        def _():
            carry_ref[...] = jnp.zeros_like(carry_ref)

        u = f_ref[...] + b_ref[...]
        lf = jnp.minimum(u, 0.0) - jnp.log1p(jnp.exp(-jnp.abs(u)))
        cum = _split3_left(_tri(b, "row_ge"), lf) + carry_ref[...]
        cum_ref[...] = cum
        cumt_ref[...] = cum.T[0:8, :]
        carry_ref[...] = cum_ref[b - 1:b, :]

    return pl.pallas_call(
        body, out_shape=(_SDS((s, LANES), f32), _SDS((8, s), f32)), grid=(s // b,),
        in_specs=[pl.BlockSpec((b, LANES), lambda i: (i, fcol)), pl.BlockSpec((1, LANES), lambda i: (0, 0))],
        out_specs=(pl.BlockSpec((b, LANES), lambda i: (i, 0)), pl.BlockSpec((8, b), lambda i: (0, i))),
        scratch_shapes=[pltpu.VMEM((1, LANES), f32)], compiler_params=_cp(("arbitrary",)), name=name,
    )(z_all, bias)


def _fox_cum_bwd(z_all, bias, dcum_t, *, name):
    s = z_all.shape[0]
    b = _attn_block(s)
    nb = s // b
    fcol = _AL["fox_f"] // LANES

    def body(f_ref, b_ref, dc_ref, df_ref, db_ref, carry_ref):
        i = pl.program_id(0)

        @pl.when(i == 0)
        def _():
            carry_ref[...] = jnp.zeros_like(carry_ref)
            db_ref[...] = jnp.zeros_like(db_ref)

        dc = dc_ref[...]
        rev = _split3_right(dc, _tri(b, "row_ge")) + carry_ref[...]
        carry_ref[...] = carry_ref[...] + jnp.sum(dc, axis=1, keepdims=True)
        dlf = jnp.concatenate([rev, jnp.zeros((LANES - 8, b), f32)], axis=0).T
        u = f_ref[...] + b_ref[...]
        df = jnp.where(_iota((b, LANES), 1) < N_HEADS, dlf * (1.0 - _sigmoid(u)), 0.0)
        df_ref[...] = df
        db_ref[...] += jnp.sum(df, axis=0, keepdims=True)

    return pl.pallas_call(
        body, out_shape=(_SDS((s, LANES), f32), _SDS((1, LANES), f32)), grid=(nb,),
        in_specs=[pl.BlockSpec((b, LANES), lambda i: (nb - 1 - i, fcol)), pl.BlockSpec((1, LANES), lambda i: (0, 0)),
                  pl.BlockSpec((8, b), lambda i: (0, nb - 1 - i))],
        out_specs=(pl.BlockSpec((b, LANES), lambda i: (nb - 1 - i, 0)), pl.BlockSpec((1, LANES), lambda i: (0, 0))),
        scratch_shapes=[pltpu.VMEM((8, 1), f32)], compiler_params=_cp(("arbitrary",)), name=name,
    )(z_all, bias, dcum_t)


MLA_QW = 2 * LANES


def _rms_rows(x):
    r = lax.rsqrt(jnp.mean(x * x, axis=-1, keepdims=True) + RMS_EPS)
    return x * r, r


def _mla_prep_fwd(z_all, gq, gkv, wuq, wk, wv, tables, *, name):
    s = z_all.shape[0]
    rb = _row_block(s)
    half = MLA_ROPE // 2

    def body(cq_ref, ckv_ref, kr_ref, gq_ref, gkv_ref, wuq_ref, wk_ref, wv_ref, cos_ref, sa_ref, sb_ref,
             q_ref, k_ref, v_ref):
        cos, sa, sb = cos_ref[...], sa_ref[...], sb_ref[...]
        xh, _ = _rms_rows(cq_ref[...])
        qp = _mm(xh * gq_ref[...], wuq_ref[...])
        kh, _ = _rms_rows(ckv_ref[...])
        nkv = kh * gkv_ref[...]
        kn = _mm(nkv, wk_ref[...])
        v_ref[...] = _mm(nkv, wv_ref[...])
        kr = _rope(kr_ref[...], cos, sa, sb, half)
        for h in range(N_HEADS):
            lo, mid, hi = h * MLA_QW, h * MLA_QW + LANES, (h + 1) * MLA_QW
            q_ref[:, lo:mid] = qp[:, lo:mid]
            q_ref[:, mid:hi] = _rope(qp[:, mid:hi], cos, sa, sb, half)
            k_ref[:, lo:mid] = kn[:, h * LANES:(h + 1) * LANES]
            k_ref[:, mid:hi] = kr

    row = lambda w, cb: pl.BlockSpec((rb, w), lambda i: (i, cb))
    whole = lambda a: pl.BlockSpec(a.shape, lambda i: (0,) * a.ndim)
    return pl.pallas_call(
        body, out_shape=(_SDS((s, N_HEADS * MLA_QW), f32), _SDS((s, N_HEADS * MLA_QW), f32), _SDS((s, GROUP), f32)),
        grid=(s // rb,),
        in_specs=[row(MLA_Q_RANK, _AL["mla_cq"] // MLA_Q_RANK), row(LANES, _AL["mla_ckv"] // LANES),
                  row(LANES, _AL["mla_k_rope"] // LANES), whole(gq), whole(gkv), whole(wuq), whole(wk), whole(wv),
                  row(LANES, 0), row(LANES, 0), row(LANES, 0)],
        out_specs=(row(N_HEADS * MLA_QW, 0), row(N_HEADS * MLA_QW, 0), row(GROUP, 0)),
        compiler_params=_cp(("parallel",)), name=name,
    )(z_all, z_all, z_all, gq, gkv, wuq, wk, wv, *tables)


def _mla_prep_bwd(z_all, gq, gkv, wuq, wk, wv, tables, dq_cat, dk_cat, dv, *, name):
    s = z_all.shape[0]
    rb = _row_block(s)
    half = MLA_ROPE // 2

    def body(cq_ref, ckv_ref, gq_ref, gkv_ref, wuq_ref, wk_ref, wv_ref, cos_ref, sa_ref, sb_ref, dq_ref, dk_ref,
             dv_ref, dcq_ref, dckv_ref, dkr_ref, dwuq_ref, dwk_ref, dwv_ref, dgq_ref, dgkv_ref):
        i = pl.program_id(0)

        @pl.when(i == 0)
        def _():
            for r in (dwuq_ref, dwk_ref, dwv_ref, dgq_ref, dgkv_ref):
                r[...] = jnp.zeros_like(r)

        cos, sa, sb = cos_ref[...], sa_ref[...], sb_ref[...]
        parts, knp = [], []
        dkr = jnp.zeros((rb, LANES), f32)
        for h in range(N_HEADS):
            lo, mid, hi = h * MLA_QW, h * MLA_QW + LANES, (h + 1) * MLA_QW
            parts += [dq_ref[:, lo:mid], _rope(dq_ref[:, mid:hi], cos, sa, sb, half, transpose=True)]
            knp.append(dk_ref[:, lo:mid])
            dkr = dkr + _rope(dk_ref[:, mid:hi], cos, sa, sb, half, transpose=True)
        dkr_ref[...] = dkr
        dqp = jnp.concatenate(parts, axis=1)
        dkn = jnp.concatenate(knp, axis=1)
        dvv = dv_ref[...]

        def norm_bwd(x_ref, g_ref, w_pairs, dx_ref, dg_ref):
            xh, r = _rms_rows(x_ref[...])
            nx = xh * g_ref[...]
            dn = jnp.zeros_like(xh)
            for w_ref, dw_ref, dy in w_pairs:
                dw_ref[...] += _mm_tn(nx, dy)
                dn = dn + _mm_nt(dy, w_ref[...])
            dxh = dn * g_ref[...]
            dx_ref[...] = r * (dxh - xh * jnp.mean(dxh * xh, axis=-1, keepdims=True))
            dg_ref[...] += jnp.sum(dn * xh, axis=0, keepdims=True)

        norm_bwd(cq_ref, gq_ref, [(wuq_ref, dwuq_ref, dqp)], dcq_ref, dgq_ref)
        norm_bwd(ckv_ref, gkv_ref, [(wk_ref, dwk_ref, dkn), (wv_ref, dwv_ref, dvv)], dckv_ref, dgkv_ref)

    row = lambda w, cb: pl.BlockSpec((rb, w), lambda i: (i, cb))
    whole = lambda a: pl.BlockSpec(a.shape, lambda i: (0,) * a.ndim)
    return pl.pallas_call(
        body,
        out_shape=(_SDS((s, MLA_Q_RANK), f32), _SDS((s, LANES), f32), _SDS((s, LANES), f32), _SDS(wuq.shape, f32),
                   _SDS(wk.shape, f32), _SDS(wv.shape, f32), _SDS(gq.shape, f32), _SDS(gkv.shape, f32)),
        grid=(s // rb,),
        in_specs=[row(MLA_Q_RANK, _AL["mla_cq"] // MLA_Q_RANK), row(LANES, _AL["mla_ckv"] // LANES), whole(gq),
                  whole(gkv), whole(wuq), whole(wk), whole(wv), row(LANES, 0), row(LANES, 0), row(LANES, 0),
                  row(N_HEADS * MLA_QW, 0), row(N_HEADS * MLA_QW, 0), row(GROUP, 0)],
        out_specs=(row(MLA_Q_RANK, 0), row(LANES, 0), row(LANES, 0), whole(wuq), whole(wk), whole(wv), whole(gq),
                   whole(gkv)),
        compiler_params=_cp(("arbitrary",)), name=name,
    )(z_all, z_all, gq, gkv, wuq, wk, wv, *tables, dq_cat, dk_cat, dv)


def _silu_grad(x):
    sg = _sigmoid(x)
    return sg * (1.0 + x * (1.0 - sg))


def _nsa_cmp_fwd(ra, rb_, pos, w1, w2, tables, *, name):
    nr = ra.shape[1]
    hw = ra.shape[2]

    def body(ra_ref, rb_ref, pos_ref, w1_ref, w2_ref, cos_ref, sa_ref, sb_ref, out_ref, hp_ref):
        for k in range(2):
            xa = ra_ref[k] + pos_ref[k, :, 0:hw]
            xb = rb_ref[k] + pos_ref[k, :, hw:2 * hw]
            hp = _mm(xa, w1_ref[k, 0:hw, :]) + _mm(xb, w1_ref[k, hw:2 * hw, :])
            hp_ref[k] = hp
            out = _mm(hp * _sigmoid(hp), w2_ref[k])
            if k == 0:
                out = _rope(out, cos_ref[...], sa_ref[...], sb_ref[...], HEAD_DIM // 2)
            out_ref[k] = out

    return pl.pallas_call(body, out_shape=(_SDS((2, nr, HEAD_DIM), f32), _SDS((2, nr, HEAD_DIM), f32)),
                          compiler_params=_cp(), name=name)(ra, rb_, pos, w1, w2, *tables)


def _nsa_cmp_bwd(ra, rb_, pos, w1, w2, tables, hp, dout, *, name):
    nr = ra.shape[1]
    hw = ra.shape[2]

    def body(ra_ref, rb_ref, pos_ref, w1_ref, w2_ref, cos_ref, sa_ref, sb_ref, hp_ref, do_ref,
             dxa_ref, dxb_ref, dw1_ref, dw2_ref):
        for k in range(2):
            d_out = do_ref[k]
            if k == 0:
                d_out = _rope(d_out, cos_ref[...], sa_ref[...], sb_ref[...], HEAD_DIM // 2, transpose=True)
            hpv = hp_ref[k]
            dw2_ref[k] = _mm_tn(hpv * _sigmoid(hpv), d_out)
            dhp = _mm_nt(d_out, w2_ref[k]) * _silu_grad(hpv)
            xa = ra_ref[k] + pos_ref[k, :, 0:hw]
            xb = rb_ref[k] + pos_ref[k, :, hw:2 * hw]
            dw1_ref[k, 0:hw, :] = _mm_tn(xa, dhp)
            dw1_ref[k, hw:2 * hw, :] = _mm_tn(xb, dhp)
            dxa_ref[k] = _mm_nt(dhp, w1_ref[k, 0:hw, :])
            dxb_ref[k] = _mm_nt(dhp, w1_ref[k, hw:2 * hw, :])

    return pl.pallas_call(
        body, out_shape=(_SDS((2, nr, hw), f32), _SDS((2, nr, hw), f32), _SDS(w1.shape, f32), _SDS(w2.shape, f32)),
        compiler_params=_cp(), name=name)(ra, rb_, pos, w1, w2, *tables, hp, dout)


def _nsa_consts(s):
    b = _attn_block(s)
    nr = s // CMP_STRIDE
    n_cmp = (s - CMP_LEN) // CMP_STRIDE + 1
    n_sel = s // SEL_LEN
    cmp_start = np.arange(n_cmp) * CMP_STRIDE
    sel_start = np.arange(n_sel) * SEL_LEN
    overlap = np.clip(np.minimum(cmp_start[:, None] + CMP_LEN, sel_start[None, :] + SEL_LEN)
                      - np.maximum(cmp_start[:, None], sel_start[None, :]), 0, None)
    m2s = np.zeros((nr, LANES), np.float32)
    m2s[:n_cmp, :n_sel] = overlap / CMP_LEN
    e3 = np.zeros((s // b, LANES, b), np.float32)
    tok = np.arange(s)
    e3[tok // b, tok // SEL_LEN, tok % b] = 1.0
    return jnp.asarray(m2s, _MXU), jnp.asarray(e3, _MXU)


def _nsa_masks(i, b, d):
    qpos = i * b + _iota((b, b), 0)
    kpos = (i - d) * b + _iota((b, b), 1)
    return (kpos <= qpos) & (kpos > qpos - WINDOW)


def _nsa_fwd(qr, kvc, ksr, vs, kwr, vw, z_all, m2s, e3, *, name):
    s = qr.shape[0]
    b = _attn_block(s)
    nq = s // b
    nr = kvc.shape[1]
    n_sel = s // SEL_LEN
    top_n = min(SEL_TOPN, n_sel)
    nd = -(-WINDOW // b)
    scale = HEAD_DIM ** -0.5
    bcol = _AL["nsa_branch"] // LANES
    H = N_HEADS

    def body(q_ref, kvc_ref, ks_ref, vs_ref, kw_ref, vw_ref, br_ref, m2s_ref, e3_ref,
             o_ref, oc_ref, os_ref, ow_ref, st_ref, sel_ref, m_sc, l_sc, acc_sc):
        i = pl.program_id(0)
        lane = _iota((b, LANES), 1)
        hs = lambda h: slice(h * HEAD_DIM, (h + 1) * HEAD_DIM)

        cmp_mask = (CMP_STRIDE * _iota((b, nr), 1) + (CMP_LEN - 1)) <= (i * b + _iota((b, nr), 0))
        imp = jnp.zeros((b, LANES), f32)
        stats = jnp.zeros((b, LANES), f32)
        for h in range(H):
            zc = jnp.where(cmp_mask, _mm_nt(q_ref[:, hs(h)], kvc_ref[0]) * scale, NEG_INF)
            m = jnp.max(zc, axis=1, keepdims=True)
            p = jnp.where(cmp_mask, jnp.exp(zc - m), 0.0)
            l = jnp.sum(p, axis=1, keepdims=True)
            some = l > 0.0
            lsafe = jnp.where(some, l, 1.0)
            pc = p * jnp.where(some, 1.0 / lsafe, 0.0)
            oc_ref[:, hs(h)] = _mm(pc, kvc_ref[1])
            imp = imp + _mm(pc, m2s_ref[...])
            stats = jnp.where(lane == h, jnp.where(some, m + jnp.log(lsafe), 0.0), stats)

        cur = jnp.right_shift(i * b + _iota((b, LANES), 0), int(math.log2(SEL_LEN)))
        forced = (lane == 0) | (lane == cur) | (lane == cur - 1)
        score = jnp.where(lane <= cur, jnp.where(forced, FORCED_BONUS, imp), NEG_INF)
        score = jnp.where(lane < n_sel, score, -3e38)
        rank = jnp.zeros((b, LANES), f32)
        for j in range(n_sel):
            col = score[:, j:j + 1]
            rank = rank + jnp.where(col > score, 1.0, jnp.where(col == score, jnp.where(lane > j, 1.0, 0.0), 0.0))
        sel = jnp.where(lane < n_sel, jnp.where(rank < top_n, 1.0, 0.0), 0.0)
        sel_ref[...] = sel
        sel_b = sel.astype(_MXU)

        def reset():
            m_sc[...] = jnp.full(m_sc.shape, NEG_INF, f32)
            l_sc[...] = jnp.zeros_like(l_sc)
            acc_sc[...] = jnp.zeros_like(acc_sc)

        def update(h, z, mask, vch):
            zm = jnp.where(mask, z, NEG_INF)
            m_old = m_sc[h]
            m_new = jnp.maximum(m_old, jnp.max(zm, axis=1, keepdims=True))
            p = jnp.where(mask, jnp.exp(zm - m_new), 0.0)
            alpha = jnp.exp(m_old - m_new)
            l_sc[h] = alpha * l_sc[h] + jnp.sum(p, axis=1, keepdims=True)
            acc_sc[h] = alpha * acc_sc[h] + _mm(p, vch)
            m_sc[h] = m_new

        def finish(out_ref, branch, stats):
            for h in range(H):
                out_ref[:, hs(h)] = acc_sc[h] / l_sc[h]
                stats = jnp.where(lane == 4 * branch + h, m_sc[h] + jnp.log(l_sc[h]), stats)
            return stats

        def sel_chunk(c, diag):
            st = pl.multiple_of(c * b, b)
            mask = _mm(sel_b, e3_ref[c]) > 0.5
            if diag:
                mask = mask & _lower_mask(b, False)
            kch, vch = ks_ref[pl.ds(st, b), :], vs_ref[pl.ds(st, b), :]
            for h in range(H):
                update(h, _mm_nt(q_ref[:, hs(h)], kch) * scale, mask, vch)

        reset()

        def sel_loop(c, carry):
            sel_chunk(c, False)
            return carry

        lax.fori_loop(0, i, sel_loop, 0)
        sel_chunk(i, True)
        stats = finish(os_ref, 1, stats)

        reset()
        for d in range(nd, -1, -1):
            @pl.when(i >= d)
            def _():
                st = pl.multiple_of((i - d) * b, b)
                mask = _nsa_masks(i, b, d)
                kch, vch = kw_ref[pl.ds(st, b), :], vw_ref[pl.ds(st, b), :]
                for h in range(H):
                    update(h, _mm_nt(q_ref[:, hs(h)], kch) * scale, mask, vch)
        stats = finish(ow_ref, 2, stats)
        st_ref[...] = stats

        g = _sigmoid(br_ref[...])
        for h in range(H):
            o_ref[:, hs(h)] = (g[:, 3 * h:3 * h + 1] * oc_ref[:, hs(h)] + g[:, 3 * h + 1:3 * h + 2] * os_ref[:, hs(h)]
                               + g[:, 3 * h + 2:3 * h + 3] * ow_ref[:, hs(h)])

    blk = lambda w: pl.BlockSpec((b, w), lambda i: (i, 0))
    whole = lambda a: pl.BlockSpec(a.shape, lambda i: (0,) * a.ndim)
    return pl.pallas_call(
        body, out_shape=tuple(_SDS((s, GROUP), f32) for _ in range(4)) + (_SDS((s, LANES), f32), _SDS((s, LANES), f32)),
        grid=(nq,),
        in_specs=[blk(GROUP), whole(kvc), whole(ksr), whole(vs), whole(kwr), whole(vw),
                  pl.BlockSpec((b, LANES), lambda i: (i, bcol)), whole(m2s), whole(e3)],
        out_specs=(blk(GROUP),) * 4 + (blk(LANES), blk(LANES)),
        scratch_shapes=[pltpu.VMEM((H, b, 1), f32), pltpu.VMEM((H, b, 1), f32), pltpu.VMEM((H, b, HEAD_DIM), f32)],
        compiler_params=_cp(("parallel",)), name=name,
    )(qr, kvc, ksr, vs, kwr, vw, z_all, m2s, e3)


def _nsa_bwd(do, qr, kvc, ksr, vs, kwr, vw, z_all, oc, os_, ow, stats, sel, e3, *, name):
    s = qr.shape[0]
    b = _attn_block(s)
    nq = s // b
    nr = kvc.shape[1]
    nd = -(-WINDOW // b)
    scale = HEAD_DIM ** -0.5
    bcol = _AL["nsa_branch"] // LANES
    H = N_HEADS

    def body(do_ref, q_ref, kvc_ref, ks_ref, vs_ref, kw_ref, vw_ref, br_ref, oc_ref, os_ref, ow_ref, st_ref, sel_ref,
             e3_ref, dq_ref, dbr_ref, dkvc_ref, dks_ref, dvs_ref, dkw_ref, dvw_ref, dob_sc, delta_sc, dq_sc, kvt_sc):
        i = pl.program_id(0)

        @pl.when(i == 0)
        def _():
            dkvc_ref[...] = jnp.zeros_like(dkvc_ref)
            kvt_sc[...] = jnp.zeros_like(kvt_sc)

        lane = _iota((b, LANES), 1)
        hs = lambda h: slice(h * HEAD_DIM, (h + 1) * HEAD_DIM)
        g = _sigmoid(br_ref[...])
        stats = st_ref[...]
        dbr = jnp.zeros((b, LANES), f32)
        outs = (oc_ref, os_ref, ow_ref)
        for h in range(H):
            doh = do_ref[:, hs(h)]
            for j in range(3):
                gj = g[:, 3 * h + j:3 * h + j + 1]
                dgj = jnp.sum(doh * outs[j][:, hs(h)], axis=1, keepdims=True)
                dbr = jnp.where(lane == 3 * h + j, dgj * gj * (1.0 - gj), dbr)
                dob_sc[j, :, hs(h)] = gj * doh
                delta_sc[j, h] = gj * dgj
        dbr_ref[...] = dbr
        dq_sc[...] = jnp.zeros_like(dq_sc)

        qts = [q_ref[:, hs(h)].T.astype(_MXU) for h in range(H)]
        dobts = {(j, h): dob_sc[j, :, hs(h)].T.astype(_MXU) for j in (1, 2) for h in range(H)}

        def branch(j, h, z, mask, kch, vch):
            p = jnp.where(mask, jnp.exp(jnp.where(mask, z, NEG_INF) - stats[:, 4 * j + h:4 * j + h + 1]), 0.0)
            dob = dob_sc[j, :, hs(h)]
            ds = p * (_mm_nt(dob, vch) - delta_sc[j, h])
            dq_sc[:, hs(h)] += _mm(ds, kch) * scale
            if j == 0:
                return _mm_tn(ds, q_ref[:, hs(h)]) * scale, _mm_tn(p, dob)
            return _mm(qts[h], ds), _mm(dobts[j, h], p)

        cmp_mask = (CMP_STRIDE * _iota((b, nr), 1) + (CMP_LEN - 1)) <= (i * b + _iota((b, nr), 0))
        kc, vc = kvc_ref[0], kvc_ref[1]
        for h in range(H):
            dk, dv = branch(0, h, _mm_nt(q_ref[:, hs(h)], kc) * scale, cmp_mask, kc, vc)
            dkvc_ref[0] += dk
            dkvc_ref[1] += dv

        sel_b = sel_ref[...].astype(_MXU)

        def chunk(j, c, mask, k_ref, v_ref):
            st = pl.multiple_of(c * b, b)
            kch, vch = k_ref[pl.ds(st, b), :], v_ref[pl.ds(st, b), :]
            dk = jnp.zeros((HEAD_DIM, b), f32)
            dv = jnp.zeros((HEAD_DIM, b), f32)
            for h in range(H):
                dkh, dvh = branch(j, h, _mm_nt(q_ref[:, hs(h)], kch) * scale, mask, kch, vch)
                dk, dv = dk + dkh, dv + dvh
            kvt_sc[2 * j - 2, c] += dk
            kvt_sc[2 * j - 1, c] += dv

        def sel_chunk(c, diag):
            mask = _mm(sel_b, e3_ref[c]) > 0.5
            if diag:
                mask = mask & _lower_mask(b, False)
            chunk(1, c, mask, ks_ref, vs_ref)

        def sel_loop(c, carry):
            sel_chunk(c, False)
            return carry

        lax.fori_loop(0, i, sel_loop, 0)
        sel_chunk(i, True)

        for d in range(nd, -1, -1):
            @pl.when(i >= d)
            def _():
                chunk(2, i - d, _nsa_masks(i, b, d), kw_ref, vw_ref)

        dq_ref[...] = dq_sc[...]

        @pl.when(i == nq - 1)
        def _():
            for c in range(nq):
                rows = slice(c * b, (c + 1) * b)
                dks_ref[rows, :] = kvt_sc[0, c].T * scale
                dvs_ref[rows, :] = kvt_sc[1, c].T
                dkw_ref[rows, :] = kvt_sc[2, c].T * scale
                dvw_ref[rows, :] = kvt_sc[3, c].T

    blk = lambda w: pl.BlockSpec((b, w), lambda i: (i, 0))
    whole = lambda a: pl.BlockSpec(a.shape, lambda i: (0,) * a.ndim)
    stream = _SDS((s, HEAD_DIM), f32)
    return pl.pallas_call(
        body, out_shape=(_SDS((s, GROUP), f32), _SDS((s, LANES), f32), _SDS(kvc.shape, f32), stream, stream, stream,
                         stream),
        grid=(nq,),
        in_specs=[blk(GROUP), blk(GROUP), whole(kvc), whole(ksr), whole(vs), whole(kwr), whole(vw),
                  pl.BlockSpec((b, LANES), lambda i: (i, bcol)), blk(GROUP), blk(GROUP), blk(GROUP), blk(LANES),
                  blk(LANES), whole(e3)],
        out_specs=(blk(GROUP), blk(LANES), whole(kvc), whole(ksr), whole(vs), whole(kwr), whole(vw)),
        scratch_shapes=[pltpu.VMEM((3, b, GROUP), f32), pltpu.VMEM((3, H, b, 1), f32), pltpu.VMEM((b, GROUP), f32),
                        pltpu.VMEM((4, nq, HEAD_DIM, b), f32)],
        compiler_params=_cp(("arbitrary",)), name=name,
    )(do, qr, kvc, ksr, vs, kwr, vw, z_all, oc, os_, ow, stats, sel, e3)


def _seg(a, name):
    parts = [lax.slice_in_dim(a, off, off + hi - lo, axis=a.ndim - 1) for off, lo, hi in _PIECES[name]]
    return parts[0] if len(parts) == 1 else jnp.concatenate(parts, axis=a.ndim - 1)


def _to_groups(segs, rows, dtype):
    cols = []
    for s, grp in enumerate(_GROUPS):
        at = 0
        for n, lo, hi, off in sorted(grp, key=lambda t: t[3]):
            if off > at:
                cols.append(jnp.zeros((rows, off - at), dtype))
            cols.append(segs[n][:, lo:hi].astype(dtype))
            at = off + hi - lo
        if at < GROUP_W:
            cols.append(jnp.zeros((rows, GROUP_W - at), dtype))
    return jnp.concatenate(cols, axis=1)


def _piece_from_shard(w_t, s):
    grp = sorted(_GROUPS[s], key=lambda t: t[3])
    ends = [t[3] for t in grp[1:]] + [GROUP_W]
    rows = []
    for (n, lo, hi, off), end in zip(grp, ends):
        first = _ORIG[n] + lo - s * CHIP_COLS
        rows.append(jnp.pad(w_t[:, first:first + hi - lo], ((0, 0), (0, end - off - (hi - lo)), (0, 0))))
    return jnp.concatenate(rows, axis=1)


def _shard_from_piece(g, s):
    return jnp.concatenate([g[:, off:off + hi - lo] for n, lo, hi, off in
                            sorted(_GROUPS[s], key=lambda t: _ORIG[t[0]] + t[1])], axis=1)


def _from_groups(a):
    return jnp.concatenate([_seg(a, n) for n, _ in _SEGS], axis=1)


def _cmp_rows(tok):
    s = tok.shape[0]
    r = tok.reshape(s // CMP_STRIDE, CMP_STRIDE * HEAD_DIM)
    return r, jnp.concatenate([r[1:], jnp.zeros((1, r.shape[1]), r.dtype)], axis=0)


def _cmp_unrows(dxa, dxb):
    s = dxa.shape[0] * CMP_STRIDE
    return (dxa + jnp.concatenate([jnp.zeros((1, dxa.shape[1]), dxa.dtype), dxb[:-1]], axis=0)).reshape(s, HEAD_DIM)


_GATES = ("sb_gate", "nsa_gate", "fox_gate", "mla_gate")


def _layer_fwd(x, p, c, tag):
    s = x.shape[0]
    b = _attn_block(s)
    h = _rms_fwd(x, p["pre_g"], out_dtype=_MXU, name=f"prenorm_{tag}")
    z = _matmul(h, p["w_in"], "nt", bias=p["b_in"], name=f"inproj_{tag}")
    o_sb = _sb_fwd(z, hp=HP_FWD, name=f"sb_fwd_{tag}")

    qr, ksr, kwr = _rope_call([(z, GROUP, _AL["nsa_q"] // GROUP), (z, LANES, _AL["nsa_k_sel"] // LANES),
                               (z, LANES, _AL["nsa_k_win"] // LANES)], c["tabs128"], HEAD_DIM // 2, False,
                              name=f"nsa_rope_{tag}")
    (rak, rbk), (rav, rbv) = _cmp_rows(_seg(z, "nsa_k_cmp")), _cmp_rows(_seg(z, "nsa_v_cmp"))
    ra, rb_ = jnp.stack([rak, rav]), jnp.stack([rbk, rbv])
    kvc, hp = _nsa_cmp_fwd(ra, rb_, p["cmp_pos"], p["cmp_w1"], p["cmp_w2"], c["tabs_cmp"], name=f"nsa_cmp_{tag}")
    vs, vw = _seg(z, "nsa_v_sel"), _seg(z, "nsa_v_win")
    o_nsa, oc, os_, ow, stats, sel = _nsa_fwd(qr, kvc, ksr, vs, kwr, vw, z, c["m2s"], c["e3"], name=f"nsa_fwd_{tag}")

    cum, cum_t8 = _fox_cum_fwd(z, p["fox_bias"], name=f"fox_cum_{tag}")
    cum_t = cum_t8.reshape(8, s // b, 1, b)
    fox_v = _seg(z, "fox_v")
    fcols = (_AL["fox_q"] // HEAD_DIM, _AL["fox_k"] // HEAD_DIM, 0)
    o_fox, lse_fox = _attn_fwd(z, z, fox_v, *fcols, HEAD_DIM, cum, cum_t, scale=HEAD_DIM ** -0.5, hp=HP_FWD,
                               name=f"fox_fwd_{tag}")

    qcat, kcat, vm = _mla_prep_fwd(z, p["gq"], p["gkv"], p["wuq"], p["wk"], p["wv"], c["tabs64"],
                                   name=f"mla_prep_{tag}")
    o_mla, lse_mla = _attn_fwd(qcat, kcat, vm, 0, 0, 0, MLA_QW, None, None, scale=(MLA_NOPE + MLA_ROPE) ** -0.5,
                               hp=HP_BWD, name=f"mla_fwd_{tag}")

    o_all = (o_sb, o_nsa, o_fox, o_mla)
    gates = jnp.concatenate([_seg(z, n) for n in _GATES], axis=1)
    mix = _gate_fwd(o_all, gates, name=f"gate_{tag}")
    u = _matmul(mix, p["w_out"], "nn", name=f"outproj_{tag}")
    y = _postnorm_fwd(u, p["post_g"], x, name=f"postnorm_{tag}")
    saved = dict(x=x, h=h, z=z, qr=qr, ksr=ksr, kwr=kwr, ra=ra, rb=rb_, kvc=kvc, hp=hp, vs=vs, vw=vw, oc=oc, os=os_,
                 ow=ow, stats=stats, sel=sel, cum=cum, cum_t=cum_t, fox_v=fox_v, o_fox=o_fox, lse_fox=lse_fox, qcat=qcat, kcat=kcat,
                 vm=vm, o_mla=o_mla, lse_mla=lse_mla, o_all=o_all, gates=gates, mix=mix, u=u)
    return y, saved


def _layer_bwd(dy, sv, p, c, tag, dw_dtype=f32):
    z = sv["z"]
    s = z.shape[0]
    du, dg_post = _rms_bwd(dy, sv["u"], p["post_g"], name=f"postnorm_bwd_{tag}")
    dmix = _matmul(du, p["w_out"], "nt", name=f"outproj_dx_{tag}")
    dw_out = _matmul(sv["mix"], du, "tn", name=f"outproj_dw_{tag}")
    do_sb, do_nsa, do_fox, do_mla, dgates = _gate_bwd(dmix, sv["o_all"], sv["gates"], name=f"gate_bwd_{tag}")
    dgate = [dgates[:, k * GROUP:(k + 1) * GROUP] for k in range(4)]

    sb_dq, sb_dk, sb_dv = _sb_bwd(z, do_sb, hp=HP_BWD, name=f"sb_bwd_{tag}")

    n_dq, n_dbr, n_dkvc, n_dks, n_dvs, n_dkw, n_dvw = _nsa_bwd(
        do_nsa, sv["qr"], sv["kvc"], sv["ksr"], sv["vs"], sv["kwr"], sv["vw"], z, sv["oc"], sv["os"], sv["ow"],
        sv["stats"], sv["sel"], c["e3"], name=f"nsa_bwd_{tag}")
    dxa, dxb, dw1, dw2 = _nsa_cmp_bwd(sv["ra"], sv["rb"], p["cmp_pos"], p["cmp_w1"], p["cmp_w2"], c["tabs_cmp"],
                                      sv["hp"], n_dkvc, name=f"nsa_cmp_bwd_{tag}")
    n_dq, n_dks, n_dkw = _rope_call([(n_dq, GROUP, 0), (n_dks, LANES, 0), (n_dkw, LANES, 0)], c["tabs128"],
                                    HEAD_DIM // 2, True, name=f"nsa_rope_bwd_{tag}")
    dpos = _colsum(jnp.concatenate([dxa[0], dxb[0], dxa[1], dxb[1]], axis=1), name=f"nsa_dpos_{tag}")
    flat = CMP_LEN * HEAD_DIM

    fcols = (_AL["fox_q"] // HEAD_DIM, _AL["fox_k"] // HEAD_DIM, 0)
    f_dq, f_dk, f_dv, f_dck = _attn_bwd(z, z, sv["fox_v"], *fcols, HEAD_DIM, do_fox, sv["o_fox"], sv["lse_fox"],
                                        sv["cum"], sv["cum_t"], scale=HEAD_DIM ** -0.5, hp=HP_BWD,
                                        name=f"fox_bwd_{tag}")
    dcum_t = jnp.pad(f_dck.reshape(N_HEADS, s), ((0, 8 - N_HEADS), (0, 0)))
    f_df, f_dbias = _fox_cum_bwd(z, p["fox_bias"], dcum_t, name=f"fox_cum_bwd_{tag}")

    m_dq, m_dk, m_dv = _attn_bwd(sv["qcat"], sv["kcat"], sv["vm"], 0, 0, 0, MLA_QW, do_mla, sv["o_mla"], sv["lse_mla"],
                                 None, None, scale=(MLA_NOPE + MLA_ROPE) ** -0.5, hp=HP_BWD, name=f"mla_bwd_{tag}")
    m_dcq, m_dckv, m_dkr, m_dwuq, m_dwk, m_dwv, m_dgq, m_dgkv = _mla_prep_bwd(
        z, p["gq"], p["gkv"], p["wuq"], p["wk"], p["wv"], c["tabs64"], m_dq, m_dk, m_dv, name=f"mla_prep_bwd_{tag}")

    dz = _to_groups(dict(
        sb_q=sb_dq, sb_k=sb_dk, sb_v=sb_dv, sb_gate=dgate[0], nsa_q=n_dq, nsa_k_cmp=_cmp_unrows(dxa[0], dxb[0]),
        nsa_v_cmp=_cmp_unrows(dxa[1], dxb[1]), nsa_k_sel=n_dks, nsa_v_sel=n_dvs, nsa_k_win=n_dkw, nsa_v_win=n_dvw,
        nsa_branch=n_dbr, nsa_gate=dgate[1], fox_q=f_dq, fox_k=f_dk, fox_v=f_dv, fox_f=f_df, fox_gate=dgate[2],
        mla_cq=m_dcq, mla_ckv=m_dckv, mla_k_rope=m_dkr, mla_gate=dgate[3]), s, _MXU)
    dh = _matmul(dz, p["w_in"], "nn", name=f"inproj_dx_{tag}")
    dw_in = _matmul(dz, sv["h"], "tn", out_dtype=dw_dtype, name=f"inproj_dw_{tag}")
    db = _colsum(dz, name=f"inproj_db_{tag}")
    dx, dg_pre = _rms_bwd(dh, sv["x"], p["pre_g"], res=dy, name=f"prenorm_bwd_{tag}")

    qw = MLA_NOPE + MLA_ROPE
    grads = {
        "pre_norm_g": dg_pre[0], "post_norm_g": dg_post[0], "w_in": dw_in, "b_in": _from_groups(db)[0],
        "w_out": dw_out, "fox_forget_bias": f_dbias[0, :N_HEADS],
        "nsa_cmp_pos_k": dpos[0, :flat].reshape(CMP_LEN, HEAD_DIM), "nsa_cmp_w1_k": dw1[0], "nsa_cmp_w2_k": dw2[0],
        "nsa_cmp_pos_v": dpos[0, flat:].reshape(CMP_LEN, HEAD_DIM), "nsa_cmp_w1_v": dw1[1], "nsa_cmp_w2_v": dw2[1],
        "mla_q_norm_g": m_dgq[0],
        "mla_w_uq": jnp.concatenate([m_dwuq[:, MLA_QW * h:MLA_QW * h + qw] for h in range(N_HEADS)], axis=1),
        "mla_kv_norm_g": m_dgkv[0],
        "mla_w_ukv": jnp.concatenate(sum([[m_dwk[:, LANES * h:LANES * (h + 1)], m_dwv[:, LANES * h:LANES * (h + 1)]]
                                          for h in range(N_HEADS)], []), axis=1),
    }
    return dx, grads


def _layer_params(w, l):
    b_in = w["b_in"][l].reshape(1, -1)
    b_segs = {n: b_in[:, _ORIG[n]:_ORIG[n] + wd] for n, wd in _SEGS}
    qw = MLA_NOPE + MLA_ROPE
    w_uq, w_ukv = w["mla_w_uq"][l], w["mla_w_ukv"][l]
    uq = []
    for h in range(N_HEADS):
        uq += [w_uq[:, qw * h:qw * (h + 1)], jnp.zeros((w_uq.shape[0], MLA_QW - qw), w_uq.dtype)]
    kw_ = 2 * LANES
    flat = CMP_LEN * HEAD_DIM
    return dict(
        pre_g=w["pre_norm_g"][l].reshape(1, -1), post_g=w["post_norm_g"][l].reshape(1, -1),
        w_in=w["w_in"][l], b_in=_to_groups(b_segs, 1, f32), w_out=w["w_out"][l],
        fox_bias=jnp.pad(w["fox_forget_bias"][l], (0, LANES - N_HEADS)).reshape(1, LANES),
        cmp_pos=jnp.stack([w["nsa_cmp_pos_k"][l].reshape(1, flat), w["nsa_cmp_pos_v"][l].reshape(1, flat)]),
        cmp_w1=jnp.stack([w["nsa_cmp_w1_k"][l], w["nsa_cmp_w1_v"][l]]),
        cmp_w2=jnp.stack([w["nsa_cmp_w2_k"][l], w["nsa_cmp_w2_v"][l]]),
        gq=w["mla_q_norm_g"][l].reshape(1, -1), gkv=w["mla_kv_norm_g"][l].reshape(1, -1),
        wuq=jnp.concatenate(uq, axis=1),
        wk=jnp.concatenate([w_ukv[:, kw_ * h:kw_ * h + LANES] for h in range(N_HEADS)], axis=1),
        wv=jnp.concatenate([w_ukv[:, kw_ * h + LANES:kw_ * (h + 1)] for h in range(N_HEADS)], axis=1),
    )


def _consts(s):
    pos = jnp.arange(s)
    m2s, e3 = _nsa_consts(s)
    return dict(tabs128=_rope_tables(pos, HEAD_DIM), tabs64=_rope_tables(pos, MLA_ROPE),
                tabs_cmp=_rope_tables(jnp.arange(s // CMP_STRIDE) * CMP_STRIDE + (CMP_LEN - 1), HEAD_DIM),
                m2s=m2s, e3=e3)


def _place():
    return lax.axis_index("x"), lax.axis_index("y"), lax.axis_index("c")


def _other_chips(x, y):
    return [(1 - x, y), (x, 1 - y), (1 - x, 1 - y)]


def _comm_call(body, out_shapes, n_sems, arrs, name):
    return pl.pallas_call(body, out_shape=tuple(out_shapes), in_specs=[_ANY] * len(arrs),
                          out_specs=tuple(_ANY for _ in out_shapes),
                          scratch_shapes=[pltpu.SemaphoreType.DMA((n_sems,)), pltpu.SemaphoreType.DMA((n_sems,))],
                          name=name)(*arrs)


def _gather_chips(arrs, *, name):
    n = len(arrs)

    def body(*refs):
        a_refs, out_refs, send_sems, recv_sems = refs[:n], refs[n:2 * n], refs[2 * n], refs[2 * n + 1]
        x, y, c = _place()
        me = 2 * x + y
        sibling = (x, y, 1 - c)
        chips = _other_chips(x, y)

        def copy(j, k, src, dst, to):
            return pltpu.make_async_remote_copy(src, dst, send_sems.at[6 * j + k], recv_sems.at[6 * j + k],
                                                device_id=to, device_id_type=_MESH)

        first = [copy(j, k, a_refs[j].at[c], out_refs[j].at[me, c], (px, py, c))
                 for k, (px, py) in enumerate(chips) for j in range(n)]
        for cp in first:
            cp.start()
        passed = []
        for k, (px, py) in enumerate(chips):
            for j in range(n):
                landed = out_refs[j].at[2 * px + py, c]
                copy(j, k, a_refs[j].at[c], landed, (px, py, c)).wait_recv()
                passed.append(copy(j, 3 + k, landed, landed, sibling))
                passed[-1].start()
        for k, (px, py) in enumerate(chips):
            for j in range(n):
                copy(j, 3 + k, a_refs[j].at[c], out_refs[j].at[2 * px + py, 1 - c], sibling).wait_recv()
        for cp in first + passed:
            cp.wait_send()

    return _comm_call(body, [_SDS((N_CHIPS,) + a.shape, a.dtype) for a in arrs], 6 * n, arrs, name)


def _alltoall_chips(arrs, modes, *, name):
    n = len(arrs)
    slot = lambda ref, mode, s: _slot_ref(ref, mode, s)
    lane_slots = modes

    def body(*refs):
        g_refs, out_refs, send_sems, recv_sems = refs[:n], refs[n:2 * n], refs[2 * n], refs[2 * n + 1]
        x, y, c = _place()
        me = 2 * x + y

        def copy(j, s):
            return pltpu.make_async_remote_copy(slot(g_refs[j], lane_slots[j], s), out_refs[j].at[me],
                                                send_sems.at[N_CHIPS * j + s], recv_sems.at[N_CHIPS * j + me],
                                                device_id=(s // 2, s % 2, c), device_id_type=_MESH)

        for s in range(N_CHIPS):
            @pl.when(s != me)
            def _():
                for j in range(n):
                    copy(j, s).start()
        for t in range(N_CHIPS):
            @pl.when(t != me)
            def _():
                for j in range(n):
                    pltpu.make_async_remote_copy(slot(g_refs[j], lane_slots[j], t), out_refs[j].at[t],
                                                 send_sems.at[N_CHIPS * j + t], recv_sems.at[N_CHIPS * j + t],
                                                 device_id=(t // 2, t % 2, c), device_id_type=_MESH).wait_recv()
        for s in range(N_CHIPS):
            @pl.when(s != me)
            def _():
                for j in range(n):
                    copy(j, s).wait_send()

    outs = [_SDS((N_CHIPS,) + _slot_shape(a, m), a.dtype) for a, m in zip(arrs, modes)]
    return _comm_call(body, outs, N_CHIPS * n, arrs, name)


def _swap_other_half(arrs, *, name):
    n = len(arrs)

    def body(*refs):
        g_refs, out_refs, send_sems, recv_sems = refs[:n], refs[n:2 * n], refs[2 * n], refs[2 * n + 1]
        x, y, c = _place()
        cps = [pltpu.make_async_remote_copy(g_refs[j].at[:, 1 - c], out_refs[j], send_sems.at[j], recv_sems.at[j],
                                            device_id=(x, y, 1 - c), device_id_type=_MESH) for j in range(n)]
        for cp in cps:
            cp.start()
        for cp in cps:
            cp.wait()

    return _comm_call(body, [_SDS((a.shape[0],) + a.shape[2:], a.dtype) for a in arrs], n, arrs, name)


def _swap_sibling(arrs, *, name):
    n = len(arrs)

    def body(*refs):
        f_refs, out_refs, send_sems, recv_sems = refs[:n], refs[n:2 * n], refs[2 * n], refs[2 * n + 1]
        x, y, c = _place()
        cps = [pltpu.make_async_remote_copy(f_refs[j], out_refs[j], send_sems.at[j], recv_sems.at[j],
                                            device_id=(x, y, 1 - c), device_id_type=_MESH) for j in range(n)]
        for cp in cps:
            cp.start()
        for cp in cps:
            cp.wait()

    return _comm_call(body, [_SDS(a.shape, a.dtype) for a in arrs], n, arrs, name)


_HBM = pl.BlockSpec(memory_space=pltpu.HBM)
_SEM = pl.BlockSpec(memory_space=pltpu.SEMAPHORE)
_EFFECT = pltpu.SideEffectType.DATAFLOW_SIDE_EFFECTING


def _slot_ref(ref, mode, s):
    return ref if mode == "same" else ref.at[s]


def _slot_shape(a, mode):
    return a.shape if mode == "same" else a.shape[1:]


def _send_start(arrs, modes, after, *, name):
    n = len(arrs)
    lands = [lax.empty((N_CHIPS,) + _slot_shape(a, m), a.dtype) for a, m in zip(arrs, modes)]

    def body(*refs):
        srcs, land_refs, send_sems, recv_sems, token = refs[:n], refs[n:2 * n], refs[2 * n + 1], refs[2 * n + 2], refs[-1]
        x, y, c = _place()
        me = 2 * x + y
        for s in range(N_CHIPS):
            @pl.when(s != me)
            def _():
                for j in range(n):
                    pltpu.make_async_remote_copy(_slot_ref(srcs[j], modes[j], s), land_refs[j].at[me],
                                                 send_sems.at[N_CHIPS * j + s], recv_sems.at[N_CHIPS * j + me],
                                                 device_id=(s // 2, s % 2, c), device_id_type=_MESH).start()
        token[...] = jnp.zeros_like(token)

    hbm = lambda a: pltpu.HBM(a.shape, a.dtype)
    sems = pltpu.SemaphoreType.DMA((N_CHIPS * n,))
    out = pl.pallas_call(
        body, name=name, out_shape=(sems, sems, *[hbm(a) for a in arrs], *[hbm(a) for a in lands], _SDS((8, LANES), f32)),
        in_specs=[_HBM] * (2 * n) + [_ANY], out_specs=(_SEM, _SEM, *[_HBM] * (2 * n), pl.BlockSpec(memory_space=pltpu.VMEM)),
        input_output_aliases={j: 2 + j for j in range(2 * n)},
        compiler_params=pltpu.CompilerParams(has_side_effects=_EFFECT),
    )(*[pltpu.with_memory_space_constraint(a, pltpu.HBM) for a in arrs + lands], after)
    return out[:-1], out[-1]


def _send_wait(started, modes, after, *, name):
    send_sems, recv_sems = started[0], started[1]
    n = (len(started) - 2) // 2
    thru = list(started[2:])

    def body(*refs):
        srcs, land_refs, send_sems, recv_sems = refs[:n], refs[n:2 * n], refs[2 * n], refs[2 * n + 1]
        x, y, c = _place()
        me = 2 * x + y
        for s in range(N_CHIPS):
            @pl.when(s != me)
            def _():
                for j in range(n):
                    cp = pltpu.make_async_remote_copy(_slot_ref(srcs[j], modes[j], s), land_refs[j].at[s],
                                                      send_sems.at[N_CHIPS * j + s], recv_sems.at[N_CHIPS * j + s],
                                                      device_id=(s // 2, s % 2, c), device_id_type=_MESH)
                    cp.wait_send()
                    cp.wait_recv()

    hbm = lambda a: pltpu.HBM(a.shape, a.dtype)
    out = pl.pallas_call(
        body, name=name, out_shape=tuple(hbm(a) for a in thru), in_specs=[_HBM] * (2 * n) + [_SEM, _SEM, _ANY],
        out_specs=tuple([_HBM] * (2 * n)), input_output_aliases={j: j for j in range(2 * n)},
        compiler_params=pltpu.CompilerParams(has_side_effects=_EFFECT),
    )(*thru, send_sems, recv_sems, after)
    return list(out[n:])


def _add_my_half(g, r, *, name):
    p, _, h, w = g.shape
    tw = _pick(w, (2048, 1024, 512, 256, 128))
    rb = max(d for d in range(16, h + 1, 16) if h % d == 0 and d * tw * 4 <= (2 << 20))

    def body(c_ref, g_ref, r_ref, o_ref):
        o_ref[...] = (g_ref[...].astype(f32) + r_ref[...].astype(f32)).astype(o_ref.dtype)

    blk = pl.BlockSpec((None, rb, tw), lambda s, i, j, c_ref: (s, i, j))
    grid_spec = pltpu.PrefetchScalarGridSpec(
        num_scalar_prefetch=1, grid=(p, h // rb, w // tw),
        in_specs=[pl.BlockSpec((None, None, rb, tw), lambda s, i, j, c_ref: (s, c_ref[0], i, j)), blk], out_specs=blk)
    c = lax.axis_index("c").astype(jnp.int32).reshape(1)
    return pl.pallas_call(body, out_shape=_SDS((p, h, w), _WIRE), grid_spec=grid_spec,
                          compiler_params=_cp(("parallel", "parallel", "parallel")), name=name)(c, g, r)


_WEIGHTS = ("pre_norm_g", "post_norm_g", "w_in", "b_in", "w_out", "fox_forget_bias", "nsa_cmp_pos_k", "nsa_cmp_w1_k",
            "nsa_cmp_w2_k", "nsa_cmp_pos_v", "nsa_cmp_w1_v", "nsa_cmp_w2_v", "mla_q_norm_g", "mla_w_uq",
            "mla_kv_norm_g", "mla_w_ukv")
_SHARD_AXIS = {"w_in": 2, "w_out": 1, "nsa_cmp_w1_k": 1, "nsa_cmp_w1_v": 1, "mla_w_uq": 2, "mla_w_ukv": 2}
_PACK_UNIT = 16 * LANES


def _pack(arrays, dtype):
    rows = []
    for a in arrays:
        v = a.astype(dtype).reshape(-1)
        pad = (-v.shape[0]) % _PACK_UNIT
        if pad:
            v = jnp.concatenate([v, jnp.zeros((pad,), dtype)])
        rows.append(v.reshape(-1, LANES))
    return jnp.concatenate(rows, axis=0)


def _unpack(flat, shapes):
    out, r = [], 0
    for shp in shapes:
        n = int(np.prod(shp))
        nr = -(-n // _PACK_UNIT) * (_PACK_UNIT // LANES)
        out.append(flat[r:r + nr].reshape(-1)[:n].reshape(shp))
        r += nr
    return out


def kernel(x, pre_norm_g, post_norm_g, w_in, b_in, w_out, fox_forget_bias, nsa_cmp_pos_k, nsa_cmp_w1_k, nsa_cmp_w2_k, nsa_cmp_pos_v, nsa_cmp_w1_v, nsa_cmp_w2_v, mla_q_norm_g, mla_w_uq, mla_kv_norm_g, mla_w_ukv, loss_target, m_pre_norm_g, m_post_norm_g, m_w_in, m_b_in, m_w_out, m_fox_forget_bias, m_nsa_cmp_pos_k, m_nsa_cmp_w1_k, m_nsa_cmp_w2_k, m_nsa_cmp_pos_v, m_nsa_cmp_w1_v, m_nsa_cmp_w2_v, m_mla_q_norm_g, m_mla_w_uq, m_mla_kv_norm_g, m_mla_w_ukv, v_pre_norm_g, v_post_norm_g, v_w_in, v_b_in, v_w_out, v_fox_forget_bias, v_nsa_cmp_pos_k, v_nsa_cmp_w1_k, v_nsa_cmp_w2_k, v_nsa_cmp_pos_v, v_nsa_cmp_w1_v, v_nsa_cmp_w2_v, v_mla_q_norm_g, v_mla_w_uq, v_mla_kv_norm_g, v_mla_w_ukv):
    given = dict(locals())
    local = {n: given[n] for n in _WEIGHTS}
    depth = pre_norm_g.shape[0]
    xs, target = x[0], loss_target[0]
    s = xs.shape[0]
    sharded = [n for n in _WEIGHTS if n in _SHARD_AXIS and n != "w_in"]
    small = [n for n in _WEIGHTS if n not in _SHARD_AXIS]
    chip = 2 * lax.axis_index("x") + lax.axis_index("y")
    core = lax.axis_index("c")
    own = lambda slots, mine: lax.dynamic_update_slice_in_dim(slots, mine[None], chip, axis=0)

    w_in_t = jnp.swapaxes(w_in, 1, 2).astype(_MXU)
    piece = lax.switch(chip, [functools.partial(_piece_from_shard, s=k) for k in range(N_CHIPS)], w_in_t)
    layer_shapes = [local[n].shape[1:] for n in sharded]
    flat = [_pack([local[n][l] for n in sharded], _MXU) for l in range(depth)]
    full = dict(local)
    for n in ["w_in"] + sharded:
        full[n] = []

    def add_layer(w_in_slots, flat_slots_):
        full["w_in"].append(w_in_slots)
        per_chip = [_unpack(flat_slots_[k], layer_shapes) for k in range(N_CHIPS)]
        for j, n in enumerate(sharded):
            full[n].append(jnp.concatenate([per_chip[k][j] for k in range(N_CHIPS)], axis=_SHARD_AXIS[n] - 1))

    halved = [piece[0].reshape(2, GROUP_W // 2, D_MODEL), flat[0].reshape(2, -1, LANES)]
    first_all = [own(a, b) for a, b in zip(_gather_chips(halved, name="gather_weights"), halved)]
    add_layer(first_all[0].reshape(N_CHIPS, GROUP_W, D_MODEL), first_all[1].reshape((N_CHIPS,) + flat[0].shape))
    later = [piece[l] for l in range(1, depth)] + flat[1:]
    started, token = _send_start(later, ["same"] * len(later), first_all[1], name="gather_later_start")
    full["pre_norm_g"] = pre_norm_g + token[0, 0]

    consts = _consts(s)
    params, act, saved = [], xs, []
    for l in range(depth):
        if l == 1:
            landed = [own(a, b) for a, b in zip(_send_wait(started, ["same"] * len(later), act,
                                                           name="gather_later_wait"), later)]
            for k in range(depth - 1):
                add_layer(landed[k], landed[depth - 1 + k])
        params.append(_layer_params(full, l))
        act, sv = _layer_fwd(act, params[l], consts, f"l{l}")
        saved.append(sv)
    dy, loss_parts = _loss_head(act, target, name="loss_head")

    def flat_slots(g, dtype):
        def part(n, k):
            a, ax = g[n], _SHARD_AXIS[n] - 1
            w = a.shape[ax] // N_CHIPS
            return lax.slice_in_dim(a, k * w, (k + 1) * w, axis=ax)
        return jnp.stack([_pack([part(n, k) for n in sharded], dtype) for k in range(N_CHIPS)])

    own_slot = lambda a: lax.dynamic_index_in_dim(a, chip, axis=0, keepdims=False)
    slots_of = lambda g: g["w_in"].reshape(N_CHIPS, GROUP_W, D_MODEL)

    modes = ["slots", "slots"]
    layer_grads, in_flight = [None] * depth, {}
    for l in reversed(range(depth)):
        dy, layer_grads[l] = _layer_bwd(dy, saved[l], params[l], consts, f"l{l}", _WIRE)
        if l > 0:
            wire = [slots_of(layer_grads[l]), flat_slots(layer_grads[l], _WIRE)]
            started, token = _send_start(wire, modes, dy, name=f"reduce_l{l}_start")
            in_flight[l] = (started, wire)
            params[l - 1] = dict(params[l - 1], post_g=params[l - 1]["post_g"] + token[0, 0])
    grad_x = dy[None]
    grads = {n: jnp.stack([layer_grads[l][n] for l in range(depth)]) for n in small}
    loss_row = jnp.concatenate([jnp.sum(loss_parts).reshape(1), jnp.zeros((LANES - 1,), f32)])
    small_shapes = [(LANES,)] + [grads[n].shape for n in small]
    contrib = _pack([loss_row] + [grads[n] for n in small], f32)

    halves = [slots_of(layer_grads[0]).reshape(N_CHIPS, 2, GROUP_W // 2, D_MODEL),
              flat_slots(layer_grads[0], _WIRE).reshape(N_CHIPS, 2, -1, LANES)]
    from_sibling = _swap_other_half(halves, name="reduce_pair")
    pair_sum = [_add_my_half(g, r, name=f"reduce_pair_add{j}") for j, (g, r) in enumerate(zip(halves, from_sibling))]
    from_chips = _alltoall_chips(pair_sum + [contrib], modes + ["same"], name="reduce_chips")
    my_half = [_sum_slots(own(slots, own_slot(ps)), name=f"reduce_chips_add{j}")
               for j, (slots, ps) in enumerate(zip(from_chips, pair_sum))]
    partial = []
    for l in range(1, depth):
        started, wire = in_flight[l]
        landed = _send_wait(started, modes, dy, name=f"reduce_l{l}_wait")
        partial += [_sum_slots(own(slots, own_slot(a)), name=f"reduce_l{l}_add{j}")
                    for j, (slots, a) in enumerate(zip(landed, wire))]
    partial.append(_sum_slots(own(from_chips[2], contrib), name="sum_small"))
    theirs = _swap_sibling(my_half + partial, name="reduce_share")
    first = core == 0
    whole = [jnp.concatenate([jnp.where(first, a, b), jnp.where(first, b, a)], axis=0)
             for a, b in zip(my_half, theirs[:2])]
    whole += [_add2(a[None], b[None], name=f"reduce_cores_add{j}")[0] for j, (a, b) in enumerate(zip(partial, theirs[2:]))]
    unpiece = [functools.partial(_shard_from_piece, s=k) for k in range(N_CHIPS)]
    summed = {"w_in": jnp.stack([lax.switch(chip, unpiece, whole[2 * l].T) for l in range(depth)])}
    rest = [_unpack(whole[2 * l + 1], layer_shapes) for l in range(depth)]
    for j, n in enumerate(sharded):
        summed[n] = jnp.stack([rest[l][j] for l in range(depth)])
    total = _unpack(whole[2 * depth], small_shapes)
    loss = total[0][0]
    summed.update(zip(small, total[1:]))

    deltas, new_m, new_v = {}, {}, {}
    for n in _WEIGHTS:
        deltas[n], new_m[n], new_v[n] = _adamw(local[n], summed[n], given["m_" + n], given["v_" + n], name=f"adamw_{n}")
    return (loss, grad_x, *[summed[n] for n in _WEIGHTS], *[deltas[n] for n in _WEIGHTS],
            *[new_m[n] for n in _WEIGHTS], *[new_v[n] for n in _WEIGHTS])
```

```python
import functools
import math

import numpy as np
import jax
import jax.numpy as jnp
from jax import lax
from jax.experimental import pallas as pl
from jax.experimental.pallas import tpu as pltpu

f32 = jnp.float32
bf16 = jnp.bfloat16
_MXU = jnp.bfloat16
_WIRE = jnp.bfloat16
_SDS = jax.ShapeDtypeStruct
_ANY = pl.BlockSpec(memory_space=pl.ANY)
_MESH = pl.DeviceIdType.MESH

D_MODEL = 2048
N_HEADS = 4
HEAD_DIM = 128
GROUP = 512
RMS_EPS = 1e-6
NEG_INF = -1e30
ROPE_THETA = 10000.0
CMP_LEN, CMP_STRIDE, SEL_LEN, SEL_TOPN, WINDOW = 32, 16, 64, 16, 512
FORCED_BONUS = 1e6
MLA_Q_RANK, MLA_KV_RANK, MLA_NOPE, MLA_ROPE = 384, 128, 128, 64
ADAM_LR, ADAM_B1, ADAM_B2, ADAM_EPS, ADAM_WD, ADAM_STEP = 0.001, 0.9, 0.999, 1e-08, 0.01, 10
LANES = 128
VMEM_LIMIT = 56 * 1024 * 1024
HP_FWD, HP_BWD = 2, 2

_SEGS = (
    ("sb_q", 512), ("sb_k", 512), ("sb_v", 512), ("sb_gate", 512), ("nsa_q", 512), ("nsa_k_cmp", 128),
    ("nsa_v_cmp", 128), ("nsa_k_sel", 128), ("nsa_v_sel", 128), ("nsa_k_win", 128), ("nsa_v_win", 128),
    ("nsa_branch", 12), ("nsa_gate", 512), ("fox_q", 512), ("fox_k", 512), ("fox_v", 512), ("fox_f", 4),
    ("fox_gate", 512), ("mla_cq", 384), ("mla_ckv", 128), ("mla_k_rope", 64), ("mla_gate", 512),
)
_ORIG, _WID = {}, {}
_o = 0
for _n, _w in _SEGS:
    _ORIG[_n], _WID[_n] = _o, _w
    _o += _w
IN_WIDTH = _o
N_CHIPS = 4
CHIP_COLS = IN_WIDTH // N_CHIPS
GROUP_W = 2048
ZW = N_CHIPS * GROUP_W
_GROUPS = (
    (("sb_q", 0, 512, 0), ("sb_k", 0, 512, 512), ("sb_v", 0, 512, 1024), ("sb_gate", 0, 212, 1536)),
    (("nsa_q", 0, 512, 0), ("nsa_k_cmp", 0, 128, 512), ("nsa_v_cmp", 0, 128, 640), ("nsa_k_sel", 0, 128, 768),
     ("nsa_v_sel", 0, 128, 896), ("nsa_k_win", 0, 128, 1024), ("nsa_v_win", 0, 128, 1152), ("nsa_branch", 0, 12, 1280),
     ("sb_gate", 212, 512, 1408), ("nsa_gate", 0, 156, 1712)),
    (("fox_q", 0, 512, 0), ("fox_k", 0, 512, 512), ("fox_v", 0, 368, 1024), ("nsa_gate", 156, 512, 1408)),
    (("mla_cq", 0, 384, 0), ("mla_ckv", 0, 128, 384), ("mla_k_rope", 0, 64, 512), ("fox_f", 0, 4, 640),
     ("fox_v", 368, 512, 768), ("fox_gate", 0, 512, 1024), ("mla_gate", 0, 512, 1536)),
)
_PIECES = {n: [] for n, _ in _SEGS}
for _s, _grp in enumerate(_GROUPS):
    _cover = sorted((_ORIG[n] + lo, _ORIG[n] + hi) for n, lo, hi, _ in _grp)
    assert _cover[0][0] == _s * CHIP_COLS and _cover[-1][1] == (_s + 1) * CHIP_COLS
    assert all(a[1] == b[0] for a, b in zip(_cover, _cover[1:]))
    _ends = sorted((off, off + hi - lo) for _, lo, hi, off in _grp)
    assert all(a[1] <= b[0] for a, b in zip(_ends, _ends[1:])) and _ends[-1][1] <= GROUP_W
    assert _ends[0][0] == 0 and all(e[0] % 16 == 0 for e in _ends)
    for _n, _lo, _hi, _off in _grp:
        _PIECES[_n].append((_s * GROUP_W + _off, _lo, _hi))
_AL = {n: p[0][0] for n, p in _PIECES.items() if len(p) == 1}


def _cp(sem=None):
    return pltpu.CompilerParams(dimension_semantics=sem, vmem_limit_bytes=VMEM_LIMIT)


def _mm(a, b):
    return jnp.dot(a.astype(_MXU), b.astype(_MXU), preferred_element_type=f32)


def _mm_nt(a, b):
    return lax.dot_general(a.astype(_MXU), b.astype(_MXU), (((1,), (1,)), ((), ())), preferred_element_type=f32)


def _mm_tn(a, b):
    return lax.dot_general(a.astype(_MXU), b.astype(_MXU), (((0,), (0,)), ((), ())), preferred_element_type=f32)


def _mm_split(x, t):
    hi = x.astype(_MXU)
    lo = (x - hi.astype(f32)).astype(_MXU)
    return jnp.dot(hi, t, preferred_element_type=f32) + jnp.dot(lo, t, preferred_element_type=f32)


def _sigmoid(x):
    return 1.0 / (1.0 + jnp.exp(-x))


def _iota(shape, dim):
    return lax.broadcasted_iota(jnp.int32, shape, dim)


def _pick(n, prefs):
    for p in prefs:
        if n % p == 0:
            return p
    return n


def _matmul(a, b, mode, *, bias=None, out_dtype=f32, name):
    grouped = b.ndim == 3
    b_shape = (b.shape[0] * b.shape[1], b.shape[2]) if grouped else b.shape
    if mode == "nn":
        (M, K), (K2, N) = a.shape, b_shape
    elif mode == "nt":
        (M, K), (N, K2) = a.shape, b_shape
    else:
        (K, M), (K2, N) = a.shape, b_shape
    assert K == K2
    tm = _pick(M, (1024, 512, 384, 256, 128))
    tn = _pick(N, (1024, 512, 384, 256, 128))
    tk = K if K <= 2048 else _pick(K, (2048, 2432, 1024, 512))
    nk = K // tk
    a_spec = {"nn": pl.BlockSpec((tm, tk), lambda i, j, k: (i, k)),
              "nt": pl.BlockSpec((tm, tk), lambda i, j, k: (i, k)),
              "tn": pl.BlockSpec((tk, tm), lambda i, j, k: (k, i))}[mode]
    if not grouped:
        b_spec = {"nn": pl.BlockSpec((tk, tn), lambda i, j, k: (k, j)),
                  "nt": pl.BlockSpec((tn, tk), lambda i, j, k: (j, k)),
                  "tn": pl.BlockSpec((tk, tn), lambda i, j, k: (k, j))}[mode]
    elif mode == "nt":
        per = b.shape[1] // tn
        b_spec = pl.BlockSpec((None, tn, tk), lambda i, j, k: (j // per, j % per, k))
    else:
        assert mode == "nn"
        per = b.shape[1] // tk
        b_spec = pl.BlockSpec((None, tk, tn), lambda i, j, k: (k // per, k % per, j))
    dot = {"nn": _mm, "nt": _mm_nt, "tn": _mm_tn}[mode]
    has_bias = bias is not None

    def body(*refs):
        if has_bias:
            a_ref, b_ref, bias_ref, o_ref, acc_ref = refs
        else:
            a_ref, b_ref, o_ref, acc_ref = refs
            bias_ref = None
        k = pl.program_id(2)
        part = dot(a_ref[...], b_ref[...])

        def finish(total):
            if has_bias:
                total = total + bias_ref[...]
            o_ref[...] = total.astype(o_ref.dtype)

        if nk == 1:
            finish(part)
        else:
            @pl.when(k == 0)
            def _():
                acc_ref[...] = part

            @pl.when(k > 0)
            def _():
                acc_ref[...] += part

            @pl.when(k == nk - 1)
            def _():
                finish(acc_ref[...])

    in_specs = [a_spec, b_spec]
    args = [a, b]
    if has_bias:
        in_specs.append(pl.BlockSpec((1, tn), lambda i, j, k: (0, j)))
        args.append(bias.reshape(1, N))
    return pl.pallas_call(
        body, out_shape=_SDS((M, N), out_dtype), grid=(M // tm, N // tn, nk),
        in_specs=in_specs, out_specs=pl.BlockSpec((tm, tn), lambda i, j, k: (i, j)),
        scratch_shapes=[pltpu.VMEM((tm, tn), f32)],
        compiler_params=_cp(("parallel", "parallel", "arbitrary")), name=name,
    )(*args)


def _row_block(s):
    return _pick(s, (512, 256, 128))


def _rms_fwd(x, g, *, out_dtype, name):
    s, d = x.shape
    rb = _row_block(s)

    def body(x_ref, g_ref, o_ref):
        xv = x_ref[...]
        r = lax.rsqrt(jnp.mean(xv * xv, axis=-1, keepdims=True) + RMS_EPS)
        o_ref[...] = (xv * r * g_ref[...]).astype(o_ref.dtype)

    return pl.pallas_call(
        body, out_shape=_SDS((s, d), out_dtype), grid=(s // rb,),
        in_specs=[pl.BlockSpec((rb, d), lambda i: (i, 0)), pl.BlockSpec((1, d), lambda i: (0, 0))],
        out_specs=pl.BlockSpec((rb, d), lambda i: (i, 0)), compiler_params=_cp(("parallel",)), name=name,
    )(x, g.reshape(1, d))


def _postnorm_fwd(u, g, x, *, name):
    s, d = u.shape
    rb = _row_block(s)

    def body(u_ref, g_ref, x_ref, o_ref):
        uv = u_ref[...]
        r = lax.rsqrt(jnp.mean(uv * uv, axis=-1, keepdims=True) + RMS_EPS)
        o_ref[...] = x_ref[...] + uv * r * g_ref[...]

    return pl.pallas_call(
        body, out_shape=_SDS((s, d), f32), grid=(s // rb,),
        in_specs=[pl.BlockSpec((rb, d), lambda i: (i, 0)), pl.BlockSpec((1, d), lambda i: (0, 0)),
                  pl.BlockSpec((rb, d), lambda i: (i, 0))],
        out_specs=pl.BlockSpec((rb, d), lambda i: (i, 0)), compiler_params=_cp(("parallel",)), name=name,
    )(u, g.reshape(1, d), x)


def _fold_rows(v):
    r = v.shape[0]
    acc = v[0:8]
    for k in range(1, r // 8):
        acc = acc + v[8 * k:8 * k + 8]
    return acc


def _rms_bwd(dy, x, g, res=None, *, name):
    s, d = x.shape
    rb = _row_block(s)
    nb = s // rb
    has_res = res is not None

    def body(*refs):
        if has_res:
            dy_ref, x_ref, g_ref, res_ref, dx_ref, dg_ref, acc_ref = refs
        else:
            dy_ref, x_ref, g_ref, dx_ref, dg_ref, acc_ref = refs
        i = pl.program_id(0)
        xv = x_ref[...]
        r = lax.rsqrt(jnp.mean(xv * xv, axis=-1, keepdims=True) + RMS_EPS)
        xh = xv * r
        dyv = dy_ref[...]
        dxh = dyv * g_ref[...]
        dx = r * (dxh - xh * jnp.mean(dxh * xh, axis=-1, keepdims=True))
        if has_res:
            dx = dx + res_ref[...]
        dx_ref[...] = dx
        part = _fold_rows(dyv * xh)

        @pl.when(i == 0)
        def _():
            acc_ref[...] = part

        @pl.when(i > 0)
        def _():
            acc_ref[...] += part

        @pl.when(i == nb - 1)
        def _():
            dg_ref[...] = jnp.sum(acc_ref[...], axis=0, keepdims=True)

    blk = pl.BlockSpec((rb, d), lambda i: (i, 0))
    in_specs = [blk, blk, pl.BlockSpec((1, d), lambda i: (0, 0))] + ([blk] if has_res else [])
    args = [dy, x, g.reshape(1, d)] + ([res] if has_res else [])
    return pl.pallas_call(
        body, out_shape=(_SDS((s, d), f32), _SDS((1, d), f32)), grid=(nb,), in_specs=in_specs,
        out_specs=(blk, pl.BlockSpec((1, d), lambda i: (0, 0))),
        scratch_shapes=[pltpu.VMEM((8, d), f32)], compiler_params=_cp(("arbitrary",)), name=name,
    )(*args)


def _loss_head(y, target, *, name):
    s, d = y.shape
    rb = _row_block(s)
    nb = s // rb

    def body(y_ref, t_ref, dy_ref, l_ref):
        i = pl.program_id(0)
        e = y_ref[...] - t_ref[...]
        dy_ref[...] = e * (1.0 / d)
        rows = _fold_rows(e * e)
        part = rows[:, 0:LANES]
        for k in range(1, d // LANES):
            part = part + rows[:, k * LANES:(k + 1) * LANES]
        part = part * (0.5 / d)

        @pl.when(i == 0)
        def _():
            l_ref[...] = part

        @pl.when(i > 0)
        def _():
            l_ref[...] += part

    blk = pl.BlockSpec((rb, d), lambda i: (i, 0))
    return pl.pallas_call(
        body, out_shape=(_SDS((s, d), f32), _SDS((8, LANES), f32)), grid=(nb,), in_specs=[blk, blk],
        out_specs=(blk, pl.BlockSpec((8, LANES), lambda i: (0, 0))),
        compiler_params=_cp(("arbitrary",)), name=name,
    )(y, target)


def _colsum(a, *, name):
    s, n = a.shape
    rb = _row_block(s)
    nb = s // rb
    tn = _pick(n, (2432, 2048, 1024, 512, 384, 128))

    def body(a_ref, o_ref, acc_ref):
        i = pl.program_id(1)
        part = _fold_rows(a_ref[...].astype(f32))

        @pl.when(i == 0)
        def _():
            acc_ref[...] = part

        @pl.when(i > 0)
        def _():
            acc_ref[...] += part

        @pl.when(i == nb - 1)
        def _():
            o_ref[...] = jnp.sum(acc_ref[...], axis=0, keepdims=True)

    return pl.pallas_call(
        body, out_shape=_SDS((1, n), f32), grid=(n // tn, nb),
        in_specs=[pl.BlockSpec((rb, tn), lambda j, i: (i, j))], out_specs=pl.BlockSpec((1, tn), lambda j, i: (0, j)),
        scratch_shapes=[pltpu.VMEM((8, tn), f32)], compiler_params=_cp(("parallel", "arbitrary")), name=name,
    )(a)


def _gate_fwd(outs, gate, *, name):
    s, d = gate.shape
    rb = _row_block(s)
    n = len(outs)
    w = d // n

    def body(*refs):
        g_ref, m_ref = refs[n], refs[n + 1]
        for k in range(n):
            gv = g_ref[:, k * w:(k + 1) * w]
            m_ref[:, k * w:(k + 1) * w] = (refs[k][...] * (gv * _sigmoid(gv))).astype(m_ref.dtype)

    blk = pl.BlockSpec((rb, d), lambda i: (i, 0))
    part = pl.BlockSpec((rb, w), lambda i: (i, 0))
    return pl.pallas_call(body, out_shape=_SDS((s, d), _MXU), grid=(s // rb,), in_specs=[part] * n + [blk],
                          out_specs=blk, compiler_params=_cp(("parallel",)), name=name)(*outs, gate)


def _gate_bwd(dmix, outs, gate, *, name):
    s, d = gate.shape
    rb = _row_block(s)
    n = len(outs)
    w = d // n

    def body(*refs):
        dm_ref, o_refs, g_ref, do_refs, dg_ref = refs[0], refs[1:1 + n], refs[1 + n], refs[2 + n:2 + 2 * n], refs[-1]
        for k in range(n):
            sl = slice(k * w, (k + 1) * w)
            gv = g_ref[:, sl]
            sg = _sigmoid(gv)
            dm = dm_ref[:, sl]
            do_refs[k][...] = dm * (gv * sg)
            dg_ref[:, sl] = dm * o_refs[k][...] * (sg * (1.0 + gv * (1.0 - sg)))

    blk = pl.BlockSpec((rb, d), lambda i: (i, 0))
    part = pl.BlockSpec((rb, w), lambda i: (i, 0))
    return pl.pallas_call(body, out_shape=tuple(_SDS((s, w), f32) for _ in range(n)) + (_SDS((s, d), f32),),
                          grid=(s // rb,), in_specs=[blk] + [part] * n + [blk], out_specs=(part,) * n + (blk,),
                          compiler_params=_cp(("parallel",)), name=name)(dmix, *outs, gate)


def _adamw(w, g, m, v, *, name):
    shape = w.shape
    cols = shape[-1]
    rows = int(np.prod(shape[:-1])) if len(shape) > 1 else 1
    to2 = lambda t: t.reshape(rows, cols)
    rb = rows
    if rows * cols * 4 > (1 << 20):
        rb = max(d for d in range(8, rows + 1, 8) if rows % d == 0 and (d * cols * 4 <= (1600 << 10) or d == 8))

    def body(w_ref, g_ref, m_ref, v_ref, d_ref, nm_ref, nv_ref):
        gv = g_ref[...]
        mn = ADAM_B1 * m_ref[...] + (1.0 - ADAM_B1) * gv
        vn = ADAM_B2 * v_ref[...] + (1.0 - ADAM_B2) * (gv * gv)
        m_hat = mn / (1.0 - ADAM_B1 ** ADAM_STEP)
        v_hat = vn / (1.0 - ADAM_B2 ** ADAM_STEP)
        d_ref[...] = -ADAM_LR * (m_hat / (jnp.sqrt(v_hat) + ADAM_EPS) + ADAM_WD * w_ref[...])
        nm_ref[...] = mn
        nv_ref[...] = vn

    blk = pl.BlockSpec((rb, cols), lambda i: (i, 0))
    out = pl.pallas_call(body, out_shape=tuple(_SDS((rows, cols), f32) for _ in range(3)), grid=(rows // rb,),
                         in_specs=[blk] * 4, out_specs=(blk,) * 3, compiler_params=_cp(("parallel",)),
                         name=name)(to2(w), to2(g), to2(m), to2(v))
    return tuple(t.reshape(shape) for t in out)


def _sum_slots(a, *, name):
    p, n, c = a.shape
    rb = max(d for d in range(8, n + 1, 8) if n % d == 0 and (p * d * c * 4 <= (6 << 20) or d == 8))

    def body(a_ref, o_ref):
        acc = a_ref[0].astype(f32)
        for k in range(1, p):
            acc = acc + a_ref[k].astype(f32)
        o_ref[...] = acc

    return pl.pallas_call(body, out_shape=_SDS((n, c), f32), grid=(n // rb,),
                          in_specs=[pl.BlockSpec((p, rb, c), lambda i: (0, i, 0))],
                          out_specs=pl.BlockSpec((rb, c), lambda i: (i, 0)), compiler_params=_cp(("parallel",)),
                          name=name)(a)


def _add2(a, b, *, name):
    p, n, c = a.shape
    rb = max(d for d in range(8, n + 1, 8) if n % d == 0 and (d * c * 4 <= (2 << 20) or d == 8))

    def body(a_ref, b_ref, o_ref):
        o_ref[...] = a_ref[...] + b_ref[...]

    blk = pl.BlockSpec((1, rb, c), lambda s, i: (s, i, 0))
    return pl.pallas_call(body, out_shape=_SDS((p, n, c), f32), grid=(p, n // rb), in_specs=[blk, blk], out_specs=blk,
                          compiler_params=_cp(("parallel", "parallel")), name=name)(a, b)


def _rope_tables(pos, dim):
    half = dim // 2
    inv = ROPE_THETA ** (-jnp.arange(half, dtype=f32) / half)
    ang = pos.astype(f32)[:, None] * inv[None, :]
    c, s = jnp.cos(ang), jnp.sin(ang)
    z = jnp.zeros_like(c)
    pad = [jnp.zeros((pos.shape[0], LANES - dim), f32)] if dim < LANES else []
    return (jnp.concatenate([c, c] + pad, axis=1), jnp.concatenate([-s, z] + pad, axis=1),
            jnp.concatenate([z, s] + pad, axis=1))


def _rope(x, cos, sa, sb, half, transpose=False):
    if transpose:
        return x * cos + pltpu.roll(x * sa, half, 1) + pltpu.roll(x * sb, LANES - half, 1)
    return x * cos + pltpu.roll(x, LANES - half, 1) * sa + pltpu.roll(x, half, 1) * sb


def _rope_call(items, tables, half, transpose, *, name):
    s = items[0][0].shape[0]
    rb = _row_block(s)
    n = len(items)

    def body(*refs):
        cos, sa, sb = refs[n][...], refs[n + 1][...], refs[n + 2][...]
        for k in range(n):
            x_ref, o_ref = refs[k], refs[n + 3 + k]
            for j in range(items[k][1] // LANES):
                sl = slice(j * LANES, (j + 1) * LANES)
                o_ref[:, sl] = _rope(x_ref[:, sl], cos, sa, sb, half, transpose)

    in_specs = [pl.BlockSpec((rb, w), functools.partial(lambda i, cb: (i, cb), cb=cb)) for _, w, cb in items]
    in_specs += [pl.BlockSpec((rb, LANES), lambda i: (i, 0))] * 3
    out_specs = tuple(pl.BlockSpec((rb, w), lambda i: (i, 0)) for _, w, _ in items)
    return pl.pallas_call(
        body, out_shape=tuple(_SDS((s, w), f32) for _, w, _ in items), grid=(s // rb,), in_specs=in_specs,
        out_specs=out_specs, compiler_params=_cp(("parallel",)), name=name,
    )(*[a for a, _, _ in items], *tables)


def _attn_block(s):
    return _pick(s, (512, 256, 128))


def _lower_mask(b, strict):
    r, c = _iota((b, b), 0), _iota((b, b), 1)
    return (c < r) if strict else (c <= r)


def _pick_lane(block, h):
    return jnp.sum(jnp.where(_iota(block.shape, 1) == h, block, 0.0), axis=1, keepdims=True)


def _head_bias(cum_blk, g, j, hp):
    if hp == N_HEADS:
        return cum_blk[:, j:j + 1]
    return _pick_lane(cum_blk, g * hp + j)


def _attn_fwd(q, k, v, qcol, kcol, vcol, dq, cum, cum_t, *, scale, hp, name):
    s = q.shape[0]
    b = _attn_block(s)
    nq = s // b
    has_bias = cum is not None
    assert qcol % hp == 0 and kcol % hp == 0 and vcol % hp == 0

    def body(*refs):
        if has_bias:
            q_ref, k_ref, v_ref, cum_ref, cumt_ref, o_ref, lse_ref = refs
        else:
            q_ref, k_ref, v_ref, o_ref, lse_ref = refs
        g, i = pl.program_id(0), pl.program_id(1)
        qs = [q_ref[:, j * dq:(j + 1) * dq].astype(_MXU) for j in range(hp)]
        cqs = [_head_bias(cum_ref[...], g, j, hp) for j in range(hp)] if has_bias else None

        def chunk(c, carry, diag):
            st = pl.multiple_of(c * b, b)
            mask = _lower_mask(b, False) if diag else None
            out = []
            for j in range(hp):
                m, l, acc = carry[j]
                z = _mm_nt(qs[j], k_ref[pl.ds(st, b), j * dq:(j + 1) * dq]) * scale
                if has_bias:
                    z = z + (cqs[j] - cumt_ref[j, c])
                if diag:
                    z = jnp.where(mask, z, NEG_INF)
                m_new = jnp.maximum(m, jnp.max(z, axis=1, keepdims=True))
                p = jnp.exp(z - m_new)
                if diag:
                    p = jnp.where(mask, p, 0.0)
                alpha = jnp.exp(m - m_new)
                l = alpha * l + jnp.sum(p, axis=1, keepdims=True)
                acc = alpha * acc + _mm(p, v_ref[pl.ds(st, b), j * HEAD_DIM:(j + 1) * HEAD_DIM])
                out.append((m_new, l, acc))
            return tuple(out)

        init = tuple((jnp.full((b, 1), NEG_INF, f32), jnp.zeros((b, 1), f32), jnp.zeros((b, HEAD_DIM), f32))
                     for _ in range(hp))
        carry = lax.fori_loop(0, i, lambda c, cr: chunk(c, cr, False), init)
        for j, (m, l, acc) in enumerate(chunk(i, carry, True)):
            o_ref[:, j * HEAD_DIM:(j + 1) * HEAD_DIM] = acc / l
            lse_ref[j] = m + jnp.log(l)

    in_specs = [pl.BlockSpec((b, hp * dq), lambda g, i: (i, qcol // hp + g)),
                pl.BlockSpec((s, hp * dq), lambda g, i: (0, kcol // hp + g)),
                pl.BlockSpec((s, hp * HEAD_DIM), lambda g, i: (0, vcol // hp + g))]
    args = [q, k, v]
    if has_bias:
        in_specs += [pl.BlockSpec((b, LANES), lambda g, i: (i, 0)),
                     pl.BlockSpec((hp, nq, 1, b), lambda g, i: (g, 0, 0, 0))]
        args += [cum, cum_t]
    return pl.pallas_call(
        body, out_shape=(_SDS((s, N_HEADS * HEAD_DIM), f32), _SDS((N_HEADS, s, 1), f32)), grid=(N_HEADS // hp, nq),
        in_specs=in_specs,
        out_specs=(pl.BlockSpec((b, hp * HEAD_DIM), lambda g, i: (i, g)),
                   pl.BlockSpec((hp, b, 1), lambda g, i: (g, i, 0))),
        compiler_params=_cp(("parallel", "parallel")), name=name,
    )(*args)


def _attn_bwd(q, k, v, qcol, kcol, vcol, dq, do, o, lse, cum, cum_t, *, scale, hp, name):
    s = q.shape[0]
    b = _attn_block(s)
    nq = s // b
    has_bias = cum is not None
    assert qcol % hp == 0 and kcol % hp == 0 and vcol % hp == 0
    hd = lambda j: slice(j * HEAD_DIM, (j + 1) * HEAD_DIM)
    hq = lambda j: slice(j * dq, (j + 1) * dq)

    def body(*refs):
        if has_bias:
            (q_ref, k_ref, v_ref, do_ref, o_ref, lse_ref, cum_ref, cumt_ref, dq_ref, dk_ref, dv_ref, dck_ref,
             dkt_sc, dvt_sc, p_sc, dp_sc) = refs
        else:
            q_ref, k_ref, v_ref, do_ref, o_ref, lse_ref, dq_ref, dk_ref, dv_ref, dkt_sc, dvt_sc = refs
        g, i = pl.program_id(0), pl.program_id(1)

        @pl.when(i == 0)
        def _():
            dkt_sc[...] = jnp.zeros_like(dkt_sc)
            dvt_sc[...] = jnp.zeros_like(dvt_sc)
            if has_bias:
                dck_ref[...] = jnp.zeros_like(dck_ref)

        qs = [q_ref[:, hq(j)].astype(_MXU) for j in range(hp)]
        dos = [do_ref[:, hd(j)].astype(_MXU) for j in range(hp)]
        qts = [q_ref[:, hq(j)].T.astype(_MXU) for j in range(hp)]
        dots = [do_ref[:, hd(j)].T.astype(_MXU) for j in range(hp)]
        lses = [lse_ref[j] for j in range(hp)]
        cqs = [_head_bias(cum_ref[...], g, j, hp) for j in range(hp)] if has_bias else None

        def probs(j, c, diag):
            st = pl.multiple_of(c * b, b)
            z = _mm_nt(qs[j], k_ref[pl.ds(st, b), hq(j)]) * scale
            if has_bias:
                z = z + (cqs[j] - cumt_ref[j, c])
            p = jnp.exp(z - lses[j])
            if diag:
                p = jnp.where(_lower_mask(b, False), p, 0.0)
            return p, _mm_nt(dos[j], v_ref[pl.ds(st, b), hd(j)])

        if has_bias:
            def first(c, accs, diag):
                out = []
                for j in range(hp):
                    p, dp = probs(j, c, diag)
                    p_sc[j, c] = p
                    dp_sc[j, c] = dp
                    out.append(accs[j] + jnp.sum(p * dp, axis=1, keepdims=True))
                return tuple(out)

            deltas = lax.fori_loop(0, i, lambda c, a: first(c, a, False),
                                   tuple(jnp.zeros((b, 1), f32) for _ in range(hp)))
            deltas = first(i, deltas, True)
        else:
            deltas = [jnp.sum(do_ref[:, hd(j)] * o_ref[:, hd(j)], axis=1, keepdims=True) for j in range(hp)]

        def chunk(c, dq_accs, diag):
            st = pl.multiple_of(c * b, b)
            out = []
            for j in range(hp):
                p, dp = (p_sc[j, c], dp_sc[j, c]) if has_bias else probs(j, c, diag)
                ds = p * (dp - deltas[j])
                dkt_sc[j, c] += _mm(qts[j], ds)
                dvt_sc[j, c] += _mm(dots[j], p)
                if has_bias:
                    dck_ref[j, c] += -jnp.sum(ds, axis=0, keepdims=True)
                out.append(dq_accs[j] + _mm(ds, k_ref[pl.ds(st, b), hq(j)]))
            return tuple(out)

        accs = lax.fori_loop(0, i, lambda c, a: chunk(c, a, False), tuple(jnp.zeros((b, dq), f32) for _ in range(hp)))
        for j, acc in enumerate(chunk(i, accs, True)):
            dq_ref[:, hq(j)] = acc * scale

        @pl.when(i == nq - 1)
        def _():
            for j in range(hp):
                for c in range(nq):
                    dk_ref[c * b:(c + 1) * b, hq(j)] = dkt_sc[j, c].T * scale
                    dv_ref[c * b:(c + 1) * b, hd(j)] = dvt_sc[j, c].T

    rowq = pl.BlockSpec((b, hp * HEAD_DIM), lambda g, i: (i, g))
    in_specs = [pl.BlockSpec((b, hp * dq), lambda g, i: (i, qcol // hp + g)),
                pl.BlockSpec((s, hp * dq), lambda g, i: (0, kcol // hp + g)),
                pl.BlockSpec((s, hp * HEAD_DIM), lambda g, i: (0, vcol // hp + g)), rowq, rowq,
                pl.BlockSpec((hp, b, 1), lambda g, i: (g, i, 0))]
    args = [q, k, v, do, o, lse]
    out_shape = [_SDS((s, N_HEADS * dq), f32), _SDS((s, N_HEADS * dq), f32), _SDS((s, N_HEADS * HEAD_DIM), f32)]
    out_specs = [pl.BlockSpec((b, hp * dq), lambda g, i: (i, g)), pl.BlockSpec((s, hp * dq), lambda g, i: (0, g)),
                 pl.BlockSpec((s, hp * HEAD_DIM), lambda g, i: (0, g))]
    if has_bias:
        in_specs += [pl.BlockSpec((b, LANES), lambda g, i: (i, 0)),
                     pl.BlockSpec((hp, nq, 1, b), lambda g, i: (g, 0, 0, 0))]
        args += [cum, cum_t]
        out_shape.append(_SDS((N_HEADS, nq, 1, b), f32))
        out_specs.append(pl.BlockSpec((hp, nq, 1, b), lambda g, i: (g, 0, 0, 0)))
    return pl.pallas_call(
        body, out_shape=tuple(out_shape), grid=(N_HEADS // hp, nq), in_specs=in_specs, out_specs=tuple(out_specs),
        scratch_shapes=[pltpu.VMEM((hp, nq, dq, b), f32), pltpu.VMEM((hp, nq, HEAD_DIM, b), f32)]
        + ([pltpu.VMEM((hp, nq, b, b), f32)] * 2 if has_bias else []),
        compiler_params=_cp(("parallel", "arbitrary")), name=name,
    )(*args)


def _tri(b, kind):
    r, c = _iota((b, b), 0), _iota((b, b), 1)
    cond = {"row_gt": r > c, "row_lt": r < c, "row_ge": r >= c, "row_le": r <= c}[kind]
    return jnp.where(cond, 1.0, 0.0).astype(_MXU)


def _log_keep(z):
    return -(jnp.maximum(z, 0.0) + jnp.log(1.0 + jnp.exp(-jnp.abs(z))))


def _sb_fwd(z_all, *, hp, name):
    s = z_all.shape[0]
    b = _attn_block(s)
    nq = s // b
    scale = HEAD_DIM ** -0.5
    qcol, kcol, vcol = (_AL[n] // (hp * HEAD_DIM) for n in ("sb_q", "sb_k", "sb_v"))
    hd = lambda j: slice(j * HEAD_DIM, (j + 1) * HEAD_DIM)

    def body(q_ref, k_ref, v_ref, o_ref):
        i = pl.program_id(1)
        qs = [q_ref[:, hd(j)].astype(_MXU) for j in range(hp)]
        upper = _tri(b, "row_gt")

        def chunk(c, carry, diag):
            st = pl.multiple_of(c * b, b)
            mask = _lower_mask(b, True) if diag else None
            out = []
            for j in range(hp):
                rsum, acc = carry[j]
                z = _mm_nt(qs[j], k_ref[pl.ds(st, b), hd(j)]) * scale
                lk = _log_keep(z)
                if diag:
                    lk = jnp.where(mask, lk, 0.0)
                a = z + lk + _mm_split(lk, upper) + rsum
                if diag:
                    a = jnp.where(mask, a, NEG_INF)
                acc = acc + _mm(jnp.exp(a), v_ref[pl.ds(st, b), hd(j)])
                out.append((rsum + jnp.sum(lk, axis=1, keepdims=True), acc))
            return tuple(out)

        init = tuple((jnp.zeros((b, 1), f32), jnp.zeros((b, HEAD_DIM), f32)) for _ in range(hp))
        carry = lax.fori_loop(0, i, lambda t, cr: chunk(i - 1 - t, cr, False), chunk(i, init, True))
        for j in range(hp):
            o_ref[:, hd(j)] = carry[j][1]

    w = hp * HEAD_DIM
    return pl.pallas_call(
        body, out_shape=_SDS((s, GROUP), f32), grid=(N_HEADS // hp, nq),
        in_specs=[pl.BlockSpec((b, w), lambda g, i: (i, qcol + g)), pl.BlockSpec((s, w), lambda g, i: (0, kcol + g)),
                  pl.BlockSpec((s, w), lambda g, i: (0, vcol + g))],
        out_specs=pl.BlockSpec((b, w), lambda g, i: (i, g)),
        compiler_params=_cp(("parallel", "parallel")), name=name,
    )(z_all, z_all, z_all)


def _sb_bwd(z_all, do, *, hp, name):
    s = z_all.shape[0]
    b = _attn_block(s)
    nq = s // b
    scale = HEAD_DIM ** -0.5
    qcol, kcol, vcol = (_AL[n] // (hp * HEAD_DIM) for n in ("sb_q", "sb_k", "sb_v"))
    hd = lambda j: slice(j * HEAD_DIM, (j + 1) * HEAD_DIM)

    def body(q_ref, k_ref, v_ref, do_ref, dq_ref, dk_ref, dv_ref, z_sc, lk_sc, r_sc):
        i = pl.program_id(1)

        @pl.when(i == 0)
        def _():
            dk_ref[...] = jnp.zeros_like(dk_ref)
            dv_ref[...] = jnp.zeros_like(dv_ref)

        qs = [q_ref[:, hd(j)].astype(_MXU) for j in range(hp)]
        dos = [do_ref[:, hd(j)].astype(_MXU) for j in range(hp)]
        upper = _tri(b, "row_gt")
        lower = _tri(b, "row_lt")

        def scores(c, rsums, diag):
            st = pl.multiple_of(c * b, b)
            out = []
            for j in range(hp):
                z = _mm_nt(qs[j], k_ref[pl.ds(st, b), hd(j)]) * scale
                lk = _log_keep(z)
                if diag:
                    lk = jnp.where(_lower_mask(b, True), lk, 0.0)
                z_sc[j, c] = z
                lk_sc[j, c] = lk
                r_sc[j, c] = _mm_split(lk, upper) + rsums[j]
                out.append(rsums[j] + jnp.sum(lk, axis=1, keepdims=True))
            return tuple(out)

        rsums = scores(i, tuple(jnp.zeros((b, 1), f32) for _ in range(hp)), True)
        lax.fori_loop(0, i, lambda t, r: scores(i - 1 - t, r, False), rsums)

        def grads(c, carry, diag):
            st = pl.multiple_of(c * b, b)
            mask = _lower_mask(b, True) if diag else None
            out = []
            for j in range(hp):
                psum, dq_acc = carry[j]
                z, lk = z_sc[j, c], lk_sc[j, c]
                lb = z + lk
                a = lb + r_sc[j, c]
                if diag:
                    a = jnp.where(mask, a, NEG_INF)
                w = jnp.exp(a)
                e = _mm_nt(dos[j], v_ref[pl.ds(st, b), hd(j)]) * w
                before = _mm_split(e, lower) + psum
                dz = e * jnp.exp(lk) - before * jnp.exp(lb)
                if diag:
                    dz = jnp.where(mask, dz, 0.0)
                dk_ref[pl.ds(st, b), hd(j)] += _mm_tn(dz, qs[j]) * scale
                dv_ref[pl.ds(st, b), hd(j)] += _mm_tn(w, dos[j])
                out.append((psum + jnp.sum(e, axis=1, keepdims=True), dq_acc + _mm(dz, k_ref[pl.ds(st, b), hd(j)])))
            return tuple(out)

        init = tuple((jnp.zeros((b, 1), f32), jnp.zeros((b, HEAD_DIM), f32)) for _ in range(hp))
        carry = grads(i, lax.fori_loop(0, i, lambda c, cr: grads(c, cr, False), init), True)
        for j in range(hp):
            dq_ref[:, hd(j)] = carry[j][1] * scale

    w = hp * HEAD_DIM
    blk = pl.BlockSpec((b, w), lambda g, i: (i, g))
    full = pl.BlockSpec((s, w), lambda g, i: (0, g))
    return pl.pallas_call(
        body, out_shape=tuple(_SDS((s, GROUP), f32) for _ in range(3)), grid=(N_HEADS // hp, nq),
        in_specs=[pl.BlockSpec((b, w), lambda g, i: (i, qcol + g)), pl.BlockSpec((s, w), lambda g, i: (0, kcol + g)),
                  pl.BlockSpec((s, w), lambda g, i: (0, vcol + g)), blk],
        out_specs=(blk, full, full),
        scratch_shapes=[pltpu.VMEM((hp, nq, b, b), f32)] * 3,
        compiler_params=_cp(("parallel", "arbitrary")), name=name,
    )(z_all, z_all, z_all, do)


def _split3_left(t, x):
    hi = x.astype(_MXU)
    r1 = x - hi.astype(f32)
    mid = r1.astype(_MXU)
    lo = (r1 - mid.astype(f32)).astype(_MXU)
    dot = functools.partial(jnp.dot, preferred_element_type=f32)
    return dot(t, hi) + dot(t, mid) + dot(t, lo)


def _split3_right(x, t):
    hi = x.astype(_MXU)
    r1 = x - hi.astype(f32)
    mid = r1.astype(_MXU)
    lo = (r1 - mid.astype(f32)).astype(_MXU)
    dot = functools.partial(jnp.dot, preferred_element_type=f32)
    return dot(hi, t) + dot(mid, t) + dot(lo, t)


def _fox_cum_fwd(z_all, bias, *, name):
    s = z_all.shape[0]
    b = _attn_block(s)
    fcol = _AL["fox_f"] // LANES

    def body(f_ref, b_ref, cum_ref, cumt_ref, carry_ref):
        i = pl.program_id(0)

        @pl.when(i == 0)
        def _():
            carry_ref[...] = jnp.zeros_like(carry_ref)

        u = f_ref[...] + b_ref[...]
        lf = jnp.minimum(u, 0.0) - jnp.log1p(jnp.exp(-jnp.abs(u)))
        cum = _split3_left(_tri(b, "row_ge"), lf) + carry_ref[...]
        cum_ref[...] = cum
        cumt_ref[...] = cum.T[0:8, :]
        carry_ref[...] = cum_ref[b - 1:b, :]

    return pl.pallas_call(
        body, out_shape=(_SDS((s, LANES), f32), _SDS((8, s), f32)), grid=(s // b,),
        in_specs=[pl.BlockSpec((b, LANES), lambda i: (i, fcol)), pl.BlockSpec((1, LANES), lambda i: (0, 0))],
        out_specs=(pl.BlockSpec((b, LANES), lambda i: (i, 0)), pl.BlockSpec((8, b), lambda i: (0, i))),
        scratch_shapes=[pltpu.VMEM((1, LANES), f32)], compiler_params=_cp(("arbitrary",)), name=name,
    )(z_all, bias)


def _fox_cum_bwd(z_all, bias, dcum_t, *, name):
    s = z_all.shape[0]
    b = _attn_block(s)
    nb = s // b
    fcol = _AL["fox_f"] // LANES

    def body(f_ref, b_ref, dc_ref, df_ref, db_ref, carry_ref):
        i = pl.program_id(0)

        @pl.when(i == 0)
        def _():
            carry_ref[...] = jnp.zeros_like(carry_ref)
            db_ref[...] = jnp.zeros_like(db_ref)

        dc = dc_ref[...]
        rev = _split3_right(dc, _tri(b, "row_ge")) + carry_ref[...]
        carry_ref[...] = carry_ref[...] + jnp.sum(dc, axis=1, keepdims=True)
        dlf = jnp.concatenate([rev, jnp.zeros((LANES - 8, b), f32)], axis=0).T
        u = f_ref[...] + b_ref[...]
        df = jnp.where(_iota((b, LANES), 1) < N_HEADS, dlf * (1.0 - _sigmoid(u)), 0.0)
        df_ref[...] = df
        db_ref[...] += jnp.sum(df, axis=0, keepdims=True)

    return pl.pallas_call(
        body, out_shape=(_SDS((s, LANES), f32), _SDS((1, LANES), f32)), grid=(nb,),
        in_specs=[pl.BlockSpec((b, LANES), lambda i: (nb - 1 - i, fcol)), pl.BlockSpec((1, LANES), lambda i: (0, 0)),
                  pl.BlockSpec((8, b), lambda i: (0, nb - 1 - i))],
        out_specs=(pl.BlockSpec((b, LANES), lambda i: (nb - 1 - i, 0)), pl.BlockSpec((1, LANES), lambda i: (0, 0))),
        scratch_shapes=[pltpu.VMEM((8, 1), f32)], compiler_params=_cp(("arbitrary",)), name=name,
    )(z_all, bias, dcum_t)


MLA_QW = 2 * LANES


def _rms_rows(x):
    r = lax.rsqrt(jnp.mean(x * x, axis=-1, keepdims=True) + RMS_EPS)
    return x * r, r


def _mla_prep_fwd(z_all, gq, gkv, wuq, wk, wv, tables, *, name):
    s = z_all.shape[0]
    rb = _row_block(s)
    half = MLA_ROPE // 2

    def body(cq_ref, ckv_ref, kr_ref, gq_ref, gkv_ref, wuq_ref, wk_ref, wv_ref, cos_ref, sa_ref, sb_ref,
             q_ref, k_ref, v_ref):
        cos, sa, sb = cos_ref[...], sa_ref[...], sb_ref[...]
        xh, _ = _rms_rows(cq_ref[...])
        qp = _mm(xh * gq_ref[...], wuq_ref[...])
        kh, _ = _rms_rows(ckv_ref[...])
        nkv = kh * gkv_ref[...]
        kn = _mm(nkv, wk_ref[...])
        v_ref[...] = _mm(nkv, wv_ref[...])
        kr = _rope(kr_ref[...], cos, sa, sb, half)
        for h in range(N_HEADS):
            lo, mid, hi = h * MLA_QW, h * MLA_QW + LANES, (h + 1) * MLA_QW
            q_ref[:, lo:mid] = qp[:, lo:mid]
            q_ref[:, mid:hi] = _rope(qp[:, mid:hi], cos, sa, sb, half)
            k_ref[:, lo:mid] = kn[:, h * LANES:(h + 1) * LANES]
            k_ref[:, mid:hi] = kr

    row = lambda w, cb: pl.BlockSpec((rb, w), lambda i: (i, cb))
    whole = lambda a: pl.BlockSpec(a.shape, lambda i: (0,) * a.ndim)
    return pl.pallas_call(
        body, out_shape=(_SDS((s, N_HEADS * MLA_QW), f32), _SDS((s, N_HEADS * MLA_QW), f32), _SDS((s, GROUP), f32)),
        grid=(s // rb,),
        in_specs=[row(MLA_Q_RANK, _AL["mla_cq"] // MLA_Q_RANK), row(LANES, _AL["mla_ckv"] // LANES),
                  row(LANES, _AL["mla_k_rope"] // LANES), whole(gq), whole(gkv), whole(wuq), whole(wk), whole(wv),
                  row(LANES, 0), row(LANES, 0), row(LANES, 0)],
        out_specs=(row(N_HEADS * MLA_QW, 0), row(N_HEADS * MLA_QW, 0), row(GROUP, 0)),
        compiler_params=_cp(("parallel",)), name=name,
    )(z_all, z_all, z_all, gq, gkv, wuq, wk, wv, *tables)


def _mla_prep_bwd(z_all, gq, gkv, wuq, wk, wv, tables, dq_cat, dk_cat, dv, *, name):
    s = z_all.shape[0]
    rb = _row_block(s)
    half = MLA_ROPE // 2

    def body(cq_ref, ckv_ref, gq_ref, gkv_ref, wuq_ref, wk_ref, wv_ref, cos_ref, sa_ref, sb_ref, dq_ref, dk_ref,
             dv_ref, dcq_ref, dckv_ref, dkr_ref, dwuq_ref, dwk_ref, dwv_ref, dgq_ref, dgkv_ref):
        i = pl.program_id(0)

        @pl.when(i == 0)
        def _():
            for r in (dwuq_ref, dwk_ref, dwv_ref, dgq_ref, dgkv_ref):
                r[...] = jnp.zeros_like(r)

        cos, sa, sb = cos_ref[...], sa_ref[...], sb_ref[...]
        parts, knp = [], []
        dkr = jnp.zeros((rb, LANES), f32)
        for h in range(N_HEADS):
            lo, mid, hi = h * MLA_QW, h * MLA_QW + LANES, (h + 1) * MLA_QW
            parts += [dq_ref[:, lo:mid], _rope(dq_ref[:, mid:hi], cos, sa, sb, half, transpose=True)]
            knp.append(dk_ref[:, lo:mid])
            dkr = dkr + _rope(dk_ref[:, mid:hi], cos, sa, sb, half, transpose=True)
        dkr_ref[...] = dkr
        dqp = jnp.concatenate(parts, axis=1)
        dkn = jnp.concatenate(knp, axis=1)
        dvv = dv_ref[...]

        def norm_bwd(x_ref, g_ref, w_pairs, dx_ref, dg_ref):
            xh, r = _rms_rows(x_ref[...])
            nx = xh * g_ref[...]
            dn = jnp.zeros_like(xh)
            for w_ref, dw_ref, dy in w_pairs:
                dw_ref[...] += _mm_tn(nx, dy)
                dn = dn + _mm_nt(dy, w_ref[...])
            dxh = dn * g_ref[...]
            dx_ref[...] = r * (dxh - xh * jnp.mean(dxh * xh, axis=-1, keepdims=True))
            dg_ref[...] += jnp.sum(dn * xh, axis=0, keepdims=True)

        norm_bwd(cq_ref, gq_ref, [(wuq_ref, dwuq_ref, dqp)], dcq_ref, dgq_ref)
        norm_bwd(ckv_ref, gkv_ref, [(wk_ref, dwk_ref, dkn), (wv_ref, dwv_ref, dvv)], dckv_ref, dgkv_ref)

    row = lambda w, cb: pl.BlockSpec((rb, w), lambda i: (i, cb))
    whole = lambda a: pl.BlockSpec(a.shape, lambda i: (0,) * a.ndim)
    return pl.pallas_call(
        body,
        out_shape=(_SDS((s, MLA_Q_RANK), f32), _SDS((s, LANES), f32), _SDS((s, LANES), f32), _SDS(wuq.shape, f32),
                   _SDS(wk.shape, f32), _SDS(wv.shape, f32), _SDS(gq.shape, f32), _SDS(gkv.shape, f32)),
        grid=(s // rb,),
        in_specs=[row(MLA_Q_RANK, _AL["mla_cq"] // MLA_Q_RANK), row(LANES, _AL["mla_ckv"] // LANES), whole(gq),
                  whole(gkv), whole(wuq), whole(wk), whole(wv), row(LANES, 0), row(LANES, 0), row(LANES, 0),
                  row(N_HEADS * MLA_QW, 0), row(N_HEADS * MLA_QW, 0), row(GROUP, 0)],
        out_specs=(row(MLA_Q_RANK, 0), row(LANES, 0), row(LANES, 0), whole(wuq), whole(wk), whole(wv), whole(gq),
                   whole(gkv)),
        compiler_params=_cp(("arbitrary",)), name=name,
    )(z_all, z_all, gq, gkv, wuq, wk, wv, *tables, dq_cat, dk_cat, dv)


def _silu_grad(x):
    sg = _sigmoid(x)
    return sg * (1.0 + x * (1.0 - sg))


def _nsa_cmp_fwd(ra, rb_, pos, w1, w2, tables, *, name):
    nr = ra.shape[1]
    hw = ra.shape[2]

    def body(ra_ref, rb_ref, pos_ref, w1_ref, w2_ref, cos_ref, sa_ref, sb_ref, out_ref, hp_ref):
        for k in range(2):
            xa = ra_ref[k] + pos_ref[k, :, 0:hw]
            xb = rb_ref[k] + pos_ref[k, :, hw:2 * hw]
            hp = _mm(xa, w1_ref[k, 0:hw, :]) + _mm(xb, w1_ref[k, hw:2 * hw, :])
            hp_ref[k] = hp
            out = _mm(hp * _sigmoid(hp), w2_ref[k])
            if k == 0:
                out = _rope(out, cos_ref[...], sa_ref[...], sb_ref[...], HEAD_DIM // 2)
            out_ref[k] = out

    return pl.pallas_call(body, out_shape=(_SDS((2, nr, HEAD_DIM), f32), _SDS((2, nr, HEAD_DIM), f32)),
                          compiler_params=_cp(), name=name)(ra, rb_, pos, w1, w2, *tables)


def _nsa_cmp_bwd(ra, rb_, pos, w1, w2, tables, hp, dout, *, name):
    nr = ra.shape[1]
    hw = ra.shape[2]

    def body(ra_ref, rb_ref, pos_ref, w1_ref, w2_ref, cos_ref, sa_ref, sb_ref, hp_ref, do_ref,
             dxa_ref, dxb_ref, dw1_ref, dw2_ref):
        for k in range(2):
            d_out = do_ref[k]
            if k == 0:
                d_out = _rope(d_out, cos_ref[...], sa_ref[...], sb_ref[...], HEAD_DIM // 2, transpose=True)
            hpv = hp_ref[k]
            dw2_ref[k] = _mm_tn(hpv * _sigmoid(hpv), d_out)
            dhp = _mm_nt(d_out, w2_ref[k]) * _silu_grad(hpv)
            xa = ra_ref[k] + pos_ref[k, :, 0:hw]
            xb = rb_ref[k] + pos_ref[k, :, hw:2 * hw]
            dw1_ref[k, 0:hw, :] = _mm_tn(xa, dhp)
            dw1_ref[k, hw:2 * hw, :] = _mm_tn(xb, dhp)
            dxa_ref[k] = _mm_nt(dhp, w1_ref[k, 0:hw, :])
            dxb_ref[k] = _mm_nt(dhp, w1_ref[k, hw:2 * hw, :])

    return pl.pallas_call(
        body, out_shape=(_SDS((2, nr, hw), f32), _SDS((2, nr, hw), f32), _SDS(w1.shape, f32), _SDS(w2.shape, f32)),
        compiler_params=_cp(), name=name)(ra, rb_, pos, w1, w2, *tables, hp, dout)


def _nsa_consts(s):
    b = _attn_block(s)
    nr = s // CMP_STRIDE
    n_cmp = (s - CMP_LEN) // CMP_STRIDE + 1
    n_sel = s // SEL_LEN
    cmp_start = np.arange(n_cmp) * CMP_STRIDE
    sel_start = np.arange(n_sel) * SEL_LEN
    overlap = np.clip(np.minimum(cmp_start[:, None] + CMP_LEN, sel_start[None, :] + SEL_LEN)
                      - np.maximum(cmp_start[:, None], sel_start[None, :]), 0, None)
    m2s = np.zeros((nr, LANES), np.float32)
    m2s[:n_cmp, :n_sel] = overlap / CMP_LEN
    e3 = np.zeros((s // b, LANES, b), np.float32)
    tok = np.arange(s)
    e3[tok // b, tok // SEL_LEN, tok % b] = 1.0
    return jnp.asarray(m2s, _MXU), jnp.asarray(e3, _MXU)


def _nsa_masks(i, b, d):
    qpos = i * b + _iota((b, b), 0)
    kpos = (i - d) * b + _iota((b, b), 1)
    return (kpos <= qpos) & (kpos > qpos - WINDOW)


def _nsa_fwd(qr, kvc, ksr, vs, kwr, vw, z_all, m2s, e3, *, name):
    s = qr.shape[0]
    b = _attn_block(s)
    nq = s // b
    nr = kvc.shape[1]
    n_sel = s // SEL_LEN
    top_n = min(SEL_TOPN, n_sel)
    nd = -(-WINDOW // b)
    scale = HEAD_DIM ** -0.5
    bcol = _AL["nsa_branch"] // LANES
    H = N_HEADS

    def body(q_ref, kvc_ref, ks_ref, vs_ref, kw_ref, vw_ref, br_ref, m2s_ref, e3_ref,
             o_ref, oc_ref, os_ref, ow_ref, st_ref, sel_ref, m_sc, l_sc, acc_sc):
        i = pl.program_id(0)
        lane = _iota((b, LANES), 1)
        hs = lambda h: slice(h * HEAD_DIM, (h + 1) * HEAD_DIM)

        cmp_mask = (CMP_STRIDE * _iota((b, nr), 1) + (CMP_LEN - 1)) <= (i * b + _iota((b, nr), 0))
        imp = jnp.zeros((b, LANES), f32)
        stats = jnp.zeros((b, LANES), f32)
        for h in range(H):
            zc = jnp.where(cmp_mask, _mm_nt(q_ref[:, hs(h)], kvc_ref[0]) * scale, NEG_INF)
            m = jnp.max(zc, axis=1, keepdims=True)
            p = jnp.where(cmp_mask, jnp.exp(zc - m), 0.0)
            l = jnp.sum(p, axis=1, keepdims=True)
            some = l > 0.0
            lsafe = jnp.where(some, l, 1.0)
            pc = p * jnp.where(some, 1.0 / lsafe, 0.0)
            oc_ref[:, hs(h)] = _mm(pc, kvc_ref[1])
            imp = imp + _mm(pc, m2s_ref[...])
            stats = jnp.where(lane == h, jnp.where(some, m + jnp.log(lsafe), 0.0), stats)

        cur = jnp.right_shift(i * b + _iota((b, LANES), 0), int(math.log2(SEL_LEN)))
        forced = (lane == 0) | (lane == cur) | (lane == cur - 1)
        score = jnp.where(lane <= cur, jnp.where(forced, FORCED_BONUS, imp), NEG_INF)
        score = jnp.where(lane < n_sel, score, -3e38)
        rank = jnp.zeros((b, LANES), f32)
        for j in range(n_sel):
            col = score[:, j:j + 1]
            rank = rank + jnp.where(col > score, 1.0, jnp.where(col == score, jnp.where(lane > j, 1.0, 0.0), 0.0))
        sel = jnp.where(lane < n_sel, jnp.where(rank < top_n, 1.0, 0.0), 0.0)
        sel_ref[...] = sel
        sel_b = sel.astype(_MXU)

        def reset():
            m_sc[...] = jnp.full(m_sc.shape, NEG_INF, f32)
            l_sc[...] = jnp.zeros_like(l_sc)
            acc_sc[...] = jnp.zeros_like(acc_sc)

        def update(h, z, mask, vch):
            zm = jnp.where(mask, z, NEG_INF)
            m_old = m_sc[h]
            m_new = jnp.maximum(m_old, jnp.max(zm, axis=1, keepdims=True))
            p = jnp.where(mask, jnp.exp(zm - m_new), 0.0)
            alpha = jnp.exp(m_old - m_new)
            l_sc[h] = alpha * l_sc[h] + jnp.sum(p, axis=1, keepdims=True)
            acc_sc[h] = alpha * acc_sc[h] + _mm(p, vch)
            m_sc[h] = m_new

        def finish(out_ref, branch, stats):
            for h in range(H):
                out_ref[:, hs(h)] = acc_sc[h] / l_sc[h]
                stats = jnp.where(lane == 4 * branch + h, m_sc[h] + jnp.log(l_sc[h]), stats)
            return stats

        def sel_chunk(c, diag):
            st = pl.multiple_of(c * b, b)
            mask = _mm(sel_b, e3_ref[c]) > 0.5
            if diag:
                mask = mask & _lower_mask(b, False)
            kch, vch = ks_ref[pl.ds(st, b), :], vs_ref[pl.ds(st, b), :]
            for h in range(H):
                update(h, _mm_nt(q_ref[:, hs(h)], kch) * scale, mask, vch)

        reset()

        def sel_loop(c, carry):
            sel_chunk(c, False)
            return carry

        lax.fori_loop(0, i, sel_loop, 0)
        sel_chunk(i, True)
        stats = finish(os_ref, 1, stats)

        reset()
        for d in range(nd, -1, -1):
            @pl.when(i >= d)
            def _():
                st = pl.multiple_of((i - d) * b, b)
                mask = _nsa_masks(i, b, d)
                kch, vch = kw_ref[pl.ds(st, b), :], vw_ref[pl.ds(st, b), :]
                for h in range(H):
                    update(h, _mm_nt(q_ref[:, hs(h)], kch) * scale, mask, vch)
        stats = finish(ow_ref, 2, stats)
        st_ref[...] = stats

        g = _sigmoid(br_ref[...])
        for h in range(H):
            o_ref[:, hs(h)] = (g[:, 3 * h:3 * h + 1] * oc_ref[:, hs(h)] + g[:, 3 * h + 1:3 * h + 2] * os_ref[:, hs(h)]
                               + g[:, 3 * h + 2:3 * h + 3] * ow_ref[:, hs(h)])

    blk = lambda w: pl.BlockSpec((b, w), lambda i: (i, 0))
    whole = lambda a: pl.BlockSpec(a.shape, lambda i: (0,) * a.ndim)
    return pl.pallas_call(
        body, out_shape=tuple(_SDS((s, GROUP), f32) for _ in range(4)) + (_SDS((s, LANES), f32), _SDS((s, LANES), f32)),
        grid=(nq,),
        in_specs=[blk(GROUP), whole(kvc), whole(ksr), whole(vs), whole(kwr), whole(vw),
                  pl.BlockSpec((b, LANES), lambda i: (i, bcol)), whole(m2s), whole(e3)],
        out_specs=(blk(GROUP),) * 4 + (blk(LANES), blk(LANES)),
        scratch_shapes=[pltpu.VMEM((H, b, 1), f32), pltpu.VMEM((H, b, 1), f32), pltpu.VMEM((H, b, HEAD_DIM), f32)],
        compiler_params=_cp(("parallel",)), name=name,
    )(qr, kvc, ksr, vs, kwr, vw, z_all, m2s, e3)


def _nsa_bwd(do, qr, kvc, ksr, vs, kwr, vw, z_all, oc, os_, ow, stats, sel, e3, *, name):
    s = qr.shape[0]
    b = _attn_block(s)
    nq = s // b
    nr = kvc.shape[1]
    nd = -(-WINDOW // b)
    scale = HEAD_DIM ** -0.5
    bcol = _AL["nsa_branch"] // LANES
    H = N_HEADS

    def body(do_ref, q_ref, kvc_ref, ks_ref, vs_ref, kw_ref, vw_ref, br_ref, oc_ref, os_ref, ow_ref, st_ref, sel_ref,
             e3_ref, dq_ref, dbr_ref, dkvc_ref, dks_ref, dvs_ref, dkw_ref, dvw_ref, dob_sc, delta_sc, dq_sc, kvt_sc):
        i = pl.program_id(0)

        @pl.when(i == 0)
        def _():
            dkvc_ref[...] = jnp.zeros_like(dkvc_ref)
            kvt_sc[...] = jnp.zeros_like(kvt_sc)

        lane = _iota((b, LANES), 1)
        hs = lambda h: slice(h * HEAD_DIM, (h + 1) * HEAD_DIM)
        g = _sigmoid(br_ref[...])
        stats = st_ref[...]
        dbr = jnp.zeros((b, LANES), f32)
        outs = (oc_ref, os_ref, ow_ref)
        for h in range(H):
            doh = do_ref[:, hs(h)]
            for j in range(3):
                gj = g[:, 3 * h + j:3 * h + j + 1]
                dgj = jnp.sum(doh * outs[j][:, hs(h)], axis=1, keepdims=True)
                dbr = jnp.where(lane == 3 * h + j, dgj * gj * (1.0 - gj), dbr)
                dob_sc[j, :, hs(h)] = gj * doh
                delta_sc[j, h] = gj * dgj
        dbr_ref[...] = dbr
        dq_sc[...] = jnp.zeros_like(dq_sc)

        qts = [q_ref[:, hs(h)].T.astype(_MXU) for h in range(H)]
        dobts = {(j, h): dob_sc[j, :, hs(h)].T.astype(_MXU) for j in (1, 2) for h in range(H)}

        def branch(j, h, z, mask, kch, vch):
            p = jnp.where(mask, jnp.exp(jnp.where(mask, z, NEG_INF) - stats[:, 4 * j + h:4 * j + h + 1]), 0.0)
            dob = dob_sc[j, :, hs(h)]
            ds = p * (_mm_nt(dob, vch) - delta_sc[j, h])
            dq_sc[:, hs(h)] += _mm(ds, kch) * scale
            if j == 0:
                return _mm_tn(ds, q_ref[:, hs(h)]) * scale, _mm_tn(p, dob)
            return _mm(qts[h], ds), _mm(dobts[j, h], p)

        cmp_mask = (CMP_STRIDE * _iota((b, nr), 1) + (CMP_LEN - 1)) <= (i * b + _iota((b, nr), 0))
        kc, vc = kvc_ref[0], kvc_ref[1]
        for h in range(H):
            dk, dv = branch(0, h, _mm_nt(q_ref[:, hs(h)], kc) * scale, cmp_mask, kc, vc)
            dkvc_ref[0] += dk
            dkvc_ref[1] += dv

        sel_b = sel_ref[...].astype(_MXU)

        def chunk(j, c, mask, k_ref, v_ref):
            st = pl.multiple_of(c * b, b)
            kch, vch = k_ref[pl.ds(st, b), :], v_ref[pl.ds(st, b), :]
            dk = jnp.zeros((HEAD_DIM, b), f32)
            dv = jnp.zeros((HEAD_DIM, b), f32)
            for h in range(H):
                dkh, dvh = branch(j, h, _mm_nt(q_ref[:, hs(h)], kch) * scale, mask, kch, vch)
                dk, dv = dk + dkh, dv + dvh
            kvt_sc[2 * j - 2, c] += dk
            kvt_sc[2 * j - 1, c] += dv

        def sel_chunk(c, diag):
            mask = _mm(sel_b, e3_ref[c]) > 0.5
            if diag:
                mask = mask & _lower_mask(b, False)
            chunk(1, c, mask, ks_ref, vs_ref)

        def sel_loop(c, carry):
            sel_chunk(c, False)
            return carry

        lax.fori_loop(0, i, sel_loop, 0)
        sel_chunk(i, True)

        for d in range(nd, -1, -1):
            @pl.when(i >= d)
            def _():
                chunk(2, i - d, _nsa_masks(i, b, d), kw_ref, vw_ref)

        dq_ref[...] = dq_sc[...]

        @pl.when(i == nq - 1)
        def _():
            for c in range(nq):
                rows = slice(c * b, (c + 1) * b)
                dks_ref[rows, :] = kvt_sc[0, c].T * scale
                dvs_ref[rows, :] = kvt_sc[1, c].T
                dkw_ref[rows, :] = kvt_sc[2, c].T * scale
                dvw_ref[rows, :] = kvt_sc[3, c].T

    blk = lambda w: pl.BlockSpec((b, w), lambda i: (i, 0))
    whole = lambda a: pl.BlockSpec(a.shape, lambda i: (0,) * a.ndim)
    stream = _SDS((s, HEAD_DIM), f32)
    return pl.pallas_call(
        body, out_shape=(_SDS((s, GROUP), f32), _SDS((s, LANES), f32), _SDS(kvc.shape, f32), stream, stream, stream,
                         stream),
        grid=(nq,),
        in_specs=[blk(GROUP), blk(GROUP), whole(kvc), whole(ksr), whole(vs), whole(kwr), whole(vw),
                  pl.BlockSpec((b, LANES), lambda i: (i, bcol)), blk(GROUP), blk(GROUP), blk(GROUP), blk(LANES),
                  blk(LANES), whole(e3)],
        out_specs=(blk(GROUP), blk(LANES), whole(kvc), whole(ksr), whole(vs), whole(kwr), whole(vw)),
        scratch_shapes=[pltpu.VMEM((3, b, GROUP), f32), pltpu.VMEM((3, H, b, 1), f32), pltpu.VMEM((b, GROUP), f32),
                        pltpu.VMEM((4, nq, HEAD_DIM, b), f32)],
        compiler_params=_cp(("arbitrary",)), name=name,
    )(do, qr, kvc, ksr, vs, kwr, vw, z_all, oc, os_, ow, stats, sel, e3)


def _seg(a, name):
    parts = [lax.slice_in_dim(a, off, off + hi - lo, axis=a.ndim - 1) for off, lo, hi in _PIECES[name]]
    return parts[0] if len(parts) == 1 else jnp.concatenate(parts, axis=a.ndim - 1)


def _to_groups(segs, rows, dtype):
    cols = []
    for s, grp in enumerate(_GROUPS):
        at = 0
        for n, lo, hi, off in sorted(grp, key=lambda t: t[3]):
            if off > at:
                cols.append(jnp.zeros((rows, off - at), dtype))
            cols.append(segs[n][:, lo:hi].astype(dtype))
            at = off + hi - lo
        if at < GROUP_W:
            cols.append(jnp.zeros((rows, GROUP_W - at), dtype))
    return jnp.concatenate(cols, axis=1)


def _piece_from_shard(w_t, s):
    grp = sorted(_GROUPS[s], key=lambda t: t[3])
    ends = [t[3] for t in grp[1:]] + [GROUP_W]
    rows = []
    for (n, lo, hi, off), end in zip(grp, ends):
        first = _ORIG[n] + lo - s * CHIP_COLS
        rows.append(jnp.pad(w_t[:, first:first + hi - lo], ((0, 0), (0, end - off - (hi - lo)), (0, 0))))
    return jnp.concatenate(rows, axis=1)


def _shard_from_piece(g, s):
    return jnp.concatenate([g[:, off:off + hi - lo] for n, lo, hi, off in
                            sorted(_GROUPS[s], key=lambda t: _ORIG[t[0]] + t[1])], axis=1)


def _from_groups(a):
    return jnp.concatenate([_seg(a, n) for n, _ in _SEGS], axis=1)


def _cmp_rows(tok):
    s = tok.shape[0]
    r = tok.reshape(s // CMP_STRIDE, CMP_STRIDE * HEAD_DIM)
    return r, jnp.concatenate([r[1:], jnp.zeros((1, r.shape[1]), r.dtype)], axis=0)


def _cmp_unrows(dxa, dxb):
    s = dxa.shape[0] * CMP_STRIDE
    return (dxa + jnp.concatenate([jnp.zeros((1, dxa.shape[1]), dxa.dtype), dxb[:-1]], axis=0)).reshape(s, HEAD_DIM)


_GATES = ("sb_gate", "nsa_gate", "fox_gate", "mla_gate")


def _layer_fwd(x, p, c, tag):
    s = x.shape[0]
    b = _attn_block(s)
    h = _rms_fwd(x, p["pre_g"], out_dtype=_MXU, name=f"prenorm_{tag}")
    z = _matmul(h, p["w_in"], "nt", bias=p["b_in"], name=f"inproj_{tag}")
    o_sb = _sb_fwd(z, hp=HP_FWD, name=f"sb_fwd_{tag}")

    qr, ksr, kwr = _rope_call([(z, GROUP, _AL["nsa_q"] // GROUP), (z, LANES, _AL["nsa_k_sel"] // LANES),
                               (z, LANES, _AL["nsa_k_win"] // LANES)], c["tabs128"], HEAD_DIM // 2, False,
                              name=f"nsa_rope_{tag}")
    (rak, rbk), (rav, rbv) = _cmp_rows(_seg(z, "nsa_k_cmp")), _cmp_rows(_seg(z, "nsa_v_cmp"))
    ra, rb_ = jnp.stack([rak, rav]), jnp.stack([rbk, rbv])
    kvc, hp = _nsa_cmp_fwd(ra, rb_, p["cmp_pos"], p["cmp_w1"], p["cmp_w2"], c["tabs_cmp"], name=f"nsa_cmp_{tag}")
    vs, vw = _seg(z, "nsa_v_sel"), _seg(z, "nsa_v_win")
    o_nsa, oc, os_, ow, stats, sel = _nsa_fwd(qr, kvc, ksr, vs, kwr, vw, z, c["m2s"], c["e3"], name=f"nsa_fwd_{tag}")

    cum, cum_t8 = _fox_cum_fwd(z, p["fox_bias"], name=f"fox_cum_{tag}")
    cum_t = cum_t8.reshape(8, s // b, 1, b)
    fox_v = _seg(z, "fox_v")
    fcols = (_AL["fox_q"] // HEAD_DIM, _AL["fox_k"] // HEAD_DIM, 0)
    o_fox, lse_fox = _attn_fwd(z, z, fox_v, *fcols, HEAD_DIM, cum, cum_t, scale=HEAD_DIM ** -0.5, hp=HP_FWD,
                               name=f"fox_fwd_{tag}")

    qcat, kcat, vm = _mla_prep_fwd(z, p["gq"], p["gkv"], p["wuq"], p["wk"], p["wv"], c["tabs64"],
                                   name=f"mla_prep_{tag}")
    o_mla, lse_mla = _attn_fwd(qcat, kcat, vm, 0, 0, 0, MLA_QW, None, None, scale=(MLA_NOPE + MLA_ROPE) ** -0.5,
                               hp=HP_BWD, name=f"mla_fwd_{tag}")

    o_all = (o_sb, o_nsa, o_fox, o_mla)
    gates = jnp.concatenate([_seg(z, n) for n in _GATES], axis=1)
    mix = _gate_fwd(o_all, gates, name=f"gate_{tag}")
    u = _matmul(mix, p["w_out"], "nn", name=f"outproj_{tag}")
    y = _postnorm_fwd(u, p["post_g"], x, name=f"postnorm_{tag}")
    saved = dict(x=x, h=h, z=z, qr=qr, ksr=ksr, kwr=kwr, ra=ra, rb=rb_, kvc=kvc, hp=hp, vs=vs, vw=vw, oc=oc, os=os_,
                 ow=ow, stats=stats, sel=sel, cum=cum, cum_t=cum_t, fox_v=fox_v, o_fox=o_fox, lse_fox=lse_fox, qcat=qcat, kcat=kcat,
                 vm=vm, o_mla=o_mla, lse_mla=lse_mla, o_all=o_all, gates=gates, mix=mix, u=u)
    return y, saved


def _layer_bwd(dy, sv, p, c, tag, dw_dtype=f32):
    z = sv["z"]
    s = z.shape[0]
    du, dg_post = _rms_bwd(dy, sv["u"], p["post_g"], name=f"postnorm_bwd_{tag}")
    dmix = _matmul(du, p["w_out"], "nt", name=f"outproj_dx_{tag}")
    dw_out = _matmul(sv["mix"], du, "tn", name=f"outproj_dw_{tag}")
    do_sb, do_nsa, do_fox, do_mla, dgates = _gate_bwd(dmix, sv["o_all"], sv["gates"], name=f"gate_bwd_{tag}")
    dgate = [dgates[:, k * GROUP:(k + 1) * GROUP] for k in range(4)]

    sb_dq, sb_dk, sb_dv = _sb_bwd(z, do_sb, hp=HP_BWD, name=f"sb_bwd_{tag}")

    n_dq, n_dbr, n_dkvc, n_dks, n_dvs, n_dkw, n_dvw = _nsa_bwd(
        do_nsa, sv["qr"], sv["kvc"], sv["ksr"], sv["vs"], sv["kwr"], sv["vw"], z, sv["oc"], sv["os"], sv["ow"],
        sv["stats"], sv["sel"], c["e3"], name=f"nsa_bwd_{tag}")
    dxa, dxb, dw1, dw2 = _nsa_cmp_bwd(sv["ra"], sv["rb"], p["cmp_pos"], p["cmp_w1"], p["cmp_w2"], c["tabs_cmp"],
                                      sv["hp"], n_dkvc, name=f"nsa_cmp_bwd_{tag}")
    n_dq, n_dks, n_dkw = _rope_call([(n_dq, GROUP, 0), (n_dks, LANES, 0), (n_dkw, LANES, 0)], c["tabs128"],
                                    HEAD_DIM // 2, True, name=f"nsa_rope_bwd_{tag}")
    dpos = _colsum(jnp.concatenate([dxa[0], dxb[0], dxa[1], dxb[1]], axis=1), name=f"nsa_dpos_{tag}")
    flat = CMP_LEN * HEAD_DIM

    fcols = (_AL["fox_q"] // HEAD_DIM, _AL["fox_k"] // HEAD_DIM, 0)
    f_dq, f_dk, f_dv, f_dck = _attn_bwd(z, z, sv["fox_v"], *fcols, HEAD_DIM, do_fox, sv["o_fox"], sv["lse_fox"],
                                        sv["cum"], sv["cum_t"], scale=HEAD_DIM ** -0.5, hp=HP_BWD,
                                        name=f"fox_bwd_{tag}")
    dcum_t = jnp.pad(f_dck.reshape(N_HEADS, s), ((0, 8 - N_HEADS), (0, 0)))
    f_df, f_dbias = _fox_cum_bwd(z, p["fox_bias"], dcum_t, name=f"fox_cum_bwd_{tag}")

    m_dq, m_dk, m_dv = _attn_bwd(sv["qcat"], sv["kcat"], sv["vm"], 0, 0, 0, MLA_QW, do_mla, sv["o_mla"], sv["lse_mla"],
                                 None, None, scale=(MLA_NOPE + MLA_ROPE) ** -0.5, hp=HP_BWD, name=f"mla_bwd_{tag}")
    m_dcq, m_dckv, m_dkr, m_dwuq, m_dwk, m_dwv, m_dgq, m_dgkv = _mla_prep_bwd(
        z, p["gq"], p["gkv"], p["wuq"], p["wk"], p["wv"], c["tabs64"], m_dq, m_dk, m_dv, name=f"mla_prep_bwd_{tag}")

    dz = _to_groups(dict(
        sb_q=sb_dq, sb_k=sb_dk, sb_v=sb_dv, sb_gate=dgate[0], nsa_q=n_dq, nsa_k_cmp=_cmp_unrows(dxa[0], dxb[0]),
        nsa_v_cmp=_cmp_unrows(dxa[1], dxb[1]), nsa_k_sel=n_dks, nsa_v_sel=n_dvs, nsa_k_win=n_dkw, nsa_v_win=n_dvw,
        nsa_branch=n_dbr, nsa_gate=dgate[1], fox_q=f_dq, fox_k=f_dk, fox_v=f_dv, fox_f=f_df, fox_gate=dgate[2],
        mla_cq=m_dcq, mla_ckv=m_dckv, mla_k_rope=m_dkr, mla_gate=dgate[3]), s, _MXU)
    dh = _matmul(dz, p["w_in"], "nn", name=f"inproj_dx_{tag}")
    dw_in = _matmul(dz, sv["h"], "tn", out_dtype=dw_dtype, name=f"inproj_dw_{tag}")
    db = _colsum(dz, name=f"inproj_db_{tag}")
    dx, dg_pre = _rms_bwd(dh, sv["x"], p["pre_g"], res=dy, name=f"prenorm_bwd_{tag}")

    qw = MLA_NOPE + MLA_ROPE
    grads = {
        "pre_norm_g": dg_pre[0], "post_norm_g": dg_post[0], "w_in": dw_in, "b_in": _from_groups(db)[0],
        "w_out": dw_out, "fox_forget_bias": f_dbias[0, :N_HEADS],
        "nsa_cmp_pos_k": dpos[0, :flat].reshape(CMP_LEN, HEAD_DIM), "nsa_cmp_w1_k": dw1[0], "nsa_cmp_w2_k": dw2[0],
        "nsa_cmp_pos_v": dpos[0, flat:].reshape(CMP_LEN, HEAD_DIM), "nsa_cmp_w1_v": dw1[1], "nsa_cmp_w2_v": dw2[1],
        "mla_q_norm_g": m_dgq[0],
        "mla_w_uq": jnp.concatenate([m_dwuq[:, MLA_QW * h:MLA_QW * h + qw] for h in range(N_HEADS)], axis=1),
        "mla_kv_norm_g": m_dgkv[0],
        "mla_w_ukv": jnp.concatenate(sum([[m_dwk[:, LANES * h:LANES * (h + 1)], m_dwv[:, LANES * h:LANES * (h + 1)]]
                                          for h in range(N_HEADS)], []), axis=1),
    }
    return dx, grads


def _layer_params(w, l):
    b_in = w["b_in"][l].reshape(1, -1)
    b_segs = {n: b_in[:, _ORIG[n]:_ORIG[n] + wd] for n, wd in _SEGS}
    qw = MLA_NOPE + MLA_ROPE
    w_uq, w_ukv = w["mla_w_uq"][l], w["mla_w_ukv"][l]
    uq = []
    for h in range(N_HEADS):
        uq += [w_uq[:, qw * h:qw * (h + 1)], jnp.zeros((w_uq.shape[0], MLA_QW - qw), w_uq.dtype)]
    kw_ = 2 * LANES
    flat = CMP_LEN * HEAD_DIM
    return dict(
        pre_g=w["pre_norm_g"][l].reshape(1, -1), post_g=w["post_norm_g"][l].reshape(1, -1),
        w_in=w["w_in"][l], b_in=_to_groups(b_segs, 1, f32), w_out=w["w_out"][l],
        fox_bias=jnp.pad(w["fox_forget_bias"][l], (0, LANES - N_HEADS)).reshape(1, LANES),
        cmp_pos=jnp.stack([w["nsa_cmp_pos_k"][l].reshape(1, flat), w["nsa_cmp_pos_v"][l].reshape(1, flat)]),
        cmp_w1=jnp.stack([w["nsa_cmp_w1_k"][l], w["nsa_cmp_w1_v"][l]]),
        cmp_w2=jnp.stack([w["nsa_cmp_w2_k"][l], w["nsa_cmp_w2_v"][l]]),
        gq=w["mla_q_norm_g"][l].reshape(1, -1), gkv=w["mla_kv_norm_g"][l].reshape(1, -1),
        wuq=jnp.concatenate(uq, axis=1),
        wk=jnp.concatenate([w_ukv[:, kw_ * h:kw_ * h + LANES] for h in range(N_HEADS)], axis=1),
        wv=jnp.concatenate([w_ukv[:, kw_ * h + LANES:kw_ * (h + 1)] for h in range(N_HEADS)], axis=1),
    )


def _consts(s):
    pos = jnp.arange(s)
    m2s, e3 = _nsa_consts(s)
    return dict(tabs128=_rope_tables(pos, HEAD_DIM), tabs64=_rope_tables(pos, MLA_ROPE),
                tabs_cmp=_rope_tables(jnp.arange(s // CMP_STRIDE) * CMP_STRIDE + (CMP_LEN - 1), HEAD_DIM),
                m2s=m2s, e3=e3)


def _place():
    return lax.axis_index("x"), lax.axis_index("y"), lax.axis_index("c")


def _other_chips(x, y):
    return [(1 - x, y), (x, 1 - y), (1 - x, 1 - y)]


def _comm_call(body, out_shapes, n_sems, arrs, name):
    return pl.pallas_call(body, out_shape=tuple(out_shapes), in_specs=[_ANY] * len(arrs),
                          out_specs=tuple(_ANY for _ in out_shapes),
                          scratch_shapes=[pltpu.SemaphoreType.DMA((n_sems,)), pltpu.SemaphoreType.DMA((n_sems,))],
                          name=name)(*arrs)


def _gather_chips(arrs, *, name):
    n = len(arrs)

    def body(*refs):
        a_refs, out_refs, send_sems, recv_sems = refs[:n], refs[n:2 * n], refs[2 * n], refs[2 * n + 1]
        x, y, c = _place()
        me = 2 * x + y
        sibling = (x, y, 1 - c)
        chips = _other_chips(x, y)

        def copy(j, k, src, dst, to):
            return pltpu.make_async_remote_copy(src, dst, send_sems.at[6 * j + k], recv_sems.at[6 * j + k],
                                                device_id=to, device_id_type=_MESH)

        first = [copy(j, k, a_refs[j].at[c], out_refs[j].at[me, c], (px, py, c))
                 for k, (px, py) in enumerate(chips) for j in range(n)]
        for cp in first:
            cp.start()
        passed = []
        for k, (px, py) in enumerate(chips):
            for j in range(n):
                landed = out_refs[j].at[2 * px + py, c]
                copy(j, k, a_refs[j].at[c], landed, (px, py, c)).wait_recv()
                passed.append(copy(j, 3 + k, landed, landed, sibling))
                passed[-1].start()
        for k, (px, py) in enumerate(chips):
            for j in range(n):
                copy(j, 3 + k, a_refs[j].at[c], out_refs[j].at[2 * px + py, 1 - c], sibling).wait_recv()
        for cp in first + passed:
            cp.wait_send()

    return _comm_call(body, [_SDS((N_CHIPS,) + a.shape, a.dtype) for a in arrs], 6 * n, arrs, name)


def _alltoall_chips(arrs, modes, *, name):
    n = len(arrs)

    def body(*refs):
        g_refs, out_refs, send_sems, recv_sems = refs[:n], refs[n:2 * n], refs[2 * n], refs[2 * n + 1]
        x, y, c = _place()
        me = 2 * x + y

        def copy(j, s):
            return pltpu.make_async_remote_copy(_slot_ref(g_refs[j], modes[j], s), out_refs[j].at[me],
                                                send_sems.at[N_CHIPS * j + s], recv_sems.at[N_CHIPS * j + me],
                                                device_id=(s // 2, s % 2, c), device_id_type=_MESH)

        for s in range(N_CHIPS):
            @pl.when(s != me)
            def _():
                for j in range(n):
                    copy(j, s).start()
        for t in range(N_CHIPS):
            @pl.when(t != me)
            def _():
                for j in range(n):
                    pltpu.make_async_remote_copy(_slot_ref(g_refs[j], modes[j], t), out_refs[j].at[t],
                                                 send_sems.at[N_CHIPS * j + t], recv_sems.at[N_CHIPS * j + t],
                                                 device_id=(t // 2, t % 2, c), device_id_type=_MESH).wait_recv()
        for s in range(N_CHIPS):
            @pl.when(s != me)
            def _():
                for j in range(n):
                    copy(j, s).wait_send()

    outs = [_SDS((N_CHIPS,) + _slot_shape(a, m), a.dtype) for a, m in zip(arrs, modes)]
    return _comm_call(body, outs, N_CHIPS * n, arrs, name)


def _swap_other_half(arrs, *, name):
    n = len(arrs)

    def body(*refs):
        g_refs, out_refs, send_sems, recv_sems = refs[:n], refs[n:2 * n], refs[2 * n], refs[2 * n + 1]
        x, y, c = _place()
        cps = [pltpu.make_async_remote_copy(g_refs[j].at[:, 1 - c], out_refs[j], send_sems.at[j], recv_sems.at[j],
                                            device_id=(x, y, 1 - c), device_id_type=_MESH) for j in range(n)]
        for cp in cps:
            cp.start()
        for cp in cps:
            cp.wait()

    return _comm_call(body, [_SDS((a.shape[0],) + a.shape[2:], a.dtype) for a in arrs], n, arrs, name)


def _swap_sibling(arrs, *, name):
    n = len(arrs)

    def body(*refs):
        f_refs, out_refs, send_sems, recv_sems = refs[:n], refs[n:2 * n], refs[2 * n], refs[2 * n + 1]
        x, y, c = _place()
        cps = [pltpu.make_async_remote_copy(f_refs[j], out_refs[j], send_sems.at[j], recv_sems.at[j],
                                            device_id=(x, y, 1 - c), device_id_type=_MESH) for j in range(n)]
        for cp in cps:
            cp.start()
        for cp in cps:
            cp.wait()

    return _comm_call(body, [_SDS(a.shape, a.dtype) for a in arrs], n, arrs, name)


_HBM = pl.BlockSpec(memory_space=pltpu.HBM)
_SEM = pl.BlockSpec(memory_space=pltpu.SEMAPHORE)
_EFFECT = pltpu.SideEffectType.DATAFLOW_SIDE_EFFECTING


def _slot_ref(ref, mode, s):
    return ref if mode == "same" else ref.at[s]


def _slot_shape(a, mode):
    return a.shape if mode == "same" else a.shape[1:]


def _send_start(arrs, modes, after, *, name):
    n = len(arrs)
    lands = [lax.empty((N_CHIPS,) + _slot_shape(a, m), a.dtype) for a, m in zip(arrs, modes)]

    def body(*refs):
        srcs, land_refs, send_sems, recv_sems, token = refs[:n], refs[n:2 * n], refs[2 * n + 1], refs[2 * n + 2], refs[-1]
        x, y, c = _place()
        me = 2 * x + y
        for s in range(N_CHIPS):
            @pl.when(s != me)
            def _():
                for j in range(n):
                    pltpu.make_async_remote_copy(_slot_ref(srcs[j], modes[j], s), land_refs[j].at[me],
                                                 send_sems.at[N_CHIPS * j + s], recv_sems.at[N_CHIPS * j + me],
                                                 device_id=(s // 2, s % 2, c), device_id_type=_MESH).start()
        token[...] = jnp.zeros_like(token)

    hbm = lambda a: pltpu.HBM(a.shape, a.dtype)
    sems = pltpu.SemaphoreType.DMA((N_CHIPS * n,))
    out = pl.pallas_call(
        body, name=name, out_shape=(sems, sems, *[hbm(a) for a in arrs], *[hbm(a) for a in lands], _SDS((8, LANES), f32)),
        in_specs=[_HBM] * (2 * n) + [_ANY], out_specs=(_SEM, _SEM, *[_HBM] * (2 * n), pl.BlockSpec(memory_space=pltpu.VMEM)),
        input_output_aliases={j: 2 + j for j in range(2 * n)},
        compiler_params=pltpu.CompilerParams(has_side_effects=_EFFECT),
    )(*[pltpu.with_memory_space_constraint(a, pltpu.HBM) for a in arrs + lands], after)
    return out[:-1], out[-1]


def _send_wait(started, modes, after, *, name):
    send_sems, recv_sems = started[0], started[1]
    n = (len(started) - 2) // 2
    thru = list(started[2:])

    def body(*refs):
        srcs, land_refs, send_sems, recv_sems = refs[:n], refs[n:2 * n], refs[2 * n], refs[2 * n + 1]
        x, y, c = _place()
        me = 2 * x + y
        for s in range(N_CHIPS):
            @pl.when(s != me)
            def _():
                for j in range(n):
                    cp = pltpu.make_async_remote_copy(_slot_ref(srcs[j], modes[j], s), land_refs[j].at[s],
                                                      send_sems.at[N_CHIPS * j + s], recv_sems.at[N_CHIPS * j + s],
                                                      device_id=(s // 2, s % 2, c), device_id_type=_MESH)
                    cp.wait_send()
                    cp.wait_recv()

    hbm = lambda a: pltpu.HBM(a.shape, a.dtype)
    out = pl.pallas_call(
        body, name=name, out_shape=tuple(hbm(a) for a in thru), in_specs=[_HBM] * (2 * n) + [_SEM, _SEM, _ANY],
        out_specs=tuple([_HBM] * (2 * n)), input_output_aliases={j: j for j in range(2 * n)},
        compiler_params=pltpu.CompilerParams(has_side_effects=_EFFECT),
    )(*thru, send_sems, recv_sems, after)
    return list(out[n:])


def _add_my_half(g, r, *, name):
    p, _, h, w = g.shape
    tw = _pick(w, (2048, 1024, 512, 256, 128))
    rb = max(d for d in range(16, h + 1, 16) if h % d == 0 and d * tw * 4 <= (2 << 20))

    def body(c_ref, g_ref, r_ref, o_ref):
        o_ref[...] = (g_ref[...].astype(f32) + r_ref[...].astype(f32)).astype(o_ref.dtype)

    blk = pl.BlockSpec((None, rb, tw), lambda s, i, j, c_ref: (s, i, j))
    grid_spec = pltpu.PrefetchScalarGridSpec(
        num_scalar_prefetch=1, grid=(p, h // rb, w // tw),
        in_specs=[pl.BlockSpec((None, None, rb, tw), lambda s, i, j, c_ref: (s, c_ref[0], i, j)), blk], out_specs=blk)
    c = lax.axis_index("c").astype(jnp.int32).reshape(1)
    return pl.pallas_call(body, out_shape=_SDS((p, h, w), _WIRE), grid_spec=grid_spec,
                          compiler_params=_cp(("parallel", "parallel", "parallel")), name=name)(c, g, r)


_WEIGHTS = ("pre_norm_g", "post_norm_g", "w_in", "b_in", "w_out", "fox_forget_bias", "nsa_cmp_pos_k", "nsa_cmp_w1_k",
            "nsa_cmp_w2_k", "nsa_cmp_pos_v", "nsa_cmp_w1_v", "nsa_cmp_w2_v", "mla_q_norm_g", "mla_w_uq",
            "mla_kv_norm_g", "mla_w_ukv")
_SHARD_AXIS = {"w_in": 2, "w_out": 1, "nsa_cmp_w1_k": 1, "nsa_cmp_w1_v": 1, "mla_w_uq": 2, "mla_w_ukv": 2}
_PACK_UNIT = 16 * LANES


def _pack(arrays, dtype):
    rows = []
    for a in arrays:
        v = a.astype(dtype).reshape(-1)
        pad = (-v.shape[0]) % _PACK_UNIT
        if pad:
            v = jnp.concatenate([v, jnp.zeros((pad,), dtype)])
        rows.append(v.reshape(-1, LANES))
    return jnp.concatenate(rows, axis=0)


def _unpack(flat, shapes):
    out, r = [], 0
    for shp in shapes:
        n = int(np.prod(shp))
        nr = -(-n // _PACK_UNIT) * (_PACK_UNIT // LANES)
        out.append(flat[r:r + nr].reshape(-1)[:n].reshape(shp))
        r += nr
    return out


def kernel(x, pre_norm_g, post_norm_g, w_in, b_in, w_out, fox_forget_bias, nsa_cmp_pos_k, nsa_cmp_w1_k, nsa_cmp_w2_k, nsa_cmp_pos_v, nsa_cmp_w1_v, nsa_cmp_w2_v, mla_q_norm_g, mla_w_uq, mla_kv_norm_g, mla_w_ukv, loss_target, m_pre_norm_g, m_post_norm_g, m_w_in, m_b_in, m_w_out, m_fox_forget_bias, m_nsa_cmp_pos_k, m_nsa_cmp_w1_k, m_nsa_cmp_w2_k, m_nsa_cmp_pos_v, m_nsa_cmp_w1_v, m_nsa_cmp_w2_v, m_mla_q_norm_g, m_mla_w_uq, m_mla_kv_norm_g, m_mla_w_ukv, v_pre_norm_g, v_post_norm_g, v_w_in, v_b_in, v_w_out, v_fox_forget_bias, v_nsa_cmp_pos_k, v_nsa_cmp_w1_k, v_nsa_cmp_w2_k, v_nsa_cmp_pos_v, v_nsa_cmp_w1_v, v_nsa_cmp_w2_v, v_mla_q_norm_g, v_mla_w_uq, v_mla_kv_norm_g, v_mla_w_ukv):
    given = dict(locals())
    local = {n: given[n] for n in _WEIGHTS}
    depth = pre_norm_g.shape[0]
    xs, target = x[0], loss_target[0]
    s = xs.shape[0]
    sharded = [n for n in _WEIGHTS if n in _SHARD_AXIS and n != "w_in"]
    small = [n for n in _WEIGHTS if n not in _SHARD_AXIS]
    chip = 2 * lax.axis_index("x") + lax.axis_index("y")
    core = lax.axis_index("c")
    own = lambda slots, mine: lax.dynamic_update_slice_in_dim(slots, mine[None], chip, axis=0)

    w_in_t = jnp.swapaxes(w_in, 1, 2).astype(_MXU)
    piece = lax.switch(chip, [functools.partial(_piece_from_shard, s=k) for k in range(N_CHIPS)], w_in_t)
    layer_shapes = [local[n].shape[1:] for n in sharded]
    flat = [_pack([local[n][l] for n in sharded], _MXU) for l in range(depth)]
    full = dict(local)
    for n in ["w_in"] + sharded:
        full[n] = []

    def add_layer(w_in_slots, flat_slots_):
        full["w_in"].append(w_in_slots)
        per_chip = [_unpack(flat_slots_[k], layer_shapes) for k in range(N_CHIPS)]
        for j, n in enumerate(sharded):
            full[n].append(jnp.concatenate([per_chip[k][j] for k in range(N_CHIPS)], axis=_SHARD_AXIS[n] - 1))

    halved = [piece[0].reshape(2, GROUP_W // 2, D_MODEL), flat[0].reshape(2, -1, LANES)]
    first_all = [own(a, b) for a, b in zip(_gather_chips(halved, name="gather_weights"), halved)]
    add_layer(first_all[0].reshape(N_CHIPS, GROUP_W, D_MODEL), first_all[1].reshape((N_CHIPS,) + flat[0].shape))
    later = [piece[l] for l in range(1, depth)] + flat[1:]
    started, token = _send_start(later, ["same"] * len(later), first_all[1], name="gather_later_start")
    full["pre_norm_g"] = pre_norm_g + token[0, 0]

    consts = _consts(s)
    params, act, saved = [], xs, []
    for l in range(depth):
        if l == 1:
            landed = [own(a, b) for a, b in zip(_send_wait(started, ["same"] * len(later), act,
                                                           name="gather_later_wait"), later)]
            for k in range(depth - 1):
                add_layer(landed[k], landed[depth - 1 + k])
        params.append(_layer_params(full, l))
        act, sv = _layer_fwd(act, params[l], consts, f"l{l}")
        saved.append(sv)
    dy, loss_parts = _loss_head(act, target, name="loss_head")

    def flat_slots(g, dtype):
        def part(n, k):
            a, ax = g[n], _SHARD_AXIS[n] - 1
            w = a.shape[ax] // N_CHIPS
            return lax.slice_in_dim(a, k * w, (k + 1) * w, axis=ax)
        return jnp.stack([_pack([part(n, k) for n in sharded], dtype) for k in range(N_CHIPS)])

    own_slot = lambda a: lax.dynamic_index_in_dim(a, chip, axis=0, keepdims=False)
    slots_of = lambda g: g["w_in"].reshape(N_CHIPS, GROUP_W, D_MODEL)

    modes = ["slots", "slots"]
    layer_grads, in_flight = [None] * depth, {}
    for l in reversed(range(depth)):
        dy, layer_grads[l] = _layer_bwd(dy, saved[l], params[l], consts, f"l{l}", _WIRE)
        if l > 0:
            wire = [slots_of(layer_grads[l]), flat_slots(layer_grads[l], _WIRE)]
            started, token = _send_start(wire, modes, dy, name=f"reduce_l{l}_start")
            in_flight[l] = (started, wire)
            params[l - 1] = dict(params[l - 1], post_g=params[l - 1]["post_g"] + token[0, 0])
    grad_x = dy[None]
    grads = {n: jnp.stack([layer_grads[l][n] for l in range(depth)]) for n in small}
    loss_row = jnp.concatenate([jnp.sum(loss_parts).reshape(1), jnp.zeros((LANES - 1,), f32)])
    small_shapes = [(LANES,)] + [grads[n].shape for n in small]
    contrib = _pack([loss_row] + [grads[n] for n in small], f32)

    halves = [slots_of(layer_grads[0]).reshape(N_CHIPS, 2, GROUP_W // 2, D_MODEL),
              flat_slots(layer_grads[0], _WIRE).reshape(N_CHIPS, 2, -1, LANES)]
    from_sibling = _swap_other_half(halves, name="reduce_pair")
    pair_sum = [_add_my_half(g, r, name=f"reduce_pair_add{j}") for j, (g, r) in enumerate(zip(halves, from_sibling))]
    from_chips = _alltoall_chips(pair_sum + [contrib], modes + ["same"], name="reduce_chips")
    my_half = [_sum_slots(own(slots, own_slot(ps)), name=f"reduce_chips_add{j}")
               for j, (slots, ps) in enumerate(zip(from_chips, pair_sum))]
    partial = []
    for l in range(1, depth):
        started, wire = in_flight[l]
        landed = _send_wait(started, modes, dy, name=f"reduce_l{l}_wait")
        partial += [_sum_slots(own(slots, own_slot(a)), name=f"reduce_l{l}_add{j}")
                    for j, (slots, a) in enumerate(zip(landed, wire))]
    partial.append(_sum_slots(own(from_chips[2], contrib), name="sum_small"))
    theirs = _swap_sibling(my_half + partial, name="reduce_share")
    first = core == 0
    whole = [jnp.concatenate([jnp.where(first, a, b), jnp.where(first, b, a)], axis=0)
             for a, b in zip(my_half, theirs[:2])]
    whole += [_add2(a[None], b[None], name=f"reduce_cores_add{j}")[0] for j, (a, b) in enumerate(zip(partial, theirs[2:]))]
    unpiece = [functools.partial(_shard_from_piece, s=k) for k in range(N_CHIPS)]
    summed = {"w_in": jnp.stack([lax.switch(chip, unpiece, whole[2 * l].T) for l in range(depth)])}
    rest = [_unpack(whole[2 * l + 1], layer_shapes) for l in range(depth)]
    for j, n in enumerate(sharded):
        summed[n] = jnp.stack([rest[l][j] for l in range(depth)])
    total = _unpack(whole[2 * depth], small_shapes)
    loss = total[0][0]
    summed.update(zip(small, total[1:]))

    deltas, new_m, new_v = {}, {}, {}
    for n in _WEIGHTS:
        deltas[n], new_m[n], new_v[n] = _adamw(local[n], summed[n], given["m_" + n], given["v_" + n], name=f"adamw_{n}")
    return (loss, grad_x, *[summed[n] for n in _WEIGHTS], *[deltas[n] for n in _WEIGHTS],
            *[new_m[n] for n in _WEIGHTS], *[new_v[n] for n in _WEIGHTS])
```

```python
import functools
import math

import numpy as np
import jax
import jax.numpy as jnp
from jax import lax
from jax.experimental import pallas as pl
from jax.experimental.pallas import tpu as pltpu

f32 = jnp.float32
bf16 = jnp.bfloat16
_MXU = jnp.bfloat16
_WIRE = jnp.bfloat16
_SDS = jax.ShapeDtypeStruct
_ANY = pl.BlockSpec(memory_space=pl.ANY)
_MESH = pl.DeviceIdType.MESH

D_MODEL = 2048
N_HEADS = 4
HEAD_DIM = 128
GROUP = 512
RMS_EPS = 1e-6
NEG_INF = -1e30
ROPE_THETA = 10000.0
CMP_LEN, CMP_STRIDE, SEL_LEN, SEL_TOPN, WINDOW = 32, 16, 64, 16, 512
FORCED_BONUS = 1e6
MLA_Q_RANK, MLA_KV_RANK, MLA_NOPE, MLA_ROPE = 384, 128, 128, 64
ADAM_LR, ADAM_B1, ADAM_B2, ADAM_EPS, ADAM_WD, ADAM_STEP = 0.001, 0.9, 0.999, 1e-08, 0.01, 10
LANES = 128
VMEM_LIMIT = 56 * 1024 * 1024
HP_FWD, HP_BWD = 2, 2

_SEGS = (
    ("sb_q", 512), ("sb_k", 512), ("sb_v", 512), ("sb_gate", 512), ("nsa_q", 512), ("nsa_k_cmp", 128),
    ("nsa_v_cmp", 128), ("nsa_k_sel", 128), ("nsa_v_sel", 128), ("nsa_k_win", 128), ("nsa_v_win", 128),
    ("nsa_branch", 12), ("nsa_gate", 512), ("fox_q", 512), ("fox_k", 512), ("fox_v", 512), ("fox_f", 4),
    ("fox_gate", 512), ("mla_cq", 384), ("mla_ckv", 128), ("mla_k_rope", 64), ("mla_gate", 512),
)
_ORIG, _WID = {}, {}
_o = 0
for _n, _w in _SEGS:
    _ORIG[_n], _WID[_n] = _o, _w
    _o += _w
IN_WIDTH = _o
N_CHIPS = 4
CHIP_COLS = IN_WIDTH // N_CHIPS
GROUP_W = 2048
ZW = N_CHIPS * GROUP_W
_GROUPS = (
    (("sb_q", 0, 512, 0), ("sb_k", 0, 512, 512), ("sb_v", 0, 512, 1024), ("sb_gate", 0, 212, 1536)),
    (("nsa_q", 0, 512, 0), ("nsa_k_cmp", 0, 128, 512), ("nsa_v_cmp", 0, 128, 640), ("nsa_k_sel", 0, 128, 768),
     ("nsa_v_sel", 0, 128, 896), ("nsa_k_win", 0, 128, 1024), ("nsa_v_win", 0, 128, 1152), ("nsa_branch", 0, 12, 1280),
     ("sb_gate", 212, 512, 1408), ("nsa_gate", 0, 156, 1712)),
    (("fox_q", 0, 512, 0), ("fox_k", 0, 512, 512), ("fox_v", 0, 368, 1024), ("nsa_gate", 156, 512, 1408)),
    (("mla_cq", 0, 384, 0), ("mla_ckv", 0, 128, 384), ("mla_k_rope", 0, 64, 512), ("fox_f", 0, 4, 640),
     ("fox_v", 368, 512, 768), ("fox_gate", 0, 512, 1024), ("mla_gate", 0, 512, 1536)),
)
_PIECES = {n: [] for n, _ in _SEGS}
for _s, _grp in enumerate(_GROUPS):
    _cover = sorted((_ORIG[n] + lo, _ORIG[n] + hi) for n, lo, hi, _ in _grp)
    assert _cover[0][0] == _s * CHIP_COLS and _cover[-1][1] == (_s + 1) * CHIP_COLS
    assert all(a[1] == b[0] for a, b in zip(_cover, _cover[1:]))
    _ends = sorted((off, off + hi - lo) for _, lo, hi, off in _grp)
    assert all(a[1] <= b[0] for a, b in zip(_ends, _ends[1:])) and _ends[-1][1] <= GROUP_W
    assert _ends[0][0] == 0 and all(e[0] % 16 == 0 for e in _ends)
    for _n, _lo, _hi, _off in _grp:
        _PIECES[_n].append((_s * GROUP_W + _off, _lo, _hi))
_AL = {n: p[0][0] for n, p in _PIECES.items() if len(p) == 1}


def _cp(sem=None):
    return pltpu.CompilerParams(dimension_semantics=sem, vmem_limit_bytes=VMEM_LIMIT)


def _mm(a, b):
    return jnp.dot(a.astype(_MXU), b.astype(_MXU), preferred_element_type=f32)


def _mm_nt(a, b):
    return lax.dot_general(a.astype(_MXU), b.astype(_MXU), (((1,), (1,)), ((), ())), preferred_element_type=f32)


def _mm_tn(a, b):
    return lax.dot_general(a.astype(_MXU), b.astype(_MXU), (((0,), (0,)), ((), ())), preferred_element_type=f32)


def _mm_split(x, t):
    hi = x.astype(_MXU)
    lo = (x - hi.astype(f32)).astype(_MXU)
    return jnp.dot(hi, t, preferred_element_type=f32) + jnp.dot(lo, t, preferred_element_type=f32)


def _sigmoid(x):
    return 1.0 / (1.0 + jnp.exp(-x))


def _iota(shape, dim):
    return lax.broadcasted_iota(jnp.int32, shape, dim)


def _pick(n, prefs):
    for p in prefs:
        if n % p == 0:
            return p
    return n


def _matmul(a, b, mode, *, bias=None, out_dtype=f32, twin=False, name):
    grouped = b.ndim == 3
    b_shape = (b.shape[0] * b.shape[1], b.shape[2]) if grouped else b.shape
    if mode == "nn":
        (M, K), (K2, N) = a.shape, b_shape
    elif mode == "nt":
        (M, K), (N, K2) = a.shape, b_shape
    else:
        (K, M), (K2, N) = a.shape, b_shape
    assert K == K2
    tm = _pick(M, (1024, 512, 384, 256, 128))
    tn = _pick(N, (1024, 512, 384, 256, 128))
    tk = K if K <= 2048 else _pick(K, (2048, 2432, 1024, 512))
    nk = K // tk
    a_spec = {"nn": pl.BlockSpec((tm, tk), lambda i, j, k: (i, k)),
              "nt": pl.BlockSpec((tm, tk), lambda i, j, k: (i, k)),
              "tn": pl.BlockSpec((tk, tm), lambda i, j, k: (k, i))}[mode]
    if not grouped:
        b_spec = {"nn": pl.BlockSpec((tk, tn), lambda i, j, k: (k, j)),
                  "nt": pl.BlockSpec((tn, tk), lambda i, j, k: (j, k)),
                  "tn": pl.BlockSpec((tk, tn), lambda i, j, k: (k, j))}[mode]
    elif mode == "nt":
        per = b.shape[1] // tn
        b_spec = pl.BlockSpec((None, tn, tk), lambda i, j, k: (j // per, j % per, k))
    else:
        assert mode == "nn"
        per = b.shape[1] // tk
        b_spec = pl.BlockSpec((None, tk, tn), lambda i, j, k: (k // per, k % per, j))
    dot = {"nn": _mm, "nt": _mm_nt, "tn": _mm_tn}[mode]
    has_bias = bias is not None

    def body(*refs):
        n_in = 3 if has_bias else 2
        a_ref, b_ref = refs[0], refs[1]
        bias_ref = refs[2] if has_bias else None
        o_ref, acc_ref = refs[n_in], refs[-1]
        k = pl.program_id(2)
        part = dot(a_ref[...], b_ref[...])

        def finish(total):
            if has_bias:
                total = total + bias_ref[...]
            o_ref[...] = total.astype(o_ref.dtype)
            if twin:
                refs[n_in + 1][...] = total.astype(_MXU)

        if nk == 1:
            finish(part)
        else:
            @pl.when(k == 0)
            def _():
                acc_ref[...] = part

            @pl.when(k > 0)
            def _():
                acc_ref[...] += part

            @pl.when(k == nk - 1)
            def _():
                finish(acc_ref[...])

    in_specs = [a_spec, b_spec]
    args = [a, b]
    if has_bias:
        in_specs.append(pl.BlockSpec((1, tn), lambda i, j, k: (0, j)))
        args.append(bias.reshape(1, N))
    o_spec = pl.BlockSpec((tm, tn), lambda i, j, k: (i, j))
    return pl.pallas_call(
        body, out_shape=(_SDS((M, N), out_dtype), _SDS((M, N), _MXU)) if twin else _SDS((M, N), out_dtype),
        grid=(M // tm, N // tn, nk), in_specs=in_specs, out_specs=(o_spec, o_spec) if twin else o_spec,
        scratch_shapes=[pltpu.VMEM((tm, tn), f32)],
        compiler_params=_cp(("parallel", "parallel", "arbitrary")), name=name,
    )(*args)


def _row_block(s):
    return _pick(s, (512, 256, 128))


def _rms_fwd(x, g, *, out_dtype, name):
    s, d = x.shape
    rb = _row_block(s)

    def body(x_ref, g_ref, o_ref):
        xv = x_ref[...]
        r = lax.rsqrt(jnp.mean(xv * xv, axis=-1, keepdims=True) + RMS_EPS)
        o_ref[...] = (xv * r * g_ref[...]).astype(o_ref.dtype)

    return pl.pallas_call(
        body, out_shape=_SDS((s, d), out_dtype), grid=(s // rb,),
        in_specs=[pl.BlockSpec((rb, d), lambda i: (i, 0)), pl.BlockSpec((1, d), lambda i: (0, 0))],
        out_specs=pl.BlockSpec((rb, d), lambda i: (i, 0)), compiler_params=_cp(("parallel",)), name=name,
    )(x, g.reshape(1, d))


def _postnorm_fwd(u, g, x, *, name):
    s, d = u.shape
    rb = _row_block(s)

    def body(u_ref, g_ref, x_ref, o_ref):
        uv = u_ref[...]
        r = lax.rsqrt(jnp.mean(uv * uv, axis=-1, keepdims=True) + RMS_EPS)
        o_ref[...] = x_ref[...] + uv * r * g_ref[...]

    return pl.pallas_call(
        body, out_shape=_SDS((s, d), f32), grid=(s // rb,),
        in_specs=[pl.BlockSpec((rb, d), lambda i: (i, 0)), pl.BlockSpec((1, d), lambda i: (0, 0)),
                  pl.BlockSpec((rb, d), lambda i: (i, 0))],
        out_specs=pl.BlockSpec((rb, d), lambda i: (i, 0)), compiler_params=_cp(("parallel",)), name=name,
    )(u, g.reshape(1, d), x)


def _fold_rows(v):
    r = v.shape[0]
    acc = v[0:8]
    for k in range(1, r // 8):
        acc = acc + v[8 * k:8 * k + 8]
    return acc


def _rms_bwd(dy, x, g, res=None, *, name):
    s, d = x.shape
    rb = _row_block(s)
    nb = s // rb
    has_res = res is not None

    def body(*refs):
        if has_res:
            dy_ref, x_ref, g_ref, res_ref, dx_ref, dg_ref, acc_ref = refs
        else:
            dy_ref, x_ref, g_ref, dx_ref, dg_ref, acc_ref = refs
        i = pl.program_id(0)
        xv = x_ref[...]
        r = lax.rsqrt(jnp.mean(xv * xv, axis=-1, keepdims=True) + RMS_EPS)
        xh = xv * r
        dyv = dy_ref[...]
        dxh = dyv * g_ref[...]
        dx = r * (dxh - xh * jnp.mean(dxh * xh, axis=-1, keepdims=True))
        if has_res:
            dx = dx + res_ref[...]
        dx_ref[...] = dx
        part = _fold_rows(dyv * xh)

        @pl.when(i == 0)
        def _():
            acc_ref[...] = part

        @pl.when(i > 0)
        def _():
            acc_ref[...] += part

        @pl.when(i == nb - 1)
        def _():
            dg_ref[...] = jnp.sum(acc_ref[...], axis=0, keepdims=True)

    blk = pl.BlockSpec((rb, d), lambda i: (i, 0))
    in_specs = [blk, blk, pl.BlockSpec((1, d), lambda i: (0, 0))] + ([blk] if has_res else [])
    args = [dy, x, g.reshape(1, d)] + ([res] if has_res else [])
    return pl.pallas_call(
        body, out_shape=(_SDS((s, d), f32), _SDS((1, d), f32)), grid=(nb,), in_specs=in_specs,
        out_specs=(blk, pl.BlockSpec((1, d), lambda i: (0, 0))),
        scratch_shapes=[pltpu.VMEM((8, d), f32)], compiler_params=_cp(("arbitrary",)), name=name,
    )(*args)


def _loss_head(y, target, *, name):
    s, d = y.shape
    rb = _row_block(s)
    nb = s // rb

    def body(y_ref, t_ref, dy_ref, l_ref):
        i = pl.program_id(0)
        e = y_ref[...] - t_ref[...]
        dy_ref[...] = e * (1.0 / d)
        rows = _fold_rows(e * e)
        part = rows[:, 0:LANES]
        for k in range(1, d // LANES):
            part = part + rows[:, k * LANES:(k + 1) * LANES]
        part = part * (0.5 / d)

        @pl.when(i == 0)
        def _():
            l_ref[...] = part

        @pl.when(i > 0)
        def _():
            l_ref[...] += part

    blk = pl.BlockSpec((rb, d), lambda i: (i, 0))
    return pl.pallas_call(
        body, out_shape=(_SDS((s, d), f32), _SDS((8, LANES), f32)), grid=(nb,), in_specs=[blk, blk],
        out_specs=(blk, pl.BlockSpec((8, LANES), lambda i: (0, 0))),
        compiler_params=_cp(("arbitrary",)), name=name,
    )(y, target)


def _colsum(a, *, name):
    s, n = a.shape
    rb = _row_block(s)
    nb = s // rb
    tn = _pick(n, (2432, 2048, 1024, 512, 384, 128))

    def body(a_ref, o_ref, acc_ref):
        i = pl.program_id(1)
        part = _fold_rows(a_ref[...].astype(f32))

        @pl.when(i == 0)
        def _():
            acc_ref[...] = part

        @pl.when(i > 0)
        def _():
            acc_ref[...] += part

        @pl.when(i == nb - 1)
        def _():
            o_ref[...] = jnp.sum(acc_ref[...], axis=0, keepdims=True)

    return pl.pallas_call(
        body, out_shape=_SDS((1, n), f32), grid=(n // tn, nb),
        in_specs=[pl.BlockSpec((rb, tn), lambda j, i: (i, j))], out_specs=pl.BlockSpec((1, tn), lambda j, i: (0, j)),
        scratch_shapes=[pltpu.VMEM((8, tn), f32)], compiler_params=_cp(("parallel", "arbitrary")), name=name,
    )(a)


def _gate_fwd(outs, gate, *, name):
    s, d = gate.shape
    rb = _row_block(s)
    n = len(outs)
    w = d // n

    def body(*refs):
        g_ref, m_ref = refs[n], refs[n + 1]
        for k in range(n):
            gv = g_ref[:, k * w:(k + 1) * w]
            m_ref[:, k * w:(k + 1) * w] = (refs[k][...] * (gv * _sigmoid(gv))).astype(m_ref.dtype)

    blk = pl.BlockSpec((rb, d), lambda i: (i, 0))
    part = pl.BlockSpec((rb, w), lambda i: (i, 0))
    return pl.pallas_call(body, out_shape=_SDS((s, d), _MXU), grid=(s // rb,), in_specs=[part] * n + [blk],
                          out_specs=blk, compiler_params=_cp(("parallel",)), name=name)(*outs, gate)


def _gate_bwd(dmix, outs, gate, *, name):
    s, d = gate.shape
    rb = _row_block(s)
    n = len(outs)
    w = d // n

    def body(*refs):
        dm_ref, o_refs, g_ref, do_refs, dg_ref = refs[0], refs[1:1 + n], refs[1 + n], refs[2 + n:2 + 2 * n], refs[-1]
        for k in range(n):
            sl = slice(k * w, (k + 1) * w)
            gv = g_ref[:, sl]
            sg = _sigmoid(gv)
            dm = dm_ref[:, sl]
            do_refs[k][...] = dm * (gv * sg)
            dg_ref[:, sl] = dm * o_refs[k][...] * (sg * (1.0 + gv * (1.0 - sg)))

    blk = pl.BlockSpec((rb, d), lambda i: (i, 0))
    part = pl.BlockSpec((rb, w), lambda i: (i, 0))
    return pl.pallas_call(body, out_shape=tuple(_SDS((s, w), f32) for _ in range(n)) + (_SDS((s, d), f32),),
                          grid=(s // rb,), in_specs=[blk] + [part] * n + [blk], out_specs=(part,) * n + (blk,),
                          compiler_params=_cp(("parallel",)), name=name)(dmix, *outs, gate)


def _adamw(w, g, m, v, *, name):
    shape = w.shape
    cols = shape[-1]
    rows = int(np.prod(shape[:-1])) if len(shape) > 1 else 1
    to2 = lambda t: t.reshape(rows, cols)
    rb = rows
    if rows * cols * 4 > (1 << 20):
        rb = max(d for d in range(8, rows + 1, 8) if rows % d == 0 and (d * cols * 4 <= (1600 << 10) or d == 8))

    def body(w_ref, g_ref, m_ref, v_ref, d_ref, nm_ref, nv_ref):
        gv = g_ref[...]
        mn = ADAM_B1 * m_ref[...] + (1.0 - ADAM_B1) * gv
        vn = ADAM_B2 * v_ref[...] + (1.0 - ADAM_B2) * (gv * gv)
        m_hat = mn / (1.0 - ADAM_B1 ** ADAM_STEP)
        v_hat = vn / (1.0 - ADAM_B2 ** ADAM_STEP)
        d_ref[...] = -ADAM_LR * (m_hat / (jnp.sqrt(v_hat) + ADAM_EPS) + ADAM_WD * w_ref[...])
        nm_ref[...] = mn
        nv_ref[...] = vn

    blk = pl.BlockSpec((rb, cols), lambda i: (i, 0))
    out = pl.pallas_call(body, out_shape=tuple(_SDS((rows, cols), f32) for _ in range(3)), grid=(rows // rb,),
                         in_specs=[blk] * 4, out_specs=(blk,) * 3, compiler_params=_cp(("parallel",)),
                         name=name)(to2(w), to2(g), to2(m), to2(v))
    return tuple(t.reshape(shape) for t in out)


def _sum_slots(a, *, name):
    p, n, c = a.shape
    rb = max(d for d in range(8, n + 1, 8) if n % d == 0 and (p * d * c * 4 <= (6 << 20) or d == 8))

    def body(a_ref, o_ref):
        acc = a_ref[0].astype(f32)
        for k in range(1, p):
            acc = acc + a_ref[k].astype(f32)
        o_ref[...] = acc

    return pl.pallas_call(body, out_shape=_SDS((n, c), f32), grid=(n // rb,),
                          in_specs=[pl.BlockSpec((p, rb, c), lambda i: (0, i, 0))],
                          out_specs=pl.BlockSpec((rb, c), lambda i: (i, 0)), compiler_params=_cp(("parallel",)),
                          name=name)(a)


def _add2(a, b, *, name):
    p, n, c = a.shape
    rb = max(d for d in range(8, n + 1, 8) if n % d == 0 and (d * c * 4 <= (2 << 20) or d == 8))

    def body(a_ref, b_ref, o_ref):
        o_ref[...] = a_ref[...] + b_ref[...]

    blk = pl.BlockSpec((1, rb, c), lambda s, i: (s, i, 0))
    return pl.pallas_call(body, out_shape=_SDS((p, n, c), f32), grid=(p, n // rb), in_specs=[blk, blk], out_specs=blk,
                          compiler_params=_cp(("parallel", "parallel")), name=name)(a, b)


def _rope_tables(pos, dim):
    half = dim // 2
    inv = ROPE_THETA ** (-jnp.arange(half, dtype=f32) / half)
    ang = pos.astype(f32)[:, None] * inv[None, :]
    c, s = jnp.cos(ang), jnp.sin(ang)
    z = jnp.zeros_like(c)
    pad = [jnp.zeros((pos.shape[0], LANES - dim), f32)] if dim < LANES else []
    return (jnp.concatenate([c, c] + pad, axis=1), jnp.concatenate([-s, z] + pad, axis=1),
            jnp.concatenate([z, s] + pad, axis=1))


def _rope(x, cos, sa, sb, half, transpose=False):
    if transpose:
        return x * cos + pltpu.roll(x * sa, half, 1) + pltpu.roll(x * sb, LANES - half, 1)
    return x * cos + pltpu.roll(x, LANES - half, 1) * sa + pltpu.roll(x, half, 1) * sb


def _rope_call(items, tables, half, transpose, *, name):
    s = items[0][0].shape[0]
    rb = _row_block(s)
    n = len(items)

    def body(*refs):
        cos, sa, sb = refs[n][...], refs[n + 1][...], refs[n + 2][...]
        for k in range(n):
            x_ref, o_ref = refs[k], refs[n + 3 + k]
            for j in range(items[k][1] // LANES):
                sl = slice(j * LANES, (j + 1) * LANES)
                o_ref[:, sl] = _rope(x_ref[:, sl], cos, sa, sb, half, transpose)

    in_specs = [pl.BlockSpec((rb, w), functools.partial(lambda i, cb: (i, cb), cb=cb)) for _, w, cb in items]
    in_specs += [pl.BlockSpec((rb, LANES), lambda i: (i, 0))] * 3
    out_specs = tuple(pl.BlockSpec((rb, w), lambda i: (i, 0)) for _, w, _ in items)
    return pl.pallas_call(
        body, out_shape=tuple(_SDS((s, w), f32) for _, w, _ in items), grid=(s // rb,), in_specs=in_specs,
        out_specs=out_specs, compiler_params=_cp(("parallel",)), name=name,
    )(*[a for a, _, _ in items], *tables)


def _attn_block(s):
    return _pick(s, (512, 256, 128))


def _lower_mask(b, strict):
    r, c = _iota((b, b), 0), _iota((b, b), 1)
    return (c < r) if strict else (c <= r)


def _pick_lane(block, h):
    return jnp.sum(jnp.where(_iota(block.shape, 1) == h, block, 0.0), axis=1, keepdims=True)


def _head_bias(cum_blk, g, j, hp):
    if hp == N_HEADS:
        return cum_blk[:, j:j + 1]
    return _pick_lane(cum_blk, g * hp + j)


def _attn_fwd(q, k, v, qcol, kcol, vcol, dq, cum, cum_t, *, scale, hp, name):
    s = q.shape[0]
    b = _attn_block(s)
    nq = s // b
    has_bias = cum is not None
    assert qcol % hp == 0 and kcol % hp == 0 and vcol % hp == 0

    def body(*refs):
        if has_bias:
            q_ref, k_ref, v_ref, cum_ref, cumt_ref, o_ref, lse_ref = refs
        else:
            q_ref, k_ref, v_ref, o_ref, lse_ref = refs
        g, i = pl.program_id(0), pl.program_id(1)
        qs = [q_ref[:, j * dq:(j + 1) * dq].astype(_MXU) for j in range(hp)]
        cqs = [_head_bias(cum_ref[...], g, j, hp) for j in range(hp)] if has_bias else None

        def chunk(c, carry, diag):
            st = pl.multiple_of(c * b, b)
            mask = _lower_mask(b, False) if diag else None
            out = []
            for j in range(hp):
                m, l, acc = carry[j]
                z = _mm_nt(qs[j], k_ref[pl.ds(st, b), j * dq:(j + 1) * dq]) * scale
                if has_bias:
                    z = z + (cqs[j] - cumt_ref[j, c])
                if diag:
                    z = jnp.where(mask, z, NEG_INF)
                m_new = jnp.maximum(m, jnp.max(z, axis=1, keepdims=True))
                p = jnp.exp(z - m_new)
                if diag:
                    p = jnp.where(mask, p, 0.0)
                alpha = jnp.exp(m - m_new)
                l = alpha * l + jnp.sum(p, axis=1, keepdims=True)
                acc = alpha * acc + _mm(p, v_ref[pl.ds(st, b), j * HEAD_DIM:(j + 1) * HEAD_DIM])
                out.append((m_new, l, acc))
            return tuple(out)

        init = tuple((jnp.full((b, 1), NEG_INF, f32), jnp.zeros((b, 1), f32), jnp.zeros((b, HEAD_DIM), f32))
                     for _ in range(hp))
        carry = lax.fori_loop(0, i, lambda c, cr: chunk(c, cr, False), init)
        for j, (m, l, acc) in enumerate(chunk(i, carry, True)):
            o_ref[:, j * HEAD_DIM:(j + 1) * HEAD_DIM] = acc / l
            lse_ref[j] = m + jnp.log(l)

    in_specs = [pl.BlockSpec((b, hp * dq), lambda g, i: (i, qcol // hp + g)),
                pl.BlockSpec((s, hp * dq), lambda g, i: (0, kcol // hp + g)),
                pl.BlockSpec((s, hp * HEAD_DIM), lambda g, i: (0, vcol // hp + g))]
    args = [q, k, v]
    if has_bias:
        in_specs += [pl.BlockSpec((b, LANES), lambda g, i: (i, 0)),
                     pl.BlockSpec((hp, nq, 1, b), lambda g, i: (g, 0, 0, 0))]
        args += [cum, cum_t]
    return pl.pallas_call(
        body, out_shape=(_SDS((s, N_HEADS * HEAD_DIM), f32), _SDS((N_HEADS, s, 1), f32)), grid=(N_HEADS // hp, nq),
        in_specs=in_specs,
        out_specs=(pl.BlockSpec((b, hp * HEAD_DIM), lambda g, i: (i, g)),
                   pl.BlockSpec((hp, b, 1), lambda g, i: (g, i, 0))),
        compiler_params=_cp(("parallel", "parallel")), name=name,
    )(*args)


def _attn_bwd(q, k, v, qcol, kcol, vcol, dq, do, o, lse, cum, cum_t, *, scale, hp, name):
    s = q.shape[0]
    b = _attn_block(s)
    nq = s // b
    has_bias = cum is not None
    assert qcol % hp == 0 and kcol % hp == 0 and vcol % hp == 0
    hd = lambda j: slice(j * HEAD_DIM, (j + 1) * HEAD_DIM)
    hq = lambda j: slice(j * dq, (j + 1) * dq)

    def body(*refs):
        if has_bias:
            (q_ref, k_ref, v_ref, do_ref, o_ref, lse_ref, cum_ref, cumt_ref, dq_ref, dk_ref, dv_ref, dck_ref,
             dkt_sc, dvt_sc, p_sc, dp_sc) = refs
        else:
            q_ref, k_ref, v_ref, do_ref, o_ref, lse_ref, dq_ref, dk_ref, dv_ref, dkt_sc, dvt_sc = refs
        g, i = pl.program_id(0), pl.program_id(1)

        @pl.when(i == 0)
        def _():
            dkt_sc[...] = jnp.zeros_like(dkt_sc)
            dvt_sc[...] = jnp.zeros_like(dvt_sc)
            if has_bias:
                dck_ref[...] = jnp.zeros_like(dck_ref)

        qs = [q_ref[:, hq(j)].astype(_MXU) for j in range(hp)]
        dos = [do_ref[:, hd(j)].astype(_MXU) for j in range(hp)]
        qts = [q_ref[:, hq(j)].astype(f32).T.astype(_MXU) for j in range(hp)]
        dots = [do_ref[:, hd(j)].T.astype(_MXU) for j in range(hp)]
        lses = [lse_ref[j] for j in range(hp)]
        cqs = [_head_bias(cum_ref[...], g, j, hp) for j in range(hp)] if has_bias else None

        def probs(j, c, diag):
            st = pl.multiple_of(c * b, b)
            z = _mm_nt(qs[j], k_ref[pl.ds(st, b), hq(j)]) * scale
            if has_bias:
                z = z + (cqs[j] - cumt_ref[j, c])
            p = jnp.exp(z - lses[j])
            if diag:
                p = jnp.where(_lower_mask(b, False), p, 0.0)
            return p, _mm_nt(dos[j], v_ref[pl.ds(st, b), hd(j)])

        if has_bias:
            def first(c, accs, diag):
                out = []
                for j in range(hp):
                    p, dp = probs(j, c, diag)
                    p_sc[j, c] = p
                    dp_sc[j, c] = dp
                    out.append(accs[j] + jnp.sum(p * dp, axis=1, keepdims=True))
                return tuple(out)

            deltas = lax.fori_loop(0, i, lambda c, a: first(c, a, False),
                                   tuple(jnp.zeros((b, 1), f32) for _ in range(hp)))
            deltas = first(i, deltas, True)
        else:
            deltas = [jnp.sum(do_ref[:, hd(j)] * o_ref[:, hd(j)], axis=1, keepdims=True) for j in range(hp)]

        def chunk(c, dq_accs, diag):
            st = pl.multiple_of(c * b, b)
            out = []
            for j in range(hp):
                p, dp = (p_sc[j, c], dp_sc[j, c]) if has_bias else probs(j, c, diag)
                ds = p * (dp - deltas[j])
                dkt_sc[j, c] += _mm(qts[j], ds)
                dvt_sc[j, c] += _mm(dots[j], p)
                if has_bias:
                    dck_ref[j, c] += -jnp.sum(ds, axis=0, keepdims=True)
                out.append(dq_accs[j] + _mm(ds, k_ref[pl.ds(st, b), hq(j)]))
            return tuple(out)

        accs = lax.fori_loop(0, i, lambda c, a: chunk(c, a, False), tuple(jnp.zeros((b, dq), f32) for _ in range(hp)))
        for j, acc in enumerate(chunk(i, accs, True)):
            dq_ref[:, hq(j)] = acc * scale

        @pl.when(i == nq - 1)
        def _():
            for j in range(hp):
                for c in range(nq):
                    dk_ref[c * b:(c + 1) * b, hq(j)] = dkt_sc[j, c].T * scale
                    dv_ref[c * b:(c + 1) * b, hd(j)] = dvt_sc[j, c].T

    rowq = pl.BlockSpec((b, hp * HEAD_DIM), lambda g, i: (i, g))
    in_specs = [pl.BlockSpec((b, hp * dq), lambda g, i: (i, qcol // hp + g)),
                pl.BlockSpec((s, hp * dq), lambda g, i: (0, kcol // hp + g)),
                pl.BlockSpec((s, hp * HEAD_DIM), lambda g, i: (0, vcol // hp + g)), rowq, rowq,
                pl.BlockSpec((hp, b, 1), lambda g, i: (g, i, 0))]
    args = [q, k, v, do, o, lse]
    out_shape = [_SDS((s, N_HEADS * dq), f32), _SDS((s, N_HEADS * dq), f32), _SDS((s, N_HEADS * HEAD_DIM), f32)]
    out_specs = [pl.BlockSpec((b, hp * dq), lambda g, i: (i, g)), pl.BlockSpec((s, hp * dq), lambda g, i: (0, g)),
                 pl.BlockSpec((s, hp * HEAD_DIM), lambda g, i: (0, g))]
    if has_bias:
        in_specs += [pl.BlockSpec((b, LANES), lambda g, i: (i, 0)),
                     pl.BlockSpec((hp, nq, 1, b), lambda g, i: (g, 0, 0, 0))]
        args += [cum, cum_t]
        out_shape.append(_SDS((N_HEADS, nq, 1, b), f32))
        out_specs.append(pl.BlockSpec((hp, nq, 1, b), lambda g, i: (g, 0, 0, 0)))
    return pl.pallas_call(
        body, out_shape=tuple(out_shape), grid=(N_HEADS // hp, nq), in_specs=in_specs, out_specs=tuple(out_specs),
        scratch_shapes=[pltpu.VMEM((hp, nq, dq, b), f32), pltpu.VMEM((hp, nq, HEAD_DIM, b), f32)]
        + ([pltpu.VMEM((hp, nq, b, b), f32)] * 2 if has_bias else []),
        compiler_params=_cp(("parallel", "arbitrary")), name=name,
    )(*args)


def _tri(b, kind):
    r, c = _iota((b, b), 0), _iota((b, b), 1)
    cond = {"row_gt": r > c, "row_lt": r < c, "row_ge": r >= c, "row_le": r <= c}[kind]
    return jnp.where(cond, 1.0, 0.0).astype(_MXU)


def _log_keep(z):
    return -(jnp.maximum(z, 0.0) + jnp.log(1.0 + jnp.exp(-jnp.abs(z))))


def _sb_fwd(z_all, *, hp, name):
    s = z_all.shape[0]
    b = _attn_block(s)
    nq = s // b
    scale = HEAD_DIM ** -0.5
    qcol, kcol, vcol = (_AL[n] // (hp * HEAD_DIM) for n in ("sb_q", "sb_k", "sb_v"))
    hd = lambda j: slice(j * HEAD_DIM, (j + 1) * HEAD_DIM)

    def body(q_ref, k_ref, v_ref, o_ref):
        i = pl.program_id(1)
        qs = [q_ref[:, hd(j)].astype(_MXU) for j in range(hp)]
        upper = _tri(b, "row_gt")

        def chunk(c, carry, diag):
            st = pl.multiple_of(c * b, b)
            mask = _lower_mask(b, True) if diag else None
            out = []
            for j in range(hp):
                rsum, acc = carry[j]
                z = _mm_nt(qs[j], k_ref[pl.ds(st, b), hd(j)]) * scale
                lk = _log_keep(z)
                if diag:
                    lk = jnp.where(mask, lk, 0.0)
                a = z + lk + _mm_split(lk, upper) + rsum
                if diag:
                    a = jnp.where(mask, a, NEG_INF)
                acc = acc + _mm(jnp.exp(a), v_ref[pl.ds(st, b), hd(j)])
                out.append((rsum + jnp.sum(lk, axis=1, keepdims=True), acc))
            return tuple(out)

        init = tuple((jnp.zeros((b, 1), f32), jnp.zeros((b, HEAD_DIM), f32)) for _ in range(hp))
        carry = lax.fori_loop(0, i, lambda t, cr: chunk(i - 1 - t, cr, False), chunk(i, init, True))
        for j in range(hp):
            o_ref[:, hd(j)] = carry[j][1]

    w = hp * HEAD_DIM
    return pl.pallas_call(
        body, out_shape=_SDS((s, GROUP), f32), grid=(N_HEADS // hp, nq),
        in_specs=[pl.BlockSpec((b, w), lambda g, i: (i, qcol + g)), pl.BlockSpec((s, w), lambda g, i: (0, kcol + g)),
                  pl.BlockSpec((s, w), lambda g, i: (0, vcol + g))],
        out_specs=pl.BlockSpec((b, w), lambda g, i: (i, g)),
        compiler_params=_cp(("parallel", "parallel")), name=name,
    )(z_all, z_all, z_all)


def _sb_bwd(z_all, do, *, hp, name):
    s = z_all.shape[0]
    b = _attn_block(s)
    nq = s // b
    scale = HEAD_DIM ** -0.5
    qcol, kcol, vcol = (_AL[n] // (hp * HEAD_DIM) for n in ("sb_q", "sb_k", "sb_v"))
    hd = lambda j: slice(j * HEAD_DIM, (j + 1) * HEAD_DIM)

    def body(q_ref, k_ref, v_ref, do_ref, dq_ref, dk_ref, dv_ref, z_sc, lk_sc, r_sc):
        i = pl.program_id(1)

        @pl.when(i == 0)
        def _():
            dk_ref[...] = jnp.zeros_like(dk_ref)
            dv_ref[...] = jnp.zeros_like(dv_ref)

        qs = [q_ref[:, hd(j)].astype(_MXU) for j in range(hp)]
        dos = [do_ref[:, hd(j)].astype(_MXU) for j in range(hp)]
        upper = _tri(b, "row_gt")
        lower = _tri(b, "row_lt")

        def scores(c, rsums, diag):
            st = pl.multiple_of(c * b, b)
            out = []
            for j in range(hp):
                z = _mm_nt(qs[j], k_ref[pl.ds(st, b), hd(j)]) * scale
                lk = _log_keep(z)
                if diag:
                    lk = jnp.where(_lower_mask(b, True), lk, 0.0)
                z_sc[j, c] = z
                lk_sc[j, c] = lk
                r_sc[j, c] = _mm_split(lk, upper) + rsums[j]
                out.append(rsums[j] + jnp.sum(lk, axis=1, keepdims=True))
            return tuple(out)

        rsums = scores(i, tuple(jnp.zeros((b, 1), f32) for _ in range(hp)), True)
        lax.fori_loop(0, i, lambda t, r: scores(i - 1 - t, r, False), rsums)

        def grads(c, carry, diag):
            st = pl.multiple_of(c * b, b)
            mask = _lower_mask(b, True) if diag else None
            out = []
            for j in range(hp):
                psum, dq_acc = carry[j]
                z, lk = z_sc[j, c], lk_sc[j, c]
                lb = z + lk
                a = lb + r_sc[j, c]
                if diag:
                    a = jnp.where(mask, a, NEG_INF)
                w = jnp.exp(a)
                e = _mm_nt(dos[j], v_ref[pl.ds(st, b), hd(j)]) * w
                before = _mm_split(e, lower) + psum
                dz = e * jnp.exp(lk) - before * jnp.exp(lb)
                if diag:
                    dz = jnp.where(mask, dz, 0.0)
                dk_ref[pl.ds(st, b), hd(j)] += _mm_tn(dz, qs[j]) * scale
                dv_ref[pl.ds(st, b), hd(j)] += _mm_tn(w, dos[j])
                out.append((psum + jnp.sum(e, axis=1, keepdims=True), dq_acc + _mm(dz, k_ref[pl.ds(st, b), hd(j)])))
            return tuple(out)

        init = tuple((jnp.zeros((b, 1), f32), jnp.zeros((b, HEAD_DIM), f32)) for _ in range(hp))
        carry = grads(i, lax.fori_loop(0, i, lambda c, cr: grads(c, cr, False), init), True)
        for j in range(hp):
            dq_ref[:, hd(j)] = carry[j][1] * scale

    w = hp * HEAD_DIM
    blk = pl.BlockSpec((b, w), lambda g, i: (i, g))
    full = pl.BlockSpec((s, w), lambda g, i: (0, g))
    return pl.pallas_call(
        body, out_shape=tuple(_SDS((s, GROUP), f32) for _ in range(3)), grid=(N_HEADS // hp, nq),
        in_specs=[pl.BlockSpec((b, w), lambda g, i: (i, qcol + g)), pl.BlockSpec((s, w), lambda g, i: (0, kcol + g)),
                  pl.BlockSpec((s, w), lambda g, i: (0, vcol + g)), blk],
        out_specs=(blk, full, full),
        scratch_shapes=[pltpu.VMEM((hp, nq, b, b), f32)] * 3,
        compiler_params=_cp(("parallel", "arbitrary")), name=name,
    )(z_all, z_all, z_all, do)


def _split3_left(t, x):
    hi = x.astype(_MXU)
    r1 = x - hi.astype(f32)
    mid = r1.astype(_MXU)
    lo = (r1 - mid.astype(f32)).astype(_MXU)
    dot = functools.partial(jnp.dot, preferred_element_type=f32)
    return dot(t, hi) + dot(t, mid) + dot(t, lo)


def _split3_right(x, t):
    hi = x.astype(_MXU)
    r1 = x - hi.astype(f32)
    mid = r1.astype(_MXU)
    lo = (r1 - mid.astype(f32)).astype(_MXU)
    dot = functools.partial(jnp.dot, preferred_element_type=f32)
    return dot(hi, t) + dot(mid, t) + dot(lo, t)


def _fox_cum_fwd(z_all, bias, *, name):
    s = z_all.shape[0]
    b = _attn_block(s)
    fcol = _AL["fox_f"] // LANES

    def body(f_ref, b_ref, cum_ref, cumt_ref, carry_ref):
        i = pl.program_id(0)

        @pl.when(i == 0)
        def _():
            carry_ref[...] = jnp.zeros_like(carry_ref)

        u = f_ref[...] + b_ref[...]
        lf = jnp.minimum(u, 0.0) - jnp.log1p(jnp.exp(-jnp.abs(u)))
        cum = _split3_left(_tri(b, "row_ge"), lf) + carry_ref[...]
        cum_ref[...] = cum
        cumt_ref[...] = cum.T[0:8, :]
        carry_ref[...] = cum_ref[b - 1:b, :]

    return pl.pallas_call(
        body, out_shape=(_SDS((s, LANES), f32), _SDS((8, s), f32)), grid=(s // b,),
        in_specs=[pl.BlockSpec((b, LANES), lambda i: (i, fcol)), pl.BlockSpec((1, LANES), lambda i: (0, 0))],
        out_specs=(pl.BlockSpec((b, LANES), lambda i: (i, 0)), pl.BlockSpec((8, b), lambda i: (0, i))),
        scratch_shapes=[pltpu.VMEM((1, LANES), f32)], compiler_params=_cp(("arbitrary",)), name=name,
    )(z_all, bias)


def _fox_cum_bwd(z_all, bias, dcum_t, *, name):
    s = z_all.shape[0]
    b = _attn_block(s)
    nb = s // b
    fcol = _AL["fox_f"] // LANES

    def body(f_ref, b_ref, dc_ref, df_ref, db_ref, carry_ref):
        i = pl.program_id(0)

        @pl.when(i == 0)
        def _():
            carry_ref[...] = jnp.zeros_like(carry_ref)
            db_ref[...] = jnp.zeros_like(db_ref)

        dc = dc_ref[...]
        rev = _split3_right(dc, _tri(b, "row_ge")) + carry_ref[...]
        carry_ref[...] = carry_ref[...] + jnp.sum(dc, axis=1, keepdims=True)
        dlf = jnp.concatenate([rev, jnp.zeros((LANES - 8, b), f32)], axis=0).T
        u = f_ref[...] + b_ref[...]
        df = jnp.where(_iota((b, LANES), 1) < N_HEADS, dlf * (1.0 - _sigmoid(u)), 0.0)
        df_ref[...] = df
        db_ref[...] += jnp.sum(df, axis=0, keepdims=True)

    return pl.pallas_call(
        body, out_shape=(_SDS((s, LANES), f32), _SDS((1, LANES), f32)), grid=(nb,),
        in_specs=[pl.BlockSpec((b, LANES), lambda i: (nb - 1 - i, fcol)), pl.BlockSpec((1, LANES), lambda i: (0, 0)),
                  pl.BlockSpec((8, b), lambda i: (0, nb - 1 - i))],
        out_specs=(pl.BlockSpec((b, LANES), lambda i: (nb - 1 - i, 0)), pl.BlockSpec((1, LANES), lambda i: (0, 0))),
        scratch_shapes=[pltpu.VMEM((8, 1), f32)], compiler_params=_cp(("arbitrary",)), name=name,
    )(z_all, bias, dcum_t)


MLA_QW = 2 * LANES


def _rms_rows(x):
    r = lax.rsqrt(jnp.mean(x * x, axis=-1, keepdims=True) + RMS_EPS)
    return x * r, r


def _mla_prep_fwd(z_all, gq, gkv, wuq, wk, wv, tables, *, name):
    s = z_all.shape[0]
    rb = _row_block(s)
    half = MLA_ROPE // 2

    def body(cq_ref, ckv_ref, kr_ref, gq_ref, gkv_ref, wuq_ref, wk_ref, wv_ref, cos_ref, sa_ref, sb_ref,
             q_ref, k_ref, v_ref):
        cos, sa, sb = cos_ref[...], sa_ref[...], sb_ref[...]
        xh, _ = _rms_rows(cq_ref[...])
        qp = _mm(xh * gq_ref[...], wuq_ref[...])
        kh, _ = _rms_rows(ckv_ref[...])
        nkv = kh * gkv_ref[...]
        kn = _mm(nkv, wk_ref[...])
        v_ref[...] = _mm(nkv, wv_ref[...]).astype(v_ref.dtype)
        kr = _rope(kr_ref[...], cos, sa, sb, half)
        for h in range(N_HEADS):
            lo, mid, hi = h * MLA_QW, h * MLA_QW + LANES, (h + 1) * MLA_QW
            q_ref[:, lo:mid] = qp[:, lo:mid].astype(q_ref.dtype)
            q_ref[:, mid:hi] = _rope(qp[:, mid:hi], cos, sa, sb, half).astype(q_ref.dtype)
            k_ref[:, lo:mid] = kn[:, h * LANES:(h + 1) * LANES].astype(k_ref.dtype)
            k_ref[:, mid:hi] = kr.astype(k_ref.dtype)

    row = lambda w, cb: pl.BlockSpec((rb, w), lambda i: (i, cb))
    whole = lambda a: pl.BlockSpec(a.shape, lambda i: (0,) * a.ndim)
    return pl.pallas_call(
        body, out_shape=(_SDS((s, N_HEADS * MLA_QW), _MXU), _SDS((s, N_HEADS * MLA_QW), _MXU), _SDS((s, GROUP), _MXU)),
        grid=(s // rb,),
        in_specs=[row(MLA_Q_RANK, _AL["mla_cq"] // MLA_Q_RANK), row(LANES, _AL["mla_ckv"] // LANES),
                  row(LANES, _AL["mla_k_rope"] // LANES), whole(gq), whole(gkv), whole(wuq), whole(wk), whole(wv),
                  row(LANES, 0), row(LANES, 0), row(LANES, 0)],
        out_specs=(row(N_HEADS * MLA_QW, 0), row(N_HEADS * MLA_QW, 0), row(GROUP, 0)),
        compiler_params=_cp(("parallel",)), name=name,
    )(z_all, z_all, z_all, gq, gkv, wuq, wk, wv, *tables)


def _mla_prep_bwd(z_all, gq, gkv, wuq, wk, wv, tables, dq_cat, dk_cat, dv, *, name):
    s = z_all.shape[0]
    rb = _row_block(s)
    half = MLA_ROPE // 2

    def body(cq_ref, ckv_ref, gq_ref, gkv_ref, wuq_ref, wk_ref, wv_ref, cos_ref, sa_ref, sb_ref, dq_ref, dk_ref,
             dv_ref, dcq_ref, dckv_ref, dkr_ref, dwuq_ref, dwk_ref, dwv_ref, dgq_ref, dgkv_ref):
        i = pl.program_id(0)

        @pl.when(i == 0)
        def _():
            for r in (dwuq_ref, dwk_ref, dwv_ref, dgq_ref, dgkv_ref):
                r[...] = jnp.zeros_like(r)

        cos, sa, sb = cos_ref[...], sa_ref[...], sb_ref[...]
        parts, knp = [], []
        dkr = jnp.zeros((rb, LANES), f32)
        for h in range(N_HEADS):
            lo, mid, hi = h * MLA_QW, h * MLA_QW + LANES, (h + 1) * MLA_QW
            parts += [dq_ref[:, lo:mid], _rope(dq_ref[:, mid:hi], cos, sa, sb, half, transpose=True)]
            knp.append(dk_ref[:, lo:mid])
            dkr = dkr + _rope(dk_ref[:, mid:hi], cos, sa, sb, half, transpose=True)
        dkr_ref[...] = dkr
        dqp = jnp.concatenate(parts, axis=1)
        dkn = jnp.concatenate(knp, axis=1)
        dvv = dv_ref[...]

        def norm_bwd(x_ref, g_ref, w_pairs, dx_ref, dg_ref):
            xh, r = _rms_rows(x_ref[...])
            nx = xh * g_ref[...]
            dn = jnp.zeros_like(xh)
            for w_ref, dw_ref, dy in w_pairs:
                dw_ref[...] += _mm_tn(nx, dy)
                dn = dn + _mm_nt(dy, w_ref[...])
            dxh = dn * g_ref[...]
            dx_ref[...] = r * (dxh - xh * jnp.mean(dxh * xh, axis=-1, keepdims=True))
            dg_ref[...] += jnp.sum(dn * xh, axis=0, keepdims=True)

        norm_bwd(cq_ref, gq_ref, [(wuq_ref, dwuq_ref, dqp)], dcq_ref, dgq_ref)
        norm_bwd(ckv_ref, gkv_ref, [(wk_ref, dwk_ref, dkn), (wv_ref, dwv_ref, dvv)], dckv_ref, dgkv_ref)

    row = lambda w, cb: pl.BlockSpec((rb, w), lambda i: (i, cb))
    whole = lambda a: pl.BlockSpec(a.shape, lambda i: (0,) * a.ndim)
    return pl.pallas_call(
        body,
        out_shape=(_SDS((s, MLA_Q_RANK), f32), _SDS((s, LANES), f32), _SDS((s, LANES), f32), _SDS(wuq.shape, f32),
                   _SDS(wk.shape, f32), _SDS(wv.shape, f32), _SDS(gq.shape, f32), _SDS(gkv.shape, f32)),
        grid=(s // rb,),
        in_specs=[row(MLA_Q_RANK, _AL["mla_cq"] // MLA_Q_RANK), row(LANES, _AL["mla_ckv"] // LANES), whole(gq),
                  whole(gkv), whole(wuq), whole(wk), whole(wv), row(LANES, 0), row(LANES, 0), row(LANES, 0),
                  row(N_HEADS * MLA_QW, 0), row(N_HEADS * MLA_QW, 0), row(GROUP, 0)],
        out_specs=(row(MLA_Q_RANK, 0), row(LANES, 0), row(LANES, 0), whole(wuq), whole(wk), whole(wv), whole(gq),
                   whole(gkv)),
        compiler_params=_cp(("arbitrary",)), name=name,
    )(z_all, z_all, gq, gkv, wuq, wk, wv, *tables, dq_cat, dk_cat, dv)


def _silu_grad(x):
    sg = _sigmoid(x)
    return sg * (1.0 + x * (1.0 - sg))


def _nsa_cmp_fwd(ra, rb_, pos, w1, w2, tables, *, name):
    nr = ra.shape[1]
    hw = ra.shape[2]

    def body(ra_ref, rb_ref, pos_ref, w1_ref, w2_ref, cos_ref, sa_ref, sb_ref, out_ref, hp_ref):
        for k in range(2):
            xa = ra_ref[k] + pos_ref[k, :, 0:hw]
            xb = rb_ref[k] + pos_ref[k, :, hw:2 * hw]
            hp = _mm(xa, w1_ref[k, 0:hw, :]) + _mm(xb, w1_ref[k, hw:2 * hw, :])
            hp_ref[k] = hp
            out = _mm(hp * _sigmoid(hp), w2_ref[k])
            if k == 0:
                out = _rope(out, cos_ref[...], sa_ref[...], sb_ref[...], HEAD_DIM // 2)
            out_ref[k] = out

    return pl.pallas_call(body, out_shape=(_SDS((2, nr, HEAD_DIM), f32), _SDS((2, nr, HEAD_DIM), f32)),
                          compiler_params=_cp(), name=name)(ra, rb_, pos, w1, w2, *tables)


def _nsa_cmp_bwd(ra, rb_, pos, w1, w2, tables, hp, dout, *, name):
    nr = ra.shape[1]
    hw = ra.shape[2]

    def body(ra_ref, rb_ref, pos_ref, w1_ref, w2_ref, cos_ref, sa_ref, sb_ref, hp_ref, do_ref,
             dxa_ref, dxb_ref, dw1_ref, dw2_ref):
        for k in range(2):
            d_out = do_ref[k]
            if k == 0:
                d_out = _rope(d_out, cos_ref[...], sa_ref[...], sb_ref[...], HEAD_DIM // 2, transpose=True)
            hpv = hp_ref[k]
            dw2_ref[k] = _mm_tn(hpv * _sigmoid(hpv), d_out)
            dhp = _mm_nt(d_out, w2_ref[k]) * _silu_grad(hpv)
            xa = ra_ref[k] + pos_ref[k, :, 0:hw]
            xb = rb_ref[k] + pos_ref[k, :, hw:2 * hw]
            dw1_ref[k, 0:hw, :] = _mm_tn(xa, dhp)
            dw1_ref[k, hw:2 * hw, :] = _mm_tn(xb, dhp)
            dxa_ref[k] = _mm_nt(dhp, w1_ref[k, 0:hw, :])
            dxb_ref[k] = _mm_nt(dhp, w1_ref[k, hw:2 * hw, :])

    return pl.pallas_call(
        body, out_shape=(_SDS((2, nr, hw), f32), _SDS((2, nr, hw), f32), _SDS(w1.shape, f32), _SDS(w2.shape, f32)),
        compiler_params=_cp(), name=name)(ra, rb_, pos, w1, w2, *tables, hp, dout)


def _nsa_consts(s):
    b = _attn_block(s)
    nr = s // CMP_STRIDE
    n_cmp = (s - CMP_LEN) // CMP_STRIDE + 1
    n_sel = s // SEL_LEN
    cmp_start = np.arange(n_cmp) * CMP_STRIDE
    sel_start = np.arange(n_sel) * SEL_LEN
    overlap = np.clip(np.minimum(cmp_start[:, None] + CMP_LEN, sel_start[None, :] + SEL_LEN)
                      - np.maximum(cmp_start[:, None], sel_start[None, :]), 0, None)
    m2s = np.zeros((nr, LANES), np.float32)
    m2s[:n_cmp, :n_sel] = overlap / CMP_LEN
    e3 = np.zeros((s // b, LANES, b), np.float32)
    tok = np.arange(s)
    e3[tok // b, tok // SEL_LEN, tok % b] = 1.0
    return jnp.asarray(m2s, _MXU), jnp.asarray(e3, _MXU)


def _nsa_masks(i, b, d):
    qpos = i * b + _iota((b, b), 0)
    kpos = (i - d) * b + _iota((b, b), 1)
    return (kpos <= qpos) & (kpos > qpos - WINDOW)


def _nsa_fwd(qr, kvc, ksr, vs, kwr, vw, z_all, m2s, e3, *, name):
    s = qr.shape[0]
    b = _attn_block(s)
    nq = s // b
    nr = kvc.shape[1]
    n_sel = s // SEL_LEN
    top_n = min(SEL_TOPN, n_sel)
    nd = -(-WINDOW // b)
    scale = HEAD_DIM ** -0.5
    bcol = _AL["nsa_branch"] // LANES
    H = N_HEADS

    def body(q_ref, kvc_ref, ks_ref, vs_ref, kw_ref, vw_ref, br_ref, m2s_ref, e3_ref,
             o_ref, oc_ref, os_ref, ow_ref, st_ref, sel_ref, m_sc, l_sc, acc_sc):
        i = pl.program_id(0)
        lane = _iota((b, LANES), 1)
        hs = lambda h: slice(h * HEAD_DIM, (h + 1) * HEAD_DIM)

        cmp_mask = (CMP_STRIDE * _iota((b, nr), 1) + (CMP_LEN - 1)) <= (i * b + _iota((b, nr), 0))
        imp = jnp.zeros((b, LANES), f32)
        stats = jnp.zeros((b, LANES), f32)
        for h in range(H):
            zc = jnp.where(cmp_mask, _mm_nt(q_ref[:, hs(h)], kvc_ref[0]) * scale, NEG_INF)
            m = jnp.max(zc, axis=1, keepdims=True)
            p = jnp.where(cmp_mask, jnp.exp(zc - m), 0.0)
            l = jnp.sum(p, axis=1, keepdims=True)
            some = l > 0.0
            lsafe = jnp.where(some, l, 1.0)
            pc = p * jnp.where(some, 1.0 / lsafe, 0.0)
            oc_ref[:, hs(h)] = _mm(pc, kvc_ref[1])
            imp = imp + _mm(pc, m2s_ref[...])
            stats = jnp.where(lane == h, jnp.where(some, m + jnp.log(lsafe), 0.0), stats)

        cur = jnp.right_shift(i * b + _iota((b, LANES), 0), int(math.log2(SEL_LEN)))
        forced = (lane == 0) | (lane == cur) | (lane == cur - 1)
        score = jnp.where(lane <= cur, jnp.where(forced, FORCED_BONUS, imp), NEG_INF)
        score = jnp.where(lane < n_sel, score, -3e38)
        rank = jnp.zeros((b, LANES), f32)
        for j in range(n_sel):
            col = score[:, j:j + 1]
            rank = rank + jnp.where(col > score, 1.0, jnp.where(col == score, jnp.where(lane > j, 1.0, 0.0), 0.0))
        sel = jnp.where(lane < n_sel, jnp.where(rank < top_n, 1.0, 0.0), 0.0)
        sel_ref[...] = sel
        sel_b = sel.astype(_MXU)

        def reset():
            m_sc[...] = jnp.full(m_sc.shape, NEG_INF, f32)
            l_sc[...] = jnp.zeros_like(l_sc)
            acc_sc[...] = jnp.zeros_like(acc_sc)

        def update(h, z, mask, vch):
            zm = jnp.where(mask, z, NEG_INF)
            m_old = m_sc[h]
            m_new = jnp.maximum(m_old, jnp.max(zm, axis=1, keepdims=True))
            p = jnp.where(mask, jnp.exp(zm - m_new), 0.0)
            alpha = jnp.exp(m_old - m_new)
            l_sc[h] = alpha * l_sc[h] + jnp.sum(p, axis=1, keepdims=True)
            acc_sc[h] = alpha * acc_sc[h] + _mm(p, vch)
            m_sc[h] = m_new

        def finish(out_ref, branch, stats):
            for h in range(H):
                out_ref[:, hs(h)] = acc_sc[h] / l_sc[h]
                stats = jnp.where(lane == 4 * branch + h, m_sc[h] + jnp.log(l_sc[h]), stats)
            return stats

        def sel_chunk(c, diag):
            st = pl.multiple_of(c * b, b)
            mask = _mm(sel_b, e3_ref[c]) > 0.5
            if diag:
                mask = mask & _lower_mask(b, False)
            kch, vch = ks_ref[pl.ds(st, b), :], vs_ref[pl.ds(st, b), :]
            for h in range(H):
                update(h, _mm_nt(q_ref[:, hs(h)], kch) * scale, mask, vch)

        reset()

        def sel_loop(c, carry):
            sel_chunk(c, False)
            return carry

        lax.fori_loop(0, i, sel_loop, 0)
        sel_chunk(i, True)
        stats = finish(os_ref, 1, stats)

        reset()
        for d in range(nd, -1, -1):
            @pl.when(i >= d)
            def _():
                st = pl.multiple_of((i - d) * b, b)
                mask = _nsa_masks(i, b, d)
                kch, vch = kw_ref[pl.ds(st, b), :], vw_ref[pl.ds(st, b), :]
                for h in range(H):
                    update(h, _mm_nt(q_ref[:, hs(h)], kch) * scale, mask, vch)
        stats = finish(ow_ref, 2, stats)
        st_ref[...] = stats

        g = _sigmoid(br_ref[...])
        for h in range(H):
            o_ref[:, hs(h)] = (g[:, 3 * h:3 * h + 1] * oc_ref[:, hs(h)] + g[:, 3 * h + 1:3 * h + 2] * os_ref[:, hs(h)]
                               + g[:, 3 * h + 2:3 * h + 3] * ow_ref[:, hs(h)])

    blk = lambda w: pl.BlockSpec((b, w), lambda i: (i, 0))
    whole = lambda a: pl.BlockSpec(a.shape, lambda i: (0,) * a.ndim)
    return pl.pallas_call(
        body, out_shape=tuple(_SDS((s, GROUP), f32) for _ in range(4)) + (_SDS((s, LANES), f32), _SDS((s, LANES), f32)),
        grid=(nq,),
        in_specs=[blk(GROUP), whole(kvc), whole(ksr), whole(vs), whole(kwr), whole(vw),
                  pl.BlockSpec((b, LANES), lambda i: (i, bcol)), whole(m2s), whole(e3)],
        out_specs=(blk(GROUP),) * 4 + (blk(LANES), blk(LANES)),
        scratch_shapes=[pltpu.VMEM((H, b, 1), f32), pltpu.VMEM((H, b, 1), f32), pltpu.VMEM((H, b, HEAD_DIM), f32)],
        compiler_params=_cp(("parallel",)), name=name,
    )(qr, kvc, ksr, vs, kwr, vw, z_all, m2s, e3)


def _nsa_bwd(do, qr, kvc, ksr, vs, kwr, vw, z_all, oc, os_, ow, stats, sel, e3, *, name):
    s = qr.shape[0]
    b = _attn_block(s)
    nq = s // b
    nr = kvc.shape[1]
    nd = -(-WINDOW // b)
    scale = HEAD_DIM ** -0.5
    bcol = _AL["nsa_branch"] // LANES
    H = N_HEADS

    def body(do_ref, q_ref, kvc_ref, ks_ref, vs_ref, kw_ref, vw_ref, br_ref, oc_ref, os_ref, ow_ref, st_ref, sel_ref,
             e3_ref, dq_ref, dbr_ref, dkvc_ref, dks_ref, dvs_ref, dkw_ref, dvw_ref, dob_sc, delta_sc, dq_sc, kvt_sc):
        i = pl.program_id(0)

        @pl.when(i == 0)
        def _():
            dkvc_ref[...] = jnp.zeros_like(dkvc_ref)
            kvt_sc[...] = jnp.zeros_like(kvt_sc)

        lane = _iota((b, LANES), 1)
        hs = lambda h: slice(h * HEAD_DIM, (h + 1) * HEAD_DIM)
        g = _sigmoid(br_ref[...])
        stats = st_ref[...]
        dbr = jnp.zeros((b, LANES), f32)
        outs = (oc_ref, os_ref, ow_ref)
        for h in range(H):
            doh = do_ref[:, hs(h)]
            for j in range(3):
                gj = g[:, 3 * h + j:3 * h + j + 1]
                dgj = jnp.sum(doh * outs[j][:, hs(h)], axis=1, keepdims=True)
                dbr = jnp.where(lane == 3 * h + j, dgj * gj * (1.0 - gj), dbr)
                dob_sc[j, :, hs(h)] = gj * doh
                delta_sc[j, h] = gj * dgj
        dbr_ref[...] = dbr
        dq_sc[...] = jnp.zeros_like(dq_sc)

        qts = [q_ref[:, hs(h)].T.astype(_MXU) for h in range(H)]
        dobts = {(j, h): dob_sc[j, :, hs(h)].T.astype(_MXU) for j in (1, 2) for h in range(H)}

        def branch(j, h, z, mask, kch, vch):
            p = jnp.where(mask, jnp.exp(jnp.where(mask, z, NEG_INF) - stats[:, 4 * j + h:4 * j + h + 1]), 0.0)
            dob = dob_sc[j, :, hs(h)]
            ds = p * (_mm_nt(dob, vch) - delta_sc[j, h])
            dq_sc[:, hs(h)] += _mm(ds, kch) * scale
            if j == 0:
                return _mm_tn(ds, q_ref[:, hs(h)]) * scale, _mm_tn(p, dob)
            return _mm(qts[h], ds), _mm(dobts[j, h], p)

        cmp_mask = (CMP_STRIDE * _iota((b, nr), 1) + (CMP_LEN - 1)) <= (i * b + _iota((b, nr), 0))
        kc, vc = kvc_ref[0], kvc_ref[1]
        for h in range(H):
            dk, dv = branch(0, h, _mm_nt(q_ref[:, hs(h)], kc) * scale, cmp_mask, kc, vc)
            dkvc_ref[0] += dk
            dkvc_ref[1] += dv

        sel_b = sel_ref[...].astype(_MXU)

        def chunk(j, c, mask, k_ref, v_ref):
            st = pl.multiple_of(c * b, b)
            kch, vch = k_ref[pl.ds(st, b), :], v_ref[pl.ds(st, b), :]
            dk = jnp.zeros((HEAD_DIM, b), f32)
            dv = jnp.zeros((HEAD_DIM, b), f32)
            for h in range(H):
                dkh, dvh = branch(j, h, _mm_nt(q_ref[:, hs(h)], kch) * scale, mask, kch, vch)
                dk, dv = dk + dkh, dv + dvh
            kvt_sc[2 * j - 2, c] += dk
            kvt_sc[2 * j - 1, c] += dv

        def sel_chunk(c, diag):
            mask = _mm(sel_b, e3_ref[c]) > 0.5
            if diag:
                mask = mask & _lower_mask(b, False)
            chunk(1, c, mask, ks_ref, vs_ref)

        def sel_loop(c, carry):
            sel_chunk(c, False)
            return carry

        lax.fori_loop(0, i, sel_loop, 0)
        sel_chunk(i, True)

        for d in range(nd, -1, -1):
            @pl.when(i >= d)
            def _():
                chunk(2, i - d, _nsa_masks(i, b, d), kw_ref, vw_ref)

        dq_ref[...] = dq_sc[...]

        @pl.when(i == nq - 1)
        def _():
            for c in range(nq):
                rows = slice(c * b, (c + 1) * b)
                dks_ref[rows, :] = kvt_sc[0, c].T * scale
                dvs_ref[rows, :] = kvt_sc[1, c].T
                dkw_ref[rows, :] = kvt_sc[2, c].T * scale
                dvw_ref[rows, :] = kvt_sc[3, c].T

    blk = lambda w: pl.BlockSpec((b, w), lambda i: (i, 0))
    whole = lambda a: pl.BlockSpec(a.shape, lambda i: (0,) * a.ndim)
    stream = _SDS((s, HEAD_DIM), f32)
    return pl.pallas_call(
        body, out_shape=(_SDS((s, GROUP), f32), _SDS((s, LANES), f32), _SDS(kvc.shape, f32), stream, stream, stream,
                         stream),
        grid=(nq,),
        in_specs=[blk(GROUP), blk(GROUP), whole(kvc), whole(ksr), whole(vs), whole(kwr), whole(vw),
                  pl.BlockSpec((b, LANES), lambda i: (i, bcol)), blk(GROUP), blk(GROUP), blk(GROUP), blk(LANES),
                  blk(LANES), whole(e3)],
        out_specs=(blk(GROUP), blk(LANES), whole(kvc), whole(ksr), whole(vs), whole(kwr), whole(vw)),
        scratch_shapes=[pltpu.VMEM((3, b, GROUP), f32), pltpu.VMEM((3, H, b, 1), f32), pltpu.VMEM((b, GROUP), f32),
                        pltpu.VMEM((4, nq, HEAD_DIM, b), f32)],
        compiler_params=_cp(("arbitrary",)), name=name,
    )(do, qr, kvc, ksr, vs, kwr, vw, z_all, oc, os_, ow, stats, sel, e3)


def _seg(a, name):
    parts = [lax.slice_in_dim(a, off, off + hi - lo, axis=a.ndim - 1) for off, lo, hi in _PIECES[name]]
    return parts[0] if len(parts) == 1 else jnp.concatenate(parts, axis=a.ndim - 1)


def _to_groups(segs, rows, dtype):
    cols = []
    for s, grp in enumerate(_GROUPS):
        at = 0
        for n, lo, hi, off in sorted(grp, key=lambda t: t[3]):
            if off > at:
                cols.append(jnp.zeros((rows, off - at), dtype))
            cols.append(segs[n][:, lo:hi].astype(dtype))
            at = off + hi - lo
        if at < GROUP_W:
            cols.append(jnp.zeros((rows, GROUP_W - at), dtype))
    return jnp.concatenate(cols, axis=1)


def _piece_from_shard(w_t, s):
    grp = sorted(_GROUPS[s], key=lambda t: t[3])
    ends = [t[3] for t in grp[1:]] + [GROUP_W]
    rows = []
    for (n, lo, hi, off), end in zip(grp, ends):
        first = _ORIG[n] + lo - s * CHIP_COLS
        rows.append(jnp.pad(w_t[:, first:first + hi - lo], ((0, 0), (0, end - off - (hi - lo)), (0, 0))))
    return jnp.concatenate(rows, axis=1)


def _shard_from_piece(g, s):
    return jnp.concatenate([g[:, off:off + hi - lo] for n, lo, hi, off in
                            sorted(_GROUPS[s], key=lambda t: _ORIG[t[0]] + t[1])], axis=1)


def _from_groups(a):
    return jnp.concatenate([_seg(a, n) for n, _ in _SEGS], axis=1)


def _cmp_rows(tok):
    s = tok.shape[0]
    r = tok.reshape(s // CMP_STRIDE, CMP_STRIDE * HEAD_DIM)
    return r, jnp.concatenate([r[1:], jnp.zeros((1, r.shape[1]), r.dtype)], axis=0)


def _cmp_unrows(dxa, dxb):
    s = dxa.shape[0] * CMP_STRIDE
    return (dxa + jnp.concatenate([jnp.zeros((1, dxa.shape[1]), dxa.dtype), dxb[:-1]], axis=0)).reshape(s, HEAD_DIM)


_GATES = ("sb_gate", "nsa_gate", "fox_gate", "mla_gate")


def _layer_fwd(x, p, c, tag):
    s = x.shape[0]
    b = _attn_block(s)
    h = _rms_fwd(x, p["pre_g"], out_dtype=_MXU, name=f"prenorm_{tag}")
    z, zb = _matmul(h, p["w_in"], "nt", bias=p["b_in"], twin=True, name=f"inproj_{tag}")
    o_sb = _sb_fwd(zb, hp=HP_FWD, name=f"sb_fwd_{tag}")

    qr, ksr, kwr = _rope_call([(z, GROUP, _AL["nsa_q"] // GROUP), (z, LANES, _AL["nsa_k_sel"] // LANES),
                               (z, LANES, _AL["nsa_k_win"] // LANES)], c["tabs128"], HEAD_DIM // 2, False,
                              name=f"nsa_rope_{tag}")
    (rak, rbk), (rav, rbv) = _cmp_rows(_seg(z, "nsa_k_cmp")), _cmp_rows(_seg(z, "nsa_v_cmp"))
    ra, rb_ = jnp.stack([rak, rav]), jnp.stack([rbk, rbv])
    kvc, hp = _nsa_cmp_fwd(ra, rb_, p["cmp_pos"], p["cmp_w1"], p["cmp_w2"], c["tabs_cmp"], name=f"nsa_cmp_{tag}")
    vs, vw = _seg(z, "nsa_v_sel"), _seg(z, "nsa_v_win")
    o_nsa, oc, os_, ow, stats, sel = _nsa_fwd(qr, kvc, ksr, vs, kwr, vw, z, c["m2s"], c["e3"], name=f"nsa_fwd_{tag}")

    cum, cum_t8 = _fox_cum_fwd(z, p["fox_bias"], name=f"fox_cum_{tag}")
    cum_t = cum_t8.reshape(8, s // b, 1, b)
    fox_v = _seg(zb, "fox_v")
    fcols = (_AL["fox_q"] // HEAD_DIM, _AL["fox_k"] // HEAD_DIM, 0)
    o_fox, lse_fox = _attn_fwd(zb, zb, fox_v, *fcols, HEAD_DIM, cum, cum_t, scale=HEAD_DIM ** -0.5, hp=HP_FWD,
                               name=f"fox_fwd_{tag}")

    qcat, kcat, vm = _mla_prep_fwd(z, p["gq"], p["gkv"], p["wuq"], p["wk"], p["wv"], c["tabs64"],
                                   name=f"mla_prep_{tag}")
    o_mla, lse_mla = _attn_fwd(qcat, kcat, vm, 0, 0, 0, MLA_QW, None, None, scale=(MLA_NOPE + MLA_ROPE) ** -0.5,
                               hp=HP_BWD, name=f"mla_fwd_{tag}")

    o_all = (o_sb, o_nsa, o_fox, o_mla)
    gates = jnp.concatenate([_seg(z, n) for n in _GATES], axis=1)
    mix = _gate_fwd(o_all, gates, name=f"gate_{tag}")
    u = _matmul(mix, p["w_out"], "nn", name=f"outproj_{tag}")
    y = _postnorm_fwd(u, p["post_g"], x, name=f"postnorm_{tag}")
    saved = dict(x=x, h=h, z=z, zb=zb, qr=qr, ksr=ksr, kwr=kwr, ra=ra, rb=rb_, kvc=kvc, hp=hp, vs=vs, vw=vw, oc=oc, os=os_,
                 ow=ow, stats=stats, sel=sel, cum=cum, cum_t=cum_t, fox_v=fox_v, o_fox=o_fox, lse_fox=lse_fox, qcat=qcat, kcat=kcat,
                 vm=vm, o_mla=o_mla, lse_mla=lse_mla, o_all=o_all, gates=gates, mix=mix, u=u)
    return y, saved


def _layer_bwd(dy, sv, p, c, tag, dw_dtype=f32):
    z = sv["z"]
    s = z.shape[0]
    du, dg_post = _rms_bwd(dy, sv["u"], p["post_g"], name=f"postnorm_bwd_{tag}")
    dmix = _matmul(du, p["w_out"], "nt", name=f"outproj_dx_{tag}")
    dw_out = _matmul(sv["mix"], du, "tn", name=f"outproj_dw_{tag}")
    do_sb, do_nsa, do_fox, do_mla, dgates = _gate_bwd(dmix, sv["o_all"], sv["gates"], name=f"gate_bwd_{tag}")
    dgate = [dgates[:, k * GROUP:(k + 1) * GROUP] for k in range(4)]

    sb_dq, sb_dk, sb_dv = _sb_bwd(sv["zb"], do_sb, hp=HP_BWD, name=f"sb_bwd_{tag}")

    n_dq, n_dbr, n_dkvc, n_dks, n_dvs, n_dkw, n_dvw = _nsa_bwd(
        do_nsa, sv["qr"], sv["kvc"], sv["ksr"], sv["vs"], sv["kwr"], sv["vw"], z, sv["oc"], sv["os"], sv["ow"],
        sv["stats"], sv["sel"], c["e3"], name=f"nsa_bwd_{tag}")
    dxa, dxb, dw1, dw2 = _nsa_cmp_bwd(sv["ra"], sv["rb"], p["cmp_pos"], p["cmp_w1"], p["cmp_w2"], c["tabs_cmp"],
                                      sv["hp"], n_dkvc, name=f"nsa_cmp_bwd_{tag}")
    n_dq, n_dks, n_dkw = _rope_call([(n_dq, GROUP, 0), (n_dks, LANES, 0), (n_dkw, LANES, 0)], c["tabs128"],
                                    HEAD_DIM // 2, True, name=f"nsa_rope_bwd_{tag}")
    dpos = _colsum(jnp.concatenate([dxa[0], dxb[0], dxa[1], dxb[1]], axis=1), name=f"nsa_dpos_{tag}")
    flat = CMP_LEN * HEAD_DIM

    fcols = (_AL["fox_q"] // HEAD_DIM, _AL["fox_k"] // HEAD_DIM, 0)
    f_dq, f_dk, f_dv, f_dck = _attn_bwd(sv["zb"], sv["zb"], sv["fox_v"], *fcols, HEAD_DIM, do_fox, sv["o_fox"], sv["lse_fox"],
                                        sv["cum"], sv["cum_t"], scale=HEAD_DIM ** -0.5, hp=HP_BWD,
                                        name=f"fox_bwd_{tag}")
    dcum_t = jnp.pad(f_dck.reshape(N_HEADS, s), ((0, 8 - N_HEADS), (0, 0)))
    f_df, f_dbias = _fox_cum_bwd(z, p["fox_bias"], dcum_t, name=f"fox_cum_bwd_{tag}")

    m_dq, m_dk, m_dv = _attn_bwd(sv["qcat"], sv["kcat"], sv["vm"], 0, 0, 0, MLA_QW, do_mla, sv["o_mla"], sv["lse_mla"],
                                 None, None, scale=(MLA_NOPE + MLA_ROPE) ** -0.5, hp=HP_BWD, name=f"mla_bwd_{tag}")
    m_dcq, m_dckv, m_dkr, m_dwuq, m_dwk, m_dwv, m_dgq, m_dgkv = _mla_prep_bwd(
        z, p["gq"], p["gkv"], p["wuq"], p["wk"], p["wv"], c["tabs64"], m_dq, m_dk, m_dv, name=f"mla_prep_bwd_{tag}")

    dz = _to_groups(dict(
        sb_q=sb_dq, sb_k=sb_dk, sb_v=sb_dv, sb_gate=dgate[0], nsa_q=n_dq, nsa_k_cmp=_cmp_unrows(dxa[0], dxb[0]),
        nsa_v_cmp=_cmp_unrows(dxa[1], dxb[1]), nsa_k_sel=n_dks, nsa_v_sel=n_dvs, nsa_k_win=n_dkw, nsa_v_win=n_dvw,
        nsa_branch=n_dbr, nsa_gate=dgate[1], fox_q=f_dq, fox_k=f_dk, fox_v=f_dv, fox_f=f_df, fox_gate=dgate[2],
        mla_cq=m_dcq, mla_ckv=m_dckv, mla_k_rope=m_dkr, mla_gate=dgate[3]), s, _MXU)
    dh = _matmul(dz, p["w_in"], "nn", name=f"inproj_dx_{tag}")
    dw_in = _matmul(dz, sv["h"], "tn", out_dtype=dw_dtype, name=f"inproj_dw_{tag}")
    db = _colsum(dz, name=f"inproj_db_{tag}")
    dx, dg_pre = _rms_bwd(dh, sv["x"], p["pre_g"], res=dy, name=f"prenorm_bwd_{tag}")

    qw = MLA_NOPE + MLA_ROPE
    grads = {
        "pre_norm_g": dg_pre[0], "post_norm_g": dg_post[0], "w_in": dw_in, "b_in": _from_groups(db)[0],
        "w_out": dw_out, "fox_forget_bias": f_dbias[0, :N_HEADS],
        "nsa_cmp_pos_k": dpos[0, :flat].reshape(CMP_LEN, HEAD_DIM), "nsa_cmp_w1_k": dw1[0], "nsa_cmp_w2_k": dw2[0],
        "nsa_cmp_pos_v": dpos[0, flat:].reshape(CMP_LEN, HEAD_DIM), "nsa_cmp_w1_v": dw1[1], "nsa_cmp_w2_v": dw2[1],
        "mla_q_norm_g": m_dgq[0],
        "mla_w_uq": jnp.concatenate([m_dwuq[:, MLA_QW * h:MLA_QW * h + qw] for h in range(N_HEADS)], axis=1),
        "mla_kv_norm_g": m_dgkv[0],
        "mla_w_ukv": jnp.concatenate(sum([[m_dwk[:, LANES * h:LANES * (h + 1)], m_dwv[:, LANES * h:LANES * (h + 1)]]
                                          for h in range(N_HEADS)], []), axis=1),
    }
    return dx, grads


def _layer_params(w, l):
    b_in = w["b_in"][l].reshape(1, -1)
    b_segs = {n: b_in[:, _ORIG[n]:_ORIG[n] + wd] for n, wd in _SEGS}
    qw = MLA_NOPE + MLA_ROPE
    w_uq, w_ukv = w["mla_w_uq"][l], w["mla_w_ukv"][l]
    uq = []
    for h in range(N_HEADS):
        uq += [w_uq[:, qw * h:qw * (h + 1)], jnp.zeros((w_uq.shape[0], MLA_QW - qw), w_uq.dtype)]
    kw_ = 2 * LANES
    flat = CMP_LEN * HEAD_DIM
    return dict(
        pre_g=w["pre_norm_g"][l].reshape(1, -1), post_g=w["post_norm_g"][l].reshape(1, -1),
        w_in=w["w_in"][l], b_in=_to_groups(b_segs, 1, f32), w_out=w["w_out"][l],
        fox_bias=jnp.pad(w["fox_forget_bias"][l], (0, LANES - N_HEADS)).reshape(1, LANES),
        cmp_pos=jnp.stack([w["nsa_cmp_pos_k"][l].reshape(1, flat), w["nsa_cmp_pos_v"][l].reshape(1, flat)]),
        cmp_w1=jnp.stack([w["nsa_cmp_w1_k"][l], w["nsa_cmp_w1_v"][l]]),
        cmp_w2=jnp.stack([w["nsa_cmp_w2_k"][l], w["nsa_cmp_w2_v"][l]]),
        gq=w["mla_q_norm_g"][l].reshape(1, -1), gkv=w["mla_kv_norm_g"][l].reshape(1, -1),
        wuq=jnp.concatenate(uq, axis=1),
        wk=jnp.concatenate([w_ukv[:, kw_ * h:kw_ * h + LANES] for h in range(N_HEADS)], axis=1),
        wv=jnp.concatenate([w_ukv[:, kw_ * h + LANES:kw_ * (h + 1)] for h in range(N_HEADS)], axis=1),
    )


def _consts(s):
    pos = jnp.arange(s)
    m2s, e3 = _nsa_consts(s)
    return dict(tabs128=_rope_tables(pos, HEAD_DIM), tabs64=_rope_tables(pos, MLA_ROPE),
                tabs_cmp=_rope_tables(jnp.arange(s // CMP_STRIDE) * CMP_STRIDE + (CMP_LEN - 1), HEAD_DIM),
                m2s=m2s, e3=e3)


def _place():
    return lax.axis_index("x"), lax.axis_index("y"), lax.axis_index("c")


def _other_chips(x, y):
    return [(1 - x, y), (x, 1 - y), (1 - x, 1 - y)]


def _comm_call(body, out_shapes, n_sems, arrs, name):
    return pl.pallas_call(body, out_shape=tuple(out_shapes), in_specs=[_ANY] * len(arrs),
                          out_specs=tuple(_ANY for _ in out_shapes),
                          scratch_shapes=[pltpu.SemaphoreType.DMA((n_sems,)), pltpu.SemaphoreType.DMA((n_sems,))],
                          name=name)(*arrs)


def _gather_chips(arrs, *, name):
    n = len(arrs)

    def body(*refs):
        a_refs, out_refs, send_sems, recv_sems = refs[:n], refs[n:2 * n], refs[2 * n], refs[2 * n + 1]
        x, y, c = _place()
        me = 2 * x + y
        sibling = (x, y, 1 - c)
        chips = _other_chips(x, y)

        def copy(j, k, src, dst, to):
            return pltpu.make_async_remote_copy(src, dst, send_sems.at[6 * j + k], recv_sems.at[6 * j + k],
                                                device_id=to, device_id_type=_MESH)

        first = [copy(j, k, a_refs[j].at[c], out_refs[j].at[me, c], (px, py, c))
                 for k, (px, py) in enumerate(chips) for j in range(n)]
        for cp in first:
            cp.start()
        passed = []
        for k, (px, py) in enumerate(chips):
            for j in range(n):
                landed = out_refs[j].at[2 * px + py, c]
                copy(j, k, a_refs[j].at[c], landed, (px, py, c)).wait_recv()
                passed.append(copy(j, 3 + k, landed, landed, sibling))
                passed[-1].start()
        for k, (px, py) in enumerate(chips):
            for j in range(n):
                copy(j, 3 + k, a_refs[j].at[c], out_refs[j].at[2 * px + py, 1 - c], sibling).wait_recv()
        for cp in first + passed:
            cp.wait_send()

    return _comm_call(body, [_SDS((N_CHIPS,) + a.shape, a.dtype) for a in arrs], 6 * n, arrs, name)


def _alltoall_chips(arrs, modes, *, name):
    n = len(arrs)

    def body(*refs):
        g_refs, out_refs, send_sems, recv_sems = refs[:n], refs[n:2 * n], refs[2 * n], refs[2 * n + 1]
        x, y, c = _place()
        me = 2 * x + y

        def copy(j, s):
            return pltpu.make_async_remote_copy(_slot_ref(g_refs[j], modes[j], s), out_refs[j].at[me],
                                                send_sems.at[N_CHIPS * j + s], recv_sems.at[N_CHIPS * j + me],
                                                device_id=(s // 2, s % 2, c), device_id_type=_MESH)

        for s in range(N_CHIPS):
            @pl.when(s != me)
            def _():
                for j in range(n):
                    copy(j, s).start()
        for t in range(N_CHIPS):
            @pl.when(t != me)
            def _():
                for j in range(n):
                    pltpu.make_async_remote_copy(_slot_ref(g_refs[j], modes[j], t), out_refs[j].at[t],
                                                 send_sems.at[N_CHIPS * j + t], recv_sems.at[N_CHIPS * j + t],
                                                 device_id=(t // 2, t % 2, c), device_id_type=_MESH).wait_recv()
        for s in range(N_CHIPS):
            @pl.when(s != me)
            def _():
                for j in range(n):
                    copy(j, s).wait_send()

    outs = [_SDS((N_CHIPS,) + _slot_shape(a, m), a.dtype) for a, m in zip(arrs, modes)]
    return _comm_call(body, outs, N_CHIPS * n, arrs, name)


def _swap_other_half(arrs, *, name):
    n = len(arrs)

    def body(*refs):
        g_refs, out_refs, send_sems, recv_sems = refs[:n], refs[n:2 * n], refs[2 * n], refs[2 * n + 1]
        x, y, c = _place()
        cps = [pltpu.make_async_remote_copy(g_refs[j].at[:, 1 - c], out_refs[j], send_sems.at[j], recv_sems.at[j],
                                            device_id=(x, y, 1 - c), device_id_type=_MESH) for j in range(n)]
        for cp in cps:
            cp.start()
        for cp in cps:
            cp.wait()

    return _comm_call(body, [_SDS((a.shape[0],) + a.shape[2:], a.dtype) for a in arrs], n, arrs, name)


def _swap_sibling(arrs, *, name):
    n = len(arrs)

    def body(*refs):
        f_refs, out_refs, send_sems, recv_sems = refs[:n], refs[n:2 * n], refs[2 * n], refs[2 * n + 1]
        x, y, c = _place()
        cps = [pltpu.make_async_remote_copy(f_refs[j], out_refs[j], send_sems.at[j], recv_sems.at[j],
                                            device_id=(x, y, 1 - c), device_id_type=_MESH) for j in range(n)]
        for cp in cps:
            cp.start()
        for cp in cps:
            cp.wait()

    return _comm_call(body, [_SDS(a.shape, a.dtype) for a in arrs], n, arrs, name)


_HBM = pl.BlockSpec(memory_space=pltpu.HBM)
_SEM = pl.BlockSpec(memory_space=pltpu.SEMAPHORE)
_EFFECT = pltpu.SideEffectType.DATAFLOW_SIDE_EFFECTING


def _slot_ref(ref, mode, s):
    return ref if mode == "same" else ref.at[s]


def _slot_shape(a, mode):
    return a.shape if mode == "same" else a.shape[1:]


def _send_start(arrs, modes, after, *, name):
    n = len(arrs)
    lands = [lax.empty((N_CHIPS,) + _slot_shape(a, m), a.dtype) for a, m in zip(arrs, modes)]

    def body(*refs):
        srcs, land_refs, send_sems, recv_sems, token = refs[:n], refs[n:2 * n], refs[2 * n + 1], refs[2 * n + 2], refs[-1]
        x, y, c = _place()
        me = 2 * x + y
        for s in range(N_CHIPS):
            @pl.when(s != me)
            def _():
                for j in range(n):
                    pltpu.make_async_remote_copy(_slot_ref(srcs[j], modes[j], s), land_refs[j].at[me],
                                                 send_sems.at[N_CHIPS * j + s], recv_sems.at[N_CHIPS * j + me],
                                                 device_id=(s // 2, s % 2, c), device_id_type=_MESH).start()
        token[...] = jnp.zeros_like(token)

    hbm = lambda a: pltpu.HBM(a.shape, a.dtype)
    sems = pltpu.SemaphoreType.DMA((N_CHIPS * n,))
    out = pl.pallas_call(
        body, name=name, out_shape=(sems, sems, *[hbm(a) for a in arrs], *[hbm(a) for a in lands], _SDS((8, LANES), f32)),
        in_specs=[_HBM] * (2 * n) + [_ANY], out_specs=(_SEM, _SEM, *[_HBM] * (2 * n), pl.BlockSpec(memory_space=pltpu.VMEM)),
        input_output_aliases={j: 2 + j for j in range(2 * n)},
        compiler_params=pltpu.CompilerParams(has_side_effects=_EFFECT),
    )(*[pltpu.with_memory_space_constraint(a, pltpu.HBM) for a in arrs + lands], after)
    return out[:-1], out[-1]


def _send_wait(started, modes, after, *, name):
    send_sems, recv_sems = started[0], started[1]
    n = (len(started) - 2) // 2
    thru = list(started[2:])

    def body(*refs):
        srcs, land_refs, send_sems, recv_sems = refs[:n], refs[n:2 * n], refs[2 * n], refs[2 * n + 1]
        x, y, c = _place()
        me = 2 * x + y
        for s in range(N_CHIPS):
            @pl.when(s != me)
            def _():
                for j in range(n):
                    cp = pltpu.make_async_remote_copy(_slot_ref(srcs[j], modes[j], s), land_refs[j].at[s],
                                                      send_sems.at[N_CHIPS * j + s], recv_sems.at[N_CHIPS * j + s],
                                                      device_id=(s // 2, s % 2, c), device_id_type=_MESH)
                    cp.wait_send()
                    cp.wait_recv()

    hbm = lambda a: pltpu.HBM(a.shape, a.dtype)
    out = pl.pallas_call(
        body, name=name, out_shape=tuple(hbm(a) for a in thru), in_specs=[_HBM] * (2 * n) + [_SEM, _SEM, _ANY],
        out_specs=tuple([_HBM] * (2 * n)), input_output_aliases={j: j for j in range(2 * n)},
        compiler_params=pltpu.CompilerParams(has_side_effects=_EFFECT),
    )(*thru, send_sems, recv_sems, after)
    return list(out[n:])


def _add_my_half(g, r, *, name):
    p, _, h, w = g.shape
    tw = _pick(w, (2048, 1024, 512, 256, 128))
    rb = max(d for d in range(16, h + 1, 16) if h % d == 0 and d * tw * 4 <= (2 << 20))

    def body(c_ref, g_ref, r_ref, o_ref):
        o_ref[...] = (g_ref[...].astype(f32) + r_ref[...].astype(f32)).astype(o_ref.dtype)

    blk = pl.BlockSpec((None, rb, tw), lambda s, i, j, c_ref: (s, i, j))
    grid_spec = pltpu.PrefetchScalarGridSpec(
        num_scalar_prefetch=1, grid=(p, h // rb, w // tw),
        in_specs=[pl.BlockSpec((None, None, rb, tw), lambda s, i, j, c_ref: (s, c_ref[0], i, j)), blk], out_specs=blk)
    c = lax.axis_index("c").astype(jnp.int32).reshape(1)
    return pl.pallas_call(body, out_shape=_SDS((p, h, w), _WIRE), grid_spec=grid_spec,
                          compiler_params=_cp(("parallel", "parallel", "parallel")), name=name)(c, g, r)


_WEIGHTS = ("pre_norm_g", "post_norm_g", "w_in", "b_in", "w_out", "fox_forget_bias", "nsa_cmp_pos_k", "nsa_cmp_w1_k",
            "nsa_cmp_w2_k", "nsa_cmp_pos_v", "nsa_cmp_w1_v", "nsa_cmp_w2_v", "mla_q_norm_g", "mla_w_uq",
            "mla_kv_norm_g", "mla_w_ukv")
_SHARD_AXIS = {"w_in": 2, "w_out": 1, "nsa_cmp_w1_k": 1, "nsa_cmp_w1_v": 1, "mla_w_uq": 2, "mla_w_ukv": 2}
_PACK_UNIT = 16 * LANES


def _pack(arrays, dtype):
    rows = []
    for a in arrays:
        v = a.astype(dtype).reshape(-1)
        pad = (-v.shape[0]) % _PACK_UNIT
        if pad:
            v = jnp.concatenate([v, jnp.zeros((pad,), dtype)])
        rows.append(v.reshape(-1, LANES))
    return jnp.concatenate(rows, axis=0)


def _unpack(flat, shapes):
    out, r = [], 0
    for shp in shapes:
        n = int(np.prod(shp))
        nr = -(-n // _PACK_UNIT) * (_PACK_UNIT // LANES)
        out.append(flat[r:r + nr].reshape(-1)[:n].reshape(shp))
        r += nr
    return out


def kernel(x, pre_norm_g, post_norm_g, w_in, b_in, w_out, fox_forget_bias, nsa_cmp_pos_k, nsa_cmp_w1_k, nsa_cmp_w2_k, nsa_cmp_pos_v, nsa_cmp_w1_v, nsa_cmp_w2_v, mla_q_norm_g, mla_w_uq, mla_kv_norm_g, mla_w_ukv, loss_target, m_pre_norm_g, m_post_norm_g, m_w_in, m_b_in, m_w_out, m_fox_forget_bias, m_nsa_cmp_pos_k, m_nsa_cmp_w1_k, m_nsa_cmp_w2_k, m_nsa_cmp_pos_v, m_nsa_cmp_w1_v, m_nsa_cmp_w2_v, m_mla_q_norm_g, m_mla_w_uq, m_mla_kv_norm_g, m_mla_w_ukv, v_pre_norm_g, v_post_norm_g, v_w_in, v_b_in, v_w_out, v_fox_forget_bias, v_nsa_cmp_pos_k, v_nsa_cmp_w1_k, v_nsa_cmp_w2_k, v_nsa_cmp_pos_v, v_nsa_cmp_w1_v, v_nsa_cmp_w2_v, v_mla_q_norm_g, v_mla_w_uq, v_mla_kv_norm_g, v_mla_w_ukv):
    given = dict(locals())
    local = {n: given[n] for n in _WEIGHTS}
    depth = pre_norm_g.shape[0]
    xs, target = x[0], loss_target[0]
    s = xs.shape[0]
    sharded = [n for n in _WEIGHTS if n in _SHARD_AXIS and n != "w_in"]
    small = [n for n in _WEIGHTS if n not in _SHARD_AXIS]
    chip = 2 * lax.axis_index("x") + lax.axis_index("y")
    core = lax.axis_index("c")
    own = lambda slots, mine: lax.dynamic_update_slice_in_dim(slots, mine[None], chip, axis=0)

    w_in_t = jnp.swapaxes(w_in, 1, 2).astype(_MXU)
    piece = lax.switch(chip, [functools.partial(_piece_from_shard, s=k) for k in range(N_CHIPS)], w_in_t)
    layer_shapes = [local[n].shape[1:] for n in sharded]
    flat = [_pack([local[n][l] for n in sharded], _MXU) for l in range(depth)]
    full = dict(local)
    for n in ["w_in"] + sharded:
        full[n] = []

    def add_layer(w_in_slots, flat_slots_):
        full["w_in"].append(w_in_slots)
        per_chip = [_unpack(flat_slots_[k], layer_shapes) for k in range(N_CHIPS)]
        for j, n in enumerate(sharded):
            full[n].append(jnp.concatenate([per_chip[k][j] for k in range(N_CHIPS)], axis=_SHARD_AXIS[n] - 1))

    halved = [piece[0].reshape(2, GROUP_W // 2, D_MODEL), flat[0].reshape(2, -1, LANES)]
    first_all = [own(a, b) for a, b in zip(_gather_chips(halved, name="gather_weights"), halved)]
    add_layer(first_all[0].reshape(N_CHIPS, GROUP_W, D_MODEL), first_all[1].reshape((N_CHIPS,) + flat[0].shape))
    later = [piece[l] for l in range(1, depth)] + flat[1:]
    started, token = _send_start(later, ["same"] * len(later), first_all[1], name="gather_later_start")
    full["pre_norm_g"] = pre_norm_g + token[0, 0]

    consts = _consts(s)
    params, act, saved = [], xs, []
    for l in range(depth):
        if l == 1:
            landed = [own(a, b) for a, b in zip(_send_wait(started, ["same"] * len(later), act,
                                                           name="gather_later_wait"), later)]
            for k in range(depth - 1):
                add_layer(landed[k], landed[depth - 1 + k])
        params.append(_layer_params(full, l))
        act, sv = _layer_fwd(act, params[l], consts, f"l{l}")
        saved.append(sv)
    dy, loss_parts = _loss_head(act, target, name="loss_head")

    def flat_slots(g, dtype):
        def part(n, k):
            a, ax = g[n], _SHARD_AXIS[n] - 1
            w = a.shape[ax] // N_CHIPS
            return lax.slice_in_dim(a, k * w, (k + 1) * w, axis=ax)
        return jnp.stack([_pack([part(n, k) for n in sharded], dtype) for k in range(N_CHIPS)])

    own_slot = lambda a: lax.dynamic_index_in_dim(a, chip, axis=0, keepdims=False)
    slots_of = lambda g: g["w_in"].reshape(N_CHIPS, GROUP_W, D_MODEL)

    modes = ["slots", "slots"]
    layer_grads, in_flight = [None] * depth, {}
    for l in reversed(range(depth)):
        dy, layer_grads[l] = _layer_bwd(dy, saved[l], params[l], consts, f"l{l}", _WIRE)
        if l > 0:
            wire = [slots_of(layer_grads[l]), flat_slots(layer_grads[l], _WIRE)]
            started, token = _send_start(wire, modes, dy, name=f"reduce_l{l}_start")
            in_flight[l] = (started, wire)
            params[l - 1] = dict(params[l - 1], post_g=params[l - 1]["post_g"] + token[0, 0])
    grad_x = dy[None]
    grads = {n: jnp.stack([layer_grads[l][n] for l in range(depth)]) for n in small}
    loss_row = jnp.concatenate([jnp.sum(loss_parts).reshape(1), jnp.zeros((LANES - 1,), f32)])
    small_shapes = [(LANES,)] + [grads[n].shape for n in small]
    contrib = _pack([loss_row] + [grads[n] for n in small], f32)

    halves = [slots_of(layer_grads[0]).reshape(N_CHIPS, 2, GROUP_W // 2, D_MODEL),
              flat_slots(layer_grads[0], _WIRE).reshape(N_CHIPS, 2, -1, LANES)]
    from_sibling = _swap_other_half(halves, name="reduce_pair")
    pair_sum = [_add_my_half(g, r, name=f"reduce_pair_add{j}") for j, (g, r) in enumerate(zip(halves, from_sibling))]
    from_chips = _alltoall_chips(pair_sum + [contrib], modes + ["same"], name="reduce_chips")
    my_half = [_sum_slots(own(slots, own_slot(ps)), name=f"reduce_chips_add{j}")
               for j, (slots, ps) in enumerate(zip(from_chips, pair_sum))]
    partial = []
    for l in range(1, depth):
        started, wire = in_flight[l]
        landed = _send_wait(started, modes, dy, name=f"reduce_l{l}_wait")
        partial += [_sum_slots(own(slots, own_slot(a)), name=f"reduce_l{l}_add{j}")
                    for j, (slots, a) in enumerate(zip(landed, wire))]
    partial.append(_sum_slots(own(from_chips[2], contrib), name="sum_small"))
    theirs = _swap_sibling(my_half + partial, name="reduce_share")
    first = core == 0
    whole = [jnp.concatenate([jnp.where(first, a, b), jnp.where(first, b, a)], axis=0)
             for a, b in zip(my_half, theirs[:2])]
    whole += [_add2(a[None], b[None], name=f"reduce_cores_add{j}")[0] for j, (a, b) in enumerate(zip(partial, theirs[2:]))]
    unpiece = [functools.partial(_shard_from_piece, s=k) for k in range(N_CHIPS)]
    summed = {"w_in": jnp.stack([lax.switch(chip, unpiece, whole[2 * l].T) for l in range(depth)])}
    rest = [_unpack(whole[2 * l + 1], layer_shapes) for l in range(depth)]
    for j, n in enumerate(sharded):
        summed[n] = jnp.stack([rest[l][j] for l in range(depth)])
    total = _unpack(whole[2 * depth], small_shapes)
    loss = total[0][0]
    summed.update(zip(small, total[1:]))

    deltas, new_m, new_v = {}, {}, {}
    for n in _WEIGHTS:
        deltas[n], new_m[n], new_v[n] = _adamw(local[n], summed[n], given["m_" + n], given["v_" + n], name=f"adamw_{n}")
    return (loss, grad_x, *[summed[n] for n in _WEIGHTS], *[deltas[n] for n in _WEIGHTS],
            *[new_m[n] for n in _WEIGHTS], *[new_v[n] for n in _WEIGHTS])
```

```python
import functools
import math

import numpy as np
import jax
import jax.numpy as jnp
from jax import lax
from jax.experimental import pallas as pl
from jax.experimental.pallas import tpu as pltpu

f32 = jnp.float32
bf16 = jnp.bfloat16
_MXU = jnp.bfloat16
_WIRE = jnp.bfloat16
_SDS = jax.ShapeDtypeStruct
_ANY = pl.BlockSpec(memory_space=pl.ANY)
_MESH = pl.DeviceIdType.MESH

D_MODEL = 2048
N_HEADS = 4
HEAD_DIM = 128
GROUP = 512
RMS_EPS = 1e-6
NEG_INF = -1e30
ROPE_THETA = 10000.0
CMP_LEN, CMP_STRIDE, SEL_LEN, SEL_TOPN, WINDOW = 32, 16, 64, 16, 512
FORCED_BONUS = 1e6
MLA_Q_RANK, MLA_KV_RANK, MLA_NOPE, MLA_ROPE = 384, 128, 128, 64
ADAM_LR, ADAM_B1, ADAM_B2, ADAM_EPS, ADAM_WD, ADAM_STEP = 0.001, 0.9, 0.999, 1e-08, 0.01, 10
LANES = 128
VMEM_LIMIT = 56 * 1024 * 1024
HP_FWD, HP_BWD = 2, 2

_SEGS = (
    ("sb_q", 512), ("sb_k", 512), ("sb_v", 512), ("sb_gate", 512), ("nsa_q", 512), ("nsa_k_cmp", 128),
    ("nsa_v_cmp", 128), ("nsa_k_sel", 128), ("nsa_v_sel", 128), ("nsa_k_win", 128), ("nsa_v_win", 128),
    ("nsa_branch", 12), ("nsa_gate", 512), ("fox_q", 512), ("fox_k", 512), ("fox_v", 512), ("fox_f", 4),
    ("fox_gate", 512), ("mla_cq", 384), ("mla_ckv", 128), ("mla_k_rope", 64), ("mla_gate", 512),
)
_ORIG, _WID = {}, {}
_o = 0
for _n, _w in _SEGS:
    _ORIG[_n], _WID[_n] = _o, _w
    _o += _w
IN_WIDTH = _o
N_CHIPS = 4
CHIP_COLS = IN_WIDTH // N_CHIPS
GROUP_W = 2048
ZW = N_CHIPS * GROUP_W
_GROUPS = (
    (("sb_q", 0, 512, 0), ("sb_k", 0, 512, 512), ("sb_v", 0, 512, 1024), ("sb_gate", 0, 212, 1536)),
    (("nsa_q", 0, 512, 0), ("nsa_k_cmp", 0, 128, 512), ("nsa_v_cmp", 0, 128, 640), ("nsa_k_sel", 0, 128, 768),
     ("nsa_v_sel", 0, 128, 896), ("nsa_k_win", 0, 128, 1024), ("nsa_v_win", 0, 128, 1152), ("nsa_branch", 0, 12, 1280),
     ("sb_gate", 212, 512, 1408), ("nsa_gate", 0, 156, 1712)),
    (("fox_q", 0, 512, 0), ("fox_k", 0, 512, 512), ("fox_v", 0, 368, 1024), ("nsa_gate", 156, 512, 1408)),
    (("mla_cq", 0, 384, 0), ("mla_ckv", 0, 128, 384), ("mla_k_rope", 0, 64, 512), ("fox_f", 0, 4, 640),
     ("fox_v", 368, 512, 768), ("fox_gate", 0, 512, 1024), ("mla_gate", 0, 512, 1536)),
)
_PIECES = {n: [] for n, _ in _SEGS}
for _s, _grp in enumerate(_GROUPS):
    _cover = sorted((_ORIG[n] + lo, _ORIG[n] + hi) for n, lo, hi, _ in _grp)
    assert _cover[0][0] == _s * CHIP_COLS and _cover[-1][1] == (_s + 1) * CHIP_COLS
    assert all(a[1] == b[0] for a, b in zip(_cover, _cover[1:]))
    _ends = sorted((off, off + hi - lo) for _, lo, hi, off in _grp)
    assert all(a[1] <= b[0] for a, b in zip(_ends, _ends[1:])) and _ends[-1][1] <= GROUP_W
    assert _ends[0][0] == 0 and all(e[0] % 16 == 0 for e in _ends)
    for _n, _lo, _hi, _off in _grp:
        _PIECES[_n].append((_s * GROUP_W + _off, _lo, _hi))
_AL = {n: p[0][0] for n, p in _PIECES.items() if len(p) == 1}


def _cp(sem=None):
    return pltpu.CompilerParams(dimension_semantics=sem, vmem_limit_bytes=VMEM_LIMIT)


def _mm(a, b):
    return jnp.dot(a.astype(_MXU), b.astype(_MXU), preferred_element_type=f32)


def _mm_nt(a, b):
    return lax.dot_general(a.astype(_MXU), b.astype(_MXU), (((1,), (1,)), ((), ())), preferred_element_type=f32)


def _mm_tn(a, b):
    return lax.dot_general(a.astype(_MXU), b.astype(_MXU), (((0,), (0,)), ((), ())), preferred_element_type=f32)


def _mm_split(x, t):
    hi = x.astype(_MXU)
    lo = (x - hi.astype(f32)).astype(_MXU)
    return jnp.dot(hi, t, preferred_element_type=f32) + jnp.dot(lo, t, preferred_element_type=f32)


def _sigmoid(x):
    return 1.0 / (1.0 + jnp.exp(-x))


def _iota(shape, dim):
    return lax.broadcasted_iota(jnp.int32, shape, dim)


def _pick(n, prefs):
    for p in prefs:
        if n % p == 0:
            return p
    return n


def _matmul(a, b, mode, *, bias=None, out_dtype=f32, name):
    grouped = b.ndim == 3
    b_shape = (b.shape[0] * b.shape[1], b.shape[2]) if grouped else b.shape
    if mode == "nn":
        (M, K), (K2, N) = a.shape, b_shape
    elif mode == "nt":
        (M, K), (N, K2) = a.shape, b_shape
    else:
        (K, M), (K2, N) = a.shape, b_shape
    assert K == K2
    tm = _pick(M, (1024, 512, 384, 256, 128))
    tn = _pick(N, (1024, 512, 384, 256, 128))
    tk = K if K <= 2048 else _pick(K, (2048, 2432, 1024, 512))
    nk = K // tk
    a_spec = {"nn": pl.BlockSpec((tm, tk), lambda i, j, k: (i, k)),
              "nt": pl.BlockSpec((tm, tk), lambda i, j, k: (i, k)),
              "tn": pl.BlockSpec((tk, tm), lambda i, j, k: (k, i))}[mode]
    if not grouped:
        b_spec = {"nn": pl.BlockSpec((tk, tn), lambda i, j, k: (k, j)),
                  "nt": pl.BlockSpec((tn, tk), lambda i, j, k: (j, k)),
                  "tn": pl.BlockSpec((tk, tn), lambda i, j, k: (k, j))}[mode]
    elif mode == "nt":
        per = b.shape[1] // tn
        b_spec = pl.BlockSpec((None, tn, tk), lambda i, j, k: (j // per, j % per, k))
    else:
        assert mode == "nn"
        per = b.shape[1] // tk
        b_spec = pl.BlockSpec((None, tk, tn), lambda i, j, k: (k // per, k % per, j))
    dot = {"nn": _mm, "nt": _mm_nt, "tn": _mm_tn}[mode]
    has_bias = bias is not None

    def body(*refs):
        if has_bias:
            a_ref, b_ref, bias_ref, o_ref, acc_ref = refs
        else:
            a_ref, b_ref, o_ref, acc_ref = refs
            bias_ref = None
        k = pl.program_id(2)
        part = dot(a_ref[...], b_ref[...])

        def finish(total):
            if has_bias:
                total = total + bias_ref[...]
            o_ref[...] = total.astype(o_ref.dtype)

        if nk == 1:
            finish(part)
        else:
            @pl.when(k == 0)
            def _():
                acc_ref[...] = part

            @pl.when(k > 0)
            def _():
                acc_ref[...] += part

            @pl.when(k == nk - 1)
            def _():
                finish(acc_ref[...])

    in_specs = [a_spec, b_spec]
    args = [a, b]
    if has_bias:
        in_specs.append(pl.BlockSpec((1, tn), lambda i, j, k: (0, j)))
        args.append(bias.reshape(1, N))
    return pl.pallas_call(
        body, out_shape=_SDS((M, N), out_dtype), grid=(M // tm, N // tn, nk),
        in_specs=in_specs, out_specs=pl.BlockSpec((tm, tn), lambda i, j, k: (i, j)),
        scratch_shapes=[pltpu.VMEM((tm, tn), f32)],
        compiler_params=_cp(("parallel", "parallel", "arbitrary")), name=name,
    )(*args)


def _row_block(s):
    return _pick(s, (512, 256, 128))


def _rms_fwd(x, g, *, out_dtype, name):
    s, d = x.shape
    rb = _row_block(s)

    def body(x_ref, g_ref, o_ref):
        xv = x_ref[...]
        r = lax.rsqrt(jnp.mean(xv * xv, axis=-1, keepdims=True) + RMS_EPS)
        o_ref[...] = (xv * r * g_ref[...]).astype(o_ref.dtype)

    return pl.pallas_call(
        body, out_shape=_SDS((s, d), out_dtype), grid=(s // rb,),
        in_specs=[pl.BlockSpec((rb, d), lambda i: (i, 0)), pl.BlockSpec((1, d), lambda i: (0, 0))],
        out_specs=pl.BlockSpec((rb, d), lambda i: (i, 0)), compiler_params=_cp(("parallel",)), name=name,
    )(x, g.reshape(1, d))


def _postnorm_fwd(u, g, x, *, name):
    s, d = u.shape
    rb = _row_block(s)

    def body(u_ref, g_ref, x_ref, o_ref):
        uv = u_ref[...]
        r = lax.rsqrt(jnp.mean(uv * uv, axis=-1, keepdims=True) + RMS_EPS)
        o_ref[...] = x_ref[...] + uv * r * g_ref[...]

    return pl.pallas_call(
        body, out_shape=_SDS((s, d), f32), grid=(s // rb,),
        in_specs=[pl.BlockSpec((rb, d), lambda i: (i, 0)), pl.BlockSpec((1, d), lambda i: (0, 0)),
                  pl.BlockSpec((rb, d), lambda i: (i, 0))],
        out_specs=pl.BlockSpec((rb, d), lambda i: (i, 0)), compiler_params=_cp(("parallel",)), name=name,
    )(u, g.reshape(1, d), x)


def _fold_rows(v):
    r = v.shape[0]
    acc = v[0:8]
    for k in range(1, r // 8):
        acc = acc + v[8 * k:8 * k + 8]
    return acc


def _rms_bwd(dy, x, g, res=None, *, out_dtype=f32, name):
    s, d = x.shape
    rb = _row_block(s)
    nb = s // rb
    has_res = res is not None

    def body(*refs):
        if has_res:
            dy_ref, x_ref, g_ref, res_ref, dx_ref, dg_ref, acc_ref = refs
        else:
            dy_ref, x_ref, g_ref, dx_ref, dg_ref, acc_ref = refs
        i = pl.program_id(0)
        xv = x_ref[...]
        r = lax.rsqrt(jnp.mean(xv * xv, axis=-1, keepdims=True) + RMS_EPS)
        xh = xv * r
        dyv = dy_ref[...]
        dxh = dyv * g_ref[...]
        dx = r * (dxh - xh * jnp.mean(dxh * xh, axis=-1, keepdims=True))
        if has_res:
            dx = dx + res_ref[...]
        dx_ref[...] = dx.astype(dx_ref.dtype)
        part = _fold_rows(dyv * xh)

        @pl.when(i == 0)
        def _():
            acc_ref[...] = part

        @pl.when(i > 0)
        def _():
            acc_ref[...] += part

        @pl.when(i == nb - 1)
        def _():
            dg_ref[...] = jnp.sum(acc_ref[...], axis=0, keepdims=True)

    blk = pl.BlockSpec((rb, d), lambda i: (i, 0))
    in_specs = [blk, blk, pl.BlockSpec((1, d), lambda i: (0, 0))] + ([blk] if has_res else [])
    args = [dy, x, g.reshape(1, d)] + ([res] if has_res else [])
    return pl.pallas_call(
        body, out_shape=(_SDS((s, d), out_dtype), _SDS((1, d), f32)), grid=(nb,), in_specs=in_specs,
        out_specs=(blk, pl.BlockSpec((1, d), lambda i: (0, 0))),
        scratch_shapes=[pltpu.VMEM((8, d), f32)], compiler_params=_cp(("arbitrary",)), name=name,
    )(*args)


def _loss_head(y, target, *, name):
    s, d = y.shape
    rb = _row_block(s)
    nb = s // rb

    def body(y_ref, t_ref, dy_ref, l_ref):
        i = pl.program_id(0)
        e = y_ref[...] - t_ref[...]
        dy_ref[...] = e * (1.0 / d)
        rows = _fold_rows(e * e)
        part = rows[:, 0:LANES]
        for k in range(1, d // LANES):
            part = part + rows[:, k * LANES:(k + 1) * LANES]
        part = part * (0.5 / d)

        @pl.when(i == 0)
        def _():
            l_ref[...] = part

        @pl.when(i > 0)
        def _():
            l_ref[...] += part

    blk = pl.BlockSpec((rb, d), lambda i: (i, 0))
    return pl.pallas_call(
        body, out_shape=(_SDS((s, d), f32), _SDS((8, LANES), f32)), grid=(nb,), in_specs=[blk, blk],
        out_specs=(blk, pl.BlockSpec((8, LANES), lambda i: (0, 0))),
        compiler_params=_cp(("arbitrary",)), name=name,
    )(y, target)


def _colsum(a, *, name):
    s, n = a.shape
    rb = _row_block(s)
    nb = s // rb
    tn = _pick(n, (2432, 2048, 1024, 512, 384, 128))

    def body(a_ref, o_ref, acc_ref):
        i = pl.program_id(1)
        part = _fold_rows(a_ref[...].astype(f32))

        @pl.when(i == 0)
        def _():
            acc_ref[...] = part

        @pl.when(i > 0)
        def _():
            acc_ref[...] += part

        @pl.when(i == nb - 1)
        def _():
            o_ref[...] = jnp.sum(acc_ref[...], axis=0, keepdims=True)

    return pl.pallas_call(
        body, out_shape=_SDS((1, n), f32), grid=(n // tn, nb),
        in_specs=[pl.BlockSpec((rb, tn), lambda j, i: (i, j))], out_specs=pl.BlockSpec((1, tn), lambda j, i: (0, j)),
        scratch_shapes=[pltpu.VMEM((8, tn), f32)], compiler_params=_cp(("parallel", "arbitrary")), name=name,
    )(a)


def _gate_fwd(outs, gate, *, name):
    s, d = gate.shape
    rb = _row_block(s)
    n = len(outs)
    w = d // n

    def body(*refs):
        g_ref, m_ref = refs[n], refs[n + 1]
        for k in range(n):
            gv = g_ref[:, k * w:(k + 1) * w]
            m_ref[:, k * w:(k + 1) * w] = (refs[k][...] * (gv * _sigmoid(gv))).astype(m_ref.dtype)

    blk = pl.BlockSpec((rb, d), lambda i: (i, 0))
    part = pl.BlockSpec((rb, w), lambda i: (i, 0))
    return pl.pallas_call(body, out_shape=_SDS((s, d), _MXU), grid=(s // rb,), in_specs=[part] * n + [blk],
                          out_specs=blk, compiler_params=_cp(("parallel",)), name=name)(*outs, gate)


def _gate_bwd(dmix, outs, gate, *, name):
    s, d = gate.shape
    rb = _row_block(s)
    n = len(outs)
    w = d // n

    def body(*refs):
        dm_ref, o_refs, g_ref, do_refs, dg_ref = refs[0], refs[1:1 + n], refs[1 + n], refs[2 + n:2 + 2 * n], refs[-1]
        for k in range(n):
            sl = slice(k * w, (k + 1) * w)
            gv = g_ref[:, sl]
            sg = _sigmoid(gv)
            dm = dm_ref[:, sl]
            do_refs[k][...] = dm * (gv * sg)
            dg_ref[:, sl] = dm * o_refs[k][...] * (sg * (1.0 + gv * (1.0 - sg)))

    blk = pl.BlockSpec((rb, d), lambda i: (i, 0))
    part = pl.BlockSpec((rb, w), lambda i: (i, 0))
    return pl.pallas_call(body, out_shape=tuple(_SDS((s, w), f32) for _ in range(n)) + (_SDS((s, d), f32),),
                          grid=(s // rb,), in_specs=[blk] + [part] * n + [blk], out_specs=(part,) * n + (blk,),
                          compiler_params=_cp(("parallel",)), name=name)(dmix, *outs, gate)


def _adamw(w, g, m, v, *, name):
    shape = w.shape
    cols = shape[-1]
    rows = int(np.prod(shape[:-1])) if len(shape) > 1 else 1
    to2 = lambda t: t.reshape(rows, cols)
    rb = rows
    if rows * cols * 4 > (1 << 20):
        rb = max(d for d in range(8, rows + 1, 8) if rows % d == 0 and (d * cols * 4 <= (1600 << 10) or d == 8))

    def body(w_ref, g_ref, m_ref, v_ref, d_ref, nm_ref, nv_ref):
        gv = g_ref[...]
        mn = ADAM_B1 * m_ref[...] + (1.0 - ADAM_B1) * gv
        vn = ADAM_B2 * v_ref[...] + (1.0 - ADAM_B2) * (gv * gv)
        m_hat = mn / (1.0 - ADAM_B1 ** ADAM_STEP)
        v_hat = vn / (1.0 - ADAM_B2 ** ADAM_STEP)
        d_ref[...] = -ADAM_LR * (m_hat / (jnp.sqrt(v_hat) + ADAM_EPS) + ADAM_WD * w_ref[...])
        nm_ref[...] = mn
        nv_ref[...] = vn

    blk = pl.BlockSpec((rb, cols), lambda i: (i, 0))
    out = pl.pallas_call(body, out_shape=tuple(_SDS((rows, cols), f32) for _ in range(3)), grid=(rows // rb,),
                         in_specs=[blk] * 4, out_specs=(blk,) * 3, compiler_params=_cp(("parallel",)),
                         name=name)(to2(w), to2(g), to2(m), to2(v))
    return tuple(t.reshape(shape) for t in out)


def _sum_slots(a, *, name):
    p, n, c = a.shape
    rb = max(d for d in range(8, n + 1, 8) if n % d == 0 and (p * d * c * 4 <= (6 << 20) or d == 8))

    def body(a_ref, o_ref):
        acc = a_ref[0].astype(f32)
        for k in range(1, p):
            acc = acc + a_ref[k].astype(f32)
        o_ref[...] = acc

    return pl.pallas_call(body, out_shape=_SDS((n, c), f32), grid=(n // rb,),
                          in_specs=[pl.BlockSpec((p, rb, c), lambda i: (0, i, 0))],
                          out_specs=pl.BlockSpec((rb, c), lambda i: (i, 0)), compiler_params=_cp(("parallel",)),
                          name=name)(a)


def _add2(a, b, *, name):
    p, n, c = a.shape
    rb = max(d for d in range(8, n + 1, 8) if n % d == 0 and (d * c * 4 <= (2 << 20) or d == 8))

    def body(a_ref, b_ref, o_ref):
        o_ref[...] = a_ref[...] + b_ref[...]

    blk = pl.BlockSpec((1, rb, c), lambda s, i: (s, i, 0))
    return pl.pallas_call(body, out_shape=_SDS((p, n, c), f32), grid=(p, n // rb), in_specs=[blk, blk], out_specs=blk,
                          compiler_params=_cp(("parallel", "parallel")), name=name)(a, b)


def _rope_tables(pos, dim):
    half = dim // 2
    inv = ROPE_THETA ** (-jnp.arange(half, dtype=f32) / half)
    ang = pos.astype(f32)[:, None] * inv[None, :]
    c, s = jnp.cos(ang), jnp.sin(ang)
    z = jnp.zeros_like(c)
    pad = [jnp.zeros((pos.shape[0], LANES - dim), f32)] if dim < LANES else []
    return (jnp.concatenate([c, c] + pad, axis=1), jnp.concatenate([-s, z] + pad, axis=1),
            jnp.concatenate([z, s] + pad, axis=1))


def _rope(x, cos, sa, sb, half, transpose=False):
    if transpose:
        return x * cos + pltpu.roll(x * sa, half, 1) + pltpu.roll(x * sb, LANES - half, 1)
    return x * cos + pltpu.roll(x, LANES - half, 1) * sa + pltpu.roll(x, half, 1) * sb


def _rope_call(items, tables, half, transpose, *, name):
    s = items[0][0].shape[0]
    rb = _row_block(s)
    n = len(items)

    def body(*refs):
        cos, sa, sb = refs[n][...], refs[n + 1][...], refs[n + 2][...]
        for k in range(n):
            x_ref, o_ref = refs[k], refs[n + 3 + k]
            for j in range(items[k][1] // LANES):
                sl = slice(j * LANES, (j + 1) * LANES)
                o_ref[:, sl] = _rope(x_ref[:, sl], cos, sa, sb, half, transpose)

    in_specs = [pl.BlockSpec((rb, w), functools.partial(lambda i, cb: (i, cb), cb=cb)) for _, w, cb in items]
    in_specs += [pl.BlockSpec((rb, LANES), lambda i: (i, 0))] * 3
    out_specs = tuple(pl.BlockSpec((rb, w), lambda i: (i, 0)) for _, w, _ in items)
    return pl.pallas_call(
        body, out_shape=tuple(_SDS((s, w), f32) for _, w, _ in items), grid=(s // rb,), in_specs=in_specs,
        out_specs=out_specs, compiler_params=_cp(("parallel",)), name=name,
    )(*[a for a, _, _ in items], *tables)


def _attn_block(s):
    return _pick(s, (512, 256, 128))


def _lower_mask(b, strict):
    r, c = _iota((b, b), 0), _iota((b, b), 1)
    return (c < r) if strict else (c <= r)


def _pick_lane(block, h):
    return jnp.sum(jnp.where(_iota(block.shape, 1) == h, block, 0.0), axis=1, keepdims=True)


def _head_bias(cum_blk, g, j, hp):
    if hp == N_HEADS:
        return cum_blk[:, j:j + 1]
    return _pick_lane(cum_blk, g * hp + j)


def _attn_fwd(q, k, v, qcol, kcol, vcol, dq, cum, cum_t, *, scale, hp, name):
    s = q.shape[0]
    b = _attn_block(s)
    nq = s // b
    has_bias = cum is not None
    assert qcol % hp == 0 and kcol % hp == 0 and vcol % hp == 0

    def body(*refs):
        if has_bias:
            q_ref, k_ref, v_ref, cum_ref, cumt_ref, o_ref, lse_ref = refs
        else:
            q_ref, k_ref, v_ref, o_ref, lse_ref = refs
        g, i = pl.program_id(0), pl.program_id(1)
        qs = [q_ref[:, j * dq:(j + 1) * dq].astype(_MXU) for j in range(hp)]
        cqs = [_head_bias(cum_ref[...], g, j, hp) for j in range(hp)] if has_bias else None

        def chunk(c, carry, diag):
            st = pl.multiple_of(c * b, b)
            mask = _lower_mask(b, False) if diag else None
            out = []
            for j in range(hp):
                m, l, acc = carry[j]
                z = _mm_nt(qs[j], k_ref[pl.ds(st, b), j * dq:(j + 1) * dq]) * scale
                if has_bias:
                    z = z + (cqs[j] - cumt_ref[j, c])
                if diag:
                    z = jnp.where(mask, z, NEG_INF)
                m_new = jnp.maximum(m, jnp.max(z, axis=1, keepdims=True))
                p = jnp.exp(z - m_new)
                if diag:
                    p = jnp.where(mask, p, 0.0)
                alpha = jnp.exp(m - m_new)
                l = alpha * l + jnp.sum(p, axis=1, keepdims=True)
                acc = alpha * acc + _mm(p, v_ref[pl.ds(st, b), j * HEAD_DIM:(j + 1) * HEAD_DIM])
                out.append((m_new, l, acc))
            return tuple(out)

        init = tuple((jnp.full((b, 1), NEG_INF, f32), jnp.zeros((b, 1), f32), jnp.zeros((b, HEAD_DIM), f32))
                     for _ in range(hp))
        carry = lax.fori_loop(0, i, lambda c, cr: chunk(c, cr, False), init)
        for j, (m, l, acc) in enumerate(chunk(i, carry, True)):
            o_ref[:, j * HEAD_DIM:(j + 1) * HEAD_DIM] = acc / l
            lse_ref[j] = m + jnp.log(l)

    in_specs = [pl.BlockSpec((b, hp * dq), lambda g, i: (i, qcol // hp + g)),
                pl.BlockSpec((s, hp * dq), lambda g, i: (0, kcol // hp + g)),
                pl.BlockSpec((s, hp * HEAD_DIM), lambda g, i: (0, vcol // hp + g))]
    args = [q, k, v]
    if has_bias:
        in_specs += [pl.BlockSpec((b, LANES), lambda g, i: (i, 0)),
                     pl.BlockSpec((hp, nq, 1, b), lambda g, i: (g, 0, 0, 0))]
        args += [cum, cum_t]
    return pl.pallas_call(
        body, out_shape=(_SDS((s, N_HEADS * HEAD_DIM), f32), _SDS((N_HEADS, s, 1), f32)), grid=(N_HEADS // hp, nq),
        in_specs=in_specs,
        out_specs=(pl.BlockSpec((b, hp * HEAD_DIM), lambda g, i: (i, g)),
                   pl.BlockSpec((hp, b, 1), lambda g, i: (g, i, 0))),
        compiler_params=_cp(("parallel", "parallel")), name=name,
    )(*args)


def _attn_bwd(q, k, v, qcol, kcol, vcol, dq, do, o, lse, cum, cum_t, *, scale, hp, name):
    s = q.shape[0]
    b = _attn_block(s)
    nq = s // b
    has_bias = cum is not None
    assert qcol % hp == 0 and kcol % hp == 0 and vcol % hp == 0
    hd = lambda j: slice(j * HEAD_DIM, (j + 1) * HEAD_DIM)
    hq = lambda j: slice(j * dq, (j + 1) * dq)

    def body(*refs):
        if has_bias:
            (q_ref, k_ref, v_ref, do_ref, o_ref, lse_ref, cum_ref, cumt_ref, dq_ref, dk_ref, dv_ref, dck_ref,
             dkt_sc, dvt_sc, p_sc, dp_sc) = refs
        else:
            q_ref, k_ref, v_ref, do_ref, o_ref, lse_ref, dq_ref, dk_ref, dv_ref, dkt_sc, dvt_sc = refs
        g, i = pl.program_id(0), pl.program_id(1)

        @pl.when(i == 0)
        def _():
            dkt_sc[...] = jnp.zeros_like(dkt_sc)
            dvt_sc[...] = jnp.zeros_like(dvt_sc)
            if has_bias:
                dck_ref[...] = jnp.zeros_like(dck_ref)

        qs = [q_ref[:, hq(j)].astype(_MXU) for j in range(hp)]
        dos = [do_ref[:, hd(j)].astype(_MXU) for j in range(hp)]
        qts = [q_ref[:, hq(j)].T.astype(_MXU) for j in range(hp)]
        dots = [do_ref[:, hd(j)].T.astype(_MXU) for j in range(hp)]
        lses = [lse_ref[j] for j in range(hp)]
        cqs = [_head_bias(cum_ref[...], g, j, hp) for j in range(hp)] if has_bias else None

        def probs(j, c, diag):
            st = pl.multiple_of(c * b, b)
            z = _mm_nt(qs[j], k_ref[pl.ds(st, b), hq(j)]) * scale
            if has_bias:
                z = z + (cqs[j] - cumt_ref[j, c])
            p = jnp.exp(z - lses[j])
            if diag:
                p = jnp.where(_lower_mask(b, False), p, 0.0)
            return p, _mm_nt(dos[j], v_ref[pl.ds(st, b), hd(j)])

        if has_bias:
            def first(c, accs, diag):
                out = []
                for j in range(hp):
                    p, dp = probs(j, c, diag)
                    p_sc[j, c] = p
                    dp_sc[j, c] = dp
                    out.append(accs[j] + jnp.sum(p * dp, axis=1, keepdims=True))
                return tuple(out)

            deltas = lax.fori_loop(0, i, lambda c, a: first(c, a, False),
                                   tuple(jnp.zeros((b, 1), f32) for _ in range(hp)))
            deltas = first(i, deltas, True)
        else:
            deltas = [jnp.sum(do_ref[:, hd(j)] * o_ref[:, hd(j)], axis=1, keepdims=True) for j in range(hp)]

        def chunk(c, dq_accs, diag):
            st = pl.multiple_of(c * b, b)
            out = []
            for j in range(hp):
                p, dp = (p_sc[j, c], dp_sc[j, c]) if has_bias else probs(j, c, diag)
                ds = p * (dp - deltas[j])
                dkt_sc[j, c] += _mm(qts[j], ds)
                dvt_sc[j, c] += _mm(dots[j], p)
                if has_bias:
                    dck_ref[j, c] += -jnp.sum(ds, axis=0, keepdims=True)
                out.append(dq_accs[j] + _mm(ds, k_ref[pl.ds(st, b), hq(j)]))
            return tuple(out)

        accs = lax.fori_loop(0, i, lambda c, a: chunk(c, a, False), tuple(jnp.zeros((b, dq), f32) for _ in range(hp)))
        for j, acc in enumerate(chunk(i, accs, True)):
            dq_ref[:, hq(j)] = acc * scale

        @pl.when(i == nq - 1)
        def _():
            for j in range(hp):
                for c in range(nq):
                    dk_ref[c * b:(c + 1) * b, hq(j)] = dkt_sc[j, c].T * scale
                    dv_ref[c * b:(c + 1) * b, hd(j)] = dvt_sc[j, c].T

    rowq = pl.BlockSpec((b, hp * HEAD_DIM), lambda g, i: (i, g))
    in_specs = [pl.BlockSpec((b, hp * dq), lambda g, i: (i, qcol // hp + g)),
                pl.BlockSpec((s, hp * dq), lambda g, i: (0, kcol // hp + g)),
                pl.BlockSpec((s, hp * HEAD_DIM), lambda g, i: (0, vcol // hp + g)), rowq, rowq,
                pl.BlockSpec((hp, b, 1), lambda g, i: (g, i, 0))]
    args = [q, k, v, do, o, lse]
    out_shape = [_SDS((s, N_HEADS * dq), f32), _SDS((s, N_HEADS * dq), f32), _SDS((s, N_HEADS * HEAD_DIM), f32)]
    out_specs = [pl.BlockSpec((b, hp * dq), lambda g, i: (i, g)), pl.BlockSpec((s, hp * dq), lambda g, i: (0, g)),
                 pl.BlockSpec((s, hp * HEAD_DIM), lambda g, i: (0, g))]
    if has_bias:
        in_specs += [pl.BlockSpec((b, LANES), lambda g, i: (i, 0)),
                     pl.BlockSpec((hp, nq, 1, b), lambda g, i: (g, 0, 0, 0))]
        args += [cum, cum_t]
        out_shape.append(_SDS((N_HEADS, nq, 1, b), f32))
        out_specs.append(pl.BlockSpec((hp, nq, 1, b), lambda g, i: (g, 0, 0, 0)))
    return pl.pallas_call(
        body, out_shape=tuple(out_shape), grid=(N_HEADS // hp, nq), in_specs=in_specs, out_specs=tuple(out_specs),
        scratch_shapes=[pltpu.VMEM((hp, nq, dq, b), f32), pltpu.VMEM((hp, nq, HEAD_DIM, b), f32)]
        + ([pltpu.VMEM((hp, nq, b, b), f32)] * 2 if has_bias else []),
        compiler_params=_cp(("parallel", "arbitrary")), name=name,
    )(*args)


def _tri(b, kind):
    r, c = _iota((b, b), 0), _iota((b, b), 1)
    cond = {"row_gt": r > c, "row_lt": r < c, "row_ge": r >= c, "row_le": r <= c}[kind]
    return jnp.where(cond, 1.0, 0.0).astype(_MXU)


def _log_keep(z):
    return -(jnp.maximum(z, 0.0) + jnp.log(1.0 + jnp.exp(-jnp.abs(z))))


def _sb_fwd(z_all, *, hp, name):
    s = z_all.shape[0]
    b = _attn_block(s)
    nq = s // b
    scale = HEAD_DIM ** -0.5
    qcol, kcol, vcol = (_AL[n] // (hp * HEAD_DIM) for n in ("sb_q", "sb_k", "sb_v"))
    hd = lambda j: slice(j * HEAD_DIM, (j + 1) * HEAD_DIM)

    def body(q_ref, k_ref, v_ref, o_ref):
        i = pl.program_id(1)
        qs = [q_ref[:, hd(j)].astype(_MXU) for j in range(hp)]
        upper = _tri(b, "row_gt")

        def chunk(c, carry, diag):
            st = pl.multiple_of(c * b, b)
            mask = _lower_mask(b, True) if diag else None
            out = []
            for j in range(hp):
                rsum, acc = carry[j]
                z = _mm_nt(qs[j], k_ref[pl.ds(st, b), hd(j)]) * scale
                lk = _log_keep(z)
                if diag:
                    lk = jnp.where(mask, lk, 0.0)
                a = z + lk + _mm_split(lk, upper) + rsum
                if diag:
                    a = jnp.where(mask, a, NEG_INF)
                acc = acc + _mm(jnp.exp(a), v_ref[pl.ds(st, b), hd(j)])
                out.append((rsum + jnp.sum(lk, axis=1, keepdims=True), acc))
            return tuple(out)

        init = tuple((jnp.zeros((b, 1), f32), jnp.zeros((b, HEAD_DIM), f32)) for _ in range(hp))
        carry = lax.fori_loop(0, i, lambda t, cr: chunk(i - 1 - t, cr, False), chunk(i, init, True))
        for j in range(hp):
            o_ref[:, hd(j)] = carry[j][1]

    w = hp * HEAD_DIM
    return pl.pallas_call(
        body, out_shape=_SDS((s, GROUP), f32), grid=(N_HEADS // hp, nq),
        in_specs=[pl.BlockSpec((b, w), lambda g, i: (i, qcol + g)), pl.BlockSpec((s, w), lambda g, i: (0, kcol + g)),
                  pl.BlockSpec((s, w), lambda g, i: (0, vcol + g))],
        out_specs=pl.BlockSpec((b, w), lambda g, i: (i, g)),
        compiler_params=_cp(("parallel", "parallel")), name=name,
    )(z_all, z_all, z_all)


def _sb_bwd(z_all, do, *, hp, name):
    s = z_all.shape[0]
    b = _attn_block(s)
    nq = s // b
    scale = HEAD_DIM ** -0.5
    qcol, kcol, vcol = (_AL[n] // (hp * HEAD_DIM) for n in ("sb_q", "sb_k", "sb_v"))
    hd = lambda j: slice(j * HEAD_DIM, (j + 1) * HEAD_DIM)

    def body(q_ref, k_ref, v_ref, do_ref, dq_ref, dk_ref, dv_ref, z_sc, lk_sc, r_sc):
        i = pl.program_id(1)

        @pl.when(i == 0)
        def _():
            dk_ref[...] = jnp.zeros_like(dk_ref)
            dv_ref[...] = jnp.zeros_like(dv_ref)

        qs = [q_ref[:, hd(j)].astype(_MXU) for j in range(hp)]
        dos = [do_ref[:, hd(j)].astype(_MXU) for j in range(hp)]
        upper = _tri(b, "row_gt")
        lower = _tri(b, "row_lt")

        def scores(c, rsums, diag):
            st = pl.multiple_of(c * b, b)
            out = []
            for j in range(hp):
                z = _mm_nt(qs[j], k_ref[pl.ds(st, b), hd(j)]) * scale
                lk = _log_keep(z)
                if diag:
                    lk = jnp.where(_lower_mask(b, True), lk, 0.0)
                z_sc[j, c] = z
                lk_sc[j, c] = lk
                r_sc[j, c] = _mm_split(lk, upper) + rsums[j]
                out.append(rsums[j] + jnp.sum(lk, axis=1, keepdims=True))
            return tuple(out)

        rsums = scores(i, tuple(jnp.zeros((b, 1), f32) for _ in range(hp)), True)
        lax.fori_loop(0, i, lambda t, r: scores(i - 1 - t, r, False), rsums)

        def grads(c, carry, diag):
            st = pl.multiple_of(c * b, b)
            mask = _lower_mask(b, True) if diag else None
            out = []
            for j in range(hp):
                psum, dq_acc = carry[j]
                z, lk = z_sc[j, c], lk_sc[j, c]
                lb = z + lk
                a = lb + r_sc[j, c]
                if diag:
                    a = jnp.where(mask, a, NEG_INF)
                w = jnp.exp(a)
                e = _mm_nt(dos[j], v_ref[pl.ds(st, b), hd(j)]) * w
                before = _mm_split(e, lower) + psum
                dz = e * jnp.exp(lk) - before * jnp.exp(lb)
                if diag:
                    dz = jnp.where(mask, dz, 0.0)
                dk_ref[pl.ds(st, b), hd(j)] += _mm_tn(dz, qs[j]) * scale
                dv_ref[pl.ds(st, b), hd(j)] += _mm_tn(w, dos[j])
                out.append((psum + jnp.sum(e, axis=1, keepdims=True), dq_acc + _mm(dz, k_ref[pl.ds(st, b), hd(j)])))
            return tuple(out)

        init = tuple((jnp.zeros((b, 1), f32), jnp.zeros((b, HEAD_DIM), f32)) for _ in range(hp))
        carry = grads(i, lax.fori_loop(0, i, lambda c, cr: grads(c, cr, False), init), True)
        for j in range(hp):
            dq_ref[:, hd(j)] = carry[j][1] * scale

    w = hp * HEAD_DIM
    blk = pl.BlockSpec((b, w), lambda g, i: (i, g))
    full = pl.BlockSpec((s, w), lambda g, i: (0, g))
    return pl.pallas_call(
        body, out_shape=tuple(_SDS((s, GROUP), f32) for _ in range(3)), grid=(N_HEADS // hp, nq),
        in_specs=[pl.BlockSpec((b, w), lambda g, i: (i, qcol + g)), pl.BlockSpec((s, w), lambda g, i: (0, kcol + g)),
                  pl.BlockSpec((s, w), lambda g, i: (0, vcol + g)), blk],
        out_specs=(blk, full, full),
        scratch_shapes=[pltpu.VMEM((hp, nq, b, b), f32)] * 3,
        compiler_params=_cp(("parallel", "arbitrary")), name=name,
    )(z_all, z_all, z_all, do)


def _split3_left(t, x):
    hi = x.astype(_MXU)
    r1 = x - hi.astype(f32)
    mid = r1.astype(_MXU)
    lo = (r1 - mid.astype(f32)).astype(_MXU)
    dot = functools.partial(jnp.dot, preferred_element_type=f32)
    return dot(t, hi) + dot(t, mid) + dot(t, lo)


def _split3_right(x, t):
    hi = x.astype(_MXU)
    r1 = x - hi.astype(f32)
    mid = r1.astype(_MXU)
    lo = (r1 - mid.astype(f32)).astype(_MXU)
    dot = functools.partial(jnp.dot, preferred_element_type=f32)
    return dot(hi, t) + dot(mid, t) + dot(lo, t)


def _fox_cum_fwd(z_all, bias, *, name):
    s = z_all.shape[0]
    b = _attn_block(s)
    fcol = _AL["fox_f"] // LANES

    def body(f_ref, b_ref, cum_ref, cumt_ref, carry_ref):
        i = pl.program_id(0)

        @pl.when(i == 0)
        def _():
            carry_ref[...] = jnp.zeros_like(carry_ref)

        u = f_ref[...] + b_ref[...]
        lf = jnp.minimum(u, 0.0) - jnp.log1p(jnp.exp(-jnp.abs(u)))
        cum = _split3_left(_tri(b, "row_ge"), lf) + carry_ref[...]
        cum_ref[...] = cum
        cumt_ref[...] = cum.T[0:8, :]
        carry_ref[...] = cum_ref[b - 1:b, :]

    return pl.pallas_call(
        body, out_shape=(_SDS((s, LANES), f32), _SDS((8, s), f32)), grid=(s // b,),
        in_specs=[pl.BlockSpec((b, LANES), lambda i: (i, fcol)), pl.BlockSpec((1, LANES), lambda i: (0, 0))],
        out_specs=(pl.BlockSpec((b, LANES), lambda i: (i, 0)), pl.BlockSpec((8, b), lambda i: (0, i))),
        scratch_shapes=[pltpu.VMEM((1, LANES), f32)], compiler_params=_cp(("arbitrary",)), name=name,
    )(z_all, bias)


def _fox_cum_bwd(z_all, bias, dcum_t, *, name):
    s = z_all.shape[0]
    b = _attn_block(s)
    nb = s // b
    fcol = _AL["fox_f"] // LANES

    def body(f_ref, b_ref, dc_ref, df_ref, db_ref, carry_ref):
        i = pl.program_id(0)

        @pl.when(i == 0)
        def _():
            carry_ref[...] = jnp.zeros_like(carry_ref)
            db_ref[...] = jnp.zeros_like(db_ref)

        dc = dc_ref[...]
        rev = _split3_right(dc, _tri(b, "row_ge")) + carry_ref[...]
        carry_ref[...] = carry_ref[...] + jnp.sum(dc, axis=1, keepdims=True)
        dlf = jnp.concatenate([rev, jnp.zeros((LANES - 8, b), f32)], axis=0).T
        u = f_ref[...] + b_ref[...]
        df = jnp.where(_iota((b, LANES), 1) < N_HEADS, dlf * (1.0 - _sigmoid(u)), 0.0)
        df_ref[...] = df
        db_ref[...] += jnp.sum(df, axis=0, keepdims=True)

    return pl.pallas_call(
        body, out_shape=(_SDS((s, LANES), f32), _SDS((1, LANES), f32)), grid=(nb,),
        in_specs=[pl.BlockSpec((b, LANES), lambda i: (nb - 1 - i, fcol)), pl.BlockSpec((1, LANES), lambda i: (0, 0)),
                  pl.BlockSpec((8, b), lambda i: (0, nb - 1 - i))],
        out_specs=(pl.BlockSpec((b, LANES), lambda i: (nb - 1 - i, 0)), pl.BlockSpec((1, LANES), lambda i: (0, 0))),
        scratch_shapes=[pltpu.VMEM((8, 1), f32)], compiler_params=_cp(("arbitrary",)), name=name,
    )(z_all, bias, dcum_t)


MLA_QW = 2 * LANES


def _rms_rows(x):
    r = lax.rsqrt(jnp.mean(x * x, axis=-1, keepdims=True) + RMS_EPS)
    return x * r, r


def _mla_prep_fwd(z_all, gq, gkv, wuq, wk, wv, tables, *, name):
    s = z_all.shape[0]
    rb = _row_block(s)
    half = MLA_ROPE // 2

    def body(cq_ref, ckv_ref, kr_ref, gq_ref, gkv_ref, wuq_ref, wk_ref, wv_ref, cos_ref, sa_ref, sb_ref,
             q_ref, k_ref, v_ref):
        cos, sa, sb = cos_ref[...], sa_ref[...], sb_ref[...]
        xh, _ = _rms_rows(cq_ref[...])
        qp = _mm(xh * gq_ref[...], wuq_ref[...])
        kh, _ = _rms_rows(ckv_ref[...])
        nkv = kh * gkv_ref[...]
        kn = _mm(nkv, wk_ref[...])
        v_ref[...] = _mm(nkv, wv_ref[...])
        kr = _rope(kr_ref[...], cos, sa, sb, half)
        for h in range(N_HEADS):
            lo, mid, hi = h * MLA_QW, h * MLA_QW + LANES, (h + 1) * MLA_QW
            q_ref[:, lo:mid] = qp[:, lo:mid]
            q_ref[:, mid:hi] = _rope(qp[:, mid:hi], cos, sa, sb, half)
            k_ref[:, lo:mid] = kn[:, h * LANES:(h + 1) * LANES]
            k_ref[:, mid:hi] = kr

    row = lambda w, cb: pl.BlockSpec((rb, w), lambda i: (i, cb))
    whole = lambda a: pl.BlockSpec(a.shape, lambda i: (0,) * a.ndim)
    return pl.pallas_call(
        body, out_shape=(_SDS((s, N_HEADS * MLA_QW), f32), _SDS((s, N_HEADS * MLA_QW), f32), _SDS((s, GROUP), f32)),
        grid=(s // rb,),
        in_specs=[row(MLA_Q_RANK, _AL["mla_cq"] // MLA_Q_RANK), row(LANES, _AL["mla_ckv"] // LANES),
                  row(LANES, _AL["mla_k_rope"] // LANES), whole(gq), whole(gkv), whole(wuq), whole(wk), whole(wv),
                  row(LANES, 0), row(LANES, 0), row(LANES, 0)],
        out_specs=(row(N_HEADS * MLA_QW, 0), row(N_HEADS * MLA_QW, 0), row(GROUP, 0)),
        compiler_params=_cp(("parallel",)), name=name,
    )(z_all, z_all, z_all, gq, gkv, wuq, wk, wv, *tables)


def _mla_prep_bwd(z_all, gq, gkv, wuq, wk, wv, tables, dq_cat, dk_cat, dv, *, name):
    s = z_all.shape[0]
    rb = _row_block(s)
    half = MLA_ROPE // 2

    def body(cq_ref, ckv_ref, gq_ref, gkv_ref, wuq_ref, wk_ref, wv_ref, cos_ref, sa_ref, sb_ref, dq_ref, dk_ref,
             dv_ref, dcq_ref, dckv_ref, dkr_ref, dwuq_ref, dwk_ref, dwv_ref, dgq_ref, dgkv_ref):
        i = pl.program_id(0)

        @pl.when(i == 0)
        def _():
            for r in (dwuq_ref, dwk_ref, dwv_ref, dgq_ref, dgkv_ref):
                r[...] = jnp.zeros_like(r)

        cos, sa, sb = cos_ref[...], sa_ref[...], sb_ref[...]
        parts, knp = [], []
        dkr = jnp.zeros((rb, LANES), f32)
        for h in range(N_HEADS):
            lo, mid, hi = h * MLA_QW, h * MLA_QW + LANES, (h + 1) * MLA_QW
            parts += [dq_ref[:, lo:mid], _rope(dq_ref[:, mid:hi], cos, sa, sb, half, transpose=True)]
            knp.append(dk_ref[:, lo:mid])
            dkr = dkr + _rope(dk_ref[:, mid:hi], cos, sa, sb, half, transpose=True)
        dkr_ref[...] = dkr
        dqp = jnp.concatenate(parts, axis=1)
        dkn = jnp.concatenate(knp, axis=1)
        dvv = dv_ref[...]

        def norm_bwd(x_ref, g_ref, w_pairs, dx_ref, dg_ref):
            xh, r = _rms_rows(x_ref[...])
            nx = xh * g_ref[...]
            dn = jnp.zeros_like(xh)
            for w_ref, dw_ref, dy in w_pairs:
                dw_ref[...] += _mm_tn(nx, dy)
                dn = dn + _mm_nt(dy, w_ref[...])
            dxh = dn * g_ref[...]
            dx_ref[...] = r * (dxh - xh * jnp.mean(dxh * xh, axis=-1, keepdims=True))
            dg_ref[...] += jnp.sum(dn * xh, axis=0, keepdims=True)

        norm_bwd(cq_ref, gq_ref, [(wuq_ref, dwuq_ref, dqp)], dcq_ref, dgq_ref)
        norm_bwd(ckv_ref, gkv_ref, [(wk_ref, dwk_ref, dkn), (wv_ref, dwv_ref, dvv)], dckv_ref, dgkv_ref)

    row = lambda w, cb: pl.BlockSpec((rb, w), lambda i: (i, cb))
    whole = lambda a: pl.BlockSpec(a.shape, lambda i: (0,) * a.ndim)
    return pl.pallas_call(
        body,
        out_shape=(_SDS((s, MLA_Q_RANK), f32), _SDS((s, LANES), f32), _SDS((s, LANES), f32), _SDS(wuq.shape, f32),
                   _SDS(wk.shape, f32), _SDS(wv.shape, f32), _SDS(gq.shape, f32), _SDS(gkv.shape, f32)),
        grid=(s // rb,),
        in_specs=[row(MLA_Q_RANK, _AL["mla_cq"] // MLA_Q_RANK), row(LANES, _AL["mla_ckv"] // LANES), whole(gq),
                  whole(gkv), whole(wuq), whole(wk), whole(wv), row(LANES, 0), row(LANES, 0), row(LANES, 0),
                  row(N_HEADS * MLA_QW, 0), row(N_HEADS * MLA_QW, 0), row(GROUP, 0)],
        out_specs=(row(MLA_Q_RANK, 0), row(LANES, 0), row(LANES, 0), whole(wuq), whole(wk), whole(wv), whole(gq),
                   whole(gkv)),
        compiler_params=_cp(("arbitrary",)), name=name,
    )(z_all, z_all, gq, gkv, wuq, wk, wv, *tables, dq_cat, dk_cat, dv)


def _silu_grad(x):
    sg = _sigmoid(x)
    return sg * (1.0 + x * (1.0 - sg))


def _nsa_cmp_fwd(ra, rb_, pos, w1, w2, tables, *, name):
    nr = ra.shape[1]
    hw = ra.shape[2]

    def body(ra_ref, rb_ref, pos_ref, w1_ref, w2_ref, cos_ref, sa_ref, sb_ref, out_ref, hp_ref):
        for k in range(2):
            xa = ra_ref[k] + pos_ref[k, :, 0:hw]
            xb = rb_ref[k] + pos_ref[k, :, hw:2 * hw]
            hp = _mm(xa, w1_ref[k, 0:hw, :]) + _mm(xb, w1_ref[k, hw:2 * hw, :])
            hp_ref[k] = hp
            out = _mm(hp * _sigmoid(hp), w2_ref[k])
            if k == 0:
                out = _rope(out, cos_ref[...], sa_ref[...], sb_ref[...], HEAD_DIM // 2)
            out_ref[k] = out

    return pl.pallas_call(body, out_shape=(_SDS((2, nr, HEAD_DIM), f32), _SDS((2, nr, HEAD_DIM), f32)),
                          compiler_params=_cp(), name=name)(ra, rb_, pos, w1, w2, *tables)


def _nsa_cmp_bwd(ra, rb_, pos, w1, w2, tables, hp, dout, *, name):
    nr = ra.shape[1]
    hw = ra.shape[2]

    def body(ra_ref, rb_ref, pos_ref, w1_ref, w2_ref, cos_ref, sa_ref, sb_ref, hp_ref, do_ref,
             dxa_ref, dxb_ref, dw1_ref, dw2_ref):
        for k in range(2):
            d_out = do_ref[k]
            if k == 0:
                d_out = _rope(d_out, cos_ref[...], sa_ref[...], sb_ref[...], HEAD_DIM // 2, transpose=True)
            hpv = hp_ref[k]
            dw2_ref[k] = _mm_tn(hpv * _sigmoid(hpv), d_out)
            dhp = _mm_nt(d_out, w2_ref[k]) * _silu_grad(hpv)
            xa = ra_ref[k] + pos_ref[k, :, 0:hw]
            xb = rb_ref[k] + pos_ref[k, :, hw:2 * hw]
            dw1_ref[k, 0:hw, :] = _mm_tn(xa, dhp)
            dw1_ref[k, hw:2 * hw, :] = _mm_tn(xb, dhp)
            dxa_ref[k] = _mm_nt(dhp, w1_ref[k, 0:hw, :])
            dxb_ref[k] = _mm_nt(dhp, w1_ref[k, hw:2 * hw, :])

    return pl.pallas_call(
        body, out_shape=(_SDS((2, nr, hw), f32), _SDS((2, nr, hw), f32), _SDS(w1.shape, f32), _SDS(w2.shape, f32)),
        compiler_params=_cp(), name=name)(ra, rb_, pos, w1, w2, *tables, hp, dout)


def _nsa_consts(s):
    b = _attn_block(s)
    nr = s // CMP_STRIDE
    n_cmp = (s - CMP_LEN) // CMP_STRIDE + 1
    n_sel = s // SEL_LEN
    cmp_start = np.arange(n_cmp) * CMP_STRIDE
    sel_start = np.arange(n_sel) * SEL_LEN
    overlap = np.clip(np.minimum(cmp_start[:, None] + CMP_LEN, sel_start[None, :] + SEL_LEN)
                      - np.maximum(cmp_start[:, None], sel_start[None, :]), 0, None)
    m2s = np.zeros((nr, LANES), np.float32)
    m2s[:n_cmp, :n_sel] = overlap / CMP_LEN
    e3 = np.zeros((s // b, LANES, b), np.float32)
    tok = np.arange(s)
    e3[tok // b, tok // SEL_LEN, tok % b] = 1.0
    return jnp.asarray(m2s, _MXU), jnp.asarray(e3, _MXU)


def _nsa_masks(i, b, d):
    qpos = i * b + _iota((b, b), 0)
    kpos = (i - d) * b + _iota((b, b), 1)
    return (kpos <= qpos) & (kpos > qpos - WINDOW)


def _nsa_fwd(qr, kvc, ksr, vs, kwr, vw, z_all, m2s, e3, *, name):
    s = qr.shape[0]
    b = _attn_block(s)
    nq = s // b
    nr = kvc.shape[1]
    n_sel = s // SEL_LEN
    top_n = min(SEL_TOPN, n_sel)
    nd = -(-WINDOW // b)
    scale = HEAD_DIM ** -0.5
    bcol = _AL["nsa_branch"] // LANES
    H = N_HEADS

    def body(q_ref, kvc_ref, ks_ref, vs_ref, kw_ref, vw_ref, br_ref, m2s_ref, e3_ref,
             o_ref, oc_ref, os_ref, ow_ref, st_ref, sel_ref, m_sc, l_sc, acc_sc):
        i = pl.program_id(0)
        lane = _iota((b, LANES), 1)
        hs = lambda h: slice(h * HEAD_DIM, (h + 1) * HEAD_DIM)

        cmp_mask = (CMP_STRIDE * _iota((b, nr), 1) + (CMP_LEN - 1)) <= (i * b + _iota((b, nr), 0))
        imp = jnp.zeros((b, LANES), f32)
        stats = jnp.zeros((b, LANES), f32)
        for h in range(H):
            zc = jnp.where(cmp_mask, _mm_nt(q_ref[:, hs(h)], kvc_ref[0]) * scale, NEG_INF)
            m = jnp.max(zc, axis=1, keepdims=True)
            p = jnp.where(cmp_mask, jnp.exp(zc - m), 0.0)
            l = jnp.sum(p, axis=1, keepdims=True)
            some = l > 0.0
            lsafe = jnp.where(some, l, 1.0)
            pc = p * jnp.where(some, 1.0 / lsafe, 0.0)
            oc_ref[:, hs(h)] = _mm(pc, kvc_ref[1])
            imp = imp + _mm(pc, m2s_ref[...])
            stats = jnp.where(lane == h, jnp.where(some, m + jnp.log(lsafe), 0.0), stats)

        cur = jnp.right_shift(i * b + _iota((b, LANES), 0), int(math.log2(SEL_LEN)))
        forced = (lane == 0) | (lane == cur) | (lane == cur - 1)
        score = jnp.where(lane <= cur, jnp.where(forced, FORCED_BONUS, imp), NEG_INF)
        score = jnp.where(lane < n_sel, score, -3e38)
        rank = jnp.zeros((b, LANES), f32)
        for j in range(n_sel):
            col = score[:, j:j + 1]
            rank = rank + jnp.where(col > score, 1.0, jnp.where(col == score, jnp.where(lane > j, 1.0, 0.0), 0.0))
        sel = jnp.where(lane < n_sel, jnp.where(rank < top_n, 1.0, 0.0), 0.0)
        sel_ref[...] = sel
        sel_b = sel.astype(_MXU)

        def reset():
            m_sc[...] = jnp.full(m_sc.shape, NEG_INF, f32)
            l_sc[...] = jnp.zeros_like(l_sc)
            acc_sc[...] = jnp.zeros_like(acc_sc)

        def update(h, z, mask, vch):
            zm = jnp.where(mask, z, NEG_INF)
            m_old = m_sc[h]
            m_new = jnp.maximum(m_old, jnp.max(zm, axis=1, keepdims=True))
            p = jnp.where(mask, jnp.exp(zm - m_new), 0.0)
            alpha = jnp.exp(m_old - m_new)
            l_sc[h] = alpha * l_sc[h] + jnp.sum(p, axis=1, keepdims=True)
            acc_sc[h] = alpha * acc_sc[h] + _mm(p, vch)
            m_sc[h] = m_new

        def finish(out_ref, branch, stats):
            for h in range(H):
                out_ref[:, hs(h)] = acc_sc[h] / l_sc[h]
                stats = jnp.where(lane == 4 * branch + h, m_sc[h] + jnp.log(l_sc[h]), stats)
            return stats

        def sel_chunk(c, diag):
            st = pl.multiple_of(c * b, b)
            mask = _mm(sel_b, e3_ref[c]) > 0.5
            if diag:
                mask = mask & _lower_mask(b, False)
            kch, vch = ks_ref[pl.ds(st, b), :], vs_ref[pl.ds(st, b), :]
            for h in range(H):
                update(h, _mm_nt(q_ref[:, hs(h)], kch) * scale, mask, vch)

        reset()

        def sel_loop(c, carry):
            sel_chunk(c, False)
            return carry

        lax.fori_loop(0, i, sel_loop, 0)
        sel_chunk(i, True)
        stats = finish(os_ref, 1, stats)

        reset()
        for d in range(nd, -1, -1):
            @pl.when(i >= d)
            def _():
                st = pl.multiple_of((i - d) * b, b)
                mask = _nsa_masks(i, b, d)
                kch, vch = kw_ref[pl.ds(st, b), :], vw_ref[pl.ds(st, b), :]
                for h in range(H):
                    update(h, _mm_nt(q_ref[:, hs(h)], kch) * scale, mask, vch)
        stats = finish(ow_ref, 2, stats)
        st_ref[...] = stats

        g = _sigmoid(br_ref[...])
        for h in range(H):
            o_ref[:, hs(h)] = (g[:, 3 * h:3 * h + 1] * oc_ref[:, hs(h)] + g[:, 3 * h + 1:3 * h + 2] * os_ref[:, hs(h)]
                               + g[:, 3 * h + 2:3 * h + 3] * ow_ref[:, hs(h)])

    blk = lambda w: pl.BlockSpec((b, w), lambda i: (i, 0))
    whole = lambda a: pl.BlockSpec(a.shape, lambda i: (0,) * a.ndim)
    return pl.pallas_call(
        body, out_shape=tuple(_SDS((s, GROUP), f32) for _ in range(4)) + (_SDS((s, LANES), f32), _SDS((s, LANES), f32)),
        grid=(nq,),
        in_specs=[blk(GROUP), whole(kvc), whole(ksr), whole(vs), whole(kwr), whole(vw),
                  pl.BlockSpec((b, LANES), lambda i: (i, bcol)), whole(m2s), whole(e3)],
        out_specs=(blk(GROUP),) * 4 + (blk(LANES), blk(LANES)),
        scratch_shapes=[pltpu.VMEM((H, b, 1), f32), pltpu.VMEM((H, b, 1), f32), pltpu.VMEM((H, b, HEAD_DIM), f32)],
        compiler_params=_cp(("parallel",)), name=name,
    )(qr, kvc, ksr, vs, kwr, vw, z_all, m2s, e3)


def _nsa_bwd(do, qr, kvc, ksr, vs, kwr, vw, z_all, oc, os_, ow, stats, sel, e3, *, name):
    s = qr.shape[0]
    b = _attn_block(s)
    nq = s // b
    nr = kvc.shape[1]
    nd = -(-WINDOW // b)
    scale = HEAD_DIM ** -0.5
    bcol = _AL["nsa_branch"] // LANES
    H = N_HEADS

    def body(do_ref, q_ref, kvc_ref, ks_ref, vs_ref, kw_ref, vw_ref, br_ref, oc_ref, os_ref, ow_ref, st_ref, sel_ref,
             e3_ref, dq_ref, dbr_ref, dkvc_ref, dks_ref, dvs_ref, dkw_ref, dvw_ref, dob_sc, delta_sc, dq_sc, kvt_sc):
        i = pl.program_id(0)

        @pl.when(i == 0)
        def _():
            dkvc_ref[...] = jnp.zeros_like(dkvc_ref)
            kvt_sc[...] = jnp.zeros_like(kvt_sc)

        lane = _iota((b, LANES), 1)
        hs = lambda h: slice(h * HEAD_DIM, (h + 1) * HEAD_DIM)
        g = _sigmoid(br_ref[...])
        stats = st_ref[...]
        dbr = jnp.zeros((b, LANES), f32)
        outs = (oc_ref, os_ref, ow_ref)
        for h in range(H):
            doh = do_ref[:, hs(h)]
            for j in range(3):
                gj = g[:, 3 * h + j:3 * h + j + 1]
                dgj = jnp.sum(doh * outs[j][:, hs(h)], axis=1, keepdims=True)
                dbr = jnp.where(lane == 3 * h + j, dgj * gj * (1.0 - gj), dbr)
                dob_sc[j, :, hs(h)] = gj * doh
                delta_sc[j, h] = gj * dgj
        dbr_ref[...] = dbr
        dq_sc[...] = jnp.zeros_like(dq_sc)

        qts = [q_ref[:, hs(h)].T.astype(_MXU) for h in range(H)]
        dobts = {(j, h): dob_sc[j, :, hs(h)].T.astype(_MXU) for j in (1, 2) for h in range(H)}

        def branch(j, h, z, mask, kch, vch):
            p = jnp.where(mask, jnp.exp(jnp.where(mask, z, NEG_INF) - stats[:, 4 * j + h:4 * j + h + 1]), 0.0)
            dob = dob_sc[j, :, hs(h)]
            ds = p * (_mm_nt(dob, vch) - delta_sc[j, h])
            dq_sc[:, hs(h)] += _mm(ds, kch) * scale
            if j == 0:
                return _mm_tn(ds, q_ref[:, hs(h)]) * scale, _mm_tn(p, dob)
            return _mm(qts[h], ds), _mm(dobts[j, h], p)

        cmp_mask = (CMP_STRIDE * _iota((b, nr), 1) + (CMP_LEN - 1)) <= (i * b + _iota((b, nr), 0))
        kc, vc = kvc_ref[0], kvc_ref[1]
        for h in range(H):
            dk, dv = branch(0, h, _mm_nt(q_ref[:, hs(h)], kc) * scale, cmp_mask, kc, vc)
            dkvc_ref[0] += dk
            dkvc_ref[1] += dv

        sel_b = sel_ref[...].astype(_MXU)

        def chunk(j, c, mask, k_ref, v_ref):
            st = pl.multiple_of(c * b, b)
            kch, vch = k_ref[pl.ds(st, b), :], v_ref[pl.ds(st, b), :]
            dk = jnp.zeros((HEAD_DIM, b), f32)
            dv = jnp.zeros((HEAD_DIM, b), f32)
            for h in range(H):
                dkh, dvh = branch(j, h, _mm_nt(q_ref[:, hs(h)], kch) * scale, mask, kch, vch)
                dk, dv = dk + dkh, dv + dvh
            kvt_sc[2 * j - 2, c] += dk
            kvt_sc[2 * j - 1, c] += dv

        def sel_chunk(c, diag):
            mask = _mm(sel_b, e3_ref[c]) > 0.5
            if diag:
                mask = mask & _lower_mask(b, False)
            chunk(1, c, mask, ks_ref, vs_ref)

        def sel_loop(c, carry):
            sel_chunk(c, False)
            return carry

        lax.fori_loop(0, i, sel_loop, 0)
        sel_chunk(i, True)

        for d in range(nd, -1, -1):
            @pl.when(i >= d)
            def _():
                chunk(2, i - d, _nsa_masks(i, b, d), kw_ref, vw_ref)

        dq_ref[...] = dq_sc[...]

        @pl.when(i == nq - 1)
        def _():
            for c in range(nq):
                rows = slice(c * b, (c + 1) * b)
                dks_ref[rows, :] = kvt_sc[0, c].T * scale
                dvs_ref[rows, :] = kvt_sc[1, c].T
                dkw_ref[rows, :] = kvt_sc[2, c].T * scale
                dvw_ref[rows, :] = kvt_sc[3, c].T

    blk = lambda w: pl.BlockSpec((b, w), lambda i: (i, 0))
    whole = lambda a: pl.BlockSpec(a.shape, lambda i: (0,) * a.ndim)
    stream = _SDS((s, HEAD_DIM), f32)
    return pl.pallas_call(
        body, out_shape=(_SDS((s, GROUP), f32), _SDS((s, LANES), f32), _SDS(kvc.shape, f32), stream, stream, stream,
                         stream),
        grid=(nq,),
        in_specs=[blk(GROUP), blk(GROUP), whole(kvc), whole(ksr), whole(vs), whole(kwr), whole(vw),
                  pl.BlockSpec((b, LANES), lambda i: (i, bcol)), blk(GROUP), blk(GROUP), blk(GROUP), blk(LANES),
                  blk(LANES), whole(e3)],
        out_specs=(blk(GROUP), blk(LANES), whole(kvc), whole(ksr), whole(vs), whole(kwr), whole(vw)),
        scratch_shapes=[pltpu.VMEM((3, b, GROUP), f32), pltpu.VMEM((3, H, b, 1), f32), pltpu.VMEM((b, GROUP), f32),
                        pltpu.VMEM((4, nq, HEAD_DIM, b), f32)],
        compiler_params=_cp(("arbitrary",)), name=name,
    )(do, qr, kvc, ksr, vs, kwr, vw, z_all, oc, os_, ow, stats, sel, e3)


def _seg(a, name):
    parts = [lax.slice_in_dim(a, off, off + hi - lo, axis=a.ndim - 1) for off, lo, hi in _PIECES[name]]
    return parts[0] if len(parts) == 1 else jnp.concatenate(parts, axis=a.ndim - 1)


def _to_groups(segs, rows, dtype):
    cols = []
    for s, grp in enumerate(_GROUPS):
        at = 0
        for n, lo, hi, off in sorted(grp, key=lambda t: t[3]):
            if off > at:
                cols.append(jnp.zeros((rows, off - at), dtype))
            cols.append(segs[n][:, lo:hi].astype(dtype))
            at = off + hi - lo
        if at < GROUP_W:
            cols.append(jnp.zeros((rows, GROUP_W - at), dtype))
    return jnp.concatenate(cols, axis=1)


def _piece_from_shard(w_t, s):
    grp = sorted(_GROUPS[s], key=lambda t: t[3])
    ends = [t[3] for t in grp[1:]] + [GROUP_W]
    rows = []
    for (n, lo, hi, off), end in zip(grp, ends):
        first = _ORIG[n] + lo - s * CHIP_COLS
        rows.append(jnp.pad(w_t[:, first:first + hi - lo], ((0, 0), (0, end - off - (hi - lo)), (0, 0))))
    return jnp.concatenate(rows, axis=1)


def _shard_from_piece(g, s):
    return jnp.concatenate([g[:, off:off + hi - lo] for n, lo, hi, off in
                            sorted(_GROUPS[s], key=lambda t: _ORIG[t[0]] + t[1])], axis=1)


def _from_groups(a):
    return jnp.concatenate([_seg(a, n) for n, _ in _SEGS], axis=1)


def _cmp_rows(tok):
    s = tok.shape[0]
    r = tok.reshape(s // CMP_STRIDE, CMP_STRIDE * HEAD_DIM)
    return r, jnp.concatenate([r[1:], jnp.zeros((1, r.shape[1]), r.dtype)], axis=0)


def _cmp_unrows(dxa, dxb):
    s = dxa.shape[0] * CMP_STRIDE
    return (dxa + jnp.concatenate([jnp.zeros((1, dxa.shape[1]), dxa.dtype), dxb[:-1]], axis=0)).reshape(s, HEAD_DIM)


_GATES = ("sb_gate", "nsa_gate", "fox_gate", "mla_gate")


def _layer_fwd(x, p, c, tag):
    s = x.shape[0]
    b = _attn_block(s)
    h = _rms_fwd(x, p["pre_g"], out_dtype=_MXU, name=f"prenorm_{tag}")
    z = _matmul(h, p["w_in"], "nt", bias=p["b_in"], name=f"inproj_{tag}")
    o_sb = _sb_fwd(z, hp=HP_FWD, name=f"sb_fwd_{tag}")

    qr, ksr, kwr = _rope_call([(z, GROUP, _AL["nsa_q"] // GROUP), (z, LANES, _AL["nsa_k_sel"] // LANES),
                               (z, LANES, _AL["nsa_k_win"] // LANES)], c["tabs128"], HEAD_DIM // 2, False,
                              name=f"nsa_rope_{tag}")
    (rak, rbk), (rav, rbv) = _cmp_rows(_seg(z, "nsa_k_cmp")), _cmp_rows(_seg(z, "nsa_v_cmp"))
    ra, rb_ = jnp.stack([rak, rav]), jnp.stack([rbk, rbv])
    kvc, hp = _nsa_cmp_fwd(ra, rb_, p["cmp_pos"], p["cmp_w1"], p["cmp_w2"], c["tabs_cmp"], name=f"nsa_cmp_{tag}")
    vs, vw = _seg(z, "nsa_v_sel"), _seg(z, "nsa_v_win")
    o_nsa, oc, os_, ow, stats, sel = _nsa_fwd(qr, kvc, ksr, vs, kwr, vw, z, c["m2s"], c["e3"], name=f"nsa_fwd_{tag}")

    cum, cum_t8 = _fox_cum_fwd(z, p["fox_bias"], name=f"fox_cum_{tag}")
    cum_t = cum_t8.reshape(8, s // b, 1, b)
    fox_v = _seg(z, "fox_v")
    fcols = (_AL["fox_q"] // HEAD_DIM, _AL["fox_k"] // HEAD_DIM, 0)
    o_fox, lse_fox = _attn_fwd(z, z, fox_v, *fcols, HEAD_DIM, cum, cum_t, scale=HEAD_DIM ** -0.5, hp=HP_FWD,
                               name=f"fox_fwd_{tag}")

    qcat, kcat, vm = _mla_prep_fwd(z, p["gq"], p["gkv"], p["wuq"], p["wk"], p["wv"], c["tabs64"],
                                   name=f"mla_prep_{tag}")
    o_mla, lse_mla = _attn_fwd(qcat, kcat, vm, 0, 0, 0, MLA_QW, None, None, scale=(MLA_NOPE + MLA_ROPE) ** -0.5,
                               hp=HP_BWD, name=f"mla_fwd_{tag}")

    o_all = (o_sb, o_nsa, o_fox, o_mla)
    gates = jnp.concatenate([_seg(z, n) for n in _GATES], axis=1)
    mix = _gate_fwd(o_all, gates, name=f"gate_{tag}")
    u = _matmul(mix, p["w_out"], "nn", name=f"outproj_{tag}")
    y = _postnorm_fwd(u, p["post_g"], x, name=f"postnorm_{tag}")
    saved = dict(x=x, h=h, z=z, qr=qr, ksr=ksr, kwr=kwr, ra=ra, rb=rb_, kvc=kvc, hp=hp, vs=vs, vw=vw, oc=oc, os=os_,
                 ow=ow, stats=stats, sel=sel, cum=cum, cum_t=cum_t, fox_v=fox_v, o_fox=o_fox, lse_fox=lse_fox, qcat=qcat, kcat=kcat,
                 vm=vm, o_mla=o_mla, lse_mla=lse_mla, o_all=o_all, gates=gates, mix=mix, u=u)
    return y, saved


def _layer_bwd(dy, sv, p, c, tag, dw_dtype=f32):
    z = sv["z"]
    s = z.shape[0]
    du, dg_post = _rms_bwd(dy, sv["u"], p["post_g"], out_dtype=_MXU, name=f"postnorm_bwd_{tag}")
    dmix = _matmul(du, p["w_out"], "nt", name=f"outproj_dx_{tag}")
    dw_out = _matmul(sv["mix"], du, "tn", name=f"outproj_dw_{tag}")
    do_sb, do_nsa, do_fox, do_mla, dgates = _gate_bwd(dmix, sv["o_all"], sv["gates"], name=f"gate_bwd_{tag}")
    dgate = [dgates[:, k * GROUP:(k + 1) * GROUP] for k in range(4)]

    sb_dq, sb_dk, sb_dv = _sb_bwd(z, do_sb, hp=HP_BWD, name=f"sb_bwd_{tag}")

    n_dq, n_dbr, n_dkvc, n_dks, n_dvs, n_dkw, n_dvw = _nsa_bwd(
        do_nsa, sv["qr"], sv["kvc"], sv["ksr"], sv["vs"], sv["kwr"], sv["vw"], z, sv["oc"], sv["os"], sv["ow"],
        sv["stats"], sv["sel"], c["e3"], name=f"nsa_bwd_{tag}")
    dxa, dxb, dw1, dw2 = _nsa_cmp_bwd(sv["ra"], sv["rb"], p["cmp_pos"], p["cmp_w1"], p["cmp_w2"], c["tabs_cmp"],
                                      sv["hp"], n_dkvc, name=f"nsa_cmp_bwd_{tag}")
    n_dq, n_dks, n_dkw = _rope_call([(n_dq, GROUP, 0), (n_dks, LANES, 0), (n_dkw, LANES, 0)], c["tabs128"],
                                    HEAD_DIM // 2, True, name=f"nsa_rope_bwd_{tag}")
    dpos = _colsum(jnp.concatenate([dxa[0], dxb[0], dxa[1], dxb[1]], axis=1), name=f"nsa_dpos_{tag}")
    flat = CMP_LEN * HEAD_DIM

    fcols = (_AL["fox_q"] // HEAD_DIM, _AL["fox_k"] // HEAD_DIM, 0)
    f_dq, f_dk, f_dv, f_dck = _attn_bwd(z, z, sv["fox_v"], *fcols, HEAD_DIM, do_fox, sv["o_fox"], sv["lse_fox"],
                                        sv["cum"], sv["cum_t"], scale=HEAD_DIM ** -0.5, hp=HP_BWD,
                                        name=f"fox_bwd_{tag}")
    dcum_t = jnp.pad(f_dck.reshape(N_HEADS, s), ((0, 8 - N_HEADS), (0, 0)))
    f_df, f_dbias = _fox_cum_bwd(z, p["fox_bias"], dcum_t, name=f"fox_cum_bwd_{tag}")

    m_dq, m_dk, m_dv = _attn_bwd(sv["qcat"], sv["kcat"], sv["vm"], 0, 0, 0, MLA_QW, do_mla, sv["o_mla"], sv["lse_mla"],
                                 None, None, scale=(MLA_NOPE + MLA_ROPE) ** -0.5, hp=HP_BWD, name=f"mla_bwd_{tag}")
    m_dcq, m_dckv, m_dkr, m_dwuq, m_dwk, m_dwv, m_dgq, m_dgkv = _mla_prep_bwd(
        z, p["gq"], p["gkv"], p["wuq"], p["wk"], p["wv"], c["tabs64"], m_dq, m_dk, m_dv, name=f"mla_prep_bwd_{tag}")

    dz = _to_groups(dict(
        sb_q=sb_dq, sb_k=sb_dk, sb_v=sb_dv, sb_gate=dgate[0], nsa_q=n_dq, nsa_k_cmp=_cmp_unrows(dxa[0], dxb[0]),
        nsa_v_cmp=_cmp_unrows(dxa[1], dxb[1]), nsa_k_sel=n_dks, nsa_v_sel=n_dvs, nsa_k_win=n_dkw, nsa_v_win=n_dvw,
        nsa_branch=n_dbr, nsa_gate=dgate[1], fox_q=f_dq, fox_k=f_dk, fox_v=f_dv, fox_f=f_df, fox_gate=dgate[2],
        mla_cq=m_dcq, mla_ckv=m_dckv, mla_k_rope=m_dkr, mla_gate=dgate[3]), s, _MXU)
    dh = _matmul(dz, p["w_in"], "nn", name=f"inproj_dx_{tag}")
    dw_in = _matmul(dz, sv["h"], "tn", out_dtype=dw_dtype, name=f"inproj_dw_{tag}")
    db = _colsum(dz, name=f"inproj_db_{tag}")
    dx, dg_pre = _rms_bwd(dh, sv["x"], p["pre_g"], res=dy, name=f"prenorm_bwd_{tag}")

    qw = MLA_NOPE + MLA_ROPE
    grads = {
        "pre_norm_g": dg_pre[0], "post_norm_g": dg_post[0], "w_in": dw_in, "b_in": _from_groups(db)[0],
        "w_out": dw_out, "fox_forget_bias": f_dbias[0, :N_HEADS],
        "nsa_cmp_pos_k": dpos[0, :flat].reshape(CMP_LEN, HEAD_DIM), "nsa_cmp_w1_k": dw1[0], "nsa_cmp_w2_k": dw2[0],
        "nsa_cmp_pos_v": dpos[0, flat:].reshape(CMP_LEN, HEAD_DIM), "nsa_cmp_w1_v": dw1[1], "nsa_cmp_w2_v": dw2[1],
        "mla_q_norm_g": m_dgq[0],
        "mla_w_uq": jnp.concatenate([m_dwuq[:, MLA_QW * h:MLA_QW * h + qw] for h in range(N_HEADS)], axis=1),
        "mla_kv_norm_g": m_dgkv[0],
        "mla_w_ukv": jnp.concatenate(sum([[m_dwk[:, LANES * h:LANES * (h + 1)], m_dwv[:, LANES * h:LANES * (h + 1)]]
                                          for h in range(N_HEADS)], []), axis=1),
    }
    return dx, grads


def _layer_params(w, l):
    b_in = w["b_in"][l].reshape(1, -1)
    b_segs = {n: b_in[:, _ORIG[n]:_ORIG[n] + wd] for n, wd in _SEGS}
    qw = MLA_NOPE + MLA_ROPE
    w_uq, w_ukv = w["mla_w_uq"][l], w["mla_w_ukv"][l]
    uq = []
    for h in range(N_HEADS):
        uq += [w_uq[:, qw * h:qw * (h + 1)], jnp.zeros((w_uq.shape[0], MLA_QW - qw), w_uq.dtype)]
    kw_ = 2 * LANES
    flat = CMP_LEN * HEAD_DIM
    return dict(
        pre_g=w["pre_norm_g"][l].reshape(1, -1), post_g=w["post_norm_g"][l].reshape(1, -1),
        w_in=w["w_in"][l], b_in=_to_groups(b_segs, 1, f32), w_out=w["w_out"][l],
        fox_bias=jnp.pad(w["fox_forget_bias"][l], (0, LANES - N_HEADS)).reshape(1, LANES),
        cmp_pos=jnp.stack([w["nsa_cmp_pos_k"][l].reshape(1, flat), w["nsa_cmp_pos_v"][l].reshape(1, flat)]),
        cmp_w1=jnp.stack([w["nsa_cmp_w1_k"][l], w["nsa_cmp_w1_v"][l]]),
        cmp_w2=jnp.stack([w["nsa_cmp_w2_k"][l], w["nsa_cmp_w2_v"][l]]),
        gq=w["mla_q_norm_g"][l].reshape(1, -1), gkv=w["mla_kv_norm_g"][l].reshape(1, -1),
        wuq=jnp.concatenate(uq, axis=1),
        wk=jnp.concatenate([w_ukv[:, kw_ * h:kw_ * h + LANES] for h in range(N_HEADS)], axis=1),
        wv=jnp.concatenate([w_ukv[:, kw_ * h + LANES:kw_ * (h + 1)] for h in range(N_HEADS)], axis=1),
    )


def _consts(s):
    pos = jnp.arange(s)
    m2s, e3 = _nsa_consts(s)
    return dict(tabs128=_rope_tables(pos, HEAD_DIM), tabs64=_rope_tables(pos, MLA_ROPE),
                tabs_cmp=_rope_tables(jnp.arange(s // CMP_STRIDE) * CMP_STRIDE + (CMP_LEN - 1), HEAD_DIM),
                m2s=m2s, e3=e3)


def _place():
    return lax.axis_index("x"), lax.axis_index("y"), lax.axis_index("c")


def _other_chips(x, y):
    return [(1 - x, y), (x, 1 - y), (1 - x, 1 - y)]


def _comm_call(body, out_shapes, n_sems, arrs, name):
    return pl.pallas_call(body, out_shape=tuple(out_shapes), in_specs=[_ANY] * len(arrs),
                          out_specs=tuple(_ANY for _ in out_shapes),
                          scratch_shapes=[pltpu.SemaphoreType.DMA((n_sems,)), pltpu.SemaphoreType.DMA((n_sems,))],
                          name=name)(*arrs)


def _gather_chips(arrs, *, name):
    n = len(arrs)

    def body(*refs):
        a_refs, out_refs, send_sems, recv_sems = refs[:n], refs[n:2 * n], refs[2 * n], refs[2 * n + 1]
        x, y, c = _place()
        me = 2 * x + y
        sibling = (x, y, 1 - c)
        chips = _other_chips(x, y)

        def copy(j, k, src, dst, to):
            return pltpu.make_async_remote_copy(src, dst, send_sems.at[6 * j + k], recv_sems.at[6 * j + k],
                                                device_id=to, device_id_type=_MESH)

        first = [copy(j, k, a_refs[j].at[c], out_refs[j].at[me, c], (px, py, c))
                 for k, (px, py) in enumerate(chips) for j in range(n)]
        for cp in first:
            cp.start()
        passed = []
        for k, (px, py) in enumerate(chips):
            for j in range(n):
                landed = out_refs[j].at[2 * px + py, c]
                copy(j, k, a_refs[j].at[c], landed, (px, py, c)).wait_recv()
                passed.append(copy(j, 3 + k, landed, landed, sibling))
                passed[-1].start()
        for k, (px, py) in enumerate(chips):
            for j in range(n):
                copy(j, 3 + k, a_refs[j].at[c], out_refs[j].at[2 * px + py, 1 - c], sibling).wait_recv()
        for cp in first + passed:
            cp.wait_send()

    return _comm_call(body, [_SDS((N_CHIPS,) + a.shape, a.dtype) for a in arrs], 6 * n, arrs, name)


def _alltoall_chips(arrs, modes, *, name):
    n = len(arrs)

    def body(*refs):
        g_refs, out_refs, send_sems, recv_sems = refs[:n], refs[n:2 * n], refs[2 * n], refs[2 * n + 1]
        x, y, c = _place()
        me = 2 * x + y

        def copy(j, s):
            return pltpu.make_async_remote_copy(_slot_ref(g_refs[j], modes[j], s), out_refs[j].at[me],
                                                send_sems.at[N_CHIPS * j + s], recv_sems.at[N_CHIPS * j + me],
                                                device_id=(s // 2, s % 2, c), device_id_type=_MESH)

        for s in range(N_CHIPS):
            @pl.when(s != me)
            def _():
                for j in range(n):
                    copy(j, s).start()
        for t in range(N_CHIPS):
            @pl.when(t != me)
            def _():
                for j in range(n):
                    pltpu.make_async_remote_copy(_slot_ref(g_refs[j], modes[j], t), out_refs[j].at[t],
                                                 send_sems.at[N_CHIPS * j + t], recv_sems.at[N_CHIPS * j + t],
                                                 device_id=(t // 2, t % 2, c), device_id_type=_MESH).wait_recv()
        for s in range(N_CHIPS):
            @pl.when(s != me)
            def _():
                for j in range(n):
                    copy(j, s).wait_send()

    outs = [_SDS((N_CHIPS,) + _slot_shape(a, m), a.dtype) for a, m in zip(arrs, modes)]
    return _comm_call(body, outs, N_CHIPS * n, arrs, name)


def _swap_other_half(arrs, *, name):
    n = len(arrs)

    def body(*refs):
        g_refs, out_refs, send_sems, recv_sems = refs[:n], refs[n:2 * n], refs[2 * n], refs[2 * n + 1]
        x, y, c = _place()
        cps = [pltpu.make_async_remote_copy(g_refs[j].at[:, 1 - c], out_refs[j], send_sems.at[j], recv_sems.at[j],
                                            device_id=(x, y, 1 - c), device_id_type=_MESH) for j in range(n)]
        for cp in cps:
            cp.start()
        for cp in cps:
            cp.wait()

    return _comm_call(body, [_SDS((a.shape[0],) + a.shape[2:], a.dtype) for a in arrs], n, arrs, name)


def _swap_sibling(arrs, *, name):
    n = len(arrs)

    def body(*refs):
        f_refs, out_refs, send_sems, recv_sems = refs[:n], refs[n:2 * n], refs[2 * n], refs[2 * n + 1]
        x, y, c = _place()
        cps = [pltpu.make_async_remote_copy(f_refs[j], out_refs[j], send_sems.at[j], recv_sems.at[j],
                                            device_id=(x, y, 1 - c), device_id_type=_MESH) for j in range(n)]
        for cp in cps:
            cp.start()
        for cp in cps:
            cp.wait()

    return _comm_call(body, [_SDS(a.shape, a.dtype) for a in arrs], n, arrs, name)


_HBM = pl.BlockSpec(memory_space=pltpu.HBM)
_SEM = pl.BlockSpec(memory_space=pltpu.SEMAPHORE)
_EFFECT = pltpu.SideEffectType.DATAFLOW_SIDE_EFFECTING


def _slot_ref(ref, mode, s):
    return ref if mode == "same" else ref.at[s]


def _slot_shape(a, mode):
    return a.shape if mode == "same" else a.shape[1:]


def _send_start(arrs, modes, after, *, name):
    n = len(arrs)
    lands = [lax.empty((N_CHIPS,) + _slot_shape(a, m), a.dtype) for a, m in zip(arrs, modes)]

    def body(*refs):
        srcs, land_refs, send_sems, recv_sems, token = refs[:n], refs[n:2 * n], refs[2 * n + 1], refs[2 * n + 2], refs[-1]
        x, y, c = _place()
        me = 2 * x + y
        for s in range(N_CHIPS):
            @pl.when(s != me)
            def _():
                for j in range(n):
                    pltpu.make_async_remote_copy(_slot_ref(srcs[j], modes[j], s), land_refs[j].at[me],
                                                 send_sems.at[N_CHIPS * j + s], recv_sems.at[N_CHIPS * j + me],
                                                 device_id=(s // 2, s % 2, c), device_id_type=_MESH).start()
        token[...] = jnp.zeros_like(token)

    hbm = lambda a: pltpu.HBM(a.shape, a.dtype)
    sems = pltpu.SemaphoreType.DMA((N_CHIPS * n,))
    out = pl.pallas_call(
        body, name=name, out_shape=(sems, sems, *[hbm(a) for a in arrs], *[hbm(a) for a in lands], _SDS((8, LANES), f32)),
        in_specs=[_HBM] * (2 * n) + [_ANY], out_specs=(_SEM, _SEM, *[_HBM] * (2 * n), pl.BlockSpec(memory_space=pltpu.VMEM)),
        input_output_aliases={j: 2 + j for j in range(2 * n)},
        compiler_params=pltpu.CompilerParams(has_side_effects=_EFFECT),
    )(*[pltpu.with_memory_space_constraint(a, pltpu.HBM) for a in arrs + lands], after)
    return out[:-1], out[-1]


def _send_wait(started, modes, after, *, name):
    send_sems, recv_sems = started[0], started[1]
    n = (len(started) - 2) // 2
    thru = list(started[2:])

    def body(*refs):
        srcs, land_refs, send_sems, recv_sems = refs[:n], refs[n:2 * n], refs[2 * n], refs[2 * n + 1]
        x, y, c = _place()
        me = 2 * x + y
        for s in range(N_CHIPS):
            @pl.when(s != me)
            def _():
                for j in range(n):
                    cp = pltpu.make_async_remote_copy(_slot_ref(srcs[j], modes[j], s), land_refs[j].at[s],
                                                      send_sems.at[N_CHIPS * j + s], recv_sems.at[N_CHIPS * j + s],
                                                      device_id=(s // 2, s % 2, c), device_id_type=_MESH)
                    cp.wait_send()
                    cp.wait_recv()

    hbm = lambda a: pltpu.HBM(a.shape, a.dtype)
    out = pl.pallas_call(
        body, name=name, out_shape=tuple(hbm(a) for a in thru), in_specs=[_HBM] * (2 * n) + [_SEM, _SEM, _ANY],
        out_specs=tuple([_HBM] * (2 * n)), input_output_aliases={j: j for j in range(2 * n)},
        compiler_params=pltpu.CompilerParams(has_side_effects=_EFFECT),
    )(*thru, send_sems, recv_sems, after)
    return list(out[n:])


def _add_my_half(g, r, *, name):
    p, _, h, w = g.shape
    tw = _pick(w, (2048, 1024, 512, 256, 128))
    rb = max(d for d in range(16, h + 1, 16) if h % d == 0 and d * tw * 4 <= (2 << 20))

    def body(c_ref, g_ref, r_ref, o_ref):
        o_ref[...] = (g_ref[...].astype(f32) + r_ref[...].astype(f32)).astype(o_ref.dtype)

    blk = pl.BlockSpec((None, rb, tw), lambda s, i, j, c_ref: (s, i, j))
    grid_spec = pltpu.PrefetchScalarGridSpec(
        num_scalar_prefetch=1, grid=(p, h // rb, w // tw),
        in_specs=[pl.BlockSpec((None, None, rb, tw), lambda s, i, j, c_ref: (s, c_ref[0], i, j)), blk], out_specs=blk)
    c = lax.axis_index("c").astype(jnp.int32).reshape(1)
    return pl.pallas_call(body, out_shape=_SDS((p, h, w), _WIRE), grid_spec=grid_spec,
                          compiler_params=_cp(("parallel", "parallel", "parallel")), name=name)(c, g, r)


_WEIGHTS = ("pre_norm_g", "post_norm_g", "w_in", "b_in", "w_out", "fox_forget_bias", "nsa_cmp_pos_k", "nsa_cmp_w1_k",
            "nsa_cmp_w2_k", "nsa_cmp_pos_v", "nsa_cmp_w1_v", "nsa_cmp_w2_v", "mla_q_norm_g", "mla_w_uq",
            "mla_kv_norm_g", "mla_w_ukv")
_SHARD_AXIS = {"w_in": 2, "w_out": 1, "nsa_cmp_w1_k": 1, "nsa_cmp_w1_v": 1, "mla_w_uq": 2, "mla_w_ukv": 2}
_PACK_UNIT = 16 * LANES


def _pack(arrays, dtype):
    rows = []
    for a in arrays:
        v = a.astype(dtype).reshape(-1)
        pad = (-v.shape[0]) % _PACK_UNIT
        if pad:
            v = jnp.concatenate([v, jnp.zeros((pad,), dtype)])
        rows.append(v.reshape(-1, LANES))
    return jnp.concatenate(rows, axis=0)


def _unpack(flat, shapes):
    out, r = [], 0
    for shp in shapes:
        n = int(np.prod(shp))
        nr = -(-n // _PACK_UNIT) * (_PACK_UNIT // LANES)
        out.append(flat[r:r + nr].reshape(-1)[:n].reshape(shp))
        r += nr
    return out


def kernel(x, pre_norm_g, post_norm_g, w_in, b_in, w_out, fox_forget_bias, nsa_cmp_pos_k, nsa_cmp_w1_k, nsa_cmp_w2_k, nsa_cmp_pos_v, nsa_cmp_w1_v, nsa_cmp_w2_v, mla_q_norm_g, mla_w_uq, mla_kv_norm_g, mla_w_ukv, loss_target, m_pre_norm_g, m_post_norm_g, m_w_in, m_b_in, m_w_out, m_fox_forget_bias, m_nsa_cmp_pos_k, m_nsa_cmp_w1_k, m_nsa_cmp_w2_k, m_nsa_cmp_pos_v, m_nsa_cmp_w1_v, m_nsa_cmp_w2_v, m_mla_q_norm_g, m_mla_w_uq, m_mla_kv_norm_g, m_mla_w_ukv, v_pre_norm_g, v_post_norm_g, v_w_in, v_b_in, v_w_out, v_fox_forget_bias, v_nsa_cmp_pos_k, v_nsa_cmp_w1_k, v_nsa_cmp_w2_k, v_nsa_cmp_pos_v, v_nsa_cmp_w1_v, v_nsa_cmp_w2_v, v_mla_q_norm_g, v_mla_w_uq, v_mla_kv_norm_g, v_mla_w_ukv):
    given = dict(locals())
    local = {n: given[n] for n in _WEIGHTS}
    depth = pre_norm_g.shape[0]
    xs, target = x[0], loss_target[0]
    s = xs.shape[0]
    sharded = [n for n in _WEIGHTS if n in _SHARD_AXIS and n != "w_in"]
    small = [n for n in _WEIGHTS if n not in _SHARD_AXIS]
    chip = 2 * lax.axis_index("x") + lax.axis_index("y")
    core = lax.axis_index("c")
    own = lambda slots, mine: lax.dynamic_update_slice_in_dim(slots, mine[None], chip, axis=0)

    w_in_t = jnp.swapaxes(w_in, 1, 2).astype(_MXU)
    piece = lax.switch(chip, [functools.partial(_piece_from_shard, s=k) for k in range(N_CHIPS)], w_in_t)
    layer_shapes = [local[n].shape[1:] for n in sharded]
    flat = [_pack([local[n][l] for n in sharded], _MXU) for l in range(depth)]
    full = dict(local)
    for n in ["w_in"] + sharded:
        full[n] = []

    def add_layer(w_in_slots, flat_slots_):
        full["w_in"].append(w_in_slots)
        per_chip = [_unpack(flat_slots_[k], layer_shapes) for k in range(N_CHIPS)]
        for j, n in enumerate(sharded):
            full[n].append(jnp.concatenate([per_chip[k][j] for k in range(N_CHIPS)], axis=_SHARD_AXIS[n] - 1))

    halved = [piece[0].reshape(2, GROUP_W // 2, D_MODEL), flat[0].reshape(2, -1, LANES)]
    first_all = [own(a, b) for a, b in zip(_gather_chips(halved, name="gather_weights"), halved)]
    add_layer(first_all[0].reshape(N_CHIPS, GROUP_W, D_MODEL), first_all[1].reshape((N_CHIPS,) + flat[0].shape))
    later = [piece[l] for l in range(1, depth)] + flat[1:]
    started, token = _send_start(later, ["same"] * len(later), first_all[1], name="gather_later_start")
    full["pre_norm_g"] = pre_norm_g + token[0, 0]

    consts = _consts(s)
    params, act, saved = [], xs, []
    for l in range(depth):
        if l == 1:
            landed = [own(a, b) for a, b in zip(_send_wait(started, ["same"] * len(later), act,
                                                           name="gather_later_wait"), later)]
            for k in range(depth - 1):
                add_layer(landed[k], landed[depth - 1 + k])
        params.append(_layer_params(full, l))
        act, sv = _layer_fwd(act, params[l], consts, f"l{l}")
        saved.append(sv)
    dy, loss_parts = _loss_head(act, target, name="loss_head")

    def flat_slots(g, dtype):
        def part(n, k):
            a, ax = g[n], _SHARD_AXIS[n] - 1
            w = a.shape[ax] // N_CHIPS
            return lax.slice_in_dim(a, k * w, (k + 1) * w, axis=ax)
        return jnp.stack([_pack([part(n, k) for n in sharded], dtype) for k in range(N_CHIPS)])

    own_slot = lambda a: lax.dynamic_index_in_dim(a, chip, axis=0, keepdims=False)
    slots_of = lambda g: g["w_in"].reshape(N_CHIPS, GROUP_W, D_MODEL)

    modes = ["slots", "slots"]
    layer_grads, in_flight = [None] * depth, {}
    for l in reversed(range(depth)):
        dy, layer_grads[l] = _layer_bwd(dy, saved[l], params[l], consts, f"l{l}", _WIRE)
        if l > 0:
            wire = [slots_of(layer_grads[l]), flat_slots(layer_grads[l], _WIRE)]
            started, token = _send_start(wire, modes, dy, name=f"reduce_l{l}_start")
            in_flight[l] = (started, wire)
            params[l - 1] = dict(params[l - 1], post_g=params[l - 1]["post_g"] + token[0, 0])
    grad_x = dy[None]
    grads = {n: jnp.stack([layer_grads[l][n] for l in range(depth)]) for n in small}
    loss_row = jnp.concatenate([jnp.sum(loss_parts).reshape(1), jnp.zeros((LANES - 1,), f32)])
    small_shapes = [(LANES,)] + [grads[n].shape for n in small]
    contrib = _pack([loss_row] + [grads[n] for n in small], f32)

    halves = [slots_of(layer_grads[0]).reshape(N_CHIPS, 2, GROUP_W // 2, D_MODEL),
              flat_slots(layer_grads[0], _WIRE).reshape(N_CHIPS, 2, -1, LANES)]
    from_sibling = _swap_other_half(halves, name="reduce_pair")
    pair_sum = [_add_my_half(g, r, name=f"reduce_pair_add{j}") for j, (g, r) in enumerate(zip(halves, from_sibling))]
    from_chips = _alltoall_chips(pair_sum + [contrib], modes + ["same"], name="reduce_chips")
    my_half = [_sum_slots(own(slots, own_slot(ps)), name=f"reduce_chips_add{j}")
               for j, (slots, ps) in enumerate(zip(from_chips, pair_sum))]
    partial = []
    for l in range(1, depth):
        started, wire = in_flight[l]
        landed = _send_wait(started, modes, dy, name=f"reduce_l{l}_wait")
        partial += [_sum_slots(own(slots, own_slot(a)), name=f"reduce_l{l}_add{j}")
                    for j, (slots, a) in enumerate(zip(landed, wire))]
    partial.append(_sum_slots(own(from_chips[2], contrib), name="sum_small"))
    theirs = _swap_sibling(my_half + partial, name="reduce_share")
    first = core == 0
    whole = [jnp.concatenate([jnp.where(first, a, b), jnp.where(first, b, a)], axis=0)
             for a, b in zip(my_half, theirs[:2])]
    whole += [_add2(a[None], b[None], name=f"reduce_cores_add{j}")[0] for j, (a, b) in enumerate(zip(partial, theirs[2:]))]
    unpiece = [functools.partial(_shard_from_piece, s=k) for k in range(N_CHIPS)]
    summed = {"w_in": jnp.stack([lax.switch(chip, unpiece, whole[2 * l].T) for l in range(depth)])}
    rest = [_unpack(whole[2 * l + 1], layer_shapes) for l in range(depth)]
    for j, n in enumerate(sharded):
        summed[n] = jnp.stack([rest[l][j] for l in range(depth)])
    total = _unpack(whole[2 * depth], small_shapes)
    loss = total[0][0]
    summed.update(zip(small, total[1:]))

    deltas, new_m, new_v = {}, {}, {}
    for n in _WEIGHTS:
        deltas[n], new_m[n], new_v[n] = _adamw(local[n], summed[n], given["m_" + n], given["v_" + n], name=f"adamw_{n}")
    return (loss, grad_x, *[summed[n] for n in _WEIGHTS], *[deltas[n] for n in _WEIGHTS],
            *[new_m[n] for n in _WEIGHTS], *[new_v[n] for n in _WEIGHTS])
```

```python
import functools
import math

import numpy as np
import jax
import jax.numpy as jnp
from jax import lax
from jax.experimental import pallas as pl
from jax.experimental.pallas import tpu as pltpu

f32 = jnp.float32
bf16 = jnp.bfloat16
_MXU = jnp.bfloat16
_WIRE = jnp.bfloat16
_SDS = jax.ShapeDtypeStruct
_ANY = pl.BlockSpec(memory_space=pl.ANY)
_MESH = pl.DeviceIdType.MESH

D_MODEL = 2048
N_HEADS = 4
HEAD_DIM = 128
GROUP = 512
RMS_EPS = 1e-6
NEG_INF = -1e30
ROPE_THETA = 10000.0
CMP_LEN, CMP_STRIDE, SEL_LEN, SEL_TOPN, WINDOW = 32, 16, 64, 16, 512
FORCED_BONUS = 1e6
MLA_Q_RANK, MLA_KV_RANK, MLA_NOPE, MLA_ROPE = 384, 128, 128, 64
ADAM_LR, ADAM_B1, ADAM_B2, ADAM_EPS, ADAM_WD, ADAM_STEP = 0.001, 0.9, 0.999, 1e-08, 0.01, 10
LANES = 128
VMEM_LIMIT = 56 * 1024 * 1024
HP_FWD, HP_BWD = 2, 2

_SEGS = (
    ("sb_q", 512), ("sb_k", 512), ("sb_v", 512), ("sb_gate", 512), ("nsa_q", 512), ("nsa_k_cmp", 128),
    ("nsa_v_cmp", 128), ("nsa_k_sel", 128), ("nsa_v_sel", 128), ("nsa_k_win", 128), ("nsa_v_win", 128),
    ("nsa_branch", 12), ("nsa_gate", 512), ("fox_q", 512), ("fox_k", 512), ("fox_v", 512), ("fox_f", 4),
    ("fox_gate", 512), ("mla_cq", 384), ("mla_ckv", 128), ("mla_k_rope", 64), ("mla_gate", 512),
)
_ORIG, _WID = {}, {}
_o = 0
for _n, _w in _SEGS:
    _ORIG[_n], _WID[_n] = _o, _w
    _o += _w
IN_WIDTH = _o
N_CHIPS = 4
CHIP_COLS = IN_WIDTH // N_CHIPS
GROUP_W = 2048
ZW = N_CHIPS * GROUP_W
_GROUPS = (
    (("sb_q", 0, 512, 0), ("sb_k", 0, 512, 512), ("sb_v", 0, 512, 1024), ("sb_gate", 0, 212, 1536)),
    (("nsa_q", 0, 512, 0), ("nsa_k_cmp", 0, 128, 512), ("nsa_v_cmp", 0, 128, 640), ("nsa_k_sel", 0, 128, 768),
     ("nsa_v_sel", 0, 128, 896), ("nsa_k_win", 0, 128, 1024), ("nsa_v_win", 0, 128, 1152), ("nsa_branch", 0, 12, 1280),
     ("sb_gate", 212, 512, 1408), ("nsa_gate", 0, 156, 1712)),
    (("fox_q", 0, 512, 0), ("fox_k", 0, 512, 512), ("fox_v", 0, 368, 1024), ("nsa_gate", 156, 512, 1408)),
    (("mla_cq", 0, 384, 0), ("mla_ckv", 0, 128, 384), ("mla_k_rope", 0, 64, 512), ("fox_f", 0, 4, 640),
     ("fox_v", 368, 512, 768), ("fox_gate", 0, 512, 1024), ("mla_gate", 0, 512, 1536)),
)
_PIECES = {n: [] for n, _ in _SEGS}
for _s, _grp in enumerate(_GROUPS):
    _cover = sorted((_ORIG[n] + lo, _ORIG[n] + hi) for n, lo, hi, _ in _grp)
    assert _cover[0][0] == _s * CHIP_COLS and _cover[-1][1] == (_s + 1) * CHIP_COLS
    assert all(a[1] == b[0] for a, b in zip(_cover, _cover[1:]))
    _ends = sorted((off, off + hi - lo) for _, lo, hi, off in _grp)
    assert all(a[1] <= b[0] for a, b in zip(_ends, _ends[1:])) and _ends[-1][1] <= GROUP_W
    assert _ends[0][0] == 0 and all(e[0] % 16 == 0 for e in _ends)
    for _n, _lo, _hi, _off in _grp:
        _PIECES[_n].append((_s * GROUP_W + _off, _lo, _hi))
_AL = {n: p[0][0] for n, p in _PIECES.items() if len(p) == 1}


def _cp(sem=None):
    return pltpu.CompilerParams(dimension_semantics=sem, vmem_limit_bytes=VMEM_LIMIT)


def _mm(a, b):
    return jnp.dot(a.astype(_MXU), b.astype(_MXU), preferred_element_type=f32)


def _mm_nt(a, b):
    return lax.dot_general(a.astype(_MXU), b.astype(_MXU), (((1,), (1,)), ((), ())), preferred_element_type=f32)


def _mm_tn(a, b):
    return lax.dot_general(a.astype(_MXU), b.astype(_MXU), (((0,), (0,)), ((), ())), preferred_element_type=f32)


def _mm_split(x, t):
    hi = x.astype(_MXU)
    lo = (x - hi.astype(f32)).astype(_MXU)
    return jnp.dot(hi, t, preferred_element_type=f32) + jnp.dot(lo, t, preferred_element_type=f32)


def _sigmoid(x):
    return 1.0 / (1.0 + jnp.exp(-x))


def _iota(shape, dim):
    return lax.broadcasted_iota(jnp.int32, shape, dim)


def _pick(n, prefs):
    for p in prefs:
        if n % p == 0:
            return p
    return n


def _matmul(a, b, mode, *, bias=None, out_dtype=f32, name):
    grouped = b.ndim == 3
    b_shape = (b.shape[0] * b.shape[1], b.shape[2]) if grouped else b.shape
    if mode == "nn":
        (M, K), (K2, N) = a.shape, b_shape
    elif mode == "nt":
        (M, K), (N, K2) = a.shape, b_shape
    else:
        (K, M), (K2, N) = a.shape, b_shape
    assert K == K2
    tm = _pick(M, (1024, 512, 384, 256, 128))
    tn = _pick(N, (1024, 512, 384, 256, 128))
    tk = K if K <= 2048 else _pick(K, (2048, 2432, 1024, 512))
    nk = K // tk
    a_spec = {"nn": pl.BlockSpec((tm, tk), lambda i, j, k: (i, k)),
              "nt": pl.BlockSpec((tm, tk), lambda i, j, k: (i, k)),
              "tn": pl.BlockSpec((tk, tm), lambda i, j, k: (k, i))}[mode]
    if not grouped:
        b_spec = {"nn": pl.BlockSpec((tk, tn), lambda i, j, k: (k, j)),
                  "nt": pl.BlockSpec((tn, tk), lambda i, j, k: (j, k)),
                  "tn": pl.BlockSpec((tk, tn), lambda i, j, k: (k, j))}[mode]
    elif mode == "nt":
        per = b.shape[1] // tn
        b_spec = pl.BlockSpec((None, tn, tk), lambda i, j, k: (j // per, j % per, k))
    else:
        assert mode == "nn"
        per = b.shape[1] // tk
        b_spec = pl.BlockSpec((None, tk, tn), lambda i, j, k: (k // per, k % per, j))
    dot = {"nn": _mm, "nt": _mm_nt, "tn": _mm_tn}[mode]
    has_bias = bias is not None

    def body(*refs):
        if has_bias:
            a_ref, b_ref, bias_ref, o_ref, acc_ref = refs
        else:
            a_ref, b_ref, o_ref, acc_ref = refs
            bias_ref = None
        k = pl.program_id(2)
        part = dot(a_ref[...], b_ref[...])

        def finish(total):
            if has_bias:
                total = total + bias_ref[...]
            o_ref[...] = total.astype(o_ref.dtype)

        if nk == 1:
            finish(part)
        else:
            @pl.when(k == 0)
            def _():
                acc_ref[...] = part

            @pl.when(k > 0)
            def _():
                acc_ref[...] += part

            @pl.when(k == nk - 1)
            def _():
                finish(acc_ref[...])

    in_specs = [a_spec, b_spec]
    args = [a, b]
    if has_bias:
        in_specs.append(pl.BlockSpec((1, tn), lambda i, j, k: (0, j)))
        args.append(bias.reshape(1, N))
    return pl.pallas_call(
        body, out_shape=_SDS((M, N), out_dtype), grid=(M // tm, N // tn, nk),
        in_specs=in_specs, out_specs=pl.BlockSpec((tm, tn), lambda i, j, k: (i, j)),
        scratch_shapes=[pltpu.VMEM((tm, tn), f32)],
        compiler_params=_cp(("parallel", "parallel", "arbitrary")), name=name,
    )(*args)


def _row_block(s):
    return _pick(s, (512, 256, 128))


def _rms_fwd(x, g, *, out_dtype, name):
    s, d = x.shape
    rb = _row_block(s)

    def body(x_ref, g_ref, o_ref):
        xv = x_ref[...]
        r = lax.rsqrt(jnp.mean(xv * xv, axis=-1, keepdims=True) + RMS_EPS)
        o_ref[...] = (xv * r * g_ref[...]).astype(o_ref.dtype)

    return pl.pallas_call(
        body, out_shape=_SDS((s, d), out_dtype), grid=(s // rb,),
        in_specs=[pl.BlockSpec((rb, d), lambda i: (i, 0)), pl.BlockSpec((1, d), lambda i: (0, 0))],
        out_specs=pl.BlockSpec((rb, d), lambda i: (i, 0)), compiler_params=_cp(("parallel",)), name=name,
    )(x, g.reshape(1, d))


def _postnorm_fwd(u, g, x, *, name):
    s, d = u.shape
    rb = _row_block(s)

    def body(u_ref, g_ref, x_ref, o_ref):
        uv = u_ref[...]
        r = lax.rsqrt(jnp.mean(uv * uv, axis=-1, keepdims=True) + RMS_EPS)
        o_ref[...] = x_ref[...] + uv * r * g_ref[...]

    return pl.pallas_call(
        body, out_shape=_SDS((s, d), f32), grid=(s // rb,),
        in_specs=[pl.BlockSpec((rb, d), lambda i: (i, 0)), pl.BlockSpec((1, d), lambda i: (0, 0)),
                  pl.BlockSpec((rb, d), lambda i: (i, 0))],
        out_specs=pl.BlockSpec((rb, d), lambda i: (i, 0)), compiler_params=_cp(("parallel",)), name=name,
    )(u, g.reshape(1, d), x)


def _fold_rows(v):
    r = v.shape[0]
    acc = v[0:8]
    for k in range(1, r // 8):
        acc = acc + v[8 * k:8 * k + 8]
    return acc


def _rms_bwd(dy, x, g, res=None, *, out_dtype=f32, name):
    s, d = x.shape
    rb = _row_block(s)
    nb = s // rb
    has_res = res is not None

    def body(*refs):
        if has_res:
            dy_ref, x_ref, g_ref, res_ref, dx_ref, dg_ref, acc_ref = refs
        else:
            dy_ref, x_ref, g_ref, dx_ref, dg_ref, acc_ref = refs
        i = pl.program_id(0)
        xv = x_ref[...]
        r = lax.rsqrt(jnp.mean(xv * xv, axis=-1, keepdims=True) + RMS_EPS)
        xh = xv * r
        dyv = dy_ref[...]
        dxh = dyv * g_ref[...]
        dx = r * (dxh - xh * jnp.mean(dxh * xh, axis=-1, keepdims=True))
        if has_res:
            dx = dx + res_ref[...]
        dx_ref[...] = dx.astype(dx_ref.dtype)
        part = _fold_rows(dyv * xh)

        @pl.when(i == 0)
        def _():
            acc_ref[...] = part

        @pl.when(i > 0)
        def _():
            acc_ref[...] += part

        @pl.when(i == nb - 1)
        def _():
            dg_ref[...] = jnp.sum(acc_ref[...], axis=0, keepdims=True)

    blk = pl.BlockSpec((rb, d), lambda i: (i, 0))
    in_specs = [blk, blk, pl.BlockSpec((1, d), lambda i: (0, 0))] + ([blk] if has_res else [])
    args = [dy, x, g.reshape(1, d)] + ([res] if has_res else [])
    return pl.pallas_call(
        body, out_shape=(_SDS((s, d), out_dtype), _SDS((1, d), f32)), grid=(nb,), in_specs=in_specs,
        out_specs=(blk, pl.BlockSpec((1, d), lambda i: (0, 0))),
        scratch_shapes=[pltpu.VMEM((8, d), f32)], compiler_params=_cp(("arbitrary",)), name=name,
    )(*args)


def _loss_head(y, target, *, name):
    s, d = y.shape
    rb = _row_block(s)
    nb = s // rb

    def body(y_ref, t_ref, dy_ref, l_ref):
        i = pl.program_id(0)
        e = y_ref[...] - t_ref[...]
        dy_ref[...] = e * (1.0 / d)
        rows = _fold_rows(e * e)
        part = rows[:, 0:LANES]
        for k in range(1, d // LANES):
            part = part + rows[:, k * LANES:(k + 1) * LANES]
        part = part * (0.5 / d)

        @pl.when(i == 0)
        def _():
            l_ref[...] = part

        @pl.when(i > 0)
        def _():
            l_ref[...] += part

    blk = pl.BlockSpec((rb, d), lambda i: (i, 0))
    return pl.pallas_call(
        body, out_shape=(_SDS((s, d), f32), _SDS((8, LANES), f32)), grid=(nb,), in_specs=[blk, blk],
        out_specs=(blk, pl.BlockSpec((8, LANES), lambda i: (0, 0))),
        compiler_params=_cp(("arbitrary",)), name=name,
    )(y, target)


def _colsum(a, *, name):
    s, n = a.shape
    rb = _row_block(s)
    nb = s // rb
    tn = _pick(n, (2432, 2048, 1024, 512, 384, 128))

    def body(a_ref, o_ref, acc_ref):
        i = pl.program_id(1)
        part = _fold_rows(a_ref[...].astype(f32))

        @pl.when(i == 0)
        def _():
            acc_ref[...] = part

        @pl.when(i > 0)
        def _():
            acc_ref[...] += part

        @pl.when(i == nb - 1)
        def _():
            o_ref[...] = jnp.sum(acc_ref[...], axis=0, keepdims=True)

    return pl.pallas_call(
        body, out_shape=_SDS((1, n), f32), grid=(n // tn, nb),
        in_specs=[pl.BlockSpec((rb, tn), lambda j, i: (i, j))], out_specs=pl.BlockSpec((1, tn), lambda j, i: (0, j)),
        scratch_shapes=[pltpu.VMEM((8, tn), f32)], compiler_params=_cp(("parallel", "arbitrary")), name=name,
    )(a)


def _gate_fwd(outs, gate, *, name):
    s, d = gate.shape
    rb = _row_block(s)
    n = len(outs)
    w = d // n

    def body(*refs):
        g_ref, m_ref = refs[n], refs[n + 1]
        for k in range(n):
            gv = g_ref[:, k * w:(k + 1) * w]
            m_ref[:, k * w:(k + 1) * w] = (refs[k][...] * (gv * _sigmoid(gv))).astype(m_ref.dtype)

    blk = pl.BlockSpec((rb, d), lambda i: (i, 0))
    part = pl.BlockSpec((rb, w), lambda i: (i, 0))
    return pl.pallas_call(body, out_shape=_SDS((s, d), _MXU), grid=(s // rb,), in_specs=[part] * n + [blk],
                          out_specs=blk, compiler_params=_cp(("parallel",)), name=name)(*outs, gate)


def _gate_bwd(dmix, outs, gate, *, do_dtypes, name):
    s, d = gate.shape
    rb = _row_block(s)
    n = len(outs)
    w = d // n

    def body(*refs):
        dm_ref, o_refs, g_ref, do_refs, dg_ref = refs[0], refs[1:1 + n], refs[1 + n], refs[2 + n:2 + 2 * n], refs[-1]
        for k in range(n):
            sl = slice(k * w, (k + 1) * w)
            gv = g_ref[:, sl]
            sg = _sigmoid(gv)
            dm = dm_ref[:, sl]
            do_refs[k][...] = (dm * (gv * sg)).astype(do_refs[k].dtype)
            dg_ref[:, sl] = dm * o_refs[k][...] * (sg * (1.0 + gv * (1.0 - sg)))

    blk = pl.BlockSpec((rb, d), lambda i: (i, 0))
    part = pl.BlockSpec((rb, w), lambda i: (i, 0))
    return pl.pallas_call(body, out_shape=tuple(_SDS((s, w), t) for t in do_dtypes) + (_SDS((s, d), f32),),
                          grid=(s // rb,), in_specs=[blk] + [part] * n + [blk], out_specs=(part,) * n + (blk,),
                          compiler_params=_cp(("parallel",)), name=name)(dmix, *outs, gate)


def _adamw(w, g, m, v, *, name):
    shape = w.shape
    cols = shape[-1]
    rows = int(np.prod(shape[:-1])) if len(shape) > 1 else 1
    to2 = lambda t: t.reshape(rows, cols)
    rb = rows
    if rows * cols * 4 > (1 << 20):
        rb = max(d for d in range(8, rows + 1, 8) if rows % d == 0 and (d * cols * 4 <= (1600 << 10) or d == 8))

    def body(w_ref, g_ref, m_ref, v_ref, d_ref, nm_ref, nv_ref):
        gv = g_ref[...]
        mn = ADAM_B1 * m_ref[...] + (1.0 - ADAM_B1) * gv
        vn = ADAM_B2 * v_ref[...] + (1.0 - ADAM_B2) * (gv * gv)
        m_hat = mn / (1.0 - ADAM_B1 ** ADAM_STEP)
        v_hat = vn / (1.0 - ADAM_B2 ** ADAM_STEP)
        d_ref[...] = -ADAM_LR * (m_hat / (jnp.sqrt(v_hat) + ADAM_EPS) + ADAM_WD * w_ref[...])
        nm_ref[...] = mn
        nv_ref[...] = vn

    blk = pl.BlockSpec((rb, cols), lambda i: (i, 0))
    out = pl.pallas_call(body, out_shape=tuple(_SDS((rows, cols), f32) for _ in range(3)), grid=(rows // rb,),
                         in_specs=[blk] * 4, out_specs=(blk,) * 3, compiler_params=_cp(("parallel",)),
                         name=name)(to2(w), to2(g), to2(m), to2(v))
    return tuple(t.reshape(shape) for t in out)


def _sum_slots(a, *, name):
    p, n, c = a.shape
    rb = max(d for d in range(8, n + 1, 8) if n % d == 0 and (p * d * c * 4 <= (6 << 20) or d == 8))

    def body(a_ref, o_ref):
        acc = a_ref[0].astype(f32)
        for k in range(1, p):
            acc = acc + a_ref[k].astype(f32)
        o_ref[...] = acc

    return pl.pallas_call(body, out_shape=_SDS((n, c), f32), grid=(n // rb,),
                          in_specs=[pl.BlockSpec((p, rb, c), lambda i: (0, i, 0))],
                          out_specs=pl.BlockSpec((rb, c), lambda i: (i, 0)), compiler_params=_cp(("parallel",)),
                          name=name)(a)


def _add2(a, b, *, name):
    p, n, c = a.shape
    rb = max(d for d in range(8, n + 1, 8) if n % d == 0 and (d * c * 4 <= (2 << 20) or d == 8))

    def body(a_ref, b_ref, o_ref):
        o_ref[...] = a_ref[...] + b_ref[...]

    blk = pl.BlockSpec((1, rb, c), lambda s, i: (s, i, 0))
    return pl.pallas_call(body, out_shape=_SDS((p, n, c), f32), grid=(p, n // rb), in_specs=[blk, blk], out_specs=blk,
                          compiler_params=_cp(("parallel", "parallel")), name=name)(a, b)


def _rope_tables(pos, dim):
    half = dim // 2
    inv = ROPE_THETA ** (-jnp.arange(half, dtype=f32) / half)
    ang = pos.astype(f32)[:, None] * inv[None, :]
    c, s = jnp.cos(ang), jnp.sin(ang)
    z = jnp.zeros_like(c)
    pad = [jnp.zeros((pos.shape[0], LANES - dim), f32)] if dim < LANES else []
    return (jnp.concatenate([c, c] + pad, axis=1), jnp.concatenate([-s, z] + pad, axis=1),
            jnp.concatenate([z, s] + pad, axis=1))


def _rope(x, cos, sa, sb, half, transpose=False):
    if transpose:
        return x * cos + pltpu.roll(x * sa, half, 1) + pltpu.roll(x * sb, LANES - half, 1)
    return x * cos + pltpu.roll(x, LANES - half, 1) * sa + pltpu.roll(x, half, 1) * sb


def _rope_call(items, tables, half, transpose, *, name):
    s = items[0][0].shape[0]
    rb = _row_block(s)
    n = len(items)

    def body(*refs):
        cos, sa, sb = refs[n][...], refs[n + 1][...], refs[n + 2][...]
        for k in range(n):
            x_ref, o_ref = refs[k], refs[n + 3 + k]
            for j in range(items[k][1] // LANES):
                sl = slice(j * LANES, (j + 1) * LANES)
                o_ref[:, sl] = _rope(x_ref[:, sl], cos, sa, sb, half, transpose)

    in_specs = [pl.BlockSpec((rb, w), functools.partial(lambda i, cb: (i, cb), cb=cb)) for _, w, cb in items]
    in_specs += [pl.BlockSpec((rb, LANES), lambda i: (i, 0))] * 3
    out_specs = tuple(pl.BlockSpec((rb, w), lambda i: (i, 0)) for _, w, _ in items)
    return pl.pallas_call(
        body, out_shape=tuple(_SDS((s, w), f32) for _, w, _ in items), grid=(s // rb,), in_specs=in_specs,
        out_specs=out_specs, compiler_params=_cp(("parallel",)), name=name,
    )(*[a for a, _, _ in items], *tables)


def _attn_block(s):
    return _pick(s, (512, 256, 128))


def _lower_mask(b, strict):
    r, c = _iota((b, b), 0), _iota((b, b), 1)
    return (c < r) if strict else (c <= r)


def _pick_lane(block, h):
    return jnp.sum(jnp.where(_iota(block.shape, 1) == h, block, 0.0), axis=1, keepdims=True)


def _head_bias(cum_blk, g, j, hp):
    if hp == N_HEADS:
        return cum_blk[:, j:j + 1]
    return _pick_lane(cum_blk, g * hp + j)


def _attn_fwd(q, k, v, qcol, kcol, vcol, dq, cum, cum_t, *, scale, hp, name):
    s = q.shape[0]
    b = _attn_block(s)
    nq = s // b
    has_bias = cum is not None
    assert qcol % hp == 0 and kcol % hp == 0 and vcol % hp == 0

    def body(*refs):
        if has_bias:
            q_ref, k_ref, v_ref, cum_ref, cumt_ref, o_ref, lse_ref = refs
        else:
            q_ref, k_ref, v_ref, o_ref, lse_ref = refs
        g, i = pl.program_id(0), pl.program_id(1)
        qs = [q_ref[:, j * dq:(j + 1) * dq].astype(_MXU) for j in range(hp)]
        cqs = [_head_bias(cum_ref[...], g, j, hp) for j in range(hp)] if has_bias else None

        def chunk(c, carry, diag):
            st = pl.multiple_of(c * b, b)
            mask = _lower_mask(b, False) if diag else None
            out = []
            for j in range(hp):
                m, l, acc = carry[j]
                z = _mm_nt(qs[j], k_ref[pl.ds(st, b), j * dq:(j + 1) * dq]) * scale
                if has_bias:
                    z = z + (cqs[j] - cumt_ref[j, c])
                if diag:
                    z = jnp.where(mask, z, NEG_INF)
                m_new = jnp.maximum(m, jnp.max(z, axis=1, keepdims=True))
                p = jnp.exp(z - m_new)
                if diag:
                    p = jnp.where(mask, p, 0.0)
                alpha = jnp.exp(m - m_new)
                l = alpha * l + jnp.sum(p, axis=1, keepdims=True)
                acc = alpha * acc + _mm(p, v_ref[pl.ds(st, b), j * HEAD_DIM:(j + 1) * HEAD_DIM])
                out.append((m_new, l, acc))
            return tuple(out)

        init = tuple((jnp.full((b, 1), NEG_INF, f32), jnp.zeros((b, 1), f32), jnp.zeros((b, HEAD_DIM), f32))
                     for _ in range(hp))
        carry = lax.fori_loop(0, i, lambda c, cr: chunk(c, cr, False), init)
        for j, (m, l, acc) in enumerate(chunk(i, carry, True)):
            o_ref[:, j * HEAD_DIM:(j + 1) * HEAD_DIM] = acc / l
            lse_ref[j] = m + jnp.log(l)

    in_specs = [pl.BlockSpec((b, hp * dq), lambda g, i: (i, qcol // hp + g)),
                pl.BlockSpec((s, hp * dq), lambda g, i: (0, kcol // hp + g)),
                pl.BlockSpec((s, hp * HEAD_DIM), lambda g, i: (0, vcol // hp + g))]
    args = [q, k, v]
    if has_bias:
        in_specs += [pl.BlockSpec((b, LANES), lambda g, i: (i, 0)),
                     pl.BlockSpec((hp, nq, 1, b), lambda g, i: (g, 0, 0, 0))]
        args += [cum, cum_t]
    return pl.pallas_call(
        body, out_shape=(_SDS((s, N_HEADS * HEAD_DIM), f32), _SDS((N_HEADS, s, 1), f32)), grid=(N_HEADS // hp, nq),
        in_specs=in_specs,
        out_specs=(pl.BlockSpec((b, hp * HEAD_DIM), lambda g, i: (i, g)),
                   pl.BlockSpec((hp, b, 1), lambda g, i: (g, i, 0))),
        compiler_params=_cp(("parallel", "parallel")), name=name,
    )(*args)


def _attn_bwd(q, k, v, qcol, kcol, vcol, dq, do, o, lse, cum, cum_t, *, scale, hp, name):
    s = q.shape[0]
    b = _attn_block(s)
    nq = s // b
    has_bias = cum is not None
    assert qcol % hp == 0 and kcol % hp == 0 and vcol % hp == 0
    hd = lambda j: slice(j * HEAD_DIM, (j + 1) * HEAD_DIM)
    hq = lambda j: slice(j * dq, (j + 1) * dq)

    def body(*refs):
        if has_bias:
            (q_ref, k_ref, v_ref, do_ref, o_ref, lse_ref, cum_ref, cumt_ref, dq_ref, dk_ref, dv_ref, dck_ref,
             dkt_sc, dvt_sc, p_sc, dp_sc) = refs
        else:
            q_ref, k_ref, v_ref, do_ref, o_ref, lse_ref, dq_ref, dk_ref, dv_ref, dkt_sc, dvt_sc = refs
        g, i = pl.program_id(0), pl.program_id(1)

        @pl.when(i == 0)
        def _():
            dkt_sc[...] = jnp.zeros_like(dkt_sc)
            dvt_sc[...] = jnp.zeros_like(dvt_sc)
            if has_bias:
                dck_ref[...] = jnp.zeros_like(dck_ref)

        qs = [q_ref[:, hq(j)].astype(_MXU) for j in range(hp)]
        dos = [do_ref[:, hd(j)].astype(_MXU) for j in range(hp)]
        qts = [q_ref[:, hq(j)].T.astype(_MXU) for j in range(hp)]
        dots = [do_ref[:, hd(j)].astype(f32).T.astype(_MXU) for j in range(hp)]
        lses = [lse_ref[j] for j in range(hp)]
        cqs = [_head_bias(cum_ref[...], g, j, hp) for j in range(hp)] if has_bias else None

        def probs(j, c, diag):
            st = pl.multiple_of(c * b, b)
            z = _mm_nt(qs[j], k_ref[pl.ds(st, b), hq(j)]) * scale
            if has_bias:
                z = z + (cqs[j] - cumt_ref[j, c])
            p = jnp.exp(z - lses[j])
            if diag:
                p = jnp.where(_lower_mask(b, False), p, 0.0)
            return p, _mm_nt(dos[j], v_ref[pl.ds(st, b), hd(j)])

        if has_bias:
            def first(c, accs, diag):
                out = []
                for j in range(hp):
                    p, dp = probs(j, c, diag)
                    p_sc[j, c] = p
                    dp_sc[j, c] = dp
                    out.append(accs[j] + jnp.sum(p * dp, axis=1, keepdims=True))
                return tuple(out)

            deltas = lax.fori_loop(0, i, lambda c, a: first(c, a, False),
                                   tuple(jnp.zeros((b, 1), f32) for _ in range(hp)))
            deltas = first(i, deltas, True)
        else:
            deltas = [jnp.sum(do_ref[:, hd(j)] * o_ref[:, hd(j)], axis=1, keepdims=True) for j in range(hp)]

        def chunk(c, dq_accs, diag):
            st = pl.multiple_of(c * b, b)
            out = []
            for j in range(hp):
                p, dp = (p_sc[j, c], dp_sc[j, c]) if has_bias else probs(j, c, diag)
                ds = p * (dp - deltas[j])
                dkt_sc[j, c] += _mm(qts[j], ds)
                dvt_sc[j, c] += _mm(dots[j], p)
                if has_bias:
                    dck_ref[j, c] += -jnp.sum(ds, axis=0, keepdims=True)
                out.append(dq_accs[j] + _mm(ds, k_ref[pl.ds(st, b), hq(j)]))
            return tuple(out)

        accs = lax.fori_loop(0, i, lambda c, a: chunk(c, a, False), tuple(jnp.zeros((b, dq), f32) for _ in range(hp)))
        for j, acc in enumerate(chunk(i, accs, True)):
            dq_ref[:, hq(j)] = acc * scale

        @pl.when(i == nq - 1)
        def _():
            for j in range(hp):
                for c in range(nq):
                    dk_ref[c * b:(c + 1) * b, hq(j)] = dkt_sc[j, c].T * scale
                    dv_ref[c * b:(c + 1) * b, hd(j)] = dvt_sc[j, c].T

    rowq = pl.BlockSpec((b, hp * HEAD_DIM), lambda g, i: (i, g))
    in_specs = [pl.BlockSpec((b, hp * dq), lambda g, i: (i, qcol // hp + g)),
                pl.BlockSpec((s, hp * dq), lambda g, i: (0, kcol // hp + g)),
                pl.BlockSpec((s, hp * HEAD_DIM), lambda g, i: (0, vcol // hp + g)), rowq, rowq,
                pl.BlockSpec((hp, b, 1), lambda g, i: (g, i, 0))]
    args = [q, k, v, do, o, lse]
    out_shape = [_SDS((s, N_HEADS * dq), f32), _SDS((s, N_HEADS * dq), f32), _SDS((s, N_HEADS * HEAD_DIM), f32)]
    out_specs = [pl.BlockSpec((b, hp * dq), lambda g, i: (i, g)), pl.BlockSpec((s, hp * dq), lambda g, i: (0, g)),
                 pl.BlockSpec((s, hp * HEAD_DIM), lambda g, i: (0, g))]
    if has_bias:
        in_specs += [pl.BlockSpec((b, LANES), lambda g, i: (i, 0)),
                     pl.BlockSpec((hp, nq, 1, b), lambda g, i: (g, 0, 0, 0))]
        args += [cum, cum_t]
        out_shape.append(_SDS((N_HEADS, nq, 1, b), f32))
        out_specs.append(pl.BlockSpec((hp, nq, 1, b), lambda g, i: (g, 0, 0, 0)))
    return pl.pallas_call(
        body, out_shape=tuple(out_shape), grid=(N_HEADS // hp, nq), in_specs=in_specs, out_specs=tuple(out_specs),
        scratch_shapes=[pltpu.VMEM((hp, nq, dq, b), f32), pltpu.VMEM((hp, nq, HEAD_DIM, b), f32)]
        + ([pltpu.VMEM((hp, nq, b, b), f32)] * 2 if has_bias else []),
        compiler_params=_cp(("parallel", "arbitrary")), name=name,
    )(*args)


def _tri(b, kind):
    r, c = _iota((b, b), 0), _iota((b, b), 1)
    cond = {"row_gt": r > c, "row_lt": r < c, "row_ge": r >= c, "row_le": r <= c}[kind]
    return jnp.where(cond, 1.0, 0.0).astype(_MXU)


def _log_keep(z):
    return -(jnp.maximum(z, 0.0) + jnp.log(1.0 + jnp.exp(-jnp.abs(z))))


def _sb_fwd(z_all, *, hp, name):
    s = z_all.shape[0]
    b = _attn_block(s)
    nq = s // b
    scale = HEAD_DIM ** -0.5
    qcol, kcol, vcol = (_AL[n] // (hp * HEAD_DIM) for n in ("sb_q", "sb_k", "sb_v"))
    hd = lambda j: slice(j * HEAD_DIM, (j + 1) * HEAD_DIM)

    def body(q_ref, k_ref, v_ref, o_ref):
        i = pl.program_id(1)
        qs = [q_ref[:, hd(j)].astype(_MXU) for j in range(hp)]
        upper = _tri(b, "row_gt")

        def chunk(c, carry, diag):
            st = pl.multiple_of(c * b, b)
            mask = _lower_mask(b, True) if diag else None
            out = []
            for j in range(hp):
                rsum, acc = carry[j]
                z = _mm_nt(qs[j], k_ref[pl.ds(st, b), hd(j)]) * scale
                lk = _log_keep(z)
                if diag:
                    lk = jnp.where(mask, lk, 0.0)
                a = z + lk + _mm_split(lk, upper) + rsum
                if diag:
                    a = jnp.where(mask, a, NEG_INF)
                acc = acc + _mm(jnp.exp(a), v_ref[pl.ds(st, b), hd(j)])
                out.append((rsum + jnp.sum(lk, axis=1, keepdims=True), acc))
            return tuple(out)

        init = tuple((jnp.zeros((b, 1), f32), jnp.zeros((b, HEAD_DIM), f32)) for _ in range(hp))
        carry = lax.fori_loop(0, i, lambda t, cr: chunk(i - 1 - t, cr, False), chunk(i, init, True))
        for j in range(hp):
            o_ref[:, hd(j)] = carry[j][1]

    w = hp * HEAD_DIM
    return pl.pallas_call(
        body, out_shape=_SDS((s, GROUP), f32), grid=(N_HEADS // hp, nq),
        in_specs=[pl.BlockSpec((b, w), lambda g, i: (i, qcol + g)), pl.BlockSpec((s, w), lambda g, i: (0, kcol + g)),
                  pl.BlockSpec((s, w), lambda g, i: (0, vcol + g))],
        out_specs=pl.BlockSpec((b, w), lambda g, i: (i, g)),
        compiler_params=_cp(("parallel", "parallel")), name=name,
    )(z_all, z_all, z_all)


def _sb_bwd(z_all, do, *, hp, name):
    s = z_all.shape[0]
    b = _attn_block(s)
    nq = s // b
    scale = HEAD_DIM ** -0.5
    qcol, kcol, vcol = (_AL[n] // (hp * HEAD_DIM) for n in ("sb_q", "sb_k", "sb_v"))
    hd = lambda j: slice(j * HEAD_DIM, (j + 1) * HEAD_DIM)

    def body(q_ref, k_ref, v_ref, do_ref, dq_ref, dk_ref, dv_ref, z_sc, lk_sc, r_sc):
        i = pl.program_id(1)

        @pl.when(i == 0)
        def _():
            dk_ref[...] = jnp.zeros_like(dk_ref)
            dv_ref[...] = jnp.zeros_like(dv_ref)

        qs = [q_ref[:, hd(j)].astype(_MXU) for j in range(hp)]
        dos = [do_ref[:, hd(j)].astype(_MXU) for j in range(hp)]
        upper = _tri(b, "row_gt")
        lower = _tri(b, "row_lt")

        def scores(c, rsums, diag):
            st = pl.multiple_of(c * b, b)
            out = []
            for j in range(hp):
                z = _mm_nt(qs[j], k_ref[pl.ds(st, b), hd(j)]) * scale
                lk = _log_keep(z)
                if diag:
                    lk = jnp.where(_lower_mask(b, True), lk, 0.0)
                z_sc[j, c] = z
                lk_sc[j, c] = lk
                r_sc[j, c] = _mm_split(lk, upper) + rsums[j]
                out.append(rsums[j] + jnp.sum(lk, axis=1, keepdims=True))
            return tuple(out)

        rsums = scores(i, tuple(jnp.zeros((b, 1), f32) for _ in range(hp)), True)
        lax.fori_loop(0, i, lambda t, r: scores(i - 1 - t, r, False), rsums)

        def grads(c, carry, diag):
            st = pl.multiple_of(c * b, b)
            mask = _lower_mask(b, True) if diag else None
            out = []
            for j in range(hp):
                psum, dq_acc = carry[j]
                z, lk = z_sc[j, c], lk_sc[j, c]
                lb = z + lk
                a = lb + r_sc[j, c]
                if diag:
                    a = jnp.where(mask, a, NEG_INF)
                w = jnp.exp(a)
                e = _mm_nt(dos[j], v_ref[pl.ds(st, b), hd(j)]) * w
                before = _mm_split(e, lower) + psum
                dz = e * jnp.exp(lk) - before * jnp.exp(lb)
                if diag:
                    dz = jnp.where(mask, dz, 0.0)
                dk_ref[pl.ds(st, b), hd(j)] += _mm_tn(dz, qs[j]) * scale
                dv_ref[pl.ds(st, b), hd(j)] += _mm_tn(w, dos[j])
                out.append((psum + jnp.sum(e, axis=1, keepdims=True), dq_acc + _mm(dz, k_ref[pl.ds(st, b), hd(j)])))
            return tuple(out)

        init = tuple((jnp.zeros((b, 1), f32), jnp.zeros((b, HEAD_DIM), f32)) for _ in range(hp))
        carry = grads(i, lax.fori_loop(0, i, lambda c, cr: grads(c, cr, False), init), True)
        for j in range(hp):
            dq_ref[:, hd(j)] = carry[j][1] * scale

    w = hp * HEAD_DIM
    blk = pl.BlockSpec((b, w), lambda g, i: (i, g))
    full = pl.BlockSpec((s, w), lambda g, i: (0, g))
    return pl.pallas_call(
        body, out_shape=tuple(_SDS((s, GROUP), f32) for _ in range(3)), grid=(N_HEADS // hp, nq),
        in_specs=[pl.BlockSpec((b, w), lambda g, i: (i, qcol + g)), pl.BlockSpec((s, w), lambda g, i: (0, kcol + g)),
                  pl.BlockSpec((s, w), lambda g, i: (0, vcol + g)), blk],
        out_specs=(blk, full, full),
        scratch_shapes=[pltpu.VMEM((hp, nq, b, b), f32)] * 3,
        compiler_params=_cp(("parallel", "arbitrary")), name=name,
    )(z_all, z_all, z_all, do)


def _split3_left(t, x):
    hi = x.astype(_MXU)
    r1 = x - hi.astype(f32)
    mid = r1.astype(_MXU)
    lo = (r1 - mid.astype(f32)).astype(_MXU)
    dot = functools.partial(jnp.dot, preferred_element_type=f32)
    return dot(t, hi) + dot(t, mid) + dot(t, lo)


def _split3_right(x, t):
    hi = x.astype(_MXU)
    r1 = x - hi.astype(f32)
    mid = r1.astype(_MXU)
    lo = (r1 - mid.astype(f32)).astype(_MXU)
    dot = functools.partial(jnp.dot, preferred_element_type=f32)
    return dot(hi, t) + dot(mid, t) + dot(lo, t)


def _fox_cum_fwd(z_all, bias, *, name):
    s = z_all.shape[0]
    b = _attn_block(s)
    fcol = _AL["fox_f"] // LANES

    def body(f_ref, b_ref, cum_ref, cumt_ref, carry_ref):
        i = pl.program_id(0)

        @pl.when(i == 0)
        def _():
            carry_ref[...] = jnp.zeros_like(carry_ref)

        u = f_ref[...] + b_ref[...]
        lf = jnp.minimum(u, 0.0) - jnp.log1p(jnp.exp(-jnp.abs(u)))
        cum = _split3_left(_tri(b, "row_ge"), lf) + carry_ref[...]
        cum_ref[...] = cum
        cumt_ref[...] = cum.T[0:8, :]
        carry_ref[...] = cum_ref[b - 1:b, :]

    return pl.pallas_call(
        body, out_shape=(_SDS((s, LANES), f32), _SDS((8, s), f32)), grid=(s // b,),
        in_specs=[pl.BlockSpec((b, LANES), lambda i: (i, fcol)), pl.BlockSpec((1, LANES), lambda i: (0, 0))],
        out_specs=(pl.BlockSpec((b, LANES), lambda i: (i, 0)), pl.BlockSpec((8, b), lambda i: (0, i))),
        scratch_shapes=[pltpu.VMEM((1, LANES), f32)], compiler_params=_cp(("arbitrary",)), name=name,
    )(z_all, bias)


def _fox_cum_bwd(z_all, bias, dcum_t, *, name):
    s = z_all.shape[0]
    b = _attn_block(s)
    nb = s // b
    fcol = _AL["fox_f"] // LANES

    def body(f_ref, b_ref, dc_ref, df_ref, db_ref, carry_ref):
        i = pl.program_id(0)

        @pl.when(i == 0)
        def _():
            carry_ref[...] = jnp.zeros_like(carry_ref)
            db_ref[...] = jnp.zeros_like(db_ref)

        dc = dc_ref[...]
        rev = _split3_right(dc, _tri(b, "row_ge")) + carry_ref[...]
        carry_ref[...] = carry_ref[...] + jnp.sum(dc, axis=1, keepdims=True)
        dlf = jnp.concatenate([rev, jnp.zeros((LANES - 8, b), f32)], axis=0).T
        u = f_ref[...] + b_ref[...]
        df = jnp.where(_iota((b, LANES), 1) < N_HEADS, dlf * (1.0 - _sigmoid(u)), 0.0)
        df_ref[...] = df
        db_ref[...] += jnp.sum(df, axis=0, keepdims=True)

    return pl.pallas_call(
        body, out_shape=(_SDS((s, LANES), f32), _SDS((1, LANES), f32)), grid=(nb,),
        in_specs=[pl.BlockSpec((b, LANES), lambda i: (nb - 1 - i, fcol)), pl.BlockSpec((1, LANES), lambda i: (0, 0)),
                  pl.BlockSpec((8, b), lambda i: (0, nb - 1 - i))],
        out_specs=(pl.BlockSpec((b, LANES), lambda i: (nb - 1 - i, 0)), pl.BlockSpec((1, LANES), lambda i: (0, 0))),
        scratch_shapes=[pltpu.VMEM((8, 1), f32)], compiler_params=_cp(("arbitrary",)), name=name,
    )(z_all, bias, dcum_t)


MLA_QW = 2 * LANES


def _rms_rows(x):
    r = lax.rsqrt(jnp.mean(x * x, axis=-1, keepdims=True) + RMS_EPS)
    return x * r, r


def _mla_prep_fwd(z_all, gq, gkv, wuq, wk, wv, tables, *, name):
    s = z_all.shape[0]
    rb = _row_block(s)
    half = MLA_ROPE // 2

    def body(cq_ref, ckv_ref, kr_ref, gq_ref, gkv_ref, wuq_ref, wk_ref, wv_ref, cos_ref, sa_ref, sb_ref,
             q_ref, k_ref, v_ref):
        cos, sa, sb = cos_ref[...], sa_ref[...], sb_ref[...]
        xh, _ = _rms_rows(cq_ref[...])
        qp = _mm(xh * gq_ref[...], wuq_ref[...])
        kh, _ = _rms_rows(ckv_ref[...])
        nkv = kh * gkv_ref[...]
        kn = _mm(nkv, wk_ref[...])
        v_ref[...] = _mm(nkv, wv_ref[...])
        kr = _rope(kr_ref[...], cos, sa, sb, half)
        for h in range(N_HEADS):
            lo, mid, hi = h * MLA_QW, h * MLA_QW + LANES, (h + 1) * MLA_QW
            q_ref[:, lo:mid] = qp[:, lo:mid]
            q_ref[:, mid:hi] = _rope(qp[:, mid:hi], cos, sa, sb, half)
            k_ref[:, lo:mid] = kn[:, h * LANES:(h + 1) * LANES]
            k_ref[:, mid:hi] = kr

    row = lambda w, cb: pl.BlockSpec((rb, w), lambda i: (i, cb))
    whole = lambda a: pl.BlockSpec(a.shape, lambda i: (0,) * a.ndim)
    return pl.pallas_call(
        body, out_shape=(_SDS((s, N_HEADS * MLA_QW), f32), _SDS((s, N_HEADS * MLA_QW), f32), _SDS((s, GROUP), f32)),
        grid=(s // rb,),
        in_specs=[row(MLA_Q_RANK, _AL["mla_cq"] // MLA_Q_RANK), row(LANES, _AL["mla_ckv"] // LANES),
                  row(LANES, _AL["mla_k_rope"] // LANES), whole(gq), whole(gkv), whole(wuq), whole(wk), whole(wv),
                  row(LANES, 0), row(LANES, 0), row(LANES, 0)],
        out_specs=(row(N_HEADS * MLA_QW, 0), row(N_HEADS * MLA_QW, 0), row(GROUP, 0)),
        compiler_params=_cp(("parallel",)), name=name,
    )(z_all, z_all, z_all, gq, gkv, wuq, wk, wv, *tables)


def _mla_prep_bwd(z_all, gq, gkv, wuq, wk, wv, tables, dq_cat, dk_cat, dv, *, name):
    s = z_all.shape[0]
    rb = _row_block(s)
    half = MLA_ROPE // 2

    def body(cq_ref, ckv_ref, gq_ref, gkv_ref, wuq_ref, wk_ref, wv_ref, cos_ref, sa_ref, sb_ref, dq_ref, dk_ref,
             dv_ref, dcq_ref, dckv_ref, dkr_ref, dwuq_ref, dwk_ref, dwv_ref, dgq_ref, dgkv_ref):
        i = pl.program_id(0)

        @pl.when(i == 0)
        def _():
            for r in (dwuq_ref, dwk_ref, dwv_ref, dgq_ref, dgkv_ref):
                r[...] = jnp.zeros_like(r)

        cos, sa, sb = cos_ref[...], sa_ref[...], sb_ref[...]
        parts, knp = [], []
        dkr = jnp.zeros((rb, LANES), f32)
        for h in range(N_HEADS):
            lo, mid, hi = h * MLA_QW, h * MLA_QW + LANES, (h + 1) * MLA_QW
            parts += [dq_ref[:, lo:mid], _rope(dq_ref[:, mid:hi], cos, sa, sb, half, transpose=True)]
            knp.append(dk_ref[:, lo:mid])
            dkr = dkr + _rope(dk_ref[:, mid:hi], cos, sa, sb, half, transpose=True)
        dkr_ref[...] = dkr
        dqp = jnp.concatenate(parts, axis=1)
        dkn = jnp.concatenate(knp, axis=1)
        dvv = dv_ref[...]

        def norm_bwd(x_ref, g_ref, w_pairs, dx_ref, dg_ref):
            xh, r = _rms_rows(x_ref[...])
            nx = xh * g_ref[...]
            dn = jnp.zeros_like(xh)
            for w_ref, dw_ref, dy in w_pairs:
                dw_ref[...] += _mm_tn(nx, dy)
                dn = dn + _mm_nt(dy, w_ref[...])
            dxh = dn * g_ref[...]
            dx_ref[...] = r * (dxh - xh * jnp.mean(dxh * xh, axis=-1, keepdims=True))
            dg_ref[...] += jnp.sum(dn * xh, axis=0, keepdims=True)

        norm_bwd(cq_ref, gq_ref, [(wuq_ref, dwuq_ref, dqp)], dcq_ref, dgq_ref)
        norm_bwd(ckv_ref, gkv_ref, [(wk_ref, dwk_ref, dkn), (wv_ref, dwv_ref, dvv)], dckv_ref, dgkv_ref)

    row = lambda w, cb: pl.BlockSpec((rb, w), lambda i: (i, cb))
    whole = lambda a: pl.BlockSpec(a.shape, lambda i: (0,) * a.ndim)
    return pl.pallas_call(
        body,
        out_shape=(_SDS((s, MLA_Q_RANK), f32), _SDS((s, LANES), f32), _SDS((s, LANES), f32), _SDS(wuq.shape, f32),
                   _SDS(wk.shape, f32), _SDS(wv.shape, f32), _SDS(gq.shape, f32), _SDS(gkv.shape, f32)),
        grid=(s // rb,),
        in_specs=[row(MLA_Q_RANK, _AL["mla_cq"] // MLA_Q_RANK), row(LANES, _AL["mla_ckv"] // LANES), whole(gq),
                  whole(gkv), whole(wuq), whole(wk), whole(wv), row(LANES, 0), row(LANES, 0), row(LANES, 0),
                  row(N_HEADS * MLA_QW, 0), row(N_HEADS * MLA_QW, 0), row(GROUP, 0)],
        out_specs=(row(MLA_Q_RANK, 0), row(LANES, 0), row(LANES, 0), whole(wuq), whole(wk), whole(wv), whole(gq),
                   whole(gkv)),
        compiler_params=_cp(("arbitrary",)), name=name,
    )(z_all, z_all, gq, gkv, wuq, wk, wv, *tables, dq_cat, dk_cat, dv)


def _silu_grad(x):
    sg = _sigmoid(x)
    return sg * (1.0 + x * (1.0 - sg))


def _nsa_cmp_fwd(ra, rb_, pos, w1, w2, tables, *, name):
    nr = ra.shape[1]
    hw = ra.shape[2]

    def body(ra_ref, rb_ref, pos_ref, w1_ref, w2_ref, cos_ref, sa_ref, sb_ref, out_ref, hp_ref):
        for k in range(2):
            xa = ra_ref[k] + pos_ref[k, :, 0:hw]
            xb = rb_ref[k] + pos_ref[k, :, hw:2 * hw]
            hp = _mm(xa, w1_ref[k, 0:hw, :]) + _mm(xb, w1_ref[k, hw:2 * hw, :])
            hp_ref[k] = hp
            out = _mm(hp * _sigmoid(hp), w2_ref[k])
            if k == 0:
                out = _rope(out, cos_ref[...], sa_ref[...], sb_ref[...], HEAD_DIM // 2)
            out_ref[k] = out

    return pl.pallas_call(body, out_shape=(_SDS((2, nr, HEAD_DIM), f32), _SDS((2, nr, HEAD_DIM), f32)),
                          compiler_params=_cp(), name=name)(ra, rb_, pos, w1, w2, *tables)


def _nsa_cmp_bwd(ra, rb_, pos, w1, w2, tables, hp, dout, *, name):
    nr = ra.shape[1]
    hw = ra.shape[2]

    def body(ra_ref, rb_ref, pos_ref, w1_ref, w2_ref, cos_ref, sa_ref, sb_ref, hp_ref, do_ref,
             dxa_ref, dxb_ref, dw1_ref, dw2_ref):
        for k in range(2):
            d_out = do_ref[k]
            if k == 0:
                d_out = _rope(d_out, cos_ref[...], sa_ref[...], sb_ref[...], HEAD_DIM // 2, transpose=True)
            hpv = hp_ref[k]
            dw2_ref[k] = _mm_tn(hpv * _sigmoid(hpv), d_out)
            dhp = _mm_nt(d_out, w2_ref[k]) * _silu_grad(hpv)
            xa = ra_ref[k] + pos_ref[k, :, 0:hw]
            xb = rb_ref[k] + pos_ref[k, :, hw:2 * hw]
            dw1_ref[k, 0:hw, :] = _mm_tn(xa, dhp)
            dw1_ref[k, hw:2 * hw, :] = _mm_tn(xb, dhp)
            dxa_ref[k] = _mm_nt(dhp, w1_ref[k, 0:hw, :])
            dxb_ref[k] = _mm_nt(dhp, w1_ref[k, hw:2 * hw, :])

    return pl.pallas_call(
        body, out_shape=(_SDS((2, nr, hw), f32), _SDS((2, nr, hw), f32), _SDS(w1.shape, f32), _SDS(w2.shape, f32)),
        compiler_params=_cp(), name=name)(ra, rb_, pos, w1, w2, *tables, hp, dout)


def _nsa_consts(s):
    b = _attn_block(s)
    nr = s // CMP_STRIDE
    n_cmp = (s - CMP_LEN) // CMP_STRIDE + 1
    n_sel = s // SEL_LEN
    cmp_start = np.arange(n_cmp) * CMP_STRIDE
    sel_start = np.arange(n_sel) * SEL_LEN
    overlap = np.clip(np.minimum(cmp_start[:, None] + CMP_LEN, sel_start[None, :] + SEL_LEN)
                      - np.maximum(cmp_start[:, None], sel_start[None, :]), 0, None)
    m2s = np.zeros((nr, LANES), np.float32)
    m2s[:n_cmp, :n_sel] = overlap / CMP_LEN
    e3 = np.zeros((s // b, LANES, b), np.float32)
    tok = np.arange(s)
    e3[tok // b, tok // SEL_LEN, tok % b] = 1.0
    return jnp.asarray(m2s, _MXU), jnp.asarray(e3, _MXU)


def _nsa_masks(i, b, d):
    qpos = i * b + _iota((b, b), 0)
    kpos = (i - d) * b + _iota((b, b), 1)
    return (kpos <= qpos) & (kpos > qpos - WINDOW)


def _nsa_fwd(qr, kvc, ksr, vs, kwr, vw, z_all, m2s, e3, *, name):
    s = qr.shape[0]
    b = _attn_block(s)
    nq = s // b
    nr = kvc.shape[1]
    n_sel = s // SEL_LEN
    top_n = min(SEL_TOPN, n_sel)
    nd = -(-WINDOW // b)
    scale = HEAD_DIM ** -0.5
    bcol = _AL["nsa_branch"] // LANES
    H = N_HEADS

    def body(q_ref, kvc_ref, ks_ref, vs_ref, kw_ref, vw_ref, br_ref, m2s_ref, e3_ref,
             o_ref, oc_ref, os_ref, ow_ref, st_ref, sel_ref, m_sc, l_sc, acc_sc):
        i = pl.program_id(0)
        lane = _iota((b, LANES), 1)
        hs = lambda h: slice(h * HEAD_DIM, (h + 1) * HEAD_DIM)

        cmp_mask = (CMP_STRIDE * _iota((b, nr), 1) + (CMP_LEN - 1)) <= (i * b + _iota((b, nr), 0))
        imp = jnp.zeros((b, LANES), f32)
        stats = jnp.zeros((b, LANES), f32)
        for h in range(H):
            zc = jnp.where(cmp_mask, _mm_nt(q_ref[:, hs(h)], kvc_ref[0]) * scale, NEG_INF)
            m = jnp.max(zc, axis=1, keepdims=True)
            p = jnp.where(cmp_mask, jnp.exp(zc - m), 0.0)
            l = jnp.sum(p, axis=1, keepdims=True)
            some = l > 0.0
            lsafe = jnp.where(some, l, 1.0)
            pc = p * jnp.where(some, 1.0 / lsafe, 0.0)
            oc_ref[:, hs(h)] = _mm(pc, kvc_ref[1])
            imp = imp + _mm(pc, m2s_ref[...])
            stats = jnp.where(lane == h, jnp.where(some, m + jnp.log(lsafe), 0.0), stats)

        cur = jnp.right_shift(i * b + _iota((b, LANES), 0), int(math.log2(SEL_LEN)))
        forced = (lane == 0) | (lane == cur) | (lane == cur - 1)
        score = jnp.where(lane <= cur, jnp.where(forced, FORCED_BONUS, imp), NEG_INF)
        score = jnp.where(lane < n_sel, score, -3e38)
        rank = jnp.zeros((b, LANES), f32)
        for j in range(n_sel):
            col = score[:, j:j + 1]
            rank = rank + jnp.where(col > score, 1.0, jnp.where(col == score, jnp.where(lane > j, 1.0, 0.0), 0.0))
        sel = jnp.where(lane < n_sel, jnp.where(rank < top_n, 1.0, 0.0), 0.0)
        sel_ref[...] = sel
        sel_b = sel.astype(_MXU)

        def reset():
            m_sc[...] = jnp.full(m_sc.shape, NEG_INF, f32)
            l_sc[...] = jnp.zeros_like(l_sc)
            acc_sc[...] = jnp.zeros_like(acc_sc)

        def update(h, z, mask, vch):
            zm = jnp.where(mask, z, NEG_INF)
            m_old = m_sc[h]
            m_new = jnp.maximum(m_old, jnp.max(zm, axis=1, keepdims=True))
            p = jnp.where(mask, jnp.exp(zm - m_new), 0.0)
            alpha = jnp.exp(m_old - m_new)
            l_sc[h] = alpha * l_sc[h] + jnp.sum(p, axis=1, keepdims=True)
            acc_sc[h] = alpha * acc_sc[h] + _mm(p, vch)
            m_sc[h] = m_new

        def finish(out_ref, branch, stats):
            for h in range(H):
                out_ref[:, hs(h)] = acc_sc[h] / l_sc[h]
                stats = jnp.where(lane == 4 * branch + h, m_sc[h] + jnp.log(l_sc[h]), stats)
            return stats

        def sel_chunk(c, diag):
            st = pl.multiple_of(c * b, b)
            mask = _mm(sel_b, e3_ref[c]) > 0.5
            if diag:
                mask = mask & _lower_mask(b, False)
            kch, vch = ks_ref[pl.ds(st, b), :], vs_ref[pl.ds(st, b), :]
            for h in range(H):
                update(h, _mm_nt(q_ref[:, hs(h)], kch) * scale, mask, vch)

        reset()

        def sel_loop(c, carry):
            sel_chunk(c, False)
            return carry

        lax.fori_loop(0, i, sel_loop, 0)
        sel_chunk(i, True)
        stats = finish(os_ref, 1, stats)

        reset()
        for d in range(nd, -1, -1):
            @pl.when(i >= d)
            def _():
                st = pl.multiple_of((i - d) * b, b)
                mask = _nsa_masks(i, b, d)
                kch, vch = kw_ref[pl.ds(st, b), :], vw_ref[pl.ds(st, b), :]
                for h in range(H):
                    update(h, _mm_nt(q_ref[:, hs(h)], kch) * scale, mask, vch)
        stats = finish(ow_ref, 2, stats)
        st_ref[...] = stats

        g = _sigmoid(br_ref[...])
        for h in range(H):
            o_ref[:, hs(h)] = (g[:, 3 * h:3 * h + 1] * oc_ref[:, hs(h)] + g[:, 3 * h + 1:3 * h + 2] * os_ref[:, hs(h)]
                               + g[:, 3 * h + 2:3 * h + 3] * ow_ref[:, hs(h)])

    blk = lambda w: pl.BlockSpec((b, w), lambda i: (i, 0))
    whole = lambda a: pl.BlockSpec(a.shape, lambda i: (0,) * a.ndim)
    return pl.pallas_call(
        body, out_shape=tuple(_SDS((s, GROUP), f32) for _ in range(4)) + (_SDS((s, LANES), f32), _SDS((s, LANES), f32)),
        grid=(nq,),
        in_specs=[blk(GROUP), whole(kvc), whole(ksr), whole(vs), whole(kwr), whole(vw),
                  pl.BlockSpec((b, LANES), lambda i: (i, bcol)), whole(m2s), whole(e3)],
        out_specs=(blk(GROUP),) * 4 + (blk(LANES), blk(LANES)),
        scratch_shapes=[pltpu.VMEM((H, b, 1), f32), pltpu.VMEM((H, b, 1), f32), pltpu.VMEM((H, b, HEAD_DIM), f32)],
        compiler_params=_cp(("parallel",)), name=name,
    )(qr, kvc, ksr, vs, kwr, vw, z_all, m2s, e3)


def _nsa_bwd(do, qr, kvc, ksr, vs, kwr, vw, z_all, oc, os_, ow, stats, sel, e3, *, name):
    s = qr.shape[0]
    b = _attn_block(s)
    nq = s // b
    nr = kvc.shape[1]
    nd = -(-WINDOW // b)
    scale = HEAD_DIM ** -0.5
    bcol = _AL["nsa_branch"] // LANES
    H = N_HEADS

    def body(do_ref, q_ref, kvc_ref, ks_ref, vs_ref, kw_ref, vw_ref, br_ref, oc_ref, os_ref, ow_ref, st_ref, sel_ref,
             e3_ref, dq_ref, dbr_ref, dkvc_ref, dks_ref, dvs_ref, dkw_ref, dvw_ref, dob_sc, delta_sc, dq_sc, kvt_sc):
        i = pl.program_id(0)

        @pl.when(i == 0)
        def _():
            dkvc_ref[...] = jnp.zeros_like(dkvc_ref)
            kvt_sc[...] = jnp.zeros_like(kvt_sc)

        lane = _iota((b, LANES), 1)
        hs = lambda h: slice(h * HEAD_DIM, (h + 1) * HEAD_DIM)
        g = _sigmoid(br_ref[...])
        stats = st_ref[...]
        dbr = jnp.zeros((b, LANES), f32)
        outs = (oc_ref, os_ref, ow_ref)
        for h in range(H):
            doh = do_ref[:, hs(h)]
            for j in range(3):
                gj = g[:, 3 * h + j:3 * h + j + 1]
                dgj = jnp.sum(doh * outs[j][:, hs(h)], axis=1, keepdims=True)
                dbr = jnp.where(lane == 3 * h + j, dgj * gj * (1.0 - gj), dbr)
                dob_sc[j, :, hs(h)] = gj * doh
                delta_sc[j, h] = gj * dgj
        dbr_ref[...] = dbr
        dq_sc[...] = jnp.zeros_like(dq_sc)

        qts = [q_ref[:, hs(h)].T.astype(_MXU) for h in range(H)]
        dobts = {(j, h): dob_sc[j, :, hs(h)].T.astype(_MXU) for j in (1, 2) for h in range(H)}

        def branch(j, h, z, mask, kch, vch):
            p = jnp.where(mask, jnp.exp(jnp.where(mask, z, NEG_INF) - stats[:, 4 * j + h:4 * j + h + 1]), 0.0)
            dob = dob_sc[j, :, hs(h)]
            ds = p * (_mm_nt(dob, vch) - delta_sc[j, h])
            dq_sc[:, hs(h)] += _mm(ds, kch) * scale
            if j == 0:
                return _mm_tn(ds, q_ref[:, hs(h)]) * scale, _mm_tn(p, dob)
            return _mm(qts[h], ds), _mm(dobts[j, h], p)

        cmp_mask = (CMP_STRIDE * _iota((b, nr), 1) + (CMP_LEN - 1)) <= (i * b + _iota((b, nr), 0))
        kc, vc = kvc_ref[0], kvc_ref[1]
        for h in range(H):
            dk, dv = branch(0, h, _mm_nt(q_ref[:, hs(h)], kc) * scale, cmp_mask, kc, vc)
            dkvc_ref[0] += dk
            dkvc_ref[1] += dv

        sel_b = sel_ref[...].astype(_MXU)

        def chunk(j, c, mask, k_ref, v_ref):
            st = pl.multiple_of(c * b, b)
            kch, vch = k_ref[pl.ds(st, b), :], v_ref[pl.ds(st, b), :]
            dk = jnp.zeros((HEAD_DIM, b), f32)
            dv = jnp.zeros((HEAD_DIM, b), f32)
            for h in range(H):
                dkh, dvh = branch(j, h, _mm_nt(q_ref[:, hs(h)], kch) * scale, mask, kch, vch)
                dk, dv = dk + dkh, dv + dvh
            kvt_sc[2 * j - 2, c] += dk
            kvt_sc[2 * j - 1, c] += dv

        def sel_chunk(c, diag):
            mask = _mm(sel_b, e3_ref[c]) > 0.5
            if diag:
                mask = mask & _lower_mask(b, False)
            chunk(1, c, mask, ks_ref, vs_ref)

        def sel_loop(c, carry):
            sel_chunk(c, False)
            return carry

        lax.fori_loop(0, i, sel_loop, 0)
        sel_chunk(i, True)

        for d in range(nd, -1, -1):
            @pl.when(i >= d)
            def _():
                chunk(2, i - d, _nsa_masks(i, b, d), kw_ref, vw_ref)

        dq_ref[...] = dq_sc[...]

        @pl.when(i == nq - 1)
        def _():
            for c in range(nq):
                rows = slice(c * b, (c + 1) * b)
                dks_ref[rows, :] = kvt_sc[0, c].T * scale
                dvs_ref[rows, :] = kvt_sc[1, c].T
                dkw_ref[rows, :] = kvt_sc[2, c].T * scale
                dvw_ref[rows, :] = kvt_sc[3, c].T

    blk = lambda w: pl.BlockSpec((b, w), lambda i: (i, 0))
    whole = lambda a: pl.BlockSpec(a.shape, lambda i: (0,) * a.ndim)
    stream = _SDS((s, HEAD_DIM), f32)
    return pl.pallas_call(
        body, out_shape=(_SDS((s, GROUP), f32), _SDS((s, LANES), f32), _SDS(kvc.shape, f32), stream, stream, stream,
                         stream),
        grid=(nq,),
        in_specs=[blk(GROUP), blk(GROUP), whole(kvc), whole(ksr), whole(vs), whole(kwr), whole(vw),
                  pl.BlockSpec((b, LANES), lambda i: (i, bcol)), blk(GROUP), blk(GROUP), blk(GROUP), blk(LANES),
                  blk(LANES), whole(e3)],
        out_specs=(blk(GROUP), blk(LANES), whole(kvc), whole(ksr), whole(vs), whole(kwr), whole(vw)),
        scratch_shapes=[pltpu.VMEM((3, b, GROUP), f32), pltpu.VMEM((3, H, b, 1), f32), pltpu.VMEM((b, GROUP), f32),
                        pltpu.VMEM((4, nq, HEAD_DIM, b), f32)],
        compiler_params=_cp(("arbitrary",)), name=name,
    )(do, qr, kvc, ksr, vs, kwr, vw, z_all, oc, os_, ow, stats, sel, e3)


def _seg(a, name):
    parts = [lax.slice_in_dim(a, off, off + hi - lo, axis=a.ndim - 1) for off, lo, hi in _PIECES[name]]
    return parts[0] if len(parts) == 1 else jnp.concatenate(parts, axis=a.ndim - 1)


def _to_groups(segs, rows, dtype):
    cols = []
    for s, grp in enumerate(_GROUPS):
        at = 0
        for n, lo, hi, off in sorted(grp, key=lambda t: t[3]):
            if off > at:
                cols.append(jnp.zeros((rows, off - at), dtype))
            cols.append(segs[n][:, lo:hi].astype(dtype))
            at = off + hi - lo
        if at < GROUP_W:
            cols.append(jnp.zeros((rows, GROUP_W - at), dtype))
    return jnp.concatenate(cols, axis=1)


def _piece_from_shard(w_t, s):
    grp = sorted(_GROUPS[s], key=lambda t: t[3])
    ends = [t[3] for t in grp[1:]] + [GROUP_W]
    rows = []
    for (n, lo, hi, off), end in zip(grp, ends):
        first = _ORIG[n] + lo - s * CHIP_COLS
        rows.append(jnp.pad(w_t[:, first:first + hi - lo], ((0, 0), (0, end - off - (hi - lo)), (0, 0))))
    return jnp.concatenate(rows, axis=1)


def _shard_from_piece(g, s):
    return jnp.concatenate([g[:, off:off + hi - lo] for n, lo, hi, off in
                            sorted(_GROUPS[s], key=lambda t: _ORIG[t[0]] + t[1])], axis=1)


def _from_groups(a):
    return jnp.concatenate([_seg(a, n) for n, _ in _SEGS], axis=1)


def _cmp_rows(tok):
    s = tok.shape[0]
    r = tok.reshape(s // CMP_STRIDE, CMP_STRIDE * HEAD_DIM)
    return r, jnp.concatenate([r[1:], jnp.zeros((1, r.shape[1]), r.dtype)], axis=0)


def _cmp_unrows(dxa, dxb):
    s = dxa.shape[0] * CMP_STRIDE
    return (dxa + jnp.concatenate([jnp.zeros((1, dxa.shape[1]), dxa.dtype), dxb[:-1]], axis=0)).reshape(s, HEAD_DIM)


_GATES = ("sb_gate", "nsa_gate", "fox_gate", "mla_gate")


def _layer_fwd(x, p, c, tag):
    s = x.shape[0]
    b = _attn_block(s)
    h = _rms_fwd(x, p["pre_g"], out_dtype=_MXU, name=f"prenorm_{tag}")
    z = _matmul(h, p["w_in"], "nt", bias=p["b_in"], name=f"inproj_{tag}")
    o_sb = _sb_fwd(z, hp=HP_FWD, name=f"sb_fwd_{tag}")

    qr, ksr, kwr = _rope_call([(z, GROUP, _AL["nsa_q"] // GROUP), (z, LANES, _AL["nsa_k_sel"] // LANES),
                               (z, LANES, _AL["nsa_k_win"] // LANES)], c["tabs128"], HEAD_DIM // 2, False,
                              name=f"nsa_rope_{tag}")
    (rak, rbk), (rav, rbv) = _cmp_rows(_seg(z, "nsa_k_cmp")), _cmp_rows(_seg(z, "nsa_v_cmp"))
    ra, rb_ = jnp.stack([rak, rav]), jnp.stack([rbk, rbv])
    kvc, hp = _nsa_cmp_fwd(ra, rb_, p["cmp_pos"], p["cmp_w1"], p["cmp_w2"], c["tabs_cmp"], name=f"nsa_cmp_{tag}")
    vs, vw = _seg(z, "nsa_v_sel"), _seg(z, "nsa_v_win")
    o_nsa, oc, os_, ow, stats, sel = _nsa_fwd(qr, kvc, ksr, vs, kwr, vw, z, c["m2s"], c["e3"], name=f"nsa_fwd_{tag}")

    cum, cum_t8 = _fox_cum_fwd(z, p["fox_bias"], name=f"fox_cum_{tag}")
    cum_t = cum_t8.reshape(8, s // b, 1, b)
    fox_v = _seg(z, "fox_v")
    fcols = (_AL["fox_q"] // HEAD_DIM, _AL["fox_k"] // HEAD_DIM, 0)
    o_fox, lse_fox = _attn_fwd(z, z, fox_v, *fcols, HEAD_DIM, cum, cum_t, scale=HEAD_DIM ** -0.5, hp=HP_FWD,
                               name=f"fox_fwd_{tag}")

    qcat, kcat, vm = _mla_prep_fwd(z, p["gq"], p["gkv"], p["wuq"], p["wk"], p["wv"], c["tabs64"],
                                   name=f"mla_prep_{tag}")
    o_mla, lse_mla = _attn_fwd(qcat, kcat, vm, 0, 0, 0, MLA_QW, None, None, scale=(MLA_NOPE + MLA_ROPE) ** -0.5,
                               hp=HP_BWD, name=f"mla_fwd_{tag}")

    o_all = (o_sb, o_nsa, o_fox, o_mla)
    gates = jnp.concatenate([_seg(z, n) for n in _GATES], axis=1)
    mix = _gate_fwd(o_all, gates, name=f"gate_{tag}")
    u = _matmul(mix, p["w_out"], "nn", name=f"outproj_{tag}")
    y = _postnorm_fwd(u, p["post_g"], x, name=f"postnorm_{tag}")
    saved = dict(x=x, h=h, z=z, qr=qr, ksr=ksr, kwr=kwr, ra=ra, rb=rb_, kvc=kvc, hp=hp, vs=vs, vw=vw, oc=oc, os=os_,
                 ow=ow, stats=stats, sel=sel, cum=cum, cum_t=cum_t, fox_v=fox_v, o_fox=o_fox, lse_fox=lse_fox, qcat=qcat, kcat=kcat,
                 vm=vm, o_mla=o_mla, lse_mla=lse_mla, o_all=o_all, gates=gates, mix=mix, u=u)
    return y, saved


def _layer_bwd(dy, sv, p, c, tag, dw_dtype=f32):
    z = sv["z"]
    s = z.shape[0]
    du, dg_post = _rms_bwd(dy, sv["u"], p["post_g"], out_dtype=_MXU, name=f"postnorm_bwd_{tag}")
    dmix = _matmul(du, p["w_out"], "nt", name=f"outproj_dx_{tag}")
    dw_out = _matmul(sv["mix"], du, "tn", name=f"outproj_dw_{tag}")
    do_sb, do_nsa, do_fox, do_mla, dgates = _gate_bwd(dmix, sv["o_all"], sv["gates"],
                                                      do_dtypes=(_MXU, f32, _MXU, f32), name=f"gate_bwd_{tag}")
    dgate = [dgates[:, k * GROUP:(k + 1) * GROUP] for k in range(4)]

    sb_dq, sb_dk, sb_dv = _sb_bwd(z, do_sb, hp=HP_BWD, name=f"sb_bwd_{tag}")

    n_dq, n_dbr, n_dkvc, n_dks, n_dvs, n_dkw, n_dvw = _nsa_bwd(
        do_nsa, sv["qr"], sv["kvc"], sv["ksr"], sv["vs"], sv["kwr"], sv["vw"], z, sv["oc"], sv["os"], sv["ow"],
        sv["stats"], sv["sel"], c["e3"], name=f"nsa_bwd_{tag}")
    dxa, dxb, dw1, dw2 = _nsa_cmp_bwd(sv["ra"], sv["rb"], p["cmp_pos"], p["cmp_w1"], p["cmp_w2"], c["tabs_cmp"],
                                      sv["hp"], n_dkvc, name=f"nsa_cmp_bwd_{tag}")
    n_dq, n_dks, n_dkw = _rope_call([(n_dq, GROUP, 0), (n_dks, LANES, 0), (n_dkw, LANES, 0)], c["tabs128"],
                                    HEAD_DIM // 2, True, name=f"nsa_rope_bwd_{tag}")
    dpos = _colsum(jnp.concatenate([dxa[0], dxb[0], dxa[1], dxb[1]], axis=1), name=f"nsa_dpos_{tag}")
    flat = CMP_LEN * HEAD_DIM

    fcols = (_AL["fox_q"] // HEAD_DIM, _AL["fox_k"] // HEAD_DIM, 0)
    f_dq, f_dk, f_dv, f_dck = _attn_bwd(z, z, sv["fox_v"], *fcols, HEAD_DIM, do_fox, sv["o_fox"], sv["lse_fox"],
                                        sv["cum"], sv["cum_t"], scale=HEAD_DIM ** -0.5, hp=HP_BWD,
                                        name=f"fox_bwd_{tag}")
    dcum_t = jnp.pad(f_dck.reshape(N_HEADS, s), ((0, 8 - N_HEADS), (0, 0)))
    f_df, f_dbias = _fox_cum_bwd(z, p["fox_bias"], dcum_t, name=f"fox_cum_bwd_{tag}")

    m_dq, m_dk, m_dv = _attn_bwd(sv["qcat"], sv["kcat"], sv["vm"], 0, 0, 0, MLA_QW, do_mla, sv["o_mla"], sv["lse_mla"],
                                 None, None, scale=(MLA_NOPE + MLA_ROPE) ** -0.5, hp=HP_BWD, name=f"mla_bwd_{tag}")
    m_dcq, m_dckv, m_dkr, m_dwuq, m_dwk, m_dwv, m_dgq, m_dgkv = _mla_prep_bwd(
        z, p["gq"], p["gkv"], p["wuq"], p["wk"], p["wv"], c["tabs64"], m_dq, m_dk, m_dv, name=f"mla_prep_bwd_{tag}")

    dz = _to_groups(dict(
        sb_q=sb_dq, sb_k=sb_dk, sb_v=sb_dv, sb_gate=dgate[0], nsa_q=n_dq, nsa_k_cmp=_cmp_unrows(dxa[0], dxb[0]),
        nsa_v_cmp=_cmp_unrows(dxa[1], dxb[1]), nsa_k_sel=n_dks, nsa_v_sel=n_dvs, nsa_k_win=n_dkw, nsa_v_win=n_dvw,
        nsa_branch=n_dbr, nsa_gate=dgate[1], fox_q=f_dq, fox_k=f_dk, fox_v=f_dv, fox_f=f_df, fox_gate=dgate[2],
        mla_cq=m_dcq, mla_ckv=m_dckv, mla_k_rope=m_dkr, mla_gate=dgate[3]), s, _MXU)
    dh = _matmul(dz, p["w_in"], "nn", name=f"inproj_dx_{tag}")
    dw_in = _matmul(dz, sv["h"], "tn", out_dtype=dw_dtype, name=f"inproj_dw_{tag}")
    db = _colsum(dz, name=f"inproj_db_{tag}")
    dx, dg_pre = _rms_bwd(dh, sv["x"], p["pre_g"], res=dy, name=f"prenorm_bwd_{tag}")

    qw = MLA_NOPE + MLA_ROPE
    grads = {
        "pre_norm_g": dg_pre[0], "post_norm_g": dg_post[0], "w_in": dw_in, "b_in": _from_groups(db)[0],
        "w_out": dw_out, "fox_forget_bias": f_dbias[0, :N_HEADS],
        "nsa_cmp_pos_k": dpos[0, :flat].reshape(CMP_LEN, HEAD_DIM), "nsa_cmp_w1_k": dw1[0], "nsa_cmp_w2_k": dw2[0],
        "nsa_cmp_pos_v": dpos[0, flat:].reshape(CMP_LEN, HEAD_DIM), "nsa_cmp_w1_v": dw1[1], "nsa_cmp_w2_v": dw2[1],
        "mla_q_norm_g": m_dgq[0],
        "mla_w_uq": jnp.concatenate([m_dwuq[:, MLA_QW * h:MLA_QW * h + qw] for h in range(N_HEADS)], axis=1),
        "mla_kv_norm_g": m_dgkv[0],
        "mla_w_ukv": jnp.concatenate(sum([[m_dwk[:, LANES * h:LANES * (h + 1)], m_dwv[:, LANES * h:LANES * (h + 1)]]
                                          for h in range(N_HEADS)], []), axis=1),
    }
    return dx, grads


def _layer_params(w, l):
    b_in = w["b_in"][l].reshape(1, -1)
    b_segs = {n: b_in[:, _ORIG[n]:_ORIG[n] + wd] for n, wd in _SEGS}
    qw = MLA_NOPE + MLA_ROPE
    w_uq, w_ukv = w["mla_w_uq"][l], w["mla_w_ukv"][l]
    uq = []
    for h in range(N_HEADS):
        uq += [w_uq[:, qw * h:qw * (h + 1)], jnp.zeros((w_uq.shape[0], MLA_QW - qw), w_uq.dtype)]
    kw_ = 2 * LANES
    flat = CMP_LEN * HEAD_DIM
    return dict(
        pre_g=w["pre_norm_g"][l].reshape(1, -1), post_g=w["post_norm_g"][l].reshape(1, -1),
        w_in=w["w_in"][l], b_in=_to_groups(b_segs, 1, f32), w_out=w["w_out"][l],
        fox_bias=jnp.pad(w["fox_forget_bias"][l], (0, LANES - N_HEADS)).reshape(1, LANES),
        cmp_pos=jnp.stack([w["nsa_cmp_pos_k"][l].reshape(1, flat), w["nsa_cmp_pos_v"][l].reshape(1, flat)]),
        cmp_w1=jnp.stack([w["nsa_cmp_w1_k"][l], w["nsa_cmp_w1_v"][l]]),
        cmp_w2=jnp.stack([w["nsa_cmp_w2_k"][l], w["nsa_cmp_w2_v"][l]]),
        gq=w["mla_q_norm_g"][l].reshape(1, -1), gkv=w["mla_kv_norm_g"][l].reshape(1, -1),
        wuq=jnp.concatenate(uq, axis=1),
        wk=jnp.concatenate([w_ukv[:, kw_ * h:kw_ * h + LANES] for h in range(N_HEADS)], axis=1),
        wv=jnp.concatenate([w_ukv[:, kw_ * h + LANES:kw_ * (h + 1)] for h in range(N_HEADS)], axis=1),
    )


def _consts(s):
    pos = jnp.arange(s)
    m2s, e3 = _nsa_consts(s)
    return dict(tabs128=_rope_tables(pos, HEAD_DIM), tabs64=_rope_tables(pos, MLA_ROPE),
                tabs_cmp=_rope_tables(jnp.arange(s // CMP_STRIDE) * CMP_STRIDE + (CMP_LEN - 1), HEAD_DIM),
                m2s=m2s, e3=e3)


def _place():
    return lax.axis_index("x"), lax.axis_index("y"), lax.axis_index("c")


def _other_chips(x, y):
    return [(1 - x, y), (x, 1 - y), (1 - x, 1 - y)]


def _comm_call(body, out_shapes, n_sems, arrs, name):
    return pl.pallas_call(body, out_shape=tuple(out_shapes), in_specs=[_ANY] * len(arrs),
                          out_specs=tuple(_ANY for _ in out_shapes),
                          scratch_shapes=[pltpu.SemaphoreType.DMA((n_sems,)), pltpu.SemaphoreType.DMA((n_sems,))],
                          name=name)(*arrs)


def _gather_chips(arrs, *, name):
    n = len(arrs)

    def body(*refs):
        a_refs, out_refs, send_sems, recv_sems = refs[:n], refs[n:2 * n], refs[2 * n], refs[2 * n + 1]
        x, y, c = _place()
        me = 2 * x + y
        sibling = (x, y, 1 - c)
        chips = _other_chips(x, y)

        def copy(j, k, src, dst, to):
            return pltpu.make_async_remote_copy(src, dst, send_sems.at[6 * j + k], recv_sems.at[6 * j + k],
                                                device_id=to, device_id_type=_MESH)

        first = [copy(j, k, a_refs[j].at[c], out_refs[j].at[me, c], (px, py, c))
                 for k, (px, py) in enumerate(chips) for j in range(n)]
        for cp in first:
            cp.start()
        passed = []
        for k, (px, py) in enumerate(chips):
            for j in range(n):
                landed = out_refs[j].at[2 * px + py, c]
                copy(j, k, a_refs[j].at[c], landed, (px, py, c)).wait_recv()
                passed.append(copy(j, 3 + k, landed, landed, sibling))
                passed[-1].start()
        for k, (px, py) in enumerate(chips):
            for j in range(n):
                copy(j, 3 + k, a_refs[j].at[c], out_refs[j].at[2 * px + py, 1 - c], sibling).wait_recv()
        for cp in first + passed:
            cp.wait_send()

    return _comm_call(body, [_SDS((N_CHIPS,) + a.shape, a.dtype) for a in arrs], 6 * n, arrs, name)


def _alltoall_chips(arrs, modes, *, name):
    n = len(arrs)

    def body(*refs):
        g_refs, out_refs, send_sems, recv_sems = refs[:n], refs[n:2 * n], refs[2 * n], refs[2 * n + 1]
        x, y, c = _place()
        me = 2 * x + y

        def copy(j, s):
            return pltpu.make_async_remote_copy(_slot_ref(g_refs[j], modes[j], s), out_refs[j].at[me],
                                                send_sems.at[N_CHIPS * j + s], recv_sems.at[N_CHIPS * j + me],
                                                device_id=(s // 2, s % 2, c), device_id_type=_MESH)

        for s in range(N_CHIPS):
            @pl.when(s != me)
            def _():
                for j in range(n):
                    copy(j, s).start()
        for t in range(N_CHIPS):
            @pl.when(t != me)
            def _():
                for j in range(n):
                    pltpu.make_async_remote_copy(_slot_ref(g_refs[j], modes[j], t), out_refs[j].at[t],
                                                 send_sems.at[N_CHIPS * j + t], recv_sems.at[N_CHIPS * j + t],
                                                 device_id=(t // 2, t % 2, c), device_id_type=_MESH).wait_recv()
        for s in range(N_CHIPS):
            @pl.when(s != me)
            def _():
                for j in range(n):
                    copy(j, s).wait_send()

    outs = [_SDS((N_CHIPS,) + _slot_shape(a, m), a.dtype) for a, m in zip(arrs, modes)]
    return _comm_call(body, outs, N_CHIPS * n, arrs, name)


def _swap_other_half(arrs, *, name):
    n = len(arrs)

    def body(*refs):
        g_refs, out_refs, send_sems, recv_sems = refs[:n], refs[n:2 * n], refs[2 * n], refs[2 * n + 1]
        x, y, c = _place()
        cps = [pltpu.make_async_remote_copy(g_refs[j].at[:, 1 - c], out_refs[j], send_sems.at[j], recv_sems.at[j],
                                            device_id=(x, y, 1 - c), device_id_type=_MESH) for j in range(n)]
        for cp in cps:
            cp.start()
        for cp in cps:
            cp.wait()

    return _comm_call(body, [_SDS((a.shape[0],) + a.shape[2:], a.dtype) for a in arrs], n, arrs, name)


def _swap_sibling(arrs, *, name):
    n = len(arrs)

    def body(*refs):
        f_refs, out_refs, send_sems, recv_sems = refs[:n], refs[n:2 * n], refs[2 * n], refs[2 * n + 1]
        x, y, c = _place()
        cps = [pltpu.make_async_remote_copy(f_refs[j], out_refs[j], send_sems.at[j], recv_sems.at[j],
                                            device_id=(x, y, 1 - c), device_id_type=_MESH) for j in range(n)]
        for cp in cps:
            cp.start()
        for cp in cps:
            cp.wait()

    return _comm_call(body, [_SDS(a.shape, a.dtype) for a in arrs], n, arrs, name)


_HBM = pl.BlockSpec(memory_space=pltpu.HBM)
_SEM = pl.BlockSpec(memory_space=pltpu.SEMAPHORE)
_EFFECT = pltpu.SideEffectType.DATAFLOW_SIDE_EFFECTING


def _slot_ref(ref, mode, s):
    return ref if mode == "same" else ref.at[s]


def _slot_shape(a, mode):
    return a.shape if mode == "same" else a.shape[1:]


def _send_start(arrs, modes, after, *, name):
    n = len(arrs)
    lands = [lax.empty((N_CHIPS,) + _slot_shape(a, m), a.dtype) for a, m in zip(arrs, modes)]

    def body(*refs):
        srcs, land_refs, send_sems, recv_sems, token = refs[:n], refs[n:2 * n], refs[2 * n + 1], refs[2 * n + 2], refs[-1]
        x, y, c = _place()
        me = 2 * x + y
        for s in range(N_CHIPS):
            @pl.when(s != me)
            def _():
                for j in range(n):
                    pltpu.make_async_remote_copy(_slot_ref(srcs[j], modes[j], s), land_refs[j].at[me],
                                                 send_sems.at[N_CHIPS * j + s], recv_sems.at[N_CHIPS * j + me],
                                                 device_id=(s // 2, s % 2, c), device_id_type=_MESH).start()
        token[...] = jnp.zeros_like(token)

    hbm = lambda a: pltpu.HBM(a.shape, a.dtype)
    sems = pltpu.SemaphoreType.DMA((N_CHIPS * n,))
    out = pl.pallas_call(
        body, name=name, out_shape=(sems, sems, *[hbm(a) for a in arrs], *[hbm(a) for a in lands], _SDS((8, LANES), f32)),
        in_specs=[_HBM] * (2 * n) + [_ANY], out_specs=(_SEM, _SEM, *[_HBM] * (2 * n), pl.BlockSpec(memory_space=pltpu.VMEM)),
        input_output_aliases={j: 2 + j for j in range(2 * n)},
        compiler_params=pltpu.CompilerParams(has_side_effects=_EFFECT),
    )(*[pltpu.with_memory_space_constraint(a, pltpu.HBM) for a in arrs + lands], after)
    return out[:-1], out[-1]


def _send_wait(started, modes, after, *, name):
    send_sems, recv_sems = started[0], started[1]
    n = (len(started) - 2) // 2
    thru = list(started[2:])

    def body(*refs):
        srcs, land_refs, send_sems, recv_sems = refs[:n], refs[n:2 * n], refs[2 * n], refs[2 * n + 1]
        x, y, c = _place()
        me = 2 * x + y
        for s in range(N_CHIPS):
            @pl.when(s != me)
            def _():
                for j in range(n):
                    cp = pltpu.make_async_remote_copy(_slot_ref(srcs[j], modes[j], s), land_refs[j].at[s],
                                                      send_sems.at[N_CHIPS * j + s], recv_sems.at[N_CHIPS * j + s],
                                                      device_id=(s // 2, s % 2, c), device_id_type=_MESH)
                    cp.wait_send()
                    cp.wait_recv()

    hbm = lambda a: pltpu.HBM(a.shape, a.dtype)
    out = pl.pallas_call(
        body, name=name, out_shape=tuple(hbm(a) for a in thru), in_specs=[_HBM] * (2 * n) + [_SEM, _SEM, _ANY],
        out_specs=tuple([_HBM] * (2 * n)), input_output_aliases={j: j for j in range(2 * n)},
        compiler_params=pltpu.CompilerParams(has_side_effects=_EFFECT),
    )(*thru, send_sems, recv_sems, after)
    return list(out[n:])


def _add_my_half(g, r, *, name):
    p, _, h, w = g.shape
    tw = _pick(w, (2048, 1024, 512, 256, 128))
    rb = max(d for d in range(16, h + 1, 16) if h % d == 0 and d * tw * 4 <= (2 << 20))

    def body(c_ref, g_ref, r_ref, o_ref):
        o_ref[...] = (g_ref[...].astype(f32) + r_ref[...].astype(f32)).astype(o_ref.dtype)

    blk = pl.BlockSpec((None, rb, tw), lambda s, i, j, c_ref: (s, i, j))
    grid_spec = pltpu.PrefetchScalarGridSpec(
        num_scalar_prefetch=1, grid=(p, h // rb, w // tw),
        in_specs=[pl.BlockSpec((None, None, rb, tw), lambda s, i, j, c_ref: (s, c_ref[0], i, j)), blk], out_specs=blk)
    c = lax.axis_index("c").astype(jnp.int32).reshape(1)
    return pl.pallas_call(body, out_shape=_SDS((p, h, w), _WIRE), grid_spec=grid_spec,
                          compiler_params=_cp(("parallel", "parallel", "parallel")), name=name)(c, g, r)


_WEIGHTS = ("pre_norm_g", "post_norm_g", "w_in", "b_in", "w_out", "fox_forget_bias", "nsa_cmp_pos_k", "nsa_cmp_w1_k",
            "nsa_cmp_w2_k", "nsa_cmp_pos_v", "nsa_cmp_w1_v", "nsa_cmp_w2_v", "mla_q_norm_g", "mla_w_uq",
            "mla_kv_norm_g", "mla_w_ukv")
_SHARD_AXIS = {"w_in": 2, "w_out": 1, "nsa_cmp_w1_k": 1, "nsa_cmp_w1_v": 1, "mla_w_uq": 2, "mla_w_ukv": 2}
_PACK_UNIT = 16 * LANES


def _pack(arrays, dtype):
    rows = []
    for a in arrays:
        v = a.astype(dtype).reshape(-1)
        pad = (-v.shape[0]) % _PACK_UNIT
        if pad:
            v = jnp.concatenate([v, jnp.zeros((pad,), dtype)])
        rows.append(v.reshape(-1, LANES))
    return jnp.concatenate(rows, axis=0)


def _unpack(flat, shapes):
    out, r = [], 0
    for shp in shapes:
        n = int(np.prod(shp))
        nr = -(-n // _PACK_UNIT) * (_PACK_UNIT // LANES)
        out.append(flat[r:r + nr].reshape(-1)[:n].reshape(shp))
        r += nr
    return out


def kernel(x, pre_norm_g, post_norm_g, w_in, b_in, w_out, fox_forget_bias, nsa_cmp_pos_k, nsa_cmp_w1_k, nsa_cmp_w2_k, nsa_cmp_pos_v, nsa_cmp_w1_v, nsa_cmp_w2_v, mla_q_norm_g, mla_w_uq, mla_kv_norm_g, mla_w_ukv, loss_target, m_pre_norm_g, m_post_norm_g, m_w_in, m_b_in, m_w_out, m_fox_forget_bias, m_nsa_cmp_pos_k, m_nsa_cmp_w1_k, m_nsa_cmp_w2_k, m_nsa_cmp_pos_v, m_nsa_cmp_w1_v, m_nsa_cmp_w2_v, m_mla_q_norm_g, m_mla_w_uq, m_mla_kv_norm_g, m_mla_w_ukv, v_pre_norm_g, v_post_norm_g, v_w_in, v_b_in, v_w_out, v_fox_forget_bias, v_nsa_cmp_pos_k, v_nsa_cmp_w1_k, v_nsa_cmp_w2_k, v_nsa_cmp_pos_v, v_nsa_cmp_w1_v, v_nsa_cmp_w2_v, v_mla_q_norm_g, v_mla_w_uq, v_mla_kv_norm_g, v_mla_w_ukv):
    given = dict(locals())
    local = {n: given[n] for n in _WEIGHTS}
    depth = pre_norm_g.shape[0]
    xs, target = x[0], loss_target[0]
    s = xs.shape[0]
    sharded = [n for n in _WEIGHTS if n in _SHARD_AXIS and n != "w_in"]
    small = [n for n in _WEIGHTS if n not in _SHARD_AXIS]
    chip = 2 * lax.axis_index("x") + lax.axis_index("y")
    core = lax.axis_index("c")
    own = lambda slots, mine: lax.dynamic_update_slice_in_dim(slots, mine[None], chip, axis=0)

    w_in_t = jnp.swapaxes(w_in, 1, 2).astype(_MXU)
    piece = lax.switch(chip, [functools.partial(_piece_from_shard, s=k) for k in range(N_CHIPS)], w_in_t)
    layer_shapes = [local[n].shape[1:] for n in sharded]
    flat = [_pack([local[n][l] for n in sharded], _MXU) for l in range(depth)]
    full = dict(local)
    for n in ["w_in"] + sharded:
        full[n] = []

    def add_layer(w_in_slots, flat_slots_):
        full["w_in"].append(w_in_slots)
        per_chip = [_unpack(flat_slots_[k], layer_shapes) for k in range(N_CHIPS)]
        for j, n in enumerate(sharded):
            full[n].append(jnp.concatenate([per_chip[k][j] for k in range(N_CHIPS)], axis=_SHARD_AXIS[n] - 1))

    halved = [piece[0].reshape(2, GROUP_W // 2, D_MODEL), flat[0].reshape(2, -1, LANES)]
    first_all = [own(a, b) for a, b in zip(_gather_chips(halved, name="gather_weights"), halved)]
    add_layer(first_all[0].reshape(N_CHIPS, GROUP_W, D_MODEL), first_all[1].reshape((N_CHIPS,) + flat[0].shape))
    later = [piece[l] for l in range(1, depth)] + flat[1:]
    started, token = _send_start(later, ["same"] * len(later), first_all[1], name="gather_later_start")
    full["pre_norm_g"] = pre_norm_g + token[0, 0]

    consts = _consts(s)
    params, act, saved = [], xs, []
    for l in range(depth):
        if l == 1:
            landed = [own(a, b) for a, b in zip(_send_wait(started, ["same"] * len(later), act,
                                                           name="gather_later_wait"), later)]
            for k in range(depth - 1):
                add_layer(landed[k], landed[depth - 1 + k])
        params.append(_layer_params(full, l))
        act, sv = _layer_fwd(act, params[l], consts, f"l{l}")
        saved.append(sv)
    dy, loss_parts = _loss_head(act, target, name="loss_head")

    def flat_slots(g, dtype):
        def part(n, k):
            a, ax = g[n], _SHARD_AXIS[n] - 1
            w = a.shape[ax] // N_CHIPS
            return lax.slice_in_dim(a, k * w, (k + 1) * w, axis=ax)
        return jnp.stack([_pack([part(n, k) for n in sharded], dtype) for k in range(N_CHIPS)])

    own_slot = lambda a: lax.dynamic_index_in_dim(a, chip, axis=0, keepdims=False)
    slots_of = lambda g: g["w_in"].reshape(N_CHIPS, GROUP_W, D_MODEL)

    modes = ["slots", "slots"]
    layer_grads, in_flight = [None] * depth, {}
    for l in reversed(range(depth)):
        dy, layer_grads[l] = _layer_bwd(dy, saved[l], params[l], consts, f"l{l}", _WIRE)
        if l > 0:
            wire = [slots_of(layer_grads[l]), flat_slots(layer_grads[l], _WIRE)]
            started, token = _send_start(wire, modes, dy, name=f"reduce_l{l}_start")
            in_flight[l] = (started, wire)
            params[l - 1] = dict(params[l - 1], post_g=params[l - 1]["post_g"] + token[0, 0])
    grad_x = dy[None]
    grads = {n: jnp.stack([layer_grads[l][n] for l in range(depth)]) for n in small}
    loss_row = jnp.concatenate([jnp.sum(loss_parts).reshape(1), jnp.zeros((LANES - 1,), f32)])
    small_shapes = [(LANES,)] + [grads[n].shape for n in small]
    contrib = _pack([loss_row] + [grads[n] for n in small], f32)

    halves = [slots_of(layer_grads[0]).reshape(N_CHIPS, 2, GROUP_W // 2, D_MODEL),
              flat_slots(layer_grads[0], _WIRE).reshape(N_CHIPS, 2, -1, LANES)]
    from_sibling = _swap_other_half(halves, name="reduce_pair")
    pair_sum = [_add_my_half(g, r, name=f"reduce_pair_add{j}") for j, (g, r) in enumerate(zip(halves, from_sibling))]
    from_chips = _alltoall_chips(pair_sum + [contrib], modes + ["same"], name="reduce_chips")
    my_half = [_sum_slots(own(slots, own_slot(ps)), name=f"reduce_chips_add{j}")
               for j, (slots, ps) in enumerate(zip(from_chips, pair_sum))]
    partial = []
    for l in range(1, depth):
        started, wire = in_flight[l]
        landed = _send_wait(started, modes, dy, name=f"reduce_l{l}_wait")
        partial += [_sum_slots(own(slots, own_slot(a)), name=f"reduce_l{l}_add{j}")
                    for j, (slots, a) in enumerate(zip(landed, wire))]
    partial.append(_sum_slots(own(from_chips[2], contrib), name="sum_small"))
    theirs = _swap_sibling(my_half + partial, name="reduce_share")
    first = core == 0
    whole = [jnp.concatenate([jnp.where(first, a, b), jnp.where(first, b, a)], axis=0)
             for a, b in zip(my_half, theirs[:2])]
    whole += [_add2(a[None], b[None], name=f"reduce_cores_add{j}")[0] for j, (a, b) in enumerate(zip(partial, theirs[2:]))]
    unpiece = [functools.partial(_shard_from_piece, s=k) for k in range(N_CHIPS)]
    summed = {"w_in": jnp.stack([lax.switch(chip, unpiece, whole[2 * l].T) for l in range(depth)])}
    rest = [_unpack(whole[2 * l + 1], layer_shapes) for l in range(depth)]
    for j, n in enumerate(sharded):
        summed[n] = jnp.stack([rest[l][j] for l in range(depth)])
    total = _unpack(whole[2 * depth], small_shapes)
    loss = total[0][0]
    summed.update(zip(small, total[1:]))

    deltas, new_m, new_v = {}, {}, {}
    for n in _WEIGHTS:
        deltas[n], new_m[n], new_v[n] = _adamw(local[n], summed[n], given["m_" + n], given["v_" + n], name=f"adamw_{n}")
    return (loss, grad_x, *[summed[n] for n in _WEIGHTS], *[deltas[n] for n in _WEIGHTS],
            *[new_m[n] for n in _WEIGHTS], *[new_v[n] for n in _WEIGHTS])
```
